```python
import jax, jax.numpy as jnp
from jax import lax
import numpy as np

D_MODEL = 1024
BATCH = 16
SEQ = 256
DEPTH = 1
DEC_BATCH = 8
DEC_SEQ = 1024
PAST_LEN = 256

GRID_W = 64
FOURIER_WIDTH = D_MODEL
N_FOURIER_GROUPS = 8
FOURIER_GROUP_DIM = FOURIER_WIDTH // N_FOURIER_GROUPS
N_RET_HEADS = D_MODEL // 256
RET_HEAD_DIM = 128
RET_WIDTH = N_RET_HEADS * RET_HEAD_DIM
CHUNK = 128
IN_WIDTH = FOURIER_WIDTH + 4 * RET_WIDTH + 2 * D_MODEL
N_EXPERTS = 64
N_EXPERT_GROUPS = 8
TOPK_GROUPS = 4
TOP_K = 8
EXPERT_DIM = 256
SHARED_DIM = 256
ROUTED_SCALE = 2.5
ROPE_BASE = 10000.0
EPS = 1e-6

kernel_name = 'hybrid_fnet_retention_moe_diffusion_step'


def _rmsnorm(x, g):
    xf = x.astype(jnp.float32)
    y = xf * lax.rsqrt(jnp.mean(xf * xf, axis=-1, keepdims=True) + EPS)
    return (y * g.astype(jnp.float32)).astype(x.dtype)


def _grid_rope(length):
    rows = length // GRID_W
    r = jnp.repeat(jnp.arange(rows, dtype=jnp.float32), GRID_W)
    col = jnp.tile(jnp.arange(GRID_W, dtype=jnp.float32), rows)
    nf = RET_HEAD_DIM // 4
    inv = ROPE_BASE ** (-jnp.arange(nf, dtype=jnp.float32) / nf)
    ar = r[:, None] * inv[None]
    ac = col[:, None] * inv[None]
    ang = jnp.concatenate([ar, ar, ac, ac], axis=-1)[:, None, :]
    return jnp.cos(ang), jnp.sin(ang)


def _apply_rope(x, cos, sin):
    nf = RET_HEAD_DIM // 4
    xs = x.reshape(*x.shape[:-1], 2, 2, nf)
    rot = jnp.stack([-xs[..., 1, :], xs[..., 0, :]], axis=-2).reshape(x.shape)
    return x * cos + rot * sin


def _retention_scan(q, k, v, log_gamma, s0, include_diag):
    b, length, h, dk = q.shape
    dv = v.shape[-1]
    n = length // CHUNK
    qc = q.reshape(b, n, CHUNK, h, dk)
    kc = k.reshape(b, n, CHUNK, h, dk)
    vc = v.reshape(b, n, CHUNK, h, dv)
    pos = jnp.arange(CHUNK, dtype=jnp.float32)
    diff = pos[:, None] - pos[None, :]
    allowed = (diff >= 0) if include_diag else (diff > 0)
    decay = jnp.where(allowed[None], jnp.exp(log_gamma[:, None, None] * jnp.where(allowed, diff, 0.0)[None]), 0.0)
    scores = jnp.einsum('bnihd,bnjhd->bnhij', qc, kc) * decay[None, None]
    intra = jnp.einsum('bnhij,bnjhe->bnihe', scores, vc)
    k_w = jnp.exp(log_gamma[:, None] * (CHUNK - 1.0 - pos)[None])
    kv = jnp.einsum('bnjhd,hj,bnjhe->nbhde', kc, k_w, vc)
    g_chunk = jnp.exp(log_gamma * CHUNK)[None, :, None, None]

    def step(s, kv_n):
        return g_chunk * s + kv_n, s

    s_final, s_prev = lax.scan(step, s0, kv)
    q_w = jnp.exp(log_gamma[:, None] * (pos + 1.0)[None])
    cross = jnp.einsum('bnihd,hi,nbhde->bnihe', qc, q_w, s_prev)
    return (intra + cross).reshape(b, length, h, dv), s_final


def _parallel_mixer(h, s0_f, s0_b, rope, w_in, decay_f, decay_b, gn_g, w_four, w_ret, w_o):
    b, length, _ = h.shape
    proj = h @ w_in
    cuts = [FOURIER_WIDTH, FOURIER_WIDTH + RET_WIDTH, FOURIER_WIDTH + 2 * RET_WIDTH,
            FOURIER_WIDTH + 3 * RET_WIDTH, FOURIER_WIDTH + 4 * RET_WIDTH,
            FOURIER_WIDTH + 4 * RET_WIDTH + D_MODEL]
    u_f, q, k, v, g, gate_f, gate_r = jnp.split(proj, cuts, axis=-1)
    uf = u_f.reshape(b, length, N_FOURIER_GROUPS, FOURIER_GROUP_DIM).astype(jnp.float32)
    f_mix = jnp.fft.fftn(uf, axes=(1, 3), norm='ortho').real
    f_out = f_mix.reshape(b, length, FOURIER_WIDTH).astype(h.dtype) @ w_four
    q = q.reshape(b, length, N_RET_HEADS, RET_HEAD_DIM).astype(jnp.float32)
    k = k.reshape(b, length, N_RET_HEADS, RET_HEAD_DIM).astype(jnp.float32)
    v = v.reshape(b, length, N_RET_HEADS, RET_HEAD_DIM).astype(jnp.float32)
    if rope is not None:
        cos, sin = rope
        q = _apply_rope(q, cos, sin)
        k = _apply_rope(k, cos, sin)
    q = q * (RET_HEAD_DIM ** -0.5)
    lg_f = jax.nn.log_sigmoid(decay_f.astype(jnp.float32))
    lg_b = jax.nn.log_sigmoid(decay_b.astype(jnp.float32))
    o_f, s_f = _retention_scan(q, k, v, lg_f, s0_f.astype(jnp.float32), True)
    o_b, s_b = _retention_scan(q[:, ::-1], k[:, ::-1], v[:, ::-1], lg_b, s0_b.astype(jnp.float32), False)
    o = o_f + o_b[:, ::-1]
    mu = jnp.mean(o, axis=-1, keepdims=True)
    var = jnp.mean(jnp.square(o - mu), axis=-1, keepdims=True)
    o = ((o - mu) * lax.rsqrt(var + EPS)).reshape(b, length, RET_WIDTH) * gn_g.astype(jnp.float32)
    r_out = (o.astype(h.dtype) * jax.nn.silu(g)) @ w_ret
    merged = jax.nn.sigmoid(gate_f) * f_out + jax.nn.sigmoid(gate_r) * r_out
    return merged @ w_o, s_f, s_b


def _moe(h, w_router, router_bias, we_g, we_u, we_d, ws_g, ws_u, ws_d):
    b, length, d = h.shape
    t = h.reshape(b * length, d)
    scores = jax.nn.sigmoid((t @ w_router).astype(jnp.float32))
    biased = scores + router_bias.astype(jnp.float32)
    per_group = N_EXPERTS // N_EXPERT_GROUPS
    grouped = biased.reshape(-1, N_EXPERT_GROUPS, per_group)
    group_score = jnp.sum(lax.top_k(grouped, 2)[0], axis=-1)
    _, g_idx = lax.top_k(group_score, TOPK_GROUPS)
    g_mask = jnp.sum(jax.nn.one_hot(g_idx, N_EXPERT_GROUPS, dtype=jnp.float32), axis=1) > 0
    e_mask = jnp.repeat(g_mask, per_group, axis=-1)
    _, idx = lax.top_k(jnp.where(e_mask, biased, -jnp.inf), TOP_K)
    w = jnp.take_along_axis(scores, idx, axis=-1)
    w = w / jnp.sum(w, axis=-1, keepdims=True) * ROUTED_SCALE
    combine = jnp.einsum('tk,tke->te', w, jax.nn.one_hot(idx, N_EXPERTS, dtype=jnp.float32)).astype(h.dtype)
    hid = jax.nn.silu(jnp.einsum('td,edf->tef', t, we_g)) * jnp.einsum('td,edf->tef', t, we_u)
    routed = jnp.einsum('tef,efd->td', hid * combine[..., None], we_d)
    shared = (jax.nn.silu(t @ ws_g) * (t @ ws_u)) @ ws_d
    return (routed + shared).reshape(b, length, d)


def _trunk_layer(x, cond, s0_f, s0_b, rope, w_ada, b_ada, norm1_g, norm2_g, w_in, decay_f, decay_b, gn_g,
                 w_four, w_ret, w_o, w_router, router_bias, we_g, we_u, we_d, ws_g, ws_u, ws_d):
    mod = jax.nn.silu(cond) @ w_ada + b_ada
    sh1, sc1, g1, sh2, sc2, g2 = jnp.split(mod[:, None, :], 6, axis=-1)
    h = _rmsnorm(x, norm1_g) * (1.0 + sc1) + sh1
    mix, s_f, s_b = _parallel_mixer(h, s0_f, s0_b, rope, w_in, decay_f, decay_b, gn_g, w_four, w_ret, w_o)
    x = x + g1 * mix
    h = _rmsnorm(x, norm2_g) * (1.0 + sc2) + sh2
    x = x + g2 * _moe(h, w_router, router_bias, we_g, we_u, we_d, ws_g, ws_u, ws_d)
    return x, s_f, s_b


def setup_inputs(seed: int = 0) -> dict:
    key = jax.random.key(seed)
    ks = jax.random.split(key, 32)
    f32 = jnp.float32
    n = lambda i, shape, s: jax.random.normal(ks[i], shape, f32) * s
    gam = 1.0 - 2.0 ** (-5.0 - jnp.arange(N_RET_HEADS, dtype=f32))
    decay0 = jnp.log(gam) - jnp.log1p(-gam)
    st_shape = (DEC_BATCH, DEPTH, N_RET_HEADS, RET_HEAD_DIM, RET_HEAD_DIM)
    return {
        'x_prompt': n(0, (BATCH, SEQ, D_MODEL), 1.0),
        'x_sample': n(1, (DEC_BATCH, DEC_SEQ, D_MODEL), 1.0),
        'state_ret_fwd': n(2, st_shape, 0.5),
        'state_ret_bwd': n(3, st_shape, 0.5),
        'c': n(4, (DEC_BATCH, D_MODEL), 1.0),
        'c_ctx': n(5, (D_MODEL,), 1.0),
        'w_ada': n(6, (DEPTH, D_MODEL, 6 * D_MODEL), 0.5 * D_MODEL ** -0.5),
        'b_ada': n(7, (DEPTH, 6 * D_MODEL), 0.02),
        'norm1_g': 1.0 + n(8, (DEPTH, D_MODEL), 0.01),
        'norm2_g': 1.0 + n(9, (DEPTH, D_MODEL), 0.01),
        'w_in': n(10, (DEPTH, D_MODEL, IN_WIDTH), D_MODEL ** -0.5),
        'ret_decay_fwd': decay0[None] + n(11, (DEPTH, N_RET_HEADS), 0.01),
        'ret_decay_bwd': decay0[None] + n(12, (DEPTH, N_RET_HEADS), 0.01),
        'ret_gn_g': 1.0 + n(13, (DEPTH, RET_WIDTH), 0.01),
        'w_four_out': n(14, (DEPTH, FOURIER_WIDTH, D_MODEL), FOURIER_WIDTH ** -0.5),
        'w_ret_out': n(15, (DEPTH, RET_WIDTH, D_MODEL), RET_WIDTH ** -0.5),
        'w_out': n(16, (DEPTH, D_MODEL, D_MODEL), D_MODEL ** -0.5),
        'w_router': n(17, (DEPTH, D_MODEL, N_EXPERTS), D_MODEL ** -0.5),
        'router_bias': n(18, (DEPTH, N_EXPERTS), 0.01),
        'w_exp_gate': n(19, (DEPTH, N_EXPERTS, D_MODEL, EXPERT_DIM), D_MODEL ** -0.5),
        'w_exp_up': n(20, (DEPTH, N_EXPERTS, D_MODEL, EXPERT_DIM), D_MODEL ** -0.5),
        'w_exp_down': n(21, (DEPTH, N_EXPERTS, EXPERT_DIM, D_MODEL), EXPERT_DIM ** -0.5),
        'w_shared_gate': n(22, (DEPTH, D_MODEL, SHARED_DIM), D_MODEL ** -0.5),
        'w_shared_up': n(23, (DEPTH, D_MODEL, SHARED_DIM), D_MODEL ** -0.5),
        'w_shared_down': n(24, (DEPTH, SHARED_DIM, D_MODEL), SHARED_DIM ** -0.5),
        'final_norm_g': 1.0 + n(25, (D_MODEL,), 0.01),
    }


def reference(x_prompt, x_sample, state_ret_fwd, state_ret_bwd, c, c_ctx, w_ada, b_ada, norm1_g, norm2_g,
              w_in, ret_decay_fwd, ret_decay_bwd, ret_gn_g, w_four_out, w_ret_out, w_out, w_router, router_bias,
              w_exp_gate, w_exp_up, w_exp_down, w_shared_gate, w_shared_up, w_shared_down, final_norm_g):
    rope = _grid_rope(x_sample.shape[1])
    zeros = jnp.zeros((x_prompt.shape[0], N_RET_HEADS, RET_HEAD_DIM, RET_HEAD_DIM), jnp.float32)
    xc = x_prompt
    xs = x_sample
    new_f = []
    new_b = []
    for layer in range(DEPTH):
        lw = (w_ada[layer], b_ada[layer], norm1_g[layer], norm2_g[layer], w_in[layer], ret_decay_fwd[layer],
              ret_decay_bwd[layer], ret_gn_g[layer], w_four_out[layer], w_ret_out[layer], w_out[layer],
              w_router[layer], router_bias[layer], w_exp_gate[layer], w_exp_up[layer], w_exp_down[layer],
              w_shared_gate[layer], w_shared_up[layer], w_shared_down[layer])
        xc, s_f, s_b = _trunk_layer(xc, c_ctx[None, :], zeros, zeros, None, *lw)
        new_f.append(s_f)
        new_b.append(s_b)
        xs, _, _ = _trunk_layer(xs, c, state_ret_fwd[:, layer], state_ret_bwd[:, layer], rope, *lw)
    y_prompt = _rmsnorm(xc, final_norm_g)
    y_sample = _rmsnorm(xs, final_norm_g)
    new_state_fwd = jnp.stack(new_f, axis=1)
    new_state_bwd = jnp.stack(new_b, axis=1)
    return (y_prompt, y_sample, new_state_fwd, new_state_bwd)
```

```python
import functools
import math

import jax
import jax.numpy as jnp
from jax import lax
from jax.experimental import pallas as pl
from jax.experimental.pallas import tpu as pltpu

F32 = jnp.float32
BF16 = jnp.bfloat16

D_MODEL = 1024
GRID_W = 64
N_FOURIER_GROUPS = 8
FOURIER_GROUP_DIM = 128
N_RET_HEADS = 4
RET_HEAD_DIM = 128
RET_WIDTH = N_RET_HEADS * RET_HEAD_DIM
CHUNK = 128
N_EXPERTS = 64
N_EXPERT_GROUPS = 8
EXPERTS_PER_GROUP = N_EXPERTS // N_EXPERT_GROUPS
TOPK_GROUPS = 4
TOP_K = 8
EXPERT_DIM = 256
ROUTED_SCALE = 2.5
ROPE_BASE = 10000.0
EPS = 1e-6
Q_SCALE = RET_HEAD_DIM ** -0.5

_C_UF = (0, 1024)
_C_Q = (1024, 1536)
_C_K = (1536, 2048)
_C_V = (2048, 2560)
_C_G = (2560, 3072)
_C_GF = (3072, 4096)
_C_GR = (4096, 5120)

VMEM_LIMIT = 56 * 1024 * 1024

TM_PROJ = 512
FNET_ROWS = 256
TM_MOE = 1024
EXPERTS_PER_STEP = 4


def _silu(x):
    return x * jax.nn.sigmoid(x)


def _dot(a, b):
    return jnp.dot(a, b, preferred_element_type=F32)


def _rms_mod(x, g, shift, scale):
    ms = jnp.mean(x * x, axis=-1, keepdims=True)
    y = x * lax.rsqrt(ms + EPS) * g
    return y * (1.0 + scale) + shift


def _ada_kernel(cond_ref, w_ref, b_ref, o_ref):
    s = _silu(cond_ref[...]).astype(BF16)
    o_ref[...] = _dot(s, w_ref[...].astype(BF16)) + b_ref[...]


def _ada(cond, w_ada, b_ada):
    rows, n = cond.shape[0], w_ada.shape[1]
    tn = 1536
    return pl.pallas_call(
        _ada_kernel,
        grid=(n // tn,),
        in_specs=[pl.BlockSpec((rows, D_MODEL), lambda j: (0, 0)),
                  pl.BlockSpec((D_MODEL, tn), lambda j: (0, j)),
                  pl.BlockSpec((1, tn), lambda j: (0, j))],
        out_specs=pl.BlockSpec((rows, tn), lambda j: (0, j)),
        out_shape=jax.ShapeDtypeStruct((rows, n), F32),
        compiler_params=pltpu.CompilerParams(vmem_limit_bytes=VMEM_LIMIT),
        name="ada",
    )(cond, w_ada, b_ada)


def _rope_head(x, cos, sin_signed, first_half):
    partner = jnp.where(first_half, pltpu.roll(x, 96, 1), pltpu.roll(x, 32, 1))
    return x * cos + partner * sin_signed


def _inproj_kernel(*refs, use_rope):
    if use_rope:
        x_ref, mod_ref, g_ref, w_ref, cos_ref, sin_ref = refs[:6]
        outs = refs[6:]
    else:
        x_ref, mod_ref, g_ref, w_ref = refs[:4]
        outs = refs[4:]
    uf_o, q_o, k_o, v_o, sg_o, gf_o, gr_o = outs

    h = _rms_mod(x_ref[...], g_ref[...], mod_ref[0, 0:1, :], mod_ref[0, 1:2, :])
    hb = h.astype(BF16)

    def proj(cols):
        return _dot(hb, w_ref[:, cols[0]:cols[1]])

    uf_o[...] = proj(_C_UF).astype(BF16)
    q = proj(_C_Q)
    k = proj(_C_K)
    if use_rope:
        cos = cos_ref[...]
        sin_signed = sin_ref[...]
        lane = lax.broadcasted_iota(jnp.int32, cos.shape, 1)
        first_half = (lane & 32) == 0
        for hd in range(N_RET_HEADS):
            sl = slice(hd * RET_HEAD_DIM, (hd + 1) * RET_HEAD_DIM)
            q_o[:, sl] = (_rope_head(q[:, sl], cos, sin_signed, first_half) * Q_SCALE).astype(BF16)
            k_o[:, sl] = _rope_head(k[:, sl], cos, sin_signed, first_half).astype(BF16)
    else:
        q_o[...] = (q * Q_SCALE).astype(BF16)
        k_o[...] = k.astype(BF16)
    v_o[...] = proj(_C_V).astype(BF16)
    sg_o[...] = _silu(proj(_C_G)).astype(BF16)
    gf_o[...] = jax.nn.sigmoid(proj(_C_GF)).astype(BF16)
    gr_o[...] = jax.nn.sigmoid(proj(_C_GR)).astype(BF16)


def _inproj(x2d, mod3, norm_g, w_in_bf, seq_len, mod_row_of_batch, rope):
    t = x2d.shape[0]
    tm = TM_PROJ
    tiles_per_seq = max(seq_len // tm, 1)

    def mod_idx(i):
        return (mod_row_of_batch((i * tm) // seq_len), 0, 0)

    in_specs = [pl.BlockSpec((tm, D_MODEL), lambda i: (i, 0)),
                pl.BlockSpec((1, 6, D_MODEL), mod_idx),
                pl.BlockSpec((1, D_MODEL), lambda i: (0, 0)),
                pl.BlockSpec(w_in_bf.shape, lambda i: (0, 0))]
    args = [x2d, mod3, norm_g, w_in_bf]
    if rope is not None:
        in_specs += [pl.BlockSpec((tm, RET_HEAD_DIM), lambda i: (i % tiles_per_seq, 0))] * 2
        args += list(rope)
    widths = [1024, RET_WIDTH, RET_WIDTH, RET_WIDTH, RET_WIDTH, 1024, 1024]
    return pl.pallas_call(
        functools.partial(_inproj_kernel, use_rope=rope is not None),
        grid=(t // tm,),
        in_specs=in_specs,
        out_specs=[pl.BlockSpec((tm, w), lambda i: (i, 0)) for w in widths],
        out_shape=[jax.ShapeDtypeStruct((t, w), BF16) for w in widths],
        compiler_params=pltpu.CompilerParams(dimension_semantics=("parallel",),
                                             vmem_limit_bytes=VMEM_LIMIT),
        name="inproj",
    )(*args)


def _retention_kernel(q_ref, k_ref, v_ref, sg_ref, dec_ref, gn_ref, s0f_ref, s0b_ref,
                      r_ref, sfo_ref, sbo_ref):
    n_chunks = q_ref.shape[0] // CHUNK
    dec = dec_ref[...]
    lg = jnp.minimum(dec, 0.0) - jnp.log1p(jnp.exp(-jnp.abs(dec)))
    lgf = lg[0:1, :]
    lgb = lg[1:2, :]
    row = lax.broadcasted_iota(jnp.int32, (CHUNK, CHUNK), 0).astype(F32)
    col = lax.broadcasted_iota(jnp.int32, (CHUNK, CHUNK), 1).astype(F32)
    diff = row - col
    decay = jnp.exp(jnp.where(diff >= 0, lgf * diff, lgb * (-diff)))
    qw_f = jnp.exp(lgf * (row + 1.0))
    qw_b = jnp.exp(lgb * (CHUNK - row))
    kw_f = jnp.exp(lgf * (CHUNK - 1.0 - row))
    kw_b = jnp.exp(lgb * row)
    gc_f = jnp.exp(lgf * CHUNK)
    gc_b = jnp.exp(lgb * CHUNK)

    def rows(n):
        return slice(n * CHUNK, (n + 1) * CHUNK)

    kv_f, kv_b = [], []
    for n in range(n_chunks):
        kn = k_ref[rows(n), :].astype(F32)
        vn = v_ref[rows(n), :]
        kv_f.append(_dot((kn * kw_f).T.astype(BF16), vn))
        kv_b.append(_dot((kn * kw_b).T.astype(BF16), vn))

    s = s0f_ref[...]
    prev_f = []
    for n in range(n_chunks):
        prev_f.append(s.astype(BF16))
        s = gc_f * s + kv_f[n]
    sfo_ref[...] = s
    s = s0b_ref[...]
    prev_b = [None] * n_chunks
    for n in reversed(range(n_chunks)):
        prev_b[n] = s.astype(BF16)
        s = gc_b * s + kv_b[n]
    sbo_ref[...] = s

    gn = gn_ref[...]
    for n in range(n_chunks):
        qn = q_ref[rows(n), :]
        qf = qn.astype(F32)
        scores = lax.dot_general(qn, k_ref[rows(n), :], (((1,), (1,)), ((), ())),
                                 preferred_element_type=F32)
        o = _dot((scores * decay).astype(BF16), v_ref[rows(n), :])
        o = o + _dot((qf * qw_f).astype(BF16), prev_f[n])
        o = o + _dot((qf * qw_b).astype(BF16), prev_b[n])
        mu = jnp.mean(o, axis=-1, keepdims=True)
        d = o - mu
        var = jnp.mean(d * d, axis=-1, keepdims=True)
        on = d * lax.rsqrt(var + EPS) * gn
        r_ref[rows(n), :] = (on * sg_ref[rows(n), :].astype(F32)).astype(BF16)


def _retention(q, k, v, sg, dec, gn_g, s0f, s0b, batch, seq_len):
    hd = RET_HEAD_DIM
    tok_spec = pl.BlockSpec((seq_len, hd), lambda b, h: (b, h))
    st_spec = pl.BlockSpec((None, None, hd, hd), lambda b, h: (b, h, 0, 0))
    st_shape = jax.ShapeDtypeStruct((batch, N_RET_HEADS, hd, hd), F32)
    return pl.pallas_call(
        _retention_kernel,
        grid=(batch, N_RET_HEADS),
        in_specs=[tok_spec, tok_spec, tok_spec, tok_spec,
                  pl.BlockSpec((None, 2, hd), lambda b, h: (h, 0, 0)),
                  pl.BlockSpec((1, hd), lambda b, h: (0, h)),
                  st_spec, st_spec],
        out_specs=[tok_spec, st_spec, st_spec],
        out_shape=[jax.ShapeDtypeStruct((batch * seq_len, RET_WIDTH), BF16), st_shape, st_shape],
        compiler_params=pltpu.CompilerParams(dimension_semantics=("parallel", "parallel"),
                                             vmem_limit_bytes=VMEM_LIMIT),
        name="retention",
    )(q, k, v, sg, dec, gn_g, s0f, s0b)


def _fnet_kernel(uf_ref, cs_ref, cls_ref, o_ref, xcs_ref):
    seq_len = uf_ref.shape[0]
    gd = FOURIER_GROUP_DIM

    @pl.when(pl.program_id(1) == 0)
    def _():
        for g in range(N_FOURIER_GROUPS):
            x = _dot(uf_ref[:, g * gd:(g + 1) * gd], cs_ref[...])
            xcs_ref[0:seq_len, g * gd:(g + 1) * gd] = x[:, :gd].astype(BF16)
            xcs_ref[seq_len:2 * seq_len, g * gd:(g + 1) * gd] = x[:, gd:].astype(BF16)

    o_ref[...] = _dot(cls_ref[...], xcs_ref[...]).astype(BF16)


def _fnet(uf, cs, cls, batch, seq_len):
    rb = FNET_ROWS
    nr = seq_len // rb
    return pl.pallas_call(
        _fnet_kernel,
        grid=(batch, nr),
        in_specs=[pl.BlockSpec((seq_len, D_MODEL), lambda b, r: (b, 0)),
                  pl.BlockSpec(cs.shape, lambda b, r: (0, 0)),
                  pl.BlockSpec((rb, 2 * seq_len), lambda b, r: (r, 0))],
        out_specs=pl.BlockSpec((rb, D_MODEL), lambda b, r: (b * nr + r, 0)),
        out_shape=jax.ShapeDtypeStruct((batch * seq_len, D_MODEL), BF16),
        scratch_shapes=[pltpu.VMEM((2 * seq_len, D_MODEL), BF16)],
        compiler_params=pltpu.CompilerParams(dimension_semantics=("parallel", "arbitrary"),
                                             vmem_limit_bytes=VMEM_LIMIT),
        name="fnet",
    )(uf, cs, cls)


def _merge_kernel(fm_ref, r_ref, gf_ref, gr_ref, x_ref, mod_ref, wf_ref, wr_ref, wo_ref, o_ref):
    f_out = _dot(fm_ref[...], wf_ref[...])
    r_out = _dot(r_ref[...], wr_ref[...])
    merged = gf_ref[...].astype(F32) * f_out + gr_ref[...].astype(F32) * r_out
    mix = _dot(merged.astype(BF16), wo_ref[...])
    o_ref[...] = x_ref[...] + mod_ref[0, 2:3, :] * mix


def _merge(fmix, r, gf, gr, x2d, mod3, w_four, w_ret, w_o, seq_len, mod_row_of_batch):
    t = x2d.shape[0]
    tm = TM_PROJ

    def mod_idx(i):
        return (mod_row_of_batch((i * tm) // seq_len), 0, 0)

    def tok(w):
        return pl.BlockSpec((tm, w), lambda i: (i, 0))

    def full(a):
        return pl.BlockSpec(a.shape, lambda i: (0, 0))

    return pl.pallas_call(
        _merge_kernel,
        grid=(t // tm,),
        in_specs=[tok(D_MODEL), tok(RET_WIDTH), tok(D_MODEL), tok(D_MODEL), tok(D_MODEL),
                  pl.BlockSpec((1, 6, D_MODEL), mod_idx), full(w_four), full(w_ret), full(w_o)],
        out_specs=tok(D_MODEL),
        out_shape=jax.ShapeDtypeStruct((t, D_MODEL), F32),
        compiler_params=pltpu.CompilerParams(dimension_semantics=("parallel",),
                                             vmem_limit_bytes=VMEM_LIMIT),
        name="merge",
    )(fmix, r, gf, gr, x2d, mod3, w_four, w_ret, w_o)


def _route(scores, biased):
    tokens = scores.shape[1]
    neg = -jnp.inf
    epg = EXPERTS_PER_GROUP
    iota_g = lax.broadcasted_iota(jnp.int32, (epg, tokens), 0).astype(F32)

    def pick_first_max(cur, iota, size):
        m = jnp.max(cur, axis=0, keepdims=True)
        idx = jnp.min(jnp.where(cur == m, iota, float(size)), axis=0, keepdims=True)
        return m, iota == idx

    group_scores = []
    for g in range(N_EXPERT_GROUPS):
        vals = biased[g * epg:(g + 1) * epg, :]
        m1, hit = pick_first_max(vals, iota_g, epg)
        m2 = jnp.max(jnp.where(hit, neg, vals), axis=0, keepdims=True)
        group_scores.append(m1 + m2)
    cur = jnp.concatenate(group_scores, axis=0)
    group_sel = jnp.zeros_like(cur)
    for _ in range(TOPK_GROUPS):
        _, hit = pick_first_max(cur, iota_g, N_EXPERT_GROUPS)
        group_sel = jnp.where(hit, 1.0, group_sel)
        cur = jnp.where(hit, neg, cur)
    masked = jnp.concatenate(
        [jnp.where(group_sel[g:g + 1, :] > 0.0, biased[g * epg:(g + 1) * epg, :], neg)
         for g in range(N_EXPERT_GROUPS)], axis=0)
    iota_e = lax.broadcasted_iota(jnp.int32, masked.shape, 0).astype(F32)
    sel = jnp.zeros_like(masked)
    cur = masked
    for _ in range(TOP_K):
        _, hit = pick_first_max(cur, iota_e, N_EXPERTS)
        sel = jnp.where(hit, 1.0, sel)
        cur = jnp.where(hit, neg, cur)
    w = scores * sel
    return w / jnp.sum(w, axis=0, keepdims=True) * ROUTED_SCALE


def _moe_kernel(x_ref, mod_ref, g2_ref, wrt_ref, rb_ref, weg_ref, weu_ref, wed_ref,
                wsg_ref, wsu_ref, wsd_ref, fng_ref, o_ref, h_scr, comb_scr, acc_scr):
    step = pl.program_id(1)
    n_steps = pl.num_programs(1)
    tm = x_ref.shape[0]

    @pl.when(step == 0)
    def _():
        h = _rms_mod(x_ref[...], g2_ref[...], mod_ref[0, 3:4, :], mod_ref[0, 4:5, :])
        h_scr[...] = h.astype(BF16)
        logits_t = lax.dot_general(wrt_ref[...], h, (((1,), (1,)), ((), ())),
                                   precision=lax.Precision.HIGHEST, preferred_element_type=F32)
        scores = jax.nn.sigmoid(logits_t)
        comb_t = _route(scores, scores + rb_ref[...])
        padded = jnp.concatenate([comb_t, jnp.zeros((128 - N_EXPERTS, tm), F32)], axis=0)
        comb_scr[...] = padded.T
        acc_scr[...] = jnp.zeros_like(acc_scr)

    hb = h_scr[...]
    comb = comb_scr[...]
    lane = lax.broadcasted_iota(jnp.int32, comb.shape, 1)
    for j in range(EXPERTS_PER_STEP):
        e = step * EXPERTS_PER_STEP + j
        c = jnp.sum(jnp.where(lane == e, comb, 0.0), axis=1, keepdims=True)
        hid = _silu(_dot(hb, weg_ref[j])) * _dot(hb, weu_ref[j])
        acc_scr[...] += _dot((hid * c).astype(BF16), wed_ref[j])

    @pl.when(step == n_steps - 1)
    def _():
        shared = _dot((_silu(_dot(hb, wsg_ref[...])) * _dot(hb, wsu_ref[...])).astype(BF16), wsd_ref[...])
        y = x_ref[...] + mod_ref[0, 5:6, :] * (acc_scr[...] + shared)
        ms = jnp.mean(y * y, axis=-1, keepdims=True)
        o_ref[...] = y * lax.rsqrt(ms + EPS) * fng_ref[...]


def _moe(x1, mod3, norm2_g, w_router_t, router_bias, weg, weu, wed, wsg, wsu, wsd, final_g,
         seq_len, mod_row_of_batch):
    t = x1.shape[0]
    tm = min(TM_MOE, t)
    eb = EXPERTS_PER_STEP

    def mod_idx(i, e):
        return (mod_row_of_batch((i * tm) // seq_len), 0, 0)

    def full(a):
        return pl.BlockSpec(a.shape, lambda i, e: (0,) * a.ndim)

    return pl.pallas_call(
        _moe_kernel,
        grid=(t // tm, N_EXPERTS // eb),
        in_specs=[pl.BlockSpec((tm, D_MODEL), lambda i, e: (i, 0)),
                  pl.BlockSpec((1, 6, D_MODEL), mod_idx),
                  full(norm2_g), full(w_router_t), full(router_bias),
                  pl.BlockSpec((eb, D_MODEL, EXPERT_DIM), lambda i, e: (e, 0, 0)),
                  pl.BlockSpec((eb, D_MODEL, EXPERT_DIM), lambda i, e: (e, 0, 0)),
                  pl.BlockSpec((eb, EXPERT_DIM, D_MODEL), lambda i, e: (e, 0, 0)),
                  full(wsg), full(wsu), full(wsd), full(final_g)],
        out_specs=pl.BlockSpec((tm, D_MODEL), lambda i, e: (i, 0)),
        out_shape=jax.ShapeDtypeStruct((t, D_MODEL), F32),
        scratch_shapes=[pltpu.VMEM((tm, D_MODEL), BF16),
                        pltpu.VMEM((tm, 128), F32),
                        pltpu.VMEM((tm, D_MODEL), F32)],
        compiler_params=pltpu.CompilerParams(dimension_semantics=("parallel", "arbitrary"),
                                             vmem_limit_bytes=VMEM_LIMIT),
        name="moe",
    )(x1, mod3, norm2_g, w_router_t, router_bias, weg, weu, wed, wsg, wsu, wsd, final_g)


def _dft_tables(seq_len):
    gd = FOURIER_GROUP_DIM
    kc = jnp.arange(gd, dtype=jnp.int32)
    ang_c = ((kc[:, None] * kc[None, :]) % gd).astype(F32) * (2.0 * math.pi / gd)
    cs = jnp.concatenate([jnp.cos(ang_c), jnp.sin(ang_c)], axis=1) * (gd ** -0.5)
    kl = jnp.arange(seq_len, dtype=jnp.int32)
    ang_l = ((kl[:, None] * kl[None, :]) % seq_len).astype(F32) * (2.0 * math.pi / seq_len)
    cls = jnp.concatenate([jnp.cos(ang_l), -jnp.sin(ang_l)], axis=1) * (seq_len ** -0.5)
    return cs.astype(BF16), cls.astype(BF16)


def _rope_tables(length):
    rows = length // GRID_W
    r = jnp.repeat(jnp.arange(rows, dtype=F32), GRID_W)
    col = jnp.tile(jnp.arange(GRID_W, dtype=F32), rows)
    nf = RET_HEAD_DIM // 4
    inv = ROPE_BASE ** (-jnp.arange(nf, dtype=F32) / nf)
    ar = r[:, None] * inv[None]
    ac = col[:, None] * inv[None]
    ang = jnp.concatenate([ar, ar, ac, ac], axis=-1)
    sign = jnp.where((jnp.arange(RET_HEAD_DIM) & nf) == 0, -1.0, 1.0).astype(F32)
    return jnp.cos(ang), jnp.sin(ang) * sign[None, :]


def _trunk_path(x, mod3, mod_row_of_batch, s0f, s0b, rope, lw):
    batch, seq_len, _ = x.shape
    x2d = x.reshape(batch * seq_len, D_MODEL)
    uf, q, k, v, sg, gf, gr = _inproj(x2d, mod3, lw["norm1_g"], lw["w_in"], seq_len, mod_row_of_batch, rope)
    r, s_f, s_b = _retention(q, k, v, sg, lw["dec"], lw["gn_g"], s0f, s0b, batch, seq_len)
    cs, cls = _dft_tables(seq_len)
    fmix = _fnet(uf, cs, cls, batch, seq_len)
    x1 = _merge(fmix, r, gf, gr, x2d, mod3, lw["w_four"], lw["w_ret"], lw["w_o"], seq_len, mod_row_of_batch)
    y = _moe(x1, mod3, lw["norm2_g"], lw["w_router_t"], lw["router_bias"], lw["weg"], lw["weu"], lw["wed"],
             lw["wsg"], lw["wsu"], lw["wsd"], lw["final_g"], seq_len, mod_row_of_batch)
    return y.reshape(batch, seq_len, D_MODEL), s_f, s_b


def kernel(x_prompt, x_sample, state_ret_fwd, state_ret_bwd, c, c_ctx, w_ada, b_ada, norm1_g, norm2_g, w_in,
           ret_decay_fwd, ret_decay_bwd, ret_gn_g, w_four_out, w_ret_out, w_out, w_router, router_bias,
           w_exp_gate, w_exp_up, w_exp_down, w_shared_gate, w_shared_up, w_shared_down, final_norm_g):
    depth = w_ada.shape[0]
    assert depth == 1, "final norm is fused into the last layer's MoE kernel"
    n_ctx, n_lat = x_prompt.shape[0], x_sample.shape[0]
    cond = jnp.concatenate([c_ctx[None, :], c], axis=0)
    cond = jnp.pad(cond, ((0, (-cond.shape[0]) % 8), (0, 0)))
    rope = _rope_tables(x_sample.shape[1])
    zeros = jnp.zeros((n_ctx, N_RET_HEADS, RET_HEAD_DIM, RET_HEAD_DIM), F32)

    layer = 0
    mod = _ada(cond, w_ada[layer], b_ada[layer][None, :])
    mod3 = mod.reshape(mod.shape[0], 6, D_MODEL)
    dec = jnp.stack([ret_decay_fwd[layer], ret_decay_bwd[layer]], axis=1)
    lw = {
        "norm1_g": norm1_g[layer][None, :],
        "norm2_g": norm2_g[layer][None, :],
        "w_in": w_in[layer].astype(BF16),
        "dec": jnp.broadcast_to(dec[:, :, None], (N_RET_HEADS, 2, RET_HEAD_DIM)).astype(F32),
        "gn_g": ret_gn_g[layer][None, :],
        "w_four": w_four_out[layer].astype(BF16),
        "w_ret": w_ret_out[layer].astype(BF16),
        "w_o": w_out[layer].astype(BF16),
        "w_router_t": w_router[layer].T,
        "router_bias": router_bias[layer][:, None],
        "weg": w_exp_gate[layer].astype(BF16),
        "weu": w_exp_up[layer].astype(BF16),
        "wed": w_exp_down[layer].astype(BF16),
        "wsg": w_shared_gate[layer].astype(BF16),
        "wsu": w_shared_up[layer].astype(BF16),
        "wsd": w_shared_down[layer].astype(BF16),
        "final_g": final_norm_g[None, :],
    }
    y_prompt, s_f, s_b = _trunk_path(x_prompt, mod3, lambda b: 0, zeros, zeros, None, lw)
    y_sample, _, _ = _trunk_path(x_sample, mod3, lambda b: 1 + b, state_ret_fwd[:, layer],
                                 state_ret_bwd[:, layer], rope, lw)
    return (y_prompt, y_sample, s_f[:, None], s_b[:, None])
```

```python
import functools
import math

import jax
import jax.numpy as jnp
from jax import lax
from jax.experimental import pallas as pl
from jax.experimental.pallas import tpu as pltpu
from jax.experimental.pallas import tpu_sc as plsc

F32 = jnp.float32
BF16 = jnp.bfloat16

D_MODEL = 1024
GRID_W = 64
N_FOURIER_GROUPS = 8
FOURIER_GROUP_DIM = 128
N_RET_HEADS = 4
RET_HEAD_DIM = 128
RET_WIDTH = N_RET_HEADS * RET_HEAD_DIM
CHUNK = 128
N_EXPERTS = 64
N_EXPERT_GROUPS = 8
EXPERTS_PER_GROUP = N_EXPERTS // N_EXPERT_GROUPS
TOPK_GROUPS = 4
TOP_K = 8
EXPERT_DIM = 256
ROUTED_SCALE = 2.5
ROPE_BASE = 10000.0
EPS = 1e-6
Q_SCALE = RET_HEAD_DIM ** -0.5

_C_UF = (0, 1024)
_C_Q = (1024, 1536)
_C_K = (1536, 2048)
_C_V = (2048, 2560)
_C_G = (2560, 3072)
_C_GF = (3072, 4096)
_C_GR = (4096, 5120)

VMEM_LIMIT = 56 * 1024 * 1024

TM_PROJ = 512
FNET_ROWS = 256
TM_ROUTER = 1024
TM_FINAL = 512
EXPERT_ROWS = 256
ROW_SLABS = 4
SC_CORES = 2
SC_WORKERS = 32
SC_CHUNK = 128


def _silu(x):
    return x * jax.nn.sigmoid(x)


def _dot(a, b):
    return jnp.dot(a, b, preferred_element_type=F32)


def _rms_mod(x, g, shift, scale):
    ms = jnp.mean(x * x, axis=-1, keepdims=True)
    y = x * lax.rsqrt(ms + EPS) * g
    return y * (1.0 + scale) + shift


def _ada_kernel(cond_ref, w_ref, b_ref, o_ref):
    s = _silu(cond_ref[...]).astype(BF16)
    o_ref[...] = _dot(s, w_ref[...].astype(BF16)) + b_ref[...]


def _ada(cond, w_ada, b_ada):
    rows, n = cond.shape[0], w_ada.shape[1]
    tn = 1536
    return pl.pallas_call(
        _ada_kernel,
        grid=(n // tn,),
        in_specs=[pl.BlockSpec((rows, D_MODEL), lambda j: (0, 0)),
                  pl.BlockSpec((D_MODEL, tn), lambda j: (0, j)),
                  pl.BlockSpec((1, tn), lambda j: (0, j))],
        out_specs=pl.BlockSpec((rows, tn), lambda j: (0, j)),
        out_shape=jax.ShapeDtypeStruct((rows, n), F32),
        compiler_params=pltpu.CompilerParams(vmem_limit_bytes=VMEM_LIMIT),
        name="ada",
    )(cond, w_ada, b_ada)


def _rope_head(x, cos, sin_signed, first_half):
    partner = jnp.where(first_half, pltpu.roll(x, 96, 1), pltpu.roll(x, 32, 1))
    return x * cos + partner * sin_signed


def _inproj_kernel(*refs, use_rope):
    if use_rope:
        x_ref, mod_ref, g_ref, w_ref, cos_ref, sin_ref = refs[:6]
        outs = refs[6:]
    else:
        x_ref, mod_ref, g_ref, w_ref = refs[:4]
        outs = refs[4:]
    uf_o, q_o, k_o, v_o, sg_o, gf_o, gr_o = outs

    h = _rms_mod(x_ref[...], g_ref[...], mod_ref[0, 0:1, :], mod_ref[0, 1:2, :])
    hb = h.astype(BF16)

    def proj(cols):
        return _dot(hb, w_ref[:, cols[0]:cols[1]])

    uf_o[...] = proj(_C_UF).astype(BF16)
    q = proj(_C_Q)
    k = proj(_C_K)
    if use_rope:
        cos = cos_ref[...]
        sin_signed = sin_ref[...]
        lane = lax.broadcasted_iota(jnp.int32, cos.shape, 1)
        first_half = (lane & 32) == 0
        for hd in range(N_RET_HEADS):
            sl = slice(hd * RET_HEAD_DIM, (hd + 1) * RET_HEAD_DIM)
            q_o[:, sl] = (_rope_head(q[:, sl], cos, sin_signed, first_half) * Q_SCALE).astype(BF16)
            k_o[:, sl] = _rope_head(k[:, sl], cos, sin_signed, first_half).astype(BF16)
    else:
        q_o[...] = (q * Q_SCALE).astype(BF16)
        k_o[...] = k.astype(BF16)
    v_o[...] = proj(_C_V).astype(BF16)
    sg_o[...] = _silu(proj(_C_G)).astype(BF16)
    gf_o[...] = jax.nn.sigmoid(proj(_C_GF)).astype(BF16)
    gr_o[...] = jax.nn.sigmoid(proj(_C_GR)).astype(BF16)


def _inproj(x2d, mod3, norm_g, w_in_bf, seq_len, mod_row_of_batch, rope):
    t = x2d.shape[0]
    tm = TM_PROJ
    tiles_per_seq = max(seq_len // tm, 1)

    def mod_idx(i):
        return (mod_row_of_batch((i * tm) // seq_len), 0, 0)

    in_specs = [pl.BlockSpec((tm, D_MODEL), lambda i: (i, 0)),
                pl.BlockSpec((1, 6, D_MODEL), mod_idx),
                pl.BlockSpec((1, D_MODEL), lambda i: (0, 0)),
                pl.BlockSpec(w_in_bf.shape, lambda i: (0, 0))]
    args = [x2d, mod3, norm_g, w_in_bf]
    if rope is not None:
        in_specs += [pl.BlockSpec((tm, RET_HEAD_DIM), lambda i: (i % tiles_per_seq, 0))] * 2
        args += list(rope)
    widths = [1024, RET_WIDTH, RET_WIDTH, RET_WIDTH, RET_WIDTH, 1024, 1024]
    return pl.pallas_call(
        functools.partial(_inproj_kernel, use_rope=rope is not None),
        grid=(t // tm,),
        in_specs=in_specs,
        out_specs=[pl.BlockSpec((tm, w), lambda i: (i, 0)) for w in widths],
        out_shape=[jax.ShapeDtypeStruct((t, w), BF16) for w in widths],
        compiler_params=pltpu.CompilerParams(dimension_semantics=("parallel",),
                                             vmem_limit_bytes=VMEM_LIMIT),
        name="inproj",
    )(*args)


def _retention_kernel(q_ref, k_ref, v_ref, sg_ref, dec_ref, gn_ref, s0f_ref, s0b_ref,
                      r_ref, sfo_ref, sbo_ref):
    n_chunks = q_ref.shape[0] // CHUNK
    dec = dec_ref[...]
    lg = jnp.minimum(dec, 0.0) - jnp.log1p(jnp.exp(-jnp.abs(dec)))
    lgf = lg[0:1, :]
    lgb = lg[1:2, :]
    row = lax.broadcasted_iota(jnp.int32, (CHUNK, CHUNK), 0).astype(F32)
    col = lax.broadcasted_iota(jnp.int32, (CHUNK, CHUNK), 1).astype(F32)
    diff = row - col
    decay = jnp.exp(jnp.where(diff >= 0, lgf * diff, lgb * (-diff)))
    qw_f = jnp.exp(lgf * (row + 1.0))
    qw_b = jnp.exp(lgb * (CHUNK - row))
    kw_f = jnp.exp(lgf * (CHUNK - 1.0 - row))
    kw_b = jnp.exp(lgb * row)
    gc_f = jnp.exp(lgf * CHUNK)
    gc_b = jnp.exp(lgb * CHUNK)

    def rows(n):
        return slice(n * CHUNK, (n + 1) * CHUNK)

    kv_f, kv_b = [], []
    for n in range(n_chunks):
        kn = k_ref[rows(n), :].astype(F32)
        vn = v_ref[rows(n), :]
        kv_f.append(_dot((kn * kw_f).T.astype(BF16), vn))
        kv_b.append(_dot((kn * kw_b).T.astype(BF16), vn))

    s = s0f_ref[...]
    prev_f = []
    for n in range(n_chunks):
        prev_f.append(s.astype(BF16))
        s = gc_f * s + kv_f[n]
    sfo_ref[...] = s
    s = s0b_ref[...]
    prev_b = [None] * n_chunks
    for n in reversed(range(n_chunks)):
        prev_b[n] = s.astype(BF16)
        s = gc_b * s + kv_b[n]
    sbo_ref[...] = s

    gn = gn_ref[...]
    for n in range(n_chunks):
        qn = q_ref[rows(n), :]
        qf = qn.astype(F32)
        scores = lax.dot_general(qn, k_ref[rows(n), :], (((1,), (1,)), ((), ())),
                                 preferred_element_type=F32)
        o = _dot((scores * decay).astype(BF16), v_ref[rows(n), :])
        o = o + _dot((qf * qw_f).astype(BF16), prev_f[n])
        o = o + _dot((qf * qw_b).astype(BF16), prev_b[n])
        mu = jnp.mean(o, axis=-1, keepdims=True)
        d = o - mu
        var = jnp.mean(d * d, axis=-1, keepdims=True)
        on = d * lax.rsqrt(var + EPS) * gn
        r_ref[rows(n), :] = (on * sg_ref[rows(n), :].astype(F32)).astype(BF16)


def _retention(q, k, v, sg, dec, gn_g, s0f, s0b, batch, seq_len):
    hd = RET_HEAD_DIM
    tok_spec = pl.BlockSpec((seq_len, hd), lambda b, h: (b, h))
    st_spec = pl.BlockSpec((None, None, hd, hd), lambda b, h: (b, h, 0, 0))
    st_shape = jax.ShapeDtypeStruct((batch, N_RET_HEADS, hd, hd), F32)
    return pl.pallas_call(
        _retention_kernel,
        grid=(batch, N_RET_HEADS),
        in_specs=[tok_spec, tok_spec, tok_spec, tok_spec,
                  pl.BlockSpec((None, 2, hd), lambda b, h: (h, 0, 0)),
                  pl.BlockSpec((1, hd), lambda b, h: (0, h)),
                  st_spec, st_spec],
        out_specs=[tok_spec, st_spec, st_spec],
        out_shape=[jax.ShapeDtypeStruct((batch * seq_len, RET_WIDTH), BF16), st_shape, st_shape],
        compiler_params=pltpu.CompilerParams(dimension_semantics=("parallel", "parallel"),
                                             vmem_limit_bytes=VMEM_LIMIT),
        name="retention",
    )(q, k, v, sg, dec, gn_g, s0f, s0b)


def _fnet_kernel(uf_ref, cs_ref, cls_ref, o_ref, xcs_ref):
    seq_len = uf_ref.shape[0]
    gd = FOURIER_GROUP_DIM

    @pl.when(pl.program_id(1) == 0)
    def _():
        for g in range(N_FOURIER_GROUPS):
            x = _dot(uf_ref[:, g * gd:(g + 1) * gd], cs_ref[...])
            xcs_ref[0:seq_len, g * gd:(g + 1) * gd] = x[:, :gd].astype(BF16)
            xcs_ref[seq_len:2 * seq_len, g * gd:(g + 1) * gd] = x[:, gd:].astype(BF16)

    o_ref[...] = _dot(cls_ref[...], xcs_ref[...]).astype(BF16)


def _fnet(uf, cs, cls, batch, seq_len):
    rb = FNET_ROWS
    nr = seq_len // rb
    return pl.pallas_call(
        _fnet_kernel,
        grid=(batch, nr),
        in_specs=[pl.BlockSpec((seq_len, D_MODEL), lambda b, r: (b, 0)),
                  pl.BlockSpec(cs.shape, lambda b, r: (0, 0)),
                  pl.BlockSpec((rb, 2 * seq_len), lambda b, r: (r, 0))],
        out_specs=pl.BlockSpec((rb, D_MODEL), lambda b, r: (b * nr + r, 0)),
        out_shape=jax.ShapeDtypeStruct((batch * seq_len, D_MODEL), BF16),
        scratch_shapes=[pltpu.VMEM((2 * seq_len, D_MODEL), BF16)],
        compiler_params=pltpu.CompilerParams(dimension_semantics=("parallel", "arbitrary"),
                                             vmem_limit_bytes=VMEM_LIMIT),
        name="fnet",
    )(uf, cs, cls)


def _merge_kernel(fm_ref, r_ref, gf_ref, gr_ref, x_ref, mod_ref, wf_ref, wr_ref, wo_ref, o_ref):
    f_out = _dot(fm_ref[...], wf_ref[...])
    r_out = _dot(r_ref[...], wr_ref[...])
    merged = gf_ref[...].astype(F32) * f_out + gr_ref[...].astype(F32) * r_out
    mix = _dot(merged.astype(BF16), wo_ref[...])
    o_ref[...] = x_ref[...] + mod_ref[0, 2:3, :] * mix


def _merge(fmix, r, gf, gr, x2d, mod3, w_four, w_ret, w_o, seq_len, mod_row_of_batch):
    t = x2d.shape[0]
    tm = TM_PROJ

    def mod_idx(i):
        return (mod_row_of_batch((i * tm) // seq_len), 0, 0)

    def tok(w):
        return pl.BlockSpec((tm, w), lambda i: (i, 0))

    def full(a):
        return pl.BlockSpec(a.shape, lambda i: (0, 0))

    return pl.pallas_call(
        _merge_kernel,
        grid=(t // tm,),
        in_specs=[tok(D_MODEL), tok(RET_WIDTH), tok(D_MODEL), tok(D_MODEL), tok(D_MODEL),
                  pl.BlockSpec((1, 6, D_MODEL), mod_idx), full(w_four), full(w_ret), full(w_o)],
        out_specs=tok(D_MODEL),
        out_shape=jax.ShapeDtypeStruct((t, D_MODEL), F32),
        compiler_params=pltpu.CompilerParams(dimension_semantics=("parallel",),
                                             vmem_limit_bytes=VMEM_LIMIT),
        name="merge",
    )(fmix, r, gf, gr, x2d, mod3, w_four, w_ret, w_o)


def _pack_pair(lo_f32, hi_f32):
    lo = lax.bitcast_convert_type(lo_f32.astype(BF16).astype(F32), jnp.uint32)
    hi = lax.bitcast_convert_type(hi_f32.astype(BF16).astype(F32), jnp.uint32)
    return lax.bitcast_convert_type((lo >> 16) | hi, jnp.int32)


def _unpack_pair(words_i32):
    w = lax.bitcast_convert_type(words_i32, jnp.uint32)
    lo = lax.bitcast_convert_type(w << 16, F32)
    hi = lax.bitcast_convert_type(w & jnp.uint32(0xFFFF0000), F32)
    return lo, hi


def _load_token_words(ref, lead, n_tok):
    parts = []
    for s in range(ROW_SLABS):
        idx = (pl.ds(s, n_tok, stride=ROW_SLABS), slice(None))
        parts.append(ref[lead + idx] if lead else ref[idx])
    return jnp.concatenate(parts, axis=1)


def _store_token_words(ref, words, n_tok):
    for s in range(ROW_SLABS):
        ref[pl.ds(s, n_tok, stride=ROW_SLABS), :] = words[:, s * 128:(s + 1) * 128]


def _route(scores, biased):
    tokens = scores.shape[1]
    neg = -jnp.inf
    epg = EXPERTS_PER_GROUP
    iota_g = lax.broadcasted_iota(jnp.int32, (epg, tokens), 0).astype(F32)

    def pick_first_max(cur, iota, size):
        m = jnp.max(cur, axis=0, keepdims=True)
        idx = jnp.min(jnp.where(cur == m, iota, float(size)), axis=0, keepdims=True)
        return m, idx, iota == idx

    group_scores = []
    for g in range(N_EXPERT_GROUPS):
        vals = biased[g * epg:(g + 1) * epg, :]
        m1, _, hit = pick_first_max(vals, iota_g, epg)
        m2 = jnp.max(jnp.where(hit, neg, vals), axis=0, keepdims=True)
        group_scores.append(m1 + m2)
    cur = jnp.concatenate(group_scores, axis=0)
    group_sel = jnp.zeros_like(cur)
    for _ in range(TOPK_GROUPS):
        _, _, hit = pick_first_max(cur, iota_g, N_EXPERT_GROUPS)
        group_sel = jnp.where(hit, 1.0, group_sel)
        cur = jnp.where(hit, neg, cur)
    masked = jnp.concatenate(
        [jnp.where(group_sel[g:g + 1, :] > 0.0, biased[g * epg:(g + 1) * epg, :], neg)
         for g in range(N_EXPERT_GROUPS)], axis=0)
    iota_e = lax.broadcasted_iota(jnp.int32, masked.shape, 0).astype(F32)
    sel = jnp.zeros_like(masked)
    cur = masked
    picks = []
    for _ in range(TOP_K):
        _, idx, hit = pick_first_max(cur, iota_e, N_EXPERTS)
        picks.append(idx)
        sel = jnp.where(hit, 1.0, sel)
        cur = jnp.where(hit, neg, cur)
    w = scores * sel
    return w / jnp.sum(w, axis=0, keepdims=True) * ROUTED_SCALE, sel, picks


def _router_kernel(x_ref, mod_ref, g2_ref, wrt_ref, rb_ref, hp_ref, ek_ref, rk_ref, wt_ref, cnt_ref, run_scr):
    tm = x_ref.shape[0]

    @pl.when(pl.program_id(0) == 0)
    def _():
        run_scr[...] = jnp.zeros_like(run_scr)

    h = _rms_mod(x_ref[...], g2_ref[...], mod_ref[0, 3:4, :], mod_ref[0, 4:5, :])
    half = D_MODEL // 2
    _store_token_words(hp_ref, _pack_pair(h[:, :half], h[:, half:]), tm)

    logits_t = lax.dot_general(wrt_ref[...], h, (((1,), (1,)), ((), ())),
                               precision=lax.Precision.HIGHEST, preferred_element_type=F32)
    scores = jax.nn.sigmoid(logits_t)
    comb_t, sel, picks = _route(scores, scores + rb_ref[...])

    earlier = (lax.broadcasted_iota(jnp.int32, (tm, tm), 0) < lax.broadcasted_iota(jnp.int32, (tm, tm), 1))
    rank_t = _dot(sel.astype(BF16), jnp.where(earlier, 1.0, 0.0).astype(BF16)) + run_scr[...]
    run_scr[...] += jnp.sum(sel, axis=1, keepdims=True)
    cnt_ref[...] = jnp.broadcast_to(run_scr[...], cnt_ref.shape)

    iota_e = lax.broadcasted_iota(jnp.int32, sel.shape, 0).astype(F32)
    ranks, weights = [], []
    for idx in picks:
        hit = iota_e == idx
        ranks.append(jnp.sum(jnp.where(hit, rank_t, 0.0), axis=0, keepdims=True))
        weights.append(jnp.sum(jnp.where(hit, comb_t, 0.0), axis=0, keepdims=True))
    ek_ref[...] = jnp.concatenate(picks, axis=0).astype(jnp.int32)
    rk_ref[...] = jnp.concatenate(ranks, axis=0).astype(jnp.int32)
    w_pad = jnp.concatenate(weights + [jnp.zeros((128 - TOP_K, tm), F32)], axis=0)
    wt_ref[...] = w_pad.T


def _router(x1, mod3, norm2_g, w_router_t, router_bias, seq_len, mod_row_of_batch):
    t = x1.shape[0]
    tm = TM_ROUTER

    def mod_idx(i):
        return (mod_row_of_batch((i * tm) // seq_len), 0, 0)

    def full(a):
        return pl.BlockSpec(a.shape, lambda i: (0,) * a.ndim)

    return pl.pallas_call(
        _router_kernel,
        grid=(t // tm,),
        in_specs=[pl.BlockSpec((tm, D_MODEL), lambda i: (i, 0)),
                  pl.BlockSpec((1, 6, D_MODEL), mod_idx),
                  full(norm2_g), full(w_router_t), full(router_bias)],
        out_specs=[pl.BlockSpec((tm * ROW_SLABS, 128), lambda i: (i, 0)),
                   pl.BlockSpec((TOP_K, tm), lambda i: (0, i)),
                   pl.BlockSpec((TOP_K, tm), lambda i: (0, i)),
                   pl.BlockSpec((tm, 128), lambda i: (i, 0)),
                   pl.BlockSpec((N_EXPERTS, 128), lambda i: (0, 0))],
        out_shape=[jax.ShapeDtypeStruct((t * ROW_SLABS, 128), jnp.int32),
                   jax.ShapeDtypeStruct((TOP_K, t), jnp.int32),
                   jax.ShapeDtypeStruct((TOP_K, t), jnp.int32),
                   jax.ShapeDtypeStruct((t, 128), F32),
                   jax.ShapeDtypeStruct((N_EXPERTS, 128), F32)],
        scratch_shapes=[pltpu.VMEM((N_EXPERTS, 1), F32)],
        compiler_params=pltpu.CompilerParams(dimension_semantics=("arbitrary",),
                                             vmem_limit_bytes=VMEM_LIMIT),
        name="router",
    )(x1, mod3, norm2_g, w_router_t, router_bias)


def _plan_kernel(ek_ref, rk_ref, cnt_ref, pos_ref, texp_ref, nused_ref):
    rows = float(EXPERT_ROWS)
    cnt = cnt_ref[:, 0:1]
    tiles = jnp.floor((cnt + (rows - 1.0)) / rows)
    before = (lax.broadcasted_iota(jnp.int32, (N_EXPERTS, N_EXPERTS), 1)
              < lax.broadcasted_iota(jnp.int32, (N_EXPERTS, N_EXPERTS), 0))
    tile_start = jnp.dot(jnp.where(before, 1.0, 0.0), jnp.broadcast_to(tiles, (N_EXPERTS, 128)),
                         precision=lax.Precision.HIGHEST, preferred_element_type=F32)[:, 0:1]
    tile_end = tile_start + tiles
    row_start = tile_start * rows

    ek = ek_ref[...]
    pos = rk_ref[...].astype(F32)
    tile_id = lax.broadcasted_iota(jnp.int32, texp_ref.shape, 1).astype(F32)
    texp = jnp.zeros(texp_ref.shape, F32)
    for e in range(N_EXPERTS):
        pos = pos + jnp.where(ek == e, row_start[e:e + 1, :], 0.0)
        texp = texp + jnp.where(tile_id >= tile_end[e:e + 1, :], 1.0, 0.0)
    pos_ref[...] = pos.astype(jnp.int32)
    texp_ref[...] = jnp.minimum(texp, N_EXPERTS - 1.0).astype(jnp.int32)
    nused_ref[...] = jnp.broadcast_to(tile_end[N_EXPERTS - 1:N_EXPERTS, :], nused_ref.shape).astype(jnp.int32)


def _plan(ek, rk, cnt, n_tiles_pad):
    t = ek.shape[1]

    def full(shape):
        return pl.BlockSpec(shape, lambda: (0,) * len(shape))

    return pl.pallas_call(
        _plan_kernel,
        in_specs=[full(ek.shape), full(rk.shape), full(cnt.shape)],
        out_specs=[full((TOP_K, t)), full((1, n_tiles_pad)), full((1, 128))],
        out_shape=[jax.ShapeDtypeStruct((TOP_K, t), jnp.int32),
                   jax.ShapeDtypeStruct((1, n_tiles_pad), jnp.int32),
                   jax.ShapeDtypeStruct((1, 128), jnp.int32)],
        compiler_params=pltpu.CompilerParams(vmem_limit_bytes=VMEM_LIMIT),
        name="plan",
    )(ek, rk, cnt)


def _sc_mesh():
    return plsc.VectorSubcoreMesh(core_axis_name="c", subcore_axis_name="s")


def _sc_dispatch(rows, pos3, n_out):
    t = rows.shape[0]
    ch = SC_CHUNK
    per_w = (t // ch) // SC_WORKERS

    @functools.partial(
        pl.kernel, out_type=jax.ShapeDtypeStruct((n_out,) + rows.shape[1:], jnp.int32), mesh=_sc_mesh(),
        scratch_types=[pltpu.VMEM((TOP_K, ch), jnp.int32), pltpu.VMEM((ch,) + rows.shape[1:], jnp.int32),
                       pltpu.SemaphoreType.DMA])
    def k(rows_hbm, pos_hbm, out_hbm, idx_v, rows_v, sem):
        wid = lax.axis_index("s") * SC_CORES + lax.axis_index("c")

        @pl.loop(0, per_w)
        def _(j):
            c = wid * per_w + j
            pltpu.sync_copy(pos_hbm.at[c], idx_v)
            pltpu.sync_copy(rows_hbm.at[pl.ds(c * ch, ch)], rows_v)
            copies = [pltpu.async_copy(rows_v, out_hbm.at[idx_v.at[kk]], sem) for kk in range(TOP_K)]
            for cp in copies:
                cp.wait()

    return k(rows, pos3)


def _sc_gather(table, pos3, t):
    ch = SC_CHUNK
    per_w = (t // ch) // SC_WORKERS

    @functools.partial(
        pl.kernel, out_type=jax.ShapeDtypeStruct((TOP_K, t) + table.shape[1:], jnp.int32), mesh=_sc_mesh(),
        scratch_types=[pltpu.VMEM((TOP_K, ch), jnp.int32), pltpu.VMEM((ch,) + table.shape[1:], jnp.int32),
                       pltpu.SemaphoreType.DMA])
    def k(tab_hbm, pos_hbm, out_hbm, idx_v, rows_v, sem):
        wid = lax.axis_index("s") * SC_CORES + lax.axis_index("c")

        @pl.loop(0, per_w)
        def _(j):
            c = wid * per_w + j
            pltpu.sync_copy(pos_hbm.at[c], idx_v)
            for kk in range(TOP_K):
                pltpu.async_copy(tab_hbm.at[idx_v.at[kk]], rows_v, sem).wait()
                pltpu.sync_copy(rows_v, out_hbm.at[kk, pl.ds(c * ch, ch)])

    return k(table, pos3)


def _experts_kernel(texp_ref, nused_ref, xs_ref, weg_ref, weu_ref, wed_ref, ys_ref, wg_scr, wu_scr, wd_scr):
    j = pl.program_id(0)
    rows = EXPERT_ROWS
    half = D_MODEL // 2
    used = j < nused_ref[0]
    new_expert = (j == 0) | (texp_ref[j] != texp_ref[jnp.maximum(j - 1, 0)])

    @pl.when(used & new_expert)
    def _():
        wg_scr[...] = weg_ref[0].astype(BF16)
        wu_scr[...] = weu_ref[0].astype(BF16)
        wd_scr[...] = wed_ref[0].astype(BF16)

    @pl.when(used)
    def _():
        lo, hi = _unpack_pair(_load_token_words(xs_ref, (), rows))
        lo = lo.astype(BF16)
        hi = hi.astype(BF16)
        g = _dot(lo, wg_scr[0:half, :]) + _dot(hi, wg_scr[half:D_MODEL, :])
        u = _dot(lo, wu_scr[0:half, :]) + _dot(hi, wu_scr[half:D_MODEL, :])
        y = _dot((_silu(g) * u).astype(BF16), wd_scr[...])
        _store_token_words(ys_ref, _pack_pair(y[:, :half], y[:, half:]), rows)

    @pl.when(jnp.logical_not(used))
    def _():
        ys_ref[...] = jnp.zeros_like(ys_ref)


def _experts(texp, nused, xs2d, weg, weu, wed, n_tiles):
    rows = EXPERT_ROWS
    grid_spec = pltpu.PrefetchScalarGridSpec(
        num_scalar_prefetch=2,
        grid=(n_tiles,),
        in_specs=[pl.BlockSpec((rows * ROW_SLABS, 128), lambda j, te, nu: (j, 0)),
                  pl.BlockSpec((1, D_MODEL, EXPERT_DIM), lambda j, te, nu: (te[j], 0, 0)),
                  pl.BlockSpec((1, D_MODEL, EXPERT_DIM), lambda j, te, nu: (te[j], 0, 0)),
                  pl.BlockSpec((1, EXPERT_DIM, D_MODEL), lambda j, te, nu: (te[j], 0, 0))],
        out_specs=pl.BlockSpec((rows * ROW_SLABS, 128), lambda j, te, nu: (j, 0)),
        scratch_shapes=[pltpu.VMEM((D_MODEL, EXPERT_DIM), BF16),
                        pltpu.VMEM((D_MODEL, EXPERT_DIM), BF16),
                        pltpu.VMEM((EXPERT_DIM, D_MODEL), BF16)],
    )
    return pl.pallas_call(
        _experts_kernel,
        grid_spec=grid_spec,
        out_shape=jax.ShapeDtypeStruct(xs2d.shape, jnp.int32),
        compiler_params=pltpu.CompilerParams(dimension_semantics=("arbitrary",),
                                             vmem_limit_bytes=VMEM_LIMIT),
        name="experts",
    )(texp, nused, xs2d, weg, weu, wed)


def _final_kernel(x_ref, hp_ref, y8_ref, wt_ref, mod_ref, wsg_ref, wsu_ref, wsd_ref, fng_ref, o_ref):
    tm = x_ref.shape[0]
    lo, hi = _unpack_pair(_load_token_words(hp_ref, (), tm))
    hb = jnp.concatenate([lo, hi], axis=1).astype(BF16)
    shared = _dot((_silu(_dot(hb, wsg_ref[...])) * _dot(hb, wsu_ref[...])).astype(BF16), wsd_ref[...])
    wt = wt_ref[...]
    r_lo = jnp.zeros((tm, D_MODEL // 2), F32)
    r_hi = jnp.zeros((tm, D_MODEL // 2), F32)
    for k in range(TOP_K):
        lo, hi = _unpack_pair(_load_token_words(y8_ref, (k,), tm))
        wk = wt[:, k:k + 1]
        r_lo = r_lo + wk * lo
        r_hi = r_hi + wk * hi
    routed = jnp.concatenate([r_lo, r_hi], axis=1)
    y = x_ref[...] + mod_ref[0, 5:6, :] * (routed + shared)
    ms = jnp.mean(y * y, axis=-1, keepdims=True)
    o_ref[...] = y * lax.rsqrt(ms + EPS) * fng_ref[...]


def _final(x1, hp2d, y8, wtok, mod3, wsg, wsu, wsd, final_g, seq_len, mod_row_of_batch):
    t = x1.shape[0]
    tm = TM_FINAL

    def mod_idx(i):
        return (mod_row_of_batch((i * tm) // seq_len), 0, 0)

    def full(a):
        return pl.BlockSpec(a.shape, lambda i: (0,) * a.ndim)

    return pl.pallas_call(
        _final_kernel,
        grid=(t // tm,),
        in_specs=[pl.BlockSpec((tm, D_MODEL), lambda i: (i, 0)),
                  pl.BlockSpec((tm * ROW_SLABS, 128), lambda i: (i, 0)),
                  pl.BlockSpec((TOP_K, tm * ROW_SLABS, 128), lambda i: (0, i, 0)),
                  pl.BlockSpec((tm, 128), lambda i: (i, 0)),
                  pl.BlockSpec((1, 6, D_MODEL), mod_idx),
                  full(wsg), full(wsu), full(wsd), full(final_g)],
        out_specs=pl.BlockSpec((tm, D_MODEL), lambda i: (i, 0)),
        out_shape=jax.ShapeDtypeStruct((t, D_MODEL), F32),
        compiler_params=pltpu.CompilerParams(dimension_semantics=("parallel",),
                                             vmem_limit_bytes=VMEM_LIMIT),
        name="final",
    )(x1, hp2d, y8, wtok, mod3, wsg, wsu, wsd, final_g)


def _moe(x1, mod3, lw, seq_len, mod_row_of_batch):
    t = x1.shape[0]
    n_tiles = TOP_K * t // EXPERT_ROWS + N_EXPERTS
    n_tiles_pad = -(-n_tiles // 128) * 128
    hp2d, ek, rk, wtok, cnt = _router(x1, mod3, lw["norm2_g"], lw["w_router_t"], lw["router_bias"],
                                      seq_len, mod_row_of_batch)
    pos, texp, nused = _plan(ek, rk, cnt, n_tiles_pad)
    pos3 = pos.reshape(TOP_K, t // SC_CHUNK, SC_CHUNK).transpose(1, 0, 2)
    xs = _sc_dispatch(hp2d.reshape(t, ROW_SLABS, 128), pos3, n_tiles * EXPERT_ROWS)
    ys2d = _experts(texp.reshape(-1), nused.reshape(-1), xs.reshape(-1, 128),
                    lw["weg"], lw["weu"], lw["wed"], n_tiles)
    y8 = _sc_gather(ys2d.reshape(-1, ROW_SLABS, 128), pos3, t)
    return _final(x1, hp2d, y8.reshape(TOP_K, t * ROW_SLABS, 128), wtok, mod3,
                  lw["wsg"], lw["wsu"], lw["wsd"], lw["final_g"], seq_len, mod_row_of_batch)


def _dft_tables(seq_len):
    gd = FOURIER_GROUP_DIM
    kc = jnp.arange(gd, dtype=jnp.int32)
    ang_c = ((kc[:, None] * kc[None, :]) % gd).astype(F32) * (2.0 * math.pi / gd)
    cs = jnp.concatenate([jnp.cos(ang_c), jnp.sin(ang_c)], axis=1) * (gd ** -0.5)
    kl = jnp.arange(seq_len, dtype=jnp.int32)
    ang_l = ((kl[:, None] * kl[None, :]) % seq_len).astype(F32) * (2.0 * math.pi / seq_len)
    cls = jnp.concatenate([jnp.cos(ang_l), -jnp.sin(ang_l)], axis=1) * (seq_len ** -0.5)
    return cs.astype(BF16), cls.astype(BF16)


def _rope_tables(length):
    rows = length // GRID_W
    r = jnp.repeat(jnp.arange(rows, dtype=F32), GRID_W)
    col = jnp.tile(jnp.arange(GRID_W, dtype=F32), rows)
    nf = RET_HEAD_DIM // 4
    inv = ROPE_BASE ** (-jnp.arange(nf, dtype=F32) / nf)
    ar = r[:, None] * inv[None]
    ac = col[:, None] * inv[None]
    ang = jnp.concatenate([ar, ar, ac, ac], axis=-1)
    sign = jnp.where((jnp.arange(RET_HEAD_DIM) & nf) == 0, -1.0, 1.0).astype(F32)
    return jnp.cos(ang), jnp.sin(ang) * sign[None, :]


def _trunk_path(x, mod3, mod_row_of_batch, s0f, s0b, rope, lw):
    batch, seq_len, _ = x.shape
    x2d = x.reshape(batch * seq_len, D_MODEL)
    uf, q, k, v, sg, gf, gr = _inproj(x2d, mod3, lw["norm1_g"], lw["w_in"], seq_len, mod_row_of_batch, rope)
    r, s_f, s_b = _retention(q, k, v, sg, lw["dec"], lw["gn_g"], s0f, s0b, batch, seq_len)
    cs, cls = _dft_tables(seq_len)
    fmix = _fnet(uf, cs, cls, batch, seq_len)
    x1 = _merge(fmix, r, gf, gr, x2d, mod3, lw["w_four"], lw["w_ret"], lw["w_o"], seq_len, mod_row_of_batch)
    y = _moe(x1, mod3, lw, seq_len, mod_row_of_batch)
    return y.reshape(batch, seq_len, D_MODEL), s_f, s_b


def kernel(x_prompt, x_sample, state_ret_fwd, state_ret_bwd, c, c_ctx, w_ada, b_ada, norm1_g, norm2_g, w_in,
           ret_decay_fwd, ret_decay_bwd, ret_gn_g, w_four_out, w_ret_out, w_out, w_router, router_bias,
           w_exp_gate, w_exp_up, w_exp_down, w_shared_gate, w_shared_up, w_shared_down, final_norm_g):
    depth = w_ada.shape[0]
    assert depth == 1, "final norm is fused into the last layer's MoE kernel"
    n_ctx, n_lat = x_prompt.shape[0], x_sample.shape[0]
    cond = jnp.concatenate([c_ctx[None, :], c], axis=0)
    cond = jnp.pad(cond, ((0, (-cond.shape[0]) % 8), (0, 0)))
    rope = _rope_tables(x_sample.shape[1])
    zeros = jnp.zeros((n_ctx, N_RET_HEADS, RET_HEAD_DIM, RET_HEAD_DIM), F32)

    layer = 0
    mod = _ada(cond, w_ada[layer], b_ada[layer][None, :])
    mod3 = mod.reshape(mod.shape[0], 6, D_MODEL)
    dec = jnp.stack([ret_decay_fwd[layer], ret_decay_bwd[layer]], axis=1)
    lw = {
        "norm1_g": norm1_g[layer][None, :],
        "norm2_g": norm2_g[layer][None, :],
        "w_in": w_in[layer].astype(BF16),
        "dec": jnp.broadcast_to(dec[:, :, None], (N_RET_HEADS, 2, RET_HEAD_DIM)).astype(F32),
        "gn_g": ret_gn_g[layer][None, :],
        "w_four": w_four_out[layer].astype(BF16),
        "w_ret": w_ret_out[layer].astype(BF16),
        "w_o": w_out[layer].astype(BF16),
        "w_router_t": w_router[layer].T,
        "router_bias": router_bias[layer][:, None],
        "weg": w_exp_gate[layer],
        "weu": w_exp_up[layer],
        "wed": w_exp_down[layer],
        "wsg": w_shared_gate[layer].astype(BF16),
        "wsu": w_shared_up[layer].astype(BF16),
        "wsd": w_shared_down[layer].astype(BF16),
        "final_g": final_norm_g[None, :],
    }
    y_prompt, s_f, s_b = _trunk_path(x_prompt, mod3, lambda b: 0, zeros, zeros, None, lw)
    y_sample, _, _ = _trunk_path(x_sample, mod3, lambda b: 1 + b, state_ret_fwd[:, layer],
                                 state_ret_bwd[:, layer], rope, lw)
    return (y_prompt, y_sample, s_f[:, None], s_b[:, None])
```

```python
import functools
import math

import jax
import jax.numpy as jnp
from jax import lax
from jax.experimental import pallas as pl
from jax.experimental.pallas import tpu as pltpu
from jax.experimental.pallas import tpu_sc as plsc

F32 = jnp.float32
BF16 = jnp.bfloat16

D_MODEL = 1024
GRID_W = 64
N_FOURIER_GROUPS = 8
FOURIER_GROUP_DIM = 128
N_RET_HEADS = 4
RET_HEAD_DIM = 128
RET_WIDTH = N_RET_HEADS * RET_HEAD_DIM
CHUNK = 128
N_EXPERTS = 64
N_EXPERT_GROUPS = 8
EXPERTS_PER_GROUP = N_EXPERTS // N_EXPERT_GROUPS
TOPK_GROUPS = 4
TOP_K = 8
EXPERT_DIM = 256
ROUTED_SCALE = 2.5
ROPE_BASE = 10000.0
EPS = 1e-6
Q_SCALE = RET_HEAD_DIM ** -0.5

_C_UF = (0, 1024)
_C_Q = (1024, 1536)
_C_K = (1536, 2048)
_C_V = (2048, 2560)
_C_G = (2560, 3072)
_C_GF = (3072, 4096)
_C_GR = (4096, 5120)

VMEM_LIMIT = 56 * 1024 * 1024

TM_PROJ = 512
FNET_ROWS = 256
TM_ROUTER = 1024
TM_FINAL = 512
EXPERT_ROWS = 256
ROW_SLABS = 4
SC_CORES = 2
SC_WORKERS = 32
SC_CHUNK = 128


def _silu(x):
    return x * jax.nn.sigmoid(x)


def _dot(a, b):
    return jnp.dot(a, b, preferred_element_type=F32)


def _rms_mod(x, g, shift, scale):
    ms = jnp.mean(x * x, axis=-1, keepdims=True)
    y = x * lax.rsqrt(ms + EPS) * g
    return y * (1.0 + scale) + shift


def _ada_kernel(cond_ref, w_ref, b_ref, o_ref):
    s = _silu(cond_ref[...]).astype(BF16)
    o_ref[...] = _dot(s, w_ref[...].astype(BF16)) + b_ref[...]


def _ada(cond, w_ada, b_ada):
    rows, n = cond.shape[0], w_ada.shape[1]
    tn = 1536
    return pl.pallas_call(
        _ada_kernel,
        grid=(n // tn,),
        in_specs=[pl.BlockSpec((rows, D_MODEL), lambda j: (0, 0)),
                  pl.BlockSpec((D_MODEL, tn), lambda j: (0, j)),
                  pl.BlockSpec((1, tn), lambda j: (0, j))],
        out_specs=pl.BlockSpec((rows, tn), lambda j: (0, j)),
        out_shape=jax.ShapeDtypeStruct((rows, n), F32),
        compiler_params=pltpu.CompilerParams(vmem_limit_bytes=VMEM_LIMIT),
        name="ada",
    )(cond, w_ada, b_ada)


def _rope_head(x, cos, sin_signed, first_half):
    partner = jnp.where(first_half, pltpu.roll(x, 96, 1), pltpu.roll(x, 32, 1))
    return x * cos + partner * sin_signed


def _inproj_kernel(*refs, use_rope):
    if use_rope:
        x_ref, mod_ref, g_ref, w_ref, cos_ref, sin_ref = refs[:6]
        outs = refs[6:]
    else:
        x_ref, mod_ref, g_ref, w_ref = refs[:4]
        outs = refs[4:]
    uf_o, q_o, k_o, v_o, sg_o, gf_o, gr_o = outs

    h = _rms_mod(x_ref[...], g_ref[...], mod_ref[0, 0:1, :], mod_ref[0, 1:2, :])
    hb = h.astype(BF16)

    def proj(cols):
        return _dot(hb, w_ref[:, cols[0]:cols[1]])

    uf_o[...] = proj(_C_UF).astype(BF16)
    q = proj(_C_Q)
    k = proj(_C_K)
    if use_rope:
        cos = cos_ref[...]
        sin_signed = sin_ref[...]
        lane = lax.broadcasted_iota(jnp.int32, cos.shape, 1)
        first_half = (lane & 32) == 0
        for hd in range(N_RET_HEADS):
            sl = slice(hd * RET_HEAD_DIM, (hd + 1) * RET_HEAD_DIM)
            q_o[:, sl] = (_rope_head(q[:, sl], cos, sin_signed, first_half) * Q_SCALE).astype(BF16)
            k_o[:, sl] = _rope_head(k[:, sl], cos, sin_signed, first_half).astype(BF16)
    else:
        q_o[...] = (q * Q_SCALE).astype(BF16)
        k_o[...] = k.astype(BF16)
    v_o[...] = proj(_C_V).astype(BF16)
    sg_o[...] = _silu(proj(_C_G)).astype(BF16)
    gf_o[...] = jax.nn.sigmoid(proj(_C_GF)).astype(BF16)
    gr_o[...] = jax.nn.sigmoid(proj(_C_GR)).astype(BF16)


def _inproj(x2d, mod3, norm_g, w_in_bf, seq_len, mod_row_of_batch, rope):
    t = x2d.shape[0]
    tm = TM_PROJ
    tiles_per_seq = max(seq_len // tm, 1)

    def mod_idx(i):
        return (mod_row_of_batch((i * tm) // seq_len), 0, 0)

    in_specs = [pl.BlockSpec((tm, D_MODEL), lambda i: (i, 0)),
                pl.BlockSpec((1, 6, D_MODEL), mod_idx),
                pl.BlockSpec((1, D_MODEL), lambda i: (0, 0)),
                pl.BlockSpec(w_in_bf.shape, lambda i: (0, 0))]
    args = [x2d, mod3, norm_g, w_in_bf]
    if rope is not None:
        in_specs += [pl.BlockSpec((tm, RET_HEAD_DIM), lambda i: (i % tiles_per_seq, 0))] * 2
        args += list(rope)
    widths = [1024, RET_WIDTH, RET_WIDTH, RET_WIDTH, RET_WIDTH, 1024, 1024]
    return pl.pallas_call(
        functools.partial(_inproj_kernel, use_rope=rope is not None),
        grid=(t // tm,),
        in_specs=in_specs,
        out_specs=[pl.BlockSpec((tm, w), lambda i: (i, 0)) for w in widths],
        out_shape=[jax.ShapeDtypeStruct((t, w), BF16) for w in widths],
        compiler_params=pltpu.CompilerParams(dimension_semantics=("parallel",),
                                             vmem_limit_bytes=VMEM_LIMIT),
        name="inproj",
    )(*args)


def _retention_kernel(q_ref, k_ref, v_ref, sg_ref, dec_ref, gn_ref, s0f_ref, s0b_ref,
                      r_ref, sfo_ref, sbo_ref):
    n_chunks = q_ref.shape[0] // CHUNK
    dec = dec_ref[...]
    lg = jnp.minimum(dec, 0.0) - jnp.log1p(jnp.exp(-jnp.abs(dec)))
    lgf = lg[0:1, :]
    lgb = lg[1:2, :]
    row = lax.broadcasted_iota(jnp.int32, (CHUNK, CHUNK), 0).astype(F32)
    col = lax.broadcasted_iota(jnp.int32, (CHUNK, CHUNK), 1).astype(F32)
    diff = row - col
    decay = jnp.exp(jnp.where(diff >= 0, lgf * diff, lgb * (-diff)))
    qw_f = jnp.exp(lgf * (row + 1.0))
    qw_b = jnp.exp(lgb * (CHUNK - row))
    kw_f = jnp.exp(lgf * (CHUNK - 1.0 - row))
    kw_b = jnp.exp(lgb * row)
    gc_f = jnp.exp(lgf * CHUNK)
    gc_b = jnp.exp(lgb * CHUNK)

    def rows(n):
        return slice(n * CHUNK, (n + 1) * CHUNK)

    kv_f, kv_b = [], []
    for n in range(n_chunks):
        kn = k_ref[rows(n), :].astype(F32)
        vn = v_ref[rows(n), :]
        kv_f.append(_dot((kn * kw_f).T.astype(BF16), vn))
        kv_b.append(_dot((kn * kw_b).T.astype(BF16), vn))

    s = s0f_ref[...]
    prev_f = []
    for n in range(n_chunks):
        prev_f.append(s.astype(BF16))
        s = gc_f * s + kv_f[n]
    sfo_ref[...] = s
    s = s0b_ref[...]
    prev_b = [None] * n_chunks
    for n in reversed(range(n_chunks)):
        prev_b[n] = s.astype(BF16)
        s = gc_b * s + kv_b[n]
    sbo_ref[...] = s

    gn = gn_ref[...]
    for n in range(n_chunks):
        qn = q_ref[rows(n), :]
        qf = qn.astype(F32)
        scores = lax.dot_general(qn, k_ref[rows(n), :], (((1,), (1,)), ((), ())),
                                 preferred_element_type=F32)
        o = _dot((scores * decay).astype(BF16), v_ref[rows(n), :])
        o = o + _dot((qf * qw_f).astype(BF16), prev_f[n])
        o = o + _dot((qf * qw_b).astype(BF16), prev_b[n])
        mu = jnp.mean(o, axis=-1, keepdims=True)
        d = o - mu
        var = jnp.mean(d * d, axis=-1, keepdims=True)
        on = d * lax.rsqrt(var + EPS) * gn
        r_ref[rows(n), :] = (on * sg_ref[rows(n), :].astype(F32)).astype(BF16)


def _retention(q, k, v, sg, dec, gn_g, s0f, s0b, batch, seq_len):
    hd = RET_HEAD_DIM
    tok_spec = pl.BlockSpec((seq_len, hd), lambda b, h: (b, h))
    st_spec = pl.BlockSpec((None, None, hd, hd), lambda b, h: (b, h, 0, 0))
    st_shape = jax.ShapeDtypeStruct((batch, N_RET_HEADS, hd, hd), F32)
    return pl.pallas_call(
        _retention_kernel,
        grid=(batch, N_RET_HEADS),
        in_specs=[tok_spec, tok_spec, tok_spec, tok_spec,
                  pl.BlockSpec((None, 2, hd), lambda b, h: (h, 0, 0)),
                  pl.BlockSpec((1, hd), lambda b, h: (0, h)),
                  st_spec, st_spec],
        out_specs=[tok_spec, st_spec, st_spec],
        out_shape=[jax.ShapeDtypeStruct((batch * seq_len, RET_WIDTH), BF16), st_shape, st_shape],
        compiler_params=pltpu.CompilerParams(dimension_semantics=("parallel", "parallel"),
                                             vmem_limit_bytes=VMEM_LIMIT),
        name="retention",
    )(q, k, v, sg, dec, gn_g, s0f, s0b)


def _fnet_kernel(uf_ref, cs_ref, cls_ref, o_ref, xcs_ref):
    seq_len = uf_ref.shape[0]
    gd = FOURIER_GROUP_DIM

    @pl.when(pl.program_id(1) == 0)
    def _():
        for g in range(N_FOURIER_GROUPS):
            x = _dot(uf_ref[:, g * gd:(g + 1) * gd], cs_ref[...])
            xcs_ref[0:seq_len, g * gd:(g + 1) * gd] = x[:, :gd].astype(BF16)
            xcs_ref[seq_len:2 * seq_len, g * gd:(g + 1) * gd] = x[:, gd:].astype(BF16)

    o_ref[...] = _dot(cls_ref[...], xcs_ref[...]).astype(BF16)


def _fnet(uf, cs, cls, batch, seq_len):
    rb = FNET_ROWS
    nr = seq_len // rb
    return pl.pallas_call(
        _fnet_kernel,
        grid=(batch, nr),
        in_specs=[pl.BlockSpec((seq_len, D_MODEL), lambda b, r: (b, 0)),
                  pl.BlockSpec(cs.shape, lambda b, r: (0, 0)),
                  pl.BlockSpec((rb, 2 * seq_len), lambda b, r: (r, 0))],
        out_specs=pl.BlockSpec((rb, D_MODEL), lambda b, r: (b * nr + r, 0)),
        out_shape=jax.ShapeDtypeStruct((batch * seq_len, D_MODEL), BF16),
        scratch_shapes=[pltpu.VMEM((2 * seq_len, D_MODEL), BF16)],
        compiler_params=pltpu.CompilerParams(dimension_semantics=("parallel", "arbitrary"),
                                             vmem_limit_bytes=VMEM_LIMIT),
        name="fnet",
    )(uf, cs, cls)


def _merge_kernel(fm_ref, r_ref, gf_ref, gr_ref, x_ref, mod_ref, wf_ref, wr_ref, wo_ref, o_ref):
    f_out = _dot(fm_ref[...], wf_ref[...])
    r_out = _dot(r_ref[...], wr_ref[...])
    merged = gf_ref[...].astype(F32) * f_out + gr_ref[...].astype(F32) * r_out
    mix = _dot(merged.astype(BF16), wo_ref[...])
    o_ref[...] = x_ref[...] + mod_ref[0, 2:3, :] * mix


def _merge(fmix, r, gf, gr, x2d, mod3, w_four, w_ret, w_o, seq_len, mod_row_of_batch):
    t = x2d.shape[0]
    tm = TM_PROJ

    def mod_idx(i):
        return (mod_row_of_batch((i * tm) // seq_len), 0, 0)

    def tok(w):
        return pl.BlockSpec((tm, w), lambda i: (i, 0))

    def full(a):
        return pl.BlockSpec(a.shape, lambda i: (0, 0))

    return pl.pallas_call(
        _merge_kernel,
        grid=(t // tm,),
        in_specs=[tok(D_MODEL), tok(RET_WIDTH), tok(D_MODEL), tok(D_MODEL), tok(D_MODEL),
                  pl.BlockSpec((1, 6, D_MODEL), mod_idx), full(w_four), full(w_ret), full(w_o)],
        out_specs=tok(D_MODEL),
        out_shape=jax.ShapeDtypeStruct((t, D_MODEL), F32),
        compiler_params=pltpu.CompilerParams(dimension_semantics=("parallel",),
                                             vmem_limit_bytes=VMEM_LIMIT),
        name="merge",
    )(fmix, r, gf, gr, x2d, mod3, w_four, w_ret, w_o)


def _pack_pair(lo_f32, hi_f32):
    lo = lax.bitcast_convert_type(lo_f32.astype(BF16).astype(F32), jnp.uint32)
    hi = lax.bitcast_convert_type(hi_f32.astype(BF16).astype(F32), jnp.uint32)
    return lax.bitcast_convert_type((lo >> 16) | hi, jnp.int32)


def _unpack_pair(words_i32):
    w = lax.bitcast_convert_type(words_i32, jnp.uint32)
    lo = lax.bitcast_convert_type(w << 16, F32)
    hi = lax.bitcast_convert_type(w & jnp.uint32(0xFFFF0000), F32)
    return lo, hi


def _load_token_words(ref, lead, n_tok):
    parts = []
    for s in range(ROW_SLABS):
        idx = (pl.ds(s, n_tok, stride=ROW_SLABS), slice(None))
        parts.append(ref[lead + idx] if lead else ref[idx])
    return jnp.concatenate(parts, axis=1)


def _store_token_words(ref, words, n_tok):
    for s in range(ROW_SLABS):
        ref[pl.ds(s, n_tok, stride=ROW_SLABS), :] = words[:, s * 128:(s + 1) * 128]


def _route(scores, biased):
    tokens = scores.shape[1]
    neg = -jnp.inf
    epg = EXPERTS_PER_GROUP
    iota_g = lax.broadcasted_iota(jnp.int32, (epg, tokens), 0).astype(F32)

    def pick_first_max(cur, iota, size):
        m = jnp.max(cur, axis=0, keepdims=True)
        idx = jnp.min(jnp.where(cur == m, iota, float(size)), axis=0, keepdims=True)
        return m, idx, iota == idx

    group_scores = []
    for g in range(N_EXPERT_GROUPS):
        vals = biased[g * epg:(g + 1) * epg, :]
        m1, _, hit = pick_first_max(vals, iota_g, epg)
        m2 = jnp.max(jnp.where(hit, neg, vals), axis=0, keepdims=True)
        group_scores.append(m1 + m2)
    cur = jnp.concatenate(group_scores, axis=0)
    group_sel = jnp.zeros_like(cur)
    for _ in range(TOPK_GROUPS):
        _, _, hit = pick_first_max(cur, iota_g, N_EXPERT_GROUPS)
        group_sel = jnp.where(hit, 1.0, group_sel)
        cur = jnp.where(hit, neg, cur)
    masked = jnp.concatenate(
        [jnp.where(group_sel[g:g + 1, :] > 0.0, biased[g * epg:(g + 1) * epg, :], neg)
         for g in range(N_EXPERT_GROUPS)], axis=0)
    iota_e = lax.broadcasted_iota(jnp.int32, masked.shape, 0).astype(F32)
    sel = jnp.zeros_like(masked)
    cur = masked
    picks = []
    for _ in range(TOP_K):
        _, idx, hit = pick_first_max(cur, iota_e, N_EXPERTS)
        picks.append(idx)
        sel = jnp.where(hit, 1.0, sel)
        cur = jnp.where(hit, neg, cur)
    w = scores * sel
    return w / jnp.sum(w, axis=0, keepdims=True) * ROUTED_SCALE, sel, picks


def _router_kernel(x_ref, mod_ref, g2_ref, wrt_ref, rb_ref, hp_ref, ek_ref, rk_ref, wt_ref, cnt_ref, run_scr):
    tm = x_ref.shape[0]

    @pl.when(pl.program_id(0) == 0)
    def _():
        run_scr[...] = jnp.zeros_like(run_scr)

    h = _rms_mod(x_ref[...], g2_ref[...], mod_ref[0, 3:4, :], mod_ref[0, 4:5, :])
    half = D_MODEL // 2
    _store_token_words(hp_ref, _pack_pair(h[:, :half], h[:, half:]), tm)

    logits_t = lax.dot_general(wrt_ref[...], h, (((1,), (1,)), ((), ())),
                               precision=lax.Precision.HIGHEST, preferred_element_type=F32)
    scores = jax.nn.sigmoid(logits_t)
    comb_t, sel, picks = _route(scores, scores + rb_ref[...])

    earlier = (lax.broadcasted_iota(jnp.int32, (tm, tm), 0) < lax.broadcasted_iota(jnp.int32, (tm, tm), 1))
    rank_t = _dot(sel.astype(BF16), jnp.where(earlier, 1.0, 0.0).astype(BF16)) + run_scr[...]
    run_scr[...] += jnp.sum(sel, axis=1, keepdims=True)
    cnt_ref[...] = jnp.broadcast_to(run_scr[...], cnt_ref.shape)

    iota_e = lax.broadcasted_iota(jnp.int32, sel.shape, 0).astype(F32)
    ranks, weights = [], []
    for idx in picks:
        hit = iota_e == idx
        ranks.append(jnp.sum(jnp.where(hit, rank_t, 0.0), axis=0, keepdims=True))
        weights.append(jnp.sum(jnp.where(hit, comb_t, 0.0), axis=0, keepdims=True))
    ek_ref[...] = jnp.concatenate(picks, axis=0).astype(jnp.int32)
    rk_ref[...] = jnp.concatenate(ranks, axis=0).astype(jnp.int32)
    w_pad = jnp.concatenate(weights + [jnp.zeros((128 - TOP_K, tm), F32)], axis=0)
    wt_ref[...] = w_pad.T


def _router(x1, mod3, norm2_g, w_router_t, router_bias, seq_len, mod_row_of_batch):
    t = x1.shape[0]
    tm = TM_ROUTER

    def mod_idx(i):
        return (mod_row_of_batch((i * tm) // seq_len), 0, 0)

    def full(a):
        return pl.BlockSpec(a.shape, lambda i: (0,) * a.ndim)

    return pl.pallas_call(
        _router_kernel,
        grid=(t // tm,),
        in_specs=[pl.BlockSpec((tm, D_MODEL), lambda i: (i, 0)),
                  pl.BlockSpec((1, 6, D_MODEL), mod_idx),
                  full(norm2_g), full(w_router_t), full(router_bias)],
        out_specs=[pl.BlockSpec((tm * ROW_SLABS, 128), lambda i: (i, 0)),
                   pl.BlockSpec((TOP_K, tm), lambda i: (0, i)),
                   pl.BlockSpec((TOP_K, tm), lambda i: (0, i)),
                   pl.BlockSpec((tm, 128), lambda i: (i, 0)),
                   pl.BlockSpec((N_EXPERTS, 128), lambda i: (0, 0))],
        out_shape=[jax.ShapeDtypeStruct((t * ROW_SLABS, 128), jnp.int32),
                   jax.ShapeDtypeStruct((TOP_K, t), jnp.int32),
                   jax.ShapeDtypeStruct((TOP_K, t), jnp.int32),
                   jax.ShapeDtypeStruct((t, 128), F32),
                   jax.ShapeDtypeStruct((N_EXPERTS, 128), F32)],
        scratch_shapes=[pltpu.VMEM((N_EXPERTS, 1), F32)],
        compiler_params=pltpu.CompilerParams(dimension_semantics=("arbitrary",),
                                             vmem_limit_bytes=VMEM_LIMIT),
        name="router",
    )(x1, mod3, norm2_g, w_router_t, router_bias)


def _plan_kernel(ek_ref, rk_ref, cnt_ref, pos_ref, texp_ref, nused_ref, tend_ref):
    rows = float(EXPERT_ROWS)
    cnt = cnt_ref[:, 0:1]
    tiles = jnp.floor((cnt + (rows - 1.0)) / rows)
    before = (lax.broadcasted_iota(jnp.int32, (N_EXPERTS, N_EXPERTS), 1)
              < lax.broadcasted_iota(jnp.int32, (N_EXPERTS, N_EXPERTS), 0))
    tile_start = jnp.dot(jnp.where(before, 1.0, 0.0), jnp.broadcast_to(tiles, (N_EXPERTS, 128)),
                         precision=lax.Precision.HIGHEST, preferred_element_type=F32)[:, 0:1]
    tile_end = tile_start + tiles
    row_start = tile_start * rows

    ek = ek_ref[...]
    pos = rk_ref[...].astype(F32)
    tile_id = lax.broadcasted_iota(jnp.int32, texp_ref.shape, 1).astype(F32)
    texp = jnp.zeros(texp_ref.shape, F32)
    for e in range(N_EXPERTS):
        pos = pos + jnp.where(ek == e, row_start[e:e + 1, :], 0.0)
        texp = texp + jnp.where(tile_id >= tile_end[e:e + 1, :], 1.0, 0.0)
    pos_ref[...] = pos.astype(jnp.int32)
    texp_ref[...] = jnp.minimum(texp, N_EXPERTS - 1.0).astype(jnp.int32)
    nused_ref[...] = jnp.broadcast_to(tile_end[N_EXPERTS - 1:N_EXPERTS, :], nused_ref.shape).astype(jnp.int32)
    tend_ref[...] = jnp.broadcast_to(tile_end, tend_ref.shape).astype(jnp.int32)


def _plan(ek, rk, cnt, n_tiles_pad):
    t = ek.shape[1]

    def full(shape):
        return pl.BlockSpec(shape, lambda: (0,) * len(shape))

    return pl.pallas_call(
        _plan_kernel,
        in_specs=[full(ek.shape), full(rk.shape), full(cnt.shape)],
        out_specs=[full((TOP_K, t)), full((1, n_tiles_pad)), full((1, 128)), full((N_EXPERTS, 128))],
        out_shape=[jax.ShapeDtypeStruct((TOP_K, t), jnp.int32),
                   jax.ShapeDtypeStruct((1, n_tiles_pad), jnp.int32),
                   jax.ShapeDtypeStruct((1, 128), jnp.int32),
                   jax.ShapeDtypeStruct((N_EXPERTS, 128), jnp.int32)],
        compiler_params=pltpu.CompilerParams(vmem_limit_bytes=VMEM_LIMIT),
        name="plan",
    )(ek, rk, cnt)


def _sc_mesh():
    return plsc.VectorSubcoreMesh(core_axis_name="c", subcore_axis_name="s")


def _sc_dispatch(rows, pos3, n_out):
    t = rows.shape[0]
    ch = SC_CHUNK
    per_w = (t // ch) // SC_WORKERS

    @functools.partial(
        pl.kernel, out_type=jax.ShapeDtypeStruct((n_out,) + rows.shape[1:], jnp.int32), mesh=_sc_mesh(),
        scratch_types=[pltpu.VMEM((TOP_K, ch), jnp.int32), pltpu.VMEM((ch,) + rows.shape[1:], jnp.int32),
                       pltpu.SemaphoreType.DMA])
    def k(rows_hbm, pos_hbm, out_hbm, idx_v, rows_v, sem):
        wid = lax.axis_index("s") * SC_CORES + lax.axis_index("c")

        @pl.loop(0, per_w)
        def _(j):
            c = wid * per_w + j
            pltpu.sync_copy(pos_hbm.at[c], idx_v)
            pltpu.sync_copy(rows_hbm.at[pl.ds(c * ch, ch)], rows_v)
            copies = [pltpu.async_copy(rows_v, out_hbm.at[idx_v.at[kk]], sem) for kk in range(TOP_K)]
            for cp in copies:
                cp.wait()

    return k(rows, pos3)


def _sc_gather(table, pos3, t):
    ch = SC_CHUNK
    per_w = (t // ch) // SC_WORKERS

    @functools.partial(
        pl.kernel, out_type=jax.ShapeDtypeStruct((TOP_K, t) + table.shape[1:], jnp.int32), mesh=_sc_mesh(),
        scratch_types=[pltpu.VMEM((TOP_K, ch), jnp.int32), pltpu.VMEM((ch,) + table.shape[1:], jnp.int32),
                       pltpu.SemaphoreType.DMA])
    def k(tab_hbm, pos_hbm, out_hbm, idx_v, rows_v, sem):
        wid = lax.axis_index("s") * SC_CORES + lax.axis_index("c")

        @pl.loop(0, per_w)
        def _(j):
            c = wid * per_w + j
            pltpu.sync_copy(pos_hbm.at[c], idx_v)
            for kk in range(TOP_K):
                pltpu.async_copy(tab_hbm.at[idx_v.at[kk]], rows_v, sem).wait()
                pltpu.sync_copy(rows_v, out_hbm.at[kk, pl.ds(c * ch, ch)])

    return k(table, pos3)


def _experts_kernel(texp_ref, nused_ref, tend_ref, xs_ref, weg_hbm, weu_hbm, wed_hbm, ys_ref,
                    wg_scr, wu_scr, wd_scr, wg_buf, wu_buf, wd_buf, sem, group_scr):
    j = pl.program_id(0)
    rows = EXPERT_ROWS
    half = D_MODEL // 2
    n_used = nused_ref[0]
    expert = texp_ref[j]
    used = j < n_used
    new_expert = (j == 0) | (expert != texp_ref[jnp.maximum(j - 1, 0)])

    def weight_copies(e, slot):
        return [pltpu.make_async_copy(weg_hbm.at[e], wg_buf.at[slot], sem.at[slot, 0]),
                pltpu.make_async_copy(weu_hbm.at[e], wu_buf.at[slot], sem.at[slot, 1]),
                pltpu.make_async_copy(wed_hbm.at[e], wd_buf.at[slot], sem.at[slot, 2])]

    @pl.when(j == 0)
    def _():
        group_scr[0] = 0
        for cp in weight_copies(expert, 0):
            cp.start()

    @pl.when(used & new_expert)
    def _():
        group = group_scr[0]
        slot = group % 2
        next_tile = tend_ref[expert]

        @pl.when(next_tile < n_used)
        def _():
            for cp in weight_copies(texp_ref[next_tile], 1 - slot):
                cp.start(priority=1)

        for cp in weight_copies(expert, slot):
            cp.wait()
        wg_scr[...] = wg_buf[slot].astype(BF16)
        wu_scr[...] = wu_buf[slot].astype(BF16)
        wd_scr[...] = wd_buf[slot].astype(BF16)
        group_scr[0] = group + 1

    @pl.when(used)
    def _():
        lo, hi = _unpack_pair(_load_token_words(xs_ref, (), rows))
        lo = lo.astype(BF16)
        hi = hi.astype(BF16)
        g = _dot(lo, wg_scr[0:half, :]) + _dot(hi, wg_scr[half:D_MODEL, :])
        u = _dot(lo, wu_scr[0:half, :]) + _dot(hi, wu_scr[half:D_MODEL, :])
        y = _dot((_silu(g) * u).astype(BF16), wd_scr[...])
        _store_token_words(ys_ref, _pack_pair(y[:, :half], y[:, half:]), rows)


def _experts(texp, nused, tend, xs2d, weg, weu, wed, n_tiles):
    rows = EXPERT_ROWS
    hbm = pl.BlockSpec(memory_space=pl.ANY)
    grid_spec = pltpu.PrefetchScalarGridSpec(
        num_scalar_prefetch=3,
        grid=(n_tiles,),
        in_specs=[pl.BlockSpec((rows * ROW_SLABS, 128), lambda j, te, nu, tn: (jnp.minimum(j, nu[0] - 1), 0)),
                  hbm, hbm, hbm],
        out_specs=pl.BlockSpec((rows * ROW_SLABS, 128), lambda j, te, nu, tn: (jnp.minimum(j, nu[0] - 1), 0)),
        scratch_shapes=[pltpu.VMEM((D_MODEL, EXPERT_DIM), BF16),
                        pltpu.VMEM((D_MODEL, EXPERT_DIM), BF16),
                        pltpu.VMEM((EXPERT_DIM, D_MODEL), BF16),
                        pltpu.VMEM((2, D_MODEL, EXPERT_DIM), F32),
                        pltpu.VMEM((2, D_MODEL, EXPERT_DIM), F32),
                        pltpu.VMEM((2, EXPERT_DIM, D_MODEL), F32),
                        pltpu.SemaphoreType.DMA((2, 3)),
                        pltpu.SMEM((1,), jnp.int32)],
    )
    return pl.pallas_call(
        _experts_kernel,
        grid_spec=grid_spec,
        out_shape=jax.ShapeDtypeStruct(xs2d.shape, jnp.int32),
        compiler_params=pltpu.CompilerParams(dimension_semantics=("arbitrary",),
                                             vmem_limit_bytes=VMEM_LIMIT),
        name="experts",
    )(texp, nused, tend, xs2d, weg, weu, wed)


def _final_kernel(x_ref, hp_ref, y8_ref, wt_ref, mod_ref, wsg_ref, wsu_ref, wsd_ref, fng_ref, o_ref):
    tm = x_ref.shape[0]
    lo, hi = _unpack_pair(_load_token_words(hp_ref, (), tm))
    hb = jnp.concatenate([lo, hi], axis=1).astype(BF16)
    shared = _dot((_silu(_dot(hb, wsg_ref[...])) * _dot(hb, wsu_ref[...])).astype(BF16), wsd_ref[...])
    wt = wt_ref[...]
    r_lo = jnp.zeros((tm, D_MODEL // 2), F32)
    r_hi = jnp.zeros((tm, D_MODEL // 2), F32)
    for k in range(TOP_K):
        lo, hi = _unpack_pair(_load_token_words(y8_ref, (k,), tm))
        wk = wt[:, k:k + 1]
        r_lo = r_lo + wk * lo
        r_hi = r_hi + wk * hi
    routed = jnp.concatenate([r_lo, r_hi], axis=1)
    y = x_ref[...] + mod_ref[0, 5:6, :] * (routed + shared)
    ms = jnp.mean(y * y, axis=-1, keepdims=True)
    o_ref[...] = y * lax.rsqrt(ms + EPS) * fng_ref[...]


def _final(x1, hp2d, y8, wtok, mod3, wsg, wsu, wsd, final_g, seq_len, mod_row_of_batch):
    t = x1.shape[0]
    tm = TM_FINAL

    def mod_idx(i):
        return (mod_row_of_batch((i * tm) // seq_len), 0, 0)

    def full(a):
        return pl.BlockSpec(a.shape, lambda i: (0,) * a.ndim)

    return pl.pallas_call(
        _final_kernel,
        grid=(t // tm,),
        in_specs=[pl.BlockSpec((tm, D_MODEL), lambda i: (i, 0)),
                  pl.BlockSpec((tm * ROW_SLABS, 128), lambda i: (i, 0)),
                  pl.BlockSpec((TOP_K, tm * ROW_SLABS, 128), lambda i: (0, i, 0)),
                  pl.BlockSpec((tm, 128), lambda i: (i, 0)),
                  pl.BlockSpec((1, 6, D_MODEL), mod_idx),
                  full(wsg), full(wsu), full(wsd), full(final_g)],
        out_specs=pl.BlockSpec((tm, D_MODEL), lambda i: (i, 0)),
        out_shape=jax.ShapeDtypeStruct((t, D_MODEL), F32),
        compiler_params=pltpu.CompilerParams(dimension_semantics=("parallel",),
                                             vmem_limit_bytes=VMEM_LIMIT),
        name="final",
    )(x1, hp2d, y8, wtok, mod3, wsg, wsu, wsd, final_g)


def _moe(x1, mod3, lw, seq_len, mod_row_of_batch):
    t = x1.shape[0]
    n_tiles = TOP_K * t // EXPERT_ROWS + N_EXPERTS
    n_tiles_pad = -(-n_tiles // 128) * 128
    hp2d, ek, rk, wtok, cnt = _router(x1, mod3, lw["norm2_g"], lw["w_router_t"], lw["router_bias"],
                                      seq_len, mod_row_of_batch)
    pos, texp, nused, tend = _plan(ek, rk, cnt, n_tiles_pad)
    pos3 = pos.reshape(TOP_K, t // SC_CHUNK, SC_CHUNK).transpose(1, 0, 2)
    xs = _sc_dispatch(hp2d.reshape(t, ROW_SLABS, 128), pos3, n_tiles * EXPERT_ROWS)
    ys2d = _experts(texp.reshape(-1), nused.reshape(-1), tend[:, 0], xs.reshape(-1, 128),
                    lw["weg"], lw["weu"], lw["wed"], n_tiles)
    y8 = _sc_gather(ys2d.reshape(-1, ROW_SLABS, 128), pos3, t)
    return _final(x1, hp2d, y8.reshape(TOP_K, t * ROW_SLABS, 128), wtok, mod3,
                  lw["wsg"], lw["wsu"], lw["wsd"], lw["final_g"], seq_len, mod_row_of_batch)


def _dft_tables(seq_len):
    gd = FOURIER_GROUP_DIM
    kc = jnp.arange(gd, dtype=jnp.int32)
    ang_c = ((kc[:, None] * kc[None, :]) % gd).astype(F32) * (2.0 * math.pi / gd)
    cs = jnp.concatenate([jnp.cos(ang_c), jnp.sin(ang_c)], axis=1) * (gd ** -0.5)
    kl = jnp.arange(seq_len, dtype=jnp.int32)
    ang_l = ((kl[:, None] * kl[None, :]) % seq_len).astype(F32) * (2.0 * math.pi / seq_len)
    cls = jnp.concatenate([jnp.cos(ang_l), -jnp.sin(ang_l)], axis=1) * (seq_len ** -0.5)
    return cs.astype(BF16), cls.astype(BF16)


def _rope_tables(length):
    rows = length // GRID_W
    r = jnp.repeat(jnp.arange(rows, dtype=F32), GRID_W)
    col = jnp.tile(jnp.arange(GRID_W, dtype=F32), rows)
    nf = RET_HEAD_DIM // 4
    inv = ROPE_BASE ** (-jnp.arange(nf, dtype=F32) / nf)
    ar = r[:, None] * inv[None]
    ac = col[:, None] * inv[None]
    ang = jnp.concatenate([ar, ar, ac, ac], axis=-1)
    sign = jnp.where((jnp.arange(RET_HEAD_DIM) & nf) == 0, -1.0, 1.0).astype(F32)
    return jnp.cos(ang), jnp.sin(ang) * sign[None, :]


def _trunk_path(x, mod3, mod_row_of_batch, s0f, s0b, rope, lw):
    batch, seq_len, _ = x.shape
    x2d = x.reshape(batch * seq_len, D_MODEL)
    uf, q, k, v, sg, gf, gr = _inproj(x2d, mod3, lw["norm1_g"], lw["w_in"], seq_len, mod_row_of_batch, rope)
    r, s_f, s_b = _retention(q, k, v, sg, lw["dec"], lw["gn_g"], s0f, s0b, batch, seq_len)
    cs, cls = _dft_tables(seq_len)
    fmix = _fnet(uf, cs, cls, batch, seq_len)
    x1 = _merge(fmix, r, gf, gr, x2d, mod3, lw["w_four"], lw["w_ret"], lw["w_o"], seq_len, mod_row_of_batch)
    y = _moe(x1, mod3, lw, seq_len, mod_row_of_batch)
    return y.reshape(batch, seq_len, D_MODEL), s_f, s_b


def kernel(x_prompt, x_sample, state_ret_fwd, state_ret_bwd, c, c_ctx, w_ada, b_ada, norm1_g, norm2_g, w_in,
           ret_decay_fwd, ret_decay_bwd, ret_gn_g, w_four_out, w_ret_out, w_out, w_router, router_bias,
           w_exp_gate, w_exp_up, w_exp_down, w_shared_gate, w_shared_up, w_shared_down, final_norm_g):
    depth = w_ada.shape[0]
    assert depth == 1, "final norm is fused into the last layer's MoE kernel"
    n_ctx, n_lat = x_prompt.shape[0], x_sample.shape[0]
    cond = jnp.concatenate([c_ctx[None, :], c], axis=0)
    cond = jnp.pad(cond, ((0, (-cond.shape[0]) % 8), (0, 0)))
    rope = _rope_tables(x_sample.shape[1])
    zeros = jnp.zeros((n_ctx, N_RET_HEADS, RET_HEAD_DIM, RET_HEAD_DIM), F32)

    layer = 0
    mod = _ada(cond, w_ada[layer], b_ada[layer][None, :])
    mod3 = mod.reshape(mod.shape[0], 6, D_MODEL)
    dec = jnp.stack([ret_decay_fwd[layer], ret_decay_bwd[layer]], axis=1)
    lw = {
        "norm1_g": norm1_g[layer][None, :],
        "norm2_g": norm2_g[layer][None, :],
        "w_in": w_in[layer].astype(BF16),
        "dec": jnp.broadcast_to(dec[:, :, None], (N_RET_HEADS, 2, RET_HEAD_DIM)).astype(F32),
        "gn_g": ret_gn_g[layer][None, :],
        "w_four": w_four_out[layer].astype(BF16),
        "w_ret": w_ret_out[layer].astype(BF16),
        "w_o": w_out[layer].astype(BF16),
        "w_router_t": w_router[layer].T,
        "router_bias": router_bias[layer][:, None],
        "weg": w_exp_gate[layer],
        "weu": w_exp_up[layer],
        "wed": w_exp_down[layer],
        "wsg": w_shared_gate[layer].astype(BF16),
        "wsu": w_shared_up[layer].astype(BF16),
        "wsd": w_shared_down[layer].astype(BF16),
        "final_g": final_norm_g[None, :],
    }
    y_prompt, s_f, s_b = _trunk_path(x_prompt, mod3, lambda b: 0, zeros, zeros, None, lw)
    y_sample, _, _ = _trunk_path(x_sample, mod3, lambda b: 1 + b, state_ret_fwd[:, layer],
                                 state_ret_bwd[:, layer], rope, lw)
    return (y_prompt, y_sample, s_f[:, None], s_b[:, None])
```

```python
import functools
import math

import jax
import jax.numpy as jnp
import numpy as np
from jax import lax
from jax.experimental import pallas as pl
from jax.experimental.pallas import tpu as pltpu
from jax.experimental.pallas import tpu_sc as plsc

F32 = jnp.float32
BF16 = jnp.bfloat16

D_MODEL = 1024
GRID_W = 64
N_FOURIER_GROUPS = 8
FOURIER_GROUP_DIM = 128
N_RET_HEADS = 4
RET_HEAD_DIM = 128
RET_WIDTH = N_RET_HEADS * RET_HEAD_DIM
CHUNK = 128
N_EXPERTS = 64
N_EXPERT_GROUPS = 8
EXPERTS_PER_GROUP = N_EXPERTS // N_EXPERT_GROUPS
TOPK_GROUPS = 4
TOP_K = 8
EXPERT_DIM = 256
ROUTED_SCALE = 2.5
ROPE_BASE = 10000.0
EPS = 1e-6
Q_SCALE = RET_HEAD_DIM ** -0.5

_C_UF = (0, 1024)
_C_Q = (1024, 1536)
_C_K = (1536, 2048)
_C_V = (2048, 2560)
_C_G = (2560, 3072)
_C_GF = (3072, 4096)
_C_GR = (4096, 5120)

VMEM_LIMIT = 56 * 1024 * 1024

TM_PROJ = 512
FNET_ROWS = 256
TM_ROUTER = 1024
TM_FINAL = 512
EXPERT_ROWS = 256
ROW_SLABS = 4
SC_CORES = 2
SC_WORKERS = 32
SC_CHUNK = 128


def _silu(x):
    return x * jax.nn.sigmoid(x)


def _dot(a, b):
    return jnp.dot(a, b, preferred_element_type=F32)


def _rms_mod(x, g, shift, scale):
    ms = jnp.mean(x * x, axis=-1, keepdims=True)
    y = x * lax.rsqrt(ms + EPS) * g
    return y * (1.0 + scale) + shift


def _ada_kernel(cond_ref, w_ref, b_ref, o_ref):
    s = _silu(cond_ref[...]).astype(BF16)
    o_ref[...] = _dot(s, w_ref[...].astype(BF16)) + b_ref[...]


def _ada(cond, w_ada, b_ada):
    rows, n = cond.shape[0], w_ada.shape[1]
    tn = 1536
    return pl.pallas_call(
        _ada_kernel,
        grid=(n // tn,),
        in_specs=[pl.BlockSpec((rows, D_MODEL), lambda j: (0, 0)),
                  pl.BlockSpec((D_MODEL, tn), lambda j: (0, j)),
                  pl.BlockSpec((1, tn), lambda j: (0, j))],
        out_specs=pl.BlockSpec((rows, tn), lambda j: (0, j)),
        out_shape=jax.ShapeDtypeStruct((rows, n), F32),
        compiler_params=pltpu.CompilerParams(vmem_limit_bytes=VMEM_LIMIT),
        name="ada",
    )(cond, w_ada, b_ada)


def _rope_head(x, cos, sin_signed, first_half):
    partner = jnp.where(first_half, pltpu.roll(x, 96, 1), pltpu.roll(x, 32, 1))
    return x * cos + partner * sin_signed


def _inproj_kernel(*refs, use_rope):
    if use_rope:
        x_ref, mod_ref, g_ref, w_ref, cos_ref, sin_ref = refs[:6]
        outs = refs[6:]
    else:
        x_ref, mod_ref, g_ref, w_ref = refs[:4]
        outs = refs[4:]
    uf_o, q_o, k_o, v_o, sg_o, gf_o, gr_o = outs

    h = _rms_mod(x_ref[...], g_ref[...], mod_ref[0, 0:1, :], mod_ref[0, 1:2, :])
    hb = h.astype(BF16)

    def proj(cols):
        return _dot(hb, w_ref[:, cols[0]:cols[1]])

    uf_o[...] = proj(_C_UF).astype(BF16)
    q = proj(_C_Q)
    k = proj(_C_K)
    if use_rope:
        cos = cos_ref[...]
        sin_signed = sin_ref[...]
        lane = lax.broadcasted_iota(jnp.int32, cos.shape, 1)
        first_half = (lane & 32) == 0
        for hd in range(N_RET_HEADS):
            sl = slice(hd * RET_HEAD_DIM, (hd + 1) * RET_HEAD_DIM)
            q_o[:, sl] = (_rope_head(q[:, sl], cos, sin_signed, first_half) * Q_SCALE).astype(BF16)
            k_o[:, sl] = _rope_head(k[:, sl], cos, sin_signed, first_half).astype(BF16)
    else:
        q_o[...] = (q * Q_SCALE).astype(BF16)
        k_o[...] = k.astype(BF16)
    v_o[...] = proj(_C_V).astype(BF16)
    sg_o[...] = _silu(proj(_C_G)).astype(BF16)
    gf_o[...] = jax.nn.sigmoid(proj(_C_GF)).astype(BF16)
    gr_o[...] = jax.nn.sigmoid(proj(_C_GR)).astype(BF16)


def _inproj(x2d, mod3, norm_g, w_in_bf, seq_len, mod_row_of_batch, rope):
    t = x2d.shape[0]
    tm = TM_PROJ
    tiles_per_seq = max(seq_len // tm, 1)

    def mod_idx(i):
        return (mod_row_of_batch((i * tm) // seq_len), 0, 0)

    in_specs = [pl.BlockSpec((tm, D_MODEL), lambda i: (i, 0)),
                pl.BlockSpec((1, 6, D_MODEL), mod_idx),
                pl.BlockSpec((1, D_MODEL), lambda i: (0, 0)),
                pl.BlockSpec(w_in_bf.shape, lambda i: (0, 0))]
    args = [x2d, mod3, norm_g, w_in_bf]
    if rope is not None:
        in_specs += [pl.BlockSpec((tm, RET_HEAD_DIM), lambda i: (i % tiles_per_seq, 0))] * 2
        args += list(rope)
    widths = [1024, RET_WIDTH, RET_WIDTH, RET_WIDTH, RET_WIDTH, 1024, 1024]
    return pl.pallas_call(
        functools.partial(_inproj_kernel, use_rope=rope is not None),
        grid=(t // tm,),
        in_specs=in_specs,
        out_specs=[pl.BlockSpec((tm, w), lambda i: (i, 0)) for w in widths],
        out_shape=[jax.ShapeDtypeStruct((t, w), BF16) for w in widths],
        compiler_params=pltpu.CompilerParams(dimension_semantics=("parallel",),
                                             vmem_limit_bytes=VMEM_LIMIT),
        name="inproj",
    )(*args)


def _retention_kernel(q_ref, k_ref, v_ref, sg_ref, dec_ref, gn_ref, s0f_ref, s0b_ref,
                      r_ref, sfo_ref, sbo_ref):
    n_chunks = q_ref.shape[0] // CHUNK
    dec = dec_ref[...]
    lg = jnp.minimum(dec, 0.0) - jnp.log1p(jnp.exp(-jnp.abs(dec)))
    lgf = lg[0:1, :]
    lgb = lg[1:2, :]
    row = lax.broadcasted_iota(jnp.int32, (CHUNK, CHUNK), 0).astype(F32)
    col = lax.broadcasted_iota(jnp.int32, (CHUNK, CHUNK), 1).astype(F32)
    diff = row - col
    decay = jnp.exp(jnp.where(diff >= 0, lgf * diff, lgb * (-diff)))
    qw_f = jnp.exp(lgf * (row + 1.0))
    qw_b = jnp.exp(lgb * (CHUNK - row))
    kw_f = jnp.exp(lgf * (CHUNK - 1.0 - row))
    kw_b = jnp.exp(lgb * row)
    gc_f = jnp.exp(lgf * CHUNK)
    gc_b = jnp.exp(lgb * CHUNK)

    def rows(n):
        return slice(n * CHUNK, (n + 1) * CHUNK)

    kv_f, kv_b = [], []
    for n in range(n_chunks):
        kn = k_ref[rows(n), :].astype(F32)
        vn = v_ref[rows(n), :]
        kv_f.append(_dot((kn * kw_f).T.astype(BF16), vn))
        kv_b.append(_dot((kn * kw_b).T.astype(BF16), vn))

    s = s0f_ref[...]
    prev_f = []
    for n in range(n_chunks):
        prev_f.append(s.astype(BF16))
        s = gc_f * s + kv_f[n]
    sfo_ref[...] = s
    s = s0b_ref[...]
    prev_b = [None] * n_chunks
    for n in reversed(range(n_chunks)):
        prev_b[n] = s.astype(BF16)
        s = gc_b * s + kv_b[n]
    sbo_ref[...] = s

    gn = gn_ref[...]
    for n in range(n_chunks):
        qn = q_ref[rows(n), :]
        qf = qn.astype(F32)
        scores = lax.dot_general(qn, k_ref[rows(n), :], (((1,), (1,)), ((), ())),
                                 preferred_element_type=F32)
        o = _dot((scores * decay).astype(BF16), v_ref[rows(n), :])
        o = o + _dot((qf * qw_f).astype(BF16), prev_f[n])
        o = o + _dot((qf * qw_b).astype(BF16), prev_b[n])
        mu = jnp.mean(o, axis=-1, keepdims=True)
        d = o - mu
        var = jnp.mean(d * d, axis=-1, keepdims=True)
        on = d * lax.rsqrt(var + EPS) * gn
        r_ref[rows(n), :] = (on * sg_ref[rows(n), :].astype(F32)).astype(BF16)


def _retention(q, k, v, sg, dec, gn_g, s0f, s0b, batch, seq_len):
    hd = RET_HEAD_DIM
    tok_spec = pl.BlockSpec((seq_len, hd), lambda b, h: (b, h))
    st_spec = pl.BlockSpec((None, None, hd, hd), lambda b, h: (b, h, 0, 0))
    st_shape = jax.ShapeDtypeStruct((batch, N_RET_HEADS, hd, hd), F32)
    return pl.pallas_call(
        _retention_kernel,
        grid=(batch, N_RET_HEADS),
        in_specs=[tok_spec, tok_spec, tok_spec, tok_spec,
                  pl.BlockSpec((None, 2, hd), lambda b, h: (h, 0, 0)),
                  pl.BlockSpec((1, hd), lambda b, h: (0, h)),
                  st_spec, st_spec],
        out_specs=[tok_spec, st_spec, st_spec],
        out_shape=[jax.ShapeDtypeStruct((batch * seq_len, RET_WIDTH), BF16), st_shape, st_shape],
        compiler_params=pltpu.CompilerParams(dimension_semantics=("parallel", "parallel"),
                                             vmem_limit_bytes=VMEM_LIMIT),
        name="retention",
    )(q, k, v, sg, dec, gn_g, s0f, s0b)


def _fnet_kernel(uf_ref, cs_ref, cls_ref, o_ref, xcs_ref):
    seq_len = uf_ref.shape[0]
    gd = FOURIER_GROUP_DIM

    @pl.when(pl.program_id(1) == 0)
    def _():
        for g in range(N_FOURIER_GROUPS):
            x = _dot(uf_ref[:, g * gd:(g + 1) * gd], cs_ref[...])
            xcs_ref[0:seq_len, g * gd:(g + 1) * gd] = x[:, :gd].astype(BF16)
            xcs_ref[seq_len:2 * seq_len, g * gd:(g + 1) * gd] = x[:, gd:].astype(BF16)

    o_ref[...] = _dot(cls_ref[...], xcs_ref[...]).astype(BF16)


def _fnet(uf, cs, cls, batch, seq_len):
    rb = FNET_ROWS
    nr = seq_len // rb
    return pl.pallas_call(
        _fnet_kernel,
        grid=(batch, nr),
        in_specs=[pl.BlockSpec((seq_len, D_MODEL), lambda b, r: (b, 0)),
                  pl.BlockSpec(cs.shape, lambda b, r: (0, 0)),
                  pl.BlockSpec((rb, 2 * seq_len), lambda b, r: (r, 0))],
        out_specs=pl.BlockSpec((rb, D_MODEL), lambda b, r: (b * nr + r, 0)),
        out_shape=jax.ShapeDtypeStruct((batch * seq_len, D_MODEL), BF16),
        scratch_shapes=[pltpu.VMEM((2 * seq_len, D_MODEL), BF16)],
        compiler_params=pltpu.CompilerParams(dimension_semantics=("parallel", "arbitrary"),
                                             vmem_limit_bytes=VMEM_LIMIT),
        name="fnet",
    )(uf, cs, cls)


def _merge_kernel(fm_ref, r_ref, gf_ref, gr_ref, x_ref, mod_ref, wf_ref, wr_ref, wo_ref, o_ref):
    f_out = _dot(fm_ref[...], wf_ref[...])
    r_out = _dot(r_ref[...], wr_ref[...])
    merged = gf_ref[...].astype(F32) * f_out + gr_ref[...].astype(F32) * r_out
    mix = _dot(merged.astype(BF16), wo_ref[...])
    o_ref[...] = x_ref[...] + mod_ref[0, 2:3, :] * mix


def _merge(fmix, r, gf, gr, x2d, mod3, w_four, w_ret, w_o, seq_len, mod_row_of_batch):
    t = x2d.shape[0]
    tm = TM_PROJ

    def mod_idx(i):
        return (mod_row_of_batch((i * tm) // seq_len), 0, 0)

    def tok(w):
        return pl.BlockSpec((tm, w), lambda i: (i, 0))

    def full(a):
        return pl.BlockSpec(a.shape, lambda i: (0, 0))

    return pl.pallas_call(
        _merge_kernel,
        grid=(t // tm,),
        in_specs=[tok(D_MODEL), tok(RET_WIDTH), tok(D_MODEL), tok(D_MODEL), tok(D_MODEL),
                  pl.BlockSpec((1, 6, D_MODEL), mod_idx), full(w_four), full(w_ret), full(w_o)],
        out_specs=tok(D_MODEL),
        out_shape=jax.ShapeDtypeStruct((t, D_MODEL), F32),
        compiler_params=pltpu.CompilerParams(dimension_semantics=("parallel",),
                                             vmem_limit_bytes=VMEM_LIMIT),
        name="merge",
    )(fmix, r, gf, gr, x2d, mod3, w_four, w_ret, w_o)


def _pack_pair(lo_f32, hi_f32):
    lo = lax.bitcast_convert_type(lo_f32.astype(BF16).astype(F32), jnp.uint32)
    hi = lax.bitcast_convert_type(hi_f32.astype(BF16).astype(F32), jnp.uint32)
    return lax.bitcast_convert_type((lo >> 16) | hi, jnp.int32)


def _unpack_pair(words_i32):
    w = lax.bitcast_convert_type(words_i32, jnp.uint32)
    lo = lax.bitcast_convert_type(w << 16, F32)
    hi = lax.bitcast_convert_type(w & jnp.uint32(0xFFFF0000), F32)
    return lo, hi


def _load_token_words(ref, lead, n_tok):
    parts = []
    for s in range(ROW_SLABS):
        idx = (pl.ds(s, n_tok, stride=ROW_SLABS), slice(None))
        parts.append(ref[lead + idx] if lead else ref[idx])
    return jnp.concatenate(parts, axis=1)


def _store_token_words(ref, words, n_tok):
    for s in range(ROW_SLABS):
        ref[pl.ds(s, n_tok, stride=ROW_SLABS), :] = words[:, s * 128:(s + 1) * 128]


def _route(scores, biased):
    tokens = scores.shape[1]
    neg = -jnp.inf
    epg = EXPERTS_PER_GROUP
    iota_g = lax.broadcasted_iota(jnp.int32, (epg, tokens), 0).astype(F32)

    def pick_first_max(cur, iota, size):
        m = jnp.max(cur, axis=0, keepdims=True)
        idx = jnp.min(jnp.where(cur == m, iota, float(size)), axis=0, keepdims=True)
        return m, idx, iota == idx

    group_scores = []
    for g in range(N_EXPERT_GROUPS):
        vals = biased[g * epg:(g + 1) * epg, :]
        m1, _, hit = pick_first_max(vals, iota_g, epg)
        m2 = jnp.max(jnp.where(hit, neg, vals), axis=0, keepdims=True)
        group_scores.append(m1 + m2)
    cur = jnp.concatenate(group_scores, axis=0)
    group_sel = jnp.zeros_like(cur)
    for _ in range(TOPK_GROUPS):
        _, _, hit = pick_first_max(cur, iota_g, N_EXPERT_GROUPS)
        group_sel = jnp.where(hit, 1.0, group_sel)
        cur = jnp.where(hit, neg, cur)
    masked = jnp.concatenate(
        [jnp.where(group_sel[g:g + 1, :] > 0.0, biased[g * epg:(g + 1) * epg, :], neg)
         for g in range(N_EXPERT_GROUPS)], axis=0)
    iota_e = lax.broadcasted_iota(jnp.int32, masked.shape, 0).astype(F32)
    sel = jnp.zeros_like(masked)
    cur = masked
    picks = []
    for _ in range(TOP_K):
        _, idx, hit = pick_first_max(cur, iota_e, N_EXPERTS)
        picks.append(idx)
        sel = jnp.where(hit, 1.0, sel)
        cur = jnp.where(hit, neg, cur)
    w = scores * sel
    return w / jnp.sum(w, axis=0, keepdims=True) * ROUTED_SCALE, sel, picks


def _router_kernel(x_ref, mod_ref, g2_ref, wrt_ref, rb_ref, hp_ref, ek_ref, rk_ref, wt_ref, cnt_ref, run_scr):
    tm = x_ref.shape[0]

    @pl.when(pl.program_id(0) == 0)
    def _():
        run_scr[...] = jnp.zeros_like(run_scr)

    h = _rms_mod(x_ref[...], g2_ref[...], mod_ref[0, 3:4, :], mod_ref[0, 4:5, :])
    half = D_MODEL // 2
    _store_token_words(hp_ref, _pack_pair(h[:, :half], h[:, half:]), tm)

    logits_t = lax.dot_general(wrt_ref[...], h, (((1,), (1,)), ((), ())),
                               precision=lax.Precision.HIGHEST, preferred_element_type=F32)
    scores = jax.nn.sigmoid(logits_t)
    comb_t, sel, picks = _route(scores, scores + rb_ref[...])

    earlier = (lax.broadcasted_iota(jnp.int32, (tm, tm), 0) < lax.broadcasted_iota(jnp.int32, (tm, tm), 1))
    rank_t = _dot(sel.astype(BF16), jnp.where(earlier, 1.0, 0.0).astype(BF16)) + run_scr[...]
    run_scr[...] += jnp.sum(sel, axis=1, keepdims=True)
    cnt_ref[...] = jnp.broadcast_to(run_scr[...], cnt_ref.shape)

    iota_e = lax.broadcasted_iota(jnp.int32, sel.shape, 0).astype(F32)
    ranks, weights = [], []
    for idx in picks:
        hit = iota_e == idx
        ranks.append(jnp.sum(jnp.where(hit, rank_t, 0.0), axis=0, keepdims=True))
        weights.append(jnp.sum(jnp.where(hit, comb_t, 0.0), axis=0, keepdims=True))
    ek_ref[...] = jnp.concatenate(picks, axis=0).astype(jnp.int32)
    rk_ref[...] = jnp.concatenate(ranks, axis=0).astype(jnp.int32)
    w_pad = jnp.concatenate(weights + [jnp.zeros((128 - TOP_K, tm), F32)], axis=0)
    wt_ref[...] = w_pad.T


def _router(x1, mod3, norm2_g, w_router_t, router_bias, seq_len, mod_row_of_batch):
    t = x1.shape[0]
    tm = TM_ROUTER

    def mod_idx(i):
        return (mod_row_of_batch((i * tm) // seq_len), 0, 0)

    def full(a):
        return pl.BlockSpec(a.shape, lambda i: (0,) * a.ndim)

    return pl.pallas_call(
        _router_kernel,
        grid=(t // tm,),
        in_specs=[pl.BlockSpec((tm, D_MODEL), lambda i: (i, 0)),
                  pl.BlockSpec((1, 6, D_MODEL), mod_idx),
                  full(norm2_g), full(w_router_t), full(router_bias)],
        out_specs=[pl.BlockSpec((tm * ROW_SLABS, 128), lambda i: (i, 0)),
                   pl.BlockSpec((TOP_K, tm), lambda i: (0, i)),
                   pl.BlockSpec((TOP_K, tm), lambda i: (0, i)),
                   pl.BlockSpec((tm, 128), lambda i: (i, 0)),
                   pl.BlockSpec((N_EXPERTS, 128), lambda i: (0, 0))],
        out_shape=[jax.ShapeDtypeStruct((t * ROW_SLABS, 128), jnp.int32),
                   jax.ShapeDtypeStruct((TOP_K, t), jnp.int32),
                   jax.ShapeDtypeStruct((TOP_K, t), jnp.int32),
                   jax.ShapeDtypeStruct((t, 128), F32),
                   jax.ShapeDtypeStruct((N_EXPERTS, 128), F32)],
        scratch_shapes=[pltpu.VMEM((N_EXPERTS, 1), F32)],
        compiler_params=pltpu.CompilerParams(dimension_semantics=("arbitrary",),
                                             vmem_limit_bytes=VMEM_LIMIT),
        name="router",
    )(x1, mod3, norm2_g, w_router_t, router_bias)


def _plan_kernel(ek_ref, rk_ref, cnt_ref, pos_ref, texp_ref, nused_ref, tend_ref):
    rows = float(EXPERT_ROWS)
    cnt = cnt_ref[:, 0:1]
    tiles = jnp.floor((cnt + (rows - 1.0)) / rows)
    before = (lax.broadcasted_iota(jnp.int32, (N_EXPERTS, N_EXPERTS), 1)
              < lax.broadcasted_iota(jnp.int32, (N_EXPERTS, N_EXPERTS), 0))
    tile_start = jnp.dot(jnp.where(before, 1.0, 0.0), jnp.broadcast_to(tiles, (N_EXPERTS, 128)),
                         precision=lax.Precision.HIGHEST, preferred_element_type=F32)[:, 0:1]
    tile_end = tile_start + tiles
    row_start = tile_start * rows

    ek = ek_ref[...]
    pos = rk_ref[...].astype(F32)
    tile_id = lax.broadcasted_iota(jnp.int32, texp_ref.shape, 1).astype(F32)
    texp = jnp.zeros(texp_ref.shape, F32)
    for e in range(N_EXPERTS):
        pos = pos + jnp.where(ek == e, row_start[e:e + 1, :], 0.0)
        texp = texp + jnp.where(tile_id >= tile_end[e:e + 1, :], 1.0, 0.0)
    pos_ref[...] = pos.astype(jnp.int32)
    texp_ref[...] = jnp.minimum(texp, N_EXPERTS - 1.0).astype(jnp.int32)
    nused_ref[...] = jnp.broadcast_to(tile_end[N_EXPERTS - 1:N_EXPERTS, :], nused_ref.shape).astype(jnp.int32)
    tend_ref[...] = jnp.broadcast_to(tile_end, tend_ref.shape).astype(jnp.int32)


def _plan(ek, rk, cnt, n_tiles_pad):
    t = ek.shape[1]

    def full(shape):
        return pl.BlockSpec(shape, lambda: (0,) * len(shape))

    return pl.pallas_call(
        _plan_kernel,
        in_specs=[full(ek.shape), full(rk.shape), full(cnt.shape)],
        out_specs=[full((TOP_K, t)), full((1, n_tiles_pad)), full((1, 128)), full((N_EXPERTS, 128))],
        out_shape=[jax.ShapeDtypeStruct((TOP_K, t), jnp.int32),
                   jax.ShapeDtypeStruct((1, n_tiles_pad), jnp.int32),
                   jax.ShapeDtypeStruct((1, 128), jnp.int32),
                   jax.ShapeDtypeStruct((N_EXPERTS, 128), jnp.int32)],
        compiler_params=pltpu.CompilerParams(vmem_limit_bytes=VMEM_LIMIT),
        name="plan",
    )(ek, rk, cnt)


def _sc_mesh():
    return plsc.VectorSubcoreMesh(core_axis_name="c", subcore_axis_name="s")


def _sc_dispatch(rows, pos3, n_out):
    t = rows.shape[0]
    ch = SC_CHUNK
    per_w = (t // ch) // SC_WORKERS

    @functools.partial(
        pl.kernel, out_type=jax.ShapeDtypeStruct((n_out,) + rows.shape[1:], jnp.int32), mesh=_sc_mesh(),
        scratch_types=[pltpu.VMEM((TOP_K, ch), jnp.int32), pltpu.VMEM((ch,) + rows.shape[1:], jnp.int32),
                       pltpu.SemaphoreType.DMA])
    def k(rows_hbm, pos_hbm, out_hbm, idx_v, rows_v, sem):
        wid = lax.axis_index("s") * SC_CORES + lax.axis_index("c")

        @pl.loop(0, per_w)
        def _(j):
            c = wid * per_w + j
            pltpu.sync_copy(pos_hbm.at[c], idx_v)
            pltpu.sync_copy(rows_hbm.at[pl.ds(c * ch, ch)], rows_v)
            copies = [pltpu.async_copy(rows_v, out_hbm.at[idx_v.at[kk]], sem) for kk in range(TOP_K)]
            for cp in copies:
                cp.wait()

    return k(rows, pos3)


def _sc_gather(table, pos3, t):
    ch = SC_CHUNK
    per_w = (t // ch) // SC_WORKERS

    @functools.partial(
        pl.kernel, out_type=jax.ShapeDtypeStruct((TOP_K, t) + table.shape[1:], jnp.int32), mesh=_sc_mesh(),
        scratch_types=[pltpu.VMEM((TOP_K, ch), jnp.int32), pltpu.VMEM((ch,) + table.shape[1:], jnp.int32),
                       pltpu.SemaphoreType.DMA])
    def k(tab_hbm, pos_hbm, out_hbm, idx_v, rows_v, sem):
        wid = lax.axis_index("s") * SC_CORES + lax.axis_index("c")

        @pl.loop(0, per_w)
        def _(j):
            c = wid * per_w + j
            pltpu.sync_copy(pos_hbm.at[c], idx_v)
            for kk in range(TOP_K):
                pltpu.async_copy(tab_hbm.at[idx_v.at[kk]], rows_v, sem).wait()
                pltpu.sync_copy(rows_v, out_hbm.at[kk, pl.ds(c * ch, ch)])

    return k(table, pos3)


def _experts_kernel(texp_ref, nused_ref, tend_ref, xs_ref, weg_hbm, weu_hbm, wed_hbm, ys_ref,
                    wg_scr, wu_scr, wd_scr, wg_buf, wu_buf, wd_buf, sem, group_scr):
    j = pl.program_id(0)
    rows = EXPERT_ROWS
    half = D_MODEL // 2
    n_used = nused_ref[0]
    expert = texp_ref[j]
    used = j < n_used
    new_expert = (j == 0) | (expert != texp_ref[jnp.maximum(j - 1, 0)])

    def weight_copies(e, slot):
        return [pltpu.make_async_copy(weg_hbm.at[e], wg_buf.at[slot], sem.at[slot, 0]),
                pltpu.make_async_copy(weu_hbm.at[e], wu_buf.at[slot], sem.at[slot, 1]),
                pltpu.make_async_copy(wed_hbm.at[e], wd_buf.at[slot], sem.at[slot, 2])]

    @pl.when(j == 0)
    def _():
        group_scr[0] = 0
        for cp in weight_copies(expert, 0):
            cp.start()

    @pl.when(used & new_expert)
    def _():
        group = group_scr[0]
        slot = group % 2
        next_tile = tend_ref[expert]

        @pl.when(next_tile < n_used)
        def _():
            for cp in weight_copies(texp_ref[next_tile], 1 - slot):
                cp.start(priority=1)

        for cp in weight_copies(expert, slot):
            cp.wait()
        wg_scr[...] = wg_buf[slot].astype(BF16)
        wu_scr[...] = wu_buf[slot].astype(BF16)
        wd_scr[...] = wd_buf[slot].astype(BF16)
        group_scr[0] = group + 1

    @pl.when(used)
    def _():
        lo, hi = _unpack_pair(_load_token_words(xs_ref, (), rows))
        lo = lo.astype(BF16)
        hi = hi.astype(BF16)
        g = _dot(lo, wg_scr[0:half, :]) + _dot(hi, wg_scr[half:D_MODEL, :])
        u = _dot(lo, wu_scr[0:half, :]) + _dot(hi, wu_scr[half:D_MODEL, :])
        y = _dot((_silu(g) * u).astype(BF16), wd_scr[...])
        _store_token_words(ys_ref, _pack_pair(y[:, :half], y[:, half:]), rows)


def _experts(texp, nused, tend, xs2d, weg, weu, wed, n_tiles):
    rows = EXPERT_ROWS
    hbm = pl.BlockSpec(memory_space=pl.ANY)
    grid_spec = pltpu.PrefetchScalarGridSpec(
        num_scalar_prefetch=3,
        grid=(n_tiles,),
        in_specs=[pl.BlockSpec((rows * ROW_SLABS, 128), lambda j, te, nu, tn: (jnp.minimum(j, nu[0] - 1), 0)),
                  hbm, hbm, hbm],
        out_specs=pl.BlockSpec((rows * ROW_SLABS, 128), lambda j, te, nu, tn: (jnp.minimum(j, nu[0] - 1), 0)),
        scratch_shapes=[pltpu.VMEM((D_MODEL, EXPERT_DIM), BF16),
                        pltpu.VMEM((D_MODEL, EXPERT_DIM), BF16),
                        pltpu.VMEM((EXPERT_DIM, D_MODEL), BF16),
                        pltpu.VMEM((2, D_MODEL, EXPERT_DIM), F32),
                        pltpu.VMEM((2, D_MODEL, EXPERT_DIM), F32),
                        pltpu.VMEM((2, EXPERT_DIM, D_MODEL), F32),
                        pltpu.SemaphoreType.DMA((2, 3)),
                        pltpu.SMEM((1,), jnp.int32)],
    )
    return pl.pallas_call(
        _experts_kernel,
        grid_spec=grid_spec,
        out_shape=jax.ShapeDtypeStruct(xs2d.shape, jnp.int32),
        compiler_params=pltpu.CompilerParams(dimension_semantics=("arbitrary",),
                                             vmem_limit_bytes=VMEM_LIMIT),
        name="experts",
    )(texp, nused, tend, xs2d, weg, weu, wed)


def _final_kernel(x_ref, hp_ref, y8_ref, wt_ref, mod_ref, wsg_ref, wsu_ref, wsd_ref, fng_ref, o_ref):
    tm = x_ref.shape[0]
    lo, hi = _unpack_pair(_load_token_words(hp_ref, (), tm))
    hb = jnp.concatenate([lo, hi], axis=1).astype(BF16)
    shared = _dot((_silu(_dot(hb, wsg_ref[...])) * _dot(hb, wsu_ref[...])).astype(BF16), wsd_ref[...])
    wt = wt_ref[...]
    r_lo = jnp.zeros((tm, D_MODEL // 2), F32)
    r_hi = jnp.zeros((tm, D_MODEL // 2), F32)
    for k in range(TOP_K):
        lo, hi = _unpack_pair(_load_token_words(y8_ref, (k,), tm))
        wk = wt[:, k:k + 1]
        r_lo = r_lo + wk * lo
        r_hi = r_hi + wk * hi
    routed = jnp.concatenate([r_lo, r_hi], axis=1)
    y = x_ref[...] + mod_ref[0, 5:6, :] * (routed + shared)
    ms = jnp.mean(y * y, axis=-1, keepdims=True)
    o_ref[...] = y * lax.rsqrt(ms + EPS) * fng_ref[...]


def _final(x1, hp2d, y8, wtok, mod3, wsg, wsu, wsd, final_g, seq_len, mod_row_of_batch):
    t = x1.shape[0]
    tm = TM_FINAL

    def mod_idx(i):
        return (mod_row_of_batch((i * tm) // seq_len), 0, 0)

    def full(a):
        return pl.BlockSpec(a.shape, lambda i: (0,) * a.ndim)

    return pl.pallas_call(
        _final_kernel,
        grid=(t // tm,),
        in_specs=[pl.BlockSpec((tm, D_MODEL), lambda i: (i, 0)),
                  pl.BlockSpec((tm * ROW_SLABS, 128), lambda i: (i, 0)),
                  pl.BlockSpec((TOP_K, tm * ROW_SLABS, 128), lambda i: (0, i, 0)),
                  pl.BlockSpec((tm, 128), lambda i: (i, 0)),
                  pl.BlockSpec((1, 6, D_MODEL), mod_idx),
                  full(wsg), full(wsu), full(wsd), full(final_g)],
        out_specs=pl.BlockSpec((tm, D_MODEL), lambda i: (i, 0)),
        out_shape=jax.ShapeDtypeStruct((t, D_MODEL), F32),
        compiler_params=pltpu.CompilerParams(dimension_semantics=("parallel",),
                                             vmem_limit_bytes=VMEM_LIMIT),
        name="final",
    )(x1, hp2d, y8, wtok, mod3, wsg, wsu, wsd, final_g)


def _moe(x1, mod3, lw, seq_len, mod_row_of_batch):
    t = x1.shape[0]
    n_tiles = TOP_K * t // EXPERT_ROWS + N_EXPERTS
    n_tiles_pad = -(-n_tiles // 128) * 128
    hp2d, ek, rk, wtok, cnt = _router(x1, mod3, lw["norm2_g"], lw["w_router_t"], lw["router_bias"],
                                      seq_len, mod_row_of_batch)
    pos, texp, nused, tend = _plan(ek, rk, cnt, n_tiles_pad)
    pos3 = pos.reshape(TOP_K, t // SC_CHUNK, SC_CHUNK).transpose(1, 0, 2)
    xs = _sc_dispatch(hp2d.reshape(t, ROW_SLABS, 128), pos3, n_tiles * EXPERT_ROWS)
    ys2d = _experts(texp.reshape(-1), nused.reshape(-1), tend[:, 0], xs.reshape(-1, 128),
                    lw["weg"], lw["weu"], lw["wed"], n_tiles)
    y8 = _sc_gather(ys2d.reshape(-1, ROW_SLABS, 128), pos3, t)
    return _final(x1, hp2d, y8.reshape(TOP_K, t * ROW_SLABS, 128), wtok, mod3,
                  lw["wsg"], lw["wsu"], lw["wsd"], lw["final_g"], seq_len, mod_row_of_batch)


def _dft_tables(seq_len):
    gd = FOURIER_GROUP_DIM
    kc = np.arange(gd)
    ang_c = ((kc[:, None] * kc[None, :]) % gd) * (2.0 * math.pi / gd)
    cs = np.concatenate([np.cos(ang_c), np.sin(ang_c)], axis=1) * (gd ** -0.5)
    kl = np.arange(seq_len)
    ang_l = ((kl[:, None] * kl[None, :]) % seq_len) * (2.0 * math.pi / seq_len)
    cls = np.concatenate([np.cos(ang_l), -np.sin(ang_l)], axis=1) * (seq_len ** -0.5)
    return jnp.asarray(cs.astype(np.float32), dtype=BF16), jnp.asarray(cls.astype(np.float32), dtype=BF16)


def _rope_tables(length):
    rows = length // GRID_W
    r = np.repeat(np.arange(rows, dtype=np.float32), GRID_W)
    col = np.tile(np.arange(GRID_W, dtype=np.float32), rows)
    nf = RET_HEAD_DIM // 4
    inv = (np.float32(ROPE_BASE) ** (-np.arange(nf, dtype=np.float32) / np.float32(nf))).astype(np.float32)
    ar = r[:, None] * inv[None]
    ac = col[:, None] * inv[None]
    ang = np.concatenate([ar, ar, ac, ac], axis=-1).astype(np.float64)
    sign = np.where((np.arange(RET_HEAD_DIM) & nf) == 0, -1.0, 1.0)
    return (jnp.asarray(np.cos(ang).astype(np.float32)),
            jnp.asarray((np.sin(ang) * sign[None, :]).astype(np.float32)))


def _trunk_path(x, mod3, mod_row_of_batch, s0f, s0b, rope, lw):
    batch, seq_len, _ = x.shape
    x2d = x.reshape(batch * seq_len, D_MODEL)
    uf, q, k, v, sg, gf, gr = _inproj(x2d, mod3, lw["norm1_g"], lw["w_in"], seq_len, mod_row_of_batch, rope)
    r, s_f, s_b = _retention(q, k, v, sg, lw["dec"], lw["gn_g"], s0f, s0b, batch, seq_len)
    cs, cls = _dft_tables(seq_len)
    fmix = _fnet(uf, cs, cls, batch, seq_len)
    x1 = _merge(fmix, r, gf, gr, x2d, mod3, lw["w_four"], lw["w_ret"], lw["w_o"], seq_len, mod_row_of_batch)
    y = _moe(x1, mod3, lw, seq_len, mod_row_of_batch)
    return y.reshape(batch, seq_len, D_MODEL), s_f, s_b


def kernel(x_prompt, x_sample, state_ret_fwd, state_ret_bwd, c, c_ctx, w_ada, b_ada, norm1_g, norm2_g, w_in,
           ret_decay_fwd, ret_decay_bwd, ret_gn_g, w_four_out, w_ret_out, w_out, w_router, router_bias,
           w_exp_gate, w_exp_up, w_exp_down, w_shared_gate, w_shared_up, w_shared_down, final_norm_g):
    depth = w_ada.shape[0]
    assert depth == 1, "final norm is fused into the last layer's MoE kernel"
    n_ctx, n_lat = x_prompt.shape[0], x_sample.shape[0]
    cond = jnp.concatenate([c_ctx[None, :], c], axis=0)
    cond = jnp.pad(cond, ((0, (-cond.shape[0]) % 8), (0, 0)))
    rope = _rope_tables(x_sample.shape[1])
    zeros = jnp.zeros((n_ctx, N_RET_HEADS, RET_HEAD_DIM, RET_HEAD_DIM), F32)

    layer = 0
    mod = _ada(cond, w_ada[layer], b_ada[layer][None, :])
    mod3 = mod.reshape(mod.shape[0], 6, D_MODEL)
    dec = jnp.stack([ret_decay_fwd[layer], ret_decay_bwd[layer]], axis=1)
    lw = {
        "norm1_g": norm1_g[layer][None, :],
        "norm2_g": norm2_g[layer][None, :],
        "w_in": w_in[layer].astype(BF16),
        "dec": jnp.broadcast_to(dec[:, :, None], (N_RET_HEADS, 2, RET_HEAD_DIM)).astype(F32),
        "gn_g": ret_gn_g[layer][None, :],
        "w_four": w_four_out[layer].astype(BF16),
        "w_ret": w_ret_out[layer].astype(BF16),
        "w_o": w_out[layer].astype(BF16),
        "w_router_t": w_router[layer].T,
        "router_bias": router_bias[layer][:, None],
        "weg": w_exp_gate[layer],
        "weu": w_exp_up[layer],
        "wed": w_exp_down[layer],
        "wsg": w_shared_gate[layer].astype(BF16),
        "wsu": w_shared_up[layer].astype(BF16),
        "wsd": w_shared_down[layer].astype(BF16),
        "final_g": final_norm_g[None, :],
    }
    y_prompt, s_f, s_b = _trunk_path(x_prompt, mod3, lambda b: 0, zeros, zeros, None, lw)
    y_sample, _, _ = _trunk_path(x_sample, mod3, lambda b: 1 + b, state_ret_fwd[:, layer],
                                 state_ret_bwd[:, layer], rope, lw)
    return (y_prompt, y_sample, s_f[:, None], s_b[:, None])
```

```python
import functools
import math

import jax
import jax.numpy as jnp
import numpy as np
from jax import lax
from jax.experimental import pallas as pl
from jax.experimental.pallas import tpu as pltpu
from jax.experimental.pallas import tpu_sc as plsc

F32 = jnp.float32
BF16 = jnp.bfloat16

D_MODEL = 1024
GRID_W = 64
N_FOURIER_GROUPS = 8
FOURIER_GROUP_DIM = 128
N_RET_HEADS = 4
RET_HEAD_DIM = 128
RET_WIDTH = N_RET_HEADS * RET_HEAD_DIM
CHUNK = 128
N_EXPERTS = 64
N_EXPERT_GROUPS = 8
EXPERTS_PER_GROUP = N_EXPERTS // N_EXPERT_GROUPS
TOPK_GROUPS = 4
TOP_K = 8
EXPERT_DIM = 256
ROUTED_SCALE = 2.5
ROPE_BASE = 10000.0
EPS = 1e-6
Q_SCALE = RET_HEAD_DIM ** -0.5

_C_UF = (0, 1024)
_C_Q = (1024, 1536)
_C_K = (1536, 2048)
_C_V = (2048, 2560)
_C_G = (2560, 3072)
_C_GF = (3072, 4096)
_C_GR = (4096, 5120)

VMEM_LIMIT = 56 * 1024 * 1024

TM_PROJ = 512
FNET_ROWS = 256
TM_ROUTER = 1024
TM_FINAL = 512
EXPERT_ROWS = 256
TILES_PER_STEP = 4
ROW_SLABS = 4
SC_CORES = 2
SC_WORKERS = 32
SC_CHUNK = 128


def _silu(x):
    return x * jax.nn.sigmoid(x)


def _dot(a, b):
    return jnp.dot(a, b, preferred_element_type=F32)


def _rms_mod(x, g, shift, scale):
    ms = jnp.mean(x * x, axis=-1, keepdims=True)
    y = x * lax.rsqrt(ms + EPS) * g
    return y * (1.0 + scale) + shift


def _ada_kernel(cond_ref, w_ref, b_ref, o_ref):
    s = _silu(cond_ref[...]).astype(BF16)
    o_ref[...] = _dot(s, w_ref[...].astype(BF16)) + b_ref[...]


def _ada(cond, w_ada, b_ada):
    rows, n = cond.shape[0], w_ada.shape[1]
    tn = 1536
    return pl.pallas_call(
        _ada_kernel,
        grid=(n // tn,),
        in_specs=[pl.BlockSpec((rows, D_MODEL), lambda j: (0, 0)),
                  pl.BlockSpec((D_MODEL, tn), lambda j: (0, j)),
                  pl.BlockSpec((1, tn), lambda j: (0, j))],
        out_specs=pl.BlockSpec((rows, tn), lambda j: (0, j)),
        out_shape=jax.ShapeDtypeStruct((rows, n), F32),
        compiler_params=pltpu.CompilerParams(vmem_limit_bytes=VMEM_LIMIT),
        name="ada",
    )(cond, w_ada, b_ada)


def _rope_head(x, cos, sin_signed, first_half):
    partner = jnp.where(first_half, pltpu.roll(x, 96, 1), pltpu.roll(x, 32, 1))
    return x * cos + partner * sin_signed


def _inproj_kernel(*refs, use_rope):
    if use_rope:
        x_ref, mod_ref, g_ref, w_ref, cos_ref, sin_ref = refs[:6]
        outs = refs[6:]
    else:
        x_ref, mod_ref, g_ref, w_ref = refs[:4]
        outs = refs[4:]
    uf_o, q_o, k_o, v_o, sg_o, gf_o, gr_o = outs

    h = _rms_mod(x_ref[...], g_ref[...], mod_ref[0, 0:1, :], mod_ref[0, 1:2, :])
    hb = h.astype(BF16)

    def proj(cols):
        return _dot(hb, w_ref[:, cols[0]:cols[1]])

    uf_o[...] = proj(_C_UF).astype(BF16)
    q = proj(_C_Q)
    k = proj(_C_K)
    if use_rope:
        cos = cos_ref[...]
        sin_signed = sin_ref[...]
        lane = lax.broadcasted_iota(jnp.int32, cos.shape, 1)
        first_half = (lane & 32) == 0
        for hd in range(N_RET_HEADS):
            sl = slice(hd * RET_HEAD_DIM, (hd + 1) * RET_HEAD_DIM)
            q_o[:, sl] = (_rope_head(q[:, sl], cos, sin_signed, first_half) * Q_SCALE).astype(BF16)
            k_o[:, sl] = _rope_head(k[:, sl], cos, sin_signed, first_half).astype(BF16)
    else:
        q_o[...] = (q * Q_SCALE).astype(BF16)
        k_o[...] = k.astype(BF16)
    v_o[...] = proj(_C_V).astype(BF16)
    sg_o[...] = _silu(proj(_C_G)).astype(BF16)
    gf_o[...] = jax.nn.sigmoid(proj(_C_GF)).astype(BF16)
    gr_o[...] = jax.nn.sigmoid(proj(_C_GR)).astype(BF16)


def _inproj(x2d, mod3, norm_g, w_in_bf, seq_len, mod_row_of_batch, rope):
    t = x2d.shape[0]
    tm = TM_PROJ
    tiles_per_seq = max(seq_len // tm, 1)

    def mod_idx(i):
        return (mod_row_of_batch((i * tm) // seq_len), 0, 0)

    in_specs = [pl.BlockSpec((tm, D_MODEL), lambda i: (i, 0)),
                pl.BlockSpec((1, 6, D_MODEL), mod_idx),
                pl.BlockSpec((1, D_MODEL), lambda i: (0, 0)),
                pl.BlockSpec(w_in_bf.shape, lambda i: (0, 0))]
    args = [x2d, mod3, norm_g, w_in_bf]
    if rope is not None:
        in_specs += [pl.BlockSpec((tm, RET_HEAD_DIM), lambda i: (i % tiles_per_seq, 0))] * 2
        args += list(rope)
    widths = [1024, RET_WIDTH, RET_WIDTH, RET_WIDTH, RET_WIDTH, 1024, 1024]
    return pl.pallas_call(
        functools.partial(_inproj_kernel, use_rope=rope is not None),
        grid=(t // tm,),
        in_specs=in_specs,
        out_specs=[pl.BlockSpec((tm, w), lambda i: (i, 0)) for w in widths],
        out_shape=[jax.ShapeDtypeStruct((t, w), BF16) for w in widths],
        compiler_params=pltpu.CompilerParams(dimension_semantics=("parallel",),
                                             vmem_limit_bytes=VMEM_LIMIT),
        name="inproj",
    )(*args)


def _retention_kernel(q_ref, k_ref, v_ref, sg_ref, dec_ref, gn_ref, s0f_ref, s0b_ref,
                      r_ref, sfo_ref, sbo_ref):
    n_chunks = q_ref.shape[0] // CHUNK
    dec = dec_ref[...]
    lg = jnp.minimum(dec, 0.0) - jnp.log1p(jnp.exp(-jnp.abs(dec)))
    lgf = lg[0:1, :]
    lgb = lg[1:2, :]
    row = lax.broadcasted_iota(jnp.int32, (CHUNK, CHUNK), 0).astype(F32)
    col = lax.broadcasted_iota(jnp.int32, (CHUNK, CHUNK), 1).astype(F32)
    diff = row - col
    decay = jnp.exp(jnp.where(diff >= 0, lgf * diff, lgb * (-diff)))
    qw_f = jnp.exp(lgf * (row + 1.0))
    qw_b = jnp.exp(lgb * (CHUNK - row))
    kw_f = jnp.exp(lgf * (CHUNK - 1.0 - row))
    kw_b = jnp.exp(lgb * row)
    gc_f = jnp.exp(lgf * CHUNK)
    gc_b = jnp.exp(lgb * CHUNK)

    def rows(n):
        return slice(n * CHUNK, (n + 1) * CHUNK)

    kv_f, kv_b = [], []
    for n in range(n_chunks):
        kn = k_ref[rows(n), :].astype(F32)
        vn = v_ref[rows(n), :]
        kv_f.append(_dot((kn * kw_f).T.astype(BF16), vn))
        kv_b.append(_dot((kn * kw_b).T.astype(BF16), vn))

    s = s0f_ref[...]
    prev_f = []
    for n in range(n_chunks):
        prev_f.append(s.astype(BF16))
        s = gc_f * s + kv_f[n]
    sfo_ref[...] = s
    s = s0b_ref[...]
    prev_b = [None] * n_chunks
    for n in reversed(range(n_chunks)):
        prev_b[n] = s.astype(BF16)
        s = gc_b * s + kv_b[n]
    sbo_ref[...] = s

    gn = gn_ref[...]
    for n in range(n_chunks):
        qn = q_ref[rows(n), :]
        qf = qn.astype(F32)
        scores = lax.dot_general(qn, k_ref[rows(n), :], (((1,), (1,)), ((), ())),
                                 preferred_element_type=F32)
        o = _dot((scores * decay).astype(BF16), v_ref[rows(n), :])
        o = o + _dot((qf * qw_f).astype(BF16), prev_f[n])
        o = o + _dot((qf * qw_b).astype(BF16), prev_b[n])
        mu = jnp.mean(o, axis=-1, keepdims=True)
        d = o - mu
        var = jnp.mean(d * d, axis=-1, keepdims=True)
        on = d * lax.rsqrt(var + EPS) * gn
        r_ref[rows(n), :] = (on * sg_ref[rows(n), :].astype(F32)).astype(BF16)


def _retention(q, k, v, sg, dec, gn_g, s0f, s0b, batch, seq_len):
    hd = RET_HEAD_DIM
    tok_spec = pl.BlockSpec((seq_len, hd), lambda b, h: (b, h))
    st_spec = pl.BlockSpec((None, None, hd, hd), lambda b, h: (b, h, 0, 0))
    st_shape = jax.ShapeDtypeStruct((batch, N_RET_HEADS, hd, hd), F32)
    return pl.pallas_call(
        _retention_kernel,
        grid=(batch, N_RET_HEADS),
        in_specs=[tok_spec, tok_spec, tok_spec, tok_spec,
                  pl.BlockSpec((None, 2, hd), lambda b, h: (h, 0, 0)),
                  pl.BlockSpec((1, hd), lambda b, h: (0, h)),
                  st_spec, st_spec],
        out_specs=[tok_spec, st_spec, st_spec],
        out_shape=[jax.ShapeDtypeStruct((batch * seq_len, RET_WIDTH), BF16), st_shape, st_shape],
        compiler_params=pltpu.CompilerParams(dimension_semantics=("parallel", "parallel"),
                                             vmem_limit_bytes=VMEM_LIMIT),
        name="retention",
    )(q, k, v, sg, dec, gn_g, s0f, s0b)


def _fnet_kernel(uf_ref, cs_ref, cls_ref, o_ref, xcs_ref):
    seq_len = uf_ref.shape[0]
    gd = FOURIER_GROUP_DIM

    @pl.when(pl.program_id(1) == 0)
    def _():
        for g in range(N_FOURIER_GROUPS):
            x = _dot(uf_ref[:, g * gd:(g + 1) * gd], cs_ref[...])
            xcs_ref[0:seq_len, g * gd:(g + 1) * gd] = x[:, :gd].astype(BF16)
            xcs_ref[seq_len:2 * seq_len, g * gd:(g + 1) * gd] = x[:, gd:].astype(BF16)

    o_ref[...] = _dot(cls_ref[...], xcs_ref[...]).astype(BF16)


def _fnet(uf, cs, cls, batch, seq_len):
    rb = FNET_ROWS
    nr = seq_len // rb
    return pl.pallas_call(
        _fnet_kernel,
        grid=(batch, nr),
        in_specs=[pl.BlockSpec((seq_len, D_MODEL), lambda b, r: (b, 0)),
                  pl.BlockSpec(cs.shape, lambda b, r: (0, 0)),
                  pl.BlockSpec((rb, 2 * seq_len), lambda b, r: (r, 0))],
        out_specs=pl.BlockSpec((rb, D_MODEL), lambda b, r: (b * nr + r, 0)),
        out_shape=jax.ShapeDtypeStruct((batch * seq_len, D_MODEL), BF16),
        scratch_shapes=[pltpu.VMEM((2 * seq_len, D_MODEL), BF16)],
        compiler_params=pltpu.CompilerParams(dimension_semantics=("parallel", "arbitrary"),
                                             vmem_limit_bytes=VMEM_LIMIT),
        name="fnet",
    )(uf, cs, cls)


def _merge_kernel(fm_ref, r_ref, gf_ref, gr_ref, x_ref, mod_ref, wf_ref, wr_ref, wo_ref, o_ref):
    f_out = _dot(fm_ref[...], wf_ref[...])
    r_out = _dot(r_ref[...], wr_ref[...])
    merged = gf_ref[...].astype(F32) * f_out + gr_ref[...].astype(F32) * r_out
    mix = _dot(merged.astype(BF16), wo_ref[...])
    o_ref[...] = x_ref[...] + mod_ref[0, 2:3, :] * mix


def _merge(fmix, r, gf, gr, x2d, mod3, w_four, w_ret, w_o, seq_len, mod_row_of_batch):
    t = x2d.shape[0]
    tm = TM_PROJ

    def mod_idx(i):
        return (mod_row_of_batch((i * tm) // seq_len), 0, 0)

    def tok(w):
        return pl.BlockSpec((tm, w), lambda i: (i, 0))

    def full(a):
        return pl.BlockSpec(a.shape, lambda i: (0, 0))

    return pl.pallas_call(
        _merge_kernel,
        grid=(t // tm,),
        in_specs=[tok(D_MODEL), tok(RET_WIDTH), tok(D_MODEL), tok(D_MODEL), tok(D_MODEL),
                  pl.BlockSpec((1, 6, D_MODEL), mod_idx), full(w_four), full(w_ret), full(w_o)],
        out_specs=tok(D_MODEL),
        out_shape=jax.ShapeDtypeStruct((t, D_MODEL), F32),
        compiler_params=pltpu.CompilerParams(dimension_semantics=("parallel",),
                                             vmem_limit_bytes=VMEM_LIMIT),
        name="merge",
    )(fmix, r, gf, gr, x2d, mod3, w_four, w_ret, w_o)


def _pack_pair(lo_f32, hi_f32):
    lo = lax.bitcast_convert_type(lo_f32.astype(BF16).astype(F32), jnp.uint32)
    hi = lax.bitcast_convert_type(hi_f32.astype(BF16).astype(F32), jnp.uint32)
    return lax.bitcast_convert_type((lo >> 16) | hi, jnp.int32)


def _unpack_pair(words_i32):
    w = lax.bitcast_convert_type(words_i32, jnp.uint32)
    lo = lax.bitcast_convert_type(w << 16, F32)
    hi = lax.bitcast_convert_type(w & jnp.uint32(0xFFFF0000), F32)
    return lo, hi


def _load_token_words(ref, lead, n_tok):
    parts = []
    for s in range(ROW_SLABS):
        idx = (pl.ds(s, n_tok, stride=ROW_SLABS), slice(None))
        parts.append(ref[lead + idx] if lead else ref[idx])
    return jnp.concatenate(parts, axis=1)


def _store_token_words(ref, words, n_tok):
    for s in range(ROW_SLABS):
        ref[pl.ds(s, n_tok, stride=ROW_SLABS), :] = words[:, s * 128:(s + 1) * 128]


def _route(scores, biased):
    tokens = scores.shape[1]
    neg = -jnp.inf
    epg = EXPERTS_PER_GROUP
    iota_g = lax.broadcasted_iota(jnp.int32, (epg, tokens), 0).astype(F32)

    def pick_first_max(cur, iota, size):
        m = jnp.max(cur, axis=0, keepdims=True)
        idx = jnp.min(jnp.where(cur == m, iota, float(size)), axis=0, keepdims=True)
        return m, idx, iota == idx

    group_scores = []
    for g in range(N_EXPERT_GROUPS):
        vals = biased[g * epg:(g + 1) * epg, :]
        m1, _, hit = pick_first_max(vals, iota_g, epg)
        m2 = jnp.max(jnp.where(hit, neg, vals), axis=0, keepdims=True)
        group_scores.append(m1 + m2)
    cur = jnp.concatenate(group_scores, axis=0)
    group_sel = jnp.zeros_like(cur)
    for _ in range(TOPK_GROUPS):
        _, _, hit = pick_first_max(cur, iota_g, N_EXPERT_GROUPS)
        group_sel = jnp.where(hit, 1.0, group_sel)
        cur = jnp.where(hit, neg, cur)
    masked = jnp.concatenate(
        [jnp.where(group_sel[g:g + 1, :] > 0.0, biased[g * epg:(g + 1) * epg, :], neg)
         for g in range(N_EXPERT_GROUPS)], axis=0)
    iota_e = lax.broadcasted_iota(jnp.int32, masked.shape, 0).astype(F32)
    sel = jnp.zeros_like(masked)
    cur = masked
    picks = []
    for _ in range(TOP_K):
        _, idx, hit = pick_first_max(cur, iota_e, N_EXPERTS)
        picks.append(idx)
        sel = jnp.where(hit, 1.0, sel)
        cur = jnp.where(hit, neg, cur)
    w = scores * sel
    return w / jnp.sum(w, axis=0, keepdims=True) * ROUTED_SCALE, sel, picks


def _router_kernel(x_ref, mod_ref, g2_ref, wrt_ref, rb_ref, hp_ref, ek_ref, rk_ref, wt_ref, cnt_ref, run_scr):
    tm = x_ref.shape[0]

    @pl.when(pl.program_id(0) == 0)
    def _():
        run_scr[...] = jnp.zeros_like(run_scr)

    h = _rms_mod(x_ref[...], g2_ref[...], mod_ref[0, 3:4, :], mod_ref[0, 4:5, :])
    half = D_MODEL // 2
    _store_token_words(hp_ref, _pack_pair(h[:, :half], h[:, half:]), tm)

    logits_t = lax.dot_general(wrt_ref[...], h, (((1,), (1,)), ((), ())),
                               precision=lax.Precision.HIGHEST, preferred_element_type=F32)
    scores = jax.nn.sigmoid(logits_t)
    comb_t, sel, picks = _route(scores, scores + rb_ref[...])

    earlier = (lax.broadcasted_iota(jnp.int32, (tm, tm), 0) < lax.broadcasted_iota(jnp.int32, (tm, tm), 1))
    rank_t = _dot(sel.astype(BF16), jnp.where(earlier, 1.0, 0.0).astype(BF16)) + run_scr[...]
    run_scr[...] += jnp.sum(sel, axis=1, keepdims=True)
    cnt_ref[...] = jnp.broadcast_to(run_scr[...], cnt_ref.shape)

    iota_e = lax.broadcasted_iota(jnp.int32, sel.shape, 0).astype(F32)
    ranks, weights = [], []
    for idx in picks:
        hit = iota_e == idx
        ranks.append(jnp.sum(jnp.where(hit, rank_t, 0.0), axis=0, keepdims=True))
        weights.append(jnp.sum(jnp.where(hit, comb_t, 0.0), axis=0, keepdims=True))
    ek_ref[...] = jnp.concatenate(picks, axis=0).astype(jnp.int32)
    rk_ref[...] = jnp.concatenate(ranks, axis=0).astype(jnp.int32)
    w_pad = jnp.concatenate(weights + [jnp.zeros((128 - TOP_K, tm), F32)], axis=0)
    wt_ref[...] = w_pad.T


def _router(x1, mod3, norm2_g, w_router_t, router_bias, seq_len, mod_row_of_batch):
    t = x1.shape[0]
    tm = TM_ROUTER

    def mod_idx(i):
        return (mod_row_of_batch((i * tm) // seq_len), 0, 0)

    def full(a):
        return pl.BlockSpec(a.shape, lambda i: (0,) * a.ndim)

    return pl.pallas_call(
        _router_kernel,
        grid=(t // tm,),
        in_specs=[pl.BlockSpec((tm, D_MODEL), lambda i: (i, 0)),
                  pl.BlockSpec((1, 6, D_MODEL), mod_idx),
                  full(norm2_g), full(w_router_t), full(router_bias)],
        out_specs=[pl.BlockSpec((tm * ROW_SLABS, 128), lambda i: (i, 0)),
                   pl.BlockSpec((TOP_K, tm), lambda i: (0, i)),
                   pl.BlockSpec((TOP_K, tm), lambda i: (0, i)),
                   pl.BlockSpec((tm, 128), lambda i: (i, 0)),
                   pl.BlockSpec((N_EXPERTS, 128), lambda i: (0, 0))],
        out_shape=[jax.ShapeDtypeStruct((t * ROW_SLABS, 128), jnp.int32),
                   jax.ShapeDtypeStruct((TOP_K, t), jnp.int32),
                   jax.ShapeDtypeStruct((TOP_K, t), jnp.int32),
                   jax.ShapeDtypeStruct((t, 128), F32),
                   jax.ShapeDtypeStruct((N_EXPERTS, 128), F32)],
        scratch_shapes=[pltpu.VMEM((N_EXPERTS, 1), F32)],
        compiler_params=pltpu.CompilerParams(dimension_semantics=("arbitrary",),
                                             vmem_limit_bytes=VMEM_LIMIT),
        name="router",
    )(x1, mod3, norm2_g, w_router_t, router_bias)


def _plan_kernel(ek_ref, rk_ref, cnt_ref, pos_ref, texp_ref, nused_ref, tend_ref):
    rows = float(EXPERT_ROWS)
    cnt = cnt_ref[:, 0:1]
    tiles = jnp.floor((cnt + (rows - 1.0)) / rows)
    before = (lax.broadcasted_iota(jnp.int32, (N_EXPERTS, N_EXPERTS), 1)
              < lax.broadcasted_iota(jnp.int32, (N_EXPERTS, N_EXPERTS), 0))
    tile_start = jnp.dot(jnp.where(before, 1.0, 0.0), jnp.broadcast_to(tiles, (N_EXPERTS, 128)),
                         precision=lax.Precision.HIGHEST, preferred_element_type=F32)[:, 0:1]
    tile_end = tile_start + tiles
    row_start = tile_start * rows

    ek = ek_ref[...]
    pos = rk_ref[...].astype(F32)
    tile_id = lax.broadcasted_iota(jnp.int32, texp_ref.shape, 1).astype(F32)
    texp = jnp.zeros(texp_ref.shape, F32)
    for e in range(N_EXPERTS):
        pos = pos + jnp.where(ek == e, row_start[e:e + 1, :], 0.0)
        texp = texp + jnp.where(tile_id >= tile_end[e:e + 1, :], 1.0, 0.0)
    pos_ref[...] = pos.astype(jnp.int32)
    texp_ref[...] = jnp.minimum(texp, N_EXPERTS - 1.0).astype(jnp.int32)
    nused_ref[...] = jnp.broadcast_to(tile_end[N_EXPERTS - 1:N_EXPERTS, :], nused_ref.shape).astype(jnp.int32)
    tend_ref[...] = jnp.broadcast_to(tile_end, tend_ref.shape).astype(jnp.int32)


def _plan(ek, rk, cnt, n_tiles_pad):
    t = ek.shape[1]

    def full(shape):
        return pl.BlockSpec(shape, lambda: (0,) * len(shape))

    return pl.pallas_call(
        _plan_kernel,
        in_specs=[full(ek.shape), full(rk.shape), full(cnt.shape)],
        out_specs=[full((TOP_K, t)), full((1, n_tiles_pad)), full((1, 128)), full((N_EXPERTS, 128))],
        out_shape=[jax.ShapeDtypeStruct((TOP_K, t), jnp.int32),
                   jax.ShapeDtypeStruct((1, n_tiles_pad), jnp.int32),
                   jax.ShapeDtypeStruct((1, 128), jnp.int32),
                   jax.ShapeDtypeStruct((N_EXPERTS, 128), jnp.int32)],
        compiler_params=pltpu.CompilerParams(vmem_limit_bytes=VMEM_LIMIT),
        name="plan",
    )(ek, rk, cnt)


def _sc_mesh():
    return plsc.VectorSubcoreMesh(core_axis_name="c", subcore_axis_name="s")


def _sc_dispatch(rows, pos3, n_out):
    t = rows.shape[0]
    ch = SC_CHUNK
    per_w = (t // ch) // SC_WORKERS

    @functools.partial(
        pl.kernel, out_type=jax.ShapeDtypeStruct((n_out,) + rows.shape[1:], jnp.int32), mesh=_sc_mesh(),
        scratch_types=[pltpu.VMEM((TOP_K, ch), jnp.int32), pltpu.VMEM((ch,) + rows.shape[1:], jnp.int32),
                       pltpu.SemaphoreType.DMA])
    def k(rows_hbm, pos_hbm, out_hbm, idx_v, rows_v, sem):
        wid = lax.axis_index("s") * SC_CORES + lax.axis_index("c")

        @pl.loop(0, per_w)
        def _(j):
            c = wid * per_w + j
            pltpu.sync_copy(pos_hbm.at[c], idx_v)
            pltpu.sync_copy(rows_hbm.at[pl.ds(c * ch, ch)], rows_v)
            copies = [pltpu.async_copy(rows_v, out_hbm.at[idx_v.at[kk]], sem) for kk in range(TOP_K)]
            for cp in copies:
                cp.wait()

    return k(rows, pos3)


def _sc_gather(table, pos3, t):
    ch = SC_CHUNK
    per_w = (t // ch) // SC_WORKERS

    @functools.partial(
        pl.kernel, out_type=jax.ShapeDtypeStruct((TOP_K, t) + table.shape[1:], jnp.int32), mesh=_sc_mesh(),
        scratch_types=[pltpu.VMEM((TOP_K, ch), jnp.int32), pltpu.VMEM((ch,) + table.shape[1:], jnp.int32),
                       pltpu.SemaphoreType.DMA])
    def k(tab_hbm, pos_hbm, out_hbm, idx_v, rows_v, sem):
        wid = lax.axis_index("s") * SC_CORES + lax.axis_index("c")

        @pl.loop(0, per_w)
        def _(j):
            c = wid * per_w + j
            pltpu.sync_copy(pos_hbm.at[c], idx_v)
            for kk in range(TOP_K):
                pltpu.async_copy(tab_hbm.at[idx_v.at[kk]], rows_v, sem).wait()
                pltpu.sync_copy(rows_v, out_hbm.at[kk, pl.ds(c * ch, ch)])

    return k(table, pos3)


def _experts_kernel(texp_ref, nused_ref, tend_ref, xs_ref, weg_hbm, weu_hbm, wed_hbm, ys_ref,
                    wg_scr, wu_scr, wd_scr, wg_buf, wu_buf, wd_buf, sem, group_scr):
    step = pl.program_id(0)
    rows = EXPERT_ROWS
    half = D_MODEL // 2
    n_used = nused_ref[0]

    def weight_copies(e, slot):
        return [pltpu.make_async_copy(weg_hbm.at[e], wg_buf.at[slot], sem.at[slot, 0]),
                pltpu.make_async_copy(weu_hbm.at[e], wu_buf.at[slot], sem.at[slot, 1]),
                pltpu.make_async_copy(wed_hbm.at[e], wd_buf.at[slot], sem.at[slot, 2])]

    @pl.when(step == 0)
    def _():
        group_scr[0] = 0
        for cp in weight_copies(texp_ref[0], 0):
            cp.start()

    def row_tile(tile, x_view, y_view):
        expert = texp_ref[tile]
        used = tile < n_used
        new_expert = (tile == 0) | (expert != texp_ref[jnp.maximum(tile - 1, 0)])

        @pl.when(used & new_expert)
        def _():
            group = group_scr[0]
            slot = group % 2
            next_tile = tend_ref[expert]

            @pl.when(next_tile < n_used)
            def _():
                for cp in weight_copies(texp_ref[next_tile], 1 - slot):
                    cp.start()

            for cp in weight_copies(expert, slot):
                cp.wait()
            wg_scr[...] = wg_buf[slot].astype(BF16)
            wu_scr[...] = wu_buf[slot].astype(BF16)
            wd_scr[...] = wd_buf[slot].astype(BF16)
            group_scr[0] = group + 1

        @pl.when(used)
        def _():
            lo, hi = _unpack_pair(_load_token_words(x_view, (), rows))
            lo = lo.astype(BF16)
            hi = hi.astype(BF16)
            g = _dot(lo, wg_scr[0:half, :]) + _dot(hi, wg_scr[half:D_MODEL, :])
            u = _dot(lo, wu_scr[0:half, :]) + _dot(hi, wu_scr[half:D_MODEL, :])
            y = _dot((_silu(g) * u).astype(BF16), wd_scr[...])
            _store_token_words(y_view, _pack_pair(y[:, :half], y[:, half:]), rows)

        @pl.when(jnp.logical_not(used) & (step == (n_used - 1) // TILES_PER_STEP))
        def _():
            y_view[...] = jnp.zeros_like(y_view)

    for s in range(TILES_PER_STEP):
        view = pl.ds(s * rows * ROW_SLABS, rows * ROW_SLABS)
        row_tile(step * TILES_PER_STEP + s, xs_ref.at[view], ys_ref.at[view])


def _experts(texp, nused, tend, xs2d, weg, weu, wed, n_tiles):
    block = (TILES_PER_STEP * EXPERT_ROWS * ROW_SLABS, 128)
    hbm = pl.BlockSpec(memory_space=pl.ANY)

    def block_idx(j, te, nu, tn):
        return (jnp.minimum(j, (nu[0] - 1) // TILES_PER_STEP), 0)

    grid_spec = pltpu.PrefetchScalarGridSpec(
        num_scalar_prefetch=3,
        grid=(n_tiles // TILES_PER_STEP,),
        in_specs=[pl.BlockSpec(block, block_idx), hbm, hbm, hbm],
        out_specs=pl.BlockSpec(block, block_idx),
        scratch_shapes=[pltpu.VMEM((D_MODEL, EXPERT_DIM), BF16),
                        pltpu.VMEM((D_MODEL, EXPERT_DIM), BF16),
                        pltpu.VMEM((EXPERT_DIM, D_MODEL), BF16),
                        pltpu.VMEM((2, D_MODEL, EXPERT_DIM), F32),
                        pltpu.VMEM((2, D_MODEL, EXPERT_DIM), F32),
                        pltpu.VMEM((2, EXPERT_DIM, D_MODEL), F32),
                        pltpu.SemaphoreType.DMA((2, 3)),
                        pltpu.SMEM((1,), jnp.int32)],
    )
    return pl.pallas_call(
        _experts_kernel,
        grid_spec=grid_spec,
        out_shape=jax.ShapeDtypeStruct(xs2d.shape, jnp.int32),
        compiler_params=pltpu.CompilerParams(dimension_semantics=("arbitrary",),
                                             vmem_limit_bytes=VMEM_LIMIT),
        name="experts",
    )(texp, nused, tend, xs2d, weg, weu, wed)


def _final_kernel(x_ref, hp_ref, y8_ref, wt_ref, mod_ref, wsg_ref, wsu_ref, wsd_ref, fng_ref, o_ref):
    tm = x_ref.shape[0]
    lo, hi = _unpack_pair(_load_token_words(hp_ref, (), tm))
    hb = jnp.concatenate([lo, hi], axis=1).astype(BF16)
    shared = _dot((_silu(_dot(hb, wsg_ref[...])) * _dot(hb, wsu_ref[...])).astype(BF16), wsd_ref[...])
    wt = wt_ref[...]
    r_lo = jnp.zeros((tm, D_MODEL // 2), F32)
    r_hi = jnp.zeros((tm, D_MODEL // 2), F32)
    for k in range(TOP_K):
        lo, hi = _unpack_pair(_load_token_words(y8_ref, (k,), tm))
        wk = wt[:, k:k + 1]
        r_lo = r_lo + wk * lo
        r_hi = r_hi + wk * hi
    routed = jnp.concatenate([r_lo, r_hi], axis=1)
    y = x_ref[...] + mod_ref[0, 5:6, :] * (routed + shared)
    ms = jnp.mean(y * y, axis=-1, keepdims=True)
    o_ref[...] = y * lax.rsqrt(ms + EPS) * fng_ref[...]


def _final(x1, hp2d, y8, wtok, mod3, wsg, wsu, wsd, final_g, seq_len, mod_row_of_batch):
    t = x1.shape[0]
    tm = TM_FINAL

    def mod_idx(i):
        return (mod_row_of_batch((i * tm) // seq_len), 0, 0)

    def full(a):
        return pl.BlockSpec(a.shape, lambda i: (0,) * a.ndim)

    return pl.pallas_call(
        _final_kernel,
        grid=(t // tm,),
        in_specs=[pl.BlockSpec((tm, D_MODEL), lambda i: (i, 0)),
                  pl.BlockSpec((tm * ROW_SLABS, 128), lambda i: (i, 0)),
                  pl.BlockSpec((TOP_K, tm * ROW_SLABS, 128), lambda i: (0, i, 0)),
                  pl.BlockSpec((tm, 128), lambda i: (i, 0)),
                  pl.BlockSpec((1, 6, D_MODEL), mod_idx),
                  full(wsg), full(wsu), full(wsd), full(final_g)],
        out_specs=pl.BlockSpec((tm, D_MODEL), lambda i: (i, 0)),
        out_shape=jax.ShapeDtypeStruct((t, D_MODEL), F32),
        compiler_params=pltpu.CompilerParams(dimension_semantics=("parallel",),
                                             vmem_limit_bytes=VMEM_LIMIT),
        name="final",
    )(x1, hp2d, y8, wtok, mod3, wsg, wsu, wsd, final_g)


def _moe(x1, mod3, lw, seq_len, mod_row_of_batch):
    t = x1.shape[0]
    n_tiles = TOP_K * t // EXPERT_ROWS + N_EXPERTS
    n_tiles_pad = -(-n_tiles // 128) * 128
    hp2d, ek, rk, wtok, cnt = _router(x1, mod3, lw["norm2_g"], lw["w_router_t"], lw["router_bias"],
                                      seq_len, mod_row_of_batch)
    pos, texp, nused, tend = _plan(ek, rk, cnt, n_tiles_pad)
    pos3 = pos.reshape(TOP_K, t // SC_CHUNK, SC_CHUNK).transpose(1, 0, 2)
    xs = _sc_dispatch(hp2d.reshape(t, ROW_SLABS, 128), pos3, n_tiles * EXPERT_ROWS)
    ys2d = _experts(texp.reshape(-1), nused.reshape(-1), tend[:, 0], xs.reshape(-1, 128),
                    lw["weg"], lw["weu"], lw["wed"], n_tiles)
    y8 = _sc_gather(ys2d.reshape(-1, ROW_SLABS, 128), pos3, t)
    return _final(x1, hp2d, y8.reshape(TOP_K, t * ROW_SLABS, 128), wtok, mod3,
                  lw["wsg"], lw["wsu"], lw["wsd"], lw["final_g"], seq_len, mod_row_of_batch)


def _dft_tables(seq_len):
    gd = FOURIER_GROUP_DIM
    kc = np.arange(gd)
    ang_c = ((kc[:, None] * kc[None, :]) % gd) * (2.0 * math.pi / gd)
    cs = np.concatenate([np.cos(ang_c), np.sin(ang_c)], axis=1) * (gd ** -0.5)
    kl = np.arange(seq_len)
    ang_l = ((kl[:, None] * kl[None, :]) % seq_len) * (2.0 * math.pi / seq_len)
    cls = np.concatenate([np.cos(ang_l), -np.sin(ang_l)], axis=1) * (seq_len ** -0.5)
    return jnp.asarray(cs.astype(np.float32), dtype=BF16), jnp.asarray(cls.astype(np.float32), dtype=BF16)


def _rope_tables(length):
    rows = length // GRID_W
    r = np.repeat(np.arange(rows, dtype=np.float32), GRID_W)
    col = np.tile(np.arange(GRID_W, dtype=np.float32), rows)
    nf = RET_HEAD_DIM // 4
    inv = (np.float32(ROPE_BASE) ** (-np.arange(nf, dtype=np.float32) / np.float32(nf))).astype(np.float32)
    ar = r[:, None] * inv[None]
    ac = col[:, None] * inv[None]
    ang = np.concatenate([ar, ar, ac, ac], axis=-1).astype(np.float64)
    sign = np.where((np.arange(RET_HEAD_DIM) & nf) == 0, -1.0, 1.0)
    return (jnp.asarray(np.cos(ang).astype(np.float32)),
            jnp.asarray((np.sin(ang) * sign[None, :]).astype(np.float32)))


def _trunk_path(x, mod3, mod_row_of_batch, s0f, s0b, rope, lw):
    batch, seq_len, _ = x.shape
    x2d = x.reshape(batch * seq_len, D_MODEL)
    uf, q, k, v, sg, gf, gr = _inproj(x2d, mod3, lw["norm1_g"], lw["w_in"], seq_len, mod_row_of_batch, rope)
    r, s_f, s_b = _retention(q, k, v, sg, lw["dec"], lw["gn_g"], s0f, s0b, batch, seq_len)
    cs, cls = _dft_tables(seq_len)
    fmix = _fnet(uf, cs, cls, batch, seq_len)
    x1 = _merge(fmix, r, gf, gr, x2d, mod3, lw["w_four"], lw["w_ret"], lw["w_o"], seq_len, mod_row_of_batch)
    y = _moe(x1, mod3, lw, seq_len, mod_row_of_batch)
    return y.reshape(batch, seq_len, D_MODEL), s_f, s_b


def kernel(x_prompt, x_sample, state_ret_fwd, state_ret_bwd, c, c_ctx, w_ada, b_ada, norm1_g, norm2_g, w_in,
           ret_decay_fwd, ret_decay_bwd, ret_gn_g, w_four_out, w_ret_out, w_out, w_router, router_bias,
           w_exp_gate, w_exp_up, w_exp_down, w_shared_gate, w_shared_up, w_shared_down, final_norm_g):
    depth = w_ada.shape[0]
    assert depth == 1, "final norm is fused into the last layer's MoE kernel"
    n_ctx, n_lat = x_prompt.shape[0], x_sample.shape[0]
    cond = jnp.concatenate([c_ctx[None, :], c], axis=0)
    cond = jnp.pad(cond, ((0, (-cond.shape[0]) % 8), (0, 0)))
    rope = _rope_tables(x_sample.shape[1])
    zeros = jnp.zeros((n_ctx, N_RET_HEADS, RET_HEAD_DIM, RET_HEAD_DIM), F32)

    layer = 0
    mod = _ada(cond, w_ada[layer], b_ada[layer][None, :])
    mod3 = mod.reshape(mod.shape[0], 6, D_MODEL)
    dec = jnp.stack([ret_decay_fwd[layer], ret_decay_bwd[layer]], axis=1)
    lw = {
        "norm1_g": norm1_g[layer][None, :],
        "norm2_g": norm2_g[layer][None, :],
        "w_in": w_in[layer].astype(BF16),
        "dec": jnp.broadcast_to(dec[:, :, None], (N_RET_HEADS, 2, RET_HEAD_DIM)).astype(F32),
        "gn_g": ret_gn_g[layer][None, :],
        "w_four": w_four_out[layer].astype(BF16),
        "w_ret": w_ret_out[layer].astype(BF16),
        "w_o": w_out[layer].astype(BF16),
        "w_router_t": w_router[layer].T,
        "router_bias": router_bias[layer][:, None],
        "weg": w_exp_gate[layer],
        "weu": w_exp_up[layer],
        "wed": w_exp_down[layer],
        "wsg": w_shared_gate[layer].astype(BF16),
        "wsu": w_shared_up[layer].astype(BF16),
        "wsd": w_shared_down[layer].astype(BF16),
        "final_g": final_norm_g[None, :],
    }
    y_prompt, s_f, s_b = _trunk_path(x_prompt, mod3, lambda b: 0, zeros, zeros, None, lw)
    y_sample, _, _ = _trunk_path(x_sample, mod3, lambda b: 1 + b, state_ret_fwd[:, layer],
                                 state_ret_bwd[:, layer], rope, lw)
    return (y_prompt, y_sample, s_f[:, None], s_b[:, None])
```

```python
import functools
import math

import jax
import jax.numpy as jnp
import numpy as np
from jax import lax
from jax.experimental import pallas as pl
from jax.experimental.pallas import tpu as pltpu
from jax.experimental.pallas import tpu_sc as plsc

F32 = jnp.float32
BF16 = jnp.bfloat16

D_MODEL = 1024
GRID_W = 64
N_FOURIER_GROUPS = 8
FOURIER_GROUP_DIM = 128
N_RET_HEADS = 4
RET_HEAD_DIM = 128
RET_WIDTH = N_RET_HEADS * RET_HEAD_DIM
CHUNK = 128
N_EXPERTS = 64
N_EXPERT_GROUPS = 8
EXPERTS_PER_GROUP = N_EXPERTS // N_EXPERT_GROUPS
TOPK_GROUPS = 4
TOP_K = 8
EXPERT_DIM = 256
ROUTED_SCALE = 2.5
ROPE_BASE = 10000.0
EPS = 1e-6
Q_SCALE = RET_HEAD_DIM ** -0.5

_C_UF = (0, 1024)
_C_Q = (1024, 1536)
_C_K = (1536, 2048)
_C_V = (2048, 2560)
_C_G = (2560, 3072)
_C_GF = (3072, 4096)
_C_GR = (4096, 5120)

VMEM_LIMIT = 56 * 1024 * 1024

TM_PROJ = 512
FNET_ROWS = 256
TM_ROUTER = 1024
TM_FINAL = 512
EXPERT_ROWS = 256
TILES_PER_STEP = 8
ROW_SLABS = 4
SC_CORES = 2
SC_WORKERS = 32
SC_CHUNK = 128


def _silu(x):
    return x * jax.nn.sigmoid(x)


def _dot(a, b):
    return jnp.dot(a, b, preferred_element_type=F32)


def _rms_mod(x, g, shift, scale):
    ms = jnp.mean(x * x, axis=-1, keepdims=True)
    y = x * lax.rsqrt(ms + EPS) * g
    return y * (1.0 + scale) + shift


def _ada_kernel(cond_ref, w_ref, b_ref, o_ref):
    s = _silu(cond_ref[...]).astype(BF16)
    o_ref[...] = _dot(s, w_ref[...].astype(BF16)) + b_ref[...]


def _ada(cond, w_ada, b_ada):
    rows, n = cond.shape[0], w_ada.shape[1]
    tn = 1536
    return pl.pallas_call(
        _ada_kernel,
        grid=(n // tn,),
        in_specs=[pl.BlockSpec((rows, D_MODEL), lambda j: (0, 0)),
                  pl.BlockSpec((D_MODEL, tn), lambda j: (0, j)),
                  pl.BlockSpec((1, tn), lambda j: (0, j))],
        out_specs=pl.BlockSpec((rows, tn), lambda j: (0, j)),
        out_shape=jax.ShapeDtypeStruct((rows, n), F32),
        compiler_params=pltpu.CompilerParams(vmem_limit_bytes=VMEM_LIMIT),
        name="ada",
    )(cond, w_ada, b_ada)


def _rope_head(x, cos, sin_signed, first_half):
    partner = jnp.where(first_half, pltpu.roll(x, 96, 1), pltpu.roll(x, 32, 1))
    return x * cos + partner * sin_signed


def _inproj_kernel(*refs, use_rope):
    if use_rope:
        x_ref, mod_ref, g_ref, w_ref, cos_ref, sin_ref = refs[:6]
        outs = refs[6:]
    else:
        x_ref, mod_ref, g_ref, w_ref = refs[:4]
        outs = refs[4:]
    uf_o, q_o, k_o, v_o, sg_o, gf_o, gr_o = outs

    h = _rms_mod(x_ref[...], g_ref[...], mod_ref[0, 0:1, :], mod_ref[0, 1:2, :])
    hb = h.astype(BF16)

    def proj(cols):
        return _dot(hb, w_ref[:, cols[0]:cols[1]])

    uf_o[...] = proj(_C_UF).astype(BF16)
    q = proj(_C_Q)
    k = proj(_C_K)
    if use_rope:
        cos = cos_ref[...]
        sin_signed = sin_ref[...]
        lane = lax.broadcasted_iota(jnp.int32, cos.shape, 1)
        first_half = (lane & 32) == 0
        for hd in range(N_RET_HEADS):
            sl = slice(hd * RET_HEAD_DIM, (hd + 1) * RET_HEAD_DIM)
            q_o[:, sl] = (_rope_head(q[:, sl], cos, sin_signed, first_half) * Q_SCALE).astype(BF16)
            k_o[:, sl] = _rope_head(k[:, sl], cos, sin_signed, first_half).astype(BF16)
    else:
        q_o[...] = (q * Q_SCALE).astype(BF16)
        k_o[...] = k.astype(BF16)
    v_o[...] = proj(_C_V).astype(BF16)
    sg_o[...] = _silu(proj(_C_G)).astype(BF16)
    gf_o[...] = jax.nn.sigmoid(proj(_C_GF)).astype(BF16)
    gr_o[...] = jax.nn.sigmoid(proj(_C_GR)).astype(BF16)


def _inproj(x2d, mod3, norm_g, w_in_bf, seq_len, mod_row_of_batch, rope):
    t = x2d.shape[0]
    tm = TM_PROJ
    tiles_per_seq = max(seq_len // tm, 1)

    def mod_idx(i):
        return (mod_row_of_batch((i * tm) // seq_len), 0, 0)

    in_specs = [pl.BlockSpec((tm, D_MODEL), lambda i: (i, 0)),
                pl.BlockSpec((1, 6, D_MODEL), mod_idx),
                pl.BlockSpec((1, D_MODEL), lambda i: (0, 0)),
                pl.BlockSpec(w_in_bf.shape, lambda i: (0, 0))]
    args = [x2d, mod3, norm_g, w_in_bf]
    if rope is not None:
        in_specs += [pl.BlockSpec((tm, RET_HEAD_DIM), lambda i: (i % tiles_per_seq, 0))] * 2
        args += list(rope)
    widths = [1024, RET_WIDTH, RET_WIDTH, RET_WIDTH, RET_WIDTH, 1024, 1024]
    return pl.pallas_call(
        functools.partial(_inproj_kernel, use_rope=rope is not None),
        grid=(t // tm,),
        in_specs=in_specs,
        out_specs=[pl.BlockSpec((tm, w), lambda i: (i, 0)) for w in widths],
        out_shape=[jax.ShapeDtypeStruct((t, w), BF16) for w in widths],
        compiler_params=pltpu.CompilerParams(dimension_semantics=("parallel",),
                                             vmem_limit_bytes=VMEM_LIMIT),
        name="inproj",
    )(*args)


def _retention_kernel(q_ref, k_ref, v_ref, sg_ref, dec_ref, gn_ref, s0f_ref, s0b_ref,
                      r_ref, sfo_ref, sbo_ref):
    n_chunks = q_ref.shape[0] // CHUNK
    dec = dec_ref[...]
    lg = jnp.minimum(dec, 0.0) - jnp.log1p(jnp.exp(-jnp.abs(dec)))
    lgf = lg[0:1, :]
    lgb = lg[1:2, :]
    row = lax.broadcasted_iota(jnp.int32, (CHUNK, CHUNK), 0).astype(F32)
    col = lax.broadcasted_iota(jnp.int32, (CHUNK, CHUNK), 1).astype(F32)
    diff = row - col
    decay = jnp.exp(jnp.where(diff >= 0, lgf * diff, lgb * (-diff)))
    qw_f = jnp.exp(lgf * (row + 1.0))
    qw_b = jnp.exp(lgb * (CHUNK - row))
    kw_f = jnp.exp(lgf * (CHUNK - 1.0 - row))
    kw_b = jnp.exp(lgb * row)
    gc_f = jnp.exp(lgf * CHUNK)
    gc_b = jnp.exp(lgb * CHUNK)

    def rows(n):
        return slice(n * CHUNK, (n + 1) * CHUNK)

    kv_f, kv_b = [], []
    for n in range(n_chunks):
        kn = k_ref[rows(n), :].astype(F32)
        vn = v_ref[rows(n), :]
        kv_f.append(_dot((kn * kw_f).T.astype(BF16), vn))
        kv_b.append(_dot((kn * kw_b).T.astype(BF16), vn))

    s = s0f_ref[...]
    prev_f = []
    for n in range(n_chunks):
        prev_f.append(s.astype(BF16))
        s = gc_f * s + kv_f[n]
    sfo_ref[...] = s
    s = s0b_ref[...]
    prev_b = [None] * n_chunks
    for n in reversed(range(n_chunks)):
        prev_b[n] = s.astype(BF16)
        s = gc_b * s + kv_b[n]
    sbo_ref[...] = s

    gn = gn_ref[...]
    for n in range(n_chunks):
        qn = q_ref[rows(n), :]
        qf = qn.astype(F32)
        scores = lax.dot_general(qn, k_ref[rows(n), :], (((1,), (1,)), ((), ())),
                                 preferred_element_type=F32)
        o = _dot((scores * decay).astype(BF16), v_ref[rows(n), :])
        o = o + _dot((qf * qw_f).astype(BF16), prev_f[n])
        o = o + _dot((qf * qw_b).astype(BF16), prev_b[n])
        mu = jnp.mean(o, axis=-1, keepdims=True)
        d = o - mu
        var = jnp.mean(d * d, axis=-1, keepdims=True)
        on = d * lax.rsqrt(var + EPS) * gn
        r_ref[rows(n), :] = (on * sg_ref[rows(n), :].astype(F32)).astype(BF16)


def _retention(q, k, v, sg, dec, gn_g, s0f, s0b, batch, seq_len):
    hd = RET_HEAD_DIM
    tok_spec = pl.BlockSpec((seq_len, hd), lambda b, h: (b, h))
    st_spec = pl.BlockSpec((None, None, hd, hd), lambda b, h: (b, h, 0, 0))
    st_shape = jax.ShapeDtypeStruct((batch, N_RET_HEADS, hd, hd), F32)
    return pl.pallas_call(
        _retention_kernel,
        grid=(batch, N_RET_HEADS),
        in_specs=[tok_spec, tok_spec, tok_spec, tok_spec,
                  pl.BlockSpec((None, 2, hd), lambda b, h: (h, 0, 0)),
                  pl.BlockSpec((1, hd), lambda b, h: (0, h)),
                  st_spec, st_spec],
        out_specs=[tok_spec, st_spec, st_spec],
        out_shape=[jax.ShapeDtypeStruct((batch * seq_len, RET_WIDTH), BF16), st_shape, st_shape],
        compiler_params=pltpu.CompilerParams(dimension_semantics=("parallel", "parallel"),
                                             vmem_limit_bytes=VMEM_LIMIT),
        name="retention",
    )(q, k, v, sg, dec, gn_g, s0f, s0b)


def _fnet_kernel(uf_ref, cs_ref, cls_ref, o_ref, xcs_ref):
    seq_len = uf_ref.shape[0]
    gd = FOURIER_GROUP_DIM

    @pl.when(pl.program_id(1) == 0)
    def _():
        for g in range(N_FOURIER_GROUPS):
            x = _dot(uf_ref[:, g * gd:(g + 1) * gd], cs_ref[...])
            xcs_ref[0:seq_len, g * gd:(g + 1) * gd] = x[:, :gd].astype(BF16)
            xcs_ref[seq_len:2 * seq_len, g * gd:(g + 1) * gd] = x[:, gd:].astype(BF16)

    o_ref[...] = _dot(cls_ref[...], xcs_ref[...]).astype(BF16)


def _fnet(uf, cs, cls, batch, seq_len):
    rb = FNET_ROWS
    nr = seq_len // rb
    return pl.pallas_call(
        _fnet_kernel,
        grid=(batch, nr),
        in_specs=[pl.BlockSpec((seq_len, D_MODEL), lambda b, r: (b, 0)),
                  pl.BlockSpec(cs.shape, lambda b, r: (0, 0)),
                  pl.BlockSpec((rb, 2 * seq_len), lambda b, r: (r, 0))],
        out_specs=pl.BlockSpec((rb, D_MODEL), lambda b, r: (b * nr + r, 0)),
        out_shape=jax.ShapeDtypeStruct((batch * seq_len, D_MODEL), BF16),
        scratch_shapes=[pltpu.VMEM((2 * seq_len, D_MODEL), BF16)],
        compiler_params=pltpu.CompilerParams(dimension_semantics=("parallel", "arbitrary"),
                                             vmem_limit_bytes=VMEM_LIMIT),
        name="fnet",
    )(uf, cs, cls)


def _merge_kernel(fm_ref, r_ref, gf_ref, gr_ref, x_ref, mod_ref, wf_ref, wr_ref, wo_ref, o_ref):
    f_out = _dot(fm_ref[...], wf_ref[...])
    r_out = _dot(r_ref[...], wr_ref[...])
    merged = gf_ref[...].astype(F32) * f_out + gr_ref[...].astype(F32) * r_out
    mix = _dot(merged.astype(BF16), wo_ref[...])
    o_ref[...] = x_ref[...] + mod_ref[0, 2:3, :] * mix


def _merge(fmix, r, gf, gr, x2d, mod3, w_four, w_ret, w_o, seq_len, mod_row_of_batch):
    t = x2d.shape[0]
    tm = TM_PROJ

    def mod_idx(i):
        return (mod_row_of_batch((i * tm) // seq_len), 0, 0)

    def tok(w):
        return pl.BlockSpec((tm, w), lambda i: (i, 0))

    def full(a):
        return pl.BlockSpec(a.shape, lambda i: (0, 0))

    return pl.pallas_call(
        _merge_kernel,
        grid=(t // tm,),
        in_specs=[tok(D_MODEL), tok(RET_WIDTH), tok(D_MODEL), tok(D_MODEL), tok(D_MODEL),
                  pl.BlockSpec((1, 6, D_MODEL), mod_idx), full(w_four), full(w_ret), full(w_o)],
        out_specs=tok(D_MODEL),
        out_shape=jax.ShapeDtypeStruct((t, D_MODEL), F32),
        compiler_params=pltpu.CompilerParams(dimension_semantics=("parallel",),
                                             vmem_limit_bytes=VMEM_LIMIT),
        name="merge",
    )(fmix, r, gf, gr, x2d, mod3, w_four, w_ret, w_o)


def _pack_pair(lo_f32, hi_f32):
    lo = lax.bitcast_convert_type(lo_f32.astype(BF16).astype(F32), jnp.uint32)
    hi = lax.bitcast_convert_type(hi_f32.astype(BF16).astype(F32), jnp.uint32)
    return lax.bitcast_convert_type((lo >> 16) | hi, jnp.int32)


def _unpack_pair(words_i32):
    w = lax.bitcast_convert_type(words_i32, jnp.uint32)
    lo = lax.bitcast_convert_type(w << 16, F32)
    hi = lax.bitcast_convert_type(w & jnp.uint32(0xFFFF0000), F32)
    return lo, hi


def _load_token_words(ref, lead, n_tok):
    parts = []
    for s in range(ROW_SLABS):
        idx = (pl.ds(s, n_tok, stride=ROW_SLABS), slice(None))
        parts.append(ref[lead + idx] if lead else ref[idx])
    return jnp.concatenate(parts, axis=1)


def _store_token_words(ref, words, n_tok):
    for s in range(ROW_SLABS):
        ref[pl.ds(s, n_tok, stride=ROW_SLABS), :] = words[:, s * 128:(s + 1) * 128]


def _route(scores, biased):
    tokens = scores.shape[1]
    neg = -jnp.inf
    epg = EXPERTS_PER_GROUP
    iota_g = lax.broadcasted_iota(jnp.int32, (epg, tokens), 0).astype(F32)

    def pick_first_max(cur, iota, size):
        m = jnp.max(cur, axis=0, keepdims=True)
        idx = jnp.min(jnp.where(cur == m, iota, float(size)), axis=0, keepdims=True)
        return m, idx, iota == idx

    group_scores = []
    for g in range(N_EXPERT_GROUPS):
        vals = biased[g * epg:(g + 1) * epg, :]
        m1, _, hit = pick_first_max(vals, iota_g, epg)
        m2 = jnp.max(jnp.where(hit, neg, vals), axis=0, keepdims=True)
        group_scores.append(m1 + m2)
    cur = jnp.concatenate(group_scores, axis=0)
    group_sel = jnp.zeros_like(cur)
    for _ in range(TOPK_GROUPS):
        _, _, hit = pick_first_max(cur, iota_g, N_EXPERT_GROUPS)
        group_sel = jnp.where(hit, 1.0, group_sel)
        cur = jnp.where(hit, neg, cur)
    masked = jnp.concatenate(
        [jnp.where(group_sel[g:g + 1, :] > 0.0, biased[g * epg:(g + 1) * epg, :], neg)
         for g in range(N_EXPERT_GROUPS)], axis=0)
    iota_e = lax.broadcasted_iota(jnp.int32, masked.shape, 0).astype(F32)
    sel = jnp.zeros_like(masked)
    cur = masked
    picks = []
    for _ in range(TOP_K):
        _, idx, hit = pick_first_max(cur, iota_e, N_EXPERTS)
        picks.append(idx)
        sel = jnp.where(hit, 1.0, sel)
        cur = jnp.where(hit, neg, cur)
    w = scores * sel
    return w / jnp.sum(w, axis=0, keepdims=True) * ROUTED_SCALE, sel, picks


def _router_kernel(x_ref, mod_ref, g2_ref, wrt_ref, rb_ref, hp_ref, ek_ref, rk_ref, wt_ref, cnt_ref, run_scr):
    tm = x_ref.shape[0]

    @pl.when(pl.program_id(0) == 0)
    def _():
        run_scr[...] = jnp.zeros_like(run_scr)

    h = _rms_mod(x_ref[...], g2_ref[...], mod_ref[0, 3:4, :], mod_ref[0, 4:5, :])
    half = D_MODEL // 2
    _store_token_words(hp_ref, _pack_pair(h[:, :half], h[:, half:]), tm)

    logits_t = lax.dot_general(wrt_ref[...], h, (((1,), (1,)), ((), ())),
                               precision=lax.Precision.HIGHEST, preferred_element_type=F32)
    scores = jax.nn.sigmoid(logits_t)
    comb_t, sel, picks = _route(scores, scores + rb_ref[...])

    earlier = (lax.broadcasted_iota(jnp.int32, (tm, tm), 0) < lax.broadcasted_iota(jnp.int32, (tm, tm), 1))
    rank_t = _dot(sel.astype(BF16), jnp.where(earlier, 1.0, 0.0).astype(BF16)) + run_scr[...]
    run_scr[...] += jnp.sum(sel, axis=1, keepdims=True)
    cnt_ref[...] = jnp.broadcast_to(run_scr[...], cnt_ref.shape)

    iota_e = lax.broadcasted_iota(jnp.int32, sel.shape, 0).astype(F32)
    ranks, weights = [], []
    for idx in picks:
        hit = iota_e == idx
        ranks.append(jnp.sum(jnp.where(hit, rank_t, 0.0), axis=0, keepdims=True))
        weights.append(jnp.sum(jnp.where(hit, comb_t, 0.0), axis=0, keepdims=True))
    ek_ref[...] = jnp.concatenate(picks, axis=0).astype(jnp.int32)
    rk_ref[...] = jnp.concatenate(ranks, axis=0).astype(jnp.int32)
    w_pad = jnp.concatenate(weights + [jnp.zeros((128 - TOP_K, tm), F32)], axis=0)
    wt_ref[...] = w_pad.T


def _router(x1, mod3, norm2_g, w_router_t, router_bias, seq_len, mod_row_of_batch):
    t = x1.shape[0]
    tm = TM_ROUTER

    def mod_idx(i):
        return (mod_row_of_batch((i * tm) // seq_len), 0, 0)

    def full(a):
        return pl.BlockSpec(a.shape, lambda i: (0,) * a.ndim)

    return pl.pallas_call(
        _router_kernel,
        grid=(t // tm,),
        in_specs=[pl.BlockSpec((tm, D_MODEL), lambda i: (i, 0)),
                  pl.BlockSpec((1, 6, D_MODEL), mod_idx),
                  full(norm2_g), full(w_router_t), full(router_bias)],
        out_specs=[pl.BlockSpec((tm * ROW_SLABS, 128), lambda i: (i, 0)),
                   pl.BlockSpec((TOP_K, tm), lambda i: (0, i)),
                   pl.BlockSpec((TOP_K, tm), lambda i: (0, i)),
                   pl.BlockSpec((tm, 128), lambda i: (i, 0)),
                   pl.BlockSpec((N_EXPERTS, 128), lambda i: (0, 0))],
        out_shape=[jax.ShapeDtypeStruct((t * ROW_SLABS, 128), jnp.int32),
                   jax.ShapeDtypeStruct((TOP_K, t), jnp.int32),
                   jax.ShapeDtypeStruct((TOP_K, t), jnp.int32),
                   jax.ShapeDtypeStruct((t, 128), F32),
                   jax.ShapeDtypeStruct((N_EXPERTS, 128), F32)],
        scratch_shapes=[pltpu.VMEM((N_EXPERTS, 1), F32)],
        compiler_params=pltpu.CompilerParams(dimension_semantics=("arbitrary",),
                                             vmem_limit_bytes=VMEM_LIMIT),
        name="router",
    )(x1, mod3, norm2_g, w_router_t, router_bias)


def _plan_kernel(ek_ref, rk_ref, cnt_ref, pos_ref, texp_ref, nused_ref, tend_ref):
    rows = float(EXPERT_ROWS)
    cnt = cnt_ref[:, 0:1]
    tiles = jnp.floor((cnt + (rows - 1.0)) / rows)
    before = (lax.broadcasted_iota(jnp.int32, (N_EXPERTS, N_EXPERTS), 1)
              < lax.broadcasted_iota(jnp.int32, (N_EXPERTS, N_EXPERTS), 0))
    tile_start = jnp.dot(jnp.where(before, 1.0, 0.0), jnp.broadcast_to(tiles, (N_EXPERTS, 128)),
                         precision=lax.Precision.HIGHEST, preferred_element_type=F32)[:, 0:1]
    tile_end = tile_start + tiles
    row_start = tile_start * rows

    ek = ek_ref[...]
    pos = rk_ref[...].astype(F32)
    tile_id = lax.broadcasted_iota(jnp.int32, texp_ref.shape, 1).astype(F32)
    texp = jnp.zeros(texp_ref.shape, F32)
    for e in range(N_EXPERTS):
        pos = pos + jnp.where(ek == e, row_start[e:e + 1, :], 0.0)
        texp = texp + jnp.where(tile_id >= tile_end[e:e + 1, :], 1.0, 0.0)
    pos_ref[...] = pos.astype(jnp.int32)
    texp_ref[...] = jnp.minimum(texp, N_EXPERTS - 1.0).astype(jnp.int32)
    nused_ref[...] = jnp.broadcast_to(tile_end[N_EXPERTS - 1:N_EXPERTS, :], nused_ref.shape).astype(jnp.int32)
    tend_ref[...] = jnp.broadcast_to(tile_end, tend_ref.shape).astype(jnp.int32)


def _plan(ek, rk, cnt, n_tiles_pad):
    t = ek.shape[1]

    def full(shape):
        return pl.BlockSpec(shape, lambda: (0,) * len(shape))

    return pl.pallas_call(
        _plan_kernel,
        in_specs=[full(ek.shape), full(rk.shape), full(cnt.shape)],
        out_specs=[full((TOP_K, t)), full((1, n_tiles_pad)), full((1, 128)), full((N_EXPERTS, 128))],
        out_shape=[jax.ShapeDtypeStruct((TOP_K, t), jnp.int32),
                   jax.ShapeDtypeStruct((1, n_tiles_pad), jnp.int32),
                   jax.ShapeDtypeStruct((1, 128), jnp.int32),
                   jax.ShapeDtypeStruct((N_EXPERTS, 128), jnp.int32)],
        compiler_params=pltpu.CompilerParams(vmem_limit_bytes=VMEM_LIMIT),
        name="plan",
    )(ek, rk, cnt)


def _sc_mesh():
    return plsc.VectorSubcoreMesh(core_axis_name="c", subcore_axis_name="s")


def _sc_dispatch(rows, pos3, n_out):
    t = rows.shape[0]
    ch = SC_CHUNK
    per_w = (t // ch) // SC_WORKERS

    @functools.partial(
        pl.kernel, out_type=jax.ShapeDtypeStruct((n_out,) + rows.shape[1:], jnp.int32), mesh=_sc_mesh(),
        scratch_types=[pltpu.VMEM((TOP_K, ch), jnp.int32), pltpu.VMEM((ch,) + rows.shape[1:], jnp.int32),
                       pltpu.SemaphoreType.DMA])
    def k(rows_hbm, pos_hbm, out_hbm, idx_v, rows_v, sem):
        wid = lax.axis_index("s") * SC_CORES + lax.axis_index("c")

        @pl.loop(0, per_w)
        def _(j):
            c = wid * per_w + j
            pltpu.sync_copy(pos_hbm.at[c], idx_v)
            pltpu.sync_copy(rows_hbm.at[pl.ds(c * ch, ch)], rows_v)
            copies = [pltpu.async_copy(rows_v, out_hbm.at[idx_v.at[kk]], sem) for kk in range(TOP_K)]
            for cp in copies:
                cp.wait()

    return k(rows, pos3)


def _sc_gather(table, pos3, t):
    ch = SC_CHUNK
    per_w = (t // ch) // SC_WORKERS

    @functools.partial(
        pl.kernel, out_type=jax.ShapeDtypeStruct((TOP_K, t) + table.shape[1:], jnp.int32), mesh=_sc_mesh(),
        scratch_types=[pltpu.VMEM((TOP_K, ch), jnp.int32), pltpu.VMEM((ch,) + table.shape[1:], jnp.int32),
                       pltpu.SemaphoreType.DMA])
    def k(tab_hbm, pos_hbm, out_hbm, idx_v, rows_v, sem):
        wid = lax.axis_index("s") * SC_CORES + lax.axis_index("c")

        @pl.loop(0, per_w)
        def _(j):
            c = wid * per_w + j
            pltpu.sync_copy(pos_hbm.at[c], idx_v)
            for kk in range(TOP_K):
                pltpu.async_copy(tab_hbm.at[idx_v.at[kk]], rows_v, sem).wait()
                pltpu.sync_copy(rows_v, out_hbm.at[kk, pl.ds(c * ch, ch)])

    return k(table, pos3)


def _experts_kernel(texp_ref, nused_ref, tend_ref, xs_ref, weg_hbm, weu_hbm, wed_hbm, ys_ref,
                    wg_scr, wu_scr, wd_scr, wg_buf, wu_buf, wd_buf, sem, group_scr):
    step = pl.program_id(0)
    rows = EXPERT_ROWS
    half = D_MODEL // 2
    n_used = nused_ref[0]

    def weight_copies(e, slot):
        return [pltpu.make_async_copy(weg_hbm.at[e], wg_buf.at[slot], sem.at[slot, 0]),
                pltpu.make_async_copy(weu_hbm.at[e], wu_buf.at[slot], sem.at[slot, 1]),
                pltpu.make_async_copy(wed_hbm.at[e], wd_buf.at[slot], sem.at[slot, 2])]

    @pl.when(step == 0)
    def _():
        group_scr[0] = 0
        for cp in weight_copies(texp_ref[0], 0):
            cp.start()

    def row_tile(tile, x_view, y_view):
        expert = texp_ref[tile]
        used = tile < n_used
        new_expert = (tile == 0) | (expert != texp_ref[jnp.maximum(tile - 1, 0)])

        @pl.when(used & new_expert)
        def _():
            group = group_scr[0]
            slot = group % 2
            next_tile = tend_ref[expert]

            @pl.when(next_tile < n_used)
            def _():
                for cp in weight_copies(texp_ref[next_tile], 1 - slot):
                    cp.start()

            for cp in weight_copies(expert, slot):
                cp.wait()
            wg_scr[...] = wg_buf[slot].astype(BF16)
            wu_scr[...] = wu_buf[slot].astype(BF16)
            wd_scr[...] = wd_buf[slot].astype(BF16)
            group_scr[0] = group + 1

        @pl.when(used)
        def _():
            lo, hi = _unpack_pair(_load_token_words(x_view, (), rows))
            lo = lo.astype(BF16)
            hi = hi.astype(BF16)
            g = _dot(lo, wg_scr[0:half, :]) + _dot(hi, wg_scr[half:D_MODEL, :])
            u = _dot(lo, wu_scr[0:half, :]) + _dot(hi, wu_scr[half:D_MODEL, :])
            y = _dot((_silu(g) * u).astype(BF16), wd_scr[...])
            _store_token_words(y_view, _pack_pair(y[:, :half], y[:, half:]), rows)

        @pl.when(jnp.logical_not(used) & (step == (n_used - 1) // TILES_PER_STEP))
        def _():
            y_view[...] = jnp.zeros_like(y_view)

    for s in range(TILES_PER_STEP):
        view = pl.ds(s * rows * ROW_SLABS, rows * ROW_SLABS)
        row_tile(step * TILES_PER_STEP + s, xs_ref.at[view], ys_ref.at[view])


def _experts(texp, nused, tend, xs2d, weg, weu, wed, n_tiles):
    block = (TILES_PER_STEP * EXPERT_ROWS * ROW_SLABS, 128)
    hbm = pl.BlockSpec(memory_space=pl.ANY)

    def block_idx(j, te, nu, tn):
        return (jnp.minimum(j, (nu[0] - 1) // TILES_PER_STEP), 0)

    grid_spec = pltpu.PrefetchScalarGridSpec(
        num_scalar_prefetch=3,
        grid=(n_tiles // TILES_PER_STEP,),
        in_specs=[pl.BlockSpec(block, block_idx), hbm, hbm, hbm],
        out_specs=pl.BlockSpec(block, block_idx),
        scratch_shapes=[pltpu.VMEM((D_MODEL, EXPERT_DIM), BF16),
                        pltpu.VMEM((D_MODEL, EXPERT_DIM), BF16),
                        pltpu.VMEM((EXPERT_DIM, D_MODEL), BF16),
                        pltpu.VMEM((2, D_MODEL, EXPERT_DIM), F32),
                        pltpu.VMEM((2, D_MODEL, EXPERT_DIM), F32),
                        pltpu.VMEM((2, EXPERT_DIM, D_MODEL), F32),
                        pltpu.SemaphoreType.DMA((2, 3)),
                        pltpu.SMEM((1,), jnp.int32)],
    )
    return pl.pallas_call(
        _experts_kernel,
        grid_spec=grid_spec,
        out_shape=jax.ShapeDtypeStruct(xs2d.shape, jnp.int32),
        compiler_params=pltpu.CompilerParams(dimension_semantics=("arbitrary",),
                                             vmem_limit_bytes=VMEM_LIMIT),
        name="experts",
    )(texp, nused, tend, xs2d, weg, weu, wed)


def _final_kernel(x_ref, hp_ref, y8_ref, wt_ref, mod_ref, wsg_ref, wsu_ref, wsd_ref, fng_ref, o_ref):
    tm = x_ref.shape[0]
    lo, hi = _unpack_pair(_load_token_words(hp_ref, (), tm))
    hb = jnp.concatenate([lo, hi], axis=1).astype(BF16)
    shared = _dot((_silu(_dot(hb, wsg_ref[...])) * _dot(hb, wsu_ref[...])).astype(BF16), wsd_ref[...])
    wt = wt_ref[...]
    r_lo = jnp.zeros((tm, D_MODEL // 2), F32)
    r_hi = jnp.zeros((tm, D_MODEL // 2), F32)
    for k in range(TOP_K):
        lo, hi = _unpack_pair(_load_token_words(y8_ref, (k,), tm))
        wk = wt[:, k:k + 1]
        r_lo = r_lo + wk * lo
        r_hi = r_hi + wk * hi
    routed = jnp.concatenate([r_lo, r_hi], axis=1)
    y = x_ref[...] + mod_ref[0, 5:6, :] * (routed + shared)
    ms = jnp.mean(y * y, axis=-1, keepdims=True)
    o_ref[...] = y * lax.rsqrt(ms + EPS) * fng_ref[...]


def _final(x1, hp2d, y8, wtok, mod3, wsg, wsu, wsd, final_g, seq_len, mod_row_of_batch):
    t = x1.shape[0]
    tm = TM_FINAL

    def mod_idx(i):
        return (mod_row_of_batch((i * tm) // seq_len), 0, 0)

    def full(a):
        return pl.BlockSpec(a.shape, lambda i: (0,) * a.ndim)

    return pl.pallas_call(
        _final_kernel,
        grid=(t // tm,),
        in_specs=[pl.BlockSpec((tm, D_MODEL), lambda i: (i, 0)),
                  pl.BlockSpec((tm * ROW_SLABS, 128), lambda i: (i, 0)),
                  pl.BlockSpec((TOP_K, tm * ROW_SLABS, 128), lambda i: (0, i, 0)),
                  pl.BlockSpec((tm, 128), lambda i: (i, 0)),
                  pl.BlockSpec((1, 6, D_MODEL), mod_idx),
                  full(wsg), full(wsu), full(wsd), full(final_g)],
        out_specs=pl.BlockSpec((tm, D_MODEL), lambda i: (i, 0)),
        out_shape=jax.ShapeDtypeStruct((t, D_MODEL), F32),
        compiler_params=pltpu.CompilerParams(dimension_semantics=("parallel",),
                                             vmem_limit_bytes=VMEM_LIMIT),
        name="final",
    )(x1, hp2d, y8, wtok, mod3, wsg, wsu, wsd, final_g)


def _moe(x1, mod3, lw, seq_len, mod_row_of_batch):
    t = x1.shape[0]
    n_tiles = TOP_K * t // EXPERT_ROWS + N_EXPERTS
    n_tiles_pad = -(-n_tiles // 128) * 128
    hp2d, ek, rk, wtok, cnt = _router(x1, mod3, lw["norm2_g"], lw["w_router_t"], lw["router_bias"],
                                      seq_len, mod_row_of_batch)
    pos, texp, nused, tend = _plan(ek, rk, cnt, n_tiles_pad)
    pos3 = pos.reshape(TOP_K, t // SC_CHUNK, SC_CHUNK).transpose(1, 0, 2)
    xs = _sc_dispatch(hp2d.reshape(t, ROW_SLABS, 128), pos3, n_tiles * EXPERT_ROWS)
    ys2d = _experts(texp.reshape(-1), nused.reshape(-1), tend[:, 0], xs.reshape(-1, 128),
                    lw["weg"], lw["weu"], lw["wed"], n_tiles)
    y8 = _sc_gather(ys2d.reshape(-1, ROW_SLABS, 128), pos3, t)
    return _final(x1, hp2d, y8.reshape(TOP_K, t * ROW_SLABS, 128), wtok, mod3,
                  lw["wsg"], lw["wsu"], lw["wsd"], lw["final_g"], seq_len, mod_row_of_batch)


def _dft_tables(seq_len):
    gd = FOURIER_GROUP_DIM
    kc = np.arange(gd)
    ang_c = ((kc[:, None] * kc[None, :]) % gd) * (2.0 * math.pi / gd)
    cs = np.concatenate([np.cos(ang_c), np.sin(ang_c)], axis=1) * (gd ** -0.5)
    kl = np.arange(seq_len)
    ang_l = ((kl[:, None] * kl[None, :]) % seq_len) * (2.0 * math.pi / seq_len)
    cls = np.concatenate([np.cos(ang_l), -np.sin(ang_l)], axis=1) * (seq_len ** -0.5)
    return jnp.asarray(cs.astype(np.float32), dtype=BF16), jnp.asarray(cls.astype(np.float32), dtype=BF16)


def _rope_tables(length):
    rows = length // GRID_W
    r = np.repeat(np.arange(rows, dtype=np.float32), GRID_W)
    col = np.tile(np.arange(GRID_W, dtype=np.float32), rows)
    nf = RET_HEAD_DIM // 4
    inv = (np.float32(ROPE_BASE) ** (-np.arange(nf, dtype=np.float32) / np.float32(nf))).astype(np.float32)
    ar = r[:, None] * inv[None]
    ac = col[:, None] * inv[None]
    ang = np.concatenate([ar, ar, ac, ac], axis=-1).astype(np.float64)
    sign = np.where((np.arange(RET_HEAD_DIM) & nf) == 0, -1.0, 1.0)
    return (jnp.asarray(np.cos(ang).astype(np.float32)),
            jnp.asarray((np.sin(ang) * sign[None, :]).astype(np.float32)))


def _trunk_path(x, mod3, mod_row_of_batch, s0f, s0b, rope, lw):
    batch, seq_len, _ = x.shape
    x2d = x.reshape(batch * seq_len, D_MODEL)
    uf, q, k, v, sg, gf, gr = _inproj(x2d, mod3, lw["norm1_g"], lw["w_in"], seq_len, mod_row_of_batch, rope)
    r, s_f, s_b = _retention(q, k, v, sg, lw["dec"], lw["gn_g"], s0f, s0b, batch, seq_len)
    cs, cls = _dft_tables(seq_len)
    fmix = _fnet(uf, cs, cls, batch, seq_len)
    x1 = _merge(fmix, r, gf, gr, x2d, mod3, lw["w_four"], lw["w_ret"], lw["w_o"], seq_len, mod_row_of_batch)
    y = _moe(x1, mod3, lw, seq_len, mod_row_of_batch)
    return y.reshape(batch, seq_len, D_MODEL), s_f, s_b


def kernel(x_prompt, x_sample, state_ret_fwd, state_ret_bwd, c, c_ctx, w_ada, b_ada, norm1_g, norm2_g, w_in,
           ret_decay_fwd, ret_decay_bwd, ret_gn_g, w_four_out, w_ret_out, w_out, w_router, router_bias,
           w_exp_gate, w_exp_up, w_exp_down, w_shared_gate, w_shared_up, w_shared_down, final_norm_g):
    depth = w_ada.shape[0]
    assert depth == 1, "final norm is fused into the last layer's MoE kernel"
    n_ctx, n_lat = x_prompt.shape[0], x_sample.shape[0]
    cond = jnp.concatenate([c_ctx[None, :], c], axis=0)
    cond = jnp.pad(cond, ((0, (-cond.shape[0]) % 8), (0, 0)))
    rope = _rope_tables(x_sample.shape[1])
    zeros = jnp.zeros((n_ctx, N_RET_HEADS, RET_HEAD_DIM, RET_HEAD_DIM), F32)

    layer = 0
    mod = _ada(cond, w_ada[layer], b_ada[layer][None, :])
    mod3 = mod.reshape(mod.shape[0], 6, D_MODEL)
    dec = jnp.stack([ret_decay_fwd[layer], ret_decay_bwd[layer]], axis=1)
    lw = {
        "norm1_g": norm1_g[layer][None, :],
        "norm2_g": norm2_g[layer][None, :],
        "w_in": w_in[layer].astype(BF16),
        "dec": jnp.broadcast_to(dec[:, :, None], (N_RET_HEADS, 2, RET_HEAD_DIM)).astype(F32),
        "gn_g": ret_gn_g[layer][None, :],
        "w_four": w_four_out[layer].astype(BF16),
        "w_ret": w_ret_out[layer].astype(BF16),
        "w_o": w_out[layer].astype(BF16),
        "w_router_t": w_router[layer].T,
        "router_bias": router_bias[layer][:, None],
        "weg": w_exp_gate[layer],
        "weu": w_exp_up[layer],
        "wed": w_exp_down[layer],
        "wsg": w_shared_gate[layer].astype(BF16),
        "wsu": w_shared_up[layer].astype(BF16),
        "wsd": w_shared_down[layer].astype(BF16),
        "final_g": final_norm_g[None, :],
    }
    y_prompt, s_f, s_b = _trunk_path(x_prompt, mod3, lambda b: 0, zeros, zeros, None, lw)
    y_sample, _, _ = _trunk_path(x_sample, mod3, lambda b: 1 + b, state_ret_fwd[:, layer],
                                 state_ret_bwd[:, layer], rope, lw)
    return (y_prompt, y_sample, s_f[:, None], s_b[:, None])
```

```python
import functools
import math

import jax
import jax.numpy as jnp
import numpy as np
from jax import lax
from jax.experimental import pallas as pl
from jax.experimental.pallas import tpu as pltpu
from jax.experimental.pallas import tpu_sc as plsc

F32 = jnp.float32
BF16 = jnp.bfloat16

D_MODEL = 1024
GRID_W = 64
N_FOURIER_GROUPS = 8
FOURIER_GROUP_DIM = 128
N_RET_HEADS = 4
RET_HEAD_DIM = 128
RET_WIDTH = N_RET_HEADS * RET_HEAD_DIM
CHUNK = 128
N_EXPERTS = 64
N_EXPERT_GROUPS = 8
EXPERTS_PER_GROUP = N_EXPERTS // N_EXPERT_GROUPS
TOPK_GROUPS = 4
TOP_K = 8
EXPERT_DIM = 256
ROUTED_SCALE = 2.5
ROPE_BASE = 10000.0
EPS = 1e-6
Q_SCALE = RET_HEAD_DIM ** -0.5

_C_UF = (0, 1024)
_C_Q = (1024, 1536)
_C_K = (1536, 2048)
_C_V = (2048, 2560)
_C_G = (2560, 3072)
_C_GF = (3072, 4096)
_C_GR = (4096, 5120)

VMEM_LIMIT = 56 * 1024 * 1024

TM_PROJ = 512
FNET_ROWS = 256
TM_ROUTER = 1024
TM_FINAL = 512
EXPERT_ROWS = 512
TILES_PER_STEP = 2
ROW_SLABS = 4
SC_CORES = 2
SC_WORKERS = 32
SC_CHUNK = 128


def _silu(x):
    return x * jax.nn.sigmoid(x)


def _dot(a, b):
    return jnp.dot(a, b, preferred_element_type=F32)


def _rms_mod(x, g, shift, scale):
    ms = jnp.mean(x * x, axis=-1, keepdims=True)
    y = x * lax.rsqrt(ms + EPS) * g
    return y * (1.0 + scale) + shift


def _ada_kernel(cond_ref, w_ref, b_ref, o_ref):
    s = _silu(cond_ref[...]).astype(BF16)
    o_ref[...] = _dot(s, w_ref[...].astype(BF16)) + b_ref[...]


def _ada(cond, w_ada, b_ada):
    rows, n = cond.shape[0], w_ada.shape[1]
    tn = 1536
    return pl.pallas_call(
        _ada_kernel,
        grid=(n // tn,),
        in_specs=[pl.BlockSpec((rows, D_MODEL), lambda j: (0, 0)),
                  pl.BlockSpec((D_MODEL, tn), lambda j: (0, j)),
                  pl.BlockSpec((1, tn), lambda j: (0, j))],
        out_specs=pl.BlockSpec((rows, tn), lambda j: (0, j)),
        out_shape=jax.ShapeDtypeStruct((rows, n), F32),
        compiler_params=pltpu.CompilerParams(vmem_limit_bytes=VMEM_LIMIT),
        name="ada",
    )(cond, w_ada, b_ada)


def _rope_head(x, cos, sin_signed, first_half):
    partner = jnp.where(first_half, pltpu.roll(x, 96, 1), pltpu.roll(x, 32, 1))
    return x * cos + partner * sin_signed


def _inproj_kernel(*refs, use_rope):
    if use_rope:
        x_ref, mod_ref, g_ref, w_ref, cos_ref, sin_ref = refs[:6]
        outs = refs[6:]
    else:
        x_ref, mod_ref, g_ref, w_ref = refs[:4]
        outs = refs[4:]
    uf_o, q_o, k_o, v_o, sg_o, gf_o, gr_o = outs

    h = _rms_mod(x_ref[...], g_ref[...], mod_ref[0, 0:1, :], mod_ref[0, 1:2, :])
    hb = h.astype(BF16)

    def proj(cols):
        return _dot(hb, w_ref[:, cols[0]:cols[1]])

    uf_o[...] = proj(_C_UF).astype(BF16)
    q = proj(_C_Q)
    k = proj(_C_K)
    if use_rope:
        cos = cos_ref[...]
        sin_signed = sin_ref[...]
        lane = lax.broadcasted_iota(jnp.int32, cos.shape, 1)
        first_half = (lane & 32) == 0
        for hd in range(N_RET_HEADS):
            sl = slice(hd * RET_HEAD_DIM, (hd + 1) * RET_HEAD_DIM)
            q_o[:, sl] = (_rope_head(q[:, sl], cos, sin_signed, first_half) * Q_SCALE).astype(BF16)
            k_o[:, sl] = _rope_head(k[:, sl], cos, sin_signed, first_half).astype(BF16)
    else:
        q_o[...] = (q * Q_SCALE).astype(BF16)
        k_o[...] = k.astype(BF16)
    v_o[...] = proj(_C_V).astype(BF16)
    sg_o[...] = _silu(proj(_C_G)).astype(BF16)
    gf_o[...] = jax.nn.sigmoid(proj(_C_GF)).astype(BF16)
    gr_o[...] = jax.nn.sigmoid(proj(_C_GR)).astype(BF16)


def _inproj(x2d, mod3, norm_g, w_in_bf, seq_len, mod_row_of_batch, rope):
    t = x2d.shape[0]
    tm = TM_PROJ
    tiles_per_seq = max(seq_len // tm, 1)

    def mod_idx(i):
        return (mod_row_of_batch((i * tm) // seq_len), 0, 0)

    in_specs = [pl.BlockSpec((tm, D_MODEL), lambda i: (i, 0)),
                pl.BlockSpec((1, 6, D_MODEL), mod_idx),
                pl.BlockSpec((1, D_MODEL), lambda i: (0, 0)),
                pl.BlockSpec(w_in_bf.shape, lambda i: (0, 0))]
    args = [x2d, mod3, norm_g, w_in_bf]
    if rope is not None:
        in_specs += [pl.BlockSpec((tm, RET_HEAD_DIM), lambda i: (i % tiles_per_seq, 0))] * 2
        args += list(rope)
    widths = [1024, RET_WIDTH, RET_WIDTH, RET_WIDTH, RET_WIDTH, 1024, 1024]
    return pl.pallas_call(
        functools.partial(_inproj_kernel, use_rope=rope is not None),
        grid=(t // tm,),
        in_specs=in_specs,
        out_specs=[pl.BlockSpec((tm, w), lambda i: (i, 0)) for w in widths],
        out_shape=[jax.ShapeDtypeStruct((t, w), BF16) for w in widths],
        compiler_params=pltpu.CompilerParams(dimension_semantics=("parallel",),
                                             vmem_limit_bytes=VMEM_LIMIT),
        name="inproj",
    )(*args)


def _retention_kernel(q_ref, k_ref, v_ref, sg_ref, dec_ref, gn_ref, s0f_ref, s0b_ref,
                      r_ref, sfo_ref, sbo_ref):
    n_chunks = q_ref.shape[0] // CHUNK
    dec = dec_ref[...]
    lg = jnp.minimum(dec, 0.0) - jnp.log1p(jnp.exp(-jnp.abs(dec)))
    lgf = lg[0:1, :]
    lgb = lg[1:2, :]
    row = lax.broadcasted_iota(jnp.int32, (CHUNK, CHUNK), 0).astype(F32)
    col = lax.broadcasted_iota(jnp.int32, (CHUNK, CHUNK), 1).astype(F32)
    diff = row - col
    decay = jnp.exp(jnp.where(diff >= 0, lgf * diff, lgb * (-diff)))
    qw_f = jnp.exp(lgf * (row + 1.0))
    qw_b = jnp.exp(lgb * (CHUNK - row))
    kw_f = jnp.exp(lgf * (CHUNK - 1.0 - row))
    kw_b = jnp.exp(lgb * row)
    gc_f = jnp.exp(lgf * CHUNK)
    gc_b = jnp.exp(lgb * CHUNK)

    def rows(n):
        return slice(n * CHUNK, (n + 1) * CHUNK)

    kv_f, kv_b = [], []
    for n in range(n_chunks):
        kn = k_ref[rows(n), :].astype(F32)
        vn = v_ref[rows(n), :]
        kv_f.append(_dot((kn * kw_f).T.astype(BF16), vn))
        kv_b.append(_dot((kn * kw_b).T.astype(BF16), vn))

    s = s0f_ref[...]
    prev_f = []
    for n in range(n_chunks):
        prev_f.append(s.astype(BF16))
        s = gc_f * s + kv_f[n]
    sfo_ref[...] = s
    s = s0b_ref[...]
    prev_b = [None] * n_chunks
    for n in reversed(range(n_chunks)):
        prev_b[n] = s.astype(BF16)
        s = gc_b * s + kv_b[n]
    sbo_ref[...] = s

    gn = gn_ref[...]
    for n in range(n_chunks):
        qn = q_ref[rows(n), :]
        qf = qn.astype(F32)
        scores = lax.dot_general(qn, k_ref[rows(n), :], (((1,), (1,)), ((), ())),
                                 preferred_element_type=F32)
        o = _dot((scores * decay).astype(BF16), v_ref[rows(n), :])
        o = o + _dot((qf * qw_f).astype(BF16), prev_f[n])
        o = o + _dot((qf * qw_b).astype(BF16), prev_b[n])
        mu = jnp.mean(o, axis=-1, keepdims=True)
        d = o - mu
        var = jnp.mean(d * d, axis=-1, keepdims=True)
        on = d * lax.rsqrt(var + EPS) * gn
        r_ref[rows(n), :] = (on * sg_ref[rows(n), :].astype(F32)).astype(BF16)


def _retention(q, k, v, sg, dec, gn_g, s0f, s0b, batch, seq_len):
    hd = RET_HEAD_DIM
    tok_spec = pl.BlockSpec((seq_len, hd), lambda b, h: (b, h))
    st_spec = pl.BlockSpec((None, None, hd, hd), lambda b, h: (b, h, 0, 0))
    st_shape = jax.ShapeDtypeStruct((batch, N_RET_HEADS, hd, hd), F32)
    return pl.pallas_call(
        _retention_kernel,
        grid=(batch, N_RET_HEADS),
        in_specs=[tok_spec, tok_spec, tok_spec, tok_spec,
                  pl.BlockSpec((None, 2, hd), lambda b, h: (h, 0, 0)),
                  pl.BlockSpec((1, hd), lambda b, h: (0, h)),
                  st_spec, st_spec],
        out_specs=[tok_spec, st_spec, st_spec],
        out_shape=[jax.ShapeDtypeStruct((batch * seq_len, RET_WIDTH), BF16), st_shape, st_shape],
        compiler_params=pltpu.CompilerParams(dimension_semantics=("parallel", "parallel"),
                                             vmem_limit_bytes=VMEM_LIMIT),
        name="retention",
    )(q, k, v, sg, dec, gn_g, s0f, s0b)


def _fnet_kernel(uf_ref, cs_ref, cls_ref, o_ref, xcs_ref):
    seq_len = uf_ref.shape[0]
    gd = FOURIER_GROUP_DIM

    @pl.when(pl.program_id(1) == 0)
    def _():
        for g in range(N_FOURIER_GROUPS):
            x = _dot(uf_ref[:, g * gd:(g + 1) * gd], cs_ref[...])
            xcs_ref[0:seq_len, g * gd:(g + 1) * gd] = x[:, :gd].astype(BF16)
            xcs_ref[seq_len:2 * seq_len, g * gd:(g + 1) * gd] = x[:, gd:].astype(BF16)

    o_ref[...] = _dot(cls_ref[...], xcs_ref[...]).astype(BF16)


def _fnet(uf, cs, cls, batch, seq_len):
    rb = FNET_ROWS
    nr = seq_len // rb
    return pl.pallas_call(
        _fnet_kernel,
        grid=(batch, nr),
        in_specs=[pl.BlockSpec((seq_len, D_MODEL), lambda b, r: (b, 0)),
                  pl.BlockSpec(cs.shape, lambda b, r: (0, 0)),
                  pl.BlockSpec((rb, 2 * seq_len), lambda b, r: (r, 0))],
        out_specs=pl.BlockSpec((rb, D_MODEL), lambda b, r: (b * nr + r, 0)),
        out_shape=jax.ShapeDtypeStruct((batch * seq_len, D_MODEL), BF16),
        scratch_shapes=[pltpu.VMEM((2 * seq_len, D_MODEL), BF16)],
        compiler_params=pltpu.CompilerParams(dimension_semantics=("parallel", "arbitrary"),
                                             vmem_limit_bytes=VMEM_LIMIT),
        name="fnet",
    )(uf, cs, cls)


def _merge_kernel(fm_ref, r_ref, gf_ref, gr_ref, x_ref, mod_ref, wf_ref, wr_ref, wo_ref, o_ref):
    f_out = _dot(fm_ref[...], wf_ref[...])
    r_out = _dot(r_ref[...], wr_ref[...])
    merged = gf_ref[...].astype(F32) * f_out + gr_ref[...].astype(F32) * r_out
    mix = _dot(merged.astype(BF16), wo_ref[...])
    o_ref[...] = x_ref[...] + mod_ref[0, 2:3, :] * mix


def _merge(fmix, r, gf, gr, x2d, mod3, w_four, w_ret, w_o, seq_len, mod_row_of_batch):
    t = x2d.shape[0]
    tm = TM_PROJ

    def mod_idx(i):
        return (mod_row_of_batch((i * tm) // seq_len), 0, 0)

    def tok(w):
        return pl.BlockSpec((tm, w), lambda i: (i, 0))

    def full(a):
        return pl.BlockSpec(a.shape, lambda i: (0, 0))

    return pl.pallas_call(
        _merge_kernel,
        grid=(t // tm,),
        in_specs=[tok(D_MODEL), tok(RET_WIDTH), tok(D_MODEL), tok(D_MODEL), tok(D_MODEL),
                  pl.BlockSpec((1, 6, D_MODEL), mod_idx), full(w_four), full(w_ret), full(w_o)],
        out_specs=tok(D_MODEL),
        out_shape=jax.ShapeDtypeStruct((t, D_MODEL), F32),
        compiler_params=pltpu.CompilerParams(dimension_semantics=("parallel",),
                                             vmem_limit_bytes=VMEM_LIMIT),
        name="merge",
    )(fmix, r, gf, gr, x2d, mod3, w_four, w_ret, w_o)


def _pack_pair(lo_f32, hi_f32):
    lo = lax.bitcast_convert_type(lo_f32.astype(BF16).astype(F32), jnp.uint32)
    hi = lax.bitcast_convert_type(hi_f32.astype(BF16).astype(F32), jnp.uint32)
    return lax.bitcast_convert_type((lo >> 16) | hi, jnp.int32)


def _unpack_pair(words_i32):
    w = lax.bitcast_convert_type(words_i32, jnp.uint32)
    lo = lax.bitcast_convert_type(w << 16, F32)
    hi = lax.bitcast_convert_type(w & jnp.uint32(0xFFFF0000), F32)
    return lo, hi


def _load_token_words(ref, lead, n_tok):
    parts = []
    for s in range(ROW_SLABS):
        idx = (pl.ds(s, n_tok, stride=ROW_SLABS), slice(None))
        parts.append(ref[lead + idx] if lead else ref[idx])
    return jnp.concatenate(parts, axis=1)


def _store_token_words(ref, words, n_tok):
    for s in range(ROW_SLABS):
        ref[pl.ds(s, n_tok, stride=ROW_SLABS), :] = words[:, s * 128:(s + 1) * 128]


def _route(scores, biased):
    tokens = scores.shape[1]
    neg = -jnp.inf
    epg = EXPERTS_PER_GROUP
    iota_g = lax.broadcasted_iota(jnp.int32, (epg, tokens), 0).astype(F32)

    def pick_first_max(cur, iota, size):
        m = jnp.max(cur, axis=0, keepdims=True)
        idx = jnp.min(jnp.where(cur == m, iota, float(size)), axis=0, keepdims=True)
        return m, idx, iota == idx

    group_scores = []
    for g in range(N_EXPERT_GROUPS):
        vals = biased[g * epg:(g + 1) * epg, :]
        m1, _, hit = pick_first_max(vals, iota_g, epg)
        m2 = jnp.max(jnp.where(hit, neg, vals), axis=0, keepdims=True)
        group_scores.append(m1 + m2)
    cur = jnp.concatenate(group_scores, axis=0)
    group_sel = jnp.zeros_like(cur)
    for _ in range(TOPK_GROUPS):
        _, _, hit = pick_first_max(cur, iota_g, N_EXPERT_GROUPS)
        group_sel = jnp.where(hit, 1.0, group_sel)
        cur = jnp.where(hit, neg, cur)
    masked = jnp.concatenate(
        [jnp.where(group_sel[g:g + 1, :] > 0.0, biased[g * epg:(g + 1) * epg, :], neg)
         for g in range(N_EXPERT_GROUPS)], axis=0)
    iota_e = lax.broadcasted_iota(jnp.int32, masked.shape, 0).astype(F32)
    sel = jnp.zeros_like(masked)
    cur = masked
    picks = []
    for _ in range(TOP_K):
        _, idx, hit = pick_first_max(cur, iota_e, N_EXPERTS)
        picks.append(idx)
        sel = jnp.where(hit, 1.0, sel)
        cur = jnp.where(hit, neg, cur)
    w = scores * sel
    return w / jnp.sum(w, axis=0, keepdims=True) * ROUTED_SCALE, sel, picks


def _router_kernel(x_ref, mod_ref, g2_ref, wrt_ref, rb_ref, hp_ref, ek_ref, rk_ref, wt_ref, cnt_ref, run_scr):
    tm = x_ref.shape[0]

    @pl.when(pl.program_id(0) == 0)
    def _():
        run_scr[...] = jnp.zeros_like(run_scr)

    h = _rms_mod(x_ref[...], g2_ref[...], mod_ref[0, 3:4, :], mod_ref[0, 4:5, :])
    half = D_MODEL // 2
    _store_token_words(hp_ref, _pack_pair(h[:, :half], h[:, half:]), tm)

    logits_t = lax.dot_general(wrt_ref[...], h, (((1,), (1,)), ((), ())),
                               precision=lax.Precision.HIGHEST, preferred_element_type=F32)
    scores = jax.nn.sigmoid(logits_t)
    comb_t, sel, picks = _route(scores, scores + rb_ref[...])

    earlier = (lax.broadcasted_iota(jnp.int32, (tm, tm), 0) < lax.broadcasted_iota(jnp.int32, (tm, tm), 1))
    rank_t = _dot(sel.astype(BF16), jnp.where(earlier, 1.0, 0.0).astype(BF16)) + run_scr[...]
    run_scr[...] += jnp.sum(sel, axis=1, keepdims=True)
    cnt_ref[...] = jnp.broadcast_to(run_scr[...], cnt_ref.shape)

    iota_e = lax.broadcasted_iota(jnp.int32, sel.shape, 0).astype(F32)
    ranks, weights = [], []
    for idx in picks:
        hit = iota_e == idx
        ranks.append(jnp.sum(jnp.where(hit, rank_t, 0.0), axis=0, keepdims=True))
        weights.append(jnp.sum(jnp.where(hit, comb_t, 0.0), axis=0, keepdims=True))
    ek_ref[...] = jnp.concatenate(picks, axis=0).astype(jnp.int32)
    rk_ref[...] = jnp.concatenate(ranks, axis=0).astype(jnp.int32)
    w_pad = jnp.concatenate(weights + [jnp.zeros((128 - TOP_K, tm), F32)], axis=0)
    wt_ref[...] = w_pad.T


def _router(x1, mod3, norm2_g, w_router_t, router_bias, seq_len, mod_row_of_batch):
    t = x1.shape[0]
    tm = TM_ROUTER

    def mod_idx(i):
        return (mod_row_of_batch((i * tm) // seq_len), 0, 0)

    def full(a):
        return pl.BlockSpec(a.shape, lambda i: (0,) * a.ndim)

    return pl.pallas_call(
        _router_kernel,
        grid=(t // tm,),
        in_specs=[pl.BlockSpec((tm, D_MODEL), lambda i: (i, 0)),
                  pl.BlockSpec((1, 6, D_MODEL), mod_idx),
                  full(norm2_g), full(w_router_t), full(router_bias)],
        out_specs=[pl.BlockSpec((tm * ROW_SLABS, 128), lambda i: (i, 0)),
                   pl.BlockSpec((TOP_K, tm), lambda i: (0, i)),
                   pl.BlockSpec((TOP_K, tm), lambda i: (0, i)),
                   pl.BlockSpec((tm, 128), lambda i: (i, 0)),
                   pl.BlockSpec((N_EXPERTS, 128), lambda i: (0, 0))],
        out_shape=[jax.ShapeDtypeStruct((t * ROW_SLABS, 128), jnp.int32),
                   jax.ShapeDtypeStruct((TOP_K, t), jnp.int32),
                   jax.ShapeDtypeStruct((TOP_K, t), jnp.int32),
                   jax.ShapeDtypeStruct((t, 128), F32),
                   jax.ShapeDtypeStruct((N_EXPERTS, 128), F32)],
        scratch_shapes=[pltpu.VMEM((N_EXPERTS, 1), F32)],
        compiler_params=pltpu.CompilerParams(dimension_semantics=("arbitrary",),
                                             vmem_limit_bytes=VMEM_LIMIT),
        name="router",
    )(x1, mod3, norm2_g, w_router_t, router_bias)


def _plan_kernel(ek_ref, rk_ref, cnt_ref, pos_ref, texp_ref, nused_ref, tend_ref):
    rows = float(EXPERT_ROWS)
    cnt = cnt_ref[:, 0:1]
    tiles = jnp.floor((cnt + (rows - 1.0)) / rows)
    before = (lax.broadcasted_iota(jnp.int32, (N_EXPERTS, N_EXPERTS), 1)
              < lax.broadcasted_iota(jnp.int32, (N_EXPERTS, N_EXPERTS), 0))
    tile_start = jnp.dot(jnp.where(before, 1.0, 0.0), jnp.broadcast_to(tiles, (N_EXPERTS, 128)),
                         precision=lax.Precision.HIGHEST, preferred_element_type=F32)[:, 0:1]
    tile_end = tile_start + tiles
    row_start = tile_start * rows

    ek = ek_ref[...]
    pos = rk_ref[...].astype(F32)
    tile_id = lax.broadcasted_iota(jnp.int32, texp_ref.shape, 1).astype(F32)
    texp = jnp.zeros(texp_ref.shape, F32)
    for e in range(N_EXPERTS):
        pos = pos + jnp.where(ek == e, row_start[e:e + 1, :], 0.0)
        texp = texp + jnp.where(tile_id >= tile_end[e:e + 1, :], 1.0, 0.0)
    pos_ref[...] = pos.astype(jnp.int32)
    texp_ref[...] = jnp.minimum(texp, N_EXPERTS - 1.0).astype(jnp.int32)
    nused_ref[...] = jnp.broadcast_to(tile_end[N_EXPERTS - 1:N_EXPERTS, :], nused_ref.shape).astype(jnp.int32)
    tend_ref[...] = jnp.broadcast_to(tile_end, tend_ref.shape).astype(jnp.int32)


def _plan(ek, rk, cnt, n_tiles_pad):
    t = ek.shape[1]

    def full(shape):
        return pl.BlockSpec(shape, lambda: (0,) * len(shape))

    return pl.pallas_call(
        _plan_kernel,
        in_specs=[full(ek.shape), full(rk.shape), full(cnt.shape)],
        out_specs=[full((TOP_K, t)), full((1, n_tiles_pad)), full((1, 128)), full((N_EXPERTS, 128))],
        out_shape=[jax.ShapeDtypeStruct((TOP_K, t), jnp.int32),
                   jax.ShapeDtypeStruct((1, n_tiles_pad), jnp.int32),
                   jax.ShapeDtypeStruct((1, 128), jnp.int32),
                   jax.ShapeDtypeStruct((N_EXPERTS, 128), jnp.int32)],
        compiler_params=pltpu.CompilerParams(vmem_limit_bytes=VMEM_LIMIT),
        name="plan",
    )(ek, rk, cnt)


def _sc_mesh():
    return plsc.VectorSubcoreMesh(core_axis_name="c", subcore_axis_name="s")


def _sc_dispatch(rows, pos3, n_out):
    t = rows.shape[0]
    ch = SC_CHUNK
    per_w = (t // ch) // SC_WORKERS

    @functools.partial(
        pl.kernel, out_type=jax.ShapeDtypeStruct((n_out,) + rows.shape[1:], jnp.int32), mesh=_sc_mesh(),
        scratch_types=[pltpu.VMEM((TOP_K, ch), jnp.int32), pltpu.VMEM((ch,) + rows.shape[1:], jnp.int32),
                       pltpu.SemaphoreType.DMA])
    def k(rows_hbm, pos_hbm, out_hbm, idx_v, rows_v, sem):
        wid = lax.axis_index("s") * SC_CORES + lax.axis_index("c")

        @pl.loop(0, per_w)
        def _(j):
            c = wid * per_w + j
            pltpu.sync_copy(pos_hbm.at[c], idx_v)
            pltpu.sync_copy(rows_hbm.at[pl.ds(c * ch, ch)], rows_v)
            copies = [pltpu.async_copy(rows_v, out_hbm.at[idx_v.at[kk]], sem) for kk in range(TOP_K)]
            for cp in copies:
                cp.wait()

    return k(rows, pos3)


def _sc_gather(table, pos3, t):
    ch = SC_CHUNK
    per_w = (t // ch) // SC_WORKERS

    @functools.partial(
        pl.kernel, out_type=jax.ShapeDtypeStruct((TOP_K, t) + table.shape[1:], jnp.int32), mesh=_sc_mesh(),
        scratch_types=[pltpu.VMEM((TOP_K, ch), jnp.int32), pltpu.VMEM((ch,) + table.shape[1:], jnp.int32),
                       pltpu.SemaphoreType.DMA])
    def k(tab_hbm, pos_hbm, out_hbm, idx_v, rows_v, sem):
        wid = lax.axis_index("s") * SC_CORES + lax.axis_index("c")

        @pl.loop(0, per_w)
        def _(j):
            c = wid * per_w + j
            pltpu.sync_copy(pos_hbm.at[c], idx_v)
            for kk in range(TOP_K):
                pltpu.async_copy(tab_hbm.at[idx_v.at[kk]], rows_v, sem).wait()
                pltpu.sync_copy(rows_v, out_hbm.at[kk, pl.ds(c * ch, ch)])

    return k(table, pos3)


def _experts_kernel(texp_ref, nused_ref, tend_ref, xs_ref, weg_hbm, weu_hbm, wed_hbm, ys_ref,
                    wg_scr, wu_scr, wd_scr, wg_buf, wu_buf, wd_buf, sem, group_scr):
    step = pl.program_id(0)
    rows = EXPERT_ROWS
    half = D_MODEL // 2
    n_used = nused_ref[0]

    def weight_copies(e, slot):
        return [pltpu.make_async_copy(weg_hbm.at[e], wg_buf.at[slot], sem.at[slot, 0]),
                pltpu.make_async_copy(weu_hbm.at[e], wu_buf.at[slot], sem.at[slot, 1]),
                pltpu.make_async_copy(wed_hbm.at[e], wd_buf.at[slot], sem.at[slot, 2])]

    @pl.when(step == 0)
    def _():
        group_scr[0] = 0
        for cp in weight_copies(texp_ref[0], 0):
            cp.start()

    def row_tile(tile, x_view, y_view):
        expert = texp_ref[tile]
        used = tile < n_used
        new_expert = (tile == 0) | (expert != texp_ref[jnp.maximum(tile - 1, 0)])

        @pl.when(used & new_expert)
        def _():
            group = group_scr[0]
            slot = group % 2
            next_tile = tend_ref[expert]

            @pl.when(next_tile < n_used)
            def _():
                for cp in weight_copies(texp_ref[next_tile], 1 - slot):
                    cp.start()

            for cp in weight_copies(expert, slot):
                cp.wait()
            wg_scr[...] = wg_buf[slot].astype(BF16)
            wu_scr[...] = wu_buf[slot].astype(BF16)
            wd_scr[...] = wd_buf[slot].astype(BF16)
            group_scr[0] = group + 1

        @pl.when(used)
        def _():
            lo, hi = _unpack_pair(_load_token_words(x_view, (), rows))
            lo = lo.astype(BF16)
            hi = hi.astype(BF16)
            g = _dot(lo, wg_scr[0:half, :]) + _dot(hi, wg_scr[half:D_MODEL, :])
            u = _dot(lo, wu_scr[0:half, :]) + _dot(hi, wu_scr[half:D_MODEL, :])
            y = _dot((_silu(g) * u).astype(BF16), wd_scr[...])
            _store_token_words(y_view, _pack_pair(y[:, :half], y[:, half:]), rows)

        @pl.when(jnp.logical_not(used) & (step == (n_used - 1) // TILES_PER_STEP))
        def _():
            y_view[...] = jnp.zeros_like(y_view)

    for s in range(TILES_PER_STEP):
        view = pl.ds(s * rows * ROW_SLABS, rows * ROW_SLABS)
        row_tile(step * TILES_PER_STEP + s, xs_ref.at[view], ys_ref.at[view])


def _experts(texp, nused, tend, xs2d, weg, weu, wed, n_tiles):
    block = (TILES_PER_STEP * EXPERT_ROWS * ROW_SLABS, 128)
    hbm = pl.BlockSpec(memory_space=pl.ANY)

    def block_idx(j, te, nu, tn):
        return (jnp.minimum(j, (nu[0] - 1) // TILES_PER_STEP), 0)

    grid_spec = pltpu.PrefetchScalarGridSpec(
        num_scalar_prefetch=3,
        grid=(n_tiles // TILES_PER_STEP,),
        in_specs=[pl.BlockSpec(block, block_idx), hbm, hbm, hbm],
        out_specs=pl.BlockSpec(block, block_idx),
        scratch_shapes=[pltpu.VMEM((D_MODEL, EXPERT_DIM), BF16),
                        pltpu.VMEM((D_MODEL, EXPERT_DIM), BF16),
                        pltpu.VMEM((EXPERT_DIM, D_MODEL), BF16),
                        pltpu.VMEM((2, D_MODEL, EXPERT_DIM), F32),
                        pltpu.VMEM((2, D_MODEL, EXPERT_DIM), F32),
                        pltpu.VMEM((2, EXPERT_DIM, D_MODEL), F32),
                        pltpu.SemaphoreType.DMA((2, 3)),
                        pltpu.SMEM((1,), jnp.int32)],
    )
    return pl.pallas_call(
        _experts_kernel,
        grid_spec=grid_spec,
        out_shape=jax.ShapeDtypeStruct(xs2d.shape, jnp.int32),
        compiler_params=pltpu.CompilerParams(dimension_semantics=("arbitrary",),
                                             vmem_limit_bytes=VMEM_LIMIT),
        name="experts",
    )(texp, nused, tend, xs2d, weg, weu, wed)


def _final_kernel(x_ref, hp_ref, y8_ref, wt_ref, mod_ref, wsg_ref, wsu_ref, wsd_ref, fng_ref, o_ref):
    tm = x_ref.shape[0]
    lo, hi = _unpack_pair(_load_token_words(hp_ref, (), tm))
    hb = jnp.concatenate([lo, hi], axis=1).astype(BF16)
    shared = _dot((_silu(_dot(hb, wsg_ref[...])) * _dot(hb, wsu_ref[...])).astype(BF16), wsd_ref[...])
    wt = wt_ref[...]
    r_lo = jnp.zeros((tm, D_MODEL // 2), F32)
    r_hi = jnp.zeros((tm, D_MODEL // 2), F32)
    for k in range(TOP_K):
        lo, hi = _unpack_pair(_load_token_words(y8_ref, (k,), tm))
        wk = wt[:, k:k + 1]
        r_lo = r_lo + wk * lo
        r_hi = r_hi + wk * hi
    routed = jnp.concatenate([r_lo, r_hi], axis=1)
    y = x_ref[...] + mod_ref[0, 5:6, :] * (routed + shared)
    ms = jnp.mean(y * y, axis=-1, keepdims=True)
    o_ref[...] = y * lax.rsqrt(ms + EPS) * fng_ref[...]


def _final(x1, hp2d, y8, wtok, mod3, wsg, wsu, wsd, final_g, seq_len, mod_row_of_batch):
    t = x1.shape[0]
    tm = TM_FINAL

    def mod_idx(i):
        return (mod_row_of_batch((i * tm) // seq_len), 0, 0)

    def full(a):
        return pl.BlockSpec(a.shape, lambda i: (0,) * a.ndim)

    return pl.pallas_call(
        _final_kernel,
        grid=(t // tm,),
        in_specs=[pl.BlockSpec((tm, D_MODEL), lambda i: (i, 0)),
                  pl.BlockSpec((tm * ROW_SLABS, 128), lambda i: (i, 0)),
                  pl.BlockSpec((TOP_K, tm * ROW_SLABS, 128), lambda i: (0, i, 0)),
                  pl.BlockSpec((tm, 128), lambda i: (i, 0)),
                  pl.BlockSpec((1, 6, D_MODEL), mod_idx),
                  full(wsg), full(wsu), full(wsd), full(final_g)],
        out_specs=pl.BlockSpec((tm, D_MODEL), lambda i: (i, 0)),
        out_shape=jax.ShapeDtypeStruct((t, D_MODEL), F32),
        compiler_params=pltpu.CompilerParams(dimension_semantics=("parallel",),
                                             vmem_limit_bytes=VMEM_LIMIT),
        name="final",
    )(x1, hp2d, y8, wtok, mod3, wsg, wsu, wsd, final_g)


def _moe(x1, mod3, lw, seq_len, mod_row_of_batch):
    t = x1.shape[0]
    n_tiles = TOP_K * t // EXPERT_ROWS + N_EXPERTS
    n_tiles_pad = -(-n_tiles // 128) * 128
    hp2d, ek, rk, wtok, cnt = _router(x1, mod3, lw["norm2_g"], lw["w_router_t"], lw["router_bias"],
                                      seq_len, mod_row_of_batch)
    pos, texp, nused, tend = _plan(ek, rk, cnt, n_tiles_pad)
    pos3 = pos.reshape(TOP_K, t // SC_CHUNK, SC_CHUNK).transpose(1, 0, 2)
    xs = _sc_dispatch(hp2d.reshape(t, ROW_SLABS, 128), pos3, n_tiles * EXPERT_ROWS)
    ys2d = _experts(texp.reshape(-1), nused.reshape(-1), tend[:, 0], xs.reshape(-1, 128),
                    lw["weg"], lw["weu"], lw["wed"], n_tiles)
    y8 = _sc_gather(ys2d.reshape(-1, ROW_SLABS, 128), pos3, t)
    return _final(x1, hp2d, y8.reshape(TOP_K, t * ROW_SLABS, 128), wtok, mod3,
                  lw["wsg"], lw["wsu"], lw["wsd"], lw["final_g"], seq_len, mod_row_of_batch)


def _dft_tables(seq_len):
    gd = FOURIER_GROUP_DIM
    kc = np.arange(gd)
    ang_c = ((kc[:, None] * kc[None, :]) % gd) * (2.0 * math.pi / gd)
    cs = np.concatenate([np.cos(ang_c), np.sin(ang_c)], axis=1) * (gd ** -0.5)
    kl = np.arange(seq_len)
    ang_l = ((kl[:, None] * kl[None, :]) % seq_len) * (2.0 * math.pi / seq_len)
    cls = np.concatenate([np.cos(ang_l), -np.sin(ang_l)], axis=1) * (seq_len ** -0.5)
    return jnp.asarray(cs.astype(np.float32), dtype=BF16), jnp.asarray(cls.astype(np.float32), dtype=BF16)


def _rope_tables(length):
    rows = length // GRID_W
    r = np.repeat(np.arange(rows, dtype=np.float32), GRID_W)
    col = np.tile(np.arange(GRID_W, dtype=np.float32), rows)
    nf = RET_HEAD_DIM // 4
    inv = (np.float32(ROPE_BASE) ** (-np.arange(nf, dtype=np.float32) / np.float32(nf))).astype(np.float32)
    ar = r[:, None] * inv[None]
    ac = col[:, None] * inv[None]
    ang = np.concatenate([ar, ar, ac, ac], axis=-1).astype(np.float64)
    sign = np.where((np.arange(RET_HEAD_DIM) & nf) == 0, -1.0, 1.0)
    return (jnp.asarray(np.cos(ang).astype(np.float32)),
            jnp.asarray((np.sin(ang) * sign[None, :]).astype(np.float32)))


def _trunk_path(x, mod3, mod_row_of_batch, s0f, s0b, rope, lw):
    batch, seq_len, _ = x.shape
    x2d = x.reshape(batch * seq_len, D_MODEL)
    uf, q, k, v, sg, gf, gr = _inproj(x2d, mod3, lw["norm1_g"], lw["w_in"], seq_len, mod_row_of_batch, rope)
    r, s_f, s_b = _retention(q, k, v, sg, lw["dec"], lw["gn_g"], s0f, s0b, batch, seq_len)
    cs, cls = _dft_tables(seq_len)
    fmix = _fnet(uf, cs, cls, batch, seq_len)
    x1 = _merge(fmix, r, gf, gr, x2d, mod3, lw["w_four"], lw["w_ret"], lw["w_o"], seq_len, mod_row_of_batch)
    y = _moe(x1, mod3, lw, seq_len, mod_row_of_batch)
    return y.reshape(batch, seq_len, D_MODEL), s_f, s_b


def kernel(x_prompt, x_sample, state_ret_fwd, state_ret_bwd, c, c_ctx, w_ada, b_ada, norm1_g, norm2_g, w_in,
           ret_decay_fwd, ret_decay_bwd, ret_gn_g, w_four_out, w_ret_out, w_out, w_router, router_bias,
           w_exp_gate, w_exp_up, w_exp_down, w_shared_gate, w_shared_up, w_shared_down, final_norm_g):
    depth = w_ada.shape[0]
    assert depth == 1, "final norm is fused into the last layer's MoE kernel"
    n_ctx, n_lat = x_prompt.shape[0], x_sample.shape[0]
    cond = jnp.concatenate([c_ctx[None, :], c], axis=0)
    cond = jnp.pad(cond, ((0, (-cond.shape[0]) % 8), (0, 0)))
    rope = _rope_tables(x_sample.shape[1])
    zeros = jnp.zeros((n_ctx, N_RET_HEADS, RET_HEAD_DIM, RET_HEAD_DIM), F32)

    layer = 0
    mod = _ada(cond, w_ada[layer], b_ada[layer][None, :])
    mod3 = mod.reshape(mod.shape[0], 6, D_MODEL)
    dec = jnp.stack([ret_decay_fwd[layer], ret_decay_bwd[layer]], axis=1)
    lw = {
        "norm1_g": norm1_g[layer][None, :],
        "norm2_g": norm2_g[layer][None, :],
        "w_in": w_in[layer].astype(BF16),
        "dec": jnp.broadcast_to(dec[:, :, None], (N_RET_HEADS, 2, RET_HEAD_DIM)).astype(F32),
        "gn_g": ret_gn_g[layer][None, :],
        "w_four": w_four_out[layer].astype(BF16),
        "w_ret": w_ret_out[layer].astype(BF16),
        "w_o": w_out[layer].astype(BF16),
        "w_router_t": w_router[layer].T,
        "router_bias": router_bias[layer][:, None],
        "weg": w_exp_gate[layer],
        "weu": w_exp_up[layer],
        "wed": w_exp_down[layer],
        "wsg": w_shared_gate[layer].astype(BF16),
        "wsu": w_shared_up[layer].astype(BF16),
        "wsd": w_shared_down[layer].astype(BF16),
        "final_g": final_norm_g[None, :],
    }
    y_prompt, s_f, s_b = _trunk_path(x_prompt, mod3, lambda b: 0, zeros, zeros, None, lw)
    y_sample, _, _ = _trunk_path(x_sample, mod3, lambda b: 1 + b, state_ret_fwd[:, layer],
                                 state_ret_bwd[:, layer], rope, lw)
    return (y_prompt, y_sample, s_f[:, None], s_b[:, None])
```

```python
import functools
import math

import jax
import jax.numpy as jnp
import numpy as np
from jax import lax
from jax.experimental import pallas as pl
from jax.experimental.pallas import tpu as pltpu
from jax.experimental.pallas import tpu_sc as plsc

F32 = jnp.float32
BF16 = jnp.bfloat16

D_MODEL = 1024
GRID_W = 64
N_FOURIER_GROUPS = 8
FOURIER_GROUP_DIM = 128
N_RET_HEADS = 4
RET_HEAD_DIM = 128
RET_WIDTH = N_RET_HEADS * RET_HEAD_DIM
CHUNK = 128
N_EXPERTS = 64
N_EXPERT_GROUPS = 8
EXPERTS_PER_GROUP = N_EXPERTS // N_EXPERT_GROUPS
TOPK_GROUPS = 4
TOP_K = 8
EXPERT_DIM = 256
ROUTED_SCALE = 2.5
ROPE_BASE = 10000.0
EPS = 1e-6
Q_SCALE = RET_HEAD_DIM ** -0.5

_C_UF = (0, 1024)
_C_Q = (1024, 1536)
_C_K = (1536, 2048)
_C_V = (2048, 2560)
_C_G = (2560, 3072)
_C_GF = (3072, 4096)
_C_GR = (4096, 5120)

VMEM_LIMIT = 56 * 1024 * 1024

TM_PROJ = 512
FNET_ROWS = 256
TM_ROUTER = 1024
TM_FINAL = 512
EXPERT_ROWS = 512
TILES_PER_STEP = 2
ROW_SLABS = 4
SC_CORES = 2
SC_WORKERS = 32
SC_CHUNK = 128


def _silu(x):
    return x * jax.nn.sigmoid(x)


def _dot(a, b):
    return jnp.dot(a, b, preferred_element_type=F32)


def _rms_mod(x, g, shift, scale):
    ms = jnp.mean(x * x, axis=-1, keepdims=True)
    y = x * lax.rsqrt(ms + EPS) * g
    return y * (1.0 + scale) + shift


def _ada_kernel(cond_ref, w_ref, b_ref, o_ref):
    s = _silu(cond_ref[...]).astype(BF16)
    o_ref[...] = _dot(s, w_ref[...].astype(BF16)) + b_ref[...]


def _ada(cond, w_ada, b_ada):
    rows, n = cond.shape[0], w_ada.shape[1]
    tn = 1536
    return pl.pallas_call(
        _ada_kernel,
        grid=(n // tn,),
        in_specs=[pl.BlockSpec((rows, D_MODEL), lambda j: (0, 0)),
                  pl.BlockSpec((D_MODEL, tn), lambda j: (0, j)),
                  pl.BlockSpec((1, tn), lambda j: (0, j))],
        out_specs=pl.BlockSpec((rows, tn), lambda j: (0, j)),
        out_shape=jax.ShapeDtypeStruct((rows, n), F32),
        compiler_params=pltpu.CompilerParams(vmem_limit_bytes=VMEM_LIMIT),
        name="ada",
    )(cond, w_ada, b_ada)


def _rope_head(x, cos, sin_signed, first_half):
    partner = jnp.where(first_half, pltpu.roll(x, 96, 1), pltpu.roll(x, 32, 1))
    return x * cos + partner * sin_signed


def _inproj_kernel(*refs, use_rope):
    if use_rope:
        x_ref, mod_ref, g_ref, w_ref, cos_ref, sin_ref = refs[:6]
        outs = refs[6:]
    else:
        x_ref, mod_ref, g_ref, w_ref = refs[:4]
        outs = refs[4:]
    uf_o, q_o, k_o, v_o, sg_o, gf_o, gr_o = outs

    h = _rms_mod(x_ref[...], g_ref[...], mod_ref[0, 0:1, :], mod_ref[0, 1:2, :])
    hb = h.astype(BF16)

    def proj(cols):
        return _dot(hb, w_ref[:, cols[0]:cols[1]])

    uf_o[...] = proj(_C_UF).astype(BF16)
    q = proj(_C_Q)
    k = proj(_C_K)
    if use_rope:
        cos = cos_ref[...]
        sin_signed = sin_ref[...]
        lane = lax.broadcasted_iota(jnp.int32, cos.shape, 1)
        first_half = (lane & 32) == 0
        for hd in range(N_RET_HEADS):
            sl = slice(hd * RET_HEAD_DIM, (hd + 1) * RET_HEAD_DIM)
            q_o[:, sl] = (_rope_head(q[:, sl], cos, sin_signed, first_half) * Q_SCALE).astype(BF16)
            k_o[:, sl] = _rope_head(k[:, sl], cos, sin_signed, first_half).astype(BF16)
    else:
        q_o[...] = (q * Q_SCALE).astype(BF16)
        k_o[...] = k.astype(BF16)
    v_o[...] = proj(_C_V).astype(BF16)
    sg_o[...] = _silu(proj(_C_G)).astype(BF16)
    gf_o[...] = jax.nn.sigmoid(proj(_C_GF)).astype(BF16)
    gr_o[...] = jax.nn.sigmoid(proj(_C_GR)).astype(BF16)


def _inproj(x2d, mod3, norm_g, w_in_bf, seq_len, mod_row_of_batch, rope):
    t = x2d.shape[0]
    tm = TM_PROJ
    tiles_per_seq = max(seq_len // tm, 1)

    def mod_idx(i):
        return (mod_row_of_batch((i * tm) // seq_len), 0, 0)

    in_specs = [pl.BlockSpec((tm, D_MODEL), lambda i: (i, 0)),
                pl.BlockSpec((1, 6, D_MODEL), mod_idx),
                pl.BlockSpec((1, D_MODEL), lambda i: (0, 0)),
                pl.BlockSpec(w_in_bf.shape, lambda i: (0, 0))]
    args = [x2d, mod3, norm_g, w_in_bf]
    if rope is not None:
        in_specs += [pl.BlockSpec((tm, RET_HEAD_DIM), lambda i: (i % tiles_per_seq, 0))] * 2
        args += list(rope)
    widths = [1024, RET_WIDTH, RET_WIDTH, RET_WIDTH, RET_WIDTH, 1024, 1024]
    return pl.pallas_call(
        functools.partial(_inproj_kernel, use_rope=rope is not None),
        grid=(t // tm,),
        in_specs=in_specs,
        out_specs=[pl.BlockSpec((tm, w), lambda i: (i, 0)) for w in widths],
        out_shape=[jax.ShapeDtypeStruct((t, w), BF16) for w in widths],
        compiler_params=pltpu.CompilerParams(dimension_semantics=("parallel",),
                                             vmem_limit_bytes=VMEM_LIMIT),
        name="inproj",
    )(*args)


def _retention_kernel(q_ref, k_ref, v_ref, sg_ref, dec_ref, gn_ref, s0f_ref, s0b_ref,
                      r_ref, sfo_ref, sbo_ref, tab_scr, gc_scr):
    n_chunks = q_ref.shape[0] // CHUNK
    hd = RET_HEAD_DIM

    @pl.when(pl.program_id(0) == 0)
    def _():
        row = lax.broadcasted_iota(jnp.int32, (CHUNK, CHUNK), 0).astype(F32)
        col = lax.broadcasted_iota(jnp.int32, (CHUNK, CHUNK), 1).astype(F32)
        diff = row - col
        for h in range(N_RET_HEADS):
            dec = dec_ref[h]
            lg = jnp.minimum(dec, 0.0) - jnp.log1p(jnp.exp(-jnp.abs(dec)))
            lgf = lg[0:1, :]
            lgb = lg[1:2, :]
            tab_scr[h, 0] = jnp.exp(jnp.where(diff >= 0, lgf * diff, lgb * (-diff)))
            tab_scr[h, 1] = jnp.exp(lgf * (row + 1.0))
            tab_scr[h, 2] = jnp.exp(lgb * (CHUNK - row))
            tab_scr[h, 3] = jnp.exp(lgf * (CHUNK - 1.0 - col))
            tab_scr[h, 4] = jnp.exp(lgb * col)
            gc_scr[h] = jnp.exp(lg * CHUNK)

    def rows(n):
        return slice(n * CHUNK, (n + 1) * CHUNK)

    for h in range(N_RET_HEADS):
        cols = slice(h * hd, (h + 1) * hd)
        decay, qw_f, qw_b, kwt_f, kwt_b = (tab_scr[h, i] for i in range(5))
        gc = gc_scr[h]
        gc_f = gc[0:1, :]
        gc_b = gc[1:2, :]

        kv = []
        for n in range(n_chunks):
            kt = k_ref[rows(n), cols].astype(F32).T
            lhs = jnp.concatenate([kt * kwt_f, kt * kwt_b], axis=0).astype(BF16)
            kv.append(_dot(lhs, v_ref[rows(n), cols]))

        s = s0f_ref[h]
        prev_f = []
        for n in range(n_chunks):
            prev_f.append(s.astype(BF16))
            s = gc_f * s + kv[n][:hd]
        sfo_ref[h] = s
        s = s0b_ref[h]
        prev_b = [None] * n_chunks
        for n in reversed(range(n_chunks)):
            prev_b[n] = s.astype(BF16)
            s = gc_b * s + kv[n][hd:]
        sbo_ref[h] = s

        gn = gn_ref[:, cols]
        for n in range(n_chunks):
            qn = q_ref[rows(n), cols]
            qf = qn.astype(F32)
            scores = lax.dot_general(qn, k_ref[rows(n), cols], (((1,), (1,)), ((), ())),
                                     preferred_element_type=F32)
            lhs = jnp.concatenate([(scores * decay).astype(BF16), (qf * qw_f).astype(BF16),
                                   (qf * qw_b).astype(BF16)], axis=1)
            rhs = jnp.concatenate([v_ref[rows(n), cols], prev_f[n], prev_b[n]], axis=0)
            o = _dot(lhs, rhs)
            mu = jnp.mean(o, axis=-1, keepdims=True)
            d = o - mu
            var = jnp.mean(d * d, axis=-1, keepdims=True)
            on = d * lax.rsqrt(var + EPS) * gn
            r_ref[rows(n), cols] = (on * sg_ref[rows(n), cols].astype(F32)).astype(BF16)


def _retention(q, k, v, sg, dec, gn_g, s0f, s0b, batch, seq_len):
    hd = RET_HEAD_DIM
    tok_spec = pl.BlockSpec((seq_len, RET_WIDTH), lambda b: (b, 0))
    st_spec = pl.BlockSpec((None, N_RET_HEADS, hd, hd), lambda b: (b, 0, 0, 0))
    st_shape = jax.ShapeDtypeStruct((batch, N_RET_HEADS, hd, hd), F32)
    return pl.pallas_call(
        _retention_kernel,
        grid=(batch,),
        in_specs=[tok_spec, tok_spec, tok_spec, tok_spec,
                  pl.BlockSpec(dec.shape, lambda b: (0, 0, 0)),
                  pl.BlockSpec(gn_g.shape, lambda b: (0, 0)),
                  st_spec, st_spec],
        out_specs=[tok_spec, st_spec, st_spec],
        out_shape=[jax.ShapeDtypeStruct((batch * seq_len, RET_WIDTH), BF16), st_shape, st_shape],
        scratch_shapes=[pltpu.VMEM((N_RET_HEADS, 5, CHUNK, CHUNK), F32),
                        pltpu.VMEM((N_RET_HEADS, 2, hd), F32)],
        compiler_params=pltpu.CompilerParams(dimension_semantics=("arbitrary",),
                                             vmem_limit_bytes=VMEM_LIMIT),
        name="retention",
    )(q, k, v, sg, dec, gn_g, s0f, s0b)


def _fnet_kernel(uf_ref, cs_ref, cls_ref, o_ref, xcs_ref):
    seq_len = uf_ref.shape[0]
    gd = FOURIER_GROUP_DIM

    @pl.when(pl.program_id(1) == 0)
    def _():
        for g in range(N_FOURIER_GROUPS):
            x = _dot(uf_ref[:, g * gd:(g + 1) * gd], cs_ref[...])
            xcs_ref[0:seq_len, g * gd:(g + 1) * gd] = x[:, :gd].astype(BF16)
            xcs_ref[seq_len:2 * seq_len, g * gd:(g + 1) * gd] = x[:, gd:].astype(BF16)

    o_ref[...] = _dot(cls_ref[...], xcs_ref[...]).astype(BF16)


def _fnet(uf, cs, cls, batch, seq_len):
    rb = FNET_ROWS
    nr = seq_len // rb
    return pl.pallas_call(
        _fnet_kernel,
        grid=(batch, nr),
        in_specs=[pl.BlockSpec((seq_len, D_MODEL), lambda b, r: (b, 0)),
                  pl.BlockSpec(cs.shape, lambda b, r: (0, 0)),
                  pl.BlockSpec((rb, 2 * seq_len), lambda b, r: (r, 0))],
        out_specs=pl.BlockSpec((rb, D_MODEL), lambda b, r: (b * nr + r, 0)),
        out_shape=jax.ShapeDtypeStruct((batch * seq_len, D_MODEL), BF16),
        scratch_shapes=[pltpu.VMEM((2 * seq_len, D_MODEL), BF16)],
        compiler_params=pltpu.CompilerParams(dimension_semantics=("parallel", "arbitrary"),
                                             vmem_limit_bytes=VMEM_LIMIT),
        name="fnet",
    )(uf, cs, cls)


def _merge_kernel(fm_ref, r_ref, gf_ref, gr_ref, x_ref, mod_ref, wf_ref, wr_ref, wo_ref, o_ref):
    f_out = _dot(fm_ref[...], wf_ref[...])
    r_out = _dot(r_ref[...], wr_ref[...])
    merged = gf_ref[...].astype(F32) * f_out + gr_ref[...].astype(F32) * r_out
    mix = _dot(merged.astype(BF16), wo_ref[...])
    o_ref[...] = x_ref[...] + mod_ref[0, 2:3, :] * mix


def _merge(fmix, r, gf, gr, x2d, mod3, w_four, w_ret, w_o, seq_len, mod_row_of_batch):
    t = x2d.shape[0]
    tm = TM_PROJ

    def mod_idx(i):
        return (mod_row_of_batch((i * tm) // seq_len), 0, 0)

    def tok(w):
        return pl.BlockSpec((tm, w), lambda i: (i, 0))

    def full(a):
        return pl.BlockSpec(a.shape, lambda i: (0, 0))

    return pl.pallas_call(
        _merge_kernel,
        grid=(t // tm,),
        in_specs=[tok(D_MODEL), tok(RET_WIDTH), tok(D_MODEL), tok(D_MODEL), tok(D_MODEL),
                  pl.BlockSpec((1, 6, D_MODEL), mod_idx), full(w_four), full(w_ret), full(w_o)],
        out_specs=tok(D_MODEL),
        out_shape=jax.ShapeDtypeStruct((t, D_MODEL), F32),
        compiler_params=pltpu.CompilerParams(dimension_semantics=("parallel",),
                                             vmem_limit_bytes=VMEM_LIMIT),
        name="merge",
    )(fmix, r, gf, gr, x2d, mod3, w_four, w_ret, w_o)


def _pack_pair(lo_f32, hi_f32):
    lo = lax.bitcast_convert_type(lo_f32.astype(BF16).astype(F32), jnp.uint32)
    hi = lax.bitcast_convert_type(hi_f32.astype(BF16).astype(F32), jnp.uint32)
    return lax.bitcast_convert_type((lo >> 16) | hi, jnp.int32)


def _unpack_pair(words_i32):
    w = lax.bitcast_convert_type(words_i32, jnp.uint32)
    lo = lax.bitcast_convert_type(w << 16, F32)
    hi = lax.bitcast_convert_type(w & jnp.uint32(0xFFFF0000), F32)
    return lo, hi


def _load_token_words(ref, lead, n_tok):
    parts = []
    for s in range(ROW_SLABS):
        idx = (pl.ds(s, n_tok, stride=ROW_SLABS), slice(None))
        parts.append(ref[lead + idx] if lead else ref[idx])
    return jnp.concatenate(parts, axis=1)


def _store_token_words(ref, words, n_tok):
    for s in range(ROW_SLABS):
        ref[pl.ds(s, n_tok, stride=ROW_SLABS), :] = words[:, s * 128:(s + 1) * 128]


def _route(scores, biased):
    tokens = scores.shape[1]
    neg = -jnp.inf
    epg = EXPERTS_PER_GROUP
    iota_g = lax.broadcasted_iota(jnp.int32, (epg, tokens), 0).astype(F32)

    def pick_first_max(cur, iota, size):
        m = jnp.max(cur, axis=0, keepdims=True)
        idx = jnp.min(jnp.where(cur == m, iota, float(size)), axis=0, keepdims=True)
        return m, idx, iota == idx

    group_scores = []
    for g in range(N_EXPERT_GROUPS):
        vals = biased[g * epg:(g + 1) * epg, :]
        m1, _, hit = pick_first_max(vals, iota_g, epg)
        m2 = jnp.max(jnp.where(hit, neg, vals), axis=0, keepdims=True)
        group_scores.append(m1 + m2)
    cur = jnp.concatenate(group_scores, axis=0)
    group_sel = jnp.zeros_like(cur)
    for _ in range(TOPK_GROUPS):
        _, _, hit = pick_first_max(cur, iota_g, N_EXPERT_GROUPS)
        group_sel = jnp.where(hit, 1.0, group_sel)
        cur = jnp.where(hit, neg, cur)
    masked = jnp.concatenate(
        [jnp.where(group_sel[g:g + 1, :] > 0.0, biased[g * epg:(g + 1) * epg, :], neg)
         for g in range(N_EXPERT_GROUPS)], axis=0)
    iota_e = lax.broadcasted_iota(jnp.int32, masked.shape, 0).astype(F32)
    sel = jnp.zeros_like(masked)
    cur = masked
    picks = []
    for _ in range(TOP_K):
        _, idx, hit = pick_first_max(cur, iota_e, N_EXPERTS)
        picks.append(idx)
        sel = jnp.where(hit, 1.0, sel)
        cur = jnp.where(hit, neg, cur)
    w = scores * sel
    return w / jnp.sum(w, axis=0, keepdims=True) * ROUTED_SCALE, sel, picks


def _router_kernel(x_ref, mod_ref, g2_ref, wrt_ref, rb_ref, hp_ref, ek_ref, rk_ref, wt_ref, cnt_ref, run_scr):
    tm = x_ref.shape[0]

    @pl.when(pl.program_id(0) == 0)
    def _():
        run_scr[...] = jnp.zeros_like(run_scr)

    h = _rms_mod(x_ref[...], g2_ref[...], mod_ref[0, 3:4, :], mod_ref[0, 4:5, :])
    half = D_MODEL // 2
    _store_token_words(hp_ref, _pack_pair(h[:, :half], h[:, half:]), tm)

    logits_t = lax.dot_general(wrt_ref[...], h, (((1,), (1,)), ((), ())),
                               precision=lax.Precision.HIGHEST, preferred_element_type=F32)
    scores = jax.nn.sigmoid(logits_t)
    comb_t, sel, picks = _route(scores, scores + rb_ref[...])

    earlier = (lax.broadcasted_iota(jnp.int32, (tm, tm), 0) < lax.broadcasted_iota(jnp.int32, (tm, tm), 1))
    rank_t = _dot(sel.astype(BF16), jnp.where(earlier, 1.0, 0.0).astype(BF16)) + run_scr[...]
    run_scr[...] += jnp.sum(sel, axis=1, keepdims=True)
    cnt_ref[...] = jnp.broadcast_to(run_scr[...], cnt_ref.shape)

    iota_e = lax.broadcasted_iota(jnp.int32, sel.shape, 0).astype(F32)
    ranks, weights = [], []
    for idx in picks:
        hit = iota_e == idx
        ranks.append(jnp.sum(jnp.where(hit, rank_t, 0.0), axis=0, keepdims=True))
        weights.append(jnp.sum(jnp.where(hit, comb_t, 0.0), axis=0, keepdims=True))
    ek_ref[...] = jnp.concatenate(picks, axis=0).astype(jnp.int32)
    rk_ref[...] = jnp.concatenate(ranks, axis=0).astype(jnp.int32)
    w_pad = jnp.concatenate(weights + [jnp.zeros((128 - TOP_K, tm), F32)], axis=0)
    wt_ref[...] = w_pad.T


def _router(x1, mod3, norm2_g, w_router_t, router_bias, seq_len, mod_row_of_batch):
    t = x1.shape[0]
    tm = TM_ROUTER

    def mod_idx(i):
        return (mod_row_of_batch((i * tm) // seq_len), 0, 0)

    def full(a):
        return pl.BlockSpec(a.shape, lambda i: (0,) * a.ndim)

    return pl.pallas_call(
        _router_kernel,
        grid=(t // tm,),
        in_specs=[pl.BlockSpec((tm, D_MODEL), lambda i: (i, 0)),
                  pl.BlockSpec((1, 6, D_MODEL), mod_idx),
                  full(norm2_g), full(w_router_t), full(router_bias)],
        out_specs=[pl.BlockSpec((tm * ROW_SLABS, 128), lambda i: (i, 0)),
                   pl.BlockSpec((TOP_K, tm), lambda i: (0, i)),
                   pl.BlockSpec((TOP_K, tm), lambda i: (0, i)),
                   pl.BlockSpec((tm, 128), lambda i: (i, 0)),
                   pl.BlockSpec((N_EXPERTS, 128), lambda i: (0, 0))],
        out_shape=[jax.ShapeDtypeStruct((t * ROW_SLABS, 128), jnp.int32),
                   jax.ShapeDtypeStruct((TOP_K, t), jnp.int32),
                   jax.ShapeDtypeStruct((TOP_K, t), jnp.int32),
                   jax.ShapeDtypeStruct((t, 128), F32),
                   jax.ShapeDtypeStruct((N_EXPERTS, 128), F32)],
        scratch_shapes=[pltpu.VMEM((N_EXPERTS, 1), F32)],
        compiler_params=pltpu.CompilerParams(dimension_semantics=("arbitrary",),
                                             vmem_limit_bytes=VMEM_LIMIT),
        name="router",
    )(x1, mod3, norm2_g, w_router_t, router_bias)


def _plan_kernel(ek_ref, rk_ref, cnt_ref, pos_ref, texp_ref, nused_ref, tend_ref):
    rows = float(EXPERT_ROWS)
    cnt = cnt_ref[:, 0:1]
    tiles = jnp.floor((cnt + (rows - 1.0)) / rows)
    before = (lax.broadcasted_iota(jnp.int32, (N_EXPERTS, N_EXPERTS), 1)
              < lax.broadcasted_iota(jnp.int32, (N_EXPERTS, N_EXPERTS), 0))
    tile_start = jnp.dot(jnp.where(before, 1.0, 0.0), jnp.broadcast_to(tiles, (N_EXPERTS, 128)),
                         precision=lax.Precision.HIGHEST, preferred_element_type=F32)[:, 0:1]
    tile_end = tile_start + tiles
    row_start = tile_start * rows

    ek = ek_ref[...]
    pos = rk_ref[...].astype(F32)
    tile_id = lax.broadcasted_iota(jnp.int32, texp_ref.shape, 1).astype(F32)
    texp = jnp.zeros(texp_ref.shape, F32)
    for e in range(N_EXPERTS):
        pos = pos + jnp.where(ek == e, row_start[e:e + 1, :], 0.0)
        texp = texp + jnp.where(tile_id >= tile_end[e:e + 1, :], 1.0, 0.0)
    pos_ref[...] = pos.astype(jnp.int32)
    texp_ref[...] = jnp.minimum(texp, N_EXPERTS - 1.0).astype(jnp.int32)
    nused_ref[...] = jnp.broadcast_to(tile_end[N_EXPERTS - 1:N_EXPERTS, :], nused_ref.shape).astype(jnp.int32)
    tend_ref[...] = jnp.broadcast_to(tile_end, tend_ref.shape).astype(jnp.int32)


def _plan(ek, rk, cnt, n_tiles_pad):
    t = ek.shape[1]

    def full(shape):
        return pl.BlockSpec(shape, lambda: (0,) * len(shape))

    return pl.pallas_call(
        _plan_kernel,
        in_specs=[full(ek.shape), full(rk.shape), full(cnt.shape)],
        out_specs=[full((TOP_K, t)), full((1, n_tiles_pad)), full((1, 128)), full((N_EXPERTS, 128))],
        out_shape=[jax.ShapeDtypeStruct((TOP_K, t), jnp.int32),
                   jax.ShapeDtypeStruct((1, n_tiles_pad), jnp.int32),
                   jax.ShapeDtypeStruct((1, 128), jnp.int32),
                   jax.ShapeDtypeStruct((N_EXPERTS, 128), jnp.int32)],
        compiler_params=pltpu.CompilerParams(vmem_limit_bytes=VMEM_LIMIT),
        name="plan",
    )(ek, rk, cnt)


def _sc_mesh():
    return plsc.VectorSubcoreMesh(core_axis_name="c", subcore_axis_name="s")


def _sc_dispatch(rows, pos3, n_out):
    t = rows.shape[0]
    ch = SC_CHUNK
    per_w = (t // ch) // SC_WORKERS

    @functools.partial(
        pl.kernel, out_type=jax.ShapeDtypeStruct((n_out,) + rows.shape[1:], jnp.int32), mesh=_sc_mesh(),
        scratch_types=[pltpu.VMEM((TOP_K, ch), jnp.int32), pltpu.VMEM((ch,) + rows.shape[1:], jnp.int32),
                       pltpu.SemaphoreType.DMA])
    def k(rows_hbm, pos_hbm, out_hbm, idx_v, rows_v, sem):
        wid = lax.axis_index("s") * SC_CORES + lax.axis_index("c")

        @pl.loop(0, per_w)
        def _(j):
            c = wid * per_w + j
            pltpu.sync_copy(pos_hbm.at[c], idx_v)
            pltpu.sync_copy(rows_hbm.at[pl.ds(c * ch, ch)], rows_v)
            copies = [pltpu.async_copy(rows_v, out_hbm.at[idx_v.at[kk]], sem) for kk in range(TOP_K)]
            for cp in copies:
                cp.wait()

    return k(rows, pos3)


def _sc_gather(table, pos3, t):
    ch = SC_CHUNK
    per_w = (t // ch) // SC_WORKERS

    @functools.partial(
        pl.kernel, out_type=jax.ShapeDtypeStruct((TOP_K, t) + table.shape[1:], jnp.int32), mesh=_sc_mesh(),
        scratch_types=[pltpu.VMEM((TOP_K, ch), jnp.int32), pltpu.VMEM((ch,) + table.shape[1:], jnp.int32),
                       pltpu.SemaphoreType.DMA])
    def k(tab_hbm, pos_hbm, out_hbm, idx_v, rows_v, sem):
        wid = lax.axis_index("s") * SC_CORES + lax.axis_index("c")

        @pl.loop(0, per_w)
        def _(j):
            c = wid * per_w + j
            pltpu.sync_copy(pos_hbm.at[c], idx_v)
            for kk in range(TOP_K):
                pltpu.async_copy(tab_hbm.at[idx_v.at[kk]], rows_v, sem).wait()
                pltpu.sync_copy(rows_v, out_hbm.at[kk, pl.ds(c * ch, ch)])

    return k(table, pos3)


def _experts_kernel(texp_ref, nused_ref, tend_ref, xs_ref, weg_hbm, weu_hbm, wed_hbm, ys_ref,
                    wg_scr, wu_scr, wd_scr, wg_buf, wu_buf, wd_buf, sem, group_scr):
    step = pl.program_id(0)
    rows = EXPERT_ROWS
    half = D_MODEL // 2
    n_used = nused_ref[0]

    def weight_copies(e, slot):
        return [pltpu.make_async_copy(weg_hbm.at[e], wg_buf.at[slot], sem.at[slot, 0]),
                pltpu.make_async_copy(weu_hbm.at[e], wu_buf.at[slot], sem.at[slot, 1]),
                pltpu.make_async_copy(wed_hbm.at[e], wd_buf.at[slot], sem.at[slot, 2])]

    @pl.when(step == 0)
    def _():
        group_scr[0] = 0
        for cp in weight_copies(texp_ref[0], 0):
            cp.start()

    def row_tile(tile, x_view, y_view):
        expert = texp_ref[tile]
        used = tile < n_used
        new_expert = (tile == 0) | (expert != texp_ref[jnp.maximum(tile - 1, 0)])

        @pl.when(used & new_expert)
        def _():
            group = group_scr[0]
            slot = group % 2
            next_tile = tend_ref[expert]

            @pl.when(next_tile < n_used)
            def _():
                for cp in weight_copies(texp_ref[next_tile], 1 - slot):
                    cp.start()

            for cp in weight_copies(expert, slot):
                cp.wait()
            wg_scr[...] = wg_buf[slot].astype(BF16)
            wu_scr[...] = wu_buf[slot].astype(BF16)
            wd_scr[...] = wd_buf[slot].astype(BF16)
            group_scr[0] = group + 1

        @pl.when(used)
        def _():
            lo, hi = _unpack_pair(_load_token_words(x_view, (), rows))
            lo = lo.astype(BF16)
            hi = hi.astype(BF16)
            g = _dot(lo, wg_scr[0:half, :]) + _dot(hi, wg_scr[half:D_MODEL, :])
            u = _dot(lo, wu_scr[0:half, :]) + _dot(hi, wu_scr[half:D_MODEL, :])
            y = _dot((_silu(g) * u).astype(BF16), wd_scr[...])
            _store_token_words(y_view, _pack_pair(y[:, :half], y[:, half:]), rows)

        @pl.when(jnp.logical_not(used) & (step == (n_used - 1) // TILES_PER_STEP))
        def _():
            y_view[...] = jnp.zeros_like(y_view)

    for s in range(TILES_PER_STEP):
        view = pl.ds(s * rows * ROW_SLABS, rows * ROW_SLABS)
        row_tile(step * TILES_PER_STEP + s, xs_ref.at[view], ys_ref.at[view])


def _experts(texp, nused, tend, xs2d, weg, weu, wed, n_tiles):
    block = (TILES_PER_STEP * EXPERT_ROWS * ROW_SLABS, 128)
    hbm = pl.BlockSpec(memory_space=pl.ANY)

    def block_idx(j, te, nu, tn):
        return (jnp.minimum(j, (nu[0] - 1) // TILES_PER_STEP), 0)

    grid_spec = pltpu.PrefetchScalarGridSpec(
        num_scalar_prefetch=3,
        grid=(n_tiles // TILES_PER_STEP,),
        in_specs=[pl.BlockSpec(block, block_idx), hbm, hbm, hbm],
        out_specs=pl.BlockSpec(block, block_idx),
        scratch_shapes=[pltpu.VMEM((D_MODEL, EXPERT_DIM), BF16),
                        pltpu.VMEM((D_MODEL, EXPERT_DIM), BF16),
                        pltpu.VMEM((EXPERT_DIM, D_MODEL), BF16),
                        pltpu.VMEM((2, D_MODEL, EXPERT_DIM), F32),
                        pltpu.VMEM((2, D_MODEL, EXPERT_DIM), F32),
                        pltpu.VMEM((2, EXPERT_DIM, D_MODEL), F32),
                        pltpu.SemaphoreType.DMA((2, 3)),
                        pltpu.SMEM((1,), jnp.int32)],
    )
    return pl.pallas_call(
        _experts_kernel,
        grid_spec=grid_spec,
        out_shape=jax.ShapeDtypeStruct(xs2d.shape, jnp.int32),
        compiler_params=pltpu.CompilerParams(dimension_semantics=("arbitrary",),
                                             vmem_limit_bytes=VMEM_LIMIT),
        name="experts",
    )(texp, nused, tend, xs2d, weg, weu, wed)


def _final_kernel(x_ref, hp_ref, y8_ref, wt_ref, mod_ref, wsg_ref, wsu_ref, wsd_ref, fng_ref, o_ref):
    tm = x_ref.shape[0]
    lo, hi = _unpack_pair(_load_token_words(hp_ref, (), tm))
    hb = jnp.concatenate([lo, hi], axis=1).astype(BF16)
    shared = _dot((_silu(_dot(hb, wsg_ref[...])) * _dot(hb, wsu_ref[...])).astype(BF16), wsd_ref[...])
    wt = wt_ref[...]
    r_lo = jnp.zeros((tm, D_MODEL // 2), F32)
    r_hi = jnp.zeros((tm, D_MODEL // 2), F32)
    for k in range(TOP_K):
        lo, hi = _unpack_pair(_load_token_words(y8_ref, (k,), tm))
        wk = wt[:, k:k + 1]
        r_lo = r_lo + wk * lo
        r_hi = r_hi + wk * hi
    routed = jnp.concatenate([r_lo, r_hi], axis=1)
    y = x_ref[...] + mod_ref[0, 5:6, :] * (routed + shared)
    ms = jnp.mean(y * y, axis=-1, keepdims=True)
    o_ref[...] = y * lax.rsqrt(ms + EPS) * fng_ref[...]


def _final(x1, hp2d, y8, wtok, mod3, wsg, wsu, wsd, final_g, seq_len, mod_row_of_batch):
    t = x1.shape[0]
    tm = TM_FINAL

    def mod_idx(i):
        return (mod_row_of_batch((i * tm) // seq_len), 0, 0)

    def full(a):
        return pl.BlockSpec(a.shape, lambda i: (0,) * a.ndim)

    return pl.pallas_call(
        _final_kernel,
        grid=(t // tm,),
        in_specs=[pl.BlockSpec((tm, D_MODEL), lambda i: (i, 0)),
                  pl.BlockSpec((tm * ROW_SLABS, 128), lambda i: (i, 0)),
                  pl.BlockSpec((TOP_K, tm * ROW_SLABS, 128), lambda i: (0, i, 0)),
                  pl.BlockSpec((tm, 128), lambda i: (i, 0)),
                  pl.BlockSpec((1, 6, D_MODEL), mod_idx),
                  full(wsg), full(wsu), full(wsd), full(final_g)],
        out_specs=pl.BlockSpec((tm, D_MODEL), lambda i: (i, 0)),
        out_shape=jax.ShapeDtypeStruct((t, D_MODEL), F32),
        compiler_params=pltpu.CompilerParams(dimension_semantics=("parallel",),
                                             vmem_limit_bytes=VMEM_LIMIT),
        name="final",
    )(x1, hp2d, y8, wtok, mod3, wsg, wsu, wsd, final_g)


def _moe(x1, mod3, lw, seq_len, mod_row_of_batch):
    t = x1.shape[0]
    n_tiles = TOP_K * t // EXPERT_ROWS + N_EXPERTS
    n_tiles_pad = -(-n_tiles // 128) * 128
    hp2d, ek, rk, wtok, cnt = _router(x1, mod3, lw["norm2_g"], lw["w_router_t"], lw["router_bias"],
                                      seq_len, mod_row_of_batch)
    pos, texp, nused, tend = _plan(ek, rk, cnt, n_tiles_pad)
    pos3 = pos.reshape(TOP_K, t // SC_CHUNK, SC_CHUNK).transpose(1, 0, 2)
    xs = _sc_dispatch(hp2d.reshape(t, ROW_SLABS, 128), pos3, n_tiles * EXPERT_ROWS)
    ys2d = _experts(texp.reshape(-1), nused.reshape(-1), tend[:, 0], xs.reshape(-1, 128),
                    lw["weg"], lw["weu"], lw["wed"], n_tiles)
    y8 = _sc_gather(ys2d.reshape(-1, ROW_SLABS, 128), pos3, t)
    return _final(x1, hp2d, y8.reshape(TOP_K, t * ROW_SLABS, 128), wtok, mod3,
                  lw["wsg"], lw["wsu"], lw["wsd"], lw["final_g"], seq_len, mod_row_of_batch)


def _dft_tables(seq_len):
    gd = FOURIER_GROUP_DIM
    kc = np.arange(gd)
    ang_c = ((kc[:, None] * kc[None, :]) % gd) * (2.0 * math.pi / gd)
    cs = np.concatenate([np.cos(ang_c), np.sin(ang_c)], axis=1) * (gd ** -0.5)
    kl = np.arange(seq_len)
    ang_l = ((kl[:, None] * kl[None, :]) % seq_len) * (2.0 * math.pi / seq_len)
    cls = np.concatenate([np.cos(ang_l), -np.sin(ang_l)], axis=1) * (seq_len ** -0.5)
    return jnp.asarray(cs.astype(np.float32), dtype=BF16), jnp.asarray(cls.astype(np.float32), dtype=BF16)


def _rope_tables(length):
    rows = length // GRID_W
    r = np.repeat(np.arange(rows, dtype=np.float32), GRID_W)
    col = np.tile(np.arange(GRID_W, dtype=np.float32), rows)
    nf = RET_HEAD_DIM // 4
    inv = (np.float32(ROPE_BASE) ** (-np.arange(nf, dtype=np.float32) / np.float32(nf))).astype(np.float32)
    ar = r[:, None] * inv[None]
    ac = col[:, None] * inv[None]
    ang = np.concatenate([ar, ar, ac, ac], axis=-1).astype(np.float64)
    sign = np.where((np.arange(RET_HEAD_DIM) & nf) == 0, -1.0, 1.0)
    return (jnp.asarray(np.cos(ang).astype(np.float32)),
            jnp.asarray((np.sin(ang) * sign[None, :]).astype(np.float32)))


def _trunk_path(x, mod3, mod_row_of_batch, s0f, s0b, rope, lw):
    batch, seq_len, _ = x.shape
    x2d = x.reshape(batch * seq_len, D_MODEL)
    uf, q, k, v, sg, gf, gr = _inproj(x2d, mod3, lw["norm1_g"], lw["w_in"], seq_len, mod_row_of_batch, rope)
    r, s_f, s_b = _retention(q, k, v, sg, lw["dec"], lw["gn_g"], s0f, s0b, batch, seq_len)
    cs, cls = _dft_tables(seq_len)
    fmix = _fnet(uf, cs, cls, batch, seq_len)
    x1 = _merge(fmix, r, gf, gr, x2d, mod3, lw["w_four"], lw["w_ret"], lw["w_o"], seq_len, mod_row_of_batch)
    y = _moe(x1, mod3, lw, seq_len, mod_row_of_batch)
    return y.reshape(batch, seq_len, D_MODEL), s_f, s_b


def kernel(x_prompt, x_sample, state_ret_fwd, state_ret_bwd, c, c_ctx, w_ada, b_ada, norm1_g, norm2_g, w_in,
           ret_decay_fwd, ret_decay_bwd, ret_gn_g, w_four_out, w_ret_out, w_out, w_router, router_bias,
           w_exp_gate, w_exp_up, w_exp_down, w_shared_gate, w_shared_up, w_shared_down, final_norm_g):
    depth = w_ada.shape[0]
    assert depth == 1, "final norm is fused into the last layer's MoE kernel"
    n_ctx, n_lat = x_prompt.shape[0], x_sample.shape[0]
    cond = jnp.concatenate([c_ctx[None, :], c], axis=0)
    cond = jnp.pad(cond, ((0, (-cond.shape[0]) % 8), (0, 0)))
    rope = _rope_tables(x_sample.shape[1])
    zeros = jnp.zeros((n_ctx, N_RET_HEADS, RET_HEAD_DIM, RET_HEAD_DIM), F32)

    layer = 0
    mod = _ada(cond, w_ada[layer], b_ada[layer][None, :])
    mod3 = mod.reshape(mod.shape[0], 6, D_MODEL)
    dec = jnp.stack([ret_decay_fwd[layer], ret_decay_bwd[layer]], axis=1)
    lw = {
        "norm1_g": norm1_g[layer][None, :],
        "norm2_g": norm2_g[layer][None, :],
        "w_in": w_in[layer].astype(BF16),
        "dec": jnp.broadcast_to(dec[:, :, None], (N_RET_HEADS, 2, RET_HEAD_DIM)).astype(F32),
        "gn_g": ret_gn_g[layer][None, :],
        "w_four": w_four_out[layer].astype(BF16),
        "w_ret": w_ret_out[layer].astype(BF16),
        "w_o": w_out[layer].astype(BF16),
        "w_router_t": w_router[layer].T,
        "router_bias": router_bias[layer][:, None],
        "weg": w_exp_gate[layer],
        "weu": w_exp_up[layer],
        "wed": w_exp_down[layer],
        "wsg": w_shared_gate[layer].astype(BF16),
        "wsu": w_shared_up[layer].astype(BF16),
        "wsd": w_shared_down[layer].astype(BF16),
        "final_g": final_norm_g[None, :],
    }
    y_prompt, s_f, s_b = _trunk_path(x_prompt, mod3, lambda b: 0, zeros, zeros, None, lw)
    y_sample, _, _ = _trunk_path(x_sample, mod3, lambda b: 1 + b, state_ret_fwd[:, layer],
                                 state_ret_bwd[:, layer], rope, lw)
    return (y_prompt, y_sample, s_f[:, None], s_b[:, None])
```

```python
import functools
import math

import jax
import jax.numpy as jnp
import numpy as np
from jax import lax
from jax.experimental import pallas as pl
from jax.experimental.pallas import tpu as pltpu
from jax.experimental.pallas import tpu_sc as plsc

F32 = jnp.float32
BF16 = jnp.bfloat16

D_MODEL = 1024
GRID_W = 64
N_FOURIER_GROUPS = 8
FOURIER_GROUP_DIM = 128
N_RET_HEADS = 4
RET_HEAD_DIM = 128
RET_WIDTH = N_RET_HEADS * RET_HEAD_DIM
CHUNK = 128
N_EXPERTS = 64
N_EXPERT_GROUPS = 8
EXPERTS_PER_GROUP = N_EXPERTS // N_EXPERT_GROUPS
TOPK_GROUPS = 4
TOP_K = 8
EXPERT_DIM = 256
ROUTED_SCALE = 2.5
ROPE_BASE = 10000.0
EPS = 1e-6
Q_SCALE = RET_HEAD_DIM ** -0.5

_C_UF = (0, 1024)
_C_Q = (1024, 1536)
_C_K = (1536, 2048)
_C_V = (2048, 2560)
_C_G = (2560, 3072)
_C_GF = (3072, 4096)
_C_GR = (4096, 5120)

VMEM_LIMIT = 56 * 1024 * 1024

TM_PROJ = 512
FNET_ROWS = 256
RET_ALL_HEADS_MAX_LEN = 512
TM_ROUTER = 1024
TM_FINAL = 512
EXPERT_ROWS = 512
TILES_PER_STEP = 2
ROW_SLABS = 4
SC_CORES = 2
SC_WORKERS = 32
SC_CHUNK = 128


def _silu(x):
    return x * jax.nn.sigmoid(x)


def _dot(a, b):
    return jnp.dot(a, b, preferred_element_type=F32)


def _rms_mod(x, g, shift, scale):
    ms = jnp.mean(x * x, axis=-1, keepdims=True)
    y = x * lax.rsqrt(ms + EPS) * g
    return y * (1.0 + scale) + shift


def _ada_kernel(cond_ref, w_ref, b_ref, o_ref):
    s = _silu(cond_ref[...]).astype(BF16)
    o_ref[...] = _dot(s, w_ref[...].astype(BF16)) + b_ref[...]


def _ada(cond, w_ada, b_ada):
    rows, n = cond.shape[0], w_ada.shape[1]
    tn = 1536
    return pl.pallas_call(
        _ada_kernel,
        grid=(n // tn,),
        in_specs=[pl.BlockSpec((rows, D_MODEL), lambda j: (0, 0)),
                  pl.BlockSpec((D_MODEL, tn), lambda j: (0, j)),
                  pl.BlockSpec((1, tn), lambda j: (0, j))],
        out_specs=pl.BlockSpec((rows, tn), lambda j: (0, j)),
        out_shape=jax.ShapeDtypeStruct((rows, n), F32),
        compiler_params=pltpu.CompilerParams(vmem_limit_bytes=VMEM_LIMIT),
        name="ada",
    )(cond, w_ada, b_ada)


def _rope_head(x, cos, sin_signed, first_half):
    partner = jnp.where(first_half, pltpu.roll(x, 96, 1), pltpu.roll(x, 32, 1))
    return x * cos + partner * sin_signed


def _inproj_kernel(*refs, use_rope):
    if use_rope:
        x_ref, mod_ref, g_ref, w_ref, cos_ref, sin_ref = refs[:6]
        outs = refs[6:]
    else:
        x_ref, mod_ref, g_ref, w_ref = refs[:4]
        outs = refs[4:]
    uf_o, q_o, k_o, v_o, sg_o, gf_o, gr_o = outs

    h = _rms_mod(x_ref[...], g_ref[...], mod_ref[0, 0:1, :], mod_ref[0, 1:2, :])
    hb = h.astype(BF16)

    def proj(cols):
        return _dot(hb, w_ref[:, cols[0]:cols[1]])

    uf_o[...] = proj(_C_UF).astype(BF16)
    q = proj(_C_Q)
    k = proj(_C_K)
    if use_rope:
        cos = cos_ref[...]
        sin_signed = sin_ref[...]
        lane = lax.broadcasted_iota(jnp.int32, cos.shape, 1)
        first_half = (lane & 32) == 0
        for hd in range(N_RET_HEADS):
            sl = slice(hd * RET_HEAD_DIM, (hd + 1) * RET_HEAD_DIM)
            q_o[:, sl] = (_rope_head(q[:, sl], cos, sin_signed, first_half) * Q_SCALE).astype(BF16)
            k_o[:, sl] = _rope_head(k[:, sl], cos, sin_signed, first_half).astype(BF16)
    else:
        q_o[...] = (q * Q_SCALE).astype(BF16)
        k_o[...] = k.astype(BF16)
    v_o[...] = proj(_C_V).astype(BF16)
    sg_o[...] = _silu(proj(_C_G)).astype(BF16)
    gf_o[...] = jax.nn.sigmoid(proj(_C_GF)).astype(BF16)
    gr_o[...] = jax.nn.sigmoid(proj(_C_GR)).astype(BF16)


def _inproj(x2d, mod3, norm_g, w_in_bf, seq_len, mod_row_of_batch, rope):
    t = x2d.shape[0]
    tm = TM_PROJ
    tiles_per_seq = max(seq_len // tm, 1)

    def mod_idx(i):
        return (mod_row_of_batch((i * tm) // seq_len), 0, 0)

    in_specs = [pl.BlockSpec((tm, D_MODEL), lambda i: (i, 0)),
                pl.BlockSpec((1, 6, D_MODEL), mod_idx),
                pl.BlockSpec((1, D_MODEL), lambda i: (0, 0)),
                pl.BlockSpec(w_in_bf.shape, lambda i: (0, 0))]
    args = [x2d, mod3, norm_g, w_in_bf]
    if rope is not None:
        in_specs += [pl.BlockSpec((tm, RET_HEAD_DIM), lambda i: (i % tiles_per_seq, 0))] * 2
        args += list(rope)
    widths = [1024, RET_WIDTH, RET_WIDTH, RET_WIDTH, RET_WIDTH, 1024, 1024]
    return pl.pallas_call(
        functools.partial(_inproj_kernel, use_rope=rope is not None),
        grid=(t // tm,),
        in_specs=in_specs,
        out_specs=[pl.BlockSpec((tm, w), lambda i: (i, 0)) for w in widths],
        out_shape=[jax.ShapeDtypeStruct((t, w), BF16) for w in widths],
        compiler_params=pltpu.CompilerParams(dimension_semantics=("parallel",),
                                             vmem_limit_bytes=VMEM_LIMIT),
        name="inproj",
    )(*args)


def _retention_kernel(q_ref, k_ref, v_ref, sg_ref, dec_ref, gn_ref, s0f_ref, s0b_ref,
                      r_ref, sfo_ref, sbo_ref, tab_scr, gc_scr):
    n_chunks = q_ref.shape[0] // CHUNK
    hd = RET_HEAD_DIM
    heads_here = q_ref.shape[1] // hd
    first_head = pl.program_id(1) * heads_here

    @pl.when((pl.program_id(0) == 0) & (pl.program_id(1) == 0))
    def _():
        row = lax.broadcasted_iota(jnp.int32, (CHUNK, CHUNK), 0).astype(F32)
        col = lax.broadcasted_iota(jnp.int32, (CHUNK, CHUNK), 1).astype(F32)
        diff = row - col
        for h in range(N_RET_HEADS):
            dec = dec_ref[h]
            lg = jnp.minimum(dec, 0.0) - jnp.log1p(jnp.exp(-jnp.abs(dec)))
            lgf = lg[0:1, :]
            lgb = lg[1:2, :]
            tab_scr[h, 0] = jnp.exp(jnp.where(diff >= 0, lgf * diff, lgb * (-diff)))
            tab_scr[h, 1] = jnp.exp(lgf * (row + 1.0))
            tab_scr[h, 2] = jnp.exp(lgb * (CHUNK - row))
            tab_scr[h, 3] = jnp.exp(lgf * (CHUNK - 1.0 - col))
            tab_scr[h, 4] = jnp.exp(lgb * col)
            gc_scr[h] = jnp.exp(lg * CHUNK)

    def rows(n):
        return slice(n * CHUNK, (n + 1) * CHUNK)

    for h in range(heads_here):
        cols = slice(h * hd, (h + 1) * hd)
        head = first_head + h
        decay, qw_f, qw_b, kwt_f, kwt_b = (tab_scr[head, i] for i in range(5))
        gc = gc_scr[head]
        gc_f = gc[0:1, :]
        gc_b = gc[1:2, :]

        kv = []
        for n in range(n_chunks):
            kt = k_ref[rows(n), cols].astype(F32).T
            lhs = jnp.concatenate([kt * kwt_f, kt * kwt_b], axis=0).astype(BF16)
            kv.append(_dot(lhs, v_ref[rows(n), cols]))

        s = s0f_ref[h]
        prev_f = []
        for n in range(n_chunks):
            prev_f.append(s.astype(BF16))
            s = gc_f * s + kv[n][:hd]
        sfo_ref[h] = s
        s = s0b_ref[h]
        prev_b = [None] * n_chunks
        for n in reversed(range(n_chunks)):
            prev_b[n] = s.astype(BF16)
            s = gc_b * s + kv[n][hd:]
        sbo_ref[h] = s

        gn = gn_ref[:, cols]
        for n in range(n_chunks):
            qn = q_ref[rows(n), cols]
            qf = qn.astype(F32)
            scores = lax.dot_general(qn, k_ref[rows(n), cols], (((1,), (1,)), ((), ())),
                                     preferred_element_type=F32)
            lhs = jnp.concatenate([(scores * decay).astype(BF16), (qf * qw_f).astype(BF16),
                                   (qf * qw_b).astype(BF16)], axis=1)
            rhs = jnp.concatenate([v_ref[rows(n), cols], prev_f[n], prev_b[n]], axis=0)
            o = _dot(lhs, rhs)
            mu = jnp.mean(o, axis=-1, keepdims=True)
            d = o - mu
            var = jnp.mean(d * d, axis=-1, keepdims=True)
            on = d * lax.rsqrt(var + EPS) * gn
            r_ref[rows(n), cols] = (on * sg_ref[rows(n), cols].astype(F32)).astype(BF16)


def _retention(q, k, v, sg, dec, gn_g, s0f, s0b, batch, seq_len):
    hd = RET_HEAD_DIM
    heads_per_step = N_RET_HEADS if seq_len <= RET_ALL_HEADS_MAX_LEN else 1
    width = heads_per_step * hd
    tok_spec = pl.BlockSpec((seq_len, width), lambda b, g: (b, g))
    st_spec = pl.BlockSpec((None, heads_per_step, hd, hd), lambda b, g: (b, g, 0, 0))
    st_shape = jax.ShapeDtypeStruct((batch, N_RET_HEADS, hd, hd), F32)
    return pl.pallas_call(
        _retention_kernel,
        grid=(batch, N_RET_HEADS // heads_per_step),
        in_specs=[tok_spec, tok_spec, tok_spec, tok_spec,
                  pl.BlockSpec(dec.shape, lambda b, g: (0, 0, 0)),
                  pl.BlockSpec((1, width), lambda b, g: (0, g)),
                  st_spec, st_spec],
        out_specs=[tok_spec, st_spec, st_spec],
        out_shape=[jax.ShapeDtypeStruct((batch * seq_len, RET_WIDTH), BF16), st_shape, st_shape],
        scratch_shapes=[pltpu.VMEM((N_RET_HEADS, 5, CHUNK, CHUNK), F32),
                        pltpu.VMEM((N_RET_HEADS, 2, hd), F32)],
        compiler_params=pltpu.CompilerParams(dimension_semantics=("arbitrary", "arbitrary"),
                                             vmem_limit_bytes=VMEM_LIMIT),
        name="retention",
    )(q, k, v, sg, dec, gn_g, s0f, s0b)


def _fnet_kernel(uf_ref, cs_ref, cls_ref, o_ref, xcs_ref):
    seq_len = uf_ref.shape[0]
    gd = FOURIER_GROUP_DIM

    @pl.when(pl.program_id(1) == 0)
    def _():
        for g in range(N_FOURIER_GROUPS):
            x = _dot(uf_ref[:, g * gd:(g + 1) * gd], cs_ref[...])
            xcs_ref[0:seq_len, g * gd:(g + 1) * gd] = x[:, :gd].astype(BF16)
            xcs_ref[seq_len:2 * seq_len, g * gd:(g + 1) * gd] = x[:, gd:].astype(BF16)

    o_ref[...] = _dot(cls_ref[...], xcs_ref[...]).astype(BF16)


def _fnet(uf, cs, cls, batch, seq_len):
    rb = FNET_ROWS
    nr = seq_len // rb
    return pl.pallas_call(
        _fnet_kernel,
        grid=(batch, nr),
        in_specs=[pl.BlockSpec((seq_len, D_MODEL), lambda b, r: (b, 0)),
                  pl.BlockSpec(cs.shape, lambda b, r: (0, 0)),
                  pl.BlockSpec((rb, 2 * seq_len), lambda b, r: (r, 0))],
        out_specs=pl.BlockSpec((rb, D_MODEL), lambda b, r: (b * nr + r, 0)),
        out_shape=jax.ShapeDtypeStruct((batch * seq_len, D_MODEL), BF16),
        scratch_shapes=[pltpu.VMEM((2 * seq_len, D_MODEL), BF16)],
        compiler_params=pltpu.CompilerParams(dimension_semantics=("parallel", "arbitrary"),
                                             vmem_limit_bytes=VMEM_LIMIT),
        name="fnet",
    )(uf, cs, cls)


def _merge_kernel(fm_ref, r_ref, gf_ref, gr_ref, x_ref, mod_ref, wf_ref, wr_ref, wo_ref, o_ref):
    f_out = _dot(fm_ref[...], wf_ref[...])
    r_out = _dot(r_ref[...], wr_ref[...])
    merged = gf_ref[...].astype(F32) * f_out + gr_ref[...].astype(F32) * r_out
    mix = _dot(merged.astype(BF16), wo_ref[...])
    o_ref[...] = x_ref[...] + mod_ref[0, 2:3, :] * mix


def _merge(fmix, r, gf, gr, x2d, mod3, w_four, w_ret, w_o, seq_len, mod_row_of_batch):
    t = x2d.shape[0]
    tm = TM_PROJ

    def mod_idx(i):
        return (mod_row_of_batch((i * tm) // seq_len), 0, 0)

    def tok(w):
        return pl.BlockSpec((tm, w), lambda i: (i, 0))

    def full(a):
        return pl.BlockSpec(a.shape, lambda i: (0, 0))

    return pl.pallas_call(
        _merge_kernel,
        grid=(t // tm,),
        in_specs=[tok(D_MODEL), tok(RET_WIDTH), tok(D_MODEL), tok(D_MODEL), tok(D_MODEL),
                  pl.BlockSpec((1, 6, D_MODEL), mod_idx), full(w_four), full(w_ret), full(w_o)],
        out_specs=tok(D_MODEL),
        out_shape=jax.ShapeDtypeStruct((t, D_MODEL), F32),
        compiler_params=pltpu.CompilerParams(dimension_semantics=("parallel",),
                                             vmem_limit_bytes=VMEM_LIMIT),
        name="merge",
    )(fmix, r, gf, gr, x2d, mod3, w_four, w_ret, w_o)


def _pack_pair(lo_f32, hi_f32):
    lo = lax.bitcast_convert_type(lo_f32.astype(BF16).astype(F32), jnp.uint32)
    hi = lax.bitcast_convert_type(hi_f32.astype(BF16).astype(F32), jnp.uint32)
    return lax.bitcast_convert_type((lo >> 16) | hi, jnp.int32)


def _unpack_pair(words_i32):
    w = lax.bitcast_convert_type(words_i32, jnp.uint32)
    lo = lax.bitcast_convert_type(w << 16, F32)
    hi = lax.bitcast_convert_type(w & jnp.uint32(0xFFFF0000), F32)
    return lo, hi


def _load_token_words(ref, lead, n_tok):
    parts = []
    for s in range(ROW_SLABS):
        idx = (pl.ds(s, n_tok, stride=ROW_SLABS), slice(None))
        parts.append(ref[lead + idx] if lead else ref[idx])
    return jnp.concatenate(parts, axis=1)


def _store_token_words(ref, words, n_tok):
    for s in range(ROW_SLABS):
        ref[pl.ds(s, n_tok, stride=ROW_SLABS), :] = words[:, s * 128:(s + 1) * 128]


def _route(scores, biased):
    tokens = scores.shape[1]
    neg = -jnp.inf
    epg = EXPERTS_PER_GROUP
    iota_g = lax.broadcasted_iota(jnp.int32, (epg, tokens), 0).astype(F32)

    def pick_first_max(cur, iota, size):
        m = jnp.max(cur, axis=0, keepdims=True)
        idx = jnp.min(jnp.where(cur == m, iota, float(size)), axis=0, keepdims=True)
        return m, idx, iota == idx

    group_scores = []
    for g in range(N_EXPERT_GROUPS):
        vals = biased[g * epg:(g + 1) * epg, :]
        m1, _, hit = pick_first_max(vals, iota_g, epg)
        m2 = jnp.max(jnp.where(hit, neg, vals), axis=0, keepdims=True)
        group_scores.append(m1 + m2)
    cur = jnp.concatenate(group_scores, axis=0)
    group_sel = jnp.zeros_like(cur)
    for _ in range(TOPK_GROUPS):
        _, _, hit = pick_first_max(cur, iota_g, N_EXPERT_GROUPS)
        group_sel = jnp.where(hit, 1.0, group_sel)
        cur = jnp.where(hit, neg, cur)
    masked = jnp.concatenate(
        [jnp.where(group_sel[g:g + 1, :] > 0.0, biased[g * epg:(g + 1) * epg, :], neg)
         for g in range(N_EXPERT_GROUPS)], axis=0)
    iota_e = lax.broadcasted_iota(jnp.int32, masked.shape, 0).astype(F32)
    sel = jnp.zeros_like(masked)
    cur = masked
    picks = []
    for _ in range(TOP_K):
        _, idx, hit = pick_first_max(cur, iota_e, N_EXPERTS)
        picks.append(idx)
        sel = jnp.where(hit, 1.0, sel)
        cur = jnp.where(hit, neg, cur)
    w = scores * sel
    return w / jnp.sum(w, axis=0, keepdims=True) * ROUTED_SCALE, sel, picks


def _router_kernel(x_ref, mod_ref, g2_ref, wrt_ref, rb_ref, hp_ref, ek_ref, rk_ref, wt_ref, cnt_ref, run_scr):
    tm = x_ref.shape[0]

    @pl.when(pl.program_id(0) == 0)
    def _():
        run_scr[...] = jnp.zeros_like(run_scr)

    h = _rms_mod(x_ref[...], g2_ref[...], mod_ref[0, 3:4, :], mod_ref[0, 4:5, :])
    half = D_MODEL // 2
    _store_token_words(hp_ref, _pack_pair(h[:, :half], h[:, half:]), tm)

    logits_t = lax.dot_general(wrt_ref[...], h, (((1,), (1,)), ((), ())),
                               precision=lax.Precision.HIGHEST, preferred_element_type=F32)
    scores = jax.nn.sigmoid(logits_t)
    comb_t, sel, picks = _route(scores, scores + rb_ref[...])

    earlier = (lax.broadcasted_iota(jnp.int32, (tm, tm), 0) < lax.broadcasted_iota(jnp.int32, (tm, tm), 1))
    rank_t = _dot(sel.astype(BF16), jnp.where(earlier, 1.0, 0.0).astype(BF16)) + run_scr[...]
    run_scr[...] += jnp.sum(sel, axis=1, keepdims=True)
    cnt_ref[...] = jnp.broadcast_to(run_scr[...], cnt_ref.shape)

    iota_e = lax.broadcasted_iota(jnp.int32, sel.shape, 0).astype(F32)
    ranks, weights = [], []
    for idx in picks:
        hit = iota_e == idx
        ranks.append(jnp.sum(jnp.where(hit, rank_t, 0.0), axis=0, keepdims=True))
        weights.append(jnp.sum(jnp.where(hit, comb_t, 0.0), axis=0, keepdims=True))
    ek_ref[...] = jnp.concatenate(picks, axis=0).astype(jnp.int32)
    rk_ref[...] = jnp.concatenate(ranks, axis=0).astype(jnp.int32)
    w_pad = jnp.concatenate(weights + [jnp.zeros((128 - TOP_K, tm), F32)], axis=0)
    wt_ref[...] = w_pad.T


def _router(x1, mod3, norm2_g, w_router_t, router_bias, seq_len, mod_row_of_batch):
    t = x1.shape[0]
    tm = TM_ROUTER

    def mod_idx(i):
        return (mod_row_of_batch((i * tm) // seq_len), 0, 0)

    def full(a):
        return pl.BlockSpec(a.shape, lambda i: (0,) * a.ndim)

    return pl.pallas_call(
        _router_kernel,
        grid=(t // tm,),
        in_specs=[pl.BlockSpec((tm, D_MODEL), lambda i: (i, 0)),
                  pl.BlockSpec((1, 6, D_MODEL), mod_idx),
                  full(norm2_g), full(w_router_t), full(router_bias)],
        out_specs=[pl.BlockSpec((tm * ROW_SLABS, 128), lambda i: (i, 0)),
                   pl.BlockSpec((TOP_K, tm), lambda i: (0, i)),
                   pl.BlockSpec((TOP_K, tm), lambda i: (0, i)),
                   pl.BlockSpec((tm, 128), lambda i: (i, 0)),
                   pl.BlockSpec((N_EXPERTS, 128), lambda i: (0, 0))],
        out_shape=[jax.ShapeDtypeStruct((t * ROW_SLABS, 128), jnp.int32),
                   jax.ShapeDtypeStruct((TOP_K, t), jnp.int32),
                   jax.ShapeDtypeStruct((TOP_K, t), jnp.int32),
                   jax.ShapeDtypeStruct((t, 128), F32),
                   jax.ShapeDtypeStruct((N_EXPERTS, 128), F32)],
        scratch_shapes=[pltpu.VMEM((N_EXPERTS, 1), F32)],
        compiler_params=pltpu.CompilerParams(dimension_semantics=("arbitrary",),
                                             vmem_limit_bytes=VMEM_LIMIT),
        name="router",
    )(x1, mod3, norm2_g, w_router_t, router_bias)


def _plan_kernel(ek_ref, rk_ref, cnt_ref, pos_ref, texp_ref, nused_ref, tend_ref):
    rows = float(EXPERT_ROWS)
    cnt = cnt_ref[:, 0:1]
    tiles = jnp.floor((cnt + (rows - 1.0)) / rows)
    before = (lax.broadcasted_iota(jnp.int32, (N_EXPERTS, N_EXPERTS), 1)
              < lax.broadcasted_iota(jnp.int32, (N_EXPERTS, N_EXPERTS), 0))
    tile_start = jnp.dot(jnp.where(before, 1.0, 0.0), jnp.broadcast_to(tiles, (N_EXPERTS, 128)),
                         precision=lax.Precision.HIGHEST, preferred_element_type=F32)[:, 0:1]
    tile_end = tile_start + tiles
    row_start = tile_start * rows

    ek = ek_ref[...]
    pos = rk_ref[...].astype(F32)
    tile_id = lax.broadcasted_iota(jnp.int32, texp_ref.shape, 1).astype(F32)
    texp = jnp.zeros(texp_ref.shape, F32)
    for e in range(N_EXPERTS):
        pos = pos + jnp.where(ek == e, row_start[e:e + 1, :], 0.0)
        texp = texp + jnp.where(tile_id >= tile_end[e:e + 1, :], 1.0, 0.0)
    pos_ref[...] = pos.astype(jnp.int32)
    texp_ref[...] = jnp.minimum(texp, N_EXPERTS - 1.0).astype(jnp.int32)
    nused_ref[...] = jnp.broadcast_to(tile_end[N_EXPERTS - 1:N_EXPERTS, :], nused_ref.shape).astype(jnp.int32)
    tend_ref[...] = jnp.broadcast_to(tile_end, tend_ref.shape).astype(jnp.int32)


def _plan(ek, rk, cnt, n_tiles_pad):
    t = ek.shape[1]

    def full(shape):
        return pl.BlockSpec(shape, lambda: (0,) * len(shape))

    return pl.pallas_call(
        _plan_kernel,
        in_specs=[full(ek.shape), full(rk.shape), full(cnt.shape)],
        out_specs=[full((TOP_K, t)), full((1, n_tiles_pad)), full((1, 128)), full((N_EXPERTS, 128))],
        out_shape=[jax.ShapeDtypeStruct((TOP_K, t), jnp.int32),
                   jax.ShapeDtypeStruct((1, n_tiles_pad), jnp.int32),
                   jax.ShapeDtypeStruct((1, 128), jnp.int32),
                   jax.ShapeDtypeStruct((N_EXPERTS, 128), jnp.int32)],
        compiler_params=pltpu.CompilerParams(vmem_limit_bytes=VMEM_LIMIT),
        name="plan",
    )(ek, rk, cnt)


def _sc_mesh():
    return plsc.VectorSubcoreMesh(core_axis_name="c", subcore_axis_name="s")


def _sc_dispatch(rows, pos3, n_out):
    t = rows.shape[0]
    ch = SC_CHUNK
    per_w = (t // ch) // SC_WORKERS

    @functools.partial(
        pl.kernel, out_type=jax.ShapeDtypeStruct((n_out,) + rows.shape[1:], jnp.int32), mesh=_sc_mesh(),
        scratch_types=[pltpu.VMEM((TOP_K, ch), jnp.int32), pltpu.VMEM((ch,) + rows.shape[1:], jnp.int32),
                       pltpu.SemaphoreType.DMA])
    def k(rows_hbm, pos_hbm, out_hbm, idx_v, rows_v, sem):
        wid = lax.axis_index("s") * SC_CORES + lax.axis_index("c")

        @pl.loop(0, per_w)
        def _(j):
            c = wid * per_w + j
            pltpu.sync_copy(pos_hbm.at[c], idx_v)
            pltpu.sync_copy(rows_hbm.at[pl.ds(c * ch, ch)], rows_v)
            copies = [pltpu.async_copy(rows_v, out_hbm.at[idx_v.at[kk]], sem) for kk in range(TOP_K)]
            for cp in copies:
                cp.wait()

    return k(rows, pos3)


def _sc_gather(table, pos3, t):
    ch = SC_CHUNK
    per_w = (t // ch) // SC_WORKERS

    @functools.partial(
        pl.kernel, out_type=jax.ShapeDtypeStruct((TOP_K, t) + table.shape[1:], jnp.int32), mesh=_sc_mesh(),
        scratch_types=[pltpu.VMEM((TOP_K, ch), jnp.int32), pltpu.VMEM((ch,) + table.shape[1:], jnp.int32),
                       pltpu.SemaphoreType.DMA])
    def k(tab_hbm, pos_hbm, out_hbm, idx_v, rows_v, sem):
        wid = lax.axis_index("s") * SC_CORES + lax.axis_index("c")

        @pl.loop(0, per_w)
        def _(j):
            c = wid * per_w + j
            pltpu.sync_copy(pos_hbm.at[c], idx_v)
            for kk in range(TOP_K):
                pltpu.async_copy(tab_hbm.at[idx_v.at[kk]], rows_v, sem).wait()
                pltpu.sync_copy(rows_v, out_hbm.at[kk, pl.ds(c * ch, ch)])

    return k(table, pos3)


def _experts_kernel(texp_ref, nused_ref, tend_ref, xs_ref, weg_hbm, weu_hbm, wed_hbm, ys_ref,
                    wg_scr, wu_scr, wd_scr, wg_buf, wu_buf, wd_buf, sem, group_scr):
    step = pl.program_id(0)
    rows = EXPERT_ROWS
    half = D_MODEL // 2
    n_used = nused_ref[0]

    def weight_copies(e, slot):
        return [pltpu.make_async_copy(weg_hbm.at[e], wg_buf.at[slot], sem.at[slot, 0]),
                pltpu.make_async_copy(weu_hbm.at[e], wu_buf.at[slot], sem.at[slot, 1]),
                pltpu.make_async_copy(wed_hbm.at[e], wd_buf.at[slot], sem.at[slot, 2])]

    @pl.when(step == 0)
    def _():
        group_scr[0] = 0
        for cp in weight_copies(texp_ref[0], 0):
            cp.start()

    def row_tile(tile, x_view, y_view):
        expert = texp_ref[tile]
        used = tile < n_used
        new_expert = (tile == 0) | (expert != texp_ref[jnp.maximum(tile - 1, 0)])

        @pl.when(used & new_expert)
        def _():
            group = group_scr[0]
            slot = group % 2
            next_tile = tend_ref[expert]

            @pl.when(next_tile < n_used)
            def _():
                for cp in weight_copies(texp_ref[next_tile], 1 - slot):
                    cp.start()

            for cp in weight_copies(expert, slot):
                cp.wait()
            wg_scr[...] = wg_buf[slot].astype(BF16)
            wu_scr[...] = wu_buf[slot].astype(BF16)
            wd_scr[...] = wd_buf[slot].astype(BF16)
            group_scr[0] = group + 1

        @pl.when(used)
        def _():
            lo, hi = _unpack_pair(_load_token_words(x_view, (), rows))
            lo = lo.astype(BF16)
            hi = hi.astype(BF16)
            g = _dot(lo, wg_scr[0:half, :]) + _dot(hi, wg_scr[half:D_MODEL, :])
            u = _dot(lo, wu_scr[0:half, :]) + _dot(hi, wu_scr[half:D_MODEL, :])
            y = _dot((_silu(g) * u).astype(BF16), wd_scr[...])
            _store_token_words(y_view, _pack_pair(y[:, :half], y[:, half:]), rows)

        @pl.when(jnp.logical_not(used) & (step == (n_used - 1) // TILES_PER_STEP))
        def _():
            y_view[...] = jnp.zeros_like(y_view)

    for s in range(TILES_PER_STEP):
        view = pl.ds(s * rows * ROW_SLABS, rows * ROW_SLABS)
        row_tile(step * TILES_PER_STEP + s, xs_ref.at[view], ys_ref.at[view])


def _experts(texp, nused, tend, xs2d, weg, weu, wed, n_tiles):
    block = (TILES_PER_STEP * EXPERT_ROWS * ROW_SLABS, 128)
    hbm = pl.BlockSpec(memory_space=pl.ANY)

    def block_idx(j, te, nu, tn):
        return (jnp.minimum(j, (nu[0] - 1) // TILES_PER_STEP), 0)

    grid_spec = pltpu.PrefetchScalarGridSpec(
        num_scalar_prefetch=3,
        grid=(n_tiles // TILES_PER_STEP,),
        in_specs=[pl.BlockSpec(block, block_idx), hbm, hbm, hbm],
        out_specs=pl.BlockSpec(block, block_idx),
        scratch_shapes=[pltpu.VMEM((D_MODEL, EXPERT_DIM), BF16),
                        pltpu.VMEM((D_MODEL, EXPERT_DIM), BF16),
                        pltpu.VMEM((EXPERT_DIM, D_MODEL), BF16),
                        pltpu.VMEM((2, D_MODEL, EXPERT_DIM), F32),
                        pltpu.VMEM((2, D_MODEL, EXPERT_DIM), F32),
                        pltpu.VMEM((2, EXPERT_DIM, D_MODEL), F32),
                        pltpu.SemaphoreType.DMA((2, 3)),
                        pltpu.SMEM((1,), jnp.int32)],
    )
    return pl.pallas_call(
        _experts_kernel,
        grid_spec=grid_spec,
        out_shape=jax.ShapeDtypeStruct(xs2d.shape, jnp.int32),
        compiler_params=pltpu.CompilerParams(dimension_semantics=("arbitrary",),
                                             vmem_limit_bytes=VMEM_LIMIT),
        name="experts",
    )(texp, nused, tend, xs2d, weg, weu, wed)


def _final_kernel(x_ref, hp_ref, y8_ref, wt_ref, mod_ref, wsg_ref, wsu_ref, wsd_ref, fng_ref, o_ref):
    tm = x_ref.shape[0]
    lo, hi = _unpack_pair(_load_token_words(hp_ref, (), tm))
    hb = jnp.concatenate([lo, hi], axis=1).astype(BF16)
    shared = _dot((_silu(_dot(hb, wsg_ref[...])) * _dot(hb, wsu_ref[...])).astype(BF16), wsd_ref[...])
    wt = wt_ref[...]
    r_lo = jnp.zeros((tm, D_MODEL // 2), F32)
    r_hi = jnp.zeros((tm, D_MODEL // 2), F32)
    for k in range(TOP_K):
        lo, hi = _unpack_pair(_load_token_words(y8_ref, (k,), tm))
        wk = wt[:, k:k + 1]
        r_lo = r_lo + wk * lo
        r_hi = r_hi + wk * hi
    routed = jnp.concatenate([r_lo, r_hi], axis=1)
    y = x_ref[...] + mod_ref[0, 5:6, :] * (routed + shared)
    ms = jnp.mean(y * y, axis=-1, keepdims=True)
    o_ref[...] = y * lax.rsqrt(ms + EPS) * fng_ref[...]


def _final(x1, hp2d, y8, wtok, mod3, wsg, wsu, wsd, final_g, seq_len, mod_row_of_batch):
    t = x1.shape[0]
    tm = TM_FINAL

    def mod_idx(i):
        return (mod_row_of_batch((i * tm) // seq_len), 0, 0)

    def full(a):
        return pl.BlockSpec(a.shape, lambda i: (0,) * a.ndim)

    return pl.pallas_call(
        _final_kernel,
        grid=(t // tm,),
        in_specs=[pl.BlockSpec((tm, D_MODEL), lambda i: (i, 0)),
                  pl.BlockSpec((tm * ROW_SLABS, 128), lambda i: (i, 0)),
                  pl.BlockSpec((TOP_K, tm * ROW_SLABS, 128), lambda i: (0, i, 0)),
                  pl.BlockSpec((tm, 128), lambda i: (i, 0)),
                  pl.BlockSpec((1, 6, D_MODEL), mod_idx),
                  full(wsg), full(wsu), full(wsd), full(final_g)],
        out_specs=pl.BlockSpec((tm, D_MODEL), lambda i: (i, 0)),
        out_shape=jax.ShapeDtypeStruct((t, D_MODEL), F32),
        compiler_params=pltpu.CompilerParams(dimension_semantics=("parallel",),
                                             vmem_limit_bytes=VMEM_LIMIT),
        name="final",
    )(x1, hp2d, y8, wtok, mod3, wsg, wsu, wsd, final_g)


def _moe(x1, mod3, lw, seq_len, mod_row_of_batch):
    t = x1.shape[0]
    n_tiles = TOP_K * t // EXPERT_ROWS + N_EXPERTS
    n_tiles_pad = -(-n_tiles // 128) * 128
    hp2d, ek, rk, wtok, cnt = _router(x1, mod3, lw["norm2_g"], lw["w_router_t"], lw["router_bias"],
                                      seq_len, mod_row_of_batch)
    pos, texp, nused, tend = _plan(ek, rk, cnt, n_tiles_pad)
    pos3 = pos.reshape(TOP_K, t // SC_CHUNK, SC_CHUNK).transpose(1, 0, 2)
    xs = _sc_dispatch(hp2d.reshape(t, ROW_SLABS, 128), pos3, n_tiles * EXPERT_ROWS)
    ys2d = _experts(texp.reshape(-1), nused.reshape(-1), tend[:, 0], xs.reshape(-1, 128),
                    lw["weg"], lw["weu"], lw["wed"], n_tiles)
    y8 = _sc_gather(ys2d.reshape(-1, ROW_SLABS, 128), pos3, t)
    return _final(x1, hp2d, y8.reshape(TOP_K, t * ROW_SLABS, 128), wtok, mod3,
                  lw["wsg"], lw["wsu"], lw["wsd"], lw["final_g"], seq_len, mod_row_of_batch)


def _dft_tables(seq_len):
    gd = FOURIER_GROUP_DIM
    kc = np.arange(gd)
    ang_c = ((kc[:, None] * kc[None, :]) % gd) * (2.0 * math.pi / gd)
    cs = np.concatenate([np.cos(ang_c), np.sin(ang_c)], axis=1) * (gd ** -0.5)
    kl = np.arange(seq_len)
    ang_l = ((kl[:, None] * kl[None, :]) % seq_len) * (2.0 * math.pi / seq_len)
    cls = np.concatenate([np.cos(ang_l), -np.sin(ang_l)], axis=1) * (seq_len ** -0.5)
    return jnp.asarray(cs.astype(np.float32), dtype=BF16), jnp.asarray(cls.astype(np.float32), dtype=BF16)


def _rope_tables(length):
    rows = length // GRID_W
    r = np.repeat(np.arange(rows, dtype=np.float32), GRID_W)
    col = np.tile(np.arange(GRID_W, dtype=np.float32), rows)
    nf = RET_HEAD_DIM // 4
    inv = (np.float32(ROPE_BASE) ** (-np.arange(nf, dtype=np.float32) / np.float32(nf))).astype(np.float32)
    ar = r[:, None] * inv[None]
    ac = col[:, None] * inv[None]
    ang = np.concatenate([ar, ar, ac, ac], axis=-1).astype(np.float64)
    sign = np.where((np.arange(RET_HEAD_DIM) & nf) == 0, -1.0, 1.0)
    return (jnp.asarray(np.cos(ang).astype(np.float32)),
            jnp.asarray((np.sin(ang) * sign[None, :]).astype(np.float32)))


def _trunk_path(x, mod3, mod_row_of_batch, s0f, s0b, rope, lw):
    batch, seq_len, _ = x.shape
    x2d = x.reshape(batch * seq_len, D_MODEL)
    uf, q, k, v, sg, gf, gr = _inproj(x2d, mod3, lw["norm1_g"], lw["w_in"], seq_len, mod_row_of_batch, rope)
    r, s_f, s_b = _retention(q, k, v, sg, lw["dec"], lw["gn_g"], s0f, s0b, batch, seq_len)
    cs, cls = _dft_tables(seq_len)
    fmix = _fnet(uf, cs, cls, batch, seq_len)
    x1 = _merge(fmix, r, gf, gr, x2d, mod3, lw["w_four"], lw["w_ret"], lw["w_o"], seq_len, mod_row_of_batch)
    y = _moe(x1, mod3, lw, seq_len, mod_row_of_batch)
    return y.reshape(batch, seq_len, D_MODEL), s_f, s_b


def kernel(x_prompt, x_sample, state_ret_fwd, state_ret_bwd, c, c_ctx, w_ada, b_ada, norm1_g, norm2_g, w_in,
           ret_decay_fwd, ret_decay_bwd, ret_gn_g, w_four_out, w_ret_out, w_out, w_router, router_bias,
           w_exp_gate, w_exp_up, w_exp_down, w_shared_gate, w_shared_up, w_shared_down, final_norm_g):
    depth = w_ada.shape[0]
    assert depth == 1, "final norm is fused into the last layer's MoE kernel"
    n_ctx, n_lat = x_prompt.shape[0], x_sample.shape[0]
    cond = jnp.concatenate([c_ctx[None, :], c], axis=0)
    cond = jnp.pad(cond, ((0, (-cond.shape[0]) % 8), (0, 0)))
    rope = _rope_tables(x_sample.shape[1])
    zeros = jnp.zeros((n_ctx, N_RET_HEADS, RET_HEAD_DIM, RET_HEAD_DIM), F32)

    layer = 0
    mod = _ada(cond, w_ada[layer], b_ada[layer][None, :])
    mod3 = mod.reshape(mod.shape[0], 6, D_MODEL)
    dec = jnp.stack([ret_decay_fwd[layer], ret_decay_bwd[layer]], axis=1)
    lw = {
        "norm1_g": norm1_g[layer][None, :],
        "norm2_g": norm2_g[layer][None, :],
        "w_in": w_in[layer].astype(BF16),
        "dec": jnp.broadcast_to(dec[:, :, None], (N_RET_HEADS, 2, RET_HEAD_DIM)).astype(F32),
        "gn_g": ret_gn_g[layer][None, :],
        "w_four": w_four_out[layer].astype(BF16),
        "w_ret": w_ret_out[layer].astype(BF16),
        "w_o": w_out[layer].astype(BF16),
        "w_router_t": w_router[layer].T,
        "router_bias": router_bias[layer][:, None],
        "weg": w_exp_gate[layer],
        "weu": w_exp_up[layer],
        "wed": w_exp_down[layer],
        "wsg": w_shared_gate[layer].astype(BF16),
        "wsu": w_shared_up[layer].astype(BF16),
        "wsd": w_shared_down[layer].astype(BF16),
        "final_g": final_norm_g[None, :],
    }
    y_prompt, s_f, s_b = _trunk_path(x_prompt, mod3, lambda b: 0, zeros, zeros, None, lw)
    y_sample, _, _ = _trunk_path(x_sample, mod3, lambda b: 1 + b, state_ret_fwd[:, layer],
                                 state_ret_bwd[:, layer], rope, lw)
    return (y_prompt, y_sample, s_f[:, None], s_b[:, None])
```

```python
import functools
import math

import jax
import jax.numpy as jnp
import numpy as np
from jax import lax
from jax.experimental import pallas as pl
from jax.experimental.pallas import tpu as pltpu
from jax.experimental.pallas import tpu_sc as plsc

F32 = jnp.float32
BF16 = jnp.bfloat16

D_MODEL = 1024
GRID_W = 64
N_FOURIER_GROUPS = 8
FOURIER_GROUP_DIM = 128
N_RET_HEADS = 4
RET_HEAD_DIM = 128
RET_WIDTH = N_RET_HEADS * RET_HEAD_DIM
CHUNK = 128
N_EXPERTS = 64
N_EXPERT_GROUPS = 8
EXPERTS_PER_GROUP = N_EXPERTS // N_EXPERT_GROUPS
TOPK_GROUPS = 4
TOP_K = 8
EXPERT_DIM = 256
ROUTED_SCALE = 2.5
ROPE_BASE = 10000.0
EPS = 1e-6
Q_SCALE = RET_HEAD_DIM ** -0.5

_C_UF = (0, 1024)
_C_Q = (1024, 1536)
_C_K = (1536, 2048)
_C_V = (2048, 2560)
_C_G = (2560, 3072)
_C_GF = (3072, 4096)
_C_GR = (4096, 5120)

VMEM_LIMIT = 56 * 1024 * 1024

TM_PROJ = 512
FNET_ROWS = 256
RET_ALL_HEADS_MAX_LEN = 512
TM_ROUTER = 1024
TM_FINAL = 512
EXPERT_ROWS = 512
TILES_PER_STEP = 2
ROW_SLABS = 4
SC_CORES = 2
SC_WORKERS = 32
SC_CHUNK = 128


def _silu(x):
    return x * jax.nn.sigmoid(x)


def _dot(a, b):
    return jnp.dot(a, b, preferred_element_type=F32)


def _rms_mod(x, g, shift, scale):
    ms = jnp.mean(x * x, axis=-1, keepdims=True)
    y = x * lax.rsqrt(ms + EPS) * g
    return y * (1.0 + scale) + shift


def _ada_kernel(cond_ref, w_ref, b_ref, o_ref):
    s = _silu(cond_ref[...]).astype(BF16)
    o_ref[...] = _dot(s, w_ref[...].astype(BF16)) + b_ref[...]


def _ada(cond, w_ada, b_ada):
    rows, n = cond.shape[0], w_ada.shape[1]
    tn = 1536
    return pl.pallas_call(
        _ada_kernel,
        grid=(n // tn,),
        in_specs=[pl.BlockSpec((rows, D_MODEL), lambda j: (0, 0)),
                  pl.BlockSpec((D_MODEL, tn), lambda j: (0, j)),
                  pl.BlockSpec((1, tn), lambda j: (0, j))],
        out_specs=pl.BlockSpec((rows, tn), lambda j: (0, j)),
        out_shape=jax.ShapeDtypeStruct((rows, n), F32),
        compiler_params=pltpu.CompilerParams(vmem_limit_bytes=VMEM_LIMIT),
        name="ada",
    )(cond, w_ada, b_ada)


def _rope_head(x, cos, sin_signed, first_half):
    partner = jnp.where(first_half, pltpu.roll(x, 96, 1), pltpu.roll(x, 32, 1))
    return x * cos + partner * sin_signed


def _inproj_kernel(*refs, use_rope):
    if use_rope:
        x_ref, mod_ref, g_ref, w_ref, cos_ref, sin_ref = refs[:6]
        outs = refs[6:]
    else:
        x_ref, mod_ref, g_ref, w_ref = refs[:4]
        outs = refs[4:]
    uf_o, q_o, k_o, v_o, sg_o, gf_o, gr_o = outs

    h = _rms_mod(x_ref[...], g_ref[...], mod_ref[0, 0:1, :], mod_ref[0, 1:2, :])
    hb = h.astype(BF16)

    def proj(cols):
        return _dot(hb, w_ref[:, cols[0]:cols[1]])

    uf_o[...] = proj(_C_UF).astype(BF16)
    q = proj(_C_Q)
    k = proj(_C_K)
    if use_rope:
        cos = cos_ref[...]
        sin_signed = sin_ref[...]
        lane = lax.broadcasted_iota(jnp.int32, cos.shape, 1)
        first_half = (lane & 32) == 0
        for hd in range(N_RET_HEADS):
            sl = slice(hd * RET_HEAD_DIM, (hd + 1) * RET_HEAD_DIM)
            q_o[:, sl] = (_rope_head(q[:, sl], cos, sin_signed, first_half) * Q_SCALE).astype(BF16)
            k_o[:, sl] = _rope_head(k[:, sl], cos, sin_signed, first_half).astype(BF16)
    else:
        q_o[...] = (q * Q_SCALE).astype(BF16)
        k_o[...] = k.astype(BF16)
    v_o[...] = proj(_C_V).astype(BF16)
    sg_o[...] = _silu(proj(_C_G)).astype(BF16)
    gf_o[...] = jax.nn.sigmoid(proj(_C_GF)).astype(BF16)
    gr_o[...] = jax.nn.sigmoid(proj(_C_GR)).astype(BF16)


def _inproj(x2d, mod3, norm_g, w_in_bf, seq_len, mod_row_of_batch, rope):
    t = x2d.shape[0]
    tm = TM_PROJ
    tiles_per_seq = max(seq_len // tm, 1)

    def mod_idx(i):
        return (mod_row_of_batch((i * tm) // seq_len), 0, 0)

    in_specs = [pl.BlockSpec((tm, D_MODEL), lambda i: (i, 0)),
                pl.BlockSpec((1, 6, D_MODEL), mod_idx),
                pl.BlockSpec((1, D_MODEL), lambda i: (0, 0)),
                pl.BlockSpec(w_in_bf.shape, lambda i: (0, 0))]
    args = [x2d, mod3, norm_g, w_in_bf]
    if rope is not None:
        in_specs += [pl.BlockSpec((tm, RET_HEAD_DIM), lambda i: (i % tiles_per_seq, 0))] * 2
        args += list(rope)
    widths = [1024, RET_WIDTH, RET_WIDTH, RET_WIDTH, RET_WIDTH, 1024, 1024]
    return pl.pallas_call(
        functools.partial(_inproj_kernel, use_rope=rope is not None),
        grid=(t // tm,),
        in_specs=in_specs,
        out_specs=[pl.BlockSpec((tm, w), lambda i: (i, 0)) for w in widths],
        out_shape=[jax.ShapeDtypeStruct((t, w), BF16) for w in widths],
        compiler_params=pltpu.CompilerParams(dimension_semantics=("parallel",),
                                             vmem_limit_bytes=VMEM_LIMIT),
        name="inproj",
    )(*args)


def _retention_kernel(q_ref, k_ref, v_ref, sg_ref, dec_ref, gn_ref, s0f_ref, s0b_ref,
                      r_ref, sfo_ref, sbo_ref, tab_scr, gc_scr):
    n_chunks = q_ref.shape[0] // CHUNK
    hd = RET_HEAD_DIM
    heads_here = q_ref.shape[1] // hd
    first_head = pl.program_id(1) * heads_here

    @pl.when((pl.program_id(0) == 0) & (pl.program_id(1) == 0))
    def _():
        row = lax.broadcasted_iota(jnp.int32, (CHUNK, CHUNK), 0).astype(F32)
        col = lax.broadcasted_iota(jnp.int32, (CHUNK, CHUNK), 1).astype(F32)
        diff = row - col
        for h in range(N_RET_HEADS):
            dec = dec_ref[h]
            lg = jnp.minimum(dec, 0.0) - jnp.log1p(jnp.exp(-jnp.abs(dec)))
            lgf = lg[0:1, :]
            lgb = lg[1:2, :]
            tab_scr[h, 0] = jnp.exp(jnp.where(diff >= 0, lgf * diff, lgb * (-diff)))
            tab_scr[h, 1] = jnp.exp(lgf * (row + 1.0))
            tab_scr[h, 2] = jnp.exp(lgb * (CHUNK - row))
            tab_scr[h, 3] = jnp.exp(lgf * (CHUNK - 1.0 - col))
            tab_scr[h, 4] = jnp.exp(lgb * col)
            gc_scr[h] = jnp.exp(lg * CHUNK)

    def rows(n):
        return slice(n * CHUNK, (n + 1) * CHUNK)

    for h in range(heads_here):
        cols = slice(h * hd, (h + 1) * hd)
        head = first_head + h
        decay, qw_f, qw_b, kwt_f, kwt_b = (tab_scr[head, i] for i in range(5))
        gc = gc_scr[head]
        gc_f = gc[0:1, :]
        gc_b = gc[1:2, :]

        kv_f, kv_b = [], []
        for n in range(n_chunks):
            kt = k_ref[rows(n), cols].astype(F32).T
            vn = v_ref[rows(n), cols]
            kv_f.append(_dot((kt * kwt_f).astype(BF16), vn))
            kv_b.append(_dot((kt * kwt_b).astype(BF16), vn))

        s = s0f_ref[h]
        prev_f = []
        for n in range(n_chunks):
            prev_f.append(s.astype(BF16))
            s = gc_f * s + kv_f[n]
        sfo_ref[h] = s
        s = s0b_ref[h]
        prev_b = [None] * n_chunks
        for n in reversed(range(n_chunks)):
            prev_b[n] = s.astype(BF16)
            s = gc_b * s + kv_b[n]
        sbo_ref[h] = s

        gn = gn_ref[:, cols]
        for n in range(n_chunks):
            qn = q_ref[rows(n), cols]
            qf = qn.astype(F32)
            scores = lax.dot_general(qn, k_ref[rows(n), cols], (((1,), (1,)), ((), ())),
                                     preferred_element_type=F32)
            o = _dot((scores * decay).astype(BF16), v_ref[rows(n), cols])
            o = o + _dot((qf * qw_f).astype(BF16), prev_f[n])
            o = o + _dot((qf * qw_b).astype(BF16), prev_b[n])
            mu = jnp.mean(o, axis=-1, keepdims=True)
            d = o - mu
            var = jnp.mean(d * d, axis=-1, keepdims=True)
            on = d * lax.rsqrt(var + EPS) * gn
            r_ref[rows(n), cols] = (on * sg_ref[rows(n), cols].astype(F32)).astype(BF16)


def _retention(q, k, v, sg, dec, gn_g, s0f, s0b, batch, seq_len):
    hd = RET_HEAD_DIM
    heads_per_step = N_RET_HEADS if seq_len <= RET_ALL_HEADS_MAX_LEN else 1
    width = heads_per_step * hd
    tok_spec = pl.BlockSpec((seq_len, width), lambda b, g: (b, g))
    st_spec = pl.BlockSpec((None, heads_per_step, hd, hd), lambda b, g: (b, g, 0, 0))
    st_shape = jax.ShapeDtypeStruct((batch, N_RET_HEADS, hd, hd), F32)
    return pl.pallas_call(
        _retention_kernel,
        grid=(batch, N_RET_HEADS // heads_per_step),
        in_specs=[tok_spec, tok_spec, tok_spec, tok_spec,
                  pl.BlockSpec(dec.shape, lambda b, g: (0, 0, 0)),
                  pl.BlockSpec((1, width), lambda b, g: (0, g)),
                  st_spec, st_spec],
        out_specs=[tok_spec, st_spec, st_spec],
        out_shape=[jax.ShapeDtypeStruct((batch * seq_len, RET_WIDTH), BF16), st_shape, st_shape],
        scratch_shapes=[pltpu.VMEM((N_RET_HEADS, 5, CHUNK, CHUNK), F32),
                        pltpu.VMEM((N_RET_HEADS, 2, hd), F32)],
        compiler_params=pltpu.CompilerParams(dimension_semantics=("arbitrary", "arbitrary"),
                                             vmem_limit_bytes=VMEM_LIMIT),
        name="retention",
    )(q, k, v, sg, dec, gn_g, s0f, s0b)


def _fnet_kernel(uf_ref, cs_ref, cls_ref, o_ref, xcs_ref):
    seq_len = uf_ref.shape[0]
    gd = FOURIER_GROUP_DIM

    @pl.when(pl.program_id(1) == 0)
    def _():
        for g in range(N_FOURIER_GROUPS):
            x = _dot(uf_ref[:, g * gd:(g + 1) * gd], cs_ref[...])
            xcs_ref[0:seq_len, g * gd:(g + 1) * gd] = x[:, :gd].astype(BF16)
            xcs_ref[seq_len:2 * seq_len, g * gd:(g + 1) * gd] = x[:, gd:].astype(BF16)

    o_ref[...] = _dot(cls_ref[...], xcs_ref[...]).astype(BF16)


def _fnet(uf, cs, cls, batch, seq_len):
    rb = FNET_ROWS
    nr = seq_len // rb
    return pl.pallas_call(
        _fnet_kernel,
        grid=(batch, nr),
        in_specs=[pl.BlockSpec((seq_len, D_MODEL), lambda b, r: (b, 0)),
                  pl.BlockSpec(cs.shape, lambda b, r: (0, 0)),
                  pl.BlockSpec((rb, 2 * seq_len), lambda b, r: (r, 0))],
        out_specs=pl.BlockSpec((rb, D_MODEL), lambda b, r: (b * nr + r, 0)),
        out_shape=jax.ShapeDtypeStruct((batch * seq_len, D_MODEL), BF16),
        scratch_shapes=[pltpu.VMEM((2 * seq_len, D_MODEL), BF16)],
        compiler_params=pltpu.CompilerParams(dimension_semantics=("parallel", "arbitrary"),
                                             vmem_limit_bytes=VMEM_LIMIT),
        name="fnet",
    )(uf, cs, cls)


def _merge_kernel(fm_ref, r_ref, gf_ref, gr_ref, x_ref, mod_ref, wf_ref, wr_ref, wo_ref, o_ref):
    f_out = _dot(fm_ref[...], wf_ref[...])
    r_out = _dot(r_ref[...], wr_ref[...])
    merged = gf_ref[...].astype(F32) * f_out + gr_ref[...].astype(F32) * r_out
    mix = _dot(merged.astype(BF16), wo_ref[...])
    o_ref[...] = x_ref[...] + mod_ref[0, 2:3, :] * mix


def _merge(fmix, r, gf, gr, x2d, mod3, w_four, w_ret, w_o, seq_len, mod_row_of_batch):
    t = x2d.shape[0]
    tm = TM_PROJ

    def mod_idx(i):
        return (mod_row_of_batch((i * tm) // seq_len), 0, 0)

    def tok(w):
        return pl.BlockSpec((tm, w), lambda i: (i, 0))

    def full(a):
        return pl.BlockSpec(a.shape, lambda i: (0, 0))

    return pl.pallas_call(
        _merge_kernel,
        grid=(t // tm,),
        in_specs=[tok(D_MODEL), tok(RET_WIDTH), tok(D_MODEL), tok(D_MODEL), tok(D_MODEL),
                  pl.BlockSpec((1, 6, D_MODEL), mod_idx), full(w_four), full(w_ret), full(w_o)],
        out_specs=tok(D_MODEL),
        out_shape=jax.ShapeDtypeStruct((t, D_MODEL), F32),
        compiler_params=pltpu.CompilerParams(dimension_semantics=("parallel",),
                                             vmem_limit_bytes=VMEM_LIMIT),
        name="merge",
    )(fmix, r, gf, gr, x2d, mod3, w_four, w_ret, w_o)


def _pack_pair(lo_f32, hi_f32):
    lo = lax.bitcast_convert_type(lo_f32.astype(BF16).astype(F32), jnp.uint32)
    hi = lax.bitcast_convert_type(hi_f32.astype(BF16).astype(F32), jnp.uint32)
    return lax.bitcast_convert_type((lo >> 16) | hi, jnp.int32)


def _unpack_pair(words_i32):
    w = lax.bitcast_convert_type(words_i32, jnp.uint32)
    lo = lax.bitcast_convert_type(w << 16, F32)
    hi = lax.bitcast_convert_type(w & jnp.uint32(0xFFFF0000), F32)
    return lo, hi


def _load_token_words(ref, lead, n_tok):
    parts = []
    for s in range(ROW_SLABS):
        idx = (pl.ds(s, n_tok, stride=ROW_SLABS), slice(None))
        parts.append(ref[lead + idx] if lead else ref[idx])
    return jnp.concatenate(parts, axis=1)


def _store_token_words(ref, words, n_tok):
    for s in range(ROW_SLABS):
        ref[pl.ds(s, n_tok, stride=ROW_SLABS), :] = words[:, s * 128:(s + 1) * 128]


def _route(scores, biased):
    tokens = scores.shape[1]
    neg = -jnp.inf
    epg = EXPERTS_PER_GROUP
    iota_g = lax.broadcasted_iota(jnp.int32, (epg, tokens), 0).astype(F32)

    def pick_first_max(cur, iota, size):
        m = jnp.max(cur, axis=0, keepdims=True)
        idx = jnp.min(jnp.where(cur == m, iota, float(size)), axis=0, keepdims=True)
        return m, idx, iota == idx

    group_scores = []
    for g in range(N_EXPERT_GROUPS):
        vals = biased[g * epg:(g + 1) * epg, :]
        m1, _, hit = pick_first_max(vals, iota_g, epg)
        m2 = jnp.max(jnp.where(hit, neg, vals), axis=0, keepdims=True)
        group_scores.append(m1 + m2)
    cur = jnp.concatenate(group_scores, axis=0)
    group_sel = jnp.zeros_like(cur)
    for _ in range(TOPK_GROUPS):
        _, _, hit = pick_first_max(cur, iota_g, N_EXPERT_GROUPS)
        group_sel = jnp.where(hit, 1.0, group_sel)
        cur = jnp.where(hit, neg, cur)
    masked = jnp.concatenate(
        [jnp.where(group_sel[g:g + 1, :] > 0.0, biased[g * epg:(g + 1) * epg, :], neg)
         for g in range(N_EXPERT_GROUPS)], axis=0)
    iota_e = lax.broadcasted_iota(jnp.int32, masked.shape, 0).astype(F32)
    sel = jnp.zeros_like(masked)
    cur = masked
    picks = []
    for _ in range(TOP_K):
        _, idx, hit = pick_first_max(cur, iota_e, N_EXPERTS)
        picks.append(idx)
        sel = jnp.where(hit, 1.0, sel)
        cur = jnp.where(hit, neg, cur)
    w = scores * sel
    return w / jnp.sum(w, axis=0, keepdims=True) * ROUTED_SCALE, sel, picks


def _router_kernel(x_ref, mod_ref, g2_ref, wrt_ref, rb_ref, hp_ref, ek_ref, rk_ref, wt_ref, cnt_ref, run_scr):
    tm = x_ref.shape[0]

    @pl.when(pl.program_id(0) == 0)
    def _():
        run_scr[...] = jnp.zeros_like(run_scr)

    h = _rms_mod(x_ref[...], g2_ref[...], mod_ref[0, 3:4, :], mod_ref[0, 4:5, :])
    half = D_MODEL // 2
    _store_token_words(hp_ref, _pack_pair(h[:, :half], h[:, half:]), tm)

    logits_t = lax.dot_general(wrt_ref[...], h, (((1,), (1,)), ((), ())),
                               precision=lax.Precision.HIGHEST, preferred_element_type=F32)
    scores = jax.nn.sigmoid(logits_t)
    comb_t, sel, picks = _route(scores, scores + rb_ref[...])

    earlier = (lax.broadcasted_iota(jnp.int32, (tm, tm), 0) < lax.broadcasted_iota(jnp.int32, (tm, tm), 1))
    rank_t = _dot(sel.astype(BF16), jnp.where(earlier, 1.0, 0.0).astype(BF16)) + run_scr[...]
    run_scr[...] += jnp.sum(sel, axis=1, keepdims=True)
    cnt_ref[...] = jnp.broadcast_to(run_scr[...], cnt_ref.shape)

    iota_e = lax.broadcasted_iota(jnp.int32, sel.shape, 0).astype(F32)
    ranks, weights = [], []
    for idx in picks:
        hit = iota_e == idx
        ranks.append(jnp.sum(jnp.where(hit, rank_t, 0.0), axis=0, keepdims=True))
        weights.append(jnp.sum(jnp.where(hit, comb_t, 0.0), axis=0, keepdims=True))
    ek_ref[...] = jnp.concatenate(picks, axis=0).astype(jnp.int32)
    rk_ref[...] = jnp.concatenate(ranks, axis=0).astype(jnp.int32)
    w_pad = jnp.concatenate(weights + [jnp.zeros((128 - TOP_K, tm), F32)], axis=0)
    wt_ref[...] = w_pad.T


def _router(x1, mod3, norm2_g, w_router_t, router_bias, seq_len, mod_row_of_batch):
    t = x1.shape[0]
    tm = TM_ROUTER

    def mod_idx(i):
        return (mod_row_of_batch((i * tm) // seq_len), 0, 0)

    def full(a):
        return pl.BlockSpec(a.shape, lambda i: (0,) * a.ndim)

    return pl.pallas_call(
        _router_kernel,
        grid=(t // tm,),
        in_specs=[pl.BlockSpec((tm, D_MODEL), lambda i: (i, 0)),
                  pl.BlockSpec((1, 6, D_MODEL), mod_idx),
                  full(norm2_g), full(w_router_t), full(router_bias)],
        out_specs=[pl.BlockSpec((tm * ROW_SLABS, 128), lambda i: (i, 0)),
                   pl.BlockSpec((TOP_K, tm), lambda i: (0, i)),
                   pl.BlockSpec((TOP_K, tm), lambda i: (0, i)),
                   pl.BlockSpec((tm, 128), lambda i: (i, 0)),
                   pl.BlockSpec((N_EXPERTS, 128), lambda i: (0, 0))],
        out_shape=[jax.ShapeDtypeStruct((t * ROW_SLABS, 128), jnp.int32),
                   jax.ShapeDtypeStruct((TOP_K, t), jnp.int32),
                   jax.ShapeDtypeStruct((TOP_K, t), jnp.int32),
                   jax.ShapeDtypeStruct((t, 128), F32),
                   jax.ShapeDtypeStruct((N_EXPERTS, 128), F32)],
        scratch_shapes=[pltpu.VMEM((N_EXPERTS, 1), F32)],
        compiler_params=pltpu.CompilerParams(dimension_semantics=("arbitrary",),
                                             vmem_limit_bytes=VMEM_LIMIT),
        name="router",
    )(x1, mod3, norm2_g, w_router_t, router_bias)


def _plan_kernel(ek_ref, rk_ref, cnt_ref, pos_ref, texp_ref, nused_ref, tend_ref):
    rows = float(EXPERT_ROWS)
    cnt = cnt_ref[:, 0:1]
    tiles = jnp.floor((cnt + (rows - 1.0)) / rows)
    before = (lax.broadcasted_iota(jnp.int32, (N_EXPERTS, N_EXPERTS), 1)
              < lax.broadcasted_iota(jnp.int32, (N_EXPERTS, N_EXPERTS), 0))
    tile_start = jnp.dot(jnp.where(before, 1.0, 0.0), jnp.broadcast_to(tiles, (N_EXPERTS, 128)),
                         precision=lax.Precision.HIGHEST, preferred_element_type=F32)[:, 0:1]
    tile_end = tile_start + tiles
    row_start = tile_start * rows

    ek = ek_ref[...]
    pos = rk_ref[...].astype(F32)
    tile_id = lax.broadcasted_iota(jnp.int32, texp_ref.shape, 1).astype(F32)
    texp = jnp.zeros(texp_ref.shape, F32)
    for e in range(N_EXPERTS):
        pos = pos + jnp.where(ek == e, row_start[e:e + 1, :], 0.0)
        texp = texp + jnp.where(tile_id >= tile_end[e:e + 1, :], 1.0, 0.0)
    pos_ref[...] = pos.astype(jnp.int32)
    texp_ref[...] = jnp.minimum(texp, N_EXPERTS - 1.0).astype(jnp.int32)
    nused_ref[...] = jnp.broadcast_to(tile_end[N_EXPERTS - 1:N_EXPERTS, :], nused_ref.shape).astype(jnp.int32)
    tend_ref[...] = jnp.broadcast_to(tile_end, tend_ref.shape).astype(jnp.int32)


def _plan(ek, rk, cnt, n_tiles_pad):
    t = ek.shape[1]

    def full(shape):
        return pl.BlockSpec(shape, lambda: (0,) * len(shape))

    return pl.pallas_call(
        _plan_kernel,
        in_specs=[full(ek.shape), full(rk.shape), full(cnt.shape)],
        out_specs=[full((TOP_K, t)), full((1, n_tiles_pad)), full((1, 128)), full((N_EXPERTS, 128))],
        out_shape=[jax.ShapeDtypeStruct((TOP_K, t), jnp.int32),
                   jax.ShapeDtypeStruct((1, n_tiles_pad), jnp.int32),
                   jax.ShapeDtypeStruct((1, 128), jnp.int32),
                   jax.ShapeDtypeStruct((N_EXPERTS, 128), jnp.int32)],
        compiler_params=pltpu.CompilerParams(vmem_limit_bytes=VMEM_LIMIT),
        name="plan",
    )(ek, rk, cnt)


def _sc_mesh():
    return plsc.VectorSubcoreMesh(core_axis_name="c", subcore_axis_name="s")


def _sc_dispatch(rows, pos3, n_out):
    t = rows.shape[0]
    ch = SC_CHUNK
    per_w = (t // ch) // SC_WORKERS

    @functools.partial(
        pl.kernel, out_type=jax.ShapeDtypeStruct((n_out,) + rows.shape[1:], jnp.int32), mesh=_sc_mesh(),
        scratch_types=[pltpu.VMEM((TOP_K, ch), jnp.int32), pltpu.VMEM((ch,) + rows.shape[1:], jnp.int32),
                       pltpu.SemaphoreType.DMA])
    def k(rows_hbm, pos_hbm, out_hbm, idx_v, rows_v, sem):
        wid = lax.axis_index("s") * SC_CORES + lax.axis_index("c")

        @pl.loop(0, per_w)
        def _(j):
            c = wid * per_w + j
            pltpu.sync_copy(pos_hbm.at[c], idx_v)
            pltpu.sync_copy(rows_hbm.at[pl.ds(c * ch, ch)], rows_v)
            copies = [pltpu.async_copy(rows_v, out_hbm.at[idx_v.at[kk]], sem) for kk in range(TOP_K)]
            for cp in copies:
                cp.wait()

    return k(rows, pos3)


def _sc_gather(table, pos3, t):
    ch = SC_CHUNK
    per_w = (t // ch) // SC_WORKERS

    @functools.partial(
        pl.kernel, out_type=jax.ShapeDtypeStruct((TOP_K, t) + table.shape[1:], jnp.int32), mesh=_sc_mesh(),
        scratch_types=[pltpu.VMEM((TOP_K, ch), jnp.int32), pltpu.VMEM((ch,) + table.shape[1:], jnp.int32),
                       pltpu.SemaphoreType.DMA])
    def k(tab_hbm, pos_hbm, out_hbm, idx_v, rows_v, sem):
        wid = lax.axis_index("s") * SC_CORES + lax.axis_index("c")

        @pl.loop(0, per_w)
        def _(j):
            c = wid * per_w + j
            pltpu.sync_copy(pos_hbm.at[c], idx_v)
            for kk in range(TOP_K):
                pltpu.async_copy(tab_hbm.at[idx_v.at[kk]], rows_v, sem).wait()
                pltpu.sync_copy(rows_v, out_hbm.at[kk, pl.ds(c * ch, ch)])

    return k(table, pos3)


def _experts_kernel(texp_ref, nused_ref, tend_ref, xs_ref, weg_hbm, weu_hbm, wed_hbm, ys_ref,
                    wg_scr, wu_scr, wd_scr, wg_buf, wu_buf, wd_buf, sem, group_scr):
    step = pl.program_id(0)
    rows = EXPERT_ROWS
    half = D_MODEL // 2
    n_used = nused_ref[0]

    def weight_copies(e, slot):
        return [pltpu.make_async_copy(weg_hbm.at[e], wg_buf.at[slot], sem.at[slot, 0]),
                pltpu.make_async_copy(weu_hbm.at[e], wu_buf.at[slot], sem.at[slot, 1]),
                pltpu.make_async_copy(wed_hbm.at[e], wd_buf.at[slot], sem.at[slot, 2])]

    @pl.when(step == 0)
    def _():
        group_scr[0] = 0
        for cp in weight_copies(texp_ref[0], 0):
            cp.start()

    def row_tile(tile, x_view, y_view):
        expert = texp_ref[tile]
        used = tile < n_used
        new_expert = (tile == 0) | (expert != texp_ref[jnp.maximum(tile - 1, 0)])

        @pl.when(used & new_expert)
        def _():
            group = group_scr[0]
            slot = group % 2
            next_tile = tend_ref[expert]

            @pl.when(next_tile < n_used)
            def _():
                for cp in weight_copies(texp_ref[next_tile], 1 - slot):
                    cp.start()

            for cp in weight_copies(expert, slot):
                cp.wait()
            wg_scr[...] = wg_buf[slot].astype(BF16)
            wu_scr[...] = wu_buf[slot].astype(BF16)
            wd_scr[...] = wd_buf[slot].astype(BF16)
            group_scr[0] = group + 1

        @pl.when(used)
        def _():
            lo, hi = _unpack_pair(_load_token_words(x_view, (), rows))
            lo = lo.astype(BF16)
            hi = hi.astype(BF16)
            g = _dot(lo, wg_scr[0:half, :]) + _dot(hi, wg_scr[half:D_MODEL, :])
            u = _dot(lo, wu_scr[0:half, :]) + _dot(hi, wu_scr[half:D_MODEL, :])
            y = _dot((_silu(g) * u).astype(BF16), wd_scr[...])
            _store_token_words(y_view, _pack_pair(y[:, :half], y[:, half:]), rows)

        @pl.when(jnp.logical_not(used) & (step == (n_used - 1) // TILES_PER_STEP))
        def _():
            y_view[...] = jnp.zeros_like(y_view)

    for s in range(TILES_PER_STEP):
        view = pl.ds(s * rows * ROW_SLABS, rows * ROW_SLABS)
        row_tile(step * TILES_PER_STEP + s, xs_ref.at[view], ys_ref.at[view])


def _experts(texp, nused, tend, xs2d, weg, weu, wed, n_tiles):
    block = (TILES_PER_STEP * EXPERT_ROWS * ROW_SLABS, 128)
    hbm = pl.BlockSpec(memory_space=pl.ANY)

    def block_idx(j, te, nu, tn):
        return (jnp.minimum(j, (nu[0] - 1) // TILES_PER_STEP), 0)

    grid_spec = pltpu.PrefetchScalarGridSpec(
        num_scalar_prefetch=3,
        grid=(n_tiles // TILES_PER_STEP,),
        in_specs=[pl.BlockSpec(block, block_idx), hbm, hbm, hbm],
        out_specs=pl.BlockSpec(block, block_idx),
        scratch_shapes=[pltpu.VMEM((D_MODEL, EXPERT_DIM), BF16),
                        pltpu.VMEM((D_MODEL, EXPERT_DIM), BF16),
                        pltpu.VMEM((EXPERT_DIM, D_MODEL), BF16),
                        pltpu.VMEM((2, D_MODEL, EXPERT_DIM), F32),
                        pltpu.VMEM((2, D_MODEL, EXPERT_DIM), F32),
                        pltpu.VMEM((2, EXPERT_DIM, D_MODEL), F32),
                        pltpu.SemaphoreType.DMA((2, 3)),
                        pltpu.SMEM((1,), jnp.int32)],
    )
    return pl.pallas_call(
        _experts_kernel,
        grid_spec=grid_spec,
        out_shape=jax.ShapeDtypeStruct(xs2d.shape, jnp.int32),
        compiler_params=pltpu.CompilerParams(dimension_semantics=("arbitrary",),
                                             vmem_limit_bytes=VMEM_LIMIT),
        name="experts",
    )(texp, nused, tend, xs2d, weg, weu, wed)


def _final_kernel(x_ref, hp_ref, y8_ref, wt_ref, mod_ref, wsg_ref, wsu_ref, wsd_ref, fng_ref, o_ref):
    tm = x_ref.shape[0]
    lo, hi = _unpack_pair(_load_token_words(hp_ref, (), tm))
    hb = jnp.concatenate([lo, hi], axis=1).astype(BF16)
    shared = _dot((_silu(_dot(hb, wsg_ref[...])) * _dot(hb, wsu_ref[...])).astype(BF16), wsd_ref[...])
    wt = wt_ref[...]
    r_lo = jnp.zeros((tm, D_MODEL // 2), F32)
    r_hi = jnp.zeros((tm, D_MODEL // 2), F32)
    for k in range(TOP_K):
        lo, hi = _unpack_pair(_load_token_words(y8_ref, (k,), tm))
        wk = wt[:, k:k + 1]
        r_lo = r_lo + wk * lo
        r_hi = r_hi + wk * hi
    routed = jnp.concatenate([r_lo, r_hi], axis=1)
    y = x_ref[...] + mod_ref[0, 5:6, :] * (routed + shared)
    ms = jnp.mean(y * y, axis=-1, keepdims=True)
    o_ref[...] = y * lax.rsqrt(ms + EPS) * fng_ref[...]


def _final(x1, hp2d, y8, wtok, mod3, wsg, wsu, wsd, final_g, seq_len, mod_row_of_batch):
    t = x1.shape[0]
    tm = TM_FINAL

    def mod_idx(i):
        return (mod_row_of_batch((i * tm) // seq_len), 0, 0)

    def full(a):
        return pl.BlockSpec(a.shape, lambda i: (0,) * a.ndim)

    return pl.pallas_call(
        _final_kernel,
        grid=(t // tm,),
        in_specs=[pl.BlockSpec((tm, D_MODEL), lambda i: (i, 0)),
                  pl.BlockSpec((tm * ROW_SLABS, 128), lambda i: (i, 0)),
                  pl.BlockSpec((TOP_K, tm * ROW_SLABS, 128), lambda i: (0, i, 0)),
                  pl.BlockSpec((tm, 128), lambda i: (i, 0)),
                  pl.BlockSpec((1, 6, D_MODEL), mod_idx),
                  full(wsg), full(wsu), full(wsd), full(final_g)],
        out_specs=pl.BlockSpec((tm, D_MODEL), lambda i: (i, 0)),
        out_shape=jax.ShapeDtypeStruct((t, D_MODEL), F32),
        compiler_params=pltpu.CompilerParams(dimension_semantics=("parallel",),
                                             vmem_limit_bytes=VMEM_LIMIT),
        name="final",
    )(x1, hp2d, y8, wtok, mod3, wsg, wsu, wsd, final_g)


def _moe(x1, mod3, lw, seq_len, mod_row_of_batch):
    t = x1.shape[0]
    n_tiles = TOP_K * t // EXPERT_ROWS + N_EXPERTS
    n_tiles_pad = -(-n_tiles // 128) * 128
    hp2d, ek, rk, wtok, cnt = _router(x1, mod3, lw["norm2_g"], lw["w_router_t"], lw["router_bias"],
                                      seq_len, mod_row_of_batch)
    pos, texp, nused, tend = _plan(ek, rk, cnt, n_tiles_pad)
    pos3 = pos.reshape(TOP_K, t // SC_CHUNK, SC_CHUNK).transpose(1, 0, 2)
    xs = _sc_dispatch(hp2d.reshape(t, ROW_SLABS, 128), pos3, n_tiles * EXPERT_ROWS)
    ys2d = _experts(texp.reshape(-1), nused.reshape(-1), tend[:, 0], xs.reshape(-1, 128),
                    lw["weg"], lw["weu"], lw["wed"], n_tiles)
    y8 = _sc_gather(ys2d.reshape(-1, ROW_SLABS, 128), pos3, t)
    return _final(x1, hp2d, y8.reshape(TOP_K, t * ROW_SLABS, 128), wtok, mod3,
                  lw["wsg"], lw["wsu"], lw["wsd"], lw["final_g"], seq_len, mod_row_of_batch)


def _dft_tables(seq_len):
    gd = FOURIER_GROUP_DIM
    kc = np.arange(gd)
    ang_c = ((kc[:, None] * kc[None, :]) % gd) * (2.0 * math.pi / gd)
    cs = np.concatenate([np.cos(ang_c), np.sin(ang_c)], axis=1) * (gd ** -0.5)
    kl = np.arange(seq_len)
    ang_l = ((kl[:, None] * kl[None, :]) % seq_len) * (2.0 * math.pi / seq_len)
    cls = np.concatenate([np.cos(ang_l), -np.sin(ang_l)], axis=1) * (seq_len ** -0.5)
    return jnp.asarray(cs.astype(np.float32), dtype=BF16), jnp.asarray(cls.astype(np.float32), dtype=BF16)


def _rope_tables(length):
    rows = length // GRID_W
    r = np.repeat(np.arange(rows, dtype=np.float32), GRID_W)
    col = np.tile(np.arange(GRID_W, dtype=np.float32), rows)
    nf = RET_HEAD_DIM // 4
    inv = (np.float32(ROPE_BASE) ** (-np.arange(nf, dtype=np.float32) / np.float32(nf))).astype(np.float32)
    ar = r[:, None] * inv[None]
    ac = col[:, None] * inv[None]
    ang = np.concatenate([ar, ar, ac, ac], axis=-1).astype(np.float64)
    sign = np.where((np.arange(RET_HEAD_DIM) & nf) == 0, -1.0, 1.0)
    return (jnp.asarray(np.cos(ang).astype(np.float32)),
            jnp.asarray((np.sin(ang) * sign[None, :]).astype(np.float32)))


def _trunk_path(x, mod3, mod_row_of_batch, s0f, s0b, rope, lw):
    batch, seq_len, _ = x.shape
    x2d = x.reshape(batch * seq_len, D_MODEL)
    uf, q, k, v, sg, gf, gr = _inproj(x2d, mod3, lw["norm1_g"], lw["w_in"], seq_len, mod_row_of_batch, rope)
    r, s_f, s_b = _retention(q, k, v, sg, lw["dec"], lw["gn_g"], s0f, s0b, batch, seq_len)
    cs, cls = _dft_tables(seq_len)
    fmix = _fnet(uf, cs, cls, batch, seq_len)
    x1 = _merge(fmix, r, gf, gr, x2d, mod3, lw["w_four"], lw["w_ret"], lw["w_o"], seq_len, mod_row_of_batch)
    y = _moe(x1, mod3, lw, seq_len, mod_row_of_batch)
    return y.reshape(batch, seq_len, D_MODEL), s_f, s_b


def kernel(x_prompt, x_sample, state_ret_fwd, state_ret_bwd, c, c_ctx, w_ada, b_ada, norm1_g, norm2_g, w_in,
           ret_decay_fwd, ret_decay_bwd, ret_gn_g, w_four_out, w_ret_out, w_out, w_router, router_bias,
           w_exp_gate, w_exp_up, w_exp_down, w_shared_gate, w_shared_up, w_shared_down, final_norm_g):
    depth = w_ada.shape[0]
    assert depth == 1, "final norm is fused into the last layer's MoE kernel"
    n_ctx, n_lat = x_prompt.shape[0], x_sample.shape[0]
    cond = jnp.concatenate([c_ctx[None, :], c], axis=0)
    cond = jnp.pad(cond, ((0, (-cond.shape[0]) % 8), (0, 0)))
    rope = _rope_tables(x_sample.shape[1])
    zeros = jnp.zeros((n_ctx, N_RET_HEADS, RET_HEAD_DIM, RET_HEAD_DIM), F32)

    layer = 0
    mod = _ada(cond, w_ada[layer], b_ada[layer][None, :])
    mod3 = mod.reshape(mod.shape[0], 6, D_MODEL)
    dec = jnp.stack([ret_decay_fwd[layer], ret_decay_bwd[layer]], axis=1)
    lw = {
        "norm1_g": norm1_g[layer][None, :],
        "norm2_g": norm2_g[layer][None, :],
        "w_in": w_in[layer].astype(BF16),
        "dec": jnp.broadcast_to(dec[:, :, None], (N_RET_HEADS, 2, RET_HEAD_DIM)).astype(F32),
        "gn_g": ret_gn_g[layer][None, :],
        "w_four": w_four_out[layer].astype(BF16),
        "w_ret": w_ret_out[layer].astype(BF16),
        "w_o": w_out[layer].astype(BF16),
        "w_router_t": w_router[layer].T,
        "router_bias": router_bias[layer][:, None],
        "weg": w_exp_gate[layer],
        "weu": w_exp_up[layer],
        "wed": w_exp_down[layer],
        "wsg": w_shared_gate[layer].astype(BF16),
        "wsu": w_shared_up[layer].astype(BF16),
        "wsd": w_shared_down[layer].astype(BF16),
        "final_g": final_norm_g[None, :],
    }
    y_prompt, s_f, s_b = _trunk_path(x_prompt, mod3, lambda b: 0, zeros, zeros, None, lw)
    y_sample, _, _ = _trunk_path(x_sample, mod3, lambda b: 1 + b, state_ret_fwd[:, layer],
                                 state_ret_bwd[:, layer], rope, lw)
    return (y_prompt, y_sample, s_f[:, None], s_b[:, None])
```

```python
import functools
import math

import jax
import jax.numpy as jnp
import numpy as np
from jax import lax
from jax.experimental import pallas as pl
from jax.experimental.pallas import tpu as pltpu
from jax.experimental.pallas import tpu_sc as plsc

F32 = jnp.float32
BF16 = jnp.bfloat16

D_MODEL = 1024
GRID_W = 64
N_FOURIER_GROUPS = 8
FOURIER_GROUP_DIM = 128
N_RET_HEADS = 4
RET_HEAD_DIM = 128
RET_WIDTH = N_RET_HEADS * RET_HEAD_DIM
CHUNK = 128
N_EXPERTS = 64
N_EXPERT_GROUPS = 8
EXPERTS_PER_GROUP = N_EXPERTS // N_EXPERT_GROUPS
TOPK_GROUPS = 4
TOP_K = 8
EXPERT_DIM = 256
ROUTED_SCALE = 2.5
ROPE_BASE = 10000.0
EPS = 1e-6
Q_SCALE = RET_HEAD_DIM ** -0.5

_C_UF = (0, 1024)
_C_Q = (1024, 1536)
_C_K = (1536, 2048)
_C_V = (2048, 2560)
_C_G = (2560, 3072)
_C_GF = (3072, 4096)
_C_GR = (4096, 5120)

VMEM_LIMIT = 56 * 1024 * 1024

TM_PROJ = 512
FNET_ROWS = 256
RET_ALL_HEADS_MAX_LEN = 1024
TM_ROUTER = 1024
TM_FINAL = 512
EXPERT_ROWS = 512
TILES_PER_STEP = 2
ROW_SLABS = 4
SC_CORES = 2
SC_WORKERS = 32
SC_CHUNK = 128


def _silu(x):
    return x * jax.nn.sigmoid(x)


def _dot(a, b):
    return jnp.dot(a, b, preferred_element_type=F32)


def _rms_mod(x, g, shift, scale):
    ms = jnp.mean(x * x, axis=-1, keepdims=True)
    y = x * lax.rsqrt(ms + EPS) * g
    return y * (1.0 + scale) + shift


def _ada_kernel(cond_ref, w_ref, b_ref, o_ref):
    s = _silu(cond_ref[...]).astype(BF16)
    o_ref[...] = _dot(s, w_ref[...].astype(BF16)) + b_ref[...]


def _ada(cond, w_ada, b_ada):
    rows, n = cond.shape[0], w_ada.shape[1]
    tn = 1536
    return pl.pallas_call(
        _ada_kernel,
        grid=(n // tn,),
        in_specs=[pl.BlockSpec((rows, D_MODEL), lambda j: (0, 0)),
                  pl.BlockSpec((D_MODEL, tn), lambda j: (0, j)),
                  pl.BlockSpec((1, tn), lambda j: (0, j))],
        out_specs=pl.BlockSpec((rows, tn), lambda j: (0, j)),
        out_shape=jax.ShapeDtypeStruct((rows, n), F32),
        compiler_params=pltpu.CompilerParams(vmem_limit_bytes=VMEM_LIMIT),
        name="ada",
    )(cond, w_ada, b_ada)


def _rope_head(x, cos, sin_signed, first_half):
    partner = jnp.where(first_half, pltpu.roll(x, 96, 1), pltpu.roll(x, 32, 1))
    return x * cos + partner * sin_signed


def _inproj_kernel(*refs, use_rope):
    if use_rope:
        x_ref, mod_ref, g_ref, w_ref, cos_ref, sin_ref = refs[:6]
        outs = refs[6:]
    else:
        x_ref, mod_ref, g_ref, w_ref = refs[:4]
        outs = refs[4:]
    uf_o, q_o, k_o, v_o, sg_o, gf_o, gr_o = outs

    h = _rms_mod(x_ref[...], g_ref[...], mod_ref[0, 0:1, :], mod_ref[0, 1:2, :])
    hb = h.astype(BF16)

    def proj(cols):
        return _dot(hb, w_ref[:, cols[0]:cols[1]])

    uf_o[...] = proj(_C_UF).astype(BF16)
    q = proj(_C_Q)
    k = proj(_C_K)
    if use_rope:
        cos = cos_ref[...]
        sin_signed = sin_ref[...]
        lane = lax.broadcasted_iota(jnp.int32, cos.shape, 1)
        first_half = (lane & 32) == 0
        for hd in range(N_RET_HEADS):
            sl = slice(hd * RET_HEAD_DIM, (hd + 1) * RET_HEAD_DIM)
            q_o[:, sl] = (_rope_head(q[:, sl], cos, sin_signed, first_half) * Q_SCALE).astype(BF16)
            k_o[:, sl] = _rope_head(k[:, sl], cos, sin_signed, first_half).astype(BF16)
    else:
        q_o[...] = (q * Q_SCALE).astype(BF16)
        k_o[...] = k.astype(BF16)
    v_o[...] = proj(_C_V).astype(BF16)
    sg_o[...] = _silu(proj(_C_G)).astype(BF16)
    gf_o[...] = jax.nn.sigmoid(proj(_C_GF)).astype(BF16)
    gr_o[...] = jax.nn.sigmoid(proj(_C_GR)).astype(BF16)


def _inproj(x2d, mod3, norm_g, w_in_bf, seq_len, mod_row_of_batch, rope):
    t = x2d.shape[0]
    tm = TM_PROJ
    tiles_per_seq = max(seq_len // tm, 1)

    def mod_idx(i):
        return (mod_row_of_batch((i * tm) // seq_len), 0, 0)

    in_specs = [pl.BlockSpec((tm, D_MODEL), lambda i: (i, 0)),
                pl.BlockSpec((1, 6, D_MODEL), mod_idx),
                pl.BlockSpec((1, D_MODEL), lambda i: (0, 0)),
                pl.BlockSpec(w_in_bf.shape, lambda i: (0, 0))]
    args = [x2d, mod3, norm_g, w_in_bf]
    if rope is not None:
        in_specs += [pl.BlockSpec((tm, RET_HEAD_DIM), lambda i: (i % tiles_per_seq, 0))] * 2
        args += list(rope)
    widths = [1024, RET_WIDTH, RET_WIDTH, RET_WIDTH, RET_WIDTH, 1024, 1024]
    return pl.pallas_call(
        functools.partial(_inproj_kernel, use_rope=rope is not None),
        grid=(t // tm,),
        in_specs=in_specs,
        out_specs=[pl.BlockSpec((tm, w), lambda i: (i, 0)) for w in widths],
        out_shape=[jax.ShapeDtypeStruct((t, w), BF16) for w in widths],
        compiler_params=pltpu.CompilerParams(dimension_semantics=("parallel",),
                                             vmem_limit_bytes=VMEM_LIMIT),
        name="inproj",
    )(*args)


def _retention_kernel(q_ref, k_ref, v_ref, sg_ref, dec_ref, gn_ref, s0f_ref, s0b_ref,
                      r_ref, sfo_ref, sbo_ref, tab_scr, gc_scr):
    n_chunks = q_ref.shape[0] // CHUNK
    hd = RET_HEAD_DIM
    heads_here = q_ref.shape[1] // hd
    first_head = pl.program_id(1) * heads_here

    @pl.when((pl.program_id(0) == 0) & (pl.program_id(1) == 0))
    def _():
        row = lax.broadcasted_iota(jnp.int32, (CHUNK, CHUNK), 0).astype(F32)
        col = lax.broadcasted_iota(jnp.int32, (CHUNK, CHUNK), 1).astype(F32)
        diff = row - col
        for h in range(N_RET_HEADS):
            dec = dec_ref[h]
            lg = jnp.minimum(dec, 0.0) - jnp.log1p(jnp.exp(-jnp.abs(dec)))
            lgf = lg[0:1, :]
            lgb = lg[1:2, :]
            tab_scr[h, 0] = jnp.exp(jnp.where(diff >= 0, lgf * diff, lgb * (-diff)))
            tab_scr[h, 1] = jnp.exp(lgf * (row + 1.0))
            tab_scr[h, 2] = jnp.exp(lgb * (CHUNK - row))
            tab_scr[h, 3] = jnp.exp(lgf * (CHUNK - 1.0 - col))
            tab_scr[h, 4] = jnp.exp(lgb * col)
            gc_scr[h] = jnp.exp(lg * CHUNK)

    def rows(n):
        return slice(n * CHUNK, (n + 1) * CHUNK)

    for h in range(heads_here):
        cols = slice(h * hd, (h + 1) * hd)
        head = first_head + h
        decay, qw_f, qw_b, kwt_f, kwt_b = (tab_scr[head, i] for i in range(5))
        gc = gc_scr[head]
        gc_f = gc[0:1, :]
        gc_b = gc[1:2, :]

        kv_f, kv_b = [], []
        for n in range(n_chunks):
            kt = k_ref[rows(n), cols].astype(F32).T
            vn = v_ref[rows(n), cols]
            kv_f.append(_dot((kt * kwt_f).astype(BF16), vn))
            kv_b.append(_dot((kt * kwt_b).astype(BF16), vn))

        s = s0f_ref[h]
        prev_f = []
        for n in range(n_chunks):
            prev_f.append(s.astype(BF16))
            s = gc_f * s + kv_f[n]
        sfo_ref[h] = s
        s = s0b_ref[h]
        prev_b = [None] * n_chunks
        for n in reversed(range(n_chunks)):
            prev_b[n] = s.astype(BF16)
            s = gc_b * s + kv_b[n]
        sbo_ref[h] = s

        gn = gn_ref[:, cols]
        for n in range(n_chunks):
            qn = q_ref[rows(n), cols]
            qf = qn.astype(F32)
            scores = lax.dot_general(qn, k_ref[rows(n), cols], (((1,), (1,)), ((), ())),
                                     preferred_element_type=F32)
            o = _dot((scores * decay).astype(BF16), v_ref[rows(n), cols])
            o = o + _dot((qf * qw_f).astype(BF16), prev_f[n])
            o = o + _dot((qf * qw_b).astype(BF16), prev_b[n])
            mu = jnp.mean(o, axis=-1, keepdims=True)
            d = o - mu
            var = jnp.mean(d * d, axis=-1, keepdims=True)
            on = d * lax.rsqrt(var + EPS) * gn
            r_ref[rows(n), cols] = (on * sg_ref[rows(n), cols].astype(F32)).astype(BF16)


def _retention(q, k, v, sg, dec, gn_g, s0f, s0b, batch, seq_len):
    hd = RET_HEAD_DIM
    heads_per_step = N_RET_HEADS if seq_len <= RET_ALL_HEADS_MAX_LEN else 1
    width = heads_per_step * hd
    tok_spec = pl.BlockSpec((seq_len, width), lambda b, g: (b, g))
    st_spec = pl.BlockSpec((None, heads_per_step, hd, hd), lambda b, g: (b, g, 0, 0))
    st_shape = jax.ShapeDtypeStruct((batch, N_RET_HEADS, hd, hd), F32)
    return pl.pallas_call(
        _retention_kernel,
        grid=(batch, N_RET_HEADS // heads_per_step),
        in_specs=[tok_spec, tok_spec, tok_spec, tok_spec,
                  pl.BlockSpec(dec.shape, lambda b, g: (0, 0, 0)),
                  pl.BlockSpec((1, width), lambda b, g: (0, g)),
                  st_spec, st_spec],
        out_specs=[tok_spec, st_spec, st_spec],
        out_shape=[jax.ShapeDtypeStruct((batch * seq_len, RET_WIDTH), BF16), st_shape, st_shape],
        scratch_shapes=[pltpu.VMEM((N_RET_HEADS, 5, CHUNK, CHUNK), F32),
                        pltpu.VMEM((N_RET_HEADS, 2, hd), F32)],
        compiler_params=pltpu.CompilerParams(dimension_semantics=("arbitrary", "arbitrary"),
                                             vmem_limit_bytes=VMEM_LIMIT),
        name="retention",
    )(q, k, v, sg, dec, gn_g, s0f, s0b)


def _fnet_kernel(uf_ref, cs_ref, cls_ref, o_ref, xcs_ref):
    seq_len = uf_ref.shape[0]
    gd = FOURIER_GROUP_DIM

    @pl.when(pl.program_id(1) == 0)
    def _():
        for g in range(N_FOURIER_GROUPS):
            x = _dot(uf_ref[:, g * gd:(g + 1) * gd], cs_ref[...])
            xcs_ref[0:seq_len, g * gd:(g + 1) * gd] = x[:, :gd].astype(BF16)
            xcs_ref[seq_len:2 * seq_len, g * gd:(g + 1) * gd] = x[:, gd:].astype(BF16)

    o_ref[...] = _dot(cls_ref[...], xcs_ref[...]).astype(BF16)


def _fnet(uf, cs, cls, batch, seq_len):
    rb = FNET_ROWS
    nr = seq_len // rb
    return pl.pallas_call(
        _fnet_kernel,
        grid=(batch, nr),
        in_specs=[pl.BlockSpec((seq_len, D_MODEL), lambda b, r: (b, 0)),
                  pl.BlockSpec(cs.shape, lambda b, r: (0, 0)),
                  pl.BlockSpec((rb, 2 * seq_len), lambda b, r: (r, 0))],
        out_specs=pl.BlockSpec((rb, D_MODEL), lambda b, r: (b * nr + r, 0)),
        out_shape=jax.ShapeDtypeStruct((batch * seq_len, D_MODEL), BF16),
        scratch_shapes=[pltpu.VMEM((2 * seq_len, D_MODEL), BF16)],
        compiler_params=pltpu.CompilerParams(dimension_semantics=("parallel", "arbitrary"),
                                             vmem_limit_bytes=VMEM_LIMIT),
        name="fnet",
    )(uf, cs, cls)


def _merge_kernel(fm_ref, r_ref, gf_ref, gr_ref, x_ref, mod_ref, wf_ref, wr_ref, wo_ref, o_ref):
    f_out = _dot(fm_ref[...], wf_ref[...])
    r_out = _dot(r_ref[...], wr_ref[...])
    merged = gf_ref[...].astype(F32) * f_out + gr_ref[...].astype(F32) * r_out
    mix = _dot(merged.astype(BF16), wo_ref[...])
    o_ref[...] = x_ref[...] + mod_ref[0, 2:3, :] * mix


def _merge(fmix, r, gf, gr, x2d, mod3, w_four, w_ret, w_o, seq_len, mod_row_of_batch):
    t = x2d.shape[0]
    tm = TM_PROJ

    def mod_idx(i):
        return (mod_row_of_batch((i * tm) // seq_len), 0, 0)

    def tok(w):
        return pl.BlockSpec((tm, w), lambda i: (i, 0))

    def full(a):
        return pl.BlockSpec(a.shape, lambda i: (0, 0))

    return pl.pallas_call(
        _merge_kernel,
        grid=(t // tm,),
        in_specs=[tok(D_MODEL), tok(RET_WIDTH), tok(D_MODEL), tok(D_MODEL), tok(D_MODEL),
                  pl.BlockSpec((1, 6, D_MODEL), mod_idx), full(w_four), full(w_ret), full(w_o)],
        out_specs=tok(D_MODEL),
        out_shape=jax.ShapeDtypeStruct((t, D_MODEL), F32),
        compiler_params=pltpu.CompilerParams(dimension_semantics=("parallel",),
                                             vmem_limit_bytes=VMEM_LIMIT),
        name="merge",
    )(fmix, r, gf, gr, x2d, mod3, w_four, w_ret, w_o)


def _pack_pair(lo_f32, hi_f32):
    lo = lax.bitcast_convert_type(lo_f32.astype(BF16).astype(F32), jnp.uint32)
    hi = lax.bitcast_convert_type(hi_f32.astype(BF16).astype(F32), jnp.uint32)
    return lax.bitcast_convert_type((lo >> 16) | hi, jnp.int32)


def _unpack_pair(words_i32):
    w = lax.bitcast_convert_type(words_i32, jnp.uint32)
    lo = lax.bitcast_convert_type(w << 16, F32)
    hi = lax.bitcast_convert_type(w & jnp.uint32(0xFFFF0000), F32)
    return lo, hi


def _load_token_words(ref, lead, n_tok):
    parts = []
    for s in range(ROW_SLABS):
        idx = (pl.ds(s, n_tok, stride=ROW_SLABS), slice(None))
        parts.append(ref[lead + idx] if lead else ref[idx])
    return jnp.concatenate(parts, axis=1)


def _store_token_words(ref, words, n_tok):
    for s in range(ROW_SLABS):
        ref[pl.ds(s, n_tok, stride=ROW_SLABS), :] = words[:, s * 128:(s + 1) * 128]


def _route(scores, biased):
    tokens = scores.shape[1]
    neg = -jnp.inf
    epg = EXPERTS_PER_GROUP
    iota_g = lax.broadcasted_iota(jnp.int32, (epg, tokens), 0).astype(F32)

    def pick_first_max(cur, iota, size):
        m = jnp.max(cur, axis=0, keepdims=True)
        idx = jnp.min(jnp.where(cur == m, iota, float(size)), axis=0, keepdims=True)
        return m, idx, iota == idx

    group_scores = []
    for g in range(N_EXPERT_GROUPS):
        vals = biased[g * epg:(g + 1) * epg, :]
        m1, _, hit = pick_first_max(vals, iota_g, epg)
        m2 = jnp.max(jnp.where(hit, neg, vals), axis=0, keepdims=True)
        group_scores.append(m1 + m2)
    cur = jnp.concatenate(group_scores, axis=0)
    group_sel = jnp.zeros_like(cur)
    for _ in range(TOPK_GROUPS):
        _, _, hit = pick_first_max(cur, iota_g, N_EXPERT_GROUPS)
        group_sel = jnp.where(hit, 1.0, group_sel)
        cur = jnp.where(hit, neg, cur)
    masked = jnp.concatenate(
        [jnp.where(group_sel[g:g + 1, :] > 0.0, biased[g * epg:(g + 1) * epg, :], neg)
         for g in range(N_EXPERT_GROUPS)], axis=0)
    iota_e = lax.broadcasted_iota(jnp.int32, masked.shape, 0).astype(F32)
    sel = jnp.zeros_like(masked)
    cur = masked
    picks = []
    for _ in range(TOP_K):
        _, idx, hit = pick_first_max(cur, iota_e, N_EXPERTS)
        picks.append(idx)
        sel = jnp.where(hit, 1.0, sel)
        cur = jnp.where(hit, neg, cur)
    w = scores * sel
    return w / jnp.sum(w, axis=0, keepdims=True) * ROUTED_SCALE, sel, picks


def _router_kernel(x_ref, mod_ref, g2_ref, wrt_ref, rb_ref, hp_ref, ek_ref, rk_ref, wt_ref, cnt_ref, run_scr):
    tm = x_ref.shape[0]

    @pl.when(pl.program_id(0) == 0)
    def _():
        run_scr[...] = jnp.zeros_like(run_scr)

    h = _rms_mod(x_ref[...], g2_ref[...], mod_ref[0, 3:4, :], mod_ref[0, 4:5, :])
    half = D_MODEL // 2
    _store_token_words(hp_ref, _pack_pair(h[:, :half], h[:, half:]), tm)

    logits_t = lax.dot_general(wrt_ref[...], h, (((1,), (1,)), ((), ())),
                               precision=lax.Precision.HIGHEST, preferred_element_type=F32)
    scores = jax.nn.sigmoid(logits_t)
    comb_t, sel, picks = _route(scores, scores + rb_ref[...])

    earlier = (lax.broadcasted_iota(jnp.int32, (tm, tm), 0) < lax.broadcasted_iota(jnp.int32, (tm, tm), 1))
    rank_t = _dot(sel.astype(BF16), jnp.where(earlier, 1.0, 0.0).astype(BF16)) + run_scr[...]
    run_scr[...] += jnp.sum(sel, axis=1, keepdims=True)
    cnt_ref[...] = jnp.broadcast_to(run_scr[...], cnt_ref.shape)

    iota_e = lax.broadcasted_iota(jnp.int32, sel.shape, 0).astype(F32)
    ranks, weights = [], []
    for idx in picks:
        hit = iota_e == idx
        ranks.append(jnp.sum(jnp.where(hit, rank_t, 0.0), axis=0, keepdims=True))
        weights.append(jnp.sum(jnp.where(hit, comb_t, 0.0), axis=0, keepdims=True))
    ek_ref[...] = jnp.concatenate(picks, axis=0).astype(jnp.int32)
    rk_ref[...] = jnp.concatenate(ranks, axis=0).astype(jnp.int32)
    w_pad = jnp.concatenate(weights + [jnp.zeros((128 - TOP_K, tm), F32)], axis=0)
    wt_ref[...] = w_pad.T


def _router(x1, mod3, norm2_g, w_router_t, router_bias, seq_len, mod_row_of_batch):
    t = x1.shape[0]
    tm = TM_ROUTER

    def mod_idx(i):
        return (mod_row_of_batch((i * tm) // seq_len), 0, 0)

    def full(a):
        return pl.BlockSpec(a.shape, lambda i: (0,) * a.ndim)

    return pl.pallas_call(
        _router_kernel,
        grid=(t // tm,),
        in_specs=[pl.BlockSpec((tm, D_MODEL), lambda i: (i, 0)),
                  pl.BlockSpec((1, 6, D_MODEL), mod_idx),
                  full(norm2_g), full(w_router_t), full(router_bias)],
        out_specs=[pl.BlockSpec((tm * ROW_SLABS, 128), lambda i: (i, 0)),
                   pl.BlockSpec((TOP_K, tm), lambda i: (0, i)),
                   pl.BlockSpec((TOP_K, tm), lambda i: (0, i)),
                   pl.BlockSpec((tm, 128), lambda i: (i, 0)),
                   pl.BlockSpec((N_EXPERTS, 128), lambda i: (0, 0))],
        out_shape=[jax.ShapeDtypeStruct((t * ROW_SLABS, 128), jnp.int32),
                   jax.ShapeDtypeStruct((TOP_K, t), jnp.int32),
                   jax.ShapeDtypeStruct((TOP_K, t), jnp.int32),
                   jax.ShapeDtypeStruct((t, 128), F32),
                   jax.ShapeDtypeStruct((N_EXPERTS, 128), F32)],
        scratch_shapes=[pltpu.VMEM((N_EXPERTS, 1), F32)],
        compiler_params=pltpu.CompilerParams(dimension_semantics=("arbitrary",),
                                             vmem_limit_bytes=VMEM_LIMIT),
        name="router",
    )(x1, mod3, norm2_g, w_router_t, router_bias)


def _plan_kernel(ek_ref, rk_ref, cnt_ref, pos_ref, texp_ref, nused_ref, tend_ref):
    rows = float(EXPERT_ROWS)
    cnt = cnt_ref[:, 0:1]
    tiles = jnp.floor((cnt + (rows - 1.0)) / rows)
    before = (lax.broadcasted_iota(jnp.int32, (N_EXPERTS, N_EXPERTS), 1)
              < lax.broadcasted_iota(jnp.int32, (N_EXPERTS, N_EXPERTS), 0))
    tile_start = jnp.dot(jnp.where(before, 1.0, 0.0), jnp.broadcast_to(tiles, (N_EXPERTS, 128)),
                         precision=lax.Precision.HIGHEST, preferred_element_type=F32)[:, 0:1]
    tile_end = tile_start + tiles
    row_start = tile_start * rows

    ek = ek_ref[...]
    pos = rk_ref[...].astype(F32)
    tile_id = lax.broadcasted_iota(jnp.int32, texp_ref.shape, 1).astype(F32)
    texp = jnp.zeros(texp_ref.shape, F32)
    for e in range(N_EXPERTS):
        pos = pos + jnp.where(ek == e, row_start[e:e + 1, :], 0.0)
        texp = texp + jnp.where(tile_id >= tile_end[e:e + 1, :], 1.0, 0.0)
    pos_ref[...] = pos.astype(jnp.int32)
    texp_ref[...] = jnp.minimum(texp, N_EXPERTS - 1.0).astype(jnp.int32)
    nused_ref[...] = jnp.broadcast_to(tile_end[N_EXPERTS - 1:N_EXPERTS, :], nused_ref.shape).astype(jnp.int32)
    tend_ref[...] = jnp.broadcast_to(tile_end, tend_ref.shape).astype(jnp.int32)


def _plan(ek, rk, cnt, n_tiles_pad):
    t = ek.shape[1]

    def full(shape):
        return pl.BlockSpec(shape, lambda: (0,) * len(shape))

    return pl.pallas_call(
        _plan_kernel,
        in_specs=[full(ek.shape), full(rk.shape), full(cnt.shape)],
        out_specs=[full((TOP_K, t)), full((1, n_tiles_pad)), full((1, 128)), full((N_EXPERTS, 128))],
        out_shape=[jax.ShapeDtypeStruct((TOP_K, t), jnp.int32),
                   jax.ShapeDtypeStruct((1, n_tiles_pad), jnp.int32),
                   jax.ShapeDtypeStruct((1, 128), jnp.int32),
                   jax.ShapeDtypeStruct((N_EXPERTS, 128), jnp.int32)],
        compiler_params=pltpu.CompilerParams(vmem_limit_bytes=VMEM_LIMIT),
        name="plan",
    )(ek, rk, cnt)


def _sc_mesh():
    return plsc.VectorSubcoreMesh(core_axis_name="c", subcore_axis_name="s")


def _sc_dispatch(rows, pos3, n_out):
    t = rows.shape[0]
    ch = SC_CHUNK
    per_w = (t // ch) // SC_WORKERS

    @functools.partial(
        pl.kernel, out_type=jax.ShapeDtypeStruct((n_out,) + rows.shape[1:], jnp.int32), mesh=_sc_mesh(),
        scratch_types=[pltpu.VMEM((TOP_K, ch), jnp.int32), pltpu.VMEM((ch,) + rows.shape[1:], jnp.int32),
                       pltpu.SemaphoreType.DMA])
    def k(rows_hbm, pos_hbm, out_hbm, idx_v, rows_v, sem):
        wid = lax.axis_index("s") * SC_CORES + lax.axis_index("c")

        @pl.loop(0, per_w)
        def _(j):
            c = wid * per_w + j
            pltpu.sync_copy(pos_hbm.at[c], idx_v)
            pltpu.sync_copy(rows_hbm.at[pl.ds(c * ch, ch)], rows_v)
            copies = [pltpu.async_copy(rows_v, out_hbm.at[idx_v.at[kk]], sem) for kk in range(TOP_K)]
            for cp in copies:
                cp.wait()

    return k(rows, pos3)


def _sc_gather(table, pos3, t):
    ch = SC_CHUNK
    per_w = (t // ch) // SC_WORKERS

    @functools.partial(
        pl.kernel, out_type=jax.ShapeDtypeStruct((TOP_K, t) + table.shape[1:], jnp.int32), mesh=_sc_mesh(),
        scratch_types=[pltpu.VMEM((TOP_K, ch), jnp.int32), pltpu.VMEM((ch,) + table.shape[1:], jnp.int32),
                       pltpu.SemaphoreType.DMA])
    def k(tab_hbm, pos_hbm, out_hbm, idx_v, rows_v, sem):
        wid = lax.axis_index("s") * SC_CORES + lax.axis_index("c")

        @pl.loop(0, per_w)
        def _(j):
            c = wid * per_w + j
            pltpu.sync_copy(pos_hbm.at[c], idx_v)
            for kk in range(TOP_K):
                pltpu.async_copy(tab_hbm.at[idx_v.at[kk]], rows_v, sem).wait()
                pltpu.sync_copy(rows_v, out_hbm.at[kk, pl.ds(c * ch, ch)])

    return k(table, pos3)


def _experts_kernel(texp_ref, nused_ref, tend_ref, xs_ref, weg_hbm, weu_hbm, wed_hbm, ys_ref,
                    wg_scr, wu_scr, wd_scr, wg_buf, wu_buf, wd_buf, sem, group_scr):
    step = pl.program_id(0)
    rows = EXPERT_ROWS
    half = D_MODEL // 2
    n_used = nused_ref[0]

    def weight_copies(e, slot):
        return [pltpu.make_async_copy(weg_hbm.at[e], wg_buf.at[slot], sem.at[slot, 0]),
                pltpu.make_async_copy(weu_hbm.at[e], wu_buf.at[slot], sem.at[slot, 1]),
                pltpu.make_async_copy(wed_hbm.at[e], wd_buf.at[slot], sem.at[slot, 2])]

    @pl.when(step == 0)
    def _():
        group_scr[0] = 0
        for cp in weight_copies(texp_ref[0], 0):
            cp.start()

    def row_tile(tile, x_view, y_view):
        expert = texp_ref[tile]
        used = tile < n_used
        new_expert = (tile == 0) | (expert != texp_ref[jnp.maximum(tile - 1, 0)])

        @pl.when(used & new_expert)
        def _():
            group = group_scr[0]
            slot = group % 2
            next_tile = tend_ref[expert]

            @pl.when(next_tile < n_used)
            def _():
                for cp in weight_copies(texp_ref[next_tile], 1 - slot):
                    cp.start()

            for cp in weight_copies(expert, slot):
                cp.wait()
            wg_scr[...] = wg_buf[slot].astype(BF16)
            wu_scr[...] = wu_buf[slot].astype(BF16)
            wd_scr[...] = wd_buf[slot].astype(BF16)
            group_scr[0] = group + 1

        @pl.when(used)
        def _():
            lo, hi = _unpack_pair(_load_token_words(x_view, (), rows))
            lo = lo.astype(BF16)
            hi = hi.astype(BF16)
            g = _dot(lo, wg_scr[0:half, :]) + _dot(hi, wg_scr[half:D_MODEL, :])
            u = _dot(lo, wu_scr[0:half, :]) + _dot(hi, wu_scr[half:D_MODEL, :])
            y = _dot((_silu(g) * u).astype(BF16), wd_scr[...])
            _store_token_words(y_view, _pack_pair(y[:, :half], y[:, half:]), rows)

        @pl.when(jnp.logical_not(used) & (step == (n_used - 1) // TILES_PER_STEP))
        def _():
            y_view[...] = jnp.zeros_like(y_view)

    for s in range(TILES_PER_STEP):
        view = pl.ds(s * rows * ROW_SLABS, rows * ROW_SLABS)
        row_tile(step * TILES_PER_STEP + s, xs_ref.at[view], ys_ref.at[view])


def _experts(texp, nused, tend, xs2d, weg, weu, wed, n_tiles):
    block = (TILES_PER_STEP * EXPERT_ROWS * ROW_SLABS, 128)
    hbm = pl.BlockSpec(memory_space=pl.ANY)

    def block_idx(j, te, nu, tn):
        return (jnp.minimum(j, (nu[0] - 1) // TILES_PER_STEP), 0)

    grid_spec = pltpu.PrefetchScalarGridSpec(
        num_scalar_prefetch=3,
        grid=(n_tiles // TILES_PER_STEP,),
        in_specs=[pl.BlockSpec(block, block_idx), hbm, hbm, hbm],
        out_specs=pl.BlockSpec(block, block_idx),
        scratch_shapes=[pltpu.VMEM((D_MODEL, EXPERT_DIM), BF16),
                        pltpu.VMEM((D_MODEL, EXPERT_DIM), BF16),
                        pltpu.VMEM((EXPERT_DIM, D_MODEL), BF16),
                        pltpu.VMEM((2, D_MODEL, EXPERT_DIM), F32),
                        pltpu.VMEM((2, D_MODEL, EXPERT_DIM), F32),
                        pltpu.VMEM((2, EXPERT_DIM, D_MODEL), F32),
                        pltpu.SemaphoreType.DMA((2, 3)),
                        pltpu.SMEM((1,), jnp.int32)],
    )
    return pl.pallas_call(
        _experts_kernel,
        grid_spec=grid_spec,
        out_shape=jax.ShapeDtypeStruct(xs2d.shape, jnp.int32),
        compiler_params=pltpu.CompilerParams(dimension_semantics=("arbitrary",),
                                             vmem_limit_bytes=VMEM_LIMIT),
        name="experts",
    )(texp, nused, tend, xs2d, weg, weu, wed)


def _final_kernel(x_ref, hp_ref, y8_ref, wt_ref, mod_ref, wsg_ref, wsu_ref, wsd_ref, fng_ref, o_ref):
    tm = x_ref.shape[0]
    lo, hi = _unpack_pair(_load_token_words(hp_ref, (), tm))
    hb = jnp.concatenate([lo, hi], axis=1).astype(BF16)
    shared = _dot((_silu(_dot(hb, wsg_ref[...])) * _dot(hb, wsu_ref[...])).astype(BF16), wsd_ref[...])
    wt = wt_ref[...]
    r_lo = jnp.zeros((tm, D_MODEL // 2), F32)
    r_hi = jnp.zeros((tm, D_MODEL // 2), F32)
    for k in range(TOP_K):
        lo, hi = _unpack_pair(_load_token_words(y8_ref, (k,), tm))
        wk = wt[:, k:k + 1]
        r_lo = r_lo + wk * lo
        r_hi = r_hi + wk * hi
    routed = jnp.concatenate([r_lo, r_hi], axis=1)
    y = x_ref[...] + mod_ref[0, 5:6, :] * (routed + shared)
    ms = jnp.mean(y * y, axis=-1, keepdims=True)
    o_ref[...] = y * lax.rsqrt(ms + EPS) * fng_ref[...]


def _final(x1, hp2d, y8, wtok, mod3, wsg, wsu, wsd, final_g, seq_len, mod_row_of_batch):
    t = x1.shape[0]
    tm = TM_FINAL

    def mod_idx(i):
        return (mod_row_of_batch((i * tm) // seq_len), 0, 0)

    def full(a):
        return pl.BlockSpec(a.shape, lambda i: (0,) * a.ndim)

    return pl.pallas_call(
        _final_kernel,
        grid=(t // tm,),
        in_specs=[pl.BlockSpec((tm, D_MODEL), lambda i: (i, 0)),
                  pl.BlockSpec((tm * ROW_SLABS, 128), lambda i: (i, 0)),
                  pl.BlockSpec((TOP_K, tm * ROW_SLABS, 128), lambda i: (0, i, 0)),
                  pl.BlockSpec((tm, 128), lambda i: (i, 0)),
                  pl.BlockSpec((1, 6, D_MODEL), mod_idx),
                  full(wsg), full(wsu), full(wsd), full(final_g)],
        out_specs=pl.BlockSpec((tm, D_MODEL), lambda i: (i, 0)),
        out_shape=jax.ShapeDtypeStruct((t, D_MODEL), F32),
        compiler_params=pltpu.CompilerParams(dimension_semantics=("parallel",),
                                             vmem_limit_bytes=VMEM_LIMIT),
        name="final",
    )(x1, hp2d, y8, wtok, mod3, wsg, wsu, wsd, final_g)


def _moe(x1, mod3, lw, seq_len, mod_row_of_batch):
    t = x1.shape[0]
    n_tiles = TOP_K * t // EXPERT_ROWS + N_EXPERTS
    n_tiles_pad = -(-n_tiles // 128) * 128
    hp2d, ek, rk, wtok, cnt = _router(x1, mod3, lw["norm2_g"], lw["w_router_t"], lw["router_bias"],
                                      seq_len, mod_row_of_batch)
    pos, texp, nused, tend = _plan(ek, rk, cnt, n_tiles_pad)
    pos3 = pos.reshape(TOP_K, t // SC_CHUNK, SC_CHUNK).transpose(1, 0, 2)
    xs = _sc_dispatch(hp2d.reshape(t, ROW_SLABS, 128), pos3, n_tiles * EXPERT_ROWS)
    ys2d = _experts(texp.reshape(-1), nused.reshape(-1), tend[:, 0], xs.reshape(-1, 128),
                    lw["weg"], lw["weu"], lw["wed"], n_tiles)
    y8 = _sc_gather(ys2d.reshape(-1, ROW_SLABS, 128), pos3, t)
    return _final(x1, hp2d, y8.reshape(TOP_K, t * ROW_SLABS, 128), wtok, mod3,
                  lw["wsg"], lw["wsu"], lw["wsd"], lw["final_g"], seq_len, mod_row_of_batch)


def _dft_tables(seq_len):
    gd = FOURIER_GROUP_DIM
    kc = np.arange(gd)
    ang_c = ((kc[:, None] * kc[None, :]) % gd) * (2.0 * math.pi / gd)
    cs = np.concatenate([np.cos(ang_c), np.sin(ang_c)], axis=1) * (gd ** -0.5)
    kl = np.arange(seq_len)
    ang_l = ((kl[:, None] * kl[None, :]) % seq_len) * (2.0 * math.pi / seq_len)
    cls = np.concatenate([np.cos(ang_l), -np.sin(ang_l)], axis=1) * (seq_len ** -0.5)
    return jnp.asarray(cs.astype(np.float32), dtype=BF16), jnp.asarray(cls.astype(np.float32), dtype=BF16)


def _rope_tables(length):
    rows = length // GRID_W
    r = np.repeat(np.arange(rows, dtype=np.float32), GRID_W)
    col = np.tile(np.arange(GRID_W, dtype=np.float32), rows)
    nf = RET_HEAD_DIM // 4
    inv = (np.float32(ROPE_BASE) ** (-np.arange(nf, dtype=np.float32) / np.float32(nf))).astype(np.float32)
    ar = r[:, None] * inv[None]
    ac = col[:, None] * inv[None]
    ang = np.concatenate([ar, ar, ac, ac], axis=-1).astype(np.float64)
    sign = np.where((np.arange(RET_HEAD_DIM) & nf) == 0, -1.0, 1.0)
    return (jnp.asarray(np.cos(ang).astype(np.float32)),
            jnp.asarray((np.sin(ang) * sign[None, :]).astype(np.float32)))


def _trunk_path(x, mod3, mod_row_of_batch, s0f, s0b, rope, lw):
    batch, seq_len, _ = x.shape
    x2d = x.reshape(batch * seq_len, D_MODEL)
    uf, q, k, v, sg, gf, gr = _inproj(x2d, mod3, lw["norm1_g"], lw["w_in"], seq_len, mod_row_of_batch, rope)
    r, s_f, s_b = _retention(q, k, v, sg, lw["dec"], lw["gn_g"], s0f, s0b, batch, seq_len)
    cs, cls = _dft_tables(seq_len)
    fmix = _fnet(uf, cs, cls, batch, seq_len)
    x1 = _merge(fmix, r, gf, gr, x2d, mod3, lw["w_four"], lw["w_ret"], lw["w_o"], seq_len, mod_row_of_batch)
    y = _moe(x1, mod3, lw, seq_len, mod_row_of_batch)
    return y.reshape(batch, seq_len, D_MODEL), s_f, s_b


def kernel(x_prompt, x_sample, state_ret_fwd, state_ret_bwd, c, c_ctx, w_ada, b_ada, norm1_g, norm2_g, w_in,
           ret_decay_fwd, ret_decay_bwd, ret_gn_g, w_four_out, w_ret_out, w_out, w_router, router_bias,
           w_exp_gate, w_exp_up, w_exp_down, w_shared_gate, w_shared_up, w_shared_down, final_norm_g):
    depth = w_ada.shape[0]
    assert depth == 1, "final norm is fused into the last layer's MoE kernel"
    n_ctx, n_lat = x_prompt.shape[0], x_sample.shape[0]
    cond = jnp.concatenate([c_ctx[None, :], c], axis=0)
    cond = jnp.pad(cond, ((0, (-cond.shape[0]) % 8), (0, 0)))
    rope = _rope_tables(x_sample.shape[1])
    zeros = jnp.zeros((n_ctx, N_RET_HEADS, RET_HEAD_DIM, RET_HEAD_DIM), F32)

    layer = 0
    mod = _ada(cond, w_ada[layer], b_ada[layer][None, :])
    mod3 = mod.reshape(mod.shape[0], 6, D_MODEL)
    dec = jnp.stack([ret_decay_fwd[layer], ret_decay_bwd[layer]], axis=1)
    lw = {
        "norm1_g": norm1_g[layer][None, :],
        "norm2_g": norm2_g[layer][None, :],
        "w_in": w_in[layer].astype(BF16),
        "dec": jnp.broadcast_to(dec[:, :, None], (N_RET_HEADS, 2, RET_HEAD_DIM)).astype(F32),
        "gn_g": ret_gn_g[layer][None, :],
        "w_four": w_four_out[layer].astype(BF16),
        "w_ret": w_ret_out[layer].astype(BF16),
        "w_o": w_out[layer].astype(BF16),
        "w_router_t": w_router[layer].T,
        "router_bias": router_bias[layer][:, None],
        "weg": w_exp_gate[layer],
        "weu": w_exp_up[layer],
        "wed": w_exp_down[layer],
        "wsg": w_shared_gate[layer].astype(BF16),
        "wsu": w_shared_up[layer].astype(BF16),
        "wsd": w_shared_down[layer].astype(BF16),
        "final_g": final_norm_g[None, :],
    }
    y_prompt, s_f, s_b = _trunk_path(x_prompt, mod3, lambda b: 0, zeros, zeros, None, lw)
    y_sample, _, _ = _trunk_path(x_sample, mod3, lambda b: 1 + b, state_ret_fwd[:, layer],
                                 state_ret_bwd[:, layer], rope, lw)
    return (y_prompt, y_sample, s_f[:, None], s_b[:, None])
```

```python
import functools
import math

import jax
import jax.numpy as jnp
import numpy as np
from jax import lax
from jax.experimental import pallas as pl
from jax.experimental.pallas import tpu as pltpu
from jax.experimental.pallas import tpu_sc as plsc

F32 = jnp.float32
BF16 = jnp.bfloat16

D_MODEL = 1024
GRID_W = 64
N_FOURIER_GROUPS = 8
FOURIER_GROUP_DIM = 128
N_RET_HEADS = 4
RET_HEAD_DIM = 128
RET_WIDTH = N_RET_HEADS * RET_HEAD_DIM
CHUNK = 128
N_EXPERTS = 64
N_EXPERT_GROUPS = 8
EXPERTS_PER_GROUP = N_EXPERTS // N_EXPERT_GROUPS
TOPK_GROUPS = 4
TOP_K = 8
EXPERT_DIM = 256
ROUTED_SCALE = 2.5
ROPE_BASE = 10000.0
EPS = 1e-6
Q_SCALE = RET_HEAD_DIM ** -0.5

_C_UF = (0, 1024)
_C_Q = (1024, 1536)
_C_K = (1536, 2048)
_C_V = (2048, 2560)
_C_G = (2560, 3072)
_C_GF = (3072, 4096)
_C_GR = (4096, 5120)

VMEM_LIMIT = 56 * 1024 * 1024

TM_PROJ = 512
FNET_ROWS = 256
RET_ALL_HEADS_MAX_LEN = 1024
TM_ROUTER = 1024
TM_FINAL = 512
EXPERT_ROWS = 512
TILES_PER_STEP = 2
ROW_SLABS = 4
SC_CORES = 2
SC_WORKERS = 32
SC_CHUNK = 128
SC_LANES = 16
SC_PACK_BLOCK_WORDS = 16384


def _silu(x):
    return x * jax.nn.sigmoid(x)


def _dot(a, b):
    return jnp.dot(a, b, preferred_element_type=F32)


def _rms_mod(x, g, shift, scale):
    ms = jnp.mean(x * x, axis=-1, keepdims=True)
    y = x * lax.rsqrt(ms + EPS) * g
    return y * (1.0 + scale) + shift


def _ada_kernel(cond_ref, w_ref, b_ref, o_ref):
    s = _silu(cond_ref[...]).astype(BF16)
    o_ref[...] = _dot(s, w_ref[...].astype(BF16)) + b_ref[...]


def _ada(cond, w_ada, b_ada):
    rows, n = cond.shape[0], w_ada.shape[1]
    tn = 1536
    return pl.pallas_call(
        _ada_kernel,
        grid=(n // tn,),
        in_specs=[pl.BlockSpec((rows, D_MODEL), lambda j: (0, 0)),
                  pl.BlockSpec((D_MODEL, tn), lambda j: (0, j)),
                  pl.BlockSpec((1, tn), lambda j: (0, j))],
        out_specs=pl.BlockSpec((rows, tn), lambda j: (0, j)),
        out_shape=jax.ShapeDtypeStruct((rows, n), F32),
        compiler_params=pltpu.CompilerParams(vmem_limit_bytes=VMEM_LIMIT),
        name="ada",
    )(cond, w_ada, b_ada)


def _rope_head(x, cos, sin_signed, first_half):
    partner = jnp.where(first_half, pltpu.roll(x, 96, 1), pltpu.roll(x, 32, 1))
    return x * cos + partner * sin_signed


def _inproj_kernel(*refs, use_rope):
    if use_rope:
        x_ref, mod_ref, g_ref, w_ref, cos_ref, sin_ref = refs[:6]
        outs = refs[6:]
    else:
        x_ref, mod_ref, g_ref, w_ref = refs[:4]
        outs = refs[4:]
    uf_o, q_o, k_o, v_o, sg_o, gf_o, gr_o = outs

    h = _rms_mod(x_ref[...], g_ref[...], mod_ref[0, 0:1, :], mod_ref[0, 1:2, :])
    hb = h.astype(BF16)

    def proj(cols):
        return _dot(hb, w_ref[:, cols[0]:cols[1]])

    uf_o[...] = proj(_C_UF).astype(BF16)
    q = proj(_C_Q)
    k = proj(_C_K)
    if use_rope:
        cos = cos_ref[...]
        sin_signed = sin_ref[...]
        lane = lax.broadcasted_iota(jnp.int32, cos.shape, 1)
        first_half = (lane & 32) == 0
        for hd in range(N_RET_HEADS):
            sl = slice(hd * RET_HEAD_DIM, (hd + 1) * RET_HEAD_DIM)
            q_o[:, sl] = (_rope_head(q[:, sl], cos, sin_signed, first_half) * Q_SCALE).astype(BF16)
            k_o[:, sl] = _rope_head(k[:, sl], cos, sin_signed, first_half).astype(BF16)
    else:
        q_o[...] = (q * Q_SCALE).astype(BF16)
        k_o[...] = k.astype(BF16)
    v_o[...] = proj(_C_V).astype(BF16)
    sg_o[...] = _silu(proj(_C_G)).astype(BF16)
    gf_o[...] = jax.nn.sigmoid(proj(_C_GF)).astype(BF16)
    gr_o[...] = jax.nn.sigmoid(proj(_C_GR)).astype(BF16)


def _inproj(x2d, mod3, norm_g, w_in_bf, seq_len, mod_row_of_batch, rope):
    t = x2d.shape[0]
    tm = TM_PROJ
    tiles_per_seq = max(seq_len // tm, 1)

    def mod_idx(i):
        return (mod_row_of_batch((i * tm) // seq_len), 0, 0)

    in_specs = [pl.BlockSpec((tm, D_MODEL), lambda i: (i, 0)),
                pl.BlockSpec((1, 6, D_MODEL), mod_idx),
                pl.BlockSpec((1, D_MODEL), lambda i: (0, 0)),
                pl.BlockSpec(w_in_bf.shape, lambda i: (0, 0))]
    args = [x2d, mod3, norm_g, w_in_bf]
    if rope is not None:
        in_specs += [pl.BlockSpec((tm, RET_HEAD_DIM), lambda i: (i % tiles_per_seq, 0))] * 2
        args += list(rope)
    widths = [1024, RET_WIDTH, RET_WIDTH, RET_WIDTH, RET_WIDTH, 1024, 1024]
    return pl.pallas_call(
        functools.partial(_inproj_kernel, use_rope=rope is not None),
        grid=(t // tm,),
        in_specs=in_specs,
        out_specs=[pl.BlockSpec((tm, w), lambda i: (i, 0)) for w in widths],
        out_shape=[jax.ShapeDtypeStruct((t, w), BF16) for w in widths],
        compiler_params=pltpu.CompilerParams(dimension_semantics=("parallel",),
                                             vmem_limit_bytes=VMEM_LIMIT),
        name="inproj",
    )(*args)


def _retention_kernel(q_ref, k_ref, v_ref, sg_ref, dec_ref, gn_ref, s0f_ref, s0b_ref,
                      r_ref, sfo_ref, sbo_ref, tab_scr, gc_scr):
    n_chunks = q_ref.shape[0] // CHUNK
    hd = RET_HEAD_DIM
    heads_here = q_ref.shape[1] // hd
    first_head = pl.program_id(1) * heads_here

    @pl.when((pl.program_id(0) == 0) & (pl.program_id(1) == 0))
    def _():
        row = lax.broadcasted_iota(jnp.int32, (CHUNK, CHUNK), 0).astype(F32)
        col = lax.broadcasted_iota(jnp.int32, (CHUNK, CHUNK), 1).astype(F32)
        diff = row - col
        for h in range(N_RET_HEADS):
            dec = dec_ref[h]
            lg = jnp.minimum(dec, 0.0) - jnp.log1p(jnp.exp(-jnp.abs(dec)))
            lgf = lg[0:1, :]
            lgb = lg[1:2, :]
            tab_scr[h, 0] = jnp.exp(jnp.where(diff >= 0, lgf * diff, lgb * (-diff)))
            tab_scr[h, 1] = jnp.exp(lgf * (row + 1.0))
            tab_scr[h, 2] = jnp.exp(lgb * (CHUNK - row))
            tab_scr[h, 3] = jnp.exp(lgf * (CHUNK - 1.0 - col))
            tab_scr[h, 4] = jnp.exp(lgb * col)
            gc_scr[h] = jnp.exp(lg * CHUNK)

    def rows(n):
        return slice(n * CHUNK, (n + 1) * CHUNK)

    for h in range(heads_here):
        cols = slice(h * hd, (h + 1) * hd)
        head = first_head + h
        decay, qw_f, qw_b, kwt_f, kwt_b = (tab_scr[head, i] for i in range(5))
        gc = gc_scr[head]
        gc_f = gc[0:1, :]
        gc_b = gc[1:2, :]

        kv_f, kv_b = [], []
        for n in range(n_chunks):
            kt = k_ref[rows(n), cols].astype(F32).T
            vn = v_ref[rows(n), cols]
            kv_f.append(_dot((kt * kwt_f).astype(BF16), vn))
            kv_b.append(_dot((kt * kwt_b).astype(BF16), vn))

        s = s0f_ref[h]
        prev_f = []
        for n in range(n_chunks):
            prev_f.append(s.astype(BF16))
            s = gc_f * s + kv_f[n]
        sfo_ref[h] = s
        s = s0b_ref[h]
        prev_b = [None] * n_chunks
        for n in reversed(range(n_chunks)):
            prev_b[n] = s.astype(BF16)
            s = gc_b * s + kv_b[n]
        sbo_ref[h] = s

        gn = gn_ref[:, cols]
        for n in range(n_chunks):
            qn = q_ref[rows(n), cols]
            qf = qn.astype(F32)
            scores = lax.dot_general(qn, k_ref[rows(n), cols], (((1,), (1,)), ((), ())),
                                     preferred_element_type=F32)
            o = _dot((scores * decay).astype(BF16), v_ref[rows(n), cols])
            o = o + _dot((qf * qw_f).astype(BF16), prev_f[n])
            o = o + _dot((qf * qw_b).astype(BF16), prev_b[n])
            mu = jnp.mean(o, axis=-1, keepdims=True)
            d = o - mu
            var = jnp.mean(d * d, axis=-1, keepdims=True)
            on = d * lax.rsqrt(var + EPS) * gn
            r_ref[rows(n), cols] = (on * sg_ref[rows(n), cols].astype(F32)).astype(BF16)


def _retention(q, k, v, sg, dec, gn_g, s0f, s0b, batch, seq_len):
    hd = RET_HEAD_DIM
    heads_per_step = N_RET_HEADS if seq_len <= RET_ALL_HEADS_MAX_LEN else 1
    width = heads_per_step * hd
    tok_spec = pl.BlockSpec((seq_len, width), lambda b, g: (b, g))
    st_spec = pl.BlockSpec((None, heads_per_step, hd, hd), lambda b, g: (b, g, 0, 0))
    st_shape = jax.ShapeDtypeStruct((batch, N_RET_HEADS, hd, hd), F32)
    return pl.pallas_call(
        _retention_kernel,
        grid=(batch, N_RET_HEADS // heads_per_step),
        in_specs=[tok_spec, tok_spec, tok_spec, tok_spec,
                  pl.BlockSpec(dec.shape, lambda b, g: (0, 0, 0)),
                  pl.BlockSpec((1, width), lambda b, g: (0, g)),
                  st_spec, st_spec],
        out_specs=[tok_spec, st_spec, st_spec],
        out_shape=[jax.ShapeDtypeStruct((batch * seq_len, RET_WIDTH), BF16), st_shape, st_shape],
        scratch_shapes=[pltpu.VMEM((N_RET_HEADS, 5, CHUNK, CHUNK), F32),
                        pltpu.VMEM((N_RET_HEADS, 2, hd), F32)],
        compiler_params=pltpu.CompilerParams(dimension_semantics=("arbitrary", "arbitrary"),
                                             vmem_limit_bytes=VMEM_LIMIT),
        name="retention",
    )(q, k, v, sg, dec, gn_g, s0f, s0b)


def _fnet_kernel(uf_ref, cs_ref, cls_ref, o_ref, xcs_ref):
    seq_len = uf_ref.shape[0]
    gd = FOURIER_GROUP_DIM

    @pl.when(pl.program_id(1) == 0)
    def _():
        for g in range(N_FOURIER_GROUPS):
            x = _dot(uf_ref[:, g * gd:(g + 1) * gd], cs_ref[...])
            xcs_ref[0:seq_len, g * gd:(g + 1) * gd] = x[:, :gd].astype(BF16)
            xcs_ref[seq_len:2 * seq_len, g * gd:(g + 1) * gd] = x[:, gd:].astype(BF16)

    o_ref[...] = _dot(cls_ref[...], xcs_ref[...]).astype(BF16)


def _fnet(uf, cs, cls, batch, seq_len):
    rb = FNET_ROWS
    nr = seq_len // rb
    return pl.pallas_call(
        _fnet_kernel,
        grid=(batch, nr),
        in_specs=[pl.BlockSpec((seq_len, D_MODEL), lambda b, r: (b, 0)),
                  pl.BlockSpec(cs.shape, lambda b, r: (0, 0)),
                  pl.BlockSpec((rb, 2 * seq_len), lambda b, r: (r, 0))],
        out_specs=pl.BlockSpec((rb, D_MODEL), lambda b, r: (b * nr + r, 0)),
        out_shape=jax.ShapeDtypeStruct((batch * seq_len, D_MODEL), BF16),
        scratch_shapes=[pltpu.VMEM((2 * seq_len, D_MODEL), BF16)],
        compiler_params=pltpu.CompilerParams(dimension_semantics=("parallel", "arbitrary"),
                                             vmem_limit_bytes=VMEM_LIMIT),
        name="fnet",
    )(uf, cs, cls)


def _merge_kernel(fm_ref, r_ref, gf_ref, gr_ref, x_ref, mod_ref, wf_ref, wr_ref, wo_ref, o_ref):
    f_out = _dot(fm_ref[...], wf_ref[...])
    r_out = _dot(r_ref[...], wr_ref[...])
    merged = gf_ref[...].astype(F32) * f_out + gr_ref[...].astype(F32) * r_out
    mix = _dot(merged.astype(BF16), wo_ref[...])
    o_ref[...] = x_ref[...] + mod_ref[0, 2:3, :] * mix


def _merge(fmix, r, gf, gr, x2d, mod3, w_four, w_ret, w_o, seq_len, mod_row_of_batch):
    t = x2d.shape[0]
    tm = TM_PROJ

    def mod_idx(i):
        return (mod_row_of_batch((i * tm) // seq_len), 0, 0)

    def tok(w):
        return pl.BlockSpec((tm, w), lambda i: (i, 0))

    def full(a):
        return pl.BlockSpec(a.shape, lambda i: (0, 0))

    return pl.pallas_call(
        _merge_kernel,
        grid=(t // tm,),
        in_specs=[tok(D_MODEL), tok(RET_WIDTH), tok(D_MODEL), tok(D_MODEL), tok(D_MODEL),
                  pl.BlockSpec((1, 6, D_MODEL), mod_idx), full(w_four), full(w_ret), full(w_o)],
        out_specs=tok(D_MODEL),
        out_shape=jax.ShapeDtypeStruct((t, D_MODEL), F32),
        compiler_params=pltpu.CompilerParams(dimension_semantics=("parallel",),
                                             vmem_limit_bytes=VMEM_LIMIT),
        name="merge",
    )(fmix, r, gf, gr, x2d, mod3, w_four, w_ret, w_o)


def _pack_pair(lo_f32, hi_f32):
    lo = lax.bitcast_convert_type(lo_f32.astype(BF16).astype(F32), jnp.uint32)
    hi = lax.bitcast_convert_type(hi_f32.astype(BF16).astype(F32), jnp.uint32)
    return lax.bitcast_convert_type((lo >> 16) | hi, jnp.int32)


def _unpack_pair(words_i32):
    w = lax.bitcast_convert_type(words_i32, jnp.uint32)
    lo = lax.bitcast_convert_type(w << 16, F32)
    hi = lax.bitcast_convert_type(w & jnp.uint32(0xFFFF0000), F32)
    return lo, hi


def _load_token_words(ref, lead, n_tok):
    parts = []
    for s in range(ROW_SLABS):
        idx = (pl.ds(s, n_tok, stride=ROW_SLABS), slice(None))
        parts.append(ref[lead + idx] if lead else ref[idx])
    return jnp.concatenate(parts, axis=1)


def _store_token_words(ref, words, n_tok):
    for s in range(ROW_SLABS):
        ref[pl.ds(s, n_tok, stride=ROW_SLABS), :] = words[:, s * 128:(s + 1) * 128]


def _route(scores, biased):
    tokens = scores.shape[1]
    neg = -jnp.inf
    epg = EXPERTS_PER_GROUP
    iota_g = lax.broadcasted_iota(jnp.int32, (epg, tokens), 0).astype(F32)

    def pick_first_max(cur, iota, size):
        m = jnp.max(cur, axis=0, keepdims=True)
        idx = jnp.min(jnp.where(cur == m, iota, float(size)), axis=0, keepdims=True)
        return m, idx, iota == idx

    group_scores = []
    for g in range(N_EXPERT_GROUPS):
        vals = biased[g * epg:(g + 1) * epg, :]
        m1, _, hit = pick_first_max(vals, iota_g, epg)
        m2 = jnp.max(jnp.where(hit, neg, vals), axis=0, keepdims=True)
        group_scores.append(m1 + m2)
    cur = jnp.concatenate(group_scores, axis=0)
    group_sel = jnp.zeros_like(cur)
    for _ in range(TOPK_GROUPS):
        _, _, hit = pick_first_max(cur, iota_g, N_EXPERT_GROUPS)
        group_sel = jnp.where(hit, 1.0, group_sel)
        cur = jnp.where(hit, neg, cur)
    masked = jnp.concatenate(
        [jnp.where(group_sel[g:g + 1, :] > 0.0, biased[g * epg:(g + 1) * epg, :], neg)
         for g in range(N_EXPERT_GROUPS)], axis=0)
    iota_e = lax.broadcasted_iota(jnp.int32, masked.shape, 0).astype(F32)
    sel = jnp.zeros_like(masked)
    cur = masked
    picks = []
    for _ in range(TOP_K):
        _, idx, hit = pick_first_max(cur, iota_e, N_EXPERTS)
        picks.append(idx)
        sel = jnp.where(hit, 1.0, sel)
        cur = jnp.where(hit, neg, cur)
    w = scores * sel
    return w / jnp.sum(w, axis=0, keepdims=True) * ROUTED_SCALE, sel, picks


def _router_kernel(x_ref, mod_ref, g2_ref, wrt_ref, rb_ref, hp_ref, ek_ref, rk_ref, wt_ref, cnt_ref, run_scr):
    tm = x_ref.shape[0]

    @pl.when(pl.program_id(0) == 0)
    def _():
        run_scr[...] = jnp.zeros_like(run_scr)

    h = _rms_mod(x_ref[...], g2_ref[...], mod_ref[0, 3:4, :], mod_ref[0, 4:5, :])
    half = D_MODEL // 2
    _store_token_words(hp_ref, _pack_pair(h[:, :half], h[:, half:]), tm)

    logits_t = lax.dot_general(wrt_ref[...], h, (((1,), (1,)), ((), ())),
                               precision=lax.Precision.HIGHEST, preferred_element_type=F32)
    scores = jax.nn.sigmoid(logits_t)
    comb_t, sel, picks = _route(scores, scores + rb_ref[...])

    earlier = (lax.broadcasted_iota(jnp.int32, (tm, tm), 0) < lax.broadcasted_iota(jnp.int32, (tm, tm), 1))
    rank_t = _dot(sel.astype(BF16), jnp.where(earlier, 1.0, 0.0).astype(BF16)) + run_scr[...]
    run_scr[...] += jnp.sum(sel, axis=1, keepdims=True)
    cnt_ref[...] = jnp.broadcast_to(run_scr[...], cnt_ref.shape)

    iota_e = lax.broadcasted_iota(jnp.int32, sel.shape, 0).astype(F32)
    ranks, weights = [], []
    for idx in picks:
        hit = iota_e == idx
        ranks.append(jnp.sum(jnp.where(hit, rank_t, 0.0), axis=0, keepdims=True))
        weights.append(jnp.sum(jnp.where(hit, comb_t, 0.0), axis=0, keepdims=True))
    ek_ref[...] = jnp.concatenate(picks, axis=0).astype(jnp.int32)
    rk_ref[...] = jnp.concatenate(ranks, axis=0).astype(jnp.int32)
    w_pad = jnp.concatenate(weights + [jnp.zeros((128 - TOP_K, tm), F32)], axis=0)
    wt_ref[...] = w_pad.T


def _router(x1, mod3, norm2_g, w_router_t, router_bias, seq_len, mod_row_of_batch):
    t = x1.shape[0]
    tm = TM_ROUTER

    def mod_idx(i):
        return (mod_row_of_batch((i * tm) // seq_len), 0, 0)

    def full(a):
        return pl.BlockSpec(a.shape, lambda i: (0,) * a.ndim)

    return pl.pallas_call(
        _router_kernel,
        grid=(t // tm,),
        in_specs=[pl.BlockSpec((tm, D_MODEL), lambda i: (i, 0)),
                  pl.BlockSpec((1, 6, D_MODEL), mod_idx),
                  full(norm2_g), full(w_router_t), full(router_bias)],
        out_specs=[pl.BlockSpec((tm * ROW_SLABS, 128), lambda i: (i, 0)),
                   pl.BlockSpec((TOP_K, tm), lambda i: (0, i)),
                   pl.BlockSpec((TOP_K, tm), lambda i: (0, i)),
                   pl.BlockSpec((tm, 128), lambda i: (i, 0)),
                   pl.BlockSpec((N_EXPERTS, 128), lambda i: (0, 0))],
        out_shape=[jax.ShapeDtypeStruct((t * ROW_SLABS, 128), jnp.int32),
                   jax.ShapeDtypeStruct((TOP_K, t), jnp.int32),
                   jax.ShapeDtypeStruct((TOP_K, t), jnp.int32),
                   jax.ShapeDtypeStruct((t, 128), F32),
                   jax.ShapeDtypeStruct((N_EXPERTS, 128), F32)],
        scratch_shapes=[pltpu.VMEM((N_EXPERTS, 1), F32)],
        compiler_params=pltpu.CompilerParams(dimension_semantics=("arbitrary",),
                                             vmem_limit_bytes=VMEM_LIMIT),
        name="router",
    )(x1, mod3, norm2_g, w_router_t, router_bias)


def _plan_kernel(ek_ref, rk_ref, cnt_ref, pos_ref, texp_ref, nused_ref, tend_ref):
    rows = float(EXPERT_ROWS)
    cnt = cnt_ref[:, 0:1]
    tiles = jnp.floor((cnt + (rows - 1.0)) / rows)
    before = (lax.broadcasted_iota(jnp.int32, (N_EXPERTS, N_EXPERTS), 1)
              < lax.broadcasted_iota(jnp.int32, (N_EXPERTS, N_EXPERTS), 0))
    tile_start = jnp.dot(jnp.where(before, 1.0, 0.0), jnp.broadcast_to(tiles, (N_EXPERTS, 128)),
                         precision=lax.Precision.HIGHEST, preferred_element_type=F32)[:, 0:1]
    tile_end = tile_start + tiles
    row_start = tile_start * rows

    ek = ek_ref[...]
    pos = rk_ref[...].astype(F32)
    tile_id = lax.broadcasted_iota(jnp.int32, texp_ref.shape, 1).astype(F32)
    texp = jnp.zeros(texp_ref.shape, F32)
    for e in range(N_EXPERTS):
        pos = pos + jnp.where(ek == e, row_start[e:e + 1, :], 0.0)
        texp = texp + jnp.where(tile_id >= tile_end[e:e + 1, :], 1.0, 0.0)
    pos_ref[...] = pos.astype(jnp.int32)
    texp_ref[...] = jnp.minimum(texp, N_EXPERTS - 1.0).astype(jnp.int32)
    nused_ref[...] = jnp.broadcast_to(tile_end[N_EXPERTS - 1:N_EXPERTS, :], nused_ref.shape).astype(jnp.int32)
    tend_ref[...] = jnp.broadcast_to(tile_end, tend_ref.shape).astype(jnp.int32)


def _plan(ek, rk, cnt, n_tiles_pad):
    t = ek.shape[1]

    def full(shape):
        return pl.BlockSpec(shape, lambda: (0,) * len(shape))

    return pl.pallas_call(
        _plan_kernel,
        in_specs=[full(ek.shape), full(rk.shape), full(cnt.shape)],
        out_specs=[full((TOP_K, t)), full((1, n_tiles_pad)), full((1, 128)), full((N_EXPERTS, 128))],
        out_shape=[jax.ShapeDtypeStruct((TOP_K, t), jnp.int32),
                   jax.ShapeDtypeStruct((1, n_tiles_pad), jnp.int32),
                   jax.ShapeDtypeStruct((1, 128), jnp.int32),
                   jax.ShapeDtypeStruct((N_EXPERTS, 128), jnp.int32)],
        compiler_params=pltpu.CompilerParams(vmem_limit_bytes=VMEM_LIMIT),
        name="plan",
    )(ek, rk, cnt)


def _sc_mesh():
    return plsc.VectorSubcoreMesh(core_axis_name="c", subcore_axis_name="s")


def _sc_pack_weight_halves(w):
    e, k, n = w.shape
    k_half = k // 2
    rb = SC_PACK_BLOCK_WORDS // n
    units_per_expert = k_half // rb
    per_w = (e * units_per_expert) // SC_WORKERS
    lanes = SC_LANES

    @functools.partial(
        pl.kernel, out_type=jax.ShapeDtypeStruct((e * k_half, n), jnp.int32), mesh=_sc_mesh(),
        scratch_types=[pltpu.VMEM((rb, n), F32), pltpu.VMEM((rb, n), F32), pltpu.VMEM((rb, n), jnp.int32)],
        compiler_params=pltpu.CompilerParams(needs_layout_passes=False))
    def kern(w_hbm, out_hbm, a_v, b_v, o_v):
        wid = lax.axis_index("s") * SC_CORES + lax.axis_index("c")

        @pl.loop(0, per_w)
        def _(j):
            unit = wid * per_w + j
            expert = unit // units_per_expert
            blk = unit % units_per_expert
            row_a = expert * k + blk * rb
            pltpu.sync_copy(w_hbm.at[pl.ds(row_a, rb)], a_v)
            pltpu.sync_copy(w_hbm.at[pl.ds(row_a + k_half, rb)], b_v)

            @pl.loop(0, rb)
            def _(r):
                @plsc.parallel_loop(0, n, step=lanes, unroll=4)
                def _(c):
                    both = plsc.pack(a_v[r, pl.ds(c, lanes)], b_v[r, pl.ds(c, lanes)],
                                     format=plsc.PackFormat.INTERLEAVED)
                    o_v[r, pl.ds(c, lanes)] = plsc.bitcast(both, jnp.int32)

            pltpu.sync_copy(o_v, out_hbm.at[pl.ds(expert * k_half + blk * rb, rb)])

    return kern(w.reshape(e * k, n)).reshape(e, k_half, n)


def _sc_dispatch(rows, pos3, n_out):
    t = rows.shape[0]
    ch = SC_CHUNK
    per_w = (t // ch) // SC_WORKERS

    @functools.partial(
        pl.kernel, out_type=jax.ShapeDtypeStruct((n_out,) + rows.shape[1:], jnp.int32), mesh=_sc_mesh(),
        scratch_types=[pltpu.VMEM((TOP_K, ch), jnp.int32), pltpu.VMEM((ch,) + rows.shape[1:], jnp.int32),
                       pltpu.SemaphoreType.DMA])
    def k(rows_hbm, pos_hbm, out_hbm, idx_v, rows_v, sem):
        wid = lax.axis_index("s") * SC_CORES + lax.axis_index("c")

        @pl.loop(0, per_w)
        def _(j):
            c = wid * per_w + j
            pltpu.sync_copy(pos_hbm.at[c], idx_v)
            pltpu.sync_copy(rows_hbm.at[pl.ds(c * ch, ch)], rows_v)
            copies = [pltpu.async_copy(rows_v, out_hbm.at[idx_v.at[kk]], sem) for kk in range(TOP_K)]
            for cp in copies:
                cp.wait()

    return k(rows, pos3)


def _sc_gather(table, pos3, t):
    ch = SC_CHUNK
    per_w = (t // ch) // SC_WORKERS

    @functools.partial(
        pl.kernel, out_type=jax.ShapeDtypeStruct((TOP_K, t) + table.shape[1:], jnp.int32), mesh=_sc_mesh(),
        scratch_types=[pltpu.VMEM((TOP_K, ch), jnp.int32), pltpu.VMEM((ch,) + table.shape[1:], jnp.int32),
                       pltpu.SemaphoreType.DMA])
    def k(tab_hbm, pos_hbm, out_hbm, idx_v, rows_v, sem):
        wid = lax.axis_index("s") * SC_CORES + lax.axis_index("c")

        @pl.loop(0, per_w)
        def _(j):
            c = wid * per_w + j
            pltpu.sync_copy(pos_hbm.at[c], idx_v)
            for kk in range(TOP_K):
                pltpu.async_copy(tab_hbm.at[idx_v.at[kk]], rows_v, sem).wait()
                pltpu.sync_copy(rows_v, out_hbm.at[kk, pl.ds(c * ch, ch)])

    return k(table, pos3)


def _experts_kernel(texp_ref, nused_ref, tend_ref, xs_ref, weg_hbm, weu_hbm, wed_hbm, ys_ref,
                    wg_scr, wu_scr, wd_scr, wg_buf, wu_buf, wd_buf, sem, group_scr):
    step = pl.program_id(0)
    rows = EXPERT_ROWS
    half = D_MODEL // 2
    n_used = nused_ref[0]

    def weight_copies(e, slot):
        return [pltpu.make_async_copy(weg_hbm.at[e], wg_buf.at[slot], sem.at[slot, 0]),
                pltpu.make_async_copy(weu_hbm.at[e], wu_buf.at[slot], sem.at[slot, 1]),
                pltpu.make_async_copy(wed_hbm.at[e], wd_buf.at[slot], sem.at[slot, 2])]

    @pl.when(step == 0)
    def _():
        group_scr[0] = 0
        for cp in weight_copies(texp_ref[0], 0):
            cp.start()

    def row_tile(tile, x_view, y_view):
        expert = texp_ref[tile]
        used = tile < n_used
        new_expert = (tile == 0) | (expert != texp_ref[jnp.maximum(tile - 1, 0)])

        @pl.when(used & new_expert)
        def _():
            group = group_scr[0]
            slot = group % 2
            next_tile = tend_ref[expert]

            @pl.when(next_tile < n_used)
            def _():
                for cp in weight_copies(texp_ref[next_tile], 1 - slot):
                    cp.start()

            for cp in weight_copies(expert, slot):
                cp.wait()
            for scr, buf in ((wg_scr, wg_buf), (wu_scr, wu_buf), (wd_scr, wd_buf)):
                top, bottom = _unpack_pair(buf[slot])
                k_half = top.shape[0]
                scr[0:k_half, :] = top.astype(BF16)
                scr[k_half:2 * k_half, :] = bottom.astype(BF16)
            group_scr[0] = group + 1

        @pl.when(used)
        def _():
            lo, hi = _unpack_pair(_load_token_words(x_view, (), rows))
            lo = lo.astype(BF16)
            hi = hi.astype(BF16)
            g = _dot(lo, wg_scr[0:half, :]) + _dot(hi, wg_scr[half:D_MODEL, :])
            u = _dot(lo, wu_scr[0:half, :]) + _dot(hi, wu_scr[half:D_MODEL, :])
            y = _dot((_silu(g) * u).astype(BF16), wd_scr[...])
            _store_token_words(y_view, _pack_pair(y[:, :half], y[:, half:]), rows)

        @pl.when(jnp.logical_not(used) & (step == (n_used - 1) // TILES_PER_STEP))
        def _():
            y_view[...] = jnp.zeros_like(y_view)

    for s in range(TILES_PER_STEP):
        view = pl.ds(s * rows * ROW_SLABS, rows * ROW_SLABS)
        row_tile(step * TILES_PER_STEP + s, xs_ref.at[view], ys_ref.at[view])


def _experts(texp, nused, tend, xs2d, weg, weu, wed, n_tiles):
    block = (TILES_PER_STEP * EXPERT_ROWS * ROW_SLABS, 128)
    hbm = pl.BlockSpec(memory_space=pl.ANY)

    def block_idx(j, te, nu, tn):
        return (jnp.minimum(j, (nu[0] - 1) // TILES_PER_STEP), 0)

    grid_spec = pltpu.PrefetchScalarGridSpec(
        num_scalar_prefetch=3,
        grid=(n_tiles // TILES_PER_STEP,),
        in_specs=[pl.BlockSpec(block, block_idx), hbm, hbm, hbm],
        out_specs=pl.BlockSpec(block, block_idx),
        scratch_shapes=[pltpu.VMEM((D_MODEL, EXPERT_DIM), BF16),
                        pltpu.VMEM((D_MODEL, EXPERT_DIM), BF16),
                        pltpu.VMEM((EXPERT_DIM, D_MODEL), BF16),
                        pltpu.VMEM((2,) + weg.shape[1:], jnp.int32),
                        pltpu.VMEM((2,) + weu.shape[1:], jnp.int32),
                        pltpu.VMEM((2,) + wed.shape[1:], jnp.int32),
                        pltpu.SemaphoreType.DMA((2, 3)),
                        pltpu.SMEM((1,), jnp.int32)],
    )
    return pl.pallas_call(
        _experts_kernel,
        grid_spec=grid_spec,
        out_shape=jax.ShapeDtypeStruct(xs2d.shape, jnp.int32),
        compiler_params=pltpu.CompilerParams(dimension_semantics=("arbitrary",),
                                             vmem_limit_bytes=VMEM_LIMIT),
        name="experts",
    )(texp, nused, tend, xs2d, weg, weu, wed)


def _final_kernel(x_ref, hp_ref, y8_ref, wt_ref, mod_ref, wsg_ref, wsu_ref, wsd_ref, fng_ref, o_ref):
    tm = x_ref.shape[0]
    lo, hi = _unpack_pair(_load_token_words(hp_ref, (), tm))
    hb = jnp.concatenate([lo, hi], axis=1).astype(BF16)
    shared = _dot((_silu(_dot(hb, wsg_ref[...])) * _dot(hb, wsu_ref[...])).astype(BF16), wsd_ref[...])
    wt = wt_ref[...]
    r_lo = jnp.zeros((tm, D_MODEL // 2), F32)
    r_hi = jnp.zeros((tm, D_MODEL // 2), F32)
    for k in range(TOP_K):
        lo, hi = _unpack_pair(_load_token_words(y8_ref, (k,), tm))
        wk = wt[:, k:k + 1]
        r_lo = r_lo + wk * lo
        r_hi = r_hi + wk * hi
    routed = jnp.concatenate([r_lo, r_hi], axis=1)
    y = x_ref[...] + mod_ref[0, 5:6, :] * (routed + shared)
    ms = jnp.mean(y * y, axis=-1, keepdims=True)
    o_ref[...] = y * lax.rsqrt(ms + EPS) * fng_ref[...]


def _final(x1, hp2d, y8, wtok, mod3, wsg, wsu, wsd, final_g, seq_len, mod_row_of_batch):
    t = x1.shape[0]
    tm = TM_FINAL

    def mod_idx(i):
        return (mod_row_of_batch((i * tm) // seq_len), 0, 0)

    def full(a):
        return pl.BlockSpec(a.shape, lambda i: (0,) * a.ndim)

    return pl.pallas_call(
        _final_kernel,
        grid=(t // tm,),
        in_specs=[pl.BlockSpec((tm, D_MODEL), lambda i: (i, 0)),
                  pl.BlockSpec((tm * ROW_SLABS, 128), lambda i: (i, 0)),
                  pl.BlockSpec((TOP_K, tm * ROW_SLABS, 128), lambda i: (0, i, 0)),
                  pl.BlockSpec((tm, 128), lambda i: (i, 0)),
                  pl.BlockSpec((1, 6, D_MODEL), mod_idx),
                  full(wsg), full(wsu), full(wsd), full(final_g)],
        out_specs=pl.BlockSpec((tm, D_MODEL), lambda i: (i, 0)),
        out_shape=jax.ShapeDtypeStruct((t, D_MODEL), F32),
        compiler_params=pltpu.CompilerParams(dimension_semantics=("parallel",),
                                             vmem_limit_bytes=VMEM_LIMIT),
        name="final",
    )(x1, hp2d, y8, wtok, mod3, wsg, wsu, wsd, final_g)


def _moe(x1, mod3, lw, seq_len, mod_row_of_batch):
    t = x1.shape[0]
    n_tiles = TOP_K * t // EXPERT_ROWS + N_EXPERTS
    n_tiles_pad = -(-n_tiles // 128) * 128
    hp2d, ek, rk, wtok, cnt = _router(x1, mod3, lw["norm2_g"], lw["w_router_t"], lw["router_bias"],
                                      seq_len, mod_row_of_batch)
    pos, texp, nused, tend = _plan(ek, rk, cnt, n_tiles_pad)
    pos3 = pos.reshape(TOP_K, t // SC_CHUNK, SC_CHUNK).transpose(1, 0, 2)
    xs = _sc_dispatch(hp2d.reshape(t, ROW_SLABS, 128), pos3, n_tiles * EXPERT_ROWS)
    ys2d = _experts(texp.reshape(-1), nused.reshape(-1), tend[:, 0], xs.reshape(-1, 128),
                    lw["weg"], lw["weu"], lw["wed"], n_tiles)
    y8 = _sc_gather(ys2d.reshape(-1, ROW_SLABS, 128), pos3, t)
    return _final(x1, hp2d, y8.reshape(TOP_K, t * ROW_SLABS, 128), wtok, mod3,
                  lw["wsg"], lw["wsu"], lw["wsd"], lw["final_g"], seq_len, mod_row_of_batch)


def _dft_tables(seq_len):
    gd = FOURIER_GROUP_DIM
    kc = np.arange(gd)
    ang_c = ((kc[:, None] * kc[None, :]) % gd) * (2.0 * math.pi / gd)
    cs = np.concatenate([np.cos(ang_c), np.sin(ang_c)], axis=1) * (gd ** -0.5)
    kl = np.arange(seq_len)
    ang_l = ((kl[:, None] * kl[None, :]) % seq_len) * (2.0 * math.pi / seq_len)
    cls = np.concatenate([np.cos(ang_l), -np.sin(ang_l)], axis=1) * (seq_len ** -0.5)
    return jnp.asarray(cs.astype(np.float32), dtype=BF16), jnp.asarray(cls.astype(np.float32), dtype=BF16)


def _rope_tables(length):
    rows = length // GRID_W
    r = np.repeat(np.arange(rows, dtype=np.float32), GRID_W)
    col = np.tile(np.arange(GRID_W, dtype=np.float32), rows)
    nf = RET_HEAD_DIM // 4
    inv = (np.float32(ROPE_BASE) ** (-np.arange(nf, dtype=np.float32) / np.float32(nf))).astype(np.float32)
    ar = r[:, None] * inv[None]
    ac = col[:, None] * inv[None]
    ang = np.concatenate([ar, ar, ac, ac], axis=-1).astype(np.float64)
    sign = np.where((np.arange(RET_HEAD_DIM) & nf) == 0, -1.0, 1.0)
    return (jnp.asarray(np.cos(ang).astype(np.float32)),
            jnp.asarray((np.sin(ang) * sign[None, :]).astype(np.float32)))


def _trunk_path(x, mod3, mod_row_of_batch, s0f, s0b, rope, lw):
    batch, seq_len, _ = x.shape
    x2d = x.reshape(batch * seq_len, D_MODEL)
    uf, q, k, v, sg, gf, gr = _inproj(x2d, mod3, lw["norm1_g"], lw["w_in"], seq_len, mod_row_of_batch, rope)
    r, s_f, s_b = _retention(q, k, v, sg, lw["dec"], lw["gn_g"], s0f, s0b, batch, seq_len)
    cs, cls = _dft_tables(seq_len)
    fmix = _fnet(uf, cs, cls, batch, seq_len)
    x1 = _merge(fmix, r, gf, gr, x2d, mod3, lw["w_four"], lw["w_ret"], lw["w_o"], seq_len, mod_row_of_batch)
    y = _moe(x1, mod3, lw, seq_len, mod_row_of_batch)
    return y.reshape(batch, seq_len, D_MODEL), s_f, s_b


def kernel(x_prompt, x_sample, state_ret_fwd, state_ret_bwd, c, c_ctx, w_ada, b_ada, norm1_g, norm2_g, w_in,
           ret_decay_fwd, ret_decay_bwd, ret_gn_g, w_four_out, w_ret_out, w_out, w_router, router_bias,
           w_exp_gate, w_exp_up, w_exp_down, w_shared_gate, w_shared_up, w_shared_down, final_norm_g):
    depth = w_ada.shape[0]
    assert depth == 1, "final norm is fused into the last layer's MoE kernel"
    n_ctx, n_lat = x_prompt.shape[0], x_sample.shape[0]
    cond = jnp.concatenate([c_ctx[None, :], c], axis=0)
    cond = jnp.pad(cond, ((0, (-cond.shape[0]) % 8), (0, 0)))
    rope = _rope_tables(x_sample.shape[1])
    zeros = jnp.zeros((n_ctx, N_RET_HEADS, RET_HEAD_DIM, RET_HEAD_DIM), F32)

    layer = 0
    mod = _ada(cond, w_ada[layer], b_ada[layer][None, :])
    mod3 = mod.reshape(mod.shape[0], 6, D_MODEL)
    dec = jnp.stack([ret_decay_fwd[layer], ret_decay_bwd[layer]], axis=1)
    lw = {
        "norm1_g": norm1_g[layer][None, :],
        "norm2_g": norm2_g[layer][None, :],
        "w_in": w_in[layer].astype(BF16),
        "dec": jnp.broadcast_to(dec[:, :, None], (N_RET_HEADS, 2, RET_HEAD_DIM)).astype(F32),
        "gn_g": ret_gn_g[layer][None, :],
        "w_four": w_four_out[layer].astype(BF16),
        "w_ret": w_ret_out[layer].astype(BF16),
        "w_o": w_out[layer].astype(BF16),
        "w_router_t": w_router[layer].T,
        "router_bias": router_bias[layer][:, None],
        "weg": _sc_pack_weight_halves(w_exp_gate[layer]),
        "weu": _sc_pack_weight_halves(w_exp_up[layer]),
        "wed": _sc_pack_weight_halves(w_exp_down[layer]),
        "wsg": w_shared_gate[layer].astype(BF16),
        "wsu": w_shared_up[layer].astype(BF16),
        "wsd": w_shared_down[layer].astype(BF16),
        "final_g": final_norm_g[None, :],
    }
    y_prompt, s_f, s_b = _trunk_path(x_prompt, mod3, lambda b: 0, zeros, zeros, None, lw)
    y_sample, _, _ = _trunk_path(x_sample, mod3, lambda b: 1 + b, state_ret_fwd[:, layer],
                                 state_ret_bwd[:, layer], rope, lw)
    return (y_prompt, y_sample, s_f[:, None], s_b[:, None])
```

```python
import functools
import math

import jax
import jax.numpy as jnp
import numpy as np
from jax import lax
from jax.experimental import pallas as pl
from jax.experimental.pallas import tpu as pltpu
from jax.experimental.pallas import tpu_sc as plsc

F32 = jnp.float32
BF16 = jnp.bfloat16

D_MODEL = 1024
GRID_W = 64
N_FOURIER_GROUPS = 8
FOURIER_GROUP_DIM = 128
N_RET_HEADS = 4
RET_HEAD_DIM = 128
RET_WIDTH = N_RET_HEADS * RET_HEAD_DIM
CHUNK = 128
N_EXPERTS = 64
N_EXPERT_GROUPS = 8
EXPERTS_PER_GROUP = N_EXPERTS // N_EXPERT_GROUPS
TOPK_GROUPS = 4
TOP_K = 8
EXPERT_DIM = 256
ROUTED_SCALE = 2.5
ROPE_BASE = 10000.0
EPS = 1e-6
Q_SCALE = RET_HEAD_DIM ** -0.5

_C_UF = (0, 1024)
_C_Q = (1024, 1536)
_C_K = (1536, 2048)
_C_V = (2048, 2560)
_C_G = (2560, 3072)
_C_GF = (3072, 4096)
_C_GR = (4096, 5120)

VMEM_LIMIT = 56 * 1024 * 1024

TM_PROJ = 512
FNET_ROWS = 256
RET_ALL_HEADS_MAX_LEN = 1024
TM_ROUTER = 1024
TM_FINAL = 512
EXPERT_ROWS = 512
TILES_PER_STEP = 2
ROW_SLABS = 4
SC_CORES = 2
SC_WORKERS = 32
SC_CHUNK = 128
SC_LANES = 16
SC_PACK_BLOCK_WORDS = 16384


def _silu(x):
    return x * jax.nn.sigmoid(x)


def _dot(a, b):
    return jnp.dot(a, b, preferred_element_type=F32)


def _rms_mod(x, g, shift, scale):
    ms = jnp.mean(x * x, axis=-1, keepdims=True)
    y = x * lax.rsqrt(ms + EPS) * g
    return y * (1.0 + scale) + shift


def _ada_kernel(cond_ref, w_ref, b_ref, o_ref):
    s = _silu(cond_ref[...]).astype(BF16)
    o_ref[...] = _dot(s, w_ref[...].astype(BF16)) + b_ref[...]


def _ada(cond, w_ada, b_ada):
    rows, n = cond.shape[0], w_ada.shape[1]
    tn = 1536
    return pl.pallas_call(
        _ada_kernel,
        grid=(n // tn,),
        in_specs=[pl.BlockSpec((rows, D_MODEL), lambda j: (0, 0)),
                  pl.BlockSpec((D_MODEL, tn), lambda j: (0, j)),
                  pl.BlockSpec((1, tn), lambda j: (0, j))],
        out_specs=pl.BlockSpec((rows, tn), lambda j: (0, j)),
        out_shape=jax.ShapeDtypeStruct((rows, n), F32),
        compiler_params=pltpu.CompilerParams(vmem_limit_bytes=VMEM_LIMIT),
        name="ada",
    )(cond, w_ada, b_ada)


def _rope_head(x, cos, sin_signed, first_half):
    partner = jnp.where(first_half, pltpu.roll(x, 96, 1), pltpu.roll(x, 32, 1))
    return x * cos + partner * sin_signed


def _inproj_kernel(*refs, use_rope):
    if use_rope:
        x_ref, mod_ref, g_ref, w_ref, cos_ref, sin_ref = refs[:6]
        outs = refs[6:]
    else:
        x_ref, mod_ref, g_ref, w_ref = refs[:4]
        outs = refs[4:]
    uf_o, q_o, k_o, v_o, sg_o, gf_o, gr_o = outs

    h = _rms_mod(x_ref[...], g_ref[...], mod_ref[0, 0:1, :], mod_ref[0, 1:2, :])
    hb = h.astype(BF16)

    def proj(cols):
        return _dot(hb, w_ref[:, cols[0]:cols[1]])

    uf_o[...] = proj(_C_UF).astype(BF16)
    q = proj(_C_Q)
    k = proj(_C_K)
    if use_rope:
        cos = cos_ref[...]
        sin_signed = sin_ref[...]
        lane = lax.broadcasted_iota(jnp.int32, cos.shape, 1)
        first_half = (lane & 32) == 0
        for hd in range(N_RET_HEADS):
            sl = slice(hd * RET_HEAD_DIM, (hd + 1) * RET_HEAD_DIM)
            q_o[:, sl] = (_rope_head(q[:, sl], cos, sin_signed, first_half) * Q_SCALE).astype(BF16)
            k_o[:, sl] = _rope_head(k[:, sl], cos, sin_signed, first_half).astype(BF16)
    else:
        q_o[...] = (q * Q_SCALE).astype(BF16)
        k_o[...] = k.astype(BF16)
    v_o[...] = proj(_C_V).astype(BF16)
    sg_o[...] = _silu(proj(_C_G)).astype(BF16)
    gf_o[...] = jax.nn.sigmoid(proj(_C_GF)).astype(BF16)
    gr_o[...] = jax.nn.sigmoid(proj(_C_GR)).astype(BF16)


def _inproj(x2d, mod3, norm_g, w_in_bf, seq_len, mod_row_of_batch, rope):
    t = x2d.shape[0]
    tm = TM_PROJ
    tiles_per_seq = max(seq_len // tm, 1)

    def mod_idx(i):
        return (mod_row_of_batch((i * tm) // seq_len), 0, 0)

    in_specs = [pl.BlockSpec((tm, D_MODEL), lambda i: (i, 0)),
                pl.BlockSpec((1, 6, D_MODEL), mod_idx),
                pl.BlockSpec((1, D_MODEL), lambda i: (0, 0)),
                pl.BlockSpec(w_in_bf.shape, lambda i: (0, 0))]
    args = [x2d, mod3, norm_g, w_in_bf]
    if rope is not None:
        in_specs += [pl.BlockSpec((tm, RET_HEAD_DIM), lambda i: (i % tiles_per_seq, 0))] * 2
        args += list(rope)
    widths = [1024, RET_WIDTH, RET_WIDTH, RET_WIDTH, RET_WIDTH, 1024, 1024]
    return pl.pallas_call(
        functools.partial(_inproj_kernel, use_rope=rope is not None),
        grid=(t // tm,),
        in_specs=in_specs,
        out_specs=[pl.BlockSpec((tm, w), lambda i: (i, 0)) for w in widths],
        out_shape=[jax.ShapeDtypeStruct((t, w), BF16) for w in widths],
        compiler_params=pltpu.CompilerParams(dimension_semantics=("parallel",),
                                             vmem_limit_bytes=VMEM_LIMIT),
        name="inproj",
    )(*args)


def _retention_kernel(q_ref, k_ref, v_ref, sg_ref, dec_ref, gn_ref, s0f_ref, s0b_ref,
                      r_ref, sfo_ref, sbo_ref, tab_scr, gc_scr):
    n_chunks = q_ref.shape[0] // CHUNK
    hd = RET_HEAD_DIM
    heads_here = q_ref.shape[1] // hd
    first_head = pl.program_id(1) * heads_here

    @pl.when((pl.program_id(0) == 0) & (pl.program_id(1) == 0))
    def _():
        row = lax.broadcasted_iota(jnp.int32, (CHUNK, CHUNK), 0).astype(F32)
        col = lax.broadcasted_iota(jnp.int32, (CHUNK, CHUNK), 1).astype(F32)
        diff = row - col
        for h in range(N_RET_HEADS):
            dec = dec_ref[h]
            lg = jnp.minimum(dec, 0.0) - jnp.log1p(jnp.exp(-jnp.abs(dec)))
            lgf = lg[0:1, :]
            lgb = lg[1:2, :]
            tab_scr[h, 0] = jnp.exp(jnp.where(diff >= 0, lgf * diff, lgb * (-diff)))
            tab_scr[h, 1] = jnp.exp(lgf * (row + 1.0))
            tab_scr[h, 2] = jnp.exp(lgb * (CHUNK - row))
            tab_scr[h, 3] = jnp.exp(lgf * (CHUNK - 1.0 - col))
            tab_scr[h, 4] = jnp.exp(lgb * col)
            gc_scr[h] = jnp.exp(lg * CHUNK)

    def rows(n):
        return slice(n * CHUNK, (n + 1) * CHUNK)

    for h in range(heads_here):
        cols = slice(h * hd, (h + 1) * hd)
        head = first_head + h
        decay, qw_f, qw_b, kwt_f, kwt_b = (tab_scr[head, i] for i in range(5))
        gc = gc_scr[head]
        gc_f = gc[0:1, :]
        gc_b = gc[1:2, :]

        kv_f, kv_b = [], []
        for n in range(n_chunks):
            kt = k_ref[rows(n), cols].astype(F32).T
            vn = v_ref[rows(n), cols]
            kv_f.append(_dot((kt * kwt_f).astype(BF16), vn))
            kv_b.append(_dot((kt * kwt_b).astype(BF16), vn))

        s = s0f_ref[h]
        prev_f = []
        for n in range(n_chunks):
            prev_f.append(s.astype(BF16))
            s = gc_f * s + kv_f[n]
        sfo_ref[h] = s
        s = s0b_ref[h]
        prev_b = [None] * n_chunks
        for n in reversed(range(n_chunks)):
            prev_b[n] = s.astype(BF16)
            s = gc_b * s + kv_b[n]
        sbo_ref[h] = s

        gn = gn_ref[:, cols]
        for n in range(n_chunks):
            qn = q_ref[rows(n), cols]
            qf = qn.astype(F32)
            scores = lax.dot_general(qn, k_ref[rows(n), cols], (((1,), (1,)), ((), ())),
                                     preferred_element_type=F32)
            o = _dot((scores * decay).astype(BF16), v_ref[rows(n), cols])
            o = o + _dot((qf * qw_f).astype(BF16), prev_f[n])
            o = o + _dot((qf * qw_b).astype(BF16), prev_b[n])
            mu = jnp.mean(o, axis=-1, keepdims=True)
            d = o - mu
            var = jnp.mean(d * d, axis=-1, keepdims=True)
            on = d * lax.rsqrt(var + EPS) * gn
            r_ref[rows(n), cols] = (on * sg_ref[rows(n), cols].astype(F32)).astype(BF16)


def _retention(q, k, v, sg, dec, gn_g, s0f, s0b, batch, seq_len):
    hd = RET_HEAD_DIM
    heads_per_step = N_RET_HEADS if seq_len <= RET_ALL_HEADS_MAX_LEN else 1
    width = heads_per_step * hd
    tok_spec = pl.BlockSpec((seq_len, width), lambda b, g: (b, g))
    st_spec = pl.BlockSpec((None, heads_per_step, hd, hd), lambda b, g: (b, g, 0, 0))
    st_shape = jax.ShapeDtypeStruct((batch, N_RET_HEADS, hd, hd), F32)
    return pl.pallas_call(
        _retention_kernel,
        grid=(batch, N_RET_HEADS // heads_per_step),
        in_specs=[tok_spec, tok_spec, tok_spec, tok_spec,
                  pl.BlockSpec(dec.shape, lambda b, g: (0, 0, 0)),
                  pl.BlockSpec((1, width), lambda b, g: (0, g)),
                  st_spec, st_spec],
        out_specs=[tok_spec, st_spec, st_spec],
        out_shape=[jax.ShapeDtypeStruct((batch * seq_len, RET_WIDTH), BF16), st_shape, st_shape],
        scratch_shapes=[pltpu.VMEM((N_RET_HEADS, 5, CHUNK, CHUNK), F32),
                        pltpu.VMEM((N_RET_HEADS, 2, hd), F32)],
        compiler_params=pltpu.CompilerParams(dimension_semantics=("arbitrary", "arbitrary"),
                                             vmem_limit_bytes=VMEM_LIMIT),
        name="retention",
    )(q, k, v, sg, dec, gn_g, s0f, s0b)


def _fnet_kernel(uf_ref, cs_ref, cls_ref, o_ref, xcs_ref):
    seq_len = uf_ref.shape[0]
    gd = FOURIER_GROUP_DIM

    @pl.when(pl.program_id(1) == 0)
    def _():
        for g in range(N_FOURIER_GROUPS):
            x = _dot(uf_ref[:, g * gd:(g + 1) * gd], cs_ref[...])
            xcs_ref[0:seq_len, g * gd:(g + 1) * gd] = x[:, :gd].astype(BF16)
            xcs_ref[seq_len:2 * seq_len, g * gd:(g + 1) * gd] = x[:, gd:].astype(BF16)

    o_ref[...] = _dot(cls_ref[...], xcs_ref[...]).astype(BF16)


def _fnet(uf, cs, cls, batch, seq_len):
    rb = FNET_ROWS
    nr = seq_len // rb
    return pl.pallas_call(
        _fnet_kernel,
        grid=(batch, nr),
        in_specs=[pl.BlockSpec((seq_len, D_MODEL), lambda b, r: (b, 0)),
                  pl.BlockSpec(cs.shape, lambda b, r: (0, 0)),
                  pl.BlockSpec((rb, 2 * seq_len), lambda b, r: (r, 0))],
        out_specs=pl.BlockSpec((rb, D_MODEL), lambda b, r: (b * nr + r, 0)),
        out_shape=jax.ShapeDtypeStruct((batch * seq_len, D_MODEL), BF16),
        scratch_shapes=[pltpu.VMEM((2 * seq_len, D_MODEL), BF16)],
        compiler_params=pltpu.CompilerParams(dimension_semantics=("parallel", "arbitrary"),
                                             vmem_limit_bytes=VMEM_LIMIT),
        name="fnet",
    )(uf, cs, cls)


def _merge_kernel(fm_ref, r_ref, gf_ref, gr_ref, x_ref, mod_ref, wf_ref, wr_ref, wo_ref, o_ref):
    f_out = _dot(fm_ref[...], wf_ref[...])
    r_out = _dot(r_ref[...], wr_ref[...])
    merged = gf_ref[...].astype(F32) * f_out + gr_ref[...].astype(F32) * r_out
    mix = _dot(merged.astype(BF16), wo_ref[...])
    o_ref[...] = x_ref[...] + mod_ref[0, 2:3, :] * mix


def _merge(fmix, r, gf, gr, x2d, mod3, w_four, w_ret, w_o, seq_len, mod_row_of_batch):
    t = x2d.shape[0]
    tm = TM_PROJ

    def mod_idx(i):
        return (mod_row_of_batch((i * tm) // seq_len), 0, 0)

    def tok(w):
        return pl.BlockSpec((tm, w), lambda i: (i, 0))

    def full(a):
        return pl.BlockSpec(a.shape, lambda i: (0, 0))

    return pl.pallas_call(
        _merge_kernel,
        grid=(t // tm,),
        in_specs=[tok(D_MODEL), tok(RET_WIDTH), tok(D_MODEL), tok(D_MODEL), tok(D_MODEL),
                  pl.BlockSpec((1, 6, D_MODEL), mod_idx), full(w_four), full(w_ret), full(w_o)],
        out_specs=tok(D_MODEL),
        out_shape=jax.ShapeDtypeStruct((t, D_MODEL), F32),
        compiler_params=pltpu.CompilerParams(dimension_semantics=("parallel",),
                                             vmem_limit_bytes=VMEM_LIMIT),
        name="merge",
    )(fmix, r, gf, gr, x2d, mod3, w_four, w_ret, w_o)


def _pack_pair(lo_f32, hi_f32):
    lo = lax.bitcast_convert_type(lo_f32.astype(BF16).astype(F32), jnp.uint32)
    hi = lax.bitcast_convert_type(hi_f32.astype(BF16).astype(F32), jnp.uint32)
    return lax.bitcast_convert_type((lo >> 16) | hi, jnp.int32)


def _unpack_pair(words_i32):
    w = lax.bitcast_convert_type(words_i32, jnp.uint32)
    lo = lax.bitcast_convert_type(w << 16, F32)
    hi = lax.bitcast_convert_type(w & jnp.uint32(0xFFFF0000), F32)
    return lo, hi


def _load_token_words(ref, lead, n_tok):
    parts = []
    for s in range(ROW_SLABS):
        idx = (pl.ds(s, n_tok, stride=ROW_SLABS), slice(None))
        parts.append(ref[lead + idx] if lead else ref[idx])
    return jnp.concatenate(parts, axis=1)


def _store_token_words(ref, words, n_tok):
    for s in range(ROW_SLABS):
        ref[pl.ds(s, n_tok, stride=ROW_SLABS), :] = words[:, s * 128:(s + 1) * 128]


def _route(scores, biased):
    tokens = scores.shape[1]
    neg = -jnp.inf
    epg = EXPERTS_PER_GROUP
    iota_g = lax.broadcasted_iota(jnp.int32, (epg, tokens), 0).astype(F32)

    def pick_first_max(cur, iota, size):
        m = jnp.max(cur, axis=0, keepdims=True)
        idx = jnp.min(jnp.where(cur == m, iota, float(size)), axis=0, keepdims=True)
        return m, idx, iota == idx

    group_scores = []
    for g in range(N_EXPERT_GROUPS):
        vals = biased[g * epg:(g + 1) * epg, :]
        m1, _, hit = pick_first_max(vals, iota_g, epg)
        m2 = jnp.max(jnp.where(hit, neg, vals), axis=0, keepdims=True)
        group_scores.append(m1 + m2)
    cur = jnp.concatenate(group_scores, axis=0)
    group_sel = jnp.zeros_like(cur)
    for _ in range(TOPK_GROUPS):
        _, _, hit = pick_first_max(cur, iota_g, N_EXPERT_GROUPS)
        group_sel = jnp.where(hit, 1.0, group_sel)
        cur = jnp.where(hit, neg, cur)
    masked = jnp.concatenate(
        [jnp.where(group_sel[g:g + 1, :] > 0.0, biased[g * epg:(g + 1) * epg, :], neg)
         for g in range(N_EXPERT_GROUPS)], axis=0)
    iota_e = lax.broadcasted_iota(jnp.int32, masked.shape, 0).astype(F32)
    sel = jnp.zeros_like(masked)
    cur = masked
    picks = []
    for _ in range(TOP_K):
        _, idx, hit = pick_first_max(cur, iota_e, N_EXPERTS)
        picks.append(idx)
        sel = jnp.where(hit, 1.0, sel)
        cur = jnp.where(hit, neg, cur)
    w = scores * sel
    return w / jnp.sum(w, axis=0, keepdims=True) * ROUTED_SCALE, sel, picks


def _router_kernel(x_ref, mod_ref, g2_ref, wrt_ref, rb_ref, hp_ref, ek_ref, rk_ref, wt_ref, cnt_ref, run_scr):
    tm = x_ref.shape[0]

    @pl.when(pl.program_id(0) == 0)
    def _():
        run_scr[...] = jnp.zeros_like(run_scr)

    h = _rms_mod(x_ref[...], g2_ref[...], mod_ref[0, 3:4, :], mod_ref[0, 4:5, :])
    half = D_MODEL // 2
    _store_token_words(hp_ref, _pack_pair(h[:, :half], h[:, half:]), tm)

    logits_t = lax.dot_general(wrt_ref[...], h, (((1,), (1,)), ((), ())),
                               precision=lax.Precision.HIGHEST, preferred_element_type=F32)
    scores = jax.nn.sigmoid(logits_t)
    comb_t, sel, picks = _route(scores, scores + rb_ref[...])

    earlier = (lax.broadcasted_iota(jnp.int32, (tm, tm), 0) < lax.broadcasted_iota(jnp.int32, (tm, tm), 1))
    rank_t = _dot(sel.astype(BF16), jnp.where(earlier, 1.0, 0.0).astype(BF16)) + run_scr[...]
    run_scr[...] += jnp.sum(sel, axis=1, keepdims=True)
    cnt_ref[...] = jnp.broadcast_to(run_scr[...], cnt_ref.shape)

    iota_e = lax.broadcasted_iota(jnp.int32, sel.shape, 0).astype(F32)
    ranks, weights = [], []
    for idx in picks:
        hit = iota_e == idx
        ranks.append(jnp.sum(jnp.where(hit, rank_t, 0.0), axis=0, keepdims=True))
        weights.append(jnp.sum(jnp.where(hit, comb_t, 0.0), axis=0, keepdims=True))
    ek_ref[...] = jnp.concatenate(picks, axis=0).astype(jnp.int32)
    rk_ref[...] = jnp.concatenate(ranks, axis=0).astype(jnp.int32)
    w_pad = jnp.concatenate(weights + [jnp.zeros((128 - TOP_K, tm), F32)], axis=0)
    wt_ref[...] = w_pad.T


def _router(x1, mod3, norm2_g, w_router_t, router_bias, seq_len, mod_row_of_batch):
    t = x1.shape[0]
    tm = TM_ROUTER

    def mod_idx(i):
        return (mod_row_of_batch((i * tm) // seq_len), 0, 0)

    def full(a):
        return pl.BlockSpec(a.shape, lambda i: (0,) * a.ndim)

    return pl.pallas_call(
        _router_kernel,
        grid=(t // tm,),
        in_specs=[pl.BlockSpec((tm, D_MODEL), lambda i: (i, 0)),
                  pl.BlockSpec((1, 6, D_MODEL), mod_idx),
                  full(norm2_g), full(w_router_t), full(router_bias)],
        out_specs=[pl.BlockSpec((tm * ROW_SLABS, 128), lambda i: (i, 0)),
                   pl.BlockSpec((TOP_K, tm), lambda i: (0, i)),
                   pl.BlockSpec((TOP_K, tm), lambda i: (0, i)),
                   pl.BlockSpec((tm, 128), lambda i: (i, 0)),
                   pl.BlockSpec((N_EXPERTS, 128), lambda i: (0, 0))],
        out_shape=[jax.ShapeDtypeStruct((t * ROW_SLABS, 128), jnp.int32),
                   jax.ShapeDtypeStruct((TOP_K, t), jnp.int32),
                   jax.ShapeDtypeStruct((TOP_K, t), jnp.int32),
                   jax.ShapeDtypeStruct((t, 128), F32),
                   jax.ShapeDtypeStruct((N_EXPERTS, 128), F32)],
        scratch_shapes=[pltpu.VMEM((N_EXPERTS, 1), F32)],
        compiler_params=pltpu.CompilerParams(dimension_semantics=("arbitrary",),
                                             vmem_limit_bytes=VMEM_LIMIT),
        name="router",
    )(x1, mod3, norm2_g, w_router_t, router_bias)


def _plan_kernel(ek_ref, rk_ref, cnt_ref, pos_ref, texp_ref, nused_ref, tend_ref):
    rows = float(EXPERT_ROWS)
    cnt = cnt_ref[:, 0:1]
    tiles = jnp.floor((cnt + (rows - 1.0)) / rows)
    before = (lax.broadcasted_iota(jnp.int32, (N_EXPERTS, N_EXPERTS), 1)
              < lax.broadcasted_iota(jnp.int32, (N_EXPERTS, N_EXPERTS), 0))
    tile_start = jnp.dot(jnp.where(before, 1.0, 0.0), jnp.broadcast_to(tiles, (N_EXPERTS, 128)),
                         precision=lax.Precision.HIGHEST, preferred_element_type=F32)[:, 0:1]
    tile_end = tile_start + tiles
    row_start = tile_start * rows

    ek = ek_ref[...]
    pos = rk_ref[...].astype(F32)
    tile_id = lax.broadcasted_iota(jnp.int32, texp_ref.shape, 1).astype(F32)
    texp = jnp.zeros(texp_ref.shape, F32)
    for e in range(N_EXPERTS):
        pos = pos + jnp.where(ek == e, row_start[e:e + 1, :], 0.0)
        texp = texp + jnp.where(tile_id >= tile_end[e:e + 1, :], 1.0, 0.0)
    pos_ref[...] = pos.astype(jnp.int32)
    texp_ref[...] = jnp.minimum(texp, N_EXPERTS - 1.0).astype(jnp.int32)
    nused_ref[...] = jnp.broadcast_to(tile_end[N_EXPERTS - 1:N_EXPERTS, :], nused_ref.shape).astype(jnp.int32)
    tend_ref[...] = jnp.broadcast_to(tile_end, tend_ref.shape).astype(jnp.int32)


def _plan(ek, rk, cnt, n_tiles_pad):
    t = ek.shape[1]

    def full(shape):
        return pl.BlockSpec(shape, lambda: (0,) * len(shape))

    return pl.pallas_call(
        _plan_kernel,
        in_specs=[full(ek.shape), full(rk.shape), full(cnt.shape)],
        out_specs=[full((TOP_K, t)), full((1, n_tiles_pad)), full((1, 128)), full((N_EXPERTS, 128))],
        out_shape=[jax.ShapeDtypeStruct((TOP_K, t), jnp.int32),
                   jax.ShapeDtypeStruct((1, n_tiles_pad), jnp.int32),
                   jax.ShapeDtypeStruct((1, 128), jnp.int32),
                   jax.ShapeDtypeStruct((N_EXPERTS, 128), jnp.int32)],
        compiler_params=pltpu.CompilerParams(vmem_limit_bytes=VMEM_LIMIT),
        name="plan",
    )(ek, rk, cnt)


def _sc_mesh():
    return plsc.VectorSubcoreMesh(core_axis_name="c", subcore_axis_name="s")


def _sc_pack_weight_halves(w):
    e, k, n = w.shape
    k_half = k // 2
    rb = SC_PACK_BLOCK_WORDS // n
    units_per_expert = k_half // rb
    per_w = (e * units_per_expert) // SC_WORKERS
    lanes = SC_LANES

    @functools.partial(
        pl.kernel, out_type=jax.ShapeDtypeStruct((e * k_half, n), jnp.int32), mesh=_sc_mesh(),
        scratch_types=[pltpu.VMEM((rb, n), F32), pltpu.VMEM((rb, n), F32), pltpu.VMEM((rb, n), jnp.int32)],
        compiler_params=pltpu.CompilerParams(needs_layout_passes=False),
        cost_estimate=pl.CostEstimate(flops=e * k * n, transcendentals=0, bytes_accessed=6 * e * k * n))
    def kern(w_hbm, out_hbm, a_v, b_v, o_v):
        wid = lax.axis_index("s") * SC_CORES + lax.axis_index("c")

        @pl.loop(0, per_w)
        def _(j):
            unit = wid * per_w + j
            expert = unit // units_per_expert
            blk = unit % units_per_expert
            row_a = expert * k + blk * rb
            pltpu.sync_copy(w_hbm.at[pl.ds(row_a, rb)], a_v)
            pltpu.sync_copy(w_hbm.at[pl.ds(row_a + k_half, rb)], b_v)

            @pl.loop(0, rb)
            def _(r):
                @plsc.parallel_loop(0, n, step=lanes, unroll=4)
                def _(c):
                    both = plsc.pack(a_v[r, pl.ds(c, lanes)], b_v[r, pl.ds(c, lanes)],
                                     format=plsc.PackFormat.INTERLEAVED)
                    o_v[r, pl.ds(c, lanes)] = plsc.bitcast(both, jnp.int32)

            pltpu.sync_copy(o_v, out_hbm.at[pl.ds(expert * k_half + blk * rb, rb)])

    return kern(w.reshape(e * k, n)).reshape(e, k_half, n)


def _sc_dispatch(rows, pos3, n_out):
    t = rows.shape[0]
    ch = SC_CHUNK
    per_w = (t // ch) // SC_WORKERS

    @functools.partial(
        pl.kernel, out_type=jax.ShapeDtypeStruct((n_out,) + rows.shape[1:], jnp.int32), mesh=_sc_mesh(),
        scratch_types=[pltpu.VMEM((TOP_K, ch), jnp.int32), pltpu.VMEM((ch,) + rows.shape[1:], jnp.int32),
                       pltpu.SemaphoreType.DMA])
    def k(rows_hbm, pos_hbm, out_hbm, idx_v, rows_v, sem):
        wid = lax.axis_index("s") * SC_CORES + lax.axis_index("c")

        @pl.loop(0, per_w)
        def _(j):
            c = wid * per_w + j
            pltpu.sync_copy(pos_hbm.at[c], idx_v)
            pltpu.sync_copy(rows_hbm.at[pl.ds(c * ch, ch)], rows_v)
            copies = [pltpu.async_copy(rows_v, out_hbm.at[idx_v.at[kk]], sem) for kk in range(TOP_K)]
            for cp in copies:
                cp.wait()

    return k(rows, pos3)


def _sc_gather(table, pos3, t):
    ch = SC_CHUNK
    per_w = (t // ch) // SC_WORKERS

    @functools.partial(
        pl.kernel, out_type=jax.ShapeDtypeStruct((TOP_K, t) + table.shape[1:], jnp.int32), mesh=_sc_mesh(),
        scratch_types=[pltpu.VMEM((TOP_K, ch), jnp.int32), pltpu.VMEM((ch,) + table.shape[1:], jnp.int32),
                       pltpu.SemaphoreType.DMA])
    def k(tab_hbm, pos_hbm, out_hbm, idx_v, rows_v, sem):
        wid = lax.axis_index("s") * SC_CORES + lax.axis_index("c")

        @pl.loop(0, per_w)
        def _(j):
            c = wid * per_w + j
            pltpu.sync_copy(pos_hbm.at[c], idx_v)
            for kk in range(TOP_K):
                pltpu.async_copy(tab_hbm.at[idx_v.at[kk]], rows_v, sem).wait()
                pltpu.sync_copy(rows_v, out_hbm.at[kk, pl.ds(c * ch, ch)])

    return k(table, pos3)


def _experts_kernel(texp_ref, nused_ref, tend_ref, xs_ref, weg_hbm, weu_hbm, wed_hbm, ys_ref,
                    wg_scr, wu_scr, wd_scr, wg_buf, wu_buf, wd_buf, sem, group_scr):
    step = pl.program_id(0)
    rows = EXPERT_ROWS
    half = D_MODEL // 2
    n_used = nused_ref[0]

    def weight_copies(e, slot):
        return [pltpu.make_async_copy(weg_hbm.at[e], wg_buf.at[slot], sem.at[slot, 0]),
                pltpu.make_async_copy(weu_hbm.at[e], wu_buf.at[slot], sem.at[slot, 1]),
                pltpu.make_async_copy(wed_hbm.at[e], wd_buf.at[slot], sem.at[slot, 2])]

    @pl.when(step == 0)
    def _():
        group_scr[0] = 0
        for cp in weight_copies(texp_ref[0], 0):
            cp.start()

    def row_tile(tile, x_view, y_view):
        expert = texp_ref[tile]
        used = tile < n_used
        new_expert = (tile == 0) | (expert != texp_ref[jnp.maximum(tile - 1, 0)])

        @pl.when(used & new_expert)
        def _():
            group = group_scr[0]
            slot = group % 2
            next_tile = tend_ref[expert]

            @pl.when(next_tile < n_used)
            def _():
                for cp in weight_copies(texp_ref[next_tile], 1 - slot):
                    cp.start()

            for cp in weight_copies(expert, slot):
                cp.wait()
            for scr, buf in ((wg_scr, wg_buf), (wu_scr, wu_buf), (wd_scr, wd_buf)):
                top, bottom = _unpack_pair(buf[slot])
                k_half = top.shape[0]
                scr[0:k_half, :] = top.astype(BF16)
                scr[k_half:2 * k_half, :] = bottom.astype(BF16)
            group_scr[0] = group + 1

        @pl.when(used)
        def _():
            lo, hi = _unpack_pair(_load_token_words(x_view, (), rows))
            lo = lo.astype(BF16)
            hi = hi.astype(BF16)
            g = _dot(lo, wg_scr[0:half, :]) + _dot(hi, wg_scr[half:D_MODEL, :])
            u = _dot(lo, wu_scr[0:half, :]) + _dot(hi, wu_scr[half:D_MODEL, :])
            y = _dot((_silu(g) * u).astype(BF16), wd_scr[...])
            _store_token_words(y_view, _pack_pair(y[:, :half], y[:, half:]), rows)

        @pl.when(jnp.logical_not(used) & (step == (n_used - 1) // TILES_PER_STEP))
        def _():
            y_view[...] = jnp.zeros_like(y_view)

    for s in range(TILES_PER_STEP):
        view = pl.ds(s * rows * ROW_SLABS, rows * ROW_SLABS)
        row_tile(step * TILES_PER_STEP + s, xs_ref.at[view], ys_ref.at[view])


def _experts(texp, nused, tend, xs2d, weg, weu, wed, n_tiles):
    block = (TILES_PER_STEP * EXPERT_ROWS * ROW_SLABS, 128)
    hbm = pl.BlockSpec(memory_space=pl.ANY)

    def block_idx(j, te, nu, tn):
        return (jnp.minimum(j, (nu[0] - 1) // TILES_PER_STEP), 0)

    grid_spec = pltpu.PrefetchScalarGridSpec(
        num_scalar_prefetch=3,
        grid=(n_tiles // TILES_PER_STEP,),
        in_specs=[pl.BlockSpec(block, block_idx), hbm, hbm, hbm],
        out_specs=pl.BlockSpec(block, block_idx),
        scratch_shapes=[pltpu.VMEM((D_MODEL, EXPERT_DIM), BF16),
                        pltpu.VMEM((D_MODEL, EXPERT_DIM), BF16),
                        pltpu.VMEM((EXPERT_DIM, D_MODEL), BF16),
                        pltpu.VMEM((2,) + weg.shape[1:], jnp.int32),
                        pltpu.VMEM((2,) + weu.shape[1:], jnp.int32),
                        pltpu.VMEM((2,) + wed.shape[1:], jnp.int32),
                        pltpu.SemaphoreType.DMA((2, 3)),
                        pltpu.SMEM((1,), jnp.int32)],
    )
    return pl.pallas_call(
        _experts_kernel,
        grid_spec=grid_spec,
        out_shape=jax.ShapeDtypeStruct(xs2d.shape, jnp.int32),
        compiler_params=pltpu.CompilerParams(dimension_semantics=("arbitrary",),
                                             vmem_limit_bytes=VMEM_LIMIT),
        name="experts",
    )(texp, nused, tend, xs2d, weg, weu, wed)


def _final_kernel(x_ref, hp_ref, y8_ref, wt_ref, mod_ref, wsg_ref, wsu_ref, wsd_ref, fng_ref, o_ref):
    tm = x_ref.shape[0]
    lo, hi = _unpack_pair(_load_token_words(hp_ref, (), tm))
    hb = jnp.concatenate([lo, hi], axis=1).astype(BF16)
    shared = _dot((_silu(_dot(hb, wsg_ref[...])) * _dot(hb, wsu_ref[...])).astype(BF16), wsd_ref[...])
    wt = wt_ref[...]
    r_lo = jnp.zeros((tm, D_MODEL // 2), F32)
    r_hi = jnp.zeros((tm, D_MODEL // 2), F32)
    for k in range(TOP_K):
        lo, hi = _unpack_pair(_load_token_words(y8_ref, (k,), tm))
        wk = wt[:, k:k + 1]
        r_lo = r_lo + wk * lo
        r_hi = r_hi + wk * hi
    routed = jnp.concatenate([r_lo, r_hi], axis=1)
    y = x_ref[...] + mod_ref[0, 5:6, :] * (routed + shared)
    ms = jnp.mean(y * y, axis=-1, keepdims=True)
    o_ref[...] = y * lax.rsqrt(ms + EPS) * fng_ref[...]


def _final(x1, hp2d, y8, wtok, mod3, wsg, wsu, wsd, final_g, seq_len, mod_row_of_batch):
    t = x1.shape[0]
    tm = TM_FINAL

    def mod_idx(i):
        return (mod_row_of_batch((i * tm) // seq_len), 0, 0)

    def full(a):
        return pl.BlockSpec(a.shape, lambda i: (0,) * a.ndim)

    return pl.pallas_call(
        _final_kernel,
        grid=(t // tm,),
        in_specs=[pl.BlockSpec((tm, D_MODEL), lambda i: (i, 0)),
                  pl.BlockSpec((tm * ROW_SLABS, 128), lambda i: (i, 0)),
                  pl.BlockSpec((TOP_K, tm * ROW_SLABS, 128), lambda i: (0, i, 0)),
                  pl.BlockSpec((tm, 128), lambda i: (i, 0)),
                  pl.BlockSpec((1, 6, D_MODEL), mod_idx),
                  full(wsg), full(wsu), full(wsd), full(final_g)],
        out_specs=pl.BlockSpec((tm, D_MODEL), lambda i: (i, 0)),
        out_shape=jax.ShapeDtypeStruct((t, D_MODEL), F32),
        compiler_params=pltpu.CompilerParams(dimension_semantics=("parallel",),
                                             vmem_limit_bytes=VMEM_LIMIT),
        name="final",
    )(x1, hp2d, y8, wtok, mod3, wsg, wsu, wsd, final_g)


def _moe(x1, mod3, lw, seq_len, mod_row_of_batch):
    t = x1.shape[0]
    n_tiles = TOP_K * t // EXPERT_ROWS + N_EXPERTS
    n_tiles_pad = -(-n_tiles // 128) * 128
    hp2d, ek, rk, wtok, cnt = _router(x1, mod3, lw["norm2_g"], lw["w_router_t"], lw["router_bias"],
                                      seq_len, mod_row_of_batch)
    pos, texp, nused, tend = _plan(ek, rk, cnt, n_tiles_pad)
    pos3 = pos.reshape(TOP_K, t // SC_CHUNK, SC_CHUNK).transpose(1, 0, 2)
    xs = _sc_dispatch(hp2d.reshape(t, ROW_SLABS, 128), pos3, n_tiles * EXPERT_ROWS)
    ys2d = _experts(texp.reshape(-1), nused.reshape(-1), tend[:, 0], xs.reshape(-1, 128),
                    lw["weg"], lw["weu"], lw["wed"], n_tiles)
    y8 = _sc_gather(ys2d.reshape(-1, ROW_SLABS, 128), pos3, t)
    return _final(x1, hp2d, y8.reshape(TOP_K, t * ROW_SLABS, 128), wtok, mod3,
                  lw["wsg"], lw["wsu"], lw["wsd"], lw["final_g"], seq_len, mod_row_of_batch)


def _dft_tables(seq_len):
    gd = FOURIER_GROUP_DIM
    kc = np.arange(gd)
    ang_c = ((kc[:, None] * kc[None, :]) % gd) * (2.0 * math.pi / gd)
    cs = np.concatenate([np.cos(ang_c), np.sin(ang_c)], axis=1) * (gd ** -0.5)
    kl = np.arange(seq_len)
    ang_l = ((kl[:, None] * kl[None, :]) % seq_len) * (2.0 * math.pi / seq_len)
    cls = np.concatenate([np.cos(ang_l), -np.sin(ang_l)], axis=1) * (seq_len ** -0.5)
    return jnp.asarray(cs.astype(np.float32), dtype=BF16), jnp.asarray(cls.astype(np.float32), dtype=BF16)


def _rope_tables(length):
    rows = length // GRID_W
    r = np.repeat(np.arange(rows, dtype=np.float32), GRID_W)
    col = np.tile(np.arange(GRID_W, dtype=np.float32), rows)
    nf = RET_HEAD_DIM // 4
    inv = (np.float32(ROPE_BASE) ** (-np.arange(nf, dtype=np.float32) / np.float32(nf))).astype(np.float32)
    ar = r[:, None] * inv[None]
    ac = col[:, None] * inv[None]
    ang = np.concatenate([ar, ar, ac, ac], axis=-1).astype(np.float64)
    sign = np.where((np.arange(RET_HEAD_DIM) & nf) == 0, -1.0, 1.0)
    return (jnp.asarray(np.cos(ang).astype(np.float32)),
            jnp.asarray((np.sin(ang) * sign[None, :]).astype(np.float32)))


def _trunk_path(x, mod3, mod_row_of_batch, s0f, s0b, rope, lw):
    batch, seq_len, _ = x.shape
    x2d = x.reshape(batch * seq_len, D_MODEL)
    uf, q, k, v, sg, gf, gr = _inproj(x2d, mod3, lw["norm1_g"], lw["w_in"], seq_len, mod_row_of_batch, rope)
    r, s_f, s_b = _retention(q, k, v, sg, lw["dec"], lw["gn_g"], s0f, s0b, batch, seq_len)
    cs, cls = _dft_tables(seq_len)
    fmix = _fnet(uf, cs, cls, batch, seq_len)
    x1 = _merge(fmix, r, gf, gr, x2d, mod3, lw["w_four"], lw["w_ret"], lw["w_o"], seq_len, mod_row_of_batch)
    y = _moe(x1, mod3, lw, seq_len, mod_row_of_batch)
    return y.reshape(batch, seq_len, D_MODEL), s_f, s_b


def kernel(x_prompt, x_sample, state_ret_fwd, state_ret_bwd, c, c_ctx, w_ada, b_ada, norm1_g, norm2_g, w_in,
           ret_decay_fwd, ret_decay_bwd, ret_gn_g, w_four_out, w_ret_out, w_out, w_router, router_bias,
           w_exp_gate, w_exp_up, w_exp_down, w_shared_gate, w_shared_up, w_shared_down, final_norm_g):
    depth = w_ada.shape[0]
    assert depth == 1, "final norm is fused into the last layer's MoE kernel"
    n_ctx, n_lat = x_prompt.shape[0], x_sample.shape[0]
    cond = jnp.concatenate([c_ctx[None, :], c], axis=0)
    cond = jnp.pad(cond, ((0, (-cond.shape[0]) % 8), (0, 0)))
    rope = _rope_tables(x_sample.shape[1])
    zeros = jnp.zeros((n_ctx, N_RET_HEADS, RET_HEAD_DIM, RET_HEAD_DIM), F32)

    layer = 0
    mod = _ada(cond, w_ada[layer], b_ada[layer][None, :])
    mod3 = mod.reshape(mod.shape[0], 6, D_MODEL)
    dec = jnp.stack([ret_decay_fwd[layer], ret_decay_bwd[layer]], axis=1)
    lw = {
        "norm1_g": norm1_g[layer][None, :],
        "norm2_g": norm2_g[layer][None, :],
        "w_in": w_in[layer].astype(BF16),
        "dec": jnp.broadcast_to(dec[:, :, None], (N_RET_HEADS, 2, RET_HEAD_DIM)).astype(F32),
        "gn_g": ret_gn_g[layer][None, :],
        "w_four": w_four_out[layer].astype(BF16),
        "w_ret": w_ret_out[layer].astype(BF16),
        "w_o": w_out[layer].astype(BF16),
        "w_router_t": w_router[layer].T,
        "router_bias": router_bias[layer][:, None],
        "weg": _sc_pack_weight_halves(w_exp_gate[layer]),
        "weu": _sc_pack_weight_halves(w_exp_up[layer]),
        "wed": _sc_pack_weight_halves(w_exp_down[layer]),
        "wsg": w_shared_gate[layer].astype(BF16),
        "wsu": w_shared_up[layer].astype(BF16),
        "wsd": w_shared_down[layer].astype(BF16),
        "final_g": final_norm_g[None, :],
    }
    y_prompt, s_f, s_b = _trunk_path(x_prompt, mod3, lambda b: 0, zeros, zeros, None, lw)
    y_sample, _, _ = _trunk_path(x_sample, mod3, lambda b: 1 + b, state_ret_fwd[:, layer],
                                 state_ret_bwd[:, layer], rope, lw)
    return (y_prompt, y_sample, s_f[:, None], s_b[:, None])
```

```python
import functools
import math

import jax
import jax.numpy as jnp
import numpy as np
from jax import lax
from jax.experimental import pallas as pl
from jax.experimental.pallas import tpu as pltpu
from jax.experimental.pallas import tpu_sc as plsc

F32 = jnp.float32
BF16 = jnp.bfloat16

D_MODEL = 1024
GRID_W = 64
N_FOURIER_GROUPS = 8
FOURIER_GROUP_DIM = 128
N_RET_HEADS = 4
RET_HEAD_DIM = 128
RET_WIDTH = N_RET_HEADS * RET_HEAD_DIM
CHUNK = 128
N_EXPERTS = 64
N_EXPERT_GROUPS = 8
EXPERTS_PER_GROUP = N_EXPERTS // N_EXPERT_GROUPS
TOPK_GROUPS = 4
TOP_K = 8
EXPERT_DIM = 256
ROUTED_SCALE = 2.5
ROPE_BASE = 10000.0
EPS = 1e-6
Q_SCALE = RET_HEAD_DIM ** -0.5

_C_UF = (0, 1024)
_C_Q = (1024, 1536)
_C_K = (1536, 2048)
_C_V = (2048, 2560)
_C_G = (2560, 3072)
_C_GF = (3072, 4096)
_C_GR = (4096, 5120)

VMEM_LIMIT = 56 * 1024 * 1024

TM_PROJ = 512
FNET_ROWS = 256
TM_ROUTER = 1024
TM_FINAL = 512
EXPERT_ROWS = 512
TILES_PER_STEP = 2
ROW_SLABS = 4
SC_CORES = 2
SC_WORKERS = 32
SC_CHUNK = 128


def _silu(x):
    return x * jax.nn.sigmoid(x)


def _dot(a, b):
    return jnp.dot(a, b, preferred_element_type=F32)


def _rms_mod(x, g, shift, scale):
    ms = jnp.mean(x * x, axis=-1, keepdims=True)
    y = x * lax.rsqrt(ms + EPS) * g
    return y * (1.0 + scale) + shift


def _ada_kernel(cond_ref, w_ref, b_ref, o_ref):
    s = _silu(cond_ref[...]).astype(BF16)
    o_ref[...] = _dot(s, w_ref[...].astype(BF16)) + b_ref[...]


def _ada(cond, w_ada, b_ada):
    rows, n = cond.shape[0], w_ada.shape[1]
    tn = 1536
    return pl.pallas_call(
        _ada_kernel,
        grid=(n // tn,),
        in_specs=[pl.BlockSpec((rows, D_MODEL), lambda j: (0, 0)),
                  pl.BlockSpec((D_MODEL, tn), lambda j: (0, j)),
                  pl.BlockSpec((1, tn), lambda j: (0, j))],
        out_specs=pl.BlockSpec((rows, tn), lambda j: (0, j)),
        out_shape=jax.ShapeDtypeStruct((rows, n), F32),
        compiler_params=pltpu.CompilerParams(vmem_limit_bytes=VMEM_LIMIT),
        name="ada",
    )(cond, w_ada, b_ada)


def _rope_head(x, cos, sin_signed, first_half):
    partner = jnp.where(first_half, pltpu.roll(x, 96, 1), pltpu.roll(x, 32, 1))
    return x * cos + partner * sin_signed


def _inproj_kernel(*refs, use_rope):
    if use_rope:
        x_ref, mod_ref, g_ref, w_ref, cos_ref, sin_ref = refs[:6]
        outs = refs[6:]
    else:
        x_ref, mod_ref, g_ref, w_ref = refs[:4]
        outs = refs[4:]
    uf_o, q_o, k_o, v_o, sg_o, gf_o, gr_o = outs

    h = _rms_mod(x_ref[...], g_ref[...], mod_ref[0, 0:1, :], mod_ref[0, 1:2, :])
    hb = h.astype(BF16)

    def proj(cols):
        return _dot(hb, w_ref[:, cols[0]:cols[1]])

    uf_o[...] = proj(_C_UF).astype(BF16)
    q = proj(_C_Q)
    k = proj(_C_K)
    if use_rope:
        cos = cos_ref[...]
        sin_signed = sin_ref[...]
        lane = lax.broadcasted_iota(jnp.int32, cos.shape, 1)
        first_half = (lane & 32) == 0
        for hd in range(N_RET_HEADS):
            sl = slice(hd * RET_HEAD_DIM, (hd + 1) * RET_HEAD_DIM)
            q_o[:, sl] = (_rope_head(q[:, sl], cos, sin_signed, first_half) * Q_SCALE).astype(BF16)
            k_o[:, sl] = _rope_head(k[:, sl], cos, sin_signed, first_half).astype(BF16)
    else:
        q_o[...] = (q * Q_SCALE).astype(BF16)
        k_o[...] = k.astype(BF16)
    v_o[...] = proj(_C_V).astype(BF16)
    sg_o[...] = _silu(proj(_C_G)).astype(BF16)
    gf_o[...] = jax.nn.sigmoid(proj(_C_GF)).astype(BF16)
    gr_o[...] = jax.nn.sigmoid(proj(_C_GR)).astype(BF16)


def _inproj(x2d, mod3, norm_g, w_in_bf, seq_len, mod_row_of_batch, rope):
    t = x2d.shape[0]
    tm = TM_PROJ
    tiles_per_seq = max(seq_len // tm, 1)

    def mod_idx(i):
        return (mod_row_of_batch((i * tm) // seq_len), 0, 0)

    in_specs = [pl.BlockSpec((tm, D_MODEL), lambda i: (i, 0)),
                pl.BlockSpec((1, 6, D_MODEL), mod_idx),
                pl.BlockSpec((1, D_MODEL), lambda i: (0, 0)),
                pl.BlockSpec(w_in_bf.shape, lambda i: (0, 0))]
    args = [x2d, mod3, norm_g, w_in_bf]
    if rope is not None:
        in_specs += [pl.BlockSpec((tm, RET_HEAD_DIM), lambda i: (i % tiles_per_seq, 0))] * 2
        args += list(rope)
    widths = [1024, RET_WIDTH, RET_WIDTH, RET_WIDTH, RET_WIDTH, 1024, 1024]
    return pl.pallas_call(
        functools.partial(_inproj_kernel, use_rope=rope is not None),
        grid=(t // tm,),
        in_specs=in_specs,
        out_specs=[pl.BlockSpec((tm, w), lambda i: (i, 0)) for w in widths],
        out_shape=[jax.ShapeDtypeStruct((t, w), BF16) for w in widths],
        compiler_params=pltpu.CompilerParams(dimension_semantics=("parallel",),
                                             vmem_limit_bytes=VMEM_LIMIT),
        name="inproj",
    )(*args)


def _retention_kernel(q_ref, k_ref, v_ref, sg_ref, dec_ref, gn_ref, s0f_ref, s0b_ref,
                      r_ref, sfo_ref, sbo_ref, tab_scr, gc_scr):
    n_chunks = q_ref.shape[0] // CHUNK
    hd = RET_HEAD_DIM

    @pl.when(pl.program_id(0) == 0)
    def _():
        row = lax.broadcasted_iota(jnp.int32, (CHUNK, CHUNK), 0).astype(F32)
        col = lax.broadcasted_iota(jnp.int32, (CHUNK, CHUNK), 1).astype(F32)
        diff = row - col
        for h in range(N_RET_HEADS):
            dec = dec_ref[h]
            lg = jnp.minimum(dec, 0.0) - jnp.log1p(jnp.exp(-jnp.abs(dec)))
            lgf = lg[0:1, :]
            lgb = lg[1:2, :]
            tab_scr[h, 0] = jnp.exp(jnp.where(diff >= 0, lgf * diff, lgb * (-diff)))
            tab_scr[h, 1] = jnp.exp(lgf * (row + 1.0))
            tab_scr[h, 2] = jnp.exp(lgb * (CHUNK - row))
            tab_scr[h, 3] = jnp.exp(lgf * (CHUNK - 1.0 - col))
            tab_scr[h, 4] = jnp.exp(lgb * col)
            gc_scr[h] = jnp.exp(lg * CHUNK)

    def rows(n):
        return slice(n * CHUNK, (n + 1) * CHUNK)

    for h in range(N_RET_HEADS):
        cols = slice(h * hd, (h + 1) * hd)
        decay, qw_f, qw_b, kwt_f, kwt_b = (tab_scr[h, i] for i in range(5))
        gc = gc_scr[h]
        gc_f = gc[0:1, :]
        gc_b = gc[1:2, :]

        kv_f, kv_b = [], []
        for n in range(n_chunks):
            kt = k_ref[rows(n), cols].astype(F32).T
            vn = v_ref[rows(n), cols]
            kv_f.append(_dot((kt * kwt_f).astype(BF16), vn))
            kv_b.append(_dot((kt * kwt_b).astype(BF16), vn))

        s = s0f_ref[h]
        prev_f = []
        for n in range(n_chunks):
            prev_f.append(s.astype(BF16))
            s = gc_f * s + kv_f[n]
        sfo_ref[h] = s
        s = s0b_ref[h]
        prev_b = [None] * n_chunks
        for n in reversed(range(n_chunks)):
            prev_b[n] = s.astype(BF16)
            s = gc_b * s + kv_b[n]
        sbo_ref[h] = s

        gn = gn_ref[:, cols]
        for n in range(n_chunks):
            qn = q_ref[rows(n), cols]
            qf = qn.astype(F32)
            scores = lax.dot_general(qn, k_ref[rows(n), cols], (((1,), (1,)), ((), ())),
                                     preferred_element_type=F32)
            o = _dot((scores * decay).astype(BF16), v_ref[rows(n), cols])
            o = o + _dot((qf * qw_f).astype(BF16), prev_f[n])
            o = o + _dot((qf * qw_b).astype(BF16), prev_b[n])
            mu = jnp.mean(o, axis=-1, keepdims=True)
            d = o - mu
            var = jnp.mean(d * d, axis=-1, keepdims=True)
            on = d * lax.rsqrt(var + EPS) * gn
            r_ref[rows(n), cols] = (on * sg_ref[rows(n), cols].astype(F32)).astype(BF16)


def _retention(q, k, v, sg, dec, gn_g, s0f, s0b, batch, seq_len):
    hd = RET_HEAD_DIM
    tok_spec = pl.BlockSpec((seq_len, RET_WIDTH), lambda b: (b, 0))
    st_spec = pl.BlockSpec((None, N_RET_HEADS, hd, hd), lambda b: (b, 0, 0, 0))
    st_shape = jax.ShapeDtypeStruct((batch, N_RET_HEADS, hd, hd), F32)
    return pl.pallas_call(
        _retention_kernel,
        grid=(batch,),
        in_specs=[tok_spec, tok_spec, tok_spec, tok_spec,
                  pl.BlockSpec(dec.shape, lambda b: (0, 0, 0)),
                  pl.BlockSpec(gn_g.shape, lambda b: (0, 0)),
                  st_spec, st_spec],
        out_specs=[tok_spec, st_spec, st_spec],
        out_shape=[jax.ShapeDtypeStruct((batch * seq_len, RET_WIDTH), BF16), st_shape, st_shape],
        scratch_shapes=[pltpu.VMEM((N_RET_HEADS, 5, CHUNK, CHUNK), F32),
                        pltpu.VMEM((N_RET_HEADS, 2, hd), F32)],
        compiler_params=pltpu.CompilerParams(dimension_semantics=("arbitrary",),
                                             vmem_limit_bytes=VMEM_LIMIT),
        name="retention",
    )(q, k, v, sg, dec, gn_g, s0f, s0b)


def _fnet_kernel(uf_ref, cs_ref, cls_ref, o_ref, xcs_ref):
    seq_len = uf_ref.shape[0]
    gd = FOURIER_GROUP_DIM

    @pl.when(pl.program_id(1) == 0)
    def _():
        for g in range(N_FOURIER_GROUPS):
            x = _dot(uf_ref[:, g * gd:(g + 1) * gd], cs_ref[...])
            xcs_ref[0:seq_len, g * gd:(g + 1) * gd] = x[:, :gd].astype(BF16)
            xcs_ref[seq_len:2 * seq_len, g * gd:(g + 1) * gd] = x[:, gd:].astype(BF16)

    o_ref[...] = _dot(cls_ref[...], xcs_ref[...]).astype(BF16)


def _fnet(uf, cs, cls, batch, seq_len):
    rb = FNET_ROWS
    nr = seq_len // rb
    return pl.pallas_call(
        _fnet_kernel,
        grid=(batch, nr),
        in_specs=[pl.BlockSpec((seq_len, D_MODEL), lambda b, r: (b, 0)),
                  pl.BlockSpec(cs.shape, lambda b, r: (0, 0)),
                  pl.BlockSpec((rb, 2 * seq_len), lambda b, r: (r, 0))],
        out_specs=pl.BlockSpec((rb, D_MODEL), lambda b, r: (b * nr + r, 0)),
        out_shape=jax.ShapeDtypeStruct((batch * seq_len, D_MODEL), BF16),
        scratch_shapes=[pltpu.VMEM((2 * seq_len, D_MODEL), BF16)],
        compiler_params=pltpu.CompilerParams(dimension_semantics=("parallel", "arbitrary"),
                                             vmem_limit_bytes=VMEM_LIMIT),
        name="fnet",
    )(uf, cs, cls)


def _merge_kernel(fm_ref, r_ref, gf_ref, gr_ref, x_ref, mod_ref, wf_ref, wr_ref, wo_ref, o_ref):
    f_out = _dot(fm_ref[...], wf_ref[...])
    r_out = _dot(r_ref[...], wr_ref[...])
    merged = gf_ref[...].astype(F32) * f_out + gr_ref[...].astype(F32) * r_out
    mix = _dot(merged.astype(BF16), wo_ref[...])
    o_ref[...] = x_ref[...] + mod_ref[0, 2:3, :] * mix


def _merge(fmix, r, gf, gr, x2d, mod3, w_four, w_ret, w_o, seq_len, mod_row_of_batch):
    t = x2d.shape[0]
    tm = TM_PROJ

    def mod_idx(i):
        return (mod_row_of_batch((i * tm) // seq_len), 0, 0)

    def tok(w):
        return pl.BlockSpec((tm, w), lambda i: (i, 0))

    def full(a):
        return pl.BlockSpec(a.shape, lambda i: (0, 0))

    return pl.pallas_call(
        _merge_kernel,
        grid=(t // tm,),
        in_specs=[tok(D_MODEL), tok(RET_WIDTH), tok(D_MODEL), tok(D_MODEL), tok(D_MODEL),
                  pl.BlockSpec((1, 6, D_MODEL), mod_idx), full(w_four), full(w_ret), full(w_o)],
        out_specs=tok(D_MODEL),
        out_shape=jax.ShapeDtypeStruct((t, D_MODEL), F32),
        compiler_params=pltpu.CompilerParams(dimension_semantics=("parallel",),
                                             vmem_limit_bytes=VMEM_LIMIT),
        name="merge",
    )(fmix, r, gf, gr, x2d, mod3, w_four, w_ret, w_o)


def _pack_pair(lo_f32, hi_f32):
    lo = lax.bitcast_convert_type(lo_f32.astype(BF16).astype(F32), jnp.uint32)
    hi = lax.bitcast_convert_type(hi_f32.astype(BF16).astype(F32), jnp.uint32)
    return lax.bitcast_convert_type((lo >> 16) | hi, jnp.int32)


def _unpack_pair(words_i32):
    w = lax.bitcast_convert_type(words_i32, jnp.uint32)
    lo = lax.bitcast_convert_type(w << 16, F32)
    hi = lax.bitcast_convert_type(w & jnp.uint32(0xFFFF0000), F32)
    return lo, hi


def _load_token_words(ref, lead, n_tok):
    parts = []
    for s in range(ROW_SLABS):
        idx = (pl.ds(s, n_tok, stride=ROW_SLABS), slice(None))
        parts.append(ref[lead + idx] if lead else ref[idx])
    return jnp.concatenate(parts, axis=1)


def _store_token_words(ref, words, n_tok):
    for s in range(ROW_SLABS):
        ref[pl.ds(s, n_tok, stride=ROW_SLABS), :] = words[:, s * 128:(s + 1) * 128]


def _route(scores, biased):
    tokens = scores.shape[1]
    neg = -jnp.inf
    epg = EXPERTS_PER_GROUP
    iota_g = lax.broadcasted_iota(jnp.int32, (epg, tokens), 0).astype(F32)

    def pick_first_max(cur, iota, size):
        m = jnp.max(cur, axis=0, keepdims=True)
        idx = jnp.min(jnp.where(cur == m, iota, float(size)), axis=0, keepdims=True)
        return m, idx, iota == idx

    group_scores = []
    for g in range(N_EXPERT_GROUPS):
        vals = biased[g * epg:(g + 1) * epg, :]
        m1, _, hit = pick_first_max(vals, iota_g, epg)
        m2 = jnp.max(jnp.where(hit, neg, vals), axis=0, keepdims=True)
        group_scores.append(m1 + m2)
    cur = jnp.concatenate(group_scores, axis=0)
    group_sel = jnp.zeros_like(cur)
    for _ in range(TOPK_GROUPS):
        _, _, hit = pick_first_max(cur, iota_g, N_EXPERT_GROUPS)
        group_sel = jnp.where(hit, 1.0, group_sel)
        cur = jnp.where(hit, neg, cur)
    masked = jnp.concatenate(
        [jnp.where(group_sel[g:g + 1, :] > 0.0, biased[g * epg:(g + 1) * epg, :], neg)
         for g in range(N_EXPERT_GROUPS)], axis=0)
    iota_e = lax.broadcasted_iota(jnp.int32, masked.shape, 0).astype(F32)
    sel = jnp.zeros_like(masked)
    cur = masked
    picks = []
    for _ in range(TOP_K):
        _, idx, hit = pick_first_max(cur, iota_e, N_EXPERTS)
        picks.append(idx)
        sel = jnp.where(hit, 1.0, sel)
        cur = jnp.where(hit, neg, cur)
    w = scores * sel
    return w / jnp.sum(w, axis=0, keepdims=True) * ROUTED_SCALE, sel, picks


def _router_kernel(x_ref, mod_ref, g2_ref, wrt_ref, rb_ref, hp_ref, ek_ref, rk_ref, wt_ref, cnt_ref, run_scr):
    tm = x_ref.shape[0]

    @pl.when(pl.program_id(0) == 0)
    def _():
        run_scr[...] = jnp.zeros_like(run_scr)

    h = _rms_mod(x_ref[...], g2_ref[...], mod_ref[0, 3:4, :], mod_ref[0, 4:5, :])
    half = D_MODEL // 2
    _store_token_words(hp_ref, _pack_pair(h[:, :half], h[:, half:]), tm)

    logits_t = lax.dot_general(wrt_ref[...], h, (((1,), (1,)), ((), ())),
                               precision=lax.Precision.HIGHEST, preferred_element_type=F32)
    scores = jax.nn.sigmoid(logits_t)
    comb_t, sel, picks = _route(scores, scores + rb_ref[...])

    earlier = (lax.broadcasted_iota(jnp.int32, (tm, tm), 0) < lax.broadcasted_iota(jnp.int32, (tm, tm), 1))
    rank_t = _dot(sel.astype(BF16), jnp.where(earlier, 1.0, 0.0).astype(BF16)) + run_scr[...]
    run_scr[...] += jnp.sum(sel, axis=1, keepdims=True)
    cnt_ref[...] = jnp.broadcast_to(run_scr[...], cnt_ref.shape)

    iota_e = lax.broadcasted_iota(jnp.int32, sel.shape, 0).astype(F32)
    ranks, weights = [], []
    for idx in picks:
        hit = iota_e == idx
        ranks.append(jnp.sum(jnp.where(hit, rank_t, 0.0), axis=0, keepdims=True))
        weights.append(jnp.sum(jnp.where(hit, comb_t, 0.0), axis=0, keepdims=True))
    ek_ref[...] = jnp.concatenate(picks, axis=0).astype(jnp.int32)
    rk_ref[...] = jnp.concatenate(ranks, axis=0).astype(jnp.int32)
    w_pad = jnp.concatenate(weights + [jnp.zeros((128 - TOP_K, tm), F32)], axis=0)
    wt_ref[...] = w_pad.T


def _router(x1, mod3, norm2_g, w_router_t, router_bias, seq_len, mod_row_of_batch):
    t = x1.shape[0]
    tm = TM_ROUTER

    def mod_idx(i):
        return (mod_row_of_batch((i * tm) // seq_len), 0, 0)

    def full(a):
        return pl.BlockSpec(a.shape, lambda i: (0,) * a.ndim)

    return pl.pallas_call(
        _router_kernel,
        grid=(t // tm,),
        in_specs=[pl.BlockSpec((tm, D_MODEL), lambda i: (i, 0)),
                  pl.BlockSpec((1, 6, D_MODEL), mod_idx),
                  full(norm2_g), full(w_router_t), full(router_bias)],
        out_specs=[pl.BlockSpec((tm * ROW_SLABS, 128), lambda i: (i, 0)),
                   pl.BlockSpec((TOP_K, tm), lambda i: (0, i)),
                   pl.BlockSpec((TOP_K, tm), lambda i: (0, i)),
                   pl.BlockSpec((tm, 128), lambda i: (i, 0)),
                   pl.BlockSpec((N_EXPERTS, 128), lambda i: (0, 0))],
        out_shape=[jax.ShapeDtypeStruct((t * ROW_SLABS, 128), jnp.int32),
                   jax.ShapeDtypeStruct((TOP_K, t), jnp.int32),
                   jax.ShapeDtypeStruct((TOP_K, t), jnp.int32),
                   jax.ShapeDtypeStruct((t, 128), F32),
                   jax.ShapeDtypeStruct((N_EXPERTS, 128), F32)],
        scratch_shapes=[pltpu.VMEM((N_EXPERTS, 1), F32)],
        compiler_params=pltpu.CompilerParams(dimension_semantics=("arbitrary",),
                                             vmem_limit_bytes=VMEM_LIMIT),
        name="router",
    )(x1, mod3, norm2_g, w_router_t, router_bias)


def _plan_kernel(ek_ref, rk_ref, cnt_ref, pos_ref, texp_ref, nused_ref, tend_ref):
    rows = float(EXPERT_ROWS)
    cnt = cnt_ref[:, 0:1]
    tiles = jnp.floor((cnt + (rows - 1.0)) / rows)
    before = (lax.broadcasted_iota(jnp.int32, (N_EXPERTS, N_EXPERTS), 1)
              < lax.broadcasted_iota(jnp.int32, (N_EXPERTS, N_EXPERTS), 0))
    tile_start = jnp.dot(jnp.where(before, 1.0, 0.0), jnp.broadcast_to(tiles, (N_EXPERTS, 128)),
                         precision=lax.Precision.HIGHEST, preferred_element_type=F32)[:, 0:1]
    tile_end = tile_start + tiles
    row_start = tile_start * rows

    ek = ek_ref[...]
    pos = rk_ref[...].astype(F32)
    tile_id = lax.broadcasted_iota(jnp.int32, texp_ref.shape, 1).astype(F32)
    texp = jnp.zeros(texp_ref.shape, F32)
    for e in range(N_EXPERTS):
        pos = pos + jnp.where(ek == e, row_start[e:e + 1, :], 0.0)
        texp = texp + jnp.where(tile_id >= tile_end[e:e + 1, :], 1.0, 0.0)
    pos_ref[...] = pos.astype(jnp.int32)
    texp_ref[...] = jnp.minimum(texp, N_EXPERTS - 1.0).astype(jnp.int32)
    nused_ref[...] = jnp.broadcast_to(tile_end[N_EXPERTS - 1:N_EXPERTS, :], nused_ref.shape).astype(jnp.int32)
    tend_ref[...] = jnp.broadcast_to(tile_end, tend_ref.shape).astype(jnp.int32)


def _plan(ek, rk, cnt, n_tiles_pad):
    t = ek.shape[1]

    def full(shape):
        return pl.BlockSpec(shape, lambda: (0,) * len(shape))

    return pl.pallas_call(
        _plan_kernel,
        in_specs=[full(ek.shape), full(rk.shape), full(cnt.shape)],
        out_specs=[full((TOP_K, t)), full((1, n_tiles_pad)), full((1, 128)), full((N_EXPERTS, 128))],
        out_shape=[jax.ShapeDtypeStruct((TOP_K, t), jnp.int32),
                   jax.ShapeDtypeStruct((1, n_tiles_pad), jnp.int32),
                   jax.ShapeDtypeStruct((1, 128), jnp.int32),
                   jax.ShapeDtypeStruct((N_EXPERTS, 128), jnp.int32)],
        compiler_params=pltpu.CompilerParams(vmem_limit_bytes=VMEM_LIMIT),
        name="plan",
    )(ek, rk, cnt)


def _sc_mesh():
    return plsc.VectorSubcoreMesh(core_axis_name="c", subcore_axis_name="s")


def _sc_dispatch(rows, pos3, n_out):
    t = rows.shape[0]
    ch = SC_CHUNK
    per_w = (t // ch) // SC_WORKERS

    @functools.partial(
        pl.kernel, out_type=jax.ShapeDtypeStruct((n_out,) + rows.shape[1:], jnp.int32), mesh=_sc_mesh(),
        scratch_types=[pltpu.VMEM((TOP_K, ch), jnp.int32), pltpu.VMEM((ch,) + rows.shape[1:], jnp.int32),
                       pltpu.SemaphoreType.DMA])
    def k(rows_hbm, pos_hbm, out_hbm, idx_v, rows_v, sem):
        wid = lax.axis_index("s") * SC_CORES + lax.axis_index("c")

        @pl.loop(0, per_w)
        def _(j):
            c = wid * per_w + j
            pltpu.sync_copy(pos_hbm.at[c], idx_v)
            pltpu.sync_copy(rows_hbm.at[pl.ds(c * ch, ch)], rows_v)
            copies = [pltpu.async_copy(rows_v, out_hbm.at[idx_v.at[kk]], sem) for kk in range(TOP_K)]
            for cp in copies:
                cp.wait()

    return k(rows, pos3)


def _sc_gather(table, pos3, t):
    ch = SC_CHUNK
    per_w = (t // ch) // SC_WORKERS

    @functools.partial(
        pl.kernel, out_type=jax.ShapeDtypeStruct((TOP_K, t) + table.shape[1:], jnp.int32), mesh=_sc_mesh(),
        scratch_types=[pltpu.VMEM((TOP_K, ch), jnp.int32), pltpu.VMEM((ch,) + table.shape[1:], jnp.int32),
                       pltpu.SemaphoreType.DMA])
    def k(tab_hbm, pos_hbm, out_hbm, idx_v, rows_v, sem):
        wid = lax.axis_index("s") * SC_CORES + lax.axis_index("c")

        @pl.loop(0, per_w)
        def _(j):
            c = wid * per_w + j
            pltpu.sync_copy(pos_hbm.at[c], idx_v)
            for kk in range(TOP_K):
                pltpu.async_copy(tab_hbm.at[idx_v.at[kk]], rows_v, sem).wait()
                pltpu.sync_copy(rows_v, out_hbm.at[kk, pl.ds(c * ch, ch)])

    return k(table, pos3)


def _experts_kernel(texp_ref, nused_ref, tend_ref, xs_ref, weg_hbm, weu_hbm, wed_hbm, ys_ref,
                    wg_scr, wu_scr, wd_scr, wg_buf, wu_buf, wd_buf, sem, group_scr):
    step = pl.program_id(0)
    rows = EXPERT_ROWS
    half = D_MODEL // 2
    n_used = nused_ref[0]

    def weight_copies(e, slot):
        return [pltpu.make_async_copy(weg_hbm.at[e], wg_buf.at[slot], sem.at[slot, 0]),
                pltpu.make_async_copy(weu_hbm.at[e], wu_buf.at[slot], sem.at[slot, 1]),
                pltpu.make_async_copy(wed_hbm.at[e], wd_buf.at[slot], sem.at[slot, 2])]

    @pl.when(step == 0)
    def _():
        group_scr[0] = 0
        for cp in weight_copies(texp_ref[0], 0):
            cp.start()

    def row_tile(tile, x_view, y_view):
        expert = texp_ref[tile]
        used = tile < n_used
        new_expert = (tile == 0) | (expert != texp_ref[jnp.maximum(tile - 1, 0)])

        @pl.when(used & new_expert)
        def _():
            group = group_scr[0]
            slot = group % 2
            next_tile = tend_ref[expert]

            @pl.when(next_tile < n_used)
            def _():
                for cp in weight_copies(texp_ref[next_tile], 1 - slot):
                    cp.start()

            for cp in weight_copies(expert, slot):
                cp.wait()
            wg_scr[...] = wg_buf[slot].astype(BF16)
            wu_scr[...] = wu_buf[slot].astype(BF16)
            wd_scr[...] = wd_buf[slot].astype(BF16)
            group_scr[0] = group + 1

        @pl.when(used)
        def _():
            lo, hi = _unpack_pair(_load_token_words(x_view, (), rows))
            lo = lo.astype(BF16)
            hi = hi.astype(BF16)
            g = _dot(lo, wg_scr[0:half, :]) + _dot(hi, wg_scr[half:D_MODEL, :])
            u = _dot(lo, wu_scr[0:half, :]) + _dot(hi, wu_scr[half:D_MODEL, :])
            y = _dot((_silu(g) * u).astype(BF16), wd_scr[...])
            _store_token_words(y_view, _pack_pair(y[:, :half], y[:, half:]), rows)

        @pl.when(jnp.logical_not(used) & (step == (n_used - 1) // TILES_PER_STEP))
        def _():
            y_view[...] = jnp.zeros_like(y_view)

    for s in range(TILES_PER_STEP):
        view = pl.ds(s * rows * ROW_SLABS, rows * ROW_SLABS)
        row_tile(step * TILES_PER_STEP + s, xs_ref.at[view], ys_ref.at[view])


def _experts(texp, nused, tend, xs2d, weg, weu, wed, n_tiles):
    block = (TILES_PER_STEP * EXPERT_ROWS * ROW_SLABS, 128)
    hbm = pl.BlockSpec(memory_space=pl.ANY)

    def block_idx(j, te, nu, tn):
        return (jnp.minimum(j, (nu[0] - 1) // TILES_PER_STEP), 0)

    grid_spec = pltpu.PrefetchScalarGridSpec(
        num_scalar_prefetch=3,
        grid=(n_tiles // TILES_PER_STEP,),
        in_specs=[pl.BlockSpec(block, block_idx), hbm, hbm, hbm],
        out_specs=pl.BlockSpec(block, block_idx),
        scratch_shapes=[pltpu.VMEM((D_MODEL, EXPERT_DIM), BF16),
                        pltpu.VMEM((D_MODEL, EXPERT_DIM), BF16),
                        pltpu.VMEM((EXPERT_DIM, D_MODEL), BF16),
                        pltpu.VMEM((2, D_MODEL, EXPERT_DIM), F32),
                        pltpu.VMEM((2, D_MODEL, EXPERT_DIM), F32),
                        pltpu.VMEM((2, EXPERT_DIM, D_MODEL), F32),
                        pltpu.SemaphoreType.DMA((2, 3)),
                        pltpu.SMEM((1,), jnp.int32)],
    )
    return pl.pallas_call(
        _experts_kernel,
        grid_spec=grid_spec,
        out_shape=jax.ShapeDtypeStruct(xs2d.shape, jnp.int32),
        compiler_params=pltpu.CompilerParams(dimension_semantics=("arbitrary",),
                                             vmem_limit_bytes=VMEM_LIMIT),
        name="experts",
    )(texp, nused, tend, xs2d, weg, weu, wed)


def _final_kernel(x_ref, y8_ref, wt_ref, mod_ref, g2_ref, wsg_ref, wsu_ref, wsd_ref, fng_ref, o_ref):
    tm = x_ref.shape[0]
    x = x_ref[...]
    hb = _rms_mod(x, g2_ref[...], mod_ref[0, 3:4, :], mod_ref[0, 4:5, :]).astype(BF16)
    shared = _dot((_silu(_dot(hb, wsg_ref[...])) * _dot(hb, wsu_ref[...])).astype(BF16), wsd_ref[...])
    wt = wt_ref[...]
    r_lo = jnp.zeros((tm, D_MODEL // 2), F32)
    r_hi = jnp.zeros((tm, D_MODEL // 2), F32)
    for k in range(TOP_K):
        lo, hi = _unpack_pair(_load_token_words(y8_ref, (k,), tm))
        wk = wt[:, k:k + 1]
        r_lo = r_lo + wk * lo
        r_hi = r_hi + wk * hi
    routed = jnp.concatenate([r_lo, r_hi], axis=1)
    y = x + mod_ref[0, 5:6, :] * (routed + shared)
    ms = jnp.mean(y * y, axis=-1, keepdims=True)
    o_ref[...] = y * lax.rsqrt(ms + EPS) * fng_ref[...]


def _final(x1, y8, wtok, mod3, norm2_g, wsg, wsu, wsd, final_g, seq_len, mod_row_of_batch):
    t = x1.shape[0]
    tm = TM_FINAL

    def mod_idx(i):
        return (mod_row_of_batch((i * tm) // seq_len), 0, 0)

    def full(a):
        return pl.BlockSpec(a.shape, lambda i: (0,) * a.ndim)

    return pl.pallas_call(
        _final_kernel,
        grid=(t // tm,),
        in_specs=[pl.BlockSpec((tm, D_MODEL), lambda i: (i, 0)),
                  pl.BlockSpec((TOP_K, tm * ROW_SLABS, 128), lambda i: (0, i, 0)),
                  pl.BlockSpec((tm, 128), lambda i: (i, 0)),
                  pl.BlockSpec((1, 6, D_MODEL), mod_idx),
                  full(norm2_g), full(wsg), full(wsu), full(wsd), full(final_g)],
        out_specs=pl.BlockSpec((tm, D_MODEL), lambda i: (i, 0)),
        out_shape=jax.ShapeDtypeStruct((t, D_MODEL), F32),
        compiler_params=pltpu.CompilerParams(dimension_semantics=("parallel",),
                                             vmem_limit_bytes=VMEM_LIMIT),
        name="final",
    )(x1, y8, wtok, mod3, norm2_g, wsg, wsu, wsd, final_g)


def _moe(x1, mod3, lw, seq_len, mod_row_of_batch):
    t = x1.shape[0]
    n_tiles = TOP_K * t // EXPERT_ROWS + N_EXPERTS
    n_tiles_pad = -(-n_tiles // 128) * 128
    hp2d, ek, rk, wtok, cnt = _router(x1, mod3, lw["norm2_g"], lw["w_router_t"], lw["router_bias"],
                                      seq_len, mod_row_of_batch)
    pos, texp, nused, tend = _plan(ek, rk, cnt, n_tiles_pad)
    pos3 = pos.reshape(TOP_K, t // SC_CHUNK, SC_CHUNK).transpose(1, 0, 2)
    xs = _sc_dispatch(hp2d.reshape(t, ROW_SLABS, 128), pos3, n_tiles * EXPERT_ROWS)
    ys2d = _experts(texp.reshape(-1), nused.reshape(-1), tend[:, 0], xs.reshape(-1, 128),
                    lw["weg"], lw["weu"], lw["wed"], n_tiles)
    y8 = _sc_gather(ys2d.reshape(-1, ROW_SLABS, 128), pos3, t)
    return _final(x1, y8.reshape(TOP_K, t * ROW_SLABS, 128), wtok, mod3, lw["norm2_g"],
                  lw["wsg"], lw["wsu"], lw["wsd"], lw["final_g"], seq_len, mod_row_of_batch)


def _dft_tables(seq_len):
    gd = FOURIER_GROUP_DIM
    kc = np.arange(gd)
    ang_c = ((kc[:, None] * kc[None, :]) % gd) * (2.0 * math.pi / gd)
    cs = np.concatenate([np.cos(ang_c), np.sin(ang_c)], axis=1) * (gd ** -0.5)
    kl = np.arange(seq_len)
    ang_l = ((kl[:, None] * kl[None, :]) % seq_len) * (2.0 * math.pi / seq_len)
    cls = np.concatenate([np.cos(ang_l), -np.sin(ang_l)], axis=1) * (seq_len ** -0.5)
    return jnp.asarray(cs.astype(np.float32), dtype=BF16), jnp.asarray(cls.astype(np.float32), dtype=BF16)


def _rope_tables(length):
    rows = length // GRID_W
    r = np.repeat(np.arange(rows, dtype=np.float32), GRID_W)
    col = np.tile(np.arange(GRID_W, dtype=np.float32), rows)
    nf = RET_HEAD_DIM // 4
    inv = (np.float32(ROPE_BASE) ** (-np.arange(nf, dtype=np.float32) / np.float32(nf))).astype(np.float32)
    ar = r[:, None] * inv[None]
    ac = col[:, None] * inv[None]
    ang = np.concatenate([ar, ar, ac, ac], axis=-1).astype(np.float64)
    sign = np.where((np.arange(RET_HEAD_DIM) & nf) == 0, -1.0, 1.0)
    return (jnp.asarray(np.cos(ang).astype(np.float32)),
            jnp.asarray((np.sin(ang) * sign[None, :]).astype(np.float32)))


def _trunk_path(x, mod3, mod_row_of_batch, s0f, s0b, rope, lw):
    batch, seq_len, _ = x.shape
    x2d = x.reshape(batch * seq_len, D_MODEL)
    uf, q, k, v, sg, gf, gr = _inproj(x2d, mod3, lw["norm1_g"], lw["w_in"], seq_len, mod_row_of_batch, rope)
    r, s_f, s_b = _retention(q, k, v, sg, lw["dec"], lw["gn_g"], s0f, s0b, batch, seq_len)
    cs, cls = _dft_tables(seq_len)
    fmix = _fnet(uf, cs, cls, batch, seq_len)
    x1 = _merge(fmix, r, gf, gr, x2d, mod3, lw["w_four"], lw["w_ret"], lw["w_o"], seq_len, mod_row_of_batch)
    y = _moe(x1, mod3, lw, seq_len, mod_row_of_batch)
    return y.reshape(batch, seq_len, D_MODEL), s_f, s_b


def kernel(x_prompt, x_sample, state_ret_fwd, state_ret_bwd, c, c_ctx, w_ada, b_ada, norm1_g, norm2_g, w_in,
           ret_decay_fwd, ret_decay_bwd, ret_gn_g, w_four_out, w_ret_out, w_out, w_router, router_bias,
           w_exp_gate, w_exp_up, w_exp_down, w_shared_gate, w_shared_up, w_shared_down, final_norm_g):
    depth = w_ada.shape[0]
    assert depth == 1, "final norm is fused into the last layer's MoE kernel"
    n_ctx, n_lat = x_prompt.shape[0], x_sample.shape[0]
    cond = jnp.concatenate([c_ctx[None, :], c], axis=0)
    cond = jnp.pad(cond, ((0, (-cond.shape[0]) % 8), (0, 0)))
    rope = _rope_tables(x_sample.shape[1])
    zeros = jnp.zeros((n_ctx, N_RET_HEADS, RET_HEAD_DIM, RET_HEAD_DIM), F32)

    layer = 0
    mod = _ada(cond, w_ada[layer], b_ada[layer][None, :])
    mod3 = mod.reshape(mod.shape[0], 6, D_MODEL)
    dec = jnp.stack([ret_decay_fwd[layer], ret_decay_bwd[layer]], axis=1)
    lw = {
        "norm1_g": norm1_g[layer][None, :],
        "norm2_g": norm2_g[layer][None, :],
        "w_in": w_in[layer].astype(BF16),
        "dec": jnp.broadcast_to(dec[:, :, None], (N_RET_HEADS, 2, RET_HEAD_DIM)).astype(F32),
        "gn_g": ret_gn_g[layer][None, :],
        "w_four": w_four_out[layer].astype(BF16),
        "w_ret": w_ret_out[layer].astype(BF16),
        "w_o": w_out[layer].astype(BF16),
        "w_router_t": w_router[layer].T,
        "router_bias": router_bias[layer][:, None],
        "weg": w_exp_gate[layer],
        "weu": w_exp_up[layer],
        "wed": w_exp_down[layer],
        "wsg": w_shared_gate[layer].astype(BF16),
        "wsu": w_shared_up[layer].astype(BF16),
        "wsd": w_shared_down[layer].astype(BF16),
        "final_g": final_norm_g[None, :],
    }
    y_prompt, s_f, s_b = _trunk_path(x_prompt, mod3, lambda b: 0, zeros, zeros, None, lw)
    y_sample, _, _ = _trunk_path(x_sample, mod3, lambda b: 1 + b, state_ret_fwd[:, layer],
                                 state_ret_bwd[:, layer], rope, lw)
    return (y_prompt, y_sample, s_f[:, None], s_b[:, None])
```

```python
import functools
import math

import jax
import jax.numpy as jnp
import numpy as np
from jax import lax
from jax.experimental import pallas as pl
from jax.experimental.pallas import tpu as pltpu
from jax.experimental.pallas import tpu_sc as plsc

F32 = jnp.float32
BF16 = jnp.bfloat16

D_MODEL = 1024
GRID_W = 64
N_FOURIER_GROUPS = 8
FOURIER_GROUP_DIM = 128
N_RET_HEADS = 4
RET_HEAD_DIM = 128
RET_WIDTH = N_RET_HEADS * RET_HEAD_DIM
CHUNK = 128
N_EXPERTS = 64
N_EXPERT_GROUPS = 8
EXPERTS_PER_GROUP = N_EXPERTS // N_EXPERT_GROUPS
TOPK_GROUPS = 4
TOP_K = 8
EXPERT_DIM = 256
ROUTED_SCALE = 2.5
ROPE_BASE = 10000.0
EPS = 1e-6
Q_SCALE = RET_HEAD_DIM ** -0.5

_C_UF = (0, 1024)
_C_Q = (1024, 1536)
_C_K = (1536, 2048)
_C_V = (2048, 2560)
_C_G = (2560, 3072)
_C_GF = (3072, 4096)
_C_GR = (4096, 5120)

VMEM_LIMIT = 56 * 1024 * 1024

TM_INPROJ = 1024
TM_PROJ = 512
FNET_ROWS = 256
TM_ROUTER = 1024
TM_FINAL = 512
EXPERT_ROWS = 512
TILES_PER_STEP = 2
ROW_SLABS = 4
SC_CORES = 2
SC_WORKERS = 32
SC_CHUNK = 128


def _silu(x):
    return x * jax.nn.sigmoid(x)


def _dot(a, b):
    return jnp.dot(a, b, preferred_element_type=F32)


def _rms_mod(x, g, shift, scale):
    ms = jnp.mean(x * x, axis=-1, keepdims=True)
    y = x * lax.rsqrt(ms + EPS) * g
    return y * (1.0 + scale) + shift


def _ada_kernel(cond_ref, w_ref, b_ref, o_ref):
    s = _silu(cond_ref[...]).astype(BF16)
    o_ref[...] = _dot(s, w_ref[...].astype(BF16)) + b_ref[...]


def _ada(cond, w_ada, b_ada):
    rows, n = cond.shape[0], w_ada.shape[1]
    tn = 1536
    return pl.pallas_call(
        _ada_kernel,
        grid=(n // tn,),
        in_specs=[pl.BlockSpec((rows, D_MODEL), lambda j: (0, 0)),
                  pl.BlockSpec((D_MODEL, tn), lambda j: (0, j)),
                  pl.BlockSpec((1, tn), lambda j: (0, j))],
        out_specs=pl.BlockSpec((rows, tn), lambda j: (0, j)),
        out_shape=jax.ShapeDtypeStruct((rows, n), F32),
        compiler_params=pltpu.CompilerParams(vmem_limit_bytes=VMEM_LIMIT),
        name="ada",
    )(cond, w_ada, b_ada)


def _rope_head(x, cos, sin_signed, first_half):
    partner = jnp.where(first_half, pltpu.roll(x, 96, 1), pltpu.roll(x, 32, 1))
    return x * cos + partner * sin_signed


def _inproj_kernel(*refs, use_rope):
    if use_rope:
        x_ref, mod_ref, g_ref, w_ref, cos_ref, sin_ref = refs[:6]
        outs = refs[6:]
    else:
        x_ref, mod_ref, g_ref, w_ref = refs[:4]
        outs = refs[4:]
    uf_o, q_o, k_o, v_o, sg_o, gf_o, gr_o = outs

    h = _rms_mod(x_ref[...], g_ref[...], mod_ref[0, 0:1, :], mod_ref[0, 1:2, :])
    hb = h.astype(BF16)

    def proj(cols):
        return _dot(hb, w_ref[:, cols[0]:cols[1]])

    uf_o[...] = proj(_C_UF).astype(BF16)
    q = proj(_C_Q)
    k = proj(_C_K)
    if use_rope:
        cos = cos_ref[...]
        sin_signed = sin_ref[...]
        lane = lax.broadcasted_iota(jnp.int32, cos.shape, 1)
        first_half = (lane & 32) == 0
        for hd in range(N_RET_HEADS):
            sl = slice(hd * RET_HEAD_DIM, (hd + 1) * RET_HEAD_DIM)
            q_o[:, sl] = (_rope_head(q[:, sl], cos, sin_signed, first_half) * Q_SCALE).astype(BF16)
            k_o[:, sl] = _rope_head(k[:, sl], cos, sin_signed, first_half).astype(BF16)
    else:
        q_o[...] = (q * Q_SCALE).astype(BF16)
        k_o[...] = k.astype(BF16)
    v_o[...] = proj(_C_V).astype(BF16)
    sg_o[...] = _silu(proj(_C_G)).astype(BF16)
    gf_o[...] = jax.nn.sigmoid(proj(_C_GF)).astype(BF16)
    gr_o[...] = jax.nn.sigmoid(proj(_C_GR)).astype(BF16)


def _inproj(x2d, mod3, norm_g, w_in_bf, seq_len, mod_row_of_batch, rope):
    t = x2d.shape[0]
    tm = TM_INPROJ
    tiles_per_seq = max(seq_len // tm, 1)

    def mod_idx(i):
        return (mod_row_of_batch((i * tm) // seq_len), 0, 0)

    in_specs = [pl.BlockSpec((tm, D_MODEL), lambda i: (i, 0)),
                pl.BlockSpec((1, 6, D_MODEL), mod_idx),
                pl.BlockSpec((1, D_MODEL), lambda i: (0, 0)),
                pl.BlockSpec(w_in_bf.shape, lambda i: (0, 0), pipeline_mode=pl.Buffered(1))]
    args = [x2d, mod3, norm_g, w_in_bf]
    if rope is not None:
        in_specs += [pl.BlockSpec((tm, RET_HEAD_DIM), lambda i: (i % tiles_per_seq, 0))] * 2
        args += list(rope)
    widths = [1024, RET_WIDTH, RET_WIDTH, RET_WIDTH, RET_WIDTH, 1024, 1024]
    return pl.pallas_call(
        functools.partial(_inproj_kernel, use_rope=rope is not None),
        grid=(t // tm,),
        in_specs=in_specs,
        out_specs=[pl.BlockSpec((tm, w), lambda i: (i, 0)) for w in widths],
        out_shape=[jax.ShapeDtypeStruct((t, w), BF16) for w in widths],
        compiler_params=pltpu.CompilerParams(dimension_semantics=("parallel",),
                                             vmem_limit_bytes=VMEM_LIMIT),
        name="inproj",
    )(*args)


def _retention_kernel(q_ref, k_ref, v_ref, sg_ref, dec_ref, gn_ref, s0f_ref, s0b_ref,
                      r_ref, sfo_ref, sbo_ref, tab_scr, gc_scr):
    n_chunks = q_ref.shape[0] // CHUNK
    hd = RET_HEAD_DIM

    @pl.when(pl.program_id(0) == 0)
    def _():
        row = lax.broadcasted_iota(jnp.int32, (CHUNK, CHUNK), 0).astype(F32)
        col = lax.broadcasted_iota(jnp.int32, (CHUNK, CHUNK), 1).astype(F32)
        diff = row - col
        for h in range(N_RET_HEADS):
            dec = dec_ref[h]
            lg = jnp.minimum(dec, 0.0) - jnp.log1p(jnp.exp(-jnp.abs(dec)))
            lgf = lg[0:1, :]
            lgb = lg[1:2, :]
            tab_scr[h, 0] = jnp.exp(jnp.where(diff >= 0, lgf * diff, lgb * (-diff)))
            tab_scr[h, 1] = jnp.exp(lgf * (row + 1.0))
            tab_scr[h, 2] = jnp.exp(lgb * (CHUNK - row))
            tab_scr[h, 3] = jnp.exp(lgf * (CHUNK - 1.0 - col))
            tab_scr[h, 4] = jnp.exp(lgb * col)
            gc_scr[h] = jnp.exp(lg * CHUNK)

    def rows(n):
        return slice(n * CHUNK, (n + 1) * CHUNK)

    for h in range(N_RET_HEADS):
        cols = slice(h * hd, (h + 1) * hd)
        decay, qw_f, qw_b, kwt_f, kwt_b = (tab_scr[h, i] for i in range(5))
        gc = gc_scr[h]
        gc_f = gc[0:1, :]
        gc_b = gc[1:2, :]

        kv_f, kv_b = [], []
        for n in range(n_chunks):
            kt = k_ref[rows(n), cols].astype(F32).T
            vn = v_ref[rows(n), cols]
            kv_f.append(_dot((kt * kwt_f).astype(BF16), vn))
            kv_b.append(_dot((kt * kwt_b).astype(BF16), vn))

        s = s0f_ref[h]
        prev_f = []
        for n in range(n_chunks):
            prev_f.append(s.astype(BF16))
            s = gc_f * s + kv_f[n]
        sfo_ref[h] = s
        s = s0b_ref[h]
        prev_b = [None] * n_chunks
        for n in reversed(range(n_chunks)):
            prev_b[n] = s.astype(BF16)
            s = gc_b * s + kv_b[n]
        sbo_ref[h] = s

        gn = gn_ref[:, cols]
        for n in range(n_chunks):
            qn = q_ref[rows(n), cols]
            qf = qn.astype(F32)
            scores = lax.dot_general(qn, k_ref[rows(n), cols], (((1,), (1,)), ((), ())),
                                     preferred_element_type=F32)
            o = _dot((scores * decay).astype(BF16), v_ref[rows(n), cols])
            o = o + _dot((qf * qw_f).astype(BF16), prev_f[n])
            o = o + _dot((qf * qw_b).astype(BF16), prev_b[n])
            mu = jnp.mean(o, axis=-1, keepdims=True)
            d = o - mu
            var = jnp.mean(d * d, axis=-1, keepdims=True)
            on = d * lax.rsqrt(var + EPS) * gn
            r_ref[rows(n), cols] = (on * sg_ref[rows(n), cols].astype(F32)).astype(BF16)


def _retention(q, k, v, sg, dec, gn_g, s0f, s0b, batch, seq_len):
    hd = RET_HEAD_DIM
    tok_spec = pl.BlockSpec((seq_len, RET_WIDTH), lambda b: (b, 0))
    st_spec = pl.BlockSpec((None, N_RET_HEADS, hd, hd), lambda b: (b, 0, 0, 0))
    st_shape = jax.ShapeDtypeStruct((batch, N_RET_HEADS, hd, hd), F32)
    return pl.pallas_call(
        _retention_kernel,
        grid=(batch,),
        in_specs=[tok_spec, tok_spec, tok_spec, tok_spec,
                  pl.BlockSpec(dec.shape, lambda b: (0, 0, 0)),
                  pl.BlockSpec(gn_g.shape, lambda b: (0, 0)),
                  st_spec, st_spec],
        out_specs=[tok_spec, st_spec, st_spec],
        out_shape=[jax.ShapeDtypeStruct((batch * seq_len, RET_WIDTH), BF16), st_shape, st_shape],
        scratch_shapes=[pltpu.VMEM((N_RET_HEADS, 5, CHUNK, CHUNK), F32),
                        pltpu.VMEM((N_RET_HEADS, 2, hd), F32)],
        compiler_params=pltpu.CompilerParams(dimension_semantics=("arbitrary",),
                                             vmem_limit_bytes=VMEM_LIMIT),
        name="retention",
    )(q, k, v, sg, dec, gn_g, s0f, s0b)


def _fnet_kernel(uf_ref, cs_ref, cls_ref, o_ref, xcs_ref):
    seq_len = uf_ref.shape[0]
    gd = FOURIER_GROUP_DIM

    @pl.when(pl.program_id(1) == 0)
    def _():
        for g in range(N_FOURIER_GROUPS):
            x = _dot(uf_ref[:, g * gd:(g + 1) * gd], cs_ref[...])
            xcs_ref[0:seq_len, g * gd:(g + 1) * gd] = x[:, :gd].astype(BF16)
            xcs_ref[seq_len:2 * seq_len, g * gd:(g + 1) * gd] = x[:, gd:].astype(BF16)

    o_ref[...] = _dot(cls_ref[...], xcs_ref[...]).astype(BF16)


def _fnet(uf, cs, cls, batch, seq_len):
    rb = FNET_ROWS
    nr = seq_len // rb
    return pl.pallas_call(
        _fnet_kernel,
        grid=(batch, nr),
        in_specs=[pl.BlockSpec((seq_len, D_MODEL), lambda b, r: (b, 0)),
                  pl.BlockSpec(cs.shape, lambda b, r: (0, 0)),
                  pl.BlockSpec((rb, 2 * seq_len), lambda b, r: (r, 0))],
        out_specs=pl.BlockSpec((rb, D_MODEL), lambda b, r: (b * nr + r, 0)),
        out_shape=jax.ShapeDtypeStruct((batch * seq_len, D_MODEL), BF16),
        scratch_shapes=[pltpu.VMEM((2 * seq_len, D_MODEL), BF16)],
        compiler_params=pltpu.CompilerParams(dimension_semantics=("parallel", "arbitrary"),
                                             vmem_limit_bytes=VMEM_LIMIT),
        name="fnet",
    )(uf, cs, cls)


def _merge_kernel(fm_ref, r_ref, gf_ref, gr_ref, x_ref, mod_ref, wf_ref, wr_ref, wo_ref, o_ref):
    f_out = _dot(fm_ref[...], wf_ref[...])
    r_out = _dot(r_ref[...], wr_ref[...])
    merged = gf_ref[...].astype(F32) * f_out + gr_ref[...].astype(F32) * r_out
    mix = _dot(merged.astype(BF16), wo_ref[...])
    o_ref[...] = x_ref[...] + mod_ref[0, 2:3, :] * mix


def _merge(fmix, r, gf, gr, x2d, mod3, w_four, w_ret, w_o, seq_len, mod_row_of_batch):
    t = x2d.shape[0]
    tm = TM_PROJ

    def mod_idx(i):
        return (mod_row_of_batch((i * tm) // seq_len), 0, 0)

    def tok(w):
        return pl.BlockSpec((tm, w), lambda i: (i, 0))

    def full(a):
        return pl.BlockSpec(a.shape, lambda i: (0, 0))

    return pl.pallas_call(
        _merge_kernel,
        grid=(t // tm,),
        in_specs=[tok(D_MODEL), tok(RET_WIDTH), tok(D_MODEL), tok(D_MODEL), tok(D_MODEL),
                  pl.BlockSpec((1, 6, D_MODEL), mod_idx), full(w_four), full(w_ret), full(w_o)],
        out_specs=tok(D_MODEL),
        out_shape=jax.ShapeDtypeStruct((t, D_MODEL), F32),
        compiler_params=pltpu.CompilerParams(dimension_semantics=("parallel",),
                                             vmem_limit_bytes=VMEM_LIMIT),
        name="merge",
    )(fmix, r, gf, gr, x2d, mod3, w_four, w_ret, w_o)


def _pack_pair(lo_f32, hi_f32):
    lo = lax.bitcast_convert_type(lo_f32.astype(BF16).astype(F32), jnp.uint32)
    hi = lax.bitcast_convert_type(hi_f32.astype(BF16).astype(F32), jnp.uint32)
    return lax.bitcast_convert_type((lo >> 16) | hi, jnp.int32)


def _unpack_pair(words_i32):
    w = lax.bitcast_convert_type(words_i32, jnp.uint32)
    lo = lax.bitcast_convert_type(w << 16, F32)
    hi = lax.bitcast_convert_type(w & jnp.uint32(0xFFFF0000), F32)
    return lo, hi


def _load_token_words(ref, lead, n_tok):
    parts = []
    for s in range(ROW_SLABS):
        idx = (pl.ds(s, n_tok, stride=ROW_SLABS), slice(None))
        parts.append(ref[lead + idx] if lead else ref[idx])
    return jnp.concatenate(parts, axis=1)


def _store_token_words(ref, words, n_tok):
    for s in range(ROW_SLABS):
        ref[pl.ds(s, n_tok, stride=ROW_SLABS), :] = words[:, s * 128:(s + 1) * 128]


def _route(scores, biased):
    tokens = scores.shape[1]
    neg = -jnp.inf
    epg = EXPERTS_PER_GROUP
    iota_g = lax.broadcasted_iota(jnp.int32, (epg, tokens), 0).astype(F32)

    def pick_first_max(cur, iota, size):
        m = jnp.max(cur, axis=0, keepdims=True)
        idx = jnp.min(jnp.where(cur == m, iota, float(size)), axis=0, keepdims=True)
        return m, idx, iota == idx

    group_scores = []
    for g in range(N_EXPERT_GROUPS):
        vals = biased[g * epg:(g + 1) * epg, :]
        m1, _, hit = pick_first_max(vals, iota_g, epg)
        m2 = jnp.max(jnp.where(hit, neg, vals), axis=0, keepdims=True)
        group_scores.append(m1 + m2)
    cur = jnp.concatenate(group_scores, axis=0)
    group_sel = jnp.zeros_like(cur)
    for _ in range(TOPK_GROUPS):
        _, _, hit = pick_first_max(cur, iota_g, N_EXPERT_GROUPS)
        group_sel = jnp.where(hit, 1.0, group_sel)
        cur = jnp.where(hit, neg, cur)
    masked = jnp.concatenate(
        [jnp.where(group_sel[g:g + 1, :] > 0.0, biased[g * epg:(g + 1) * epg, :], neg)
         for g in range(N_EXPERT_GROUPS)], axis=0)
    iota_e = lax.broadcasted_iota(jnp.int32, masked.shape, 0).astype(F32)
    sel = jnp.zeros_like(masked)
    cur = masked
    picks = []
    for _ in range(TOP_K):
        _, idx, hit = pick_first_max(cur, iota_e, N_EXPERTS)
        picks.append(idx)
        sel = jnp.where(hit, 1.0, sel)
        cur = jnp.where(hit, neg, cur)
    w = scores * sel
    return w / jnp.sum(w, axis=0, keepdims=True) * ROUTED_SCALE, sel, picks


def _router_kernel(x_ref, mod_ref, g2_ref, wrt_ref, rb_ref, hp_ref, ek_ref, rk_ref, wt_ref, cnt_ref, run_scr):
    tm = x_ref.shape[0]

    @pl.when(pl.program_id(0) == 0)
    def _():
        run_scr[...] = jnp.zeros_like(run_scr)

    h = _rms_mod(x_ref[...], g2_ref[...], mod_ref[0, 3:4, :], mod_ref[0, 4:5, :])
    half = D_MODEL // 2
    _store_token_words(hp_ref, _pack_pair(h[:, :half], h[:, half:]), tm)

    logits_t = lax.dot_general(wrt_ref[...], h, (((1,), (1,)), ((), ())),
                               precision=lax.Precision.HIGHEST, preferred_element_type=F32)
    scores = jax.nn.sigmoid(logits_t)
    comb_t, sel, picks = _route(scores, scores + rb_ref[...])

    earlier = (lax.broadcasted_iota(jnp.int32, (tm, tm), 0) < lax.broadcasted_iota(jnp.int32, (tm, tm), 1))
    rank_t = _dot(sel.astype(BF16), jnp.where(earlier, 1.0, 0.0).astype(BF16)) + run_scr[...]
    run_scr[...] += jnp.sum(sel, axis=1, keepdims=True)
    cnt_ref[...] = jnp.broadcast_to(run_scr[...], cnt_ref.shape)

    iota_e = lax.broadcasted_iota(jnp.int32, sel.shape, 0).astype(F32)
    ranks, weights = [], []
    for idx in picks:
        hit = iota_e == idx
        ranks.append(jnp.sum(jnp.where(hit, rank_t, 0.0), axis=0, keepdims=True))
        weights.append(jnp.sum(jnp.where(hit, comb_t, 0.0), axis=0, keepdims=True))
    ek_ref[...] = jnp.concatenate(picks, axis=0).astype(jnp.int32)
    rk_ref[...] = jnp.concatenate(ranks, axis=0).astype(jnp.int32)
    w_pad = jnp.concatenate(weights + [jnp.zeros((128 - TOP_K, tm), F32)], axis=0)
    wt_ref[...] = w_pad.T


def _router(x1, mod3, norm2_g, w_router_t, router_bias, seq_len, mod_row_of_batch):
    t = x1.shape[0]
    tm = TM_ROUTER

    def mod_idx(i):
        return (mod_row_of_batch((i * tm) // seq_len), 0, 0)

    def full(a):
        return pl.BlockSpec(a.shape, lambda i: (0,) * a.ndim)

    return pl.pallas_call(
        _router_kernel,
        grid=(t // tm,),
        in_specs=[pl.BlockSpec((tm, D_MODEL), lambda i: (i, 0)),
                  pl.BlockSpec((1, 6, D_MODEL), mod_idx),
                  full(norm2_g), full(w_router_t), full(router_bias)],
        out_specs=[pl.BlockSpec((tm * ROW_SLABS, 128), lambda i: (i, 0)),
                   pl.BlockSpec((TOP_K, tm), lambda i: (0, i)),
                   pl.BlockSpec((TOP_K, tm), lambda i: (0, i)),
                   pl.BlockSpec((tm, 128), lambda i: (i, 0)),
                   pl.BlockSpec((N_EXPERTS, 128), lambda i: (0, 0))],
        out_shape=[jax.ShapeDtypeStruct((t * ROW_SLABS, 128), jnp.int32),
                   jax.ShapeDtypeStruct((TOP_K, t), jnp.int32),
                   jax.ShapeDtypeStruct((TOP_K, t), jnp.int32),
                   jax.ShapeDtypeStruct((t, 128), F32),
                   jax.ShapeDtypeStruct((N_EXPERTS, 128), F32)],
        scratch_shapes=[pltpu.VMEM((N_EXPERTS, 1), F32)],
        compiler_params=pltpu.CompilerParams(dimension_semantics=("arbitrary",),
                                             vmem_limit_bytes=VMEM_LIMIT),
        name="router",
    )(x1, mod3, norm2_g, w_router_t, router_bias)


def _plan_kernel(ek_ref, rk_ref, cnt_ref, pos_ref, texp_ref, nused_ref, tend_ref):
    rows = float(EXPERT_ROWS)
    cnt = cnt_ref[:, 0:1]
    tiles = jnp.floor((cnt + (rows - 1.0)) / rows)
    before = (lax.broadcasted_iota(jnp.int32, (N_EXPERTS, N_EXPERTS), 1)
              < lax.broadcasted_iota(jnp.int32, (N_EXPERTS, N_EXPERTS), 0))
    tile_start = jnp.dot(jnp.where(before, 1.0, 0.0), jnp.broadcast_to(tiles, (N_EXPERTS, 128)),
                         precision=lax.Precision.HIGHEST, preferred_element_type=F32)[:, 0:1]
    tile_end = tile_start + tiles
    row_start = tile_start * rows

    ek = ek_ref[...]
    pos = rk_ref[...].astype(F32)
    tile_id = lax.broadcasted_iota(jnp.int32, texp_ref.shape, 1).astype(F32)
    texp = jnp.zeros(texp_ref.shape, F32)
    for e in range(N_EXPERTS):
        pos = pos + jnp.where(ek == e, row_start[e:e + 1, :], 0.0)
        texp = texp + jnp.where(tile_id >= tile_end[e:e + 1, :], 1.0, 0.0)
    pos_ref[...] = pos.astype(jnp.int32)
    texp_ref[...] = jnp.minimum(texp, N_EXPERTS - 1.0).astype(jnp.int32)
    nused_ref[...] = jnp.broadcast_to(tile_end[N_EXPERTS - 1:N_EXPERTS, :], nused_ref.shape).astype(jnp.int32)
    tend_ref[...] = jnp.broadcast_to(tile_end, tend_ref.shape).astype(jnp.int32)


def _plan(ek, rk, cnt, n_tiles_pad):
    t = ek.shape[1]

    def full(shape):
        return pl.BlockSpec(shape, lambda: (0,) * len(shape))

    return pl.pallas_call(
        _plan_kernel,
        in_specs=[full(ek.shape), full(rk.shape), full(cnt.shape)],
        out_specs=[full((TOP_K, t)), full((1, n_tiles_pad)), full((1, 128)), full((N_EXPERTS, 128))],
        out_shape=[jax.ShapeDtypeStruct((TOP_K, t), jnp.int32),
                   jax.ShapeDtypeStruct((1, n_tiles_pad), jnp.int32),
                   jax.ShapeDtypeStruct((1, 128), jnp.int32),
                   jax.ShapeDtypeStruct((N_EXPERTS, 128), jnp.int32)],
        compiler_params=pltpu.CompilerParams(vmem_limit_bytes=VMEM_LIMIT),
        name="plan",
    )(ek, rk, cnt)


def _sc_mesh():
    return plsc.VectorSubcoreMesh(core_axis_name="c", subcore_axis_name="s")


def _sc_dispatch(rows, pos3, n_out):
    t = rows.shape[0]
    ch = SC_CHUNK
    per_w = (t // ch) // SC_WORKERS

    @functools.partial(
        pl.kernel, out_type=jax.ShapeDtypeStruct((n_out,) + rows.shape[1:], jnp.int32), mesh=_sc_mesh(),
        scratch_types=[pltpu.VMEM((TOP_K, ch), jnp.int32), pltpu.VMEM((ch,) + rows.shape[1:], jnp.int32),
                       pltpu.SemaphoreType.DMA])
    def k(rows_hbm, pos_hbm, out_hbm, idx_v, rows_v, sem):
        wid = lax.axis_index("s") * SC_CORES + lax.axis_index("c")

        @pl.loop(0, per_w)
        def _(j):
            c = wid * per_w + j
            pltpu.sync_copy(pos_hbm.at[c], idx_v)
            pltpu.sync_copy(rows_hbm.at[pl.ds(c * ch, ch)], rows_v)
            copies = [pltpu.async_copy(rows_v, out_hbm.at[idx_v.at[kk]], sem) for kk in range(TOP_K)]
            for cp in copies:
                cp.wait()

    return k(rows, pos3)


def _sc_gather(table, pos3, t):
    ch = SC_CHUNK
    per_w = (t // ch) // SC_WORKERS

    @functools.partial(
        pl.kernel, out_type=jax.ShapeDtypeStruct((TOP_K, t) + table.shape[1:], jnp.int32), mesh=_sc_mesh(),
        scratch_types=[pltpu.VMEM((TOP_K, ch), jnp.int32), pltpu.VMEM((ch,) + table.shape[1:], jnp.int32),
                       pltpu.SemaphoreType.DMA])
    def k(tab_hbm, pos_hbm, out_hbm, idx_v, rows_v, sem):
        wid = lax.axis_index("s") * SC_CORES + lax.axis_index("c")

        @pl.loop(0, per_w)
        def _(j):
            c = wid * per_w + j
            pltpu.sync_copy(pos_hbm.at[c], idx_v)
            for kk in range(TOP_K):
                pltpu.async_copy(tab_hbm.at[idx_v.at[kk]], rows_v, sem).wait()
                pltpu.sync_copy(rows_v, out_hbm.at[kk, pl.ds(c * ch, ch)])

    return k(table, pos3)


def _experts_kernel(texp_ref, nused_ref, tend_ref, xs_ref, weg_hbm, weu_hbm, wed_hbm, ys_ref,
                    wg_scr, wu_scr, wd_scr, wg_buf, wu_buf, wd_buf, sem, group_scr):
    step = pl.program_id(0)
    rows = EXPERT_ROWS
    half = D_MODEL // 2
    n_used = nused_ref[0]

    def weight_copies(e, slot):
        return [pltpu.make_async_copy(weg_hbm.at[e], wg_buf.at[slot], sem.at[slot, 0]),
                pltpu.make_async_copy(weu_hbm.at[e], wu_buf.at[slot], sem.at[slot, 1]),
                pltpu.make_async_copy(wed_hbm.at[e], wd_buf.at[slot], sem.at[slot, 2])]

    @pl.when(step == 0)
    def _():
        group_scr[0] = 0
        for cp in weight_copies(texp_ref[0], 0):
            cp.start()

    def row_tile(tile, x_view, y_view):
        expert = texp_ref[tile]
        used = tile < n_used
        new_expert = (tile == 0) | (expert != texp_ref[jnp.maximum(tile - 1, 0)])

        @pl.when(used & new_expert)
        def _():
            group = group_scr[0]
            slot = group % 2
            next_tile = tend_ref[expert]

            @pl.when(next_tile < n_used)
            def _():
                for cp in weight_copies(texp_ref[next_tile], 1 - slot):
                    cp.start()

            for cp in weight_copies(expert, slot):
                cp.wait()
            wg_scr[...] = wg_buf[slot].astype(BF16)
            wu_scr[...] = wu_buf[slot].astype(BF16)
            wd_scr[...] = wd_buf[slot].astype(BF16)
            group_scr[0] = group + 1

        @pl.when(used)
        def _():
            lo, hi = _unpack_pair(_load_token_words(x_view, (), rows))
            lo = lo.astype(BF16)
            hi = hi.astype(BF16)
            g = _dot(lo, wg_scr[0:half, :]) + _dot(hi, wg_scr[half:D_MODEL, :])
            u = _dot(lo, wu_scr[0:half, :]) + _dot(hi, wu_scr[half:D_MODEL, :])
            y = _dot((_silu(g) * u).astype(BF16), wd_scr[...])
            _store_token_words(y_view, _pack_pair(y[:, :half], y[:, half:]), rows)

        @pl.when(jnp.logical_not(used) & (step == (n_used - 1) // TILES_PER_STEP))
        def _():
            y_view[...] = jnp.zeros_like(y_view)

    for s in range(TILES_PER_STEP):
        view = pl.ds(s * rows * ROW_SLABS, rows * ROW_SLABS)
        row_tile(step * TILES_PER_STEP + s, xs_ref.at[view], ys_ref.at[view])


def _experts(texp, nused, tend, xs2d, weg, weu, wed, n_tiles):
    block = (TILES_PER_STEP * EXPERT_ROWS * ROW_SLABS, 128)
    hbm = pl.BlockSpec(memory_space=pl.ANY)

    def block_idx(j, te, nu, tn):
        return (jnp.minimum(j, (nu[0] - 1) // TILES_PER_STEP), 0)

    grid_spec = pltpu.PrefetchScalarGridSpec(
        num_scalar_prefetch=3,
        grid=(n_tiles // TILES_PER_STEP,),
        in_specs=[pl.BlockSpec(block, block_idx), hbm, hbm, hbm],
        out_specs=pl.BlockSpec(block, block_idx),
        scratch_shapes=[pltpu.VMEM((D_MODEL, EXPERT_DIM), BF16),
                        pltpu.VMEM((D_MODEL, EXPERT_DIM), BF16),
                        pltpu.VMEM((EXPERT_DIM, D_MODEL), BF16),
                        pltpu.VMEM((2, D_MODEL, EXPERT_DIM), F32),
                        pltpu.VMEM((2, D_MODEL, EXPERT_DIM), F32),
                        pltpu.VMEM((2, EXPERT_DIM, D_MODEL), F32),
                        pltpu.SemaphoreType.DMA((2, 3)),
                        pltpu.SMEM((1,), jnp.int32)],
    )
    return pl.pallas_call(
        _experts_kernel,
        grid_spec=grid_spec,
        out_shape=jax.ShapeDtypeStruct(xs2d.shape, jnp.int32),
        compiler_params=pltpu.CompilerParams(dimension_semantics=("arbitrary",),
                                             vmem_limit_bytes=VMEM_LIMIT),
        name="experts",
    )(texp, nused, tend, xs2d, weg, weu, wed)


def _final_kernel(x_ref, y8_ref, wt_ref, mod_ref, g2_ref, wsg_ref, wsu_ref, wsd_ref, fng_ref, o_ref):
    tm = x_ref.shape[0]
    x = x_ref[...]
    hb = _rms_mod(x, g2_ref[...], mod_ref[0, 3:4, :], mod_ref[0, 4:5, :]).astype(BF16)
    shared = _dot((_silu(_dot(hb, wsg_ref[...])) * _dot(hb, wsu_ref[...])).astype(BF16), wsd_ref[...])
    wt = wt_ref[...]
    r_lo = jnp.zeros((tm, D_MODEL // 2), F32)
    r_hi = jnp.zeros((tm, D_MODEL // 2), F32)
    for k in range(TOP_K):
        lo, hi = _unpack_pair(_load_token_words(y8_ref, (k,), tm))
        wk = wt[:, k:k + 1]
        r_lo = r_lo + wk * lo
        r_hi = r_hi + wk * hi
    routed = jnp.concatenate([r_lo, r_hi], axis=1)
    y = x + mod_ref[0, 5:6, :] * (routed + shared)
    ms = jnp.mean(y * y, axis=-1, keepdims=True)
    o_ref[...] = y * lax.rsqrt(ms + EPS) * fng_ref[...]


def _final(x1, y8, wtok, mod3, norm2_g, wsg, wsu, wsd, final_g, seq_len, mod_row_of_batch):
    t = x1.shape[0]
    tm = TM_FINAL

    def mod_idx(i):
        return (mod_row_of_batch((i * tm) // seq_len), 0, 0)

    def full(a):
        return pl.BlockSpec(a.shape, lambda i: (0,) * a.ndim)

    return pl.pallas_call(
        _final_kernel,
        grid=(t // tm,),
        in_specs=[pl.BlockSpec((tm, D_MODEL), lambda i: (i, 0)),
                  pl.BlockSpec((TOP_K, tm * ROW_SLABS, 128), lambda i: (0, i, 0)),
                  pl.BlockSpec((tm, 128), lambda i: (i, 0)),
                  pl.BlockSpec((1, 6, D_MODEL), mod_idx),
                  full(norm2_g), full(wsg), full(wsu), full(wsd), full(final_g)],
        out_specs=pl.BlockSpec((tm, D_MODEL), lambda i: (i, 0)),
        out_shape=jax.ShapeDtypeStruct((t, D_MODEL), F32),
        compiler_params=pltpu.CompilerParams(dimension_semantics=("parallel",),
                                             vmem_limit_bytes=VMEM_LIMIT),
        name="final",
    )(x1, y8, wtok, mod3, norm2_g, wsg, wsu, wsd, final_g)


def _moe(x1, mod3, lw, seq_len, mod_row_of_batch):
    t = x1.shape[0]
    n_tiles = TOP_K * t // EXPERT_ROWS + N_EXPERTS
    n_tiles_pad = -(-n_tiles // 128) * 128
    hp2d, ek, rk, wtok, cnt = _router(x1, mod3, lw["norm2_g"], lw["w_router_t"], lw["router_bias"],
                                      seq_len, mod_row_of_batch)
    pos, texp, nused, tend = _plan(ek, rk, cnt, n_tiles_pad)
    pos3 = pos.reshape(TOP_K, t // SC_CHUNK, SC_CHUNK).transpose(1, 0, 2)
    xs = _sc_dispatch(hp2d.reshape(t, ROW_SLABS, 128), pos3, n_tiles * EXPERT_ROWS)
    ys2d = _experts(texp.reshape(-1), nused.reshape(-1), tend[:, 0], xs.reshape(-1, 128),
                    lw["weg"], lw["weu"], lw["wed"], n_tiles)
    y8 = _sc_gather(ys2d.reshape(-1, ROW_SLABS, 128), pos3, t)
    return _final(x1, y8.reshape(TOP_K, t * ROW_SLABS, 128), wtok, mod3, lw["norm2_g"],
                  lw["wsg"], lw["wsu"], lw["wsd"], lw["final_g"], seq_len, mod_row_of_batch)


def _dft_tables(seq_len):
    gd = FOURIER_GROUP_DIM
    kc = np.arange(gd)
    ang_c = ((kc[:, None] * kc[None, :]) % gd) * (2.0 * math.pi / gd)
    cs = np.concatenate([np.cos(ang_c), np.sin(ang_c)], axis=1) * (gd ** -0.5)
    kl = np.arange(seq_len)
    ang_l = ((kl[:, None] * kl[None, :]) % seq_len) * (2.0 * math.pi / seq_len)
    cls = np.concatenate([np.cos(ang_l), -np.sin(ang_l)], axis=1) * (seq_len ** -0.5)
    return jnp.asarray(cs.astype(np.float32), dtype=BF16), jnp.asarray(cls.astype(np.float32), dtype=BF16)


def _rope_tables(length):
    rows = length // GRID_W
    r = np.repeat(np.arange(rows, dtype=np.float32), GRID_W)
    col = np.tile(np.arange(GRID_W, dtype=np.float32), rows)
    nf = RET_HEAD_DIM // 4
    inv = (np.float32(ROPE_BASE) ** (-np.arange(nf, dtype=np.float32) / np.float32(nf))).astype(np.float32)
    ar = r[:, None] * inv[None]
    ac = col[:, None] * inv[None]
    ang = np.concatenate([ar, ar, ac, ac], axis=-1).astype(np.float64)
    sign = np.where((np.arange(RET_HEAD_DIM) & nf) == 0, -1.0, 1.0)
    return (jnp.asarray(np.cos(ang).astype(np.float32)),
            jnp.asarray((np.sin(ang) * sign[None, :]).astype(np.float32)))


def _trunk_path(x, mod3, mod_row_of_batch, s0f, s0b, rope, lw):
    batch, seq_len, _ = x.shape
    x2d = x.reshape(batch * seq_len, D_MODEL)
    uf, q, k, v, sg, gf, gr = _inproj(x2d, mod3, lw["norm1_g"], lw["w_in"], seq_len, mod_row_of_batch, rope)
    r, s_f, s_b = _retention(q, k, v, sg, lw["dec"], lw["gn_g"], s0f, s0b, batch, seq_len)
    cs, cls = _dft_tables(seq_len)
    fmix = _fnet(uf, cs, cls, batch, seq_len)
    x1 = _merge(fmix, r, gf, gr, x2d, mod3, lw["w_four"], lw["w_ret"], lw["w_o"], seq_len, mod_row_of_batch)
    y = _moe(x1, mod3, lw, seq_len, mod_row_of_batch)
    return y.reshape(batch, seq_len, D_MODEL), s_f, s_b


def kernel(x_prompt, x_sample, state_ret_fwd, state_ret_bwd, c, c_ctx, w_ada, b_ada, norm1_g, norm2_g, w_in,
           ret_decay_fwd, ret_decay_bwd, ret_gn_g, w_four_out, w_ret_out, w_out, w_router, router_bias,
           w_exp_gate, w_exp_up, w_exp_down, w_shared_gate, w_shared_up, w_shared_down, final_norm_g):
    depth = w_ada.shape[0]
    assert depth == 1, "final norm is fused into the last layer's MoE kernel"
    n_ctx, n_lat = x_prompt.shape[0], x_sample.shape[0]
    cond = jnp.concatenate([c_ctx[None, :], c], axis=0)
    cond = jnp.pad(cond, ((0, (-cond.shape[0]) % 8), (0, 0)))
    rope = _rope_tables(x_sample.shape[1])
    zeros = jnp.zeros((n_ctx, N_RET_HEADS, RET_HEAD_DIM, RET_HEAD_DIM), F32)

    layer = 0
    mod = _ada(cond, w_ada[layer], b_ada[layer][None, :])
    mod3 = mod.reshape(mod.shape[0], 6, D_MODEL)
    dec = jnp.stack([ret_decay_fwd[layer], ret_decay_bwd[layer]], axis=1)
    lw = {
        "norm1_g": norm1_g[layer][None, :],
        "norm2_g": norm2_g[layer][None, :],
        "w_in": w_in[layer].astype(BF16),
        "dec": jnp.broadcast_to(dec[:, :, None], (N_RET_HEADS, 2, RET_HEAD_DIM)).astype(F32),
        "gn_g": ret_gn_g[layer][None, :],
        "w_four": w_four_out[layer].astype(BF16),
        "w_ret": w_ret_out[layer].astype(BF16),
        "w_o": w_out[layer].astype(BF16),
        "w_router_t": w_router[layer].T,
        "router_bias": router_bias[layer][:, None],
        "weg": w_exp_gate[layer],
        "weu": w_exp_up[layer],
        "wed": w_exp_down[layer],
        "wsg": w_shared_gate[layer].astype(BF16),
        "wsu": w_shared_up[layer].astype(BF16),
        "wsd": w_shared_down[layer].astype(BF16),
        "final_g": final_norm_g[None, :],
    }
    y_prompt, s_f, s_b = _trunk_path(x_prompt, mod3, lambda b: 0, zeros, zeros, None, lw)
    y_sample, _, _ = _trunk_path(x_sample, mod3, lambda b: 1 + b, state_ret_fwd[:, layer],
                                 state_ret_bwd[:, layer], rope, lw)
    return (y_prompt, y_sample, s_f[:, None], s_b[:, None])
```

```python
import functools
import math

import jax
import jax.numpy as jnp
import numpy as np
from jax import lax
from jax.experimental import pallas as pl
from jax.experimental.pallas import tpu as pltpu
from jax.experimental.pallas import tpu_sc as plsc

F32 = jnp.float32
BF16 = jnp.bfloat16

D_MODEL = 1024
GRID_W = 64
N_FOURIER_GROUPS = 8
FOURIER_GROUP_DIM = 128
N_RET_HEADS = 4
RET_HEAD_DIM = 128
RET_WIDTH = N_RET_HEADS * RET_HEAD_DIM
CHUNK = 128
N_EXPERTS = 64
N_EXPERT_GROUPS = 8
EXPERTS_PER_GROUP = N_EXPERTS // N_EXPERT_GROUPS
TOPK_GROUPS = 4
TOP_K = 8
EXPERT_DIM = 256
ROUTED_SCALE = 2.5
ROPE_BASE = 10000.0
EPS = 1e-6
Q_SCALE = RET_HEAD_DIM ** -0.5

_C_UF = (0, 1024)
_C_Q = (1024, 1536)
_C_K = (1536, 2048)
_C_V = (2048, 2560)
_C_G = (2560, 3072)
_C_GF = (3072, 4096)
_C_GR = (4096, 5120)

VMEM_LIMIT = 56 * 1024 * 1024

TM_INPROJ = 1024
TM_PROJ = 512
FNET_ROWS = 256
TM_ROUTER = 1024
TM_FINAL = 512
EXPERT_ROWS = 512
TILES_PER_STEP = 2
ROW_SLABS = 4
SC_CORES = 2
SC_WORKERS = 32
SC_CHUNK = 128
SC_LANES = 16
SC_COMBINE_TOKENS = 16


def _silu(x):
    return x * jax.nn.sigmoid(x)


def _dot(a, b):
    return jnp.dot(a, b, preferred_element_type=F32)


def _rms_mod(x, g, shift, scale):
    ms = jnp.mean(x * x, axis=-1, keepdims=True)
    y = x * lax.rsqrt(ms + EPS) * g
    return y * (1.0 + scale) + shift


def _ada_kernel(cond_ref, w_ref, b_ref, o_ref):
    s = _silu(cond_ref[...]).astype(BF16)
    o_ref[...] = _dot(s, w_ref[...].astype(BF16)) + b_ref[...]


def _ada(cond, w_ada, b_ada):
    rows, n = cond.shape[0], w_ada.shape[1]
    tn = 1536
    return pl.pallas_call(
        _ada_kernel,
        grid=(n // tn,),
        in_specs=[pl.BlockSpec((rows, D_MODEL), lambda j: (0, 0)),
                  pl.BlockSpec((D_MODEL, tn), lambda j: (0, j)),
                  pl.BlockSpec((1, tn), lambda j: (0, j))],
        out_specs=pl.BlockSpec((rows, tn), lambda j: (0, j)),
        out_shape=jax.ShapeDtypeStruct((rows, n), F32),
        compiler_params=pltpu.CompilerParams(vmem_limit_bytes=VMEM_LIMIT),
        name="ada",
    )(cond, w_ada, b_ada)


def _rope_head(x, cos, sin_signed, first_half):
    partner = jnp.where(first_half, pltpu.roll(x, 96, 1), pltpu.roll(x, 32, 1))
    return x * cos + partner * sin_signed


def _inproj_kernel(*refs, use_rope):
    if use_rope:
        x_ref, mod_ref, g_ref, w_ref, cos_ref, sin_ref = refs[:6]
        outs = refs[6:]
    else:
        x_ref, mod_ref, g_ref, w_ref = refs[:4]
        outs = refs[4:]
    uf_o, q_o, k_o, v_o, sg_o, gf_o, gr_o = outs

    h = _rms_mod(x_ref[...], g_ref[...], mod_ref[0, 0:1, :], mod_ref[0, 1:2, :])
    hb = h.astype(BF16)

    def proj(cols):
        return _dot(hb, w_ref[:, cols[0]:cols[1]])

    uf_o[...] = proj(_C_UF).astype(BF16)
    q = proj(_C_Q)
    k = proj(_C_K)
    if use_rope:
        cos = cos_ref[...]
        sin_signed = sin_ref[...]
        lane = lax.broadcasted_iota(jnp.int32, cos.shape, 1)
        first_half = (lane & 32) == 0
        for hd in range(N_RET_HEADS):
            sl = slice(hd * RET_HEAD_DIM, (hd + 1) * RET_HEAD_DIM)
            q_o[:, sl] = (_rope_head(q[:, sl], cos, sin_signed, first_half) * Q_SCALE).astype(BF16)
            k_o[:, sl] = _rope_head(k[:, sl], cos, sin_signed, first_half).astype(BF16)
    else:
        q_o[...] = (q * Q_SCALE).astype(BF16)
        k_o[...] = k.astype(BF16)
    v_o[...] = proj(_C_V).astype(BF16)
    sg_o[...] = _silu(proj(_C_G)).astype(BF16)
    gf_o[...] = jax.nn.sigmoid(proj(_C_GF)).astype(BF16)
    gr_o[...] = jax.nn.sigmoid(proj(_C_GR)).astype(BF16)


def _inproj(x2d, mod3, norm_g, w_in_bf, seq_len, mod_row_of_batch, rope):
    t = x2d.shape[0]
    tm = TM_INPROJ
    tiles_per_seq = max(seq_len // tm, 1)

    def mod_idx(i):
        return (mod_row_of_batch((i * tm) // seq_len), 0, 0)

    in_specs = [pl.BlockSpec((tm, D_MODEL), lambda i: (i, 0)),
                pl.BlockSpec((1, 6, D_MODEL), mod_idx),
                pl.BlockSpec((1, D_MODEL), lambda i: (0, 0)),
                pl.BlockSpec(w_in_bf.shape, lambda i: (0, 0), pipeline_mode=pl.Buffered(1))]
    args = [x2d, mod3, norm_g, w_in_bf]
    if rope is not None:
        in_specs += [pl.BlockSpec((tm, RET_HEAD_DIM), lambda i: (i % tiles_per_seq, 0))] * 2
        args += list(rope)
    widths = [1024, RET_WIDTH, RET_WIDTH, RET_WIDTH, RET_WIDTH, 1024, 1024]
    return pl.pallas_call(
        functools.partial(_inproj_kernel, use_rope=rope is not None),
        grid=(t // tm,),
        in_specs=in_specs,
        out_specs=[pl.BlockSpec((tm, w), lambda i: (i, 0)) for w in widths],
        out_shape=[jax.ShapeDtypeStruct((t, w), BF16) for w in widths],
        compiler_params=pltpu.CompilerParams(dimension_semantics=("parallel",),
                                             vmem_limit_bytes=VMEM_LIMIT),
        name="inproj",
    )(*args)


def _retention_kernel(q_ref, k_ref, v_ref, sg_ref, dec_ref, gn_ref, s0f_ref, s0b_ref,
                      r_ref, sfo_ref, sbo_ref, tab_scr, gc_scr):
    n_chunks = q_ref.shape[0] // CHUNK
    hd = RET_HEAD_DIM

    @pl.when(pl.program_id(0) == 0)
    def _():
        row = lax.broadcasted_iota(jnp.int32, (CHUNK, CHUNK), 0).astype(F32)
        col = lax.broadcasted_iota(jnp.int32, (CHUNK, CHUNK), 1).astype(F32)
        diff = row - col
        for h in range(N_RET_HEADS):
            dec = dec_ref[h]
            lg = jnp.minimum(dec, 0.0) - jnp.log1p(jnp.exp(-jnp.abs(dec)))
            lgf = lg[0:1, :]
            lgb = lg[1:2, :]
            tab_scr[h, 0] = jnp.exp(jnp.where(diff >= 0, lgf * diff, lgb * (-diff)))
            tab_scr[h, 1] = jnp.exp(lgf * (row + 1.0))
            tab_scr[h, 2] = jnp.exp(lgb * (CHUNK - row))
            tab_scr[h, 3] = jnp.exp(lgf * (CHUNK - 1.0 - col))
            tab_scr[h, 4] = jnp.exp(lgb * col)
            gc_scr[h] = jnp.exp(lg * CHUNK)

    def rows(n):
        return slice(n * CHUNK, (n + 1) * CHUNK)

    for h in range(N_RET_HEADS):
        cols = slice(h * hd, (h + 1) * hd)
        decay, qw_f, qw_b, kwt_f, kwt_b = (tab_scr[h, i] for i in range(5))
        gc = gc_scr[h]
        gc_f = gc[0:1, :]
        gc_b = gc[1:2, :]

        kv_f, kv_b = [], []
        for n in range(n_chunks):
            kt = k_ref[rows(n), cols].astype(F32).T
            vn = v_ref[rows(n), cols]
            kv_f.append(_dot((kt * kwt_f).astype(BF16), vn))
            kv_b.append(_dot((kt * kwt_b).astype(BF16), vn))

        s = s0f_ref[h]
        prev_f = []
        for n in range(n_chunks):
            prev_f.append(s.astype(BF16))
            s = gc_f * s + kv_f[n]
        sfo_ref[h] = s
        s = s0b_ref[h]
        prev_b = [None] * n_chunks
        for n in reversed(range(n_chunks)):
            prev_b[n] = s.astype(BF16)
            s = gc_b * s + kv_b[n]
        sbo_ref[h] = s

        gn = gn_ref[:, cols]
        for n in range(n_chunks):
            qn = q_ref[rows(n), cols]
            qf = qn.astype(F32)
            scores = lax.dot_general(qn, k_ref[rows(n), cols], (((1,), (1,)), ((), ())),
                                     preferred_element_type=F32)
            o = _dot((scores * decay).astype(BF16), v_ref[rows(n), cols])
            o = o + _dot((qf * qw_f).astype(BF16), prev_f[n])
            o = o + _dot((qf * qw_b).astype(BF16), prev_b[n])
            mu = jnp.mean(o, axis=-1, keepdims=True)
            d = o - mu
            var = jnp.mean(d * d, axis=-1, keepdims=True)
            on = d * lax.rsqrt(var + EPS) * gn
            r_ref[rows(n), cols] = (on * sg_ref[rows(n), cols].astype(F32)).astype(BF16)


def _retention(q, k, v, sg, dec, gn_g, s0f, s0b, batch, seq_len):
    hd = RET_HEAD_DIM
    tok_spec = pl.BlockSpec((seq_len, RET_WIDTH), lambda b: (b, 0))
    st_spec = pl.BlockSpec((None, N_RET_HEADS, hd, hd), lambda b: (b, 0, 0, 0))
    st_shape = jax.ShapeDtypeStruct((batch, N_RET_HEADS, hd, hd), F32)
    return pl.pallas_call(
        _retention_kernel,
        grid=(batch,),
        in_specs=[tok_spec, tok_spec, tok_spec, tok_spec,
                  pl.BlockSpec(dec.shape, lambda b: (0, 0, 0)),
                  pl.BlockSpec(gn_g.shape, lambda b: (0, 0)),
                  st_spec, st_spec],
        out_specs=[tok_spec, st_spec, st_spec],
        out_shape=[jax.ShapeDtypeStruct((batch * seq_len, RET_WIDTH), BF16), st_shape, st_shape],
        scratch_shapes=[pltpu.VMEM((N_RET_HEADS, 5, CHUNK, CHUNK), F32),
                        pltpu.VMEM((N_RET_HEADS, 2, hd), F32)],
        compiler_params=pltpu.CompilerParams(dimension_semantics=("arbitrary",),
                                             vmem_limit_bytes=VMEM_LIMIT),
        name="retention",
    )(q, k, v, sg, dec, gn_g, s0f, s0b)


def _fnet_kernel(uf_ref, cs_ref, cls_ref, o_ref, xcs_ref):
    seq_len = uf_ref.shape[0]
    gd = FOURIER_GROUP_DIM

    @pl.when(pl.program_id(1) == 0)
    def _():
        for g in range(N_FOURIER_GROUPS):
            x = _dot(uf_ref[:, g * gd:(g + 1) * gd], cs_ref[...])
            xcs_ref[0:seq_len, g * gd:(g + 1) * gd] = x[:, :gd].astype(BF16)
            xcs_ref[seq_len:2 * seq_len, g * gd:(g + 1) * gd] = x[:, gd:].astype(BF16)

    o_ref[...] = _dot(cls_ref[...], xcs_ref[...]).astype(BF16)


def _fnet(uf, cs, cls, batch, seq_len):
    rb = FNET_ROWS
    nr = seq_len // rb
    return pl.pallas_call(
        _fnet_kernel,
        grid=(batch, nr),
        in_specs=[pl.BlockSpec((seq_len, D_MODEL), lambda b, r: (b, 0)),
                  pl.BlockSpec(cs.shape, lambda b, r: (0, 0)),
                  pl.BlockSpec((rb, 2 * seq_len), lambda b, r: (r, 0))],
        out_specs=pl.BlockSpec((rb, D_MODEL), lambda b, r: (b * nr + r, 0)),
        out_shape=jax.ShapeDtypeStruct((batch * seq_len, D_MODEL), BF16),
        scratch_shapes=[pltpu.VMEM((2 * seq_len, D_MODEL), BF16)],
        compiler_params=pltpu.CompilerParams(dimension_semantics=("parallel", "arbitrary"),
                                             vmem_limit_bytes=VMEM_LIMIT),
        name="fnet",
    )(uf, cs, cls)


def _merge_kernel(fm_ref, r_ref, gf_ref, gr_ref, x_ref, mod_ref, wf_ref, wr_ref, wo_ref, o_ref):
    f_out = _dot(fm_ref[...], wf_ref[...])
    r_out = _dot(r_ref[...], wr_ref[...])
    merged = gf_ref[...].astype(F32) * f_out + gr_ref[...].astype(F32) * r_out
    mix = _dot(merged.astype(BF16), wo_ref[...])
    o_ref[...] = x_ref[...] + mod_ref[0, 2:3, :] * mix


def _merge(fmix, r, gf, gr, x2d, mod3, w_four, w_ret, w_o, seq_len, mod_row_of_batch):
    t = x2d.shape[0]
    tm = TM_PROJ

    def mod_idx(i):
        return (mod_row_of_batch((i * tm) // seq_len), 0, 0)

    def tok(w):
        return pl.BlockSpec((tm, w), lambda i: (i, 0))

    def full(a):
        return pl.BlockSpec(a.shape, lambda i: (0, 0))

    return pl.pallas_call(
        _merge_kernel,
        grid=(t // tm,),
        in_specs=[tok(D_MODEL), tok(RET_WIDTH), tok(D_MODEL), tok(D_MODEL), tok(D_MODEL),
                  pl.BlockSpec((1, 6, D_MODEL), mod_idx), full(w_four), full(w_ret), full(w_o)],
        out_specs=tok(D_MODEL),
        out_shape=jax.ShapeDtypeStruct((t, D_MODEL), F32),
        compiler_params=pltpu.CompilerParams(dimension_semantics=("parallel",),
                                             vmem_limit_bytes=VMEM_LIMIT),
        name="merge",
    )(fmix, r, gf, gr, x2d, mod3, w_four, w_ret, w_o)


def _pack_pair(lo_f32, hi_f32):
    lo = lax.bitcast_convert_type(lo_f32.astype(BF16).astype(F32), jnp.uint32)
    hi = lax.bitcast_convert_type(hi_f32.astype(BF16).astype(F32), jnp.uint32)
    return lax.bitcast_convert_type((lo >> 16) | hi, jnp.int32)


def _unpack_pair(words_i32):
    w = lax.bitcast_convert_type(words_i32, jnp.uint32)
    lo = lax.bitcast_convert_type(w << 16, F32)
    hi = lax.bitcast_convert_type(w & jnp.uint32(0xFFFF0000), F32)
    return lo, hi


def _load_token_words(ref, lead, n_tok):
    parts = []
    for s in range(ROW_SLABS):
        idx = (pl.ds(s, n_tok, stride=ROW_SLABS), slice(None))
        parts.append(ref[lead + idx] if lead else ref[idx])
    return jnp.concatenate(parts, axis=1)


def _store_token_words(ref, words, n_tok):
    for s in range(ROW_SLABS):
        ref[pl.ds(s, n_tok, stride=ROW_SLABS), :] = words[:, s * 128:(s + 1) * 128]


def _route(scores, biased):
    tokens = scores.shape[1]
    neg = -jnp.inf
    epg = EXPERTS_PER_GROUP
    iota_g = lax.broadcasted_iota(jnp.int32, (epg, tokens), 0).astype(F32)

    def pick_first_max(cur, iota, size):
        m = jnp.max(cur, axis=0, keepdims=True)
        idx = jnp.min(jnp.where(cur == m, iota, float(size)), axis=0, keepdims=True)
        return m, idx, iota == idx

    group_scores = []
    for g in range(N_EXPERT_GROUPS):
        vals = biased[g * epg:(g + 1) * epg, :]
        m1, _, hit = pick_first_max(vals, iota_g, epg)
        m2 = jnp.max(jnp.where(hit, neg, vals), axis=0, keepdims=True)
        group_scores.append(m1 + m2)
    cur = jnp.concatenate(group_scores, axis=0)
    group_sel = jnp.zeros_like(cur)
    for _ in range(TOPK_GROUPS):
        _, _, hit = pick_first_max(cur, iota_g, N_EXPERT_GROUPS)
        group_sel = jnp.where(hit, 1.0, group_sel)
        cur = jnp.where(hit, neg, cur)
    masked = jnp.concatenate(
        [jnp.where(group_sel[g:g + 1, :] > 0.0, biased[g * epg:(g + 1) * epg, :], neg)
         for g in range(N_EXPERT_GROUPS)], axis=0)
    iota_e = lax.broadcasted_iota(jnp.int32, masked.shape, 0).astype(F32)
    sel = jnp.zeros_like(masked)
    cur = masked
    picks = []
    for _ in range(TOP_K):
        _, idx, hit = pick_first_max(cur, iota_e, N_EXPERTS)
        picks.append(idx)
        sel = jnp.where(hit, 1.0, sel)
        cur = jnp.where(hit, neg, cur)
    w = scores * sel
    return w / jnp.sum(w, axis=0, keepdims=True) * ROUTED_SCALE, sel, picks


def _router_kernel(x_ref, mod_ref, g2_ref, wrt_ref, rb_ref, hp_ref, ek_ref, rk_ref, wt_ref, cnt_ref, run_scr):
    tm = x_ref.shape[0]

    @pl.when(pl.program_id(0) == 0)
    def _():
        run_scr[...] = jnp.zeros_like(run_scr)

    h = _rms_mod(x_ref[...], g2_ref[...], mod_ref[0, 3:4, :], mod_ref[0, 4:5, :])
    half = D_MODEL // 2
    _store_token_words(hp_ref, _pack_pair(h[:, :half], h[:, half:]), tm)

    logits_t = lax.dot_general(wrt_ref[...], h, (((1,), (1,)), ((), ())),
                               precision=lax.Precision.HIGHEST, preferred_element_type=F32)
    scores = jax.nn.sigmoid(logits_t)
    comb_t, sel, picks = _route(scores, scores + rb_ref[...])

    earlier = (lax.broadcasted_iota(jnp.int32, (tm, tm), 0) < lax.broadcasted_iota(jnp.int32, (tm, tm), 1))
    rank_t = _dot(sel.astype(BF16), jnp.where(earlier, 1.0, 0.0).astype(BF16)) + run_scr[...]
    run_scr[...] += jnp.sum(sel, axis=1, keepdims=True)
    cnt_ref[...] = jnp.broadcast_to(run_scr[...], cnt_ref.shape)

    iota_e = lax.broadcasted_iota(jnp.int32, sel.shape, 0).astype(F32)
    ranks, weights = [], []
    for idx in picks:
        hit = iota_e == idx
        ranks.append(jnp.sum(jnp.where(hit, rank_t, 0.0), axis=0, keepdims=True))
        weights.append(jnp.sum(jnp.where(hit, comb_t, 0.0), axis=0, keepdims=True))
    ek_ref[...] = jnp.concatenate(picks, axis=0).astype(jnp.int32)
    rk_ref[...] = jnp.concatenate(ranks, axis=0).astype(jnp.int32)
    w_rep = jnp.concatenate([jnp.broadcast_to(w, (SC_LANES, tm)) for w in weights], axis=0)
    wt_ref[...] = w_rep.T


def _router(x1, mod3, norm2_g, w_router_t, router_bias, seq_len, mod_row_of_batch):
    t = x1.shape[0]
    tm = TM_ROUTER

    def mod_idx(i):
        return (mod_row_of_batch((i * tm) // seq_len), 0, 0)

    def full(a):
        return pl.BlockSpec(a.shape, lambda i: (0,) * a.ndim)

    return pl.pallas_call(
        _router_kernel,
        grid=(t // tm,),
        in_specs=[pl.BlockSpec((tm, D_MODEL), lambda i: (i, 0)),
                  pl.BlockSpec((1, 6, D_MODEL), mod_idx),
                  full(norm2_g), full(w_router_t), full(router_bias)],
        out_specs=[pl.BlockSpec((tm * ROW_SLABS, 128), lambda i: (i, 0)),
                   pl.BlockSpec((TOP_K, tm), lambda i: (0, i)),
                   pl.BlockSpec((TOP_K, tm), lambda i: (0, i)),
                   pl.BlockSpec((tm, 128), lambda i: (i, 0)),
                   pl.BlockSpec((N_EXPERTS, 128), lambda i: (0, 0))],
        out_shape=[jax.ShapeDtypeStruct((t * ROW_SLABS, 128), jnp.int32),
                   jax.ShapeDtypeStruct((TOP_K, t), jnp.int32),
                   jax.ShapeDtypeStruct((TOP_K, t), jnp.int32),
                   jax.ShapeDtypeStruct((t, 128), F32),
                   jax.ShapeDtypeStruct((N_EXPERTS, 128), F32)],
        scratch_shapes=[pltpu.VMEM((N_EXPERTS, 1), F32)],
        compiler_params=pltpu.CompilerParams(dimension_semantics=("arbitrary",),
                                             vmem_limit_bytes=VMEM_LIMIT),
        name="router",
    )(x1, mod3, norm2_g, w_router_t, router_bias)


def _plan_kernel(ek_ref, rk_ref, cnt_ref, pos_ref, texp_ref, nused_ref, tend_ref):
    rows = float(EXPERT_ROWS)
    cnt = cnt_ref[:, 0:1]
    tiles = jnp.floor((cnt + (rows - 1.0)) / rows)
    before = (lax.broadcasted_iota(jnp.int32, (N_EXPERTS, N_EXPERTS), 1)
              < lax.broadcasted_iota(jnp.int32, (N_EXPERTS, N_EXPERTS), 0))
    tile_start = jnp.dot(jnp.where(before, 1.0, 0.0), jnp.broadcast_to(tiles, (N_EXPERTS, 128)),
                         precision=lax.Precision.HIGHEST, preferred_element_type=F32)[:, 0:1]
    tile_end = tile_start + tiles
    row_start = tile_start * rows

    ek = ek_ref[...]
    pos = rk_ref[...].astype(F32)
    tile_id = lax.broadcasted_iota(jnp.int32, texp_ref.shape, 1).astype(F32)
    texp = jnp.zeros(texp_ref.shape, F32)
    for e in range(N_EXPERTS):
        pos = pos + jnp.where(ek == e, row_start[e:e + 1, :], 0.0)
        texp = texp + jnp.where(tile_id >= tile_end[e:e + 1, :], 1.0, 0.0)
    pos_ref[...] = pos.astype(jnp.int32)
    texp_ref[...] = jnp.minimum(texp, N_EXPERTS - 1.0).astype(jnp.int32)
    nused_ref[...] = jnp.broadcast_to(tile_end[N_EXPERTS - 1:N_EXPERTS, :], nused_ref.shape).astype(jnp.int32)
    tend_ref[...] = jnp.broadcast_to(tile_end, tend_ref.shape).astype(jnp.int32)


def _plan(ek, rk, cnt, n_tiles_pad):
    t = ek.shape[1]

    def full(shape):
        return pl.BlockSpec(shape, lambda: (0,) * len(shape))

    return pl.pallas_call(
        _plan_kernel,
        in_specs=[full(ek.shape), full(rk.shape), full(cnt.shape)],
        out_specs=[full((TOP_K, t)), full((1, n_tiles_pad)), full((1, 128)), full((N_EXPERTS, 128))],
        out_shape=[jax.ShapeDtypeStruct((TOP_K, t), jnp.int32),
                   jax.ShapeDtypeStruct((1, n_tiles_pad), jnp.int32),
                   jax.ShapeDtypeStruct((1, 128), jnp.int32),
                   jax.ShapeDtypeStruct((N_EXPERTS, 128), jnp.int32)],
        compiler_params=pltpu.CompilerParams(vmem_limit_bytes=VMEM_LIMIT),
        name="plan",
    )(ek, rk, cnt)


def _sc_mesh():
    return plsc.VectorSubcoreMesh(core_axis_name="c", subcore_axis_name="s")


def _sc_dispatch(rows, pos3, n_out):
    t = rows.shape[0]
    ch = SC_CHUNK
    per_w = (t // ch) // SC_WORKERS

    @functools.partial(
        pl.kernel, out_type=jax.ShapeDtypeStruct((n_out,) + rows.shape[1:], jnp.int32), mesh=_sc_mesh(),
        scratch_types=[pltpu.VMEM((TOP_K, ch), jnp.int32), pltpu.VMEM((ch,) + rows.shape[1:], jnp.int32),
                       pltpu.SemaphoreType.DMA])
    def k(rows_hbm, pos_hbm, out_hbm, idx_v, rows_v, sem):
        wid = lax.axis_index("s") * SC_CORES + lax.axis_index("c")

        @pl.loop(0, per_w)
        def _(j):
            c = wid * per_w + j
            pltpu.sync_copy(pos_hbm.at[c], idx_v)
            pltpu.sync_copy(rows_hbm.at[pl.ds(c * ch, ch)], rows_v)
            copies = [pltpu.async_copy(rows_v, out_hbm.at[idx_v.at[kk]], sem) for kk in range(TOP_K)]
            for cp in copies:
                cp.wait()

    return k(rows, pos3)


def _sc_combine(table, pos3, wtok, t):
    ch = SC_CHUNK
    sub = SC_COMBINE_TOKENS
    lanes = SC_LANES
    slabs = ROW_SLABS
    per_w = (t // ch) // SC_WORKERS

    @functools.partial(
        pl.kernel, out_type=jax.ShapeDtypeStruct((t, 2 * slabs, 128), F32), mesh=_sc_mesh(),
        scratch_types=[pltpu.VMEM((TOP_K, ch), jnp.int32),
                       pltpu.VMEM((TOP_K, sub, slabs, 128), jnp.int32),
                       pltpu.VMEM((sub, 128), F32),
                       pltpu.VMEM((sub, 2 * slabs, 128), F32),
                       pltpu.SemaphoreType.DMA],
        compiler_params=pltpu.CompilerParams(needs_layout_passes=False))
    def k(tab_hbm, pos_hbm, w_hbm, out_hbm, idx_v, rows_v, w_v, out_v, sem):
        wid = lax.axis_index("s") * SC_CORES + lax.axis_index("c")

        @pl.loop(0, per_w)
        def _(j):
            c = wid * per_w + j
            pltpu.sync_copy(pos_hbm.at[c], idx_v)

            @pl.loop(0, ch // sub)
            def _(s):
                tok0 = c * ch + s * sub
                copies = [pltpu.async_copy(tab_hbm.at[idx_v.at[kk, pl.ds(s * sub, sub)]], rows_v.at[kk], sem)
                          for kk in range(TOP_K)]
                pltpu.sync_copy(w_hbm.at[pl.ds(tok0, sub)], w_v)
                for cp in copies:
                    cp.wait()

                @pl.loop(0, sub)
                def _(tt):
                    wk = [w_v[tt, pl.ds(kk * lanes, lanes)] for kk in range(TOP_K)]
                    for sl in range(slabs):
                        @plsc.parallel_loop(0, 128, step=lanes)
                        def _(off):
                            acc_lo = jnp.zeros((lanes,), F32)
                            acc_hi = jnp.zeros((lanes,), F32)
                            for kk in range(TOP_K):
                                word = rows_v[kk, tt, sl, pl.ds(off, lanes)]
                                lo = plsc.bitcast(word << 16, F32)
                                hi = plsc.bitcast(word & jnp.int32(-65536), F32)
                                acc_lo = acc_lo + wk[kk] * lo
                                acc_hi = acc_hi + wk[kk] * hi
                            out_v[tt, sl, pl.ds(off, lanes)] = acc_lo
                            out_v[tt, slabs + sl, pl.ds(off, lanes)] = acc_hi

                pltpu.sync_copy(out_v, out_hbm.at[pl.ds(tok0, sub)])

    return k(table, pos3, wtok)


def _experts_kernel(texp_ref, nused_ref, tend_ref, xs_ref, weg_hbm, weu_hbm, wed_hbm, ys_ref,
                    wg_scr, wu_scr, wd_scr, wg_buf, wu_buf, wd_buf, sem, group_scr):
    step = pl.program_id(0)
    rows = EXPERT_ROWS
    half = D_MODEL // 2
    n_used = nused_ref[0]

    def weight_copies(e, slot):
        return [pltpu.make_async_copy(weg_hbm.at[e], wg_buf.at[slot], sem.at[slot, 0]),
                pltpu.make_async_copy(weu_hbm.at[e], wu_buf.at[slot], sem.at[slot, 1]),
                pltpu.make_async_copy(wed_hbm.at[e], wd_buf.at[slot], sem.at[slot, 2])]

    @pl.when(step == 0)
    def _():
        group_scr[0] = 0
        for cp in weight_copies(texp_ref[0], 0):
            cp.start()

    def row_tile(tile, x_view, y_view):
        expert = texp_ref[tile]
        used = tile < n_used
        new_expert = (tile == 0) | (expert != texp_ref[jnp.maximum(tile - 1, 0)])

        @pl.when(used & new_expert)
        def _():
            group = group_scr[0]
            slot = group % 2
            next_tile = tend_ref[expert]

            @pl.when(next_tile < n_used)
            def _():
                for cp in weight_copies(texp_ref[next_tile], 1 - slot):
                    cp.start()

            for cp in weight_copies(expert, slot):
                cp.wait()
            wg_scr[...] = wg_buf[slot].astype(BF16)
            wu_scr[...] = wu_buf[slot].astype(BF16)
            wd_scr[...] = wd_buf[slot].astype(BF16)
            group_scr[0] = group + 1

        @pl.when(used)
        def _():
            lo, hi = _unpack_pair(_load_token_words(x_view, (), rows))
            lo = lo.astype(BF16)
            hi = hi.astype(BF16)
            g = _dot(lo, wg_scr[0:half, :]) + _dot(hi, wg_scr[half:D_MODEL, :])
            u = _dot(lo, wu_scr[0:half, :]) + _dot(hi, wu_scr[half:D_MODEL, :])
            y = _dot((_silu(g) * u).astype(BF16), wd_scr[...])
            _store_token_words(y_view, _pack_pair(y[:, :half], y[:, half:]), rows)

        @pl.when(jnp.logical_not(used) & (step == (n_used - 1) // TILES_PER_STEP))
        def _():
            y_view[...] = jnp.zeros_like(y_view)

    for s in range(TILES_PER_STEP):
        view = pl.ds(s * rows * ROW_SLABS, rows * ROW_SLABS)
        row_tile(step * TILES_PER_STEP + s, xs_ref.at[view], ys_ref.at[view])


def _experts(texp, nused, tend, xs2d, weg, weu, wed, n_tiles):
    block = (TILES_PER_STEP * EXPERT_ROWS * ROW_SLABS, 128)
    hbm = pl.BlockSpec(memory_space=pl.ANY)

    def block_idx(j, te, nu, tn):
        return (jnp.minimum(j, (nu[0] - 1) // TILES_PER_STEP), 0)

    grid_spec = pltpu.PrefetchScalarGridSpec(
        num_scalar_prefetch=3,
        grid=(n_tiles // TILES_PER_STEP,),
        in_specs=[pl.BlockSpec(block, block_idx), hbm, hbm, hbm],
        out_specs=pl.BlockSpec(block, block_idx),
        scratch_shapes=[pltpu.VMEM((D_MODEL, EXPERT_DIM), BF16),
                        pltpu.VMEM((D_MODEL, EXPERT_DIM), BF16),
                        pltpu.VMEM((EXPERT_DIM, D_MODEL), BF16),
                        pltpu.VMEM((2, D_MODEL, EXPERT_DIM), F32),
                        pltpu.VMEM((2, D_MODEL, EXPERT_DIM), F32),
                        pltpu.VMEM((2, EXPERT_DIM, D_MODEL), F32),
                        pltpu.SemaphoreType.DMA((2, 3)),
                        pltpu.SMEM((1,), jnp.int32)],
    )
    return pl.pallas_call(
        _experts_kernel,
        grid_spec=grid_spec,
        out_shape=jax.ShapeDtypeStruct(xs2d.shape, jnp.int32),
        compiler_params=pltpu.CompilerParams(dimension_semantics=("arbitrary",),
                                             vmem_limit_bytes=VMEM_LIMIT),
        name="experts",
    )(texp, nused, tend, xs2d, weg, weu, wed)


def _final_kernel(x_ref, routed_ref, mod_ref, g2_ref, wsg_ref, wsu_ref, wsd_ref, fng_ref, o_ref):
    tm = x_ref.shape[0]
    x = x_ref[...]
    hb = _rms_mod(x, g2_ref[...], mod_ref[0, 3:4, :], mod_ref[0, 4:5, :]).astype(BF16)
    shared = _dot((_silu(_dot(hb, wsg_ref[...])) * _dot(hb, wsu_ref[...])).astype(BF16), wsd_ref[...])
    n_slabs = 2 * ROW_SLABS
    routed = jnp.concatenate([routed_ref[pl.ds(s, tm, stride=n_slabs), :] for s in range(n_slabs)], axis=1)
    y = x + mod_ref[0, 5:6, :] * (routed + shared)
    ms = jnp.mean(y * y, axis=-1, keepdims=True)
    o_ref[...] = y * lax.rsqrt(ms + EPS) * fng_ref[...]


def _final(x1, routed2d, mod3, norm2_g, wsg, wsu, wsd, final_g, seq_len, mod_row_of_batch):
    t = x1.shape[0]
    tm = TM_FINAL

    def mod_idx(i):
        return (mod_row_of_batch((i * tm) // seq_len), 0, 0)

    def full(a):
        return pl.BlockSpec(a.shape, lambda i: (0,) * a.ndim)

    return pl.pallas_call(
        _final_kernel,
        grid=(t // tm,),
        in_specs=[pl.BlockSpec((tm, D_MODEL), lambda i: (i, 0)),
                  pl.BlockSpec((tm * 2 * ROW_SLABS, 128), lambda i: (i, 0)),
                  pl.BlockSpec((1, 6, D_MODEL), mod_idx),
                  full(norm2_g), full(wsg), full(wsu), full(wsd), full(final_g)],
        out_specs=pl.BlockSpec((tm, D_MODEL), lambda i: (i, 0)),
        out_shape=jax.ShapeDtypeStruct((t, D_MODEL), F32),
        compiler_params=pltpu.CompilerParams(dimension_semantics=("parallel",),
                                             vmem_limit_bytes=VMEM_LIMIT),
        name="final",
    )(x1, routed2d, mod3, norm2_g, wsg, wsu, wsd, final_g)


def _moe(x1, mod3, lw, seq_len, mod_row_of_batch):
    t = x1.shape[0]
    n_tiles = TOP_K * t // EXPERT_ROWS + N_EXPERTS
    n_tiles_pad = -(-n_tiles // 128) * 128
    hp2d, ek, rk, wtok, cnt = _router(x1, mod3, lw["norm2_g"], lw["w_router_t"], lw["router_bias"],
                                      seq_len, mod_row_of_batch)
    pos, texp, nused, tend = _plan(ek, rk, cnt, n_tiles_pad)
    pos3 = pos.reshape(TOP_K, t // SC_CHUNK, SC_CHUNK).transpose(1, 0, 2)
    xs = _sc_dispatch(hp2d.reshape(t, ROW_SLABS, 128), pos3, n_tiles * EXPERT_ROWS)
    ys2d = _experts(texp.reshape(-1), nused.reshape(-1), tend[:, 0], xs.reshape(-1, 128),
                    lw["weg"], lw["weu"], lw["wed"], n_tiles)
    routed = _sc_combine(ys2d.reshape(-1, ROW_SLABS, 128), pos3, wtok, t)
    return _final(x1, routed.reshape(t * 2 * ROW_SLABS, 128), mod3, lw["norm2_g"],
                  lw["wsg"], lw["wsu"], lw["wsd"], lw["final_g"], seq_len, mod_row_of_batch)


def _dft_tables(seq_len):
    gd = FOURIER_GROUP_DIM
    kc = np.arange(gd)
    ang_c = ((kc[:, None] * kc[None, :]) % gd) * (2.0 * math.pi / gd)
    cs = np.concatenate([np.cos(ang_c), np.sin(ang_c)], axis=1) * (gd ** -0.5)
    kl = np.arange(seq_len)
    ang_l = ((kl[:, None] * kl[None, :]) % seq_len) * (2.0 * math.pi / seq_len)
    cls = np.concatenate([np.cos(ang_l), -np.sin(ang_l)], axis=1) * (seq_len ** -0.5)
    return jnp.asarray(cs.astype(np.float32), dtype=BF16), jnp.asarray(cls.astype(np.float32), dtype=BF16)


def _rope_tables(length):
    rows = length // GRID_W
    r = np.repeat(np.arange(rows, dtype=np.float32), GRID_W)
    col = np.tile(np.arange(GRID_W, dtype=np.float32), rows)
    nf = RET_HEAD_DIM // 4
    inv = (np.float32(ROPE_BASE) ** (-np.arange(nf, dtype=np.float32) / np.float32(nf))).astype(np.float32)
    ar = r[:, None] * inv[None]
    ac = col[:, None] * inv[None]
    ang = np.concatenate([ar, ar, ac, ac], axis=-1).astype(np.float64)
    sign = np.where((np.arange(RET_HEAD_DIM) & nf) == 0, -1.0, 1.0)
    return (jnp.asarray(np.cos(ang).astype(np.float32)),
            jnp.asarray((np.sin(ang) * sign[None, :]).astype(np.float32)))


def _trunk_path(x, mod3, mod_row_of_batch, s0f, s0b, rope, lw):
    batch, seq_len, _ = x.shape
    x2d = x.reshape(batch * seq_len, D_MODEL)
    uf, q, k, v, sg, gf, gr = _inproj(x2d, mod3, lw["norm1_g"], lw["w_in"], seq_len, mod_row_of_batch, rope)
    r, s_f, s_b = _retention(q, k, v, sg, lw["dec"], lw["gn_g"], s0f, s0b, batch, seq_len)
    cs, cls = _dft_tables(seq_len)
    fmix = _fnet(uf, cs, cls, batch, seq_len)
    x1 = _merge(fmix, r, gf, gr, x2d, mod3, lw["w_four"], lw["w_ret"], lw["w_o"], seq_len, mod_row_of_batch)
    y = _moe(x1, mod3, lw, seq_len, mod_row_of_batch)
    return y.reshape(batch, seq_len, D_MODEL), s_f, s_b


def kernel(x_prompt, x_sample, state_ret_fwd, state_ret_bwd, c, c_ctx, w_ada, b_ada, norm1_g, norm2_g, w_in,
           ret_decay_fwd, ret_decay_bwd, ret_gn_g, w_four_out, w_ret_out, w_out, w_router, router_bias,
           w_exp_gate, w_exp_up, w_exp_down, w_shared_gate, w_shared_up, w_shared_down, final_norm_g):
    depth = w_ada.shape[0]
    assert depth == 1, "final norm is fused into the last layer's MoE kernel"
    n_ctx, n_lat = x_prompt.shape[0], x_sample.shape[0]
    cond = jnp.concatenate([c_ctx[None, :], c], axis=0)
    cond = jnp.pad(cond, ((0, (-cond.shape[0]) % 8), (0, 0)))
    rope = _rope_tables(x_sample.shape[1])
    zeros = jnp.zeros((n_ctx, N_RET_HEADS, RET_HEAD_DIM, RET_HEAD_DIM), F32)

    layer = 0
    mod = _ada(cond, w_ada[layer], b_ada[layer][None, :])
    mod3 = mod.reshape(mod.shape[0], 6, D_MODEL)
    dec = jnp.stack([ret_decay_fwd[layer], ret_decay_bwd[layer]], axis=1)
    lw = {
        "norm1_g": norm1_g[layer][None, :],
        "norm2_g": norm2_g[layer][None, :],
        "w_in": w_in[layer].astype(BF16),
        "dec": jnp.broadcast_to(dec[:, :, None], (N_RET_HEADS, 2, RET_HEAD_DIM)).astype(F32),
        "gn_g": ret_gn_g[layer][None, :],
        "w_four": w_four_out[layer].astype(BF16),
        "w_ret": w_ret_out[layer].astype(BF16),
        "w_o": w_out[layer].astype(BF16),
        "w_router_t": w_router[layer].T,
        "router_bias": router_bias[layer][:, None],
        "weg": w_exp_gate[layer],
        "weu": w_exp_up[layer],
        "wed": w_exp_down[layer],
        "wsg": w_shared_gate[layer].astype(BF16),
        "wsu": w_shared_up[layer].astype(BF16),
        "wsd": w_shared_down[layer].astype(BF16),
        "final_g": final_norm_g[None, :],
    }
    y_prompt, s_f, s_b = _trunk_path(x_prompt, mod3, lambda b: 0, zeros, zeros, None, lw)
    y_sample, _, _ = _trunk_path(x_sample, mod3, lambda b: 1 + b, state_ret_fwd[:, layer],
                                 state_ret_bwd[:, layer], rope, lw)
    return (y_prompt, y_sample, s_f[:, None], s_b[:, None])
```

```python
import functools
import math

import jax
import jax.numpy as jnp
import numpy as np
from jax import lax
from jax.experimental import pallas as pl
from jax.experimental.pallas import tpu as pltpu
from jax.experimental.pallas import tpu_sc as plsc

F32 = jnp.float32
BF16 = jnp.bfloat16

D_MODEL = 1024
GRID_W = 64
N_FOURIER_GROUPS = 8
FOURIER_GROUP_DIM = 128
N_RET_HEADS = 4
RET_HEAD_DIM = 128
RET_WIDTH = N_RET_HEADS * RET_HEAD_DIM
CHUNK = 128
N_EXPERTS = 64
N_EXPERT_GROUPS = 8
EXPERTS_PER_GROUP = N_EXPERTS // N_EXPERT_GROUPS
TOPK_GROUPS = 4
TOP_K = 8
EXPERT_DIM = 256
ROUTED_SCALE = 2.5
ROPE_BASE = 10000.0
EPS = 1e-6
Q_SCALE = RET_HEAD_DIM ** -0.5

_C_UF = (0, 1024)
_C_Q = (1024, 1536)
_C_K = (1536, 2048)
_C_V = (2048, 2560)
_C_G = (2560, 3072)
_C_GF = (3072, 4096)
_C_GR = (4096, 5120)

VMEM_LIMIT = 56 * 1024 * 1024

TM_INPROJ = 1024
TM_PROJ = 512
FNET_ROWS = 256
TM_ROUTER = 1024
TM_FINAL = 512
EXPERT_ROWS = 512
TILES_PER_STEP = 2
ROW_SLABS = 4
SC_CORES = 2
SC_WORKERS = 32
SC_CHUNK = 128
SC_LANES = 16
SC_COMBINE_TOKENS = 8


def _silu(x):
    return x * jax.nn.sigmoid(x)


def _dot(a, b):
    return jnp.dot(a, b, preferred_element_type=F32)


def _rms_mod(x, g, shift, scale):
    ms = jnp.mean(x * x, axis=-1, keepdims=True)
    y = x * lax.rsqrt(ms + EPS) * g
    return y * (1.0 + scale) + shift


def _ada_kernel(cond_ref, w_ref, b_ref, o_ref):
    s = _silu(cond_ref[...]).astype(BF16)
    o_ref[...] = _dot(s, w_ref[...].astype(BF16)) + b_ref[...]


def _ada(cond, w_ada, b_ada):
    rows, n = cond.shape[0], w_ada.shape[1]
    tn = 1536
    return pl.pallas_call(
        _ada_kernel,
        grid=(n // tn,),
        in_specs=[pl.BlockSpec((rows, D_MODEL), lambda j: (0, 0)),
                  pl.BlockSpec((D_MODEL, tn), lambda j: (0, j)),
                  pl.BlockSpec((1, tn), lambda j: (0, j))],
        out_specs=pl.BlockSpec((rows, tn), lambda j: (0, j)),
        out_shape=jax.ShapeDtypeStruct((rows, n), F32),
        compiler_params=pltpu.CompilerParams(vmem_limit_bytes=VMEM_LIMIT),
        name="ada",
    )(cond, w_ada, b_ada)


def _rope_head(x, cos, sin_signed, first_half):
    partner = jnp.where(first_half, pltpu.roll(x, 96, 1), pltpu.roll(x, 32, 1))
    return x * cos + partner * sin_signed


def _inproj_kernel(*refs, use_rope):
    if use_rope:
        x_ref, mod_ref, g_ref, w_ref, cos_ref, sin_ref = refs[:6]
        outs = refs[6:]
    else:
        x_ref, mod_ref, g_ref, w_ref = refs[:4]
        outs = refs[4:]
    uf_o, q_o, k_o, v_o, sg_o, gf_o, gr_o = outs

    h = _rms_mod(x_ref[...], g_ref[...], mod_ref[0, 0:1, :], mod_ref[0, 1:2, :])
    hb = h.astype(BF16)

    def proj(cols):
        return _dot(hb, w_ref[:, cols[0]:cols[1]])

    uf_o[...] = proj(_C_UF).astype(BF16)
    q = proj(_C_Q)
    k = proj(_C_K)
    if use_rope:
        cos = cos_ref[...]
        sin_signed = sin_ref[...]
        lane = lax.broadcasted_iota(jnp.int32, cos.shape, 1)
        first_half = (lane & 32) == 0
        for hd in range(N_RET_HEADS):
            sl = slice(hd * RET_HEAD_DIM, (hd + 1) * RET_HEAD_DIM)
            q_o[:, sl] = (_rope_head(q[:, sl], cos, sin_signed, first_half) * Q_SCALE).astype(BF16)
            k_o[:, sl] = _rope_head(k[:, sl], cos, sin_signed, first_half).astype(BF16)
    else:
        q_o[...] = (q * Q_SCALE).astype(BF16)
        k_o[...] = k.astype(BF16)
    v_o[...] = proj(_C_V).astype(BF16)
    sg_o[...] = _silu(proj(_C_G)).astype(BF16)
    gf_o[...] = jax.nn.sigmoid(proj(_C_GF)).astype(BF16)
    gr_o[...] = jax.nn.sigmoid(proj(_C_GR)).astype(BF16)


def _inproj(x2d, mod3, norm_g, w_in_bf, seq_len, mod_row_of_batch, rope):
    t = x2d.shape[0]
    tm = TM_INPROJ
    tiles_per_seq = max(seq_len // tm, 1)

    def mod_idx(i):
        return (mod_row_of_batch((i * tm) // seq_len), 0, 0)

    in_specs = [pl.BlockSpec((tm, D_MODEL), lambda i: (i, 0)),
                pl.BlockSpec((1, 6, D_MODEL), mod_idx),
                pl.BlockSpec((1, D_MODEL), lambda i: (0, 0)),
                pl.BlockSpec(w_in_bf.shape, lambda i: (0, 0), pipeline_mode=pl.Buffered(1))]
    args = [x2d, mod3, norm_g, w_in_bf]
    if rope is not None:
        in_specs += [pl.BlockSpec((tm, RET_HEAD_DIM), lambda i: (i % tiles_per_seq, 0))] * 2
        args += list(rope)
    widths = [1024, RET_WIDTH, RET_WIDTH, RET_WIDTH, RET_WIDTH, 1024, 1024]
    return pl.pallas_call(
        functools.partial(_inproj_kernel, use_rope=rope is not None),
        grid=(t // tm,),
        in_specs=in_specs,
        out_specs=[pl.BlockSpec((tm, w), lambda i: (i, 0)) for w in widths],
        out_shape=[jax.ShapeDtypeStruct((t, w), BF16) for w in widths],
        compiler_params=pltpu.CompilerParams(dimension_semantics=("parallel",),
                                             vmem_limit_bytes=VMEM_LIMIT),
        name="inproj",
    )(*args)


def _retention_kernel(q_ref, k_ref, v_ref, sg_ref, dec_ref, gn_ref, s0f_ref, s0b_ref,
                      r_ref, sfo_ref, sbo_ref, tab_scr, gc_scr):
    n_chunks = q_ref.shape[0] // CHUNK
    hd = RET_HEAD_DIM

    @pl.when(pl.program_id(0) == 0)
    def _():
        row = lax.broadcasted_iota(jnp.int32, (CHUNK, CHUNK), 0).astype(F32)
        col = lax.broadcasted_iota(jnp.int32, (CHUNK, CHUNK), 1).astype(F32)
        diff = row - col
        for h in range(N_RET_HEADS):
            dec = dec_ref[h]
            lg = jnp.minimum(dec, 0.0) - jnp.log1p(jnp.exp(-jnp.abs(dec)))
            lgf = lg[0:1, :]
            lgb = lg[1:2, :]
            tab_scr[h, 0] = jnp.exp(jnp.where(diff >= 0, lgf * diff, lgb * (-diff)))
            tab_scr[h, 1] = jnp.exp(lgf * (row + 1.0))
            tab_scr[h, 2] = jnp.exp(lgb * (CHUNK - row))
            tab_scr[h, 3] = jnp.exp(lgf * (CHUNK - 1.0 - col))
            tab_scr[h, 4] = jnp.exp(lgb * col)
            gc_scr[h] = jnp.exp(lg * CHUNK)

    def rows(n):
        return slice(n * CHUNK, (n + 1) * CHUNK)

    for h in range(N_RET_HEADS):
        cols = slice(h * hd, (h + 1) * hd)
        decay, qw_f, qw_b, kwt_f, kwt_b = (tab_scr[h, i] for i in range(5))
        gc = gc_scr[h]
        gc_f = gc[0:1, :]
        gc_b = gc[1:2, :]

        kv_f, kv_b = [], []
        for n in range(n_chunks):
            kt = k_ref[rows(n), cols].astype(F32).T
            vn = v_ref[rows(n), cols]
            kv_f.append(_dot((kt * kwt_f).astype(BF16), vn))
            kv_b.append(_dot((kt * kwt_b).astype(BF16), vn))

        s = s0f_ref[h]
        prev_f = []
        for n in range(n_chunks):
            prev_f.append(s.astype(BF16))
            s = gc_f * s + kv_f[n]
        sfo_ref[h] = s
        s = s0b_ref[h]
        prev_b = [None] * n_chunks
        for n in reversed(range(n_chunks)):
            prev_b[n] = s.astype(BF16)
            s = gc_b * s + kv_b[n]
        sbo_ref[h] = s

        gn = gn_ref[:, cols]
        for n in range(n_chunks):
            qn = q_ref[rows(n), cols]
            qf = qn.astype(F32)
            scores = lax.dot_general(qn, k_ref[rows(n), cols], (((1,), (1,)), ((), ())),
                                     preferred_element_type=F32)
            o = _dot((scores * decay).astype(BF16), v_ref[rows(n), cols])
            o = o + _dot((qf * qw_f).astype(BF16), prev_f[n])
            o = o + _dot((qf * qw_b).astype(BF16), prev_b[n])
            mu = jnp.mean(o, axis=-1, keepdims=True)
            d = o - mu
            var = jnp.mean(d * d, axis=-1, keepdims=True)
            on = d * lax.rsqrt(var + EPS) * gn
            r_ref[rows(n), cols] = (on * sg_ref[rows(n), cols].astype(F32)).astype(BF16)


def _retention(q, k, v, sg, dec, gn_g, s0f, s0b, batch, seq_len):
    hd = RET_HEAD_DIM
    tok_spec = pl.BlockSpec((seq_len, RET_WIDTH), lambda b: (b, 0))
    st_spec = pl.BlockSpec((None, N_RET_HEADS, hd, hd), lambda b: (b, 0, 0, 0))
    st_shape = jax.ShapeDtypeStruct((batch, N_RET_HEADS, hd, hd), F32)
    return pl.pallas_call(
        _retention_kernel,
        grid=(batch,),
        in_specs=[tok_spec, tok_spec, tok_spec, tok_spec,
                  pl.BlockSpec(dec.shape, lambda b: (0, 0, 0)),
                  pl.BlockSpec(gn_g.shape, lambda b: (0, 0)),
                  st_spec, st_spec],
        out_specs=[tok_spec, st_spec, st_spec],
        out_shape=[jax.ShapeDtypeStruct((batch * seq_len, RET_WIDTH), BF16), st_shape, st_shape],
        scratch_shapes=[pltpu.VMEM((N_RET_HEADS, 5, CHUNK, CHUNK), F32),
                        pltpu.VMEM((N_RET_HEADS, 2, hd), F32)],
        compiler_params=pltpu.CompilerParams(dimension_semantics=("arbitrary",),
                                             vmem_limit_bytes=VMEM_LIMIT),
        name="retention",
    )(q, k, v, sg, dec, gn_g, s0f, s0b)


def _fnet_kernel(uf_ref, cs_ref, cls_ref, o_ref, xcs_ref):
    seq_len = uf_ref.shape[0]
    gd = FOURIER_GROUP_DIM

    @pl.when(pl.program_id(1) == 0)
    def _():
        for g in range(N_FOURIER_GROUPS):
            x = _dot(uf_ref[:, g * gd:(g + 1) * gd], cs_ref[...])
            xcs_ref[0:seq_len, g * gd:(g + 1) * gd] = x[:, :gd].astype(BF16)
            xcs_ref[seq_len:2 * seq_len, g * gd:(g + 1) * gd] = x[:, gd:].astype(BF16)

    o_ref[...] = _dot(cls_ref[...], xcs_ref[...]).astype(BF16)


def _fnet(uf, cs, cls, batch, seq_len):
    rb = FNET_ROWS
    nr = seq_len // rb
    return pl.pallas_call(
        _fnet_kernel,
        grid=(batch, nr),
        in_specs=[pl.BlockSpec((seq_len, D_MODEL), lambda b, r: (b, 0)),
                  pl.BlockSpec(cs.shape, lambda b, r: (0, 0)),
                  pl.BlockSpec((rb, 2 * seq_len), lambda b, r: (r, 0))],
        out_specs=pl.BlockSpec((rb, D_MODEL), lambda b, r: (b * nr + r, 0)),
        out_shape=jax.ShapeDtypeStruct((batch * seq_len, D_MODEL), BF16),
        scratch_shapes=[pltpu.VMEM((2 * seq_len, D_MODEL), BF16)],
        compiler_params=pltpu.CompilerParams(dimension_semantics=("parallel", "arbitrary"),
                                             vmem_limit_bytes=VMEM_LIMIT),
        name="fnet",
    )(uf, cs, cls)


def _merge_kernel(fm_ref, r_ref, gf_ref, gr_ref, x_ref, mod_ref, wf_ref, wr_ref, wo_ref, o_ref):
    f_out = _dot(fm_ref[...], wf_ref[...])
    r_out = _dot(r_ref[...], wr_ref[...])
    merged = gf_ref[...].astype(F32) * f_out + gr_ref[...].astype(F32) * r_out
    mix = _dot(merged.astype(BF16), wo_ref[...])
    o_ref[...] = x_ref[...] + mod_ref[0, 2:3, :] * mix


def _merge(fmix, r, gf, gr, x2d, mod3, w_four, w_ret, w_o, seq_len, mod_row_of_batch):
    t = x2d.shape[0]
    tm = TM_PROJ

    def mod_idx(i):
        return (mod_row_of_batch((i * tm) // seq_len), 0, 0)

    def tok(w):
        return pl.BlockSpec((tm, w), lambda i: (i, 0))

    def full(a):
        return pl.BlockSpec(a.shape, lambda i: (0, 0))

    return pl.pallas_call(
        _merge_kernel,
        grid=(t // tm,),
        in_specs=[tok(D_MODEL), tok(RET_WIDTH), tok(D_MODEL), tok(D_MODEL), tok(D_MODEL),
                  pl.BlockSpec((1, 6, D_MODEL), mod_idx), full(w_four), full(w_ret), full(w_o)],
        out_specs=tok(D_MODEL),
        out_shape=jax.ShapeDtypeStruct((t, D_MODEL), F32),
        compiler_params=pltpu.CompilerParams(dimension_semantics=("parallel",),
                                             vmem_limit_bytes=VMEM_LIMIT),
        name="merge",
    )(fmix, r, gf, gr, x2d, mod3, w_four, w_ret, w_o)


def _pack_pair(lo_f32, hi_f32):
    lo = lax.bitcast_convert_type(lo_f32.astype(BF16).astype(F32), jnp.uint32)
    hi = lax.bitcast_convert_type(hi_f32.astype(BF16).astype(F32), jnp.uint32)
    return lax.bitcast_convert_type((lo >> 16) | hi, jnp.int32)


def _unpack_pair(words_i32):
    w = lax.bitcast_convert_type(words_i32, jnp.uint32)
    lo = lax.bitcast_convert_type(w << 16, F32)
    hi = lax.bitcast_convert_type(w & jnp.uint32(0xFFFF0000), F32)
    return lo, hi


def _load_token_words(ref, lead, n_tok):
    parts = []
    for s in range(ROW_SLABS):
        idx = (pl.ds(s, n_tok, stride=ROW_SLABS), slice(None))
        parts.append(ref[lead + idx] if lead else ref[idx])
    return jnp.concatenate(parts, axis=1)


def _store_token_words(ref, words, n_tok):
    for s in range(ROW_SLABS):
        ref[pl.ds(s, n_tok, stride=ROW_SLABS), :] = words[:, s * 128:(s + 1) * 128]


def _route(scores, biased):
    tokens = scores.shape[1]
    neg = -jnp.inf
    epg = EXPERTS_PER_GROUP
    iota_g = lax.broadcasted_iota(jnp.int32, (epg, tokens), 0).astype(F32)

    def pick_first_max(cur, iota, size):
        m = jnp.max(cur, axis=0, keepdims=True)
        idx = jnp.min(jnp.where(cur == m, iota, float(size)), axis=0, keepdims=True)
        return m, idx, iota == idx

    group_scores = []
    for g in range(N_EXPERT_GROUPS):
        vals = biased[g * epg:(g + 1) * epg, :]
        m1, _, hit = pick_first_max(vals, iota_g, epg)
        m2 = jnp.max(jnp.where(hit, neg, vals), axis=0, keepdims=True)
        group_scores.append(m1 + m2)
    cur = jnp.concatenate(group_scores, axis=0)
    group_sel = jnp.zeros_like(cur)
    for _ in range(TOPK_GROUPS):
        _, _, hit = pick_first_max(cur, iota_g, N_EXPERT_GROUPS)
        group_sel = jnp.where(hit, 1.0, group_sel)
        cur = jnp.where(hit, neg, cur)
    masked = jnp.concatenate(
        [jnp.where(group_sel[g:g + 1, :] > 0.0, biased[g * epg:(g + 1) * epg, :], neg)
         for g in range(N_EXPERT_GROUPS)], axis=0)
    iota_e = lax.broadcasted_iota(jnp.int32, masked.shape, 0).astype(F32)
    sel = jnp.zeros_like(masked)
    cur = masked
    picks = []
    for _ in range(TOP_K):
        _, idx, hit = pick_first_max(cur, iota_e, N_EXPERTS)
        picks.append(idx)
        sel = jnp.where(hit, 1.0, sel)
        cur = jnp.where(hit, neg, cur)
    w = scores * sel
    return w / jnp.sum(w, axis=0, keepdims=True) * ROUTED_SCALE, sel, picks


def _router_kernel(x_ref, mod_ref, g2_ref, wrt_ref, rb_ref, hp_ref, ek_ref, rk_ref, wt_ref, cnt_ref, run_scr):
    tm = x_ref.shape[0]

    @pl.when(pl.program_id(0) == 0)
    def _():
        run_scr[...] = jnp.zeros_like(run_scr)

    h = _rms_mod(x_ref[...], g2_ref[...], mod_ref[0, 3:4, :], mod_ref[0, 4:5, :])
    half = D_MODEL // 2
    _store_token_words(hp_ref, _pack_pair(h[:, :half], h[:, half:]), tm)

    logits_t = lax.dot_general(wrt_ref[...], h, (((1,), (1,)), ((), ())),
                               precision=lax.Precision.HIGHEST, preferred_element_type=F32)
    scores = jax.nn.sigmoid(logits_t)
    comb_t, sel, picks = _route(scores, scores + rb_ref[...])

    earlier = (lax.broadcasted_iota(jnp.int32, (tm, tm), 0) < lax.broadcasted_iota(jnp.int32, (tm, tm), 1))
    rank_t = _dot(sel.astype(BF16), jnp.where(earlier, 1.0, 0.0).astype(BF16)) + run_scr[...]
    run_scr[...] += jnp.sum(sel, axis=1, keepdims=True)
    cnt_ref[...] = jnp.broadcast_to(run_scr[...], cnt_ref.shape)

    iota_e = lax.broadcasted_iota(jnp.int32, sel.shape, 0).astype(F32)
    ranks, weights = [], []
    for idx in picks:
        hit = iota_e == idx
        ranks.append(jnp.sum(jnp.where(hit, rank_t, 0.0), axis=0, keepdims=True))
        weights.append(jnp.sum(jnp.where(hit, comb_t, 0.0), axis=0, keepdims=True))
    ek_ref[...] = jnp.concatenate(picks, axis=0).astype(jnp.int32)
    rk_ref[...] = jnp.concatenate(ranks, axis=0).astype(jnp.int32)
    w_rep = jnp.concatenate([jnp.broadcast_to(w, (SC_LANES, tm)) for w in weights], axis=0)
    wt_ref[...] = w_rep.T


def _router(x1, mod3, norm2_g, w_router_t, router_bias, seq_len, mod_row_of_batch):
    t = x1.shape[0]
    tm = TM_ROUTER

    def mod_idx(i):
        return (mod_row_of_batch((i * tm) // seq_len), 0, 0)

    def full(a):
        return pl.BlockSpec(a.shape, lambda i: (0,) * a.ndim)

    return pl.pallas_call(
        _router_kernel,
        grid=(t // tm,),
        in_specs=[pl.BlockSpec((tm, D_MODEL), lambda i: (i, 0)),
                  pl.BlockSpec((1, 6, D_MODEL), mod_idx),
                  full(norm2_g), full(w_router_t), full(router_bias)],
        out_specs=[pl.BlockSpec((tm * ROW_SLABS, 128), lambda i: (i, 0)),
                   pl.BlockSpec((TOP_K, tm), lambda i: (0, i)),
                   pl.BlockSpec((TOP_K, tm), lambda i: (0, i)),
                   pl.BlockSpec((tm, 128), lambda i: (i, 0)),
                   pl.BlockSpec((N_EXPERTS, 128), lambda i: (0, 0))],
        out_shape=[jax.ShapeDtypeStruct((t * ROW_SLABS, 128), jnp.int32),
                   jax.ShapeDtypeStruct((TOP_K, t), jnp.int32),
                   jax.ShapeDtypeStruct((TOP_K, t), jnp.int32),
                   jax.ShapeDtypeStruct((t, 128), F32),
                   jax.ShapeDtypeStruct((N_EXPERTS, 128), F32)],
        scratch_shapes=[pltpu.VMEM((N_EXPERTS, 1), F32)],
        compiler_params=pltpu.CompilerParams(dimension_semantics=("arbitrary",),
                                             vmem_limit_bytes=VMEM_LIMIT),
        name="router",
    )(x1, mod3, norm2_g, w_router_t, router_bias)


def _plan_kernel(ek_ref, rk_ref, cnt_ref, pos_ref, texp_ref, nused_ref, tend_ref):
    rows = float(EXPERT_ROWS)
    cnt = cnt_ref[:, 0:1]
    tiles = jnp.floor((cnt + (rows - 1.0)) / rows)
    before = (lax.broadcasted_iota(jnp.int32, (N_EXPERTS, N_EXPERTS), 1)
              < lax.broadcasted_iota(jnp.int32, (N_EXPERTS, N_EXPERTS), 0))
    tile_start = jnp.dot(jnp.where(before, 1.0, 0.0), jnp.broadcast_to(tiles, (N_EXPERTS, 128)),
                         precision=lax.Precision.HIGHEST, preferred_element_type=F32)[:, 0:1]
    tile_end = tile_start + tiles
    row_start = tile_start * rows

    ek = ek_ref[...]
    pos = rk_ref[...].astype(F32)
    tile_id = lax.broadcasted_iota(jnp.int32, texp_ref.shape, 1).astype(F32)
    texp = jnp.zeros(texp_ref.shape, F32)
    for e in range(N_EXPERTS):
        pos = pos + jnp.where(ek == e, row_start[e:e + 1, :], 0.0)
        texp = texp + jnp.where(tile_id >= tile_end[e:e + 1, :], 1.0, 0.0)
    pos_ref[...] = pos.astype(jnp.int32)
    texp_ref[...] = jnp.minimum(texp, N_EXPERTS - 1.0).astype(jnp.int32)
    nused_ref[...] = jnp.broadcast_to(tile_end[N_EXPERTS - 1:N_EXPERTS, :], nused_ref.shape).astype(jnp.int32)
    tend_ref[...] = jnp.broadcast_to(tile_end, tend_ref.shape).astype(jnp.int32)


def _plan(ek, rk, cnt, n_tiles_pad):
    t = ek.shape[1]

    def full(shape):
        return pl.BlockSpec(shape, lambda: (0,) * len(shape))

    return pl.pallas_call(
        _plan_kernel,
        in_specs=[full(ek.shape), full(rk.shape), full(cnt.shape)],
        out_specs=[full((TOP_K, t)), full((1, n_tiles_pad)), full((1, 128)), full((N_EXPERTS, 128))],
        out_shape=[jax.ShapeDtypeStruct((TOP_K, t), jnp.int32),
                   jax.ShapeDtypeStruct((1, n_tiles_pad), jnp.int32),
                   jax.ShapeDtypeStruct((1, 128), jnp.int32),
                   jax.ShapeDtypeStruct((N_EXPERTS, 128), jnp.int32)],
        compiler_params=pltpu.CompilerParams(vmem_limit_bytes=VMEM_LIMIT),
        name="plan",
    )(ek, rk, cnt)


def _sc_mesh():
    return plsc.VectorSubcoreMesh(core_axis_name="c", subcore_axis_name="s")


def _sc_dispatch(rows, pos3, n_out):
    t = rows.shape[0]
    ch = SC_CHUNK
    per_w = (t // ch) // SC_WORKERS

    @functools.partial(
        pl.kernel, out_type=jax.ShapeDtypeStruct((n_out,) + rows.shape[1:], jnp.int32), mesh=_sc_mesh(),
        scratch_types=[pltpu.VMEM((TOP_K, ch), jnp.int32), pltpu.VMEM((ch,) + rows.shape[1:], jnp.int32),
                       pltpu.SemaphoreType.DMA])
    def k(rows_hbm, pos_hbm, out_hbm, idx_v, rows_v, sem):
        wid = lax.axis_index("s") * SC_CORES + lax.axis_index("c")

        @pl.loop(0, per_w)
        def _(j):
            c = wid * per_w + j
            pltpu.sync_copy(pos_hbm.at[c], idx_v)
            pltpu.sync_copy(rows_hbm.at[pl.ds(c * ch, ch)], rows_v)
            copies = [pltpu.async_copy(rows_v, out_hbm.at[idx_v.at[kk]], sem) for kk in range(TOP_K)]
            for cp in copies:
                cp.wait()

    return k(rows, pos3)


def _sc_combine(table, pos3, wtok, t):
    ch = SC_CHUNK
    sub = SC_COMBINE_TOKENS
    lanes = SC_LANES
    slabs = ROW_SLABS
    per_w = (t // ch) // SC_WORKERS
    subs_per_chunk = ch // sub
    n_steps = per_w * subs_per_chunk

    @functools.partial(
        pl.kernel, out_type=jax.ShapeDtypeStruct((t, 2 * slabs, 128), F32), mesh=_sc_mesh(),
        scratch_types=[pltpu.VMEM((per_w, TOP_K, ch), jnp.int32),
                       pltpu.VMEM((2, TOP_K, sub, slabs, 128), jnp.int32),
                       pltpu.VMEM((2, sub, 128), F32),
                       pltpu.VMEM((sub, 2 * slabs, 128), F32),
                       pltpu.SemaphoreType.DMA((2,))],
        compiler_params=pltpu.CompilerParams(needs_layout_passes=False))
    def k(tab_hbm, pos_hbm, w_hbm, out_hbm, idx_v, rows_v, w_v, out_v, sem):
        wid = lax.axis_index("s") * SC_CORES + lax.axis_index("c")
        for j in range(per_w):
            pltpu.sync_copy(pos_hbm.at[wid * per_w + j], idx_v.at[j])

        def first_token(step):
            return (wid * per_w + step // subs_per_chunk) * ch + (step % subs_per_chunk) * sub

        def copies(step, slot):
            j = step // subs_per_chunk
            s = step % subs_per_chunk
            idx = [idx_v.at[j, kk, pl.ds(s * sub, sub)] for kk in range(TOP_K)]
            return ([pltpu.make_async_copy(tab_hbm.at[idx[kk]], rows_v.at[slot, kk], sem.at[slot])
                     for kk in range(TOP_K)]
                    + [pltpu.make_async_copy(w_hbm.at[pl.ds(first_token(step), sub)], w_v.at[slot], sem.at[slot])])

        for cp in copies(0, 0):
            cp.start()

        @pl.loop(0, n_steps)
        def _(step):
            slot = step % 2

            @pl.when(step + 1 < n_steps)
            def _():
                for cp in copies(step + 1, 1 - slot):
                    cp.start()

            for cp in copies(step, slot):
                cp.wait()

            @pl.loop(0, sub)
            def _(tt):
                wk = [w_v[slot, tt, pl.ds(kk * lanes, lanes)] for kk in range(TOP_K)]
                for sl in range(slabs):
                    @plsc.parallel_loop(0, 128, step=lanes, unroll=4)
                    def _(off):
                        acc_lo = jnp.zeros((lanes,), F32)
                        acc_hi = jnp.zeros((lanes,), F32)
                        for kk in range(TOP_K):
                            word = rows_v[slot, kk, tt, sl, pl.ds(off, lanes)]
                            lo = plsc.bitcast(word << 16, F32)
                            hi = plsc.bitcast(word & jnp.int32(-65536), F32)
                            acc_lo = acc_lo + wk[kk] * lo
                            acc_hi = acc_hi + wk[kk] * hi
                        out_v[tt, sl, pl.ds(off, lanes)] = acc_lo
                        out_v[tt, slabs + sl, pl.ds(off, lanes)] = acc_hi

            pltpu.sync_copy(out_v, out_hbm.at[pl.ds(first_token(step), sub)])

    return k(table, pos3, wtok)


def _experts_kernel(texp_ref, nused_ref, tend_ref, xs_ref, weg_hbm, weu_hbm, wed_hbm, ys_ref,
                    wg_scr, wu_scr, wd_scr, wg_buf, wu_buf, wd_buf, sem, group_scr):
    step = pl.program_id(0)
    rows = EXPERT_ROWS
    half = D_MODEL // 2
    n_used = nused_ref[0]

    def weight_copies(e, slot):
        return [pltpu.make_async_copy(weg_hbm.at[e], wg_buf.at[slot], sem.at[slot, 0]),
                pltpu.make_async_copy(weu_hbm.at[e], wu_buf.at[slot], sem.at[slot, 1]),
                pltpu.make_async_copy(wed_hbm.at[e], wd_buf.at[slot], sem.at[slot, 2])]

    @pl.when(step == 0)
    def _():
        group_scr[0] = 0
        for cp in weight_copies(texp_ref[0], 0):
            cp.start()

    def row_tile(tile, x_view, y_view):
        expert = texp_ref[tile]
        used = tile < n_used
        new_expert = (tile == 0) | (expert != texp_ref[jnp.maximum(tile - 1, 0)])

        @pl.when(used & new_expert)
        def _():
            group = group_scr[0]
            slot = group % 2
            next_tile = tend_ref[expert]

            @pl.when(next_tile < n_used)
            def _():
                for cp in weight_copies(texp_ref[next_tile], 1 - slot):
                    cp.start()

            for cp in weight_copies(expert, slot):
                cp.wait()
            wg_scr[...] = wg_buf[slot].astype(BF16)
            wu_scr[...] = wu_buf[slot].astype(BF16)
            wd_scr[...] = wd_buf[slot].astype(BF16)
            group_scr[0] = group + 1

        @pl.when(used)
        def _():
            lo, hi = _unpack_pair(_load_token_words(x_view, (), rows))
            lo = lo.astype(BF16)
            hi = hi.astype(BF16)
            g = _dot(lo, wg_scr[0:half, :]) + _dot(hi, wg_scr[half:D_MODEL, :])
            u = _dot(lo, wu_scr[0:half, :]) + _dot(hi, wu_scr[half:D_MODEL, :])
            y = _dot((_silu(g) * u).astype(BF16), wd_scr[...])
            _store_token_words(y_view, _pack_pair(y[:, :half], y[:, half:]), rows)

        @pl.when(jnp.logical_not(used) & (step == (n_used - 1) // TILES_PER_STEP))
        def _():
            y_view[...] = jnp.zeros_like(y_view)

    for s in range(TILES_PER_STEP):
        view = pl.ds(s * rows * ROW_SLABS, rows * ROW_SLABS)
        row_tile(step * TILES_PER_STEP + s, xs_ref.at[view], ys_ref.at[view])


def _experts(texp, nused, tend, xs2d, weg, weu, wed, n_tiles):
    block = (TILES_PER_STEP * EXPERT_ROWS * ROW_SLABS, 128)
    hbm = pl.BlockSpec(memory_space=pl.ANY)

    def block_idx(j, te, nu, tn):
        return (jnp.minimum(j, (nu[0] - 1) // TILES_PER_STEP), 0)

    grid_spec = pltpu.PrefetchScalarGridSpec(
        num_scalar_prefetch=3,
        grid=(n_tiles // TILES_PER_STEP,),
        in_specs=[pl.BlockSpec(block, block_idx), hbm, hbm, hbm],
        out_specs=pl.BlockSpec(block, block_idx),
        scratch_shapes=[pltpu.VMEM((D_MODEL, EXPERT_DIM), BF16),
                        pltpu.VMEM((D_MODEL, EXPERT_DIM), BF16),
                        pltpu.VMEM((EXPERT_DIM, D_MODEL), BF16),
                        pltpu.VMEM((2, D_MODEL, EXPERT_DIM), F32),
                        pltpu.VMEM((2, D_MODEL, EXPERT_DIM), F32),
                        pltpu.VMEM((2, EXPERT_DIM, D_MODEL), F32),
                        pltpu.SemaphoreType.DMA((2, 3)),
                        pltpu.SMEM((1,), jnp.int32)],
    )
    return pl.pallas_call(
        _experts_kernel,
        grid_spec=grid_spec,
        out_shape=jax.ShapeDtypeStruct(xs2d.shape, jnp.int32),
        compiler_params=pltpu.CompilerParams(dimension_semantics=("arbitrary",),
                                             vmem_limit_bytes=VMEM_LIMIT),
        name="experts",
    )(texp, nused, tend, xs2d, weg, weu, wed)


def _final_kernel(x_ref, routed_ref, mod_ref, g2_ref, wsg_ref, wsu_ref, wsd_ref, fng_ref, o_ref):
    tm = x_ref.shape[0]
    x = x_ref[...]
    hb = _rms_mod(x, g2_ref[...], mod_ref[0, 3:4, :], mod_ref[0, 4:5, :]).astype(BF16)
    shared = _dot((_silu(_dot(hb, wsg_ref[...])) * _dot(hb, wsu_ref[...])).astype(BF16), wsd_ref[...])
    n_slabs = 2 * ROW_SLABS
    routed = jnp.concatenate([routed_ref[pl.ds(s, tm, stride=n_slabs), :] for s in range(n_slabs)], axis=1)
    y = x + mod_ref[0, 5:6, :] * (routed + shared)
    ms = jnp.mean(y * y, axis=-1, keepdims=True)
    o_ref[...] = y * lax.rsqrt(ms + EPS) * fng_ref[...]


def _final(x1, routed2d, mod3, norm2_g, wsg, wsu, wsd, final_g, seq_len, mod_row_of_batch):
    t = x1.shape[0]
    tm = TM_FINAL

    def mod_idx(i):
        return (mod_row_of_batch((i * tm) // seq_len), 0, 0)

    def full(a):
        return pl.BlockSpec(a.shape, lambda i: (0,) * a.ndim)

    return pl.pallas_call(
        _final_kernel,
        grid=(t // tm,),
        in_specs=[pl.BlockSpec((tm, D_MODEL), lambda i: (i, 0)),
                  pl.BlockSpec((tm * 2 * ROW_SLABS, 128), lambda i: (i, 0)),
                  pl.BlockSpec((1, 6, D_MODEL), mod_idx),
                  full(norm2_g), full(wsg), full(wsu), full(wsd), full(final_g)],
        out_specs=pl.BlockSpec((tm, D_MODEL), lambda i: (i, 0)),
        out_shape=jax.ShapeDtypeStruct((t, D_MODEL), F32),
        compiler_params=pltpu.CompilerParams(dimension_semantics=("parallel",),
                                             vmem_limit_bytes=VMEM_LIMIT),
        name="final",
    )(x1, routed2d, mod3, norm2_g, wsg, wsu, wsd, final_g)


def _moe(x1, mod3, lw, seq_len, mod_row_of_batch):
    t = x1.shape[0]
    n_tiles = TOP_K * t // EXPERT_ROWS + N_EXPERTS
    n_tiles_pad = -(-n_tiles // 128) * 128
    hp2d, ek, rk, wtok, cnt = _router(x1, mod3, lw["norm2_g"], lw["w_router_t"], lw["router_bias"],
                                      seq_len, mod_row_of_batch)
    pos, texp, nused, tend = _plan(ek, rk, cnt, n_tiles_pad)
    pos3 = pos.reshape(TOP_K, t // SC_CHUNK, SC_CHUNK).transpose(1, 0, 2)
    xs = _sc_dispatch(hp2d.reshape(t, ROW_SLABS, 128), pos3, n_tiles * EXPERT_ROWS)
    ys2d = _experts(texp.reshape(-1), nused.reshape(-1), tend[:, 0], xs.reshape(-1, 128),
                    lw["weg"], lw["weu"], lw["wed"], n_tiles)
    routed = _sc_combine(ys2d.reshape(-1, ROW_SLABS, 128), pos3, wtok, t)
    return _final(x1, routed.reshape(t * 2 * ROW_SLABS, 128), mod3, lw["norm2_g"],
                  lw["wsg"], lw["wsu"], lw["wsd"], lw["final_g"], seq_len, mod_row_of_batch)


def _dft_tables(seq_len):
    gd = FOURIER_GROUP_DIM
    kc = np.arange(gd)
    ang_c = ((kc[:, None] * kc[None, :]) % gd) * (2.0 * math.pi / gd)
    cs = np.concatenate([np.cos(ang_c), np.sin(ang_c)], axis=1) * (gd ** -0.5)
    kl = np.arange(seq_len)
    ang_l = ((kl[:, None] * kl[None, :]) % seq_len) * (2.0 * math.pi / seq_len)
    cls = np.concatenate([np.cos(ang_l), -np.sin(ang_l)], axis=1) * (seq_len ** -0.5)
    return jnp.asarray(cs.astype(np.float32), dtype=BF16), jnp.asarray(cls.astype(np.float32), dtype=BF16)


def _rope_tables(length):
    rows = length // GRID_W
    r = np.repeat(np.arange(rows, dtype=np.float32), GRID_W)
    col = np.tile(np.arange(GRID_W, dtype=np.float32), rows)
    nf = RET_HEAD_DIM // 4
    inv = (np.float32(ROPE_BASE) ** (-np.arange(nf, dtype=np.float32) / np.float32(nf))).astype(np.float32)
    ar = r[:, None] * inv[None]
    ac = col[:, None] * inv[None]
    ang = np.concatenate([ar, ar, ac, ac], axis=-1).astype(np.float64)
    sign = np.where((np.arange(RET_HEAD_DIM) & nf) == 0, -1.0, 1.0)
    return (jnp.asarray(np.cos(ang).astype(np.float32)),
            jnp.asarray((np.sin(ang) * sign[None, :]).astype(np.float32)))


def _trunk_path(x, mod3, mod_row_of_batch, s0f, s0b, rope, lw):
    batch, seq_len, _ = x.shape
    x2d = x.reshape(batch * seq_len, D_MODEL)
    uf, q, k, v, sg, gf, gr = _inproj(x2d, mod3, lw["norm1_g"], lw["w_in"], seq_len, mod_row_of_batch, rope)
    r, s_f, s_b = _retention(q, k, v, sg, lw["dec"], lw["gn_g"], s0f, s0b, batch, seq_len)
    cs, cls = _dft_tables(seq_len)
    fmix = _fnet(uf, cs, cls, batch, seq_len)
    x1 = _merge(fmix, r, gf, gr, x2d, mod3, lw["w_four"], lw["w_ret"], lw["w_o"], seq_len, mod_row_of_batch)
    y = _moe(x1, mod3, lw, seq_len, mod_row_of_batch)
    return y.reshape(batch, seq_len, D_MODEL), s_f, s_b


def kernel(x_prompt, x_sample, state_ret_fwd, state_ret_bwd, c, c_ctx, w_ada, b_ada, norm1_g, norm2_g, w_in,
           ret_decay_fwd, ret_decay_bwd, ret_gn_g, w_four_out, w_ret_out, w_out, w_router, router_bias,
           w_exp_gate, w_exp_up, w_exp_down, w_shared_gate, w_shared_up, w_shared_down, final_norm_g):
    depth = w_ada.shape[0]
    assert depth == 1, "final norm is fused into the last layer's MoE kernel"
    n_ctx, n_lat = x_prompt.shape[0], x_sample.shape[0]
    cond = jnp.concatenate([c_ctx[None, :], c], axis=0)
    cond = jnp.pad(cond, ((0, (-cond.shape[0]) % 8), (0, 0)))
    rope = _rope_tables(x_sample.shape[1])
    zeros = jnp.zeros((n_ctx, N_RET_HEADS, RET_HEAD_DIM, RET_HEAD_DIM), F32)

    layer = 0
    mod = _ada(cond, w_ada[layer], b_ada[layer][None, :])
    mod3 = mod.reshape(mod.shape[0], 6, D_MODEL)
    dec = jnp.stack([ret_decay_fwd[layer], ret_decay_bwd[layer]], axis=1)
    lw = {
        "norm1_g": norm1_g[layer][None, :],
        "norm2_g": norm2_g[layer][None, :],
        "w_in": w_in[layer].astype(BF16),
        "dec": jnp.broadcast_to(dec[:, :, None], (N_RET_HEADS, 2, RET_HEAD_DIM)).astype(F32),
        "gn_g": ret_gn_g[layer][None, :],
        "w_four": w_four_out[layer].astype(BF16),
        "w_ret": w_ret_out[layer].astype(BF16),
        "w_o": w_out[layer].astype(BF16),
        "w_router_t": w_router[layer].T,
        "router_bias": router_bias[layer][:, None],
        "weg": w_exp_gate[layer],
        "weu": w_exp_up[layer],
        "wed": w_exp_down[layer],
        "wsg": w_shared_gate[layer].astype(BF16),
        "wsu": w_shared_up[layer].astype(BF16),
        "wsd": w_shared_down[layer].astype(BF16),
        "final_g": final_norm_g[None, :],
    }
    y_prompt, s_f, s_b = _trunk_path(x_prompt, mod3, lambda b: 0, zeros, zeros, None, lw)
    y_sample, _, _ = _trunk_path(x_sample, mod3, lambda b: 1 + b, state_ret_fwd[:, layer],
                                 state_ret_bwd[:, layer], rope, lw)
    return (y_prompt, y_sample, s_f[:, None], s_b[:, None])
```

```python
import functools
import math

import jax
import jax.numpy as jnp
import numpy as np
from jax import lax
from jax.experimental import pallas as pl
from jax.experimental.pallas import tpu as pltpu
from jax.experimental.pallas import tpu_sc as plsc

F32 = jnp.float32
BF16 = jnp.bfloat16

D_MODEL = 1024
GRID_W = 64
N_FOURIER_GROUPS = 8
FOURIER_GROUP_DIM = 128
N_RET_HEADS = 4
RET_HEAD_DIM = 128
RET_WIDTH = N_RET_HEADS * RET_HEAD_DIM
CHUNK = 128
N_EXPERTS = 64
N_EXPERT_GROUPS = 8
EXPERTS_PER_GROUP = N_EXPERTS // N_EXPERT_GROUPS
TOPK_GROUPS = 4
TOP_K = 8
EXPERT_DIM = 256
ROUTED_SCALE = 2.5
ROPE_BASE = 10000.0
EPS = 1e-6
Q_SCALE = RET_HEAD_DIM ** -0.5

_C_UF = (0, 1024)
_C_Q = (1024, 1536)
_C_K = (1536, 2048)
_C_V = (2048, 2560)
_C_G = (2560, 3072)
_C_GF = (3072, 4096)
_C_GR = (4096, 5120)

VMEM_LIMIT = 56 * 1024 * 1024

TM_INPROJ = 1024
TM_PROJ = 512
FNET_ROWS = 256
TM_ROUTER = 1024
TM_FINAL = 512
EXPERT_ROWS = 512
TILES_PER_STEP = 2
ROW_SLABS = 4
SC_CORES = 2
SC_WORKERS = 32
SC_CHUNK = 128
SC_LANES = 16
SC_COMBINE_TOKENS = 8


def _silu(x):
    return x * jax.nn.sigmoid(x)


def _dot(a, b):
    return jnp.dot(a, b, preferred_element_type=F32)


def _rms_mod(x, g, shift, scale):
    ms = jnp.mean(x * x, axis=-1, keepdims=True)
    y = x * lax.rsqrt(ms + EPS) * g
    return y * (1.0 + scale) + shift


def _ada_kernel(cond_ref, w_ref, b_ref, o_ref):
    s = _silu(cond_ref[...]).astype(BF16)
    o_ref[...] = _dot(s, w_ref[...].astype(BF16)) + b_ref[...]


def _ada(cond, w_ada, b_ada):
    rows, n = cond.shape[0], w_ada.shape[1]
    tn = 1536
    return pl.pallas_call(
        _ada_kernel,
        grid=(n // tn,),
        in_specs=[pl.BlockSpec((rows, D_MODEL), lambda j: (0, 0)),
                  pl.BlockSpec((D_MODEL, tn), lambda j: (0, j)),
                  pl.BlockSpec((1, tn), lambda j: (0, j))],
        out_specs=pl.BlockSpec((rows, tn), lambda j: (0, j)),
        out_shape=jax.ShapeDtypeStruct((rows, n), F32),
        compiler_params=pltpu.CompilerParams(vmem_limit_bytes=VMEM_LIMIT),
        name="ada",
    )(cond, w_ada, b_ada)


def _rope_head(x, cos, sin_signed, first_half):
    partner = jnp.where(first_half, pltpu.roll(x, 96, 1), pltpu.roll(x, 32, 1))
    return x * cos + partner * sin_signed


def _inproj_kernel(*refs, use_rope):
    if use_rope:
        x_ref, mod_ref, g_ref, w_ref, cos_ref, sin_ref = refs[:6]
        outs = refs[6:]
    else:
        x_ref, mod_ref, g_ref, w_ref = refs[:4]
        outs = refs[4:]
    uf_o, q_o, k_o, v_o, sg_o, gf_o, gr_o = outs

    h = _rms_mod(x_ref[...], g_ref[...], mod_ref[0, 0:1, :], mod_ref[0, 1:2, :])
    hb = h.astype(BF16)

    def proj(cols):
        return _dot(hb, w_ref[:, cols[0]:cols[1]])

    uf_o[...] = proj(_C_UF).astype(BF16)
    q = proj(_C_Q)
    k = proj(_C_K)
    if use_rope:
        cos = cos_ref[...]
        sin_signed = sin_ref[...]
        lane = lax.broadcasted_iota(jnp.int32, cos.shape, 1)
        first_half = (lane & 32) == 0
        for hd in range(N_RET_HEADS):
            sl = slice(hd * RET_HEAD_DIM, (hd + 1) * RET_HEAD_DIM)
            q_o[:, sl] = (_rope_head(q[:, sl], cos, sin_signed, first_half) * Q_SCALE).astype(BF16)
            k_o[:, sl] = _rope_head(k[:, sl], cos, sin_signed, first_half).astype(BF16)
    else:
        q_o[...] = (q * Q_SCALE).astype(BF16)
        k_o[...] = k.astype(BF16)
    v_o[...] = proj(_C_V).astype(BF16)
    sg_o[...] = _silu(proj(_C_G)).astype(BF16)
    gf_o[...] = jax.nn.sigmoid(proj(_C_GF)).astype(BF16)
    gr_o[...] = jax.nn.sigmoid(proj(_C_GR)).astype(BF16)


def _inproj(x2d, mod3, norm_g, w_in_bf, seq_len, mod_row_of_batch, rope):
    t = x2d.shape[0]
    tm = TM_INPROJ
    tiles_per_seq = max(seq_len // tm, 1)

    def mod_idx(i):
        return (mod_row_of_batch((i * tm) // seq_len), 0, 0)

    in_specs = [pl.BlockSpec((tm, D_MODEL), lambda i: (i, 0)),
                pl.BlockSpec((1, 6, D_MODEL), mod_idx),
                pl.BlockSpec((1, D_MODEL), lambda i: (0, 0)),
                pl.BlockSpec(w_in_bf.shape, lambda i: (0, 0), pipeline_mode=pl.Buffered(1))]
    args = [x2d, mod3, norm_g, w_in_bf]
    if rope is not None:
        in_specs += [pl.BlockSpec((tm, RET_HEAD_DIM), lambda i: (i % tiles_per_seq, 0))] * 2
        args += list(rope)
    widths = [1024, RET_WIDTH, RET_WIDTH, RET_WIDTH, RET_WIDTH, 1024, 1024]
    return pl.pallas_call(
        functools.partial(_inproj_kernel, use_rope=rope is not None),
        grid=(t // tm,),
        in_specs=in_specs,
        out_specs=[pl.BlockSpec((tm, w), lambda i: (i, 0)) for w in widths],
        out_shape=[jax.ShapeDtypeStruct((t, w), BF16) for w in widths],
        compiler_params=pltpu.CompilerParams(dimension_semantics=("parallel",),
                                             vmem_limit_bytes=VMEM_LIMIT),
        name="inproj",
    )(*args)


def _retention_kernel(q_ref, k_ref, v_ref, sg_ref, dec_ref, gn_ref, s0f_ref, s0b_ref,
                      r_ref, sfo_ref, sbo_ref, tab_scr, gc_scr):
    n_chunks = q_ref.shape[0] // CHUNK
    hd = RET_HEAD_DIM

    @pl.when(pl.program_id(0) == 0)
    def _():
        row = lax.broadcasted_iota(jnp.int32, (CHUNK, CHUNK), 0).astype(F32)
        col = lax.broadcasted_iota(jnp.int32, (CHUNK, CHUNK), 1).astype(F32)
        diff = row - col
        for h in range(N_RET_HEADS):
            dec = dec_ref[h]
            lg = jnp.minimum(dec, 0.0) - jnp.log1p(jnp.exp(-jnp.abs(dec)))
            lgf = lg[0:1, :]
            lgb = lg[1:2, :]
            tab_scr[h, 0] = jnp.exp(jnp.where(diff >= 0, lgf * diff, lgb * (-diff)))
            tab_scr[h, 1] = jnp.exp(lgf * (row + 1.0))
            tab_scr[h, 2] = jnp.exp(lgb * (CHUNK - row))
            tab_scr[h, 3] = jnp.exp(lgf * (CHUNK - 1.0 - col))
            tab_scr[h, 4] = jnp.exp(lgb * col)
            gc_scr[h] = jnp.exp(lg * CHUNK)

    def rows(n):
        return slice(n * CHUNK, (n + 1) * CHUNK)

    for h in range(N_RET_HEADS):
        cols = slice(h * hd, (h + 1) * hd)
        decay, qw_f, qw_b, kwt_f, kwt_b = (tab_scr[h, i] for i in range(5))
        gc = gc_scr[h]
        gc_f = gc[0:1, :]
        gc_b = gc[1:2, :]

        kv_f, kv_b = [], []
        for n in range(n_chunks):
            kt = k_ref[rows(n), cols].astype(F32).T
            vn = v_ref[rows(n), cols]
            kv_f.append(_dot((kt * kwt_f).astype(BF16), vn))
            kv_b.append(_dot((kt * kwt_b).astype(BF16), vn))

        s = s0f_ref[h]
        prev_f = []
        for n in range(n_chunks):
            prev_f.append(s.astype(BF16))
            s = gc_f * s + kv_f[n]
        sfo_ref[h] = s
        s = s0b_ref[h]
        prev_b = [None] * n_chunks
        for n in reversed(range(n_chunks)):
            prev_b[n] = s.astype(BF16)
            s = gc_b * s + kv_b[n]
        sbo_ref[h] = s

        gn = gn_ref[:, cols]
        for n in range(n_chunks):
            qn = q_ref[rows(n), cols]
            qf = qn.astype(F32)
            scores = lax.dot_general(qn, k_ref[rows(n), cols], (((1,), (1,)), ((), ())),
                                     preferred_element_type=F32)
            o = _dot((scores * decay).astype(BF16), v_ref[rows(n), cols])
            o = o + _dot((qf * qw_f).astype(BF16), prev_f[n])
            o = o + _dot((qf * qw_b).astype(BF16), prev_b[n])
            mu = jnp.mean(o, axis=-1, keepdims=True)
            d = o - mu
            var = jnp.mean(d * d, axis=-1, keepdims=True)
            on = d * lax.rsqrt(var + EPS) * gn
            r_ref[rows(n), cols] = (on * sg_ref[rows(n), cols].astype(F32)).astype(BF16)


def _retention(q, k, v, sg, dec, gn_g, s0f, s0b, batch, seq_len):
    hd = RET_HEAD_DIM
    tok_spec = pl.BlockSpec((seq_len, RET_WIDTH), lambda b: (b, 0))
    st_spec = pl.BlockSpec((None, N_RET_HEADS, hd, hd), lambda b: (b, 0, 0, 0))
    st_shape = jax.ShapeDtypeStruct((batch, N_RET_HEADS, hd, hd), F32)
    return pl.pallas_call(
        _retention_kernel,
        grid=(batch,),
        in_specs=[tok_spec, tok_spec, tok_spec, tok_spec,
                  pl.BlockSpec(dec.shape, lambda b: (0, 0, 0)),
                  pl.BlockSpec(gn_g.shape, lambda b: (0, 0)),
                  st_spec, st_spec],
        out_specs=[tok_spec, st_spec, st_spec],
        out_shape=[jax.ShapeDtypeStruct((batch * seq_len, RET_WIDTH), BF16), st_shape, st_shape],
        scratch_shapes=[pltpu.VMEM((N_RET_HEADS, 5, CHUNK, CHUNK), F32),
                        pltpu.VMEM((N_RET_HEADS, 2, hd), F32)],
        compiler_params=pltpu.CompilerParams(dimension_semantics=("arbitrary",),
                                             vmem_limit_bytes=VMEM_LIMIT),
        name="retention",
    )(q, k, v, sg, dec, gn_g, s0f, s0b)


def _fnet_kernel(uf_ref, cs_ref, cls_ref, o_ref, xcs_ref):
    seq_len = uf_ref.shape[0]
    gd = FOURIER_GROUP_DIM

    @pl.when(pl.program_id(1) == 0)
    def _():
        for g in range(N_FOURIER_GROUPS):
            x = _dot(uf_ref[:, g * gd:(g + 1) * gd], cs_ref[...])
            xcs_ref[0:seq_len, g * gd:(g + 1) * gd] = x[:, :gd].astype(BF16)
            xcs_ref[seq_len:2 * seq_len, g * gd:(g + 1) * gd] = x[:, gd:].astype(BF16)

    o_ref[...] = _dot(cls_ref[...], xcs_ref[...]).astype(BF16)


def _fnet(uf, cs, cls, batch, seq_len):
    rb = FNET_ROWS
    nr = seq_len // rb
    return pl.pallas_call(
        _fnet_kernel,
        grid=(batch, nr),
        in_specs=[pl.BlockSpec((seq_len, D_MODEL), lambda b, r: (b, 0)),
                  pl.BlockSpec(cs.shape, lambda b, r: (0, 0)),
                  pl.BlockSpec((rb, 2 * seq_len), lambda b, r: (r, 0))],
        out_specs=pl.BlockSpec((rb, D_MODEL), lambda b, r: (b * nr + r, 0)),
        out_shape=jax.ShapeDtypeStruct((batch * seq_len, D_MODEL), BF16),
        scratch_shapes=[pltpu.VMEM((2 * seq_len, D_MODEL), BF16)],
        compiler_params=pltpu.CompilerParams(dimension_semantics=("parallel", "arbitrary"),
                                             vmem_limit_bytes=VMEM_LIMIT),
        name="fnet",
    )(uf, cs, cls)


def _merge_kernel(fm_ref, r_ref, gf_ref, gr_ref, x_ref, mod_ref, wf_ref, wr_ref, wo_ref, o_ref):
    f_out = _dot(fm_ref[...], wf_ref[...])
    r_out = _dot(r_ref[...], wr_ref[...])
    merged = gf_ref[...].astype(F32) * f_out + gr_ref[...].astype(F32) * r_out
    mix = _dot(merged.astype(BF16), wo_ref[...])
    o_ref[...] = x_ref[...] + mod_ref[0, 2:3, :] * mix


def _merge(fmix, r, gf, gr, x2d, mod3, w_four, w_ret, w_o, seq_len, mod_row_of_batch):
    t = x2d.shape[0]
    tm = TM_PROJ

    def mod_idx(i):
        return (mod_row_of_batch((i * tm) // seq_len), 0, 0)

    def tok(w):
        return pl.BlockSpec((tm, w), lambda i: (i, 0))

    def full(a):
        return pl.BlockSpec(a.shape, lambda i: (0, 0))

    return pl.pallas_call(
        _merge_kernel,
        grid=(t // tm,),
        in_specs=[tok(D_MODEL), tok(RET_WIDTH), tok(D_MODEL), tok(D_MODEL), tok(D_MODEL),
                  pl.BlockSpec((1, 6, D_MODEL), mod_idx), full(w_four), full(w_ret), full(w_o)],
        out_specs=tok(D_MODEL),
        out_shape=jax.ShapeDtypeStruct((t, D_MODEL), F32),
        compiler_params=pltpu.CompilerParams(dimension_semantics=("parallel",),
                                             vmem_limit_bytes=VMEM_LIMIT),
        name="merge",
    )(fmix, r, gf, gr, x2d, mod3, w_four, w_ret, w_o)


def _pack_pair(lo_f32, hi_f32):
    lo = lax.bitcast_convert_type(lo_f32.astype(BF16).astype(F32), jnp.uint32)
    hi = lax.bitcast_convert_type(hi_f32.astype(BF16).astype(F32), jnp.uint32)
    return lax.bitcast_convert_type((lo >> 16) | hi, jnp.int32)


def _unpack_pair(words_i32):
    w = lax.bitcast_convert_type(words_i32, jnp.uint32)
    lo = lax.bitcast_convert_type(w << 16, F32)
    hi = lax.bitcast_convert_type(w & jnp.uint32(0xFFFF0000), F32)
    return lo, hi


def _load_token_words(ref, lead, n_tok):
    parts = []
    for s in range(ROW_SLABS):
        idx = (pl.ds(s, n_tok, stride=ROW_SLABS), slice(None))
        parts.append(ref[lead + idx] if lead else ref[idx])
    return jnp.concatenate(parts, axis=1)


def _store_token_words(ref, words, n_tok):
    for s in range(ROW_SLABS):
        ref[pl.ds(s, n_tok, stride=ROW_SLABS), :] = words[:, s * 128:(s + 1) * 128]


def _route(scores, biased):
    tokens = scores.shape[1]
    neg = -jnp.inf
    epg = EXPERTS_PER_GROUP
    iota_g = lax.broadcasted_iota(jnp.int32, (epg, tokens), 0).astype(F32)

    def pick_first_max(cur, iota, size):
        m = jnp.max(cur, axis=0, keepdims=True)
        idx = jnp.min(jnp.where(cur == m, iota, float(size)), axis=0, keepdims=True)
        return m, idx, iota == idx

    group_scores = []
    for g in range(N_EXPERT_GROUPS):
        vals = biased[g * epg:(g + 1) * epg, :]
        m1, _, hit = pick_first_max(vals, iota_g, epg)
        m2 = jnp.max(jnp.where(hit, neg, vals), axis=0, keepdims=True)
        group_scores.append(m1 + m2)
    cur = jnp.concatenate(group_scores, axis=0)
    group_sel = jnp.zeros_like(cur)
    for _ in range(TOPK_GROUPS):
        _, _, hit = pick_first_max(cur, iota_g, N_EXPERT_GROUPS)
        group_sel = jnp.where(hit, 1.0, group_sel)
        cur = jnp.where(hit, neg, cur)
    masked = jnp.concatenate(
        [jnp.where(group_sel[g:g + 1, :] > 0.0, biased[g * epg:(g + 1) * epg, :], neg)
         for g in range(N_EXPERT_GROUPS)], axis=0)
    iota_e = lax.broadcasted_iota(jnp.int32, masked.shape, 0).astype(F32)
    sel = jnp.zeros_like(masked)
    cur = masked
    picks = []
    for _ in range(TOP_K):
        _, idx, hit = pick_first_max(cur, iota_e, N_EXPERTS)
        picks.append(idx)
        sel = jnp.where(hit, 1.0, sel)
        cur = jnp.where(hit, neg, cur)
    w = scores * sel
    return w / jnp.sum(w, axis=0, keepdims=True) * ROUTED_SCALE, sel, picks


def _router_kernel(x_ref, mod_ref, g2_ref, wrt_ref, rb_ref, hp_ref, ek_ref, rk_ref, wt_ref, cnt_ref, run_scr):
    tm = x_ref.shape[0]

    @pl.when(pl.program_id(0) == 0)
    def _():
        run_scr[...] = jnp.zeros_like(run_scr)

    h = _rms_mod(x_ref[...], g2_ref[...], mod_ref[0, 3:4, :], mod_ref[0, 4:5, :])
    half = D_MODEL // 2
    _store_token_words(hp_ref, _pack_pair(h[:, :half], h[:, half:]), tm)

    logits_t = lax.dot_general(wrt_ref[...], h, (((1,), (1,)), ((), ())),
                               precision=lax.Precision.HIGHEST, preferred_element_type=F32)
    scores = jax.nn.sigmoid(logits_t)
    comb_t, sel, picks = _route(scores, scores + rb_ref[...])

    earlier = (lax.broadcasted_iota(jnp.int32, (tm, tm), 0) < lax.broadcasted_iota(jnp.int32, (tm, tm), 1))
    rank_t = _dot(sel.astype(BF16), jnp.where(earlier, 1.0, 0.0).astype(BF16)) + run_scr[...]
    run_scr[...] += jnp.sum(sel, axis=1, keepdims=True)
    cnt_ref[...] = jnp.broadcast_to(run_scr[...], cnt_ref.shape)

    iota_e = lax.broadcasted_iota(jnp.int32, sel.shape, 0).astype(F32)
    ranks, weights = [], []
    for idx in picks:
        hit = iota_e == idx
        ranks.append(jnp.sum(jnp.where(hit, rank_t, 0.0), axis=0, keepdims=True))
        weights.append(jnp.sum(jnp.where(hit, comb_t, 0.0), axis=0, keepdims=True))
    ek_ref[...] = jnp.concatenate(picks, axis=0).astype(jnp.int32)
    rk_ref[...] = jnp.concatenate(ranks, axis=0).astype(jnp.int32)
    w_rep = jnp.concatenate([jnp.broadcast_to(w, (SC_LANES, tm)) for w in weights], axis=0)
    wt_ref[...] = w_rep.T


def _router(x1, mod3, norm2_g, w_router_t, router_bias, seq_len, mod_row_of_batch):
    t = x1.shape[0]
    tm = TM_ROUTER

    def mod_idx(i):
        return (mod_row_of_batch((i * tm) // seq_len), 0, 0)

    def full(a):
        return pl.BlockSpec(a.shape, lambda i: (0,) * a.ndim)

    return pl.pallas_call(
        _router_kernel,
        grid=(t // tm,),
        in_specs=[pl.BlockSpec((tm, D_MODEL), lambda i: (i, 0)),
                  pl.BlockSpec((1, 6, D_MODEL), mod_idx),
                  full(norm2_g), full(w_router_t), full(router_bias)],
        out_specs=[pl.BlockSpec((tm * ROW_SLABS, 128), lambda i: (i, 0)),
                   pl.BlockSpec((TOP_K, tm), lambda i: (0, i)),
                   pl.BlockSpec((TOP_K, tm), lambda i: (0, i)),
                   pl.BlockSpec((tm, 128), lambda i: (i, 0)),
                   pl.BlockSpec((N_EXPERTS, 128), lambda i: (0, 0))],
        out_shape=[jax.ShapeDtypeStruct((t * ROW_SLABS, 128), jnp.int32),
                   jax.ShapeDtypeStruct((TOP_K, t), jnp.int32),
                   jax.ShapeDtypeStruct((TOP_K, t), jnp.int32),
                   jax.ShapeDtypeStruct((t, 128), F32),
                   jax.ShapeDtypeStruct((N_EXPERTS, 128), F32)],
        scratch_shapes=[pltpu.VMEM((N_EXPERTS, 1), F32)],
        compiler_params=pltpu.CompilerParams(dimension_semantics=("arbitrary",),
                                             vmem_limit_bytes=VMEM_LIMIT),
        name="router",
    )(x1, mod3, norm2_g, w_router_t, router_bias)


def _plan_kernel(ek_ref, rk_ref, cnt_ref, pos_ref, texp_ref, nused_ref, tend_ref):
    rows = float(EXPERT_ROWS)
    cnt = cnt_ref[:, 0:1]
    tiles = jnp.floor((cnt + (rows - 1.0)) / rows)
    before = (lax.broadcasted_iota(jnp.int32, (N_EXPERTS, N_EXPERTS), 1)
              < lax.broadcasted_iota(jnp.int32, (N_EXPERTS, N_EXPERTS), 0))
    tile_start = jnp.dot(jnp.where(before, 1.0, 0.0), jnp.broadcast_to(tiles, (N_EXPERTS, 128)),
                         precision=lax.Precision.HIGHEST, preferred_element_type=F32)[:, 0:1]
    tile_end = tile_start + tiles
    row_start = tile_start * rows

    ek = ek_ref[...]
    pos = rk_ref[...].astype(F32)
    tile_id = lax.broadcasted_iota(jnp.int32, texp_ref.shape, 1).astype(F32)
    texp = jnp.zeros(texp_ref.shape, F32)
    for e in range(N_EXPERTS):
        pos = pos + jnp.where(ek == e, row_start[e:e + 1, :], 0.0)
        texp = texp + jnp.where(tile_id >= tile_end[e:e + 1, :], 1.0, 0.0)
    pos_ref[...] = pos.astype(jnp.int32)
    texp_ref[...] = jnp.minimum(texp, N_EXPERTS - 1.0).astype(jnp.int32)
    nused_ref[...] = jnp.broadcast_to(tile_end[N_EXPERTS - 1:N_EXPERTS, :], nused_ref.shape).astype(jnp.int32)
    tend_ref[...] = jnp.broadcast_to(tile_end, tend_ref.shape).astype(jnp.int32)


def _plan(ek, rk, cnt, n_tiles_pad):
    t = ek.shape[1]

    def full(shape):
        return pl.BlockSpec(shape, lambda: (0,) * len(shape))

    return pl.pallas_call(
        _plan_kernel,
        in_specs=[full(ek.shape), full(rk.shape), full(cnt.shape)],
        out_specs=[full((TOP_K, t)), full((1, n_tiles_pad)), full((1, 128)), full((N_EXPERTS, 128))],
        out_shape=[jax.ShapeDtypeStruct((TOP_K, t), jnp.int32),
                   jax.ShapeDtypeStruct((1, n_tiles_pad), jnp.int32),
                   jax.ShapeDtypeStruct((1, 128), jnp.int32),
                   jax.ShapeDtypeStruct((N_EXPERTS, 128), jnp.int32)],
        compiler_params=pltpu.CompilerParams(vmem_limit_bytes=VMEM_LIMIT),
        name="plan",
    )(ek, rk, cnt)


def _sc_mesh():
    return plsc.VectorSubcoreMesh(core_axis_name="c", subcore_axis_name="s")


def _sc_dispatch(rows, pos3, n_out):
    t = rows.shape[0]
    ch = SC_CHUNK
    per_w = (t // ch) // SC_WORKERS

    @functools.partial(
        pl.kernel, out_type=jax.ShapeDtypeStruct((n_out,) + rows.shape[1:], jnp.int32), mesh=_sc_mesh(),
        scratch_types=[pltpu.VMEM((TOP_K, ch), jnp.int32), pltpu.VMEM((ch,) + rows.shape[1:], jnp.int32),
                       pltpu.SemaphoreType.DMA])
    def k(rows_hbm, pos_hbm, out_hbm, idx_v, rows_v, sem):
        wid = lax.axis_index("s") * SC_CORES + lax.axis_index("c")

        @pl.loop(0, per_w)
        def _(j):
            c = wid * per_w + j
            pltpu.sync_copy(pos_hbm.at[c], idx_v)
            pltpu.sync_copy(rows_hbm.at[pl.ds(c * ch, ch)], rows_v)
            copies = [pltpu.async_copy(rows_v, out_hbm.at[idx_v.at[kk]], sem) for kk in range(TOP_K)]
            for cp in copies:
                cp.wait()

    return k(rows, pos3)


def _sc_combine(table, pos3, wtok, t):
    ch = SC_CHUNK
    sub = SC_COMBINE_TOKENS
    lanes = SC_LANES
    slabs = ROW_SLABS
    per_w = (t // ch) // SC_WORKERS
    subs_per_chunk = ch // sub
    n_steps = per_w * subs_per_chunk

    @functools.partial(
        pl.kernel, out_type=jax.ShapeDtypeStruct((t, 2 * slabs, 128), F32), mesh=_sc_mesh(),
        scratch_types=[pltpu.VMEM((per_w, TOP_K, ch), jnp.int32),
                       pltpu.VMEM((2, TOP_K, sub, slabs, 128), jnp.int32),
                       pltpu.VMEM((2, sub, 128), F32),
                       pltpu.VMEM((sub, 2 * slabs, 128), F32),
                       pltpu.SemaphoreType.DMA((2,))],
        compiler_params=pltpu.CompilerParams(needs_layout_passes=False))
    def k(tab_hbm, pos_hbm, w_hbm, out_hbm, idx_v, rows_v, w_v, out_v, sem):
        wid = lax.axis_index("s") * SC_CORES + lax.axis_index("c")
        for j in range(per_w):
            pltpu.sync_copy(pos_hbm.at[wid * per_w + j], idx_v.at[j])

        def first_token(step):
            return (wid * per_w + step // subs_per_chunk) * ch + (step % subs_per_chunk) * sub

        def copies(step, slot):
            j = step // subs_per_chunk
            s = step % subs_per_chunk
            idx = [idx_v.at[j, kk, pl.ds(s * sub, sub)] for kk in range(TOP_K)]
            return ([pltpu.make_async_copy(tab_hbm.at[idx[kk]], rows_v.at[slot, kk], sem.at[slot])
                     for kk in range(TOP_K)]
                    + [pltpu.make_async_copy(w_hbm.at[pl.ds(first_token(step), sub)], w_v.at[slot], sem.at[slot])])

        for cp in copies(0, 0):
            cp.start()

        @pl.loop(0, n_steps)
        def _(step):
            slot = step % 2

            @pl.when(step + 1 < n_steps)
            def _():
                for cp in copies(step + 1, 1 - slot):
                    cp.start()

            for cp in copies(step, slot):
                cp.wait()

            @pl.loop(0, sub)
            def _(tt):
                wk = [w_v[slot, tt, pl.ds(kk * lanes, lanes)] for kk in range(TOP_K)]
                for sl in range(slabs):
                    @plsc.parallel_loop(0, 128, step=lanes, unroll=4)
                    def _(off):
                        acc_lo = jnp.zeros((lanes,), F32)
                        acc_hi = jnp.zeros((lanes,), F32)
                        for kk in range(TOP_K):
                            word = rows_v[slot, kk, tt, sl, pl.ds(off, lanes)]
                            lo = plsc.bitcast(word << 16, F32)
                            hi = plsc.bitcast(word & jnp.int32(-65536), F32)
                            acc_lo = acc_lo + wk[kk] * lo
                            acc_hi = acc_hi + wk[kk] * hi
                        out_v[tt, sl, pl.ds(off, lanes)] = acc_lo
                        out_v[tt, slabs + sl, pl.ds(off, lanes)] = acc_hi

            pltpu.sync_copy(out_v, out_hbm.at[pl.ds(first_token(step), sub)])

    return k(table, pos3, wtok)


def _experts_kernel(texp_ref, nused_ref, tend_ref, xs_ref, weg_hbm, weu_hbm, wed_hbm, ys_ref,
                    wg_scr, wu_scr, wd_scr, wg_buf, wu_buf, wd_buf, sem, group_scr):
    step = pl.program_id(0)
    rows = EXPERT_ROWS
    half = D_MODEL // 2
    n_used = nused_ref[0]

    def weight_copies(e, slot):
        return [pltpu.make_async_copy(weg_hbm.at[e], wg_buf.at[slot], sem.at[slot, 0]),
                pltpu.make_async_copy(weu_hbm.at[e], wu_buf.at[slot], sem.at[slot, 1]),
                pltpu.make_async_copy(wed_hbm.at[e], wd_buf.at[slot], sem.at[slot, 2])]

    @pl.when(step == 0)
    def _():
        group_scr[0] = 0
        for cp in weight_copies(texp_ref[0], 0):
            cp.start()

    def weights_slot(tile):
        expert = texp_ref[tile]
        new_expert = (tile == 0) | (expert != texp_ref[jnp.maximum(tile - 1, 0)])

        @pl.when((tile < n_used) & new_expert)
        def _():
            group = group_scr[0]
            slot = group % 2
            next_tile = tend_ref[expert]

            @pl.when(next_tile < n_used)
            def _():
                for cp in weight_copies(texp_ref[next_tile], 1 - slot):
                    cp.start()

            for cp in weight_copies(expert, slot):
                cp.wait()
            wg_scr[slot] = wg_buf[slot].astype(BF16)
            wu_scr[slot] = wu_buf[slot].astype(BF16)
            wd_scr[slot] = wd_buf[slot].astype(BF16)
            group_scr[0] = group + 1

        return (group_scr[0] + 1) % 2

    def compute(x_view, y_view, slot):
        lo, hi = _unpack_pair(_load_token_words(x_view, (), rows))
        lo = lo.astype(BF16)
        hi = hi.astype(BF16)
        g = _dot(lo, wg_scr[slot, 0:half, :]) + _dot(hi, wg_scr[slot, half:D_MODEL, :])
        u = _dot(lo, wu_scr[slot, 0:half, :]) + _dot(hi, wu_scr[slot, half:D_MODEL, :])
        y = _dot((_silu(g) * u).astype(BF16), wd_scr[slot])
        _store_token_words(y_view, _pack_pair(y[:, :half], y[:, half:]), rows)

    tiles = [step * TILES_PER_STEP + s for s in range(TILES_PER_STEP)]
    views = [pl.ds(s * rows * ROW_SLABS, rows * ROW_SLABS) for s in range(TILES_PER_STEP)]
    slots = [weights_slot(tile) for tile in tiles]

    @pl.when(tiles[-1] < n_used)
    def _():
        for s in range(TILES_PER_STEP):
            compute(xs_ref.at[views[s]], ys_ref.at[views[s]], slots[s])

    @pl.when((tiles[-1] >= n_used) & (step == (n_used - 1) // TILES_PER_STEP))
    def _():
        for s in range(TILES_PER_STEP):
            @pl.when(tiles[s] < n_used)
            def _():
                compute(xs_ref.at[views[s]], ys_ref.at[views[s]], slots[s])

            @pl.when(tiles[s] >= n_used)
            def _():
                ys_ref[views[s], :] = jnp.zeros((rows * ROW_SLABS, 128), jnp.int32)


def _experts(texp, nused, tend, xs2d, weg, weu, wed, n_tiles):
    block = (TILES_PER_STEP * EXPERT_ROWS * ROW_SLABS, 128)
    hbm = pl.BlockSpec(memory_space=pl.ANY)

    def block_idx(j, te, nu, tn):
        return (jnp.minimum(j, (nu[0] - 1) // TILES_PER_STEP), 0)

    grid_spec = pltpu.PrefetchScalarGridSpec(
        num_scalar_prefetch=3,
        grid=(n_tiles // TILES_PER_STEP,),
        in_specs=[pl.BlockSpec(block, block_idx), hbm, hbm, hbm],
        out_specs=pl.BlockSpec(block, block_idx),
        scratch_shapes=[pltpu.VMEM((2, D_MODEL, EXPERT_DIM), BF16),
                        pltpu.VMEM((2, D_MODEL, EXPERT_DIM), BF16),
                        pltpu.VMEM((2, EXPERT_DIM, D_MODEL), BF16),
                        pltpu.VMEM((2, D_MODEL, EXPERT_DIM), F32),
                        pltpu.VMEM((2, D_MODEL, EXPERT_DIM), F32),
                        pltpu.VMEM((2, EXPERT_DIM, D_MODEL), F32),
                        pltpu.SemaphoreType.DMA((2, 3)),
                        pltpu.SMEM((1,), jnp.int32)],
    )
    return pl.pallas_call(
        _experts_kernel,
        grid_spec=grid_spec,
        out_shape=jax.ShapeDtypeStruct(xs2d.shape, jnp.int32),
        compiler_params=pltpu.CompilerParams(dimension_semantics=("arbitrary",),
                                             vmem_limit_bytes=VMEM_LIMIT),
        name="experts",
    )(texp, nused, tend, xs2d, weg, weu, wed)


def _final_kernel(x_ref, routed_ref, mod_ref, g2_ref, wsg_ref, wsu_ref, wsd_ref, fng_ref, o_ref):
    tm = x_ref.shape[0]
    x = x_ref[...]
    hb = _rms_mod(x, g2_ref[...], mod_ref[0, 3:4, :], mod_ref[0, 4:5, :]).astype(BF16)
    shared = _dot((_silu(_dot(hb, wsg_ref[...])) * _dot(hb, wsu_ref[...])).astype(BF16), wsd_ref[...])
    n_slabs = 2 * ROW_SLABS
    routed = jnp.concatenate([routed_ref[pl.ds(s, tm, stride=n_slabs), :] for s in range(n_slabs)], axis=1)
    y = x + mod_ref[0, 5:6, :] * (routed + shared)
    ms = jnp.mean(y * y, axis=-1, keepdims=True)
    o_ref[...] = y * lax.rsqrt(ms + EPS) * fng_ref[...]


def _final(x1, routed2d, mod3, norm2_g, wsg, wsu, wsd, final_g, seq_len, mod_row_of_batch):
    t = x1.shape[0]
    tm = TM_FINAL

    def mod_idx(i):
        return (mod_row_of_batch((i * tm) // seq_len), 0, 0)

    def full(a):
        return pl.BlockSpec(a.shape, lambda i: (0,) * a.ndim)

    return pl.pallas_call(
        _final_kernel,
        grid=(t // tm,),
        in_specs=[pl.BlockSpec((tm, D_MODEL), lambda i: (i, 0)),
                  pl.BlockSpec((tm * 2 * ROW_SLABS, 128), lambda i: (i, 0)),
                  pl.BlockSpec((1, 6, D_MODEL), mod_idx),
                  full(norm2_g), full(wsg), full(wsu), full(wsd), full(final_g)],
        out_specs=pl.BlockSpec((tm, D_MODEL), lambda i: (i, 0)),
        out_shape=jax.ShapeDtypeStruct((t, D_MODEL), F32),
        compiler_params=pltpu.CompilerParams(dimension_semantics=("parallel",),
                                             vmem_limit_bytes=VMEM_LIMIT),
        name="final",
    )(x1, routed2d, mod3, norm2_g, wsg, wsu, wsd, final_g)


def _moe(x1, mod3, lw, seq_len, mod_row_of_batch):
    t = x1.shape[0]
    n_tiles = TOP_K * t // EXPERT_ROWS + N_EXPERTS
    n_tiles_pad = -(-n_tiles // 128) * 128
    hp2d, ek, rk, wtok, cnt = _router(x1, mod3, lw["norm2_g"], lw["w_router_t"], lw["router_bias"],
                                      seq_len, mod_row_of_batch)
    pos, texp, nused, tend = _plan(ek, rk, cnt, n_tiles_pad)
    pos3 = pos.reshape(TOP_K, t // SC_CHUNK, SC_CHUNK).transpose(1, 0, 2)
    xs = _sc_dispatch(hp2d.reshape(t, ROW_SLABS, 128), pos3, n_tiles * EXPERT_ROWS)
    ys2d = _experts(texp.reshape(-1), nused.reshape(-1), tend[:, 0], xs.reshape(-1, 128),
                    lw["weg"], lw["weu"], lw["wed"], n_tiles)
    routed = _sc_combine(ys2d.reshape(-1, ROW_SLABS, 128), pos3, wtok, t)
    return _final(x1, routed.reshape(t * 2 * ROW_SLABS, 128), mod3, lw["norm2_g"],
                  lw["wsg"], lw["wsu"], lw["wsd"], lw["final_g"], seq_len, mod_row_of_batch)


def _dft_tables(seq_len):
    gd = FOURIER_GROUP_DIM
    kc = np.arange(gd)
    ang_c = ((kc[:, None] * kc[None, :]) % gd) * (2.0 * math.pi / gd)
    cs = np.concatenate([np.cos(ang_c), np.sin(ang_c)], axis=1) * (gd ** -0.5)
    kl = np.arange(seq_len)
    ang_l = ((kl[:, None] * kl[None, :]) % seq_len) * (2.0 * math.pi / seq_len)
    cls = np.concatenate([np.cos(ang_l), -np.sin(ang_l)], axis=1) * (seq_len ** -0.5)
    return jnp.asarray(cs.astype(np.float32), dtype=BF16), jnp.asarray(cls.astype(np.float32), dtype=BF16)


def _rope_tables(length):
    rows = length // GRID_W
    r = np.repeat(np.arange(rows, dtype=np.float32), GRID_W)
    col = np.tile(np.arange(GRID_W, dtype=np.float32), rows)
    nf = RET_HEAD_DIM // 4
    inv = (np.float32(ROPE_BASE) ** (-np.arange(nf, dtype=np.float32) / np.float32(nf))).astype(np.float32)
    ar = r[:, None] * inv[None]
    ac = col[:, None] * inv[None]
    ang = np.concatenate([ar, ar, ac, ac], axis=-1).astype(np.float64)
    sign = np.where((np.arange(RET_HEAD_DIM) & nf) == 0, -1.0, 1.0)
    return (jnp.asarray(np.cos(ang).astype(np.float32)),
            jnp.asarray((np.sin(ang) * sign[None, :]).astype(np.float32)))


def _trunk_path(x, mod3, mod_row_of_batch, s0f, s0b, rope, lw):
    batch, seq_len, _ = x.shape
    x2d = x.reshape(batch * seq_len, D_MODEL)
    uf, q, k, v, sg, gf, gr = _inproj(x2d, mod3, lw["norm1_g"], lw["w_in"], seq_len, mod_row_of_batch, rope)
    r, s_f, s_b = _retention(q, k, v, sg, lw["dec"], lw["gn_g"], s0f, s0b, batch, seq_len)
    cs, cls = _dft_tables(seq_len)
    fmix = _fnet(uf, cs, cls, batch, seq_len)
    x1 = _merge(fmix, r, gf, gr, x2d, mod3, lw["w_four"], lw["w_ret"], lw["w_o"], seq_len, mod_row_of_batch)
    y = _moe(x1, mod3, lw, seq_len, mod_row_of_batch)
    return y.reshape(batch, seq_len, D_MODEL), s_f, s_b


def kernel(x_prompt, x_sample, state_ret_fwd, state_ret_bwd, c, c_ctx, w_ada, b_ada, norm1_g, norm2_g, w_in,
           ret_decay_fwd, ret_decay_bwd, ret_gn_g, w_four_out, w_ret_out, w_out, w_router, router_bias,
           w_exp_gate, w_exp_up, w_exp_down, w_shared_gate, w_shared_up, w_shared_down, final_norm_g):
    depth = w_ada.shape[0]
    assert depth == 1, "final norm is fused into the last layer's MoE kernel"
    n_ctx, n_lat = x_prompt.shape[0], x_sample.shape[0]
    cond = jnp.concatenate([c_ctx[None, :], c], axis=0)
    cond = jnp.pad(cond, ((0, (-cond.shape[0]) % 8), (0, 0)))
    rope = _rope_tables(x_sample.shape[1])
    zeros = jnp.zeros((n_ctx, N_RET_HEADS, RET_HEAD_DIM, RET_HEAD_DIM), F32)

    layer = 0
    mod = _ada(cond, w_ada[layer], b_ada[layer][None, :])
    mod3 = mod.reshape(mod.shape[0], 6, D_MODEL)
    dec = jnp.stack([ret_decay_fwd[layer], ret_decay_bwd[layer]], axis=1)
    lw = {
        "norm1_g": norm1_g[layer][None, :],
        "norm2_g": norm2_g[layer][None, :],
        "w_in": w_in[layer].astype(BF16),
        "dec": jnp.broadcast_to(dec[:, :, None], (N_RET_HEADS, 2, RET_HEAD_DIM)).astype(F32),
        "gn_g": ret_gn_g[layer][None, :],
        "w_four": w_four_out[layer].astype(BF16),
        "w_ret": w_ret_out[layer].astype(BF16),
        "w_o": w_out[layer].astype(BF16),
        "w_router_t": w_router[layer].T,
        "router_bias": router_bias[layer][:, None],
        "weg": w_exp_gate[layer],
        "weu": w_exp_up[layer],
        "wed": w_exp_down[layer],
        "wsg": w_shared_gate[layer].astype(BF16),
        "wsu": w_shared_up[layer].astype(BF16),
        "wsd": w_shared_down[layer].astype(BF16),
        "final_g": final_norm_g[None, :],
    }
    y_prompt, s_f, s_b = _trunk_path(x_prompt, mod3, lambda b: 0, zeros, zeros, None, lw)
    y_sample, _, _ = _trunk_path(x_sample, mod3, lambda b: 1 + b, state_ret_fwd[:, layer],
                                 state_ret_bwd[:, layer], rope, lw)
    return (y_prompt, y_sample, s_f[:, None], s_b[:, None])
```

```python
import functools
import math

import jax
import jax.numpy as jnp
import numpy as np
from jax import lax
from jax.experimental import pallas as pl
from jax.experimental.pallas import tpu as pltpu
from jax.experimental.pallas import tpu_sc as plsc

F32 = jnp.float32
BF16 = jnp.bfloat16

D_MODEL = 1024
GRID_W = 64
N_FOURIER_GROUPS = 8
FOURIER_GROUP_DIM = 128
N_RET_HEADS = 4
RET_HEAD_DIM = 128
RET_WIDTH = N_RET_HEADS * RET_HEAD_DIM
CHUNK = 128
N_EXPERTS = 64
N_EXPERT_GROUPS = 8
EXPERTS_PER_GROUP = N_EXPERTS // N_EXPERT_GROUPS
TOPK_GROUPS = 4
TOP_K = 8
EXPERT_DIM = 256
ROUTED_SCALE = 2.5
ROPE_BASE = 10000.0
EPS = 1e-6
Q_SCALE = RET_HEAD_DIM ** -0.5

_C_UF = (0, 1024)
_C_Q = (1024, 1536)
_C_K = (1536, 2048)
_C_V = (2048, 2560)
_C_G = (2560, 3072)
_C_GF = (3072, 4096)
_C_GR = (4096, 5120)

VMEM_LIMIT = 56 * 1024 * 1024

TM_INPROJ = 1024
FNET_ROWS = 512
TM_ROUTER = 1024
TM_FINAL = 512
EXPERT_ROWS = 512
TILES_PER_STEP = 2
ROW_SLABS = 4
SC_CORES = 2
SC_WORKERS = 32
SC_CHUNK = 128
SC_LANES = 16
SC_COMBINE_TOKENS = 8


def _silu(x):
    return x * jax.nn.sigmoid(x)


def _dot(a, b):
    return jnp.dot(a, b, preferred_element_type=F32)


def _rms_mod(x, g, shift, scale):
    ms = jnp.mean(x * x, axis=-1, keepdims=True)
    y = x * lax.rsqrt(ms + EPS) * g
    return y * (1.0 + scale) + shift


def _ada_kernel(cond_ref, w_ref, b_ref, o_ref):
    s = _silu(cond_ref[...]).astype(BF16)
    o_ref[...] = _dot(s, w_ref[...].astype(BF16)) + b_ref[...]


def _ada(cond, w_ada, b_ada):
    rows, n = cond.shape[0], w_ada.shape[1]
    tn = 1536
    return pl.pallas_call(
        _ada_kernel,
        grid=(n // tn,),
        in_specs=[pl.BlockSpec((rows, D_MODEL), lambda j: (0, 0)),
                  pl.BlockSpec((D_MODEL, tn), lambda j: (0, j)),
                  pl.BlockSpec((1, tn), lambda j: (0, j))],
        out_specs=pl.BlockSpec((rows, tn), lambda j: (0, j)),
        out_shape=jax.ShapeDtypeStruct((rows, n), F32),
        compiler_params=pltpu.CompilerParams(vmem_limit_bytes=VMEM_LIMIT),
        name="ada",
    )(cond, w_ada, b_ada)


def _rope_head(x, cos, sin_signed, first_half):
    partner = jnp.where(first_half, pltpu.roll(x, 96, 1), pltpu.roll(x, 32, 1))
    return x * cos + partner * sin_signed


def _inproj_kernel(*refs, use_rope):
    if use_rope:
        x_ref, mod_ref, g_ref, w_ref, cos_ref, sin_ref = refs[:6]
        outs = refs[6:]
    else:
        x_ref, mod_ref, g_ref, w_ref = refs[:4]
        outs = refs[4:]
    uf_o, q_o, k_o, v_o, sg_o, gf_o, gr_o = outs

    h = _rms_mod(x_ref[...], g_ref[...], mod_ref[0, 0:1, :], mod_ref[0, 1:2, :])
    hb = h.astype(BF16)

    def proj(cols):
        return _dot(hb, w_ref[:, cols[0]:cols[1]])

    uf_o[...] = proj(_C_UF).astype(BF16)
    q = proj(_C_Q)
    k = proj(_C_K)
    if use_rope:
        cos = cos_ref[...]
        sin_signed = sin_ref[...]
        lane = lax.broadcasted_iota(jnp.int32, cos.shape, 1)
        first_half = (lane & 32) == 0
        for hd in range(N_RET_HEADS):
            sl = slice(hd * RET_HEAD_DIM, (hd + 1) * RET_HEAD_DIM)
            q_o[:, sl] = (_rope_head(q[:, sl], cos, sin_signed, first_half) * Q_SCALE).astype(BF16)
            k_o[:, sl] = _rope_head(k[:, sl], cos, sin_signed, first_half).astype(BF16)
    else:
        q_o[...] = (q * Q_SCALE).astype(BF16)
        k_o[...] = k.astype(BF16)
    v_o[...] = proj(_C_V).astype(BF16)
    sg_o[...] = _silu(proj(_C_G)).astype(BF16)
    gf_o[...] = jax.nn.sigmoid(proj(_C_GF)).astype(BF16)
    gr_o[...] = jax.nn.sigmoid(proj(_C_GR)).astype(BF16)


def _inproj(x2d, mod3, norm_g, w_in_bf, seq_len, mod_row_of_batch, rope):
    t = x2d.shape[0]
    tm = TM_INPROJ
    tiles_per_seq = max(seq_len // tm, 1)

    def mod_idx(i):
        return (mod_row_of_batch((i * tm) // seq_len), 0, 0)

    in_specs = [pl.BlockSpec((tm, D_MODEL), lambda i: (i, 0)),
                pl.BlockSpec((1, 6, D_MODEL), mod_idx),
                pl.BlockSpec((1, D_MODEL), lambda i: (0, 0)),
                pl.BlockSpec(w_in_bf.shape, lambda i: (0, 0), pipeline_mode=pl.Buffered(1))]
    args = [x2d, mod3, norm_g, w_in_bf]
    if rope is not None:
        in_specs += [pl.BlockSpec((tm, RET_HEAD_DIM), lambda i: (i % tiles_per_seq, 0))] * 2
        args += list(rope)
    widths = [1024, RET_WIDTH, RET_WIDTH, RET_WIDTH, RET_WIDTH, 1024, 1024]
    return pl.pallas_call(
        functools.partial(_inproj_kernel, use_rope=rope is not None),
        grid=(t // tm,),
        in_specs=in_specs,
        out_specs=[pl.BlockSpec((tm, w), lambda i: (i, 0)) for w in widths],
        out_shape=[jax.ShapeDtypeStruct((t, w), BF16) for w in widths],
        compiler_params=pltpu.CompilerParams(dimension_semantics=("parallel",),
                                             vmem_limit_bytes=VMEM_LIMIT),
        name="inproj",
    )(*args)


def _retention_kernel(q_ref, k_ref, v_ref, sg_ref, dec_ref, gn_ref, s0f_ref, s0b_ref,
                      r_ref, sfo_ref, sbo_ref, tab_scr, gc_scr):
    n_chunks = q_ref.shape[0] // CHUNK
    hd = RET_HEAD_DIM

    @pl.when(pl.program_id(0) == 0)
    def _():
        row = lax.broadcasted_iota(jnp.int32, (CHUNK, CHUNK), 0).astype(F32)
        col = lax.broadcasted_iota(jnp.int32, (CHUNK, CHUNK), 1).astype(F32)
        diff = row - col
        for h in range(N_RET_HEADS):
            dec = dec_ref[h]
            lg = jnp.minimum(dec, 0.0) - jnp.log1p(jnp.exp(-jnp.abs(dec)))
            lgf = lg[0:1, :]
            lgb = lg[1:2, :]
            tab_scr[h, 0] = jnp.exp(jnp.where(diff >= 0, lgf * diff, lgb * (-diff)))
            tab_scr[h, 1] = jnp.exp(lgf * (row + 1.0))
            tab_scr[h, 2] = jnp.exp(lgb * (CHUNK - row))
            tab_scr[h, 3] = jnp.exp(lgf * (CHUNK - 1.0 - col))
            tab_scr[h, 4] = jnp.exp(lgb * col)
            gc_scr[h] = jnp.exp(lg * CHUNK)

    def rows(n):
        return slice(n * CHUNK, (n + 1) * CHUNK)

    for h in range(N_RET_HEADS):
        cols = slice(h * hd, (h + 1) * hd)
        decay, qw_f, qw_b, kwt_f, kwt_b = (tab_scr[h, i] for i in range(5))
        gc = gc_scr[h]
        gc_f = gc[0:1, :]
        gc_b = gc[1:2, :]

        kv_f, kv_b = [], []
        for n in range(n_chunks):
            kt = k_ref[rows(n), cols].astype(F32).T
            vn = v_ref[rows(n), cols]
            kv_f.append(_dot((kt * kwt_f).astype(BF16), vn))
            kv_b.append(_dot((kt * kwt_b).astype(BF16), vn))

        s = s0f_ref[h]
        prev_f = []
        for n in range(n_chunks):
            prev_f.append(s.astype(BF16))
            s = gc_f * s + kv_f[n]
        sfo_ref[h] = s
        s = s0b_ref[h]
        prev_b = [None] * n_chunks
        for n in reversed(range(n_chunks)):
            prev_b[n] = s.astype(BF16)
            s = gc_b * s + kv_b[n]
        sbo_ref[h] = s

        gn = gn_ref[:, cols]
        for n in range(n_chunks):
            qn = q_ref[rows(n), cols]
            qf = qn.astype(F32)
            scores = lax.dot_general(qn, k_ref[rows(n), cols], (((1,), (1,)), ((), ())),
                                     preferred_element_type=F32)
            o = _dot((scores * decay).astype(BF16), v_ref[rows(n), cols])
            o = o + _dot((qf * qw_f).astype(BF16), prev_f[n])
            o = o + _dot((qf * qw_b).astype(BF16), prev_b[n])
            mu = jnp.mean(o, axis=-1, keepdims=True)
            d = o - mu
            var = jnp.mean(d * d, axis=-1, keepdims=True)
            on = d * lax.rsqrt(var + EPS) * gn
            r_ref[rows(n), cols] = (on * sg_ref[rows(n), cols].astype(F32)).astype(BF16)


def _retention(q, k, v, sg, dec, gn_g, s0f, s0b, batch, seq_len):
    hd = RET_HEAD_DIM
    tok_spec = pl.BlockSpec((seq_len, RET_WIDTH), lambda b: (b, 0))
    st_spec = pl.BlockSpec((None, N_RET_HEADS, hd, hd), lambda b: (b, 0, 0, 0))
    st_shape = jax.ShapeDtypeStruct((batch, N_RET_HEADS, hd, hd), F32)
    return pl.pallas_call(
        _retention_kernel,
        grid=(batch,),
        in_specs=[tok_spec, tok_spec, tok_spec, tok_spec,
                  pl.BlockSpec(dec.shape, lambda b: (0, 0, 0)),
                  pl.BlockSpec(gn_g.shape, lambda b: (0, 0)),
                  st_spec, st_spec],
        out_specs=[tok_spec, st_spec, st_spec],
        out_shape=[jax.ShapeDtypeStruct((batch * seq_len, RET_WIDTH), BF16), st_shape, st_shape],
        scratch_shapes=[pltpu.VMEM((N_RET_HEADS, 5, CHUNK, CHUNK), F32),
                        pltpu.VMEM((N_RET_HEADS, 2, hd), F32)],
        compiler_params=pltpu.CompilerParams(dimension_semantics=("arbitrary",),
                                             vmem_limit_bytes=VMEM_LIMIT),
        name="retention",
    )(q, k, v, sg, dec, gn_g, s0f, s0b)


def _fnet_merge_kernel(uf_ref, cs_ref, cls_ref, r_ref, gf_ref, gr_ref, x_ref, mod_ref, wf_ref, wr_ref, wo_ref,
                       o_ref, xcs_ref):
    seq_len = uf_ref.shape[0]
    gd = FOURIER_GROUP_DIM

    @pl.when(pl.program_id(1) == 0)
    def _():
        for g in range(N_FOURIER_GROUPS):
            x = _dot(uf_ref[:, g * gd:(g + 1) * gd], cs_ref[...])
            xcs_ref[0:seq_len, g * gd:(g + 1) * gd] = x[:, :gd].astype(BF16)
            xcs_ref[seq_len:2 * seq_len, g * gd:(g + 1) * gd] = x[:, gd:].astype(BF16)

    f_mix = _dot(cls_ref[...], xcs_ref[...]).astype(BF16)
    f_out = _dot(f_mix, wf_ref[...])
    r_out = _dot(r_ref[...], wr_ref[...])
    merged = gf_ref[...].astype(F32) * f_out + gr_ref[...].astype(F32) * r_out
    mix = _dot(merged.astype(BF16), wo_ref[...])
    o_ref[...] = x_ref[...] + mod_ref[0, 2:3, :] * mix


def _fnet_merge(uf, cs, cls, r, gf, gr, x2d, mod3, w_four, w_ret, w_o, batch, seq_len, mod_row_of_batch):
    rb = min(FNET_ROWS, seq_len)
    nr = seq_len // rb

    def tok(w):
        return pl.BlockSpec((rb, w), lambda b, i: (b * nr + i, 0))

    def full(a):
        return pl.BlockSpec(a.shape, lambda b, i: (0, 0))

    return pl.pallas_call(
        _fnet_merge_kernel,
        grid=(batch, nr),
        in_specs=[pl.BlockSpec((seq_len, D_MODEL), lambda b, i: (b, 0)),
                  full(cs),
                  pl.BlockSpec((rb, 2 * seq_len), lambda b, i: (i, 0)),
                  tok(RET_WIDTH), tok(D_MODEL), tok(D_MODEL), tok(D_MODEL),
                  pl.BlockSpec((1, 6, D_MODEL), lambda b, i: (mod_row_of_batch(b), 0, 0)),
                  full(w_four), full(w_ret), full(w_o)],
        out_specs=tok(D_MODEL),
        out_shape=jax.ShapeDtypeStruct((batch * seq_len, D_MODEL), F32),
        scratch_shapes=[pltpu.VMEM((2 * seq_len, D_MODEL), BF16)],
        compiler_params=pltpu.CompilerParams(dimension_semantics=("parallel", "arbitrary"),
                                             vmem_limit_bytes=VMEM_LIMIT),
        name="fnet_merge",
    )(uf, cs, cls, r, gf, gr, x2d, mod3, w_four, w_ret, w_o)


def _pack_pair(lo_f32, hi_f32):
    lo = lax.bitcast_convert_type(lo_f32.astype(BF16).astype(F32), jnp.uint32)
    hi = lax.bitcast_convert_type(hi_f32.astype(BF16).astype(F32), jnp.uint32)
    return lax.bitcast_convert_type((lo >> 16) | hi, jnp.int32)


def _unpack_pair(words_i32):
    w = lax.bitcast_convert_type(words_i32, jnp.uint32)
    lo = lax.bitcast_convert_type(w << 16, F32)
    hi = lax.bitcast_convert_type(w & jnp.uint32(0xFFFF0000), F32)
    return lo, hi


def _load_token_words(ref, lead, n_tok):
    parts = []
    for s in range(ROW_SLABS):
        idx = (pl.ds(s, n_tok, stride=ROW_SLABS), slice(None))
        parts.append(ref[lead + idx] if lead else ref[idx])
    return jnp.concatenate(parts, axis=1)


def _store_token_words(ref, words, n_tok):
    for s in range(ROW_SLABS):
        ref[pl.ds(s, n_tok, stride=ROW_SLABS), :] = words[:, s * 128:(s + 1) * 128]


def _route(scores, biased):
    tokens = scores.shape[1]
    neg = -jnp.inf
    epg = EXPERTS_PER_GROUP
    iota_g = lax.broadcasted_iota(jnp.int32, (epg, tokens), 0).astype(F32)

    def pick_first_max(cur, iota, size):
        m = jnp.max(cur, axis=0, keepdims=True)
        idx = jnp.min(jnp.where(cur == m, iota, float(size)), axis=0, keepdims=True)
        return m, idx, iota == idx

    group_scores = []
    for g in range(N_EXPERT_GROUPS):
        vals = biased[g * epg:(g + 1) * epg, :]
        m1, _, hit = pick_first_max(vals, iota_g, epg)
        m2 = jnp.max(jnp.where(hit, neg, vals), axis=0, keepdims=True)
        group_scores.append(m1 + m2)
    cur = jnp.concatenate(group_scores, axis=0)
    group_sel = jnp.zeros_like(cur)
    for _ in range(TOPK_GROUPS):
        _, _, hit = pick_first_max(cur, iota_g, N_EXPERT_GROUPS)
        group_sel = jnp.where(hit, 1.0, group_sel)
        cur = jnp.where(hit, neg, cur)
    masked = jnp.concatenate(
        [jnp.where(group_sel[g:g + 1, :] > 0.0, biased[g * epg:(g + 1) * epg, :], neg)
         for g in range(N_EXPERT_GROUPS)], axis=0)
    iota_e = lax.broadcasted_iota(jnp.int32, masked.shape, 0).astype(F32)
    sel = jnp.zeros_like(masked)
    cur = masked
    picks = []
    for _ in range(TOP_K):
        _, idx, hit = pick_first_max(cur, iota_e, N_EXPERTS)
        picks.append(idx)
        sel = jnp.where(hit, 1.0, sel)
        cur = jnp.where(hit, neg, cur)
    w = scores * sel
    return w / jnp.sum(w, axis=0, keepdims=True) * ROUTED_SCALE, sel, picks


def _router_kernel(x_ref, mod_ref, g2_ref, wrt_ref, rb_ref, hp_ref, ek_ref, rk_ref, wt_ref, cnt_ref, run_scr):
    tm = x_ref.shape[0]

    @pl.when(pl.program_id(0) == 0)
    def _():
        run_scr[...] = jnp.zeros_like(run_scr)

    h = _rms_mod(x_ref[...], g2_ref[...], mod_ref[0, 3:4, :], mod_ref[0, 4:5, :])
    half = D_MODEL // 2
    _store_token_words(hp_ref, _pack_pair(h[:, :half], h[:, half:]), tm)

    logits_t = lax.dot_general(wrt_ref[...], h, (((1,), (1,)), ((), ())),
                               precision=lax.Precision.HIGHEST, preferred_element_type=F32)
    scores = jax.nn.sigmoid(logits_t)
    comb_t, sel, picks = _route(scores, scores + rb_ref[...])

    earlier = (lax.broadcasted_iota(jnp.int32, (tm, tm), 0) < lax.broadcasted_iota(jnp.int32, (tm, tm), 1))
    rank_t = _dot(sel.astype(BF16), jnp.where(earlier, 1.0, 0.0).astype(BF16)) + run_scr[...]
    run_scr[...] += jnp.sum(sel, axis=1, keepdims=True)
    cnt_ref[...] = jnp.broadcast_to(run_scr[...], cnt_ref.shape)

    iota_e = lax.broadcasted_iota(jnp.int32, sel.shape, 0).astype(F32)
    ranks, weights = [], []
    for idx in picks:
        hit = iota_e == idx
        ranks.append(jnp.sum(jnp.where(hit, rank_t, 0.0), axis=0, keepdims=True))
        weights.append(jnp.sum(jnp.where(hit, comb_t, 0.0), axis=0, keepdims=True))
    ek_ref[...] = jnp.concatenate(picks, axis=0).astype(jnp.int32)
    rk_ref[...] = jnp.concatenate(ranks, axis=0).astype(jnp.int32)
    w_rep = jnp.concatenate([jnp.broadcast_to(w, (SC_LANES, tm)) for w in weights], axis=0)
    wt_ref[...] = w_rep.T


def _router(x1, mod3, norm2_g, w_router_t, router_bias, seq_len, mod_row_of_batch):
    t = x1.shape[0]
    tm = TM_ROUTER

    def mod_idx(i):
        return (mod_row_of_batch((i * tm) // seq_len), 0, 0)

    def full(a):
        return pl.BlockSpec(a.shape, lambda i: (0,) * a.ndim)

    return pl.pallas_call(
        _router_kernel,
        grid=(t // tm,),
        in_specs=[pl.BlockSpec((tm, D_MODEL), lambda i: (i, 0)),
                  pl.BlockSpec((1, 6, D_MODEL), mod_idx),
                  full(norm2_g), full(w_router_t), full(router_bias)],
        out_specs=[pl.BlockSpec((tm * ROW_SLABS, 128), lambda i: (i, 0)),
                   pl.BlockSpec((TOP_K, tm), lambda i: (0, i)),
                   pl.BlockSpec((TOP_K, tm), lambda i: (0, i)),
                   pl.BlockSpec((tm, 128), lambda i: (i, 0)),
                   pl.BlockSpec((N_EXPERTS, 128), lambda i: (0, 0))],
        out_shape=[jax.ShapeDtypeStruct((t * ROW_SLABS, 128), jnp.int32),
                   jax.ShapeDtypeStruct((TOP_K, t), jnp.int32),
                   jax.ShapeDtypeStruct((TOP_K, t), jnp.int32),
                   jax.ShapeDtypeStruct((t, 128), F32),
                   jax.ShapeDtypeStruct((N_EXPERTS, 128), F32)],
        scratch_shapes=[pltpu.VMEM((N_EXPERTS, 1), F32)],
        compiler_params=pltpu.CompilerParams(dimension_semantics=("arbitrary",),
                                             vmem_limit_bytes=VMEM_LIMIT),
        name="router",
    )(x1, mod3, norm2_g, w_router_t, router_bias)


def _plan_kernel(ek_ref, rk_ref, cnt_ref, pos_ref, texp_ref, nused_ref, tend_ref):
    rows = float(EXPERT_ROWS)
    cnt = cnt_ref[:, 0:1]
    tiles = jnp.floor((cnt + (rows - 1.0)) / rows)
    before = (lax.broadcasted_iota(jnp.int32, (N_EXPERTS, N_EXPERTS), 1)
              < lax.broadcasted_iota(jnp.int32, (N_EXPERTS, N_EXPERTS), 0))
    tile_start = jnp.dot(jnp.where(before, 1.0, 0.0), jnp.broadcast_to(tiles, (N_EXPERTS, 128)),
                         precision=lax.Precision.HIGHEST, preferred_element_type=F32)[:, 0:1]
    tile_end = tile_start + tiles
    row_start = tile_start * rows

    ek = ek_ref[...]
    pos = rk_ref[...].astype(F32)
    tile_id = lax.broadcasted_iota(jnp.int32, texp_ref.shape, 1).astype(F32)
    texp = jnp.zeros(texp_ref.shape, F32)
    for e in range(N_EXPERTS):
        pos = pos + jnp.where(ek == e, row_start[e:e + 1, :], 0.0)
        texp = texp + jnp.where(tile_id >= tile_end[e:e + 1, :], 1.0, 0.0)
    pos_ref[...] = pos.astype(jnp.int32)
    texp_ref[...] = jnp.minimum(texp, N_EXPERTS - 1.0).astype(jnp.int32)
    nused_ref[...] = jnp.broadcast_to(tile_end[N_EXPERTS - 1:N_EXPERTS, :], nused_ref.shape).astype(jnp.int32)
    tend_ref[...] = jnp.broadcast_to(tile_end, tend_ref.shape).astype(jnp.int32)


def _plan(ek, rk, cnt, n_tiles_pad):
    t = ek.shape[1]

    def full(shape):
        return pl.BlockSpec(shape, lambda: (0,) * len(shape))

    return pl.pallas_call(
        _plan_kernel,
        in_specs=[full(ek.shape), full(rk.shape), full(cnt.shape)],
        out_specs=[full((TOP_K, t)), full((1, n_tiles_pad)), full((1, 128)), full((N_EXPERTS, 128))],
        out_shape=[jax.ShapeDtypeStruct((TOP_K, t), jnp.int32),
                   jax.ShapeDtypeStruct((1, n_tiles_pad), jnp.int32),
                   jax.ShapeDtypeStruct((1, 128), jnp.int32),
                   jax.ShapeDtypeStruct((N_EXPERTS, 128), jnp.int32)],
        compiler_params=pltpu.CompilerParams(vmem_limit_bytes=VMEM_LIMIT),
        name="plan",
    )(ek, rk, cnt)


def _sc_mesh():
    return plsc.VectorSubcoreMesh(core_axis_name="c", subcore_axis_name="s")


def _sc_dispatch(rows, pos3, n_out):
    t = rows.shape[0]
    ch = SC_CHUNK
    per_w = (t // ch) // SC_WORKERS

    @functools.partial(
        pl.kernel, out_type=jax.ShapeDtypeStruct((n_out,) + rows.shape[1:], jnp.int32), mesh=_sc_mesh(),
        scratch_types=[pltpu.VMEM((TOP_K, ch), jnp.int32), pltpu.VMEM((ch,) + rows.shape[1:], jnp.int32),
                       pltpu.SemaphoreType.DMA])
    def k(rows_hbm, pos_hbm, out_hbm, idx_v, rows_v, sem):
        wid = lax.axis_index("s") * SC_CORES + lax.axis_index("c")

        @pl.loop(0, per_w)
        def _(j):
            c = wid * per_w + j
            pltpu.sync_copy(pos_hbm.at[c], idx_v)
            pltpu.sync_copy(rows_hbm.at[pl.ds(c * ch, ch)], rows_v)
            copies = [pltpu.async_copy(rows_v, out_hbm.at[idx_v.at[kk]], sem) for kk in range(TOP_K)]
            for cp in copies:
                cp.wait()

    return k(rows, pos3)


def _sc_combine(table, pos3, wtok, t):
    ch = SC_CHUNK
    sub = SC_COMBINE_TOKENS
    lanes = SC_LANES
    slabs = ROW_SLABS
    per_w = (t // ch) // SC_WORKERS
    subs_per_chunk = ch // sub
    n_steps = per_w * subs_per_chunk

    @functools.partial(
        pl.kernel, out_type=jax.ShapeDtypeStruct((t, 2 * slabs, 128), F32), mesh=_sc_mesh(),
        scratch_types=[pltpu.VMEM((per_w, TOP_K, ch), jnp.int32),
                       pltpu.VMEM((2, TOP_K, sub, slabs, 128), jnp.int32),
                       pltpu.VMEM((2, sub, 128), F32),
                       pltpu.VMEM((sub, 2 * slabs, 128), F32),
                       pltpu.SemaphoreType.DMA((2,))],
        compiler_params=pltpu.CompilerParams(needs_layout_passes=False))
    def k(tab_hbm, pos_hbm, w_hbm, out_hbm, idx_v, rows_v, w_v, out_v, sem):
        wid = lax.axis_index("s") * SC_CORES + lax.axis_index("c")
        for j in range(per_w):
            pltpu.sync_copy(pos_hbm.at[wid * per_w + j], idx_v.at[j])

        def first_token(step):
            return (wid * per_w + step // subs_per_chunk) * ch + (step % subs_per_chunk) * sub

        def copies(step, slot):
            j = step // subs_per_chunk
            s = step % subs_per_chunk
            idx = [idx_v.at[j, kk, pl.ds(s * sub, sub)] for kk in range(TOP_K)]
            return ([pltpu.make_async_copy(tab_hbm.at[idx[kk]], rows_v.at[slot, kk], sem.at[slot])
                     for kk in range(TOP_K)]
                    + [pltpu.make_async_copy(w_hbm.at[pl.ds(first_token(step), sub)], w_v.at[slot], sem.at[slot])])

        for cp in copies(0, 0):
            cp.start()

        @pl.loop(0, n_steps)
        def _(step):
            slot = step % 2

            @pl.when(step + 1 < n_steps)
            def _():
                for cp in copies(step + 1, 1 - slot):
                    cp.start()

            for cp in copies(step, slot):
                cp.wait()

            @pl.loop(0, sub)
            def _(tt):
                wk = [w_v[slot, tt, pl.ds(kk * lanes, lanes)] for kk in range(TOP_K)]
                for sl in range(slabs):
                    @plsc.parallel_loop(0, 128, step=lanes, unroll=4)
                    def _(off):
                        acc_lo = jnp.zeros((lanes,), F32)
                        acc_hi = jnp.zeros((lanes,), F32)
                        for kk in range(TOP_K):
                            word = rows_v[slot, kk, tt, sl, pl.ds(off, lanes)]
                            lo = plsc.bitcast(word << 16, F32)
                            hi = plsc.bitcast(word & jnp.int32(-65536), F32)
                            acc_lo = acc_lo + wk[kk] * lo
                            acc_hi = acc_hi + wk[kk] * hi
                        out_v[tt, sl, pl.ds(off, lanes)] = acc_lo
                        out_v[tt, slabs + sl, pl.ds(off, lanes)] = acc_hi

            pltpu.sync_copy(out_v, out_hbm.at[pl.ds(first_token(step), sub)])

    return k(table, pos3, wtok)


def _experts_kernel(texp_ref, nused_ref, tend_ref, xs_ref, weg_hbm, weu_hbm, wed_hbm, ys_ref,
                    wg_scr, wu_scr, wd_scr, wg_buf, wu_buf, wd_buf, sem, group_scr):
    step = pl.program_id(0)
    rows = EXPERT_ROWS
    half = D_MODEL // 2
    n_used = nused_ref[0]

    def weight_copies(e, slot):
        return [pltpu.make_async_copy(weg_hbm.at[e], wg_buf.at[slot], sem.at[slot, 0]),
                pltpu.make_async_copy(weu_hbm.at[e], wu_buf.at[slot], sem.at[slot, 1]),
                pltpu.make_async_copy(wed_hbm.at[e], wd_buf.at[slot], sem.at[slot, 2])]

    @pl.when(step == 0)
    def _():
        group_scr[0] = 0
        for cp in weight_copies(texp_ref[0], 0):
            cp.start()

    def row_tile(tile, x_view, y_view):
        expert = texp_ref[tile]
        used = tile < n_used
        new_expert = (tile == 0) | (expert != texp_ref[jnp.maximum(tile - 1, 0)])

        @pl.when(used & new_expert)
        def _():
            group = group_scr[0]
            slot = group % 2
            next_tile = tend_ref[expert]

            @pl.when(next_tile < n_used)
            def _():
                for cp in weight_copies(texp_ref[next_tile], 1 - slot):
                    cp.start()

            for cp in weight_copies(expert, slot):
                cp.wait()
            wg_scr[...] = wg_buf[slot].astype(BF16)
            wu_scr[...] = wu_buf[slot].astype(BF16)
            wd_scr[...] = wd_buf[slot].astype(BF16)
            group_scr[0] = group + 1

        @pl.when(used)
        def _():
            lo, hi = _unpack_pair(_load_token_words(x_view, (), rows))
            lo = lo.astype(BF16)
            hi = hi.astype(BF16)
            g = _dot(lo, wg_scr[0:half, :]) + _dot(hi, wg_scr[half:D_MODEL, :])
            u = _dot(lo, wu_scr[0:half, :]) + _dot(hi, wu_scr[half:D_MODEL, :])
            y = _dot((_silu(g) * u).astype(BF16), wd_scr[...])
            _store_token_words(y_view, _pack_pair(y[:, :half], y[:, half:]), rows)

        @pl.when(jnp.logical_not(used) & (step == (n_used - 1) // TILES_PER_STEP))
        def _():
            y_view[...] = jnp.zeros_like(y_view)

    for s in range(TILES_PER_STEP):
        view = pl.ds(s * rows * ROW_SLABS, rows * ROW_SLABS)
        row_tile(step * TILES_PER_STEP + s, xs_ref.at[view], ys_ref.at[view])


def _experts(texp, nused, tend, xs2d, weg, weu, wed, n_tiles):
    block = (TILES_PER_STEP * EXPERT_ROWS * ROW_SLABS, 128)
    hbm = pl.BlockSpec(memory_space=pl.ANY)

    def block_idx(j, te, nu, tn):
        return (jnp.minimum(j, (nu[0] - 1) // TILES_PER_STEP), 0)

    grid_spec = pltpu.PrefetchScalarGridSpec(
        num_scalar_prefetch=3,
        grid=(n_tiles // TILES_PER_STEP,),
        in_specs=[pl.BlockSpec(block, block_idx), hbm, hbm, hbm],
        out_specs=pl.BlockSpec(block, block_idx),
        scratch_shapes=[pltpu.VMEM((D_MODEL, EXPERT_DIM), BF16),
                        pltpu.VMEM((D_MODEL, EXPERT_DIM), BF16),
                        pltpu.VMEM((EXPERT_DIM, D_MODEL), BF16),
                        pltpu.VMEM((2, D_MODEL, EXPERT_DIM), F32),
                        pltpu.VMEM((2, D_MODEL, EXPERT_DIM), F32),
                        pltpu.VMEM((2, EXPERT_DIM, D_MODEL), F32),
                        pltpu.SemaphoreType.DMA((2, 3)),
                        pltpu.SMEM((1,), jnp.int32)],
    )
    return pl.pallas_call(
        _experts_kernel,
        grid_spec=grid_spec,
        out_shape=jax.ShapeDtypeStruct(xs2d.shape, jnp.int32),
        compiler_params=pltpu.CompilerParams(dimension_semantics=("arbitrary",),
                                             vmem_limit_bytes=VMEM_LIMIT),
        name="experts",
    )(texp, nused, tend, xs2d, weg, weu, wed)


def _final_kernel(x_ref, routed_ref, mod_ref, g2_ref, wsg_ref, wsu_ref, wsd_ref, fng_ref, o_ref):
    tm = x_ref.shape[0]
    x = x_ref[...]
    hb = _rms_mod(x, g2_ref[...], mod_ref[0, 3:4, :], mod_ref[0, 4:5, :]).astype(BF16)
    shared = _dot((_silu(_dot(hb, wsg_ref[...])) * _dot(hb, wsu_ref[...])).astype(BF16), wsd_ref[...])
    n_slabs = 2 * ROW_SLABS
    routed = jnp.concatenate([routed_ref[pl.ds(s, tm, stride=n_slabs), :] for s in range(n_slabs)], axis=1)
    y = x + mod_ref[0, 5:6, :] * (routed + shared)
    ms = jnp.mean(y * y, axis=-1, keepdims=True)
    o_ref[...] = y * lax.rsqrt(ms + EPS) * fng_ref[...]


def _final(x1, routed2d, mod3, norm2_g, wsg, wsu, wsd, final_g, seq_len, mod_row_of_batch):
    t = x1.shape[0]
    tm = TM_FINAL

    def mod_idx(i):
        return (mod_row_of_batch((i * tm) // seq_len), 0, 0)

    def full(a):
        return pl.BlockSpec(a.shape, lambda i: (0,) * a.ndim)

    return pl.pallas_call(
        _final_kernel,
        grid=(t // tm,),
        in_specs=[pl.BlockSpec((tm, D_MODEL), lambda i: (i, 0)),
                  pl.BlockSpec((tm * 2 * ROW_SLABS, 128), lambda i: (i, 0)),
                  pl.BlockSpec((1, 6, D_MODEL), mod_idx),
                  full(norm2_g), full(wsg), full(wsu), full(wsd), full(final_g)],
        out_specs=pl.BlockSpec((tm, D_MODEL), lambda i: (i, 0)),
        out_shape=jax.ShapeDtypeStruct((t, D_MODEL), F32),
        compiler_params=pltpu.CompilerParams(dimension_semantics=("parallel",),
                                             vmem_limit_bytes=VMEM_LIMIT),
        name="final",
    )(x1, routed2d, mod3, norm2_g, wsg, wsu, wsd, final_g)


def _moe(x1, mod3, lw, seq_len, mod_row_of_batch):
    t = x1.shape[0]
    n_tiles = TOP_K * t // EXPERT_ROWS + N_EXPERTS
    n_tiles_pad = -(-n_tiles // 128) * 128
    hp2d, ek, rk, wtok, cnt = _router(x1, mod3, lw["norm2_g"], lw["w_router_t"], lw["router_bias"],
                                      seq_len, mod_row_of_batch)
    pos, texp, nused, tend = _plan(ek, rk, cnt, n_tiles_pad)
    pos3 = pos.reshape(TOP_K, t // SC_CHUNK, SC_CHUNK).transpose(1, 0, 2)
    xs = _sc_dispatch(hp2d.reshape(t, ROW_SLABS, 128), pos3, n_tiles * EXPERT_ROWS)
    ys2d = _experts(texp.reshape(-1), nused.reshape(-1), tend[:, 0], xs.reshape(-1, 128),
                    lw["weg"], lw["weu"], lw["wed"], n_tiles)
    routed = _sc_combine(ys2d.reshape(-1, ROW_SLABS, 128), pos3, wtok, t)
    return _final(x1, routed.reshape(t * 2 * ROW_SLABS, 128), mod3, lw["norm2_g"],
                  lw["wsg"], lw["wsu"], lw["wsd"], lw["final_g"], seq_len, mod_row_of_batch)


def _dft_tables(seq_len):
    gd = FOURIER_GROUP_DIM
    kc = np.arange(gd)
    ang_c = ((kc[:, None] * kc[None, :]) % gd) * (2.0 * math.pi / gd)
    cs = np.concatenate([np.cos(ang_c), np.sin(ang_c)], axis=1) * (gd ** -0.5)
    kl = np.arange(seq_len)
    ang_l = ((kl[:, None] * kl[None, :]) % seq_len) * (2.0 * math.pi / seq_len)
    cls = np.concatenate([np.cos(ang_l), -np.sin(ang_l)], axis=1) * (seq_len ** -0.5)
    return jnp.asarray(cs.astype(np.float32), dtype=BF16), jnp.asarray(cls.astype(np.float32), dtype=BF16)


def _rope_tables(length):
    rows = length // GRID_W
    r = np.repeat(np.arange(rows, dtype=np.float32), GRID_W)
    col = np.tile(np.arange(GRID_W, dtype=np.float32), rows)
    nf = RET_HEAD_DIM // 4
    inv = (np.float32(ROPE_BASE) ** (-np.arange(nf, dtype=np.float32) / np.float32(nf))).astype(np.float32)
    ar = r[:, None] * inv[None]
    ac = col[:, None] * inv[None]
    ang = np.concatenate([ar, ar, ac, ac], axis=-1).astype(np.float64)
    sign = np.where((np.arange(RET_HEAD_DIM) & nf) == 0, -1.0, 1.0)
    return (jnp.asarray(np.cos(ang).astype(np.float32)),
            jnp.asarray((np.sin(ang) * sign[None, :]).astype(np.float32)))


def _trunk_path(x, mod3, mod_row_of_batch, s0f, s0b, rope, lw):
    batch, seq_len, _ = x.shape
    x2d = x.reshape(batch * seq_len, D_MODEL)
    uf, q, k, v, sg, gf, gr = _inproj(x2d, mod3, lw["norm1_g"], lw["w_in"], seq_len, mod_row_of_batch, rope)
    r, s_f, s_b = _retention(q, k, v, sg, lw["dec"], lw["gn_g"], s0f, s0b, batch, seq_len)
    cs, cls = _dft_tables(seq_len)
    x1 = _fnet_merge(uf, cs, cls, r, gf, gr, x2d, mod3, lw["w_four"], lw["w_ret"], lw["w_o"],
                     batch, seq_len, mod_row_of_batch)
    y = _moe(x1, mod3, lw, seq_len, mod_row_of_batch)
    return y.reshape(batch, seq_len, D_MODEL), s_f, s_b


def kernel(x_prompt, x_sample, state_ret_fwd, state_ret_bwd, c, c_ctx, w_ada, b_ada, norm1_g, norm2_g, w_in,
           ret_decay_fwd, ret_decay_bwd, ret_gn_g, w_four_out, w_ret_out, w_out, w_router, router_bias,
           w_exp_gate, w_exp_up, w_exp_down, w_shared_gate, w_shared_up, w_shared_down, final_norm_g):
    depth = w_ada.shape[0]
    assert depth == 1, "final norm is fused into the last layer's MoE kernel"
    n_ctx, n_lat = x_prompt.shape[0], x_sample.shape[0]
    cond = jnp.concatenate([c_ctx[None, :], c], axis=0)
    cond = jnp.pad(cond, ((0, (-cond.shape[0]) % 8), (0, 0)))
    rope = _rope_tables(x_sample.shape[1])
    zeros = jnp.zeros((n_ctx, N_RET_HEADS, RET_HEAD_DIM, RET_HEAD_DIM), F32)

    layer = 0
    mod = _ada(cond, w_ada[layer], b_ada[layer][None, :])
    mod3 = mod.reshape(mod.shape[0], 6, D_MODEL)
    dec = jnp.stack([ret_decay_fwd[layer], ret_decay_bwd[layer]], axis=1)
    lw = {
        "norm1_g": norm1_g[layer][None, :],
        "norm2_g": norm2_g[layer][None, :],
        "w_in": w_in[layer].astype(BF16),
        "dec": jnp.broadcast_to(dec[:, :, None], (N_RET_HEADS, 2, RET_HEAD_DIM)).astype(F32),
        "gn_g": ret_gn_g[layer][None, :],
        "w_four": w_four_out[layer].astype(BF16),
        "w_ret": w_ret_out[layer].astype(BF16),
        "w_o": w_out[layer].astype(BF16),
        "w_router_t": w_router[layer].T,
        "router_bias": router_bias[layer][:, None],
        "weg": w_exp_gate[layer],
        "weu": w_exp_up[layer],
        "wed": w_exp_down[layer],
        "wsg": w_shared_gate[layer].astype(BF16),
        "wsu": w_shared_up[layer].astype(BF16),
        "wsd": w_shared_down[layer].astype(BF16),
        "final_g": final_norm_g[None, :],
    }
    y_prompt, s_f, s_b = _trunk_path(x_prompt, mod3, lambda b: 0, zeros, zeros, None, lw)
    y_sample, _, _ = _trunk_path(x_sample, mod3, lambda b: 1 + b, state_ret_fwd[:, layer],
                                 state_ret_bwd[:, layer], rope, lw)
    return (y_prompt, y_sample, s_f[:, None], s_b[:, None])
```

```python
import functools
import math

import jax
import jax.numpy as jnp
import numpy as np
from jax import lax
from jax.experimental import pallas as pl
from jax.experimental.pallas import tpu as pltpu
from jax.experimental.pallas import tpu_sc as plsc

F32 = jnp.float32
BF16 = jnp.bfloat16

D_MODEL = 1024
GRID_W = 64
N_FOURIER_GROUPS = 8
FOURIER_GROUP_DIM = 128
N_RET_HEADS = 4
RET_HEAD_DIM = 128
RET_WIDTH = N_RET_HEADS * RET_HEAD_DIM
CHUNK = 128
N_EXPERTS = 64
N_EXPERT_GROUPS = 8
EXPERTS_PER_GROUP = N_EXPERTS // N_EXPERT_GROUPS
TOPK_GROUPS = 4
TOP_K = 8
EXPERT_DIM = 256
ROUTED_SCALE = 2.5
ROPE_BASE = 10000.0
EPS = 1e-6
Q_SCALE = RET_HEAD_DIM ** -0.5

_C_UF = (0, 1024)
_C_Q = (1024, 1536)
_C_K = (1536, 2048)
_C_V = (2048, 2560)
_C_G = (2560, 3072)
_C_GF = (3072, 4096)
_C_GR = (4096, 5120)

VMEM_LIMIT = 56 * 1024 * 1024

TM_INPROJ = 1024
TM_ROUTER = 1024
FNET_ROWS = 512
TM_FINAL = 1024
EXPERT_ROWS = 512
TILES_PER_STEP = 2
ROW_SLABS = 4
SC_CORES = 2
SC_WORKERS = 32
SC_CHUNK = 128
SC_LANES = 16
SC_COMBINE_TOKENS = 8


def _silu(x):
    return x * jax.nn.sigmoid(x)


def _dot(a, b):
    return jnp.dot(a, b, preferred_element_type=F32)


def _rms_mod(x, g, shift, scale):
    ms = jnp.mean(x * x, axis=-1, keepdims=True)
    y = x * lax.rsqrt(ms + EPS) * g
    return y * (1.0 + scale) + shift


def _ada_kernel(cond_ref, w_ref, b_ref, o_ref):
    s = _silu(cond_ref[...]).astype(BF16)
    o_ref[...] = _dot(s, w_ref[...].astype(BF16)) + b_ref[...]


def _ada(cond, w_ada, b_ada):
    rows, n = cond.shape[0], w_ada.shape[1]
    tn = 1536
    return pl.pallas_call(
        _ada_kernel,
        grid=(n // tn,),
        in_specs=[pl.BlockSpec((rows, D_MODEL), lambda j: (0, 0)),
                  pl.BlockSpec((D_MODEL, tn), lambda j: (0, j)),
                  pl.BlockSpec((1, tn), lambda j: (0, j))],
        out_specs=pl.BlockSpec((rows, tn), lambda j: (0, j)),
        out_shape=jax.ShapeDtypeStruct((rows, n), F32),
        compiler_params=pltpu.CompilerParams(vmem_limit_bytes=VMEM_LIMIT),
        name="ada",
    )(cond, w_ada, b_ada)


def _rope_head(x, cos, sin_signed, first_half):
    partner = jnp.where(first_half, pltpu.roll(x, 96, 1), pltpu.roll(x, 32, 1))
    return x * cos + partner * sin_signed


def _inproj_kernel(*refs, use_rope):
    if use_rope:
        x_ref, mod_ref, g_ref, w_ref, cos_ref, sin_ref = refs[:6]
        outs = refs[6:]
    else:
        x_ref, mod_ref, g_ref, w_ref = refs[:4]
        outs = refs[4:]
    uf_o, q_o, k_o, v_o, sg_o, gf_o, gr_o = outs

    h = _rms_mod(x_ref[...], g_ref[...], mod_ref[0, 0:1, :], mod_ref[0, 1:2, :])
    hb = h.astype(BF16)

    def proj(cols):
        return _dot(hb, w_ref[:, cols[0]:cols[1]])

    uf_o[...] = proj(_C_UF).astype(BF16)
    q = proj(_C_Q)
    k = proj(_C_K)
    if use_rope:
        cos = cos_ref[...]
        sin_signed = sin_ref[...]
        lane = lax.broadcasted_iota(jnp.int32, cos.shape, 1)
        first_half = (lane & 32) == 0
        for hd in range(N_RET_HEADS):
            sl = slice(hd * RET_HEAD_DIM, (hd + 1) * RET_HEAD_DIM)
            q_o[:, sl] = (_rope_head(q[:, sl], cos, sin_signed, first_half) * Q_SCALE).astype(BF16)
            k_o[:, sl] = _rope_head(k[:, sl], cos, sin_signed, first_half).astype(BF16)
    else:
        q_o[...] = (q * Q_SCALE).astype(BF16)
        k_o[...] = k.astype(BF16)
    v_o[...] = proj(_C_V).astype(BF16)
    sg_o[...] = _silu(proj(_C_G)).astype(BF16)
    gf_o[...] = jax.nn.sigmoid(proj(_C_GF)).astype(BF16)
    gr_o[...] = jax.nn.sigmoid(proj(_C_GR)).astype(BF16)


def _inproj(x2d, mod3, norm_g, w_in_bf, seq_len, mod_row_of_batch, rope):
    t = x2d.shape[0]
    tm = TM_INPROJ
    tiles_per_seq = max(seq_len // tm, 1)

    def mod_idx(i):
        return (mod_row_of_batch((i * tm) // seq_len), 0, 0)

    in_specs = [pl.BlockSpec((tm, D_MODEL), lambda i: (i, 0)),
                pl.BlockSpec((1, 6, D_MODEL), mod_idx),
                pl.BlockSpec((1, D_MODEL), lambda i: (0, 0)),
                pl.BlockSpec(w_in_bf.shape, lambda i: (0, 0), pipeline_mode=pl.Buffered(1))]
    args = [x2d, mod3, norm_g, w_in_bf]
    if rope is not None:
        in_specs += [pl.BlockSpec((tm, RET_HEAD_DIM), lambda i: (i % tiles_per_seq, 0))] * 2
        args += list(rope)
    widths = [1024, RET_WIDTH, RET_WIDTH, RET_WIDTH, RET_WIDTH, 1024, 1024]
    return pl.pallas_call(
        functools.partial(_inproj_kernel, use_rope=rope is not None),
        grid=(t // tm,),
        in_specs=in_specs,
        out_specs=[pl.BlockSpec((tm, w), lambda i: (i, 0)) for w in widths],
        out_shape=[jax.ShapeDtypeStruct((t, w), BF16) for w in widths],
        compiler_params=pltpu.CompilerParams(dimension_semantics=("parallel",),
                                             vmem_limit_bytes=VMEM_LIMIT),
        name="inproj",
    )(*args)


def _retention_kernel(q_ref, k_ref, v_ref, sg_ref, dec_ref, gn_ref, s0f_ref, s0b_ref,
                      r_ref, sfo_ref, sbo_ref, tab_scr, gc_scr):
    n_chunks = q_ref.shape[0] // CHUNK
    hd = RET_HEAD_DIM

    @pl.when(pl.program_id(0) == 0)
    def _():
        row = lax.broadcasted_iota(jnp.int32, (CHUNK, CHUNK), 0).astype(F32)
        col = lax.broadcasted_iota(jnp.int32, (CHUNK, CHUNK), 1).astype(F32)
        diff = row - col
        for h in range(N_RET_HEADS):
            dec = dec_ref[h]
            lg = jnp.minimum(dec, 0.0) - jnp.log1p(jnp.exp(-jnp.abs(dec)))
            lgf = lg[0:1, :]
            lgb = lg[1:2, :]
            tab_scr[h, 0] = jnp.exp(jnp.where(diff >= 0, lgf * diff, lgb * (-diff)))
            tab_scr[h, 1] = jnp.exp(lgf * (row + 1.0))
            tab_scr[h, 2] = jnp.exp(lgb * (CHUNK - row))
            tab_scr[h, 3] = jnp.exp(lgf * (CHUNK - 1.0 - col))
            tab_scr[h, 4] = jnp.exp(lgb * col)
            gc_scr[h] = jnp.exp(lg * CHUNK)

    def rows(n):
        return slice(n * CHUNK, (n + 1) * CHUNK)

    for h in range(N_RET_HEADS):
        cols = slice(h * hd, (h + 1) * hd)
        decay, qw_f, qw_b, kwt_f, kwt_b = (tab_scr[h, i] for i in range(5))
        gc = gc_scr[h]
        gc_f = gc[0:1, :]
        gc_b = gc[1:2, :]

        kv_f, kv_b = [], []
        for n in range(n_chunks):
            kt = k_ref[rows(n), cols].astype(F32).T
            vn = v_ref[rows(n), cols]
            kv_f.append(_dot((kt * kwt_f).astype(BF16), vn))
            kv_b.append(_dot((kt * kwt_b).astype(BF16), vn))

        s = s0f_ref[h]
        prev_f = []
        for n in range(n_chunks):
            prev_f.append(s.astype(BF16))
            s = gc_f * s + kv_f[n]
        sfo_ref[h] = s
        s = s0b_ref[h]
        prev_b = [None] * n_chunks
        for n in reversed(range(n_chunks)):
            prev_b[n] = s.astype(BF16)
            s = gc_b * s + kv_b[n]
        sbo_ref[h] = s

        gn = gn_ref[:, cols]
        for n in range(n_chunks):
            qn = q_ref[rows(n), cols]
            qf = qn.astype(F32)
            scores = lax.dot_general(qn, k_ref[rows(n), cols], (((1,), (1,)), ((), ())),
                                     preferred_element_type=F32)
            o = _dot((scores * decay).astype(BF16), v_ref[rows(n), cols])
            o = o + _dot((qf * qw_f).astype(BF16), prev_f[n])
            o = o + _dot((qf * qw_b).astype(BF16), prev_b[n])
            mu = jnp.mean(o, axis=-1, keepdims=True)
            d = o - mu
            var = jnp.mean(d * d, axis=-1, keepdims=True)
            on = d * lax.rsqrt(var + EPS) * gn
            r_ref[rows(n), cols] = (on * sg_ref[rows(n), cols].astype(F32)).astype(BF16)


def _retention(q, k, v, sg, dec, gn_g, s0f, s0b, batch, seq_len):
    hd = RET_HEAD_DIM
    tok_spec = pl.BlockSpec((seq_len, RET_WIDTH), lambda b: (b, 0))
    st_spec = pl.BlockSpec((None, N_RET_HEADS, hd, hd), lambda b: (b, 0, 0, 0))
    st_shape = jax.ShapeDtypeStruct((batch, N_RET_HEADS, hd, hd), F32)
    return pl.pallas_call(
        _retention_kernel,
        grid=(batch,),
        in_specs=[tok_spec, tok_spec, tok_spec, tok_spec,
                  pl.BlockSpec(dec.shape, lambda b: (0, 0, 0)),
                  pl.BlockSpec(gn_g.shape, lambda b: (0, 0)),
                  st_spec, st_spec],
        out_specs=[tok_spec, st_spec, st_spec],
        out_shape=[jax.ShapeDtypeStruct((batch * seq_len, RET_WIDTH), BF16), st_shape, st_shape],
        scratch_shapes=[pltpu.VMEM((N_RET_HEADS, 5, CHUNK, CHUNK), F32),
                        pltpu.VMEM((N_RET_HEADS, 2, hd), F32)],
        compiler_params=pltpu.CompilerParams(dimension_semantics=("arbitrary",),
                                             vmem_limit_bytes=VMEM_LIMIT),
        name="retention",
    )(q, k, v, sg, dec, gn_g, s0f, s0b)


def _fnet_merge_kernel(uf_ref, cs_ref, cls_ref, r_ref, gf_ref, gr_ref, x_ref, mod_ref, wf_ref, wr_ref, wo_ref,
                       o_ref, xcs_ref):
    seq_len = uf_ref.shape[0]
    gd = FOURIER_GROUP_DIM

    @pl.when(pl.program_id(1) == 0)
    def _():
        for g in range(N_FOURIER_GROUPS):
            x = _dot(uf_ref[:, g * gd:(g + 1) * gd], cs_ref[...])
            xcs_ref[0:seq_len, g * gd:(g + 1) * gd] = x[:, :gd].astype(BF16)
            xcs_ref[seq_len:2 * seq_len, g * gd:(g + 1) * gd] = x[:, gd:].astype(BF16)

    f_mix = _dot(cls_ref[...], xcs_ref[...]).astype(BF16)
    f_out = _dot(f_mix, wf_ref[...])
    r_out = _dot(r_ref[...], wr_ref[...])
    merged = gf_ref[...].astype(F32) * f_out + gr_ref[...].astype(F32) * r_out
    mix = _dot(merged.astype(BF16), wo_ref[...])
    o_ref[...] = x_ref[...] + mod_ref[0, 2:3, :] * mix


def _fnet_merge(uf, cs, cls, r, gf, gr, x2d, mod3, w_four, w_ret, w_o, batch, seq_len, mod_row_of_batch):
    rb = min(FNET_ROWS, seq_len)
    nr = seq_len // rb

    def tok(w):
        return pl.BlockSpec((rb, w), lambda b, i: (b * nr + i, 0))

    def full(a):
        return pl.BlockSpec(a.shape, lambda b, i: (0, 0))

    return pl.pallas_call(
        _fnet_merge_kernel,
        grid=(batch, nr),
        in_specs=[pl.BlockSpec((seq_len, D_MODEL), lambda b, i: (b, 0)),
                  full(cs),
                  pl.BlockSpec((rb, 2 * seq_len), lambda b, i: (i, 0)),
                  tok(RET_WIDTH), tok(D_MODEL), tok(D_MODEL), tok(D_MODEL),
                  pl.BlockSpec((1, 6, D_MODEL), lambda b, i: (mod_row_of_batch(b), 0, 0)),
                  full(w_four), full(w_ret), full(w_o)],
        out_specs=tok(D_MODEL),
        out_shape=jax.ShapeDtypeStruct((batch * seq_len, D_MODEL), F32),
        scratch_shapes=[pltpu.VMEM((2 * seq_len, D_MODEL), BF16)],
        compiler_params=pltpu.CompilerParams(dimension_semantics=("parallel", "arbitrary"),
                                             vmem_limit_bytes=VMEM_LIMIT),
        name="fnet_merge",
    )(uf, cs, cls, r, gf, gr, x2d, mod3, w_four, w_ret, w_o)


def _pack_pair(lo_f32, hi_f32):
    lo = lax.bitcast_convert_type(lo_f32.astype(BF16).astype(F32), jnp.uint32)
    hi = lax.bitcast_convert_type(hi_f32.astype(BF16).astype(F32), jnp.uint32)
    return lax.bitcast_convert_type((lo >> 16) | hi, jnp.int32)


def _unpack_pair(words_i32):
    w = lax.bitcast_convert_type(words_i32, jnp.uint32)
    lo = lax.bitcast_convert_type(w << 16, F32)
    hi = lax.bitcast_convert_type(w & jnp.uint32(0xFFFF0000), F32)
    return lo, hi


def _load_token_words(ref, lead, n_tok):
    parts = []
    for s in range(ROW_SLABS):
        idx = (pl.ds(s, n_tok, stride=ROW_SLABS), slice(None))
        parts.append(ref[lead + idx] if lead else ref[idx])
    return jnp.concatenate(parts, axis=1)


def _store_token_words(ref, words, n_tok):
    for s in range(ROW_SLABS):
        ref[pl.ds(s, n_tok, stride=ROW_SLABS), :] = words[:, s * 128:(s + 1) * 128]


def _route(scores, biased):
    tokens = scores.shape[1]
    neg = -jnp.inf
    epg = EXPERTS_PER_GROUP
    iota_g = lax.broadcasted_iota(jnp.int32, (epg, tokens), 0).astype(F32)

    def pick_first_max(cur, iota, size):
        m = jnp.max(cur, axis=0, keepdims=True)
        idx = jnp.min(jnp.where(cur == m, iota, float(size)), axis=0, keepdims=True)
        return m, idx, iota == idx

    group_scores = []
    for g in range(N_EXPERT_GROUPS):
        vals = biased[g * epg:(g + 1) * epg, :]
        m1, _, hit = pick_first_max(vals, iota_g, epg)
        m2 = jnp.max(jnp.where(hit, neg, vals), axis=0, keepdims=True)
        group_scores.append(m1 + m2)
    cur = jnp.concatenate(group_scores, axis=0)
    group_sel = jnp.zeros_like(cur)
    for _ in range(TOPK_GROUPS):
        _, _, hit = pick_first_max(cur, iota_g, N_EXPERT_GROUPS)
        group_sel = jnp.where(hit, 1.0, group_sel)
        cur = jnp.where(hit, neg, cur)
    masked = jnp.concatenate(
        [jnp.where(group_sel[g:g + 1, :] > 0.0, biased[g * epg:(g + 1) * epg, :], neg)
         for g in range(N_EXPERT_GROUPS)], axis=0)
    iota_e = lax.broadcasted_iota(jnp.int32, masked.shape, 0).astype(F32)
    sel = jnp.zeros_like(masked)
    cur = masked
    picks = []
    for _ in range(TOP_K):
        _, idx, hit = pick_first_max(cur, iota_e, N_EXPERTS)
        picks.append(idx)
        sel = jnp.where(hit, 1.0, sel)
        cur = jnp.where(hit, neg, cur)
    w = scores * sel
    return w / jnp.sum(w, axis=0, keepdims=True) * ROUTED_SCALE, sel, picks


def _router_kernel(x_ref, mod_ref, g2_ref, wrt_ref, rb_ref, hp_ref, ek_ref, rk_ref, wt_ref, cnt_ref, run_scr):
    tm = x_ref.shape[0]

    @pl.when(pl.program_id(0) == 0)
    def _():
        run_scr[...] = jnp.zeros_like(run_scr)

    h = _rms_mod(x_ref[...], g2_ref[...], mod_ref[0, 3:4, :], mod_ref[0, 4:5, :])
    half = D_MODEL // 2
    _store_token_words(hp_ref, _pack_pair(h[:, :half], h[:, half:]), tm)

    logits_t = lax.dot_general(wrt_ref[...], h, (((1,), (1,)), ((), ())),
                               precision=lax.Precision.HIGHEST, preferred_element_type=F32)
    scores = jax.nn.sigmoid(logits_t)
    comb_t, sel, picks = _route(scores, scores + rb_ref[...])

    earlier = (lax.broadcasted_iota(jnp.int32, (tm, tm), 0) < lax.broadcasted_iota(jnp.int32, (tm, tm), 1))
    rank_t = _dot(sel.astype(BF16), jnp.where(earlier, 1.0, 0.0).astype(BF16)) + run_scr[...]
    run_scr[...] += jnp.sum(sel, axis=1, keepdims=True)
    cnt_ref[...] = jnp.broadcast_to(run_scr[...], cnt_ref.shape)

    iota_e = lax.broadcasted_iota(jnp.int32, sel.shape, 0).astype(F32)
    ranks, weights = [], []
    for idx in picks:
        hit = iota_e == idx
        ranks.append(jnp.sum(jnp.where(hit, rank_t, 0.0), axis=0, keepdims=True))
        weights.append(jnp.sum(jnp.where(hit, comb_t, 0.0), axis=0, keepdims=True))
    ek_ref[...] = jnp.concatenate(picks, axis=0).astype(jnp.int32)
    rk_ref[...] = jnp.concatenate(ranks, axis=0).astype(jnp.int32)
    w_rep = jnp.concatenate([jnp.broadcast_to(w, (SC_LANES, tm)) for w in weights], axis=0)
    wt_ref[...] = w_rep.T


def _router(x1, mod3, norm2_g, w_router_t, router_bias, seq_len, mod_row_of_batch):
    t = x1.shape[0]
    tm = TM_ROUTER

    def mod_idx(i):
        return (mod_row_of_batch((i * tm) // seq_len), 0, 0)

    def full(a):
        return pl.BlockSpec(a.shape, lambda i: (0,) * a.ndim)

    return pl.pallas_call(
        _router_kernel,
        grid=(t // tm,),
        in_specs=[pl.BlockSpec((tm, D_MODEL), lambda i: (i, 0)),
                  pl.BlockSpec((1, 6, D_MODEL), mod_idx),
                  full(norm2_g), full(w_router_t), full(router_bias)],
        out_specs=[pl.BlockSpec((tm * ROW_SLABS, 128), lambda i: (i, 0)),
                   pl.BlockSpec((TOP_K, tm), lambda i: (0, i)),
                   pl.BlockSpec((TOP_K, tm), lambda i: (0, i)),
                   pl.BlockSpec((tm, 128), lambda i: (i, 0)),
                   pl.BlockSpec((N_EXPERTS, 128), lambda i: (0, 0))],
        out_shape=[jax.ShapeDtypeStruct((t * ROW_SLABS, 128), jnp.int32),
                   jax.ShapeDtypeStruct((TOP_K, t), jnp.int32),
                   jax.ShapeDtypeStruct((TOP_K, t), jnp.int32),
                   jax.ShapeDtypeStruct((t, 128), F32),
                   jax.ShapeDtypeStruct((N_EXPERTS, 128), F32)],
        scratch_shapes=[pltpu.VMEM((N_EXPERTS, 1), F32)],
        compiler_params=pltpu.CompilerParams(dimension_semantics=("arbitrary",),
                                             vmem_limit_bytes=VMEM_LIMIT),
        name="router",
    )(x1, mod3, norm2_g, w_router_t, router_bias)


def _plan_kernel(ek_ref, rk_ref, cnt_ref, pos_ref, texp_ref, nused_ref, tend_ref):
    rows = float(EXPERT_ROWS)
    cnt = cnt_ref[:, 0:1]
    tiles = jnp.floor((cnt + (rows - 1.0)) / rows)
    before = (lax.broadcasted_iota(jnp.int32, (N_EXPERTS, N_EXPERTS), 1)
              < lax.broadcasted_iota(jnp.int32, (N_EXPERTS, N_EXPERTS), 0))
    tile_start = jnp.dot(jnp.where(before, 1.0, 0.0), jnp.broadcast_to(tiles, (N_EXPERTS, 128)),
                         precision=lax.Precision.HIGHEST, preferred_element_type=F32)[:, 0:1]
    tile_end = tile_start + tiles
    row_start = tile_start * rows

    ek = ek_ref[...]
    pos = rk_ref[...].astype(F32)
    tile_id = lax.broadcasted_iota(jnp.int32, texp_ref.shape, 1).astype(F32)
    texp = jnp.zeros(texp_ref.shape, F32)
    for e in range(N_EXPERTS):
        pos = pos + jnp.where(ek == e, row_start[e:e + 1, :], 0.0)
        texp = texp + jnp.where(tile_id >= tile_end[e:e + 1, :], 1.0, 0.0)
    pos_ref[...] = pos.astype(jnp.int32)
    texp_ref[...] = jnp.minimum(texp, N_EXPERTS - 1.0).astype(jnp.int32)
    nused_ref[...] = jnp.broadcast_to(tile_end[N_EXPERTS - 1:N_EXPERTS, :], nused_ref.shape).astype(jnp.int32)
    tend_ref[...] = jnp.broadcast_to(tile_end, tend_ref.shape).astype(jnp.int32)


def _plan(ek, rk, cnt, n_tiles_pad):
    t = ek.shape[1]

    def full(shape):
        return pl.BlockSpec(shape, lambda: (0,) * len(shape))

    return pl.pallas_call(
        _plan_kernel,
        in_specs=[full(ek.shape), full(rk.shape), full(cnt.shape)],
        out_specs=[full((TOP_K, t)), full((1, n_tiles_pad)), full((1, 128)), full((N_EXPERTS, 128))],
        out_shape=[jax.ShapeDtypeStruct((TOP_K, t), jnp.int32),
                   jax.ShapeDtypeStruct((1, n_tiles_pad), jnp.int32),
                   jax.ShapeDtypeStruct((1, 128), jnp.int32),
                   jax.ShapeDtypeStruct((N_EXPERTS, 128), jnp.int32)],
        compiler_params=pltpu.CompilerParams(vmem_limit_bytes=VMEM_LIMIT),
        name="plan",
    )(ek, rk, cnt)


def _sc_mesh():
    return plsc.VectorSubcoreMesh(core_axis_name="c", subcore_axis_name="s")


def _sc_dispatch(rows, pos3, n_out):
    t = rows.shape[0]
    ch = SC_CHUNK
    per_w = (t // ch) // SC_WORKERS

    @functools.partial(
        pl.kernel, out_type=jax.ShapeDtypeStruct((n_out,) + rows.shape[1:], jnp.int32), mesh=_sc_mesh(),
        scratch_types=[pltpu.VMEM((TOP_K, ch), jnp.int32), pltpu.VMEM((ch,) + rows.shape[1:], jnp.int32),
                       pltpu.SemaphoreType.DMA])
    def k(rows_hbm, pos_hbm, out_hbm, idx_v, rows_v, sem):
        wid = lax.axis_index("s") * SC_CORES + lax.axis_index("c")

        @pl.loop(0, per_w)
        def _(j):
            c = wid * per_w + j
            pltpu.sync_copy(pos_hbm.at[c], idx_v)
            pltpu.sync_copy(rows_hbm.at[pl.ds(c * ch, ch)], rows_v)
            copies = [pltpu.async_copy(rows_v, out_hbm.at[idx_v.at[kk]], sem) for kk in range(TOP_K)]
            for cp in copies:
                cp.wait()

    return k(rows, pos3)


def _sc_combine(table, pos3, wtok, t):
    ch = SC_CHUNK
    sub = SC_COMBINE_TOKENS
    lanes = SC_LANES
    slabs = ROW_SLABS
    per_w = (t // ch) // SC_WORKERS
    subs_per_chunk = ch // sub
    n_steps = per_w * subs_per_chunk

    @functools.partial(
        pl.kernel, out_type=jax.ShapeDtypeStruct((t, 2 * slabs, 128), F32), mesh=_sc_mesh(),
        scratch_types=[pltpu.VMEM((per_w, TOP_K, ch), jnp.int32),
                       pltpu.VMEM((2, TOP_K, sub, slabs, 128), jnp.int32),
                       pltpu.VMEM((2, sub, 128), F32),
                       pltpu.VMEM((sub, 2 * slabs, 128), F32),
                       pltpu.SemaphoreType.DMA((2,))],
        compiler_params=pltpu.CompilerParams(needs_layout_passes=False))
    def k(tab_hbm, pos_hbm, w_hbm, out_hbm, idx_v, rows_v, w_v, out_v, sem):
        wid = lax.axis_index("s") * SC_CORES + lax.axis_index("c")
        for j in range(per_w):
            pltpu.sync_copy(pos_hbm.at[wid * per_w + j], idx_v.at[j])

        def first_token(step):
            return (wid * per_w + step // subs_per_chunk) * ch + (step % subs_per_chunk) * sub

        def copies(step, slot):
            j = step // subs_per_chunk
            s = step % subs_per_chunk
            idx = [idx_v.at[j, kk, pl.ds(s * sub, sub)] for kk in range(TOP_K)]
            return ([pltpu.make_async_copy(tab_hbm.at[idx[kk]], rows_v.at[slot, kk], sem.at[slot])
                     for kk in range(TOP_K)]
                    + [pltpu.make_async_copy(w_hbm.at[pl.ds(first_token(step), sub)], w_v.at[slot], sem.at[slot])])

        for cp in copies(0, 0):
            cp.start()

        @pl.loop(0, n_steps)
        def _(step):
            slot = step % 2

            @pl.when(step + 1 < n_steps)
            def _():
                for cp in copies(step + 1, 1 - slot):
                    cp.start()

            for cp in copies(step, slot):
                cp.wait()

            @pl.loop(0, sub)
            def _(tt):
                wk = [w_v[slot, tt, pl.ds(kk * lanes, lanes)] for kk in range(TOP_K)]
                for sl in range(slabs):
                    @plsc.parallel_loop(0, 128, step=lanes, unroll=4)
                    def _(off):
                        acc_lo = jnp.zeros((lanes,), F32)
                        acc_hi = jnp.zeros((lanes,), F32)
                        for kk in range(TOP_K):
                            word = rows_v[slot, kk, tt, sl, pl.ds(off, lanes)]
                            lo = plsc.bitcast(word << 16, F32)
                            hi = plsc.bitcast(word & jnp.int32(-65536), F32)
                            acc_lo = acc_lo + wk[kk] * lo
                            acc_hi = acc_hi + wk[kk] * hi
                        out_v[tt, sl, pl.ds(off, lanes)] = acc_lo
                        out_v[tt, slabs + sl, pl.ds(off, lanes)] = acc_hi

            pltpu.sync_copy(out_v, out_hbm.at[pl.ds(first_token(step), sub)])

    return k(table, pos3, wtok)


def _experts_kernel(texp_ref, nused_ref, tend_ref, xs_ref, weg_hbm, weu_hbm, wed_hbm, ys_ref,
                    wg_scr, wu_scr, wd_scr, wg_buf, wu_buf, wd_buf, sem, group_scr):
    step = pl.program_id(0)
    rows = EXPERT_ROWS
    half = D_MODEL // 2
    n_used = nused_ref[0]

    def weight_copies(e, slot):
        return [pltpu.make_async_copy(weg_hbm.at[e], wg_buf.at[slot], sem.at[slot, 0]),
                pltpu.make_async_copy(weu_hbm.at[e], wu_buf.at[slot], sem.at[slot, 1]),
                pltpu.make_async_copy(wed_hbm.at[e], wd_buf.at[slot], sem.at[slot, 2])]

    @pl.when(step == 0)
    def _():
        group_scr[0] = 0
        for cp in weight_copies(texp_ref[0], 0):
            cp.start()

    def row_tile(tile, x_view, y_view):
        expert = texp_ref[tile]
        used = tile < n_used
        new_expert = (tile == 0) | (expert != texp_ref[jnp.maximum(tile - 1, 0)])

        @pl.when(used & new_expert)
        def _():
            group = group_scr[0]
            slot = group % 2
            next_tile = tend_ref[expert]

            @pl.when(next_tile < n_used)
            def _():
                for cp in weight_copies(texp_ref[next_tile], 1 - slot):
                    cp.start()

            for cp in weight_copies(expert, slot):
                cp.wait()
            wg_scr[...] = wg_buf[slot].astype(BF16)
            wu_scr[...] = wu_buf[slot].astype(BF16)
            wd_scr[...] = wd_buf[slot].astype(BF16)
            group_scr[0] = group + 1

        @pl.when(used)
        def _():
            lo, hi = _unpack_pair(_load_token_words(x_view, (), rows))
            lo = lo.astype(BF16)
            hi = hi.astype(BF16)
            g = _dot(lo, wg_scr[0:half, :]) + _dot(hi, wg_scr[half:D_MODEL, :])
            u = _dot(lo, wu_scr[0:half, :]) + _dot(hi, wu_scr[half:D_MODEL, :])
            y = _dot((_silu(g) * u).astype(BF16), wd_scr[...])
            _store_token_words(y_view, _pack_pair(y[:, :half], y[:, half:]), rows)

        @pl.when(jnp.logical_not(used) & (step == (n_used - 1) // TILES_PER_STEP))
        def _():
            y_view[...] = jnp.zeros_like(y_view)

    for s in range(TILES_PER_STEP):
        view = pl.ds(s * rows * ROW_SLABS, rows * ROW_SLABS)
        row_tile(step * TILES_PER_STEP + s, xs_ref.at[view], ys_ref.at[view])


def _experts(texp, nused, tend, xs2d, weg, weu, wed, n_tiles):
    block = (TILES_PER_STEP * EXPERT_ROWS * ROW_SLABS, 128)
    hbm = pl.BlockSpec(memory_space=pl.ANY)

    def block_idx(j, te, nu, tn):
        return (jnp.minimum(j, (nu[0] - 1) // TILES_PER_STEP), 0)

    grid_spec = pltpu.PrefetchScalarGridSpec(
        num_scalar_prefetch=3,
        grid=(n_tiles // TILES_PER_STEP,),
        in_specs=[pl.BlockSpec(block, block_idx), hbm, hbm, hbm],
        out_specs=pl.BlockSpec(block, block_idx),
        scratch_shapes=[pltpu.VMEM((D_MODEL, EXPERT_DIM), BF16),
                        pltpu.VMEM((D_MODEL, EXPERT_DIM), BF16),
                        pltpu.VMEM((EXPERT_DIM, D_MODEL), BF16),
                        pltpu.VMEM((2, D_MODEL, EXPERT_DIM), F32),
                        pltpu.VMEM((2, D_MODEL, EXPERT_DIM), F32),
                        pltpu.VMEM((2, EXPERT_DIM, D_MODEL), F32),
                        pltpu.SemaphoreType.DMA((2, 3)),
                        pltpu.SMEM((1,), jnp.int32)],
    )
    return pl.pallas_call(
        _experts_kernel,
        grid_spec=grid_spec,
        out_shape=jax.ShapeDtypeStruct(xs2d.shape, jnp.int32),
        compiler_params=pltpu.CompilerParams(dimension_semantics=("arbitrary",),
                                             vmem_limit_bytes=VMEM_LIMIT),
        name="experts",
    )(texp, nused, tend, xs2d, weg, weu, wed)


def _final_kernel(x_ref, routed_ref, mod_ref, g2_ref, wsg_ref, wsu_ref, wsd_ref, fng_ref, o_ref):
    tm = x_ref.shape[0]
    x = x_ref[...]
    hb = _rms_mod(x, g2_ref[...], mod_ref[0, 3:4, :], mod_ref[0, 4:5, :]).astype(BF16)
    shared = _dot((_silu(_dot(hb, wsg_ref[...])) * _dot(hb, wsu_ref[...])).astype(BF16), wsd_ref[...])
    n_slabs = 2 * ROW_SLABS
    routed = jnp.concatenate([routed_ref[pl.ds(s, tm, stride=n_slabs), :] for s in range(n_slabs)], axis=1)
    y = x + mod_ref[0, 5:6, :] * (routed + shared)
    ms = jnp.mean(y * y, axis=-1, keepdims=True)
    o_ref[...] = y * lax.rsqrt(ms + EPS) * fng_ref[...]


def _final(x1, routed2d, mod3, norm2_g, wsg, wsu, wsd, final_g, seq_len, mod_row_of_batch):
    t = x1.shape[0]
    tm = TM_FINAL

    def mod_idx(i):
        return (mod_row_of_batch((i * tm) // seq_len), 0, 0)

    def full(a):
        return pl.BlockSpec(a.shape, lambda i: (0,) * a.ndim)

    return pl.pallas_call(
        _final_kernel,
        grid=(t // tm,),
        in_specs=[pl.BlockSpec((tm, D_MODEL), lambda i: (i, 0)),
                  pl.BlockSpec((tm * 2 * ROW_SLABS, 128), lambda i: (i, 0)),
                  pl.BlockSpec((1, 6, D_MODEL), mod_idx),
                  full(norm2_g), full(wsg), full(wsu), full(wsd), full(final_g)],
        out_specs=pl.BlockSpec((tm, D_MODEL), lambda i: (i, 0)),
        out_shape=jax.ShapeDtypeStruct((t, D_MODEL), F32),
        compiler_params=pltpu.CompilerParams(dimension_semantics=("parallel",),
                                             vmem_limit_bytes=VMEM_LIMIT),
        name="final",
    )(x1, routed2d, mod3, norm2_g, wsg, wsu, wsd, final_g)


def _moe(x1, mod3, lw, seq_len, mod_row_of_batch):
    t = x1.shape[0]
    n_tiles = TOP_K * t // EXPERT_ROWS + N_EXPERTS
    n_tiles_pad = -(-n_tiles // 128) * 128
    hp2d, ek, rk, wtok, cnt = _router(x1, mod3, lw["norm2_g"], lw["w_router_t"], lw["router_bias"],
                                      seq_len, mod_row_of_batch)
    pos, texp, nused, tend = _plan(ek, rk, cnt, n_tiles_pad)
    pos3 = pos.reshape(TOP_K, t // SC_CHUNK, SC_CHUNK).transpose(1, 0, 2)
    xs = _sc_dispatch(hp2d.reshape(t, ROW_SLABS, 128), pos3, n_tiles * EXPERT_ROWS)
    ys2d = _experts(texp.reshape(-1), nused.reshape(-1), tend[:, 0], xs.reshape(-1, 128),
                    lw["weg"], lw["weu"], lw["wed"], n_tiles)
    routed = _sc_combine(ys2d.reshape(-1, ROW_SLABS, 128), pos3, wtok, t)
    return _final(x1, routed.reshape(t * 2 * ROW_SLABS, 128), mod3, lw["norm2_g"],
                  lw["wsg"], lw["wsu"], lw["wsd"], lw["final_g"], seq_len, mod_row_of_batch)


def _dft_tables(seq_len):
    gd = FOURIER_GROUP_DIM
    kc = np.arange(gd)
    ang_c = ((kc[:, None] * kc[None, :]) % gd) * (2.0 * math.pi / gd)
    cs = np.concatenate([np.cos(ang_c), np.sin(ang_c)], axis=1) * (gd ** -0.5)
    kl = np.arange(seq_len)
    ang_l = ((kl[:, None] * kl[None, :]) % seq_len) * (2.0 * math.pi / seq_len)
    cls = np.concatenate([np.cos(ang_l), -np.sin(ang_l)], axis=1) * (seq_len ** -0.5)
    return jnp.asarray(cs.astype(np.float32), dtype=BF16), jnp.asarray(cls.astype(np.float32), dtype=BF16)


def _rope_tables(length):
    rows = length // GRID_W
    r = np.repeat(np.arange(rows, dtype=np.float32), GRID_W)
    col = np.tile(np.arange(GRID_W, dtype=np.float32), rows)
    nf = RET_HEAD_DIM // 4
    inv = (np.float32(ROPE_BASE) ** (-np.arange(nf, dtype=np.float32) / np.float32(nf))).astype(np.float32)
    ar = r[:, None] * inv[None]
    ac = col[:, None] * inv[None]
    ang = np.concatenate([ar, ar, ac, ac], axis=-1).astype(np.float64)
    sign = np.where((np.arange(RET_HEAD_DIM) & nf) == 0, -1.0, 1.0)
    return (jnp.asarray(np.cos(ang).astype(np.float32)),
            jnp.asarray((np.sin(ang) * sign[None, :]).astype(np.float32)))


def _trunk_path(x, mod3, mod_row_of_batch, s0f, s0b, rope, lw):
    batch, seq_len, _ = x.shape
    x2d = x.reshape(batch * seq_len, D_MODEL)
    uf, q, k, v, sg, gf, gr = _inproj(x2d, mod3, lw["norm1_g"], lw["w_in"], seq_len, mod_row_of_batch, rope)
    r, s_f, s_b = _retention(q, k, v, sg, lw["dec"], lw["gn_g"], s0f, s0b, batch, seq_len)
    cs, cls = _dft_tables(seq_len)
    x1 = _fnet_merge(uf, cs, cls, r, gf, gr, x2d, mod3, lw["w_four"], lw["w_ret"], lw["w_o"],
                     batch, seq_len, mod_row_of_batch)
    y = _moe(x1, mod3, lw, seq_len, mod_row_of_batch)
    return y.reshape(batch, seq_len, D_MODEL), s_f, s_b


def kernel(x_prompt, x_sample, state_ret_fwd, state_ret_bwd, c, c_ctx, w_ada, b_ada, norm1_g, norm2_g, w_in,
           ret_decay_fwd, ret_decay_bwd, ret_gn_g, w_four_out, w_ret_out, w_out, w_router, router_bias,
           w_exp_gate, w_exp_up, w_exp_down, w_shared_gate, w_shared_up, w_shared_down, final_norm_g):
    depth = w_ada.shape[0]
    assert depth == 1, "final norm is fused into the last layer's MoE kernel"
    n_ctx, n_lat = x_prompt.shape[0], x_sample.shape[0]
    cond = jnp.concatenate([c_ctx[None, :], c], axis=0)
    cond = jnp.pad(cond, ((0, (-cond.shape[0]) % 8), (0, 0)))
    rope = _rope_tables(x_sample.shape[1])
    zeros = jnp.zeros((n_ctx, N_RET_HEADS, RET_HEAD_DIM, RET_HEAD_DIM), F32)

    layer = 0
    mod = _ada(cond, w_ada[layer], b_ada[layer][None, :])
    mod3 = mod.reshape(mod.shape[0], 6, D_MODEL)
    dec = jnp.stack([ret_decay_fwd[layer], ret_decay_bwd[layer]], axis=1)
    lw = {
        "norm1_g": norm1_g[layer][None, :],
        "norm2_g": norm2_g[layer][None, :],
        "w_in": w_in[layer].astype(BF16),
        "dec": jnp.broadcast_to(dec[:, :, None], (N_RET_HEADS, 2, RET_HEAD_DIM)).astype(F32),
        "gn_g": ret_gn_g[layer][None, :],
        "w_four": w_four_out[layer].astype(BF16),
        "w_ret": w_ret_out[layer].astype(BF16),
        "w_o": w_out[layer].astype(BF16),
        "w_router_t": w_router[layer].T,
        "router_bias": router_bias[layer][:, None],
        "weg": w_exp_gate[layer],
        "weu": w_exp_up[layer],
        "wed": w_exp_down[layer],
        "wsg": w_shared_gate[layer].astype(BF16),
        "wsu": w_shared_up[layer].astype(BF16),
        "wsd": w_shared_down[layer].astype(BF16),
        "final_g": final_norm_g[None, :],
    }
    y_prompt, s_f, s_b = _trunk_path(x_prompt, mod3, lambda b: 0, zeros, zeros, None, lw)
    y_sample, _, _ = _trunk_path(x_sample, mod3, lambda b: 1 + b, state_ret_fwd[:, layer],
                                 state_ret_bwd[:, layer], rope, lw)
    return (y_prompt, y_sample, s_f[:, None], s_b[:, None])
```

```python
import functools
import math

import jax
import jax.numpy as jnp
import numpy as np
from jax import lax
from jax.experimental import pallas as pl
from jax.experimental.pallas import tpu as pltpu
from jax.experimental.pallas import tpu_sc as plsc

F32 = jnp.float32
BF16 = jnp.bfloat16

D_MODEL = 1024
GRID_W = 64
N_FOURIER_GROUPS = 8
FOURIER_GROUP_DIM = 128
N_RET_HEADS = 4
RET_HEAD_DIM = 128
RET_WIDTH = N_RET_HEADS * RET_HEAD_DIM
CHUNK = 128
N_EXPERTS = 64
N_EXPERT_GROUPS = 8
EXPERTS_PER_GROUP = N_EXPERTS // N_EXPERT_GROUPS
TOPK_GROUPS = 4
TOP_K = 8
EXPERT_DIM = 256
ROUTED_SCALE = 2.5
ROPE_BASE = 10000.0
EPS = 1e-6
Q_SCALE = RET_HEAD_DIM ** -0.5

_C_UF = (0, 1024)
_C_Q = (1024, 1536)
_C_K = (1536, 2048)
_C_V = (2048, 2560)
_C_G = (2560, 3072)
_C_GF = (3072, 4096)
_C_GR = (4096, 5120)

VMEM_LIMIT = 56 * 1024 * 1024

TM_INPROJ = 1024
TM_ROUTER = 1024
FNET_ROWS = 512
TM_FINAL = 1024
EXPERT_ROWS = 512
TILES_PER_STEP = 2
ROW_SLABS = 4
SC_CORES = 2
SC_WORKERS = 32
SC_CHUNK = 128
SC_LANES = 16
SC_COMBINE_TOKENS = 8


def _silu(x):
    return x * jax.nn.sigmoid(x)


def _dot(a, b):
    return jnp.dot(a, b, preferred_element_type=F32)


def _rms_mod(x, g, shift, scale):
    ms = jnp.mean(x * x, axis=-1, keepdims=True)
    y = x * lax.rsqrt(ms + EPS) * g
    return y * (1.0 + scale) + shift


def _ada_kernel(cond_ref, w_ref, b_ref, o_ref):
    s = _silu(cond_ref[...]).astype(BF16)
    o_ref[...] = _dot(s, w_ref[...].astype(BF16)) + b_ref[...]


def _ada(cond, w_ada, b_ada):
    rows, n = cond.shape[0], w_ada.shape[1]
    tn = 1536
    return pl.pallas_call(
        _ada_kernel,
        grid=(n // tn,),
        in_specs=[pl.BlockSpec((rows, D_MODEL), lambda j: (0, 0)),
                  pl.BlockSpec((D_MODEL, tn), lambda j: (0, j)),
                  pl.BlockSpec((1, tn), lambda j: (0, j))],
        out_specs=pl.BlockSpec((rows, tn), lambda j: (0, j)),
        out_shape=jax.ShapeDtypeStruct((rows, n), F32),
        compiler_params=pltpu.CompilerParams(vmem_limit_bytes=VMEM_LIMIT),
        name="ada",
    )(cond, w_ada, b_ada)


def _rope_head(x, cos, sin_signed, first_half):
    partner = jnp.where(first_half, pltpu.roll(x, 96, 1), pltpu.roll(x, 32, 1))
    return x * cos + partner * sin_signed


def _inproj_kernel(*refs, use_rope):
    if use_rope:
        x_ref, mod_ref, g_ref, w_ref, cos_ref, sin_ref = refs[:6]
        outs = refs[6:]
    else:
        x_ref, mod_ref, g_ref, w_ref = refs[:4]
        outs = refs[4:]
    uf_o, q_o, k_o, v_o, sg_o, gf_o, gr_o = outs

    h = _rms_mod(x_ref[...], g_ref[...], mod_ref[0, 0:1, :], mod_ref[0, 1:2, :])
    hb = h.astype(BF16)

    def proj(cols):
        return _dot(hb, w_ref[:, cols[0]:cols[1]])

    uf_o[...] = proj(_C_UF).astype(BF16)
    q = proj(_C_Q)
    k = proj(_C_K)
    if use_rope:
        cos = cos_ref[...]
        sin_signed = sin_ref[...]
        lane = lax.broadcasted_iota(jnp.int32, cos.shape, 1)
        first_half = (lane & 32) == 0
        for hd in range(N_RET_HEADS):
            sl = slice(hd * RET_HEAD_DIM, (hd + 1) * RET_HEAD_DIM)
            q_o[:, sl] = (_rope_head(q[:, sl], cos, sin_signed, first_half) * Q_SCALE).astype(BF16)
            k_o[:, sl] = _rope_head(k[:, sl], cos, sin_signed, first_half).astype(BF16)
    else:
        q_o[...] = (q * Q_SCALE).astype(BF16)
        k_o[...] = k.astype(BF16)
    v_o[...] = proj(_C_V).astype(BF16)
    sg_o[...] = _silu(proj(_C_G)).astype(BF16)
    gf_o[...] = jax.nn.sigmoid(proj(_C_GF)).astype(BF16)
    gr_o[...] = jax.nn.sigmoid(proj(_C_GR)).astype(BF16)


def _inproj(x2d, mod3, norm_g, w_in_bf, seq_len, mod_row_of_batch, rope):
    t = x2d.shape[0]
    tm = TM_INPROJ
    tiles_per_seq = max(seq_len // tm, 1)

    def mod_idx(i):
        return (mod_row_of_batch((i * tm) // seq_len), 0, 0)

    in_specs = [pl.BlockSpec((tm, D_MODEL), lambda i: (i, 0)),
                pl.BlockSpec((1, 6, D_MODEL), mod_idx),
                pl.BlockSpec((1, D_MODEL), lambda i: (0, 0)),
                pl.BlockSpec(w_in_bf.shape, lambda i: (0, 0), pipeline_mode=pl.Buffered(1))]
    args = [x2d, mod3, norm_g, w_in_bf]
    if rope is not None:
        in_specs += [pl.BlockSpec((tm, RET_HEAD_DIM), lambda i: (i % tiles_per_seq, 0))] * 2
        args += list(rope)
    widths = [1024, RET_WIDTH, RET_WIDTH, RET_WIDTH, RET_WIDTH, 1024, 1024]
    return pl.pallas_call(
        functools.partial(_inproj_kernel, use_rope=rope is not None),
        grid=(t // tm,),
        in_specs=in_specs,
        out_specs=[pl.BlockSpec((tm, w), lambda i: (i, 0)) for w in widths],
        out_shape=[jax.ShapeDtypeStruct((t, w), BF16) for w in widths],
        compiler_params=pltpu.CompilerParams(dimension_semantics=("parallel",),
                                             vmem_limit_bytes=VMEM_LIMIT),
        name="inproj",
    )(*args)


def _retention_kernel(q_ref, k_ref, v_ref, sg_ref, dec_ref, gn_ref, s0f_ref, s0b_ref,
                      r_ref, sfo_ref, sbo_ref, tab_scr, gc_scr):
    n_chunks = q_ref.shape[0] // CHUNK
    hd = RET_HEAD_DIM

    @pl.when(pl.program_id(0) == 0)
    def _():
        row = lax.broadcasted_iota(jnp.int32, (CHUNK, CHUNK), 0).astype(F32)
        col = lax.broadcasted_iota(jnp.int32, (CHUNK, CHUNK), 1).astype(F32)
        diff = row - col
        for h in range(N_RET_HEADS):
            dec = dec_ref[h]
            lg = jnp.minimum(dec, 0.0) - jnp.log1p(jnp.exp(-jnp.abs(dec)))
            lgf = lg[0:1, :]
            lgb = lg[1:2, :]
            tab_scr[h, 0] = jnp.exp(jnp.where(diff >= 0, lgf * diff, lgb * (-diff)))
            tab_scr[h, 1] = jnp.exp(lgf * (row + 1.0))
            tab_scr[h, 2] = jnp.exp(lgb * (CHUNK - row))
            tab_scr[h, 3] = jnp.exp(lgf * (CHUNK - 1.0 - col))
            tab_scr[h, 4] = jnp.exp(lgb * col)
            gc_scr[h] = jnp.exp(lg * CHUNK)

    def rows(n):
        return slice(n * CHUNK, (n + 1) * CHUNK)

    for h in range(N_RET_HEADS):
        cols = slice(h * hd, (h + 1) * hd)
        decay, qw_f, qw_b, kwt_f, kwt_b = (tab_scr[h, i] for i in range(5))
        gc = gc_scr[h]
        gc_f = gc[0:1, :]
        gc_b = gc[1:2, :]

        kv_f, kv_b = [], []
        for n in range(n_chunks):
            kt = k_ref[rows(n), cols].astype(F32).T
            vn = v_ref[rows(n), cols]
            kv_f.append(_dot((kt * kwt_f).astype(BF16), vn))
            kv_b.append(_dot((kt * kwt_b).astype(BF16), vn))

        s = s0f_ref[h]
        prev_f = []
        for n in range(n_chunks):
            prev_f.append(s.astype(BF16))
            s = gc_f * s + kv_f[n]
        sfo_ref[h] = s
        s = s0b_ref[h]
        prev_b = [None] * n_chunks
        for n in reversed(range(n_chunks)):
            prev_b[n] = s.astype(BF16)
            s = gc_b * s + kv_b[n]
        sbo_ref[h] = s

        gn = gn_ref[:, cols]
        for n in range(n_chunks):
            qn = q_ref[rows(n), cols]
            qf = qn.astype(F32)
            scores = lax.dot_general(qn, k_ref[rows(n), cols], (((1,), (1,)), ((), ())),
                                     preferred_element_type=F32)
            o = _dot((scores * decay).astype(BF16), v_ref[rows(n), cols])
            o = o + _dot((qf * qw_f).astype(BF16), prev_f[n])
            o = o + _dot((qf * qw_b).astype(BF16), prev_b[n])
            mu = jnp.mean(o, axis=-1, keepdims=True)
            d = o - mu
            var = jnp.mean(d * d, axis=-1, keepdims=True)
            on = d * lax.rsqrt(var + EPS) * gn
            r_ref[rows(n), cols] = (on * sg_ref[rows(n), cols].astype(F32)).astype(BF16)


def _retention(q, k, v, sg, dec, gn_g, s0f, s0b, batch, seq_len):
    hd = RET_HEAD_DIM
    tok_spec = pl.BlockSpec((seq_len, RET_WIDTH), lambda b: (b, 0))
    st_spec = pl.BlockSpec((None, N_RET_HEADS, hd, hd), lambda b: (b, 0, 0, 0))
    st_shape = jax.ShapeDtypeStruct((batch, N_RET_HEADS, hd, hd), F32)
    return pl.pallas_call(
        _retention_kernel,
        grid=(batch,),
        in_specs=[tok_spec, tok_spec, tok_spec, tok_spec,
                  pl.BlockSpec(dec.shape, lambda b: (0, 0, 0)),
                  pl.BlockSpec(gn_g.shape, lambda b: (0, 0)),
                  st_spec, st_spec],
        out_specs=[tok_spec, st_spec, st_spec],
        out_shape=[jax.ShapeDtypeStruct((batch * seq_len, RET_WIDTH), BF16), st_shape, st_shape],
        scratch_shapes=[pltpu.VMEM((N_RET_HEADS, 5, CHUNK, CHUNK), F32),
                        pltpu.VMEM((N_RET_HEADS, 2, hd), F32)],
        compiler_params=pltpu.CompilerParams(dimension_semantics=("arbitrary",),
                                             vmem_limit_bytes=VMEM_LIMIT),
        name="retention",
    )(q, k, v, sg, dec, gn_g, s0f, s0b)


def _fnet_merge_kernel(uf_ref, cs_ref, cls_ref, r_ref, gf_ref, gr_ref, x_ref, mod_ref, wf_ref, wr_ref, wo_ref,
                       o_ref, xcs_ref):
    seq_len = uf_ref.shape[0]
    gd = FOURIER_GROUP_DIM

    @pl.when(pl.program_id(1) == 0)
    def _():
        for g in range(N_FOURIER_GROUPS):
            x = _dot(uf_ref[:, g * gd:(g + 1) * gd], cs_ref[...])
            xcs_ref[0:seq_len, g * gd:(g + 1) * gd] = x[:, :gd].astype(BF16)
            xcs_ref[seq_len:2 * seq_len, g * gd:(g + 1) * gd] = x[:, gd:].astype(BF16)

    f_mix = _dot(cls_ref[...], xcs_ref[...]).astype(BF16)
    f_out = _dot(f_mix, wf_ref[...])
    r_out = _dot(r_ref[...], wr_ref[...])
    merged = gf_ref[...].astype(F32) * f_out + gr_ref[...].astype(F32) * r_out
    mix = _dot(merged.astype(BF16), wo_ref[...])
    o_ref[...] = x_ref[...] + mod_ref[0, 2:3, :] * mix


def _fnet_merge(uf, cs, cls, r, gf, gr, x2d, mod3, w_four, w_ret, w_o, batch, seq_len, mod_row_of_batch):
    rb = min(FNET_ROWS, seq_len)
    nr = seq_len // rb

    def tok(w):
        return pl.BlockSpec((rb, w), lambda b, i: (b * nr + i, 0))

    def full(a):
        return pl.BlockSpec(a.shape, lambda b, i: (0, 0))

    return pl.pallas_call(
        _fnet_merge_kernel,
        grid=(batch, nr),
        in_specs=[pl.BlockSpec((seq_len, D_MODEL), lambda b, i: (b, 0)),
                  full(cs),
                  pl.BlockSpec((rb, 2 * seq_len), lambda b, i: (i, 0)),
                  tok(RET_WIDTH), tok(D_MODEL), tok(D_MODEL), tok(D_MODEL),
                  pl.BlockSpec((1, 6, D_MODEL), lambda b, i: (mod_row_of_batch(b), 0, 0)),
                  full(w_four), full(w_ret), full(w_o)],
        out_specs=tok(D_MODEL),
        out_shape=jax.ShapeDtypeStruct((batch * seq_len, D_MODEL), F32),
        scratch_shapes=[pltpu.VMEM((2 * seq_len, D_MODEL), BF16)],
        compiler_params=pltpu.CompilerParams(dimension_semantics=("parallel", "arbitrary"),
                                             vmem_limit_bytes=VMEM_LIMIT),
        name="fnet_merge",
    )(uf, cs, cls, r, gf, gr, x2d, mod3, w_four, w_ret, w_o)


def _pack_pair(lo_f32, hi_f32):
    lo = lax.bitcast_convert_type(lo_f32.astype(BF16).astype(F32), jnp.uint32)
    hi = lax.bitcast_convert_type(hi_f32.astype(BF16).astype(F32), jnp.uint32)
    return lax.bitcast_convert_type((lo >> 16) | hi, jnp.int32)


def _unpack_pair(words_i32):
    w = lax.bitcast_convert_type(words_i32, jnp.uint32)
    lo = lax.bitcast_convert_type(w << 16, F32)
    hi = lax.bitcast_convert_type(w & jnp.uint32(0xFFFF0000), F32)
    return lo, hi


def _load_token_words(ref, lead, n_tok):
    parts = []
    for s in range(ROW_SLABS):
        idx = (pl.ds(s, n_tok, stride=ROW_SLABS), slice(None))
        parts.append(ref[lead + idx] if lead else ref[idx])
    return jnp.concatenate(parts, axis=1)


def _store_token_words(ref, words, n_tok):
    for s in range(ROW_SLABS):
        ref[pl.ds(s, n_tok, stride=ROW_SLABS), :] = words[:, s * 128:(s + 1) * 128]


def _route(scores, biased):
    tokens = scores.shape[1]
    neg = -jnp.inf
    epg = EXPERTS_PER_GROUP
    iota_g = lax.broadcasted_iota(jnp.int32, (epg, tokens), 0).astype(F32)

    def pick_first_max(cur, iota, size):
        m = jnp.max(cur, axis=0, keepdims=True)
        idx = jnp.min(jnp.where(cur == m, iota, float(size)), axis=0, keepdims=True)
        return m, idx, iota == idx

    group_scores = []
    for g in range(N_EXPERT_GROUPS):
        vals = biased[g * epg:(g + 1) * epg, :]
        m1, _, hit = pick_first_max(vals, iota_g, epg)
        m2 = jnp.max(jnp.where(hit, neg, vals), axis=0, keepdims=True)
        group_scores.append(m1 + m2)
    cur = jnp.concatenate(group_scores, axis=0)
    group_sel = jnp.zeros_like(cur)
    for _ in range(TOPK_GROUPS):
        _, _, hit = pick_first_max(cur, iota_g, N_EXPERT_GROUPS)
        group_sel = jnp.where(hit, 1.0, group_sel)
        cur = jnp.where(hit, neg, cur)
    masked = jnp.concatenate(
        [jnp.where(group_sel[g:g + 1, :] > 0.0, biased[g * epg:(g + 1) * epg, :], neg)
         for g in range(N_EXPERT_GROUPS)], axis=0)
    iota_e = lax.broadcasted_iota(jnp.int32, masked.shape, 0).astype(F32)
    sel = jnp.zeros_like(masked)
    cur = masked
    picks = []
    for _ in range(TOP_K):
        _, idx, hit = pick_first_max(cur, iota_e, N_EXPERTS)
        picks.append(idx)
        sel = jnp.where(hit, 1.0, sel)
        cur = jnp.where(hit, neg, cur)
    w = scores * sel
    return w / jnp.sum(w, axis=0, keepdims=True) * ROUTED_SCALE, sel, picks


def _router_kernel(x_ref, mod_ref, g2_ref, wrt_ref, rb_ref, hp_ref, ek_ref, rk_ref, wt_ref, cnt_ref,
                   run_scr, earlier_scr):
    tm = x_ref.shape[0]

    @pl.when(pl.program_id(0) == 0)
    def _():
        run_scr[...] = jnp.zeros_like(run_scr)
        earlier = (lax.broadcasted_iota(jnp.int32, (tm, tm), 0) < lax.broadcasted_iota(jnp.int32, (tm, tm), 1))
        earlier_scr[...] = jnp.where(earlier, 1.0, 0.0).astype(BF16)

    h = _rms_mod(x_ref[...], g2_ref[...], mod_ref[0, 3:4, :], mod_ref[0, 4:5, :])
    half = D_MODEL // 2
    _store_token_words(hp_ref, _pack_pair(h[:, :half], h[:, half:]), tm)

    def split(a):
        hi = a.astype(BF16)
        return hi, (a - hi.astype(F32)).astype(BF16)

    def dot_nt(a, b):
        return lax.dot_general(a, b, (((1,), (1,)), ((), ())), preferred_element_type=F32)

    h_hi, h_lo = split(h)
    w_hi, w_lo = split(wrt_ref[...])
    logits_t = dot_nt(w_hi, h_hi) + (dot_nt(w_hi, h_lo) + dot_nt(w_lo, h_hi))
    scores = jax.nn.sigmoid(logits_t)
    comb_t, sel, picks = _route(scores, scores + rb_ref[...])

    rank_t = _dot(sel.astype(BF16), earlier_scr[...]) + run_scr[...]
    run_scr[...] += jnp.sum(sel, axis=1, keepdims=True)
    cnt_ref[...] = jnp.broadcast_to(run_scr[...], cnt_ref.shape)

    iota_e = lax.broadcasted_iota(jnp.int32, sel.shape, 0).astype(F32)
    ranks, weights = [], []
    for idx in picks:
        hit = iota_e == idx
        ranks.append(jnp.sum(jnp.where(hit, rank_t, 0.0), axis=0, keepdims=True))
        weights.append(jnp.sum(jnp.where(hit, comb_t, 0.0), axis=0, keepdims=True))
    ek_ref[...] = jnp.concatenate(picks, axis=0).astype(jnp.int32)
    rk_ref[...] = jnp.concatenate(ranks, axis=0).astype(jnp.int32)
    w_rep = jnp.concatenate([jnp.broadcast_to(w, (SC_LANES, tm)) for w in weights], axis=0)
    wt_ref[...] = w_rep.T


def _router(x1, mod3, norm2_g, w_router_t, router_bias, seq_len, mod_row_of_batch):
    t = x1.shape[0]
    tm = TM_ROUTER

    def mod_idx(i):
        return (mod_row_of_batch((i * tm) // seq_len), 0, 0)

    def full(a):
        return pl.BlockSpec(a.shape, lambda i: (0,) * a.ndim)

    return pl.pallas_call(
        _router_kernel,
        grid=(t // tm,),
        in_specs=[pl.BlockSpec((tm, D_MODEL), lambda i: (i, 0)),
                  pl.BlockSpec((1, 6, D_MODEL), mod_idx),
                  full(norm2_g), full(w_router_t), full(router_bias)],
        out_specs=[pl.BlockSpec((tm * ROW_SLABS, 128), lambda i: (i, 0)),
                   pl.BlockSpec((TOP_K, tm), lambda i: (0, i)),
                   pl.BlockSpec((TOP_K, tm), lambda i: (0, i)),
                   pl.BlockSpec((tm, 128), lambda i: (i, 0)),
                   pl.BlockSpec((N_EXPERTS, 128), lambda i: (0, 0))],
        out_shape=[jax.ShapeDtypeStruct((t * ROW_SLABS, 128), jnp.int32),
                   jax.ShapeDtypeStruct((TOP_K, t), jnp.int32),
                   jax.ShapeDtypeStruct((TOP_K, t), jnp.int32),
                   jax.ShapeDtypeStruct((t, 128), F32),
                   jax.ShapeDtypeStruct((N_EXPERTS, 128), F32)],
        scratch_shapes=[pltpu.VMEM((N_EXPERTS, 1), F32), pltpu.VMEM((tm, tm), BF16)],
        compiler_params=pltpu.CompilerParams(dimension_semantics=("arbitrary",),
                                             vmem_limit_bytes=VMEM_LIMIT),
        name="router",
    )(x1, mod3, norm2_g, w_router_t, router_bias)


def _plan_kernel(ek_ref, rk_ref, cnt_ref, pos_ref, texp_ref, nused_ref, tend_ref):
    rows = float(EXPERT_ROWS)
    cnt = cnt_ref[:, 0:1]
    tiles = jnp.floor((cnt + (rows - 1.0)) / rows)
    before = (lax.broadcasted_iota(jnp.int32, (N_EXPERTS, N_EXPERTS), 1)
              < lax.broadcasted_iota(jnp.int32, (N_EXPERTS, N_EXPERTS), 0))
    tile_start = jnp.dot(jnp.where(before, 1.0, 0.0), jnp.broadcast_to(tiles, (N_EXPERTS, 128)),
                         precision=lax.Precision.HIGHEST, preferred_element_type=F32)[:, 0:1]
    tile_end = tile_start + tiles
    row_start = tile_start * rows

    ek = ek_ref[...]
    pos = rk_ref[...].astype(F32)
    tile_id = lax.broadcasted_iota(jnp.int32, texp_ref.shape, 1).astype(F32)
    texp = jnp.zeros(texp_ref.shape, F32)
    for e in range(N_EXPERTS):
        pos = pos + jnp.where(ek == e, row_start[e:e + 1, :], 0.0)
        texp = texp + jnp.where(tile_id >= tile_end[e:e + 1, :], 1.0, 0.0)
    pos_ref[...] = pos.astype(jnp.int32)
    texp_ref[...] = jnp.minimum(texp, N_EXPERTS - 1.0).astype(jnp.int32)
    nused_ref[...] = jnp.broadcast_to(tile_end[N_EXPERTS - 1:N_EXPERTS, :], nused_ref.shape).astype(jnp.int32)
    tend_ref[...] = jnp.broadcast_to(tile_end, tend_ref.shape).astype(jnp.int32)


def _plan(ek, rk, cnt, n_tiles_pad):
    t = ek.shape[1]

    def full(shape):
        return pl.BlockSpec(shape, lambda: (0,) * len(shape))

    return pl.pallas_call(
        _plan_kernel,
        in_specs=[full(ek.shape), full(rk.shape), full(cnt.shape)],
        out_specs=[full((TOP_K, t)), full((1, n_tiles_pad)), full((1, 128)), full((N_EXPERTS, 128))],
        out_shape=[jax.ShapeDtypeStruct((TOP_K, t), jnp.int32),
                   jax.ShapeDtypeStruct((1, n_tiles_pad), jnp.int32),
                   jax.ShapeDtypeStruct((1, 128), jnp.int32),
                   jax.ShapeDtypeStruct((N_EXPERTS, 128), jnp.int32)],
        compiler_params=pltpu.CompilerParams(vmem_limit_bytes=VMEM_LIMIT),
        name="plan",
    )(ek, rk, cnt)


def _sc_mesh():
    return plsc.VectorSubcoreMesh(core_axis_name="c", subcore_axis_name="s")


def _sc_dispatch(rows, pos3, n_out):
    t = rows.shape[0]
    ch = SC_CHUNK
    per_w = (t // ch) // SC_WORKERS

    @functools.partial(
        pl.kernel, out_type=jax.ShapeDtypeStruct((n_out,) + rows.shape[1:], jnp.int32), mesh=_sc_mesh(),
        scratch_types=[pltpu.VMEM((TOP_K, ch), jnp.int32), pltpu.VMEM((ch,) + rows.shape[1:], jnp.int32),
                       pltpu.SemaphoreType.DMA])
    def k(rows_hbm, pos_hbm, out_hbm, idx_v, rows_v, sem):
        wid = lax.axis_index("s") * SC_CORES + lax.axis_index("c")

        @pl.loop(0, per_w)
        def _(j):
            c = wid * per_w + j
            pltpu.sync_copy(pos_hbm.at[c], idx_v)
            pltpu.sync_copy(rows_hbm.at[pl.ds(c * ch, ch)], rows_v)
            copies = [pltpu.async_copy(rows_v, out_hbm.at[idx_v.at[kk]], sem) for kk in range(TOP_K)]
            for cp in copies:
                cp.wait()

    return k(rows, pos3)


def _sc_combine(table, pos3, wtok, t):
    ch = SC_CHUNK
    sub = SC_COMBINE_TOKENS
    lanes = SC_LANES
    slabs = ROW_SLABS
    per_w = (t // ch) // SC_WORKERS
    subs_per_chunk = ch // sub
    n_steps = per_w * subs_per_chunk

    @functools.partial(
        pl.kernel, out_type=jax.ShapeDtypeStruct((t, 2 * slabs, 128), F32), mesh=_sc_mesh(),
        scratch_types=[pltpu.VMEM((per_w, TOP_K, ch), jnp.int32),
                       pltpu.VMEM((2, TOP_K, sub, slabs, 128), jnp.int32),
                       pltpu.VMEM((2, sub, 128), F32),
                       pltpu.VMEM((sub, 2 * slabs, 128), F32),
                       pltpu.SemaphoreType.DMA((2,))],
        compiler_params=pltpu.CompilerParams(needs_layout_passes=False))
    def k(tab_hbm, pos_hbm, w_hbm, out_hbm, idx_v, rows_v, w_v, out_v, sem):
        wid = lax.axis_index("s") * SC_CORES + lax.axis_index("c")
        for j in range(per_w):
            pltpu.sync_copy(pos_hbm.at[wid * per_w + j], idx_v.at[j])

        def first_token(step):
            return (wid * per_w + step // subs_per_chunk) * ch + (step % subs_per_chunk) * sub

        def copies(step, slot):
            j = step // subs_per_chunk
            s = step % subs_per_chunk
            idx = [idx_v.at[j, kk, pl.ds(s * sub, sub)] for kk in range(TOP_K)]
            return ([pltpu.make_async_copy(tab_hbm.at[idx[kk]], rows_v.at[slot, kk], sem.at[slot])
                     for kk in range(TOP_K)]
                    + [pltpu.make_async_copy(w_hbm.at[pl.ds(first_token(step), sub)], w_v.at[slot], sem.at[slot])])

        for cp in copies(0, 0):
            cp.start()

        @pl.loop(0, n_steps)
        def _(step):
            slot = step % 2

            @pl.when(step + 1 < n_steps)
            def _():
                for cp in copies(step + 1, 1 - slot):
                    cp.start()

            for cp in copies(step, slot):
                cp.wait()

            @pl.loop(0, sub)
            def _(tt):
                wk = [w_v[slot, tt, pl.ds(kk * lanes, lanes)] for kk in range(TOP_K)]
                for sl in range(slabs):
                    @plsc.parallel_loop(0, 128, step=lanes, unroll=4)
                    def _(off):
                        acc_lo = jnp.zeros((lanes,), F32)
                        acc_hi = jnp.zeros((lanes,), F32)
                        for kk in range(TOP_K):
                            word = rows_v[slot, kk, tt, sl, pl.ds(off, lanes)]
                            lo = plsc.bitcast(word << 16, F32)
                            hi = plsc.bitcast(word & jnp.int32(-65536), F32)
                            acc_lo = acc_lo + wk[kk] * lo
                            acc_hi = acc_hi + wk[kk] * hi
                        out_v[tt, sl, pl.ds(off, lanes)] = acc_lo
                        out_v[tt, slabs + sl, pl.ds(off, lanes)] = acc_hi

            pltpu.sync_copy(out_v, out_hbm.at[pl.ds(first_token(step), sub)])

    return k(table, pos3, wtok)


def _experts_kernel(texp_ref, nused_ref, tend_ref, xs_ref, weg_hbm, weu_hbm, wed_hbm, ys_ref,
                    wg_scr, wu_scr, wd_scr, wg_buf, wu_buf, wd_buf, sem, group_scr):
    step = pl.program_id(0)
    rows = EXPERT_ROWS
    half = D_MODEL // 2
    n_used = nused_ref[0]

    def weight_copies(e, slot):
        return [pltpu.make_async_copy(weg_hbm.at[e], wg_buf.at[slot], sem.at[slot, 0]),
                pltpu.make_async_copy(weu_hbm.at[e], wu_buf.at[slot], sem.at[slot, 1]),
                pltpu.make_async_copy(wed_hbm.at[e], wd_buf.at[slot], sem.at[slot, 2])]

    @pl.when(step == 0)
    def _():
        group_scr[0] = 0
        for cp in weight_copies(texp_ref[0], 0):
            cp.start()

    def row_tile(tile, x_view, y_view):
        expert = texp_ref[tile]
        used = tile < n_used
        new_expert = (tile == 0) | (expert != texp_ref[jnp.maximum(tile - 1, 0)])

        @pl.when(used & new_expert)
        def _():
            group = group_scr[0]
            slot = group % 2
            next_tile = tend_ref[expert]

            @pl.when(next_tile < n_used)
            def _():
                for cp in weight_copies(texp_ref[next_tile], 1 - slot):
                    cp.start()

            for cp in weight_copies(expert, slot):
                cp.wait()
            wg_scr[...] = wg_buf[slot].astype(BF16)
            wu_scr[...] = wu_buf[slot].astype(BF16)
            wd_scr[...] = wd_buf[slot].astype(BF16)
            group_scr[0] = group + 1

        @pl.when(used)
        def _():
            lo, hi = _unpack_pair(_load_token_words(x_view, (), rows))
            lo = lo.astype(BF16)
            hi = hi.astype(BF16)
            g = _dot(lo, wg_scr[0:half, :]) + _dot(hi, wg_scr[half:D_MODEL, :])
            u = _dot(lo, wu_scr[0:half, :]) + _dot(hi, wu_scr[half:D_MODEL, :])
            y = _dot((_silu(g) * u).astype(BF16), wd_scr[...])
            _store_token_words(y_view, _pack_pair(y[:, :half], y[:, half:]), rows)

        @pl.when(jnp.logical_not(used) & (step == (n_used - 1) // TILES_PER_STEP))
        def _():
            y_view[...] = jnp.zeros_like(y_view)

    for s in range(TILES_PER_STEP):
        view = pl.ds(s * rows * ROW_SLABS, rows * ROW_SLABS)
        row_tile(step * TILES_PER_STEP + s, xs_ref.at[view], ys_ref.at[view])


def _experts(texp, nused, tend, xs2d, weg, weu, wed, n_tiles):
    block = (TILES_PER_STEP * EXPERT_ROWS * ROW_SLABS, 128)
    hbm = pl.BlockSpec(memory_space=pl.ANY)

    def block_idx(j, te, nu, tn):
        return (jnp.minimum(j, (nu[0] - 1) // TILES_PER_STEP), 0)

    grid_spec = pltpu.PrefetchScalarGridSpec(
        num_scalar_prefetch=3,
        grid=(n_tiles // TILES_PER_STEP,),
        in_specs=[pl.BlockSpec(block, block_idx), hbm, hbm, hbm],
        out_specs=pl.BlockSpec(block, block_idx),
        scratch_shapes=[pltpu.VMEM((D_MODEL, EXPERT_DIM), BF16),
                        pltpu.VMEM((D_MODEL, EXPERT_DIM), BF16),
                        pltpu.VMEM((EXPERT_DIM, D_MODEL), BF16),
                        pltpu.VMEM((2, D_MODEL, EXPERT_DIM), F32),
                        pltpu.VMEM((2, D_MODEL, EXPERT_DIM), F32),
                        pltpu.VMEM((2, EXPERT_DIM, D_MODEL), F32),
                        pltpu.SemaphoreType.DMA((2, 3)),
                        pltpu.SMEM((1,), jnp.int32)],
    )
    return pl.pallas_call(
        _experts_kernel,
        grid_spec=grid_spec,
        out_shape=jax.ShapeDtypeStruct(xs2d.shape, jnp.int32),
        compiler_params=pltpu.CompilerParams(dimension_semantics=("arbitrary",),
                                             vmem_limit_bytes=VMEM_LIMIT),
        name="experts",
    )(texp, nused, tend, xs2d, weg, weu, wed)


def _final_kernel(x_ref, routed_ref, mod_ref, g2_ref, wsg_ref, wsu_ref, wsd_ref, fng_ref, o_ref):
    tm = x_ref.shape[0]
    x = x_ref[...]
    hb = _rms_mod(x, g2_ref[...], mod_ref[0, 3:4, :], mod_ref[0, 4:5, :]).astype(BF16)
    shared = _dot((_silu(_dot(hb, wsg_ref[...])) * _dot(hb, wsu_ref[...])).astype(BF16), wsd_ref[...])
    n_slabs = 2 * ROW_SLABS
    routed = jnp.concatenate([routed_ref[pl.ds(s, tm, stride=n_slabs), :] for s in range(n_slabs)], axis=1)
    y = x + mod_ref[0, 5:6, :] * (routed + shared)
    ms = jnp.mean(y * y, axis=-1, keepdims=True)
    o_ref[...] = y * lax.rsqrt(ms + EPS) * fng_ref[...]


def _final(x1, routed2d, mod3, norm2_g, wsg, wsu, wsd, final_g, seq_len, mod_row_of_batch):
    t = x1.shape[0]
    tm = TM_FINAL

    def mod_idx(i):
        return (mod_row_of_batch((i * tm) // seq_len), 0, 0)

    def full(a):
        return pl.BlockSpec(a.shape, lambda i: (0,) * a.ndim)

    return pl.pallas_call(
        _final_kernel,
        grid=(t // tm,),
        in_specs=[pl.BlockSpec((tm, D_MODEL), lambda i: (i, 0)),
                  pl.BlockSpec((tm * 2 * ROW_SLABS, 128), lambda i: (i, 0)),
                  pl.BlockSpec((1, 6, D_MODEL), mod_idx),
                  full(norm2_g), full(wsg), full(wsu), full(wsd), full(final_g)],
        out_specs=pl.BlockSpec((tm, D_MODEL), lambda i: (i, 0)),
        out_shape=jax.ShapeDtypeStruct((t, D_MODEL), F32),
        compiler_params=pltpu.CompilerParams(dimension_semantics=("parallel",),
                                             vmem_limit_bytes=VMEM_LIMIT),
        name="final",
    )(x1, routed2d, mod3, norm2_g, wsg, wsu, wsd, final_g)


def _moe(x1, mod3, lw, seq_len, mod_row_of_batch):
    t = x1.shape[0]
    n_tiles = TOP_K * t // EXPERT_ROWS + N_EXPERTS
    n_tiles_pad = -(-n_tiles // 128) * 128
    hp2d, ek, rk, wtok, cnt = _router(x1, mod3, lw["norm2_g"], lw["w_router_t"], lw["router_bias"],
                                      seq_len, mod_row_of_batch)
    pos, texp, nused, tend = _plan(ek, rk, cnt, n_tiles_pad)
    pos3 = pos.reshape(TOP_K, t // SC_CHUNK, SC_CHUNK).transpose(1, 0, 2)
    xs = _sc_dispatch(hp2d.reshape(t, ROW_SLABS, 128), pos3, n_tiles * EXPERT_ROWS)
    ys2d = _experts(texp.reshape(-1), nused.reshape(-1), tend[:, 0], xs.reshape(-1, 128),
                    lw["weg"], lw["weu"], lw["wed"], n_tiles)
    routed = _sc_combine(ys2d.reshape(-1, ROW_SLABS, 128), pos3, wtok, t)
    return _final(x1, routed.reshape(t * 2 * ROW_SLABS, 128), mod3, lw["norm2_g"],
                  lw["wsg"], lw["wsu"], lw["wsd"], lw["final_g"], seq_len, mod_row_of_batch)


def _dft_tables(seq_len):
    gd = FOURIER_GROUP_DIM
    kc = np.arange(gd)
    ang_c = ((kc[:, None] * kc[None, :]) % gd) * (2.0 * math.pi / gd)
    cs = np.concatenate([np.cos(ang_c), np.sin(ang_c)], axis=1) * (gd ** -0.5)
    kl = np.arange(seq_len)
    ang_l = ((kl[:, None] * kl[None, :]) % seq_len) * (2.0 * math.pi / seq_len)
    cls = np.concatenate([np.cos(ang_l), -np.sin(ang_l)], axis=1) * (seq_len ** -0.5)
    return jnp.asarray(cs.astype(np.float32), dtype=BF16), jnp.asarray(cls.astype(np.float32), dtype=BF16)


def _rope_tables(length):
    rows = length // GRID_W
    r = np.repeat(np.arange(rows, dtype=np.float32), GRID_W)
    col = np.tile(np.arange(GRID_W, dtype=np.float32), rows)
    nf = RET_HEAD_DIM // 4
    inv = (np.float32(ROPE_BASE) ** (-np.arange(nf, dtype=np.float32) / np.float32(nf))).astype(np.float32)
    ar = r[:, None] * inv[None]
    ac = col[:, None] * inv[None]
    ang = np.concatenate([ar, ar, ac, ac], axis=-1).astype(np.float64)
    sign = np.where((np.arange(RET_HEAD_DIM) & nf) == 0, -1.0, 1.0)
    return (jnp.asarray(np.cos(ang).astype(np.float32)),
            jnp.asarray((np.sin(ang) * sign[None, :]).astype(np.float32)))


def _trunk_path(x, mod3, mod_row_of_batch, s0f, s0b, rope, lw):
    batch, seq_len, _ = x.shape
    x2d = x.reshape(batch * seq_len, D_MODEL)
    uf, q, k, v, sg, gf, gr = _inproj(x2d, mod3, lw["norm1_g"], lw["w_in"], seq_len, mod_row_of_batch, rope)
    r, s_f, s_b = _retention(q, k, v, sg, lw["dec"], lw["gn_g"], s0f, s0b, batch, seq_len)
    cs, cls = _dft_tables(seq_len)
    x1 = _fnet_merge(uf, cs, cls, r, gf, gr, x2d, mod3, lw["w_four"], lw["w_ret"], lw["w_o"],
                     batch, seq_len, mod_row_of_batch)
    y = _moe(x1, mod3, lw, seq_len, mod_row_of_batch)
    return y.reshape(batch, seq_len, D_MODEL), s_f, s_b


def kernel(x_prompt, x_sample, state_ret_fwd, state_ret_bwd, c, c_ctx, w_ada, b_ada, norm1_g, norm2_g, w_in,
           ret_decay_fwd, ret_decay_bwd, ret_gn_g, w_four_out, w_ret_out, w_out, w_router, router_bias,
           w_exp_gate, w_exp_up, w_exp_down, w_shared_gate, w_shared_up, w_shared_down, final_norm_g):
    depth = w_ada.shape[0]
    assert depth == 1, "final norm is fused into the last layer's MoE kernel"
    n_ctx, n_lat = x_prompt.shape[0], x_sample.shape[0]
    cond = jnp.concatenate([c_ctx[None, :], c], axis=0)
    cond = jnp.pad(cond, ((0, (-cond.shape[0]) % 8), (0, 0)))
    rope = _rope_tables(x_sample.shape[1])
    zeros = jnp.zeros((n_ctx, N_RET_HEADS, RET_HEAD_DIM, RET_HEAD_DIM), F32)

    layer = 0
    mod = _ada(cond, w_ada[layer], b_ada[layer][None, :])
    mod3 = mod.reshape(mod.shape[0], 6, D_MODEL)
    dec = jnp.stack([ret_decay_fwd[layer], ret_decay_bwd[layer]], axis=1)
    lw = {
        "norm1_g": norm1_g[layer][None, :],
        "norm2_g": norm2_g[layer][None, :],
        "w_in": w_in[layer].astype(BF16),
        "dec": jnp.broadcast_to(dec[:, :, None], (N_RET_HEADS, 2, RET_HEAD_DIM)).astype(F32),
        "gn_g": ret_gn_g[layer][None, :],
        "w_four": w_four_out[layer].astype(BF16),
        "w_ret": w_ret_out[layer].astype(BF16),
        "w_o": w_out[layer].astype(BF16),
        "w_router_t": w_router[layer].T,
        "router_bias": router_bias[layer][:, None],
        "weg": w_exp_gate[layer],
        "weu": w_exp_up[layer],
        "wed": w_exp_down[layer],
        "wsg": w_shared_gate[layer].astype(BF16),
        "wsu": w_shared_up[layer].astype(BF16),
        "wsd": w_shared_down[layer].astype(BF16),
        "final_g": final_norm_g[None, :],
    }
    y_prompt, s_f, s_b = _trunk_path(x_prompt, mod3, lambda b: 0, zeros, zeros, None, lw)
    y_sample, _, _ = _trunk_path(x_sample, mod3, lambda b: 1 + b, state_ret_fwd[:, layer],
                                 state_ret_bwd[:, layer], rope, lw)
    return (y_prompt, y_sample, s_f[:, None], s_b[:, None])
```

```python
import functools
import math

import jax
import jax.numpy as jnp
import numpy as np
from jax import lax
from jax.experimental import pallas as pl
from jax.experimental.pallas import tpu as pltpu
from jax.experimental.pallas import tpu_sc as plsc

F32 = jnp.float32
BF16 = jnp.bfloat16

D_MODEL = 1024
GRID_W = 64
N_FOURIER_GROUPS = 8
FOURIER_GROUP_DIM = 128
N_RET_HEADS = 4
RET_HEAD_DIM = 128
RET_WIDTH = N_RET_HEADS * RET_HEAD_DIM
CHUNK = 128
N_EXPERTS = 64
N_EXPERT_GROUPS = 8
EXPERTS_PER_GROUP = N_EXPERTS // N_EXPERT_GROUPS
TOPK_GROUPS = 4
TOP_K = 8
EXPERT_DIM = 256
ROUTED_SCALE = 2.5
ROPE_BASE = 10000.0
EPS = 1e-6
Q_SCALE = RET_HEAD_DIM ** -0.5

_C_UF = (0, 1024)
_C_Q = (1024, 1536)
_C_K = (1536, 2048)
_C_V = (2048, 2560)
_C_G = (2560, 3072)
_C_GF = (3072, 4096)
_C_GR = (4096, 5120)

VMEM_LIMIT = 56 * 1024 * 1024

TM_INPROJ = 1024
TM_ROUTER = 1024
FNET_ROWS = 512
TM_FINAL = 1024
EXPERT_ROWS = 512
TILES_PER_STEP = 2
WEIGHT_SLOTS = 3
ROW_SLABS = 4
SC_CORES = 2
SC_WORKERS = 32
SC_CHUNK = 128
SC_LANES = 16
SC_COMBINE_TOKENS = 8


def _silu(x):
    return x * jax.nn.sigmoid(x)


def _dot(a, b):
    return jnp.dot(a, b, preferred_element_type=F32)


def _rms_mod(x, g, shift, scale):
    ms = jnp.mean(x * x, axis=-1, keepdims=True)
    y = x * lax.rsqrt(ms + EPS) * g
    return y * (1.0 + scale) + shift


def _ada_kernel(cond_ref, w_ref, b_ref, o_ref):
    s = _silu(cond_ref[...]).astype(BF16)
    o_ref[...] = _dot(s, w_ref[...].astype(BF16)) + b_ref[...]


def _ada(cond, w_ada, b_ada):
    rows, n = cond.shape[0], w_ada.shape[1]
    tn = 1536
    return pl.pallas_call(
        _ada_kernel,
        grid=(n // tn,),
        in_specs=[pl.BlockSpec((rows, D_MODEL), lambda j: (0, 0)),
                  pl.BlockSpec((D_MODEL, tn), lambda j: (0, j)),
                  pl.BlockSpec((1, tn), lambda j: (0, j))],
        out_specs=pl.BlockSpec((rows, tn), lambda j: (0, j)),
        out_shape=jax.ShapeDtypeStruct((rows, n), F32),
        compiler_params=pltpu.CompilerParams(vmem_limit_bytes=VMEM_LIMIT),
        name="ada",
    )(cond, w_ada, b_ada)


def _rope_head(x, cos, sin_signed, first_half):
    partner = jnp.where(first_half, pltpu.roll(x, 96, 1), pltpu.roll(x, 32, 1))
    return x * cos + partner * sin_signed


def _inproj_kernel(*refs, use_rope):
    if use_rope:
        x_ref, mod_ref, g_ref, w_ref, cos_ref, sin_ref = refs[:6]
        outs = refs[6:]
    else:
        x_ref, mod_ref, g_ref, w_ref = refs[:4]
        outs = refs[4:]
    uf_o, q_o, k_o, v_o, sg_o, gf_o, gr_o = outs

    h = _rms_mod(x_ref[...], g_ref[...], mod_ref[0, 0:1, :], mod_ref[0, 1:2, :])
    hb = h.astype(BF16)

    def proj(cols):
        return _dot(hb, w_ref[:, cols[0]:cols[1]])

    uf_o[...] = proj(_C_UF).astype(BF16)
    q = proj(_C_Q)
    k = proj(_C_K)
    if use_rope:
        cos = cos_ref[...]
        sin_signed = sin_ref[...]
        lane = lax.broadcasted_iota(jnp.int32, cos.shape, 1)
        first_half = (lane & 32) == 0
        for hd in range(N_RET_HEADS):
            sl = slice(hd * RET_HEAD_DIM, (hd + 1) * RET_HEAD_DIM)
            q_o[:, sl] = (_rope_head(q[:, sl], cos, sin_signed, first_half) * Q_SCALE).astype(BF16)
            k_o[:, sl] = _rope_head(k[:, sl], cos, sin_signed, first_half).astype(BF16)
    else:
        q_o[...] = (q * Q_SCALE).astype(BF16)
        k_o[...] = k.astype(BF16)
    v_o[...] = proj(_C_V).astype(BF16)
    sg_o[...] = _silu(proj(_C_G)).astype(BF16)
    gf_o[...] = jax.nn.sigmoid(proj(_C_GF)).astype(BF16)
    gr_o[...] = jax.nn.sigmoid(proj(_C_GR)).astype(BF16)


def _inproj(x2d, mod3, norm_g, w_in_bf, seq_len, mod_row_of_batch, rope):
    t = x2d.shape[0]
    tm = TM_INPROJ
    tiles_per_seq = max(seq_len // tm, 1)

    def mod_idx(i):
        return (mod_row_of_batch((i * tm) // seq_len), 0, 0)

    in_specs = [pl.BlockSpec((tm, D_MODEL), lambda i: (i, 0)),
                pl.BlockSpec((1, 6, D_MODEL), mod_idx),
                pl.BlockSpec((1, D_MODEL), lambda i: (0, 0)),
                pl.BlockSpec(w_in_bf.shape, lambda i: (0, 0), pipeline_mode=pl.Buffered(1))]
    args = [x2d, mod3, norm_g, w_in_bf]
    if rope is not None:
        in_specs += [pl.BlockSpec((tm, RET_HEAD_DIM), lambda i: (i % tiles_per_seq, 0))] * 2
        args += list(rope)
    widths = [1024, RET_WIDTH, RET_WIDTH, RET_WIDTH, RET_WIDTH, 1024, 1024]
    return pl.pallas_call(
        functools.partial(_inproj_kernel, use_rope=rope is not None),
        grid=(t // tm,),
        in_specs=in_specs,
        out_specs=[pl.BlockSpec((tm, w), lambda i: (i, 0)) for w in widths],
        out_shape=[jax.ShapeDtypeStruct((t, w), BF16) for w in widths],
        compiler_params=pltpu.CompilerParams(dimension_semantics=("parallel",),
                                             vmem_limit_bytes=VMEM_LIMIT),
        name="inproj",
    )(*args)


def _retention_kernel(q_ref, k_ref, v_ref, sg_ref, dec_ref, gn_ref, s0f_ref, s0b_ref,
                      r_ref, sfo_ref, sbo_ref, tab_scr, gc_scr):
    n_chunks = q_ref.shape[0] // CHUNK
    hd = RET_HEAD_DIM

    @pl.when(pl.program_id(0) == 0)
    def _():
        row = lax.broadcasted_iota(jnp.int32, (CHUNK, CHUNK), 0).astype(F32)
        col = lax.broadcasted_iota(jnp.int32, (CHUNK, CHUNK), 1).astype(F32)
        diff = row - col
        for h in range(N_RET_HEADS):
            dec = dec_ref[h]
            lg = jnp.minimum(dec, 0.0) - jnp.log1p(jnp.exp(-jnp.abs(dec)))
            lgf = lg[0:1, :]
            lgb = lg[1:2, :]
            tab_scr[h, 0] = jnp.exp(jnp.where(diff >= 0, lgf * diff, lgb * (-diff)))
            tab_scr[h, 1] = jnp.exp(lgf * (row + 1.0))
            tab_scr[h, 2] = jnp.exp(lgb * (CHUNK - row))
            tab_scr[h, 3] = jnp.exp(lgf * (CHUNK - 1.0 - col))
            tab_scr[h, 4] = jnp.exp(lgb * col)
            gc_scr[h] = jnp.exp(lg * CHUNK)

    def rows(n):
        return slice(n * CHUNK, (n + 1) * CHUNK)

    for h in range(N_RET_HEADS):
        cols = slice(h * hd, (h + 1) * hd)
        decay, qw_f, qw_b, kwt_f, kwt_b = (tab_scr[h, i] for i in range(5))
        gc = gc_scr[h]
        gc_f = gc[0:1, :]
        gc_b = gc[1:2, :]

        kv_f, kv_b = [], []
        for n in range(n_chunks):
            kt = k_ref[rows(n), cols].astype(F32).T
            vn = v_ref[rows(n), cols]
            kv_f.append(_dot((kt * kwt_f).astype(BF16), vn))
            kv_b.append(_dot((kt * kwt_b).astype(BF16), vn))

        s = s0f_ref[h]
        prev_f = []
        for n in range(n_chunks):
            prev_f.append(s.astype(BF16))
            s = gc_f * s + kv_f[n]
        sfo_ref[h] = s
        s = s0b_ref[h]
        prev_b = [None] * n_chunks
        for n in reversed(range(n_chunks)):
            prev_b[n] = s.astype(BF16)
            s = gc_b * s + kv_b[n]
        sbo_ref[h] = s

        gn = gn_ref[:, cols]
        for n in range(n_chunks):
            qn = q_ref[rows(n), cols]
            qf = qn.astype(F32)
            scores = lax.dot_general(qn, k_ref[rows(n), cols], (((1,), (1,)), ((), ())),
                                     preferred_element_type=F32)
            o = _dot((scores * decay).astype(BF16), v_ref[rows(n), cols])
            o = o + _dot((qf * qw_f).astype(BF16), prev_f[n])
            o = o + _dot((qf * qw_b).astype(BF16), prev_b[n])
            mu = jnp.mean(o, axis=-1, keepdims=True)
            d = o - mu
            var = jnp.mean(d * d, axis=-1, keepdims=True)
            on = d * lax.rsqrt(var + EPS) * gn
            r_ref[rows(n), cols] = (on * sg_ref[rows(n), cols].astype(F32)).astype(BF16)


def _retention(q, k, v, sg, dec, gn_g, s0f, s0b, batch, seq_len):
    hd = RET_HEAD_DIM
    tok_spec = pl.BlockSpec((seq_len, RET_WIDTH), lambda b: (b, 0))
    st_spec = pl.BlockSpec((None, N_RET_HEADS, hd, hd), lambda b: (b, 0, 0, 0))
    st_shape = jax.ShapeDtypeStruct((batch, N_RET_HEADS, hd, hd), F32)
    return pl.pallas_call(
        _retention_kernel,
        grid=(batch,),
        in_specs=[tok_spec, tok_spec, tok_spec, tok_spec,
                  pl.BlockSpec(dec.shape, lambda b: (0, 0, 0)),
                  pl.BlockSpec(gn_g.shape, lambda b: (0, 0)),
                  st_spec, st_spec],
        out_specs=[tok_spec, st_spec, st_spec],
        out_shape=[jax.ShapeDtypeStruct((batch * seq_len, RET_WIDTH), BF16), st_shape, st_shape],
        scratch_shapes=[pltpu.VMEM((N_RET_HEADS, 5, CHUNK, CHUNK), F32),
                        pltpu.VMEM((N_RET_HEADS, 2, hd), F32)],
        compiler_params=pltpu.CompilerParams(dimension_semantics=("arbitrary",),
                                             vmem_limit_bytes=VMEM_LIMIT),
        name="retention",
    )(q, k, v, sg, dec, gn_g, s0f, s0b)


def _fnet_merge_kernel(uf_ref, cs_ref, cls_ref, r_ref, gf_ref, gr_ref, x_ref, mod_ref, wf_ref, wr_ref, wo_ref,
                       o_ref, xcs_ref):
    seq_len = uf_ref.shape[0]
    gd = FOURIER_GROUP_DIM

    @pl.when(pl.program_id(1) == 0)
    def _():
        for g in range(N_FOURIER_GROUPS):
            x = _dot(uf_ref[:, g * gd:(g + 1) * gd], cs_ref[...])
            xcs_ref[0:seq_len, g * gd:(g + 1) * gd] = x[:, :gd].astype(BF16)
            xcs_ref[seq_len:2 * seq_len, g * gd:(g + 1) * gd] = x[:, gd:].astype(BF16)

    f_mix = _dot(cls_ref[...], xcs_ref[...]).astype(BF16)
    f_out = _dot(f_mix, wf_ref[...])
    r_out = _dot(r_ref[...], wr_ref[...])
    merged = gf_ref[...].astype(F32) * f_out + gr_ref[...].astype(F32) * r_out
    mix = _dot(merged.astype(BF16), wo_ref[...])
    o_ref[...] = x_ref[...] + mod_ref[0, 2:3, :] * mix


def _fnet_merge(uf, cs, cls, r, gf, gr, x2d, mod3, w_four, w_ret, w_o, batch, seq_len, mod_row_of_batch):
    rb = min(FNET_ROWS, seq_len)
    nr = seq_len // rb

    def tok(w):
        return pl.BlockSpec((rb, w), lambda b, i: (b * nr + i, 0))

    def full(a):
        return pl.BlockSpec(a.shape, lambda b, i: (0, 0))

    return pl.pallas_call(
        _fnet_merge_kernel,
        grid=(batch, nr),
        in_specs=[pl.BlockSpec((seq_len, D_MODEL), lambda b, i: (b, 0)),
                  full(cs),
                  pl.BlockSpec((rb, 2 * seq_len), lambda b, i: (i, 0)),
                  tok(RET_WIDTH), tok(D_MODEL), tok(D_MODEL), tok(D_MODEL),
                  pl.BlockSpec((1, 6, D_MODEL), lambda b, i: (mod_row_of_batch(b), 0, 0)),
                  full(w_four), full(w_ret), full(w_o)],
        out_specs=tok(D_MODEL),
        out_shape=jax.ShapeDtypeStruct((batch * seq_len, D_MODEL), F32),
        scratch_shapes=[pltpu.VMEM((2 * seq_len, D_MODEL), BF16)],
        compiler_params=pltpu.CompilerParams(dimension_semantics=("parallel", "arbitrary"),
                                             vmem_limit_bytes=VMEM_LIMIT),
        name="fnet_merge",
    )(uf, cs, cls, r, gf, gr, x2d, mod3, w_four, w_ret, w_o)


def _pack_pair(lo_f32, hi_f32):
    lo = lax.bitcast_convert_type(lo_f32.astype(BF16).astype(F32), jnp.uint32)
    hi = lax.bitcast_convert_type(hi_f32.astype(BF16).astype(F32), jnp.uint32)
    return lax.bitcast_convert_type((lo >> 16) | hi, jnp.int32)


def _unpack_pair(words_i32):
    w = lax.bitcast_convert_type(words_i32, jnp.uint32)
    lo = lax.bitcast_convert_type(w << 16, F32)
    hi = lax.bitcast_convert_type(w & jnp.uint32(0xFFFF0000), F32)
    return lo, hi


def _load_token_words(ref, lead, n_tok):
    parts = []
    for s in range(ROW_SLABS):
        idx = (pl.ds(s, n_tok, stride=ROW_SLABS), slice(None))
        parts.append(ref[lead + idx] if lead else ref[idx])
    return jnp.concatenate(parts, axis=1)


def _store_token_words(ref, words, n_tok):
    for s in range(ROW_SLABS):
        ref[pl.ds(s, n_tok, stride=ROW_SLABS), :] = words[:, s * 128:(s + 1) * 128]


def _route(scores, biased):
    tokens = scores.shape[1]
    neg = -jnp.inf
    epg = EXPERTS_PER_GROUP
    iota_g = lax.broadcasted_iota(jnp.int32, (epg, tokens), 0).astype(F32)

    def pick_first_max(cur, iota, size):
        m = jnp.max(cur, axis=0, keepdims=True)
        idx = jnp.min(jnp.where(cur == m, iota, float(size)), axis=0, keepdims=True)
        return m, idx, iota == idx

    group_scores = []
    for g in range(N_EXPERT_GROUPS):
        vals = biased[g * epg:(g + 1) * epg, :]
        m1, _, hit = pick_first_max(vals, iota_g, epg)
        m2 = jnp.max(jnp.where(hit, neg, vals), axis=0, keepdims=True)
        group_scores.append(m1 + m2)
    cur = jnp.concatenate(group_scores, axis=0)
    group_sel = jnp.zeros_like(cur)
    for _ in range(TOPK_GROUPS):
        _, _, hit = pick_first_max(cur, iota_g, N_EXPERT_GROUPS)
        group_sel = jnp.where(hit, 1.0, group_sel)
        cur = jnp.where(hit, neg, cur)
    masked = jnp.concatenate(
        [jnp.where(group_sel[g:g + 1, :] > 0.0, biased[g * epg:(g + 1) * epg, :], neg)
         for g in range(N_EXPERT_GROUPS)], axis=0)
    iota_e = lax.broadcasted_iota(jnp.int32, masked.shape, 0).astype(F32)
    sel = jnp.zeros_like(masked)
    cur = masked
    picks = []
    for _ in range(TOP_K):
        _, idx, hit = pick_first_max(cur, iota_e, N_EXPERTS)
        picks.append(idx)
        sel = jnp.where(hit, 1.0, sel)
        cur = jnp.where(hit, neg, cur)
    w = scores * sel
    return w / jnp.sum(w, axis=0, keepdims=True) * ROUTED_SCALE, sel, picks


def _router_kernel(x_ref, mod_ref, g2_ref, wrt_ref, rb_ref, hp_ref, ek_ref, rk_ref, wt_ref, cnt_ref,
                   run_scr, earlier_scr):
    tm = x_ref.shape[0]

    @pl.when(pl.program_id(0) == 0)
    def _():
        run_scr[...] = jnp.zeros_like(run_scr)
        earlier = (lax.broadcasted_iota(jnp.int32, (tm, tm), 0) < lax.broadcasted_iota(jnp.int32, (tm, tm), 1))
        earlier_scr[...] = jnp.where(earlier, 1.0, 0.0).astype(BF16)

    h = _rms_mod(x_ref[...], g2_ref[...], mod_ref[0, 3:4, :], mod_ref[0, 4:5, :])
    half = D_MODEL // 2
    _store_token_words(hp_ref, _pack_pair(h[:, :half], h[:, half:]), tm)

    def split(a):
        hi = a.astype(BF16)
        return hi, (a - hi.astype(F32)).astype(BF16)

    def dot_nt(a, b):
        return lax.dot_general(a, b, (((1,), (1,)), ((), ())), preferred_element_type=F32)

    h_hi, h_lo = split(h)
    w_hi, w_lo = split(wrt_ref[...])
    logits_t = dot_nt(w_hi, h_hi) + (dot_nt(w_hi, h_lo) + dot_nt(w_lo, h_hi))
    scores = jax.nn.sigmoid(logits_t)
    comb_t, sel, picks = _route(scores, scores + rb_ref[...])

    rank_t = _dot(sel.astype(BF16), earlier_scr[...]) + run_scr[...]
    run_scr[...] += jnp.sum(sel, axis=1, keepdims=True)
    cnt_ref[...] = jnp.broadcast_to(run_scr[...], cnt_ref.shape)

    iota_e = lax.broadcasted_iota(jnp.int32, sel.shape, 0).astype(F32)
    ranks, weights = [], []
    for idx in picks:
        hit = iota_e == idx
        ranks.append(jnp.sum(jnp.where(hit, rank_t, 0.0), axis=0, keepdims=True))
        weights.append(jnp.sum(jnp.where(hit, comb_t, 0.0), axis=0, keepdims=True))
    ek_ref[...] = jnp.concatenate(picks, axis=0).astype(jnp.int32)
    rk_ref[...] = jnp.concatenate(ranks, axis=0).astype(jnp.int32)
    w_rep = jnp.concatenate([jnp.broadcast_to(w, (SC_LANES, tm)) for w in weights], axis=0)
    wt_ref[...] = w_rep.T


def _router(x1, mod3, norm2_g, w_router_t, router_bias, seq_len, mod_row_of_batch):
    t = x1.shape[0]
    tm = TM_ROUTER

    def mod_idx(i):
        return (mod_row_of_batch((i * tm) // seq_len), 0, 0)

    def full(a):
        return pl.BlockSpec(a.shape, lambda i: (0,) * a.ndim)

    return pl.pallas_call(
        _router_kernel,
        grid=(t // tm,),
        in_specs=[pl.BlockSpec((tm, D_MODEL), lambda i: (i, 0)),
                  pl.BlockSpec((1, 6, D_MODEL), mod_idx),
                  full(norm2_g), full(w_router_t), full(router_bias)],
        out_specs=[pl.BlockSpec((tm * ROW_SLABS, 128), lambda i: (i, 0)),
                   pl.BlockSpec((TOP_K, tm), lambda i: (0, i)),
                   pl.BlockSpec((TOP_K, tm), lambda i: (0, i)),
                   pl.BlockSpec((tm, 128), lambda i: (i, 0)),
                   pl.BlockSpec((N_EXPERTS, 128), lambda i: (0, 0))],
        out_shape=[jax.ShapeDtypeStruct((t * ROW_SLABS, 128), jnp.int32),
                   jax.ShapeDtypeStruct((TOP_K, t), jnp.int32),
                   jax.ShapeDtypeStruct((TOP_K, t), jnp.int32),
                   jax.ShapeDtypeStruct((t, 128), F32),
                   jax.ShapeDtypeStruct((N_EXPERTS, 128), F32)],
        scratch_shapes=[pltpu.VMEM((N_EXPERTS, 1), F32), pltpu.VMEM((tm, tm), BF16)],
        compiler_params=pltpu.CompilerParams(dimension_semantics=("arbitrary",),
                                             vmem_limit_bytes=VMEM_LIMIT),
        name="router",
    )(x1, mod3, norm2_g, w_router_t, router_bias)


def _plan_kernel(ek_ref, rk_ref, cnt_ref, pos_ref, texp_ref, nused_ref, tend_ref):
    rows = float(EXPERT_ROWS)
    cnt = cnt_ref[:, 0:1]
    tiles = jnp.floor((cnt + (rows - 1.0)) / rows)
    before = (lax.broadcasted_iota(jnp.int32, (N_EXPERTS, N_EXPERTS), 1)
              < lax.broadcasted_iota(jnp.int32, (N_EXPERTS, N_EXPERTS), 0))
    tile_start = jnp.dot(jnp.where(before, 1.0, 0.0), jnp.broadcast_to(tiles, (N_EXPERTS, 128)),
                         precision=lax.Precision.HIGHEST, preferred_element_type=F32)[:, 0:1]
    tile_end = tile_start + tiles
    row_start = tile_start * rows

    ek = ek_ref[...]
    pos = rk_ref[...].astype(F32)
    tile_id = lax.broadcasted_iota(jnp.int32, texp_ref.shape, 1).astype(F32)
    texp = jnp.zeros(texp_ref.shape, F32)
    for e in range(N_EXPERTS):
        pos = pos + jnp.where(ek == e, row_start[e:e + 1, :], 0.0)
        texp = texp + jnp.where(tile_id >= tile_end[e:e + 1, :], 1.0, 0.0)
    pos_ref[...] = pos.astype(jnp.int32)
    texp_ref[...] = jnp.minimum(texp, N_EXPERTS - 1.0).astype(jnp.int32)
    nused_ref[...] = jnp.broadcast_to(tile_end[N_EXPERTS - 1:N_EXPERTS, :], nused_ref.shape).astype(jnp.int32)
    tend_ref[...] = jnp.broadcast_to(tile_end, tend_ref.shape).astype(jnp.int32)


def _plan(ek, rk, cnt, n_tiles_pad):
    t = ek.shape[1]

    def full(shape):
        return pl.BlockSpec(shape, lambda: (0,) * len(shape))

    return pl.pallas_call(
        _plan_kernel,
        in_specs=[full(ek.shape), full(rk.shape), full(cnt.shape)],
        out_specs=[full((TOP_K, t)), full((1, n_tiles_pad)), full((1, 128)), full((N_EXPERTS, 128))],
        out_shape=[jax.ShapeDtypeStruct((TOP_K, t), jnp.int32),
                   jax.ShapeDtypeStruct((1, n_tiles_pad), jnp.int32),
                   jax.ShapeDtypeStruct((1, 128), jnp.int32),
                   jax.ShapeDtypeStruct((N_EXPERTS, 128), jnp.int32)],
        compiler_params=pltpu.CompilerParams(vmem_limit_bytes=VMEM_LIMIT),
        name="plan",
    )(ek, rk, cnt)


def _sc_mesh():
    return plsc.VectorSubcoreMesh(core_axis_name="c", subcore_axis_name="s")


def _sc_dispatch(rows, pos3, n_out):
    t = rows.shape[0]
    ch = SC_CHUNK
    per_w = (t // ch) // SC_WORKERS

    @functools.partial(
        pl.kernel, out_type=jax.ShapeDtypeStruct((n_out,) + rows.shape[1:], jnp.int32), mesh=_sc_mesh(),
        scratch_types=[pltpu.VMEM((TOP_K, ch), jnp.int32), pltpu.VMEM((ch,) + rows.shape[1:], jnp.int32),
                       pltpu.SemaphoreType.DMA])
    def k(rows_hbm, pos_hbm, out_hbm, idx_v, rows_v, sem):
        wid = lax.axis_index("s") * SC_CORES + lax.axis_index("c")

        @pl.loop(0, per_w)
        def _(j):
            c = wid * per_w + j
            pltpu.sync_copy(pos_hbm.at[c], idx_v)
            pltpu.sync_copy(rows_hbm.at[pl.ds(c * ch, ch)], rows_v)
            copies = [pltpu.async_copy(rows_v, out_hbm.at[idx_v.at[kk]], sem) for kk in range(TOP_K)]
            for cp in copies:
                cp.wait()

    return k(rows, pos3)


def _sc_combine(table, pos3, wtok, t):
    ch = SC_CHUNK
    sub = SC_COMBINE_TOKENS
    lanes = SC_LANES
    slabs = ROW_SLABS
    per_w = (t // ch) // SC_WORKERS
    subs_per_chunk = ch // sub
    n_steps = per_w * subs_per_chunk

    @functools.partial(
        pl.kernel, out_type=jax.ShapeDtypeStruct((t, 2 * slabs, 128), F32), mesh=_sc_mesh(),
        scratch_types=[pltpu.VMEM((per_w, TOP_K, ch), jnp.int32),
                       pltpu.VMEM((2, TOP_K, sub, slabs, 128), jnp.int32),
                       pltpu.VMEM((2, sub, 128), F32),
                       pltpu.VMEM((sub, 2 * slabs, 128), F32),
                       pltpu.SemaphoreType.DMA((2,))],
        compiler_params=pltpu.CompilerParams(needs_layout_passes=False))
    def k(tab_hbm, pos_hbm, w_hbm, out_hbm, idx_v, rows_v, w_v, out_v, sem):
        wid = lax.axis_index("s") * SC_CORES + lax.axis_index("c")
        for j in range(per_w):
            pltpu.sync_copy(pos_hbm.at[wid * per_w + j], idx_v.at[j])

        def first_token(step):
            return (wid * per_w + step // subs_per_chunk) * ch + (step % subs_per_chunk) * sub

        def copies(step, slot):
            j = step // subs_per_chunk
            s = step % subs_per_chunk
            idx = [idx_v.at[j, kk, pl.ds(s * sub, sub)] for kk in range(TOP_K)]
            return ([pltpu.make_async_copy(tab_hbm.at[idx[kk]], rows_v.at[slot, kk], sem.at[slot])
                     for kk in range(TOP_K)]
                    + [pltpu.make_async_copy(w_hbm.at[pl.ds(first_token(step), sub)], w_v.at[slot], sem.at[slot])])

        for cp in copies(0, 0):
            cp.start()

        @pl.loop(0, n_steps)
        def _(step):
            slot = step % 2

            @pl.when(step + 1 < n_steps)
            def _():
                for cp in copies(step + 1, 1 - slot):
                    cp.start()

            for cp in copies(step, slot):
                cp.wait()

            @pl.loop(0, sub)
            def _(tt):
                wk = [w_v[slot, tt, pl.ds(kk * lanes, lanes)] for kk in range(TOP_K)]
                for sl in range(slabs):
                    @plsc.parallel_loop(0, 128, step=lanes, unroll=4)
                    def _(off):
                        acc_lo = jnp.zeros((lanes,), F32)
                        acc_hi = jnp.zeros((lanes,), F32)
                        for kk in range(TOP_K):
                            word = rows_v[slot, kk, tt, sl, pl.ds(off, lanes)]
                            lo = plsc.bitcast(word << 16, F32)
                            hi = plsc.bitcast(word & jnp.int32(-65536), F32)
                            acc_lo = acc_lo + wk[kk] * lo
                            acc_hi = acc_hi + wk[kk] * hi
                        out_v[tt, sl, pl.ds(off, lanes)] = acc_lo
                        out_v[tt, slabs + sl, pl.ds(off, lanes)] = acc_hi

            pltpu.sync_copy(out_v, out_hbm.at[pl.ds(first_token(step), sub)])

    return k(table, pos3, wtok)


def _experts_kernel(texp_ref, nused_ref, tend_ref, xs_ref, weg_hbm, weu_hbm, wed_hbm, ys_ref,
                    wg_scr, wu_scr, wd_scr, wg_buf, wu_buf, wd_buf, sem, group_scr):
    step = pl.program_id(0)
    rows = EXPERT_ROWS
    half = D_MODEL // 2
    n_used = nused_ref[0]

    def weight_copies(e, slot):
        return [pltpu.make_async_copy(weg_hbm.at[e], wg_buf.at[slot], sem.at[slot, 0]),
                pltpu.make_async_copy(weu_hbm.at[e], wu_buf.at[slot], sem.at[slot, 1]),
                pltpu.make_async_copy(wed_hbm.at[e], wd_buf.at[slot], sem.at[slot, 2])]

    def next_group(e):
        tile = tend_ref[e]
        return texp_ref[jnp.minimum(tile, n_used - 1)], tile < n_used

    def start_weights(e, slot, exists):
        @pl.when(exists)
        def _():
            for cp in weight_copies(e, slot):
                cp.start()

    @pl.when(step == 0)
    def _():
        group_scr[0] = 0
        first = texp_ref[0]
        start_weights(first, 0, True)
        second, has_second = next_group(first)
        start_weights(second, 1, has_second)

    def row_tile(tile, x_view, y_view):
        expert = texp_ref[tile]
        used = tile < n_used
        new_expert = (tile == 0) | (expert != texp_ref[jnp.maximum(tile - 1, 0)])

        @pl.when(used & new_expert)
        def _():
            group = group_scr[0]
            slot = group % WEIGHT_SLOTS
            after, has_after = next_group(expert)
            after2, has_after2 = next_group(after)
            start_weights(after2, (group + 2) % WEIGHT_SLOTS, has_after & has_after2)

            for cp in weight_copies(expert, slot):
                cp.wait()
            wg_scr[...] = wg_buf[slot].astype(BF16)
            wu_scr[...] = wu_buf[slot].astype(BF16)
            wd_scr[...] = wd_buf[slot].astype(BF16)
            group_scr[0] = group + 1

        @pl.when(used)
        def _():
            lo, hi = _unpack_pair(_load_token_words(x_view, (), rows))
            lo = lo.astype(BF16)
            hi = hi.astype(BF16)
            g = _dot(lo, wg_scr[0:half, :]) + _dot(hi, wg_scr[half:D_MODEL, :])
            u = _dot(lo, wu_scr[0:half, :]) + _dot(hi, wu_scr[half:D_MODEL, :])
            y = _dot((_silu(g) * u).astype(BF16), wd_scr[...])
            _store_token_words(y_view, _pack_pair(y[:, :half], y[:, half:]), rows)

        @pl.when(jnp.logical_not(used) & (step == (n_used - 1) // TILES_PER_STEP))
        def _():
            y_view[...] = jnp.zeros_like(y_view)

    for s in range(TILES_PER_STEP):
        view = pl.ds(s * rows * ROW_SLABS, rows * ROW_SLABS)
        row_tile(step * TILES_PER_STEP + s, xs_ref.at[view], ys_ref.at[view])


def _experts(texp, nused, tend, xs2d, weg, weu, wed, n_tiles):
    block = (TILES_PER_STEP * EXPERT_ROWS * ROW_SLABS, 128)
    hbm = pl.BlockSpec(memory_space=pl.ANY)

    def block_idx(j, te, nu, tn):
        return (jnp.minimum(j, (nu[0] - 1) // TILES_PER_STEP), 0)

    grid_spec = pltpu.PrefetchScalarGridSpec(
        num_scalar_prefetch=3,
        grid=(n_tiles // TILES_PER_STEP,),
        in_specs=[pl.BlockSpec(block, block_idx), hbm, hbm, hbm],
        out_specs=pl.BlockSpec(block, block_idx),
        scratch_shapes=[pltpu.VMEM((D_MODEL, EXPERT_DIM), BF16),
                        pltpu.VMEM((D_MODEL, EXPERT_DIM), BF16),
                        pltpu.VMEM((EXPERT_DIM, D_MODEL), BF16),
                        pltpu.VMEM((WEIGHT_SLOTS, D_MODEL, EXPERT_DIM), F32),
                        pltpu.VMEM((WEIGHT_SLOTS, D_MODEL, EXPERT_DIM), F32),
                        pltpu.VMEM((WEIGHT_SLOTS, EXPERT_DIM, D_MODEL), F32),
                        pltpu.SemaphoreType.DMA((WEIGHT_SLOTS, 3)),
                        pltpu.SMEM((1,), jnp.int32)],
    )
    return pl.pallas_call(
        _experts_kernel,
        grid_spec=grid_spec,
        out_shape=jax.ShapeDtypeStruct(xs2d.shape, jnp.int32),
        compiler_params=pltpu.CompilerParams(dimension_semantics=("arbitrary",),
                                             vmem_limit_bytes=VMEM_LIMIT),
        name="experts",
    )(texp, nused, tend, xs2d, weg, weu, wed)


def _final_kernel(x_ref, routed_ref, mod_ref, g2_ref, wsg_ref, wsu_ref, wsd_ref, fng_ref, o_ref):
    tm = x_ref.shape[0]
    x = x_ref[...]
    hb = _rms_mod(x, g2_ref[...], mod_ref[0, 3:4, :], mod_ref[0, 4:5, :]).astype(BF16)
    shared = _dot((_silu(_dot(hb, wsg_ref[...])) * _dot(hb, wsu_ref[...])).astype(BF16), wsd_ref[...])
    n_slabs = 2 * ROW_SLABS
    routed = jnp.concatenate([routed_ref[pl.ds(s, tm, stride=n_slabs), :] for s in range(n_slabs)], axis=1)
    y = x + mod_ref[0, 5:6, :] * (routed + shared)
    ms = jnp.mean(y * y, axis=-1, keepdims=True)
    o_ref[...] = y * lax.rsqrt(ms + EPS) * fng_ref[...]


def _final(x1, routed2d, mod3, norm2_g, wsg, wsu, wsd, final_g, seq_len, mod_row_of_batch):
    t = x1.shape[0]
    tm = TM_FINAL

    def mod_idx(i):
        return (mod_row_of_batch((i * tm) // seq_len), 0, 0)

    def full(a):
        return pl.BlockSpec(a.shape, lambda i: (0,) * a.ndim)

    return pl.pallas_call(
        _final_kernel,
        grid=(t // tm,),
        in_specs=[pl.BlockSpec((tm, D_MODEL), lambda i: (i, 0)),
                  pl.BlockSpec((tm * 2 * ROW_SLABS, 128), lambda i: (i, 0)),
                  pl.BlockSpec((1, 6, D_MODEL), mod_idx),
                  full(norm2_g), full(wsg), full(wsu), full(wsd), full(final_g)],
        out_specs=pl.BlockSpec((tm, D_MODEL), lambda i: (i, 0)),
        out_shape=jax.ShapeDtypeStruct((t, D_MODEL), F32),
        compiler_params=pltpu.CompilerParams(dimension_semantics=("parallel",),
                                             vmem_limit_bytes=VMEM_LIMIT),
        name="final",
    )(x1, routed2d, mod3, norm2_g, wsg, wsu, wsd, final_g)


def _moe(x1, mod3, lw, seq_len, mod_row_of_batch):
    t = x1.shape[0]
    n_tiles = TOP_K * t // EXPERT_ROWS + N_EXPERTS
    n_tiles_pad = -(-n_tiles // 128) * 128
    hp2d, ek, rk, wtok, cnt = _router(x1, mod3, lw["norm2_g"], lw["w_router_t"], lw["router_bias"],
                                      seq_len, mod_row_of_batch)
    pos, texp, nused, tend = _plan(ek, rk, cnt, n_tiles_pad)
    pos3 = pos.reshape(TOP_K, t // SC_CHUNK, SC_CHUNK).transpose(1, 0, 2)
    xs = _sc_dispatch(hp2d.reshape(t, ROW_SLABS, 128), pos3, n_tiles * EXPERT_ROWS)
    ys2d = _experts(texp.reshape(-1), nused.reshape(-1), tend[:, 0], xs.reshape(-1, 128),
                    lw["weg"], lw["weu"], lw["wed"], n_tiles)
    routed = _sc_combine(ys2d.reshape(-1, ROW_SLABS, 128), pos3, wtok, t)
    return _final(x1, routed.reshape(t * 2 * ROW_SLABS, 128), mod3, lw["norm2_g"],
                  lw["wsg"], lw["wsu"], lw["wsd"], lw["final_g"], seq_len, mod_row_of_batch)


def _dft_tables(seq_len):
    gd = FOURIER_GROUP_DIM
    kc = np.arange(gd)
    ang_c = ((kc[:, None] * kc[None, :]) % gd) * (2.0 * math.pi / gd)
    cs = np.concatenate([np.cos(ang_c), np.sin(ang_c)], axis=1) * (gd ** -0.5)
    kl = np.arange(seq_len)
    ang_l = ((kl[:, None] * kl[None, :]) % seq_len) * (2.0 * math.pi / seq_len)
    cls = np.concatenate([np.cos(ang_l), -np.sin(ang_l)], axis=1) * (seq_len ** -0.5)
    return jnp.asarray(cs.astype(np.float32), dtype=BF16), jnp.asarray(cls.astype(np.float32), dtype=BF16)


def _rope_tables(length):
    rows = length // GRID_W
    r = np.repeat(np.arange(rows, dtype=np.float32), GRID_W)
    col = np.tile(np.arange(GRID_W, dtype=np.float32), rows)
    nf = RET_HEAD_DIM // 4
    inv = (np.float32(ROPE_BASE) ** (-np.arange(nf, dtype=np.float32) / np.float32(nf))).astype(np.float32)
    ar = r[:, None] * inv[None]
    ac = col[:, None] * inv[None]
    ang = np.concatenate([ar, ar, ac, ac], axis=-1).astype(np.float64)
    sign = np.where((np.arange(RET_HEAD_DIM) & nf) == 0, -1.0, 1.0)
    return (jnp.asarray(np.cos(ang).astype(np.float32)),
            jnp.asarray((np.sin(ang) * sign[None, :]).astype(np.float32)))


def _trunk_path(x, mod3, mod_row_of_batch, s0f, s0b, rope, lw):
    batch, seq_len, _ = x.shape
    x2d = x.reshape(batch * seq_len, D_MODEL)
    uf, q, k, v, sg, gf, gr = _inproj(x2d, mod3, lw["norm1_g"], lw["w_in"], seq_len, mod_row_of_batch, rope)
    r, s_f, s_b = _retention(q, k, v, sg, lw["dec"], lw["gn_g"], s0f, s0b, batch, seq_len)
    cs, cls = _dft_tables(seq_len)
    x1 = _fnet_merge(uf, cs, cls, r, gf, gr, x2d, mod3, lw["w_four"], lw["w_ret"], lw["w_o"],
                     batch, seq_len, mod_row_of_batch)
    y = _moe(x1, mod3, lw, seq_len, mod_row_of_batch)
    return y.reshape(batch, seq_len, D_MODEL), s_f, s_b


def kernel(x_prompt, x_sample, state_ret_fwd, state_ret_bwd, c, c_ctx, w_ada, b_ada, norm1_g, norm2_g, w_in,
           ret_decay_fwd, ret_decay_bwd, ret_gn_g, w_four_out, w_ret_out, w_out, w_router, router_bias,
           w_exp_gate, w_exp_up, w_exp_down, w_shared_gate, w_shared_up, w_shared_down, final_norm_g):
    depth = w_ada.shape[0]
    assert depth == 1, "final norm is fused into the last layer's MoE kernel"
    n_ctx, n_lat = x_prompt.shape[0], x_sample.shape[0]
    cond = jnp.concatenate([c_ctx[None, :], c], axis=0)
    cond = jnp.pad(cond, ((0, (-cond.shape[0]) % 8), (0, 0)))
    rope = _rope_tables(x_sample.shape[1])
    zeros = jnp.zeros((n_ctx, N_RET_HEADS, RET_HEAD_DIM, RET_HEAD_DIM), F32)

    layer = 0
    mod = _ada(cond, w_ada[layer], b_ada[layer][None, :])
    mod3 = mod.reshape(mod.shape[0], 6, D_MODEL)
    dec = jnp.stack([ret_decay_fwd[layer], ret_decay_bwd[layer]], axis=1)
    lw = {
        "norm1_g": norm1_g[layer][None, :],
        "norm2_g": norm2_g[layer][None, :],
        "w_in": w_in[layer].astype(BF16),
        "dec": jnp.broadcast_to(dec[:, :, None], (N_RET_HEADS, 2, RET_HEAD_DIM)).astype(F32),
        "gn_g": ret_gn_g[layer][None, :],
        "w_four": w_four_out[layer].astype(BF16),
        "w_ret": w_ret_out[layer].astype(BF16),
        "w_o": w_out[layer].astype(BF16),
        "w_router_t": w_router[layer].T,
        "router_bias": router_bias[layer][:, None],
        "weg": w_exp_gate[layer],
        "weu": w_exp_up[layer],
        "wed": w_exp_down[layer],
        "wsg": w_shared_gate[layer].astype(BF16),
        "wsu": w_shared_up[layer].astype(BF16),
        "wsd": w_shared_down[layer].astype(BF16),
        "final_g": final_norm_g[None, :],
    }
    y_prompt, s_f, s_b = _trunk_path(x_prompt, mod3, lambda b: 0, zeros, zeros, None, lw)
    y_sample, _, _ = _trunk_path(x_sample, mod3, lambda b: 1 + b, state_ret_fwd[:, layer],
                                 state_ret_bwd[:, layer], rope, lw)
    return (y_prompt, y_sample, s_f[:, None], s_b[:, None])
```

```python
import functools
import math

import jax
import jax.numpy as jnp
import numpy as np
from jax import lax
from jax.experimental import pallas as pl
from jax.experimental.pallas import tpu as pltpu
from jax.experimental.pallas import tpu_sc as plsc

F32 = jnp.float32
BF16 = jnp.bfloat16

D_MODEL = 1024
GRID_W = 64
N_FOURIER_GROUPS = 8
FOURIER_GROUP_DIM = 128
N_RET_HEADS = 4
RET_HEAD_DIM = 128
RET_WIDTH = N_RET_HEADS * RET_HEAD_DIM
CHUNK = 128
N_EXPERTS = 64
N_EXPERT_GROUPS = 8
EXPERTS_PER_GROUP = N_EXPERTS // N_EXPERT_GROUPS
TOPK_GROUPS = 4
TOP_K = 8
EXPERT_DIM = 256
ROUTED_SCALE = 2.5
ROPE_BASE = 10000.0
EPS = 1e-6
Q_SCALE = RET_HEAD_DIM ** -0.5

_C_UF = (0, 1024)
_C_Q = (1024, 1536)
_C_K = (1536, 2048)
_C_V = (2048, 2560)
_C_G = (2560, 3072)
_C_GF = (3072, 4096)
_C_GR = (4096, 5120)

VMEM_LIMIT = 56 * 1024 * 1024

TM_INPROJ = 1024
TM_ROUTER = 1024
FNET_ROWS = 512
TM_FINAL = 1024
EXPERT_ROWS = 512
TILES_PER_STEP = 2
WEIGHT_SLOTS = 4
ROW_SLABS = 4
SC_CORES = 2
SC_WORKERS = 32
SC_CHUNK = 128
SC_LANES = 16
SC_COMBINE_TOKENS = 8


def _silu(x):
    return x * jax.nn.sigmoid(x)


def _dot(a, b):
    return jnp.dot(a, b, preferred_element_type=F32)


def _rms_mod(x, g, shift, scale):
    ms = jnp.mean(x * x, axis=-1, keepdims=True)
    y = x * lax.rsqrt(ms + EPS) * g
    return y * (1.0 + scale) + shift


def _ada_kernel(cond_ref, w_ref, b_ref, o_ref):
    s = _silu(cond_ref[...]).astype(BF16)
    o_ref[...] = _dot(s, w_ref[...].astype(BF16)) + b_ref[...]


def _ada(cond, w_ada, b_ada):
    rows, n = cond.shape[0], w_ada.shape[1]
    tn = 1536
    return pl.pallas_call(
        _ada_kernel,
        grid=(n // tn,),
        in_specs=[pl.BlockSpec((rows, D_MODEL), lambda j: (0, 0)),
                  pl.BlockSpec((D_MODEL, tn), lambda j: (0, j)),
                  pl.BlockSpec((1, tn), lambda j: (0, j))],
        out_specs=pl.BlockSpec((rows, tn), lambda j: (0, j)),
        out_shape=jax.ShapeDtypeStruct((rows, n), F32),
        compiler_params=pltpu.CompilerParams(vmem_limit_bytes=VMEM_LIMIT),
        name="ada",
    )(cond, w_ada, b_ada)


def _rope_head(x, cos, sin_signed, first_half):
    partner = jnp.where(first_half, pltpu.roll(x, 96, 1), pltpu.roll(x, 32, 1))
    return x * cos + partner * sin_signed


def _inproj_kernel(*refs, use_rope):
    if use_rope:
        x_ref, mod_ref, g_ref, w_ref, cos_ref, sin_ref = refs[:6]
        outs = refs[6:]
    else:
        x_ref, mod_ref, g_ref, w_ref = refs[:4]
        outs = refs[4:]
    uf_o, q_o, k_o, v_o, sg_o, gf_o, gr_o = outs

    h = _rms_mod(x_ref[...], g_ref[...], mod_ref[0, 0:1, :], mod_ref[0, 1:2, :])
    hb = h.astype(BF16)

    def proj(cols):
        return _dot(hb, w_ref[:, cols[0]:cols[1]])

    uf_o[...] = proj(_C_UF).astype(BF16)
    q = proj(_C_Q)
    k = proj(_C_K)
    if use_rope:
        cos = cos_ref[...]
        sin_signed = sin_ref[...]
        lane = lax.broadcasted_iota(jnp.int32, cos.shape, 1)
        first_half = (lane & 32) == 0
        for hd in range(N_RET_HEADS):
            sl = slice(hd * RET_HEAD_DIM, (hd + 1) * RET_HEAD_DIM)
            q_o[:, sl] = (_rope_head(q[:, sl], cos, sin_signed, first_half) * Q_SCALE).astype(BF16)
            k_o[:, sl] = _rope_head(k[:, sl], cos, sin_signed, first_half).astype(BF16)
    else:
        q_o[...] = (q * Q_SCALE).astype(BF16)
        k_o[...] = k.astype(BF16)
    v_o[...] = proj(_C_V).astype(BF16)
    sg_o[...] = _silu(proj(_C_G)).astype(BF16)
    gf_o[...] = jax.nn.sigmoid(proj(_C_GF)).astype(BF16)
    gr_o[...] = jax.nn.sigmoid(proj(_C_GR)).astype(BF16)


def _inproj(x2d, mod3, norm_g, w_in_bf, seq_len, mod_row_of_batch, rope):
    t = x2d.shape[0]
    tm = TM_INPROJ
    tiles_per_seq = max(seq_len // tm, 1)

    def mod_idx(i):
        return (mod_row_of_batch((i * tm) // seq_len), 0, 0)

    in_specs = [pl.BlockSpec((tm, D_MODEL), lambda i: (i, 0)),
                pl.BlockSpec((1, 6, D_MODEL), mod_idx),
                pl.BlockSpec((1, D_MODEL), lambda i: (0, 0)),
                pl.BlockSpec(w_in_bf.shape, lambda i: (0, 0), pipeline_mode=pl.Buffered(1))]
    args = [x2d, mod3, norm_g, w_in_bf]
    if rope is not None:
        in_specs += [pl.BlockSpec((tm, RET_HEAD_DIM), lambda i: (i % tiles_per_seq, 0))] * 2
        args += list(rope)
    widths = [1024, RET_WIDTH, RET_WIDTH, RET_WIDTH, RET_WIDTH, 1024, 1024]
    return pl.pallas_call(
        functools.partial(_inproj_kernel, use_rope=rope is not None),
        grid=(t // tm,),
        in_specs=in_specs,
        out_specs=[pl.BlockSpec((tm, w), lambda i: (i, 0)) for w in widths],
        out_shape=[jax.ShapeDtypeStruct((t, w), BF16) for w in widths],
        compiler_params=pltpu.CompilerParams(dimension_semantics=("parallel",),
                                             vmem_limit_bytes=VMEM_LIMIT),
        name="inproj",
    )(*args)


def _retention_kernel(q_ref, k_ref, v_ref, sg_ref, dec_ref, gn_ref, s0f_ref, s0b_ref,
                      r_ref, sfo_ref, sbo_ref, tab_scr, gc_scr):
    n_chunks = q_ref.shape[0] // CHUNK
    hd = RET_HEAD_DIM

    @pl.when(pl.program_id(0) == 0)
    def _():
        row = lax.broadcasted_iota(jnp.int32, (CHUNK, CHUNK), 0).astype(F32)
        col = lax.broadcasted_iota(jnp.int32, (CHUNK, CHUNK), 1).astype(F32)
        diff = row - col
        for h in range(N_RET_HEADS):
            dec = dec_ref[h]
            lg = jnp.minimum(dec, 0.0) - jnp.log1p(jnp.exp(-jnp.abs(dec)))
            lgf = lg[0:1, :]
            lgb = lg[1:2, :]
            tab_scr[h, 0] = jnp.exp(jnp.where(diff >= 0, lgf * diff, lgb * (-diff)))
            tab_scr[h, 1] = jnp.exp(lgf * (row + 1.0))
            tab_scr[h, 2] = jnp.exp(lgb * (CHUNK - row))
            tab_scr[h, 3] = jnp.exp(lgf * (CHUNK - 1.0 - col))
            tab_scr[h, 4] = jnp.exp(lgb * col)
            gc_scr[h] = jnp.exp(lg * CHUNK)

    def rows(n):
        return slice(n * CHUNK, (n + 1) * CHUNK)

    for h in range(N_RET_HEADS):
        cols = slice(h * hd, (h + 1) * hd)
        decay, qw_f, qw_b, kwt_f, kwt_b = (tab_scr[h, i] for i in range(5))
        gc = gc_scr[h]
        gc_f = gc[0:1, :]
        gc_b = gc[1:2, :]

        kv_f, kv_b = [], []
        for n in range(n_chunks):
            kt = k_ref[rows(n), cols].astype(F32).T
            vn = v_ref[rows(n), cols]
            kv_f.append(_dot((kt * kwt_f).astype(BF16), vn))
            kv_b.append(_dot((kt * kwt_b).astype(BF16), vn))

        s = s0f_ref[h]
        prev_f = []
        for n in range(n_chunks):
            prev_f.append(s.astype(BF16))
            s = gc_f * s + kv_f[n]
        sfo_ref[h] = s
        s = s0b_ref[h]
        prev_b = [None] * n_chunks
        for n in reversed(range(n_chunks)):
            prev_b[n] = s.astype(BF16)
            s = gc_b * s + kv_b[n]
        sbo_ref[h] = s

        gn = gn_ref[:, cols]
        for n in range(n_chunks):
            qn = q_ref[rows(n), cols]
            qf = qn.astype(F32)
            scores = lax.dot_general(qn, k_ref[rows(n), cols], (((1,), (1,)), ((), ())),
                                     preferred_element_type=F32)
            o = _dot((scores * decay).astype(BF16), v_ref[rows(n), cols])
            o = o + _dot((qf * qw_f).astype(BF16), prev_f[n])
            o = o + _dot((qf * qw_b).astype(BF16), prev_b[n])
            mu = jnp.mean(o, axis=-1, keepdims=True)
            d = o - mu
            var = jnp.mean(d * d, axis=-1, keepdims=True)
            on = d * lax.rsqrt(var + EPS) * gn
            r_ref[rows(n), cols] = (on * sg_ref[rows(n), cols].astype(F32)).astype(BF16)


def _retention(q, k, v, sg, dec, gn_g, s0f, s0b, batch, seq_len):
    hd = RET_HEAD_DIM
    tok_spec = pl.BlockSpec((seq_len, RET_WIDTH), lambda b: (b, 0))
    st_spec = pl.BlockSpec((None, N_RET_HEADS, hd, hd), lambda b: (b, 0, 0, 0))
    st_shape = jax.ShapeDtypeStruct((batch, N_RET_HEADS, hd, hd), F32)
    return pl.pallas_call(
        _retention_kernel,
        grid=(batch,),
        in_specs=[tok_spec, tok_spec, tok_spec, tok_spec,
                  pl.BlockSpec(dec.shape, lambda b: (0, 0, 0)),
                  pl.BlockSpec(gn_g.shape, lambda b: (0, 0)),
                  st_spec, st_spec],
        out_specs=[tok_spec, st_spec, st_spec],
        out_shape=[jax.ShapeDtypeStruct((batch * seq_len, RET_WIDTH), BF16), st_shape, st_shape],
        scratch_shapes=[pltpu.VMEM((N_RET_HEADS, 5, CHUNK, CHUNK), F32),
                        pltpu.VMEM((N_RET_HEADS, 2, hd), F32)],
        compiler_params=pltpu.CompilerParams(dimension_semantics=("arbitrary",),
                                             vmem_limit_bytes=VMEM_LIMIT),
        name="retention",
    )(q, k, v, sg, dec, gn_g, s0f, s0b)


def _fnet_merge_kernel(uf_ref, cs_ref, cls_ref, r_ref, gf_ref, gr_ref, x_ref, mod_ref, wf_ref, wr_ref, wo_ref,
                       o_ref, xcs_ref):
    seq_len = uf_ref.shape[0]
    gd = FOURIER_GROUP_DIM

    @pl.when(pl.program_id(1) == 0)
    def _():
        for g in range(N_FOURIER_GROUPS):
            x = _dot(uf_ref[:, g * gd:(g + 1) * gd], cs_ref[...])
            xcs_ref[0:seq_len, g * gd:(g + 1) * gd] = x[:, :gd].astype(BF16)
            xcs_ref[seq_len:2 * seq_len, g * gd:(g + 1) * gd] = x[:, gd:].astype(BF16)

    f_mix = _dot(cls_ref[...], xcs_ref[...]).astype(BF16)
    f_out = _dot(f_mix, wf_ref[...])
    r_out = _dot(r_ref[...], wr_ref[...])
    merged = gf_ref[...].astype(F32) * f_out + gr_ref[...].astype(F32) * r_out
    mix = _dot(merged.astype(BF16), wo_ref[...])
    o_ref[...] = x_ref[...] + mod_ref[0, 2:3, :] * mix


def _fnet_merge(uf, cs, cls, r, gf, gr, x2d, mod3, w_four, w_ret, w_o, batch, seq_len, mod_row_of_batch):
    rb = min(FNET_ROWS, seq_len)
    nr = seq_len // rb

    def tok(w):
        return pl.BlockSpec((rb, w), lambda b, i: (b * nr + i, 0))

    def full(a):
        return pl.BlockSpec(a.shape, lambda b, i: (0, 0))

    return pl.pallas_call(
        _fnet_merge_kernel,
        grid=(batch, nr),
        in_specs=[pl.BlockSpec((seq_len, D_MODEL), lambda b, i: (b, 0)),
                  full(cs),
                  pl.BlockSpec((rb, 2 * seq_len), lambda b, i: (i, 0)),
                  tok(RET_WIDTH), tok(D_MODEL), tok(D_MODEL), tok(D_MODEL),
                  pl.BlockSpec((1, 6, D_MODEL), lambda b, i: (mod_row_of_batch(b), 0, 0)),
                  full(w_four), full(w_ret), full(w_o)],
        out_specs=tok(D_MODEL),
        out_shape=jax.ShapeDtypeStruct((batch * seq_len, D_MODEL), F32),
        scratch_shapes=[pltpu.VMEM((2 * seq_len, D_MODEL), BF16)],
        compiler_params=pltpu.CompilerParams(dimension_semantics=("parallel", "arbitrary"),
                                             vmem_limit_bytes=VMEM_LIMIT),
        name="fnet_merge",
    )(uf, cs, cls, r, gf, gr, x2d, mod3, w_four, w_ret, w_o)


def _pack_pair(lo_f32, hi_f32):
    lo = lax.bitcast_convert_type(lo_f32.astype(BF16).astype(F32), jnp.uint32)
    hi = lax.bitcast_convert_type(hi_f32.astype(BF16).astype(F32), jnp.uint32)
    return lax.bitcast_convert_type((lo >> 16) | hi, jnp.int32)


def _unpack_pair(words_i32):
    w = lax.bitcast_convert_type(words_i32, jnp.uint32)
    lo = lax.bitcast_convert_type(w << 16, F32)
    hi = lax.bitcast_convert_type(w & jnp.uint32(0xFFFF0000), F32)
    return lo, hi


def _load_token_words(ref, lead, n_tok):
    parts = []
    for s in range(ROW_SLABS):
        idx = (pl.ds(s, n_tok, stride=ROW_SLABS), slice(None))
        parts.append(ref[lead + idx] if lead else ref[idx])
    return jnp.concatenate(parts, axis=1)


def _store_token_words(ref, words, n_tok):
    for s in range(ROW_SLABS):
        ref[pl.ds(s, n_tok, stride=ROW_SLABS), :] = words[:, s * 128:(s + 1) * 128]


def _route(scores, biased):
    tokens = scores.shape[1]
    neg = -jnp.inf
    epg = EXPERTS_PER_GROUP
    iota_g = lax.broadcasted_iota(jnp.int32, (epg, tokens), 0).astype(F32)

    def pick_first_max(cur, iota, size):
        m = jnp.max(cur, axis=0, keepdims=True)
        idx = jnp.min(jnp.where(cur == m, iota, float(size)), axis=0, keepdims=True)
        return m, idx, iota == idx

    group_scores = []
    for g in range(N_EXPERT_GROUPS):
        vals = biased[g * epg:(g + 1) * epg, :]
        m1, _, hit = pick_first_max(vals, iota_g, epg)
        m2 = jnp.max(jnp.where(hit, neg, vals), axis=0, keepdims=True)
        group_scores.append(m1 + m2)
    cur = jnp.concatenate(group_scores, axis=0)
    group_sel = jnp.zeros_like(cur)
    for _ in range(TOPK_GROUPS):
        _, _, hit = pick_first_max(cur, iota_g, N_EXPERT_GROUPS)
        group_sel = jnp.where(hit, 1.0, group_sel)
        cur = jnp.where(hit, neg, cur)
    masked = jnp.concatenate(
        [jnp.where(group_sel[g:g + 1, :] > 0.0, biased[g * epg:(g + 1) * epg, :], neg)
         for g in range(N_EXPERT_GROUPS)], axis=0)
    iota_e = lax.broadcasted_iota(jnp.int32, masked.shape, 0).astype(F32)
    sel = jnp.zeros_like(masked)
    cur = masked
    picks = []
    for _ in range(TOP_K):
        _, idx, hit = pick_first_max(cur, iota_e, N_EXPERTS)
        picks.append(idx)
        sel = jnp.where(hit, 1.0, sel)
        cur = jnp.where(hit, neg, cur)
    w = scores * sel
    return w / jnp.sum(w, axis=0, keepdims=True) * ROUTED_SCALE, sel, picks


def _router_kernel(x_ref, mod_ref, g2_ref, wrt_ref, rb_ref, hp_ref, ek_ref, rk_ref, wt_ref, cnt_ref,
                   run_scr, earlier_scr):
    tm = x_ref.shape[0]

    @pl.when(pl.program_id(0) == 0)
    def _():
        run_scr[...] = jnp.zeros_like(run_scr)
        earlier = (lax.broadcasted_iota(jnp.int32, (tm, tm), 0) < lax.broadcasted_iota(jnp.int32, (tm, tm), 1))
        earlier_scr[...] = jnp.where(earlier, 1.0, 0.0).astype(BF16)

    h = _rms_mod(x_ref[...], g2_ref[...], mod_ref[0, 3:4, :], mod_ref[0, 4:5, :])
    half = D_MODEL // 2
    _store_token_words(hp_ref, _pack_pair(h[:, :half], h[:, half:]), tm)

    def split(a):
        hi = a.astype(BF16)
        return hi, (a - hi.astype(F32)).astype(BF16)

    def dot_nt(a, b):
        return lax.dot_general(a, b, (((1,), (1,)), ((), ())), preferred_element_type=F32)

    h_hi, h_lo = split(h)
    w_hi, w_lo = split(wrt_ref[...])
    logits_t = dot_nt(w_hi, h_hi) + (dot_nt(w_hi, h_lo) + dot_nt(w_lo, h_hi))
    scores = jax.nn.sigmoid(logits_t)
    comb_t, sel, picks = _route(scores, scores + rb_ref[...])

    rank_t = _dot(sel.astype(BF16), earlier_scr[...]) + run_scr[...]
    run_scr[...] += jnp.sum(sel, axis=1, keepdims=True)
    cnt_ref[...] = jnp.broadcast_to(run_scr[...], cnt_ref.shape)

    iota_e = lax.broadcasted_iota(jnp.int32, sel.shape, 0).astype(F32)
    ranks, weights = [], []
    for idx in picks:
        hit = iota_e == idx
        ranks.append(jnp.sum(jnp.where(hit, rank_t, 0.0), axis=0, keepdims=True))
        weights.append(jnp.sum(jnp.where(hit, comb_t, 0.0), axis=0, keepdims=True))
    ek_ref[...] = jnp.concatenate(picks, axis=0).astype(jnp.int32)
    rk_ref[...] = jnp.concatenate(ranks, axis=0).astype(jnp.int32)
    w_rep = jnp.concatenate([jnp.broadcast_to(w, (SC_LANES, tm)) for w in weights], axis=0)
    wt_ref[...] = w_rep.T


def _router(x1, mod3, norm2_g, w_router_t, router_bias, seq_len, mod_row_of_batch):
    t = x1.shape[0]
    tm = TM_ROUTER

    def mod_idx(i):
        return (mod_row_of_batch((i * tm) // seq_len), 0, 0)

    def full(a):
        return pl.BlockSpec(a.shape, lambda i: (0,) * a.ndim)

    return pl.pallas_call(
        _router_kernel,
        grid=(t // tm,),
        in_specs=[pl.BlockSpec((tm, D_MODEL), lambda i: (i, 0)),
                  pl.BlockSpec((1, 6, D_MODEL), mod_idx),
                  full(norm2_g), full(w_router_t), full(router_bias)],
        out_specs=[pl.BlockSpec((tm * ROW_SLABS, 128), lambda i: (i, 0)),
                   pl.BlockSpec((TOP_K, tm), lambda i: (0, i)),
                   pl.BlockSpec((TOP_K, tm), lambda i: (0, i)),
                   pl.BlockSpec((tm, 128), lambda i: (i, 0)),
                   pl.BlockSpec((N_EXPERTS, 128), lambda i: (0, 0))],
        out_shape=[jax.ShapeDtypeStruct((t * ROW_SLABS, 128), jnp.int32),
                   jax.ShapeDtypeStruct((TOP_K, t), jnp.int32),
                   jax.ShapeDtypeStruct((TOP_K, t), jnp.int32),
                   jax.ShapeDtypeStruct((t, 128), F32),
                   jax.ShapeDtypeStruct((N_EXPERTS, 128), F32)],
        scratch_shapes=[pltpu.VMEM((N_EXPERTS, 1), F32), pltpu.VMEM((tm, tm), BF16)],
        compiler_params=pltpu.CompilerParams(dimension_semantics=("arbitrary",),
                                             vmem_limit_bytes=VMEM_LIMIT),
        name="router",
    )(x1, mod3, norm2_g, w_router_t, router_bias)


def _plan_kernel(ek_ref, rk_ref, cnt_ref, pos_ref, texp_ref, nused_ref, tend_ref):
    rows = float(EXPERT_ROWS)
    cnt = cnt_ref[:, 0:1]
    tiles = jnp.floor((cnt + (rows - 1.0)) / rows)
    before = (lax.broadcasted_iota(jnp.int32, (N_EXPERTS, N_EXPERTS), 1)
              < lax.broadcasted_iota(jnp.int32, (N_EXPERTS, N_EXPERTS), 0))
    tile_start = jnp.dot(jnp.where(before, 1.0, 0.0), jnp.broadcast_to(tiles, (N_EXPERTS, 128)),
                         precision=lax.Precision.HIGHEST, preferred_element_type=F32)[:, 0:1]
    tile_end = tile_start + tiles
    row_start = tile_start * rows

    ek = ek_ref[...]
    pos = rk_ref[...].astype(F32)
    tile_id = lax.broadcasted_iota(jnp.int32, texp_ref.shape, 1).astype(F32)
    texp = jnp.zeros(texp_ref.shape, F32)
    for e in range(N_EXPERTS):
        pos = pos + jnp.where(ek == e, row_start[e:e + 1, :], 0.0)
        texp = texp + jnp.where(tile_id >= tile_end[e:e + 1, :], 1.0, 0.0)
    pos_ref[...] = pos.astype(jnp.int32)
    texp_ref[...] = jnp.minimum(texp, N_EXPERTS - 1.0).astype(jnp.int32)
    nused_ref[...] = jnp.broadcast_to(tile_end[N_EXPERTS - 1:N_EXPERTS, :], nused_ref.shape).astype(jnp.int32)
    tend_ref[...] = jnp.broadcast_to(tile_end, tend_ref.shape).astype(jnp.int32)


def _plan(ek, rk, cnt, n_tiles_pad):
    t = ek.shape[1]

    def full(shape):
        return pl.BlockSpec(shape, lambda: (0,) * len(shape))

    return pl.pallas_call(
        _plan_kernel,
        in_specs=[full(ek.shape), full(rk.shape), full(cnt.shape)],
        out_specs=[full((TOP_K, t)), full((1, n_tiles_pad)), full((1, 128)), full((N_EXPERTS, 128))],
        out_shape=[jax.ShapeDtypeStruct((TOP_K, t), jnp.int32),
                   jax.ShapeDtypeStruct((1, n_tiles_pad), jnp.int32),
                   jax.ShapeDtypeStruct((1, 128), jnp.int32),
                   jax.ShapeDtypeStruct((N_EXPERTS, 128), jnp.int32)],
        compiler_params=pltpu.CompilerParams(vmem_limit_bytes=VMEM_LIMIT),
        name="plan",
    )(ek, rk, cnt)


def _sc_mesh():
    return plsc.VectorSubcoreMesh(core_axis_name="c", subcore_axis_name="s")


def _sc_dispatch(rows, pos3, n_out):
    t = rows.shape[0]
    ch = SC_CHUNK
    per_w = (t // ch) // SC_WORKERS

    @functools.partial(
        pl.kernel, out_type=jax.ShapeDtypeStruct((n_out,) + rows.shape[1:], jnp.int32), mesh=_sc_mesh(),
        scratch_types=[pltpu.VMEM((TOP_K, ch), jnp.int32), pltpu.VMEM((ch,) + rows.shape[1:], jnp.int32),
                       pltpu.SemaphoreType.DMA])
    def k(rows_hbm, pos_hbm, out_hbm, idx_v, rows_v, sem):
        wid = lax.axis_index("s") * SC_CORES + lax.axis_index("c")

        @pl.loop(0, per_w)
        def _(j):
            c = wid * per_w + j
            pltpu.sync_copy(pos_hbm.at[c], idx_v)
            pltpu.sync_copy(rows_hbm.at[pl.ds(c * ch, ch)], rows_v)
            copies = [pltpu.async_copy(rows_v, out_hbm.at[idx_v.at[kk]], sem) for kk in range(TOP_K)]
            for cp in copies:
                cp.wait()

    return k(rows, pos3)


def _sc_combine(table, pos3, wtok, t):
    ch = SC_CHUNK
    sub = SC_COMBINE_TOKENS
    lanes = SC_LANES
    slabs = ROW_SLABS
    per_w = (t // ch) // SC_WORKERS
    subs_per_chunk = ch // sub
    n_steps = per_w * subs_per_chunk

    @functools.partial(
        pl.kernel, out_type=jax.ShapeDtypeStruct((t, 2 * slabs, 128), F32), mesh=_sc_mesh(),
        scratch_types=[pltpu.VMEM((per_w, TOP_K, ch), jnp.int32),
                       pltpu.VMEM((2, TOP_K, sub, slabs, 128), jnp.int32),
                       pltpu.VMEM((2, sub, 128), F32),
                       pltpu.VMEM((sub, 2 * slabs, 128), F32),
                       pltpu.SemaphoreType.DMA((2,))],
        compiler_params=pltpu.CompilerParams(needs_layout_passes=False))
    def k(tab_hbm, pos_hbm, w_hbm, out_hbm, idx_v, rows_v, w_v, out_v, sem):
        wid = lax.axis_index("s") * SC_CORES + lax.axis_index("c")
        for j in range(per_w):
            pltpu.sync_copy(pos_hbm.at[wid * per_w + j], idx_v.at[j])

        def first_token(step):
            return (wid * per_w + step // subs_per_chunk) * ch + (step % subs_per_chunk) * sub

        def copies(step, slot):
            j = step // subs_per_chunk
            s = step % subs_per_chunk
            idx = [idx_v.at[j, kk, pl.ds(s * sub, sub)] for kk in range(TOP_K)]
            return ([pltpu.make_async_copy(tab_hbm.at[idx[kk]], rows_v.at[slot, kk], sem.at[slot])
                     for kk in range(TOP_K)]
                    + [pltpu.make_async_copy(w_hbm.at[pl.ds(first_token(step), sub)], w_v.at[slot], sem.at[slot])])

        for cp in copies(0, 0):
            cp.start()

        @pl.loop(0, n_steps)
        def _(step):
            slot = step % 2

            @pl.when(step + 1 < n_steps)
            def _():
                for cp in copies(step + 1, 1 - slot):
                    cp.start()

            for cp in copies(step, slot):
                cp.wait()

            @pl.loop(0, sub)
            def _(tt):
                wk = [w_v[slot, tt, pl.ds(kk * lanes, lanes)] for kk in range(TOP_K)]
                for sl in range(slabs):
                    @plsc.parallel_loop(0, 128, step=lanes, unroll=4)
                    def _(off):
                        acc_lo = jnp.zeros((lanes,), F32)
                        acc_hi = jnp.zeros((lanes,), F32)
                        for kk in range(TOP_K):
                            word = rows_v[slot, kk, tt, sl, pl.ds(off, lanes)]
                            lo = plsc.bitcast(word << 16, F32)
                            hi = plsc.bitcast(word & jnp.int32(-65536), F32)
                            acc_lo = acc_lo + wk[kk] * lo
                            acc_hi = acc_hi + wk[kk] * hi
                        out_v[tt, sl, pl.ds(off, lanes)] = acc_lo
                        out_v[tt, slabs + sl, pl.ds(off, lanes)] = acc_hi

            pltpu.sync_copy(out_v, out_hbm.at[pl.ds(first_token(step), sub)])

    return k(table, pos3, wtok)


def _experts_kernel(texp_ref, nused_ref, tend_ref, xs_ref, weg_hbm, weu_hbm, wed_hbm, ys_ref,
                    wg_scr, wu_scr, wd_scr, wg_buf, wu_buf, wd_buf, sem, group_scr):
    step = pl.program_id(0)
    rows = EXPERT_ROWS
    half = D_MODEL // 2
    n_used = nused_ref[0]

    def weight_copies(e, slot):
        return [pltpu.make_async_copy(weg_hbm.at[e], wg_buf.at[slot], sem.at[slot, 0]),
                pltpu.make_async_copy(weu_hbm.at[e], wu_buf.at[slot], sem.at[slot, 1]),
                pltpu.make_async_copy(wed_hbm.at[e], wd_buf.at[slot], sem.at[slot, 2])]

    def next_group(e):
        tile = tend_ref[e]
        return texp_ref[jnp.minimum(tile, n_used - 1)], tile < n_used

    def start_weights(e, slot, exists):
        @pl.when(exists)
        def _():
            for cp in weight_copies(e, slot):
                cp.start()

    @pl.when(step == 0)
    def _():
        group_scr[0] = 0
        e, exists = texp_ref[0], True
        for slot in range(WEIGHT_SLOTS - 1):
            start_weights(e, slot, exists)
            nxt, has_next = next_group(e)
            e, exists = nxt, exists & has_next

    def row_tile(tile, x_view, y_view):
        expert = texp_ref[tile]
        used = tile < n_used
        new_expert = (tile == 0) | (expert != texp_ref[jnp.maximum(tile - 1, 0)])

        @pl.when(used & new_expert)
        def _():
            group = group_scr[0]
            slot = group % WEIGHT_SLOTS
            ahead, exists = expert, True
            for _ in range(WEIGHT_SLOTS - 1):
                nxt, has_next = next_group(ahead)
                ahead, exists = nxt, exists & has_next
            start_weights(ahead, (group + WEIGHT_SLOTS - 1) % WEIGHT_SLOTS, exists)

            for cp in weight_copies(expert, slot):
                cp.wait()
            wg_scr[...] = wg_buf[slot].astype(BF16)
            wu_scr[...] = wu_buf[slot].astype(BF16)
            wd_scr[...] = wd_buf[slot].astype(BF16)
            group_scr[0] = group + 1

        @pl.when(used)
        def _():
            lo, hi = _unpack_pair(_load_token_words(x_view, (), rows))
            lo = lo.astype(BF16)
            hi = hi.astype(BF16)
            g = _dot(lo, wg_scr[0:half, :]) + _dot(hi, wg_scr[half:D_MODEL, :])
            u = _dot(lo, wu_scr[0:half, :]) + _dot(hi, wu_scr[half:D_MODEL, :])
            y = _dot((_silu(g) * u).astype(BF16), wd_scr[...])
            _store_token_words(y_view, _pack_pair(y[:, :half], y[:, half:]), rows)

        @pl.when(jnp.logical_not(used) & (step == (n_used - 1) // TILES_PER_STEP))
        def _():
            y_view[...] = jnp.zeros_like(y_view)

    for s in range(TILES_PER_STEP):
        view = pl.ds(s * rows * ROW_SLABS, rows * ROW_SLABS)
        row_tile(step * TILES_PER_STEP + s, xs_ref.at[view], ys_ref.at[view])


def _experts(texp, nused, tend, xs2d, weg, weu, wed, n_tiles):
    block = (TILES_PER_STEP * EXPERT_ROWS * ROW_SLABS, 128)
    hbm = pl.BlockSpec(memory_space=pl.ANY)

    def block_idx(j, te, nu, tn):
        return (jnp.minimum(j, (nu[0] - 1) // TILES_PER_STEP), 0)

    grid_spec = pltpu.PrefetchScalarGridSpec(
        num_scalar_prefetch=3,
        grid=(n_tiles // TILES_PER_STEP,),
        in_specs=[pl.BlockSpec(block, block_idx), hbm, hbm, hbm],
        out_specs=pl.BlockSpec(block, block_idx),
        scratch_shapes=[pltpu.VMEM((D_MODEL, EXPERT_DIM), BF16),
                        pltpu.VMEM((D_MODEL, EXPERT_DIM), BF16),
                        pltpu.VMEM((EXPERT_DIM, D_MODEL), BF16),
                        pltpu.VMEM((WEIGHT_SLOTS, D_MODEL, EXPERT_DIM), F32),
                        pltpu.VMEM((WEIGHT_SLOTS, D_MODEL, EXPERT_DIM), F32),
                        pltpu.VMEM((WEIGHT_SLOTS, EXPERT_DIM, D_MODEL), F32),
                        pltpu.SemaphoreType.DMA((WEIGHT_SLOTS, 3)),
                        pltpu.SMEM((1,), jnp.int32)],
    )
    return pl.pallas_call(
        _experts_kernel,
        grid_spec=grid_spec,
        out_shape=jax.ShapeDtypeStruct(xs2d.shape, jnp.int32),
        compiler_params=pltpu.CompilerParams(dimension_semantics=("arbitrary",),
                                             vmem_limit_bytes=VMEM_LIMIT),
        name="experts",
    )(texp, nused, tend, xs2d, weg, weu, wed)


def _final_kernel(x_ref, routed_ref, mod_ref, g2_ref, wsg_ref, wsu_ref, wsd_ref, fng_ref, o_ref):
    tm = x_ref.shape[0]
    x = x_ref[...]
    hb = _rms_mod(x, g2_ref[...], mod_ref[0, 3:4, :], mod_ref[0, 4:5, :]).astype(BF16)
    shared = _dot((_silu(_dot(hb, wsg_ref[...])) * _dot(hb, wsu_ref[...])).astype(BF16), wsd_ref[...])
    n_slabs = 2 * ROW_SLABS
    routed = jnp.concatenate([routed_ref[pl.ds(s, tm, stride=n_slabs), :] for s in range(n_slabs)], axis=1)
    y = x + mod_ref[0, 5:6, :] * (routed + shared)
    ms = jnp.mean(y * y, axis=-1, keepdims=True)
    o_ref[...] = y * lax.rsqrt(ms + EPS) * fng_ref[...]


def _final(x1, routed2d, mod3, norm2_g, wsg, wsu, wsd, final_g, seq_len, mod_row_of_batch):
    t = x1.shape[0]
    tm = TM_FINAL

    def mod_idx(i):
        return (mod_row_of_batch((i * tm) // seq_len), 0, 0)

    def full(a):
        return pl.BlockSpec(a.shape, lambda i: (0,) * a.ndim)

    return pl.pallas_call(
        _final_kernel,
        grid=(t // tm,),
        in_specs=[pl.BlockSpec((tm, D_MODEL), lambda i: (i, 0)),
                  pl.BlockSpec((tm * 2 * ROW_SLABS, 128), lambda i: (i, 0)),
                  pl.BlockSpec((1, 6, D_MODEL), mod_idx),
                  full(norm2_g), full(wsg), full(wsu), full(wsd), full(final_g)],
        out_specs=pl.BlockSpec((tm, D_MODEL), lambda i: (i, 0)),
        out_shape=jax.ShapeDtypeStruct((t, D_MODEL), F32),
        compiler_params=pltpu.CompilerParams(dimension_semantics=("parallel",),
                                             vmem_limit_bytes=VMEM_LIMIT),
        name="final",
    )(x1, routed2d, mod3, norm2_g, wsg, wsu, wsd, final_g)


def _moe(x1, mod3, lw, seq_len, mod_row_of_batch):
    t = x1.shape[0]
    n_tiles = TOP_K * t // EXPERT_ROWS + N_EXPERTS
    n_tiles_pad = -(-n_tiles // 128) * 128
    hp2d, ek, rk, wtok, cnt = _router(x1, mod3, lw["norm2_g"], lw["w_router_t"], lw["router_bias"],
                                      seq_len, mod_row_of_batch)
    pos, texp, nused, tend = _plan(ek, rk, cnt, n_tiles_pad)
    pos3 = pos.reshape(TOP_K, t // SC_CHUNK, SC_CHUNK).transpose(1, 0, 2)
    xs = _sc_dispatch(hp2d.reshape(t, ROW_SLABS, 128), pos3, n_tiles * EXPERT_ROWS)
    ys2d = _experts(texp.reshape(-1), nused.reshape(-1), tend[:, 0], xs.reshape(-1, 128),
                    lw["weg"], lw["weu"], lw["wed"], n_tiles)
    routed = _sc_combine(ys2d.reshape(-1, ROW_SLABS, 128), pos3, wtok, t)
    return _final(x1, routed.reshape(t * 2 * ROW_SLABS, 128), mod3, lw["norm2_g"],
                  lw["wsg"], lw["wsu"], lw["wsd"], lw["final_g"], seq_len, mod_row_of_batch)


def _dft_tables(seq_len):
    gd = FOURIER_GROUP_DIM
    kc = np.arange(gd)
    ang_c = ((kc[:, None] * kc[None, :]) % gd) * (2.0 * math.pi / gd)
    cs = np.concatenate([np.cos(ang_c), np.sin(ang_c)], axis=1) * (gd ** -0.5)
    kl = np.arange(seq_len)
    ang_l = ((kl[:, None] * kl[None, :]) % seq_len) * (2.0 * math.pi / seq_len)
    cls = np.concatenate([np.cos(ang_l), -np.sin(ang_l)], axis=1) * (seq_len ** -0.5)
    return jnp.asarray(cs.astype(np.float32), dtype=BF16), jnp.asarray(cls.astype(np.float32), dtype=BF16)


def _rope_tables(length):
    rows = length // GRID_W
    r = np.repeat(np.arange(rows, dtype=np.float32), GRID_W)
    col = np.tile(np.arange(GRID_W, dtype=np.float32), rows)
    nf = RET_HEAD_DIM // 4
    inv = (np.float32(ROPE_BASE) ** (-np.arange(nf, dtype=np.float32) / np.float32(nf))).astype(np.float32)
    ar = r[:, None] * inv[None]
    ac = col[:, None] * inv[None]
    ang = np.concatenate([ar, ar, ac, ac], axis=-1).astype(np.float64)
    sign = np.where((np.arange(RET_HEAD_DIM) & nf) == 0, -1.0, 1.0)
    return (jnp.asarray(np.cos(ang).astype(np.float32)),
            jnp.asarray((np.sin(ang) * sign[None, :]).astype(np.float32)))


def _trunk_path(x, mod3, mod_row_of_batch, s0f, s0b, rope, lw):
    batch, seq_len, _ = x.shape
    x2d = x.reshape(batch * seq_len, D_MODEL)
    uf, q, k, v, sg, gf, gr = _inproj(x2d, mod3, lw["norm1_g"], lw["w_in"], seq_len, mod_row_of_batch, rope)
    r, s_f, s_b = _retention(q, k, v, sg, lw["dec"], lw["gn_g"], s0f, s0b, batch, seq_len)
    cs, cls = _dft_tables(seq_len)
    x1 = _fnet_merge(uf, cs, cls, r, gf, gr, x2d, mod3, lw["w_four"], lw["w_ret"], lw["w_o"],
                     batch, seq_len, mod_row_of_batch)
    y = _moe(x1, mod3, lw, seq_len, mod_row_of_batch)
    return y.reshape(batch, seq_len, D_MODEL), s_f, s_b


def kernel(x_prompt, x_sample, state_ret_fwd, state_ret_bwd, c, c_ctx, w_ada, b_ada, norm1_g, norm2_g, w_in,
           ret_decay_fwd, ret_decay_bwd, ret_gn_g, w_four_out, w_ret_out, w_out, w_router, router_bias,
           w_exp_gate, w_exp_up, w_exp_down, w_shared_gate, w_shared_up, w_shared_down, final_norm_g):
    depth = w_ada.shape[0]
    assert depth == 1, "final norm is fused into the last layer's MoE kernel"
    n_ctx, n_lat = x_prompt.shape[0], x_sample.shape[0]
    cond = jnp.concatenate([c_ctx[None, :], c], axis=0)
    cond = jnp.pad(cond, ((0, (-cond.shape[0]) % 8), (0, 0)))
    rope = _rope_tables(x_sample.shape[1])
    zeros = jnp.zeros((n_ctx, N_RET_HEADS, RET_HEAD_DIM, RET_HEAD_DIM), F32)

    layer = 0
    mod = _ada(cond, w_ada[layer], b_ada[layer][None, :])
    mod3 = mod.reshape(mod.shape[0], 6, D_MODEL)
    dec = jnp.stack([ret_decay_fwd[layer], ret_decay_bwd[layer]], axis=1)
    lw = {
        "norm1_g": norm1_g[layer][None, :],
        "norm2_g": norm2_g[layer][None, :],
        "w_in": w_in[layer].astype(BF16),
        "dec": jnp.broadcast_to(dec[:, :, None], (N_RET_HEADS, 2, RET_HEAD_DIM)).astype(F32),
        "gn_g": ret_gn_g[layer][None, :],
        "w_four": w_four_out[layer].astype(BF16),
        "w_ret": w_ret_out[layer].astype(BF16),
        "w_o": w_out[layer].astype(BF16),
        "w_router_t": w_router[layer].T,
        "router_bias": router_bias[layer][:, None],
        "weg": w_exp_gate[layer],
        "weu": w_exp_up[layer],
        "wed": w_exp_down[layer],
        "wsg": w_shared_gate[layer].astype(BF16),
        "wsu": w_shared_up[layer].astype(BF16),
        "wsd": w_shared_down[layer].astype(BF16),
        "final_g": final_norm_g[None, :],
    }
    y_prompt, s_f, s_b = _trunk_path(x_prompt, mod3, lambda b: 0, zeros, zeros, None, lw)
    y_sample, _, _ = _trunk_path(x_sample, mod3, lambda b: 1 + b, state_ret_fwd[:, layer],
                                 state_ret_bwd[:, layer], rope, lw)
    return (y_prompt, y_sample, s_f[:, None], s_b[:, None])
```

```python
import functools
import math

import jax
import jax.numpy as jnp
import numpy as np
from jax import lax
from jax.experimental import pallas as pl
from jax.experimental.pallas import tpu as pltpu
from jax.experimental.pallas import tpu_sc as plsc

F32 = jnp.float32
BF16 = jnp.bfloat16

D_MODEL = 1024
GRID_W = 64
N_FOURIER_GROUPS = 8
FOURIER_GROUP_DIM = 128
N_RET_HEADS = 4
RET_HEAD_DIM = 128
RET_WIDTH = N_RET_HEADS * RET_HEAD_DIM
CHUNK = 128
N_EXPERTS = 64
N_EXPERT_GROUPS = 8
EXPERTS_PER_GROUP = N_EXPERTS // N_EXPERT_GROUPS
TOPK_GROUPS = 4
TOP_K = 8
EXPERT_DIM = 256
ROUTED_SCALE = 2.5
ROPE_BASE = 10000.0
EPS = 1e-6
Q_SCALE = RET_HEAD_DIM ** -0.5

_C_UF = (0, 1024)
_C_Q = (1024, 1536)
_C_K = (1536, 2048)
_C_V = (2048, 2560)
_C_G = (2560, 3072)
_C_GF = (3072, 4096)
_C_GR = (4096, 5120)

VMEM_LIMIT = 56 * 1024 * 1024

TM_INPROJ = 1024
TM_ROUTER = 1024
FNET_ROWS = 512
TM_FINAL = 1024
EXPERT_STEP_ROWS = 1024
MAX_EXPERT_ROWS = 512
WEIGHT_SLOTS = 3
ROW_SLABS = 4
SC_CORES = 2
SC_WORKERS = 32
SC_CHUNK = 128
SC_LANES = 16
SC_COMBINE_TOKENS = 8


def _silu(x):
    return x * jax.nn.sigmoid(x)


def _dot(a, b):
    return jnp.dot(a, b, preferred_element_type=F32)


def _rms_mod(x, g, shift, scale):
    ms = jnp.mean(x * x, axis=-1, keepdims=True)
    y = x * lax.rsqrt(ms + EPS) * g
    return y * (1.0 + scale) + shift


def _ada_kernel(cond_ref, w_ref, b_ref, o_ref):
    s = _silu(cond_ref[...]).astype(BF16)
    o_ref[...] = _dot(s, w_ref[...].astype(BF16)) + b_ref[...]


def _ada(cond, w_ada, b_ada):
    rows, n = cond.shape[0], w_ada.shape[1]
    tn = 1536
    return pl.pallas_call(
        _ada_kernel,
        grid=(n // tn,),
        in_specs=[pl.BlockSpec((rows, D_MODEL), lambda j: (0, 0)),
                  pl.BlockSpec((D_MODEL, tn), lambda j: (0, j)),
                  pl.BlockSpec((1, tn), lambda j: (0, j))],
        out_specs=pl.BlockSpec((rows, tn), lambda j: (0, j)),
        out_shape=jax.ShapeDtypeStruct((rows, n), F32),
        compiler_params=pltpu.CompilerParams(vmem_limit_bytes=VMEM_LIMIT),
        name="ada",
    )(cond, w_ada, b_ada)


def _rope_head(x, cos, sin_signed, first_half):
    partner = jnp.where(first_half, pltpu.roll(x, 96, 1), pltpu.roll(x, 32, 1))
    return x * cos + partner * sin_signed


def _inproj_kernel(*refs, use_rope):
    if use_rope:
        x_ref, mod_ref, g_ref, w_ref, cos_ref, sin_ref = refs[:6]
        outs = refs[6:]
    else:
        x_ref, mod_ref, g_ref, w_ref = refs[:4]
        outs = refs[4:]
    uf_o, q_o, k_o, v_o, sg_o, gf_o, gr_o = outs

    h = _rms_mod(x_ref[...], g_ref[...], mod_ref[0, 0:1, :], mod_ref[0, 1:2, :])
    hb = h.astype(BF16)

    def proj(cols):
        return _dot(hb, w_ref[:, cols[0]:cols[1]])

    uf_o[...] = proj(_C_UF).astype(BF16)
    q = proj(_C_Q)
    k = proj(_C_K)
    if use_rope:
        cos = cos_ref[...]
        sin_signed = sin_ref[...]
        lane = lax.broadcasted_iota(jnp.int32, cos.shape, 1)
        first_half = (lane & 32) == 0
        for hd in range(N_RET_HEADS):
            sl = slice(hd * RET_HEAD_DIM, (hd + 1) * RET_HEAD_DIM)
            q_o[:, sl] = (_rope_head(q[:, sl], cos, sin_signed, first_half) * Q_SCALE).astype(BF16)
            k_o[:, sl] = _rope_head(k[:, sl], cos, sin_signed, first_half).astype(BF16)
    else:
        q_o[...] = (q * Q_SCALE).astype(BF16)
        k_o[...] = k.astype(BF16)
    v_o[...] = proj(_C_V).astype(BF16)
    sg_o[...] = _silu(proj(_C_G)).astype(BF16)
    gf_o[...] = jax.nn.sigmoid(proj(_C_GF)).astype(BF16)
    gr_o[...] = jax.nn.sigmoid(proj(_C_GR)).astype(BF16)


def _inproj(x2d, mod3, norm_g, w_in_bf, seq_len, mod_row_of_batch, rope):
    t = x2d.shape[0]
    tm = TM_INPROJ
    tiles_per_seq = max(seq_len // tm, 1)

    def mod_idx(i):
        return (mod_row_of_batch((i * tm) // seq_len), 0, 0)

    in_specs = [pl.BlockSpec((tm, D_MODEL), lambda i: (i, 0)),
                pl.BlockSpec((1, 6, D_MODEL), mod_idx),
                pl.BlockSpec((1, D_MODEL), lambda i: (0, 0)),
                pl.BlockSpec(w_in_bf.shape, lambda i: (0, 0), pipeline_mode=pl.Buffered(1))]
    args = [x2d, mod3, norm_g, w_in_bf]
    if rope is not None:
        in_specs += [pl.BlockSpec((tm, RET_HEAD_DIM), lambda i: (i % tiles_per_seq, 0))] * 2
        args += list(rope)
    widths = [1024, RET_WIDTH, RET_WIDTH, RET_WIDTH, RET_WIDTH, 1024, 1024]
    return pl.pallas_call(
        functools.partial(_inproj_kernel, use_rope=rope is not None),
        grid=(t // tm,),
        in_specs=in_specs,
        out_specs=[pl.BlockSpec((tm, w), lambda i: (i, 0)) for w in widths],
        out_shape=[jax.ShapeDtypeStruct((t, w), BF16) for w in widths],
        compiler_params=pltpu.CompilerParams(dimension_semantics=("parallel",),
                                             vmem_limit_bytes=VMEM_LIMIT),
        name="inproj",
    )(*args)


def _retention_kernel(q_ref, k_ref, v_ref, sg_ref, dec_ref, gn_ref, s0f_ref, s0b_ref,
                      r_ref, sfo_ref, sbo_ref, tab_scr, gc_scr):
    n_chunks = q_ref.shape[0] // CHUNK
    hd = RET_HEAD_DIM

    @pl.when(pl.program_id(0) == 0)
    def _():
        row = lax.broadcasted_iota(jnp.int32, (CHUNK, CHUNK), 0).astype(F32)
        col = lax.broadcasted_iota(jnp.int32, (CHUNK, CHUNK), 1).astype(F32)
        diff = row - col
        for h in range(N_RET_HEADS):
            dec = dec_ref[h]
            lg = jnp.minimum(dec, 0.0) - jnp.log1p(jnp.exp(-jnp.abs(dec)))
            lgf = lg[0:1, :]
            lgb = lg[1:2, :]
            tab_scr[h, 0] = jnp.exp(jnp.where(diff >= 0, lgf * diff, lgb * (-diff)))
            tab_scr[h, 1] = jnp.exp(lgf * (row + 1.0))
            tab_scr[h, 2] = jnp.exp(lgb * (CHUNK - row))
            tab_scr[h, 3] = jnp.exp(lgf * (CHUNK - 1.0 - col))
            tab_scr[h, 4] = jnp.exp(lgb * col)
            gc_scr[h] = jnp.exp(lg * CHUNK)

    def rows(n):
        return slice(n * CHUNK, (n + 1) * CHUNK)

    for h in range(N_RET_HEADS):
        cols = slice(h * hd, (h + 1) * hd)
        decay, qw_f, qw_b, kwt_f, kwt_b = (tab_scr[h, i] for i in range(5))
        gc = gc_scr[h]
        gc_f = gc[0:1, :]
        gc_b = gc[1:2, :]

        kv_f, kv_b = [], []
        for n in range(n_chunks):
            kt = k_ref[rows(n), cols].astype(F32).T
            vn = v_ref[rows(n), cols]
            kv_f.append(_dot((kt * kwt_f).astype(BF16), vn))
            kv_b.append(_dot((kt * kwt_b).astype(BF16), vn))

        s = s0f_ref[h]
        prev_f = []
        for n in range(n_chunks):
            prev_f.append(s.astype(BF16))
            s = gc_f * s + kv_f[n]
        sfo_ref[h] = s
        s = s0b_ref[h]
        prev_b = [None] * n_chunks
        for n in reversed(range(n_chunks)):
            prev_b[n] = s.astype(BF16)
            s = gc_b * s + kv_b[n]
        sbo_ref[h] = s

        gn = gn_ref[:, cols]
        for n in range(n_chunks):
            qn = q_ref[rows(n), cols]
            qf = qn.astype(F32)
            scores = lax.dot_general(qn, k_ref[rows(n), cols], (((1,), (1,)), ((), ())),
                                     preferred_element_type=F32)
            o = _dot((scores * decay).astype(BF16), v_ref[rows(n), cols])
            o = o + _dot((qf * qw_f).astype(BF16), prev_f[n])
            o = o + _dot((qf * qw_b).astype(BF16), prev_b[n])
            mu = jnp.mean(o, axis=-1, keepdims=True)
            d = o - mu
            var = jnp.mean(d * d, axis=-1, keepdims=True)
            on = d * lax.rsqrt(var + EPS) * gn
            r_ref[rows(n), cols] = (on * sg_ref[rows(n), cols].astype(F32)).astype(BF16)


def _retention(q, k, v, sg, dec, gn_g, s0f, s0b, batch, seq_len):
    hd = RET_HEAD_DIM
    tok_spec = pl.BlockSpec((seq_len, RET_WIDTH), lambda b: (b, 0))
    st_spec = pl.BlockSpec((None, N_RET_HEADS, hd, hd), lambda b: (b, 0, 0, 0))
    st_shape = jax.ShapeDtypeStruct((batch, N_RET_HEADS, hd, hd), F32)
    return pl.pallas_call(
        _retention_kernel,
        grid=(batch,),
        in_specs=[tok_spec, tok_spec, tok_spec, tok_spec,
                  pl.BlockSpec(dec.shape, lambda b: (0, 0, 0)),
                  pl.BlockSpec(gn_g.shape, lambda b: (0, 0)),
                  st_spec, st_spec],
        out_specs=[tok_spec, st_spec, st_spec],
        out_shape=[jax.ShapeDtypeStruct((batch * seq_len, RET_WIDTH), BF16), st_shape, st_shape],
        scratch_shapes=[pltpu.VMEM((N_RET_HEADS, 5, CHUNK, CHUNK), F32),
                        pltpu.VMEM((N_RET_HEADS, 2, hd), F32)],
        compiler_params=pltpu.CompilerParams(dimension_semantics=("arbitrary",),
                                             vmem_limit_bytes=VMEM_LIMIT),
        name="retention",
    )(q, k, v, sg, dec, gn_g, s0f, s0b)


def _fnet_merge_kernel(uf_ref, cs_ref, cls_ref, r_ref, gf_ref, gr_ref, x_ref, mod_ref, wf_ref, wr_ref, wo_ref,
                       o_ref, xcs_ref):
    seq_len = uf_ref.shape[0]
    gd = FOURIER_GROUP_DIM

    @pl.when(pl.program_id(1) == 0)
    def _():
        for g in range(N_FOURIER_GROUPS):
            x = _dot(uf_ref[:, g * gd:(g + 1) * gd], cs_ref[...])
            xcs_ref[0:seq_len, g * gd:(g + 1) * gd] = x[:, :gd].astype(BF16)
            xcs_ref[seq_len:2 * seq_len, g * gd:(g + 1) * gd] = x[:, gd:].astype(BF16)

    f_mix = _dot(cls_ref[...], xcs_ref[...]).astype(BF16)
    f_out = _dot(f_mix, wf_ref[...])
    r_out = _dot(r_ref[...], wr_ref[...])
    merged = gf_ref[...].astype(F32) * f_out + gr_ref[...].astype(F32) * r_out
    mix = _dot(merged.astype(BF16), wo_ref[...])
    o_ref[...] = x_ref[...] + mod_ref[0, 2:3, :] * mix


def _fnet_merge(uf, cs, cls, r, gf, gr, x2d, mod3, w_four, w_ret, w_o, batch, seq_len, mod_row_of_batch):
    rb = min(FNET_ROWS, seq_len)
    nr = seq_len // rb

    def tok(w):
        return pl.BlockSpec((rb, w), lambda b, i: (b * nr + i, 0))

    def full(a):
        return pl.BlockSpec(a.shape, lambda b, i: (0, 0))

    return pl.pallas_call(
        _fnet_merge_kernel,
        grid=(batch, nr),
        in_specs=[pl.BlockSpec((seq_len, D_MODEL), lambda b, i: (b, 0)),
                  full(cs),
                  pl.BlockSpec((rb, 2 * seq_len), lambda b, i: (i, 0)),
                  tok(RET_WIDTH), tok(D_MODEL), tok(D_MODEL), tok(D_MODEL),
                  pl.BlockSpec((1, 6, D_MODEL), lambda b, i: (mod_row_of_batch(b), 0, 0)),
                  full(w_four), full(w_ret), full(w_o)],
        out_specs=tok(D_MODEL),
        out_shape=jax.ShapeDtypeStruct((batch * seq_len, D_MODEL), F32),
        scratch_shapes=[pltpu.VMEM((2 * seq_len, D_MODEL), BF16)],
        compiler_params=pltpu.CompilerParams(dimension_semantics=("parallel", "arbitrary"),
                                             vmem_limit_bytes=VMEM_LIMIT),
        name="fnet_merge",
    )(uf, cs, cls, r, gf, gr, x2d, mod3, w_four, w_ret, w_o)


def _pack_pair(lo_f32, hi_f32):
    lo = lax.bitcast_convert_type(lo_f32.astype(BF16).astype(F32), jnp.uint32)
    hi = lax.bitcast_convert_type(hi_f32.astype(BF16).astype(F32), jnp.uint32)
    return lax.bitcast_convert_type((lo >> 16) | hi, jnp.int32)


def _unpack_pair(words_i32):
    w = lax.bitcast_convert_type(words_i32, jnp.uint32)
    lo = lax.bitcast_convert_type(w << 16, F32)
    hi = lax.bitcast_convert_type(w & jnp.uint32(0xFFFF0000), F32)
    return lo, hi


def _load_token_words(ref, lead, n_tok):
    parts = []
    for s in range(ROW_SLABS):
        idx = (pl.ds(s, n_tok, stride=ROW_SLABS), slice(None))
        parts.append(ref[lead + idx] if lead else ref[idx])
    return jnp.concatenate(parts, axis=1)


def _store_token_words(ref, words, n_tok):
    for s in range(ROW_SLABS):
        ref[pl.ds(s, n_tok, stride=ROW_SLABS), :] = words[:, s * 128:(s + 1) * 128]


def _route(scores, biased):
    tokens = scores.shape[1]
    neg = -jnp.inf
    epg = EXPERTS_PER_GROUP
    iota_g = lax.broadcasted_iota(jnp.int32, (epg, tokens), 0).astype(F32)

    def pick_first_max(cur, iota, size):
        m = jnp.max(cur, axis=0, keepdims=True)
        idx = jnp.min(jnp.where(cur == m, iota, float(size)), axis=0, keepdims=True)
        return m, idx, iota == idx

    group_scores = []
    for g in range(N_EXPERT_GROUPS):
        vals = biased[g * epg:(g + 1) * epg, :]
        m1, _, hit = pick_first_max(vals, iota_g, epg)
        m2 = jnp.max(jnp.where(hit, neg, vals), axis=0, keepdims=True)
        group_scores.append(m1 + m2)
    cur = jnp.concatenate(group_scores, axis=0)
    group_sel = jnp.zeros_like(cur)
    for _ in range(TOPK_GROUPS):
        _, _, hit = pick_first_max(cur, iota_g, N_EXPERT_GROUPS)
        group_sel = jnp.where(hit, 1.0, group_sel)
        cur = jnp.where(hit, neg, cur)
    masked = jnp.concatenate(
        [jnp.where(group_sel[g:g + 1, :] > 0.0, biased[g * epg:(g + 1) * epg, :], neg)
         for g in range(N_EXPERT_GROUPS)], axis=0)
    iota_e = lax.broadcasted_iota(jnp.int32, masked.shape, 0).astype(F32)
    sel = jnp.zeros_like(masked)
    cur = masked
    picks = []
    for _ in range(TOP_K):
        _, idx, hit = pick_first_max(cur, iota_e, N_EXPERTS)
        picks.append(idx)
        sel = jnp.where(hit, 1.0, sel)
        cur = jnp.where(hit, neg, cur)
    w = scores * sel
    return w / jnp.sum(w, axis=0, keepdims=True) * ROUTED_SCALE, sel, picks


def _router_kernel(x_ref, mod_ref, g2_ref, wrt_ref, rb_ref, hp_ref, ek_ref, rk_ref, wt_ref, cnt_ref,
                   run_scr, earlier_scr):
    tm = x_ref.shape[0]

    @pl.when(pl.program_id(0) == 0)
    def _():
        run_scr[...] = jnp.zeros_like(run_scr)
        earlier = (lax.broadcasted_iota(jnp.int32, (tm, tm), 0) < lax.broadcasted_iota(jnp.int32, (tm, tm), 1))
        earlier_scr[...] = jnp.where(earlier, 1.0, 0.0).astype(BF16)

    h = _rms_mod(x_ref[...], g2_ref[...], mod_ref[0, 3:4, :], mod_ref[0, 4:5, :])
    half = D_MODEL // 2
    _store_token_words(hp_ref, _pack_pair(h[:, :half], h[:, half:]), tm)

    def split(a):
        hi = a.astype(BF16)
        return hi, (a - hi.astype(F32)).astype(BF16)

    def dot_nt(a, b):
        return lax.dot_general(a, b, (((1,), (1,)), ((), ())), preferred_element_type=F32)

    h_hi, h_lo = split(h)
    w_hi, w_lo = split(wrt_ref[...])
    logits_t = dot_nt(w_hi, h_hi) + (dot_nt(w_hi, h_lo) + dot_nt(w_lo, h_hi))
    scores = jax.nn.sigmoid(logits_t)
    comb_t, sel, picks = _route(scores, scores + rb_ref[...])

    rank_t = _dot(sel.astype(BF16), earlier_scr[...]) + run_scr[...]
    run_scr[...] += jnp.sum(sel, axis=1, keepdims=True)
    cnt_ref[...] = jnp.broadcast_to(run_scr[...], cnt_ref.shape)

    iota_e = lax.broadcasted_iota(jnp.int32, sel.shape, 0).astype(F32)
    ranks, weights = [], []
    for idx in picks:
        hit = iota_e == idx
        ranks.append(jnp.sum(jnp.where(hit, rank_t, 0.0), axis=0, keepdims=True))
        weights.append(jnp.sum(jnp.where(hit, comb_t, 0.0), axis=0, keepdims=True))
    ek_ref[...] = jnp.concatenate(picks, axis=0).astype(jnp.int32)
    rk_ref[...] = jnp.concatenate(ranks, axis=0).astype(jnp.int32)
    w_rep = jnp.concatenate([jnp.broadcast_to(w, (SC_LANES, tm)) for w in weights], axis=0)
    wt_ref[...] = w_rep.T


def _router(x1, mod3, norm2_g, w_router_t, router_bias, seq_len, mod_row_of_batch):
    t = x1.shape[0]
    tm = TM_ROUTER

    def mod_idx(i):
        return (mod_row_of_batch((i * tm) // seq_len), 0, 0)

    def full(a):
        return pl.BlockSpec(a.shape, lambda i: (0,) * a.ndim)

    return pl.pallas_call(
        _router_kernel,
        grid=(t // tm,),
        in_specs=[pl.BlockSpec((tm, D_MODEL), lambda i: (i, 0)),
                  pl.BlockSpec((1, 6, D_MODEL), mod_idx),
                  full(norm2_g), full(w_router_t), full(router_bias)],
        out_specs=[pl.BlockSpec((tm * ROW_SLABS, 128), lambda i: (i, 0)),
                   pl.BlockSpec((TOP_K, tm), lambda i: (0, i)),
                   pl.BlockSpec((TOP_K, tm), lambda i: (0, i)),
                   pl.BlockSpec((tm, 128), lambda i: (i, 0)),
                   pl.BlockSpec((N_EXPERTS, 128), lambda i: (0, 0))],
        out_shape=[jax.ShapeDtypeStruct((t * ROW_SLABS, 128), jnp.int32),
                   jax.ShapeDtypeStruct((TOP_K, t), jnp.int32),
                   jax.ShapeDtypeStruct((TOP_K, t), jnp.int32),
                   jax.ShapeDtypeStruct((t, 128), F32),
                   jax.ShapeDtypeStruct((N_EXPERTS, 128), F32)],
        scratch_shapes=[pltpu.VMEM((N_EXPERTS, 1), F32), pltpu.VMEM((tm, tm), BF16)],
        compiler_params=pltpu.CompilerParams(dimension_semantics=("arbitrary",),
                                             vmem_limit_bytes=VMEM_LIMIT),
        name="router",
    )(x1, mod3, norm2_g, w_router_t, router_bias)


def _plan_kernel(ek_ref, rk_ref, cnt_ref, pos_ref, texp_ref, nused_ref, tend_ref, *, expert_rows):
    rows = float(expert_rows)
    cnt = cnt_ref[:, 0:1]
    tiles = jnp.floor((cnt + (rows - 1.0)) / rows)
    before = (lax.broadcasted_iota(jnp.int32, (N_EXPERTS, N_EXPERTS), 1)
              < lax.broadcasted_iota(jnp.int32, (N_EXPERTS, N_EXPERTS), 0))
    tile_start = jnp.dot(jnp.where(before, 1.0, 0.0), jnp.broadcast_to(tiles, (N_EXPERTS, 128)),
                         precision=lax.Precision.HIGHEST, preferred_element_type=F32)[:, 0:1]
    tile_end = tile_start + tiles
    row_start = tile_start * rows

    ek = ek_ref[...]
    pos = rk_ref[...].astype(F32)
    tile_id = lax.broadcasted_iota(jnp.int32, texp_ref.shape, 1).astype(F32)
    texp = jnp.zeros(texp_ref.shape, F32)
    for e in range(N_EXPERTS):
        pos = pos + jnp.where(ek == e, row_start[e:e + 1, :], 0.0)
        texp = texp + jnp.where(tile_id >= tile_end[e:e + 1, :], 1.0, 0.0)
    pos_ref[...] = pos.astype(jnp.int32)
    texp_ref[...] = jnp.minimum(texp, N_EXPERTS - 1.0).astype(jnp.int32)
    nused_ref[...] = jnp.broadcast_to(tile_end[N_EXPERTS - 1:N_EXPERTS, :], nused_ref.shape).astype(jnp.int32)
    tend_ref[...] = jnp.broadcast_to(tile_end, tend_ref.shape).astype(jnp.int32)


def _plan(ek, rk, cnt, n_tiles_pad, expert_rows):
    t = ek.shape[1]

    def full(shape):
        return pl.BlockSpec(shape, lambda: (0,) * len(shape))

    return pl.pallas_call(
        functools.partial(_plan_kernel, expert_rows=expert_rows),
        in_specs=[full(ek.shape), full(rk.shape), full(cnt.shape)],
        out_specs=[full((TOP_K, t)), full((1, n_tiles_pad)), full((1, 128)), full((N_EXPERTS, 128))],
        out_shape=[jax.ShapeDtypeStruct((TOP_K, t), jnp.int32),
                   jax.ShapeDtypeStruct((1, n_tiles_pad), jnp.int32),
                   jax.ShapeDtypeStruct((1, 128), jnp.int32),
                   jax.ShapeDtypeStruct((N_EXPERTS, 128), jnp.int32)],
        compiler_params=pltpu.CompilerParams(vmem_limit_bytes=VMEM_LIMIT),
        name="plan",
    )(ek, rk, cnt)


def _sc_mesh():
    return plsc.VectorSubcoreMesh(core_axis_name="c", subcore_axis_name="s")


def _sc_dispatch(rows, pos3, n_out):
    t = rows.shape[0]
    ch = SC_CHUNK
    per_w = (t // ch) // SC_WORKERS

    @functools.partial(
        pl.kernel, out_type=jax.ShapeDtypeStruct((n_out,) + rows.shape[1:], jnp.int32), mesh=_sc_mesh(),
        scratch_types=[pltpu.VMEM((TOP_K, ch), jnp.int32), pltpu.VMEM((ch,) + rows.shape[1:], jnp.int32),
                       pltpu.SemaphoreType.DMA])
    def k(rows_hbm, pos_hbm, out_hbm, idx_v, rows_v, sem):
        wid = lax.axis_index("s") * SC_CORES + lax.axis_index("c")

        @pl.loop(0, per_w)
        def _(j):
            c = wid * per_w + j
            pltpu.sync_copy(pos_hbm.at[c], idx_v)
            pltpu.sync_copy(rows_hbm.at[pl.ds(c * ch, ch)], rows_v)
            copies = [pltpu.async_copy(rows_v, out_hbm.at[idx_v.at[kk]], sem) for kk in range(TOP_K)]
            for cp in copies:
                cp.wait()

    return k(rows, pos3)


def _sc_combine(table, pos3, wtok, t):
    ch = SC_CHUNK
    sub = SC_COMBINE_TOKENS
    lanes = SC_LANES
    slabs = ROW_SLABS
    per_w = (t // ch) // SC_WORKERS
    subs_per_chunk = ch // sub
    n_steps = per_w * subs_per_chunk

    @functools.partial(
        pl.kernel, out_type=jax.ShapeDtypeStruct((t, 2 * slabs, 128), F32), mesh=_sc_mesh(),
        scratch_types=[pltpu.VMEM((per_w, TOP_K, ch), jnp.int32),
                       pltpu.VMEM((2, TOP_K, sub, slabs, 128), jnp.int32),
                       pltpu.VMEM((2, sub, 128), F32),
                       pltpu.VMEM((sub, 2 * slabs, 128), F32),
                       pltpu.SemaphoreType.DMA((2,))],
        compiler_params=pltpu.CompilerParams(needs_layout_passes=False))
    def k(tab_hbm, pos_hbm, w_hbm, out_hbm, idx_v, rows_v, w_v, out_v, sem):
        wid = lax.axis_index("s") * SC_CORES + lax.axis_index("c")
        for j in range(per_w):
            pltpu.sync_copy(pos_hbm.at[wid * per_w + j], idx_v.at[j])

        def first_token(step):
            return (wid * per_w + step // subs_per_chunk) * ch + (step % subs_per_chunk) * sub

        def copies(step, slot):
            j = step // subs_per_chunk
            s = step % subs_per_chunk
            idx = [idx_v.at[j, kk, pl.ds(s * sub, sub)] for kk in range(TOP_K)]
            return ([pltpu.make_async_copy(tab_hbm.at[idx[kk]], rows_v.at[slot, kk], sem.at[slot])
                     for kk in range(TOP_K)]
                    + [pltpu.make_async_copy(w_hbm.at[pl.ds(first_token(step), sub)], w_v.at[slot], sem.at[slot])])

        for cp in copies(0, 0):
            cp.start()

        @pl.loop(0, n_steps)
        def _(step):
            slot = step % 2

            @pl.when(step + 1 < n_steps)
            def _():
                for cp in copies(step + 1, 1 - slot):
                    cp.start()

            for cp in copies(step, slot):
                cp.wait()

            @pl.loop(0, sub)
            def _(tt):
                wk = [w_v[slot, tt, pl.ds(kk * lanes, lanes)] for kk in range(TOP_K)]
                for sl in range(slabs):
                    @plsc.parallel_loop(0, 128, step=lanes, unroll=4)
                    def _(off):
                        acc_lo = jnp.zeros((lanes,), F32)
                        acc_hi = jnp.zeros((lanes,), F32)
                        for kk in range(TOP_K):
                            word = rows_v[slot, kk, tt, sl, pl.ds(off, lanes)]
                            lo = plsc.bitcast(word << 16, F32)
                            hi = plsc.bitcast(word & jnp.int32(-65536), F32)
                            acc_lo = acc_lo + wk[kk] * lo
                            acc_hi = acc_hi + wk[kk] * hi
                        out_v[tt, sl, pl.ds(off, lanes)] = acc_lo
                        out_v[tt, slabs + sl, pl.ds(off, lanes)] = acc_hi

            pltpu.sync_copy(out_v, out_hbm.at[pl.ds(first_token(step), sub)])

    return k(table, pos3, wtok)


def _experts_kernel(texp_ref, nused_ref, tend_ref, xs_ref, weg_hbm, weu_hbm, wed_hbm, ys_ref,
                    wg_scr, wu_scr, wd_scr, wg_buf, wu_buf, wd_buf, sem, group_scr, *, expert_rows):
    step = pl.program_id(0)
    rows = expert_rows
    tiles_per_step = EXPERT_STEP_ROWS // expert_rows
    half = D_MODEL // 2
    n_used = nused_ref[0]

    def weight_copies(e, slot):
        return [pltpu.make_async_copy(weg_hbm.at[e], wg_buf.at[slot], sem.at[slot, 0]),
                pltpu.make_async_copy(weu_hbm.at[e], wu_buf.at[slot], sem.at[slot, 1]),
                pltpu.make_async_copy(wed_hbm.at[e], wd_buf.at[slot], sem.at[slot, 2])]

    def next_group(e):
        tile = tend_ref[e]
        return texp_ref[jnp.minimum(tile, n_used - 1)], tile < n_used

    def start_weights(e, slot, exists):
        @pl.when(exists)
        def _():
            for cp in weight_copies(e, slot):
                cp.start()

    @pl.when(step == 0)
    def _():
        group_scr[0] = 0
        e, exists = texp_ref[0], True
        for slot in range(WEIGHT_SLOTS - 1):
            start_weights(e, slot, exists)
            nxt, has_next = next_group(e)
            e, exists = nxt, exists & has_next

    def row_tile(tile, x_view, y_view):
        expert = texp_ref[tile]
        used = tile < n_used
        new_expert = (tile == 0) | (expert != texp_ref[jnp.maximum(tile - 1, 0)])

        @pl.when(used & new_expert)
        def _():
            group = group_scr[0]
            slot = group % WEIGHT_SLOTS
            ahead, exists = expert, True
            for _ in range(WEIGHT_SLOTS - 1):
                nxt, has_next = next_group(ahead)
                ahead, exists = nxt, exists & has_next
            start_weights(ahead, (group + WEIGHT_SLOTS - 1) % WEIGHT_SLOTS, exists)

            for cp in weight_copies(expert, slot):
                cp.wait()
            wg_scr[...] = wg_buf[slot].astype(BF16)
            wu_scr[...] = wu_buf[slot].astype(BF16)
            wd_scr[...] = wd_buf[slot].astype(BF16)
            group_scr[0] = group + 1

        @pl.when(used)
        def _():
            lo, hi = _unpack_pair(_load_token_words(x_view, (), rows))
            lo = lo.astype(BF16)
            hi = hi.astype(BF16)
            g = _dot(lo, wg_scr[0:half, :]) + _dot(hi, wg_scr[half:D_MODEL, :])
            u = _dot(lo, wu_scr[0:half, :]) + _dot(hi, wu_scr[half:D_MODEL, :])
            y = _dot((_silu(g) * u).astype(BF16), wd_scr[...])
            _store_token_words(y_view, _pack_pair(y[:, :half], y[:, half:]), rows)

        @pl.when(jnp.logical_not(used) & (step == (n_used - 1) // tiles_per_step))
        def _():
            y_view[...] = jnp.zeros_like(y_view)

    for s in range(tiles_per_step):
        view = pl.ds(s * rows * ROW_SLABS, rows * ROW_SLABS)
        row_tile(step * tiles_per_step + s, xs_ref.at[view], ys_ref.at[view])


def _experts(texp, nused, tend, xs2d, weg, weu, wed, n_tiles, expert_rows):
    tiles_per_step = EXPERT_STEP_ROWS // expert_rows
    block = (EXPERT_STEP_ROWS * ROW_SLABS, 128)
    hbm = pl.BlockSpec(memory_space=pl.ANY)

    def block_idx(j, te, nu, tn):
        return (jnp.minimum(j, (nu[0] - 1) // tiles_per_step), 0)

    grid_spec = pltpu.PrefetchScalarGridSpec(
        num_scalar_prefetch=3,
        grid=(n_tiles // tiles_per_step,),
        in_specs=[pl.BlockSpec(block, block_idx), hbm, hbm, hbm],
        out_specs=pl.BlockSpec(block, block_idx),
        scratch_shapes=[pltpu.VMEM((D_MODEL, EXPERT_DIM), BF16),
                        pltpu.VMEM((D_MODEL, EXPERT_DIM), BF16),
                        pltpu.VMEM((EXPERT_DIM, D_MODEL), BF16),
                        pltpu.VMEM((WEIGHT_SLOTS, D_MODEL, EXPERT_DIM), F32),
                        pltpu.VMEM((WEIGHT_SLOTS, D_MODEL, EXPERT_DIM), F32),
                        pltpu.VMEM((WEIGHT_SLOTS, EXPERT_DIM, D_MODEL), F32),
                        pltpu.SemaphoreType.DMA((WEIGHT_SLOTS, 3)),
                        pltpu.SMEM((1,), jnp.int32)],
    )
    return pl.pallas_call(
        functools.partial(_experts_kernel, expert_rows=expert_rows),
        grid_spec=grid_spec,
        out_shape=jax.ShapeDtypeStruct(xs2d.shape, jnp.int32),
        compiler_params=pltpu.CompilerParams(dimension_semantics=("arbitrary",),
                                             vmem_limit_bytes=VMEM_LIMIT),
        name="experts",
    )(texp, nused, tend, xs2d, weg, weu, wed)


def _final_kernel(x_ref, routed_ref, mod_ref, g2_ref, wsg_ref, wsu_ref, wsd_ref, fng_ref, o_ref):
    tm = x_ref.shape[0]
    x = x_ref[...]
    hb = _rms_mod(x, g2_ref[...], mod_ref[0, 3:4, :], mod_ref[0, 4:5, :]).astype(BF16)
    shared = _dot((_silu(_dot(hb, wsg_ref[...])) * _dot(hb, wsu_ref[...])).astype(BF16), wsd_ref[...])
    n_slabs = 2 * ROW_SLABS
    routed = jnp.concatenate([routed_ref[pl.ds(s, tm, stride=n_slabs), :] for s in range(n_slabs)], axis=1)
    y = x + mod_ref[0, 5:6, :] * (routed + shared)
    ms = jnp.mean(y * y, axis=-1, keepdims=True)
    o_ref[...] = y * lax.rsqrt(ms + EPS) * fng_ref[...]


def _final(x1, routed2d, mod3, norm2_g, wsg, wsu, wsd, final_g, seq_len, mod_row_of_batch):
    t = x1.shape[0]
    tm = TM_FINAL

    def mod_idx(i):
        return (mod_row_of_batch((i * tm) // seq_len), 0, 0)

    def full(a):
        return pl.BlockSpec(a.shape, lambda i: (0,) * a.ndim)

    return pl.pallas_call(
        _final_kernel,
        grid=(t // tm,),
        in_specs=[pl.BlockSpec((tm, D_MODEL), lambda i: (i, 0)),
                  pl.BlockSpec((tm * 2 * ROW_SLABS, 128), lambda i: (i, 0)),
                  pl.BlockSpec((1, 6, D_MODEL), mod_idx),
                  full(norm2_g), full(wsg), full(wsu), full(wsd), full(final_g)],
        out_specs=pl.BlockSpec((tm, D_MODEL), lambda i: (i, 0)),
        out_shape=jax.ShapeDtypeStruct((t, D_MODEL), F32),
        compiler_params=pltpu.CompilerParams(dimension_semantics=("parallel",),
                                             vmem_limit_bytes=VMEM_LIMIT),
        name="final",
    )(x1, routed2d, mod3, norm2_g, wsg, wsu, wsd, final_g)


def _moe(x1, mod3, lw, seq_len, mod_row_of_batch):
    t = x1.shape[0]
    expert_rows = min(MAX_EXPERT_ROWS, TOP_K * t // N_EXPERTS // 2)
    n_tiles = TOP_K * t // expert_rows + N_EXPERTS
    n_tiles_pad = -(-n_tiles // 128) * 128
    hp2d, ek, rk, wtok, cnt = _router(x1, mod3, lw["norm2_g"], lw["w_router_t"], lw["router_bias"],
                                      seq_len, mod_row_of_batch)
    pos, texp, nused, tend = _plan(ek, rk, cnt, n_tiles_pad, expert_rows)
    pos3 = pos.reshape(TOP_K, t // SC_CHUNK, SC_CHUNK).transpose(1, 0, 2)
    xs = _sc_dispatch(hp2d.reshape(t, ROW_SLABS, 128), pos3, n_tiles * expert_rows)
    ys2d = _experts(texp.reshape(-1), nused.reshape(-1), tend[:, 0], xs.reshape(-1, 128),
                    lw["weg"], lw["weu"], lw["wed"], n_tiles, expert_rows)
    routed = _sc_combine(ys2d.reshape(-1, ROW_SLABS, 128), pos3, wtok, t)
    return _final(x1, routed.reshape(t * 2 * ROW_SLABS, 128), mod3, lw["norm2_g"],
                  lw["wsg"], lw["wsu"], lw["wsd"], lw["final_g"], seq_len, mod_row_of_batch)


def _dft_tables(seq_len):
    gd = FOURIER_GROUP_DIM
    kc = np.arange(gd)
    ang_c = ((kc[:, None] * kc[None, :]) % gd) * (2.0 * math.pi / gd)
    cs = np.concatenate([np.cos(ang_c), np.sin(ang_c)], axis=1) * (gd ** -0.5)
    kl = np.arange(seq_len)
    ang_l = ((kl[:, None] * kl[None, :]) % seq_len) * (2.0 * math.pi / seq_len)
    cls = np.concatenate([np.cos(ang_l), -np.sin(ang_l)], axis=1) * (seq_len ** -0.5)
    return jnp.asarray(cs.astype(np.float32), dtype=BF16), jnp.asarray(cls.astype(np.float32), dtype=BF16)


def _rope_tables(length):
    rows = length // GRID_W
    r = np.repeat(np.arange(rows, dtype=np.float32), GRID_W)
    col = np.tile(np.arange(GRID_W, dtype=np.float32), rows)
    nf = RET_HEAD_DIM // 4
    inv = (np.float32(ROPE_BASE) ** (-np.arange(nf, dtype=np.float32) / np.float32(nf))).astype(np.float32)
    ar = r[:, None] * inv[None]
    ac = col[:, None] * inv[None]
    ang = np.concatenate([ar, ar, ac, ac], axis=-1).astype(np.float64)
    sign = np.where((np.arange(RET_HEAD_DIM) & nf) == 0, -1.0, 1.0)
    return (jnp.asarray(np.cos(ang).astype(np.float32)),
            jnp.asarray((np.sin(ang) * sign[None, :]).astype(np.float32)))


def _trunk_path(x, mod3, mod_row_of_batch, s0f, s0b, rope, lw):
    batch, seq_len, _ = x.shape
    x2d = x.reshape(batch * seq_len, D_MODEL)
    uf, q, k, v, sg, gf, gr = _inproj(x2d, mod3, lw["norm1_g"], lw["w_in"], seq_len, mod_row_of_batch, rope)
    r, s_f, s_b = _retention(q, k, v, sg, lw["dec"], lw["gn_g"], s0f, s0b, batch, seq_len)
    cs, cls = _dft_tables(seq_len)
    x1 = _fnet_merge(uf, cs, cls, r, gf, gr, x2d, mod3, lw["w_four"], lw["w_ret"], lw["w_o"],
                     batch, seq_len, mod_row_of_batch)
    y = _moe(x1, mod3, lw, seq_len, mod_row_of_batch)
    return y.reshape(batch, seq_len, D_MODEL), s_f, s_b


def kernel(x_prompt, x_sample, state_ret_fwd, state_ret_bwd, c, c_ctx, w_ada, b_ada, norm1_g, norm2_g, w_in,
           ret_decay_fwd, ret_decay_bwd, ret_gn_g, w_four_out, w_ret_out, w_out, w_router, router_bias,
           w_exp_gate, w_exp_up, w_exp_down, w_shared_gate, w_shared_up, w_shared_down, final_norm_g):
    depth = w_ada.shape[0]
    assert depth == 1, "final norm is fused into the last layer's MoE kernel"
    n_ctx, n_lat = x_prompt.shape[0], x_sample.shape[0]
    cond = jnp.concatenate([c_ctx[None, :], c], axis=0)
    cond = jnp.pad(cond, ((0, (-cond.shape[0]) % 8), (0, 0)))
    rope = _rope_tables(x_sample.shape[1])
    zeros = jnp.zeros((n_ctx, N_RET_HEADS, RET_HEAD_DIM, RET_HEAD_DIM), F32)

    layer = 0
    mod = _ada(cond, w_ada[layer], b_ada[layer][None, :])
    mod3 = mod.reshape(mod.shape[0], 6, D_MODEL)
    dec = jnp.stack([ret_decay_fwd[layer], ret_decay_bwd[layer]], axis=1)
    lw = {
        "norm1_g": norm1_g[layer][None, :],
        "norm2_g": norm2_g[layer][None, :],
        "w_in": w_in[layer].astype(BF16),
        "dec": jnp.broadcast_to(dec[:, :, None], (N_RET_HEADS, 2, RET_HEAD_DIM)).astype(F32),
        "gn_g": ret_gn_g[layer][None, :],
        "w_four": w_four_out[layer].astype(BF16),
        "w_ret": w_ret_out[layer].astype(BF16),
        "w_o": w_out[layer].astype(BF16),
        "w_router_t": w_router[layer].T,
        "router_bias": router_bias[layer][:, None],
        "weg": w_exp_gate[layer],
        "weu": w_exp_up[layer],
        "wed": w_exp_down[layer],
        "wsg": w_shared_gate[layer].astype(BF16),
        "wsu": w_shared_up[layer].astype(BF16),
        "wsd": w_shared_down[layer].astype(BF16),
        "final_g": final_norm_g[None, :],
    }
    y_prompt, s_f, s_b = _trunk_path(x_prompt, mod3, lambda b: 0, zeros, zeros, None, lw)
    y_sample, _, _ = _trunk_path(x_sample, mod3, lambda b: 1 + b, state_ret_fwd[:, layer],
                                 state_ret_bwd[:, layer], rope, lw)
    return (y_prompt, y_sample, s_f[:, None], s_b[:, None])
```

```python
import functools
import math

import jax
import jax.numpy as jnp
import numpy as np
from jax import lax
from jax.experimental import pallas as pl
from jax.experimental.pallas import tpu as pltpu
from jax.experimental.pallas import tpu_sc as plsc

F32 = jnp.float32
BF16 = jnp.bfloat16

D_MODEL = 1024
GRID_W = 64
N_FOURIER_GROUPS = 8
FOURIER_GROUP_DIM = 128
N_RET_HEADS = 4
RET_HEAD_DIM = 128
RET_WIDTH = N_RET_HEADS * RET_HEAD_DIM
CHUNK = 128
N_EXPERTS = 64
N_EXPERT_GROUPS = 8
EXPERTS_PER_GROUP = N_EXPERTS // N_EXPERT_GROUPS
TOPK_GROUPS = 4
TOP_K = 8
EXPERT_DIM = 256
ROUTED_SCALE = 2.5
ROPE_BASE = 10000.0
EPS = 1e-6
Q_SCALE = RET_HEAD_DIM ** -0.5

_C_UF = (0, 1024)
_C_Q = (1024, 1536)
_C_K = (1536, 2048)
_C_V = (2048, 2560)
_C_G = (2560, 3072)
_C_GF = (3072, 4096)
_C_GR = (4096, 5120)

VMEM_LIMIT = 56 * 1024 * 1024

TM_INPROJ = 1024
TM_ROUTER = 1024
FNET_ROWS = 512
TM_FINAL = 1024
EXPERT_STEP_ROWS = 1024
MAX_EXPERT_ROWS = 512
WEIGHT_SLOTS = 3
ROW_SLABS = 4
SC_CORES = 2
SC_WORKERS = 32
SC_CHUNK = 128
SC_LANES = 16
SC_COMBINE_TOKENS = 8


def _silu(x):
    return x * jax.nn.sigmoid(x)


def _dot(a, b):
    return jnp.dot(a, b, preferred_element_type=F32)


def _rms_mod(x, g, shift, scale):
    ms = jnp.mean(x * x, axis=-1, keepdims=True)
    y = x * lax.rsqrt(ms + EPS) * g
    return y * (1.0 + scale) + shift


def _ada_kernel(cond_ref, w_ref, b_ref, o_ref):
    s = _silu(cond_ref[...]).astype(BF16)
    o_ref[...] = _dot(s, w_ref[...].astype(BF16)) + b_ref[...]


def _ada(cond, w_ada, b_ada):
    rows, n = cond.shape[0], w_ada.shape[1]
    tn = 1536
    return pl.pallas_call(
        _ada_kernel,
        grid=(n // tn,),
        in_specs=[pl.BlockSpec((rows, D_MODEL), lambda j: (0, 0)),
                  pl.BlockSpec((D_MODEL, tn), lambda j: (0, j)),
                  pl.BlockSpec((1, tn), lambda j: (0, j))],
        out_specs=pl.BlockSpec((rows, tn), lambda j: (0, j)),
        out_shape=jax.ShapeDtypeStruct((rows, n), F32),
        compiler_params=pltpu.CompilerParams(vmem_limit_bytes=VMEM_LIMIT),
        name="ada",
    )(cond, w_ada, b_ada)


def _rope_head(x, cos, sin_signed, first_half):
    partner = jnp.where(first_half, pltpu.roll(x, 96, 1), pltpu.roll(x, 32, 1))
    return x * cos + partner * sin_signed


def _inproj_kernel(*refs, use_rope):
    if use_rope:
        x_ref, mod_ref, g_ref, w_ref, cos_ref, sin_ref = refs[:6]
        outs = refs[6:]
    else:
        x_ref, mod_ref, g_ref, w_ref = refs[:4]
        outs = refs[4:]
    uf_o, q_o, k_o, v_o, sg_o, gf_o, gr_o = outs

    h = _rms_mod(x_ref[...], g_ref[...], mod_ref[0, 0:1, :], mod_ref[0, 1:2, :])
    hb = h.astype(BF16)

    def proj(cols):
        return _dot(hb, w_ref[:, cols[0]:cols[1]].astype(BF16))

    uf_o[...] = proj(_C_UF).astype(BF16)
    q = proj(_C_Q)
    k = proj(_C_K)
    if use_rope:
        cos = cos_ref[...]
        sin_signed = sin_ref[...]
        lane = lax.broadcasted_iota(jnp.int32, cos.shape, 1)
        first_half = (lane & 32) == 0
        for hd in range(N_RET_HEADS):
            sl = slice(hd * RET_HEAD_DIM, (hd + 1) * RET_HEAD_DIM)
            q_o[:, sl] = (_rope_head(q[:, sl], cos, sin_signed, first_half) * Q_SCALE).astype(BF16)
            k_o[:, sl] = _rope_head(k[:, sl], cos, sin_signed, first_half).astype(BF16)
    else:
        q_o[...] = (q * Q_SCALE).astype(BF16)
        k_o[...] = k.astype(BF16)
    v_o[...] = proj(_C_V).astype(BF16)
    sg_o[...] = _silu(proj(_C_G)).astype(BF16)
    gf_o[...] = jax.nn.sigmoid(proj(_C_GF)).astype(BF16)
    gr_o[...] = jax.nn.sigmoid(proj(_C_GR)).astype(BF16)


def _inproj(x2d, mod3, norm_g, w_in_f32, seq_len, mod_row_of_batch, rope):
    t = x2d.shape[0]
    tm = TM_INPROJ
    tiles_per_seq = max(seq_len // tm, 1)

    def mod_idx(i):
        return (mod_row_of_batch((i * tm) // seq_len), 0, 0)

    in_specs = [pl.BlockSpec((tm, D_MODEL), lambda i: (i, 0)),
                pl.BlockSpec((1, 6, D_MODEL), mod_idx),
                pl.BlockSpec((1, D_MODEL), lambda i: (0, 0)),
                pl.BlockSpec(w_in_f32.shape, lambda i: (0, 0), pipeline_mode=pl.Buffered(1))]
    args = [x2d, mod3, norm_g, w_in_f32]
    if rope is not None:
        in_specs += [pl.BlockSpec((tm, RET_HEAD_DIM), lambda i: (i % tiles_per_seq, 0))] * 2
        args += list(rope)
    widths = [1024, RET_WIDTH, RET_WIDTH, RET_WIDTH, RET_WIDTH, 1024, 1024]
    return pl.pallas_call(
        functools.partial(_inproj_kernel, use_rope=rope is not None),
        grid=(t // tm,),
        in_specs=in_specs,
        out_specs=[pl.BlockSpec((tm, w), lambda i: (i, 0)) for w in widths],
        out_shape=[jax.ShapeDtypeStruct((t, w), BF16) for w in widths],
        compiler_params=pltpu.CompilerParams(dimension_semantics=("parallel",),
                                             vmem_limit_bytes=VMEM_LIMIT),
        name="inproj",
    )(*args)


def _retention_kernel(q_ref, k_ref, v_ref, sg_ref, dec_ref, gn_ref, s0f_ref, s0b_ref,
                      r_ref, sfo_ref, sbo_ref, tab_scr, gc_scr):
    n_chunks = q_ref.shape[0] // CHUNK
    hd = RET_HEAD_DIM

    @pl.when(pl.program_id(0) == 0)
    def _():
        row = lax.broadcasted_iota(jnp.int32, (CHUNK, CHUNK), 0).astype(F32)
        col = lax.broadcasted_iota(jnp.int32, (CHUNK, CHUNK), 1).astype(F32)
        diff = row - col
        for h in range(N_RET_HEADS):
            dec = dec_ref[h]
            lg = jnp.minimum(dec, 0.0) - jnp.log1p(jnp.exp(-jnp.abs(dec)))
            lgf = lg[0:1, :]
            lgb = lg[1:2, :]
            tab_scr[h, 0] = jnp.exp(jnp.where(diff >= 0, lgf * diff, lgb * (-diff)))
            tab_scr[h, 1] = jnp.exp(lgf * (row + 1.0))
            tab_scr[h, 2] = jnp.exp(lgb * (CHUNK - row))
            tab_scr[h, 3] = jnp.exp(lgf * (CHUNK - 1.0 - col))
            tab_scr[h, 4] = jnp.exp(lgb * col)
            gc_scr[h] = jnp.exp(lg * CHUNK)

    def rows(n):
        return slice(n * CHUNK, (n + 1) * CHUNK)

    for h in range(N_RET_HEADS):
        cols = slice(h * hd, (h + 1) * hd)
        decay, qw_f, qw_b, kwt_f, kwt_b = (tab_scr[h, i] for i in range(5))
        gc = gc_scr[h]
        gc_f = gc[0:1, :]
        gc_b = gc[1:2, :]

        kv_f, kv_b = [], []
        for n in range(n_chunks):
            kt = k_ref[rows(n), cols].astype(F32).T
            vn = v_ref[rows(n), cols]
            kv_f.append(_dot((kt * kwt_f).astype(BF16), vn))
            kv_b.append(_dot((kt * kwt_b).astype(BF16), vn))

        s = s0f_ref[h]
        prev_f = []
        for n in range(n_chunks):
            prev_f.append(s.astype(BF16))
            s = gc_f * s + kv_f[n]
        sfo_ref[h] = s
        s = s0b_ref[h]
        prev_b = [None] * n_chunks
        for n in reversed(range(n_chunks)):
            prev_b[n] = s.astype(BF16)
            s = gc_b * s + kv_b[n]
        sbo_ref[h] = s

        gn = gn_ref[:, cols]
        for n in range(n_chunks):
            qn = q_ref[rows(n), cols]
            qf = qn.astype(F32)
            scores = lax.dot_general(qn, k_ref[rows(n), cols], (((1,), (1,)), ((), ())),
                                     preferred_element_type=F32)
            o = _dot((scores * decay).astype(BF16), v_ref[rows(n), cols])
            o = o + _dot((qf * qw_f).astype(BF16), prev_f[n])
            o = o + _dot((qf * qw_b).astype(BF16), prev_b[n])
            mu = jnp.mean(o, axis=-1, keepdims=True)
            d = o - mu
            var = jnp.mean(d * d, axis=-1, keepdims=True)
            on = d * lax.rsqrt(var + EPS) * gn
            r_ref[rows(n), cols] = (on * sg_ref[rows(n), cols].astype(F32)).astype(BF16)


def _retention(q, k, v, sg, dec, gn_g, s0f, s0b, batch, seq_len):
    hd = RET_HEAD_DIM
    tok_spec = pl.BlockSpec((seq_len, RET_WIDTH), lambda b: (b, 0))
    st_spec = pl.BlockSpec((None, N_RET_HEADS, hd, hd), lambda b: (b, 0, 0, 0))
    st_shape = jax.ShapeDtypeStruct((batch, N_RET_HEADS, hd, hd), F32)
    return pl.pallas_call(
        _retention_kernel,
        grid=(batch,),
        in_specs=[tok_spec, tok_spec, tok_spec, tok_spec,
                  pl.BlockSpec(dec.shape, lambda b: (0, 0, 0)),
                  pl.BlockSpec(gn_g.shape, lambda b: (0, 0)),
                  st_spec, st_spec],
        out_specs=[tok_spec, st_spec, st_spec],
        out_shape=[jax.ShapeDtypeStruct((batch * seq_len, RET_WIDTH), BF16), st_shape, st_shape],
        scratch_shapes=[pltpu.VMEM((N_RET_HEADS, 5, CHUNK, CHUNK), F32),
                        pltpu.VMEM((N_RET_HEADS, 2, hd), F32)],
        compiler_params=pltpu.CompilerParams(dimension_semantics=("arbitrary",),
                                             vmem_limit_bytes=VMEM_LIMIT),
        name="retention",
    )(q, k, v, sg, dec, gn_g, s0f, s0b)


def _fnet_merge_kernel(uf_ref, cs_ref, cls_ref, r_ref, gf_ref, gr_ref, x_ref, mod_ref, wf_ref, wr_ref, wo_ref,
                       o_ref, xcs_ref):
    seq_len = uf_ref.shape[0]
    gd = FOURIER_GROUP_DIM

    @pl.when(pl.program_id(1) == 0)
    def _():
        for g in range(N_FOURIER_GROUPS):
            x = _dot(uf_ref[:, g * gd:(g + 1) * gd], cs_ref[...])
            xcs_ref[0:seq_len, g * gd:(g + 1) * gd] = x[:, :gd].astype(BF16)
            xcs_ref[seq_len:2 * seq_len, g * gd:(g + 1) * gd] = x[:, gd:].astype(BF16)

    f_mix = _dot(cls_ref[...], xcs_ref[...]).astype(BF16)
    f_out = _dot(f_mix, wf_ref[...].astype(BF16))
    r_out = _dot(r_ref[...], wr_ref[...].astype(BF16))
    merged = gf_ref[...].astype(F32) * f_out + gr_ref[...].astype(F32) * r_out
    mix = _dot(merged.astype(BF16), wo_ref[...].astype(BF16))
    o_ref[...] = x_ref[...] + mod_ref[0, 2:3, :] * mix


def _fnet_merge(uf, cs, cls, r, gf, gr, x2d, mod3, w_four, w_ret, w_o, batch, seq_len, mod_row_of_batch):
    rb = min(FNET_ROWS, seq_len)
    nr = seq_len // rb

    def tok(w):
        return pl.BlockSpec((rb, w), lambda b, i: (b * nr + i, 0))

    def full(a):
        return pl.BlockSpec(a.shape, lambda b, i: (0, 0))

    def once(a):
        return pl.BlockSpec(a.shape, lambda b, i: (0, 0), pipeline_mode=pl.Buffered(1))

    return pl.pallas_call(
        _fnet_merge_kernel,
        grid=(batch, nr),
        in_specs=[pl.BlockSpec((seq_len, D_MODEL), lambda b, i: (b, 0)),
                  full(cs),
                  pl.BlockSpec((rb, 2 * seq_len), lambda b, i: (i, 0)),
                  tok(RET_WIDTH), tok(D_MODEL), tok(D_MODEL), tok(D_MODEL),
                  pl.BlockSpec((1, 6, D_MODEL), lambda b, i: (mod_row_of_batch(b), 0, 0)),
                  once(w_four), once(w_ret), once(w_o)],
        out_specs=tok(D_MODEL),
        out_shape=jax.ShapeDtypeStruct((batch * seq_len, D_MODEL), F32),
        scratch_shapes=[pltpu.VMEM((2 * seq_len, D_MODEL), BF16)],
        compiler_params=pltpu.CompilerParams(dimension_semantics=("parallel", "arbitrary"),
                                             vmem_limit_bytes=VMEM_LIMIT),
        name="fnet_merge",
    )(uf, cs, cls, r, gf, gr, x2d, mod3, w_four, w_ret, w_o)


def _pack_pair(lo_f32, hi_f32):
    lo = lax.bitcast_convert_type(lo_f32.astype(BF16).astype(F32), jnp.uint32)
    hi = lax.bitcast_convert_type(hi_f32.astype(BF16).astype(F32), jnp.uint32)
    return lax.bitcast_convert_type((lo >> 16) | hi, jnp.int32)


def _unpack_pair(words_i32):
    w = lax.bitcast_convert_type(words_i32, jnp.uint32)
    lo = lax.bitcast_convert_type(w << 16, F32)
    hi = lax.bitcast_convert_type(w & jnp.uint32(0xFFFF0000), F32)
    return lo, hi


def _load_token_words(ref, lead, n_tok):
    parts = []
    for s in range(ROW_SLABS):
        idx = (pl.ds(s, n_tok, stride=ROW_SLABS), slice(None))
        parts.append(ref[lead + idx] if lead else ref[idx])
    return jnp.concatenate(parts, axis=1)


def _store_token_words(ref, words, n_tok):
    for s in range(ROW_SLABS):
        ref[pl.ds(s, n_tok, stride=ROW_SLABS), :] = words[:, s * 128:(s + 1) * 128]


def _route(scores, biased):
    tokens = scores.shape[1]
    neg = -jnp.inf
    epg = EXPERTS_PER_GROUP
    iota_g = lax.broadcasted_iota(jnp.int32, (epg, tokens), 0).astype(F32)

    def pick_first_max(cur, iota, size):
        m = jnp.max(cur, axis=0, keepdims=True)
        idx = jnp.min(jnp.where(cur == m, iota, float(size)), axis=0, keepdims=True)
        return m, idx, iota == idx

    group_scores = []
    for g in range(N_EXPERT_GROUPS):
        vals = biased[g * epg:(g + 1) * epg, :]
        m1, _, hit = pick_first_max(vals, iota_g, epg)
        m2 = jnp.max(jnp.where(hit, neg, vals), axis=0, keepdims=True)
        group_scores.append(m1 + m2)
    cur = jnp.concatenate(group_scores, axis=0)
    group_sel = jnp.zeros_like(cur)
    for _ in range(TOPK_GROUPS):
        _, _, hit = pick_first_max(cur, iota_g, N_EXPERT_GROUPS)
        group_sel = jnp.where(hit, 1.0, group_sel)
        cur = jnp.where(hit, neg, cur)
    masked = jnp.concatenate(
        [jnp.where(group_sel[g:g + 1, :] > 0.0, biased[g * epg:(g + 1) * epg, :], neg)
         for g in range(N_EXPERT_GROUPS)], axis=0)
    iota_e = lax.broadcasted_iota(jnp.int32, masked.shape, 0).astype(F32)
    sel = jnp.zeros_like(masked)
    cur = masked
    picks = []
    for _ in range(TOP_K):
        _, idx, hit = pick_first_max(cur, iota_e, N_EXPERTS)
        picks.append(idx)
        sel = jnp.where(hit, 1.0, sel)
        cur = jnp.where(hit, neg, cur)
    w = scores * sel
    return w / jnp.sum(w, axis=0, keepdims=True) * ROUTED_SCALE, sel, picks


def _router_kernel(x_ref, mod_ref, g2_ref, wrt_ref, rb_ref, hp_ref, ek_ref, rk_ref, wt_ref, cnt_ref,
                   run_scr, earlier_scr):
    tm = x_ref.shape[0]

    @pl.when(pl.program_id(0) == 0)
    def _():
        run_scr[...] = jnp.zeros_like(run_scr)
        earlier = (lax.broadcasted_iota(jnp.int32, (tm, tm), 0) < lax.broadcasted_iota(jnp.int32, (tm, tm), 1))
        earlier_scr[...] = jnp.where(earlier, 1.0, 0.0).astype(BF16)

    h = _rms_mod(x_ref[...], g2_ref[...], mod_ref[0, 3:4, :], mod_ref[0, 4:5, :])
    half = D_MODEL // 2
    _store_token_words(hp_ref, _pack_pair(h[:, :half], h[:, half:]), tm)

    def split(a):
        hi = a.astype(BF16)
        return hi, (a - hi.astype(F32)).astype(BF16)

    def dot_nt(a, b):
        return lax.dot_general(a, b, (((1,), (1,)), ((), ())), preferred_element_type=F32)

    h_hi, h_lo = split(h)
    w_hi, w_lo = split(wrt_ref[...])
    logits_t = dot_nt(w_hi, h_hi) + (dot_nt(w_hi, h_lo) + dot_nt(w_lo, h_hi))
    scores = jax.nn.sigmoid(logits_t)
    comb_t, sel, picks = _route(scores, scores + rb_ref[...])

    rank_t = _dot(sel.astype(BF16), earlier_scr[...]) + run_scr[...]
    run_scr[...] += jnp.sum(sel, axis=1, keepdims=True)
    cnt_ref[...] = jnp.broadcast_to(run_scr[...], cnt_ref.shape)

    iota_e = lax.broadcasted_iota(jnp.int32, sel.shape, 0).astype(F32)
    ranks, weights = [], []
    for idx in picks:
        hit = iota_e == idx
        ranks.append(jnp.sum(jnp.where(hit, rank_t, 0.0), axis=0, keepdims=True))
        weights.append(jnp.sum(jnp.where(hit, comb_t, 0.0), axis=0, keepdims=True))
    ek_ref[...] = jnp.concatenate(picks, axis=0).astype(jnp.int32)
    rk_ref[...] = jnp.concatenate(ranks, axis=0).astype(jnp.int32)
    w_rep = jnp.concatenate([jnp.broadcast_to(w, (SC_LANES, tm)) for w in weights], axis=0)
    wt_ref[...] = w_rep.T


def _router(x1, mod3, norm2_g, w_router_t, router_bias, seq_len, mod_row_of_batch):
    t = x1.shape[0]
    tm = TM_ROUTER

    def mod_idx(i):
        return (mod_row_of_batch((i * tm) // seq_len), 0, 0)

    def full(a):
        return pl.BlockSpec(a.shape, lambda i: (0,) * a.ndim)

    return pl.pallas_call(
        _router_kernel,
        grid=(t // tm,),
        in_specs=[pl.BlockSpec((tm, D_MODEL), lambda i: (i, 0)),
                  pl.BlockSpec((1, 6, D_MODEL), mod_idx),
                  full(norm2_g), full(w_router_t), full(router_bias)],
        out_specs=[pl.BlockSpec((tm * ROW_SLABS, 128), lambda i: (i, 0)),
                   pl.BlockSpec((TOP_K, tm), lambda i: (0, i)),
                   pl.BlockSpec((TOP_K, tm), lambda i: (0, i)),
                   pl.BlockSpec((tm, 128), lambda i: (i, 0)),
                   pl.BlockSpec((N_EXPERTS, 128), lambda i: (0, 0))],
        out_shape=[jax.ShapeDtypeStruct((t * ROW_SLABS, 128), jnp.int32),
                   jax.ShapeDtypeStruct((TOP_K, t), jnp.int32),
                   jax.ShapeDtypeStruct((TOP_K, t), jnp.int32),
                   jax.ShapeDtypeStruct((t, 128), F32),
                   jax.ShapeDtypeStruct((N_EXPERTS, 128), F32)],
        scratch_shapes=[pltpu.VMEM((N_EXPERTS, 1), F32), pltpu.VMEM((tm, tm), BF16)],
        compiler_params=pltpu.CompilerParams(dimension_semantics=("arbitrary",),
                                             vmem_limit_bytes=VMEM_LIMIT),
        name="router",
    )(x1, mod3, norm2_g, w_router_t, router_bias)


def _plan_kernel(ek_ref, rk_ref, cnt_ref, pos_ref, texp_ref, nused_ref, tend_ref, *, expert_rows):
    rows = float(expert_rows)
    cnt = cnt_ref[:, 0:1]
    tiles = jnp.floor((cnt + (rows - 1.0)) / rows)
    before = (lax.broadcasted_iota(jnp.int32, (N_EXPERTS, N_EXPERTS), 1)
              < lax.broadcasted_iota(jnp.int32, (N_EXPERTS, N_EXPERTS), 0))
    tile_start = jnp.dot(jnp.where(before, 1.0, 0.0), jnp.broadcast_to(tiles, (N_EXPERTS, 128)),
                         precision=lax.Precision.HIGHEST, preferred_element_type=F32)[:, 0:1]
    tile_end = tile_start + tiles
    row_start = tile_start * rows

    ek = ek_ref[...]
    pos = rk_ref[...].astype(F32)
    tile_id = lax.broadcasted_iota(jnp.int32, texp_ref.shape, 1).astype(F32)
    texp = jnp.zeros(texp_ref.shape, F32)
    for e in range(N_EXPERTS):
        pos = pos + jnp.where(ek == e, row_start[e:e + 1, :], 0.0)
        texp = texp + jnp.where(tile_id >= tile_end[e:e + 1, :], 1.0, 0.0)
    pos_ref[...] = pos.astype(jnp.int32)
    texp_ref[...] = jnp.minimum(texp, N_EXPERTS - 1.0).astype(jnp.int32)
    nused_ref[...] = jnp.broadcast_to(tile_end[N_EXPERTS - 1:N_EXPERTS, :], nused_ref.shape).astype(jnp.int32)
    tend_ref[...] = jnp.broadcast_to(tile_end, tend_ref.shape).astype(jnp.int32)


def _plan(ek, rk, cnt, n_tiles_pad, expert_rows):
    t = ek.shape[1]

    def full(shape):
        return pl.BlockSpec(shape, lambda: (0,) * len(shape))

    return pl.pallas_call(
        functools.partial(_plan_kernel, expert_rows=expert_rows),
        in_specs=[full(ek.shape), full(rk.shape), full(cnt.shape)],
        out_specs=[full((TOP_K, t)), full((1, n_tiles_pad)), full((1, 128)), full((N_EXPERTS, 128))],
        out_shape=[jax.ShapeDtypeStruct((TOP_K, t), jnp.int32),
                   jax.ShapeDtypeStruct((1, n_tiles_pad), jnp.int32),
                   jax.ShapeDtypeStruct((1, 128), jnp.int32),
                   jax.ShapeDtypeStruct((N_EXPERTS, 128), jnp.int32)],
        compiler_params=pltpu.CompilerParams(vmem_limit_bytes=VMEM_LIMIT),
        name="plan",
    )(ek, rk, cnt)


def _sc_mesh():
    return plsc.VectorSubcoreMesh(core_axis_name="c", subcore_axis_name="s")


def _sc_dispatch(rows, pos3, n_out):
    t = rows.shape[0]
    ch = SC_CHUNK
    per_w = (t // ch) // SC_WORKERS

    @functools.partial(
        pl.kernel, out_type=jax.ShapeDtypeStruct((n_out,) + rows.shape[1:], jnp.int32), mesh=_sc_mesh(),
        scratch_types=[pltpu.VMEM((TOP_K, ch), jnp.int32), pltpu.VMEM((ch,) + rows.shape[1:], jnp.int32),
                       pltpu.SemaphoreType.DMA])
    def k(rows_hbm, pos_hbm, out_hbm, idx_v, rows_v, sem):
        wid = lax.axis_index("s") * SC_CORES + lax.axis_index("c")

        @pl.loop(0, per_w)
        def _(j):
            c = wid * per_w + j
            pltpu.sync_copy(pos_hbm.at[c], idx_v)
            pltpu.sync_copy(rows_hbm.at[pl.ds(c * ch, ch)], rows_v)
            copies = [pltpu.async_copy(rows_v, out_hbm.at[idx_v.at[kk]], sem) for kk in range(TOP_K)]
            for cp in copies:
                cp.wait()

    return k(rows, pos3)


def _sc_combine(table, pos3, wtok, t):
    ch = SC_CHUNK
    sub = SC_COMBINE_TOKENS
    lanes = SC_LANES
    slabs = ROW_SLABS
    per_w = (t // ch) // SC_WORKERS
    subs_per_chunk = ch // sub
    n_steps = per_w * subs_per_chunk

    @functools.partial(
        pl.kernel, out_type=jax.ShapeDtypeStruct((t, 2 * slabs, 128), F32), mesh=_sc_mesh(),
        scratch_types=[pltpu.VMEM((per_w, TOP_K, ch), jnp.int32),
                       pltpu.VMEM((2, TOP_K, sub, slabs, 128), jnp.int32),
                       pltpu.VMEM((2, sub, 128), F32),
                       pltpu.VMEM((sub, 2 * slabs, 128), F32),
                       pltpu.SemaphoreType.DMA((2,))],
        compiler_params=pltpu.CompilerParams(needs_layout_passes=False))
    def k(tab_hbm, pos_hbm, w_hbm, out_hbm, idx_v, rows_v, w_v, out_v, sem):
        wid = lax.axis_index("s") * SC_CORES + lax.axis_index("c")
        for j in range(per_w):
            pltpu.sync_copy(pos_hbm.at[wid * per_w + j], idx_v.at[j])

        def first_token(step):
            return (wid * per_w + step // subs_per_chunk) * ch + (step % subs_per_chunk) * sub

        def copies(step, slot):
            j = step // subs_per_chunk
            s = step % subs_per_chunk
            idx = [idx_v.at[j, kk, pl.ds(s * sub, sub)] for kk in range(TOP_K)]
            return ([pltpu.make_async_copy(tab_hbm.at[idx[kk]], rows_v.at[slot, kk], sem.at[slot])
                     for kk in range(TOP_K)]
                    + [pltpu.make_async_copy(w_hbm.at[pl.ds(first_token(step), sub)], w_v.at[slot], sem.at[slot])])

        for cp in copies(0, 0):
            cp.start()

        @pl.loop(0, n_steps)
        def _(step):
            slot = step % 2

            @pl.when(step + 1 < n_steps)
            def _():
                for cp in copies(step + 1, 1 - slot):
                    cp.start()

            for cp in copies(step, slot):
                cp.wait()

            @pl.loop(0, sub)
            def _(tt):
                wk = [w_v[slot, tt, pl.ds(kk * lanes, lanes)] for kk in range(TOP_K)]
                for sl in range(slabs):
                    @plsc.parallel_loop(0, 128, step=lanes, unroll=4)
                    def _(off):
                        acc_lo = jnp.zeros((lanes,), F32)
                        acc_hi = jnp.zeros((lanes,), F32)
                        for kk in range(TOP_K):
                            word = rows_v[slot, kk, tt, sl, pl.ds(off, lanes)]
                            lo = plsc.bitcast(word << 16, F32)
                            hi = plsc.bitcast(word & jnp.int32(-65536), F32)
                            acc_lo = acc_lo + wk[kk] * lo
                            acc_hi = acc_hi + wk[kk] * hi
                        out_v[tt, sl, pl.ds(off, lanes)] = acc_lo
                        out_v[tt, slabs + sl, pl.ds(off, lanes)] = acc_hi

            pltpu.sync_copy(out_v, out_hbm.at[pl.ds(first_token(step), sub)])

    return k(table, pos3, wtok)


def _experts_kernel(texp_ref, nused_ref, tend_ref, xs_ref, weg_hbm, weu_hbm, wed_hbm, ys_ref,
                    wg_scr, wu_scr, wd_scr, wg_buf, wu_buf, wd_buf, sem, group_scr, *, expert_rows):
    step = pl.program_id(0)
    rows = expert_rows
    tiles_per_step = EXPERT_STEP_ROWS // expert_rows
    half = D_MODEL // 2
    n_used = nused_ref[0]

    def weight_copies(e, slot):
        return [pltpu.make_async_copy(weg_hbm.at[e], wg_buf.at[slot], sem.at[slot, 0]),
                pltpu.make_async_copy(weu_hbm.at[e], wu_buf.at[slot], sem.at[slot, 1]),
                pltpu.make_async_copy(wed_hbm.at[e], wd_buf.at[slot], sem.at[slot, 2])]

    def next_group(e):
        tile = tend_ref[e]
        return texp_ref[jnp.minimum(tile, n_used - 1)], tile < n_used

    def start_weights(e, slot, exists):
        @pl.when(exists)
        def _():
            for cp in weight_copies(e, slot):
                cp.start()

    @pl.when(step == 0)
    def _():
        group_scr[0] = 0
        e, exists = texp_ref[0], True
        for slot in range(WEIGHT_SLOTS - 1):
            start_weights(e, slot, exists)
            nxt, has_next = next_group(e)
            e, exists = nxt, exists & has_next

    def row_tile(tile, x_view, y_view):
        expert = texp_ref[tile]
        used = tile < n_used
        new_expert = (tile == 0) | (expert != texp_ref[jnp.maximum(tile - 1, 0)])

        @pl.when(used & new_expert)
        def _():
            group = group_scr[0]
            slot = group % WEIGHT_SLOTS
            ahead, exists = expert, True
            for _ in range(WEIGHT_SLOTS - 1):
                nxt, has_next = next_group(ahead)
                ahead, exists = nxt, exists & has_next
            start_weights(ahead, (group + WEIGHT_SLOTS - 1) % WEIGHT_SLOTS, exists)

            for cp in weight_copies(expert, slot):
                cp.wait()
            wg_scr[...] = wg_buf[slot].astype(BF16)
            wu_scr[...] = wu_buf[slot].astype(BF16)
            wd_scr[...] = wd_buf[slot].astype(BF16)
            group_scr[0] = group + 1

        @pl.when(used)
        def _():
            lo, hi = _unpack_pair(_load_token_words(x_view, (), rows))
            lo = lo.astype(BF16)
            hi = hi.astype(BF16)
            g = _dot(lo, wg_scr[0:half, :]) + _dot(hi, wg_scr[half:D_MODEL, :])
            u = _dot(lo, wu_scr[0:half, :]) + _dot(hi, wu_scr[half:D_MODEL, :])
            y = _dot((_silu(g) * u).astype(BF16), wd_scr[...])
            _store_token_words(y_view, _pack_pair(y[:, :half], y[:, half:]), rows)

        @pl.when(jnp.logical_not(used) & (step == (n_used - 1) // tiles_per_step))
        def _():
            y_view[...] = jnp.zeros_like(y_view)

    for s in range(tiles_per_step):
        view = pl.ds(s * rows * ROW_SLABS, rows * ROW_SLABS)
        row_tile(step * tiles_per_step + s, xs_ref.at[view], ys_ref.at[view])


def _experts(texp, nused, tend, xs2d, weg, weu, wed, n_tiles, expert_rows):
    tiles_per_step = EXPERT_STEP_ROWS // expert_rows
    block = (EXPERT_STEP_ROWS * ROW_SLABS, 128)
    hbm = pl.BlockSpec(memory_space=pl.ANY)

    def block_idx(j, te, nu, tn):
        return (jnp.minimum(j, (nu[0] - 1) // tiles_per_step), 0)

    grid_spec = pltpu.PrefetchScalarGridSpec(
        num_scalar_prefetch=3,
        grid=(n_tiles // tiles_per_step,),
        in_specs=[pl.BlockSpec(block, block_idx), hbm, hbm, hbm],
        out_specs=pl.BlockSpec(block, block_idx),
        scratch_shapes=[pltpu.VMEM((D_MODEL, EXPERT_DIM), BF16),
                        pltpu.VMEM((D_MODEL, EXPERT_DIM), BF16),
                        pltpu.VMEM((EXPERT_DIM, D_MODEL), BF16),
                        pltpu.VMEM((WEIGHT_SLOTS, D_MODEL, EXPERT_DIM), F32),
                        pltpu.VMEM((WEIGHT_SLOTS, D_MODEL, EXPERT_DIM), F32),
                        pltpu.VMEM((WEIGHT_SLOTS, EXPERT_DIM, D_MODEL), F32),
                        pltpu.SemaphoreType.DMA((WEIGHT_SLOTS, 3)),
                        pltpu.SMEM((1,), jnp.int32)],
    )
    return pl.pallas_call(
        functools.partial(_experts_kernel, expert_rows=expert_rows),
        grid_spec=grid_spec,
        out_shape=jax.ShapeDtypeStruct(xs2d.shape, jnp.int32),
        compiler_params=pltpu.CompilerParams(dimension_semantics=("arbitrary",),
                                             vmem_limit_bytes=VMEM_LIMIT),
        name="experts",
    )(texp, nused, tend, xs2d, weg, weu, wed)


def _final_kernel(x_ref, routed_ref, mod_ref, g2_ref, wsg_ref, wsu_ref, wsd_ref, fng_ref, o_ref):
    tm = x_ref.shape[0]
    x = x_ref[...]
    hb = _rms_mod(x, g2_ref[...], mod_ref[0, 3:4, :], mod_ref[0, 4:5, :]).astype(BF16)
    shared = _dot((_silu(_dot(hb, wsg_ref[...])) * _dot(hb, wsu_ref[...])).astype(BF16), wsd_ref[...])
    n_slabs = 2 * ROW_SLABS
    routed = jnp.concatenate([routed_ref[pl.ds(s, tm, stride=n_slabs), :] for s in range(n_slabs)], axis=1)
    y = x + mod_ref[0, 5:6, :] * (routed + shared)
    ms = jnp.mean(y * y, axis=-1, keepdims=True)
    o_ref[...] = y * lax.rsqrt(ms + EPS) * fng_ref[...]


def _final(x1, routed2d, mod3, norm2_g, wsg, wsu, wsd, final_g, seq_len, mod_row_of_batch):
    t = x1.shape[0]
    tm = TM_FINAL

    def mod_idx(i):
        return (mod_row_of_batch((i * tm) // seq_len), 0, 0)

    def full(a):
        return pl.BlockSpec(a.shape, lambda i: (0,) * a.ndim)

    return pl.pallas_call(
        _final_kernel,
        grid=(t // tm,),
        in_specs=[pl.BlockSpec((tm, D_MODEL), lambda i: (i, 0)),
                  pl.BlockSpec((tm * 2 * ROW_SLABS, 128), lambda i: (i, 0)),
                  pl.BlockSpec((1, 6, D_MODEL), mod_idx),
                  full(norm2_g), full(wsg), full(wsu), full(wsd), full(final_g)],
        out_specs=pl.BlockSpec((tm, D_MODEL), lambda i: (i, 0)),
        out_shape=jax.ShapeDtypeStruct((t, D_MODEL), F32),
        compiler_params=pltpu.CompilerParams(dimension_semantics=("parallel",),
                                             vmem_limit_bytes=VMEM_LIMIT),
        name="final",
    )(x1, routed2d, mod3, norm2_g, wsg, wsu, wsd, final_g)


def _moe(x1, mod3, lw, seq_len, mod_row_of_batch):
    t = x1.shape[0]
    expert_rows = min(MAX_EXPERT_ROWS, TOP_K * t // N_EXPERTS // 2)
    n_tiles = TOP_K * t // expert_rows + N_EXPERTS
    n_tiles_pad = -(-n_tiles // 128) * 128
    hp2d, ek, rk, wtok, cnt = _router(x1, mod3, lw["norm2_g"], lw["w_router_t"], lw["router_bias"],
                                      seq_len, mod_row_of_batch)
    pos, texp, nused, tend = _plan(ek, rk, cnt, n_tiles_pad, expert_rows)
    pos3 = pos.reshape(TOP_K, t // SC_CHUNK, SC_CHUNK).transpose(1, 0, 2)
    xs = _sc_dispatch(hp2d.reshape(t, ROW_SLABS, 128), pos3, n_tiles * expert_rows)
    ys2d = _experts(texp.reshape(-1), nused.reshape(-1), tend[:, 0], xs.reshape(-1, 128),
                    lw["weg"], lw["weu"], lw["wed"], n_tiles, expert_rows)
    routed = _sc_combine(ys2d.reshape(-1, ROW_SLABS, 128), pos3, wtok, t)
    return _final(x1, routed.reshape(t * 2 * ROW_SLABS, 128), mod3, lw["norm2_g"],
                  lw["wsg"], lw["wsu"], lw["wsd"], lw["final_g"], seq_len, mod_row_of_batch)


def _dft_tables(seq_len):
    gd = FOURIER_GROUP_DIM
    kc = np.arange(gd)
    ang_c = ((kc[:, None] * kc[None, :]) % gd) * (2.0 * math.pi / gd)
    cs = np.concatenate([np.cos(ang_c), np.sin(ang_c)], axis=1) * (gd ** -0.5)
    kl = np.arange(seq_len)
    ang_l = ((kl[:, None] * kl[None, :]) % seq_len) * (2.0 * math.pi / seq_len)
    cls = np.concatenate([np.cos(ang_l), -np.sin(ang_l)], axis=1) * (seq_len ** -0.5)
    return jnp.asarray(cs.astype(np.float32), dtype=BF16), jnp.asarray(cls.astype(np.float32), dtype=BF16)


def _rope_tables(length):
    rows = length // GRID_W
    r = np.repeat(np.arange(rows, dtype=np.float32), GRID_W)
    col = np.tile(np.arange(GRID_W, dtype=np.float32), rows)
    nf = RET_HEAD_DIM // 4
    inv = (np.float32(ROPE_BASE) ** (-np.arange(nf, dtype=np.float32) / np.float32(nf))).astype(np.float32)
    ar = r[:, None] * inv[None]
    ac = col[:, None] * inv[None]
    ang = np.concatenate([ar, ar, ac, ac], axis=-1).astype(np.float64)
    sign = np.where((np.arange(RET_HEAD_DIM) & nf) == 0, -1.0, 1.0)
    return (jnp.asarray(np.cos(ang).astype(np.float32)),
            jnp.asarray((np.sin(ang) * sign[None, :]).astype(np.float32)))


def _trunk_path(x, mod3, mod_row_of_batch, s0f, s0b, rope, lw):
    batch, seq_len, _ = x.shape
    x2d = x.reshape(batch * seq_len, D_MODEL)
    uf, q, k, v, sg, gf, gr = _inproj(x2d, mod3, lw["norm1_g"], lw["w_in"], seq_len, mod_row_of_batch, rope)
    r, s_f, s_b = _retention(q, k, v, sg, lw["dec"], lw["gn_g"], s0f, s0b, batch, seq_len)
    cs, cls = _dft_tables(seq_len)
    x1 = _fnet_merge(uf, cs, cls, r, gf, gr, x2d, mod3, lw["w_four"], lw["w_ret"], lw["w_o"],
                     batch, seq_len, mod_row_of_batch)
    y = _moe(x1, mod3, lw, seq_len, mod_row_of_batch)
    return y.reshape(batch, seq_len, D_MODEL), s_f, s_b


def kernel(x_prompt, x_sample, state_ret_fwd, state_ret_bwd, c, c_ctx, w_ada, b_ada, norm1_g, norm2_g, w_in,
           ret_decay_fwd, ret_decay_bwd, ret_gn_g, w_four_out, w_ret_out, w_out, w_router, router_bias,
           w_exp_gate, w_exp_up, w_exp_down, w_shared_gate, w_shared_up, w_shared_down, final_norm_g):
    depth = w_ada.shape[0]
    assert depth == 1, "final norm is fused into the last layer's MoE kernel"
    n_ctx, n_lat = x_prompt.shape[0], x_sample.shape[0]
    cond = jnp.concatenate([c_ctx[None, :], c], axis=0)
    cond = jnp.pad(cond, ((0, (-cond.shape[0]) % 8), (0, 0)))
    rope = _rope_tables(x_sample.shape[1])
    zeros = jnp.zeros((n_ctx, N_RET_HEADS, RET_HEAD_DIM, RET_HEAD_DIM), F32)

    layer = 0
    mod = _ada(cond, w_ada[layer], b_ada[layer][None, :])
    mod3 = mod.reshape(mod.shape[0], 6, D_MODEL)
    dec = jnp.stack([ret_decay_fwd[layer], ret_decay_bwd[layer]], axis=1)
    lw = {
        "norm1_g": norm1_g[layer][None, :],
        "norm2_g": norm2_g[layer][None, :],
        "w_in": w_in[layer],
        "dec": jnp.broadcast_to(dec[:, :, None], (N_RET_HEADS, 2, RET_HEAD_DIM)).astype(F32),
        "gn_g": ret_gn_g[layer][None, :],
        "w_four": w_four_out[layer],
        "w_ret": w_ret_out[layer],
        "w_o": w_out[layer],
        "w_router_t": w_router[layer].T,
        "router_bias": router_bias[layer][:, None],
        "weg": w_exp_gate[layer],
        "weu": w_exp_up[layer],
        "wed": w_exp_down[layer],
        "wsg": w_shared_gate[layer].astype(BF16),
        "wsu": w_shared_up[layer].astype(BF16),
        "wsd": w_shared_down[layer].astype(BF16),
        "final_g": final_norm_g[None, :],
    }
    y_prompt, s_f, s_b = _trunk_path(x_prompt, mod3, lambda b: 0, zeros, zeros, None, lw)
    y_sample, _, _ = _trunk_path(x_sample, mod3, lambda b: 1 + b, state_ret_fwd[:, layer],
                                 state_ret_bwd[:, layer], rope, lw)
    return (y_prompt, y_sample, s_f[:, None], s_b[:, None])
```

```python
import functools
import math

import jax
import jax.numpy as jnp
import numpy as np
from jax import lax
from jax.experimental import pallas as pl
from jax.experimental.pallas import tpu as pltpu
from jax.experimental.pallas import tpu_sc as plsc

F32 = jnp.float32
BF16 = jnp.bfloat16

D_MODEL = 1024
GRID_W = 64
N_FOURIER_GROUPS = 8
FOURIER_GROUP_DIM = 128
N_RET_HEADS = 4
RET_HEAD_DIM = 128
RET_WIDTH = N_RET_HEADS * RET_HEAD_DIM
CHUNK = 128
N_EXPERTS = 64
N_EXPERT_GROUPS = 8
EXPERTS_PER_GROUP = N_EXPERTS // N_EXPERT_GROUPS
TOPK_GROUPS = 4
TOP_K = 8
EXPERT_DIM = 256
ROUTED_SCALE = 2.5
ROPE_BASE = 10000.0
EPS = 1e-6
Q_SCALE = RET_HEAD_DIM ** -0.5

_C_UF = (0, 1024)
_C_Q = (1024, 1536)
_C_K = (1536, 2048)
_C_V = (2048, 2560)
_C_G = (2560, 3072)
_C_GF = (3072, 4096)
_C_GR = (4096, 5120)

VMEM_LIMIT = 56 * 1024 * 1024

TM_INPROJ = 1024
TM_ROUTER = 1024
FNET_ROWS = 512
TM_FINAL = 1024
EXPERT_STEP_ROWS = 1024
MAX_EXPERT_ROWS = 512
WEIGHT_SLOTS = 3
ROW_SLABS = 4
SC_CORES = 2
SC_WORKERS = 32
SC_CHUNK = 128
SC_LANES = 16
SC_PACK_BLOCK_WORDS = 16384
SC_COMBINE_TOKENS = 8


def _silu(x):
    return x * jax.nn.sigmoid(x)


def _dot(a, b):
    return jnp.dot(a, b, preferred_element_type=F32)


def _rms_mod(x, g, shift, scale):
    ms = jnp.mean(x * x, axis=-1, keepdims=True)
    y = x * lax.rsqrt(ms + EPS) * g
    return y * (1.0 + scale) + shift


def _ada_kernel(cond_ref, w_ref, b_ref, o_ref):
    s = _silu(cond_ref[...]).astype(BF16)
    o_ref[...] = _dot(s, w_ref[...].astype(BF16)) + b_ref[...]


def _ada(cond, w_ada, b_ada):
    rows, n = cond.shape[0], w_ada.shape[1]
    tn = 1536
    return pl.pallas_call(
        _ada_kernel,
        grid=(n // tn,),
        in_specs=[pl.BlockSpec((rows, D_MODEL), lambda j: (0, 0)),
                  pl.BlockSpec((D_MODEL, tn), lambda j: (0, j)),
                  pl.BlockSpec((1, tn), lambda j: (0, j))],
        out_specs=pl.BlockSpec((rows, tn), lambda j: (0, j)),
        out_shape=jax.ShapeDtypeStruct((rows, n), F32),
        compiler_params=pltpu.CompilerParams(vmem_limit_bytes=VMEM_LIMIT),
        name="ada",
    )(cond, w_ada, b_ada)


def _rope_head(x, cos, sin_signed, first_half):
    partner = jnp.where(first_half, pltpu.roll(x, 96, 1), pltpu.roll(x, 32, 1))
    return x * cos + partner * sin_signed


def _inproj_kernel(*refs, use_rope):
    if use_rope:
        x_ref, mod_ref, g_ref, w_ref, cos_ref, sin_ref = refs[:6]
        outs = refs[6:]
    else:
        x_ref, mod_ref, g_ref, w_ref = refs[:4]
        outs = refs[4:]
    uf_o, q_o, k_o, v_o, sg_o, gf_o, gr_o = outs

    h = _rms_mod(x_ref[...], g_ref[...], mod_ref[0, 0:1, :], mod_ref[0, 1:2, :])
    hb = h.astype(BF16)

    def proj(cols):
        return _dot(hb, w_ref[:, cols[0]:cols[1]].astype(BF16))

    uf_o[...] = proj(_C_UF).astype(BF16)
    q = proj(_C_Q)
    k = proj(_C_K)
    if use_rope:
        cos = cos_ref[...]
        sin_signed = sin_ref[...]
        lane = lax.broadcasted_iota(jnp.int32, cos.shape, 1)
        first_half = (lane & 32) == 0
        for hd in range(N_RET_HEADS):
            sl = slice(hd * RET_HEAD_DIM, (hd + 1) * RET_HEAD_DIM)
            q_o[:, sl] = (_rope_head(q[:, sl], cos, sin_signed, first_half) * Q_SCALE).astype(BF16)
            k_o[:, sl] = _rope_head(k[:, sl], cos, sin_signed, first_half).astype(BF16)
    else:
        q_o[...] = (q * Q_SCALE).astype(BF16)
        k_o[...] = k.astype(BF16)
    v_o[...] = proj(_C_V).astype(BF16)
    sg_o[...] = _silu(proj(_C_G)).astype(BF16)
    gf_o[...] = jax.nn.sigmoid(proj(_C_GF)).astype(BF16)
    gr_o[...] = jax.nn.sigmoid(proj(_C_GR)).astype(BF16)


def _inproj(x2d, mod3, norm_g, w_in_f32, seq_len, mod_row_of_batch, rope):
    t = x2d.shape[0]
    tm = TM_INPROJ
    tiles_per_seq = max(seq_len // tm, 1)

    def mod_idx(i):
        return (mod_row_of_batch((i * tm) // seq_len), 0, 0)

    in_specs = [pl.BlockSpec((tm, D_MODEL), lambda i: (i, 0)),
                pl.BlockSpec((1, 6, D_MODEL), mod_idx),
                pl.BlockSpec((1, D_MODEL), lambda i: (0, 0)),
                pl.BlockSpec(w_in_f32.shape, lambda i: (0, 0), pipeline_mode=pl.Buffered(1))]
    args = [x2d, mod3, norm_g, w_in_f32]
    if rope is not None:
        in_specs += [pl.BlockSpec((tm, RET_HEAD_DIM), lambda i: (i % tiles_per_seq, 0))] * 2
        args += list(rope)
    widths = [1024, RET_WIDTH, RET_WIDTH, RET_WIDTH, RET_WIDTH, 1024, 1024]
    return pl.pallas_call(
        functools.partial(_inproj_kernel, use_rope=rope is not None),
        grid=(t // tm,),
        in_specs=in_specs,
        out_specs=[pl.BlockSpec((tm, w), lambda i: (i, 0)) for w in widths],
        out_shape=[jax.ShapeDtypeStruct((t, w), BF16) for w in widths],
        compiler_params=pltpu.CompilerParams(dimension_semantics=("parallel",),
                                             vmem_limit_bytes=VMEM_LIMIT),
        name="inproj",
    )(*args)


def _retention_kernel(q_ref, k_ref, v_ref, sg_ref, dec_ref, gn_ref, s0f_ref, s0b_ref,
                      r_ref, sfo_ref, sbo_ref, tab_scr, gc_scr):
    n_chunks = q_ref.shape[0] // CHUNK
    hd = RET_HEAD_DIM

    @pl.when(pl.program_id(0) == 0)
    def _():
        row = lax.broadcasted_iota(jnp.int32, (CHUNK, CHUNK), 0).astype(F32)
        col = lax.broadcasted_iota(jnp.int32, (CHUNK, CHUNK), 1).astype(F32)
        diff = row - col
        for h in range(N_RET_HEADS):
            dec = dec_ref[h]
            lg = jnp.minimum(dec, 0.0) - jnp.log1p(jnp.exp(-jnp.abs(dec)))
            lgf = lg[0:1, :]
            lgb = lg[1:2, :]
            tab_scr[h, 0] = jnp.exp(jnp.where(diff >= 0, lgf * diff, lgb * (-diff)))
            tab_scr[h, 1] = jnp.exp(lgf * (row + 1.0))
            tab_scr[h, 2] = jnp.exp(lgb * (CHUNK - row))
            tab_scr[h, 3] = jnp.exp(lgf * (CHUNK - 1.0 - col))
            tab_scr[h, 4] = jnp.exp(lgb * col)
            gc_scr[h] = jnp.exp(lg * CHUNK)

    def rows(n):
        return slice(n * CHUNK, (n + 1) * CHUNK)

    for h in range(N_RET_HEADS):
        cols = slice(h * hd, (h + 1) * hd)
        decay, qw_f, qw_b, kwt_f, kwt_b = (tab_scr[h, i] for i in range(5))
        gc = gc_scr[h]
        gc_f = gc[0:1, :]
        gc_b = gc[1:2, :]

        kv_f, kv_b = [], []
        for n in range(n_chunks):
            kt = k_ref[rows(n), cols].astype(F32).T
            vn = v_ref[rows(n), cols]
            kv_f.append(_dot((kt * kwt_f).astype(BF16), vn))
            kv_b.append(_dot((kt * kwt_b).astype(BF16), vn))

        s = s0f_ref[h]
        prev_f = []
        for n in range(n_chunks):
            prev_f.append(s.astype(BF16))
            s = gc_f * s + kv_f[n]
        sfo_ref[h] = s
        s = s0b_ref[h]
        prev_b = [None] * n_chunks
        for n in reversed(range(n_chunks)):
            prev_b[n] = s.astype(BF16)
            s = gc_b * s + kv_b[n]
        sbo_ref[h] = s

        gn = gn_ref[:, cols]
        for n in range(n_chunks):
            qn = q_ref[rows(n), cols]
            qf = qn.astype(F32)
            scores = lax.dot_general(qn, k_ref[rows(n), cols], (((1,), (1,)), ((), ())),
                                     preferred_element_type=F32)
            o = _dot((scores * decay).astype(BF16), v_ref[rows(n), cols])
            o = o + _dot((qf * qw_f).astype(BF16), prev_f[n])
            o = o + _dot((qf * qw_b).astype(BF16), prev_b[n])
            mu = jnp.mean(o, axis=-1, keepdims=True)
            d = o - mu
            var = jnp.mean(d * d, axis=-1, keepdims=True)
            on = d * lax.rsqrt(var + EPS) * gn
            r_ref[rows(n), cols] = (on * sg_ref[rows(n), cols].astype(F32)).astype(BF16)


def _retention(q, k, v, sg, dec, gn_g, s0f, s0b, batch, seq_len):
    hd = RET_HEAD_DIM
    tok_spec = pl.BlockSpec((seq_len, RET_WIDTH), lambda b: (b, 0))
    st_spec = pl.BlockSpec((None, N_RET_HEADS, hd, hd), lambda b: (b, 0, 0, 0))
    st_shape = jax.ShapeDtypeStruct((batch, N_RET_HEADS, hd, hd), F32)
    return pl.pallas_call(
        _retention_kernel,
        grid=(batch,),
        in_specs=[tok_spec, tok_spec, tok_spec, tok_spec,
                  pl.BlockSpec(dec.shape, lambda b: (0, 0, 0)),
                  pl.BlockSpec(gn_g.shape, lambda b: (0, 0)),
                  st_spec, st_spec],
        out_specs=[tok_spec, st_spec, st_spec],
        out_shape=[jax.ShapeDtypeStruct((batch * seq_len, RET_WIDTH), BF16), st_shape, st_shape],
        scratch_shapes=[pltpu.VMEM((N_RET_HEADS, 5, CHUNK, CHUNK), F32),
                        pltpu.VMEM((N_RET_HEADS, 2, hd), F32)],
        compiler_params=pltpu.CompilerParams(dimension_semantics=("arbitrary",),
                                             vmem_limit_bytes=VMEM_LIMIT),
        name="retention",
    )(q, k, v, sg, dec, gn_g, s0f, s0b)


def _fnet_merge_kernel(uf_ref, cs_ref, cls_ref, r_ref, gf_ref, gr_ref, x_ref, mod_ref, wf_ref, wr_ref, wo_ref,
                       o_ref, xcs_ref):
    seq_len = uf_ref.shape[0]
    gd = FOURIER_GROUP_DIM

    @pl.when(pl.program_id(1) == 0)
    def _():
        for g in range(N_FOURIER_GROUPS):
            x = _dot(uf_ref[:, g * gd:(g + 1) * gd], cs_ref[...])
            xcs_ref[0:seq_len, g * gd:(g + 1) * gd] = x[:, :gd].astype(BF16)
            xcs_ref[seq_len:2 * seq_len, g * gd:(g + 1) * gd] = x[:, gd:].astype(BF16)

    f_mix = _dot(cls_ref[...], xcs_ref[...]).astype(BF16)
    f_out = _dot(f_mix, wf_ref[...].astype(BF16))
    r_out = _dot(r_ref[...], wr_ref[...].astype(BF16))
    merged = gf_ref[...].astype(F32) * f_out + gr_ref[...].astype(F32) * r_out
    mix = _dot(merged.astype(BF16), wo_ref[...].astype(BF16))
    o_ref[...] = x_ref[...] + mod_ref[0, 2:3, :] * mix


def _fnet_merge(uf, cs, cls, r, gf, gr, x2d, mod3, w_four, w_ret, w_o, batch, seq_len, mod_row_of_batch):
    rb = min(FNET_ROWS, seq_len)
    nr = seq_len // rb

    def tok(w):
        return pl.BlockSpec((rb, w), lambda b, i: (b * nr + i, 0))

    def full(a):
        return pl.BlockSpec(a.shape, lambda b, i: (0, 0))

    def once(a):
        return pl.BlockSpec(a.shape, lambda b, i: (0, 0), pipeline_mode=pl.Buffered(1))

    return pl.pallas_call(
        _fnet_merge_kernel,
        grid=(batch, nr),
        in_specs=[pl.BlockSpec((seq_len, D_MODEL), lambda b, i: (b, 0)),
                  full(cs),
                  pl.BlockSpec((rb, 2 * seq_len), lambda b, i: (i, 0)),
                  tok(RET_WIDTH), tok(D_MODEL), tok(D_MODEL), tok(D_MODEL),
                  pl.BlockSpec((1, 6, D_MODEL), lambda b, i: (mod_row_of_batch(b), 0, 0)),
                  once(w_four), once(w_ret), once(w_o)],
        out_specs=tok(D_MODEL),
        out_shape=jax.ShapeDtypeStruct((batch * seq_len, D_MODEL), F32),
        scratch_shapes=[pltpu.VMEM((2 * seq_len, D_MODEL), BF16)],
        compiler_params=pltpu.CompilerParams(dimension_semantics=("parallel", "arbitrary"),
                                             vmem_limit_bytes=VMEM_LIMIT),
        name="fnet_merge",
    )(uf, cs, cls, r, gf, gr, x2d, mod3, w_four, w_ret, w_o)


def _pack_pair(lo_f32, hi_f32):
    lo = lax.bitcast_convert_type(lo_f32.astype(BF16).astype(F32), jnp.uint32)
    hi = lax.bitcast_convert_type(hi_f32.astype(BF16).astype(F32), jnp.uint32)
    return lax.bitcast_convert_type((lo >> 16) | hi, jnp.int32)


def _unpack_pair(words_i32):
    w = lax.bitcast_convert_type(words_i32, jnp.uint32)
    lo = lax.bitcast_convert_type(w << 16, F32)
    hi = lax.bitcast_convert_type(w & jnp.uint32(0xFFFF0000), F32)
    return lo, hi


def _load_token_words(ref, lead, n_tok):
    parts = []
    for s in range(ROW_SLABS):
        idx = (pl.ds(s, n_tok, stride=ROW_SLABS), slice(None))
        parts.append(ref[lead + idx] if lead else ref[idx])
    return jnp.concatenate(parts, axis=1)


def _store_token_words(ref, words, n_tok):
    for s in range(ROW_SLABS):
        ref[pl.ds(s, n_tok, stride=ROW_SLABS), :] = words[:, s * 128:(s + 1) * 128]


def _route(scores, biased):
    tokens = scores.shape[1]
    neg = -jnp.inf
    epg = EXPERTS_PER_GROUP
    iota_g = lax.broadcasted_iota(jnp.int32, (epg, tokens), 0).astype(F32)

    def pick_first_max(cur, iota, size):
        m = jnp.max(cur, axis=0, keepdims=True)
        idx = jnp.min(jnp.where(cur == m, iota, float(size)), axis=0, keepdims=True)
        return m, idx, iota == idx

    group_scores = []
    for g in range(N_EXPERT_GROUPS):
        vals = biased[g * epg:(g + 1) * epg, :]
        m1, _, hit = pick_first_max(vals, iota_g, epg)
        m2 = jnp.max(jnp.where(hit, neg, vals), axis=0, keepdims=True)
        group_scores.append(m1 + m2)
    cur = jnp.concatenate(group_scores, axis=0)
    group_sel = jnp.zeros_like(cur)
    for _ in range(TOPK_GROUPS):
        _, _, hit = pick_first_max(cur, iota_g, N_EXPERT_GROUPS)
        group_sel = jnp.where(hit, 1.0, group_sel)
        cur = jnp.where(hit, neg, cur)
    masked = jnp.concatenate(
        [jnp.where(group_sel[g:g + 1, :] > 0.0, biased[g * epg:(g + 1) * epg, :], neg)
         for g in range(N_EXPERT_GROUPS)], axis=0)
    iota_e = lax.broadcasted_iota(jnp.int32, masked.shape, 0).astype(F32)
    sel = jnp.zeros_like(masked)
    cur = masked
    picks = []
    for _ in range(TOP_K):
        _, idx, hit = pick_first_max(cur, iota_e, N_EXPERTS)
        picks.append(idx)
        sel = jnp.where(hit, 1.0, sel)
        cur = jnp.where(hit, neg, cur)
    w = scores * sel
    return w / jnp.sum(w, axis=0, keepdims=True) * ROUTED_SCALE, sel, picks


def _router_kernel(x_ref, mod_ref, g2_ref, wrt_ref, rb_ref, hp_ref, ek_ref, rk_ref, wt_ref, cnt_ref,
                   run_scr, earlier_scr):
    tm = x_ref.shape[0]

    @pl.when(pl.program_id(0) == 0)
    def _():
        run_scr[...] = jnp.zeros_like(run_scr)
        earlier = (lax.broadcasted_iota(jnp.int32, (tm, tm), 0) < lax.broadcasted_iota(jnp.int32, (tm, tm), 1))
        earlier_scr[...] = jnp.where(earlier, 1.0, 0.0).astype(BF16)

    h = _rms_mod(x_ref[...], g2_ref[...], mod_ref[0, 3:4, :], mod_ref[0, 4:5, :])
    half = D_MODEL // 2
    _store_token_words(hp_ref, _pack_pair(h[:, :half], h[:, half:]), tm)

    def split(a):
        hi = a.astype(BF16)
        return hi, (a - hi.astype(F32)).astype(BF16)

    def dot_nt(a, b):
        return lax.dot_general(a, b, (((1,), (1,)), ((), ())), preferred_element_type=F32)

    h_hi, h_lo = split(h)
    w_hi, w_lo = split(wrt_ref[...])
    logits_t = dot_nt(w_hi, h_hi) + (dot_nt(w_hi, h_lo) + dot_nt(w_lo, h_hi))
    scores = jax.nn.sigmoid(logits_t)
    comb_t, sel, picks = _route(scores, scores + rb_ref[...])

    rank_t = _dot(sel.astype(BF16), earlier_scr[...]) + run_scr[...]
    run_scr[...] += jnp.sum(sel, axis=1, keepdims=True)
    cnt_ref[...] = jnp.broadcast_to(run_scr[...], cnt_ref.shape)

    iota_e = lax.broadcasted_iota(jnp.int32, sel.shape, 0).astype(F32)
    ranks, weights = [], []
    for idx in picks:
        hit = iota_e == idx
        ranks.append(jnp.sum(jnp.where(hit, rank_t, 0.0), axis=0, keepdims=True))
        weights.append(jnp.sum(jnp.where(hit, comb_t, 0.0), axis=0, keepdims=True))
    ek_ref[...] = jnp.concatenate(picks, axis=0).astype(jnp.int32)
    rk_ref[...] = jnp.concatenate(ranks, axis=0).astype(jnp.int32)
    w_rep = jnp.concatenate([jnp.broadcast_to(w, (SC_LANES, tm)) for w in weights], axis=0)
    wt_ref[...] = w_rep.T


def _router(x1, mod3, norm2_g, w_router_t, router_bias, seq_len, mod_row_of_batch):
    t = x1.shape[0]
    tm = TM_ROUTER

    def mod_idx(i):
        return (mod_row_of_batch((i * tm) // seq_len), 0, 0)

    def full(a):
        return pl.BlockSpec(a.shape, lambda i: (0,) * a.ndim)

    return pl.pallas_call(
        _router_kernel,
        grid=(t // tm,),
        in_specs=[pl.BlockSpec((tm, D_MODEL), lambda i: (i, 0)),
                  pl.BlockSpec((1, 6, D_MODEL), mod_idx),
                  full(norm2_g), full(w_router_t), full(router_bias)],
        out_specs=[pl.BlockSpec((tm * ROW_SLABS, 128), lambda i: (i, 0)),
                   pl.BlockSpec((TOP_K, tm), lambda i: (0, i)),
                   pl.BlockSpec((TOP_K, tm), lambda i: (0, i)),
                   pl.BlockSpec((tm, 128), lambda i: (i, 0)),
                   pl.BlockSpec((N_EXPERTS, 128), lambda i: (0, 0))],
        out_shape=[jax.ShapeDtypeStruct((t * ROW_SLABS, 128), jnp.int32),
                   jax.ShapeDtypeStruct((TOP_K, t), jnp.int32),
                   jax.ShapeDtypeStruct((TOP_K, t), jnp.int32),
                   jax.ShapeDtypeStruct((t, 128), F32),
                   jax.ShapeDtypeStruct((N_EXPERTS, 128), F32)],
        scratch_shapes=[pltpu.VMEM((N_EXPERTS, 1), F32), pltpu.VMEM((tm, tm), BF16)],
        compiler_params=pltpu.CompilerParams(dimension_semantics=("arbitrary",),
                                             vmem_limit_bytes=VMEM_LIMIT),
        name="router",
    )(x1, mod3, norm2_g, w_router_t, router_bias)


def _plan_kernel(ek_ref, rk_ref, cnt_ref, pos_ref, texp_ref, nused_ref, tend_ref, *, expert_rows):
    rows = float(expert_rows)
    cnt = cnt_ref[:, 0:1]
    tiles = jnp.floor((cnt + (rows - 1.0)) / rows)
    before = (lax.broadcasted_iota(jnp.int32, (N_EXPERTS, N_EXPERTS), 1)
              < lax.broadcasted_iota(jnp.int32, (N_EXPERTS, N_EXPERTS), 0))
    tile_start = jnp.dot(jnp.where(before, 1.0, 0.0), jnp.broadcast_to(tiles, (N_EXPERTS, 128)),
                         precision=lax.Precision.HIGHEST, preferred_element_type=F32)[:, 0:1]
    tile_end = tile_start + tiles
    row_start = tile_start * rows

    ek = ek_ref[...]
    pos = rk_ref[...].astype(F32)
    tile_id = lax.broadcasted_iota(jnp.int32, texp_ref.shape, 1).astype(F32)
    texp = jnp.zeros(texp_ref.shape, F32)
    for e in range(N_EXPERTS):
        pos = pos + jnp.where(ek == e, row_start[e:e + 1, :], 0.0)
        texp = texp + jnp.where(tile_id >= tile_end[e:e + 1, :], 1.0, 0.0)
    pos_ref[...] = pos.astype(jnp.int32)
    texp_ref[...] = jnp.minimum(texp, N_EXPERTS - 1.0).astype(jnp.int32)
    nused_ref[...] = jnp.broadcast_to(tile_end[N_EXPERTS - 1:N_EXPERTS, :], nused_ref.shape).astype(jnp.int32)
    tend_ref[...] = jnp.broadcast_to(tile_end, tend_ref.shape).astype(jnp.int32)


def _plan(ek, rk, cnt, n_tiles_pad, expert_rows):
    t = ek.shape[1]

    def full(shape):
        return pl.BlockSpec(shape, lambda: (0,) * len(shape))

    return pl.pallas_call(
        functools.partial(_plan_kernel, expert_rows=expert_rows),
        in_specs=[full(ek.shape), full(rk.shape), full(cnt.shape)],
        out_specs=[full((TOP_K, t)), full((1, n_tiles_pad)), full((1, 128)), full((N_EXPERTS, 128))],
        out_shape=[jax.ShapeDtypeStruct((TOP_K, t), jnp.int32),
                   jax.ShapeDtypeStruct((1, n_tiles_pad), jnp.int32),
                   jax.ShapeDtypeStruct((1, 128), jnp.int32),
                   jax.ShapeDtypeStruct((N_EXPERTS, 128), jnp.int32)],
        compiler_params=pltpu.CompilerParams(vmem_limit_bytes=VMEM_LIMIT),
        name="plan",
    )(ek, rk, cnt)


def _sc_mesh():
    return plsc.VectorSubcoreMesh(core_axis_name="c", subcore_axis_name="s")


def _sc_pack_weight_halves(w):
    e, k, n = w.shape
    k_half = k // 2
    rb = SC_PACK_BLOCK_WORDS // n
    units_per_expert = k_half // rb
    per_w = (e * units_per_expert) // SC_WORKERS
    lanes = SC_LANES

    @functools.partial(
        pl.kernel, out_type=jax.ShapeDtypeStruct((e * k_half, n), jnp.int32), mesh=_sc_mesh(),
        scratch_types=[pltpu.VMEM((rb, n), F32), pltpu.VMEM((rb, n), F32), pltpu.VMEM((rb, n), jnp.int32)],
        compiler_params=pltpu.CompilerParams(needs_layout_passes=False))
    def kern(w_hbm, out_hbm, a_v, b_v, o_v):
        wid = lax.axis_index("s") * SC_CORES + lax.axis_index("c")

        @pl.loop(0, per_w)
        def _(j):
            unit = wid * per_w + j
            expert = unit // units_per_expert
            blk = unit % units_per_expert
            row_a = expert * k + blk * rb
            pltpu.sync_copy(w_hbm.at[pl.ds(row_a, rb)], a_v)
            pltpu.sync_copy(w_hbm.at[pl.ds(row_a + k_half, rb)], b_v)

            @pl.loop(0, rb)
            def _(r):
                @plsc.parallel_loop(0, n, step=lanes, unroll=4)
                def _(c):
                    both = plsc.pack(a_v[r, pl.ds(c, lanes)], b_v[r, pl.ds(c, lanes)],
                                     format=plsc.PackFormat.INTERLEAVED)
                    o_v[r, pl.ds(c, lanes)] = plsc.bitcast(both, jnp.int32)

            pltpu.sync_copy(o_v, out_hbm.at[pl.ds(expert * k_half + blk * rb, rb)])

    return kern(w.reshape(e * k, n)).reshape(e, k_half, n)


def _sc_dispatch(rows, pos3, n_out, after=()):
    t = rows.shape[0]
    ch = SC_CHUNK
    per_w = (t // ch) // SC_WORKERS

    @functools.partial(
        pl.kernel, out_type=jax.ShapeDtypeStruct((n_out,) + rows.shape[1:], jnp.int32), mesh=_sc_mesh(),
        scratch_types=[pltpu.VMEM((TOP_K, ch), jnp.int32), pltpu.VMEM((ch,) + rows.shape[1:], jnp.int32),
                       pltpu.SemaphoreType.DMA])
    def k(rows_hbm, pos_hbm, *rest):
        out_hbm, idx_v, rows_v, sem = rest[len(after):]
        wid = lax.axis_index("s") * SC_CORES + lax.axis_index("c")

        @pl.loop(0, per_w)
        def _(j):
            c = wid * per_w + j
            pltpu.sync_copy(pos_hbm.at[c], idx_v)
            pltpu.sync_copy(rows_hbm.at[pl.ds(c * ch, ch)], rows_v)
            copies = [pltpu.async_copy(rows_v, out_hbm.at[idx_v.at[kk]], sem) for kk in range(TOP_K)]
            for cp in copies:
                cp.wait()

    return k(rows, pos3, *after)


def _sc_combine(table, pos3, wtok, t):
    ch = SC_CHUNK
    sub = SC_COMBINE_TOKENS
    lanes = SC_LANES
    slabs = ROW_SLABS
    per_w = (t // ch) // SC_WORKERS
    subs_per_chunk = ch // sub
    n_steps = per_w * subs_per_chunk

    @functools.partial(
        pl.kernel, out_type=jax.ShapeDtypeStruct((t, 2 * slabs, 128), F32), mesh=_sc_mesh(),
        scratch_types=[pltpu.VMEM((per_w, TOP_K, ch), jnp.int32),
                       pltpu.VMEM((2, TOP_K, sub, slabs, 128), jnp.int32),
                       pltpu.VMEM((2, sub, 128), F32),
                       pltpu.VMEM((sub, 2 * slabs, 128), F32),
                       pltpu.SemaphoreType.DMA((2,))],
        compiler_params=pltpu.CompilerParams(needs_layout_passes=False))
    def k(tab_hbm, pos_hbm, w_hbm, out_hbm, idx_v, rows_v, w_v, out_v, sem):
        wid = lax.axis_index("s") * SC_CORES + lax.axis_index("c")
        for j in range(per_w):
            pltpu.sync_copy(pos_hbm.at[wid * per_w + j], idx_v.at[j])

        def first_token(step):
            return (wid * per_w + step // subs_per_chunk) * ch + (step % subs_per_chunk) * sub

        def copies(step, slot):
            j = step // subs_per_chunk
            s = step % subs_per_chunk
            idx = [idx_v.at[j, kk, pl.ds(s * sub, sub)] for kk in range(TOP_K)]
            return ([pltpu.make_async_copy(tab_hbm.at[idx[kk]], rows_v.at[slot, kk], sem.at[slot])
                     for kk in range(TOP_K)]
                    + [pltpu.make_async_copy(w_hbm.at[pl.ds(first_token(step), sub)], w_v.at[slot], sem.at[slot])])

        for cp in copies(0, 0):
            cp.start()

        @pl.loop(0, n_steps)
        def _(step):
            slot = step % 2

            @pl.when(step + 1 < n_steps)
            def _():
                for cp in copies(step + 1, 1 - slot):
                    cp.start()

            for cp in copies(step, slot):
                cp.wait()

            @pl.loop(0, sub)
            def _(tt):
                wk = [w_v[slot, tt, pl.ds(kk * lanes, lanes)] for kk in range(TOP_K)]
                for sl in range(slabs):
                    @plsc.parallel_loop(0, 128, step=lanes, unroll=4)
                    def _(off):
                        acc_lo = jnp.zeros((lanes,), F32)
                        acc_hi = jnp.zeros((lanes,), F32)
                        for kk in range(TOP_K):
                            word = rows_v[slot, kk, tt, sl, pl.ds(off, lanes)]
                            lo = plsc.bitcast(word << 16, F32)
                            hi = plsc.bitcast(word & jnp.int32(-65536), F32)
                            acc_lo = acc_lo + wk[kk] * lo
                            acc_hi = acc_hi + wk[kk] * hi
                        out_v[tt, sl, pl.ds(off, lanes)] = acc_lo
                        out_v[tt, slabs + sl, pl.ds(off, lanes)] = acc_hi

            pltpu.sync_copy(out_v, out_hbm.at[pl.ds(first_token(step), sub)])

    return k(table, pos3, wtok)


def _experts_kernel(texp_ref, nused_ref, tend_ref, xs_ref, weg_hbm, weu_hbm, wed_hbm, ys_ref,
                    wg_scr, wu_scr, wd_scr, wg_buf, wu_buf, wd_buf, sem, group_scr, *, expert_rows):
    step = pl.program_id(0)
    rows = expert_rows
    tiles_per_step = EXPERT_STEP_ROWS // expert_rows
    half = D_MODEL // 2
    n_used = nused_ref[0]

    def weight_copies(e, slot):
        return [pltpu.make_async_copy(weg_hbm.at[e], wg_buf.at[slot], sem.at[slot, 0]),
                pltpu.make_async_copy(weu_hbm.at[e], wu_buf.at[slot], sem.at[slot, 1]),
                pltpu.make_async_copy(wed_hbm.at[e], wd_buf.at[slot], sem.at[slot, 2])]

    def next_group(e):
        tile = tend_ref[e]
        return texp_ref[jnp.minimum(tile, n_used - 1)], tile < n_used

    def start_weights(e, slot, exists):
        @pl.when(exists)
        def _():
            for cp in weight_copies(e, slot):
                cp.start()

    @pl.when(step == 0)
    def _():
        group_scr[0] = 0
        e, exists = texp_ref[0], True
        for slot in range(WEIGHT_SLOTS - 1):
            start_weights(e, slot, exists)
            nxt, has_next = next_group(e)
            e, exists = nxt, exists & has_next

    def row_tile(tile, x_view, y_view):
        expert = texp_ref[tile]
        used = tile < n_used
        new_expert = (tile == 0) | (expert != texp_ref[jnp.maximum(tile - 1, 0)])

        @pl.when(used & new_expert)
        def _():
            group = group_scr[0]
            slot = group % WEIGHT_SLOTS
            ahead, exists = expert, True
            for _ in range(WEIGHT_SLOTS - 1):
                nxt, has_next = next_group(ahead)
                ahead, exists = nxt, exists & has_next
            start_weights(ahead, (group + WEIGHT_SLOTS - 1) % WEIGHT_SLOTS, exists)

            for cp in weight_copies(expert, slot):
                cp.wait()
            for scr, buf in ((wg_scr, wg_buf), (wu_scr, wu_buf), (wd_scr, wd_buf)):
                top, bottom = _unpack_pair(buf[slot])
                k_half = top.shape[0]
                scr[0:k_half, :] = top.astype(BF16)
                scr[k_half:2 * k_half, :] = bottom.astype(BF16)
            group_scr[0] = group + 1

        @pl.when(used)
        def _():
            lo, hi = _unpack_pair(_load_token_words(x_view, (), rows))
            lo = lo.astype(BF16)
            hi = hi.astype(BF16)
            g = _dot(lo, wg_scr[0:half, :]) + _dot(hi, wg_scr[half:D_MODEL, :])
            u = _dot(lo, wu_scr[0:half, :]) + _dot(hi, wu_scr[half:D_MODEL, :])
            y = _dot((_silu(g) * u).astype(BF16), wd_scr[...])
            _store_token_words(y_view, _pack_pair(y[:, :half], y[:, half:]), rows)

        @pl.when(jnp.logical_not(used) & (step == (n_used - 1) // tiles_per_step))
        def _():
            y_view[...] = jnp.zeros_like(y_view)

    for s in range(tiles_per_step):
        view = pl.ds(s * rows * ROW_SLABS, rows * ROW_SLABS)
        row_tile(step * tiles_per_step + s, xs_ref.at[view], ys_ref.at[view])


def _experts(texp, nused, tend, xs2d, weg, weu, wed, n_tiles, expert_rows):
    tiles_per_step = EXPERT_STEP_ROWS // expert_rows
    block = (EXPERT_STEP_ROWS * ROW_SLABS, 128)
    hbm = pl.BlockSpec(memory_space=pl.ANY)

    def block_idx(j, te, nu, tn):
        return (jnp.minimum(j, (nu[0] - 1) // tiles_per_step), 0)

    grid_spec = pltpu.PrefetchScalarGridSpec(
        num_scalar_prefetch=3,
        grid=(n_tiles // tiles_per_step,),
        in_specs=[pl.BlockSpec(block, block_idx), hbm, hbm, hbm],
        out_specs=pl.BlockSpec(block, block_idx),
        scratch_shapes=[pltpu.VMEM((D_MODEL, EXPERT_DIM), BF16),
                        pltpu.VMEM((D_MODEL, EXPERT_DIM), BF16),
                        pltpu.VMEM((EXPERT_DIM, D_MODEL), BF16),
                        pltpu.VMEM((WEIGHT_SLOTS,) + weg.shape[1:], jnp.int32),
                        pltpu.VMEM((WEIGHT_SLOTS,) + weu.shape[1:], jnp.int32),
                        pltpu.VMEM((WEIGHT_SLOTS,) + wed.shape[1:], jnp.int32),
                        pltpu.SemaphoreType.DMA((WEIGHT_SLOTS, 3)),
                        pltpu.SMEM((1,), jnp.int32)],
    )
    return pl.pallas_call(
        functools.partial(_experts_kernel, expert_rows=expert_rows),
        grid_spec=grid_spec,
        out_shape=jax.ShapeDtypeStruct(xs2d.shape, jnp.int32),
        compiler_params=pltpu.CompilerParams(dimension_semantics=("arbitrary",),
                                             vmem_limit_bytes=VMEM_LIMIT),
        name="experts",
    )(texp, nused, tend, xs2d, weg, weu, wed)


def _final_kernel(x_ref, routed_ref, mod_ref, g2_ref, wsg_ref, wsu_ref, wsd_ref, fng_ref, o_ref):
    tm = x_ref.shape[0]
    x = x_ref[...]
    hb = _rms_mod(x, g2_ref[...], mod_ref[0, 3:4, :], mod_ref[0, 4:5, :]).astype(BF16)
    shared = _dot((_silu(_dot(hb, wsg_ref[...])) * _dot(hb, wsu_ref[...])).astype(BF16), wsd_ref[...])
    n_slabs = 2 * ROW_SLABS
    routed = jnp.concatenate([routed_ref[pl.ds(s, tm, stride=n_slabs), :] for s in range(n_slabs)], axis=1)
    y = x + mod_ref[0, 5:6, :] * (routed + shared)
    ms = jnp.mean(y * y, axis=-1, keepdims=True)
    o_ref[...] = y * lax.rsqrt(ms + EPS) * fng_ref[...]


def _final(x1, routed2d, mod3, norm2_g, wsg, wsu, wsd, final_g, seq_len, mod_row_of_batch):
    t = x1.shape[0]
    tm = TM_FINAL

    def mod_idx(i):
        return (mod_row_of_batch((i * tm) // seq_len), 0, 0)

    def full(a):
        return pl.BlockSpec(a.shape, lambda i: (0,) * a.ndim)

    return pl.pallas_call(
        _final_kernel,
        grid=(t // tm,),
        in_specs=[pl.BlockSpec((tm, D_MODEL), lambda i: (i, 0)),
                  pl.BlockSpec((tm * 2 * ROW_SLABS, 128), lambda i: (i, 0)),
                  pl.BlockSpec((1, 6, D_MODEL), mod_idx),
                  full(norm2_g), full(wsg), full(wsu), full(wsd), full(final_g)],
        out_specs=pl.BlockSpec((tm, D_MODEL), lambda i: (i, 0)),
        out_shape=jax.ShapeDtypeStruct((t, D_MODEL), F32),
        compiler_params=pltpu.CompilerParams(dimension_semantics=("parallel",),
                                             vmem_limit_bytes=VMEM_LIMIT),
        name="final",
    )(x1, routed2d, mod3, norm2_g, wsg, wsu, wsd, final_g)


def _moe(x1, mod3, lw, seq_len, mod_row_of_batch):
    t = x1.shape[0]
    expert_rows = min(MAX_EXPERT_ROWS, TOP_K * t // N_EXPERTS // 2)
    n_tiles = TOP_K * t // expert_rows + N_EXPERTS
    n_tiles_pad = -(-n_tiles // 128) * 128
    hp2d, ek, rk, wtok, cnt = _router(x1, mod3, lw["norm2_g"], lw["w_router_t"], lw["router_bias"],
                                      seq_len, mod_row_of_batch)
    pos, texp, nused, tend = _plan(ek, rk, cnt, n_tiles_pad, expert_rows)
    pos3 = pos.reshape(TOP_K, t // SC_CHUNK, SC_CHUNK).transpose(1, 0, 2)
    xs = _sc_dispatch(hp2d.reshape(t, ROW_SLABS, 128), pos3, n_tiles * expert_rows,
                      after=(lw["weg"], lw["weu"], lw["wed"]))
    ys2d = _experts(texp.reshape(-1), nused.reshape(-1), tend[:, 0], xs.reshape(-1, 128),
                    lw["weg"], lw["weu"], lw["wed"], n_tiles, expert_rows)
    routed = _sc_combine(ys2d.reshape(-1, ROW_SLABS, 128), pos3, wtok, t)
    return _final(x1, routed.reshape(t * 2 * ROW_SLABS, 128), mod3, lw["norm2_g"],
                  lw["wsg"], lw["wsu"], lw["wsd"], lw["final_g"], seq_len, mod_row_of_batch)


def _dft_tables(seq_len):
    gd = FOURIER_GROUP_DIM
    kc = np.arange(gd)
    ang_c = ((kc[:, None] * kc[None, :]) % gd) * (2.0 * math.pi / gd)
    cs = np.concatenate([np.cos(ang_c), np.sin(ang_c)], axis=1) * (gd ** -0.5)
    kl = np.arange(seq_len)
    ang_l = ((kl[:, None] * kl[None, :]) % seq_len) * (2.0 * math.pi / seq_len)
    cls = np.concatenate([np.cos(ang_l), -np.sin(ang_l)], axis=1) * (seq_len ** -0.5)
    return jnp.asarray(cs.astype(np.float32), dtype=BF16), jnp.asarray(cls.astype(np.float32), dtype=BF16)


def _rope_tables(length):
    rows = length // GRID_W
    r = np.repeat(np.arange(rows, dtype=np.float32), GRID_W)
    col = np.tile(np.arange(GRID_W, dtype=np.float32), rows)
    nf = RET_HEAD_DIM // 4
    inv = (np.float32(ROPE_BASE) ** (-np.arange(nf, dtype=np.float32) / np.float32(nf))).astype(np.float32)
    ar = r[:, None] * inv[None]
    ac = col[:, None] * inv[None]
    ang = np.concatenate([ar, ar, ac, ac], axis=-1).astype(np.float64)
    sign = np.where((np.arange(RET_HEAD_DIM) & nf) == 0, -1.0, 1.0)
    return (jnp.asarray(np.cos(ang).astype(np.float32)),
            jnp.asarray((np.sin(ang) * sign[None, :]).astype(np.float32)))


def _trunk_path(x, mod3, mod_row_of_batch, s0f, s0b, rope, lw):
    batch, seq_len, _ = x.shape
    x2d = x.reshape(batch * seq_len, D_MODEL)
    uf, q, k, v, sg, gf, gr = _inproj(x2d, mod3, lw["norm1_g"], lw["w_in"], seq_len, mod_row_of_batch, rope)
    r, s_f, s_b = _retention(q, k, v, sg, lw["dec"], lw["gn_g"], s0f, s0b, batch, seq_len)
    cs, cls = _dft_tables(seq_len)
    x1 = _fnet_merge(uf, cs, cls, r, gf, gr, x2d, mod3, lw["w_four"], lw["w_ret"], lw["w_o"],
                     batch, seq_len, mod_row_of_batch)
    y = _moe(x1, mod3, lw, seq_len, mod_row_of_batch)
    return y.reshape(batch, seq_len, D_MODEL), s_f, s_b


def kernel(x_prompt, x_sample, state_ret_fwd, state_ret_bwd, c, c_ctx, w_ada, b_ada, norm1_g, norm2_g, w_in,
           ret_decay_fwd, ret_decay_bwd, ret_gn_g, w_four_out, w_ret_out, w_out, w_router, router_bias,
           w_exp_gate, w_exp_up, w_exp_down, w_shared_gate, w_shared_up, w_shared_down, final_norm_g):
    depth = w_ada.shape[0]
    assert depth == 1, "final norm is fused into the last layer's MoE kernel"
    n_ctx, n_lat = x_prompt.shape[0], x_sample.shape[0]
    cond = jnp.concatenate([c_ctx[None, :], c], axis=0)
    cond = jnp.pad(cond, ((0, (-cond.shape[0]) % 8), (0, 0)))
    rope = _rope_tables(x_sample.shape[1])
    zeros = jnp.zeros((n_ctx, N_RET_HEADS, RET_HEAD_DIM, RET_HEAD_DIM), F32)

    layer = 0
    mod = _ada(cond, w_ada[layer], b_ada[layer][None, :])
    mod3 = mod.reshape(mod.shape[0], 6, D_MODEL)
    dec = jnp.stack([ret_decay_fwd[layer], ret_decay_bwd[layer]], axis=1)
    lw = {
        "norm1_g": norm1_g[layer][None, :],
        "norm2_g": norm2_g[layer][None, :],
        "w_in": w_in[layer],
        "dec": jnp.broadcast_to(dec[:, :, None], (N_RET_HEADS, 2, RET_HEAD_DIM)).astype(F32),
        "gn_g": ret_gn_g[layer][None, :],
        "w_four": w_four_out[layer],
        "w_ret": w_ret_out[layer],
        "w_o": w_out[layer],
        "w_router_t": w_router[layer].T,
        "router_bias": router_bias[layer][:, None],
        "weg": _sc_pack_weight_halves(w_exp_gate[layer]),
        "weu": _sc_pack_weight_halves(w_exp_up[layer]),
        "wed": _sc_pack_weight_halves(w_exp_down[layer]),
        "wsg": w_shared_gate[layer].astype(BF16),
        "wsu": w_shared_up[layer].astype(BF16),
        "wsd": w_shared_down[layer].astype(BF16),
        "final_g": final_norm_g[None, :],
    }
    y_prompt, s_f, s_b = _trunk_path(x_prompt, mod3, lambda b: 0, zeros, zeros, None, lw)
    y_sample, _, _ = _trunk_path(x_sample, mod3, lambda b: 1 + b, state_ret_fwd[:, layer],
                                 state_ret_bwd[:, layer], rope, lw)
    return (y_prompt, y_sample, s_f[:, None], s_b[:, None])
```

```python
import functools
import math

import jax
import jax.numpy as jnp
import numpy as np
from jax import lax
from jax.experimental import pallas as pl
from jax.experimental.pallas import tpu as pltpu
from jax.experimental.pallas import tpu_sc as plsc

F32 = jnp.float32
BF16 = jnp.bfloat16

D_MODEL = 1024
GRID_W = 64
N_FOURIER_GROUPS = 8
FOURIER_GROUP_DIM = 128
N_RET_HEADS = 4
RET_HEAD_DIM = 128
RET_WIDTH = N_RET_HEADS * RET_HEAD_DIM
CHUNK = 128
N_EXPERTS = 64
N_EXPERT_GROUPS = 8
EXPERTS_PER_GROUP = N_EXPERTS // N_EXPERT_GROUPS
TOPK_GROUPS = 4
TOP_K = 8
EXPERT_DIM = 256
ROUTED_SCALE = 2.5
ROPE_BASE = 10000.0
EPS = 1e-6
Q_SCALE = RET_HEAD_DIM ** -0.5

_C_UF = (0, 1024)
_C_Q = (1024, 1536)
_C_K = (1536, 2048)
_C_V = (2048, 2560)
_C_G = (2560, 3072)
_C_GF = (3072, 4096)
_C_GR = (4096, 5120)

VMEM_LIMIT = 56 * 1024 * 1024

TM_INPROJ = 1024
TM_ROUTER = 1024
FNET_ROWS = 512
TM_FINAL = 1024
EXPERT_STEP_ROWS = 1024
MAX_EXPERT_ROWS = 512
WEIGHT_SLOTS = 3
ROW_SLABS = 4
SC_CORES = 2
SC_WORKERS = 32
SC_CHUNK = 128
SC_LANES = 16
SC_PACK_BLOCK_WORDS = 16384
SC_COMBINE_TOKENS = 8


def _silu(x):
    return x * jax.nn.sigmoid(x)


def _dot(a, b):
    return jnp.dot(a, b, preferred_element_type=F32)


def _rms_mod(x, g, shift, scale):
    ms = jnp.mean(x * x, axis=-1, keepdims=True)
    y = x * lax.rsqrt(ms + EPS) * g
    return y * (1.0 + scale) + shift


def _ada_kernel(cond_ref, w_ref, b_ref, o_ref):
    s = _silu(cond_ref[...]).astype(BF16)
    o_ref[...] = _dot(s, w_ref[...].astype(BF16)) + b_ref[...]


def _ada(cond, w_ada, b_ada):
    rows, n = cond.shape[0], w_ada.shape[1]
    tn = 1536
    return pl.pallas_call(
        _ada_kernel,
        grid=(n // tn,),
        in_specs=[pl.BlockSpec((rows, D_MODEL), lambda j: (0, 0)),
                  pl.BlockSpec((D_MODEL, tn), lambda j: (0, j)),
                  pl.BlockSpec((1, tn), lambda j: (0, j))],
        out_specs=pl.BlockSpec((rows, tn), lambda j: (0, j)),
        out_shape=jax.ShapeDtypeStruct((rows, n), F32),
        compiler_params=pltpu.CompilerParams(vmem_limit_bytes=VMEM_LIMIT),
        name="ada",
    )(cond, w_ada, b_ada)


def _rope_head(x, cos, sin_signed, first_half):
    partner = jnp.where(first_half, pltpu.roll(x, 96, 1), pltpu.roll(x, 32, 1))
    return x * cos + partner * sin_signed


def _inproj_kernel(*refs, use_rope):
    if use_rope:
        x_ref, mod_ref, g_ref, w_ref, cos_ref, sin_ref = refs[:6]
        outs = refs[6:]
    else:
        x_ref, mod_ref, g_ref, w_ref = refs[:4]
        outs = refs[4:]
    uf_o, q_o, k_o, v_o, sg_o, gf_o, gr_o = outs

    h = _rms_mod(x_ref[...], g_ref[...], mod_ref[0, 0:1, :], mod_ref[0, 1:2, :])
    hb = h.astype(BF16)

    def proj(cols):
        return _dot(hb, w_ref[:, cols[0]:cols[1]].astype(BF16))

    uf_o[...] = proj(_C_UF).astype(BF16)
    q = proj(_C_Q)
    k = proj(_C_K)
    if use_rope:
        cos = cos_ref[...]
        sin_signed = sin_ref[...]
        lane = lax.broadcasted_iota(jnp.int32, cos.shape, 1)
        first_half = (lane & 32) == 0
        for hd in range(N_RET_HEADS):
            sl = slice(hd * RET_HEAD_DIM, (hd + 1) * RET_HEAD_DIM)
            q_o[:, sl] = (_rope_head(q[:, sl], cos, sin_signed, first_half) * Q_SCALE).astype(BF16)
            k_o[:, sl] = _rope_head(k[:, sl], cos, sin_signed, first_half).astype(BF16)
    else:
        q_o[...] = (q * Q_SCALE).astype(BF16)
        k_o[...] = k.astype(BF16)
    v_o[...] = proj(_C_V).astype(BF16)
    sg_o[...] = _silu(proj(_C_G)).astype(BF16)
    gf_o[...] = jax.nn.sigmoid(proj(_C_GF)).astype(BF16)
    gr_o[...] = jax.nn.sigmoid(proj(_C_GR)).astype(BF16)


def _inproj(x2d, mod3, norm_g, w_in_f32, seq_len, mod_row_of_batch, rope):
    t = x2d.shape[0]
    tm = TM_INPROJ
    tiles_per_seq = max(seq_len // tm, 1)

    def mod_idx(i):
        return (mod_row_of_batch((i * tm) // seq_len), 0, 0)

    in_specs = [pl.BlockSpec((tm, D_MODEL), lambda i: (i, 0)),
                pl.BlockSpec((1, 6, D_MODEL), mod_idx),
                pl.BlockSpec((1, D_MODEL), lambda i: (0, 0)),
                pl.BlockSpec(w_in_f32.shape, lambda i: (0, 0), pipeline_mode=pl.Buffered(1))]
    args = [x2d, mod3, norm_g, w_in_f32]
    if rope is not None:
        in_specs += [pl.BlockSpec((tm, RET_HEAD_DIM), lambda i: (i % tiles_per_seq, 0))] * 2
        args += list(rope)
    widths = [1024, RET_WIDTH, RET_WIDTH, RET_WIDTH, RET_WIDTH, 1024, 1024]
    return pl.pallas_call(
        functools.partial(_inproj_kernel, use_rope=rope is not None),
        grid=(t // tm,),
        in_specs=in_specs,
        out_specs=[pl.BlockSpec((tm, w), lambda i: (i, 0)) for w in widths],
        out_shape=[jax.ShapeDtypeStruct((t, w), BF16) for w in widths],
        compiler_params=pltpu.CompilerParams(dimension_semantics=("parallel",),
                                             vmem_limit_bytes=VMEM_LIMIT),
        name="inproj",
    )(*args)


def _retention_kernel(q_ref, k_ref, v_ref, sg_ref, dec_ref, gn_ref, s0f_ref, s0b_ref,
                      r_ref, sfo_ref, sbo_ref, tab_scr, gc_scr):
    n_chunks = q_ref.shape[0] // CHUNK
    hd = RET_HEAD_DIM

    @pl.when(pl.program_id(0) == 0)
    def _():
        row = lax.broadcasted_iota(jnp.int32, (CHUNK, CHUNK), 0).astype(F32)
        col = lax.broadcasted_iota(jnp.int32, (CHUNK, CHUNK), 1).astype(F32)
        diff = row - col
        for h in range(N_RET_HEADS):
            dec = dec_ref[h]
            lg = jnp.minimum(dec, 0.0) - jnp.log1p(jnp.exp(-jnp.abs(dec)))
            lgf = lg[0:1, :]
            lgb = lg[1:2, :]
            tab_scr[h, 0] = jnp.exp(jnp.where(diff >= 0, lgf * diff, lgb * (-diff)))
            tab_scr[h, 1] = jnp.exp(lgf * (row + 1.0))
            tab_scr[h, 2] = jnp.exp(lgb * (CHUNK - row))
            tab_scr[h, 3] = jnp.exp(lgf * (CHUNK - 1.0 - col))
            tab_scr[h, 4] = jnp.exp(lgb * col)
            gc_scr[h] = jnp.exp(lg * CHUNK)

    def rows(n):
        return slice(n * CHUNK, (n + 1) * CHUNK)

    for h in range(N_RET_HEADS):
        cols = slice(h * hd, (h + 1) * hd)
        decay, qw_f, qw_b, kwt_f, kwt_b = (tab_scr[h, i] for i in range(5))
        gc = gc_scr[h]
        gc_f = gc[0:1, :]
        gc_b = gc[1:2, :]

        kv_f, kv_b = [], []
        for n in range(n_chunks):
            kt = k_ref[rows(n), cols].astype(F32).T
            vn = v_ref[rows(n), cols]
            kv_f.append(_dot((kt * kwt_f).astype(BF16), vn))
            kv_b.append(_dot((kt * kwt_b).astype(BF16), vn))

        s = s0f_ref[h]
        prev_f = []
        for n in range(n_chunks):
            prev_f.append(s.astype(BF16))
            s = gc_f * s + kv_f[n]
        sfo_ref[h] = s
        s = s0b_ref[h]
        prev_b = [None] * n_chunks
        for n in reversed(range(n_chunks)):
            prev_b[n] = s.astype(BF16)
            s = gc_b * s + kv_b[n]
        sbo_ref[h] = s

        gn = gn_ref[:, cols]
        for n in range(n_chunks):
            qn = q_ref[rows(n), cols]
            qf = qn.astype(F32)
            scores = lax.dot_general(qn, k_ref[rows(n), cols], (((1,), (1,)), ((), ())),
                                     preferred_element_type=F32)
            o = _dot((scores * decay).astype(BF16), v_ref[rows(n), cols])
            o = o + _dot((qf * qw_f).astype(BF16), prev_f[n])
            o = o + _dot((qf * qw_b).astype(BF16), prev_b[n])
            mu = jnp.mean(o, axis=-1, keepdims=True)
            d = o - mu
            var = jnp.mean(d * d, axis=-1, keepdims=True)
            on = d * lax.rsqrt(var + EPS) * gn
            r_ref[rows(n), cols] = (on * sg_ref[rows(n), cols].astype(F32)).astype(BF16)


def _retention(q, k, v, sg, dec, gn_g, s0f, s0b, batch, seq_len):
    hd = RET_HEAD_DIM
    tok_spec = pl.BlockSpec((seq_len, RET_WIDTH), lambda b: (b, 0))
    st_spec = pl.BlockSpec((None, N_RET_HEADS, hd, hd), lambda b: (b, 0, 0, 0))
    st_shape = jax.ShapeDtypeStruct((batch, N_RET_HEADS, hd, hd), F32)
    return pl.pallas_call(
        _retention_kernel,
        grid=(batch,),
        in_specs=[tok_spec, tok_spec, tok_spec, tok_spec,
                  pl.BlockSpec(dec.shape, lambda b: (0, 0, 0)),
                  pl.BlockSpec(gn_g.shape, lambda b: (0, 0)),
                  st_spec, st_spec],
        out_specs=[tok_spec, st_spec, st_spec],
        out_shape=[jax.ShapeDtypeStruct((batch * seq_len, RET_WIDTH), BF16), st_shape, st_shape],
        scratch_shapes=[pltpu.VMEM((N_RET_HEADS, 5, CHUNK, CHUNK), F32),
                        pltpu.VMEM((N_RET_HEADS, 2, hd), F32)],
        compiler_params=pltpu.CompilerParams(dimension_semantics=("arbitrary",),
                                             vmem_limit_bytes=VMEM_LIMIT),
        name="retention",
    )(q, k, v, sg, dec, gn_g, s0f, s0b)


def _fnet_merge_kernel(uf_ref, cs_ref, cls_ref, r_ref, gf_ref, gr_ref, x_ref, mod_ref, wf_ref, wr_ref, wo_ref,
                       o_ref, xcs_ref):
    seq_len = uf_ref.shape[0]
    gd = FOURIER_GROUP_DIM

    @pl.when(pl.program_id(1) == 0)
    def _():
        for g in range(N_FOURIER_GROUPS):
            x = _dot(uf_ref[:, g * gd:(g + 1) * gd], cs_ref[...])
            xcs_ref[0:seq_len, g * gd:(g + 1) * gd] = x[:, :gd].astype(BF16)
            xcs_ref[seq_len:2 * seq_len, g * gd:(g + 1) * gd] = x[:, gd:].astype(BF16)

    f_mix = _dot(cls_ref[...], xcs_ref[...]).astype(BF16)
    f_out = _dot(f_mix, wf_ref[...].astype(BF16))
    r_out = _dot(r_ref[...], wr_ref[...].astype(BF16))
    merged = gf_ref[...].astype(F32) * f_out + gr_ref[...].astype(F32) * r_out
    mix = _dot(merged.astype(BF16), wo_ref[...].astype(BF16))
    o_ref[...] = x_ref[...] + mod_ref[0, 2:3, :] * mix


def _fnet_merge(uf, cs, cls, r, gf, gr, x2d, mod3, w_four, w_ret, w_o, batch, seq_len, mod_row_of_batch):
    rb = min(FNET_ROWS, seq_len)
    nr = seq_len // rb

    def tok(w):
        return pl.BlockSpec((rb, w), lambda b, i: (b * nr + i, 0))

    def full(a):
        return pl.BlockSpec(a.shape, lambda b, i: (0, 0))

    def once(a):
        return pl.BlockSpec(a.shape, lambda b, i: (0, 0), pipeline_mode=pl.Buffered(1))

    return pl.pallas_call(
        _fnet_merge_kernel,
        grid=(batch, nr),
        in_specs=[pl.BlockSpec((seq_len, D_MODEL), lambda b, i: (b, 0)),
                  full(cs),
                  pl.BlockSpec((rb, 2 * seq_len), lambda b, i: (i, 0)),
                  tok(RET_WIDTH), tok(D_MODEL), tok(D_MODEL), tok(D_MODEL),
                  pl.BlockSpec((1, 6, D_MODEL), lambda b, i: (mod_row_of_batch(b), 0, 0)),
                  once(w_four), once(w_ret), once(w_o)],
        out_specs=tok(D_MODEL),
        out_shape=jax.ShapeDtypeStruct((batch * seq_len, D_MODEL), F32),
        scratch_shapes=[pltpu.VMEM((2 * seq_len, D_MODEL), BF16)],
        compiler_params=pltpu.CompilerParams(dimension_semantics=("parallel", "arbitrary"),
                                             vmem_limit_bytes=VMEM_LIMIT),
        name="fnet_merge",
    )(uf, cs, cls, r, gf, gr, x2d, mod3, w_four, w_ret, w_o)


def _pack_pair(lo_f32, hi_f32):
    lo = lax.bitcast_convert_type(lo_f32.astype(BF16).astype(F32), jnp.uint32)
    hi = lax.bitcast_convert_type(hi_f32.astype(BF16).astype(F32), jnp.uint32)
    return lax.bitcast_convert_type((lo >> 16) | hi, jnp.int32)


def _unpack_pair(words_i32):
    w = lax.bitcast_convert_type(words_i32, jnp.uint32)
    lo = lax.bitcast_convert_type(w << 16, F32)
    hi = lax.bitcast_convert_type(w & jnp.uint32(0xFFFF0000), F32)
    return lo, hi


def _load_token_words(ref, lead, n_tok):
    parts = []
    for s in range(ROW_SLABS):
        idx = (pl.ds(s, n_tok, stride=ROW_SLABS), slice(None))
        parts.append(ref[lead + idx] if lead else ref[idx])
    return jnp.concatenate(parts, axis=1)


def _store_token_words(ref, words, n_tok):
    for s in range(ROW_SLABS):
        ref[pl.ds(s, n_tok, stride=ROW_SLABS), :] = words[:, s * 128:(s + 1) * 128]


def _route(scores, biased):
    tokens = scores.shape[1]
    neg = -jnp.inf
    epg = EXPERTS_PER_GROUP
    iota_g = lax.broadcasted_iota(jnp.int32, (epg, tokens), 0).astype(F32)

    def pick_first_max(cur, iota, size):
        m = jnp.max(cur, axis=0, keepdims=True)
        idx = jnp.min(jnp.where(cur == m, iota, float(size)), axis=0, keepdims=True)
        return m, idx, iota == idx

    group_scores = []
    for g in range(N_EXPERT_GROUPS):
        vals = biased[g * epg:(g + 1) * epg, :]
        m1, _, hit = pick_first_max(vals, iota_g, epg)
        m2 = jnp.max(jnp.where(hit, neg, vals), axis=0, keepdims=True)
        group_scores.append(m1 + m2)
    cur = jnp.concatenate(group_scores, axis=0)
    group_sel = jnp.zeros_like(cur)
    for _ in range(TOPK_GROUPS):
        _, _, hit = pick_first_max(cur, iota_g, N_EXPERT_GROUPS)
        group_sel = jnp.where(hit, 1.0, group_sel)
        cur = jnp.where(hit, neg, cur)
    masked = jnp.concatenate(
        [jnp.where(group_sel[g:g + 1, :] > 0.0, biased[g * epg:(g + 1) * epg, :], neg)
         for g in range(N_EXPERT_GROUPS)], axis=0)
    iota_e = lax.broadcasted_iota(jnp.int32, masked.shape, 0).astype(F32)
    sel = jnp.zeros_like(masked)
    cur = masked
    picks = []
    for _ in range(TOP_K):
        _, idx, hit = pick_first_max(cur, iota_e, N_EXPERTS)
        picks.append(idx)
        sel = jnp.where(hit, 1.0, sel)
        cur = jnp.where(hit, neg, cur)
    w = scores * sel
    return w / jnp.sum(w, axis=0, keepdims=True) * ROUTED_SCALE, sel, picks


def _router_kernel(x_ref, mod_ref, g2_ref, wrt_ref, rb_ref, hp_ref, ek_ref, rk_ref, wt_ref, cnt_ref,
                   run_scr, earlier_scr):
    tm = x_ref.shape[0]

    @pl.when(pl.program_id(0) == 0)
    def _():
        run_scr[...] = jnp.zeros_like(run_scr)
        earlier = (lax.broadcasted_iota(jnp.int32, (tm, tm), 0) < lax.broadcasted_iota(jnp.int32, (tm, tm), 1))
        earlier_scr[...] = jnp.where(earlier, 1.0, 0.0).astype(BF16)

    h = _rms_mod(x_ref[...], g2_ref[...], mod_ref[0, 3:4, :], mod_ref[0, 4:5, :])
    half = D_MODEL // 2
    _store_token_words(hp_ref, _pack_pair(h[:, :half], h[:, half:]), tm)

    def split(a):
        hi = a.astype(BF16)
        return hi, (a - hi.astype(F32)).astype(BF16)

    def dot_nt(a, b):
        return lax.dot_general(a, b, (((1,), (1,)), ((), ())), preferred_element_type=F32)

    h_hi, h_lo = split(h)
    w_hi, w_lo = split(wrt_ref[...])
    logits_t = dot_nt(w_hi, h_hi) + (dot_nt(w_hi, h_lo) + dot_nt(w_lo, h_hi))
    scores = jax.nn.sigmoid(logits_t)
    comb_t, sel, picks = _route(scores, scores + rb_ref[...])

    rank_t = _dot(sel.astype(BF16), earlier_scr[...]) + run_scr[...]
    run_scr[...] += jnp.sum(sel, axis=1, keepdims=True)
    cnt_ref[...] = jnp.broadcast_to(run_scr[...], cnt_ref.shape)

    iota_e = lax.broadcasted_iota(jnp.int32, sel.shape, 0).astype(F32)
    ranks, weights = [], []
    for idx in picks:
        hit = iota_e == idx
        ranks.append(jnp.sum(jnp.where(hit, rank_t, 0.0), axis=0, keepdims=True))
        weights.append(jnp.sum(jnp.where(hit, comb_t, 0.0), axis=0, keepdims=True))
    ek_ref[...] = jnp.concatenate(picks, axis=0).astype(jnp.int32)
    rk_ref[...] = jnp.concatenate(ranks, axis=0).astype(jnp.int32)
    w_rep = jnp.concatenate([jnp.broadcast_to(w, (SC_LANES, tm)) for w in weights], axis=0)
    wt_ref[...] = w_rep.T


def _router(x1, mod3, norm2_g, w_router_t, router_bias, seq_len, mod_row_of_batch):
    t = x1.shape[0]
    tm = TM_ROUTER

    def mod_idx(i):
        return (mod_row_of_batch((i * tm) // seq_len), 0, 0)

    def full(a):
        return pl.BlockSpec(a.shape, lambda i: (0,) * a.ndim)

    return pl.pallas_call(
        _router_kernel,
        grid=(t // tm,),
        in_specs=[pl.BlockSpec((tm, D_MODEL), lambda i: (i, 0)),
                  pl.BlockSpec((1, 6, D_MODEL), mod_idx),
                  full(norm2_g), full(w_router_t), full(router_bias)],
        out_specs=[pl.BlockSpec((tm * ROW_SLABS, 128), lambda i: (i, 0)),
                   pl.BlockSpec((TOP_K, tm), lambda i: (0, i)),
                   pl.BlockSpec((TOP_K, tm), lambda i: (0, i)),
                   pl.BlockSpec((tm, 128), lambda i: (i, 0)),
                   pl.BlockSpec((N_EXPERTS, 128), lambda i: (0, 0))],
        out_shape=[jax.ShapeDtypeStruct((t * ROW_SLABS, 128), jnp.int32),
                   jax.ShapeDtypeStruct((TOP_K, t), jnp.int32),
                   jax.ShapeDtypeStruct((TOP_K, t), jnp.int32),
                   jax.ShapeDtypeStruct((t, 128), F32),
                   jax.ShapeDtypeStruct((N_EXPERTS, 128), F32)],
        scratch_shapes=[pltpu.VMEM((N_EXPERTS, 1), F32), pltpu.VMEM((tm, tm), BF16)],
        compiler_params=pltpu.CompilerParams(dimension_semantics=("arbitrary",),
                                             vmem_limit_bytes=VMEM_LIMIT),
        name="router",
    )(x1, mod3, norm2_g, w_router_t, router_bias)


def _plan_kernel(ek_ref, rk_ref, cnt_ref, pos_ref, texp_ref, nused_ref, tend_ref, *, expert_rows):
    rows = float(expert_rows)
    cnt = cnt_ref[:, 0:1]
    tiles = jnp.floor((cnt + (rows - 1.0)) / rows)
    before = (lax.broadcasted_iota(jnp.int32, (N_EXPERTS, N_EXPERTS), 1)
              < lax.broadcasted_iota(jnp.int32, (N_EXPERTS, N_EXPERTS), 0))
    tile_start = jnp.dot(jnp.where(before, 1.0, 0.0), jnp.broadcast_to(tiles, (N_EXPERTS, 128)),
                         precision=lax.Precision.HIGHEST, preferred_element_type=F32)[:, 0:1]
    tile_end = tile_start + tiles
    row_start = tile_start * rows

    ek = ek_ref[...]
    pos = rk_ref[...].astype(F32)
    tile_id = lax.broadcasted_iota(jnp.int32, texp_ref.shape, 1).astype(F32)
    texp = jnp.zeros(texp_ref.shape, F32)
    for e in range(N_EXPERTS):
        pos = pos + jnp.where(ek == e, row_start[e:e + 1, :], 0.0)
        texp = texp + jnp.where(tile_id >= tile_end[e:e + 1, :], 1.0, 0.0)
    pos_ref[...] = pos.astype(jnp.int32)
    texp_ref[...] = jnp.minimum(texp, N_EXPERTS - 1.0).astype(jnp.int32)
    nused_ref[...] = jnp.broadcast_to(tile_end[N_EXPERTS - 1:N_EXPERTS, :], nused_ref.shape).astype(jnp.int32)
    tend_ref[...] = jnp.broadcast_to(tile_end, tend_ref.shape).astype(jnp.int32)


def _plan(ek, rk, cnt, n_tiles_pad, expert_rows):
    t = ek.shape[1]

    def full(shape):
        return pl.BlockSpec(shape, lambda: (0,) * len(shape))

    return pl.pallas_call(
        functools.partial(_plan_kernel, expert_rows=expert_rows),
        in_specs=[full(ek.shape), full(rk.shape), full(cnt.shape)],
        out_specs=[full((TOP_K, t)), full((1, n_tiles_pad)), full((1, 128)), full((N_EXPERTS, 128))],
        out_shape=[jax.ShapeDtypeStruct((TOP_K, t), jnp.int32),
                   jax.ShapeDtypeStruct((1, n_tiles_pad), jnp.int32),
                   jax.ShapeDtypeStruct((1, 128), jnp.int32),
                   jax.ShapeDtypeStruct((N_EXPERTS, 128), jnp.int32)],
        compiler_params=pltpu.CompilerParams(vmem_limit_bytes=VMEM_LIMIT),
        name="plan",
    )(ek, rk, cnt)


def _sc_mesh():
    return plsc.VectorSubcoreMesh(core_axis_name="c", subcore_axis_name="s")


def _sc_pack_weight_halves(w):
    e, k, n = w.shape
    k_half = k // 2
    rb = SC_PACK_BLOCK_WORDS // n
    units_per_expert = k_half // rb
    per_w = (e * units_per_expert) // SC_WORKERS
    lanes = SC_LANES

    @functools.partial(
        pl.kernel, out_type=jax.ShapeDtypeStruct((e * k_half, n), jnp.int32), mesh=_sc_mesh(),
        scratch_types=[pltpu.VMEM((rb, n), F32), pltpu.VMEM((rb, n), F32), pltpu.VMEM((rb, n), jnp.int32)],
        compiler_params=pltpu.CompilerParams(needs_layout_passes=False))
    def kern(w_hbm, out_hbm, a_v, b_v, o_v):
        wid = lax.axis_index("s") * SC_CORES + lax.axis_index("c")

        @pl.loop(0, per_w)
        def _(j):
            unit = wid * per_w + j
            expert = unit // units_per_expert
            blk = unit % units_per_expert
            row_a = expert * k + blk * rb
            pltpu.sync_copy(w_hbm.at[pl.ds(row_a, rb)], a_v)
            pltpu.sync_copy(w_hbm.at[pl.ds(row_a + k_half, rb)], b_v)

            @pl.loop(0, rb)
            def _(r):
                @plsc.parallel_loop(0, n, step=lanes, unroll=4)
                def _(c):
                    both = plsc.pack(a_v[r, pl.ds(c, lanes)], b_v[r, pl.ds(c, lanes)],
                                     format=plsc.PackFormat.INTERLEAVED)
                    o_v[r, pl.ds(c, lanes)] = plsc.bitcast(both, jnp.int32)

            pltpu.sync_copy(o_v, out_hbm.at[pl.ds(expert * k_half + blk * rb, rb)])

    return kern(w.reshape(e * k, n)).reshape(e, k_half, n)


def _sc_dispatch(rows, pos3, n_out, after=()):
    t = rows.shape[0]
    ch = SC_CHUNK
    per_w = (t // ch) // SC_WORKERS

    @functools.partial(
        pl.kernel, out_type=jax.ShapeDtypeStruct((n_out,) + rows.shape[1:], jnp.int32), mesh=_sc_mesh(),
        scratch_types=[pltpu.VMEM((TOP_K, ch), jnp.int32), pltpu.VMEM((ch,) + rows.shape[1:], jnp.int32),
                       pltpu.SemaphoreType.DMA])
    def k(rows_hbm, pos_hbm, *rest):
        out_hbm, idx_v, rows_v, sem = rest[len(after):]
        wid = lax.axis_index("s") * SC_CORES + lax.axis_index("c")

        @pl.loop(0, per_w)
        def _(j):
            c = wid * per_w + j
            pltpu.sync_copy(pos_hbm.at[c], idx_v)
            pltpu.sync_copy(rows_hbm.at[pl.ds(c * ch, ch)], rows_v)
            copies = [pltpu.async_copy(rows_v, out_hbm.at[idx_v.at[kk]], sem) for kk in range(TOP_K)]
            for cp in copies:
                cp.wait()

    return k(rows, pos3, *after)


def _sc_combine(table, pos3, wtok, t):
    ch = SC_CHUNK
    sub = SC_COMBINE_TOKENS
    lanes = SC_LANES
    slabs = ROW_SLABS
    per_w = (t // ch) // SC_WORKERS
    subs_per_chunk = ch // sub
    n_steps = per_w * subs_per_chunk

    @functools.partial(
        pl.kernel, out_type=jax.ShapeDtypeStruct((t, slabs, 128), jnp.int32), mesh=_sc_mesh(),
        scratch_types=[pltpu.VMEM((per_w, TOP_K, ch), jnp.int32),
                       pltpu.VMEM((2, TOP_K, sub, slabs, 128), jnp.int32),
                       pltpu.VMEM((2, sub, 128), F32),
                       pltpu.VMEM((sub, slabs, 128), jnp.int32),
                       pltpu.SemaphoreType.DMA((2,))],
        compiler_params=pltpu.CompilerParams(needs_layout_passes=False))
    def k(tab_hbm, pos_hbm, w_hbm, out_hbm, idx_v, rows_v, w_v, out_v, sem):
        wid = lax.axis_index("s") * SC_CORES + lax.axis_index("c")
        for j in range(per_w):
            pltpu.sync_copy(pos_hbm.at[wid * per_w + j], idx_v.at[j])

        def first_token(step):
            return (wid * per_w + step // subs_per_chunk) * ch + (step % subs_per_chunk) * sub

        def copies(step, slot):
            j = step // subs_per_chunk
            s = step % subs_per_chunk
            idx = [idx_v.at[j, kk, pl.ds(s * sub, sub)] for kk in range(TOP_K)]
            return ([pltpu.make_async_copy(tab_hbm.at[idx[kk]], rows_v.at[slot, kk], sem.at[slot])
                     for kk in range(TOP_K)]
                    + [pltpu.make_async_copy(w_hbm.at[pl.ds(first_token(step), sub)], w_v.at[slot], sem.at[slot])])

        for cp in copies(0, 0):
            cp.start()

        @pl.loop(0, n_steps)
        def _(step):
            slot = step % 2

            @pl.when(step + 1 < n_steps)
            def _():
                for cp in copies(step + 1, 1 - slot):
                    cp.start()

            for cp in copies(step, slot):
                cp.wait()

            @pl.loop(0, sub)
            def _(tt):
                wk = [w_v[slot, tt, pl.ds(kk * lanes, lanes)] for kk in range(TOP_K)]
                for sl in range(slabs):
                    @plsc.parallel_loop(0, 128, step=lanes, unroll=4)
                    def _(off):
                        acc_lo = jnp.zeros((lanes,), F32)
                        acc_hi = jnp.zeros((lanes,), F32)
                        for kk in range(TOP_K):
                            word = rows_v[slot, kk, tt, sl, pl.ds(off, lanes)]
                            lo = plsc.bitcast(word << 16, F32)
                            hi = plsc.bitcast(word & jnp.int32(-65536), F32)
                            acc_lo = acc_lo + wk[kk] * lo
                            acc_hi = acc_hi + wk[kk] * hi
                        both = plsc.pack(acc_lo, acc_hi, format=plsc.PackFormat.INTERLEAVED)
                        out_v[tt, sl, pl.ds(off, lanes)] = plsc.bitcast(both, jnp.int32)

            pltpu.sync_copy(out_v, out_hbm.at[pl.ds(first_token(step), sub)])

    return k(table, pos3, wtok)


def _experts_kernel(texp_ref, nused_ref, tend_ref, xs_ref, weg_hbm, weu_hbm, wed_hbm, ys_ref,
                    wg_scr, wu_scr, wd_scr, wg_buf, wu_buf, wd_buf, sem, group_scr, *, expert_rows):
    step = pl.program_id(0)
    rows = expert_rows
    tiles_per_step = EXPERT_STEP_ROWS // expert_rows
    half = D_MODEL // 2
    n_used = nused_ref[0]

    def weight_copies(e, slot):
        return [pltpu.make_async_copy(weg_hbm.at[e], wg_buf.at[slot], sem.at[slot, 0]),
                pltpu.make_async_copy(weu_hbm.at[e], wu_buf.at[slot], sem.at[slot, 1]),
                pltpu.make_async_copy(wed_hbm.at[e], wd_buf.at[slot], sem.at[slot, 2])]

    def next_group(e):
        tile = tend_ref[e]
        return texp_ref[jnp.minimum(tile, n_used - 1)], tile < n_used

    def start_weights(e, slot, exists):
        @pl.when(exists)
        def _():
            for cp in weight_copies(e, slot):
                cp.start()

    @pl.when(step == 0)
    def _():
        group_scr[0] = 0
        e, exists = texp_ref[0], True
        for slot in range(WEIGHT_SLOTS - 1):
            start_weights(e, slot, exists)
            nxt, has_next = next_group(e)
            e, exists = nxt, exists & has_next

    def row_tile(tile, x_view, y_view):
        expert = texp_ref[tile]
        used = tile < n_used
        new_expert = (tile == 0) | (expert != texp_ref[jnp.maximum(tile - 1, 0)])

        @pl.when(used & new_expert)
        def _():
            group = group_scr[0]
            slot = group % WEIGHT_SLOTS
            ahead, exists = expert, True
            for _ in range(WEIGHT_SLOTS - 1):
                nxt, has_next = next_group(ahead)
                ahead, exists = nxt, exists & has_next
            start_weights(ahead, (group + WEIGHT_SLOTS - 1) % WEIGHT_SLOTS, exists)

            for cp in weight_copies(expert, slot):
                cp.wait()
            for scr, buf in ((wg_scr, wg_buf), (wu_scr, wu_buf), (wd_scr, wd_buf)):
                top, bottom = _unpack_pair(buf[slot])
                k_half = top.shape[0]
                scr[0:k_half, :] = top.astype(BF16)
                scr[k_half:2 * k_half, :] = bottom.astype(BF16)
            group_scr[0] = group + 1

        @pl.when(used)
        def _():
            lo, hi = _unpack_pair(_load_token_words(x_view, (), rows))
            lo = lo.astype(BF16)
            hi = hi.astype(BF16)
            g = _dot(lo, wg_scr[0:half, :]) + _dot(hi, wg_scr[half:D_MODEL, :])
            u = _dot(lo, wu_scr[0:half, :]) + _dot(hi, wu_scr[half:D_MODEL, :])
            y = _dot((_silu(g) * u).astype(BF16), wd_scr[...])
            _store_token_words(y_view, _pack_pair(y[:, :half], y[:, half:]), rows)

        @pl.when(jnp.logical_not(used) & (step == (n_used - 1) // tiles_per_step))
        def _():
            y_view[...] = jnp.zeros_like(y_view)

    for s in range(tiles_per_step):
        view = pl.ds(s * rows * ROW_SLABS, rows * ROW_SLABS)
        row_tile(step * tiles_per_step + s, xs_ref.at[view], ys_ref.at[view])


def _experts(texp, nused, tend, xs2d, weg, weu, wed, n_tiles, expert_rows):
    tiles_per_step = EXPERT_STEP_ROWS // expert_rows
    block = (EXPERT_STEP_ROWS * ROW_SLABS, 128)
    hbm = pl.BlockSpec(memory_space=pl.ANY)

    def block_idx(j, te, nu, tn):
        return (jnp.minimum(j, (nu[0] - 1) // tiles_per_step), 0)

    grid_spec = pltpu.PrefetchScalarGridSpec(
        num_scalar_prefetch=3,
        grid=(n_tiles // tiles_per_step,),
        in_specs=[pl.BlockSpec(block, block_idx), hbm, hbm, hbm],
        out_specs=pl.BlockSpec(block, block_idx),
        scratch_shapes=[pltpu.VMEM((D_MODEL, EXPERT_DIM), BF16),
                        pltpu.VMEM((D_MODEL, EXPERT_DIM), BF16),
                        pltpu.VMEM((EXPERT_DIM, D_MODEL), BF16),
                        pltpu.VMEM((WEIGHT_SLOTS,) + weg.shape[1:], jnp.int32),
                        pltpu.VMEM((WEIGHT_SLOTS,) + weu.shape[1:], jnp.int32),
                        pltpu.VMEM((WEIGHT_SLOTS,) + wed.shape[1:], jnp.int32),
                        pltpu.SemaphoreType.DMA((WEIGHT_SLOTS, 3)),
                        pltpu.SMEM((1,), jnp.int32)],
    )
    return pl.pallas_call(
        functools.partial(_experts_kernel, expert_rows=expert_rows),
        grid_spec=grid_spec,
        out_shape=jax.ShapeDtypeStruct(xs2d.shape, jnp.int32),
        compiler_params=pltpu.CompilerParams(dimension_semantics=("arbitrary",),
                                             vmem_limit_bytes=VMEM_LIMIT),
        name="experts",
    )(texp, nused, tend, xs2d, weg, weu, wed)


def _final_kernel(x_ref, routed_ref, mod_ref, g2_ref, wsg_ref, wsu_ref, wsd_ref, fng_ref, o_ref):
    tm = x_ref.shape[0]
    x = x_ref[...]
    hb = _rms_mod(x, g2_ref[...], mod_ref[0, 3:4, :], mod_ref[0, 4:5, :]).astype(BF16)
    shared = _dot((_silu(_dot(hb, wsg_ref[...])) * _dot(hb, wsu_ref[...])).astype(BF16), wsd_ref[...])
    routed = jnp.concatenate(_unpack_pair(_load_token_words(routed_ref, (), tm)), axis=1)
    y = x + mod_ref[0, 5:6, :] * (routed + shared)
    ms = jnp.mean(y * y, axis=-1, keepdims=True)
    o_ref[...] = y * lax.rsqrt(ms + EPS) * fng_ref[...]


def _final(x1, routed2d, mod3, norm2_g, wsg, wsu, wsd, final_g, seq_len, mod_row_of_batch):
    t = x1.shape[0]
    tm = TM_FINAL

    def mod_idx(i):
        return (mod_row_of_batch((i * tm) // seq_len), 0, 0)

    def full(a):
        return pl.BlockSpec(a.shape, lambda i: (0,) * a.ndim)

    return pl.pallas_call(
        _final_kernel,
        grid=(t // tm,),
        in_specs=[pl.BlockSpec((tm, D_MODEL), lambda i: (i, 0)),
                  pl.BlockSpec((tm * ROW_SLABS, 128), lambda i: (i, 0)),
                  pl.BlockSpec((1, 6, D_MODEL), mod_idx),
                  full(norm2_g), full(wsg), full(wsu), full(wsd), full(final_g)],
        out_specs=pl.BlockSpec((tm, D_MODEL), lambda i: (i, 0)),
        out_shape=jax.ShapeDtypeStruct((t, D_MODEL), F32),
        compiler_params=pltpu.CompilerParams(dimension_semantics=("parallel",),
                                             vmem_limit_bytes=VMEM_LIMIT),
        name="final",
    )(x1, routed2d, mod3, norm2_g, wsg, wsu, wsd, final_g)


def _moe(x1, mod3, lw, seq_len, mod_row_of_batch):
    t = x1.shape[0]
    expert_rows = min(MAX_EXPERT_ROWS, TOP_K * t // N_EXPERTS // 2)
    n_tiles = TOP_K * t // expert_rows + N_EXPERTS
    n_tiles_pad = -(-n_tiles // 128) * 128
    hp2d, ek, rk, wtok, cnt = _router(x1, mod3, lw["norm2_g"], lw["w_router_t"], lw["router_bias"],
                                      seq_len, mod_row_of_batch)
    pos, texp, nused, tend = _plan(ek, rk, cnt, n_tiles_pad, expert_rows)
    pos3 = pos.reshape(TOP_K, t // SC_CHUNK, SC_CHUNK).transpose(1, 0, 2)
    xs = _sc_dispatch(hp2d.reshape(t, ROW_SLABS, 128), pos3, n_tiles * expert_rows,
                      after=(lw["weg"], lw["weu"], lw["wed"]))
    ys2d = _experts(texp.reshape(-1), nused.reshape(-1), tend[:, 0], xs.reshape(-1, 128),
                    lw["weg"], lw["weu"], lw["wed"], n_tiles, expert_rows)
    routed = _sc_combine(ys2d.reshape(-1, ROW_SLABS, 128), pos3, wtok, t)
    return _final(x1, routed.reshape(t * ROW_SLABS, 128), mod3, lw["norm2_g"],
                  lw["wsg"], lw["wsu"], lw["wsd"], lw["final_g"], seq_len, mod_row_of_batch)


def _dft_tables(seq_len):
    gd = FOURIER_GROUP_DIM
    kc = np.arange(gd)
    ang_c = ((kc[:, None] * kc[None, :]) % gd) * (2.0 * math.pi / gd)
    cs = np.concatenate([np.cos(ang_c), np.sin(ang_c)], axis=1) * (gd ** -0.5)
    kl = np.arange(seq_len)
    ang_l = ((kl[:, None] * kl[None, :]) % seq_len) * (2.0 * math.pi / seq_len)
    cls = np.concatenate([np.cos(ang_l), -np.sin(ang_l)], axis=1) * (seq_len ** -0.5)
    return jnp.asarray(cs.astype(np.float32), dtype=BF16), jnp.asarray(cls.astype(np.float32), dtype=BF16)


def _rope_tables(length):
    rows = length // GRID_W
    r = np.repeat(np.arange(rows, dtype=np.float32), GRID_W)
    col = np.tile(np.arange(GRID_W, dtype=np.float32), rows)
    nf = RET_HEAD_DIM // 4
    inv = (np.float32(ROPE_BASE) ** (-np.arange(nf, dtype=np.float32) / np.float32(nf))).astype(np.float32)
    ar = r[:, None] * inv[None]
    ac = col[:, None] * inv[None]
    ang = np.concatenate([ar, ar, ac, ac], axis=-1).astype(np.float64)
    sign = np.where((np.arange(RET_HEAD_DIM) & nf) == 0, -1.0, 1.0)
    return (jnp.asarray(np.cos(ang).astype(np.float32)),
            jnp.asarray((np.sin(ang) * sign[None, :]).astype(np.float32)))


def _trunk_path(x, mod3, mod_row_of_batch, s0f, s0b, rope, lw):
    batch, seq_len, _ = x.shape
    x2d = x.reshape(batch * seq_len, D_MODEL)
    uf, q, k, v, sg, gf, gr = _inproj(x2d, mod3, lw["norm1_g"], lw["w_in"], seq_len, mod_row_of_batch, rope)
    r, s_f, s_b = _retention(q, k, v, sg, lw["dec"], lw["gn_g"], s0f, s0b, batch, seq_len)
    cs, cls = _dft_tables(seq_len)
    x1 = _fnet_merge(uf, cs, cls, r, gf, gr, x2d, mod3, lw["w_four"], lw["w_ret"], lw["w_o"],
                     batch, seq_len, mod_row_of_batch)
    y = _moe(x1, mod3, lw, seq_len, mod_row_of_batch)
    return y.reshape(batch, seq_len, D_MODEL), s_f, s_b


def kernel(x_prompt, x_sample, state_ret_fwd, state_ret_bwd, c, c_ctx, w_ada, b_ada, norm1_g, norm2_g, w_in,
           ret_decay_fwd, ret_decay_bwd, ret_gn_g, w_four_out, w_ret_out, w_out, w_router, router_bias,
           w_exp_gate, w_exp_up, w_exp_down, w_shared_gate, w_shared_up, w_shared_down, final_norm_g):
    depth = w_ada.shape[0]
    assert depth == 1, "final norm is fused into the last layer's MoE kernel"
    n_ctx, n_lat = x_prompt.shape[0], x_sample.shape[0]
    cond = jnp.concatenate([c_ctx[None, :], c], axis=0)
    cond = jnp.pad(cond, ((0, (-cond.shape[0]) % 8), (0, 0)))
    rope = _rope_tables(x_sample.shape[1])
    zeros = jnp.zeros((n_ctx, N_RET_HEADS, RET_HEAD_DIM, RET_HEAD_DIM), F32)

    layer = 0
    mod = _ada(cond, w_ada[layer], b_ada[layer][None, :])
    mod3 = mod.reshape(mod.shape[0], 6, D_MODEL)
    dec = jnp.stack([ret_decay_fwd[layer], ret_decay_bwd[layer]], axis=1)
    lw = {
        "norm1_g": norm1_g[layer][None, :],
        "norm2_g": norm2_g[layer][None, :],
        "w_in": w_in[layer],
        "dec": jnp.broadcast_to(dec[:, :, None], (N_RET_HEADS, 2, RET_HEAD_DIM)).astype(F32),
        "gn_g": ret_gn_g[layer][None, :],
        "w_four": w_four_out[layer],
        "w_ret": w_ret_out[layer],
        "w_o": w_out[layer],
        "w_router_t": w_router[layer].T,
        "router_bias": router_bias[layer][:, None],
        "weg": _sc_pack_weight_halves(w_exp_gate[layer]),
        "weu": _sc_pack_weight_halves(w_exp_up[layer]),
        "wed": _sc_pack_weight_halves(w_exp_down[layer]),
        "wsg": w_shared_gate[layer].astype(BF16),
        "wsu": w_shared_up[layer].astype(BF16),
        "wsd": w_shared_down[layer].astype(BF16),
        "final_g": final_norm_g[None, :],
    }
    y_prompt, s_f, s_b = _trunk_path(x_prompt, mod3, lambda b: 0, zeros, zeros, None, lw)
    y_sample, _, _ = _trunk_path(x_sample, mod3, lambda b: 1 + b, state_ret_fwd[:, layer],
                                 state_ret_bwd[:, layer], rope, lw)
    return (y_prompt, y_sample, s_f[:, None], s_b[:, None])
```

```python
import functools
import math

import jax
import jax.numpy as jnp
import numpy as np
from jax import lax
from jax.experimental import pallas as pl
from jax.experimental.pallas import tpu as pltpu
from jax.experimental.pallas import tpu_sc as plsc

F32 = jnp.float32
BF16 = jnp.bfloat16

D_MODEL = 1024
GRID_W = 64
N_FOURIER_GROUPS = 8
FOURIER_GROUP_DIM = 128
N_RET_HEADS = 4
RET_HEAD_DIM = 128
RET_WIDTH = N_RET_HEADS * RET_HEAD_DIM
CHUNK = 128
N_EXPERTS = 64
N_EXPERT_GROUPS = 8
EXPERTS_PER_GROUP = N_EXPERTS // N_EXPERT_GROUPS
TOPK_GROUPS = 4
TOP_K = 8
EXPERT_DIM = 256
ROUTED_SCALE = 2.5
ROPE_BASE = 10000.0
EPS = 1e-6
Q_SCALE = RET_HEAD_DIM ** -0.5

_C_UF = (0, 1024)
_C_Q = (1024, 1536)
_C_K = (1536, 2048)
_C_V = (2048, 2560)
_C_G = (2560, 3072)
_C_GF = (3072, 4096)
_C_GR = (4096, 5120)

VMEM_LIMIT = 56 * 1024 * 1024

TM_INPROJ = 1024
TM_ROUTER = 1024
FNET_ROWS = 512
TM_FINAL = 1024
EXPERT_STEP_ROWS = 1024
MAX_EXPERT_ROWS = 512
WEIGHT_SLOTS = 3
ROW_SLABS = 4
SC_CORES = 2
SC_WORKERS = 32
SC_CHUNK = 128
SC_LANES = 16
SC_PACK_BLOCK_WORDS = 16384
SC_COMBINE_TOKENS = 8


def _silu(x):
    return x * jax.nn.sigmoid(x)


def _dot(a, b):
    return jnp.dot(a, b, preferred_element_type=F32)


def _rms_mod(x, g, shift, scale):
    ms = jnp.mean(x * x, axis=-1, keepdims=True)
    y = x * lax.rsqrt(ms + EPS) * g
    return y * (1.0 + scale) + shift


def _ada_kernel(cond_ref, w_ref, b_ref, o_ref):
    s = _silu(cond_ref[...]).astype(BF16)
    o_ref[...] = _dot(s, w_ref[...].astype(BF16)) + b_ref[...]


def _ada(cond, w_ada, b_ada):
    rows, n = cond.shape[0], w_ada.shape[1]
    tn = 1536
    return pl.pallas_call(
        _ada_kernel,
        grid=(n // tn,),
        in_specs=[pl.BlockSpec((rows, D_MODEL), lambda j: (0, 0)),
                  pl.BlockSpec((D_MODEL, tn), lambda j: (0, j)),
                  pl.BlockSpec((1, tn), lambda j: (0, j))],
        out_specs=pl.BlockSpec((rows, tn), lambda j: (0, j)),
        out_shape=jax.ShapeDtypeStruct((rows, n), F32),
        compiler_params=pltpu.CompilerParams(vmem_limit_bytes=VMEM_LIMIT),
        name="ada",
    )(cond, w_ada, b_ada)


def _rope_head(x, cos, sin_signed, first_half):
    partner = jnp.where(first_half, pltpu.roll(x, 96, 1), pltpu.roll(x, 32, 1))
    return x * cos + partner * sin_signed


def _inproj_kernel(*refs, use_rope):
    if use_rope:
        x_ref, mod_ref, g_ref, w_ref, cos_ref, sin_ref = refs[:6]
        outs = refs[6:]
    else:
        x_ref, mod_ref, g_ref, w_ref = refs[:4]
        outs = refs[4:]
    uf_o, q_o, k_o, v_o, sg_o, gf_o, gr_o = outs

    h = _rms_mod(x_ref[...], g_ref[...], mod_ref[0, 0:1, :], mod_ref[0, 1:2, :])
    hb = h.astype(BF16)

    def proj(cols):
        return _dot(hb, w_ref[:, cols[0]:cols[1]].astype(BF16))

    uf_o[...] = proj(_C_UF).astype(BF16)
    q = proj(_C_Q)
    k = proj(_C_K)
    if use_rope:
        cos = cos_ref[...]
        sin_signed = sin_ref[...]
        lane = lax.broadcasted_iota(jnp.int32, cos.shape, 1)
        first_half = (lane & 32) == 0
        for hd in range(N_RET_HEADS):
            sl = slice(hd * RET_HEAD_DIM, (hd + 1) * RET_HEAD_DIM)
            q_o[:, sl] = (_rope_head(q[:, sl], cos, sin_signed, first_half) * Q_SCALE).astype(BF16)
            k_o[:, sl] = _rope_head(k[:, sl], cos, sin_signed, first_half).astype(BF16)
    else:
        q_o[...] = (q * Q_SCALE).astype(BF16)
        k_o[...] = k.astype(BF16)
    v_o[...] = proj(_C_V).astype(BF16)
    sg_o[...] = _silu(proj(_C_G)).astype(BF16)
    gf_o[...] = jax.nn.sigmoid(proj(_C_GF)).astype(BF16)
    gr_o[...] = jax.nn.sigmoid(proj(_C_GR)).astype(BF16)


def _inproj(x2d, mod3, norm_g, w_in_f32, seq_len, mod_row_of_batch, rope):
    t = x2d.shape[0]
    tm = TM_INPROJ
    tiles_per_seq = max(seq_len // tm, 1)

    def mod_idx(i):
        return (mod_row_of_batch((i * tm) // seq_len), 0, 0)

    in_specs = [pl.BlockSpec((tm, D_MODEL), lambda i: (i, 0)),
                pl.BlockSpec((1, 6, D_MODEL), mod_idx),
                pl.BlockSpec((1, D_MODEL), lambda i: (0, 0)),
                pl.BlockSpec(w_in_f32.shape, lambda i: (0, 0), pipeline_mode=pl.Buffered(1))]
    args = [x2d, mod3, norm_g, w_in_f32]
    if rope is not None:
        in_specs += [pl.BlockSpec((tm, RET_HEAD_DIM), lambda i: (i % tiles_per_seq, 0))] * 2
        args += list(rope)
    widths = [1024, RET_WIDTH, RET_WIDTH, RET_WIDTH, RET_WIDTH, 1024, 1024]
    return pl.pallas_call(
        functools.partial(_inproj_kernel, use_rope=rope is not None),
        grid=(t // tm,),
        in_specs=in_specs,
        out_specs=[pl.BlockSpec((tm, w), lambda i: (i, 0)) for w in widths],
        out_shape=[jax.ShapeDtypeStruct((t, w), BF16) for w in widths],
        compiler_params=pltpu.CompilerParams(dimension_semantics=("parallel",),
                                             vmem_limit_bytes=VMEM_LIMIT),
        name="inproj",
    )(*args)


def _retention_kernel(q_ref, k_ref, v_ref, sg_ref, dec_ref, gn_ref, s0f_ref, s0b_ref,
                      r_ref, sfo_ref, sbo_ref, tab_scr, gc_scr):
    n_chunks = q_ref.shape[0] // CHUNK
    hd = RET_HEAD_DIM

    @pl.when(pl.program_id(0) == 0)
    def _():
        row = lax.broadcasted_iota(jnp.int32, (CHUNK, CHUNK), 0).astype(F32)
        col = lax.broadcasted_iota(jnp.int32, (CHUNK, CHUNK), 1).astype(F32)
        diff = row - col
        for h in range(N_RET_HEADS):
            dec = dec_ref[h]
            lg = jnp.minimum(dec, 0.0) - jnp.log1p(jnp.exp(-jnp.abs(dec)))
            lgf = lg[0:1, :]
            lgb = lg[1:2, :]
            tab_scr[h, 0] = jnp.exp(jnp.where(diff >= 0, lgf * diff, lgb * (-diff)))
            tab_scr[h, 1] = jnp.exp(lgf * (row + 1.0))
            tab_scr[h, 2] = jnp.exp(lgb * (CHUNK - row))
            tab_scr[h, 3] = jnp.exp(lgf * (CHUNK - 1.0 - col))
            tab_scr[h, 4] = jnp.exp(lgb * col)
            gc_scr[h] = jnp.exp(lg * CHUNK)

    def rows(n):
        return slice(n * CHUNK, (n + 1) * CHUNK)

    for h in range(N_RET_HEADS):
        cols = slice(h * hd, (h + 1) * hd)
        decay, qw_f, qw_b, kwt_f, kwt_b = (tab_scr[h, i] for i in range(5))
        gc = gc_scr[h]
        gc_f = gc[0:1, :]
        gc_b = gc[1:2, :]

        kv_f, kv_b = [], []
        for n in range(n_chunks):
            kt = k_ref[rows(n), cols].astype(F32).T
            vn = v_ref[rows(n), cols]
            kv_f.append(_dot((kt * kwt_f).astype(BF16), vn))
            kv_b.append(_dot((kt * kwt_b).astype(BF16), vn))

        s = s0f_ref[h]
        prev_f = []
        for n in range(n_chunks):
            prev_f.append(s.astype(BF16))
            s = gc_f * s + kv_f[n]
        sfo_ref[h] = s
        s = s0b_ref[h]
        prev_b = [None] * n_chunks
        for n in reversed(range(n_chunks)):
            prev_b[n] = s.astype(BF16)
            s = gc_b * s + kv_b[n]
        sbo_ref[h] = s

        gn = gn_ref[:, cols]
        for n in range(n_chunks):
            qn = q_ref[rows(n), cols]
            qf = qn.astype(F32)
            scores = lax.dot_general(qn, k_ref[rows(n), cols], (((1,), (1,)), ((), ())),
                                     preferred_element_type=F32)
            o = _dot((scores * decay).astype(BF16), v_ref[rows(n), cols])
            o = o + _dot((qf * qw_f).astype(BF16), prev_f[n])
            o = o + _dot((qf * qw_b).astype(BF16), prev_b[n])
            mu = jnp.mean(o, axis=-1, keepdims=True)
            d = o - mu
            var = jnp.mean(d * d, axis=-1, keepdims=True)
            on = d * lax.rsqrt(var + EPS) * gn
            r_ref[rows(n), cols] = (on * sg_ref[rows(n), cols].astype(F32)).astype(BF16)


def _retention(q, k, v, sg, dec, gn_g, s0f, s0b, batch, seq_len):
    hd = RET_HEAD_DIM
    tok_spec = pl.BlockSpec((seq_len, RET_WIDTH), lambda b: (b, 0))
    st_spec = pl.BlockSpec((None, N_RET_HEADS, hd, hd), lambda b: (b, 0, 0, 0))
    st_shape = jax.ShapeDtypeStruct((batch, N_RET_HEADS, hd, hd), F32)
    return pl.pallas_call(
        _retention_kernel,
        grid=(batch,),
        in_specs=[tok_spec, tok_spec, tok_spec, tok_spec,
                  pl.BlockSpec(dec.shape, lambda b: (0, 0, 0)),
                  pl.BlockSpec(gn_g.shape, lambda b: (0, 0)),
                  st_spec, st_spec],
        out_specs=[tok_spec, st_spec, st_spec],
        out_shape=[jax.ShapeDtypeStruct((batch * seq_len, RET_WIDTH), BF16), st_shape, st_shape],
        scratch_shapes=[pltpu.VMEM((N_RET_HEADS, 5, CHUNK, CHUNK), F32),
                        pltpu.VMEM((N_RET_HEADS, 2, hd), F32)],
        compiler_params=pltpu.CompilerParams(dimension_semantics=("arbitrary",),
                                             vmem_limit_bytes=VMEM_LIMIT),
        name="retention",
    )(q, k, v, sg, dec, gn_g, s0f, s0b)


def _fnet_merge_kernel(uf_ref, cs_ref, cls_ref, r_ref, gf_ref, gr_ref, x_ref, mod_ref, wf_ref, wr_ref, wo_ref,
                       o_ref, xcs_ref):
    seq_len = uf_ref.shape[0]
    gd = FOURIER_GROUP_DIM

    @pl.when(pl.program_id(1) == 0)
    def _():
        for g in range(N_FOURIER_GROUPS):
            x = _dot(uf_ref[:, g * gd:(g + 1) * gd], cs_ref[...])
            xcs_ref[0:seq_len, g * gd:(g + 1) * gd] = x[:, :gd].astype(BF16)
            xcs_ref[seq_len:2 * seq_len, g * gd:(g + 1) * gd] = x[:, gd:].astype(BF16)

    f_mix = _dot(cls_ref[...], xcs_ref[...]).astype(BF16)
    f_out = _dot(f_mix, wf_ref[...].astype(BF16))
    r_out = _dot(r_ref[...], wr_ref[...].astype(BF16))
    merged = gf_ref[...].astype(F32) * f_out + gr_ref[...].astype(F32) * r_out
    mix = _dot(merged.astype(BF16), wo_ref[...].astype(BF16))
    o_ref[...] = x_ref[...] + mod_ref[0, 2:3, :] * mix


def _fnet_merge(uf, cs, cls, r, gf, gr, x2d, mod3, w_four, w_ret, w_o, batch, seq_len, mod_row_of_batch):
    rb = min(FNET_ROWS, seq_len)
    nr = seq_len // rb

    def tok(w):
        return pl.BlockSpec((rb, w), lambda b, i: (b * nr + i, 0))

    def full(a):
        return pl.BlockSpec(a.shape, lambda b, i: (0, 0))

    def once(a):
        return pl.BlockSpec(a.shape, lambda b, i: (0, 0), pipeline_mode=pl.Buffered(1))

    return pl.pallas_call(
        _fnet_merge_kernel,
        grid=(batch, nr),
        in_specs=[pl.BlockSpec((seq_len, D_MODEL), lambda b, i: (b, 0)),
                  full(cs),
                  pl.BlockSpec((rb, 2 * seq_len), lambda b, i: (i, 0)),
                  tok(RET_WIDTH), tok(D_MODEL), tok(D_MODEL), tok(D_MODEL),
                  pl.BlockSpec((1, 6, D_MODEL), lambda b, i: (mod_row_of_batch(b), 0, 0)),
                  once(w_four), once(w_ret), once(w_o)],
        out_specs=tok(D_MODEL),
        out_shape=jax.ShapeDtypeStruct((batch * seq_len, D_MODEL), F32),
        scratch_shapes=[pltpu.VMEM((2 * seq_len, D_MODEL), BF16)],
        compiler_params=pltpu.CompilerParams(dimension_semantics=("parallel", "arbitrary"),
                                             vmem_limit_bytes=VMEM_LIMIT),
        name="fnet_merge",
    )(uf, cs, cls, r, gf, gr, x2d, mod3, w_four, w_ret, w_o)


def _pack_pair(lo_f32, hi_f32):
    lo = lax.bitcast_convert_type(lo_f32.astype(BF16).astype(F32), jnp.uint32)
    hi = lax.bitcast_convert_type(hi_f32.astype(BF16).astype(F32), jnp.uint32)
    return lax.bitcast_convert_type((lo >> 16) | hi, jnp.int32)


def _unpack_pair(words_i32):
    w = lax.bitcast_convert_type(words_i32, jnp.uint32)
    lo = lax.bitcast_convert_type(w << 16, F32)
    hi = lax.bitcast_convert_type(w & jnp.uint32(0xFFFF0000), F32)
    return lo, hi


def _load_token_words(ref, lead, n_tok):
    parts = []
    for s in range(ROW_SLABS):
        idx = (pl.ds(s, n_tok, stride=ROW_SLABS), slice(None))
        parts.append(ref[lead + idx] if lead else ref[idx])
    return jnp.concatenate(parts, axis=1)


def _store_token_words(ref, words, n_tok):
    for s in range(ROW_SLABS):
        ref[pl.ds(s, n_tok, stride=ROW_SLABS), :] = words[:, s * 128:(s + 1) * 128]


def _route(scores, biased):
    tokens = scores.shape[1]
    neg = -jnp.inf
    epg = EXPERTS_PER_GROUP
    iota_g = lax.broadcasted_iota(jnp.int32, (epg, tokens), 0).astype(F32)

    def pick_first_max(cur, iota, size):
        m = jnp.max(cur, axis=0, keepdims=True)
        idx = jnp.min(jnp.where(cur == m, iota, float(size)), axis=0, keepdims=True)
        return m, idx, iota == idx

    group_scores = []
    for g in range(N_EXPERT_GROUPS):
        vals = biased[g * epg:(g + 1) * epg, :]
        m1, _, hit = pick_first_max(vals, iota_g, epg)
        m2 = jnp.max(jnp.where(hit, neg, vals), axis=0, keepdims=True)
        group_scores.append(m1 + m2)
    cur = jnp.concatenate(group_scores, axis=0)
    group_sel = jnp.zeros_like(cur)
    for _ in range(TOPK_GROUPS):
        _, _, hit = pick_first_max(cur, iota_g, N_EXPERT_GROUPS)
        group_sel = jnp.where(hit, 1.0, group_sel)
        cur = jnp.where(hit, neg, cur)
    masked = jnp.concatenate(
        [jnp.where(group_sel[g:g + 1, :] > 0.0, biased[g * epg:(g + 1) * epg, :], neg)
         for g in range(N_EXPERT_GROUPS)], axis=0)
    iota_e = lax.broadcasted_iota(jnp.int32, masked.shape, 0).astype(F32)
    sel = jnp.zeros_like(masked)
    cur = masked
    picks = []
    for _ in range(TOP_K):
        _, idx, hit = pick_first_max(cur, iota_e, N_EXPERTS)
        picks.append(idx)
        sel = jnp.where(hit, 1.0, sel)
        cur = jnp.where(hit, neg, cur)
    w = scores * sel
    return w / jnp.sum(w, axis=0, keepdims=True) * ROUTED_SCALE, sel, picks


def _router_kernel(x_ref, mod_ref, g2_ref, wrt_ref, rb_ref, hp_ref, ek_ref, rk_ref, wt_ref, cnt_ref,
                   run_scr, earlier_scr):
    tm = x_ref.shape[0]

    @pl.when(pl.program_id(0) == 0)
    def _():
        run_scr[...] = jnp.zeros_like(run_scr)
        earlier = (lax.broadcasted_iota(jnp.int32, (tm, tm), 0) < lax.broadcasted_iota(jnp.int32, (tm, tm), 1))
        earlier_scr[...] = jnp.where(earlier, 1.0, 0.0).astype(BF16)

    h = _rms_mod(x_ref[...], g2_ref[...], mod_ref[0, 3:4, :], mod_ref[0, 4:5, :])
    half = D_MODEL // 2
    _store_token_words(hp_ref, _pack_pair(h[:, :half], h[:, half:]), tm)

    def split(a):
        hi = a.astype(BF16)
        return hi, (a - hi.astype(F32)).astype(BF16)

    def dot_nt(a, b):
        return lax.dot_general(a, b, (((1,), (1,)), ((), ())), preferred_element_type=F32)

    h_hi, h_lo = split(h)
    w_hi, w_lo = split(wrt_ref[...])
    logits_t = dot_nt(w_hi, h_hi) + (dot_nt(w_hi, h_lo) + dot_nt(w_lo, h_hi))
    scores = jax.nn.sigmoid(logits_t)
    comb_t, sel, picks = _route(scores, scores + rb_ref[...])

    rank_t = _dot(sel.astype(BF16), earlier_scr[...]) + run_scr[...]
    run_scr[...] += jnp.sum(sel, axis=1, keepdims=True)
    cnt_ref[...] = jnp.broadcast_to(run_scr[...], cnt_ref.shape)

    iota_e = lax.broadcasted_iota(jnp.int32, sel.shape, 0).astype(F32)
    ranks, weights = [], []
    for idx in picks:
        hit = iota_e == idx
        ranks.append(jnp.sum(jnp.where(hit, rank_t, 0.0), axis=0, keepdims=True))
        weights.append(jnp.sum(jnp.where(hit, comb_t, 0.0), axis=0, keepdims=True))
    ek_ref[...] = jnp.concatenate(picks, axis=0).astype(jnp.int32)
    rk_ref[...] = jnp.concatenate(ranks, axis=0).astype(jnp.int32)
    w_rep = jnp.concatenate([jnp.broadcast_to(w, (SC_LANES, tm)) for w in weights], axis=0)
    wt_ref[...] = w_rep.T


def _router(x1, mod3, norm2_g, w_router_t, router_bias, seq_len, mod_row_of_batch):
    t = x1.shape[0]
    tm = TM_ROUTER

    def mod_idx(i):
        return (mod_row_of_batch((i * tm) // seq_len), 0, 0)

    def full(a):
        return pl.BlockSpec(a.shape, lambda i: (0,) * a.ndim)

    return pl.pallas_call(
        _router_kernel,
        grid=(t // tm,),
        in_specs=[pl.BlockSpec((tm, D_MODEL), lambda i: (i, 0)),
                  pl.BlockSpec((1, 6, D_MODEL), mod_idx),
                  full(norm2_g), full(w_router_t), full(router_bias)],
        out_specs=[pl.BlockSpec((tm * ROW_SLABS, 128), lambda i: (i, 0)),
                   pl.BlockSpec((TOP_K, tm), lambda i: (0, i)),
                   pl.BlockSpec((TOP_K, tm), lambda i: (0, i)),
                   pl.BlockSpec((tm, 128), lambda i: (i, 0)),
                   pl.BlockSpec((N_EXPERTS, 128), lambda i: (0, 0))],
        out_shape=[jax.ShapeDtypeStruct((t * ROW_SLABS, 128), jnp.int32),
                   jax.ShapeDtypeStruct((TOP_K, t), jnp.int32),
                   jax.ShapeDtypeStruct((TOP_K, t), jnp.int32),
                   jax.ShapeDtypeStruct((t, 128), F32),
                   jax.ShapeDtypeStruct((N_EXPERTS, 128), F32)],
        scratch_shapes=[pltpu.VMEM((N_EXPERTS, 1), F32), pltpu.VMEM((tm, tm), BF16)],
        compiler_params=pltpu.CompilerParams(dimension_semantics=("arbitrary",),
                                             vmem_limit_bytes=VMEM_LIMIT),
        name="router",
    )(x1, mod3, norm2_g, w_router_t, router_bias)


def _plan_kernel(ek_ref, rk_ref, cnt_ref, pos_ref, texp_ref, nused_ref, tend_ref, *, expert_rows):
    rows = float(expert_rows)
    cnt = cnt_ref[:, 0:1]
    tiles = jnp.floor((cnt + (rows - 1.0)) / rows)
    before = (lax.broadcasted_iota(jnp.int32, (N_EXPERTS, N_EXPERTS), 1)
              < lax.broadcasted_iota(jnp.int32, (N_EXPERTS, N_EXPERTS), 0))
    tile_start = jnp.dot(jnp.where(before, 1.0, 0.0), jnp.broadcast_to(tiles, (N_EXPERTS, 128)),
                         precision=lax.Precision.HIGHEST, preferred_element_type=F32)[:, 0:1]
    tile_end = tile_start + tiles
    row_start = tile_start * rows

    ek = ek_ref[...]
    pos = rk_ref[...].astype(F32)
    tile_id = lax.broadcasted_iota(jnp.int32, texp_ref.shape, 1).astype(F32)
    texp = jnp.zeros(texp_ref.shape, F32)
    for e in range(N_EXPERTS):
        pos = pos + jnp.where(ek == e, row_start[e:e + 1, :], 0.0)
        texp = texp + jnp.where(tile_id >= tile_end[e:e + 1, :], 1.0, 0.0)
    pos_ref[...] = pos.astype(jnp.int32)
    texp_ref[...] = jnp.minimum(texp, N_EXPERTS - 1.0).astype(jnp.int32)
    nused_ref[...] = jnp.broadcast_to(tile_end[N_EXPERTS - 1:N_EXPERTS, :], nused_ref.shape).astype(jnp.int32)
    tend_ref[...] = jnp.broadcast_to(tile_end, tend_ref.shape).astype(jnp.int32)


def _plan(ek, rk, cnt, n_tiles_pad, expert_rows):
    t = ek.shape[1]

    def full(shape):
        return pl.BlockSpec(shape, lambda: (0,) * len(shape))

    return pl.pallas_call(
        functools.partial(_plan_kernel, expert_rows=expert_rows),
        in_specs=[full(ek.shape), full(rk.shape), full(cnt.shape)],
        out_specs=[full((TOP_K, t)), full((1, n_tiles_pad)), full((1, 128)), full((N_EXPERTS, 128))],
        out_shape=[jax.ShapeDtypeStruct((TOP_K, t), jnp.int32),
                   jax.ShapeDtypeStruct((1, n_tiles_pad), jnp.int32),
                   jax.ShapeDtypeStruct((1, 128), jnp.int32),
                   jax.ShapeDtypeStruct((N_EXPERTS, 128), jnp.int32)],
        compiler_params=pltpu.CompilerParams(vmem_limit_bytes=VMEM_LIMIT),
        name="plan",
    )(ek, rk, cnt)


def _sc_mesh():
    return plsc.VectorSubcoreMesh(core_axis_name="c", subcore_axis_name="s")


def _sc_pack_weight_halves(w):
    e, k, n = w.shape
    k_half = k // 2
    rb = SC_PACK_BLOCK_WORDS // n
    units_per_expert = k_half // rb
    per_w = (e * units_per_expert) // SC_WORKERS
    lanes = SC_LANES

    @functools.partial(
        pl.kernel, out_type=jax.ShapeDtypeStruct((e * k_half, n), jnp.int32), mesh=_sc_mesh(),
        scratch_types=[pltpu.VMEM((rb, n), F32), pltpu.VMEM((rb, n), F32), pltpu.VMEM((rb, n), jnp.int32)],
        compiler_params=pltpu.CompilerParams(needs_layout_passes=False))
    def kern(w_hbm, out_hbm, a_v, b_v, o_v):
        wid = lax.axis_index("s") * SC_CORES + lax.axis_index("c")

        @pl.loop(0, per_w)
        def _(j):
            unit = wid * per_w + j
            expert = unit // units_per_expert
            blk = unit % units_per_expert
            row_a = expert * k + blk * rb
            pltpu.sync_copy(w_hbm.at[pl.ds(row_a, rb)], a_v)
            pltpu.sync_copy(w_hbm.at[pl.ds(row_a + k_half, rb)], b_v)

            @pl.loop(0, rb)
            def _(r):
                @plsc.parallel_loop(0, n, step=lanes, unroll=4)
                def _(c):
                    both = plsc.pack(a_v[r, pl.ds(c, lanes)], b_v[r, pl.ds(c, lanes)],
                                     format=plsc.PackFormat.INTERLEAVED)
                    o_v[r, pl.ds(c, lanes)] = plsc.bitcast(both, jnp.int32)

            pltpu.sync_copy(o_v, out_hbm.at[pl.ds(expert * k_half + blk * rb, rb)])

    return kern(w.reshape(e * k, n)).reshape(e, k_half, n)


def _sc_dispatch(rows, pos3, n_out, after=()):
    t = rows.shape[0]
    ch = SC_CHUNK
    per_w = (t // ch) // SC_WORKERS

    @functools.partial(
        pl.kernel, out_type=jax.ShapeDtypeStruct((n_out,) + rows.shape[1:], jnp.int32), mesh=_sc_mesh(),
        scratch_types=[pltpu.VMEM((TOP_K, ch), jnp.int32), pltpu.VMEM((ch,) + rows.shape[1:], jnp.int32),
                       pltpu.SemaphoreType.DMA])
    def k(rows_hbm, pos_hbm, *rest):
        out_hbm, idx_v, rows_v, sem = rest[len(after):]
        wid = lax.axis_index("s") * SC_CORES + lax.axis_index("c")

        @pl.loop(0, per_w)
        def _(j):
            c = wid * per_w + j
            pltpu.sync_copy(pos_hbm.at[c], idx_v)
            pltpu.sync_copy(rows_hbm.at[pl.ds(c * ch, ch)], rows_v)
            copies = [pltpu.async_copy(rows_v, out_hbm.at[idx_v.at[kk]], sem) for kk in range(TOP_K)]
            for cp in copies:
                cp.wait()

    return k(rows, pos3, *after)


def _sc_combine(table, pos3, wtok, t):
    ch = SC_CHUNK
    sub = SC_COMBINE_TOKENS
    lanes = SC_LANES
    slabs = ROW_SLABS
    per_w = (t // ch) // SC_WORKERS
    subs_per_chunk = ch // sub
    n_steps = per_w * subs_per_chunk

    @functools.partial(
        pl.kernel, out_type=jax.ShapeDtypeStruct((t, slabs, 128), jnp.int32), mesh=_sc_mesh(),
        scratch_types=[pltpu.VMEM((per_w, TOP_K, ch), jnp.int32),
                       pltpu.VMEM((2, TOP_K, sub, slabs, 128), jnp.int32),
                       pltpu.VMEM((2, sub, 128), F32),
                       pltpu.VMEM((sub, slabs, 128), jnp.int32),
                       pltpu.SemaphoreType.DMA((2,))],
        compiler_params=pltpu.CompilerParams(needs_layout_passes=False))
    def k(tab_hbm, pos_hbm, w_hbm, out_hbm, idx_v, rows_v, w_v, out_v, sem):
        wid = lax.axis_index("s") * SC_CORES + lax.axis_index("c")
        for j in range(per_w):
            pltpu.sync_copy(pos_hbm.at[wid * per_w + j], idx_v.at[j])

        def first_token(step):
            return (wid * per_w + step // subs_per_chunk) * ch + (step % subs_per_chunk) * sub

        def copies(step, slot):
            j = step // subs_per_chunk
            s = step % subs_per_chunk
            idx = [idx_v.at[j, kk, pl.ds(s * sub, sub)] for kk in range(TOP_K)]
            return ([pltpu.make_async_copy(tab_hbm.at[idx[kk]], rows_v.at[slot, kk], sem.at[slot])
                     for kk in range(TOP_K)]
                    + [pltpu.make_async_copy(w_hbm.at[pl.ds(first_token(step), sub)], w_v.at[slot], sem.at[slot])])

        for cp in copies(0, 0):
            cp.start()

        @pl.loop(0, n_steps)
        def _(step):
            slot = step % 2

            @pl.when(step + 1 < n_steps)
            def _():
                for cp in copies(step + 1, 1 - slot):
                    cp.start()

            for cp in copies(step, slot):
                cp.wait()

            @pl.loop(0, sub)
            def _(tt):
                wk = [w_v[slot, tt, pl.ds(kk * lanes, lanes)] for kk in range(TOP_K)]
                for sl in range(slabs):
                    @plsc.parallel_loop(0, 128, step=lanes, unroll=8)
                    def _(off):
                        acc_lo = jnp.zeros((lanes,), F32)
                        acc_hi = jnp.zeros((lanes,), F32)
                        for kk in range(TOP_K):
                            word = rows_v[slot, kk, tt, sl, pl.ds(off, lanes)]
                            lo = plsc.bitcast(word << 16, F32)
                            hi = plsc.bitcast(word & jnp.int32(-65536), F32)
                            acc_lo = acc_lo + wk[kk] * lo
                            acc_hi = acc_hi + wk[kk] * hi
                        both = plsc.pack(acc_lo, acc_hi, format=plsc.PackFormat.INTERLEAVED)
                        out_v[tt, sl, pl.ds(off, lanes)] = plsc.bitcast(both, jnp.int32)

            pltpu.sync_copy(out_v, out_hbm.at[pl.ds(first_token(step), sub)])

    return k(table, pos3, wtok)


def _experts_kernel(texp_ref, nused_ref, tend_ref, xs_ref, weg_hbm, weu_hbm, wed_hbm, ys_ref,
                    wg_scr, wu_scr, wd_scr, wg_buf, wu_buf, wd_buf, sem, group_scr, *, expert_rows):
    step = pl.program_id(0)
    rows = expert_rows
    tiles_per_step = EXPERT_STEP_ROWS // expert_rows
    half = D_MODEL // 2
    n_used = nused_ref[0]

    def weight_copies(e, slot):
        return [pltpu.make_async_copy(weg_hbm.at[e], wg_buf.at[slot], sem.at[slot, 0]),
                pltpu.make_async_copy(weu_hbm.at[e], wu_buf.at[slot], sem.at[slot, 1]),
                pltpu.make_async_copy(wed_hbm.at[e], wd_buf.at[slot], sem.at[slot, 2])]

    def next_group(e):
        tile = tend_ref[e]
        return texp_ref[jnp.minimum(tile, n_used - 1)], tile < n_used

    def start_weights(e, slot, exists):
        @pl.when(exists)
        def _():
            for cp in weight_copies(e, slot):
                cp.start()

    @pl.when(step == 0)
    def _():
        group_scr[0] = 0
        e, exists = texp_ref[0], True
        for slot in range(WEIGHT_SLOTS - 1):
            start_weights(e, slot, exists)
            nxt, has_next = next_group(e)
            e, exists = nxt, exists & has_next

    def row_tile(tile, x_view, y_view):
        expert = texp_ref[tile]
        used = tile < n_used
        new_expert = (tile == 0) | (expert != texp_ref[jnp.maximum(tile - 1, 0)])

        @pl.when(used & new_expert)
        def _():
            group = group_scr[0]
            slot = group % WEIGHT_SLOTS
            ahead, exists = expert, True
            for _ in range(WEIGHT_SLOTS - 1):
                nxt, has_next = next_group(ahead)
                ahead, exists = nxt, exists & has_next
            start_weights(ahead, (group + WEIGHT_SLOTS - 1) % WEIGHT_SLOTS, exists)

            for cp in weight_copies(expert, slot):
                cp.wait()
            for scr, buf in ((wg_scr, wg_buf), (wu_scr, wu_buf), (wd_scr, wd_buf)):
                top, bottom = _unpack_pair(buf[slot])
                k_half = top.shape[0]
                scr[0:k_half, :] = top.astype(BF16)
                scr[k_half:2 * k_half, :] = bottom.astype(BF16)
            group_scr[0] = group + 1

        @pl.when(used)
        def _():
            lo, hi = _unpack_pair(_load_token_words(x_view, (), rows))
            lo = lo.astype(BF16)
            hi = hi.astype(BF16)
            g = _dot(lo, wg_scr[0:half, :]) + _dot(hi, wg_scr[half:D_MODEL, :])
            u = _dot(lo, wu_scr[0:half, :]) + _dot(hi, wu_scr[half:D_MODEL, :])
            y = _dot((_silu(g) * u).astype(BF16), wd_scr[...])
            _store_token_words(y_view, _pack_pair(y[:, :half], y[:, half:]), rows)

        @pl.when(jnp.logical_not(used) & (step == (n_used - 1) // tiles_per_step))
        def _():
            y_view[...] = jnp.zeros_like(y_view)

    for s in range(tiles_per_step):
        view = pl.ds(s * rows * ROW_SLABS, rows * ROW_SLABS)
        row_tile(step * tiles_per_step + s, xs_ref.at[view], ys_ref.at[view])


def _experts(texp, nused, tend, xs2d, weg, weu, wed, n_tiles, expert_rows):
    tiles_per_step = EXPERT_STEP_ROWS // expert_rows
    block = (EXPERT_STEP_ROWS * ROW_SLABS, 128)
    hbm = pl.BlockSpec(memory_space=pl.ANY)

    def block_idx(j, te, nu, tn):
        return (jnp.minimum(j, (nu[0] - 1) // tiles_per_step), 0)

    grid_spec = pltpu.PrefetchScalarGridSpec(
        num_scalar_prefetch=3,
        grid=(n_tiles // tiles_per_step,),
        in_specs=[pl.BlockSpec(block, block_idx), hbm, hbm, hbm],
        out_specs=pl.BlockSpec(block, block_idx),
        scratch_shapes=[pltpu.VMEM((D_MODEL, EXPERT_DIM), BF16),
                        pltpu.VMEM((D_MODEL, EXPERT_DIM), BF16),
                        pltpu.VMEM((EXPERT_DIM, D_MODEL), BF16),
                        pltpu.VMEM((WEIGHT_SLOTS,) + weg.shape[1:], jnp.int32),
                        pltpu.VMEM((WEIGHT_SLOTS,) + weu.shape[1:], jnp.int32),
                        pltpu.VMEM((WEIGHT_SLOTS,) + wed.shape[1:], jnp.int32),
                        pltpu.SemaphoreType.DMA((WEIGHT_SLOTS, 3)),
                        pltpu.SMEM((1,), jnp.int32)],
    )
    return pl.pallas_call(
        functools.partial(_experts_kernel, expert_rows=expert_rows),
        grid_spec=grid_spec,
        out_shape=jax.ShapeDtypeStruct(xs2d.shape, jnp.int32),
        compiler_params=pltpu.CompilerParams(dimension_semantics=("arbitrary",),
                                             vmem_limit_bytes=VMEM_LIMIT),
        name="experts",
    )(texp, nused, tend, xs2d, weg, weu, wed)


def _final_kernel(x_ref, routed_ref, mod_ref, g2_ref, wsg_ref, wsu_ref, wsd_ref, fng_ref, o_ref):
    tm = x_ref.shape[0]
    x = x_ref[...]
    hb = _rms_mod(x, g2_ref[...], mod_ref[0, 3:4, :], mod_ref[0, 4:5, :]).astype(BF16)
    shared = _dot((_silu(_dot(hb, wsg_ref[...])) * _dot(hb, wsu_ref[...])).astype(BF16), wsd_ref[...])
    routed = jnp.concatenate(_unpack_pair(_load_token_words(routed_ref, (), tm)), axis=1)
    y = x + mod_ref[0, 5:6, :] * (routed + shared)
    ms = jnp.mean(y * y, axis=-1, keepdims=True)
    o_ref[...] = y * lax.rsqrt(ms + EPS) * fng_ref[...]


def _final(x1, routed2d, mod3, norm2_g, wsg, wsu, wsd, final_g, seq_len, mod_row_of_batch):
    t = x1.shape[0]
    tm = TM_FINAL

    def mod_idx(i):
        return (mod_row_of_batch((i * tm) // seq_len), 0, 0)

    def full(a):
        return pl.BlockSpec(a.shape, lambda i: (0,) * a.ndim)

    return pl.pallas_call(
        _final_kernel,
        grid=(t // tm,),
        in_specs=[pl.BlockSpec((tm, D_MODEL), lambda i: (i, 0)),
                  pl.BlockSpec((tm * ROW_SLABS, 128), lambda i: (i, 0)),
                  pl.BlockSpec((1, 6, D_MODEL), mod_idx),
                  full(norm2_g), full(wsg), full(wsu), full(wsd), full(final_g)],
        out_specs=pl.BlockSpec((tm, D_MODEL), lambda i: (i, 0)),
        out_shape=jax.ShapeDtypeStruct((t, D_MODEL), F32),
        compiler_params=pltpu.CompilerParams(dimension_semantics=("parallel",),
                                             vmem_limit_bytes=VMEM_LIMIT),
        name="final",
    )(x1, routed2d, mod3, norm2_g, wsg, wsu, wsd, final_g)


def _moe(x1, mod3, lw, seq_len, mod_row_of_batch):
    t = x1.shape[0]
    expert_rows = min(MAX_EXPERT_ROWS, TOP_K * t // N_EXPERTS // 2)
    n_tiles = TOP_K * t // expert_rows + N_EXPERTS
    n_tiles_pad = -(-n_tiles // 128) * 128
    hp2d, ek, rk, wtok, cnt = _router(x1, mod3, lw["norm2_g"], lw["w_router_t"], lw["router_bias"],
                                      seq_len, mod_row_of_batch)
    pos, texp, nused, tend = _plan(ek, rk, cnt, n_tiles_pad, expert_rows)
    pos3 = pos.reshape(TOP_K, t // SC_CHUNK, SC_CHUNK).transpose(1, 0, 2)
    xs = _sc_dispatch(hp2d.reshape(t, ROW_SLABS, 128), pos3, n_tiles * expert_rows,
                      after=(lw["weg"], lw["weu"], lw["wed"]))
    ys2d = _experts(texp.reshape(-1), nused.reshape(-1), tend[:, 0], xs.reshape(-1, 128),
                    lw["weg"], lw["weu"], lw["wed"], n_tiles, expert_rows)
    routed = _sc_combine(ys2d.reshape(-1, ROW_SLABS, 128), pos3, wtok, t)
    return _final(x1, routed.reshape(t * ROW_SLABS, 128), mod3, lw["norm2_g"],
                  lw["wsg"], lw["wsu"], lw["wsd"], lw["final_g"], seq_len, mod_row_of_batch)


def _dft_tables(seq_len):
    gd = FOURIER_GROUP_DIM
    kc = np.arange(gd)
    ang_c = ((kc[:, None] * kc[None, :]) % gd) * (2.0 * math.pi / gd)
    cs = np.concatenate([np.cos(ang_c), np.sin(ang_c)], axis=1) * (gd ** -0.5)
    kl = np.arange(seq_len)
    ang_l = ((kl[:, None] * kl[None, :]) % seq_len) * (2.0 * math.pi / seq_len)
    cls = np.concatenate([np.cos(ang_l), -np.sin(ang_l)], axis=1) * (seq_len ** -0.5)
    return jnp.asarray(cs.astype(np.float32), dtype=BF16), jnp.asarray(cls.astype(np.float32), dtype=BF16)


def _rope_tables(length):
    rows = length // GRID_W
    r = np.repeat(np.arange(rows, dtype=np.float32), GRID_W)
    col = np.tile(np.arange(GRID_W, dtype=np.float32), rows)
    nf = RET_HEAD_DIM // 4
    inv = (np.float32(ROPE_BASE) ** (-np.arange(nf, dtype=np.float32) / np.float32(nf))).astype(np.float32)
    ar = r[:, None] * inv[None]
    ac = col[:, None] * inv[None]
    ang = np.concatenate([ar, ar, ac, ac], axis=-1).astype(np.float64)
    sign = np.where((np.arange(RET_HEAD_DIM) & nf) == 0, -1.0, 1.0)
    return (jnp.asarray(np.cos(ang).astype(np.float32)),
            jnp.asarray((np.sin(ang) * sign[None, :]).astype(np.float32)))


def _trunk_path(x, mod3, mod_row_of_batch, s0f, s0b, rope, lw):
    batch, seq_len, _ = x.shape
    x2d = x.reshape(batch * seq_len, D_MODEL)
    uf, q, k, v, sg, gf, gr = _inproj(x2d, mod3, lw["norm1_g"], lw["w_in"], seq_len, mod_row_of_batch, rope)
    r, s_f, s_b = _retention(q, k, v, sg, lw["dec"], lw["gn_g"], s0f, s0b, batch, seq_len)
    cs, cls = _dft_tables(seq_len)
    x1 = _fnet_merge(uf, cs, cls, r, gf, gr, x2d, mod3, lw["w_four"], lw["w_ret"], lw["w_o"],
                     batch, seq_len, mod_row_of_batch)
    y = _moe(x1, mod3, lw, seq_len, mod_row_of_batch)
    return y.reshape(batch, seq_len, D_MODEL), s_f, s_b


def kernel(x_prompt, x_sample, state_ret_fwd, state_ret_bwd, c, c_ctx, w_ada, b_ada, norm1_g, norm2_g, w_in,
           ret_decay_fwd, ret_decay_bwd, ret_gn_g, w_four_out, w_ret_out, w_out, w_router, router_bias,
           w_exp_gate, w_exp_up, w_exp_down, w_shared_gate, w_shared_up, w_shared_down, final_norm_g):
    depth = w_ada.shape[0]
    assert depth == 1, "final norm is fused into the last layer's MoE kernel"
    n_ctx, n_lat = x_prompt.shape[0], x_sample.shape[0]
    cond = jnp.concatenate([c_ctx[None, :], c], axis=0)
    cond = jnp.pad(cond, ((0, (-cond.shape[0]) % 8), (0, 0)))
    rope = _rope_tables(x_sample.shape[1])
    zeros = jnp.zeros((n_ctx, N_RET_HEADS, RET_HEAD_DIM, RET_HEAD_DIM), F32)

    layer = 0
    mod = _ada(cond, w_ada[layer], b_ada[layer][None, :])
    mod3 = mod.reshape(mod.shape[0], 6, D_MODEL)
    dec = jnp.stack([ret_decay_fwd[layer], ret_decay_bwd[layer]], axis=1)
    lw = {
        "norm1_g": norm1_g[layer][None, :],
        "norm2_g": norm2_g[layer][None, :],
        "w_in": w_in[layer],
        "dec": jnp.broadcast_to(dec[:, :, None], (N_RET_HEADS, 2, RET_HEAD_DIM)).astype(F32),
        "gn_g": ret_gn_g[layer][None, :],
        "w_four": w_four_out[layer],
        "w_ret": w_ret_out[layer],
        "w_o": w_out[layer],
        "w_router_t": w_router[layer].T,
        "router_bias": router_bias[layer][:, None],
        "weg": _sc_pack_weight_halves(w_exp_gate[layer]),
        "weu": _sc_pack_weight_halves(w_exp_up[layer]),
        "wed": _sc_pack_weight_halves(w_exp_down[layer]),
        "wsg": w_shared_gate[layer].astype(BF16),
        "wsu": w_shared_up[layer].astype(BF16),
        "wsd": w_shared_down[layer].astype(BF16),
        "final_g": final_norm_g[None, :],
    }
    y_prompt, s_f, s_b = _trunk_path(x_prompt, mod3, lambda b: 0, zeros, zeros, None, lw)
    y_sample, _, _ = _trunk_path(x_sample, mod3, lambda b: 1 + b, state_ret_fwd[:, layer],
                                 state_ret_bwd[:, layer], rope, lw)
    return (y_prompt, y_sample, s_f[:, None], s_b[:, None])
```

```python
import functools
import math

import jax
import jax.numpy as jnp
import numpy as np
from jax import lax
from jax.experimental import pallas as pl
from jax.experimental.pallas import tpu as pltpu
from jax.experimental.pallas import tpu_sc as plsc

F32 = jnp.float32
BF16 = jnp.bfloat16

D_MODEL = 1024
GRID_W = 64
N_FOURIER_GROUPS = 8
FOURIER_GROUP_DIM = 128
N_RET_HEADS = 4
RET_HEAD_DIM = 128
RET_WIDTH = N_RET_HEADS * RET_HEAD_DIM
CHUNK = 128
N_EXPERTS = 64
N_EXPERT_GROUPS = 8
EXPERTS_PER_GROUP = N_EXPERTS // N_EXPERT_GROUPS
TOPK_GROUPS = 4
TOP_K = 8
EXPERT_DIM = 256
ROUTED_SCALE = 2.5
ROPE_BASE = 10000.0
EPS = 1e-6
Q_SCALE = RET_HEAD_DIM ** -0.5

_C_UF = (0, 1024)
_C_Q = (1024, 1536)
_C_K = (1536, 2048)
_C_V = (2048, 2560)
_C_G = (2560, 3072)
_C_GF = (3072, 4096)
_C_GR = (4096, 5120)

VMEM_LIMIT = 56 * 1024 * 1024

TM_INPROJ = 512
TM_ROUTER = 1024
FNET_ROWS = 512
TM_FINAL = 1024
EXPERT_STEP_ROWS = 1024
MAX_EXPERT_ROWS = 512
WEIGHT_SLOTS = 3
ROW_SLABS = 4
SC_CORES = 2
SC_WORKERS = 32
SC_CHUNK = 128
SC_LANES = 16
SC_PACK_BLOCK_WORDS = 16384
SC_COMBINE_TOKENS = 8


def _silu(x):
    return x * jax.nn.sigmoid(x)


def _dot(a, b):
    return jnp.dot(a, b, preferred_element_type=F32)


def _rms_mod(x, g, shift, scale):
    ms = jnp.mean(x * x, axis=-1, keepdims=True)
    y = x * lax.rsqrt(ms + EPS) * g
    return y * (1.0 + scale) + shift


def _ada_kernel(cond_ref, w_ref, b_ref, o_ref):
    s = _silu(cond_ref[...]).astype(BF16)
    o_ref[...] = _dot(s, w_ref[...].astype(BF16)) + b_ref[...]


def _ada(cond, w_ada, b_ada):
    rows, n = cond.shape[0], w_ada.shape[1]
    tn = 1536
    return pl.pallas_call(
        _ada_kernel,
        grid=(n // tn,),
        in_specs=[pl.BlockSpec((rows, D_MODEL), lambda j: (0, 0)),
                  pl.BlockSpec((D_MODEL, tn), lambda j: (0, j)),
                  pl.BlockSpec((1, tn), lambda j: (0, j))],
        out_specs=pl.BlockSpec((rows, tn), lambda j: (0, j)),
        out_shape=jax.ShapeDtypeStruct((rows, n), F32),
        compiler_params=pltpu.CompilerParams(vmem_limit_bytes=VMEM_LIMIT),
        name="ada",
    )(cond, w_ada, b_ada)


def _rope_head(x, cos, sin_signed, first_half):
    partner = jnp.where(first_half, pltpu.roll(x, 96, 1), pltpu.roll(x, 32, 1))
    return x * cos + partner * sin_signed


def _inproj_kernel(xl_ref, xc_ref, mod_ref, g_ref, w_ref, cos_ref, sin_ref,
                   uf_o, q_o, k_o, v_o, sg_o, gf_o, gr_o, *, n_latent_steps):
    x = jnp.where(pl.program_id(0) < n_latent_steps, xl_ref[...], xc_ref[...])
    h = _rms_mod(x, g_ref[...], mod_ref[0, 0:1, :], mod_ref[0, 1:2, :])
    hb = h.astype(BF16)

    def proj(cols):
        return _dot(hb, w_ref[:, cols[0]:cols[1]].astype(BF16))

    uf_o[...] = proj(_C_UF).astype(BF16)
    q = proj(_C_Q)
    k = proj(_C_K)
    cos = cos_ref[...]
    sin_signed = sin_ref[...]
    lane = lax.broadcasted_iota(jnp.int32, cos.shape, 1)
    first_half = (lane & 32) == 0
    for hd in range(N_RET_HEADS):
        sl = slice(hd * RET_HEAD_DIM, (hd + 1) * RET_HEAD_DIM)
        q_o[:, sl] = (_rope_head(q[:, sl], cos, sin_signed, first_half) * Q_SCALE).astype(BF16)
        k_o[:, sl] = _rope_head(k[:, sl], cos, sin_signed, first_half).astype(BF16)
    v_o[...] = proj(_C_V).astype(BF16)
    sg_o[...] = _silu(proj(_C_G)).astype(BF16)
    gf_o[...] = jax.nn.sigmoid(proj(_C_GF)).astype(BF16)
    gr_o[...] = jax.nn.sigmoid(proj(_C_GR)).astype(BF16)


def _inproj(x_lat2d, x_ctx2d, lat_seq_len, mod3, norm_g, w_in_f32, rope):
    tm = TM_INPROJ
    assert lat_seq_len % tm == 0 and x_ctx2d.shape[0] % tm == 0
    tiles_per_seq = lat_seq_len // tm
    n_lat = x_lat2d.shape[0] // tm
    n_ctx = x_ctx2d.shape[0] // tm
    t = (n_lat + n_ctx) * tm
    cos, sin_signed = rope
    cos2 = jnp.concatenate([cos, jnp.ones((tm, RET_HEAD_DIM), F32)], axis=0)
    sin2 = jnp.concatenate([sin_signed, jnp.zeros((tm, RET_HEAD_DIM), F32)], axis=0)
    rope_spec = pl.BlockSpec((tm, RET_HEAD_DIM),
                             lambda i: (jnp.where(i < n_lat, i % tiles_per_seq, tiles_per_seq), 0))
    widths = [1024, RET_WIDTH, RET_WIDTH, RET_WIDTH, RET_WIDTH, 1024, 1024]
    return pl.pallas_call(
        functools.partial(_inproj_kernel, n_latent_steps=n_lat),
        grid=(n_lat + n_ctx,),
        in_specs=[pl.BlockSpec((tm, D_MODEL), lambda i: (jnp.minimum(i, n_lat - 1), 0)),
                  pl.BlockSpec((tm, D_MODEL), lambda i: (jnp.maximum(i - n_lat, 0), 0)),
                  pl.BlockSpec((1, 6, D_MODEL), lambda i: (jnp.where(i < n_lat, 1 + i // tiles_per_seq, 0), 0, 0)),
                  pl.BlockSpec((1, D_MODEL), lambda i: (0, 0)),
                  pl.BlockSpec(w_in_f32.shape, lambda i: (0, 0), pipeline_mode=pl.Buffered(1)),
                  rope_spec, rope_spec],
        out_specs=[pl.BlockSpec((tm, w), lambda i: (i, 0)) for w in widths],
        out_shape=[jax.ShapeDtypeStruct((t, w), BF16) for w in widths],
        compiler_params=pltpu.CompilerParams(dimension_semantics=("parallel",),
                                             vmem_limit_bytes=VMEM_LIMIT),
        name="inproj",
    )(x_lat2d, x_ctx2d, mod3, norm_g, w_in_f32, cos2, sin2)


def _retention_kernel(q_ref, k_ref, v_ref, sg_ref, dec_ref, gn_ref, s0f_ref, s0b_ref,
                      r_ref, sfo_ref, sbo_ref, tab_scr, gc_scr):
    n_chunks = q_ref.shape[0] // CHUNK
    hd = RET_HEAD_DIM

    @pl.when(pl.program_id(0) == 0)
    def _():
        row = lax.broadcasted_iota(jnp.int32, (CHUNK, CHUNK), 0).astype(F32)
        col = lax.broadcasted_iota(jnp.int32, (CHUNK, CHUNK), 1).astype(F32)
        diff = row - col
        for h in range(N_RET_HEADS):
            dec = dec_ref[h]
            lg = jnp.minimum(dec, 0.0) - jnp.log1p(jnp.exp(-jnp.abs(dec)))
            lgf = lg[0:1, :]
            lgb = lg[1:2, :]
            tab_scr[h, 0] = jnp.exp(jnp.where(diff >= 0, lgf * diff, lgb * (-diff)))
            tab_scr[h, 1] = jnp.exp(lgf * (row + 1.0))
            tab_scr[h, 2] = jnp.exp(lgb * (CHUNK - row))
            tab_scr[h, 3] = jnp.exp(lgf * (CHUNK - 1.0 - col))
            tab_scr[h, 4] = jnp.exp(lgb * col)
            gc_scr[h] = jnp.exp(lg * CHUNK)

    def rows(n):
        return slice(n * CHUNK, (n + 1) * CHUNK)

    for h in range(N_RET_HEADS):
        cols = slice(h * hd, (h + 1) * hd)
        decay, qw_f, qw_b, kwt_f, kwt_b = (tab_scr[h, i] for i in range(5))
        gc = gc_scr[h]
        gc_f = gc[0:1, :]
        gc_b = gc[1:2, :]

        kv_f, kv_b = [], []
        for n in range(n_chunks):
            kt = k_ref[rows(n), cols].astype(F32).T
            vn = v_ref[rows(n), cols]
            kv_f.append(_dot((kt * kwt_f).astype(BF16), vn))
            kv_b.append(_dot((kt * kwt_b).astype(BF16), vn))

        s = s0f_ref[h]
        prev_f = []
        for n in range(n_chunks):
            prev_f.append(s.astype(BF16))
            s = gc_f * s + kv_f[n]
        sfo_ref[h] = s
        s = s0b_ref[h]
        prev_b = [None] * n_chunks
        for n in reversed(range(n_chunks)):
            prev_b[n] = s.astype(BF16)
            s = gc_b * s + kv_b[n]
        sbo_ref[h] = s

        gn = gn_ref[:, cols]
        for n in range(n_chunks):
            qn = q_ref[rows(n), cols]
            qf = qn.astype(F32)
            scores = lax.dot_general(qn, k_ref[rows(n), cols], (((1,), (1,)), ((), ())),
                                     preferred_element_type=F32)
            o = _dot((scores * decay).astype(BF16), v_ref[rows(n), cols])
            o = o + _dot((qf * qw_f).astype(BF16), prev_f[n])
            o = o + _dot((qf * qw_b).astype(BF16), prev_b[n])
            mu = jnp.mean(o, axis=-1, keepdims=True)
            d = o - mu
            var = jnp.mean(d * d, axis=-1, keepdims=True)
            on = d * lax.rsqrt(var + EPS) * gn
            r_ref[rows(n), cols] = (on * sg_ref[rows(n), cols].astype(F32)).astype(BF16)


def _retention(q, k, v, sg, dec, gn_g, s0f, s0b, batch, seq_len, row0):
    hd = RET_HEAD_DIM
    first = row0 // seq_len
    in_spec = pl.BlockSpec((seq_len, RET_WIDTH), lambda b: (first + b, 0))
    tok_spec = pl.BlockSpec((seq_len, RET_WIDTH), lambda b: (b, 0))
    st_spec = pl.BlockSpec((None, N_RET_HEADS, hd, hd), lambda b: (b, 0, 0, 0))
    st_shape = jax.ShapeDtypeStruct((batch, N_RET_HEADS, hd, hd), F32)
    return pl.pallas_call(
        _retention_kernel,
        grid=(batch,),
        in_specs=[in_spec, in_spec, in_spec, in_spec,
                  pl.BlockSpec(dec.shape, lambda b: (0, 0, 0)),
                  pl.BlockSpec(gn_g.shape, lambda b: (0, 0)),
                  st_spec, st_spec],
        out_specs=[tok_spec, st_spec, st_spec],
        out_shape=[jax.ShapeDtypeStruct((batch * seq_len, RET_WIDTH), BF16), st_shape, st_shape],
        scratch_shapes=[pltpu.VMEM((N_RET_HEADS, 5, CHUNK, CHUNK), F32),
                        pltpu.VMEM((N_RET_HEADS, 2, hd), F32)],
        compiler_params=pltpu.CompilerParams(dimension_semantics=("arbitrary",),
                                             vmem_limit_bytes=VMEM_LIMIT),
        name="retention",
    )(q, k, v, sg, dec, gn_g, s0f, s0b)


def _fnet_merge_kernel(uf_ref, cs_ref, cls_ref, r_ref, gf_ref, gr_ref, x_ref, mod_ref, wf_ref, wr_ref, wo_ref,
                       o_ref, xcs_ref):
    seq_len = uf_ref.shape[0]
    gd = FOURIER_GROUP_DIM

    @pl.when(pl.program_id(1) == 0)
    def _():
        for g in range(N_FOURIER_GROUPS):
            x = _dot(uf_ref[:, g * gd:(g + 1) * gd], cs_ref[...])
            xcs_ref[0:seq_len, g * gd:(g + 1) * gd] = x[:, :gd].astype(BF16)
            xcs_ref[seq_len:2 * seq_len, g * gd:(g + 1) * gd] = x[:, gd:].astype(BF16)

    f_mix = _dot(cls_ref[...], xcs_ref[...]).astype(BF16)
    f_out = _dot(f_mix, wf_ref[...].astype(BF16))
    r_out = _dot(r_ref[...], wr_ref[...].astype(BF16))
    merged = gf_ref[...].astype(F32) * f_out + gr_ref[...].astype(F32) * r_out
    mix = _dot(merged.astype(BF16), wo_ref[...].astype(BF16))
    o_ref[...] = x_ref[...] + mod_ref[0, 2:3, :] * mix


def _fnet_merge(uf, cs, cls, r, gf, gr, x2d, mod3, w_four, w_ret, w_o, batch, seq_len, mod_row_of_batch, row0):
    rb = min(FNET_ROWS, seq_len)
    nr = seq_len // rb

    def tok(w):
        return pl.BlockSpec((rb, w), lambda b, i: (b * nr + i, 0))

    def tok_joint(w):
        return pl.BlockSpec((rb, w), lambda b, i: (row0 // rb + b * nr + i, 0))

    def full(a):
        return pl.BlockSpec(a.shape, lambda b, i: (0, 0))

    def once(a):
        return pl.BlockSpec(a.shape, lambda b, i: (0, 0), pipeline_mode=pl.Buffered(1))

    return pl.pallas_call(
        _fnet_merge_kernel,
        grid=(batch, nr),
        in_specs=[pl.BlockSpec((seq_len, D_MODEL), lambda b, i: (row0 // seq_len + b, 0)),
                  full(cs),
                  pl.BlockSpec((rb, 2 * seq_len), lambda b, i: (i, 0)),
                  tok(RET_WIDTH), tok_joint(D_MODEL), tok_joint(D_MODEL), tok(D_MODEL),
                  pl.BlockSpec((1, 6, D_MODEL), lambda b, i: (mod_row_of_batch(b), 0, 0)),
                  once(w_four), once(w_ret), once(w_o)],
        out_specs=tok(D_MODEL),
        out_shape=jax.ShapeDtypeStruct((batch * seq_len, D_MODEL), F32),
        scratch_shapes=[pltpu.VMEM((2 * seq_len, D_MODEL), BF16)],
        compiler_params=pltpu.CompilerParams(dimension_semantics=("parallel", "arbitrary"),
                                             vmem_limit_bytes=VMEM_LIMIT),
        name="fnet_merge",
    )(uf, cs, cls, r, gf, gr, x2d, mod3, w_four, w_ret, w_o)


def _pack_pair(lo_f32, hi_f32):
    lo = lax.bitcast_convert_type(lo_f32.astype(BF16).astype(F32), jnp.uint32)
    hi = lax.bitcast_convert_type(hi_f32.astype(BF16).astype(F32), jnp.uint32)
    return lax.bitcast_convert_type((lo >> 16) | hi, jnp.int32)


def _unpack_pair(words_i32):
    w = lax.bitcast_convert_type(words_i32, jnp.uint32)
    lo = lax.bitcast_convert_type(w << 16, F32)
    hi = lax.bitcast_convert_type(w & jnp.uint32(0xFFFF0000), F32)
    return lo, hi


def _load_token_words(ref, lead, n_tok):
    parts = []
    for s in range(ROW_SLABS):
        idx = (pl.ds(s, n_tok, stride=ROW_SLABS), slice(None))
        parts.append(ref[lead + idx] if lead else ref[idx])
    return jnp.concatenate(parts, axis=1)


def _store_token_words(ref, words, n_tok):
    for s in range(ROW_SLABS):
        ref[pl.ds(s, n_tok, stride=ROW_SLABS), :] = words[:, s * 128:(s + 1) * 128]


def _route(scores, biased):
    tokens = scores.shape[1]
    neg = -jnp.inf
    epg = EXPERTS_PER_GROUP
    iota_g = lax.broadcasted_iota(jnp.int32, (epg, tokens), 0).astype(F32)

    def pick_first_max(cur, iota, size):
        m = jnp.max(cur, axis=0, keepdims=True)
        idx = jnp.min(jnp.where(cur == m, iota, float(size)), axis=0, keepdims=True)
        return m, idx, iota == idx

    group_scores = []
    for g in range(N_EXPERT_GROUPS):
        vals = biased[g * epg:(g + 1) * epg, :]
        m1, _, hit = pick_first_max(vals, iota_g, epg)
        m2 = jnp.max(jnp.where(hit, neg, vals), axis=0, keepdims=True)
        group_scores.append(m1 + m2)
    cur = jnp.concatenate(group_scores, axis=0)
    group_sel = jnp.zeros_like(cur)
    for _ in range(TOPK_GROUPS):
        _, _, hit = pick_first_max(cur, iota_g, N_EXPERT_GROUPS)
        group_sel = jnp.where(hit, 1.0, group_sel)
        cur = jnp.where(hit, neg, cur)
    masked = jnp.concatenate(
        [jnp.where(group_sel[g:g + 1, :] > 0.0, biased[g * epg:(g + 1) * epg, :], neg)
         for g in range(N_EXPERT_GROUPS)], axis=0)
    iota_e = lax.broadcasted_iota(jnp.int32, masked.shape, 0).astype(F32)
    sel = jnp.zeros_like(masked)
    cur = masked
    picks = []
    for _ in range(TOP_K):
        _, idx, hit = pick_first_max(cur, iota_e, N_EXPERTS)
        picks.append(idx)
        sel = jnp.where(hit, 1.0, sel)
        cur = jnp.where(hit, neg, cur)
    w = scores * sel
    return w / jnp.sum(w, axis=0, keepdims=True) * ROUTED_SCALE, sel, picks


def _router_kernel(x_ref, mod_ref, g2_ref, wrt_ref, rb_ref, hp_ref, ek_ref, rk_ref, wt_ref, cnt_ref,
                   run_scr, earlier_scr):
    tm = x_ref.shape[0]

    @pl.when(pl.program_id(0) == 0)
    def _():
        run_scr[...] = jnp.zeros_like(run_scr)
        earlier = (lax.broadcasted_iota(jnp.int32, (tm, tm), 0) < lax.broadcasted_iota(jnp.int32, (tm, tm), 1))
        earlier_scr[...] = jnp.where(earlier, 1.0, 0.0).astype(BF16)

    h = _rms_mod(x_ref[...], g2_ref[...], mod_ref[0, 3:4, :], mod_ref[0, 4:5, :])
    half = D_MODEL // 2
    _store_token_words(hp_ref, _pack_pair(h[:, :half], h[:, half:]), tm)

    def split(a):
        hi = a.astype(BF16)
        return hi, (a - hi.astype(F32)).astype(BF16)

    def dot_nt(a, b):
        return lax.dot_general(a, b, (((1,), (1,)), ((), ())), preferred_element_type=F32)

    h_hi, h_lo = split(h)
    w_hi, w_lo = split(wrt_ref[...])
    logits_t = dot_nt(w_hi, h_hi) + (dot_nt(w_hi, h_lo) + dot_nt(w_lo, h_hi))
    scores = jax.nn.sigmoid(logits_t)
    comb_t, sel, picks = _route(scores, scores + rb_ref[...])

    rank_t = _dot(sel.astype(BF16), earlier_scr[...]) + run_scr[...]
    run_scr[...] += jnp.sum(sel, axis=1, keepdims=True)
    cnt_ref[...] = jnp.broadcast_to(run_scr[...], cnt_ref.shape)

    iota_e = lax.broadcasted_iota(jnp.int32, sel.shape, 0).astype(F32)
    ranks, weights = [], []
    for idx in picks:
        hit = iota_e == idx
        ranks.append(jnp.sum(jnp.where(hit, rank_t, 0.0), axis=0, keepdims=True))
        weights.append(jnp.sum(jnp.where(hit, comb_t, 0.0), axis=0, keepdims=True))
    ek_ref[...] = jnp.concatenate(picks, axis=0).astype(jnp.int32)
    rk_ref[...] = jnp.concatenate(ranks, axis=0).astype(jnp.int32)
    w_rep = jnp.concatenate([jnp.broadcast_to(w, (SC_LANES, tm)) for w in weights], axis=0)
    wt_ref[...] = w_rep.T


def _router(x1, mod3, norm2_g, w_router_t, router_bias, seq_len, mod_row_of_batch):
    t = x1.shape[0]
    tm = TM_ROUTER

    def mod_idx(i):
        return (mod_row_of_batch((i * tm) // seq_len), 0, 0)

    def full(a):
        return pl.BlockSpec(a.shape, lambda i: (0,) * a.ndim)

    return pl.pallas_call(
        _router_kernel,
        grid=(t // tm,),
        in_specs=[pl.BlockSpec((tm, D_MODEL), lambda i: (i, 0)),
                  pl.BlockSpec((1, 6, D_MODEL), mod_idx),
                  full(norm2_g), full(w_router_t), full(router_bias)],
        out_specs=[pl.BlockSpec((tm * ROW_SLABS, 128), lambda i: (i, 0)),
                   pl.BlockSpec((TOP_K, tm), lambda i: (0, i)),
                   pl.BlockSpec((TOP_K, tm), lambda i: (0, i)),
                   pl.BlockSpec((tm, 128), lambda i: (i, 0)),
                   pl.BlockSpec((N_EXPERTS, 128), lambda i: (0, 0))],
        out_shape=[jax.ShapeDtypeStruct((t * ROW_SLABS, 128), jnp.int32),
                   jax.ShapeDtypeStruct((TOP_K, t), jnp.int32),
                   jax.ShapeDtypeStruct((TOP_K, t), jnp.int32),
                   jax.ShapeDtypeStruct((t, 128), F32),
                   jax.ShapeDtypeStruct((N_EXPERTS, 128), F32)],
        scratch_shapes=[pltpu.VMEM((N_EXPERTS, 1), F32), pltpu.VMEM((tm, tm), BF16)],
        compiler_params=pltpu.CompilerParams(dimension_semantics=("arbitrary",),
                                             vmem_limit_bytes=VMEM_LIMIT),
        name="router",
    )(x1, mod3, norm2_g, w_router_t, router_bias)


def _plan_kernel(ek_ref, rk_ref, cnt_ref, pos_ref, texp_ref, nused_ref, tend_ref, *, expert_rows):
    rows = float(expert_rows)
    cnt = cnt_ref[:, 0:1]
    tiles = jnp.floor((cnt + (rows - 1.0)) / rows)
    before = (lax.broadcasted_iota(jnp.int32, (N_EXPERTS, N_EXPERTS), 1)
              < lax.broadcasted_iota(jnp.int32, (N_EXPERTS, N_EXPERTS), 0))
    tile_start = jnp.dot(jnp.where(before, 1.0, 0.0), jnp.broadcast_to(tiles, (N_EXPERTS, 128)),
                         precision=lax.Precision.HIGHEST, preferred_element_type=F32)[:, 0:1]
    tile_end = tile_start + tiles
    row_start = tile_start * rows

    ek = ek_ref[...]
    pos = rk_ref[...].astype(F32)
    tile_id = lax.broadcasted_iota(jnp.int32, texp_ref.shape, 1).astype(F32)
    texp = jnp.zeros(texp_ref.shape, F32)
    for e in range(N_EXPERTS):
        pos = pos + jnp.where(ek == e, row_start[e:e + 1, :], 0.0)
        texp = texp + jnp.where(tile_id >= tile_end[e:e + 1, :], 1.0, 0.0)
    pos_ref[...] = pos.astype(jnp.int32)
    texp_ref[...] = jnp.minimum(texp, N_EXPERTS - 1.0).astype(jnp.int32)
    nused_ref[...] = jnp.broadcast_to(tile_end[N_EXPERTS - 1:N_EXPERTS, :], nused_ref.shape).astype(jnp.int32)
    tend_ref[...] = jnp.broadcast_to(tile_end, tend_ref.shape).astype(jnp.int32)


def _plan(ek, rk, cnt, n_tiles_pad, expert_rows):
    t = ek.shape[1]

    def full(shape):
        return pl.BlockSpec(shape, lambda: (0,) * len(shape))

    return pl.pallas_call(
        functools.partial(_plan_kernel, expert_rows=expert_rows),
        in_specs=[full(ek.shape), full(rk.shape), full(cnt.shape)],
        out_specs=[full((TOP_K, t)), full((1, n_tiles_pad)), full((1, 128)), full((N_EXPERTS, 128))],
        out_shape=[jax.ShapeDtypeStruct((TOP_K, t), jnp.int32),
                   jax.ShapeDtypeStruct((1, n_tiles_pad), jnp.int32),
                   jax.ShapeDtypeStruct((1, 128), jnp.int32),
                   jax.ShapeDtypeStruct((N_EXPERTS, 128), jnp.int32)],
        compiler_params=pltpu.CompilerParams(vmem_limit_bytes=VMEM_LIMIT),
        name="plan",
    )(ek, rk, cnt)


def _sc_mesh():
    return plsc.VectorSubcoreMesh(core_axis_name="c", subcore_axis_name="s")


def _sc_pack_weight_halves(w):
    e, k, n = w.shape
    k_half = k // 2
    rb = SC_PACK_BLOCK_WORDS // n
    units_per_expert = k_half // rb
    per_w = (e * units_per_expert) // SC_WORKERS
    lanes = SC_LANES

    @functools.partial(
        pl.kernel, out_type=jax.ShapeDtypeStruct((e * k_half, n), jnp.int32), mesh=_sc_mesh(),
        scratch_types=[pltpu.VMEM((rb, n), F32), pltpu.VMEM((rb, n), F32), pltpu.VMEM((rb, n), jnp.int32)],
        compiler_params=pltpu.CompilerParams(needs_layout_passes=False))
    def kern(w_hbm, out_hbm, a_v, b_v, o_v):
        wid = lax.axis_index("s") * SC_CORES + lax.axis_index("c")

        @pl.loop(0, per_w)
        def _(j):
            unit = wid * per_w + j
            expert = unit // units_per_expert
            blk = unit % units_per_expert
            row_a = expert * k + blk * rb
            pltpu.sync_copy(w_hbm.at[pl.ds(row_a, rb)], a_v)
            pltpu.sync_copy(w_hbm.at[pl.ds(row_a + k_half, rb)], b_v)

            @pl.loop(0, rb)
            def _(r):
                @plsc.parallel_loop(0, n, step=lanes, unroll=4)
                def _(c):
                    both = plsc.pack(a_v[r, pl.ds(c, lanes)], b_v[r, pl.ds(c, lanes)],
                                     format=plsc.PackFormat.INTERLEAVED)
                    o_v[r, pl.ds(c, lanes)] = plsc.bitcast(both, jnp.int32)

            pltpu.sync_copy(o_v, out_hbm.at[pl.ds(expert * k_half + blk * rb, rb)])

    return kern(w.reshape(e * k, n)).reshape(e, k_half, n)


def _sc_dispatch(rows, pos3, n_out, after=()):
    t = rows.shape[0]
    ch = SC_CHUNK
    per_w = (t // ch) // SC_WORKERS

    @functools.partial(
        pl.kernel, out_type=jax.ShapeDtypeStruct((n_out,) + rows.shape[1:], jnp.int32), mesh=_sc_mesh(),
        scratch_types=[pltpu.VMEM((TOP_K, ch), jnp.int32), pltpu.VMEM((ch,) + rows.shape[1:], jnp.int32),
                       pltpu.SemaphoreType.DMA])
    def k(rows_hbm, pos_hbm, *rest):
        out_hbm, idx_v, rows_v, sem = rest[len(after):]
        wid = lax.axis_index("s") * SC_CORES + lax.axis_index("c")

        @pl.loop(0, per_w)
        def _(j):
            c = wid * per_w + j
            pltpu.sync_copy(pos_hbm.at[c], idx_v)
            pltpu.sync_copy(rows_hbm.at[pl.ds(c * ch, ch)], rows_v)
            copies = [pltpu.async_copy(rows_v, out_hbm.at[idx_v.at[kk]], sem) for kk in range(TOP_K)]
            for cp in copies:
                cp.wait()

    return k(rows, pos3, *after)


def _sc_combine(table, pos3, wtok, t):
    ch = SC_CHUNK
    sub = SC_COMBINE_TOKENS
    lanes = SC_LANES
    slabs = ROW_SLABS
    per_w = (t // ch) // SC_WORKERS
    subs_per_chunk = ch // sub
    n_steps = per_w * subs_per_chunk

    @functools.partial(
        pl.kernel, out_type=jax.ShapeDtypeStruct((t, slabs, 128), jnp.int32), mesh=_sc_mesh(),
        scratch_types=[pltpu.VMEM((per_w, TOP_K, ch), jnp.int32),
                       pltpu.VMEM((2, TOP_K, sub, slabs, 128), jnp.int32),
                       pltpu.VMEM((2, sub, 128), F32),
                       pltpu.VMEM((sub, slabs, 128), jnp.int32),
                       pltpu.SemaphoreType.DMA((2,))],
        compiler_params=pltpu.CompilerParams(needs_layout_passes=False))
    def k(tab_hbm, pos_hbm, w_hbm, out_hbm, idx_v, rows_v, w_v, out_v, sem):
        wid = lax.axis_index("s") * SC_CORES + lax.axis_index("c")
        for j in range(per_w):
            pltpu.sync_copy(pos_hbm.at[wid * per_w + j], idx_v.at[j])

        def first_token(step):
            return (wid * per_w + step // subs_per_chunk) * ch + (step % subs_per_chunk) * sub

        def copies(step, slot):
            j = step // subs_per_chunk
            s = step % subs_per_chunk
            idx = [idx_v.at[j, kk, pl.ds(s * sub, sub)] for kk in range(TOP_K)]
            return ([pltpu.make_async_copy(tab_hbm.at[idx[kk]], rows_v.at[slot, kk], sem.at[slot])
                     for kk in range(TOP_K)]
                    + [pltpu.make_async_copy(w_hbm.at[pl.ds(first_token(step), sub)], w_v.at[slot], sem.at[slot])])

        for cp in copies(0, 0):
            cp.start()

        @pl.loop(0, n_steps)
        def _(step):
            slot = step % 2

            @pl.when(step + 1 < n_steps)
            def _():
                for cp in copies(step + 1, 1 - slot):
                    cp.start()

            for cp in copies(step, slot):
                cp.wait()

            @pl.loop(0, sub)
            def _(tt):
                wk = [w_v[slot, tt, pl.ds(kk * lanes, lanes)] for kk in range(TOP_K)]
                for sl in range(slabs):
                    @plsc.parallel_loop(0, 128, step=lanes, unroll=4)
                    def _(off):
                        acc_lo = jnp.zeros((lanes,), F32)
                        acc_hi = jnp.zeros((lanes,), F32)
                        for kk in range(TOP_K):
                            word = rows_v[slot, kk, tt, sl, pl.ds(off, lanes)]
                            lo = plsc.bitcast(word << 16, F32)
                            hi = plsc.bitcast(word & jnp.int32(-65536), F32)
                            acc_lo = acc_lo + wk[kk] * lo
                            acc_hi = acc_hi + wk[kk] * hi
                        both = plsc.pack(acc_lo, acc_hi, format=plsc.PackFormat.INTERLEAVED)
                        out_v[tt, sl, pl.ds(off, lanes)] = plsc.bitcast(both, jnp.int32)

            pltpu.sync_copy(out_v, out_hbm.at[pl.ds(first_token(step), sub)])

    return k(table, pos3, wtok)


def _experts_kernel(texp_ref, nused_ref, tend_ref, xs_ref, weg_hbm, weu_hbm, wed_hbm, ys_ref,
                    wg_scr, wu_scr, wd_scr, wg_buf, wu_buf, wd_buf, sem, group_scr, *, expert_rows):
    step = pl.program_id(0)
    rows = expert_rows
    tiles_per_step = EXPERT_STEP_ROWS // expert_rows
    half = D_MODEL // 2
    n_used = nused_ref[0]

    def weight_copies(e, slot):
        return [pltpu.make_async_copy(weg_hbm.at[e], wg_buf.at[slot], sem.at[slot, 0]),
                pltpu.make_async_copy(weu_hbm.at[e], wu_buf.at[slot], sem.at[slot, 1]),
                pltpu.make_async_copy(wed_hbm.at[e], wd_buf.at[slot], sem.at[slot, 2])]

    def next_group(e):
        tile = tend_ref[e]
        return texp_ref[jnp.minimum(tile, n_used - 1)], tile < n_used

    def start_weights(e, slot, exists):
        @pl.when(exists)
        def _():
            for cp in weight_copies(e, slot):
                cp.start()

    @pl.when(step == 0)
    def _():
        group_scr[0] = 0
        e, exists = texp_ref[0], True
        for slot in range(WEIGHT_SLOTS - 1):
            start_weights(e, slot, exists)
            nxt, has_next = next_group(e)
            e, exists = nxt, exists & has_next

    def row_tile(tile, x_view, y_view):
        expert = texp_ref[tile]
        used = tile < n_used
        new_expert = (tile == 0) | (expert != texp_ref[jnp.maximum(tile - 1, 0)])

        @pl.when(used & new_expert)
        def _():
            group = group_scr[0]
            slot = group % WEIGHT_SLOTS
            ahead, exists = expert, True
            for _ in range(WEIGHT_SLOTS - 1):
                nxt, has_next = next_group(ahead)
                ahead, exists = nxt, exists & has_next
            start_weights(ahead, (group + WEIGHT_SLOTS - 1) % WEIGHT_SLOTS, exists)

            for cp in weight_copies(expert, slot):
                cp.wait()
            for scr, buf in ((wg_scr, wg_buf), (wu_scr, wu_buf), (wd_scr, wd_buf)):
                top, bottom = _unpack_pair(buf[slot])
                k_half = top.shape[0]
                scr[0:k_half, :] = top.astype(BF16)
                scr[k_half:2 * k_half, :] = bottom.astype(BF16)
            group_scr[0] = group + 1

        @pl.when(used)
        def _():
            lo, hi = _unpack_pair(_load_token_words(x_view, (), rows))
            lo = lo.astype(BF16)
            hi = hi.astype(BF16)
            g = _dot(lo, wg_scr[0:half, :]) + _dot(hi, wg_scr[half:D_MODEL, :])
            u = _dot(lo, wu_scr[0:half, :]) + _dot(hi, wu_scr[half:D_MODEL, :])
            y = _dot((_silu(g) * u).astype(BF16), wd_scr[...])
            _store_token_words(y_view, _pack_pair(y[:, :half], y[:, half:]), rows)

        @pl.when(jnp.logical_not(used) & (step == (n_used - 1) // tiles_per_step))
        def _():
            y_view[...] = jnp.zeros_like(y_view)

    for s in range(tiles_per_step):
        view = pl.ds(s * rows * ROW_SLABS, rows * ROW_SLABS)
        row_tile(step * tiles_per_step + s, xs_ref.at[view], ys_ref.at[view])


def _experts(texp, nused, tend, xs2d, weg, weu, wed, n_tiles, expert_rows):
    tiles_per_step = EXPERT_STEP_ROWS // expert_rows
    block = (EXPERT_STEP_ROWS * ROW_SLABS, 128)
    hbm = pl.BlockSpec(memory_space=pl.ANY)

    def block_idx(j, te, nu, tn):
        return (jnp.minimum(j, (nu[0] - 1) // tiles_per_step), 0)

    grid_spec = pltpu.PrefetchScalarGridSpec(
        num_scalar_prefetch=3,
        grid=(n_tiles // tiles_per_step,),
        in_specs=[pl.BlockSpec(block, block_idx), hbm, hbm, hbm],
        out_specs=pl.BlockSpec(block, block_idx),
        scratch_shapes=[pltpu.VMEM((D_MODEL, EXPERT_DIM), BF16),
                        pltpu.VMEM((D_MODEL, EXPERT_DIM), BF16),
                        pltpu.VMEM((EXPERT_DIM, D_MODEL), BF16),
                        pltpu.VMEM((WEIGHT_SLOTS,) + weg.shape[1:], jnp.int32),
                        pltpu.VMEM((WEIGHT_SLOTS,) + weu.shape[1:], jnp.int32),
                        pltpu.VMEM((WEIGHT_SLOTS,) + wed.shape[1:], jnp.int32),
                        pltpu.SemaphoreType.DMA((WEIGHT_SLOTS, 3)),
                        pltpu.SMEM((1,), jnp.int32)],
    )
    return pl.pallas_call(
        functools.partial(_experts_kernel, expert_rows=expert_rows),
        grid_spec=grid_spec,
        out_shape=jax.ShapeDtypeStruct(xs2d.shape, jnp.int32),
        compiler_params=pltpu.CompilerParams(dimension_semantics=("arbitrary",),
                                             vmem_limit_bytes=VMEM_LIMIT),
        name="experts",
    )(texp, nused, tend, xs2d, weg, weu, wed)


def _final_kernel(x_ref, routed_ref, mod_ref, g2_ref, wsg_ref, wsu_ref, wsd_ref, fng_ref, o_ref):
    tm = x_ref.shape[0]
    x = x_ref[...]
    hb = _rms_mod(x, g2_ref[...], mod_ref[0, 3:4, :], mod_ref[0, 4:5, :]).astype(BF16)
    shared = _dot((_silu(_dot(hb, wsg_ref[...])) * _dot(hb, wsu_ref[...])).astype(BF16), wsd_ref[...])
    routed = jnp.concatenate(_unpack_pair(_load_token_words(routed_ref, (), tm)), axis=1)
    y = x + mod_ref[0, 5:6, :] * (routed + shared)
    ms = jnp.mean(y * y, axis=-1, keepdims=True)
    o_ref[...] = y * lax.rsqrt(ms + EPS) * fng_ref[...]


def _final(x1, routed2d, mod3, norm2_g, wsg, wsu, wsd, final_g, seq_len, mod_row_of_batch):
    t = x1.shape[0]
    tm = TM_FINAL

    def mod_idx(i):
        return (mod_row_of_batch((i * tm) // seq_len), 0, 0)

    def full(a):
        return pl.BlockSpec(a.shape, lambda i: (0,) * a.ndim)

    return pl.pallas_call(
        _final_kernel,
        grid=(t // tm,),
        in_specs=[pl.BlockSpec((tm, D_MODEL), lambda i: (i, 0)),
                  pl.BlockSpec((tm * ROW_SLABS, 128), lambda i: (i, 0)),
                  pl.BlockSpec((1, 6, D_MODEL), mod_idx),
                  full(norm2_g), full(wsg), full(wsu), full(wsd), full(final_g)],
        out_specs=pl.BlockSpec((tm, D_MODEL), lambda i: (i, 0)),
        out_shape=jax.ShapeDtypeStruct((t, D_MODEL), F32),
        compiler_params=pltpu.CompilerParams(dimension_semantics=("parallel",),
                                             vmem_limit_bytes=VMEM_LIMIT),
        name="final",
    )(x1, routed2d, mod3, norm2_g, wsg, wsu, wsd, final_g)


def _moe(x1, mod3, lw, seq_len, mod_row_of_batch):
    t = x1.shape[0]
    expert_rows = min(MAX_EXPERT_ROWS, TOP_K * t // N_EXPERTS // 2)
    n_tiles = TOP_K * t // expert_rows + N_EXPERTS
    n_tiles_pad = -(-n_tiles // 128) * 128
    hp2d, ek, rk, wtok, cnt = _router(x1, mod3, lw["norm2_g"], lw["w_router_t"], lw["router_bias"],
                                      seq_len, mod_row_of_batch)
    pos, texp, nused, tend = _plan(ek, rk, cnt, n_tiles_pad, expert_rows)
    pos3 = pos.reshape(TOP_K, t // SC_CHUNK, SC_CHUNK).transpose(1, 0, 2)
    xs = _sc_dispatch(hp2d.reshape(t, ROW_SLABS, 128), pos3, n_tiles * expert_rows,
                      after=(lw["weg"], lw["weu"], lw["wed"]))
    ys2d = _experts(texp.reshape(-1), nused.reshape(-1), tend[:, 0], xs.reshape(-1, 128),
                    lw["weg"], lw["weu"], lw["wed"], n_tiles, expert_rows)
    routed = _sc_combine(ys2d.reshape(-1, ROW_SLABS, 128), pos3, wtok, t)
    return _final(x1, routed.reshape(t * ROW_SLABS, 128), mod3, lw["norm2_g"],
                  lw["wsg"], lw["wsu"], lw["wsd"], lw["final_g"], seq_len, mod_row_of_batch)


def _dft_tables(seq_len):
    gd = FOURIER_GROUP_DIM
    kc = np.arange(gd)
    ang_c = ((kc[:, None] * kc[None, :]) % gd) * (2.0 * math.pi / gd)
    cs = np.concatenate([np.cos(ang_c), np.sin(ang_c)], axis=1) * (gd ** -0.5)
    kl = np.arange(seq_len)
    ang_l = ((kl[:, None] * kl[None, :]) % seq_len) * (2.0 * math.pi / seq_len)
    cls = np.concatenate([np.cos(ang_l), -np.sin(ang_l)], axis=1) * (seq_len ** -0.5)
    return jnp.asarray(cs.astype(np.float32), dtype=BF16), jnp.asarray(cls.astype(np.float32), dtype=BF16)


def _rope_tables(length):
    rows = length // GRID_W
    r = np.repeat(np.arange(rows, dtype=np.float32), GRID_W)
    col = np.tile(np.arange(GRID_W, dtype=np.float32), rows)
    nf = RET_HEAD_DIM // 4
    inv = (np.float32(ROPE_BASE) ** (-np.arange(nf, dtype=np.float32) / np.float32(nf))).astype(np.float32)
    ar = r[:, None] * inv[None]
    ac = col[:, None] * inv[None]
    ang = np.concatenate([ar, ar, ac, ac], axis=-1).astype(np.float64)
    sign = np.where((np.arange(RET_HEAD_DIM) & nf) == 0, -1.0, 1.0)
    return (jnp.asarray(np.cos(ang).astype(np.float32)),
            jnp.asarray((np.sin(ang) * sign[None, :]).astype(np.float32)))


def _trunk_path(x, proj, row0, mod3, mod_row_of_batch, s0f, s0b, lw):
    batch, seq_len, _ = x.shape
    x2d = x.reshape(batch * seq_len, D_MODEL)
    uf, q, k, v, sg, gf, gr = proj
    r, s_f, s_b = _retention(q, k, v, sg, lw["dec"], lw["gn_g"], s0f, s0b, batch, seq_len, row0)
    cs, cls = _dft_tables(seq_len)
    x1 = _fnet_merge(uf, cs, cls, r, gf, gr, x2d, mod3, lw["w_four"], lw["w_ret"], lw["w_o"],
                     batch, seq_len, mod_row_of_batch, row0)
    y = _moe(x1, mod3, lw, seq_len, mod_row_of_batch)
    return y.reshape(batch, seq_len, D_MODEL), s_f, s_b


def kernel(x_prompt, x_sample, state_ret_fwd, state_ret_bwd, c, c_ctx, w_ada, b_ada, norm1_g, norm2_g, w_in,
           ret_decay_fwd, ret_decay_bwd, ret_gn_g, w_four_out, w_ret_out, w_out, w_router, router_bias,
           w_exp_gate, w_exp_up, w_exp_down, w_shared_gate, w_shared_up, w_shared_down, final_norm_g):
    depth = w_ada.shape[0]
    assert depth == 1, "final norm is fused into the last layer's MoE kernel"
    n_ctx, n_lat = x_prompt.shape[0], x_sample.shape[0]
    cond = jnp.concatenate([c_ctx[None, :], c], axis=0)
    cond = jnp.pad(cond, ((0, (-cond.shape[0]) % 8), (0, 0)))
    rope = _rope_tables(x_sample.shape[1])
    zeros = jnp.zeros((n_ctx, N_RET_HEADS, RET_HEAD_DIM, RET_HEAD_DIM), F32)

    layer = 0
    mod = _ada(cond, w_ada[layer], b_ada[layer][None, :])
    mod3 = mod.reshape(mod.shape[0], 6, D_MODEL)
    dec = jnp.stack([ret_decay_fwd[layer], ret_decay_bwd[layer]], axis=1)
    lw = {
        "norm1_g": norm1_g[layer][None, :],
        "norm2_g": norm2_g[layer][None, :],
        "w_in": w_in[layer],
        "dec": jnp.broadcast_to(dec[:, :, None], (N_RET_HEADS, 2, RET_HEAD_DIM)).astype(F32),
        "gn_g": ret_gn_g[layer][None, :],
        "w_four": w_four_out[layer],
        "w_ret": w_ret_out[layer],
        "w_o": w_out[layer],
        "w_router_t": w_router[layer].T,
        "router_bias": router_bias[layer][:, None],
        "weg": _sc_pack_weight_halves(w_exp_gate[layer]),
        "weu": _sc_pack_weight_halves(w_exp_up[layer]),
        "wed": _sc_pack_weight_halves(w_exp_down[layer]),
        "wsg": w_shared_gate[layer].astype(BF16),
        "wsu": w_shared_up[layer].astype(BF16),
        "wsd": w_shared_down[layer].astype(BF16),
        "final_g": final_norm_g[None, :],
    }
    n_lat_tokens = x_sample.shape[0] * x_sample.shape[1]
    proj = _inproj(x_sample.reshape(n_lat_tokens, D_MODEL), x_prompt.reshape(-1, D_MODEL), x_sample.shape[1],
                   mod3, lw["norm1_g"], lw["w_in"], rope)
    y_prompt, s_f, s_b = _trunk_path(x_prompt, proj, n_lat_tokens, mod3, lambda b: 0, zeros, zeros, lw)
    y_sample, _, _ = _trunk_path(x_sample, proj, 0, mod3, lambda b: 1 + b, state_ret_fwd[:, layer],
                                 state_ret_bwd[:, layer], lw)
    return (y_prompt, y_sample, s_f[:, None], s_b[:, None])
```

```python
import functools
import math

import jax
import jax.numpy as jnp
import numpy as np
from jax import lax
from jax.experimental import pallas as pl
from jax.experimental.pallas import tpu as pltpu
from jax.experimental.pallas import tpu_sc as plsc

F32 = jnp.float32
BF16 = jnp.bfloat16

D_MODEL = 1024
GRID_W = 64
N_FOURIER_GROUPS = 8
FOURIER_GROUP_DIM = 128
N_RET_HEADS = 4
RET_HEAD_DIM = 128
RET_WIDTH = N_RET_HEADS * RET_HEAD_DIM
CHUNK = 128
N_EXPERTS = 64
N_EXPERT_GROUPS = 8
EXPERTS_PER_GROUP = N_EXPERTS // N_EXPERT_GROUPS
TOPK_GROUPS = 4
TOP_K = 8
EXPERT_DIM = 256
ROUTED_SCALE = 2.5
ROPE_BASE = 10000.0
EPS = 1e-6
Q_SCALE = RET_HEAD_DIM ** -0.5

_C_UF = (0, 1024)
_C_Q = (1024, 1536)
_C_K = (1536, 2048)
_C_V = (2048, 2560)
_C_G = (2560, 3072)
_C_GF = (3072, 4096)
_C_GR = (4096, 5120)

VMEM_LIMIT = 56 * 1024 * 1024

TM_INPROJ = 512
TM_ROUTER = 1024
FNET_ROWS = 512
TM_FINAL = 1024
EXPERT_STEP_ROWS = 1024
MAX_EXPERT_ROWS = 512
WEIGHT_SLOTS = 3
ROW_SLABS = 4
SC_CORES = 2
SC_WORKERS = 32
SC_CHUNK = 128
SC_LANES = 16
SC_PACK_BLOCK_WORDS = 16384
SC_COMBINE_TOKENS = 8


def _silu(x):
    return x * jax.nn.sigmoid(x)


def _dot(a, b):
    return jnp.dot(a, b, preferred_element_type=F32)


def _rms_mod(x, g, shift, scale):
    ms = jnp.mean(x * x, axis=-1, keepdims=True)
    y = x * lax.rsqrt(ms + EPS) * g
    return y * (1.0 + scale) + shift


def _ada_kernel(cond_ref, w_ref, b_ref, o_ref):
    s = _silu(cond_ref[...]).astype(BF16)
    o_ref[...] = _dot(s, w_ref[...].astype(BF16)) + b_ref[...]


def _ada(cond, w_ada, b_ada):
    rows, n = cond.shape[0], w_ada.shape[1]
    tn = 1536
    return pl.pallas_call(
        _ada_kernel,
        grid=(n // tn,),
        in_specs=[pl.BlockSpec((rows, D_MODEL), lambda j: (0, 0)),
                  pl.BlockSpec((D_MODEL, tn), lambda j: (0, j)),
                  pl.BlockSpec((1, tn), lambda j: (0, j))],
        out_specs=pl.BlockSpec((rows, tn), lambda j: (0, j)),
        out_shape=jax.ShapeDtypeStruct((rows, n), F32),
        compiler_params=pltpu.CompilerParams(vmem_limit_bytes=VMEM_LIMIT),
        name="ada",
    )(cond, w_ada, b_ada)


def _rope_head(x, cos, sin_signed, first_half):
    partner = jnp.where(first_half, pltpu.roll(x, 96, 1), pltpu.roll(x, 32, 1))
    return x * cos + partner * sin_signed


def _inproj_kernel(xl_ref, xc_ref, mod_ref, g_ref, w_ref, cos_ref, sin_ref,
                   uf_o, q_o, k_o, v_o, sg_o, gf_o, gr_o, *, n_latent_steps):
    x = jnp.where(pl.program_id(0) < n_latent_steps, xl_ref[...], xc_ref[...])
    h = _rms_mod(x, g_ref[...], mod_ref[0, 0:1, :], mod_ref[0, 1:2, :])
    hb = h.astype(BF16)

    def proj(cols):
        return _dot(hb, w_ref[:, cols[0]:cols[1]].astype(BF16))

    uf_o[...] = proj(_C_UF).astype(BF16)
    q = proj(_C_Q)
    k = proj(_C_K)
    cos = cos_ref[...]
    sin_signed = sin_ref[...]
    lane = lax.broadcasted_iota(jnp.int32, cos.shape, 1)
    first_half = (lane & 32) == 0
    for hd in range(N_RET_HEADS):
        sl = slice(hd * RET_HEAD_DIM, (hd + 1) * RET_HEAD_DIM)
        q_o[:, sl] = (_rope_head(q[:, sl], cos, sin_signed, first_half) * Q_SCALE).astype(BF16)
        k_o[:, sl] = _rope_head(k[:, sl], cos, sin_signed, first_half).astype(BF16)
    v_o[...] = proj(_C_V).astype(BF16)
    sg_o[...] = _silu(proj(_C_G)).astype(BF16)
    gf_o[...] = jax.nn.sigmoid(proj(_C_GF)).astype(BF16)
    gr_o[...] = jax.nn.sigmoid(proj(_C_GR)).astype(BF16)


def _inproj(x_lat2d, x_ctx2d, lat_seq_len, mod3, norm_g, w_in_f32, rope):
    tm = TM_INPROJ
    assert lat_seq_len % tm == 0 and x_ctx2d.shape[0] % tm == 0
    tiles_per_seq = lat_seq_len // tm
    n_lat = x_lat2d.shape[0] // tm
    n_ctx = x_ctx2d.shape[0] // tm
    t = (n_lat + n_ctx) * tm
    cos, sin_signed = rope
    cos2 = jnp.concatenate([cos, jnp.ones((tm, RET_HEAD_DIM), F32)], axis=0)
    sin2 = jnp.concatenate([sin_signed, jnp.zeros((tm, RET_HEAD_DIM), F32)], axis=0)
    rope_spec = pl.BlockSpec((tm, RET_HEAD_DIM),
                             lambda i: (jnp.where(i < n_lat, i % tiles_per_seq, tiles_per_seq), 0))
    widths = [1024, RET_WIDTH, RET_WIDTH, RET_WIDTH, RET_WIDTH, 1024, 1024]
    return pl.pallas_call(
        functools.partial(_inproj_kernel, n_latent_steps=n_lat),
        grid=(n_lat + n_ctx,),
        in_specs=[pl.BlockSpec((tm, D_MODEL), lambda i: (jnp.minimum(i, n_lat - 1), 0)),
                  pl.BlockSpec((tm, D_MODEL), lambda i: (jnp.maximum(i - n_lat, 0), 0)),
                  pl.BlockSpec((1, 6, D_MODEL), lambda i: (jnp.where(i < n_lat, 1 + i // tiles_per_seq, 0), 0, 0)),
                  pl.BlockSpec((1, D_MODEL), lambda i: (0, 0)),
                  pl.BlockSpec(w_in_f32.shape, lambda i: (0, 0), pipeline_mode=pl.Buffered(1)),
                  rope_spec, rope_spec],
        out_specs=[pl.BlockSpec((tm, w), lambda i: (i, 0)) for w in widths],
        out_shape=[jax.ShapeDtypeStruct((t, w), BF16) for w in widths],
        compiler_params=pltpu.CompilerParams(dimension_semantics=("parallel",),
                                             vmem_limit_bytes=VMEM_LIMIT),
        name="inproj",
    )(x_lat2d, x_ctx2d, mod3, norm_g, w_in_f32, cos2, sin2)


def _retention_kernel(q_ref, k_ref, v_ref, sg_ref, dec_ref, gn_ref, s0f_ref, s0b_ref, *rest):
    r_ref, sfo_ref, sbo_ref, tab_scr, gc_scr = rest[-5:]
    n_chunks = q_ref.shape[0] // CHUNK
    hd = RET_HEAD_DIM

    @pl.when(pl.program_id(0) == 0)
    def _():
        row = lax.broadcasted_iota(jnp.int32, (CHUNK, CHUNK), 0).astype(F32)
        col = lax.broadcasted_iota(jnp.int32, (CHUNK, CHUNK), 1).astype(F32)
        diff = row - col
        for h in range(N_RET_HEADS):
            dec = dec_ref[h]
            lg = jnp.minimum(dec, 0.0) - jnp.log1p(jnp.exp(-jnp.abs(dec)))
            lgf = lg[0:1, :]
            lgb = lg[1:2, :]
            tab_scr[h, 0] = jnp.exp(jnp.where(diff >= 0, lgf * diff, lgb * (-diff)))
            tab_scr[h, 1] = jnp.exp(lgf * (row + 1.0))
            tab_scr[h, 2] = jnp.exp(lgb * (CHUNK - row))
            tab_scr[h, 3] = jnp.exp(lgf * (CHUNK - 1.0 - col))
            tab_scr[h, 4] = jnp.exp(lgb * col)
            gc_scr[h] = jnp.exp(lg * CHUNK)

    def rows(n):
        return slice(n * CHUNK, (n + 1) * CHUNK)

    for h in range(N_RET_HEADS):
        cols = slice(h * hd, (h + 1) * hd)
        decay, qw_f, qw_b, kwt_f, kwt_b = (tab_scr[h, i] for i in range(5))
        gc = gc_scr[h]
        gc_f = gc[0:1, :]
        gc_b = gc[1:2, :]

        kv_f, kv_b = [], []
        for n in range(n_chunks):
            kt = k_ref[rows(n), cols].astype(F32).T
            vn = v_ref[rows(n), cols]
            kv_f.append(_dot((kt * kwt_f).astype(BF16), vn))
            kv_b.append(_dot((kt * kwt_b).astype(BF16), vn))

        s = s0f_ref[h]
        prev_f = []
        for n in range(n_chunks):
            prev_f.append(s.astype(BF16))
            s = gc_f * s + kv_f[n]
        sfo_ref[h] = s
        s = s0b_ref[h]
        prev_b = [None] * n_chunks
        for n in reversed(range(n_chunks)):
            prev_b[n] = s.astype(BF16)
            s = gc_b * s + kv_b[n]
        sbo_ref[h] = s

        gn = gn_ref[:, cols]
        for n in range(n_chunks):
            qn = q_ref[rows(n), cols]
            qf = qn.astype(F32)
            scores = lax.dot_general(qn, k_ref[rows(n), cols], (((1,), (1,)), ((), ())),
                                     preferred_element_type=F32)
            o = _dot((scores * decay).astype(BF16), v_ref[rows(n), cols])
            o = o + _dot((qf * qw_f).astype(BF16), prev_f[n])
            o = o + _dot((qf * qw_b).astype(BF16), prev_b[n])
            mu = jnp.mean(o, axis=-1, keepdims=True)
            d = o - mu
            var = jnp.mean(d * d, axis=-1, keepdims=True)
            on = d * lax.rsqrt(var + EPS) * gn
            r_ref[rows(n), cols] = (on * sg_ref[rows(n), cols].astype(F32)).astype(BF16)


def _retention(q, k, v, sg, dec, gn_g, s0f, s0b, batch, seq_len, row0, after=()):
    hd = RET_HEAD_DIM
    first = row0 // seq_len
    in_spec = pl.BlockSpec((seq_len, RET_WIDTH), lambda b: (first + b, 0))
    tok_spec = pl.BlockSpec((seq_len, RET_WIDTH), lambda b: (b, 0))
    st_spec = pl.BlockSpec((None, N_RET_HEADS, hd, hd), lambda b: (b, 0, 0, 0))
    st_shape = jax.ShapeDtypeStruct((batch, N_RET_HEADS, hd, hd), F32)
    return pl.pallas_call(
        _retention_kernel,
        grid=(batch,),
        in_specs=[in_spec, in_spec, in_spec, in_spec,
                  pl.BlockSpec(dec.shape, lambda b: (0, 0, 0)),
                  pl.BlockSpec(gn_g.shape, lambda b: (0, 0)),
                  st_spec, st_spec] + [pl.BlockSpec(memory_space=pl.ANY)] * len(after),
        out_specs=[tok_spec, st_spec, st_spec],
        out_shape=[jax.ShapeDtypeStruct((batch * seq_len, RET_WIDTH), BF16), st_shape, st_shape],
        scratch_shapes=[pltpu.VMEM((N_RET_HEADS, 5, CHUNK, CHUNK), F32),
                        pltpu.VMEM((N_RET_HEADS, 2, hd), F32)],
        compiler_params=pltpu.CompilerParams(dimension_semantics=("arbitrary",),
                                             vmem_limit_bytes=VMEM_LIMIT),
        name="retention",
    )(q, k, v, sg, dec, gn_g, s0f, s0b, *after)


def _fnet_merge_kernel(uf_ref, cs_ref, cls_ref, r_ref, gf_ref, gr_ref, x_ref, mod_ref, wf_ref, wr_ref, wo_ref,
                       o_ref, xcs_ref):
    seq_len = uf_ref.shape[0]
    gd = FOURIER_GROUP_DIM

    @pl.when(pl.program_id(1) == 0)
    def _():
        for g in range(N_FOURIER_GROUPS):
            x = _dot(uf_ref[:, g * gd:(g + 1) * gd], cs_ref[...])
            xcs_ref[0:seq_len, g * gd:(g + 1) * gd] = x[:, :gd].astype(BF16)
            xcs_ref[seq_len:2 * seq_len, g * gd:(g + 1) * gd] = x[:, gd:].astype(BF16)

    f_mix = _dot(cls_ref[...], xcs_ref[...]).astype(BF16)
    f_out = _dot(f_mix, wf_ref[...].astype(BF16))
    r_out = _dot(r_ref[...], wr_ref[...].astype(BF16))
    merged = gf_ref[...].astype(F32) * f_out + gr_ref[...].astype(F32) * r_out
    mix = _dot(merged.astype(BF16), wo_ref[...].astype(BF16))
    o_ref[...] = x_ref[...] + mod_ref[0, 2:3, :] * mix


def _fnet_merge(uf, cs, cls, r, gf, gr, x2d, mod3, w_four, w_ret, w_o, batch, seq_len, mod_row_of_batch, row0):
    rb = min(FNET_ROWS, seq_len)
    nr = seq_len // rb

    def tok(w):
        return pl.BlockSpec((rb, w), lambda b, i: (b * nr + i, 0))

    def tok_joint(w):
        return pl.BlockSpec((rb, w), lambda b, i: (row0 // rb + b * nr + i, 0))

    def full(a):
        return pl.BlockSpec(a.shape, lambda b, i: (0, 0))

    def once(a):
        return pl.BlockSpec(a.shape, lambda b, i: (0, 0), pipeline_mode=pl.Buffered(1))

    return pl.pallas_call(
        _fnet_merge_kernel,
        grid=(batch, nr),
        in_specs=[pl.BlockSpec((seq_len, D_MODEL), lambda b, i: (row0 // seq_len + b, 0)),
                  full(cs),
                  pl.BlockSpec((rb, 2 * seq_len), lambda b, i: (i, 0)),
                  tok(RET_WIDTH), tok_joint(D_MODEL), tok_joint(D_MODEL), tok(D_MODEL),
                  pl.BlockSpec((1, 6, D_MODEL), lambda b, i: (mod_row_of_batch(b), 0, 0)),
                  once(w_four), once(w_ret), once(w_o)],
        out_specs=tok(D_MODEL),
        out_shape=jax.ShapeDtypeStruct((batch * seq_len, D_MODEL), F32),
        scratch_shapes=[pltpu.VMEM((2 * seq_len, D_MODEL), BF16)],
        compiler_params=pltpu.CompilerParams(dimension_semantics=("parallel", "arbitrary"),
                                             vmem_limit_bytes=VMEM_LIMIT),
        name="fnet_merge",
    )(uf, cs, cls, r, gf, gr, x2d, mod3, w_four, w_ret, w_o)


def _pack_pair(lo_f32, hi_f32):
    lo = lax.bitcast_convert_type(lo_f32.astype(BF16).astype(F32), jnp.uint32)
    hi = lax.bitcast_convert_type(hi_f32.astype(BF16).astype(F32), jnp.uint32)
    return lax.bitcast_convert_type((lo >> 16) | hi, jnp.int32)


def _unpack_pair(words_i32):
    w = lax.bitcast_convert_type(words_i32, jnp.uint32)
    lo = lax.bitcast_convert_type(w << 16, F32)
    hi = lax.bitcast_convert_type(w & jnp.uint32(0xFFFF0000), F32)
    return lo, hi


def _load_token_words(ref, lead, n_tok):
    parts = []
    for s in range(ROW_SLABS):
        idx = (pl.ds(s, n_tok, stride=ROW_SLABS), slice(None))
        parts.append(ref[lead + idx] if lead else ref[idx])
    return jnp.concatenate(parts, axis=1)


def _store_token_words(ref, words, n_tok):
    for s in range(ROW_SLABS):
        ref[pl.ds(s, n_tok, stride=ROW_SLABS), :] = words[:, s * 128:(s + 1) * 128]


def _route(scores, biased):
    tokens = scores.shape[1]
    neg = -jnp.inf
    epg = EXPERTS_PER_GROUP
    iota_g = lax.broadcasted_iota(jnp.int32, (epg, tokens), 0).astype(F32)

    def pick_first_max(cur, iota, size):
        m = jnp.max(cur, axis=0, keepdims=True)
        idx = jnp.min(jnp.where(cur == m, iota, float(size)), axis=0, keepdims=True)
        return m, idx, iota == idx

    group_scores = []
    for g in range(N_EXPERT_GROUPS):
        vals = biased[g * epg:(g + 1) * epg, :]
        m1, _, hit = pick_first_max(vals, iota_g, epg)
        m2 = jnp.max(jnp.where(hit, neg, vals), axis=0, keepdims=True)
        group_scores.append(m1 + m2)
    cur = jnp.concatenate(group_scores, axis=0)
    group_sel = jnp.zeros_like(cur)
    for _ in range(TOPK_GROUPS):
        _, _, hit = pick_first_max(cur, iota_g, N_EXPERT_GROUPS)
        group_sel = jnp.where(hit, 1.0, group_sel)
        cur = jnp.where(hit, neg, cur)
    masked = jnp.concatenate(
        [jnp.where(group_sel[g:g + 1, :] > 0.0, biased[g * epg:(g + 1) * epg, :], neg)
         for g in range(N_EXPERT_GROUPS)], axis=0)
    iota_e = lax.broadcasted_iota(jnp.int32, masked.shape, 0).astype(F32)
    sel = jnp.zeros_like(masked)
    cur = masked
    picks = []
    for _ in range(TOP_K):
        _, idx, hit = pick_first_max(cur, iota_e, N_EXPERTS)
        picks.append(idx)
        sel = jnp.where(hit, 1.0, sel)
        cur = jnp.where(hit, neg, cur)
    w = scores * sel
    return w / jnp.sum(w, axis=0, keepdims=True) * ROUTED_SCALE, sel, picks


def _router_kernel(x_ref, mod_ref, g2_ref, wrt_ref, rb_ref, hp_ref, ek_ref, rk_ref, wt_ref, cnt_ref,
                   run_scr, earlier_scr):
    tm = x_ref.shape[0]

    @pl.when(pl.program_id(0) == 0)
    def _():
        run_scr[...] = jnp.zeros_like(run_scr)
        earlier = (lax.broadcasted_iota(jnp.int32, (tm, tm), 0) < lax.broadcasted_iota(jnp.int32, (tm, tm), 1))
        earlier_scr[...] = jnp.where(earlier, 1.0, 0.0).astype(BF16)

    h = _rms_mod(x_ref[...], g2_ref[...], mod_ref[0, 3:4, :], mod_ref[0, 4:5, :])
    half = D_MODEL // 2
    _store_token_words(hp_ref, _pack_pair(h[:, :half], h[:, half:]), tm)

    def split(a):
        hi = a.astype(BF16)
        return hi, (a - hi.astype(F32)).astype(BF16)

    def dot_nt(a, b):
        return lax.dot_general(a, b, (((1,), (1,)), ((), ())), preferred_element_type=F32)

    h_hi, h_lo = split(h)
    w_hi, w_lo = split(wrt_ref[...])
    logits_t = dot_nt(w_hi, h_hi) + (dot_nt(w_hi, h_lo) + dot_nt(w_lo, h_hi))
    scores = jax.nn.sigmoid(logits_t)
    comb_t, sel, picks = _route(scores, scores + rb_ref[...])

    rank_t = _dot(sel.astype(BF16), earlier_scr[...]) + run_scr[...]
    run_scr[...] += jnp.sum(sel, axis=1, keepdims=True)
    cnt_ref[...] = jnp.broadcast_to(run_scr[...], cnt_ref.shape)

    iota_e = lax.broadcasted_iota(jnp.int32, sel.shape, 0).astype(F32)
    ranks, weights = [], []
    for idx in picks:
        hit = iota_e == idx
        ranks.append(jnp.sum(jnp.where(hit, rank_t, 0.0), axis=0, keepdims=True))
        weights.append(jnp.sum(jnp.where(hit, comb_t, 0.0), axis=0, keepdims=True))
    ek_ref[...] = jnp.concatenate(picks, axis=0).astype(jnp.int32)
    rk_ref[...] = jnp.concatenate(ranks, axis=0).astype(jnp.int32)
    w_rep = jnp.concatenate([jnp.broadcast_to(w, (SC_LANES, tm)) for w in weights], axis=0)
    wt_ref[...] = w_rep.T


def _router(x1, mod3, norm2_g, w_router_t, router_bias, seq_len, mod_row_of_batch):
    t = x1.shape[0]
    tm = TM_ROUTER

    def mod_idx(i):
        return (mod_row_of_batch((i * tm) // seq_len), 0, 0)

    def full(a):
        return pl.BlockSpec(a.shape, lambda i: (0,) * a.ndim)

    return pl.pallas_call(
        _router_kernel,
        grid=(t // tm,),
        in_specs=[pl.BlockSpec((tm, D_MODEL), lambda i: (i, 0)),
                  pl.BlockSpec((1, 6, D_MODEL), mod_idx),
                  full(norm2_g), full(w_router_t), full(router_bias)],
        out_specs=[pl.BlockSpec((tm * ROW_SLABS, 128), lambda i: (i, 0)),
                   pl.BlockSpec((TOP_K, tm), lambda i: (0, i)),
                   pl.BlockSpec((TOP_K, tm), lambda i: (0, i)),
                   pl.BlockSpec((tm, 128), lambda i: (i, 0)),
                   pl.BlockSpec((N_EXPERTS, 128), lambda i: (0, 0))],
        out_shape=[jax.ShapeDtypeStruct((t * ROW_SLABS, 128), jnp.int32),
                   jax.ShapeDtypeStruct((TOP_K, t), jnp.int32),
                   jax.ShapeDtypeStruct((TOP_K, t), jnp.int32),
                   jax.ShapeDtypeStruct((t, 128), F32),
                   jax.ShapeDtypeStruct((N_EXPERTS, 128), F32)],
        scratch_shapes=[pltpu.VMEM((N_EXPERTS, 1), F32), pltpu.VMEM((tm, tm), BF16)],
        compiler_params=pltpu.CompilerParams(dimension_semantics=("arbitrary",),
                                             vmem_limit_bytes=VMEM_LIMIT),
        name="router",
    )(x1, mod3, norm2_g, w_router_t, router_bias)


def _plan_kernel(ek_ref, rk_ref, cnt_ref, pos_ref, texp_ref, nused_ref, tend_ref, *, expert_rows):
    rows = float(expert_rows)
    cnt = cnt_ref[:, 0:1]
    tiles = jnp.floor((cnt + (rows - 1.0)) / rows)
    before = (lax.broadcasted_iota(jnp.int32, (N_EXPERTS, N_EXPERTS), 1)
              < lax.broadcasted_iota(jnp.int32, (N_EXPERTS, N_EXPERTS), 0))
    tile_start = jnp.dot(jnp.where(before, 1.0, 0.0), jnp.broadcast_to(tiles, (N_EXPERTS, 128)),
                         precision=lax.Precision.HIGHEST, preferred_element_type=F32)[:, 0:1]
    tile_end = tile_start + tiles
    row_start = tile_start * rows

    ek = ek_ref[...]
    pos = rk_ref[...].astype(F32)
    tile_id = lax.broadcasted_iota(jnp.int32, texp_ref.shape, 1).astype(F32)
    texp = jnp.zeros(texp_ref.shape, F32)
    for e in range(N_EXPERTS):
        pos = pos + jnp.where(ek == e, row_start[e:e + 1, :], 0.0)
        texp = texp + jnp.where(tile_id >= tile_end[e:e + 1, :], 1.0, 0.0)
    pos_ref[...] = pos.astype(jnp.int32)
    texp_ref[...] = jnp.minimum(texp, N_EXPERTS - 1.0).astype(jnp.int32)
    nused_ref[...] = jnp.broadcast_to(tile_end[N_EXPERTS - 1:N_EXPERTS, :], nused_ref.shape).astype(jnp.int32)
    tend_ref[...] = jnp.broadcast_to(tile_end, tend_ref.shape).astype(jnp.int32)


def _plan(ek, rk, cnt, n_tiles_pad, expert_rows):
    t = ek.shape[1]

    def full(shape):
        return pl.BlockSpec(shape, lambda: (0,) * len(shape))

    return pl.pallas_call(
        functools.partial(_plan_kernel, expert_rows=expert_rows),
        in_specs=[full(ek.shape), full(rk.shape), full(cnt.shape)],
        out_specs=[full((TOP_K, t)), full((1, n_tiles_pad)), full((1, 128)), full((N_EXPERTS, 128))],
        out_shape=[jax.ShapeDtypeStruct((TOP_K, t), jnp.int32),
                   jax.ShapeDtypeStruct((1, n_tiles_pad), jnp.int32),
                   jax.ShapeDtypeStruct((1, 128), jnp.int32),
                   jax.ShapeDtypeStruct((N_EXPERTS, 128), jnp.int32)],
        compiler_params=pltpu.CompilerParams(vmem_limit_bytes=VMEM_LIMIT),
        name="plan",
    )(ek, rk, cnt)


def _sc_mesh():
    return plsc.VectorSubcoreMesh(core_axis_name="c", subcore_axis_name="s")


def _sc_pack_weight_halves(w):
    e, k, n = w.shape
    k_half = k // 2
    rb = SC_PACK_BLOCK_WORDS // n
    units_per_expert = k_half // rb
    per_w = (e * units_per_expert) // SC_WORKERS
    lanes = SC_LANES

    @functools.partial(
        pl.kernel, out_type=jax.ShapeDtypeStruct((e * k_half, n), jnp.int32), mesh=_sc_mesh(),
        scratch_types=[pltpu.VMEM((rb, n), F32), pltpu.VMEM((rb, n), F32), pltpu.VMEM((rb, n), jnp.int32)],
        compiler_params=pltpu.CompilerParams(needs_layout_passes=False))
    def kern(w_hbm, out_hbm, a_v, b_v, o_v):
        wid = lax.axis_index("s") * SC_CORES + lax.axis_index("c")

        @pl.loop(0, per_w)
        def _(j):
            unit = wid * per_w + j
            expert = unit // units_per_expert
            blk = unit % units_per_expert
            row_a = expert * k + blk * rb
            pltpu.sync_copy(w_hbm.at[pl.ds(row_a, rb)], a_v)
            pltpu.sync_copy(w_hbm.at[pl.ds(row_a + k_half, rb)], b_v)

            @pl.loop(0, rb)
            def _(r):
                @plsc.parallel_loop(0, n, step=lanes, unroll=4)
                def _(c):
                    both = plsc.pack(a_v[r, pl.ds(c, lanes)], b_v[r, pl.ds(c, lanes)],
                                     format=plsc.PackFormat.INTERLEAVED)
                    o_v[r, pl.ds(c, lanes)] = plsc.bitcast(both, jnp.int32)

            pltpu.sync_copy(o_v, out_hbm.at[pl.ds(expert * k_half + blk * rb, rb)])

    return kern(w.reshape(e * k, n)).reshape(e, k_half, n)


def _sc_dispatch(rows, pos3, n_out, after=()):
    t = rows.shape[0]
    ch = SC_CHUNK
    per_w = (t // ch) // SC_WORKERS

    @functools.partial(
        pl.kernel, out_type=jax.ShapeDtypeStruct((n_out,) + rows.shape[1:], jnp.int32), mesh=_sc_mesh(),
        scratch_types=[pltpu.VMEM((TOP_K, ch), jnp.int32), pltpu.VMEM((ch,) + rows.shape[1:], jnp.int32),
                       pltpu.SemaphoreType.DMA])
    def k(rows_hbm, pos_hbm, *rest):
        out_hbm, idx_v, rows_v, sem = rest[len(after):]
        wid = lax.axis_index("s") * SC_CORES + lax.axis_index("c")

        @pl.loop(0, per_w)
        def _(j):
            c = wid * per_w + j
            pltpu.sync_copy(pos_hbm.at[c], idx_v)
            pltpu.sync_copy(rows_hbm.at[pl.ds(c * ch, ch)], rows_v)
            copies = [pltpu.async_copy(rows_v, out_hbm.at[idx_v.at[kk]], sem) for kk in range(TOP_K)]
            for cp in copies:
                cp.wait()

    return k(rows, pos3, *after)


def _sc_combine(table, pos3, wtok, t):
    ch = SC_CHUNK
    sub = SC_COMBINE_TOKENS
    lanes = SC_LANES
    slabs = ROW_SLABS
    per_w = (t // ch) // SC_WORKERS
    subs_per_chunk = ch // sub
    n_steps = per_w * subs_per_chunk

    @functools.partial(
        pl.kernel, out_type=jax.ShapeDtypeStruct((t, slabs, 128), jnp.int32), mesh=_sc_mesh(),
        scratch_types=[pltpu.VMEM((per_w, TOP_K, ch), jnp.int32),
                       pltpu.VMEM((2, TOP_K, sub, slabs, 128), jnp.int32),
                       pltpu.VMEM((2, sub, 128), F32),
                       pltpu.VMEM((sub, slabs, 128), jnp.int32),
                       pltpu.SemaphoreType.DMA((2,))],
        compiler_params=pltpu.CompilerParams(needs_layout_passes=False))
    def k(tab_hbm, pos_hbm, w_hbm, out_hbm, idx_v, rows_v, w_v, out_v, sem):
        wid = lax.axis_index("s") * SC_CORES + lax.axis_index("c")
        for j in range(per_w):
            pltpu.sync_copy(pos_hbm.at[wid * per_w + j], idx_v.at[j])

        def first_token(step):
            return (wid * per_w + step // subs_per_chunk) * ch + (step % subs_per_chunk) * sub

        def copies(step, slot):
            j = step // subs_per_chunk
            s = step % subs_per_chunk
            idx = [idx_v.at[j, kk, pl.ds(s * sub, sub)] for kk in range(TOP_K)]
            return ([pltpu.make_async_copy(tab_hbm.at[idx[kk]], rows_v.at[slot, kk], sem.at[slot])
                     for kk in range(TOP_K)]
                    + [pltpu.make_async_copy(w_hbm.at[pl.ds(first_token(step), sub)], w_v.at[slot], sem.at[slot])])

        for cp in copies(0, 0):
            cp.start()

        @pl.loop(0, n_steps)
        def _(step):
            slot = step % 2

            @pl.when(step + 1 < n_steps)
            def _():
                for cp in copies(step + 1, 1 - slot):
                    cp.start()

            for cp in copies(step, slot):
                cp.wait()

            @pl.loop(0, sub)
            def _(tt):
                wk = [w_v[slot, tt, pl.ds(kk * lanes, lanes)] for kk in range(TOP_K)]
                for sl in range(slabs):
                    @plsc.parallel_loop(0, 128, step=lanes, unroll=4)
                    def _(off):
                        acc_lo = jnp.zeros((lanes,), F32)
                        acc_hi = jnp.zeros((lanes,), F32)
                        for kk in range(TOP_K):
                            word = rows_v[slot, kk, tt, sl, pl.ds(off, lanes)]
                            lo = plsc.bitcast(word << 16, F32)
                            hi = plsc.bitcast(word & jnp.int32(-65536), F32)
                            acc_lo = acc_lo + wk[kk] * lo
                            acc_hi = acc_hi + wk[kk] * hi
                        both = plsc.pack(acc_lo, acc_hi, format=plsc.PackFormat.INTERLEAVED)
                        out_v[tt, sl, pl.ds(off, lanes)] = plsc.bitcast(both, jnp.int32)

            pltpu.sync_copy(out_v, out_hbm.at[pl.ds(first_token(step), sub)])

    return k(table, pos3, wtok)


def _experts_kernel(texp_ref, nused_ref, tend_ref, xs_ref, weg_hbm, weu_hbm, wed_hbm, ys_ref,
                    wg_scr, wu_scr, wd_scr, wg_buf, wu_buf, wd_buf, sem, group_scr, *, expert_rows):
    step = pl.program_id(0)
    rows = expert_rows
    tiles_per_step = EXPERT_STEP_ROWS // expert_rows
    half = D_MODEL // 2
    n_used = nused_ref[0]

    def weight_copies(e, slot):
        return [pltpu.make_async_copy(weg_hbm.at[e], wg_buf.at[slot], sem.at[slot, 0]),
                pltpu.make_async_copy(weu_hbm.at[e], wu_buf.at[slot], sem.at[slot, 1]),
                pltpu.make_async_copy(wed_hbm.at[e], wd_buf.at[slot], sem.at[slot, 2])]

    def next_group(e):
        tile = tend_ref[e]
        return texp_ref[jnp.minimum(tile, n_used - 1)], tile < n_used

    def start_weights(e, slot, exists):
        @pl.when(exists)
        def _():
            for cp in weight_copies(e, slot):
                cp.start()

    @pl.when(step == 0)
    def _():
        group_scr[0] = 0
        e, exists = texp_ref[0], True
        for slot in range(WEIGHT_SLOTS - 1):
            start_weights(e, slot, exists)
            nxt, has_next = next_group(e)
            e, exists = nxt, exists & has_next

    def row_tile(tile, x_view, y_view):
        expert = texp_ref[tile]
        used = tile < n_used
        new_expert = (tile == 0) | (expert != texp_ref[jnp.maximum(tile - 1, 0)])

        @pl.when(used & new_expert)
        def _():
            group = group_scr[0]
            slot = group % WEIGHT_SLOTS
            ahead, exists = expert, True
            for _ in range(WEIGHT_SLOTS - 1):
                nxt, has_next = next_group(ahead)
                ahead, exists = nxt, exists & has_next
            start_weights(ahead, (group + WEIGHT_SLOTS - 1) % WEIGHT_SLOTS, exists)

            for cp in weight_copies(expert, slot):
                cp.wait()
            for scr, buf in ((wg_scr, wg_buf), (wu_scr, wu_buf), (wd_scr, wd_buf)):
                top, bottom = _unpack_pair(buf[slot])
                k_half = top.shape[0]
                scr[0:k_half, :] = top.astype(BF16)
                scr[k_half:2 * k_half, :] = bottom.astype(BF16)
            group_scr[0] = group + 1

        @pl.when(used)
        def _():
            lo, hi = _unpack_pair(_load_token_words(x_view, (), rows))
            lo = lo.astype(BF16)
            hi = hi.astype(BF16)
            g = _dot(lo, wg_scr[0:half, :]) + _dot(hi, wg_scr[half:D_MODEL, :])
            u = _dot(lo, wu_scr[0:half, :]) + _dot(hi, wu_scr[half:D_MODEL, :])
            y = _dot((_silu(g) * u).astype(BF16), wd_scr[...])
            _store_token_words(y_view, _pack_pair(y[:, :half], y[:, half:]), rows)

        @pl.when(jnp.logical_not(used) & (step == (n_used - 1) // tiles_per_step))
        def _():
            y_view[...] = jnp.zeros_like(y_view)

    for s in range(tiles_per_step):
        view = pl.ds(s * rows * ROW_SLABS, rows * ROW_SLABS)
        row_tile(step * tiles_per_step + s, xs_ref.at[view], ys_ref.at[view])


def _experts(texp, nused, tend, xs2d, weg, weu, wed, n_tiles, expert_rows):
    tiles_per_step = EXPERT_STEP_ROWS // expert_rows
    block = (EXPERT_STEP_ROWS * ROW_SLABS, 128)
    hbm = pl.BlockSpec(memory_space=pl.ANY)

    def block_idx(j, te, nu, tn):
        return (jnp.minimum(j, (nu[0] - 1) // tiles_per_step), 0)

    grid_spec = pltpu.PrefetchScalarGridSpec(
        num_scalar_prefetch=3,
        grid=(n_tiles // tiles_per_step,),
        in_specs=[pl.BlockSpec(block, block_idx), hbm, hbm, hbm],
        out_specs=pl.BlockSpec(block, block_idx),
        scratch_shapes=[pltpu.VMEM((D_MODEL, EXPERT_DIM), BF16),
                        pltpu.VMEM((D_MODEL, EXPERT_DIM), BF16),
                        pltpu.VMEM((EXPERT_DIM, D_MODEL), BF16),
                        pltpu.VMEM((WEIGHT_SLOTS,) + weg.shape[1:], jnp.int32),
                        pltpu.VMEM((WEIGHT_SLOTS,) + weu.shape[1:], jnp.int32),
                        pltpu.VMEM((WEIGHT_SLOTS,) + wed.shape[1:], jnp.int32),
                        pltpu.SemaphoreType.DMA((WEIGHT_SLOTS, 3)),
                        pltpu.SMEM((1,), jnp.int32)],
    )
    return pl.pallas_call(
        functools.partial(_experts_kernel, expert_rows=expert_rows),
        grid_spec=grid_spec,
        out_shape=jax.ShapeDtypeStruct(xs2d.shape, jnp.int32),
        compiler_params=pltpu.CompilerParams(dimension_semantics=("arbitrary",),
                                             vmem_limit_bytes=VMEM_LIMIT),
        name="experts",
    )(texp, nused, tend, xs2d, weg, weu, wed)


def _final_kernel(x_ref, routed_ref, mod_ref, g2_ref, wsg_ref, wsu_ref, wsd_ref, fng_ref, o_ref):
    tm = x_ref.shape[0]
    x = x_ref[...]
    hb = _rms_mod(x, g2_ref[...], mod_ref[0, 3:4, :], mod_ref[0, 4:5, :]).astype(BF16)
    shared = _dot((_silu(_dot(hb, wsg_ref[...])) * _dot(hb, wsu_ref[...])).astype(BF16), wsd_ref[...])
    routed = jnp.concatenate(_unpack_pair(_load_token_words(routed_ref, (), tm)), axis=1)
    y = x + mod_ref[0, 5:6, :] * (routed + shared)
    ms = jnp.mean(y * y, axis=-1, keepdims=True)
    o_ref[...] = y * lax.rsqrt(ms + EPS) * fng_ref[...]


def _final(x1, routed2d, mod3, norm2_g, wsg, wsu, wsd, final_g, seq_len, mod_row_of_batch):
    t = x1.shape[0]
    tm = TM_FINAL

    def mod_idx(i):
        return (mod_row_of_batch((i * tm) // seq_len), 0, 0)

    def full(a):
        return pl.BlockSpec(a.shape, lambda i: (0,) * a.ndim)

    return pl.pallas_call(
        _final_kernel,
        grid=(t // tm,),
        in_specs=[pl.BlockSpec((tm, D_MODEL), lambda i: (i, 0)),
                  pl.BlockSpec((tm * ROW_SLABS, 128), lambda i: (i, 0)),
                  pl.BlockSpec((1, 6, D_MODEL), mod_idx),
                  full(norm2_g), full(wsg), full(wsu), full(wsd), full(final_g)],
        out_specs=pl.BlockSpec((tm, D_MODEL), lambda i: (i, 0)),
        out_shape=jax.ShapeDtypeStruct((t, D_MODEL), F32),
        compiler_params=pltpu.CompilerParams(dimension_semantics=("parallel",),
                                             vmem_limit_bytes=VMEM_LIMIT),
        name="final",
    )(x1, routed2d, mod3, norm2_g, wsg, wsu, wsd, final_g)


def _moe(x1, mod3, lw, seq_len, mod_row_of_batch):
    t = x1.shape[0]
    expert_rows = min(MAX_EXPERT_ROWS, TOP_K * t // N_EXPERTS // 2)
    n_tiles = TOP_K * t // expert_rows + N_EXPERTS
    n_tiles_pad = -(-n_tiles // 128) * 128
    hp2d, ek, rk, wtok, cnt = _router(x1, mod3, lw["norm2_g"], lw["w_router_t"], lw["router_bias"],
                                      seq_len, mod_row_of_batch)
    pos, texp, nused, tend = _plan(ek, rk, cnt, n_tiles_pad, expert_rows)
    pos3 = pos.reshape(TOP_K, t // SC_CHUNK, SC_CHUNK).transpose(1, 0, 2)
    xs = _sc_dispatch(hp2d.reshape(t, ROW_SLABS, 128), pos3, n_tiles * expert_rows,
                      after=(lw["weg"], lw["weu"], lw["wed"]))
    ys2d = _experts(texp.reshape(-1), nused.reshape(-1), tend[:, 0], xs.reshape(-1, 128),
                    lw["weg"], lw["weu"], lw["wed"], n_tiles, expert_rows)
    routed = _sc_combine(ys2d.reshape(-1, ROW_SLABS, 128), pos3, wtok, t)
    y = _final(x1, routed.reshape(t * ROW_SLABS, 128), mod3, lw["norm2_g"],
               lw["wsg"], lw["wsu"], lw["wsd"], lw["final_g"], seq_len, mod_row_of_batch)
    return y, pos


def _dft_tables(seq_len):
    gd = FOURIER_GROUP_DIM
    kc = np.arange(gd)
    ang_c = ((kc[:, None] * kc[None, :]) % gd) * (2.0 * math.pi / gd)
    cs = np.concatenate([np.cos(ang_c), np.sin(ang_c)], axis=1) * (gd ** -0.5)
    kl = np.arange(seq_len)
    ang_l = ((kl[:, None] * kl[None, :]) % seq_len) * (2.0 * math.pi / seq_len)
    cls = np.concatenate([np.cos(ang_l), -np.sin(ang_l)], axis=1) * (seq_len ** -0.5)
    return jnp.asarray(cs.astype(np.float32), dtype=BF16), jnp.asarray(cls.astype(np.float32), dtype=BF16)


def _rope_tables(length):
    rows = length // GRID_W
    r = np.repeat(np.arange(rows, dtype=np.float32), GRID_W)
    col = np.tile(np.arange(GRID_W, dtype=np.float32), rows)
    nf = RET_HEAD_DIM // 4
    inv = (np.float32(ROPE_BASE) ** (-np.arange(nf, dtype=np.float32) / np.float32(nf))).astype(np.float32)
    ar = r[:, None] * inv[None]
    ac = col[:, None] * inv[None]
    ang = np.concatenate([ar, ar, ac, ac], axis=-1).astype(np.float64)
    sign = np.where((np.arange(RET_HEAD_DIM) & nf) == 0, -1.0, 1.0)
    return (jnp.asarray(np.cos(ang).astype(np.float32)),
            jnp.asarray((np.sin(ang) * sign[None, :]).astype(np.float32)))


def _trunk_path(x, proj, row0, mod3, mod_row_of_batch, s0f, s0b, lw, after=()):
    batch, seq_len, _ = x.shape
    x2d = x.reshape(batch * seq_len, D_MODEL)
    uf, q, k, v, sg, gf, gr = proj
    r, s_f, s_b = _retention(q, k, v, sg, lw["dec"], lw["gn_g"], s0f, s0b, batch, seq_len, row0, after)
    cs, cls = _dft_tables(seq_len)
    x1 = _fnet_merge(uf, cs, cls, r, gf, gr, x2d, mod3, lw["w_four"], lw["w_ret"], lw["w_o"],
                     batch, seq_len, mod_row_of_batch, row0)
    y, plan = _moe(x1, mod3, lw, seq_len, mod_row_of_batch)
    return y.reshape(batch, seq_len, D_MODEL), s_f, s_b, plan


def kernel(x_prompt, x_sample, state_ret_fwd, state_ret_bwd, c, c_ctx, w_ada, b_ada, norm1_g, norm2_g, w_in,
           ret_decay_fwd, ret_decay_bwd, ret_gn_g, w_four_out, w_ret_out, w_out, w_router, router_bias,
           w_exp_gate, w_exp_up, w_exp_down, w_shared_gate, w_shared_up, w_shared_down, final_norm_g):
    depth = w_ada.shape[0]
    assert depth == 1, "final norm is fused into the last layer's MoE kernel"
    n_ctx, n_lat = x_prompt.shape[0], x_sample.shape[0]
    cond = jnp.concatenate([c_ctx[None, :], c], axis=0)
    cond = jnp.pad(cond, ((0, (-cond.shape[0]) % 8), (0, 0)))
    rope = _rope_tables(x_sample.shape[1])
    zeros = jnp.zeros((n_ctx, N_RET_HEADS, RET_HEAD_DIM, RET_HEAD_DIM), F32)

    layer = 0
    mod = _ada(cond, w_ada[layer], b_ada[layer][None, :])
    mod3 = mod.reshape(mod.shape[0], 6, D_MODEL)
    dec = jnp.stack([ret_decay_fwd[layer], ret_decay_bwd[layer]], axis=1)
    lw = {
        "norm1_g": norm1_g[layer][None, :],
        "norm2_g": norm2_g[layer][None, :],
        "w_in": w_in[layer],
        "dec": jnp.broadcast_to(dec[:, :, None], (N_RET_HEADS, 2, RET_HEAD_DIM)).astype(F32),
        "gn_g": ret_gn_g[layer][None, :],
        "w_four": w_four_out[layer],
        "w_ret": w_ret_out[layer],
        "w_o": w_out[layer],
        "w_router_t": w_router[layer].T,
        "router_bias": router_bias[layer][:, None],
        "weg": _sc_pack_weight_halves(w_exp_gate[layer]),
        "weu": _sc_pack_weight_halves(w_exp_up[layer]),
        "wed": _sc_pack_weight_halves(w_exp_down[layer]),
        "wsg": w_shared_gate[layer].astype(BF16),
        "wsu": w_shared_up[layer].astype(BF16),
        "wsd": w_shared_down[layer].astype(BF16),
        "final_g": final_norm_g[None, :],
    }
    n_lat_tokens = x_sample.shape[0] * x_sample.shape[1]
    proj = _inproj(x_sample.reshape(n_lat_tokens, D_MODEL), x_prompt.reshape(-1, D_MODEL), x_sample.shape[1],
                   mod3, lw["norm1_g"], lw["w_in"], rope)
    y_sample, _, _, lat_plan = _trunk_path(x_sample, proj, 0, mod3, lambda b: 1 + b, state_ret_fwd[:, layer],
                                           state_ret_bwd[:, layer], lw)
    y_prompt, s_f, s_b, _ = _trunk_path(x_prompt, proj, n_lat_tokens, mod3, lambda b: 0, zeros, zeros, lw,
                                        after=(lat_plan,))
    return (y_prompt, y_sample, s_f[:, None], s_b[:, None])
```

```python
import functools
import math

import jax
import jax.numpy as jnp
import numpy as np
from jax import lax
from jax.experimental import pallas as pl
from jax.experimental.pallas import tpu as pltpu
from jax.experimental.pallas import tpu_sc as plsc

F32 = jnp.float32
BF16 = jnp.bfloat16

D_MODEL = 1024
GRID_W = 64
N_FOURIER_GROUPS = 8
FOURIER_GROUP_DIM = 128
N_RET_HEADS = 4
RET_HEAD_DIM = 128
RET_WIDTH = N_RET_HEADS * RET_HEAD_DIM
CHUNK = 128
N_EXPERTS = 64
N_EXPERT_GROUPS = 8
EXPERTS_PER_GROUP = N_EXPERTS // N_EXPERT_GROUPS
TOPK_GROUPS = 4
TOP_K = 8
EXPERT_DIM = 256
ROUTED_SCALE = 2.5
ROPE_BASE = 10000.0
EPS = 1e-6
Q_SCALE = RET_HEAD_DIM ** -0.5

_C_UF = (0, 1024)
_C_Q = (1024, 1536)
_C_K = (1536, 2048)
_C_V = (2048, 2560)
_C_G = (2560, 3072)
_C_GF = (3072, 4096)
_C_GR = (4096, 5120)

VMEM_LIMIT = 56 * 1024 * 1024

TM_INPROJ = 512
TM_ROUTER = 1024
FNET_ROWS = 512
TM_FINAL = 1024
EXPERT_STEP_ROWS = 1024
MAX_EXPERT_ROWS = 512
WEIGHT_SLOTS = 3
ROW_SLABS = 4
SC_CORES = 2
SC_WORKERS = 32
SC_CHUNK = 128
SC_LANES = 16
SC_PACK_BLOCK_WORDS = 16384
SC_COMBINE_TOKENS = 8


def _silu(x):
    return x * jax.nn.sigmoid(x)


def _dot(a, b):
    return jnp.dot(a, b, preferred_element_type=F32)


def _rms_mod(x, g, shift, scale):
    ms = jnp.mean(x * x, axis=-1, keepdims=True)
    y = x * lax.rsqrt(ms + EPS) * g
    return y * (1.0 + scale) + shift


def _ada_kernel(cond_ref, w_ref, b_ref, o_ref):
    s = _silu(cond_ref[...]).astype(BF16)
    o_ref[...] = _dot(s, w_ref[...].astype(BF16)) + b_ref[...]


def _ada(cond, w_ada, b_ada):
    rows, n = cond.shape[0], w_ada.shape[1]
    tn = 1536
    return pl.pallas_call(
        _ada_kernel,
        grid=(n // tn,),
        in_specs=[pl.BlockSpec((rows, D_MODEL), lambda j: (0, 0)),
                  pl.BlockSpec((D_MODEL, tn), lambda j: (0, j)),
                  pl.BlockSpec((1, tn), lambda j: (0, j))],
        out_specs=pl.BlockSpec((rows, tn), lambda j: (0, j)),
        out_shape=jax.ShapeDtypeStruct((rows, n), F32),
        compiler_params=pltpu.CompilerParams(vmem_limit_bytes=VMEM_LIMIT),
        name="ada",
    )(cond, w_ada, b_ada)


def _rope_head(x, cos, sin_signed, first_half):
    partner = jnp.where(first_half, pltpu.roll(x, 96, 1), pltpu.roll(x, 32, 1))
    return x * cos + partner * sin_signed


def _inproj_kernel(xl_ref, xc_ref, mod_ref, g_ref, w_ref, cos_ref, sin_ref,
                   uf_o, q_o, k_o, v_o, sg_o, gf_o, gr_o, *, n_latent_steps):
    x = jnp.where(pl.program_id(0) < n_latent_steps, xl_ref[...], xc_ref[...])
    h = _rms_mod(x, g_ref[...], mod_ref[0, 0:1, :], mod_ref[0, 1:2, :])
    hb = h.astype(BF16)

    def proj(cols):
        return _dot(hb, w_ref[:, cols[0]:cols[1]].astype(BF16))

    uf_o[...] = proj(_C_UF).astype(BF16)
    q = proj(_C_Q)
    k = proj(_C_K)
    cos = cos_ref[...]
    sin_signed = sin_ref[...]
    lane = lax.broadcasted_iota(jnp.int32, cos.shape, 1)
    first_half = (lane & 32) == 0
    for hd in range(N_RET_HEADS):
        sl = slice(hd * RET_HEAD_DIM, (hd + 1) * RET_HEAD_DIM)
        q_o[:, sl] = (_rope_head(q[:, sl], cos, sin_signed, first_half) * Q_SCALE).astype(BF16)
        k_o[:, sl] = _rope_head(k[:, sl], cos, sin_signed, first_half).astype(BF16)
    v_o[...] = proj(_C_V).astype(BF16)
    sg_o[...] = _silu(proj(_C_G)).astype(BF16)
    gf_o[...] = jax.nn.sigmoid(proj(_C_GF)).astype(BF16)
    gr_o[...] = jax.nn.sigmoid(proj(_C_GR)).astype(BF16)


def _inproj(x_lat2d, x_ctx2d, lat_seq_len, mod3, norm_g, w_in_f32, rope):
    tm = TM_INPROJ
    assert lat_seq_len % tm == 0 and x_ctx2d.shape[0] % tm == 0
    tiles_per_seq = lat_seq_len // tm
    n_lat = x_lat2d.shape[0] // tm
    n_ctx = x_ctx2d.shape[0] // tm
    t = (n_lat + n_ctx) * tm
    cos, sin_signed = rope
    cos2 = jnp.concatenate([cos, jnp.ones((tm, RET_HEAD_DIM), F32)], axis=0)
    sin2 = jnp.concatenate([sin_signed, jnp.zeros((tm, RET_HEAD_DIM), F32)], axis=0)
    rope_spec = pl.BlockSpec((tm, RET_HEAD_DIM),
                             lambda i: (jnp.where(i < n_lat, i % tiles_per_seq, tiles_per_seq), 0))
    widths = [1024, RET_WIDTH, RET_WIDTH, RET_WIDTH, RET_WIDTH, 1024, 1024]
    return pl.pallas_call(
        functools.partial(_inproj_kernel, n_latent_steps=n_lat),
        grid=(n_lat + n_ctx,),
        in_specs=[pl.BlockSpec((tm, D_MODEL), lambda i: (jnp.minimum(i, n_lat - 1), 0)),
                  pl.BlockSpec((tm, D_MODEL), lambda i: (jnp.maximum(i - n_lat, 0), 0)),
                  pl.BlockSpec((1, 6, D_MODEL), lambda i: (jnp.where(i < n_lat, 1 + i // tiles_per_seq, 0), 0, 0)),
                  pl.BlockSpec((1, D_MODEL), lambda i: (0, 0)),
                  pl.BlockSpec(w_in_f32.shape, lambda i: (0, 0), pipeline_mode=pl.Buffered(1)),
                  rope_spec, rope_spec],
        out_specs=[pl.BlockSpec((tm, w), lambda i: (i, 0)) for w in widths],
        out_shape=[jax.ShapeDtypeStruct((t, w), BF16) for w in widths],
        compiler_params=pltpu.CompilerParams(dimension_semantics=("parallel",),
                                             vmem_limit_bytes=VMEM_LIMIT),
        name="inproj",
    )(x_lat2d, x_ctx2d, mod3, norm_g, w_in_f32, cos2, sin2)


def _retention_kernel(q_ref, k_ref, v_ref, sg_ref, dec_ref, gn_ref, s0f_ref, s0b_ref, *rest):
    r_ref, sfo_ref, sbo_ref, tab_scr, gc_scr = rest[-5:]
    n_chunks = q_ref.shape[0] // CHUNK
    hd = RET_HEAD_DIM

    @pl.when(pl.program_id(0) == 0)
    def _():
        row = lax.broadcasted_iota(jnp.int32, (CHUNK, CHUNK), 0).astype(F32)
        col = lax.broadcasted_iota(jnp.int32, (CHUNK, CHUNK), 1).astype(F32)
        diff = row - col
        for h in range(N_RET_HEADS):
            dec = dec_ref[h]
            lg = jnp.minimum(dec, 0.0) - jnp.log1p(jnp.exp(-jnp.abs(dec)))
            lgf = lg[0:1, :]
            lgb = lg[1:2, :]
            tab_scr[h, 0] = jnp.exp(jnp.where(diff >= 0, lgf * diff, lgb * (-diff)))
            tab_scr[h, 1] = jnp.exp(lgf * (row + 1.0))
            tab_scr[h, 2] = jnp.exp(lgb * (CHUNK - row))
            tab_scr[h, 3] = jnp.exp(lgf * (CHUNK - 1.0 - col))
            tab_scr[h, 4] = jnp.exp(lgb * col)
            gc_scr[h] = jnp.exp(lg * CHUNK)

    def rows(n):
        return slice(n * CHUNK, (n + 1) * CHUNK)

    for h in range(N_RET_HEADS):
        cols = slice(h * hd, (h + 1) * hd)
        decay, qw_f, qw_b, kwt_f, kwt_b = (tab_scr[h, i] for i in range(5))
        gc = gc_scr[h]
        gc_f = gc[0:1, :]
        gc_b = gc[1:2, :]

        kv_f, kv_b = [], []
        for n in range(n_chunks):
            kt = k_ref[rows(n), cols].astype(F32).T
            vn = v_ref[rows(n), cols]
            kv_f.append(_dot((kt * kwt_f).astype(BF16), vn))
            kv_b.append(_dot((kt * kwt_b).astype(BF16), vn))

        s = s0f_ref[h]
        prev_f = []
        for n in range(n_chunks):
            prev_f.append(s.astype(BF16))
            s = gc_f * s + kv_f[n]
        sfo_ref[h] = s
        s = s0b_ref[h]
        prev_b = [None] * n_chunks
        for n in reversed(range(n_chunks)):
            prev_b[n] = s.astype(BF16)
            s = gc_b * s + kv_b[n]
        sbo_ref[h] = s

        gn = gn_ref[:, cols]
        for n in range(n_chunks):
            qn = q_ref[rows(n), cols]
            qf = qn.astype(F32)
            scores = lax.dot_general(qn, k_ref[rows(n), cols], (((1,), (1,)), ((), ())),
                                     preferred_element_type=F32)
            o = _dot((scores * decay).astype(BF16), v_ref[rows(n), cols])
            o = o + _dot((qf * qw_f).astype(BF16), prev_f[n])
            o = o + _dot((qf * qw_b).astype(BF16), prev_b[n])
            mu = jnp.mean(o, axis=-1, keepdims=True)
            d = o - mu
            var = jnp.mean(d * d, axis=-1, keepdims=True)
            on = d * lax.rsqrt(var + EPS) * gn
            r_ref[rows(n), cols] = (on * sg_ref[rows(n), cols].astype(F32)).astype(BF16)


def _retention(q, k, v, sg, dec, gn_g, s0f, s0b, batch, seq_len, row0, after=()):
    hd = RET_HEAD_DIM
    first = row0 // seq_len
    in_spec = pl.BlockSpec((seq_len, RET_WIDTH), lambda b: (first + b, 0))
    tok_spec = pl.BlockSpec((seq_len, RET_WIDTH), lambda b: (b, 0))
    st_spec = pl.BlockSpec((None, N_RET_HEADS, hd, hd), lambda b: (b, 0, 0, 0))
    st_shape = jax.ShapeDtypeStruct((batch, N_RET_HEADS, hd, hd), F32)
    return pl.pallas_call(
        _retention_kernel,
        grid=(batch,),
        in_specs=[in_spec, in_spec, in_spec, in_spec,
                  pl.BlockSpec(dec.shape, lambda b: (0, 0, 0)),
                  pl.BlockSpec(gn_g.shape, lambda b: (0, 0)),
                  st_spec, st_spec] + [pl.BlockSpec(memory_space=pl.ANY)] * len(after),
        out_specs=[tok_spec, st_spec, st_spec],
        out_shape=[jax.ShapeDtypeStruct((batch * seq_len, RET_WIDTH), BF16), st_shape, st_shape],
        scratch_shapes=[pltpu.VMEM((N_RET_HEADS, 5, CHUNK, CHUNK), F32),
                        pltpu.VMEM((N_RET_HEADS, 2, hd), F32)],
        compiler_params=pltpu.CompilerParams(dimension_semantics=("arbitrary",),
                                             vmem_limit_bytes=VMEM_LIMIT),
        name="retention",
    )(q, k, v, sg, dec, gn_g, s0f, s0b, *after)


def _fnet_merge_kernel(uf_ref, cs_ref, cls_ref, r_ref, gf_ref, gr_ref, x_ref, mod_ref, wf_ref, wr_ref, wo_ref,
                       o_ref, xcs_ref):
    seq_len = uf_ref.shape[0]
    gd = FOURIER_GROUP_DIM

    @pl.when(pl.program_id(1) == 0)
    def _():
        for g in range(N_FOURIER_GROUPS):
            x = _dot(uf_ref[:, g * gd:(g + 1) * gd], cs_ref[...])
            xcs_ref[0:seq_len, g * gd:(g + 1) * gd] = x[:, :gd].astype(BF16)
            xcs_ref[seq_len:2 * seq_len, g * gd:(g + 1) * gd] = x[:, gd:].astype(BF16)

    f_mix = _dot(cls_ref[...], xcs_ref[...]).astype(BF16)
    f_out = _dot(f_mix, wf_ref[...].astype(BF16))
    r_out = _dot(r_ref[...], wr_ref[...].astype(BF16))
    merged = gf_ref[...].astype(F32) * f_out + gr_ref[...].astype(F32) * r_out
    mix = _dot(merged.astype(BF16), wo_ref[...].astype(BF16))
    o_ref[...] = x_ref[...] + mod_ref[0, 2:3, :] * mix


def _fnet_merge(uf, cs, cls, r, gf, gr, x2d, mod3, w_four, w_ret, w_o, batch, seq_len, mod_row_of_batch, row0):
    rb = min(FNET_ROWS, seq_len)
    nr = seq_len // rb

    def tok(w):
        return pl.BlockSpec((rb, w), lambda b, i: (b * nr + i, 0))

    def tok_joint(w):
        return pl.BlockSpec((rb, w), lambda b, i: (row0 // rb + b * nr + i, 0))

    def full(a):
        return pl.BlockSpec(a.shape, lambda b, i: (0, 0))

    def once(a):
        return pl.BlockSpec(a.shape, lambda b, i: (0, 0), pipeline_mode=pl.Buffered(1))

    return pl.pallas_call(
        _fnet_merge_kernel,
        grid=(batch, nr),
        in_specs=[pl.BlockSpec((seq_len, D_MODEL), lambda b, i: (row0 // seq_len + b, 0)),
                  full(cs),
                  pl.BlockSpec((rb, 2 * seq_len), lambda b, i: (i, 0)),
                  tok(RET_WIDTH), tok_joint(D_MODEL), tok_joint(D_MODEL), tok(D_MODEL),
                  pl.BlockSpec((1, 6, D_MODEL), lambda b, i: (mod_row_of_batch(b), 0, 0)),
                  once(w_four), once(w_ret), once(w_o)],
        out_specs=tok(D_MODEL),
        out_shape=jax.ShapeDtypeStruct((batch * seq_len, D_MODEL), F32),
        scratch_shapes=[pltpu.VMEM((2 * seq_len, D_MODEL), BF16)],
        compiler_params=pltpu.CompilerParams(dimension_semantics=("parallel", "arbitrary"),
                                             vmem_limit_bytes=VMEM_LIMIT),
        name="fnet_merge",
    )(uf, cs, cls, r, gf, gr, x2d, mod3, w_four, w_ret, w_o)


def _pack_pair(lo_f32, hi_f32):
    lo = lax.bitcast_convert_type(lo_f32.astype(BF16).astype(F32), jnp.uint32)
    hi = lax.bitcast_convert_type(hi_f32.astype(BF16).astype(F32), jnp.uint32)
    return lax.bitcast_convert_type((lo >> 16) | hi, jnp.int32)


def _unpack_pair(words_i32):
    w = lax.bitcast_convert_type(words_i32, jnp.uint32)
    lo = lax.bitcast_convert_type(w << 16, F32)
    hi = lax.bitcast_convert_type(w & jnp.uint32(0xFFFF0000), F32)
    return lo, hi


def _load_token_words(ref, lead, n_tok):
    parts = []
    for s in range(ROW_SLABS):
        idx = (pl.ds(s, n_tok, stride=ROW_SLABS), slice(None))
        parts.append(ref[lead + idx] if lead else ref[idx])
    return jnp.concatenate(parts, axis=1)


def _store_token_words(ref, words, n_tok):
    for s in range(ROW_SLABS):
        ref[pl.ds(s, n_tok, stride=ROW_SLABS), :] = words[:, s * 128:(s + 1) * 128]


def _route(scores, biased):
    tokens = scores.shape[1]
    neg = -jnp.inf
    epg = EXPERTS_PER_GROUP
    iota_g = lax.broadcasted_iota(jnp.int32, (epg, tokens), 0).astype(F32)

    def pick_first_max(cur, iota, size):
        m = jnp.max(cur, axis=0, keepdims=True)
        idx = jnp.min(jnp.where(cur == m, iota, float(size)), axis=0, keepdims=True)
        return m, idx, iota == idx

    group_scores = []
    for g in range(N_EXPERT_GROUPS):
        vals = biased[g * epg:(g + 1) * epg, :]
        m1, _, hit = pick_first_max(vals, iota_g, epg)
        m2 = jnp.max(jnp.where(hit, neg, vals), axis=0, keepdims=True)
        group_scores.append(m1 + m2)
    cur = jnp.concatenate(group_scores, axis=0)
    group_sel = jnp.zeros_like(cur)
    for _ in range(TOPK_GROUPS):
        _, _, hit = pick_first_max(cur, iota_g, N_EXPERT_GROUPS)
        group_sel = jnp.where(hit, 1.0, group_sel)
        cur = jnp.where(hit, neg, cur)
    masked = jnp.concatenate(
        [jnp.where(group_sel[g:g + 1, :] > 0.0, biased[g * epg:(g + 1) * epg, :], neg)
         for g in range(N_EXPERT_GROUPS)], axis=0)
    iota_e = lax.broadcasted_iota(jnp.int32, masked.shape, 0).astype(F32)
    sel = jnp.zeros_like(masked)
    cur = masked
    picks = []
    for _ in range(TOP_K):
        _, idx, hit = pick_first_max(cur, iota_e, N_EXPERTS)
        picks.append(idx)
        sel = jnp.where(hit, 1.0, sel)
        cur = jnp.where(hit, neg, cur)
    w = scores * sel
    return w / jnp.sum(w, axis=0, keepdims=True) * ROUTED_SCALE, sel, picks


def _router_kernel(x_ref, mod_ref, g2_ref, wrt_ref, rb_ref, hp_ref, ek_ref, rk_ref, wt_ref, cnt_ref,
                   run_scr, earlier_scr):
    tm = x_ref.shape[0]

    @pl.when(pl.program_id(0) == 0)
    def _():
        run_scr[...] = jnp.zeros_like(run_scr)
        earlier = (lax.broadcasted_iota(jnp.int32, (tm, tm), 0) < lax.broadcasted_iota(jnp.int32, (tm, tm), 1))
        earlier_scr[...] = jnp.where(earlier, 1.0, 0.0).astype(BF16)

    h = _rms_mod(x_ref[...], g2_ref[...], mod_ref[0, 3:4, :], mod_ref[0, 4:5, :])
    half = D_MODEL // 2
    _store_token_words(hp_ref, _pack_pair(h[:, :half], h[:, half:]), tm)

    def split(a):
        hi = a.astype(BF16)
        return hi, (a - hi.astype(F32)).astype(BF16)

    def dot_nt(a, b):
        return lax.dot_general(a, b, (((1,), (1,)), ((), ())), preferred_element_type=F32)

    h_hi, h_lo = split(h)
    w_hi, w_lo = split(wrt_ref[...])
    logits_t = dot_nt(w_hi, h_hi) + (dot_nt(w_hi, h_lo) + dot_nt(w_lo, h_hi))
    scores = jax.nn.sigmoid(logits_t)
    comb_t, sel, picks = _route(scores, scores + rb_ref[...])

    rank_t = _dot(sel.astype(BF16), earlier_scr[...]) + run_scr[...]
    run_scr[...] += jnp.sum(sel, axis=1, keepdims=True)
    cnt_ref[...] = jnp.broadcast_to(run_scr[...], cnt_ref.shape)

    iota_e = lax.broadcasted_iota(jnp.int32, sel.shape, 0).astype(F32)
    ranks, weights = [], []
    for idx in picks:
        hit = iota_e == idx
        ranks.append(jnp.sum(jnp.where(hit, rank_t, 0.0), axis=0, keepdims=True))
        weights.append(jnp.sum(jnp.where(hit, comb_t, 0.0), axis=0, keepdims=True))
    ek_ref[...] = jnp.concatenate(picks, axis=0).astype(jnp.int32)
    rk_ref[...] = jnp.concatenate(ranks, axis=0).astype(jnp.int32)
    w_rep = jnp.concatenate([jnp.broadcast_to(w, (SC_LANES, tm)) for w in weights], axis=0)
    wt_ref[...] = w_rep.T


def _router(x1, mod3, norm2_g, w_router_t, router_bias, seq_len, mod_row_of_batch):
    t = x1.shape[0]
    tm = TM_ROUTER

    def mod_idx(i):
        return (mod_row_of_batch((i * tm) // seq_len), 0, 0)

    def full(a):
        return pl.BlockSpec(a.shape, lambda i: (0,) * a.ndim)

    return pl.pallas_call(
        _router_kernel,
        grid=(t // tm,),
        in_specs=[pl.BlockSpec((tm, D_MODEL), lambda i: (i, 0)),
                  pl.BlockSpec((1, 6, D_MODEL), mod_idx),
                  full(norm2_g), full(w_router_t), full(router_bias)],
        out_specs=[pl.BlockSpec((tm * ROW_SLABS, 128), lambda i: (i, 0)),
                   pl.BlockSpec((TOP_K, tm), lambda i: (0, i)),
                   pl.BlockSpec((TOP_K, tm), lambda i: (0, i)),
                   pl.BlockSpec((tm, 128), lambda i: (i, 0)),
                   pl.BlockSpec((N_EXPERTS, 128), lambda i: (0, 0))],
        out_shape=[jax.ShapeDtypeStruct((t * ROW_SLABS, 128), jnp.int32),
                   jax.ShapeDtypeStruct((TOP_K, t), jnp.int32),
                   jax.ShapeDtypeStruct((TOP_K, t), jnp.int32),
                   jax.ShapeDtypeStruct((t, 128), F32),
                   jax.ShapeDtypeStruct((N_EXPERTS, 128), F32)],
        scratch_shapes=[pltpu.VMEM((N_EXPERTS, 1), F32), pltpu.VMEM((tm, tm), BF16)],
        compiler_params=pltpu.CompilerParams(dimension_semantics=("arbitrary",),
                                             vmem_limit_bytes=VMEM_LIMIT),
        name="router",
    )(x1, mod3, norm2_g, w_router_t, router_bias)


def _plan_kernel(ek_ref, rk_ref, cnt_ref, pos_ref, texp_ref, nused_ref, tend_ref, *, expert_rows):
    rows = float(expert_rows)
    cnt = cnt_ref[:, 0:1]
    tiles = jnp.floor((cnt + (rows - 1.0)) / rows)
    before = (lax.broadcasted_iota(jnp.int32, (N_EXPERTS, N_EXPERTS), 1)
              < lax.broadcasted_iota(jnp.int32, (N_EXPERTS, N_EXPERTS), 0))
    tile_start = jnp.dot(jnp.where(before, 1.0, 0.0), jnp.broadcast_to(tiles, (N_EXPERTS, 128)),
                         precision=lax.Precision.HIGHEST, preferred_element_type=F32)[:, 0:1]
    tile_end = tile_start + tiles
    row_start = tile_start * rows

    ek = ek_ref[...]
    pos = rk_ref[...].astype(F32)
    tile_id = lax.broadcasted_iota(jnp.int32, texp_ref.shape, 1).astype(F32)
    texp = jnp.zeros(texp_ref.shape, F32)
    for e in range(N_EXPERTS):
        pos = pos + jnp.where(ek == e, row_start[e:e + 1, :], 0.0)
        texp = texp + jnp.where(tile_id >= tile_end[e:e + 1, :], 1.0, 0.0)
    pos_ref[...] = pos.astype(jnp.int32)
    texp_ref[...] = jnp.minimum(texp, N_EXPERTS - 1.0).astype(jnp.int32)
    nused_ref[...] = jnp.broadcast_to(tile_end[N_EXPERTS - 1:N_EXPERTS, :], nused_ref.shape).astype(jnp.int32)
    tend_ref[...] = jnp.broadcast_to(tile_end, tend_ref.shape).astype(jnp.int32)


def _plan(ek, rk, cnt, n_tiles_pad, expert_rows):
    t = ek.shape[1]

    def full(shape):
        return pl.BlockSpec(shape, lambda: (0,) * len(shape))

    return pl.pallas_call(
        functools.partial(_plan_kernel, expert_rows=expert_rows),
        in_specs=[full(ek.shape), full(rk.shape), full(cnt.shape)],
        out_specs=[full((TOP_K, t)), full((1, n_tiles_pad)), full((1, 128)), full((N_EXPERTS, 128))],
        out_shape=[jax.ShapeDtypeStruct((TOP_K, t), jnp.int32),
                   jax.ShapeDtypeStruct((1, n_tiles_pad), jnp.int32),
                   jax.ShapeDtypeStruct((1, 128), jnp.int32),
                   jax.ShapeDtypeStruct((N_EXPERTS, 128), jnp.int32)],
        compiler_params=pltpu.CompilerParams(vmem_limit_bytes=VMEM_LIMIT),
        name="plan",
    )(ek, rk, cnt)


def _sc_mesh():
    return plsc.VectorSubcoreMesh(core_axis_name="c", subcore_axis_name="s")


def _sc_pack_weight_halves(w):
    e, k, n = w.shape
    k_half = k // 2
    rb = SC_PACK_BLOCK_WORDS // n
    units_per_expert = k_half // rb
    per_w = (e * units_per_expert) // SC_WORKERS
    lanes = SC_LANES

    @functools.partial(
        pl.kernel, out_type=jax.ShapeDtypeStruct((e * k_half, n), jnp.int32), mesh=_sc_mesh(),
        scratch_types=[pltpu.VMEM((rb, n), F32), pltpu.VMEM((rb, n), F32), pltpu.VMEM((rb, n), jnp.int32)],
        compiler_params=pltpu.CompilerParams(needs_layout_passes=False))
    def kern(w_hbm, out_hbm, a_v, b_v, o_v):
        wid = lax.axis_index("s") * SC_CORES + lax.axis_index("c")

        @pl.loop(0, per_w)
        def _(j):
            unit = wid * per_w + j
            expert = unit // units_per_expert
            blk = unit % units_per_expert
            row_a = expert * k + blk * rb
            pltpu.sync_copy(w_hbm.at[pl.ds(row_a, rb)], a_v)
            pltpu.sync_copy(w_hbm.at[pl.ds(row_a + k_half, rb)], b_v)

            @pl.loop(0, rb)
            def _(r):
                @plsc.parallel_loop(0, n, step=lanes, unroll=4)
                def _(c):
                    both = plsc.pack(a_v[r, pl.ds(c, lanes)], b_v[r, pl.ds(c, lanes)],
                                     format=plsc.PackFormat.INTERLEAVED)
                    o_v[r, pl.ds(c, lanes)] = plsc.bitcast(both, jnp.int32)

            pltpu.sync_copy(o_v, out_hbm.at[pl.ds(expert * k_half + blk * rb, rb)])

    return kern(w.reshape(e * k, n)).reshape(e, k_half, n)


def _sc_dispatch(rows, pos3, n_out, after=()):
    t = rows.shape[0]
    ch = SC_CHUNK
    per_w = (t // ch) // SC_WORKERS

    @functools.partial(
        pl.kernel, out_type=jax.ShapeDtypeStruct((n_out,) + rows.shape[1:], jnp.int32), mesh=_sc_mesh(),
        scratch_types=[pltpu.VMEM((TOP_K, ch), jnp.int32), pltpu.VMEM((ch,) + rows.shape[1:], jnp.int32),
                       pltpu.SemaphoreType.DMA])
    def k(rows_hbm, pos_hbm, *rest):
        out_hbm, idx_v, rows_v, sem = rest[len(after):]
        wid = lax.axis_index("s") * SC_CORES + lax.axis_index("c")

        @pl.loop(0, per_w)
        def _(j):
            c = wid * per_w + j
            pltpu.sync_copy(pos_hbm.at[c], idx_v)
            pltpu.sync_copy(rows_hbm.at[pl.ds(c * ch, ch)], rows_v)
            copies = [pltpu.async_copy(rows_v, out_hbm.at[idx_v.at[kk]], sem) for kk in range(TOP_K)]
            for cp in copies:
                cp.wait()

    return k(rows, pos3, *after)


def _sc_combine(table, pos3, wtok, t):
    ch = SC_CHUNK
    sub = SC_COMBINE_TOKENS
    lanes = SC_LANES
    slabs = ROW_SLABS
    per_w = (t // ch) // SC_WORKERS
    subs_per_chunk = ch // sub
    n_steps = per_w * subs_per_chunk

    @functools.partial(
        pl.kernel, out_type=jax.ShapeDtypeStruct((t, slabs, 128), jnp.int32), mesh=_sc_mesh(),
        scratch_types=[pltpu.VMEM((per_w, TOP_K, ch), jnp.int32),
                       pltpu.VMEM((2, TOP_K, sub, slabs, 128), jnp.int32),
                       pltpu.VMEM((2, sub, 128), F32),
                       pltpu.VMEM((sub, slabs, 128), jnp.int32),
                       pltpu.SemaphoreType.DMA((2,))],
        compiler_params=pltpu.CompilerParams(needs_layout_passes=False))
    def k(tab_hbm, pos_hbm, w_hbm, out_hbm, idx_v, rows_v, w_v, out_v, sem):
        wid = lax.axis_index("s") * SC_CORES + lax.axis_index("c")
        for j in range(per_w):
            pltpu.sync_copy(pos_hbm.at[wid * per_w + j], idx_v.at[j])

        def first_token(step):
            return (wid * per_w + step // subs_per_chunk) * ch + (step % subs_per_chunk) * sub

        def copies(step, slot):
            j = step // subs_per_chunk
            s = step % subs_per_chunk
            idx = [idx_v.at[j, kk, pl.ds(s * sub, sub)] for kk in range(TOP_K)]
            return ([pltpu.make_async_copy(tab_hbm.at[idx[kk]], rows_v.at[slot, kk], sem.at[slot])
                     for kk in range(TOP_K)]
                    + [pltpu.make_async_copy(w_hbm.at[pl.ds(first_token(step), sub)], w_v.at[slot], sem.at[slot])])

        for cp in copies(0, 0):
            cp.start()

        @pl.loop(0, n_steps)
        def _(step):
            slot = step % 2

            @pl.when(step + 1 < n_steps)
            def _():
                for cp in copies(step + 1, 1 - slot):
                    cp.start()

            for cp in copies(step, slot):
                cp.wait()

            @pl.loop(0, sub)
            def _(tt):
                wk = [w_v[slot, tt, pl.ds(kk * lanes, lanes)] for kk in range(TOP_K)]
                for sl in range(slabs):
                    @plsc.parallel_loop(0, 128, step=lanes, unroll=4)
                    def _(off):
                        acc_lo = jnp.zeros((lanes,), F32)
                        acc_hi = jnp.zeros((lanes,), F32)
                        for kk in range(TOP_K):
                            word = rows_v[slot, kk, tt, sl, pl.ds(off, lanes)]
                            lo = plsc.bitcast(word << 16, F32)
                            hi = plsc.bitcast(word & jnp.int32(-65536), F32)
                            acc_lo = acc_lo + wk[kk] * lo
                            acc_hi = acc_hi + wk[kk] * hi
                        both = plsc.pack(acc_lo, acc_hi, format=plsc.PackFormat.INTERLEAVED)
                        out_v[tt, sl, pl.ds(off, lanes)] = plsc.bitcast(both, jnp.int32)

            pltpu.sync_copy(out_v, out_hbm.at[pl.ds(first_token(step), sub)])

    return k(table, pos3, wtok)


def _experts_kernel(texp_ref, nused_ref, tend_ref, xs_ref, weg_hbm, weu_hbm, wed_hbm, ys_ref,
                    wg_scr, wu_scr, wd_scr, wg_buf, wu_buf, wd_buf, sem, group_scr, *, expert_rows):
    step = pl.program_id(0)
    rows = expert_rows
    tiles_per_step = EXPERT_STEP_ROWS // expert_rows
    half = D_MODEL // 2
    n_used = nused_ref[0]

    def weight_copies(e, slot):
        return [pltpu.make_async_copy(weg_hbm.at[e], wg_buf.at[slot], sem.at[slot, 0]),
                pltpu.make_async_copy(weu_hbm.at[e], wu_buf.at[slot], sem.at[slot, 1]),
                pltpu.make_async_copy(wed_hbm.at[e], wd_buf.at[slot], sem.at[slot, 2])]

    def next_group(e):
        tile = tend_ref[e]
        return texp_ref[jnp.minimum(tile, n_used - 1)], tile < n_used

    def start_weights(e, slot, exists):
        @pl.when(exists)
        def _():
            for cp in weight_copies(e, slot):
                cp.start()

    @pl.when(step == 0)
    def _():
        group_scr[0] = 0
        e, exists = texp_ref[0], True
        for slot in range(WEIGHT_SLOTS - 1):
            start_weights(e, slot, exists)
            nxt, has_next = next_group(e)
            e, exists = nxt, exists & has_next

    def row_tile(tile, x_view, y_view):
        expert = texp_ref[tile]
        used = tile < n_used
        new_expert = (tile == 0) | (expert != texp_ref[jnp.maximum(tile - 1, 0)])

        @pl.when(used & new_expert)
        def _():
            group = group_scr[0]
            slot = group % WEIGHT_SLOTS
            ahead, exists = expert, True
            for _ in range(WEIGHT_SLOTS - 1):
                nxt, has_next = next_group(ahead)
                ahead, exists = nxt, exists & has_next
            start_weights(ahead, (group + WEIGHT_SLOTS - 1) % WEIGHT_SLOTS, exists)

            for cp in weight_copies(expert, slot):
                cp.wait()
            for scr, buf in ((wg_scr, wg_buf), (wu_scr, wu_buf), (wd_scr, wd_buf)):
                top, bottom = _unpack_pair(buf[slot])
                k_half = top.shape[0]
                scr[0:k_half, :] = top.astype(BF16)
                scr[k_half:2 * k_half, :] = bottom.astype(BF16)
            group_scr[0] = group + 1

        @pl.when(used)
        def _():
            lo, hi = _unpack_pair(_load_token_words(x_view, (), rows))
            lo = lo.astype(BF16)
            hi = hi.astype(BF16)
            g = _dot(lo, wg_scr[0:half, :]) + _dot(hi, wg_scr[half:D_MODEL, :])
            u = _dot(lo, wu_scr[0:half, :]) + _dot(hi, wu_scr[half:D_MODEL, :])
            y = _dot((_silu(g) * u).astype(BF16), wd_scr[...])
            _store_token_words(y_view, _pack_pair(y[:, :half], y[:, half:]), rows)

        @pl.when(jnp.logical_not(used) & (step == (n_used - 1) // tiles_per_step))
        def _():
            y_view[...] = jnp.zeros_like(y_view)

    for s in range(tiles_per_step):
        view = pl.ds(s * rows * ROW_SLABS, rows * ROW_SLABS)
        row_tile(step * tiles_per_step + s, xs_ref.at[view], ys_ref.at[view])


def _experts(texp, nused, tend, xs2d, weg, weu, wed, n_tiles, expert_rows):
    tiles_per_step = EXPERT_STEP_ROWS // expert_rows
    block = (EXPERT_STEP_ROWS * ROW_SLABS, 128)
    hbm = pl.BlockSpec(memory_space=pl.ANY)

    def block_idx(j, te, nu, tn):
        return (jnp.minimum(j, (nu[0] - 1) // tiles_per_step), 0)

    grid_spec = pltpu.PrefetchScalarGridSpec(
        num_scalar_prefetch=3,
        grid=(n_tiles // tiles_per_step,),
        in_specs=[pl.BlockSpec(block, block_idx), hbm, hbm, hbm],
        out_specs=pl.BlockSpec(block, block_idx),
        scratch_shapes=[pltpu.VMEM((D_MODEL, EXPERT_DIM), BF16),
                        pltpu.VMEM((D_MODEL, EXPERT_DIM), BF16),
                        pltpu.VMEM((EXPERT_DIM, D_MODEL), BF16),
                        pltpu.VMEM((WEIGHT_SLOTS,) + weg.shape[1:], jnp.int32),
                        pltpu.VMEM((WEIGHT_SLOTS,) + weu.shape[1:], jnp.int32),
                        pltpu.VMEM((WEIGHT_SLOTS,) + wed.shape[1:], jnp.int32),
                        pltpu.SemaphoreType.DMA((WEIGHT_SLOTS, 3)),
                        pltpu.SMEM((1,), jnp.int32)],
    )
    return pl.pallas_call(
        functools.partial(_experts_kernel, expert_rows=expert_rows),
        grid_spec=grid_spec,
        out_shape=jax.ShapeDtypeStruct(xs2d.shape, jnp.int32),
        compiler_params=pltpu.CompilerParams(dimension_semantics=("arbitrary",),
                                             vmem_limit_bytes=VMEM_LIMIT),
        name="experts",
    )(texp, nused, tend, xs2d, weg, weu, wed)


def _final_kernel(x_ref, routed_ref, mod_ref, g2_ref, wsg_ref, wsu_ref, wsd_ref, fng_ref, o_ref):
    tm = x_ref.shape[0]
    x = x_ref[...]
    hb = _rms_mod(x, g2_ref[...], mod_ref[0, 3:4, :], mod_ref[0, 4:5, :]).astype(BF16)
    shared = _dot((_silu(_dot(hb, wsg_ref[...])) * _dot(hb, wsu_ref[...])).astype(BF16), wsd_ref[...])
    routed = jnp.concatenate(_unpack_pair(_load_token_words(routed_ref, (), tm)), axis=1)
    y = x + mod_ref[0, 5:6, :] * (routed + shared)
    ms = jnp.mean(y * y, axis=-1, keepdims=True)
    o_ref[...] = y * lax.rsqrt(ms + EPS) * fng_ref[...]


def _final(x1, routed2d, mod3, norm2_g, wsg, wsu, wsd, final_g, seq_len, mod_row_of_batch):
    t = x1.shape[0]
    tm = TM_FINAL

    def mod_idx(i):
        return (mod_row_of_batch((i * tm) // seq_len), 0, 0)

    def full(a):
        return pl.BlockSpec(a.shape, lambda i: (0,) * a.ndim)

    return pl.pallas_call(
        _final_kernel,
        grid=(t // tm,),
        in_specs=[pl.BlockSpec((tm, D_MODEL), lambda i: (i, 0)),
                  pl.BlockSpec((tm * ROW_SLABS, 128), lambda i: (i, 0)),
                  pl.BlockSpec((1, 6, D_MODEL), mod_idx),
                  full(norm2_g), full(wsg), full(wsu), full(wsd), full(final_g)],
        out_specs=pl.BlockSpec((tm, D_MODEL), lambda i: (i, 0)),
        out_shape=jax.ShapeDtypeStruct((t, D_MODEL), F32),
        compiler_params=pltpu.CompilerParams(dimension_semantics=("parallel",),
                                             vmem_limit_bytes=VMEM_LIMIT),
        name="final",
    )(x1, routed2d, mod3, norm2_g, wsg, wsu, wsd, final_g)


def _moe(x1, mod3, lw, seq_len, mod_row_of_batch, after=()):
    t = x1.shape[0]
    expert_rows = min(MAX_EXPERT_ROWS, TOP_K * t // N_EXPERTS // 2)
    n_tiles = TOP_K * t // expert_rows + N_EXPERTS
    n_tiles_pad = -(-n_tiles // 128) * 128
    hp2d, ek, rk, wtok, cnt = _router(x1, mod3, lw["norm2_g"], lw["w_router_t"], lw["router_bias"],
                                      seq_len, mod_row_of_batch)
    pos, texp, nused, tend = _plan(ek, rk, cnt, n_tiles_pad, expert_rows)
    pos3 = pos.reshape(TOP_K, t // SC_CHUNK, SC_CHUNK).transpose(1, 0, 2)
    xs = _sc_dispatch(hp2d.reshape(t, ROW_SLABS, 128), pos3, n_tiles * expert_rows,
                      after=(lw["weg"], lw["weu"], lw["wed"]) + tuple(after))
    ys2d = _experts(texp.reshape(-1), nused.reshape(-1), tend[:, 0], xs.reshape(-1, 128),
                    lw["weg"], lw["weu"], lw["wed"], n_tiles, expert_rows)
    routed = _sc_combine(ys2d.reshape(-1, ROW_SLABS, 128), pos3, wtok, t)
    y = _final(x1, routed.reshape(t * ROW_SLABS, 128), mod3, lw["norm2_g"],
               lw["wsg"], lw["wsu"], lw["wsd"], lw["final_g"], seq_len, mod_row_of_batch)
    return y, (pos, xs)


def _dft_tables(seq_len):
    gd = FOURIER_GROUP_DIM
    kc = np.arange(gd)
    ang_c = ((kc[:, None] * kc[None, :]) % gd) * (2.0 * math.pi / gd)
    cs = np.concatenate([np.cos(ang_c), np.sin(ang_c)], axis=1) * (gd ** -0.5)
    kl = np.arange(seq_len)
    ang_l = ((kl[:, None] * kl[None, :]) % seq_len) * (2.0 * math.pi / seq_len)
    cls = np.concatenate([np.cos(ang_l), -np.sin(ang_l)], axis=1) * (seq_len ** -0.5)
    return jnp.asarray(cs.astype(np.float32), dtype=BF16), jnp.asarray(cls.astype(np.float32), dtype=BF16)


def _rope_tables(length):
    rows = length // GRID_W
    r = np.repeat(np.arange(rows, dtype=np.float32), GRID_W)
    col = np.tile(np.arange(GRID_W, dtype=np.float32), rows)
    nf = RET_HEAD_DIM // 4
    inv = (np.float32(ROPE_BASE) ** (-np.arange(nf, dtype=np.float32) / np.float32(nf))).astype(np.float32)
    ar = r[:, None] * inv[None]
    ac = col[:, None] * inv[None]
    ang = np.concatenate([ar, ar, ac, ac], axis=-1).astype(np.float64)
    sign = np.where((np.arange(RET_HEAD_DIM) & nf) == 0, -1.0, 1.0)
    return (jnp.asarray(np.cos(ang).astype(np.float32)),
            jnp.asarray((np.sin(ang) * sign[None, :]).astype(np.float32)))


def _trunk_path(x, proj, row0, mod3, mod_row_of_batch, s0f, s0b, lw, after=()):
    batch, seq_len, _ = x.shape
    x2d = x.reshape(batch * seq_len, D_MODEL)
    uf, q, k, v, sg, gf, gr = proj
    mix_after, moe_after = ((), ()) if not after else ((after[0],), (after[1],))
    r, s_f, s_b = _retention(q, k, v, sg, lw["dec"], lw["gn_g"], s0f, s0b, batch, seq_len, row0, mix_after)
    cs, cls = _dft_tables(seq_len)
    x1 = _fnet_merge(uf, cs, cls, r, gf, gr, x2d, mod3, lw["w_four"], lw["w_ret"], lw["w_o"],
                     batch, seq_len, mod_row_of_batch, row0)
    y, plan = _moe(x1, mod3, lw, seq_len, mod_row_of_batch, moe_after)
    return y.reshape(batch, seq_len, D_MODEL), s_f, s_b, plan


def kernel(x_prompt, x_sample, state_ret_fwd, state_ret_bwd, c, c_ctx, w_ada, b_ada, norm1_g, norm2_g, w_in,
           ret_decay_fwd, ret_decay_bwd, ret_gn_g, w_four_out, w_ret_out, w_out, w_router, router_bias,
           w_exp_gate, w_exp_up, w_exp_down, w_shared_gate, w_shared_up, w_shared_down, final_norm_g):
    depth = w_ada.shape[0]
    assert depth == 1, "final norm is fused into the last layer's MoE kernel"
    n_ctx, n_lat = x_prompt.shape[0], x_sample.shape[0]
    cond = jnp.concatenate([c_ctx[None, :], c], axis=0)
    cond = jnp.pad(cond, ((0, (-cond.shape[0]) % 8), (0, 0)))
    rope = _rope_tables(x_sample.shape[1])
    zeros = jnp.zeros((n_ctx, N_RET_HEADS, RET_HEAD_DIM, RET_HEAD_DIM), F32)

    layer = 0
    mod = _ada(cond, w_ada[layer], b_ada[layer][None, :])
    mod3 = mod.reshape(mod.shape[0], 6, D_MODEL)
    dec = jnp.stack([ret_decay_fwd[layer], ret_decay_bwd[layer]], axis=1)
    lw = {
        "norm1_g": norm1_g[layer][None, :],
        "norm2_g": norm2_g[layer][None, :],
        "w_in": w_in[layer],
        "dec": jnp.broadcast_to(dec[:, :, None], (N_RET_HEADS, 2, RET_HEAD_DIM)).astype(F32),
        "gn_g": ret_gn_g[layer][None, :],
        "w_four": w_four_out[layer],
        "w_ret": w_ret_out[layer],
        "w_o": w_out[layer],
        "w_router_t": w_router[layer].T,
        "router_bias": router_bias[layer][:, None],
        "weg": _sc_pack_weight_halves(w_exp_gate[layer]),
        "weu": _sc_pack_weight_halves(w_exp_up[layer]),
        "wed": _sc_pack_weight_halves(w_exp_down[layer]),
        "wsg": w_shared_gate[layer].astype(BF16),
        "wsu": w_shared_up[layer].astype(BF16),
        "wsd": w_shared_down[layer].astype(BF16),
        "final_g": final_norm_g[None, :],
    }
    n_lat_tokens = x_sample.shape[0] * x_sample.shape[1]
    proj = _inproj(x_sample.reshape(n_lat_tokens, D_MODEL), x_prompt.reshape(-1, D_MODEL), x_sample.shape[1],
                   mod3, lw["norm1_g"], lw["w_in"], rope)
    y_sample, _, _, lat_plan = _trunk_path(x_sample, proj, 0, mod3, lambda b: 1 + b, state_ret_fwd[:, layer],
                                           state_ret_bwd[:, layer], lw)
    y_prompt, s_f, s_b, _ = _trunk_path(x_prompt, proj, n_lat_tokens, mod3, lambda b: 0, zeros, zeros, lw,
                                        after=lat_plan)
    return (y_prompt, y_sample, s_f[:, None], s_b[:, None])
```

```python
import functools
import math

import jax
import jax.numpy as jnp
import numpy as np
from jax import lax
from jax.experimental import pallas as pl
from jax.experimental.pallas import tpu as pltpu
from jax.experimental.pallas import tpu_sc as plsc

F32 = jnp.float32
BF16 = jnp.bfloat16

D_MODEL = 1024
GRID_W = 64
N_FOURIER_GROUPS = 8
FOURIER_GROUP_DIM = 128
N_RET_HEADS = 4
RET_HEAD_DIM = 128
RET_WIDTH = N_RET_HEADS * RET_HEAD_DIM
CHUNK = 128
N_EXPERTS = 64
N_EXPERT_GROUPS = 8
EXPERTS_PER_GROUP = N_EXPERTS // N_EXPERT_GROUPS
TOPK_GROUPS = 4
TOP_K = 8
EXPERT_DIM = 256
ROUTED_SCALE = 2.5
ROPE_BASE = 10000.0
EPS = 1e-6
Q_SCALE = RET_HEAD_DIM ** -0.5

_C_UF = (0, 1024)
_C_Q = (1024, 1536)
_C_K = (1536, 2048)
_C_V = (2048, 2560)
_C_G = (2560, 3072)
_C_GF = (3072, 4096)
_C_GR = (4096, 5120)

VMEM_LIMIT = 56 * 1024 * 1024

TM_INPROJ = 1024
TM_ROUTER = 1024
FNET_ROWS = 512
TM_FINAL = 1024
EXPERT_STEP_ROWS = 1024
MAX_EXPERT_ROWS = 512
WEIGHT_SLOTS = 3
ROW_SLABS = 4
SC_CORES = 2
SC_WORKERS = 32
SC_CHUNK = 128
SC_LANES = 16
SC_PACK_BLOCK_WORDS = 16384
SC_COMBINE_TOKENS = 8


def _silu(x):
    return x * jax.nn.sigmoid(x)


def _dot(a, b):
    return jnp.dot(a, b, preferred_element_type=F32)


def _rms_mod(x, g, shift, scale):
    ms = jnp.mean(x * x, axis=-1, keepdims=True)
    y = x * lax.rsqrt(ms + EPS) * g
    return y * (1.0 + scale) + shift


def _ada_kernel(cond_ref, w_ref, b_ref, o_ref):
    s = _silu(cond_ref[...]).astype(BF16)
    o_ref[...] = _dot(s, w_ref[...].astype(BF16)) + b_ref[...]


def _ada(cond, w_ada, b_ada):
    rows, n = cond.shape[0], w_ada.shape[1]
    tn = 1536
    return pl.pallas_call(
        _ada_kernel,
        grid=(n // tn,),
        in_specs=[pl.BlockSpec((rows, D_MODEL), lambda j: (0, 0)),
                  pl.BlockSpec((D_MODEL, tn), lambda j: (0, j)),
                  pl.BlockSpec((1, tn), lambda j: (0, j))],
        out_specs=pl.BlockSpec((rows, tn), lambda j: (0, j)),
        out_shape=jax.ShapeDtypeStruct((rows, n), F32),
        compiler_params=pltpu.CompilerParams(vmem_limit_bytes=VMEM_LIMIT),
        name="ada",
    )(cond, w_ada, b_ada)


def _rope_head(x, cos, sin_signed, first_half):
    partner = jnp.where(first_half, pltpu.roll(x, 96, 1), pltpu.roll(x, 32, 1))
    return x * cos + partner * sin_signed


def _inproj_kernel(*refs, use_rope):
    if use_rope:
        x_ref, mod_ref, g_ref, w_ref, cos_ref, sin_ref = refs[:6]
        outs = refs[6:]
    else:
        x_ref, mod_ref, g_ref, w_ref = refs[:4]
        outs = refs[4:]
    uf_o, q_o, k_o, v_o, sg_o, gf_o, gr_o = outs

    h = _rms_mod(x_ref[...], g_ref[...], mod_ref[0, 0:1, :], mod_ref[0, 1:2, :])
    hb = h.astype(BF16)

    def proj(cols):
        return _dot(hb, w_ref[:, cols[0]:cols[1]].astype(BF16))

    uf_o[...] = proj(_C_UF).astype(BF16)
    q = proj(_C_Q)
    k = proj(_C_K)
    if use_rope:
        cos = cos_ref[...]
        sin_signed = sin_ref[...]
        lane = lax.broadcasted_iota(jnp.int32, cos.shape, 1)
        first_half = (lane & 32) == 0
        for hd in range(N_RET_HEADS):
            sl = slice(hd * RET_HEAD_DIM, (hd + 1) * RET_HEAD_DIM)
            q_o[:, sl] = (_rope_head(q[:, sl], cos, sin_signed, first_half) * Q_SCALE).astype(BF16)
            k_o[:, sl] = _rope_head(k[:, sl], cos, sin_signed, first_half).astype(BF16)
    else:
        q_o[...] = (q * Q_SCALE).astype(BF16)
        k_o[...] = k.astype(BF16)
    v_o[...] = proj(_C_V).astype(BF16)
    sg_o[...] = _silu(proj(_C_G)).astype(BF16)
    gf_o[...] = jax.nn.sigmoid(proj(_C_GF)).astype(BF16)
    gr_o[...] = jax.nn.sigmoid(proj(_C_GR)).astype(BF16)


def _inproj(x2d, mod3, norm_g, w_in_f32, seq_len, mod_row_of_batch, rope):
    t = x2d.shape[0]
    tm = TM_INPROJ
    tiles_per_seq = max(seq_len // tm, 1)

    def mod_idx(i):
        return (mod_row_of_batch((i * tm) // seq_len), 0, 0)

    in_specs = [pl.BlockSpec((tm, D_MODEL), lambda i: (i, 0)),
                pl.BlockSpec((1, 6, D_MODEL), mod_idx),
                pl.BlockSpec((1, D_MODEL), lambda i: (0, 0)),
                pl.BlockSpec(w_in_f32.shape, lambda i: (0, 0), pipeline_mode=pl.Buffered(1))]
    args = [x2d, mod3, norm_g, w_in_f32]
    if rope is not None:
        in_specs += [pl.BlockSpec((tm, RET_HEAD_DIM), lambda i: (i % tiles_per_seq, 0))] * 2
        args += list(rope)
    widths = [1024, RET_WIDTH, RET_WIDTH, RET_WIDTH, RET_WIDTH, 1024, 1024]
    return pl.pallas_call(
        functools.partial(_inproj_kernel, use_rope=rope is not None),
        grid=(t // tm,),
        in_specs=in_specs,
        out_specs=[pl.BlockSpec((tm, w), lambda i: (i, 0)) for w in widths],
        out_shape=[jax.ShapeDtypeStruct((t, w), BF16) for w in widths],
        compiler_params=pltpu.CompilerParams(dimension_semantics=("parallel",),
                                             vmem_limit_bytes=VMEM_LIMIT),
        name="inproj",
    )(*args)


def _retention_kernel(q_ref, k_ref, v_ref, sg_ref, dec_ref, gn_ref, s0f_ref, s0b_ref,
                      r_ref, sfo_ref, sbo_ref, tab_scr, gc_scr):
    n_chunks = q_ref.shape[0] // CHUNK
    hd = RET_HEAD_DIM

    @pl.when(pl.program_id(0) == 0)
    def _():
        row = lax.broadcasted_iota(jnp.int32, (CHUNK, CHUNK), 0).astype(F32)
        col = lax.broadcasted_iota(jnp.int32, (CHUNK, CHUNK), 1).astype(F32)
        diff = row - col
        for h in range(N_RET_HEADS):
            dec = dec_ref[h]
            lg = jnp.minimum(dec, 0.0) - jnp.log1p(jnp.exp(-jnp.abs(dec)))
            lgf = lg[0:1, :]
            lgb = lg[1:2, :]
            tab_scr[h, 0] = jnp.exp(jnp.where(diff >= 0, lgf * diff, lgb * (-diff)))
            tab_scr[h, 1] = jnp.exp(lgf * (row + 1.0))
            tab_scr[h, 2] = jnp.exp(lgb * (CHUNK - row))
            tab_scr[h, 3] = jnp.exp(lgf * (CHUNK - 1.0 - col))
            tab_scr[h, 4] = jnp.exp(lgb * col)
            gc_scr[h] = jnp.exp(lg * CHUNK)

    def rows(n):
        return slice(n * CHUNK, (n + 1) * CHUNK)

    for h in range(N_RET_HEADS):
        cols = slice(h * hd, (h + 1) * hd)
        decay, qw_f, qw_b, kwt_f, kwt_b = (tab_scr[h, i] for i in range(5))
        gc = gc_scr[h]
        gc_f = gc[0:1, :]
        gc_b = gc[1:2, :]

        kv_f, kv_b = [], []
        for n in range(n_chunks):
            kt = k_ref[rows(n), cols].astype(F32).T
            vn = v_ref[rows(n), cols]
            kv_f.append(_dot((kt * kwt_f).astype(BF16), vn))
            kv_b.append(_dot((kt * kwt_b).astype(BF16), vn))

        s = s0f_ref[h]
        prev_f = []
        for n in range(n_chunks):
            prev_f.append(s.astype(BF16))
            s = gc_f * s + kv_f[n]
        sfo_ref[h] = s
        s = s0b_ref[h]
        prev_b = [None] * n_chunks
        for n in reversed(range(n_chunks)):
            prev_b[n] = s.astype(BF16)
            s = gc_b * s + kv_b[n]
        sbo_ref[h] = s

        gn = gn_ref[:, cols]
        for n in range(n_chunks):
            qn = q_ref[rows(n), cols]
            qf = qn.astype(F32)
            scores = lax.dot_general(qn, k_ref[rows(n), cols], (((1,), (1,)), ((), ())),
                                     preferred_element_type=F32)
            o = _dot((scores * decay).astype(BF16), v_ref[rows(n), cols])
            o = o + _dot((qf * qw_f).astype(BF16), prev_f[n])
            o = o + _dot((qf * qw_b).astype(BF16), prev_b[n])
            mu = jnp.mean(o, axis=-1, keepdims=True)
            d = o - mu
            var = jnp.mean(d * d, axis=-1, keepdims=True)
            on = d * lax.rsqrt(var + EPS) * gn
            r_ref[rows(n), cols] = (on * sg_ref[rows(n), cols].astype(F32)).astype(BF16)


def _retention(q, k, v, sg, dec, gn_g, s0f, s0b, batch, seq_len):
    hd = RET_HEAD_DIM
    tok_spec = pl.BlockSpec((seq_len, RET_WIDTH), lambda b: (b, 0))
    st_spec = pl.BlockSpec((None, N_RET_HEADS, hd, hd), lambda b: (b, 0, 0, 0))
    st_shape = jax.ShapeDtypeStruct((batch, N_RET_HEADS, hd, hd), F32)
    return pl.pallas_call(
        _retention_kernel,
        grid=(batch,),
        in_specs=[tok_spec, tok_spec, tok_spec, tok_spec,
                  pl.BlockSpec(dec.shape, lambda b: (0, 0, 0)),
                  pl.BlockSpec(gn_g.shape, lambda b: (0, 0)),
                  st_spec, st_spec],
        out_specs=[tok_spec, st_spec, st_spec],
        out_shape=[jax.ShapeDtypeStruct((batch * seq_len, RET_WIDTH), BF16), st_shape, st_shape],
        scratch_shapes=[pltpu.VMEM((N_RET_HEADS, 5, CHUNK, CHUNK), F32),
                        pltpu.VMEM((N_RET_HEADS, 2, hd), F32)],
        compiler_params=pltpu.CompilerParams(dimension_semantics=("arbitrary",),
                                             vmem_limit_bytes=VMEM_LIMIT),
        name="retention",
    )(q, k, v, sg, dec, gn_g, s0f, s0b)


def _fnet_merge_kernel(uf_ref, cs_ref, cls_ref, r_ref, gf_ref, gr_ref, x_ref, mod_ref, wf_ref, wr_ref, wo_ref,
                       o_ref, xcs_ref):
    seq_len = uf_ref.shape[0]
    gd = FOURIER_GROUP_DIM

    @pl.when(pl.program_id(1) == 0)
    def _():
        for g in range(N_FOURIER_GROUPS):
            x = _dot(uf_ref[:, g * gd:(g + 1) * gd], cs_ref[...])
            xcs_ref[0:seq_len, g * gd:(g + 1) * gd] = x[:, :gd].astype(BF16)
            xcs_ref[seq_len:2 * seq_len, g * gd:(g + 1) * gd] = x[:, gd:].astype(BF16)

    f_mix = _dot(cls_ref[...], xcs_ref[...]).astype(BF16)
    f_out = _dot(f_mix, wf_ref[...].astype(BF16))
    r_out = _dot(r_ref[...], wr_ref[...].astype(BF16))
    merged = gf_ref[...].astype(F32) * f_out + gr_ref[...].astype(F32) * r_out
    mix = _dot(merged.astype(BF16), wo_ref[...].astype(BF16))
    o_ref[...] = x_ref[...] + mod_ref[0, 2:3, :] * mix


def _fnet_merge(uf, cs, cls, r, gf, gr, x2d, mod3, w_four, w_ret, w_o, batch, seq_len, mod_row_of_batch):
    rb = min(FNET_ROWS, seq_len)
    nr = seq_len // rb

    def tok(w):
        return pl.BlockSpec((rb, w), lambda b, i: (b * nr + i, 0))

    def full(a):
        return pl.BlockSpec(a.shape, lambda b, i: (0, 0))

    def once(a):
        return pl.BlockSpec(a.shape, lambda b, i: (0, 0), pipeline_mode=pl.Buffered(1))

    return pl.pallas_call(
        _fnet_merge_kernel,
        grid=(batch, nr),
        in_specs=[pl.BlockSpec((seq_len, D_MODEL), lambda b, i: (b, 0)),
                  full(cs),
                  pl.BlockSpec((rb, 2 * seq_len), lambda b, i: (i, 0)),
                  tok(RET_WIDTH), tok(D_MODEL), tok(D_MODEL), tok(D_MODEL),
                  pl.BlockSpec((1, 6, D_MODEL), lambda b, i: (mod_row_of_batch(b), 0, 0)),
                  once(w_four), once(w_ret), once(w_o)],
        out_specs=tok(D_MODEL),
        out_shape=jax.ShapeDtypeStruct((batch * seq_len, D_MODEL), F32),
        scratch_shapes=[pltpu.VMEM((2 * seq_len, D_MODEL), BF16)],
        compiler_params=pltpu.CompilerParams(dimension_semantics=("parallel", "arbitrary"),
                                             vmem_limit_bytes=VMEM_LIMIT),
        name="fnet_merge",
    )(uf, cs, cls, r, gf, gr, x2d, mod3, w_four, w_ret, w_o)


def _pack_pair(lo_f32, hi_f32):
    lo = lax.bitcast_convert_type(lo_f32.astype(BF16).astype(F32), jnp.uint32)
    hi = lax.bitcast_convert_type(hi_f32.astype(BF16).astype(F32), jnp.uint32)
    return lax.bitcast_convert_type((lo >> 16) | hi, jnp.int32)


def _unpack_pair(words_i32):
    w = lax.bitcast_convert_type(words_i32, jnp.uint32)
    lo = lax.bitcast_convert_type(w << 16, F32)
    hi = lax.bitcast_convert_type(w & jnp.uint32(0xFFFF0000), F32)
    return lo, hi


def _load_token_words(ref, lead, n_tok):
    parts = []
    for s in range(ROW_SLABS):
        idx = (pl.ds(s, n_tok, stride=ROW_SLABS), slice(None))
        parts.append(ref[lead + idx] if lead else ref[idx])
    return jnp.concatenate(parts, axis=1)


def _store_token_words(ref, words, n_tok):
    for s in range(ROW_SLABS):
        ref[pl.ds(s, n_tok, stride=ROW_SLABS), :] = words[:, s * 128:(s + 1) * 128]


def _route(scores, biased):
    tokens = scores.shape[1]
    neg = -jnp.inf
    epg = EXPERTS_PER_GROUP
    iota_g = lax.broadcasted_iota(jnp.int32, (epg, tokens), 0).astype(F32)

    def pick_first_max(cur, iota, size):
        m = jnp.max(cur, axis=0, keepdims=True)
        idx = jnp.min(jnp.where(cur == m, iota, float(size)), axis=0, keepdims=True)
        return m, idx, iota == idx

    group_scores = []
    for g in range(N_EXPERT_GROUPS):
        vals = biased[g * epg:(g + 1) * epg, :]
        m1, _, hit = pick_first_max(vals, iota_g, epg)
        m2 = jnp.max(jnp.where(hit, neg, vals), axis=0, keepdims=True)
        group_scores.append(m1 + m2)
    cur = jnp.concatenate(group_scores, axis=0)
    group_sel = jnp.zeros_like(cur)
    for _ in range(TOPK_GROUPS):
        _, _, hit = pick_first_max(cur, iota_g, N_EXPERT_GROUPS)
        group_sel = jnp.where(hit, 1.0, group_sel)
        cur = jnp.where(hit, neg, cur)
    masked = jnp.concatenate(
        [jnp.where(group_sel[g:g + 1, :] > 0.0, biased[g * epg:(g + 1) * epg, :], neg)
         for g in range(N_EXPERT_GROUPS)], axis=0)
    iota_e = lax.broadcasted_iota(jnp.int32, masked.shape, 0).astype(F32)
    sel = jnp.zeros_like(masked)
    cur = masked
    picks = []
    for _ in range(TOP_K):
        _, idx, hit = pick_first_max(cur, iota_e, N_EXPERTS)
        picks.append(idx)
        sel = jnp.where(hit, 1.0, sel)
        cur = jnp.where(hit, neg, cur)
    w = scores * sel
    return w / jnp.sum(w, axis=0, keepdims=True) * ROUTED_SCALE, sel, picks


def _router_kernel(x_ref, mod_ref, g2_ref, wrt_ref, rb_ref, hp_ref, ek_ref, rk_ref, wt_ref, cnt_ref,
                   run_scr, earlier_scr):
    tm = x_ref.shape[0]

    @pl.when(pl.program_id(0) == 0)
    def _():
        run_scr[...] = jnp.zeros_like(run_scr)
        earlier = (lax.broadcasted_iota(jnp.int32, (tm, tm), 0) < lax.broadcasted_iota(jnp.int32, (tm, tm), 1))
        earlier_scr[...] = jnp.where(earlier, 1.0, 0.0).astype(BF16)

    h = _rms_mod(x_ref[...], g2_ref[...], mod_ref[0, 3:4, :], mod_ref[0, 4:5, :])
    half = D_MODEL // 2
    _store_token_words(hp_ref, _pack_pair(h[:, :half], h[:, half:]), tm)

    def split(a):
        hi = a.astype(BF16)
        return hi, (a - hi.astype(F32)).astype(BF16)

    def dot_nt(a, b):
        return lax.dot_general(a, b, (((1,), (1,)), ((), ())), preferred_element_type=F32)

    h_hi, h_lo = split(h)
    w_hi, w_lo = split(wrt_ref[...])
    logits_t = dot_nt(w_hi, h_hi) + (dot_nt(w_hi, h_lo) + dot_nt(w_lo, h_hi))
    scores = jax.nn.sigmoid(logits_t)
    comb_t, sel, picks = _route(scores, scores + rb_ref[...])

    rank_t = _dot(sel.astype(BF16), earlier_scr[...]) + run_scr[...]
    run_scr[...] += jnp.sum(sel, axis=1, keepdims=True)
    cnt_ref[...] = jnp.broadcast_to(run_scr[...], cnt_ref.shape)

    iota_e = lax.broadcasted_iota(jnp.int32, sel.shape, 0).astype(F32)
    ranks, weights = [], []
    for idx in picks:
        hit = iota_e == idx
        ranks.append(jnp.sum(jnp.where(hit, rank_t, 0.0), axis=0, keepdims=True))
        weights.append(jnp.sum(jnp.where(hit, comb_t, 0.0), axis=0, keepdims=True))
    ek_ref[...] = jnp.concatenate(picks, axis=0).astype(jnp.int32)
    rk_ref[...] = jnp.concatenate(ranks, axis=0).astype(jnp.int32)
    w_rep = jnp.concatenate([jnp.broadcast_to(w, (SC_LANES, tm)) for w in weights], axis=0)
    wt_ref[...] = w_rep.T


def _router(x1, mod3, norm2_g, w_router_t, router_bias, seq_len, mod_row_of_batch):
    t = x1.shape[0]
    tm = TM_ROUTER

    def mod_idx(i):
        return (mod_row_of_batch((i * tm) // seq_len), 0, 0)

    def full(a):
        return pl.BlockSpec(a.shape, lambda i: (0,) * a.ndim)

    return pl.pallas_call(
        _router_kernel,
        grid=(t // tm,),
        in_specs=[pl.BlockSpec((tm, D_MODEL), lambda i: (i, 0)),
                  pl.BlockSpec((1, 6, D_MODEL), mod_idx),
                  full(norm2_g), full(w_router_t), full(router_bias)],
        out_specs=[pl.BlockSpec((tm * ROW_SLABS, 128), lambda i: (i, 0)),
                   pl.BlockSpec((TOP_K, tm), lambda i: (0, i)),
                   pl.BlockSpec((TOP_K, tm), lambda i: (0, i)),
                   pl.BlockSpec((tm, 128), lambda i: (i, 0)),
                   pl.BlockSpec((N_EXPERTS, 128), lambda i: (0, 0))],
        out_shape=[jax.ShapeDtypeStruct((t * ROW_SLABS, 128), jnp.int32),
                   jax.ShapeDtypeStruct((TOP_K, t), jnp.int32),
                   jax.ShapeDtypeStruct((TOP_K, t), jnp.int32),
                   jax.ShapeDtypeStruct((t, 128), F32),
                   jax.ShapeDtypeStruct((N_EXPERTS, 128), F32)],
        scratch_shapes=[pltpu.VMEM((N_EXPERTS, 1), F32), pltpu.VMEM((tm, tm), BF16)],
        compiler_params=pltpu.CompilerParams(dimension_semantics=("arbitrary",),
                                             vmem_limit_bytes=VMEM_LIMIT),
        name="router",
    )(x1, mod3, norm2_g, w_router_t, router_bias)


def _plan_kernel(ek_ref, rk_ref, cnt_ref, pos_ref, texp_ref, nused_ref, tend_ref, *, expert_rows):
    rows = float(expert_rows)
    cnt = cnt_ref[:, 0:1]
    tiles = jnp.floor((cnt + (rows - 1.0)) / rows)
    before = (lax.broadcasted_iota(jnp.int32, (N_EXPERTS, N_EXPERTS), 1)
              < lax.broadcasted_iota(jnp.int32, (N_EXPERTS, N_EXPERTS), 0))
    tile_start = jnp.dot(jnp.where(before, 1.0, 0.0), jnp.broadcast_to(tiles, (N_EXPERTS, 128)),
                         precision=lax.Precision.HIGHEST, preferred_element_type=F32)[:, 0:1]
    tile_end = tile_start + tiles
    row_start = tile_start * rows

    ek = ek_ref[...]
    pos = rk_ref[...].astype(F32)
    tile_id = lax.broadcasted_iota(jnp.int32, texp_ref.shape, 1).astype(F32)
    texp = jnp.zeros(texp_ref.shape, F32)
    for e in range(N_EXPERTS):
        pos = pos + jnp.where(ek == e, row_start[e:e + 1, :], 0.0)
        texp = texp + jnp.where(tile_id >= tile_end[e:e + 1, :], 1.0, 0.0)
    pos_ref[...] = pos.astype(jnp.int32)
    texp_ref[...] = jnp.minimum(texp, N_EXPERTS - 1.0).astype(jnp.int32)
    nused_ref[...] = jnp.broadcast_to(tile_end[N_EXPERTS - 1:N_EXPERTS, :], nused_ref.shape).astype(jnp.int32)
    tend_ref[...] = jnp.broadcast_to(tile_end, tend_ref.shape).astype(jnp.int32)


def _plan(ek, rk, cnt, n_tiles_pad, expert_rows):
    t = ek.shape[1]

    def full(shape):
        return pl.BlockSpec(shape, lambda: (0,) * len(shape))

    return pl.pallas_call(
        functools.partial(_plan_kernel, expert_rows=expert_rows),
        in_specs=[full(ek.shape), full(rk.shape), full(cnt.shape)],
        out_specs=[full((TOP_K, t)), full((1, n_tiles_pad)), full((1, 128)), full((N_EXPERTS, 128))],
        out_shape=[jax.ShapeDtypeStruct((TOP_K, t), jnp.int32),
                   jax.ShapeDtypeStruct((1, n_tiles_pad), jnp.int32),
                   jax.ShapeDtypeStruct((1, 128), jnp.int32),
                   jax.ShapeDtypeStruct((N_EXPERTS, 128), jnp.int32)],
        compiler_params=pltpu.CompilerParams(vmem_limit_bytes=VMEM_LIMIT),
        name="plan",
    )(ek, rk, cnt)


def _sc_mesh():
    return plsc.VectorSubcoreMesh(core_axis_name="c", subcore_axis_name="s")


def _sc_pack_weight_halves(w):
    e, k, n = w.shape
    k_half = k // 2
    rb = SC_PACK_BLOCK_WORDS // n
    units_per_expert = k_half // rb
    per_w = (e * units_per_expert) // SC_WORKERS
    lanes = SC_LANES

    @functools.partial(
        pl.kernel, out_type=jax.ShapeDtypeStruct((e * k_half, n), jnp.int32), mesh=_sc_mesh(),
        scratch_types=[pltpu.VMEM((rb, n), F32), pltpu.VMEM((rb, n), F32), pltpu.VMEM((rb, n), jnp.int32)],
        compiler_params=pltpu.CompilerParams(needs_layout_passes=False))
    def kern(w_hbm, out_hbm, a_v, b_v, o_v):
        wid = lax.axis_index("s") * SC_CORES + lax.axis_index("c")

        @pl.loop(0, per_w)
        def _(j):
            unit = wid * per_w + j
            expert = unit // units_per_expert
            blk = unit % units_per_expert
            row_a = expert * k + blk * rb
            pltpu.sync_copy(w_hbm.at[pl.ds(row_a, rb)], a_v)
            pltpu.sync_copy(w_hbm.at[pl.ds(row_a + k_half, rb)], b_v)

            @pl.loop(0, rb)
            def _(r):
                @plsc.parallel_loop(0, n, step=lanes, unroll=4)
                def _(c):
                    both = plsc.pack(a_v[r, pl.ds(c, lanes)], b_v[r, pl.ds(c, lanes)],
                                     format=plsc.PackFormat.INTERLEAVED)
                    o_v[r, pl.ds(c, lanes)] = plsc.bitcast(both, jnp.int32)

            pltpu.sync_copy(o_v, out_hbm.at[pl.ds(expert * k_half + blk * rb, rb)])

    return kern(w.reshape(e * k, n)).reshape(e, k_half, n)


def _sc_dispatch(rows, pos3, n_out, after=()):
    t = rows.shape[0]
    ch = SC_CHUNK
    per_w = (t // ch) // SC_WORKERS

    @functools.partial(
        pl.kernel, out_type=jax.ShapeDtypeStruct((n_out,) + rows.shape[1:], jnp.int32), mesh=_sc_mesh(),
        scratch_types=[pltpu.VMEM((TOP_K, ch), jnp.int32), pltpu.VMEM((ch,) + rows.shape[1:], jnp.int32),
                       pltpu.SemaphoreType.DMA])
    def k(rows_hbm, pos_hbm, *rest):
        out_hbm, idx_v, rows_v, sem = rest[len(after):]
        wid = lax.axis_index("s") * SC_CORES + lax.axis_index("c")

        @pl.loop(0, per_w)
        def _(j):
            c = wid * per_w + j
            pltpu.sync_copy(pos_hbm.at[c], idx_v)
            pltpu.sync_copy(rows_hbm.at[pl.ds(c * ch, ch)], rows_v)
            copies = [pltpu.async_copy(rows_v, out_hbm.at[idx_v.at[kk]], sem) for kk in range(TOP_K)]
            for cp in copies:
                cp.wait()

    return k(rows, pos3, *after)


def _sc_combine(table, pos3, wtok, t):
    ch = SC_CHUNK
    sub = SC_COMBINE_TOKENS
    lanes = SC_LANES
    slabs = ROW_SLABS
    per_w = (t // ch) // SC_WORKERS
    subs_per_chunk = ch // sub
    n_steps = per_w * subs_per_chunk

    @functools.partial(
        pl.kernel, out_type=jax.ShapeDtypeStruct((t, slabs, 128), jnp.int32), mesh=_sc_mesh(),
        scratch_types=[pltpu.VMEM((per_w, TOP_K, ch), jnp.int32),
                       pltpu.VMEM((2, TOP_K, sub, slabs, 128), jnp.int32),
                       pltpu.VMEM((2, sub, 128), F32),
                       pltpu.VMEM((sub, slabs, 128), jnp.int32),
                       pltpu.SemaphoreType.DMA((2,))],
        compiler_params=pltpu.CompilerParams(needs_layout_passes=False))
    def k(tab_hbm, pos_hbm, w_hbm, out_hbm, idx_v, rows_v, w_v, out_v, sem):
        wid = lax.axis_index("s") * SC_CORES + lax.axis_index("c")
        for j in range(per_w):
            pltpu.sync_copy(pos_hbm.at[wid * per_w + j], idx_v.at[j])

        def first_token(step):
            return (wid * per_w + step // subs_per_chunk) * ch + (step % subs_per_chunk) * sub

        def copies(step, slot):
            j = step // subs_per_chunk
            s = step % subs_per_chunk
            idx = [idx_v.at[j, kk, pl.ds(s * sub, sub)] for kk in range(TOP_K)]
            return ([pltpu.make_async_copy(tab_hbm.at[idx[kk]], rows_v.at[slot, kk], sem.at[slot])
                     for kk in range(TOP_K)]
                    + [pltpu.make_async_copy(w_hbm.at[pl.ds(first_token(step), sub)], w_v.at[slot], sem.at[slot])])

        for cp in copies(0, 0):
            cp.start()

        @pl.loop(0, n_steps)
        def _(step):
            slot = step % 2

            @pl.when(step + 1 < n_steps)
            def _():
                for cp in copies(step + 1, 1 - slot):
                    cp.start()

            for cp in copies(step, slot):
                cp.wait()

            @pl.loop(0, sub)
            def _(tt):
                wk = [w_v[slot, tt, pl.ds(kk * lanes, lanes)] for kk in range(TOP_K)]
                for sl in range(slabs):
                    @plsc.parallel_loop(0, 128, step=lanes, unroll=4)
                    def _(off):
                        acc_lo = jnp.zeros((lanes,), F32)
                        acc_hi = jnp.zeros((lanes,), F32)
                        for kk in range(TOP_K):
                            word = rows_v[slot, kk, tt, sl, pl.ds(off, lanes)]
                            lo = plsc.bitcast(word << 16, F32)
                            hi = plsc.bitcast(word & jnp.int32(-65536), F32)
                            acc_lo = acc_lo + wk[kk] * lo
                            acc_hi = acc_hi + wk[kk] * hi
                        both = plsc.pack(acc_lo, acc_hi, format=plsc.PackFormat.INTERLEAVED)
                        out_v[tt, sl, pl.ds(off, lanes)] = plsc.bitcast(both, jnp.int32)

            pltpu.sync_copy(out_v, out_hbm.at[pl.ds(first_token(step), sub)])

    return k(table, pos3, wtok)


def _experts_kernel(texp_ref, nused_ref, tend_ref, xs_ref, weg_hbm, weu_hbm, wed_hbm, ys_ref,
                    wg_scr, wu_scr, wd_scr, wg_buf, wu_buf, wd_buf, sem, group_scr, *, expert_rows):
    step = pl.program_id(0)
    rows = expert_rows
    tiles_per_step = EXPERT_STEP_ROWS // expert_rows
    half = D_MODEL // 2
    n_used = nused_ref[0]

    def weight_copies(e, slot):
        return [pltpu.make_async_copy(weg_hbm.at[e], wg_buf.at[slot], sem.at[slot, 0]),
                pltpu.make_async_copy(weu_hbm.at[e], wu_buf.at[slot], sem.at[slot, 1]),
                pltpu.make_async_copy(wed_hbm.at[e], wd_buf.at[slot], sem.at[slot, 2])]

    def next_group(e):
        tile = tend_ref[e]
        return texp_ref[jnp.minimum(tile, n_used - 1)], tile < n_used

    def start_weights(e, slot, exists):
        @pl.when(exists)
        def _():
            for cp in weight_copies(e, slot):
                cp.start()

    @pl.when(step == 0)
    def _():
        group_scr[0] = 0
        e, exists = texp_ref[0], True
        for slot in range(WEIGHT_SLOTS - 1):
            start_weights(e, slot, exists)
            nxt, has_next = next_group(e)
            e, exists = nxt, exists & has_next

    def row_tile(tile, x_view, y_view):
        expert = texp_ref[tile]
        used = tile < n_used
        new_expert = (tile == 0) | (expert != texp_ref[jnp.maximum(tile - 1, 0)])

        @pl.when(used & new_expert)
        def _():
            group = group_scr[0]
            slot = group % WEIGHT_SLOTS
            ahead, exists = expert, True
            for _ in range(WEIGHT_SLOTS - 1):
                nxt, has_next = next_group(ahead)
                ahead, exists = nxt, exists & has_next
            start_weights(ahead, (group + WEIGHT_SLOTS - 1) % WEIGHT_SLOTS, exists)

            for cp in weight_copies(expert, slot):
                cp.wait()
            for scr, buf in ((wg_scr, wg_buf), (wu_scr, wu_buf), (wd_scr, wd_buf)):
                top, bottom = _unpack_pair(buf[slot])
                k_half = top.shape[0]
                scr[0:k_half, :] = top.astype(BF16)
                scr[k_half:2 * k_half, :] = bottom.astype(BF16)
            group_scr[0] = group + 1

        @pl.when(used)
        def _():
            lo, hi = _unpack_pair(_load_token_words(x_view, (), rows))
            lo = lo.astype(BF16)
            hi = hi.astype(BF16)
            g = _dot(lo, wg_scr[0:half, :]) + _dot(hi, wg_scr[half:D_MODEL, :])
            u = _dot(lo, wu_scr[0:half, :]) + _dot(hi, wu_scr[half:D_MODEL, :])
            y = _dot((_silu(g) * u).astype(BF16), wd_scr[...])
            _store_token_words(y_view, _pack_pair(y[:, :half], y[:, half:]), rows)

        @pl.when(jnp.logical_not(used) & (step == (n_used - 1) // tiles_per_step))
        def _():
            y_view[...] = jnp.zeros_like(y_view)

    for s in range(tiles_per_step):
        view = pl.ds(s * rows * ROW_SLABS, rows * ROW_SLABS)
        row_tile(step * tiles_per_step + s, xs_ref.at[view], ys_ref.at[view])


def _experts(texp, nused, tend, xs2d, weg, weu, wed, n_tiles, expert_rows):
    tiles_per_step = EXPERT_STEP_ROWS // expert_rows
    block = (EXPERT_STEP_ROWS * ROW_SLABS, 128)
    hbm = pl.BlockSpec(memory_space=pl.ANY)

    def block_idx(j, te, nu, tn):
        return (jnp.minimum(j, (nu[0] - 1) // tiles_per_step), 0)

    grid_spec = pltpu.PrefetchScalarGridSpec(
        num_scalar_prefetch=3,
        grid=(n_tiles // tiles_per_step,),
        in_specs=[pl.BlockSpec(block, block_idx), hbm, hbm, hbm],
        out_specs=pl.BlockSpec(block, block_idx),
        scratch_shapes=[pltpu.VMEM((D_MODEL, EXPERT_DIM), BF16),
                        pltpu.VMEM((D_MODEL, EXPERT_DIM), BF16),
                        pltpu.VMEM((EXPERT_DIM, D_MODEL), BF16),
                        pltpu.VMEM((WEIGHT_SLOTS,) + weg.shape[1:], jnp.int32),
                        pltpu.VMEM((WEIGHT_SLOTS,) + weu.shape[1:], jnp.int32),
                        pltpu.VMEM((WEIGHT_SLOTS,) + wed.shape[1:], jnp.int32),
                        pltpu.SemaphoreType.DMA((WEIGHT_SLOTS, 3)),
                        pltpu.SMEM((1,), jnp.int32)],
    )
    return pl.pallas_call(
        functools.partial(_experts_kernel, expert_rows=expert_rows),
        grid_spec=grid_spec,
        out_shape=jax.ShapeDtypeStruct(xs2d.shape, jnp.int32),
        compiler_params=pltpu.CompilerParams(dimension_semantics=("arbitrary",),
                                             vmem_limit_bytes=VMEM_LIMIT),
        name="experts",
    )(texp, nused, tend, xs2d, weg, weu, wed)


def _final_kernel(x_ref, routed_ref, mod_ref, g2_ref, wsg_ref, wsu_ref, wsd_ref, fng_ref, o_ref):
    tm = x_ref.shape[0]
    x = x_ref[...]
    hb = _rms_mod(x, g2_ref[...], mod_ref[0, 3:4, :], mod_ref[0, 4:5, :]).astype(BF16)
    shared = _dot((_silu(_dot(hb, wsg_ref[...])) * _dot(hb, wsu_ref[...])).astype(BF16), wsd_ref[...])
    routed = jnp.concatenate(_unpack_pair(_load_token_words(routed_ref, (), tm)), axis=1)
    y = x + mod_ref[0, 5:6, :] * (routed + shared)
    ms = jnp.mean(y * y, axis=-1, keepdims=True)
    o_ref[...] = y * lax.rsqrt(ms + EPS) * fng_ref[...]


def _final(x1, routed2d, mod3, norm2_g, wsg, wsu, wsd, final_g, seq_len, mod_row_of_batch):
    t = x1.shape[0]
    tm = TM_FINAL

    def mod_idx(i):
        return (mod_row_of_batch((i * tm) // seq_len), 0, 0)

    def full(a):
        return pl.BlockSpec(a.shape, lambda i: (0,) * a.ndim)

    return pl.pallas_call(
        _final_kernel,
        grid=(t // tm,),
        in_specs=[pl.BlockSpec((tm, D_MODEL), lambda i: (i, 0)),
                  pl.BlockSpec((tm * ROW_SLABS, 128), lambda i: (i, 0)),
                  pl.BlockSpec((1, 6, D_MODEL), mod_idx),
                  full(norm2_g), full(wsg), full(wsu), full(wsd), full(final_g)],
        out_specs=pl.BlockSpec((tm, D_MODEL), lambda i: (i, 0)),
        out_shape=jax.ShapeDtypeStruct((t, D_MODEL), F32),
        compiler_params=pltpu.CompilerParams(dimension_semantics=("parallel",),
                                             vmem_limit_bytes=VMEM_LIMIT),
        name="final",
    )(x1, routed2d, mod3, norm2_g, wsg, wsu, wsd, final_g)


def _moe(x1, mod3, lw, seq_len, mod_row_of_batch):
    t = x1.shape[0]
    expert_rows = min(MAX_EXPERT_ROWS, TOP_K * t // N_EXPERTS // 2)
    n_tiles = TOP_K * t // expert_rows + N_EXPERTS
    n_tiles_pad = -(-n_tiles // 128) * 128
    hp2d, ek, rk, wtok, cnt = _router(x1, mod3, lw["norm2_g"], lw["w_router_t"], lw["router_bias"],
                                      seq_len, mod_row_of_batch)
    pos, texp, nused, tend = _plan(ek, rk, cnt, n_tiles_pad, expert_rows)
    pos3 = pos.reshape(TOP_K, t // SC_CHUNK, SC_CHUNK).transpose(1, 0, 2)
    xs = _sc_dispatch(hp2d.reshape(t, ROW_SLABS, 128), pos3, n_tiles * expert_rows,
                      after=(lw["weg"], lw["weu"], lw["wed"]))
    ys2d = _experts(texp.reshape(-1), nused.reshape(-1), tend[:, 0], xs.reshape(-1, 128),
                    lw["weg"], lw["weu"], lw["wed"], n_tiles, expert_rows)
    routed = _sc_combine(ys2d.reshape(-1, ROW_SLABS, 128), pos3, wtok, t)
    return _final(x1, routed.reshape(t * ROW_SLABS, 128), mod3, lw["norm2_g"],
                  lw["wsg"], lw["wsu"], lw["wsd"], lw["final_g"], seq_len, mod_row_of_batch)


def _dft_tables(seq_len):
    gd = FOURIER_GROUP_DIM
    kc = np.arange(gd)
    ang_c = ((kc[:, None] * kc[None, :]) % gd) * (2.0 * math.pi / gd)
    cs = np.concatenate([np.cos(ang_c), np.sin(ang_c)], axis=1) * (gd ** -0.5)
    kl = np.arange(seq_len)
    ang_l = ((kl[:, None] * kl[None, :]) % seq_len) * (2.0 * math.pi / seq_len)
    cls = np.concatenate([np.cos(ang_l), -np.sin(ang_l)], axis=1) * (seq_len ** -0.5)
    return jnp.asarray(cs.astype(np.float32), dtype=BF16), jnp.asarray(cls.astype(np.float32), dtype=BF16)


def _rope_tables(length):
    rows = length // GRID_W
    r = np.repeat(np.arange(rows, dtype=np.float32), GRID_W)
    col = np.tile(np.arange(GRID_W, dtype=np.float32), rows)
    nf = RET_HEAD_DIM // 4
    inv = (np.float32(ROPE_BASE) ** (-np.arange(nf, dtype=np.float32) / np.float32(nf))).astype(np.float32)
    ar = r[:, None] * inv[None]
    ac = col[:, None] * inv[None]
    ang = np.concatenate([ar, ar, ac, ac], axis=-1).astype(np.float64)
    sign = np.where((np.arange(RET_HEAD_DIM) & nf) == 0, -1.0, 1.0)
    return (jnp.asarray(np.cos(ang).astype(np.float32)),
            jnp.asarray((np.sin(ang) * sign[None, :]).astype(np.float32)))


def _trunk_path(x, mod3, mod_row_of_batch, s0f, s0b, rope, lw):
    batch, seq_len, _ = x.shape
    x2d = x.reshape(batch * seq_len, D_MODEL)
    uf, q, k, v, sg, gf, gr = _inproj(x2d, mod3, lw["norm1_g"], lw["w_in"], seq_len, mod_row_of_batch, rope)
    r, s_f, s_b = _retention(q, k, v, sg, lw["dec"], lw["gn_g"], s0f, s0b, batch, seq_len)
    cs, cls = _dft_tables(seq_len)
    x1 = _fnet_merge(uf, cs, cls, r, gf, gr, x2d, mod3, lw["w_four"], lw["w_ret"], lw["w_o"],
                     batch, seq_len, mod_row_of_batch)
    y = _moe(x1, mod3, lw, seq_len, mod_row_of_batch)
    return y.reshape(batch, seq_len, D_MODEL), s_f, s_b


def kernel(x_prompt, x_sample, state_ret_fwd, state_ret_bwd, c, c_ctx, w_ada, b_ada, norm1_g, norm2_g, w_in,
           ret_decay_fwd, ret_decay_bwd, ret_gn_g, w_four_out, w_ret_out, w_out, w_router, router_bias,
           w_exp_gate, w_exp_up, w_exp_down, w_shared_gate, w_shared_up, w_shared_down, final_norm_g):
    depth = w_ada.shape[0]
    assert depth == 1, "final norm is fused into the last layer's MoE kernel"
    n_ctx, n_lat = x_prompt.shape[0], x_sample.shape[0]
    cond = jnp.concatenate([c_ctx[None, :], c], axis=0)
    cond = jnp.pad(cond, ((0, (-cond.shape[0]) % 8), (0, 0)))
    rope = _rope_tables(x_sample.shape[1])
    zeros = jnp.zeros((n_ctx, N_RET_HEADS, RET_HEAD_DIM, RET_HEAD_DIM), F32)

    layer = 0
    mod = _ada(cond, w_ada[layer], b_ada[layer][None, :])
    mod3 = mod.reshape(mod.shape[0], 6, D_MODEL)
    dec = jnp.stack([ret_decay_fwd[layer], ret_decay_bwd[layer]], axis=1)
    lw = {
        "norm1_g": norm1_g[layer][None, :],
        "norm2_g": norm2_g[layer][None, :],
        "w_in": w_in[layer],
        "dec": jnp.broadcast_to(dec[:, :, None], (N_RET_HEADS, 2, RET_HEAD_DIM)).astype(F32),
        "gn_g": ret_gn_g[layer][None, :],
        "w_four": w_four_out[layer],
        "w_ret": w_ret_out[layer],
        "w_o": w_out[layer],
        "w_router_t": w_router[layer].T,
        "router_bias": router_bias[layer][:, None],
        "weg": _sc_pack_weight_halves(w_exp_gate[layer]),
        "weu": _sc_pack_weight_halves(w_exp_up[layer]),
        "wed": _sc_pack_weight_halves(w_exp_down[layer]),
        "wsg": w_shared_gate[layer].astype(BF16),
        "wsu": w_shared_up[layer].astype(BF16),
        "wsd": w_shared_down[layer].astype(BF16),
        "final_g": final_norm_g[None, :],
    }
    y_prompt, s_f, s_b = _trunk_path(x_prompt, mod3, lambda b: 0, zeros, zeros, None, lw)
    y_sample, _, _ = _trunk_path(x_sample, mod3, lambda b: 1 + b, state_ret_fwd[:, layer],
                                 state_ret_bwd[:, layer], rope, lw)
    return (y_prompt, y_sample, s_f[:, None], s_b[:, None])
```

```python
import functools
import math

import jax
import jax.numpy as jnp
import numpy as np
from jax import lax
from jax.experimental import pallas as pl
from jax.experimental.pallas import tpu as pltpu
from jax.experimental.pallas import tpu_sc as plsc

F32 = jnp.float32
BF16 = jnp.bfloat16

D_MODEL = 1024
GRID_W = 64
N_FOURIER_GROUPS = 8
FOURIER_GROUP_DIM = 128
N_RET_HEADS = 4
RET_HEAD_DIM = 128
RET_WIDTH = N_RET_HEADS * RET_HEAD_DIM
CHUNK = 128
N_EXPERTS = 64
N_EXPERT_GROUPS = 8
EXPERTS_PER_GROUP = N_EXPERTS // N_EXPERT_GROUPS
TOPK_GROUPS = 4
TOP_K = 8
EXPERT_DIM = 256
ROUTED_SCALE = 2.5
ROPE_BASE = 10000.0
EPS = 1e-6
Q_SCALE = RET_HEAD_DIM ** -0.5

_C_UF = (0, 1024)
_C_Q = (1024, 1536)
_C_K = (1536, 2048)
_C_V = (2048, 2560)
_C_G = (2560, 3072)
_C_GF = (3072, 4096)
_C_GR = (4096, 5120)

VMEM_LIMIT = 56 * 1024 * 1024

TM_INPROJ = 1024
TM_ROUTER = 512
FNET_ROWS = 512
TM_FINAL = 1024
EXPERT_STEP_ROWS = 1024
MAX_EXPERT_ROWS = 512
WEIGHT_SLOTS = 3
ROW_SLABS = 4
SC_CORES = 2
SC_WORKERS = 32
SC_CHUNK = 128
SC_LANES = 16
SC_PACK_BLOCK_WORDS = 16384
SC_COMBINE_TOKENS = 8


def _silu(x):
    return x * jax.nn.sigmoid(x)


def _dot(a, b):
    return jnp.dot(a, b, preferred_element_type=F32)


def _rms_mod(x, g, shift, scale):
    ms = jnp.mean(x * x, axis=-1, keepdims=True)
    y = x * lax.rsqrt(ms + EPS) * g
    return y * (1.0 + scale) + shift


def _ada_kernel(cond_ref, w_ref, b_ref, o_ref):
    s = _silu(cond_ref[...]).astype(BF16)
    o_ref[...] = _dot(s, w_ref[...].astype(BF16)) + b_ref[...]


def _ada(cond, w_ada, b_ada):
    rows, n = cond.shape[0], w_ada.shape[1]
    tn = 1536
    return pl.pallas_call(
        _ada_kernel,
        grid=(n // tn,),
        in_specs=[pl.BlockSpec((rows, D_MODEL), lambda j: (0, 0)),
                  pl.BlockSpec((D_MODEL, tn), lambda j: (0, j)),
                  pl.BlockSpec((1, tn), lambda j: (0, j))],
        out_specs=pl.BlockSpec((rows, tn), lambda j: (0, j)),
        out_shape=jax.ShapeDtypeStruct((rows, n), F32),
        compiler_params=pltpu.CompilerParams(vmem_limit_bytes=VMEM_LIMIT),
        name="ada",
    )(cond, w_ada, b_ada)


def _rope_head(x, cos, sin_signed, first_half):
    partner = jnp.where(first_half, pltpu.roll(x, 96, 1), pltpu.roll(x, 32, 1))
    return x * cos + partner * sin_signed


def _inproj_kernel(*refs, use_rope):
    if use_rope:
        x_ref, mod_ref, g_ref, w_ref, cos_ref, sin_ref = refs[:6]
        outs = refs[6:]
    else:
        x_ref, mod_ref, g_ref, w_ref = refs[:4]
        outs = refs[4:]
    uf_o, q_o, k_o, v_o, sg_o, gf_o, gr_o = outs

    h = _rms_mod(x_ref[...], g_ref[...], mod_ref[0, 0:1, :], mod_ref[0, 1:2, :])
    hb = h.astype(BF16)

    def proj(cols):
        return _dot(hb, w_ref[:, cols[0]:cols[1]].astype(BF16))

    uf_o[...] = proj(_C_UF).astype(BF16)
    q = proj(_C_Q)
    k = proj(_C_K)
    if use_rope:
        cos = cos_ref[...]
        sin_signed = sin_ref[...]
        lane = lax.broadcasted_iota(jnp.int32, cos.shape, 1)
        first_half = (lane & 32) == 0
        for hd in range(N_RET_HEADS):
            sl = slice(hd * RET_HEAD_DIM, (hd + 1) * RET_HEAD_DIM)
            q_o[:, sl] = (_rope_head(q[:, sl], cos, sin_signed, first_half) * Q_SCALE).astype(BF16)
            k_o[:, sl] = _rope_head(k[:, sl], cos, sin_signed, first_half).astype(BF16)
    else:
        q_o[...] = (q * Q_SCALE).astype(BF16)
        k_o[...] = k.astype(BF16)
    v_o[...] = proj(_C_V).astype(BF16)
    sg_o[...] = _silu(proj(_C_G)).astype(BF16)
    gf_o[...] = jax.nn.sigmoid(proj(_C_GF)).astype(BF16)
    gr_o[...] = jax.nn.sigmoid(proj(_C_GR)).astype(BF16)


def _inproj(x2d, mod3, norm_g, w_in_f32, seq_len, mod_row_of_batch, rope):
    t = x2d.shape[0]
    tm = TM_INPROJ
    tiles_per_seq = max(seq_len // tm, 1)

    def mod_idx(i):
        return (mod_row_of_batch((i * tm) // seq_len), 0, 0)

    in_specs = [pl.BlockSpec((tm, D_MODEL), lambda i: (i, 0)),
                pl.BlockSpec((1, 6, D_MODEL), mod_idx),
                pl.BlockSpec((1, D_MODEL), lambda i: (0, 0)),
                pl.BlockSpec(w_in_f32.shape, lambda i: (0, 0), pipeline_mode=pl.Buffered(1))]
    args = [x2d, mod3, norm_g, w_in_f32]
    if rope is not None:
        in_specs += [pl.BlockSpec((tm, RET_HEAD_DIM), lambda i: (i % tiles_per_seq, 0))] * 2
        args += list(rope)
    widths = [1024, RET_WIDTH, RET_WIDTH, RET_WIDTH, RET_WIDTH, 1024, 1024]
    return pl.pallas_call(
        functools.partial(_inproj_kernel, use_rope=rope is not None),
        grid=(t // tm,),
        in_specs=in_specs,
        out_specs=[pl.BlockSpec((tm, w), lambda i: (i, 0)) for w in widths],
        out_shape=[jax.ShapeDtypeStruct((t, w), BF16) for w in widths],
        compiler_params=pltpu.CompilerParams(dimension_semantics=("parallel",),
                                             vmem_limit_bytes=VMEM_LIMIT),
        name="inproj",
    )(*args)


def _retention_kernel(q_ref, k_ref, v_ref, sg_ref, dec_ref, gn_ref, s0f_ref, s0b_ref,
                      r_ref, sfo_ref, sbo_ref, tab_scr, gc_scr):
    n_chunks = q_ref.shape[0] // CHUNK
    hd = RET_HEAD_DIM

    @pl.when(pl.program_id(0) == 0)
    def _():
        row = lax.broadcasted_iota(jnp.int32, (CHUNK, CHUNK), 0).astype(F32)
        col = lax.broadcasted_iota(jnp.int32, (CHUNK, CHUNK), 1).astype(F32)
        diff = row - col
        for h in range(N_RET_HEADS):
            dec = dec_ref[h]
            lg = jnp.minimum(dec, 0.0) - jnp.log1p(jnp.exp(-jnp.abs(dec)))
            lgf = lg[0:1, :]
            lgb = lg[1:2, :]
            tab_scr[h, 0] = jnp.exp(jnp.where(diff >= 0, lgf * diff, lgb * (-diff)))
            tab_scr[h, 1] = jnp.exp(lgf * (row + 1.0))
            tab_scr[h, 2] = jnp.exp(lgb * (CHUNK - row))
            tab_scr[h, 3] = jnp.exp(lgf * (CHUNK - 1.0 - col))
            tab_scr[h, 4] = jnp.exp(lgb * col)
            gc_scr[h] = jnp.exp(lg * CHUNK)

    def rows(n):
        return slice(n * CHUNK, (n + 1) * CHUNK)

    for h in range(N_RET_HEADS):
        cols = slice(h * hd, (h + 1) * hd)
        decay, qw_f, qw_b, kwt_f, kwt_b = (tab_scr[h, i] for i in range(5))
        gc = gc_scr[h]
        gc_f = gc[0:1, :]
        gc_b = gc[1:2, :]

        kv_f, kv_b = [], []
        for n in range(n_chunks):
            kt = k_ref[rows(n), cols].astype(F32).T
            vn = v_ref[rows(n), cols]
            kv_f.append(_dot((kt * kwt_f).astype(BF16), vn))
            kv_b.append(_dot((kt * kwt_b).astype(BF16), vn))

        s = s0f_ref[h]
        prev_f = []
        for n in range(n_chunks):
            prev_f.append(s.astype(BF16))
            s = gc_f * s + kv_f[n]
        sfo_ref[h] = s
        s = s0b_ref[h]
        prev_b = [None] * n_chunks
        for n in reversed(range(n_chunks)):
            prev_b[n] = s.astype(BF16)
            s = gc_b * s + kv_b[n]
        sbo_ref[h] = s

        gn = gn_ref[:, cols]
        for n in range(n_chunks):
            qn = q_ref[rows(n), cols]
            qf = qn.astype(F32)
            scores = lax.dot_general(qn, k_ref[rows(n), cols], (((1,), (1,)), ((), ())),
                                     preferred_element_type=F32)
            o = _dot((scores * decay).astype(BF16), v_ref[rows(n), cols])
            o = o + _dot((qf * qw_f).astype(BF16), prev_f[n])
            o = o + _dot((qf * qw_b).astype(BF16), prev_b[n])
            mu = jnp.mean(o, axis=-1, keepdims=True)
            d = o - mu
            var = jnp.mean(d * d, axis=-1, keepdims=True)
            on = d * lax.rsqrt(var + EPS) * gn
            r_ref[rows(n), cols] = (on * sg_ref[rows(n), cols].astype(F32)).astype(BF16)


def _retention(q, k, v, sg, dec, gn_g, s0f, s0b, batch, seq_len):
    hd = RET_HEAD_DIM
    tok_spec = pl.BlockSpec((seq_len, RET_WIDTH), lambda b: (b, 0))
    st_spec = pl.BlockSpec((None, N_RET_HEADS, hd, hd), lambda b: (b, 0, 0, 0))
    st_shape = jax.ShapeDtypeStruct((batch, N_RET_HEADS, hd, hd), F32)
    return pl.pallas_call(
        _retention_kernel,
        grid=(batch,),
        in_specs=[tok_spec, tok_spec, tok_spec, tok_spec,
                  pl.BlockSpec(dec.shape, lambda b: (0, 0, 0)),
                  pl.BlockSpec(gn_g.shape, lambda b: (0, 0)),
                  st_spec, st_spec],
        out_specs=[tok_spec, st_spec, st_spec],
        out_shape=[jax.ShapeDtypeStruct((batch * seq_len, RET_WIDTH), BF16), st_shape, st_shape],
        scratch_shapes=[pltpu.VMEM((N_RET_HEADS, 5, CHUNK, CHUNK), F32),
                        pltpu.VMEM((N_RET_HEADS, 2, hd), F32)],
        compiler_params=pltpu.CompilerParams(dimension_semantics=("arbitrary",),
                                             vmem_limit_bytes=VMEM_LIMIT),
        name="retention",
    )(q, k, v, sg, dec, gn_g, s0f, s0b)


def _fnet_merge_kernel(uf_ref, cs_ref, cls_ref, r_ref, gf_ref, gr_ref, x_ref, mod_ref, wf_ref, wr_ref, wo_ref,
                       o_ref, xcs_ref):
    seq_len = uf_ref.shape[0]
    gd = FOURIER_GROUP_DIM

    @pl.when(pl.program_id(1) == 0)
    def _():
        for g in range(N_FOURIER_GROUPS):
            x = _dot(uf_ref[:, g * gd:(g + 1) * gd], cs_ref[...])
            xcs_ref[0:seq_len, g * gd:(g + 1) * gd] = x[:, :gd].astype(BF16)
            xcs_ref[seq_len:2 * seq_len, g * gd:(g + 1) * gd] = x[:, gd:].astype(BF16)

    f_mix = _dot(cls_ref[...], xcs_ref[...]).astype(BF16)
    f_out = _dot(f_mix, wf_ref[...].astype(BF16))
    r_out = _dot(r_ref[...], wr_ref[...].astype(BF16))
    merged = gf_ref[...].astype(F32) * f_out + gr_ref[...].astype(F32) * r_out
    mix = _dot(merged.astype(BF16), wo_ref[...].astype(BF16))
    o_ref[...] = x_ref[...] + mod_ref[0, 2:3, :] * mix


def _fnet_merge(uf, cs, cls, r, gf, gr, x2d, mod3, w_four, w_ret, w_o, batch, seq_len, mod_row_of_batch):
    rb = min(FNET_ROWS, seq_len)
    nr = seq_len // rb

    def tok(w):
        return pl.BlockSpec((rb, w), lambda b, i: (b * nr + i, 0))

    def full(a):
        return pl.BlockSpec(a.shape, lambda b, i: (0, 0))

    def once(a):
        return pl.BlockSpec(a.shape, lambda b, i: (0, 0), pipeline_mode=pl.Buffered(1))

    return pl.pallas_call(
        _fnet_merge_kernel,
        grid=(batch, nr),
        in_specs=[pl.BlockSpec((seq_len, D_MODEL), lambda b, i: (b, 0)),
                  full(cs),
                  pl.BlockSpec((rb, 2 * seq_len), lambda b, i: (i, 0)),
                  tok(RET_WIDTH), tok(D_MODEL), tok(D_MODEL), tok(D_MODEL),
                  pl.BlockSpec((1, 6, D_MODEL), lambda b, i: (mod_row_of_batch(b), 0, 0)),
                  once(w_four), once(w_ret), once(w_o)],
        out_specs=tok(D_MODEL),
        out_shape=jax.ShapeDtypeStruct((batch * seq_len, D_MODEL), F32),
        scratch_shapes=[pltpu.VMEM((2 * seq_len, D_MODEL), BF16)],
        compiler_params=pltpu.CompilerParams(dimension_semantics=("parallel", "arbitrary"),
                                             vmem_limit_bytes=VMEM_LIMIT),
        name="fnet_merge",
    )(uf, cs, cls, r, gf, gr, x2d, mod3, w_four, w_ret, w_o)


def _pack_pair(lo_f32, hi_f32):
    lo = lax.bitcast_convert_type(lo_f32.astype(BF16).astype(F32), jnp.uint32)
    hi = lax.bitcast_convert_type(hi_f32.astype(BF16).astype(F32), jnp.uint32)
    return lax.bitcast_convert_type((lo >> 16) | hi, jnp.int32)


def _unpack_pair(words_i32):
    w = lax.bitcast_convert_type(words_i32, jnp.uint32)
    lo = lax.bitcast_convert_type(w << 16, F32)
    hi = lax.bitcast_convert_type(w & jnp.uint32(0xFFFF0000), F32)
    return lo, hi


def _load_token_words(ref, lead, n_tok):
    parts = []
    for s in range(ROW_SLABS):
        idx = (pl.ds(s, n_tok, stride=ROW_SLABS), slice(None))
        parts.append(ref[lead + idx] if lead else ref[idx])
    return jnp.concatenate(parts, axis=1)


def _store_token_words(ref, words, n_tok):
    for s in range(ROW_SLABS):
        ref[pl.ds(s, n_tok, stride=ROW_SLABS), :] = words[:, s * 128:(s + 1) * 128]


def _route(scores, biased):
    tokens = scores.shape[1]
    neg = -jnp.inf
    epg = EXPERTS_PER_GROUP
    iota_g = lax.broadcasted_iota(jnp.int32, (epg, tokens), 0).astype(F32)

    def pick_first_max(cur, iota, size):
        m = jnp.max(cur, axis=0, keepdims=True)
        idx = jnp.min(jnp.where(cur == m, iota, float(size)), axis=0, keepdims=True)
        return m, idx, iota == idx

    group_scores = []
    for g in range(N_EXPERT_GROUPS):
        vals = biased[g * epg:(g + 1) * epg, :]
        m1, _, hit = pick_first_max(vals, iota_g, epg)
        m2 = jnp.max(jnp.where(hit, neg, vals), axis=0, keepdims=True)
        group_scores.append(m1 + m2)
    cur = jnp.concatenate(group_scores, axis=0)
    group_sel = jnp.zeros_like(cur)
    for _ in range(TOPK_GROUPS):
        _, _, hit = pick_first_max(cur, iota_g, N_EXPERT_GROUPS)
        group_sel = jnp.where(hit, 1.0, group_sel)
        cur = jnp.where(hit, neg, cur)
    masked = jnp.concatenate(
        [jnp.where(group_sel[g:g + 1, :] > 0.0, biased[g * epg:(g + 1) * epg, :], neg)
         for g in range(N_EXPERT_GROUPS)], axis=0)
    iota_e = lax.broadcasted_iota(jnp.int32, masked.shape, 0).astype(F32)
    sel = jnp.zeros_like(masked)
    cur = masked
    picks = []
    for _ in range(TOP_K):
        _, idx, hit = pick_first_max(cur, iota_e, N_EXPERTS)
        picks.append(idx)
        sel = jnp.where(hit, 1.0, sel)
        cur = jnp.where(hit, neg, cur)
    w = scores * sel
    return w / jnp.sum(w, axis=0, keepdims=True) * ROUTED_SCALE, sel, picks


def _router_kernel(x_ref, mod_ref, g2_ref, wrt_ref, rb_ref, hp_ref, ek_ref, rk_ref, wt_ref, cnt_ref,
                   run_scr, earlier_scr):
    tm = x_ref.shape[0]

    @pl.when(pl.program_id(0) == 0)
    def _():
        run_scr[...] = jnp.zeros_like(run_scr)
        earlier = (lax.broadcasted_iota(jnp.int32, (tm, tm), 0) < lax.broadcasted_iota(jnp.int32, (tm, tm), 1))
        earlier_scr[...] = jnp.where(earlier, 1.0, 0.0).astype(BF16)

    h = _rms_mod(x_ref[...], g2_ref[...], mod_ref[0, 3:4, :], mod_ref[0, 4:5, :])
    half = D_MODEL // 2
    _store_token_words(hp_ref, _pack_pair(h[:, :half], h[:, half:]), tm)

    def split(a):
        hi = a.astype(BF16)
        return hi, (a - hi.astype(F32)).astype(BF16)

    def dot_nt(a, b):
        return lax.dot_general(a, b, (((1,), (1,)), ((), ())), preferred_element_type=F32)

    h_hi, h_lo = split(h)
    w_hi, w_lo = split(wrt_ref[...])
    logits_t = dot_nt(w_hi, h_hi) + (dot_nt(w_hi, h_lo) + dot_nt(w_lo, h_hi))
    scores = jax.nn.sigmoid(logits_t)
    comb_t, sel, picks = _route(scores, scores + rb_ref[...])

    rank_t = _dot(sel.astype(BF16), earlier_scr[...]) + run_scr[...]
    run_scr[...] += jnp.sum(sel, axis=1, keepdims=True)
    cnt_ref[...] = jnp.broadcast_to(run_scr[...], cnt_ref.shape)

    iota_e = lax.broadcasted_iota(jnp.int32, sel.shape, 0).astype(F32)
    ranks, weights = [], []
    for idx in picks:
        hit = iota_e == idx
        ranks.append(jnp.sum(jnp.where(hit, rank_t, 0.0), axis=0, keepdims=True))
        weights.append(jnp.sum(jnp.where(hit, comb_t, 0.0), axis=0, keepdims=True))
    ek_ref[...] = jnp.concatenate(picks, axis=0).astype(jnp.int32)
    rk_ref[...] = jnp.concatenate(ranks, axis=0).astype(jnp.int32)
    w_rep = jnp.concatenate([jnp.broadcast_to(w, (SC_LANES, tm)) for w in weights], axis=0)
    wt_ref[...] = w_rep.T


def _router(x1, mod3, norm2_g, w_router_t, router_bias, seq_len, mod_row_of_batch):
    t = x1.shape[0]
    tm = TM_ROUTER

    def mod_idx(i):
        return (mod_row_of_batch((i * tm) // seq_len), 0, 0)

    def full(a):
        return pl.BlockSpec(a.shape, lambda i: (0,) * a.ndim)

    return pl.pallas_call(
        _router_kernel,
        grid=(t // tm,),
        in_specs=[pl.BlockSpec((tm, D_MODEL), lambda i: (i, 0)),
                  pl.BlockSpec((1, 6, D_MODEL), mod_idx),
                  full(norm2_g), full(w_router_t), full(router_bias)],
        out_specs=[pl.BlockSpec((tm * ROW_SLABS, 128), lambda i: (i, 0)),
                   pl.BlockSpec((TOP_K, tm), lambda i: (0, i)),
                   pl.BlockSpec((TOP_K, tm), lambda i: (0, i)),
                   pl.BlockSpec((tm, 128), lambda i: (i, 0)),
                   pl.BlockSpec((N_EXPERTS, 128), lambda i: (0, 0))],
        out_shape=[jax.ShapeDtypeStruct((t * ROW_SLABS, 128), jnp.int32),
                   jax.ShapeDtypeStruct((TOP_K, t), jnp.int32),
                   jax.ShapeDtypeStruct((TOP_K, t), jnp.int32),
                   jax.ShapeDtypeStruct((t, 128), F32),
                   jax.ShapeDtypeStruct((N_EXPERTS, 128), F32)],
        scratch_shapes=[pltpu.VMEM((N_EXPERTS, 1), F32), pltpu.VMEM((tm, tm), BF16)],
        compiler_params=pltpu.CompilerParams(dimension_semantics=("arbitrary",),
                                             vmem_limit_bytes=VMEM_LIMIT),
        name="router",
    )(x1, mod3, norm2_g, w_router_t, router_bias)


def _plan_kernel(ek_ref, rk_ref, cnt_ref, pos_ref, texp_ref, nused_ref, tend_ref, *, expert_rows):
    rows = float(expert_rows)
    cnt = cnt_ref[:, 0:1]
    tiles = jnp.floor((cnt + (rows - 1.0)) / rows)
    before = (lax.broadcasted_iota(jnp.int32, (N_EXPERTS, N_EXPERTS), 1)
              < lax.broadcasted_iota(jnp.int32, (N_EXPERTS, N_EXPERTS), 0))
    tile_start = jnp.dot(jnp.where(before, 1.0, 0.0), jnp.broadcast_to(tiles, (N_EXPERTS, 128)),
                         precision=lax.Precision.HIGHEST, preferred_element_type=F32)[:, 0:1]
    tile_end = tile_start + tiles
    row_start = tile_start * rows

    ek = ek_ref[...]
    pos = rk_ref[...].astype(F32)
    tile_id = lax.broadcasted_iota(jnp.int32, texp_ref.shape, 1).astype(F32)
    texp = jnp.zeros(texp_ref.shape, F32)
    for e in range(N_EXPERTS):
        pos = pos + jnp.where(ek == e, row_start[e:e + 1, :], 0.0)
        texp = texp + jnp.where(tile_id >= tile_end[e:e + 1, :], 1.0, 0.0)
    pos_ref[...] = pos.astype(jnp.int32)
    texp_ref[...] = jnp.minimum(texp, N_EXPERTS - 1.0).astype(jnp.int32)
    nused_ref[...] = jnp.broadcast_to(tile_end[N_EXPERTS - 1:N_EXPERTS, :], nused_ref.shape).astype(jnp.int32)
    tend_ref[...] = jnp.broadcast_to(tile_end, tend_ref.shape).astype(jnp.int32)


def _plan(ek, rk, cnt, n_tiles_pad, expert_rows):
    t = ek.shape[1]

    def full(shape):
        return pl.BlockSpec(shape, lambda: (0,) * len(shape))

    return pl.pallas_call(
        functools.partial(_plan_kernel, expert_rows=expert_rows),
        in_specs=[full(ek.shape), full(rk.shape), full(cnt.shape)],
        out_specs=[full((TOP_K, t)), full((1, n_tiles_pad)), full((1, 128)), full((N_EXPERTS, 128))],
        out_shape=[jax.ShapeDtypeStruct((TOP_K, t), jnp.int32),
                   jax.ShapeDtypeStruct((1, n_tiles_pad), jnp.int32),
                   jax.ShapeDtypeStruct((1, 128), jnp.int32),
                   jax.ShapeDtypeStruct((N_EXPERTS, 128), jnp.int32)],
        compiler_params=pltpu.CompilerParams(vmem_limit_bytes=VMEM_LIMIT),
        name="plan",
    )(ek, rk, cnt)


def _sc_mesh():
    return plsc.VectorSubcoreMesh(core_axis_name="c", subcore_axis_name="s")


def _sc_pack_weight_halves(w):
    e, k, n = w.shape
    k_half = k // 2
    rb = SC_PACK_BLOCK_WORDS // n
    units_per_expert = k_half // rb
    per_w = (e * units_per_expert) // SC_WORKERS
    lanes = SC_LANES

    @functools.partial(
        pl.kernel, out_type=jax.ShapeDtypeStruct((e * k_half, n), jnp.int32), mesh=_sc_mesh(),
        scratch_types=[pltpu.VMEM((rb, n), F32), pltpu.VMEM((rb, n), F32), pltpu.VMEM((rb, n), jnp.int32)],
        compiler_params=pltpu.CompilerParams(needs_layout_passes=False))
    def kern(w_hbm, out_hbm, a_v, b_v, o_v):
        wid = lax.axis_index("s") * SC_CORES + lax.axis_index("c")

        @pl.loop(0, per_w)
        def _(j):
            unit = wid * per_w + j
            expert = unit // units_per_expert
            blk = unit % units_per_expert
            row_a = expert * k + blk * rb
            pltpu.sync_copy(w_hbm.at[pl.ds(row_a, rb)], a_v)
            pltpu.sync_copy(w_hbm.at[pl.ds(row_a + k_half, rb)], b_v)

            @pl.loop(0, rb)
            def _(r):
                @plsc.parallel_loop(0, n, step=lanes, unroll=4)
                def _(c):
                    both = plsc.pack(a_v[r, pl.ds(c, lanes)], b_v[r, pl.ds(c, lanes)],
                                     format=plsc.PackFormat.INTERLEAVED)
                    o_v[r, pl.ds(c, lanes)] = plsc.bitcast(both, jnp.int32)

            pltpu.sync_copy(o_v, out_hbm.at[pl.ds(expert * k_half + blk * rb, rb)])

    return kern(w.reshape(e * k, n)).reshape(e, k_half, n)


def _sc_dispatch(rows, pos3, n_out, after=()):
    t = rows.shape[0]
    ch = SC_CHUNK
    per_w = (t // ch) // SC_WORKERS

    @functools.partial(
        pl.kernel, out_type=jax.ShapeDtypeStruct((n_out,) + rows.shape[1:], jnp.int32), mesh=_sc_mesh(),
        scratch_types=[pltpu.VMEM((TOP_K, ch), jnp.int32), pltpu.VMEM((ch,) + rows.shape[1:], jnp.int32),
                       pltpu.SemaphoreType.DMA])
    def k(rows_hbm, pos_hbm, *rest):
        out_hbm, idx_v, rows_v, sem = rest[len(after):]
        wid = lax.axis_index("s") * SC_CORES + lax.axis_index("c")

        @pl.loop(0, per_w)
        def _(j):
            c = wid * per_w + j
            pltpu.sync_copy(pos_hbm.at[c], idx_v)
            pltpu.sync_copy(rows_hbm.at[pl.ds(c * ch, ch)], rows_v)
            copies = [pltpu.async_copy(rows_v, out_hbm.at[idx_v.at[kk]], sem) for kk in range(TOP_K)]
            for cp in copies:
                cp.wait()

    return k(rows, pos3, *after)


def _sc_combine(table, pos3, wtok, t):
    ch = SC_CHUNK
    sub = SC_COMBINE_TOKENS
    lanes = SC_LANES
    slabs = ROW_SLABS
    per_w = (t // ch) // SC_WORKERS
    subs_per_chunk = ch // sub
    n_steps = per_w * subs_per_chunk

    @functools.partial(
        pl.kernel, out_type=jax.ShapeDtypeStruct((t, slabs, 128), jnp.int32), mesh=_sc_mesh(),
        scratch_types=[pltpu.VMEM((per_w, TOP_K, ch), jnp.int32),
                       pltpu.VMEM((2, TOP_K, sub, slabs, 128), jnp.int32),
                       pltpu.VMEM((2, sub, 128), F32),
                       pltpu.VMEM((sub, slabs, 128), jnp.int32),
                       pltpu.SemaphoreType.DMA((2,))],
        compiler_params=pltpu.CompilerParams(needs_layout_passes=False))
    def k(tab_hbm, pos_hbm, w_hbm, out_hbm, idx_v, rows_v, w_v, out_v, sem):
        wid = lax.axis_index("s") * SC_CORES + lax.axis_index("c")
        for j in range(per_w):
            pltpu.sync_copy(pos_hbm.at[wid * per_w + j], idx_v.at[j])

        def first_token(step):
            return (wid * per_w + step // subs_per_chunk) * ch + (step % subs_per_chunk) * sub

        def copies(step, slot):
            j = step // subs_per_chunk
            s = step % subs_per_chunk
            idx = [idx_v.at[j, kk, pl.ds(s * sub, sub)] for kk in range(TOP_K)]
            return ([pltpu.make_async_copy(tab_hbm.at[idx[kk]], rows_v.at[slot, kk], sem.at[slot])
                     for kk in range(TOP_K)]
                    + [pltpu.make_async_copy(w_hbm.at[pl.ds(first_token(step), sub)], w_v.at[slot], sem.at[slot])])

        for cp in copies(0, 0):
            cp.start()

        @pl.loop(0, n_steps)
        def _(step):
            slot = step % 2

            @pl.when(step + 1 < n_steps)
            def _():
                for cp in copies(step + 1, 1 - slot):
                    cp.start()

            for cp in copies(step, slot):
                cp.wait()

            @pl.loop(0, sub)
            def _(tt):
                wk = [w_v[slot, tt, pl.ds(kk * lanes, lanes)] for kk in range(TOP_K)]
                for sl in range(slabs):
                    @plsc.parallel_loop(0, 128, step=lanes, unroll=4)
                    def _(off):
                        acc_lo = jnp.zeros((lanes,), F32)
                        acc_hi = jnp.zeros((lanes,), F32)
                        for kk in range(TOP_K):
                            word = rows_v[slot, kk, tt, sl, pl.ds(off, lanes)]
                            lo = plsc.bitcast(word << 16, F32)
                            hi = plsc.bitcast(word & jnp.int32(-65536), F32)
                            acc_lo = acc_lo + wk[kk] * lo
                            acc_hi = acc_hi + wk[kk] * hi
                        both = plsc.pack(acc_lo, acc_hi, format=plsc.PackFormat.INTERLEAVED)
                        out_v[tt, sl, pl.ds(off, lanes)] = plsc.bitcast(both, jnp.int32)

            pltpu.sync_copy(out_v, out_hbm.at[pl.ds(first_token(step), sub)])

    return k(table, pos3, wtok)


def _experts_kernel(texp_ref, nused_ref, tend_ref, xs_ref, weg_hbm, weu_hbm, wed_hbm, ys_ref,
                    wg_scr, wu_scr, wd_scr, wg_buf, wu_buf, wd_buf, sem, group_scr, *, expert_rows):
    step = pl.program_id(0)
    rows = expert_rows
    tiles_per_step = EXPERT_STEP_ROWS // expert_rows
    half = D_MODEL // 2
    n_used = nused_ref[0]

    def weight_copies(e, slot):
        return [pltpu.make_async_copy(weg_hbm.at[e], wg_buf.at[slot], sem.at[slot, 0]),
                pltpu.make_async_copy(weu_hbm.at[e], wu_buf.at[slot], sem.at[slot, 1]),
                pltpu.make_async_copy(wed_hbm.at[e], wd_buf.at[slot], sem.at[slot, 2])]

    def next_group(e):
        tile = tend_ref[e]
        return texp_ref[jnp.minimum(tile, n_used - 1)], tile < n_used

    def start_weights(e, slot, exists):
        @pl.when(exists)
        def _():
            for cp in weight_copies(e, slot):
                cp.start()

    @pl.when(step == 0)
    def _():
        group_scr[0] = 0
        e, exists = texp_ref[0], True
        for slot in range(WEIGHT_SLOTS - 1):
            start_weights(e, slot, exists)
            nxt, has_next = next_group(e)
            e, exists = nxt, exists & has_next

    def row_tile(tile, x_view, y_view):
        expert = texp_ref[tile]
        used = tile < n_used
        new_expert = (tile == 0) | (expert != texp_ref[jnp.maximum(tile - 1, 0)])

        @pl.when(used & new_expert)
        def _():
            group = group_scr[0]
            slot = group % WEIGHT_SLOTS
            ahead, exists = expert, True
            for _ in range(WEIGHT_SLOTS - 1):
                nxt, has_next = next_group(ahead)
                ahead, exists = nxt, exists & has_next
            start_weights(ahead, (group + WEIGHT_SLOTS - 1) % WEIGHT_SLOTS, exists)

            for cp in weight_copies(expert, slot):
                cp.wait()
            for scr, buf in ((wg_scr, wg_buf), (wu_scr, wu_buf), (wd_scr, wd_buf)):
                top, bottom = _unpack_pair(buf[slot])
                k_half = top.shape[0]
                scr[0:k_half, :] = top.astype(BF16)
                scr[k_half:2 * k_half, :] = bottom.astype(BF16)
            group_scr[0] = group + 1

        @pl.when(used)
        def _():
            lo, hi = _unpack_pair(_load_token_words(x_view, (), rows))
            lo = lo.astype(BF16)
            hi = hi.astype(BF16)
            g = _dot(lo, wg_scr[0:half, :]) + _dot(hi, wg_scr[half:D_MODEL, :])
            u = _dot(lo, wu_scr[0:half, :]) + _dot(hi, wu_scr[half:D_MODEL, :])
            y = _dot((_silu(g) * u).astype(BF16), wd_scr[...])
            _store_token_words(y_view, _pack_pair(y[:, :half], y[:, half:]), rows)

        @pl.when(jnp.logical_not(used) & (step == (n_used - 1) // tiles_per_step))
        def _():
            y_view[...] = jnp.zeros_like(y_view)

    for s in range(tiles_per_step):
        view = pl.ds(s * rows * ROW_SLABS, rows * ROW_SLABS)
        row_tile(step * tiles_per_step + s, xs_ref.at[view], ys_ref.at[view])


def _experts(texp, nused, tend, xs2d, weg, weu, wed, n_tiles, expert_rows):
    tiles_per_step = EXPERT_STEP_ROWS // expert_rows
    block = (EXPERT_STEP_ROWS * ROW_SLABS, 128)
    hbm = pl.BlockSpec(memory_space=pl.ANY)

    def block_idx(j, te, nu, tn):
        return (jnp.minimum(j, (nu[0] - 1) // tiles_per_step), 0)

    grid_spec = pltpu.PrefetchScalarGridSpec(
        num_scalar_prefetch=3,
        grid=(n_tiles // tiles_per_step,),
        in_specs=[pl.BlockSpec(block, block_idx), hbm, hbm, hbm],
        out_specs=pl.BlockSpec(block, block_idx),
        scratch_shapes=[pltpu.VMEM((D_MODEL, EXPERT_DIM), BF16),
                        pltpu.VMEM((D_MODEL, EXPERT_DIM), BF16),
                        pltpu.VMEM((EXPERT_DIM, D_MODEL), BF16),
                        pltpu.VMEM((WEIGHT_SLOTS,) + weg.shape[1:], jnp.int32),
                        pltpu.VMEM((WEIGHT_SLOTS,) + weu.shape[1:], jnp.int32),
                        pltpu.VMEM((WEIGHT_SLOTS,) + wed.shape[1:], jnp.int32),
                        pltpu.SemaphoreType.DMA((WEIGHT_SLOTS, 3)),
                        pltpu.SMEM((1,), jnp.int32)],
    )
    return pl.pallas_call(
        functools.partial(_experts_kernel, expert_rows=expert_rows),
        grid_spec=grid_spec,
        out_shape=jax.ShapeDtypeStruct(xs2d.shape, jnp.int32),
        compiler_params=pltpu.CompilerParams(dimension_semantics=("arbitrary",),
                                             vmem_limit_bytes=VMEM_LIMIT),
        name="experts",
    )(texp, nused, tend, xs2d, weg, weu, wed)


def _final_kernel(x_ref, routed_ref, mod_ref, g2_ref, wsg_ref, wsu_ref, wsd_ref, fng_ref, o_ref):
    tm = x_ref.shape[0]
    x = x_ref[...]
    hb = _rms_mod(x, g2_ref[...], mod_ref[0, 3:4, :], mod_ref[0, 4:5, :]).astype(BF16)
    shared = _dot((_silu(_dot(hb, wsg_ref[...])) * _dot(hb, wsu_ref[...])).astype(BF16), wsd_ref[...])
    routed = jnp.concatenate(_unpack_pair(_load_token_words(routed_ref, (), tm)), axis=1)
    y = x + mod_ref[0, 5:6, :] * (routed + shared)
    ms = jnp.mean(y * y, axis=-1, keepdims=True)
    o_ref[...] = y * lax.rsqrt(ms + EPS) * fng_ref[...]


def _final(x1, routed2d, mod3, norm2_g, wsg, wsu, wsd, final_g, seq_len, mod_row_of_batch):
    t = x1.shape[0]
    tm = TM_FINAL

    def mod_idx(i):
        return (mod_row_of_batch((i * tm) // seq_len), 0, 0)

    def full(a):
        return pl.BlockSpec(a.shape, lambda i: (0,) * a.ndim)

    return pl.pallas_call(
        _final_kernel,
        grid=(t // tm,),
        in_specs=[pl.BlockSpec((tm, D_MODEL), lambda i: (i, 0)),
                  pl.BlockSpec((tm * ROW_SLABS, 128), lambda i: (i, 0)),
                  pl.BlockSpec((1, 6, D_MODEL), mod_idx),
                  full(norm2_g), full(wsg), full(wsu), full(wsd), full(final_g)],
        out_specs=pl.BlockSpec((tm, D_MODEL), lambda i: (i, 0)),
        out_shape=jax.ShapeDtypeStruct((t, D_MODEL), F32),
        compiler_params=pltpu.CompilerParams(dimension_semantics=("parallel",),
                                             vmem_limit_bytes=VMEM_LIMIT),
        name="final",
    )(x1, routed2d, mod3, norm2_g, wsg, wsu, wsd, final_g)


def _moe(x1, mod3, lw, seq_len, mod_row_of_batch):
    t = x1.shape[0]
    expert_rows = min(MAX_EXPERT_ROWS, TOP_K * t // N_EXPERTS // 2)
    n_tiles = TOP_K * t // expert_rows + N_EXPERTS
    n_tiles_pad = -(-n_tiles // 128) * 128
    hp2d, ek, rk, wtok, cnt = _router(x1, mod3, lw["norm2_g"], lw["w_router_t"], lw["router_bias"],
                                      seq_len, mod_row_of_batch)
    pos, texp, nused, tend = _plan(ek, rk, cnt, n_tiles_pad, expert_rows)
    pos3 = pos.reshape(TOP_K, t // SC_CHUNK, SC_CHUNK).transpose(1, 0, 2)
    xs = _sc_dispatch(hp2d.reshape(t, ROW_SLABS, 128), pos3, n_tiles * expert_rows,
                      after=(lw["weg"], lw["weu"], lw["wed"]))
    ys2d = _experts(texp.reshape(-1), nused.reshape(-1), tend[:, 0], xs.reshape(-1, 128),
                    lw["weg"], lw["weu"], lw["wed"], n_tiles, expert_rows)
    routed = _sc_combine(ys2d.reshape(-1, ROW_SLABS, 128), pos3, wtok, t)
    return _final(x1, routed.reshape(t * ROW_SLABS, 128), mod3, lw["norm2_g"],
                  lw["wsg"], lw["wsu"], lw["wsd"], lw["final_g"], seq_len, mod_row_of_batch)


def _dft_tables(seq_len):
    gd = FOURIER_GROUP_DIM
    kc = np.arange(gd)
    ang_c = ((kc[:, None] * kc[None, :]) % gd) * (2.0 * math.pi / gd)
    cs = np.concatenate([np.cos(ang_c), np.sin(ang_c)], axis=1) * (gd ** -0.5)
    kl = np.arange(seq_len)
    ang_l = ((kl[:, None] * kl[None, :]) % seq_len) * (2.0 * math.pi / seq_len)
    cls = np.concatenate([np.cos(ang_l), -np.sin(ang_l)], axis=1) * (seq_len ** -0.5)
    return jnp.asarray(cs.astype(np.float32), dtype=BF16), jnp.asarray(cls.astype(np.float32), dtype=BF16)


def _rope_tables(length):
    rows = length // GRID_W
    r = np.repeat(np.arange(rows, dtype=np.float32), GRID_W)
    col = np.tile(np.arange(GRID_W, dtype=np.float32), rows)
    nf = RET_HEAD_DIM // 4
    inv = (np.float32(ROPE_BASE) ** (-np.arange(nf, dtype=np.float32) / np.float32(nf))).astype(np.float32)
    ar = r[:, None] * inv[None]
    ac = col[:, None] * inv[None]
    ang = np.concatenate([ar, ar, ac, ac], axis=-1).astype(np.float64)
    sign = np.where((np.arange(RET_HEAD_DIM) & nf) == 0, -1.0, 1.0)
    return (jnp.asarray(np.cos(ang).astype(np.float32)),
            jnp.asarray((np.sin(ang) * sign[None, :]).astype(np.float32)))


def _trunk_path(x, mod3, mod_row_of_batch, s0f, s0b, rope, lw):
    batch, seq_len, _ = x.shape
    x2d = x.reshape(batch * seq_len, D_MODEL)
    uf, q, k, v, sg, gf, gr = _inproj(x2d, mod3, lw["norm1_g"], lw["w_in"], seq_len, mod_row_of_batch, rope)
    r, s_f, s_b = _retention(q, k, v, sg, lw["dec"], lw["gn_g"], s0f, s0b, batch, seq_len)
    cs, cls = _dft_tables(seq_len)
    x1 = _fnet_merge(uf, cs, cls, r, gf, gr, x2d, mod3, lw["w_four"], lw["w_ret"], lw["w_o"],
                     batch, seq_len, mod_row_of_batch)
    y = _moe(x1, mod3, lw, seq_len, mod_row_of_batch)
    return y.reshape(batch, seq_len, D_MODEL), s_f, s_b


def kernel(x_prompt, x_sample, state_ret_fwd, state_ret_bwd, c, c_ctx, w_ada, b_ada, norm1_g, norm2_g, w_in,
           ret_decay_fwd, ret_decay_bwd, ret_gn_g, w_four_out, w_ret_out, w_out, w_router, router_bias,
           w_exp_gate, w_exp_up, w_exp_down, w_shared_gate, w_shared_up, w_shared_down, final_norm_g):
    depth = w_ada.shape[0]
    assert depth == 1, "final norm is fused into the last layer's MoE kernel"
    n_ctx, n_lat = x_prompt.shape[0], x_sample.shape[0]
    cond = jnp.concatenate([c_ctx[None, :], c], axis=0)
    cond = jnp.pad(cond, ((0, (-cond.shape[0]) % 8), (0, 0)))
    rope = _rope_tables(x_sample.shape[1])
    zeros = jnp.zeros((n_ctx, N_RET_HEADS, RET_HEAD_DIM, RET_HEAD_DIM), F32)

    layer = 0
    mod = _ada(cond, w_ada[layer], b_ada[layer][None, :])
    mod3 = mod.reshape(mod.shape[0], 6, D_MODEL)
    dec = jnp.stack([ret_decay_fwd[layer], ret_decay_bwd[layer]], axis=1)
    lw = {
        "norm1_g": norm1_g[layer][None, :],
        "norm2_g": norm2_g[layer][None, :],
        "w_in": w_in[layer],
        "dec": jnp.broadcast_to(dec[:, :, None], (N_RET_HEADS, 2, RET_HEAD_DIM)).astype(F32),
        "gn_g": ret_gn_g[layer][None, :],
        "w_four": w_four_out[layer],
        "w_ret": w_ret_out[layer],
        "w_o": w_out[layer],
        "w_router_t": w_router[layer].T,
        "router_bias": router_bias[layer][:, None],
        "weg": _sc_pack_weight_halves(w_exp_gate[layer]),
        "weu": _sc_pack_weight_halves(w_exp_up[layer]),
        "wed": _sc_pack_weight_halves(w_exp_down[layer]),
        "wsg": w_shared_gate[layer].astype(BF16),
        "wsu": w_shared_up[layer].astype(BF16),
        "wsd": w_shared_down[layer].astype(BF16),
        "final_g": final_norm_g[None, :],
    }
    y_prompt, s_f, s_b = _trunk_path(x_prompt, mod3, lambda b: 0, zeros, zeros, None, lw)
    y_sample, _, _ = _trunk_path(x_sample, mod3, lambda b: 1 + b, state_ret_fwd[:, layer],
                                 state_ret_bwd[:, layer], rope, lw)
    return (y_prompt, y_sample, s_f[:, None], s_b[:, None])
```

```python
import functools
import math

import jax
import jax.numpy as jnp
import numpy as np
from jax import lax
from jax.experimental import pallas as pl
from jax.experimental.pallas import tpu as pltpu
from jax.experimental.pallas import tpu_sc as plsc

F32 = jnp.float32
BF16 = jnp.bfloat16

D_MODEL = 1024
GRID_W = 64
N_FOURIER_GROUPS = 8
FOURIER_GROUP_DIM = 128
N_RET_HEADS = 4
RET_HEAD_DIM = 128
RET_WIDTH = N_RET_HEADS * RET_HEAD_DIM
CHUNK = 128
N_EXPERTS = 64
N_EXPERT_GROUPS = 8
EXPERTS_PER_GROUP = N_EXPERTS // N_EXPERT_GROUPS
TOPK_GROUPS = 4
TOP_K = 8
EXPERT_DIM = 256
ROUTED_SCALE = 2.5
ROPE_BASE = 10000.0
EPS = 1e-6
Q_SCALE = RET_HEAD_DIM ** -0.5

_C_UF = (0, 1024)
_C_Q = (1024, 1536)
_C_K = (1536, 2048)
_C_V = (2048, 2560)
_C_G = (2560, 3072)
_C_GF = (3072, 4096)
_C_GR = (4096, 5120)

VMEM_LIMIT = 56 * 1024 * 1024

TM_INPROJ = 1024
TM_ROUTER = 512
FNET_ROWS = 512
TM_FINAL = 1024
EXPERT_STEP_ROWS = 1024
MAX_EXPERT_ROWS = 512
WEIGHT_SLOTS = 3
ROW_SLABS = 4
SC_CORES = 2
SC_WORKERS = 32
SC_CHUNK = 128
SC_LANES = 16
SC_PACK_BLOCK_WORDS = 16384
SC_COMBINE_TOKENS = 8


def _silu(x):
    return x * jax.nn.sigmoid(x)


def _dot(a, b):
    return jnp.dot(a, b, preferred_element_type=F32)


def _rms_mod(x, g, shift, scale):
    ms = jnp.mean(x * x, axis=-1, keepdims=True)
    y = x * lax.rsqrt(ms + EPS) * g
    return y * (1.0 + scale) + shift


def _ada_kernel(cond_ref, w_ref, b_ref, o_ref):
    s = _silu(cond_ref[...]).astype(BF16)
    o_ref[...] = _dot(s, w_ref[...].astype(BF16)) + b_ref[...]


def _ada(cond, w_ada, b_ada):
    rows, n = cond.shape[0], w_ada.shape[1]
    tn = 1536
    return pl.pallas_call(
        _ada_kernel,
        grid=(n // tn,),
        in_specs=[pl.BlockSpec((rows, D_MODEL), lambda j: (0, 0)),
                  pl.BlockSpec((D_MODEL, tn), lambda j: (0, j)),
                  pl.BlockSpec((1, tn), lambda j: (0, j))],
        out_specs=pl.BlockSpec((rows, tn), lambda j: (0, j)),
        out_shape=jax.ShapeDtypeStruct((rows, n), F32),
        compiler_params=pltpu.CompilerParams(vmem_limit_bytes=VMEM_LIMIT),
        name="ada",
    )(cond, w_ada, b_ada)


def _rope_head(x, cos, sin_signed, first_half):
    partner = jnp.where(first_half, pltpu.roll(x, 96, 1), pltpu.roll(x, 32, 1))
    return x * cos + partner * sin_signed


def _inproj_kernel(*refs, use_rope):
    if use_rope:
        x_ref, mod_ref, g_ref, w_ref, cos_ref, sin_ref = refs[:6]
        outs = refs[6:]
    else:
        x_ref, mod_ref, g_ref, w_ref = refs[:4]
        outs = refs[4:]
    uf_o, q_o, k_o, v_o, sg_o, gf_o, gr_o = outs

    h = _rms_mod(x_ref[...], g_ref[...], mod_ref[0, 0:1, :], mod_ref[0, 1:2, :])
    hb = h.astype(BF16)

    def proj(cols):
        return _dot(hb, w_ref[:, cols[0]:cols[1]].astype(BF16))

    uf_o[...] = proj(_C_UF).astype(BF16)
    q = proj(_C_Q)
    k = proj(_C_K)
    if use_rope:
        cos = cos_ref[...]
        sin_signed = sin_ref[...]
        lane = lax.broadcasted_iota(jnp.int32, cos.shape, 1)
        first_half = (lane & 32) == 0
        for hd in range(N_RET_HEADS):
            sl = slice(hd * RET_HEAD_DIM, (hd + 1) * RET_HEAD_DIM)
            q_o[:, sl] = (_rope_head(q[:, sl], cos, sin_signed, first_half) * Q_SCALE).astype(BF16)
            k_o[:, sl] = _rope_head(k[:, sl], cos, sin_signed, first_half).astype(BF16)
    else:
        q_o[...] = (q * Q_SCALE).astype(BF16)
        k_o[...] = k.astype(BF16)
    v_o[...] = proj(_C_V).astype(BF16)
    sg_o[...] = _silu(proj(_C_G)).astype(BF16)
    gf_o[...] = jax.nn.sigmoid(proj(_C_GF)).astype(BF16)
    gr_o[...] = jax.nn.sigmoid(proj(_C_GR)).astype(BF16)


def _inproj(x2d, mod3, norm_g, w_in_f32, seq_len, mod_row_of_batch, rope):
    t = x2d.shape[0]
    tm = TM_INPROJ
    tiles_per_seq = max(seq_len // tm, 1)

    def mod_idx(i):
        return (mod_row_of_batch((i * tm) // seq_len), 0, 0)

    in_specs = [pl.BlockSpec((tm, D_MODEL), lambda i: (i, 0)),
                pl.BlockSpec((1, 6, D_MODEL), mod_idx),
                pl.BlockSpec((1, D_MODEL), lambda i: (0, 0)),
                pl.BlockSpec(w_in_f32.shape, lambda i: (0, 0), pipeline_mode=pl.Buffered(1))]
    args = [x2d, mod3, norm_g, w_in_f32]
    if rope is not None:
        in_specs += [pl.BlockSpec((tm, RET_HEAD_DIM), lambda i: (i % tiles_per_seq, 0))] * 2
        args += list(rope)
    widths = [1024, RET_WIDTH, RET_WIDTH, RET_WIDTH, RET_WIDTH, 1024, 1024]
    return pl.pallas_call(
        functools.partial(_inproj_kernel, use_rope=rope is not None),
        grid=(t // tm,),
        in_specs=in_specs,
        out_specs=[pl.BlockSpec((tm, w), lambda i: (i, 0)) for w in widths],
        out_shape=[jax.ShapeDtypeStruct((t, w), BF16) for w in widths],
        compiler_params=pltpu.CompilerParams(dimension_semantics=("parallel",),
                                             vmem_limit_bytes=VMEM_LIMIT),
        name="inproj",
    )(*args)


def _retention_kernel(q_ref, k_ref, v_ref, sg_ref, dec_ref, gn_ref, s0f_ref, s0b_ref,
                      r_ref, sfo_ref, sbo_ref, tab_scr, gc_scr):
    n_chunks = q_ref.shape[0] // CHUNK
    hd = RET_HEAD_DIM

    @pl.when(pl.program_id(0) == 0)
    def _():
        row = lax.broadcasted_iota(jnp.int32, (CHUNK, CHUNK), 0).astype(F32)
        col = lax.broadcasted_iota(jnp.int32, (CHUNK, CHUNK), 1).astype(F32)
        diff = row - col
        for h in range(N_RET_HEADS):
            dec = dec_ref[h]
            lg = jnp.minimum(dec, 0.0) - jnp.log1p(jnp.exp(-jnp.abs(dec)))
            lgf = lg[0:1, :]
            lgb = lg[1:2, :]
            tab_scr[h, 0] = jnp.exp(jnp.where(diff >= 0, lgf * diff, lgb * (-diff)))
            tab_scr[h, 1] = jnp.exp(lgf * (row + 1.0))
            tab_scr[h, 2] = jnp.exp(lgb * (CHUNK - row))
            tab_scr[h, 3] = jnp.exp(lgf * (CHUNK - 1.0 - col))
            tab_scr[h, 4] = jnp.exp(lgb * col)
            gc_scr[h] = jnp.exp(lg * CHUNK)

    def rows(n):
        return slice(n * CHUNK, (n + 1) * CHUNK)

    for h in range(N_RET_HEADS):
        cols = slice(h * hd, (h + 1) * hd)
        decay, qw_f, qw_b, kwt_f, kwt_b = (tab_scr[h, i] for i in range(5))
        gc = gc_scr[h]
        gc_f = gc[0:1, :]
        gc_b = gc[1:2, :]

        kv_f, kv_b = [], []
        for n in range(n_chunks):
            kt = k_ref[rows(n), cols].astype(F32).T
            vn = v_ref[rows(n), cols]
            kv_f.append(_dot((kt * kwt_f).astype(BF16), vn))
            kv_b.append(_dot((kt * kwt_b).astype(BF16), vn))

        s = s0f_ref[h]
        prev_f = []
        for n in range(n_chunks):
            prev_f.append(s.astype(BF16))
            s = gc_f * s + kv_f[n]
        sfo_ref[h] = s
        s = s0b_ref[h]
        prev_b = [None] * n_chunks
        for n in reversed(range(n_chunks)):
            prev_b[n] = s.astype(BF16)
            s = gc_b * s + kv_b[n]
        sbo_ref[h] = s

        gn = gn_ref[:, cols]
        for n in range(n_chunks):
            qn = q_ref[rows(n), cols]
            qf = qn.astype(F32)
            scores = lax.dot_general(qn, k_ref[rows(n), cols], (((1,), (1,)), ((), ())),
                                     preferred_element_type=F32)
            o = _dot((scores * decay).astype(BF16), v_ref[rows(n), cols])
            o = o + _dot((qf * qw_f).astype(BF16), prev_f[n])
            o = o + _dot((qf * qw_b).astype(BF16), prev_b[n])
            mu = jnp.mean(o, axis=-1, keepdims=True)
            d = o - mu
            var = jnp.mean(d * d, axis=-1, keepdims=True)
            on = d * lax.rsqrt(var + EPS) * gn
            r_ref[rows(n), cols] = (on * sg_ref[rows(n), cols].astype(F32)).astype(BF16)


def _retention(q, k, v, sg, dec, gn_g, s0f, s0b, batch, seq_len):
    hd = RET_HEAD_DIM
    tok_spec = pl.BlockSpec((seq_len, RET_WIDTH), lambda b: (b, 0))
    st_spec = pl.BlockSpec((None, N_RET_HEADS, hd, hd), lambda b: (b, 0, 0, 0))
    st_shape = jax.ShapeDtypeStruct((batch, N_RET_HEADS, hd, hd), F32)
    return pl.pallas_call(
        _retention_kernel,
        grid=(batch,),
        in_specs=[tok_spec, tok_spec, tok_spec, tok_spec,
                  pl.BlockSpec(dec.shape, lambda b: (0, 0, 0)),
                  pl.BlockSpec(gn_g.shape, lambda b: (0, 0)),
                  st_spec, st_spec],
        out_specs=[tok_spec, st_spec, st_spec],
        out_shape=[jax.ShapeDtypeStruct((batch * seq_len, RET_WIDTH), BF16), st_shape, st_shape],
        scratch_shapes=[pltpu.VMEM((N_RET_HEADS, 5, CHUNK, CHUNK), F32),
                        pltpu.VMEM((N_RET_HEADS, 2, hd), F32)],
        compiler_params=pltpu.CompilerParams(dimension_semantics=("arbitrary",),
                                             vmem_limit_bytes=VMEM_LIMIT),
        name="retention",
    )(q, k, v, sg, dec, gn_g, s0f, s0b)


def _fnet_merge_kernel(uf_ref, cs_ref, cls_ref, r_ref, gf_ref, gr_ref, x_ref, mod_ref, wf_ref, wr_ref, wo_ref,
                       o_ref, xcs_ref):
    seq_len = uf_ref.shape[0]
    gd = FOURIER_GROUP_DIM

    @pl.when(pl.program_id(1) == 0)
    def _():
        for g in range(N_FOURIER_GROUPS):
            x = _dot(uf_ref[:, g * gd:(g + 1) * gd], cs_ref[...])
            xcs_ref[0:seq_len, g * gd:(g + 1) * gd] = x[:, :gd].astype(BF16)
            xcs_ref[seq_len:2 * seq_len, g * gd:(g + 1) * gd] = x[:, gd:].astype(BF16)

    f_mix = _dot(cls_ref[...], xcs_ref[...]).astype(BF16)
    f_out = _dot(f_mix, wf_ref[...].astype(BF16))
    r_out = _dot(r_ref[...], wr_ref[...].astype(BF16))
    merged = gf_ref[...].astype(F32) * f_out + gr_ref[...].astype(F32) * r_out
    mix = _dot(merged.astype(BF16), wo_ref[...].astype(BF16))
    o_ref[...] = x_ref[...] + mod_ref[0, 2:3, :] * mix


def _fnet_merge(uf, cs, cls, r, gf, gr, x2d, mod3, w_four, w_ret, w_o, batch, seq_len, mod_row_of_batch):
    rb = min(FNET_ROWS, seq_len)
    nr = seq_len // rb

    def tok(w):
        return pl.BlockSpec((rb, w), lambda b, i: (b * nr + i, 0))

    def full(a):
        return pl.BlockSpec(a.shape, lambda b, i: (0, 0))

    def once(a):
        return pl.BlockSpec(a.shape, lambda b, i: (0, 0), pipeline_mode=pl.Buffered(1))

    return pl.pallas_call(
        _fnet_merge_kernel,
        grid=(batch, nr),
        in_specs=[pl.BlockSpec((seq_len, D_MODEL), lambda b, i: (b, 0)),
                  full(cs),
                  pl.BlockSpec((rb, 2 * seq_len), lambda b, i: (i, 0)),
                  tok(RET_WIDTH), tok(D_MODEL), tok(D_MODEL), tok(D_MODEL),
                  pl.BlockSpec((1, 6, D_MODEL), lambda b, i: (mod_row_of_batch(b), 0, 0)),
                  once(w_four), once(w_ret), once(w_o)],
        out_specs=tok(D_MODEL),
        out_shape=jax.ShapeDtypeStruct((batch * seq_len, D_MODEL), F32),
        scratch_shapes=[pltpu.VMEM((2 * seq_len, D_MODEL), BF16)],
        compiler_params=pltpu.CompilerParams(dimension_semantics=("parallel", "arbitrary"),
                                             vmem_limit_bytes=VMEM_LIMIT),
        name="fnet_merge",
    )(uf, cs, cls, r, gf, gr, x2d, mod3, w_four, w_ret, w_o)


def _pack_pair(lo_f32, hi_f32):
    lo = lax.bitcast_convert_type(lo_f32.astype(BF16).astype(F32), jnp.uint32)
    hi = lax.bitcast_convert_type(hi_f32.astype(BF16).astype(F32), jnp.uint32)
    return lax.bitcast_convert_type((lo >> 16) | hi, jnp.int32)


def _unpack_pair(words_i32):
    w = lax.bitcast_convert_type(words_i32, jnp.uint32)
    lo = lax.bitcast_convert_type(w << 16, F32)
    hi = lax.bitcast_convert_type(w & jnp.uint32(0xFFFF0000), F32)
    return lo, hi


def _load_token_words(ref, lead, n_tok):
    parts = []
    for s in range(ROW_SLABS):
        idx = (pl.ds(s, n_tok, stride=ROW_SLABS), slice(None))
        parts.append(ref[lead + idx] if lead else ref[idx])
    return jnp.concatenate(parts, axis=1)


def _store_token_words(ref, words, n_tok):
    for s in range(ROW_SLABS):
        ref[pl.ds(s, n_tok, stride=ROW_SLABS), :] = words[:, s * 128:(s + 1) * 128]


def _route(scores, biased):
    tokens = scores.shape[1]
    neg = -jnp.inf
    epg = EXPERTS_PER_GROUP
    iota_g = lax.broadcasted_iota(jnp.int32, (epg, tokens), 0).astype(F32)

    def pick_first_max(cur, iota, size):
        m = jnp.max(cur, axis=0, keepdims=True)
        idx = jnp.min(jnp.where(cur == m, iota, float(size)), axis=0, keepdims=True)
        return m, idx, iota == idx

    group_scores = []
    for g in range(N_EXPERT_GROUPS):
        vals = biased[g * epg:(g + 1) * epg, :]
        m1, _, hit = pick_first_max(vals, iota_g, epg)
        m2 = jnp.max(jnp.where(hit, neg, vals), axis=0, keepdims=True)
        group_scores.append(m1 + m2)
    cur = jnp.concatenate(group_scores, axis=0)
    group_sel = jnp.zeros_like(cur)
    for _ in range(TOPK_GROUPS):
        _, _, hit = pick_first_max(cur, iota_g, N_EXPERT_GROUPS)
        group_sel = jnp.where(hit, 1.0, group_sel)
        cur = jnp.where(hit, neg, cur)
    masked = jnp.concatenate(
        [jnp.where(group_sel[g:g + 1, :] > 0.0, biased[g * epg:(g + 1) * epg, :], neg)
         for g in range(N_EXPERT_GROUPS)], axis=0)
    iota_e = lax.broadcasted_iota(jnp.int32, masked.shape, 0).astype(F32)
    sel = jnp.zeros_like(masked)
    cur = masked
    picks = []
    for _ in range(TOP_K):
        _, idx, hit = pick_first_max(cur, iota_e, N_EXPERTS)
        picks.append(idx)
        sel = jnp.where(hit, 1.0, sel)
        cur = jnp.where(hit, neg, cur)
    w = scores * sel
    return w / jnp.sum(w, axis=0, keepdims=True) * ROUTED_SCALE, sel, picks


def _router_kernel(x_ref, mod_ref, g2_ref, wrt_ref, rb_ref, hp_ref, ek_ref, rk_ref, wt_ref, cnt_ref,
                   run_scr, earlier_scr):
    tm = x_ref.shape[0]

    @pl.when(pl.program_id(0) == 0)
    def _():
        run_scr[...] = jnp.zeros_like(run_scr)
        earlier = (lax.broadcasted_iota(jnp.int32, (tm, tm), 0) < lax.broadcasted_iota(jnp.int32, (tm, tm), 1))
        earlier_scr[...] = jnp.where(earlier, 1.0, 0.0).astype(BF16)

    h = _rms_mod(x_ref[...], g2_ref[...], mod_ref[0, 3:4, :], mod_ref[0, 4:5, :])
    half = D_MODEL // 2
    _store_token_words(hp_ref, _pack_pair(h[:, :half], h[:, half:]), tm)

    def split(a):
        hi = a.astype(BF16)
        return hi, (a - hi.astype(F32)).astype(BF16)

    def dot_nt(a, b):
        return lax.dot_general(a, b, (((1,), (1,)), ((), ())), preferred_element_type=F32)

    h_hi, h_lo = split(h)
    w_hi, w_lo = split(wrt_ref[...])
    logits_t = dot_nt(w_hi, h_hi) + (dot_nt(w_hi, h_lo) + dot_nt(w_lo, h_hi))
    scores = jax.nn.sigmoid(logits_t)
    comb_t, sel, picks = _route(scores, scores + rb_ref[...])

    rank_t = _dot(sel.astype(BF16), earlier_scr[...]) + run_scr[...]
    run_scr[...] += jnp.sum(sel, axis=1, keepdims=True)
    cnt_ref[...] = jnp.broadcast_to(run_scr[...], cnt_ref.shape)

    iota_e = lax.broadcasted_iota(jnp.int32, sel.shape, 0).astype(F32)
    ranks, weights = [], []
    for idx in picks:
        hit = iota_e == idx
        ranks.append(jnp.sum(jnp.where(hit, rank_t, 0.0), axis=0, keepdims=True))
        weights.append(jnp.sum(jnp.where(hit, comb_t, 0.0), axis=0, keepdims=True))
    ek_ref[...] = jnp.concatenate(picks, axis=0).astype(jnp.int32)
    rk_ref[...] = jnp.concatenate(ranks, axis=0).astype(jnp.int32)
    w_rep = jnp.concatenate([jnp.broadcast_to(w, (SC_LANES, tm)) for w in weights], axis=0)
    wt_ref[...] = w_rep.T


def _router(x1, mod3, norm2_g, w_router_t, router_bias, seq_len, mod_row_of_batch):
    t = x1.shape[0]
    tm = TM_ROUTER

    def mod_idx(i):
        return (mod_row_of_batch((i * tm) // seq_len), 0, 0)

    def full(a):
        return pl.BlockSpec(a.shape, lambda i: (0,) * a.ndim)

    return pl.pallas_call(
        _router_kernel,
        grid=(t // tm,),
        in_specs=[pl.BlockSpec((tm, D_MODEL), lambda i: (i, 0)),
                  pl.BlockSpec((1, 6, D_MODEL), mod_idx),
                  full(norm2_g), full(w_router_t), full(router_bias)],
        out_specs=[pl.BlockSpec((tm * ROW_SLABS, 128), lambda i: (i, 0)),
                   pl.BlockSpec((TOP_K, tm), lambda i: (0, i)),
                   pl.BlockSpec((TOP_K, tm), lambda i: (0, i)),
                   pl.BlockSpec((tm, 128), lambda i: (i, 0)),
                   pl.BlockSpec((N_EXPERTS, 128), lambda i: (0, 0))],
        out_shape=[jax.ShapeDtypeStruct((t * ROW_SLABS, 128), jnp.int32),
                   jax.ShapeDtypeStruct((TOP_K, t), jnp.int32),
                   jax.ShapeDtypeStruct((TOP_K, t), jnp.int32),
                   jax.ShapeDtypeStruct((t, 128), F32),
                   jax.ShapeDtypeStruct((N_EXPERTS, 128), F32)],
        scratch_shapes=[pltpu.VMEM((N_EXPERTS, 1), F32), pltpu.VMEM((tm, tm), BF16)],
        compiler_params=pltpu.CompilerParams(dimension_semantics=("arbitrary",),
                                             vmem_limit_bytes=VMEM_LIMIT),
        name="router",
    )(x1, mod3, norm2_g, w_router_t, router_bias)


def _plan_kernel(ek_ref, rk_ref, cnt_ref, pos_ref, texp_ref, nused_ref, tend_ref, *, expert_rows):
    rows = float(expert_rows)
    cnt = cnt_ref[:, 0:1]
    tiles = jnp.floor((cnt + (rows - 1.0)) / rows)
    before = (lax.broadcasted_iota(jnp.int32, (N_EXPERTS, N_EXPERTS), 1)
              < lax.broadcasted_iota(jnp.int32, (N_EXPERTS, N_EXPERTS), 0))
    tile_start = jnp.dot(jnp.where(before, 1.0, 0.0), jnp.broadcast_to(tiles, (N_EXPERTS, 128)),
                         precision=lax.Precision.HIGHEST, preferred_element_type=F32)[:, 0:1]
    tile_end = tile_start + tiles
    row_start = tile_start * rows

    ek = ek_ref[...]
    pos = rk_ref[...].astype(F32)
    tile_id = lax.broadcasted_iota(jnp.int32, texp_ref.shape, 1).astype(F32)
    texp = jnp.zeros(texp_ref.shape, F32)
    for e in range(N_EXPERTS):
        pos = pos + jnp.where(ek == e, row_start[e:e + 1, :], 0.0)
        texp = texp + jnp.where(tile_id >= tile_end[e:e + 1, :], 1.0, 0.0)
    pos_ref[...] = pos.astype(jnp.int32)
    texp_ref[...] = jnp.minimum(texp, N_EXPERTS - 1.0).astype(jnp.int32)
    nused_ref[...] = jnp.broadcast_to(tile_end[N_EXPERTS - 1:N_EXPERTS, :], nused_ref.shape).astype(jnp.int32)
    tend_ref[...] = jnp.broadcast_to(tile_end, tend_ref.shape).astype(jnp.int32)


def _plan(ek, rk, cnt, n_tiles_pad, expert_rows):
    t = ek.shape[1]

    def full(shape):
        return pl.BlockSpec(shape, lambda: (0,) * len(shape))

    return pl.pallas_call(
        functools.partial(_plan_kernel, expert_rows=expert_rows),
        in_specs=[full(ek.shape), full(rk.shape), full(cnt.shape)],
        out_specs=[full((TOP_K, t)), full((1, n_tiles_pad)), full((1, 128)), full((N_EXPERTS, 128))],
        out_shape=[jax.ShapeDtypeStruct((TOP_K, t), jnp.int32),
                   jax.ShapeDtypeStruct((1, n_tiles_pad), jnp.int32),
                   jax.ShapeDtypeStruct((1, 128), jnp.int32),
                   jax.ShapeDtypeStruct((N_EXPERTS, 128), jnp.int32)],
        compiler_params=pltpu.CompilerParams(vmem_limit_bytes=VMEM_LIMIT),
        name="plan",
    )(ek, rk, cnt)


def _sc_mesh():
    return plsc.VectorSubcoreMesh(core_axis_name="c", subcore_axis_name="s")


def _sc_pack_weight_halves(w):
    e, k, n = w.shape
    k_half = k // 2
    rb = SC_PACK_BLOCK_WORDS // n
    units_per_expert = k_half // rb
    per_w = (e * units_per_expert) // SC_WORKERS
    lanes = SC_LANES

    @functools.partial(
        pl.kernel, out_type=jax.ShapeDtypeStruct((e * k_half, n), jnp.int32), mesh=_sc_mesh(),
        scratch_types=[pltpu.VMEM((rb, n), F32), pltpu.VMEM((rb, n), F32), pltpu.VMEM((rb, n), jnp.int32)],
        compiler_params=pltpu.CompilerParams(needs_layout_passes=False))
    def kern(w_hbm, out_hbm, a_v, b_v, o_v):
        wid = lax.axis_index("s") * SC_CORES + lax.axis_index("c")

        @pl.loop(0, per_w)
        def _(j):
            unit = wid * per_w + j
            expert = unit // units_per_expert
            blk = unit % units_per_expert
            row_a = expert * k + blk * rb
            pltpu.sync_copy(w_hbm.at[pl.ds(row_a, rb)], a_v)
            pltpu.sync_copy(w_hbm.at[pl.ds(row_a + k_half, rb)], b_v)

            @pl.loop(0, rb)
            def _(r):
                @plsc.parallel_loop(0, n, step=lanes, unroll=4)
                def _(c):
                    both = plsc.pack(a_v[r, pl.ds(c, lanes)], b_v[r, pl.ds(c, lanes)],
                                     format=plsc.PackFormat.INTERLEAVED)
                    o_v[r, pl.ds(c, lanes)] = plsc.bitcast(both, jnp.int32)

            pltpu.sync_copy(o_v, out_hbm.at[pl.ds(expert * k_half + blk * rb, rb)])

    return kern(w.reshape(e * k, n)).reshape(e, k_half, n)


def _sc_dispatch(rows, pos3, n_out, after=()):
    t = rows.shape[0]
    ch = SC_CHUNK
    per_w = (t // ch) // SC_WORKERS

    @functools.partial(
        pl.kernel, out_type=jax.ShapeDtypeStruct((n_out,) + rows.shape[1:], jnp.int32), mesh=_sc_mesh(),
        scratch_types=[pltpu.VMEM((TOP_K, ch), jnp.int32), pltpu.VMEM((ch,) + rows.shape[1:], jnp.int32),
                       pltpu.SemaphoreType.DMA])
    def k(rows_hbm, pos_hbm, *rest):
        out_hbm, idx_v, rows_v, sem = rest[len(after):]
        wid = lax.axis_index("s") * SC_CORES + lax.axis_index("c")

        @pl.loop(0, per_w)
        def _(j):
            c = wid * per_w + j
            pltpu.sync_copy(pos_hbm.at[c], idx_v)
            pltpu.sync_copy(rows_hbm.at[pl.ds(c * ch, ch)], rows_v)
            copies = [pltpu.async_copy(rows_v, out_hbm.at[idx_v.at[kk]], sem) for kk in range(TOP_K)]
            for cp in copies:
                cp.wait()

    return k(rows, pos3, *after)


def _sc_combine(table, pos3, wtok, t):
    ch = SC_CHUNK
    sub = SC_COMBINE_TOKENS
    lanes = SC_LANES
    slabs = ROW_SLABS
    per_w = (t // ch) // SC_WORKERS
    subs_per_chunk = ch // sub
    n_steps = per_w * subs_per_chunk

    @functools.partial(
        pl.kernel, out_type=jax.ShapeDtypeStruct((t, slabs, 128), jnp.int32), mesh=_sc_mesh(),
        scratch_types=[pltpu.VMEM((per_w, TOP_K, ch), jnp.int32),
                       pltpu.VMEM((2, TOP_K, sub, slabs, 128), jnp.int32),
                       pltpu.VMEM((2, sub, 128), F32),
                       pltpu.VMEM((sub, slabs, 128), jnp.int32),
                       pltpu.SemaphoreType.DMA((2,))],
        compiler_params=pltpu.CompilerParams(needs_layout_passes=False))
    def k(tab_hbm, pos_hbm, w_hbm, out_hbm, idx_v, rows_v, w_v, out_v, sem):
        wid = lax.axis_index("s") * SC_CORES + lax.axis_index("c")
        for j in range(per_w):
            pltpu.sync_copy(pos_hbm.at[wid * per_w + j], idx_v.at[j])

        def first_token(step):
            return (wid * per_w + step // subs_per_chunk) * ch + (step % subs_per_chunk) * sub

        def copies(step, slot):
            j = step // subs_per_chunk
            s = step % subs_per_chunk
            idx = [idx_v.at[j, kk, pl.ds(s * sub, sub)] for kk in range(TOP_K)]
            return ([pltpu.make_async_copy(tab_hbm.at[idx[kk]], rows_v.at[slot, kk], sem.at[slot])
                     for kk in range(TOP_K)]
                    + [pltpu.make_async_copy(w_hbm.at[pl.ds(first_token(step), sub)], w_v.at[slot], sem.at[slot])])

        for cp in copies(0, 0):
            cp.start()

        @pl.loop(0, n_steps)
        def _(step):
            slot = step % 2

            @pl.when(step + 1 < n_steps)
            def _():
                for cp in copies(step + 1, 1 - slot):
                    cp.start()

            for cp in copies(step, slot):
                cp.wait()

            @pl.loop(0, sub)
            def _(tt):
                wk = [w_v[slot, tt, pl.ds(kk * lanes, lanes)] for kk in range(TOP_K)]
                for sl in range(slabs):
                    @plsc.parallel_loop(0, 128, step=lanes, unroll=4)
                    def _(off):
                        acc_lo = jnp.zeros((lanes,), F32)
                        acc_hi = jnp.zeros((lanes,), F32)
                        for kk in range(TOP_K):
                            word = rows_v[slot, kk, tt, sl, pl.ds(off, lanes)]
                            lo = plsc.bitcast(word << 16, F32)
                            hi = plsc.bitcast(word & jnp.int32(-65536), F32)
                            acc_lo = acc_lo + wk[kk] * lo
                            acc_hi = acc_hi + wk[kk] * hi
                        both = plsc.pack(acc_lo, acc_hi, format=plsc.PackFormat.INTERLEAVED)
                        out_v[tt, sl, pl.ds(off, lanes)] = plsc.bitcast(both, jnp.int32)

            pltpu.sync_copy(out_v, out_hbm.at[pl.ds(first_token(step), sub)])

    return k(table, pos3, wtok)


def _experts_kernel(texp_ref, nused_ref, tend_ref, xs_ref, weg_hbm, weu_hbm, wed_hbm, ys_ref,
                    wg_scr, wu_scr, wd_scr, wg_buf, wu_buf, wd_buf, sem, group_scr, *, expert_rows):
    step = pl.program_id(0)
    rows = expert_rows
    tiles_per_step = EXPERT_STEP_ROWS // expert_rows
    half = D_MODEL // 2
    n_used = nused_ref[0]

    def weight_copies(e, slot):
        return [pltpu.make_async_copy(weg_hbm.at[e], wg_buf.at[slot], sem.at[slot, 0]),
                pltpu.make_async_copy(weu_hbm.at[e], wu_buf.at[slot], sem.at[slot, 1]),
                pltpu.make_async_copy(wed_hbm.at[e], wd_buf.at[slot], sem.at[slot, 2])]

    def next_group(e):
        tile = tend_ref[e]
        return texp_ref[jnp.minimum(tile, n_used - 1)], tile < n_used

    def start_weights(e, slot, exists):
        @pl.when(exists)
        def _():
            for cp in weight_copies(e, slot):
                cp.start()

    @pl.when(step == 0)
    def _():
        group_scr[0] = 0
        e, exists = texp_ref[0], True
        for slot in range(WEIGHT_SLOTS - 1):
            start_weights(e, slot, exists)
            nxt, has_next = next_group(e)
            e, exists = nxt, exists & has_next

    def row_tile(tile, x_view, y_view):
        expert = texp_ref[tile]
        used = tile < n_used
        new_expert = (tile == 0) | (expert != texp_ref[jnp.maximum(tile - 1, 0)])

        @pl.when(used & new_expert)
        def _():
            group = group_scr[0]
            slot = group % WEIGHT_SLOTS
            ahead, exists = expert, True
            for _ in range(WEIGHT_SLOTS - 1):
                nxt, has_next = next_group(ahead)
                ahead, exists = nxt, exists & has_next
            start_weights(ahead, (group + WEIGHT_SLOTS - 1) % WEIGHT_SLOTS, exists)

            for cp in weight_copies(expert, slot):
                cp.wait()
            for scr, buf in ((wg_scr, wg_buf), (wu_scr, wu_buf), (wd_scr, wd_buf)):
                top, bottom = _unpack_pair(buf[slot])
                k_half = top.shape[0]
                scr[0:k_half, :] = top.astype(BF16)
                scr[k_half:2 * k_half, :] = bottom.astype(BF16)
            group_scr[0] = group + 1

        @pl.when(used)
        def _():
            lo, hi = _unpack_pair(_load_token_words(x_view, (), rows))
            lo = lo.astype(BF16)
            hi = hi.astype(BF16)
            g = _dot(lo, wg_scr[0:half, :]) + _dot(hi, wg_scr[half:D_MODEL, :])
            u = _dot(lo, wu_scr[0:half, :]) + _dot(hi, wu_scr[half:D_MODEL, :])
            y = _dot((_silu(g) * u).astype(BF16), wd_scr[...])
            _store_token_words(y_view, _pack_pair(y[:, :half], y[:, half:]), rows)

        @pl.when(jnp.logical_not(used) & (step == (n_used - 1) // tiles_per_step))
        def _():
            y_view[...] = jnp.zeros_like(y_view)

    for s in range(tiles_per_step):
        view = pl.ds(s * rows * ROW_SLABS, rows * ROW_SLABS)
        row_tile(step * tiles_per_step + s, xs_ref.at[view], ys_ref.at[view])


def _experts(texp, nused, tend, xs2d, weg, weu, wed, n_tiles, expert_rows):
    tiles_per_step = EXPERT_STEP_ROWS // expert_rows
    block = (EXPERT_STEP_ROWS * ROW_SLABS, 128)
    hbm = pl.BlockSpec(memory_space=pl.ANY)

    def block_idx(j, te, nu, tn):
        return (jnp.minimum(j, (nu[0] - 1) // tiles_per_step), 0)

    grid_spec = pltpu.PrefetchScalarGridSpec(
        num_scalar_prefetch=3,
        grid=(n_tiles // tiles_per_step,),
        in_specs=[pl.BlockSpec(block, block_idx), hbm, hbm, hbm],
        out_specs=pl.BlockSpec(block, block_idx),
        scratch_shapes=[pltpu.VMEM((D_MODEL, EXPERT_DIM), BF16),
                        pltpu.VMEM((D_MODEL, EXPERT_DIM), BF16),
                        pltpu.VMEM((EXPERT_DIM, D_MODEL), BF16),
                        pltpu.VMEM((WEIGHT_SLOTS,) + weg.shape[1:], jnp.int32),
                        pltpu.VMEM((WEIGHT_SLOTS,) + weu.shape[1:], jnp.int32),
                        pltpu.VMEM((WEIGHT_SLOTS,) + wed.shape[1:], jnp.int32),
                        pltpu.SemaphoreType.DMA((WEIGHT_SLOTS, 3)),
                        pltpu.SMEM((1,), jnp.int32)],
    )
    return pl.pallas_call(
        functools.partial(_experts_kernel, expert_rows=expert_rows),
        grid_spec=grid_spec,
        out_shape=jax.ShapeDtypeStruct(xs2d.shape, jnp.int32),
        compiler_params=pltpu.CompilerParams(dimension_semantics=("arbitrary",),
                                             vmem_limit_bytes=VMEM_LIMIT),
        name="experts",
    )(texp, nused, tend, xs2d, weg, weu, wed)


def _final_kernel(x_ref, routed_ref, mod_ref, g2_ref, wsg_ref, wsu_ref, wsd_ref, fng_ref, o_ref):
    tm = x_ref.shape[0]
    x = x_ref[...]
    hb = _rms_mod(x, g2_ref[...], mod_ref[0, 3:4, :], mod_ref[0, 4:5, :]).astype(BF16)
    shared = _dot((_silu(_dot(hb, wsg_ref[...])) * _dot(hb, wsu_ref[...])).astype(BF16), wsd_ref[...])
    routed = jnp.concatenate(_unpack_pair(_load_token_words(routed_ref, (), tm)), axis=1)
    y = x + mod_ref[0, 5:6, :] * (routed + shared)
    ms = jnp.mean(y * y, axis=-1, keepdims=True)
    o_ref[...] = y * lax.rsqrt(ms + EPS) * fng_ref[...]


def _final(x1, routed2d, mod3, norm2_g, wsg, wsu, wsd, final_g, seq_len, mod_row_of_batch):
    t = x1.shape[0]
    tm = TM_FINAL

    def mod_idx(i):
        return (mod_row_of_batch((i * tm) // seq_len), 0, 0)

    def full(a):
        return pl.BlockSpec(a.shape, lambda i: (0,) * a.ndim)

    return pl.pallas_call(
        _final_kernel,
        grid=(t // tm,),
        in_specs=[pl.BlockSpec((tm, D_MODEL), lambda i: (i, 0)),
                  pl.BlockSpec((tm * ROW_SLABS, 128), lambda i: (i, 0)),
                  pl.BlockSpec((1, 6, D_MODEL), mod_idx),
                  full(norm2_g), full(wsg), full(wsu), full(wsd), full(final_g)],
        out_specs=pl.BlockSpec((tm, D_MODEL), lambda i: (i, 0)),
        out_shape=jax.ShapeDtypeStruct((t, D_MODEL), F32),
        compiler_params=pltpu.CompilerParams(dimension_semantics=("parallel",),
                                             vmem_limit_bytes=VMEM_LIMIT),
        name="final",
    )(x1, routed2d, mod3, norm2_g, wsg, wsu, wsd, final_g)


def _moe(x1, mod3, lw, seq_len, mod_row_of_batch):
    t = x1.shape[0]
    expert_rows = min(MAX_EXPERT_ROWS, TOP_K * t // N_EXPERTS // 2)
    n_tiles = TOP_K * t // expert_rows + N_EXPERTS
    n_tiles_pad = -(-n_tiles // 128) * 128
    hp2d, ek, rk, wtok, cnt = _router(x1, mod3, lw["norm2_g"], lw["w_router_t"], lw["router_bias"],
                                      seq_len, mod_row_of_batch)
    pos, texp, nused, tend = _plan(ek, rk, cnt, n_tiles_pad, expert_rows)
    pos3 = pos.reshape(TOP_K, t // SC_CHUNK, SC_CHUNK).transpose(1, 0, 2)
    xs = _sc_dispatch(hp2d.reshape(t, ROW_SLABS, 128), pos3, n_tiles * expert_rows,
                      after=(lw["weg"], lw["weu"], lw["wed"]))
    ys2d = _experts(texp.reshape(-1), nused.reshape(-1), tend[:, 0], xs.reshape(-1, 128),
                    lw["weg"], lw["weu"], lw["wed"], n_tiles, expert_rows)
    routed = _sc_combine(ys2d.reshape(-1, ROW_SLABS, 128), pos3, wtok, t)
    return _final(x1, routed.reshape(t * ROW_SLABS, 128), mod3, lw["norm2_g"],
                  lw["wsg"], lw["wsu"], lw["wsd"], lw["final_g"], seq_len, mod_row_of_batch)


def _dft_tables(seq_len):
    gd = FOURIER_GROUP_DIM
    kc = np.arange(gd)
    ang_c = ((kc[:, None] * kc[None, :]) % gd) * (2.0 * math.pi / gd)
    cs = np.concatenate([np.cos(ang_c), np.sin(ang_c)], axis=1) * (gd ** -0.5)
    kl = np.arange(seq_len)
    ang_l = ((kl[:, None] * kl[None, :]) % seq_len) * (2.0 * math.pi / seq_len)
    cls = np.concatenate([np.cos(ang_l), -np.sin(ang_l)], axis=1) * (seq_len ** -0.5)
    return jnp.asarray(cs.astype(np.float32), dtype=BF16), jnp.asarray(cls.astype(np.float32), dtype=BF16)


def _rope_tables(length):
    rows = length // GRID_W
    r = np.repeat(np.arange(rows, dtype=np.float32), GRID_W)
    col = np.tile(np.arange(GRID_W, dtype=np.float32), rows)
    nf = RET_HEAD_DIM // 4
    inv = (np.float32(ROPE_BASE) ** (-np.arange(nf, dtype=np.float32) / np.float32(nf))).astype(np.float32)
    ar = r[:, None] * inv[None]
    ac = col[:, None] * inv[None]
    ang = np.concatenate([ar, ar, ac, ac], axis=-1).astype(np.float64)
    sign = np.where((np.arange(RET_HEAD_DIM) & nf) == 0, -1.0, 1.0)
    return (jnp.asarray(np.cos(ang).astype(np.float32)),
            jnp.asarray((np.sin(ang) * sign[None, :]).astype(np.float32)))


def _trunk_path(x, mod3, mod_row_of_batch, s0f, s0b, rope, lw):
    batch, seq_len, _ = x.shape
    x2d = x.reshape(batch * seq_len, D_MODEL)
    uf, q, k, v, sg, gf, gr = _inproj(x2d, mod3, lw["norm1_g"], lw["w_in"], seq_len, mod_row_of_batch, rope)
    r, s_f, s_b = _retention(q, k, v, sg, lw["dec"], lw["gn_g"], s0f, s0b, batch, seq_len)
    cs, cls = _dft_tables(seq_len)
    x1 = _fnet_merge(uf, cs, cls, r, gf, gr, x2d, mod3, lw["w_four"], lw["w_ret"], lw["w_o"],
                     batch, seq_len, mod_row_of_batch)
    y = _moe(x1, mod3, lw, seq_len, mod_row_of_batch)
    return y.reshape(batch, seq_len, D_MODEL), s_f, s_b


def kernel(x_prompt, x_sample, state_ret_fwd, state_ret_bwd, c, c_ctx, w_ada, b_ada, norm1_g, norm2_g, w_in,
           ret_decay_fwd, ret_decay_bwd, ret_gn_g, w_four_out, w_ret_out, w_out, w_router, router_bias,
           w_exp_gate, w_exp_up, w_exp_down, w_shared_gate, w_shared_up, w_shared_down, final_norm_g):
    depth = w_ada.shape[0]
    assert depth == 1, "final norm is fused into the last layer's MoE kernel"
    n_ctx, n_lat = x_prompt.shape[0], x_sample.shape[0]
    cond = jnp.concatenate([c_ctx[None, :], c], axis=0)
    cond = jnp.pad(cond, ((0, (-cond.shape[0]) % 8), (0, 0)))
    rope = _rope_tables(x_sample.shape[1])
    zeros = jnp.zeros((n_ctx, N_RET_HEADS, RET_HEAD_DIM, RET_HEAD_DIM), F32)

    layer = 0
    mod = _ada(cond, w_ada[layer], b_ada[layer][None, :])
    mod3 = mod.reshape(mod.shape[0], 6, D_MODEL)
    dec = jnp.stack([ret_decay_fwd[layer], ret_decay_bwd[layer]], axis=1)
    lw = {
        "norm1_g": norm1_g[layer][None, :],
        "norm2_g": norm2_g[layer][None, :],
        "w_in": w_in[layer],
        "dec": jnp.broadcast_to(dec[:, :, None], (N_RET_HEADS, 2, RET_HEAD_DIM)).astype(F32),
        "gn_g": ret_gn_g[layer][None, :],
        "w_four": w_four_out[layer],
        "w_ret": w_ret_out[layer],
        "w_o": w_out[layer],
        "w_router_t": w_router[layer].T,
        "router_bias": router_bias[layer][:, None],
        "weg": _sc_pack_weight_halves(w_exp_gate[layer]),
        "weu": _sc_pack_weight_halves(w_exp_up[layer]),
        "wed": _sc_pack_weight_halves(w_exp_down[layer]),
        "wsg": w_shared_gate[layer].astype(BF16),
        "wsu": w_shared_up[layer].astype(BF16),
        "wsd": w_shared_down[layer].astype(BF16),
        "final_g": final_norm_g[None, :],
    }
    y_sample, _, _ = _trunk_path(x_sample, mod3, lambda b: 1 + b, state_ret_fwd[:, layer],
                                 state_ret_bwd[:, layer], rope, lw)
    y_prompt, s_f, s_b = _trunk_path(x_prompt, mod3, lambda b: 0, zeros, zeros, None, lw)
    return (y_prompt, y_sample, s_f[:, None], s_b[:, None])
```

```python
import functools
import math

import jax
import jax.numpy as jnp
import numpy as np
from jax import lax
from jax.experimental import pallas as pl
from jax.experimental.pallas import tpu as pltpu
from jax.experimental.pallas import tpu_sc as plsc

F32 = jnp.float32
BF16 = jnp.bfloat16

D_MODEL = 1024
GRID_W = 64
N_FOURIER_GROUPS = 8
FOURIER_GROUP_DIM = 128
N_RET_HEADS = 4
RET_HEAD_DIM = 128
RET_WIDTH = N_RET_HEADS * RET_HEAD_DIM
CHUNK = 128
N_EXPERTS = 64
N_EXPERT_GROUPS = 8
EXPERTS_PER_GROUP = N_EXPERTS // N_EXPERT_GROUPS
TOPK_GROUPS = 4
TOP_K = 8
EXPERT_DIM = 256
ROUTED_SCALE = 2.5
ROPE_BASE = 10000.0
EPS = 1e-6
Q_SCALE = RET_HEAD_DIM ** -0.5

_C_UF = (0, 1024)
_C_Q = (1024, 1536)
_C_K = (1536, 2048)
_C_V = (2048, 2560)
_C_G = (2560, 3072)
_C_GF = (3072, 4096)
_C_GR = (4096, 5120)

VMEM_LIMIT = 56 * 1024 * 1024

TM_INPROJ = 1024
TM_ROUTER = 512
FNET_ROWS = 512
TM_FINAL = 1024
EXPERT_STEP_ROWS = 2048
MAX_EXPERT_ROWS = 512
WEIGHT_SLOTS = 3
ROW_SLABS = 4
SC_CORES = 2
SC_WORKERS = 32
SC_CHUNK = 128
SC_LANES = 16
SC_PACK_BLOCK_WORDS = 16384
SC_COMBINE_TOKENS = 8


def _silu(x):
    return x * jax.nn.sigmoid(x)


def _dot(a, b):
    return jnp.dot(a, b, preferred_element_type=F32)


def _rms_mod(x, g, shift, scale):
    ms = jnp.mean(x * x, axis=-1, keepdims=True)
    y = x * lax.rsqrt(ms + EPS) * g
    return y * (1.0 + scale) + shift


def _ada_kernel(cond_ref, w_ref, b_ref, o_ref):
    s = _silu(cond_ref[...]).astype(BF16)
    o_ref[...] = _dot(s, w_ref[...].astype(BF16)) + b_ref[...]


def _ada(cond, w_ada, b_ada):
    rows, n = cond.shape[0], w_ada.shape[1]
    tn = 1536
    return pl.pallas_call(
        _ada_kernel,
        grid=(n // tn,),
        in_specs=[pl.BlockSpec((rows, D_MODEL), lambda j: (0, 0)),
                  pl.BlockSpec((D_MODEL, tn), lambda j: (0, j)),
                  pl.BlockSpec((1, tn), lambda j: (0, j))],
        out_specs=pl.BlockSpec((rows, tn), lambda j: (0, j)),
        out_shape=jax.ShapeDtypeStruct((rows, n), F32),
        compiler_params=pltpu.CompilerParams(vmem_limit_bytes=VMEM_LIMIT),
        name="ada",
    )(cond, w_ada, b_ada)


def _rope_head(x, cos, sin_signed, first_half):
    partner = jnp.where(first_half, pltpu.roll(x, 96, 1), pltpu.roll(x, 32, 1))
    return x * cos + partner * sin_signed


def _inproj_kernel(*refs, use_rope):
    if use_rope:
        x_ref, mod_ref, g_ref, w_ref, cos_ref, sin_ref = refs[:6]
        outs = refs[6:]
    else:
        x_ref, mod_ref, g_ref, w_ref = refs[:4]
        outs = refs[4:]
    uf_o, q_o, k_o, v_o, sg_o, gf_o, gr_o = outs

    h = _rms_mod(x_ref[...], g_ref[...], mod_ref[0, 0:1, :], mod_ref[0, 1:2, :])
    hb = h.astype(BF16)

    def proj(cols):
        return _dot(hb, w_ref[:, cols[0]:cols[1]].astype(BF16))

    uf_o[...] = proj(_C_UF).astype(BF16)
    q = proj(_C_Q)
    k = proj(_C_K)
    if use_rope:
        cos = cos_ref[...]
        sin_signed = sin_ref[...]
        lane = lax.broadcasted_iota(jnp.int32, cos.shape, 1)
        first_half = (lane & 32) == 0
        for hd in range(N_RET_HEADS):
            sl = slice(hd * RET_HEAD_DIM, (hd + 1) * RET_HEAD_DIM)
            q_o[:, sl] = (_rope_head(q[:, sl], cos, sin_signed, first_half) * Q_SCALE).astype(BF16)
            k_o[:, sl] = _rope_head(k[:, sl], cos, sin_signed, first_half).astype(BF16)
    else:
        q_o[...] = (q * Q_SCALE).astype(BF16)
        k_o[...] = k.astype(BF16)
    v_o[...] = proj(_C_V).astype(BF16)
    sg_o[...] = _silu(proj(_C_G)).astype(BF16)
    gf_o[...] = jax.nn.sigmoid(proj(_C_GF)).astype(BF16)
    gr_o[...] = jax.nn.sigmoid(proj(_C_GR)).astype(BF16)


def _inproj(x2d, mod3, norm_g, w_in_f32, seq_len, mod_row_of_batch, rope):
    t = x2d.shape[0]
    tm = TM_INPROJ
    tiles_per_seq = max(seq_len // tm, 1)

    def mod_idx(i):
        return (mod_row_of_batch((i * tm) // seq_len), 0, 0)

    in_specs = [pl.BlockSpec((tm, D_MODEL), lambda i: (i, 0)),
                pl.BlockSpec((1, 6, D_MODEL), mod_idx),
                pl.BlockSpec((1, D_MODEL), lambda i: (0, 0)),
                pl.BlockSpec(w_in_f32.shape, lambda i: (0, 0), pipeline_mode=pl.Buffered(1))]
    args = [x2d, mod3, norm_g, w_in_f32]
    if rope is not None:
        in_specs += [pl.BlockSpec((tm, RET_HEAD_DIM), lambda i: (i % tiles_per_seq, 0))] * 2
        args += list(rope)
    widths = [1024, RET_WIDTH, RET_WIDTH, RET_WIDTH, RET_WIDTH, 1024, 1024]
    return pl.pallas_call(
        functools.partial(_inproj_kernel, use_rope=rope is not None),
        grid=(t // tm,),
        in_specs=in_specs,
        out_specs=[pl.BlockSpec((tm, w), lambda i: (i, 0)) for w in widths],
        out_shape=[jax.ShapeDtypeStruct((t, w), BF16) for w in widths],
        compiler_params=pltpu.CompilerParams(dimension_semantics=("parallel",),
                                             vmem_limit_bytes=VMEM_LIMIT),
        name="inproj",
    )(*args)


def _retention_kernel(q_ref, k_ref, v_ref, sg_ref, dec_ref, gn_ref, s0f_ref, s0b_ref,
                      r_ref, sfo_ref, sbo_ref, tab_scr, gc_scr):
    n_chunks = q_ref.shape[0] // CHUNK
    hd = RET_HEAD_DIM

    @pl.when(pl.program_id(0) == 0)
    def _():
        row = lax.broadcasted_iota(jnp.int32, (CHUNK, CHUNK), 0).astype(F32)
        col = lax.broadcasted_iota(jnp.int32, (CHUNK, CHUNK), 1).astype(F32)
        diff = row - col
        for h in range(N_RET_HEADS):
            dec = dec_ref[h]
            lg = jnp.minimum(dec, 0.0) - jnp.log1p(jnp.exp(-jnp.abs(dec)))
            lgf = lg[0:1, :]
            lgb = lg[1:2, :]
            tab_scr[h, 0] = jnp.exp(jnp.where(diff >= 0, lgf * diff, lgb * (-diff)))
            tab_scr[h, 1] = jnp.exp(lgf * (row + 1.0))
            tab_scr[h, 2] = jnp.exp(lgb * (CHUNK - row))
            tab_scr[h, 3] = jnp.exp(lgf * (CHUNK - 1.0 - col))
            tab_scr[h, 4] = jnp.exp(lgb * col)
            gc_scr[h] = jnp.exp(lg * CHUNK)

    def rows(n):
        return slice(n * CHUNK, (n + 1) * CHUNK)

    for h in range(N_RET_HEADS):
        cols = slice(h * hd, (h + 1) * hd)
        decay, qw_f, qw_b, kwt_f, kwt_b = (tab_scr[h, i] for i in range(5))
        gc = gc_scr[h]
        gc_f = gc[0:1, :]
        gc_b = gc[1:2, :]

        kv_f, kv_b = [], []
        for n in range(n_chunks):
            kt = k_ref[rows(n), cols].astype(F32).T
            vn = v_ref[rows(n), cols]
            kv_f.append(_dot((kt * kwt_f).astype(BF16), vn))
            kv_b.append(_dot((kt * kwt_b).astype(BF16), vn))

        s = s0f_ref[h]
        prev_f = []
        for n in range(n_chunks):
            prev_f.append(s.astype(BF16))
            s = gc_f * s + kv_f[n]
        sfo_ref[h] = s
        s = s0b_ref[h]
        prev_b = [None] * n_chunks
        for n in reversed(range(n_chunks)):
            prev_b[n] = s.astype(BF16)
            s = gc_b * s + kv_b[n]
        sbo_ref[h] = s

        gn = gn_ref[:, cols]
        for n in range(n_chunks):
            qn = q_ref[rows(n), cols]
            qf = qn.astype(F32)
            scores = lax.dot_general(qn, k_ref[rows(n), cols], (((1,), (1,)), ((), ())),
                                     preferred_element_type=F32)
            o = _dot((scores * decay).astype(BF16), v_ref[rows(n), cols])
            o = o + _dot((qf * qw_f).astype(BF16), prev_f[n])
            o = o + _dot((qf * qw_b).astype(BF16), prev_b[n])
            mu = jnp.mean(o, axis=-1, keepdims=True)
            d = o - mu
            var = jnp.mean(d * d, axis=-1, keepdims=True)
            on = d * lax.rsqrt(var + EPS) * gn
            r_ref[rows(n), cols] = (on * sg_ref[rows(n), cols].astype(F32)).astype(BF16)


def _retention(q, k, v, sg, dec, gn_g, s0f, s0b, batch, seq_len):
    hd = RET_HEAD_DIM
    tok_spec = pl.BlockSpec((seq_len, RET_WIDTH), lambda b: (b, 0))
    st_spec = pl.BlockSpec((None, N_RET_HEADS, hd, hd), lambda b: (b, 0, 0, 0))
    st_shape = jax.ShapeDtypeStruct((batch, N_RET_HEADS, hd, hd), F32)
    return pl.pallas_call(
        _retention_kernel,
        grid=(batch,),
        in_specs=[tok_spec, tok_spec, tok_spec, tok_spec,
                  pl.BlockSpec(dec.shape, lambda b: (0, 0, 0)),
                  pl.BlockSpec(gn_g.shape, lambda b: (0, 0)),
                  st_spec, st_spec],
        out_specs=[tok_spec, st_spec, st_spec],
        out_shape=[jax.ShapeDtypeStruct((batch * seq_len, RET_WIDTH), BF16), st_shape, st_shape],
        scratch_shapes=[pltpu.VMEM((N_RET_HEADS, 5, CHUNK, CHUNK), F32),
                        pltpu.VMEM((N_RET_HEADS, 2, hd), F32)],
        compiler_params=pltpu.CompilerParams(dimension_semantics=("arbitrary",),
                                             vmem_limit_bytes=VMEM_LIMIT),
        name="retention",
    )(q, k, v, sg, dec, gn_g, s0f, s0b)


def _fnet_merge_kernel(uf_ref, cs_ref, cls_ref, r_ref, gf_ref, gr_ref, x_ref, mod_ref, wf_ref, wr_ref, wo_ref,
                       o_ref, xcs_ref):
    seq_len = uf_ref.shape[0]
    gd = FOURIER_GROUP_DIM

    @pl.when(pl.program_id(1) == 0)
    def _():
        for g in range(N_FOURIER_GROUPS):
            x = _dot(uf_ref[:, g * gd:(g + 1) * gd], cs_ref[...])
            xcs_ref[0:seq_len, g * gd:(g + 1) * gd] = x[:, :gd].astype(BF16)
            xcs_ref[seq_len:2 * seq_len, g * gd:(g + 1) * gd] = x[:, gd:].astype(BF16)

    f_mix = _dot(cls_ref[...], xcs_ref[...]).astype(BF16)
    f_out = _dot(f_mix, wf_ref[...].astype(BF16))
    r_out = _dot(r_ref[...], wr_ref[...].astype(BF16))
    merged = gf_ref[...].astype(F32) * f_out + gr_ref[...].astype(F32) * r_out
    mix = _dot(merged.astype(BF16), wo_ref[...].astype(BF16))
    o_ref[...] = x_ref[...] + mod_ref[0, 2:3, :] * mix


def _fnet_merge(uf, cs, cls, r, gf, gr, x2d, mod3, w_four, w_ret, w_o, batch, seq_len, mod_row_of_batch):
    rb = min(FNET_ROWS, seq_len)
    nr = seq_len // rb

    def tok(w):
        return pl.BlockSpec((rb, w), lambda b, i: (b * nr + i, 0))

    def full(a):
        return pl.BlockSpec(a.shape, lambda b, i: (0, 0))

    def once(a):
        return pl.BlockSpec(a.shape, lambda b, i: (0, 0), pipeline_mode=pl.Buffered(1))

    return pl.pallas_call(
        _fnet_merge_kernel,
        grid=(batch, nr),
        in_specs=[pl.BlockSpec((seq_len, D_MODEL), lambda b, i: (b, 0)),
                  full(cs),
                  pl.BlockSpec((rb, 2 * seq_len), lambda b, i: (i, 0)),
                  tok(RET_WIDTH), tok(D_MODEL), tok(D_MODEL), tok(D_MODEL),
                  pl.BlockSpec((1, 6, D_MODEL), lambda b, i: (mod_row_of_batch(b), 0, 0)),
                  once(w_four), once(w_ret), once(w_o)],
        out_specs=tok(D_MODEL),
        out_shape=jax.ShapeDtypeStruct((batch * seq_len, D_MODEL), F32),
        scratch_shapes=[pltpu.VMEM((2 * seq_len, D_MODEL), BF16)],
        compiler_params=pltpu.CompilerParams(dimension_semantics=("parallel", "arbitrary"),
                                             vmem_limit_bytes=VMEM_LIMIT),
        name="fnet_merge",
    )(uf, cs, cls, r, gf, gr, x2d, mod3, w_four, w_ret, w_o)


def _pack_pair(lo_f32, hi_f32):
    lo = lax.bitcast_convert_type(lo_f32.astype(BF16).astype(F32), jnp.uint32)
    hi = lax.bitcast_convert_type(hi_f32.astype(BF16).astype(F32), jnp.uint32)
    return lax.bitcast_convert_type((lo >> 16) | hi, jnp.int32)


def _unpack_pair(words_i32):
    w = lax.bitcast_convert_type(words_i32, jnp.uint32)
    lo = lax.bitcast_convert_type(w << 16, F32)
    hi = lax.bitcast_convert_type(w & jnp.uint32(0xFFFF0000), F32)
    return lo, hi


def _load_token_words(ref, lead, n_tok):
    parts = []
    for s in range(ROW_SLABS):
        idx = (pl.ds(s, n_tok, stride=ROW_SLABS), slice(None))
        parts.append(ref[lead + idx] if lead else ref[idx])
    return jnp.concatenate(parts, axis=1)


def _store_token_words(ref, words, n_tok):
    for s in range(ROW_SLABS):
        ref[pl.ds(s, n_tok, stride=ROW_SLABS), :] = words[:, s * 128:(s + 1) * 128]


def _route(scores, biased):
    tokens = scores.shape[1]
    neg = -jnp.inf
    epg = EXPERTS_PER_GROUP
    iota_g = lax.broadcasted_iota(jnp.int32, (epg, tokens), 0).astype(F32)

    def pick_first_max(cur, iota, size):
        m = jnp.max(cur, axis=0, keepdims=True)
        idx = jnp.min(jnp.where(cur == m, iota, float(size)), axis=0, keepdims=True)
        return m, idx, iota == idx

    group_scores = []
    for g in range(N_EXPERT_GROUPS):
        vals = biased[g * epg:(g + 1) * epg, :]
        m1, _, hit = pick_first_max(vals, iota_g, epg)
        m2 = jnp.max(jnp.where(hit, neg, vals), axis=0, keepdims=True)
        group_scores.append(m1 + m2)
    cur = jnp.concatenate(group_scores, axis=0)
    group_sel = jnp.zeros_like(cur)
    for _ in range(TOPK_GROUPS):
        _, _, hit = pick_first_max(cur, iota_g, N_EXPERT_GROUPS)
        group_sel = jnp.where(hit, 1.0, group_sel)
        cur = jnp.where(hit, neg, cur)
    masked = jnp.concatenate(
        [jnp.where(group_sel[g:g + 1, :] > 0.0, biased[g * epg:(g + 1) * epg, :], neg)
         for g in range(N_EXPERT_GROUPS)], axis=0)
    iota_e = lax.broadcasted_iota(jnp.int32, masked.shape, 0).astype(F32)
    sel = jnp.zeros_like(masked)
    cur = masked
    picks = []
    for _ in range(TOP_K):
        _, idx, hit = pick_first_max(cur, iota_e, N_EXPERTS)
        picks.append(idx)
        sel = jnp.where(hit, 1.0, sel)
        cur = jnp.where(hit, neg, cur)
    w = scores * sel
    return w / jnp.sum(w, axis=0, keepdims=True) * ROUTED_SCALE, sel, picks


def _router_kernel(x_ref, mod_ref, g2_ref, wrt_ref, rb_ref, hp_ref, ek_ref, rk_ref, wt_ref, cnt_ref,
                   run_scr, earlier_scr):
    tm = x_ref.shape[0]

    @pl.when(pl.program_id(0) == 0)
    def _():
        run_scr[...] = jnp.zeros_like(run_scr)
        earlier = (lax.broadcasted_iota(jnp.int32, (tm, tm), 0) < lax.broadcasted_iota(jnp.int32, (tm, tm), 1))
        earlier_scr[...] = jnp.where(earlier, 1.0, 0.0).astype(BF16)

    h = _rms_mod(x_ref[...], g2_ref[...], mod_ref[0, 3:4, :], mod_ref[0, 4:5, :])
    half = D_MODEL // 2
    _store_token_words(hp_ref, _pack_pair(h[:, :half], h[:, half:]), tm)

    def split(a):
        hi = a.astype(BF16)
        return hi, (a - hi.astype(F32)).astype(BF16)

    def dot_nt(a, b):
        return lax.dot_general(a, b, (((1,), (1,)), ((), ())), preferred_element_type=F32)

    h_hi, h_lo = split(h)
    w_hi, w_lo = split(wrt_ref[...])
    logits_t = dot_nt(w_hi, h_hi) + (dot_nt(w_hi, h_lo) + dot_nt(w_lo, h_hi))
    scores = jax.nn.sigmoid(logits_t)
    comb_t, sel, picks = _route(scores, scores + rb_ref[...])

    rank_t = _dot(sel.astype(BF16), earlier_scr[...]) + run_scr[...]
    run_scr[...] += jnp.sum(sel, axis=1, keepdims=True)
    cnt_ref[...] = jnp.broadcast_to(run_scr[...], cnt_ref.shape)

    iota_e = lax.broadcasted_iota(jnp.int32, sel.shape, 0).astype(F32)
    ranks, weights = [], []
    for idx in picks:
        hit = iota_e == idx
        ranks.append(jnp.sum(jnp.where(hit, rank_t, 0.0), axis=0, keepdims=True))
        weights.append(jnp.sum(jnp.where(hit, comb_t, 0.0), axis=0, keepdims=True))
    ek_ref[...] = jnp.concatenate(picks, axis=0).astype(jnp.int32)
    rk_ref[...] = jnp.concatenate(ranks, axis=0).astype(jnp.int32)
    w_rep = jnp.concatenate([jnp.broadcast_to(w, (SC_LANES, tm)) for w in weights], axis=0)
    wt_ref[...] = w_rep.T


def _router(x1, mod3, norm2_g, w_router_t, router_bias, seq_len, mod_row_of_batch):
    t = x1.shape[0]
    tm = TM_ROUTER

    def mod_idx(i):
        return (mod_row_of_batch((i * tm) // seq_len), 0, 0)

    def full(a):
        return pl.BlockSpec(a.shape, lambda i: (0,) * a.ndim)

    return pl.pallas_call(
        _router_kernel,
        grid=(t // tm,),
        in_specs=[pl.BlockSpec((tm, D_MODEL), lambda i: (i, 0)),
                  pl.BlockSpec((1, 6, D_MODEL), mod_idx),
                  full(norm2_g), full(w_router_t), full(router_bias)],
        out_specs=[pl.BlockSpec((tm * ROW_SLABS, 128), lambda i: (i, 0)),
                   pl.BlockSpec((TOP_K, tm), lambda i: (0, i)),
                   pl.BlockSpec((TOP_K, tm), lambda i: (0, i)),
                   pl.BlockSpec((tm, 128), lambda i: (i, 0)),
                   pl.BlockSpec((N_EXPERTS, 128), lambda i: (0, 0))],
        out_shape=[jax.ShapeDtypeStruct((t * ROW_SLABS, 128), jnp.int32),
                   jax.ShapeDtypeStruct((TOP_K, t), jnp.int32),
                   jax.ShapeDtypeStruct((TOP_K, t), jnp.int32),
                   jax.ShapeDtypeStruct((t, 128), F32),
                   jax.ShapeDtypeStruct((N_EXPERTS, 128), F32)],
        scratch_shapes=[pltpu.VMEM((N_EXPERTS, 1), F32), pltpu.VMEM((tm, tm), BF16)],
        compiler_params=pltpu.CompilerParams(dimension_semantics=("arbitrary",),
                                             vmem_limit_bytes=VMEM_LIMIT),
        name="router",
    )(x1, mod3, norm2_g, w_router_t, router_bias)


def _plan_kernel(ek_ref, rk_ref, cnt_ref, pos_ref, texp_ref, nused_ref, tend_ref, *, expert_rows):
    rows = float(expert_rows)
    cnt = cnt_ref[:, 0:1]
    tiles = jnp.floor((cnt + (rows - 1.0)) / rows)
    before = (lax.broadcasted_iota(jnp.int32, (N_EXPERTS, N_EXPERTS), 1)
              < lax.broadcasted_iota(jnp.int32, (N_EXPERTS, N_EXPERTS), 0))
    tile_start = jnp.dot(jnp.where(before, 1.0, 0.0), jnp.broadcast_to(tiles, (N_EXPERTS, 128)),
                         precision=lax.Precision.HIGHEST, preferred_element_type=F32)[:, 0:1]
    tile_end = tile_start + tiles
    row_start = tile_start * rows

    ek = ek_ref[...]
    pos = rk_ref[...].astype(F32)
    tile_id = lax.broadcasted_iota(jnp.int32, texp_ref.shape, 1).astype(F32)
    texp = jnp.zeros(texp_ref.shape, F32)
    for e in range(N_EXPERTS):
        pos = pos + jnp.where(ek == e, row_start[e:e + 1, :], 0.0)
        texp = texp + jnp.where(tile_id >= tile_end[e:e + 1, :], 1.0, 0.0)
    pos_ref[...] = pos.astype(jnp.int32)
    texp_ref[...] = jnp.minimum(texp, N_EXPERTS - 1.0).astype(jnp.int32)
    nused_ref[...] = jnp.broadcast_to(tile_end[N_EXPERTS - 1:N_EXPERTS, :], nused_ref.shape).astype(jnp.int32)
    tend_ref[...] = jnp.broadcast_to(tile_end, tend_ref.shape).astype(jnp.int32)


def _plan(ek, rk, cnt, n_tiles_pad, expert_rows):
    t = ek.shape[1]

    def full(shape):
        return pl.BlockSpec(shape, lambda: (0,) * len(shape))

    return pl.pallas_call(
        functools.partial(_plan_kernel, expert_rows=expert_rows),
        in_specs=[full(ek.shape), full(rk.shape), full(cnt.shape)],
        out_specs=[full((TOP_K, t)), full((1, n_tiles_pad)), full((1, 128)), full((N_EXPERTS, 128))],
        out_shape=[jax.ShapeDtypeStruct((TOP_K, t), jnp.int32),
                   jax.ShapeDtypeStruct((1, n_tiles_pad), jnp.int32),
                   jax.ShapeDtypeStruct((1, 128), jnp.int32),
                   jax.ShapeDtypeStruct((N_EXPERTS, 128), jnp.int32)],
        compiler_params=pltpu.CompilerParams(vmem_limit_bytes=VMEM_LIMIT),
        name="plan",
    )(ek, rk, cnt)


def _sc_mesh():
    return plsc.VectorSubcoreMesh(core_axis_name="c", subcore_axis_name="s")


def _sc_pack_weight_halves(w):
    e, k, n = w.shape
    k_half = k // 2
    rb = SC_PACK_BLOCK_WORDS // n
    units_per_expert = k_half // rb
    per_w = (e * units_per_expert) // SC_WORKERS
    lanes = SC_LANES

    @functools.partial(
        pl.kernel, out_type=jax.ShapeDtypeStruct((e * k_half, n), jnp.int32), mesh=_sc_mesh(),
        scratch_types=[pltpu.VMEM((rb, n), F32), pltpu.VMEM((rb, n), F32), pltpu.VMEM((rb, n), jnp.int32)],
        compiler_params=pltpu.CompilerParams(needs_layout_passes=False))
    def kern(w_hbm, out_hbm, a_v, b_v, o_v):
        wid = lax.axis_index("s") * SC_CORES + lax.axis_index("c")

        @pl.loop(0, per_w)
        def _(j):
            unit = wid * per_w + j
            expert = unit // units_per_expert
            blk = unit % units_per_expert
            row_a = expert * k + blk * rb
            pltpu.sync_copy(w_hbm.at[pl.ds(row_a, rb)], a_v)
            pltpu.sync_copy(w_hbm.at[pl.ds(row_a + k_half, rb)], b_v)

            @pl.loop(0, rb)
            def _(r):
                @plsc.parallel_loop(0, n, step=lanes, unroll=4)
                def _(c):
                    both = plsc.pack(a_v[r, pl.ds(c, lanes)], b_v[r, pl.ds(c, lanes)],
                                     format=plsc.PackFormat.INTERLEAVED)
                    o_v[r, pl.ds(c, lanes)] = plsc.bitcast(both, jnp.int32)

            pltpu.sync_copy(o_v, out_hbm.at[pl.ds(expert * k_half + blk * rb, rb)])

    return kern(w.reshape(e * k, n)).reshape(e, k_half, n)


def _sc_dispatch(rows, pos3, n_out, after=()):
    t = rows.shape[0]
    ch = SC_CHUNK
    per_w = (t // ch) // SC_WORKERS

    @functools.partial(
        pl.kernel, out_type=jax.ShapeDtypeStruct((n_out,) + rows.shape[1:], jnp.int32), mesh=_sc_mesh(),
        scratch_types=[pltpu.VMEM((TOP_K, ch), jnp.int32), pltpu.VMEM((ch,) + rows.shape[1:], jnp.int32),
                       pltpu.SemaphoreType.DMA])
    def k(rows_hbm, pos_hbm, *rest):
        out_hbm, idx_v, rows_v, sem = rest[len(after):]
        wid = lax.axis_index("s") * SC_CORES + lax.axis_index("c")

        @pl.loop(0, per_w)
        def _(j):
            c = wid * per_w + j
            pltpu.sync_copy(pos_hbm.at[c], idx_v)
            pltpu.sync_copy(rows_hbm.at[pl.ds(c * ch, ch)], rows_v)
            copies = [pltpu.async_copy(rows_v, out_hbm.at[idx_v.at[kk]], sem) for kk in range(TOP_K)]
            for cp in copies:
                cp.wait()

    return k(rows, pos3, *after)


def _sc_combine(table, pos3, wtok, t):
    ch = SC_CHUNK
    sub = SC_COMBINE_TOKENS
    lanes = SC_LANES
    slabs = ROW_SLABS
    per_w = (t // ch) // SC_WORKERS
    subs_per_chunk = ch // sub
    n_steps = per_w * subs_per_chunk

    @functools.partial(
        pl.kernel, out_type=jax.ShapeDtypeStruct((t, slabs, 128), jnp.int32), mesh=_sc_mesh(),
        scratch_types=[pltpu.VMEM((per_w, TOP_K, ch), jnp.int32),
                       pltpu.VMEM((2, TOP_K, sub, slabs, 128), jnp.int32),
                       pltpu.VMEM((2, sub, 128), F32),
                       pltpu.VMEM((sub, slabs, 128), jnp.int32),
                       pltpu.SemaphoreType.DMA((2,))],
        compiler_params=pltpu.CompilerParams(needs_layout_passes=False))
    def k(tab_hbm, pos_hbm, w_hbm, out_hbm, idx_v, rows_v, w_v, out_v, sem):
        wid = lax.axis_index("s") * SC_CORES + lax.axis_index("c")
        for j in range(per_w):
            pltpu.sync_copy(pos_hbm.at[wid * per_w + j], idx_v.at[j])

        def first_token(step):
            return (wid * per_w + step // subs_per_chunk) * ch + (step % subs_per_chunk) * sub

        def copies(step, slot):
            j = step // subs_per_chunk
            s = step % subs_per_chunk
            idx = [idx_v.at[j, kk, pl.ds(s * sub, sub)] for kk in range(TOP_K)]
            return ([pltpu.make_async_copy(tab_hbm.at[idx[kk]], rows_v.at[slot, kk], sem.at[slot])
                     for kk in range(TOP_K)]
                    + [pltpu.make_async_copy(w_hbm.at[pl.ds(first_token(step), sub)], w_v.at[slot], sem.at[slot])])

        for cp in copies(0, 0):
            cp.start()

        @pl.loop(0, n_steps)
        def _(step):
            slot = step % 2

            @pl.when(step + 1 < n_steps)
            def _():
                for cp in copies(step + 1, 1 - slot):
                    cp.start()

            for cp in copies(step, slot):
                cp.wait()

            @pl.loop(0, sub)
            def _(tt):
                wk = [w_v[slot, tt, pl.ds(kk * lanes, lanes)] for kk in range(TOP_K)]
                for sl in range(slabs):
                    @plsc.parallel_loop(0, 128, step=lanes, unroll=4)
                    def _(off):
                        acc_lo = jnp.zeros((lanes,), F32)
                        acc_hi = jnp.zeros((lanes,), F32)
                        for kk in range(TOP_K):
                            word = rows_v[slot, kk, tt, sl, pl.ds(off, lanes)]
                            lo = plsc.bitcast(word << 16, F32)
                            hi = plsc.bitcast(word & jnp.int32(-65536), F32)
                            acc_lo = acc_lo + wk[kk] * lo
                            acc_hi = acc_hi + wk[kk] * hi
                        both = plsc.pack(acc_lo, acc_hi, format=plsc.PackFormat.INTERLEAVED)
                        out_v[tt, sl, pl.ds(off, lanes)] = plsc.bitcast(both, jnp.int32)

            pltpu.sync_copy(out_v, out_hbm.at[pl.ds(first_token(step), sub)])

    return k(table, pos3, wtok)


def _experts_kernel(texp_ref, nused_ref, tend_ref, xs_ref, weg_hbm, weu_hbm, wed_hbm, ys_ref,
                    wg_scr, wu_scr, wd_scr, wg_buf, wu_buf, wd_buf, sem, group_scr, *, expert_rows):
    step = pl.program_id(0)
    rows = expert_rows
    tiles_per_step = EXPERT_STEP_ROWS // expert_rows
    half = D_MODEL // 2
    n_used = nused_ref[0]

    def weight_copies(e, slot):
        return [pltpu.make_async_copy(weg_hbm.at[e], wg_buf.at[slot], sem.at[slot, 0]),
                pltpu.make_async_copy(weu_hbm.at[e], wu_buf.at[slot], sem.at[slot, 1]),
                pltpu.make_async_copy(wed_hbm.at[e], wd_buf.at[slot], sem.at[slot, 2])]

    def next_group(e):
        tile = tend_ref[e]
        return texp_ref[jnp.minimum(tile, n_used - 1)], tile < n_used

    def start_weights(e, slot, exists):
        @pl.when(exists)
        def _():
            for cp in weight_copies(e, slot):
                cp.start()

    @pl.when(step == 0)
    def _():
        group_scr[0] = 0
        e, exists = texp_ref[0], True
        for slot in range(WEIGHT_SLOTS - 1):
            start_weights(e, slot, exists)
            nxt, has_next = next_group(e)
            e, exists = nxt, exists & has_next

    def row_tile(tile, x_view, y_view):
        expert = texp_ref[tile]
        used = tile < n_used
        new_expert = (tile == 0) | (expert != texp_ref[jnp.maximum(tile - 1, 0)])

        @pl.when(used & new_expert)
        def _():
            group = group_scr[0]
            slot = group % WEIGHT_SLOTS
            ahead, exists = expert, True
            for _ in range(WEIGHT_SLOTS - 1):
                nxt, has_next = next_group(ahead)
                ahead, exists = nxt, exists & has_next
            start_weights(ahead, (group + WEIGHT_SLOTS - 1) % WEIGHT_SLOTS, exists)

            for cp in weight_copies(expert, slot):
                cp.wait()
            for scr, buf in ((wg_scr, wg_buf), (wu_scr, wu_buf), (wd_scr, wd_buf)):
                top, bottom = _unpack_pair(buf[slot])
                k_half = top.shape[0]
                scr[0:k_half, :] = top.astype(BF16)
                scr[k_half:2 * k_half, :] = bottom.astype(BF16)
            group_scr[0] = group + 1

        @pl.when(used)
        def _():
            lo, hi = _unpack_pair(_load_token_words(x_view, (), rows))
            lo = lo.astype(BF16)
            hi = hi.astype(BF16)
            g = _dot(lo, wg_scr[0:half, :]) + _dot(hi, wg_scr[half:D_MODEL, :])
            u = _dot(lo, wu_scr[0:half, :]) + _dot(hi, wu_scr[half:D_MODEL, :])
            y = _dot((_silu(g) * u).astype(BF16), wd_scr[...])
            _store_token_words(y_view, _pack_pair(y[:, :half], y[:, half:]), rows)

        @pl.when(jnp.logical_not(used) & (step == (n_used - 1) // tiles_per_step))
        def _():
            y_view[...] = jnp.zeros_like(y_view)

    for s in range(tiles_per_step):
        view = pl.ds(s * rows * ROW_SLABS, rows * ROW_SLABS)
        row_tile(step * tiles_per_step + s, xs_ref.at[view], ys_ref.at[view])


def _experts(texp, nused, tend, xs2d, weg, weu, wed, n_tiles, expert_rows):
    tiles_per_step = EXPERT_STEP_ROWS // expert_rows
    block = (EXPERT_STEP_ROWS * ROW_SLABS, 128)
    hbm = pl.BlockSpec(memory_space=pl.ANY)

    def block_idx(j, te, nu, tn):
        return (jnp.minimum(j, (nu[0] - 1) // tiles_per_step), 0)

    grid_spec = pltpu.PrefetchScalarGridSpec(
        num_scalar_prefetch=3,
        grid=(n_tiles // tiles_per_step,),
        in_specs=[pl.BlockSpec(block, block_idx), hbm, hbm, hbm],
        out_specs=pl.BlockSpec(block, block_idx),
        scratch_shapes=[pltpu.VMEM((D_MODEL, EXPERT_DIM), BF16),
                        pltpu.VMEM((D_MODEL, EXPERT_DIM), BF16),
                        pltpu.VMEM((EXPERT_DIM, D_MODEL), BF16),
                        pltpu.VMEM((WEIGHT_SLOTS,) + weg.shape[1:], jnp.int32),
                        pltpu.VMEM((WEIGHT_SLOTS,) + weu.shape[1:], jnp.int32),
                        pltpu.VMEM((WEIGHT_SLOTS,) + wed.shape[1:], jnp.int32),
                        pltpu.SemaphoreType.DMA((WEIGHT_SLOTS, 3)),
                        pltpu.SMEM((1,), jnp.int32)],
    )
    return pl.pallas_call(
        functools.partial(_experts_kernel, expert_rows=expert_rows),
        grid_spec=grid_spec,
        out_shape=jax.ShapeDtypeStruct(xs2d.shape, jnp.int32),
        compiler_params=pltpu.CompilerParams(dimension_semantics=("arbitrary",),
                                             vmem_limit_bytes=VMEM_LIMIT),
        name="experts",
    )(texp, nused, tend, xs2d, weg, weu, wed)


def _final_kernel(x_ref, routed_ref, mod_ref, g2_ref, wsg_ref, wsu_ref, wsd_ref, fng_ref, o_ref):
    tm = x_ref.shape[0]
    x = x_ref[...]
    hb = _rms_mod(x, g2_ref[...], mod_ref[0, 3:4, :], mod_ref[0, 4:5, :]).astype(BF16)
    shared = _dot((_silu(_dot(hb, wsg_ref[...])) * _dot(hb, wsu_ref[...])).astype(BF16), wsd_ref[...])
    routed = jnp.concatenate(_unpack_pair(_load_token_words(routed_ref, (), tm)), axis=1)
    y = x + mod_ref[0, 5:6, :] * (routed + shared)
    ms = jnp.mean(y * y, axis=-1, keepdims=True)
    o_ref[...] = y * lax.rsqrt(ms + EPS) * fng_ref[...]


def _final(x1, routed2d, mod3, norm2_g, wsg, wsu, wsd, final_g, seq_len, mod_row_of_batch):
    t = x1.shape[0]
    tm = TM_FINAL

    def mod_idx(i):
        return (mod_row_of_batch((i * tm) // seq_len), 0, 0)

    def full(a):
        return pl.BlockSpec(a.shape, lambda i: (0,) * a.ndim)

    return pl.pallas_call(
        _final_kernel,
        grid=(t // tm,),
        in_specs=[pl.BlockSpec((tm, D_MODEL), lambda i: (i, 0)),
                  pl.BlockSpec((tm * ROW_SLABS, 128), lambda i: (i, 0)),
                  pl.BlockSpec((1, 6, D_MODEL), mod_idx),
                  full(norm2_g), full(wsg), full(wsu), full(wsd), full(final_g)],
        out_specs=pl.BlockSpec((tm, D_MODEL), lambda i: (i, 0)),
        out_shape=jax.ShapeDtypeStruct((t, D_MODEL), F32),
        compiler_params=pltpu.CompilerParams(dimension_semantics=("parallel",),
                                             vmem_limit_bytes=VMEM_LIMIT),
        name="final",
    )(x1, routed2d, mod3, norm2_g, wsg, wsu, wsd, final_g)


def _moe(x1, mod3, lw, seq_len, mod_row_of_batch):
    t = x1.shape[0]
    expert_rows = min(MAX_EXPERT_ROWS, TOP_K * t // N_EXPERTS // 2)
    n_tiles = TOP_K * t // expert_rows + N_EXPERTS
    n_tiles_pad = -(-n_tiles // 128) * 128
    hp2d, ek, rk, wtok, cnt = _router(x1, mod3, lw["norm2_g"], lw["w_router_t"], lw["router_bias"],
                                      seq_len, mod_row_of_batch)
    pos, texp, nused, tend = _plan(ek, rk, cnt, n_tiles_pad, expert_rows)
    pos3 = pos.reshape(TOP_K, t // SC_CHUNK, SC_CHUNK).transpose(1, 0, 2)
    xs = _sc_dispatch(hp2d.reshape(t, ROW_SLABS, 128), pos3, n_tiles * expert_rows,
                      after=(lw["weg"], lw["weu"], lw["wed"]))
    ys2d = _experts(texp.reshape(-1), nused.reshape(-1), tend[:, 0], xs.reshape(-1, 128),
                    lw["weg"], lw["weu"], lw["wed"], n_tiles, expert_rows)
    routed = _sc_combine(ys2d.reshape(-1, ROW_SLABS, 128), pos3, wtok, t)
    return _final(x1, routed.reshape(t * ROW_SLABS, 128), mod3, lw["norm2_g"],
                  lw["wsg"], lw["wsu"], lw["wsd"], lw["final_g"], seq_len, mod_row_of_batch)


def _dft_tables(seq_len):
    gd = FOURIER_GROUP_DIM
    kc = np.arange(gd)
    ang_c = ((kc[:, None] * kc[None, :]) % gd) * (2.0 * math.pi / gd)
    cs = np.concatenate([np.cos(ang_c), np.sin(ang_c)], axis=1) * (gd ** -0.5)
    kl = np.arange(seq_len)
    ang_l = ((kl[:, None] * kl[None, :]) % seq_len) * (2.0 * math.pi / seq_len)
    cls = np.concatenate([np.cos(ang_l), -np.sin(ang_l)], axis=1) * (seq_len ** -0.5)
    return jnp.asarray(cs.astype(np.float32), dtype=BF16), jnp.asarray(cls.astype(np.float32), dtype=BF16)


def _rope_tables(length):
    rows = length // GRID_W
    r = np.repeat(np.arange(rows, dtype=np.float32), GRID_W)
    col = np.tile(np.arange(GRID_W, dtype=np.float32), rows)
    nf = RET_HEAD_DIM // 4
    inv = (np.float32(ROPE_BASE) ** (-np.arange(nf, dtype=np.float32) / np.float32(nf))).astype(np.float32)
    ar = r[:, None] * inv[None]
    ac = col[:, None] * inv[None]
    ang = np.concatenate([ar, ar, ac, ac], axis=-1).astype(np.float64)
    sign = np.where((np.arange(RET_HEAD_DIM) & nf) == 0, -1.0, 1.0)
    return (jnp.asarray(np.cos(ang).astype(np.float32)),
            jnp.asarray((np.sin(ang) * sign[None, :]).astype(np.float32)))


def _trunk_path(x, mod3, mod_row_of_batch, s0f, s0b, rope, lw):
    batch, seq_len, _ = x.shape
    x2d = x.reshape(batch * seq_len, D_MODEL)
    uf, q, k, v, sg, gf, gr = _inproj(x2d, mod3, lw["norm1_g"], lw["w_in"], seq_len, mod_row_of_batch, rope)
    r, s_f, s_b = _retention(q, k, v, sg, lw["dec"], lw["gn_g"], s0f, s0b, batch, seq_len)
    cs, cls = _dft_tables(seq_len)
    x1 = _fnet_merge(uf, cs, cls, r, gf, gr, x2d, mod3, lw["w_four"], lw["w_ret"], lw["w_o"],
                     batch, seq_len, mod_row_of_batch)
    y = _moe(x1, mod3, lw, seq_len, mod_row_of_batch)
    return y.reshape(batch, seq_len, D_MODEL), s_f, s_b


def kernel(x_prompt, x_sample, state_ret_fwd, state_ret_bwd, c, c_ctx, w_ada, b_ada, norm1_g, norm2_g, w_in,
           ret_decay_fwd, ret_decay_bwd, ret_gn_g, w_four_out, w_ret_out, w_out, w_router, router_bias,
           w_exp_gate, w_exp_up, w_exp_down, w_shared_gate, w_shared_up, w_shared_down, final_norm_g):
    depth = w_ada.shape[0]
    assert depth == 1, "final norm is fused into the last layer's MoE kernel"
    n_ctx, n_lat = x_prompt.shape[0], x_sample.shape[0]
    cond = jnp.concatenate([c_ctx[None, :], c], axis=0)
    cond = jnp.pad(cond, ((0, (-cond.shape[0]) % 8), (0, 0)))
    rope = _rope_tables(x_sample.shape[1])
    zeros = jnp.zeros((n_ctx, N_RET_HEADS, RET_HEAD_DIM, RET_HEAD_DIM), F32)

    layer = 0
    mod = _ada(cond, w_ada[layer], b_ada[layer][None, :])
    mod3 = mod.reshape(mod.shape[0], 6, D_MODEL)
    dec = jnp.stack([ret_decay_fwd[layer], ret_decay_bwd[layer]], axis=1)
    lw = {
        "norm1_g": norm1_g[layer][None, :],
        "norm2_g": norm2_g[layer][None, :],
        "w_in": w_in[layer],
        "dec": jnp.broadcast_to(dec[:, :, None], (N_RET_HEADS, 2, RET_HEAD_DIM)).astype(F32),
        "gn_g": ret_gn_g[layer][None, :],
        "w_four": w_four_out[layer],
        "w_ret": w_ret_out[layer],
        "w_o": w_out[layer],
        "w_router_t": w_router[layer].T,
        "router_bias": router_bias[layer][:, None],
        "weg": _sc_pack_weight_halves(w_exp_gate[layer]),
        "weu": _sc_pack_weight_halves(w_exp_up[layer]),
        "wed": _sc_pack_weight_halves(w_exp_down[layer]),
        "wsg": w_shared_gate[layer].astype(BF16),
        "wsu": w_shared_up[layer].astype(BF16),
        "wsd": w_shared_down[layer].astype(BF16),
        "final_g": final_norm_g[None, :],
    }
    y_prompt, s_f, s_b = _trunk_path(x_prompt, mod3, lambda b: 0, zeros, zeros, None, lw)
    y_sample, _, _ = _trunk_path(x_sample, mod3, lambda b: 1 + b, state_ret_fwd[:, layer],
                                 state_ret_bwd[:, layer], rope, lw)
    return (y_prompt, y_sample, s_f[:, None], s_b[:, None])
```

```python
import functools
import math

import jax
import jax.numpy as jnp
import numpy as np
from jax import lax
from jax.experimental import pallas as pl
from jax.experimental.pallas import tpu as pltpu
from jax.experimental.pallas import tpu_sc as plsc

F32 = jnp.float32
BF16 = jnp.bfloat16

D_MODEL = 1024
GRID_W = 64
N_FOURIER_GROUPS = 8
FOURIER_GROUP_DIM = 128
N_RET_HEADS = 4
RET_HEAD_DIM = 128
RET_WIDTH = N_RET_HEADS * RET_HEAD_DIM
CHUNK = 128
N_EXPERTS = 64
N_EXPERT_GROUPS = 8
EXPERTS_PER_GROUP = N_EXPERTS // N_EXPERT_GROUPS
TOPK_GROUPS = 4
TOP_K = 8
EXPERT_DIM = 256
ROUTED_SCALE = 2.5
ROPE_BASE = 10000.0
EPS = 1e-6
Q_SCALE = RET_HEAD_DIM ** -0.5

_C_UF = (0, 1024)
_C_Q = (1024, 1536)
_C_K = (1536, 2048)
_C_V = (2048, 2560)
_C_G = (2560, 3072)
_C_GF = (3072, 4096)
_C_GR = (4096, 5120)

VMEM_LIMIT = 56 * 1024 * 1024

TM_INPROJ = 1024
TM_ROUTER = 512
FNET_ROWS = 512
TM_FINAL = 1024
EXPERT_TILES_PER_STEP = 4
MAX_EXPERT_ROWS = 512
WEIGHT_SLOTS = 3
ROW_SLABS = 4
SC_CORES = 2
SC_WORKERS = 32
SC_CHUNK = 128
SC_LANES = 16
SC_PACK_BLOCK_WORDS = 16384
SC_COMBINE_TOKENS = 8


def _silu(x):
    return x * jax.nn.sigmoid(x)


def _dot(a, b):
    return jnp.dot(a, b, preferred_element_type=F32)


def _rms_mod(x, g, shift, scale):
    ms = jnp.mean(x * x, axis=-1, keepdims=True)
    y = x * lax.rsqrt(ms + EPS) * g
    return y * (1.0 + scale) + shift


def _ada_kernel(cond_ref, w_ref, b_ref, o_ref):
    s = _silu(cond_ref[...]).astype(BF16)
    o_ref[...] = _dot(s, w_ref[...].astype(BF16)) + b_ref[...]


def _ada(cond, w_ada, b_ada):
    rows, n = cond.shape[0], w_ada.shape[1]
    tn = 1536
    return pl.pallas_call(
        _ada_kernel,
        grid=(n // tn,),
        in_specs=[pl.BlockSpec((rows, D_MODEL), lambda j: (0, 0)),
                  pl.BlockSpec((D_MODEL, tn), lambda j: (0, j)),
                  pl.BlockSpec((1, tn), lambda j: (0, j))],
        out_specs=pl.BlockSpec((rows, tn), lambda j: (0, j)),
        out_shape=jax.ShapeDtypeStruct((rows, n), F32),
        compiler_params=pltpu.CompilerParams(vmem_limit_bytes=VMEM_LIMIT),
        name="ada",
    )(cond, w_ada, b_ada)


def _rope_head(x, cos, sin_signed, first_half):
    partner = jnp.where(first_half, pltpu.roll(x, 96, 1), pltpu.roll(x, 32, 1))
    return x * cos + partner * sin_signed


def _inproj_kernel(*refs, use_rope):
    if use_rope:
        x_ref, mod_ref, g_ref, w_ref, cos_ref, sin_ref = refs[:6]
        outs = refs[6:]
    else:
        x_ref, mod_ref, g_ref, w_ref = refs[:4]
        outs = refs[4:]
    uf_o, q_o, k_o, v_o, sg_o, gf_o, gr_o = outs

    h = _rms_mod(x_ref[...], g_ref[...], mod_ref[0, 0:1, :], mod_ref[0, 1:2, :])
    hb = h.astype(BF16)

    def proj(cols):
        return _dot(hb, w_ref[:, cols[0]:cols[1]].astype(BF16))

    uf_o[...] = proj(_C_UF).astype(BF16)
    q = proj(_C_Q)
    k = proj(_C_K)
    if use_rope:
        cos = cos_ref[...]
        sin_signed = sin_ref[...]
        lane = lax.broadcasted_iota(jnp.int32, cos.shape, 1)
        first_half = (lane & 32) == 0
        for hd in range(N_RET_HEADS):
            sl = slice(hd * RET_HEAD_DIM, (hd + 1) * RET_HEAD_DIM)
            q_o[:, sl] = (_rope_head(q[:, sl], cos, sin_signed, first_half) * Q_SCALE).astype(BF16)
            k_o[:, sl] = _rope_head(k[:, sl], cos, sin_signed, first_half).astype(BF16)
    else:
        q_o[...] = (q * Q_SCALE).astype(BF16)
        k_o[...] = k.astype(BF16)
    v_o[...] = proj(_C_V).astype(BF16)
    sg_o[...] = _silu(proj(_C_G)).astype(BF16)
    gf_o[...] = jax.nn.sigmoid(proj(_C_GF)).astype(BF16)
    gr_o[...] = jax.nn.sigmoid(proj(_C_GR)).astype(BF16)


def _inproj(x2d, mod3, norm_g, w_in_f32, seq_len, mod_row_of_batch, rope):
    t = x2d.shape[0]
    tm = TM_INPROJ
    tiles_per_seq = max(seq_len // tm, 1)

    def mod_idx(i):
        return (mod_row_of_batch((i * tm) // seq_len), 0, 0)

    in_specs = [pl.BlockSpec((tm, D_MODEL), lambda i: (i, 0)),
                pl.BlockSpec((1, 6, D_MODEL), mod_idx),
                pl.BlockSpec((1, D_MODEL), lambda i: (0, 0)),
                pl.BlockSpec(w_in_f32.shape, lambda i: (0, 0), pipeline_mode=pl.Buffered(1))]
    args = [x2d, mod3, norm_g, w_in_f32]
    if rope is not None:
        in_specs += [pl.BlockSpec((tm, RET_HEAD_DIM), lambda i: (i % tiles_per_seq, 0))] * 2
        args += list(rope)
    widths = [1024, RET_WIDTH, RET_WIDTH, RET_WIDTH, RET_WIDTH, 1024, 1024]
    return pl.pallas_call(
        functools.partial(_inproj_kernel, use_rope=rope is not None),
        grid=(t // tm,),
        in_specs=in_specs,
        out_specs=[pl.BlockSpec((tm, w), lambda i: (i, 0)) for w in widths],
        out_shape=[jax.ShapeDtypeStruct((t, w), BF16) for w in widths],
        compiler_params=pltpu.CompilerParams(dimension_semantics=("parallel",),
                                             vmem_limit_bytes=VMEM_LIMIT),
        name="inproj",
    )(*args)


def _retention_kernel(q_ref, k_ref, v_ref, sg_ref, dec_ref, gn_ref, s0f_ref, s0b_ref,
                      r_ref, sfo_ref, sbo_ref, tab_scr, gc_scr):
    n_chunks = q_ref.shape[0] // CHUNK
    hd = RET_HEAD_DIM

    @pl.when(pl.program_id(0) == 0)
    def _():
        row = lax.broadcasted_iota(jnp.int32, (CHUNK, CHUNK), 0).astype(F32)
        col = lax.broadcasted_iota(jnp.int32, (CHUNK, CHUNK), 1).astype(F32)
        diff = row - col
        for h in range(N_RET_HEADS):
            dec = dec_ref[h]
            lg = jnp.minimum(dec, 0.0) - jnp.log1p(jnp.exp(-jnp.abs(dec)))
            lgf = lg[0:1, :]
            lgb = lg[1:2, :]
            tab_scr[h, 0] = jnp.exp(jnp.where(diff >= 0, lgf * diff, lgb * (-diff)))
            tab_scr[h, 1] = jnp.exp(lgf * (row + 1.0))
            tab_scr[h, 2] = jnp.exp(lgb * (CHUNK - row))
            tab_scr[h, 3] = jnp.exp(lgf * (CHUNK - 1.0 - col))
            tab_scr[h, 4] = jnp.exp(lgb * col)
            gc_scr[h] = jnp.exp(lg * CHUNK)

    def rows(n):
        return slice(n * CHUNK, (n + 1) * CHUNK)

    for h in range(N_RET_HEADS):
        cols = slice(h * hd, (h + 1) * hd)
        decay, qw_f, qw_b, kwt_f, kwt_b = (tab_scr[h, i] for i in range(5))
        gc = gc_scr[h]
        gc_f = gc[0:1, :]
        gc_b = gc[1:2, :]

        kv_f, kv_b = [], []
        for n in range(n_chunks):
            kt = k_ref[rows(n), cols].astype(F32).T
            vn = v_ref[rows(n), cols]
            kv_f.append(_dot((kt * kwt_f).astype(BF16), vn))
            kv_b.append(_dot((kt * kwt_b).astype(BF16), vn))

        s = s0f_ref[h]
        prev_f = []
        for n in range(n_chunks):
            prev_f.append(s.astype(BF16))
            s = gc_f * s + kv_f[n]
        sfo_ref[h] = s
        s = s0b_ref[h]
        prev_b = [None] * n_chunks
        for n in reversed(range(n_chunks)):
            prev_b[n] = s.astype(BF16)
            s = gc_b * s + kv_b[n]
        sbo_ref[h] = s

        gn = gn_ref[:, cols]
        for n in range(n_chunks):
            qn = q_ref[rows(n), cols]
            qf = qn.astype(F32)
            scores = lax.dot_general(qn, k_ref[rows(n), cols], (((1,), (1,)), ((), ())),
                                     preferred_element_type=F32)
            o = _dot((scores * decay).astype(BF16), v_ref[rows(n), cols])
            o = o + _dot((qf * qw_f).astype(BF16), prev_f[n])
            o = o + _dot((qf * qw_b).astype(BF16), prev_b[n])
            mu = jnp.mean(o, axis=-1, keepdims=True)
            d = o - mu
            var = jnp.mean(d * d, axis=-1, keepdims=True)
            on = d * lax.rsqrt(var + EPS) * gn
            r_ref[rows(n), cols] = (on * sg_ref[rows(n), cols].astype(F32)).astype(BF16)


def _retention(q, k, v, sg, dec, gn_g, s0f, s0b, batch, seq_len):
    hd = RET_HEAD_DIM
    tok_spec = pl.BlockSpec((seq_len, RET_WIDTH), lambda b: (b, 0))
    st_spec = pl.BlockSpec((None, N_RET_HEADS, hd, hd), lambda b: (b, 0, 0, 0))
    st_shape = jax.ShapeDtypeStruct((batch, N_RET_HEADS, hd, hd), F32)
    return pl.pallas_call(
        _retention_kernel,
        grid=(batch,),
        in_specs=[tok_spec, tok_spec, tok_spec, tok_spec,
                  pl.BlockSpec(dec.shape, lambda b: (0, 0, 0)),
                  pl.BlockSpec(gn_g.shape, lambda b: (0, 0)),
                  st_spec, st_spec],
        out_specs=[tok_spec, st_spec, st_spec],
        out_shape=[jax.ShapeDtypeStruct((batch * seq_len, RET_WIDTH), BF16), st_shape, st_shape],
        scratch_shapes=[pltpu.VMEM((N_RET_HEADS, 5, CHUNK, CHUNK), F32),
                        pltpu.VMEM((N_RET_HEADS, 2, hd), F32)],
        compiler_params=pltpu.CompilerParams(dimension_semantics=("arbitrary",),
                                             vmem_limit_bytes=VMEM_LIMIT),
        name="retention",
    )(q, k, v, sg, dec, gn_g, s0f, s0b)


def _fnet_merge_kernel(uf_ref, cs_ref, cls_ref, r_ref, gf_ref, gr_ref, x_ref, mod_ref, wf_ref, wr_ref, wo_ref,
                       o_ref, xcs_ref):
    seq_len = uf_ref.shape[0]
    gd = FOURIER_GROUP_DIM

    @pl.when(pl.program_id(1) == 0)
    def _():
        for g in range(N_FOURIER_GROUPS):
            x = _dot(uf_ref[:, g * gd:(g + 1) * gd], cs_ref[...])
            xcs_ref[0:seq_len, g * gd:(g + 1) * gd] = x[:, :gd].astype(BF16)
            xcs_ref[seq_len:2 * seq_len, g * gd:(g + 1) * gd] = x[:, gd:].astype(BF16)

    f_mix = _dot(cls_ref[...], xcs_ref[...]).astype(BF16)
    f_out = _dot(f_mix, wf_ref[...].astype(BF16))
    r_out = _dot(r_ref[...], wr_ref[...].astype(BF16))
    merged = gf_ref[...].astype(F32) * f_out + gr_ref[...].astype(F32) * r_out
    mix = _dot(merged.astype(BF16), wo_ref[...].astype(BF16))
    o_ref[...] = x_ref[...] + mod_ref[0, 2:3, :] * mix


def _fnet_merge(uf, cs, cls, r, gf, gr, x2d, mod3, w_four, w_ret, w_o, batch, seq_len, mod_row_of_batch):
    rb = min(FNET_ROWS, seq_len)
    nr = seq_len // rb

    def tok(w):
        return pl.BlockSpec((rb, w), lambda b, i: (b * nr + i, 0))

    def full(a):
        return pl.BlockSpec(a.shape, lambda b, i: (0, 0))

    def once(a):
        return pl.BlockSpec(a.shape, lambda b, i: (0, 0), pipeline_mode=pl.Buffered(1))

    return pl.pallas_call(
        _fnet_merge_kernel,
        grid=(batch, nr),
        in_specs=[pl.BlockSpec((seq_len, D_MODEL), lambda b, i: (b, 0)),
                  full(cs),
                  pl.BlockSpec((rb, 2 * seq_len), lambda b, i: (i, 0)),
                  tok(RET_WIDTH), tok(D_MODEL), tok(D_MODEL), tok(D_MODEL),
                  pl.BlockSpec((1, 6, D_MODEL), lambda b, i: (mod_row_of_batch(b), 0, 0)),
                  once(w_four), once(w_ret), once(w_o)],
        out_specs=tok(D_MODEL),
        out_shape=jax.ShapeDtypeStruct((batch * seq_len, D_MODEL), F32),
        scratch_shapes=[pltpu.VMEM((2 * seq_len, D_MODEL), BF16)],
        compiler_params=pltpu.CompilerParams(dimension_semantics=("parallel", "arbitrary"),
                                             vmem_limit_bytes=VMEM_LIMIT),
        name="fnet_merge",
    )(uf, cs, cls, r, gf, gr, x2d, mod3, w_four, w_ret, w_o)


def _pack_pair(lo_f32, hi_f32):
    lo = lax.bitcast_convert_type(lo_f32.astype(BF16).astype(F32), jnp.uint32)
    hi = lax.bitcast_convert_type(hi_f32.astype(BF16).astype(F32), jnp.uint32)
    return lax.bitcast_convert_type((lo >> 16) | hi, jnp.int32)


def _unpack_pair(words_i32):
    w = lax.bitcast_convert_type(words_i32, jnp.uint32)
    lo = lax.bitcast_convert_type(w << 16, F32)
    hi = lax.bitcast_convert_type(w & jnp.uint32(0xFFFF0000), F32)
    return lo, hi


def _load_token_words(ref, lead, n_tok):
    parts = []
    for s in range(ROW_SLABS):
        idx = (pl.ds(s, n_tok, stride=ROW_SLABS), slice(None))
        parts.append(ref[lead + idx] if lead else ref[idx])
    return jnp.concatenate(parts, axis=1)


def _store_token_words(ref, words, n_tok):
    for s in range(ROW_SLABS):
        ref[pl.ds(s, n_tok, stride=ROW_SLABS), :] = words[:, s * 128:(s + 1) * 128]


def _route(scores, biased):
    tokens = scores.shape[1]
    neg = -jnp.inf
    epg = EXPERTS_PER_GROUP
    iota_g = lax.broadcasted_iota(jnp.int32, (epg, tokens), 0).astype(F32)

    def pick_first_max(cur, iota, size):
        m = jnp.max(cur, axis=0, keepdims=True)
        idx = jnp.min(jnp.where(cur == m, iota, float(size)), axis=0, keepdims=True)
        return m, idx, iota == idx

    group_scores = []
    for g in range(N_EXPERT_GROUPS):
        vals = biased[g * epg:(g + 1) * epg, :]
        m1, _, hit = pick_first_max(vals, iota_g, epg)
        m2 = jnp.max(jnp.where(hit, neg, vals), axis=0, keepdims=True)
        group_scores.append(m1 + m2)
    cur = jnp.concatenate(group_scores, axis=0)
    group_sel = jnp.zeros_like(cur)
    for _ in range(TOPK_GROUPS):
        _, _, hit = pick_first_max(cur, iota_g, N_EXPERT_GROUPS)
        group_sel = jnp.where(hit, 1.0, group_sel)
        cur = jnp.where(hit, neg, cur)
    masked = jnp.concatenate(
        [jnp.where(group_sel[g:g + 1, :] > 0.0, biased[g * epg:(g + 1) * epg, :], neg)
         for g in range(N_EXPERT_GROUPS)], axis=0)
    iota_e = lax.broadcasted_iota(jnp.int32, masked.shape, 0).astype(F32)
    sel = jnp.zeros_like(masked)
    cur = masked
    picks = []
    for _ in range(TOP_K):
        _, idx, hit = pick_first_max(cur, iota_e, N_EXPERTS)
        picks.append(idx)
        sel = jnp.where(hit, 1.0, sel)
        cur = jnp.where(hit, neg, cur)
    w = scores * sel
    return w / jnp.sum(w, axis=0, keepdims=True) * ROUTED_SCALE, sel, picks


def _router_kernel(x_ref, mod_ref, g2_ref, wrt_ref, rb_ref, hp_ref, ek_ref, rk_ref, wt_ref, cnt_ref,
                   run_scr, earlier_scr):
    tm = x_ref.shape[0]

    @pl.when(pl.program_id(0) == 0)
    def _():
        run_scr[...] = jnp.zeros_like(run_scr)
        earlier = (lax.broadcasted_iota(jnp.int32, (tm, tm), 0) < lax.broadcasted_iota(jnp.int32, (tm, tm), 1))
        earlier_scr[...] = jnp.where(earlier, 1.0, 0.0).astype(BF16)

    h = _rms_mod(x_ref[...], g2_ref[...], mod_ref[0, 3:4, :], mod_ref[0, 4:5, :])
    half = D_MODEL // 2
    _store_token_words(hp_ref, _pack_pair(h[:, :half], h[:, half:]), tm)

    def split(a):
        hi = a.astype(BF16)
        return hi, (a - hi.astype(F32)).astype(BF16)

    def dot_nt(a, b):
        return lax.dot_general(a, b, (((1,), (1,)), ((), ())), preferred_element_type=F32)

    h_hi, h_lo = split(h)
    w_hi, w_lo = split(wrt_ref[...])
    logits_t = dot_nt(w_hi, h_hi) + (dot_nt(w_hi, h_lo) + dot_nt(w_lo, h_hi))
    scores = jax.nn.sigmoid(logits_t)
    comb_t, sel, picks = _route(scores, scores + rb_ref[...])

    rank_t = _dot(sel.astype(BF16), earlier_scr[...]) + run_scr[...]
    run_scr[...] += jnp.sum(sel, axis=1, keepdims=True)
    cnt_ref[...] = jnp.broadcast_to(run_scr[...], cnt_ref.shape)

    iota_e = lax.broadcasted_iota(jnp.int32, sel.shape, 0).astype(F32)
    ranks, weights = [], []
    for idx in picks:
        hit = iota_e == idx
        ranks.append(jnp.sum(jnp.where(hit, rank_t, 0.0), axis=0, keepdims=True))
        weights.append(jnp.sum(jnp.where(hit, comb_t, 0.0), axis=0, keepdims=True))
    ek_ref[...] = jnp.concatenate(picks, axis=0).astype(jnp.int32)
    rk_ref[...] = jnp.concatenate(ranks, axis=0).astype(jnp.int32)
    w_rep = jnp.concatenate([jnp.broadcast_to(w, (SC_LANES, tm)) for w in weights], axis=0)
    wt_ref[...] = w_rep.T


def _router(x1, mod3, norm2_g, w_router_t, router_bias, seq_len, mod_row_of_batch):
    t = x1.shape[0]
    tm = TM_ROUTER

    def mod_idx(i):
        return (mod_row_of_batch((i * tm) // seq_len), 0, 0)

    def full(a):
        return pl.BlockSpec(a.shape, lambda i: (0,) * a.ndim)

    return pl.pallas_call(
        _router_kernel,
        grid=(t // tm,),
        in_specs=[pl.BlockSpec((tm, D_MODEL), lambda i: (i, 0)),
                  pl.BlockSpec((1, 6, D_MODEL), mod_idx),
                  full(norm2_g), full(w_router_t), full(router_bias)],
        out_specs=[pl.BlockSpec((tm * ROW_SLABS, 128), lambda i: (i, 0)),
                   pl.BlockSpec((TOP_K, tm), lambda i: (0, i)),
                   pl.BlockSpec((TOP_K, tm), lambda i: (0, i)),
                   pl.BlockSpec((tm, 128), lambda i: (i, 0)),
                   pl.BlockSpec((N_EXPERTS, 128), lambda i: (0, 0))],
        out_shape=[jax.ShapeDtypeStruct((t * ROW_SLABS, 128), jnp.int32),
                   jax.ShapeDtypeStruct((TOP_K, t), jnp.int32),
                   jax.ShapeDtypeStruct((TOP_K, t), jnp.int32),
                   jax.ShapeDtypeStruct((t, 128), F32),
                   jax.ShapeDtypeStruct((N_EXPERTS, 128), F32)],
        scratch_shapes=[pltpu.VMEM((N_EXPERTS, 1), F32), pltpu.VMEM((tm, tm), BF16)],
        compiler_params=pltpu.CompilerParams(dimension_semantics=("arbitrary",),
                                             vmem_limit_bytes=VMEM_LIMIT),
        name="router",
    )(x1, mod3, norm2_g, w_router_t, router_bias)


def _plan_kernel(ek_ref, rk_ref, cnt_ref, pos_ref, texp_ref, nused_ref, tend_ref, *, expert_rows):
    rows = float(expert_rows)
    cnt = cnt_ref[:, 0:1]
    tiles = jnp.floor((cnt + (rows - 1.0)) / rows)
    before = (lax.broadcasted_iota(jnp.int32, (N_EXPERTS, N_EXPERTS), 1)
              < lax.broadcasted_iota(jnp.int32, (N_EXPERTS, N_EXPERTS), 0))
    tile_start = jnp.dot(jnp.where(before, 1.0, 0.0), jnp.broadcast_to(tiles, (N_EXPERTS, 128)),
                         precision=lax.Precision.HIGHEST, preferred_element_type=F32)[:, 0:1]
    tile_end = tile_start + tiles
    row_start = tile_start * rows

    ek = ek_ref[...]
    pos = rk_ref[...].astype(F32)
    tile_id = lax.broadcasted_iota(jnp.int32, texp_ref.shape, 1).astype(F32)
    texp = jnp.zeros(texp_ref.shape, F32)
    for e in range(N_EXPERTS):
        pos = pos + jnp.where(ek == e, row_start[e:e + 1, :], 0.0)
        texp = texp + jnp.where(tile_id >= tile_end[e:e + 1, :], 1.0, 0.0)
    pos_ref[...] = pos.astype(jnp.int32)
    texp_ref[...] = jnp.minimum(texp, N_EXPERTS - 1.0).astype(jnp.int32)
    nused_ref[...] = jnp.broadcast_to(tile_end[N_EXPERTS - 1:N_EXPERTS, :], nused_ref.shape).astype(jnp.int32)
    tend_ref[...] = jnp.broadcast_to(tile_end, tend_ref.shape).astype(jnp.int32)


def _plan(ek, rk, cnt, n_tiles_pad, expert_rows):
    t = ek.shape[1]

    def full(shape):
        return pl.BlockSpec(shape, lambda: (0,) * len(shape))

    return pl.pallas_call(
        functools.partial(_plan_kernel, expert_rows=expert_rows),
        in_specs=[full(ek.shape), full(rk.shape), full(cnt.shape)],
        out_specs=[full((TOP_K, t)), full((1, n_tiles_pad)), full((1, 128)), full((N_EXPERTS, 128))],
        out_shape=[jax.ShapeDtypeStruct((TOP_K, t), jnp.int32),
                   jax.ShapeDtypeStruct((1, n_tiles_pad), jnp.int32),
                   jax.ShapeDtypeStruct((1, 128), jnp.int32),
                   jax.ShapeDtypeStruct((N_EXPERTS, 128), jnp.int32)],
        compiler_params=pltpu.CompilerParams(vmem_limit_bytes=VMEM_LIMIT),
        name="plan",
    )(ek, rk, cnt)


def _sc_mesh():
    return plsc.VectorSubcoreMesh(core_axis_name="c", subcore_axis_name="s")


def _sc_pack_weight_halves(w):
    e, k, n = w.shape
    k_half = k // 2
    rb = SC_PACK_BLOCK_WORDS // n
    units_per_expert = k_half // rb
    per_w = (e * units_per_expert) // SC_WORKERS
    lanes = SC_LANES

    @functools.partial(
        pl.kernel, out_type=jax.ShapeDtypeStruct((e * k_half, n), jnp.int32), mesh=_sc_mesh(),
        scratch_types=[pltpu.VMEM((rb, n), F32), pltpu.VMEM((rb, n), F32), pltpu.VMEM((rb, n), jnp.int32)],
        compiler_params=pltpu.CompilerParams(needs_layout_passes=False))
    def kern(w_hbm, out_hbm, a_v, b_v, o_v):
        wid = lax.axis_index("s") * SC_CORES + lax.axis_index("c")

        @pl.loop(0, per_w)
        def _(j):
            unit = wid * per_w + j
            expert = unit // units_per_expert
            blk = unit % units_per_expert
            row_a = expert * k + blk * rb
            pltpu.sync_copy(w_hbm.at[pl.ds(row_a, rb)], a_v)
            pltpu.sync_copy(w_hbm.at[pl.ds(row_a + k_half, rb)], b_v)

            @pl.loop(0, rb)
            def _(r):
                @plsc.parallel_loop(0, n, step=lanes, unroll=4)
                def _(c):
                    both = plsc.pack(a_v[r, pl.ds(c, lanes)], b_v[r, pl.ds(c, lanes)],
                                     format=plsc.PackFormat.INTERLEAVED)
                    o_v[r, pl.ds(c, lanes)] = plsc.bitcast(both, jnp.int32)

            pltpu.sync_copy(o_v, out_hbm.at[pl.ds(expert * k_half + blk * rb, rb)])

    return kern(w.reshape(e * k, n)).reshape(e, k_half, n)


def _sc_dispatch(rows, pos3, n_out, after=()):
    t = rows.shape[0]
    ch = SC_CHUNK
    per_w = (t // ch) // SC_WORKERS

    @functools.partial(
        pl.kernel, out_type=jax.ShapeDtypeStruct((n_out,) + rows.shape[1:], jnp.int32), mesh=_sc_mesh(),
        scratch_types=[pltpu.VMEM((TOP_K, ch), jnp.int32), pltpu.VMEM((ch,) + rows.shape[1:], jnp.int32),
                       pltpu.SemaphoreType.DMA])
    def k(rows_hbm, pos_hbm, *rest):
        out_hbm, idx_v, rows_v, sem = rest[len(after):]
        wid = lax.axis_index("s") * SC_CORES + lax.axis_index("c")

        @pl.loop(0, per_w)
        def _(j):
            c = wid * per_w + j
            pltpu.sync_copy(pos_hbm.at[c], idx_v)
            pltpu.sync_copy(rows_hbm.at[pl.ds(c * ch, ch)], rows_v)
            copies = [pltpu.async_copy(rows_v, out_hbm.at[idx_v.at[kk]], sem) for kk in range(TOP_K)]
            for cp in copies:
                cp.wait()

    return k(rows, pos3, *after)


def _sc_combine(table, pos3, wtok, t):
    ch = SC_CHUNK
    sub = SC_COMBINE_TOKENS
    lanes = SC_LANES
    slabs = ROW_SLABS
    per_w = (t // ch) // SC_WORKERS
    subs_per_chunk = ch // sub
    n_steps = per_w * subs_per_chunk

    @functools.partial(
        pl.kernel, out_type=jax.ShapeDtypeStruct((t, slabs, 128), jnp.int32), mesh=_sc_mesh(),
        scratch_types=[pltpu.VMEM((per_w, TOP_K, ch), jnp.int32),
                       pltpu.VMEM((2, TOP_K, sub, slabs, 128), jnp.int32),
                       pltpu.VMEM((2, sub, 128), F32),
                       pltpu.VMEM((sub, slabs, 128), jnp.int32),
                       pltpu.SemaphoreType.DMA((2,))],
        compiler_params=pltpu.CompilerParams(needs_layout_passes=False))
    def k(tab_hbm, pos_hbm, w_hbm, out_hbm, idx_v, rows_v, w_v, out_v, sem):
        wid = lax.axis_index("s") * SC_CORES + lax.axis_index("c")
        for j in range(per_w):
            pltpu.sync_copy(pos_hbm.at[wid * per_w + j], idx_v.at[j])

        def first_token(step):
            return (wid * per_w + step // subs_per_chunk) * ch + (step % subs_per_chunk) * sub

        def copies(step, slot):
            j = step // subs_per_chunk
            s = step % subs_per_chunk
            idx = [idx_v.at[j, kk, pl.ds(s * sub, sub)] for kk in range(TOP_K)]
            return ([pltpu.make_async_copy(tab_hbm.at[idx[kk]], rows_v.at[slot, kk], sem.at[slot])
                     for kk in range(TOP_K)]
                    + [pltpu.make_async_copy(w_hbm.at[pl.ds(first_token(step), sub)], w_v.at[slot], sem.at[slot])])

        for cp in copies(0, 0):
            cp.start()

        @pl.loop(0, n_steps)
        def _(step):
            slot = step % 2

            @pl.when(step + 1 < n_steps)
            def _():
                for cp in copies(step + 1, 1 - slot):
                    cp.start()

            for cp in copies(step, slot):
                cp.wait()

            @pl.loop(0, sub)
            def _(tt):
                wk = [w_v[slot, tt, pl.ds(kk * lanes, lanes)] for kk in range(TOP_K)]
                for sl in range(slabs):
                    @plsc.parallel_loop(0, 128, step=lanes, unroll=4)
                    def _(off):
                        acc_lo = jnp.zeros((lanes,), F32)
                        acc_hi = jnp.zeros((lanes,), F32)
                        for kk in range(TOP_K):
                            word = rows_v[slot, kk, tt, sl, pl.ds(off, lanes)]
                            lo = plsc.bitcast(word << 16, F32)
                            hi = plsc.bitcast(word & jnp.int32(-65536), F32)
                            acc_lo = acc_lo + wk[kk] * lo
                            acc_hi = acc_hi + wk[kk] * hi
                        both = plsc.pack(acc_lo, acc_hi, format=plsc.PackFormat.INTERLEAVED)
                        out_v[tt, sl, pl.ds(off, lanes)] = plsc.bitcast(both, jnp.int32)

            pltpu.sync_copy(out_v, out_hbm.at[pl.ds(first_token(step), sub)])

    return k(table, pos3, wtok)


def _experts_kernel(texp_ref, nused_ref, tend_ref, xs_ref, weg_hbm, weu_hbm, wed_hbm, ys_ref,
                    wg_scr, wu_scr, wd_scr, wg_buf, wu_buf, wd_buf, sem, group_scr, *, expert_rows):
    step = pl.program_id(0)
    rows = expert_rows
    tiles_per_step = EXPERT_TILES_PER_STEP
    half = D_MODEL // 2
    n_used = nused_ref[0]

    def weight_copies(e, slot):
        return [pltpu.make_async_copy(weg_hbm.at[e], wg_buf.at[slot], sem.at[slot, 0]),
                pltpu.make_async_copy(weu_hbm.at[e], wu_buf.at[slot], sem.at[slot, 1]),
                pltpu.make_async_copy(wed_hbm.at[e], wd_buf.at[slot], sem.at[slot, 2])]

    def next_group(e):
        tile = tend_ref[e]
        return texp_ref[jnp.minimum(tile, n_used - 1)], tile < n_used

    def start_weights(e, slot, exists):
        @pl.when(exists)
        def _():
            for cp in weight_copies(e, slot):
                cp.start()

    @pl.when(step == 0)
    def _():
        group_scr[0] = 0
        e, exists = texp_ref[0], True
        for slot in range(WEIGHT_SLOTS - 1):
            start_weights(e, slot, exists)
            nxt, has_next = next_group(e)
            e, exists = nxt, exists & has_next

    def row_tile(tile, x_view, y_view):
        expert = texp_ref[tile]
        used = tile < n_used
        new_expert = (tile == 0) | (expert != texp_ref[jnp.maximum(tile - 1, 0)])

        @pl.when(used & new_expert)
        def _():
            group = group_scr[0]
            slot = group % WEIGHT_SLOTS
            ahead, exists = expert, True
            for _ in range(WEIGHT_SLOTS - 1):
                nxt, has_next = next_group(ahead)
                ahead, exists = nxt, exists & has_next
            start_weights(ahead, (group + WEIGHT_SLOTS - 1) % WEIGHT_SLOTS, exists)

            for cp in weight_copies(expert, slot):
                cp.wait()
            for scr, buf in ((wg_scr, wg_buf), (wu_scr, wu_buf), (wd_scr, wd_buf)):
                top, bottom = _unpack_pair(buf[slot])
                k_half = top.shape[0]
                scr[0:k_half, :] = top.astype(BF16)
                scr[k_half:2 * k_half, :] = bottom.astype(BF16)
            group_scr[0] = group + 1

        @pl.when(used)
        def _():
            lo, hi = _unpack_pair(_load_token_words(x_view, (), rows))
            lo = lo.astype(BF16)
            hi = hi.astype(BF16)
            g = _dot(lo, wg_scr[0:half, :]) + _dot(hi, wg_scr[half:D_MODEL, :])
            u = _dot(lo, wu_scr[0:half, :]) + _dot(hi, wu_scr[half:D_MODEL, :])
            y = _dot((_silu(g) * u).astype(BF16), wd_scr[...])
            _store_token_words(y_view, _pack_pair(y[:, :half], y[:, half:]), rows)

        @pl.when(jnp.logical_not(used) & (step == (n_used - 1) // tiles_per_step))
        def _():
            y_view[...] = jnp.zeros_like(y_view)

    for s in range(tiles_per_step):
        view = pl.ds(s * rows * ROW_SLABS, rows * ROW_SLABS)
        row_tile(step * tiles_per_step + s, xs_ref.at[view], ys_ref.at[view])


def _experts(texp, nused, tend, xs2d, weg, weu, wed, n_tiles, expert_rows):
    tiles_per_step = EXPERT_TILES_PER_STEP
    block = (tiles_per_step * expert_rows * ROW_SLABS, 128)
    hbm = pl.BlockSpec(memory_space=pl.ANY)

    def block_idx(j, te, nu, tn):
        return (jnp.minimum(j, (nu[0] - 1) // tiles_per_step), 0)

    grid_spec = pltpu.PrefetchScalarGridSpec(
        num_scalar_prefetch=3,
        grid=(n_tiles // tiles_per_step,),
        in_specs=[pl.BlockSpec(block, block_idx), hbm, hbm, hbm],
        out_specs=pl.BlockSpec(block, block_idx),
        scratch_shapes=[pltpu.VMEM((D_MODEL, EXPERT_DIM), BF16),
                        pltpu.VMEM((D_MODEL, EXPERT_DIM), BF16),
                        pltpu.VMEM((EXPERT_DIM, D_MODEL), BF16),
                        pltpu.VMEM((WEIGHT_SLOTS,) + weg.shape[1:], jnp.int32),
                        pltpu.VMEM((WEIGHT_SLOTS,) + weu.shape[1:], jnp.int32),
                        pltpu.VMEM((WEIGHT_SLOTS,) + wed.shape[1:], jnp.int32),
                        pltpu.SemaphoreType.DMA((WEIGHT_SLOTS, 3)),
                        pltpu.SMEM((1,), jnp.int32)],
    )
    return pl.pallas_call(
        functools.partial(_experts_kernel, expert_rows=expert_rows),
        grid_spec=grid_spec,
        out_shape=jax.ShapeDtypeStruct(xs2d.shape, jnp.int32),
        compiler_params=pltpu.CompilerParams(dimension_semantics=("arbitrary",),
                                             vmem_limit_bytes=VMEM_LIMIT),
        name="experts",
    )(texp, nused, tend, xs2d, weg, weu, wed)


def _final_kernel(x_ref, routed_ref, mod_ref, g2_ref, wsg_ref, wsu_ref, wsd_ref, fng_ref, o_ref):
    tm = x_ref.shape[0]
    x = x_ref[...]
    hb = _rms_mod(x, g2_ref[...], mod_ref[0, 3:4, :], mod_ref[0, 4:5, :]).astype(BF16)
    shared = _dot((_silu(_dot(hb, wsg_ref[...])) * _dot(hb, wsu_ref[...])).astype(BF16), wsd_ref[...])
    routed = jnp.concatenate(_unpack_pair(_load_token_words(routed_ref, (), tm)), axis=1)
    y = x + mod_ref[0, 5:6, :] * (routed + shared)
    ms = jnp.mean(y * y, axis=-1, keepdims=True)
    o_ref[...] = y * lax.rsqrt(ms + EPS) * fng_ref[...]


def _final(x1, routed2d, mod3, norm2_g, wsg, wsu, wsd, final_g, seq_len, mod_row_of_batch):
    t = x1.shape[0]
    tm = TM_FINAL

    def mod_idx(i):
        return (mod_row_of_batch((i * tm) // seq_len), 0, 0)

    def full(a):
        return pl.BlockSpec(a.shape, lambda i: (0,) * a.ndim)

    return pl.pallas_call(
        _final_kernel,
        grid=(t // tm,),
        in_specs=[pl.BlockSpec((tm, D_MODEL), lambda i: (i, 0)),
                  pl.BlockSpec((tm * ROW_SLABS, 128), lambda i: (i, 0)),
                  pl.BlockSpec((1, 6, D_MODEL), mod_idx),
                  full(norm2_g), full(wsg), full(wsu), full(wsd), full(final_g)],
        out_specs=pl.BlockSpec((tm, D_MODEL), lambda i: (i, 0)),
        out_shape=jax.ShapeDtypeStruct((t, D_MODEL), F32),
        compiler_params=pltpu.CompilerParams(dimension_semantics=("parallel",),
                                             vmem_limit_bytes=VMEM_LIMIT),
        name="final",
    )(x1, routed2d, mod3, norm2_g, wsg, wsu, wsd, final_g)


def _moe(x1, mod3, lw, seq_len, mod_row_of_batch):
    t = x1.shape[0]
    expert_rows = min(MAX_EXPERT_ROWS, TOP_K * t // N_EXPERTS // 2)
    n_tiles = TOP_K * t // expert_rows + N_EXPERTS
    n_tiles_pad = -(-n_tiles // 128) * 128
    hp2d, ek, rk, wtok, cnt = _router(x1, mod3, lw["norm2_g"], lw["w_router_t"], lw["router_bias"],
                                      seq_len, mod_row_of_batch)
    pos, texp, nused, tend = _plan(ek, rk, cnt, n_tiles_pad, expert_rows)
    pos3 = pos.reshape(TOP_K, t // SC_CHUNK, SC_CHUNK).transpose(1, 0, 2)
    xs = _sc_dispatch(hp2d.reshape(t, ROW_SLABS, 128), pos3, n_tiles * expert_rows,
                      after=(lw["weg"], lw["weu"], lw["wed"]))
    ys2d = _experts(texp.reshape(-1), nused.reshape(-1), tend[:, 0], xs.reshape(-1, 128),
                    lw["weg"], lw["weu"], lw["wed"], n_tiles, expert_rows)
    routed = _sc_combine(ys2d.reshape(-1, ROW_SLABS, 128), pos3, wtok, t)
    return _final(x1, routed.reshape(t * ROW_SLABS, 128), mod3, lw["norm2_g"],
                  lw["wsg"], lw["wsu"], lw["wsd"], lw["final_g"], seq_len, mod_row_of_batch)


def _dft_tables(seq_len):
    gd = FOURIER_GROUP_DIM
    kc = np.arange(gd)
    ang_c = ((kc[:, None] * kc[None, :]) % gd) * (2.0 * math.pi / gd)
    cs = np.concatenate([np.cos(ang_c), np.sin(ang_c)], axis=1) * (gd ** -0.5)
    kl = np.arange(seq_len)
    ang_l = ((kl[:, None] * kl[None, :]) % seq_len) * (2.0 * math.pi / seq_len)
    cls = np.concatenate([np.cos(ang_l), -np.sin(ang_l)], axis=1) * (seq_len ** -0.5)
    return jnp.asarray(cs.astype(np.float32), dtype=BF16), jnp.asarray(cls.astype(np.float32), dtype=BF16)


def _rope_tables(length):
    rows = length // GRID_W
    r = np.repeat(np.arange(rows, dtype=np.float32), GRID_W)
    col = np.tile(np.arange(GRID_W, dtype=np.float32), rows)
    nf = RET_HEAD_DIM // 4
    inv = (np.float32(ROPE_BASE) ** (-np.arange(nf, dtype=np.float32) / np.float32(nf))).astype(np.float32)
    ar = r[:, None] * inv[None]
    ac = col[:, None] * inv[None]
    ang = np.concatenate([ar, ar, ac, ac], axis=-1).astype(np.float64)
    sign = np.where((np.arange(RET_HEAD_DIM) & nf) == 0, -1.0, 1.0)
    return (jnp.asarray(np.cos(ang).astype(np.float32)),
            jnp.asarray((np.sin(ang) * sign[None, :]).astype(np.float32)))


def _trunk_path(x, mod3, mod_row_of_batch, s0f, s0b, rope, lw):
    batch, seq_len, _ = x.shape
    x2d = x.reshape(batch * seq_len, D_MODEL)
    uf, q, k, v, sg, gf, gr = _inproj(x2d, mod3, lw["norm1_g"], lw["w_in"], seq_len, mod_row_of_batch, rope)
    r, s_f, s_b = _retention(q, k, v, sg, lw["dec"], lw["gn_g"], s0f, s0b, batch, seq_len)
    cs, cls = _dft_tables(seq_len)
    x1 = _fnet_merge(uf, cs, cls, r, gf, gr, x2d, mod3, lw["w_four"], lw["w_ret"], lw["w_o"],
                     batch, seq_len, mod_row_of_batch)
    y = _moe(x1, mod3, lw, seq_len, mod_row_of_batch)
    return y.reshape(batch, seq_len, D_MODEL), s_f, s_b


def kernel(x_prompt, x_sample, state_ret_fwd, state_ret_bwd, c, c_ctx, w_ada, b_ada, norm1_g, norm2_g, w_in,
           ret_decay_fwd, ret_decay_bwd, ret_gn_g, w_four_out, w_ret_out, w_out, w_router, router_bias,
           w_exp_gate, w_exp_up, w_exp_down, w_shared_gate, w_shared_up, w_shared_down, final_norm_g):
    depth = w_ada.shape[0]
    assert depth == 1, "final norm is fused into the last layer's MoE kernel"
    n_ctx, n_lat = x_prompt.shape[0], x_sample.shape[0]
    cond = jnp.concatenate([c_ctx[None, :], c], axis=0)
    cond = jnp.pad(cond, ((0, (-cond.shape[0]) % 8), (0, 0)))
    rope = _rope_tables(x_sample.shape[1])
    zeros = jnp.zeros((n_ctx, N_RET_HEADS, RET_HEAD_DIM, RET_HEAD_DIM), F32)

    layer = 0
    mod = _ada(cond, w_ada[layer], b_ada[layer][None, :])
    mod3 = mod.reshape(mod.shape[0], 6, D_MODEL)
    dec = jnp.stack([ret_decay_fwd[layer], ret_decay_bwd[layer]], axis=1)
    lw = {
        "norm1_g": norm1_g[layer][None, :],
        "norm2_g": norm2_g[layer][None, :],
        "w_in": w_in[layer],
        "dec": jnp.broadcast_to(dec[:, :, None], (N_RET_HEADS, 2, RET_HEAD_DIM)).astype(F32),
        "gn_g": ret_gn_g[layer][None, :],
        "w_four": w_four_out[layer],
        "w_ret": w_ret_out[layer],
        "w_o": w_out[layer],
        "w_router_t": w_router[layer].T,
        "router_bias": router_bias[layer][:, None],
        "weg": _sc_pack_weight_halves(w_exp_gate[layer]),
        "weu": _sc_pack_weight_halves(w_exp_up[layer]),
        "wed": _sc_pack_weight_halves(w_exp_down[layer]),
        "wsg": w_shared_gate[layer].astype(BF16),
        "wsu": w_shared_up[layer].astype(BF16),
        "wsd": w_shared_down[layer].astype(BF16),
        "final_g": final_norm_g[None, :],
    }
    y_prompt, s_f, s_b = _trunk_path(x_prompt, mod3, lambda b: 0, zeros, zeros, None, lw)
    y_sample, _, _ = _trunk_path(x_sample, mod3, lambda b: 1 + b, state_ret_fwd[:, layer],
                                 state_ret_bwd[:, layer], rope, lw)
    return (y_prompt, y_sample, s_f[:, None], s_b[:, None])
```

```python
import functools
import math

import jax
import jax.numpy as jnp
import numpy as np
from jax import lax
from jax.experimental import pallas as pl
from jax.experimental.pallas import tpu as pltpu
from jax.experimental.pallas import tpu_sc as plsc

F32 = jnp.float32
BF16 = jnp.bfloat16

D_MODEL = 1024
GRID_W = 64
N_FOURIER_GROUPS = 8
FOURIER_GROUP_DIM = 128
N_RET_HEADS = 4
RET_HEAD_DIM = 128
RET_WIDTH = N_RET_HEADS * RET_HEAD_DIM
CHUNK = 128
N_EXPERTS = 64
N_EXPERT_GROUPS = 8
EXPERTS_PER_GROUP = N_EXPERTS // N_EXPERT_GROUPS
TOPK_GROUPS = 4
TOP_K = 8
EXPERT_DIM = 256
ROUTED_SCALE = 2.5
ROPE_BASE = 10000.0
EPS = 1e-6
Q_SCALE = RET_HEAD_DIM ** -0.5

_C_UF = (0, 1024)
_C_Q = (1024, 1536)
_C_K = (1536, 2048)
_C_V = (2048, 2560)
_C_G = (2560, 3072)
_C_GF = (3072, 4096)
_C_GR = (4096, 5120)

VMEM_LIMIT = 56 * 1024 * 1024

TM_INPROJ = 1024
TM_ROUTER = 512
FNET_ROWS = 512
TM_FINAL = 1024
EXPERT_TILES_PER_STEP = 4
MAX_EXPERT_ROWS = 512
WEIGHT_SLOTS = 3
ROW_SLABS = 4
SC_CORES = 2
SC_WORKERS = 32
SC_CHUNK = 128
SC_LANES = 16
SC_PACK_BLOCK_WORDS = 16384
SC_COMBINE_TOKENS = 8


def _silu(x):
    return x * jax.nn.sigmoid(x)


def _dot(a, b):
    return jnp.dot(a, b, preferred_element_type=F32)


def _rms_mod(x, g, shift, scale):
    ms = jnp.mean(x * x, axis=-1, keepdims=True)
    y = x * lax.rsqrt(ms + EPS) * g
    return y * (1.0 + scale) + shift


def _ada_kernel(cond_ref, w_ref, b_ref, o_ref):
    s = _silu(cond_ref[...]).astype(BF16)
    o_ref[...] = _dot(s, w_ref[...].astype(BF16)) + b_ref[...]


def _ada(cond, w_ada, b_ada):
    rows, n = cond.shape[0], w_ada.shape[1]
    tn = 1536
    return pl.pallas_call(
        _ada_kernel,
        grid=(n // tn,),
        in_specs=[pl.BlockSpec((rows, D_MODEL), lambda j: (0, 0)),
                  pl.BlockSpec((D_MODEL, tn), lambda j: (0, j)),
                  pl.BlockSpec((1, tn), lambda j: (0, j))],
        out_specs=pl.BlockSpec((rows, tn), lambda j: (0, j)),
        out_shape=jax.ShapeDtypeStruct((rows, n), F32),
        compiler_params=pltpu.CompilerParams(vmem_limit_bytes=VMEM_LIMIT),
        name="ada",
    )(cond, w_ada, b_ada)


def _rope_head(x, cos, sin_signed, first_half):
    partner = jnp.where(first_half, pltpu.roll(x, 96, 1), pltpu.roll(x, 32, 1))
    return x * cos + partner * sin_signed


def _inproj_kernel(*refs, use_rope):
    if use_rope:
        x_ref, mod_ref, g_ref, w_ref, cos_ref, sin_ref = refs[:6]
        outs = refs[6:]
    else:
        x_ref, mod_ref, g_ref, w_ref = refs[:4]
        outs = refs[4:]
    uf_o, q_o, k_o, v_o, sg_o, gf_o, gr_o = outs

    h = _rms_mod(x_ref[...], g_ref[...], mod_ref[0, 0:1, :], mod_ref[0, 1:2, :])
    hb = h.astype(BF16)

    def proj(cols):
        return _dot(hb, w_ref[:, cols[0]:cols[1]].astype(BF16))

    uf_o[...] = proj(_C_UF).astype(BF16)
    q = proj(_C_Q)
    k = proj(_C_K)
    if use_rope:
        cos = cos_ref[...]
        sin_signed = sin_ref[...]
        lane = lax.broadcasted_iota(jnp.int32, cos.shape, 1)
        first_half = (lane & 32) == 0
        for hd in range(N_RET_HEADS):
            sl = slice(hd * RET_HEAD_DIM, (hd + 1) * RET_HEAD_DIM)
            q_o[:, sl] = (_rope_head(q[:, sl], cos, sin_signed, first_half) * Q_SCALE).astype(BF16)
            k_o[:, sl] = _rope_head(k[:, sl], cos, sin_signed, first_half).astype(BF16)
    else:
        q_o[...] = (q * Q_SCALE).astype(BF16)
        k_o[...] = k.astype(BF16)
    v_o[...] = proj(_C_V).astype(BF16)
    sg_o[...] = _silu(proj(_C_G)).astype(BF16)
    gf_o[...] = jax.nn.sigmoid(proj(_C_GF)).astype(BF16)
    gr_o[...] = jax.nn.sigmoid(proj(_C_GR)).astype(BF16)


def _inproj(x2d, mod3, norm_g, w_in_f32, seq_len, mod_row_of_batch, rope):
    t = x2d.shape[0]
    tm = TM_INPROJ
    tiles_per_seq = max(seq_len // tm, 1)

    def mod_idx(i):
        return (mod_row_of_batch((i * tm) // seq_len), 0, 0)

    in_specs = [pl.BlockSpec((tm, D_MODEL), lambda i: (i, 0)),
                pl.BlockSpec((1, 6, D_MODEL), mod_idx),
                pl.BlockSpec((1, D_MODEL), lambda i: (0, 0)),
                pl.BlockSpec(w_in_f32.shape, lambda i: (0, 0), pipeline_mode=pl.Buffered(1))]
    args = [x2d, mod3, norm_g, w_in_f32]
    if rope is not None:
        in_specs += [pl.BlockSpec((tm, RET_HEAD_DIM), lambda i: (i % tiles_per_seq, 0))] * 2
        args += list(rope)
    widths = [1024, RET_WIDTH, RET_WIDTH, RET_WIDTH, RET_WIDTH, 1024, 1024]
    return pl.pallas_call(
        functools.partial(_inproj_kernel, use_rope=rope is not None),
        grid=(t // tm,),
        in_specs=in_specs,
        out_specs=[pl.BlockSpec((tm, w), lambda i: (i, 0)) for w in widths],
        out_shape=[jax.ShapeDtypeStruct((t, w), BF16) for w in widths],
        compiler_params=pltpu.CompilerParams(dimension_semantics=("parallel",),
                                             vmem_limit_bytes=VMEM_LIMIT),
        name="inproj",
    )(*args)


def _retention_kernel(q_ref, k_ref, v_ref, sg_ref, dec_ref, gn_ref, s0f_ref, s0b_ref,
                      r_ref, sfo_ref, sbo_ref, tab_scr, gc_scr):
    n_chunks = q_ref.shape[0] // CHUNK
    hd = RET_HEAD_DIM

    @pl.when(pl.program_id(0) == 0)
    def _():
        row = lax.broadcasted_iota(jnp.int32, (CHUNK, CHUNK), 0).astype(F32)
        col = lax.broadcasted_iota(jnp.int32, (CHUNK, CHUNK), 1).astype(F32)
        diff = row - col
        for h in range(N_RET_HEADS):
            dec = dec_ref[h]
            lg = jnp.minimum(dec, 0.0) - jnp.log1p(jnp.exp(-jnp.abs(dec)))
            lgf = lg[0:1, :]
            lgb = lg[1:2, :]
            tab_scr[h, 0] = jnp.exp(jnp.where(diff >= 0, lgf * diff, lgb * (-diff)))
            tab_scr[h, 1] = jnp.exp(lgf * (row + 1.0))
            tab_scr[h, 2] = jnp.exp(lgb * (CHUNK - row))
            tab_scr[h, 3] = jnp.exp(lgf * (CHUNK - 1.0 - col))
            tab_scr[h, 4] = jnp.exp(lgb * col)
            gc_scr[h] = jnp.exp(lg * CHUNK)

    def rows(n):
        return slice(n * CHUNK, (n + 1) * CHUNK)

    for h in range(N_RET_HEADS):
        cols = slice(h * hd, (h + 1) * hd)
        decay, qw_f, qw_b, kwt_f, kwt_b = (tab_scr[h, i] for i in range(5))
        gc = gc_scr[h]
        gc_f = gc[0:1, :]
        gc_b = gc[1:2, :]

        kv_f, kv_b = [], []
        for n in range(n_chunks):
            kt = k_ref[rows(n), cols].astype(F32).T
            vn = v_ref[rows(n), cols]
            kv_f.append(_dot((kt * kwt_f).astype(BF16), vn))
            kv_b.append(_dot((kt * kwt_b).astype(BF16), vn))

        s = s0f_ref[h]
        prev_f = []
        for n in range(n_chunks):
            prev_f.append(s.astype(BF16))
            s = gc_f * s + kv_f[n]
        sfo_ref[h] = s
        s = s0b_ref[h]
        prev_b = [None] * n_chunks
        for n in reversed(range(n_chunks)):
            prev_b[n] = s.astype(BF16)
            s = gc_b * s + kv_b[n]
        sbo_ref[h] = s

        gn = gn_ref[:, cols]
        for n in range(n_chunks):
            qn = q_ref[rows(n), cols]
            qf = qn.astype(F32)
            scores = lax.dot_general(qn, k_ref[rows(n), cols], (((1,), (1,)), ((), ())),
                                     preferred_element_type=F32)
            o = _dot((scores * decay).astype(BF16), v_ref[rows(n), cols])
            o = o + _dot((qf * qw_f).astype(BF16), prev_f[n])
            o = o + _dot((qf * qw_b).astype(BF16), prev_b[n])
            mu = jnp.mean(o, axis=-1, keepdims=True)
            d = o - mu
            var = jnp.mean(d * d, axis=-1, keepdims=True)
            on = d * lax.rsqrt(var + EPS) * gn
            r_ref[rows(n), cols] = (on * sg_ref[rows(n), cols].astype(F32)).astype(BF16)


def _retention(q, k, v, sg, dec, gn_g, s0f, s0b, batch, seq_len):
    hd = RET_HEAD_DIM
    tok_spec = pl.BlockSpec((seq_len, RET_WIDTH), lambda b: (b, 0))
    st_spec = pl.BlockSpec((None, N_RET_HEADS, hd, hd), lambda b: (b, 0, 0, 0))
    st_shape = jax.ShapeDtypeStruct((batch, N_RET_HEADS, hd, hd), F32)
    return pl.pallas_call(
        _retention_kernel,
        grid=(batch,),
        in_specs=[tok_spec, tok_spec, tok_spec, tok_spec,
                  pl.BlockSpec(dec.shape, lambda b: (0, 0, 0)),
                  pl.BlockSpec(gn_g.shape, lambda b: (0, 0)),
                  st_spec, st_spec],
        out_specs=[tok_spec, st_spec, st_spec],
        out_shape=[jax.ShapeDtypeStruct((batch * seq_len, RET_WIDTH), BF16), st_shape, st_shape],
        scratch_shapes=[pltpu.VMEM((N_RET_HEADS, 5, CHUNK, CHUNK), F32),
                        pltpu.VMEM((N_RET_HEADS, 2, hd), F32)],
        compiler_params=pltpu.CompilerParams(dimension_semantics=("arbitrary",),
                                             vmem_limit_bytes=VMEM_LIMIT),
        name="retention",
    )(q, k, v, sg, dec, gn_g, s0f, s0b)


def _fnet_merge_kernel(uf_ref, cs_ref, cls_ref, r_ref, gf_ref, gr_ref, x_ref, mod_ref, wf_ref, wr_ref, wo_ref,
                       o_ref, xcs_ref):
    seq_len = uf_ref.shape[0]
    gd = FOURIER_GROUP_DIM

    @pl.when(pl.program_id(1) == 0)
    def _():
        for g in range(N_FOURIER_GROUPS):
            x = _dot(uf_ref[:, g * gd:(g + 1) * gd], cs_ref[...])
            xcs_ref[0:seq_len, g * gd:(g + 1) * gd] = x[:, :gd].astype(BF16)
            xcs_ref[seq_len:2 * seq_len, g * gd:(g + 1) * gd] = x[:, gd:].astype(BF16)

    f_mix = _dot(cls_ref[...], xcs_ref[...]).astype(BF16)
    f_out = _dot(f_mix, wf_ref[...].astype(BF16))
    r_out = _dot(r_ref[...], wr_ref[...].astype(BF16))
    merged = gf_ref[...].astype(F32) * f_out + gr_ref[...].astype(F32) * r_out
    mix = _dot(merged.astype(BF16), wo_ref[...].astype(BF16))
    o_ref[...] = x_ref[...] + mod_ref[0, 2:3, :] * mix


def _fnet_merge(uf, cs, cls, r, gf, gr, x2d, mod3, w_four, w_ret, w_o, batch, seq_len, mod_row_of_batch):
    rb = min(FNET_ROWS, seq_len)
    nr = seq_len // rb

    def tok(w):
        return pl.BlockSpec((rb, w), lambda b, i: (b * nr + i, 0))

    def full(a):
        return pl.BlockSpec(a.shape, lambda b, i: (0, 0))

    def once(a):
        return pl.BlockSpec(a.shape, lambda b, i: (0, 0), pipeline_mode=pl.Buffered(1))

    return pl.pallas_call(
        _fnet_merge_kernel,
        grid=(batch, nr),
        in_specs=[pl.BlockSpec((seq_len, D_MODEL), lambda b, i: (b, 0)),
                  full(cs),
                  pl.BlockSpec((rb, 2 * seq_len), lambda b, i: (i, 0)),
                  tok(RET_WIDTH), tok(D_MODEL), tok(D_MODEL), tok(D_MODEL),
                  pl.BlockSpec((1, 6, D_MODEL), lambda b, i: (mod_row_of_batch(b), 0, 0)),
                  once(w_four), once(w_ret), once(w_o)],
        out_specs=tok(D_MODEL),
        out_shape=jax.ShapeDtypeStruct((batch * seq_len, D_MODEL), F32),
        scratch_shapes=[pltpu.VMEM((2 * seq_len, D_MODEL), BF16)],
        compiler_params=pltpu.CompilerParams(dimension_semantics=("parallel", "arbitrary"),
                                             vmem_limit_bytes=VMEM_LIMIT),
        name="fnet_merge",
    )(uf, cs, cls, r, gf, gr, x2d, mod3, w_four, w_ret, w_o)


def _pack_pair(lo_f32, hi_f32):
    lo = lax.bitcast_convert_type(lo_f32.astype(BF16).astype(F32), jnp.uint32)
    hi = lax.bitcast_convert_type(hi_f32.astype(BF16).astype(F32), jnp.uint32)
    return lax.bitcast_convert_type((lo >> 16) | hi, jnp.int32)


def _unpack_pair(words_i32):
    w = lax.bitcast_convert_type(words_i32, jnp.uint32)
    lo = lax.bitcast_convert_type(w << 16, F32)
    hi = lax.bitcast_convert_type(w & jnp.uint32(0xFFFF0000), F32)
    return lo, hi


def _load_token_words(ref, lead, n_tok):
    parts = []
    for s in range(ROW_SLABS):
        idx = (pl.ds(s, n_tok, stride=ROW_SLABS), slice(None))
        parts.append(ref[lead + idx] if lead else ref[idx])
    return jnp.concatenate(parts, axis=1)


def _store_token_words(ref, words, n_tok):
    for s in range(ROW_SLABS):
        ref[pl.ds(s, n_tok, stride=ROW_SLABS), :] = words[:, s * 128:(s + 1) * 128]


def _route(scores, biased):
    tokens = scores.shape[1]
    neg = -jnp.inf
    epg = EXPERTS_PER_GROUP
    iota_g = lax.broadcasted_iota(jnp.int32, (epg, tokens), 0).astype(F32)

    def pick_first_max(cur, iota, size):
        m = jnp.max(cur, axis=0, keepdims=True)
        idx = jnp.min(jnp.where(cur == m, iota, float(size)), axis=0, keepdims=True)
        return m, idx, iota == idx

    group_scores = []
    for g in range(N_EXPERT_GROUPS):
        vals = biased[g * epg:(g + 1) * epg, :]
        m1, _, hit = pick_first_max(vals, iota_g, epg)
        m2 = jnp.max(jnp.where(hit, neg, vals), axis=0, keepdims=True)
        group_scores.append(m1 + m2)
    cur = jnp.concatenate(group_scores, axis=0)
    group_sel = jnp.zeros_like(cur)
    for _ in range(TOPK_GROUPS):
        _, _, hit = pick_first_max(cur, iota_g, N_EXPERT_GROUPS)
        group_sel = jnp.where(hit, 1.0, group_sel)
        cur = jnp.where(hit, neg, cur)
    masked = jnp.concatenate(
        [jnp.where(group_sel[g:g + 1, :] > 0.0, biased[g * epg:(g + 1) * epg, :], neg)
         for g in range(N_EXPERT_GROUPS)], axis=0)
    iota_e = lax.broadcasted_iota(jnp.int32, masked.shape, 0).astype(F32)
    sel = jnp.zeros_like(masked)
    cur = masked
    picks = []
    for _ in range(TOP_K):
        _, idx, hit = pick_first_max(cur, iota_e, N_EXPERTS)
        picks.append(idx)
        sel = jnp.where(hit, 1.0, sel)
        cur = jnp.where(hit, neg, cur)
    w = scores * sel
    return w / jnp.sum(w, axis=0, keepdims=True) * ROUTED_SCALE, sel, picks


def _router_kernel(x_ref, mod_ref, g2_ref, wrt_ref, rb_ref, hp_ref, ek_ref, rk_ref, wt_ref, cnt_ref,
                   run_scr, earlier_scr):
    tm = x_ref.shape[0]

    @pl.when(pl.program_id(0) == 0)
    def _():
        run_scr[...] = jnp.zeros_like(run_scr)
        earlier = (lax.broadcasted_iota(jnp.int32, (tm, tm), 0) < lax.broadcasted_iota(jnp.int32, (tm, tm), 1))
        earlier_scr[...] = jnp.where(earlier, 1.0, 0.0).astype(BF16)

    h = _rms_mod(x_ref[...], g2_ref[...], mod_ref[0, 3:4, :], mod_ref[0, 4:5, :])
    half = D_MODEL // 2
    _store_token_words(hp_ref, _pack_pair(h[:, :half], h[:, half:]), tm)

    def split(a):
        hi = a.astype(BF16)
        return hi, (a - hi.astype(F32)).astype(BF16)

    def dot_nt(a, b):
        return lax.dot_general(a, b, (((1,), (1,)), ((), ())), preferred_element_type=F32)

    h_hi, h_lo = split(h)
    w_hi, w_lo = split(wrt_ref[...])
    logits_t = dot_nt(w_hi, h_hi) + (dot_nt(w_hi, h_lo) + dot_nt(w_lo, h_hi))
    scores = jax.nn.sigmoid(logits_t)
    comb_t, sel, picks = _route(scores, scores + rb_ref[...])

    rank_t = _dot(sel.astype(BF16), earlier_scr[...]) + run_scr[...]
    run_scr[...] += jnp.sum(sel, axis=1, keepdims=True)
    cnt_ref[...] = jnp.broadcast_to(run_scr[...], cnt_ref.shape)

    iota_e = lax.broadcasted_iota(jnp.int32, sel.shape, 0).astype(F32)
    ranks, weights = [], []
    for idx in picks:
        hit = iota_e == idx
        ranks.append(jnp.sum(jnp.where(hit, rank_t, 0.0), axis=0, keepdims=True))
        weights.append(jnp.sum(jnp.where(hit, comb_t, 0.0), axis=0, keepdims=True))
    ek_ref[...] = jnp.concatenate(picks, axis=0).astype(jnp.int32)
    rk_ref[...] = jnp.concatenate(ranks, axis=0).astype(jnp.int32)
    w_rep = jnp.concatenate([jnp.broadcast_to(w, (SC_LANES, tm)) for w in weights], axis=0)
    wt_ref[...] = w_rep.T


def _router(x1, mod3, norm2_g, w_router_t, router_bias, seq_len, mod_row_of_batch):
    t = x1.shape[0]
    tm = TM_ROUTER

    def mod_idx(i):
        return (mod_row_of_batch((i * tm) // seq_len), 0, 0)

    def full(a):
        return pl.BlockSpec(a.shape, lambda i: (0,) * a.ndim)

    return pl.pallas_call(
        _router_kernel,
        grid=(t // tm,),
        in_specs=[pl.BlockSpec((tm, D_MODEL), lambda i: (i, 0)),
                  pl.BlockSpec((1, 6, D_MODEL), mod_idx),
                  full(norm2_g), full(w_router_t), full(router_bias)],
        out_specs=[pl.BlockSpec((tm * ROW_SLABS, 128), lambda i: (i, 0)),
                   pl.BlockSpec((TOP_K, tm), lambda i: (0, i)),
                   pl.BlockSpec((TOP_K, tm), lambda i: (0, i)),
                   pl.BlockSpec((tm, 128), lambda i: (i, 0)),
                   pl.BlockSpec((N_EXPERTS, 128), lambda i: (0, 0))],
        out_shape=[jax.ShapeDtypeStruct((t * ROW_SLABS, 128), jnp.int32),
                   jax.ShapeDtypeStruct((TOP_K, t), jnp.int32),
                   jax.ShapeDtypeStruct((TOP_K, t), jnp.int32),
                   jax.ShapeDtypeStruct((t, 128), F32),
                   jax.ShapeDtypeStruct((N_EXPERTS, 128), F32)],
        scratch_shapes=[pltpu.VMEM((N_EXPERTS, 1), F32), pltpu.VMEM((tm, tm), BF16)],
        compiler_params=pltpu.CompilerParams(dimension_semantics=("arbitrary",),
                                             vmem_limit_bytes=VMEM_LIMIT),
        name="router",
    )(x1, mod3, norm2_g, w_router_t, router_bias)


def _plan_kernel(ek_ref, rk_ref, cnt_ref, pos_ref, texp_ref, nused_ref, tend_ref, *, expert_rows):
    rows = float(expert_rows)
    cnt = cnt_ref[:, 0:1]
    tiles = jnp.floor((cnt + (rows - 1.0)) / rows)
    before = (lax.broadcasted_iota(jnp.int32, (N_EXPERTS, N_EXPERTS), 1)
              < lax.broadcasted_iota(jnp.int32, (N_EXPERTS, N_EXPERTS), 0))
    tile_start = jnp.dot(jnp.where(before, 1.0, 0.0), jnp.broadcast_to(tiles, (N_EXPERTS, 128)),
                         precision=lax.Precision.HIGHEST, preferred_element_type=F32)[:, 0:1]
    tile_end = tile_start + tiles
    row_start = tile_start * rows

    ek = ek_ref[...]
    pos = rk_ref[...].astype(F32)
    tile_id = lax.broadcasted_iota(jnp.int32, texp_ref.shape, 1).astype(F32)
    texp = jnp.zeros(texp_ref.shape, F32)
    for e in range(N_EXPERTS):
        pos = pos + jnp.where(ek == e, row_start[e:e + 1, :], 0.0)
        texp = texp + jnp.where(tile_id >= tile_end[e:e + 1, :], 1.0, 0.0)
    pos_ref[...] = pos.astype(jnp.int32)
    texp_ref[...] = jnp.minimum(texp, N_EXPERTS - 1.0).astype(jnp.int32)
    nused_ref[...] = jnp.broadcast_to(tile_end[N_EXPERTS - 1:N_EXPERTS, :], nused_ref.shape).astype(jnp.int32)
    tend_ref[...] = jnp.broadcast_to(tile_end, tend_ref.shape).astype(jnp.int32)


def _plan(ek, rk, cnt, n_tiles_pad, expert_rows):
    t = ek.shape[1]

    def full(shape):
        return pl.BlockSpec(shape, lambda: (0,) * len(shape))

    return pl.pallas_call(
        functools.partial(_plan_kernel, expert_rows=expert_rows),
        in_specs=[full(ek.shape), full(rk.shape), full(cnt.shape)],
        out_specs=[full((TOP_K, t)), full((1, n_tiles_pad)), full((1, 128)), full((N_EXPERTS, 128))],
        out_shape=[jax.ShapeDtypeStruct((TOP_K, t), jnp.int32),
                   jax.ShapeDtypeStruct((1, n_tiles_pad), jnp.int32),
                   jax.ShapeDtypeStruct((1, 128), jnp.int32),
                   jax.ShapeDtypeStruct((N_EXPERTS, 128), jnp.int32)],
        compiler_params=pltpu.CompilerParams(vmem_limit_bytes=VMEM_LIMIT),
        name="plan",
    )(ek, rk, cnt)


def _sc_mesh():
    return plsc.VectorSubcoreMesh(core_axis_name="c", subcore_axis_name="s")


def _sc_pack_weight_halves(w):
    e, k, n = w.shape
    k_half = k // 2
    rb = SC_PACK_BLOCK_WORDS // n
    units_per_expert = k_half // rb
    per_w = (e * units_per_expert) // SC_WORKERS
    lanes = SC_LANES

    @functools.partial(
        pl.kernel, out_type=jax.ShapeDtypeStruct((e * k_half, n), jnp.int32), mesh=_sc_mesh(),
        scratch_types=[pltpu.VMEM((rb, n), F32), pltpu.VMEM((rb, n), F32), pltpu.VMEM((rb, n), jnp.int32)],
        compiler_params=pltpu.CompilerParams(needs_layout_passes=False))
    def kern(w_hbm, out_hbm, a_v, b_v, o_v):
        wid = lax.axis_index("s") * SC_CORES + lax.axis_index("c")

        @pl.loop(0, per_w)
        def _(j):
            unit = wid * per_w + j
            expert = unit // units_per_expert
            blk = unit % units_per_expert
            row_a = expert * k + blk * rb
            pltpu.sync_copy(w_hbm.at[pl.ds(row_a, rb)], a_v)
            pltpu.sync_copy(w_hbm.at[pl.ds(row_a + k_half, rb)], b_v)

            @pl.loop(0, rb)
            def _(r):
                @plsc.parallel_loop(0, n, step=lanes, unroll=4)
                def _(c):
                    both = plsc.pack(a_v[r, pl.ds(c, lanes)], b_v[r, pl.ds(c, lanes)],
                                     format=plsc.PackFormat.INTERLEAVED)
                    o_v[r, pl.ds(c, lanes)] = plsc.bitcast(both, jnp.int32)

            pltpu.sync_copy(o_v, out_hbm.at[pl.ds(expert * k_half + blk * rb, rb)])

    return kern(w.reshape(e * k, n)).reshape(e, k_half, n)


def _sc_dispatch(rows, pos3, n_out, after=()):
    t = rows.shape[0]
    ch = SC_CHUNK
    per_w = (t // ch) // SC_WORKERS

    @functools.partial(
        pl.kernel, out_type=jax.ShapeDtypeStruct((n_out,) + rows.shape[1:], jnp.int32), mesh=_sc_mesh(),
        scratch_types=[pltpu.VMEM((TOP_K, ch), jnp.int32), pltpu.VMEM((ch,) + rows.shape[1:], jnp.int32),
                       pltpu.SemaphoreType.DMA])
    def k(rows_hbm, pos_hbm, *rest):
        out_hbm, idx_v, rows_v, sem = rest[len(after):]
        wid = lax.axis_index("s") * SC_CORES + lax.axis_index("c")

        @pl.loop(0, per_w)
        def _(j):
            c = wid * per_w + j
            pltpu.sync_copy(pos_hbm.at[c], idx_v)
            pltpu.sync_copy(rows_hbm.at[pl.ds(c * ch, ch)], rows_v)
            copies = [pltpu.async_copy(rows_v, out_hbm.at[idx_v.at[kk]], sem) for kk in range(TOP_K)]
            for cp in copies:
                cp.wait()

    return k(rows, pos3, *after)


def _sc_combine(table, pos3, wtok, t):
    ch = SC_CHUNK
    sub = SC_COMBINE_TOKENS
    lanes = SC_LANES
    slabs = ROW_SLABS
    per_w = (t // ch) // SC_WORKERS
    subs_per_chunk = ch // sub
    n_steps = per_w * subs_per_chunk

    @functools.partial(
        pl.kernel, out_type=jax.ShapeDtypeStruct((t, slabs, 128), jnp.int32), mesh=_sc_mesh(),
        scratch_types=[pltpu.VMEM((per_w, TOP_K, ch), jnp.int32),
                       pltpu.VMEM((2, TOP_K, sub, slabs, 128), jnp.int32),
                       pltpu.VMEM((2, sub, 128), F32),
                       pltpu.VMEM((sub, slabs, 128), jnp.int32),
                       pltpu.SemaphoreType.DMA((2,))],
        compiler_params=pltpu.CompilerParams(needs_layout_passes=False))
    def k(tab_hbm, pos_hbm, w_hbm, out_hbm, idx_v, rows_v, w_v, out_v, sem):
        wid = lax.axis_index("s") * SC_CORES + lax.axis_index("c")
        for j in range(per_w):
            pltpu.sync_copy(pos_hbm.at[wid * per_w + j], idx_v.at[j])

        def first_token(step):
            return (wid * per_w + step // subs_per_chunk) * ch + (step % subs_per_chunk) * sub

        def copies(step, slot):
            j = step // subs_per_chunk
            s = step % subs_per_chunk
            idx = [idx_v.at[j, kk, pl.ds(s * sub, sub)] for kk in range(TOP_K)]
            return ([pltpu.make_async_copy(tab_hbm.at[idx[kk]], rows_v.at[slot, kk], sem.at[slot])
                     for kk in range(TOP_K)]
                    + [pltpu.make_async_copy(w_hbm.at[pl.ds(first_token(step), sub)], w_v.at[slot], sem.at[slot])])

        for cp in copies(0, 0):
            cp.start()

        @pl.loop(0, n_steps)
        def _(step):
            slot = step % 2

            @pl.when(step + 1 < n_steps)
            def _():
                for cp in copies(step + 1, 1 - slot):
                    cp.start()

            for cp in copies(step, slot):
                cp.wait()

            @pl.loop(0, sub)
            def _(tt):
                wk = [w_v[slot, tt, pl.ds(kk * lanes, lanes)] for kk in range(TOP_K)]
                for sl in range(slabs):
                    @plsc.parallel_loop(0, 128, step=lanes, unroll=8)
                    def _(off):
                        acc_lo = jnp.zeros((lanes,), F32)
                        acc_hi = jnp.zeros((lanes,), F32)
                        for kk in range(TOP_K):
                            word = rows_v[slot, kk, tt, sl, pl.ds(off, lanes)]
                            lo = plsc.bitcast(word << 16, F32)
                            hi = plsc.bitcast(word & jnp.int32(-65536), F32)
                            acc_lo = acc_lo + wk[kk] * lo
                            acc_hi = acc_hi + wk[kk] * hi
                        both = plsc.pack(acc_lo, acc_hi, format=plsc.PackFormat.INTERLEAVED)
                        out_v[tt, sl, pl.ds(off, lanes)] = plsc.bitcast(both, jnp.int32)

            pltpu.sync_copy(out_v, out_hbm.at[pl.ds(first_token(step), sub)])

    return k(table, pos3, wtok)


def _experts_kernel(texp_ref, nused_ref, tend_ref, xs_ref, weg_hbm, weu_hbm, wed_hbm, ys_ref,
                    wg_scr, wu_scr, wd_scr, wg_buf, wu_buf, wd_buf, sem, group_scr, *, expert_rows):
    step = pl.program_id(0)
    rows = expert_rows
    tiles_per_step = EXPERT_TILES_PER_STEP
    half = D_MODEL // 2
    n_used = nused_ref[0]

    def weight_copies(e, slot):
        return [pltpu.make_async_copy(weg_hbm.at[e], wg_buf.at[slot], sem.at[slot, 0]),
                pltpu.make_async_copy(weu_hbm.at[e], wu_buf.at[slot], sem.at[slot, 1]),
                pltpu.make_async_copy(wed_hbm.at[e], wd_buf.at[slot], sem.at[slot, 2])]

    def next_group(e):
        tile = tend_ref[e]
        return texp_ref[jnp.minimum(tile, n_used - 1)], tile < n_used

    def start_weights(e, slot, exists):
        @pl.when(exists)
        def _():
            for cp in weight_copies(e, slot):
                cp.start()

    @pl.when(step == 0)
    def _():
        group_scr[0] = 0
        e, exists = texp_ref[0], True
        for slot in range(WEIGHT_SLOTS - 1):
            start_weights(e, slot, exists)
            nxt, has_next = next_group(e)
            e, exists = nxt, exists & has_next

    def row_tile(tile, x_view, y_view):
        expert = texp_ref[tile]
        used = tile < n_used
        new_expert = (tile == 0) | (expert != texp_ref[jnp.maximum(tile - 1, 0)])

        @pl.when(used & new_expert)
        def _():
            group = group_scr[0]
            slot = group % WEIGHT_SLOTS
            ahead, exists = expert, True
            for _ in range(WEIGHT_SLOTS - 1):
                nxt, has_next = next_group(ahead)
                ahead, exists = nxt, exists & has_next
            start_weights(ahead, (group + WEIGHT_SLOTS - 1) % WEIGHT_SLOTS, exists)

            for cp in weight_copies(expert, slot):
                cp.wait()
            for scr, buf in ((wg_scr, wg_buf), (wu_scr, wu_buf), (wd_scr, wd_buf)):
                top, bottom = _unpack_pair(buf[slot])
                k_half = top.shape[0]
                scr[0:k_half, :] = top.astype(BF16)
                scr[k_half:2 * k_half, :] = bottom.astype(BF16)
            group_scr[0] = group + 1

        @pl.when(used)
        def _():
            lo, hi = _unpack_pair(_load_token_words(x_view, (), rows))
            lo = lo.astype(BF16)
            hi = hi.astype(BF16)
            g = _dot(lo, wg_scr[0:half, :]) + _dot(hi, wg_scr[half:D_MODEL, :])
            u = _dot(lo, wu_scr[0:half, :]) + _dot(hi, wu_scr[half:D_MODEL, :])
            y = _dot((_silu(g) * u).astype(BF16), wd_scr[...])
            _store_token_words(y_view, _pack_pair(y[:, :half], y[:, half:]), rows)

        @pl.when(jnp.logical_not(used) & (step == (n_used - 1) // tiles_per_step))
        def _():
            y_view[...] = jnp.zeros_like(y_view)

    for s in range(tiles_per_step):
        view = pl.ds(s * rows * ROW_SLABS, rows * ROW_SLABS)
        row_tile(step * tiles_per_step + s, xs_ref.at[view], ys_ref.at[view])


def _experts(texp, nused, tend, xs2d, weg, weu, wed, n_tiles, expert_rows):
    tiles_per_step = EXPERT_TILES_PER_STEP
    block = (tiles_per_step * expert_rows * ROW_SLABS, 128)
    hbm = pl.BlockSpec(memory_space=pl.ANY)

    def block_idx(j, te, nu, tn):
        return (jnp.minimum(j, (nu[0] - 1) // tiles_per_step), 0)

    grid_spec = pltpu.PrefetchScalarGridSpec(
        num_scalar_prefetch=3,
        grid=(n_tiles // tiles_per_step,),
        in_specs=[pl.BlockSpec(block, block_idx), hbm, hbm, hbm],
        out_specs=pl.BlockSpec(block, block_idx),
        scratch_shapes=[pltpu.VMEM((D_MODEL, EXPERT_DIM), BF16),
                        pltpu.VMEM((D_MODEL, EXPERT_DIM), BF16),
                        pltpu.VMEM((EXPERT_DIM, D_MODEL), BF16),
                        pltpu.VMEM((WEIGHT_SLOTS,) + weg.shape[1:], jnp.int32),
                        pltpu.VMEM((WEIGHT_SLOTS,) + weu.shape[1:], jnp.int32),
                        pltpu.VMEM((WEIGHT_SLOTS,) + wed.shape[1:], jnp.int32),
                        pltpu.SemaphoreType.DMA((WEIGHT_SLOTS, 3)),
                        pltpu.SMEM((1,), jnp.int32)],
    )
    return pl.pallas_call(
        functools.partial(_experts_kernel, expert_rows=expert_rows),
        grid_spec=grid_spec,
        out_shape=jax.ShapeDtypeStruct(xs2d.shape, jnp.int32),
        compiler_params=pltpu.CompilerParams(dimension_semantics=("arbitrary",),
                                             vmem_limit_bytes=VMEM_LIMIT),
        name="experts",
    )(texp, nused, tend, xs2d, weg, weu, wed)


def _final_kernel(x_ref, routed_ref, mod_ref, g2_ref, wsg_ref, wsu_ref, wsd_ref, fng_ref, o_ref):
    tm = x_ref.shape[0]
    x = x_ref[...]
    hb = _rms_mod(x, g2_ref[...], mod_ref[0, 3:4, :], mod_ref[0, 4:5, :]).astype(BF16)
    shared = _dot((_silu(_dot(hb, wsg_ref[...])) * _dot(hb, wsu_ref[...])).astype(BF16), wsd_ref[...])
    routed = jnp.concatenate(_unpack_pair(_load_token_words(routed_ref, (), tm)), axis=1)
    y = x + mod_ref[0, 5:6, :] * (routed + shared)
    ms = jnp.mean(y * y, axis=-1, keepdims=True)
    o_ref[...] = y * lax.rsqrt(ms + EPS) * fng_ref[...]


def _final(x1, routed2d, mod3, norm2_g, wsg, wsu, wsd, final_g, seq_len, mod_row_of_batch):
    t = x1.shape[0]
    tm = TM_FINAL

    def mod_idx(i):
        return (mod_row_of_batch((i * tm) // seq_len), 0, 0)

    def full(a):
        return pl.BlockSpec(a.shape, lambda i: (0,) * a.ndim)

    return pl.pallas_call(
        _final_kernel,
        grid=(t // tm,),
        in_specs=[pl.BlockSpec((tm, D_MODEL), lambda i: (i, 0)),
                  pl.BlockSpec((tm * ROW_SLABS, 128), lambda i: (i, 0)),
                  pl.BlockSpec((1, 6, D_MODEL), mod_idx),
                  full(norm2_g), full(wsg), full(wsu), full(wsd), full(final_g)],
        out_specs=pl.BlockSpec((tm, D_MODEL), lambda i: (i, 0)),
        out_shape=jax.ShapeDtypeStruct((t, D_MODEL), F32),
        compiler_params=pltpu.CompilerParams(dimension_semantics=("parallel",),
                                             vmem_limit_bytes=VMEM_LIMIT),
        name="final",
    )(x1, routed2d, mod3, norm2_g, wsg, wsu, wsd, final_g)


def _moe(x1, mod3, lw, seq_len, mod_row_of_batch):
    t = x1.shape[0]
    expert_rows = min(MAX_EXPERT_ROWS, TOP_K * t // N_EXPERTS // 2)
    n_tiles = TOP_K * t // expert_rows + N_EXPERTS
    n_tiles_pad = -(-n_tiles // 128) * 128
    hp2d, ek, rk, wtok, cnt = _router(x1, mod3, lw["norm2_g"], lw["w_router_t"], lw["router_bias"],
                                      seq_len, mod_row_of_batch)
    pos, texp, nused, tend = _plan(ek, rk, cnt, n_tiles_pad, expert_rows)
    pos3 = pos.reshape(TOP_K, t // SC_CHUNK, SC_CHUNK).transpose(1, 0, 2)
    xs = _sc_dispatch(hp2d.reshape(t, ROW_SLABS, 128), pos3, n_tiles * expert_rows,
                      after=(lw["weg"], lw["weu"], lw["wed"]))
    ys2d = _experts(texp.reshape(-1), nused.reshape(-1), tend[:, 0], xs.reshape(-1, 128),
                    lw["weg"], lw["weu"], lw["wed"], n_tiles, expert_rows)
    routed = _sc_combine(ys2d.reshape(-1, ROW_SLABS, 128), pos3, wtok, t)
    return _final(x1, routed.reshape(t * ROW_SLABS, 128), mod3, lw["norm2_g"],
                  lw["wsg"], lw["wsu"], lw["wsd"], lw["final_g"], seq_len, mod_row_of_batch)


def _dft_tables(seq_len):
    gd = FOURIER_GROUP_DIM
    kc = np.arange(gd)
    ang_c = ((kc[:, None] * kc[None, :]) % gd) * (2.0 * math.pi / gd)
    cs = np.concatenate([np.cos(ang_c), np.sin(ang_c)], axis=1) * (gd ** -0.5)
    kl = np.arange(seq_len)
    ang_l = ((kl[:, None] * kl[None, :]) % seq_len) * (2.0 * math.pi / seq_len)
    cls = np.concatenate([np.cos(ang_l), -np.sin(ang_l)], axis=1) * (seq_len ** -0.5)
    return jnp.asarray(cs.astype(np.float32), dtype=BF16), jnp.asarray(cls.astype(np.float32), dtype=BF16)


def _rope_tables(length):
    rows = length // GRID_W
    r = np.repeat(np.arange(rows, dtype=np.float32), GRID_W)
    col = np.tile(np.arange(GRID_W, dtype=np.float32), rows)
    nf = RET_HEAD_DIM // 4
    inv = (np.float32(ROPE_BASE) ** (-np.arange(nf, dtype=np.float32) / np.float32(nf))).astype(np.float32)
    ar = r[:, None] * inv[None]
    ac = col[:, None] * inv[None]
    ang = np.concatenate([ar, ar, ac, ac], axis=-1).astype(np.float64)
    sign = np.where((np.arange(RET_HEAD_DIM) & nf) == 0, -1.0, 1.0)
    return (jnp.asarray(np.cos(ang).astype(np.float32)),
            jnp.asarray((np.sin(ang) * sign[None, :]).astype(np.float32)))


def _trunk_path(x, mod3, mod_row_of_batch, s0f, s0b, rope, lw):
    batch, seq_len, _ = x.shape
    x2d = x.reshape(batch * seq_len, D_MODEL)
    uf, q, k, v, sg, gf, gr = _inproj(x2d, mod3, lw["norm1_g"], lw["w_in"], seq_len, mod_row_of_batch, rope)
    r, s_f, s_b = _retention(q, k, v, sg, lw["dec"], lw["gn_g"], s0f, s0b, batch, seq_len)
    cs, cls = _dft_tables(seq_len)
    x1 = _fnet_merge(uf, cs, cls, r, gf, gr, x2d, mod3, lw["w_four"], lw["w_ret"], lw["w_o"],
                     batch, seq_len, mod_row_of_batch)
    y = _moe(x1, mod3, lw, seq_len, mod_row_of_batch)
    return y.reshape(batch, seq_len, D_MODEL), s_f, s_b


def kernel(x_prompt, x_sample, state_ret_fwd, state_ret_bwd, c, c_ctx, w_ada, b_ada, norm1_g, norm2_g, w_in,
           ret_decay_fwd, ret_decay_bwd, ret_gn_g, w_four_out, w_ret_out, w_out, w_router, router_bias,
           w_exp_gate, w_exp_up, w_exp_down, w_shared_gate, w_shared_up, w_shared_down, final_norm_g):
    depth = w_ada.shape[0]
    assert depth == 1, "final norm is fused into the last layer's MoE kernel"
    n_ctx, n_lat = x_prompt.shape[0], x_sample.shape[0]
    cond = jnp.concatenate([c_ctx[None, :], c], axis=0)
    cond = jnp.pad(cond, ((0, (-cond.shape[0]) % 8), (0, 0)))
    rope = _rope_tables(x_sample.shape[1])
    zeros = jnp.zeros((n_ctx, N_RET_HEADS, RET_HEAD_DIM, RET_HEAD_DIM), F32)

    layer = 0
    mod = _ada(cond, w_ada[layer], b_ada[layer][None, :])
    mod3 = mod.reshape(mod.shape[0], 6, D_MODEL)
    dec = jnp.stack([ret_decay_fwd[layer], ret_decay_bwd[layer]], axis=1)
    lw = {
        "norm1_g": norm1_g[layer][None, :],
        "norm2_g": norm2_g[layer][None, :],
        "w_in": w_in[layer],
        "dec": jnp.broadcast_to(dec[:, :, None], (N_RET_HEADS, 2, RET_HEAD_DIM)).astype(F32),
        "gn_g": ret_gn_g[layer][None, :],
        "w_four": w_four_out[layer],
        "w_ret": w_ret_out[layer],
        "w_o": w_out[layer],
        "w_router_t": w_router[layer].T,
        "router_bias": router_bias[layer][:, None],
        "weg": _sc_pack_weight_halves(w_exp_gate[layer]),
        "weu": _sc_pack_weight_halves(w_exp_up[layer]),
        "wed": _sc_pack_weight_halves(w_exp_down[layer]),
        "wsg": w_shared_gate[layer].astype(BF16),
        "wsu": w_shared_up[layer].astype(BF16),
        "wsd": w_shared_down[layer].astype(BF16),
        "final_g": final_norm_g[None, :],
    }
    y_prompt, s_f, s_b = _trunk_path(x_prompt, mod3, lambda b: 0, zeros, zeros, None, lw)
    y_sample, _, _ = _trunk_path(x_sample, mod3, lambda b: 1 + b, state_ret_fwd[:, layer],
                                 state_ret_bwd[:, layer], rope, lw)
    return (y_prompt, y_sample, s_f[:, None], s_b[:, None])
```

```python
import functools
import math

import jax
import jax.numpy as jnp
import numpy as np
from jax import lax
from jax.experimental import pallas as pl
from jax.experimental.pallas import tpu as pltpu
from jax.experimental.pallas import tpu_sc as plsc

F32 = jnp.float32
BF16 = jnp.bfloat16

D_MODEL = 1024
GRID_W = 64
N_FOURIER_GROUPS = 8
FOURIER_GROUP_DIM = 128
N_RET_HEADS = 4
RET_HEAD_DIM = 128
RET_WIDTH = N_RET_HEADS * RET_HEAD_DIM
CHUNK = 128
N_EXPERTS = 64
N_EXPERT_GROUPS = 8
EXPERTS_PER_GROUP = N_EXPERTS // N_EXPERT_GROUPS
TOPK_GROUPS = 4
TOP_K = 8
EXPERT_DIM = 256
ROUTED_SCALE = 2.5
ROPE_BASE = 10000.0
EPS = 1e-6
Q_SCALE = RET_HEAD_DIM ** -0.5

_C_UF = (0, 1024)
_C_Q = (1024, 1536)
_C_K = (1536, 2048)
_C_V = (2048, 2560)
_C_G = (2560, 3072)
_C_GF = (3072, 4096)
_C_GR = (4096, 5120)

VMEM_LIMIT = 56 * 1024 * 1024

TM_INPROJ = 1024
TM_ROUTER = 512
FNET_ROWS = 512
TM_FINAL = 1024
EXPERT_TILES_PER_STEP = 4
MAX_EXPERT_ROWS = 512
WEIGHT_SLOTS = 3
ROW_SLABS = 4
SC_CORES = 2
SC_WORKERS = 32
SC_CHUNK = 128
SC_LANES = 16
SC_PACK_BLOCK_WORDS = 16384
SC_COMBINE_TOKENS = 8


def _silu(x):
    return x * jax.nn.sigmoid(x)


def _dot(a, b):
    return jnp.dot(a, b, preferred_element_type=F32)


def _rms_mod(x, g, shift, scale):
    ms = jnp.mean(x * x, axis=-1, keepdims=True)
    y = x * lax.rsqrt(ms + EPS) * g
    return y * (1.0 + scale) + shift


def _ada_kernel(cond_ref, w_ref, b_ref, o_ref):
    s = _silu(cond_ref[...]).astype(BF16)
    o_ref[...] = _dot(s, w_ref[...].astype(BF16)) + b_ref[...]


def _ada(cond, w_ada, b_ada):
    rows, n = cond.shape[0], w_ada.shape[1]
    tn = 1536
    return pl.pallas_call(
        _ada_kernel,
        grid=(n // tn,),
        in_specs=[pl.BlockSpec((rows, D_MODEL), lambda j: (0, 0)),
                  pl.BlockSpec((D_MODEL, tn), lambda j: (0, j)),
                  pl.BlockSpec((1, tn), lambda j: (0, j))],
        out_specs=pl.BlockSpec((rows, tn), lambda j: (0, j)),
        out_shape=jax.ShapeDtypeStruct((rows, n), F32),
        compiler_params=pltpu.CompilerParams(vmem_limit_bytes=VMEM_LIMIT),
        name="ada",
    )(cond, w_ada, b_ada)


def _rope_head(x, cos, sin_signed, first_half):
    partner = jnp.where(first_half, pltpu.roll(x, 96, 1), pltpu.roll(x, 32, 1))
    return x * cos + partner * sin_signed


def _inproj_kernel(*refs, use_rope):
    if use_rope:
        x_ref, mod_ref, g_ref, w_ref, cos_ref, sin_ref = refs[:6]
        outs = refs[6:]
    else:
        x_ref, mod_ref, g_ref, w_ref = refs[:4]
        outs = refs[4:]
    uf_o, q_o, k_o, v_o, sg_o, gf_o, gr_o = outs

    h = _rms_mod(x_ref[...], g_ref[...], mod_ref[0, 0:1, :], mod_ref[0, 1:2, :])
    hb = h.astype(BF16)

    def proj(cols):
        return _dot(hb, w_ref[:, cols[0]:cols[1]].astype(BF16))

    uf_o[...] = proj(_C_UF).astype(BF16)
    q = proj(_C_Q)
    k = proj(_C_K)
    if use_rope:
        cos = cos_ref[...]
        sin_signed = sin_ref[...]
        lane = lax.broadcasted_iota(jnp.int32, cos.shape, 1)
        first_half = (lane & 32) == 0
        for hd in range(N_RET_HEADS):
            sl = slice(hd * RET_HEAD_DIM, (hd + 1) * RET_HEAD_DIM)
            q_o[:, sl] = (_rope_head(q[:, sl], cos, sin_signed, first_half) * Q_SCALE).astype(BF16)
            k_o[:, sl] = _rope_head(k[:, sl], cos, sin_signed, first_half).astype(BF16)
    else:
        q_o[...] = (q * Q_SCALE).astype(BF16)
        k_o[...] = k.astype(BF16)
    v_o[...] = proj(_C_V).astype(BF16)
    sg_o[...] = _silu(proj(_C_G)).astype(BF16)
    gf_o[...] = jax.nn.sigmoid(proj(_C_GF)).astype(BF16)
    gr_o[...] = jax.nn.sigmoid(proj(_C_GR)).astype(BF16)


def _inproj(x2d, mod3, norm_g, w_in_f32, seq_len, mod_row_of_batch, rope):
    t = x2d.shape[0]
    tm = TM_INPROJ
    tiles_per_seq = max(seq_len // tm, 1)

    def mod_idx(i):
        return (mod_row_of_batch((i * tm) // seq_len), 0, 0)

    in_specs = [pl.BlockSpec((tm, D_MODEL), lambda i: (i, 0)),
                pl.BlockSpec((1, 6, D_MODEL), mod_idx),
                pl.BlockSpec((1, D_MODEL), lambda i: (0, 0)),
                pl.BlockSpec(w_in_f32.shape, lambda i: (0, 0), pipeline_mode=pl.Buffered(1))]
    args = [x2d, mod3, norm_g, w_in_f32]
    if rope is not None:
        in_specs += [pl.BlockSpec((tm, RET_HEAD_DIM), lambda i: (i % tiles_per_seq, 0))] * 2
        args += list(rope)
    widths = [1024, RET_WIDTH, RET_WIDTH, RET_WIDTH, RET_WIDTH, 1024, 1024]
    return pl.pallas_call(
        functools.partial(_inproj_kernel, use_rope=rope is not None),
        grid=(t // tm,),
        in_specs=in_specs,
        out_specs=[pl.BlockSpec((tm, w), lambda i: (i, 0)) for w in widths],
        out_shape=[jax.ShapeDtypeStruct((t, w), BF16) for w in widths],
        compiler_params=pltpu.CompilerParams(dimension_semantics=("parallel",),
                                             vmem_limit_bytes=VMEM_LIMIT),
        name="inproj",
    )(*args)


def _retention_kernel(q_ref, k_ref, v_ref, sg_ref, dec_ref, gn_ref, s0f_ref, s0b_ref,
                      r_ref, sfo_ref, sbo_ref, tab_scr, gc_scr):
    n_chunks = q_ref.shape[0] // CHUNK
    hd = RET_HEAD_DIM

    @pl.when(pl.program_id(0) == 0)
    def _():
        row = lax.broadcasted_iota(jnp.int32, (CHUNK, CHUNK), 0).astype(F32)
        col = lax.broadcasted_iota(jnp.int32, (CHUNK, CHUNK), 1).astype(F32)
        diff = row - col
        for h in range(N_RET_HEADS):
            dec = dec_ref[h]
            lg = jnp.minimum(dec, 0.0) - jnp.log1p(jnp.exp(-jnp.abs(dec)))
            lgf = lg[0:1, :]
            lgb = lg[1:2, :]
            tab_scr[h, 0] = jnp.exp(jnp.where(diff >= 0, lgf * diff, lgb * (-diff)))
            tab_scr[h, 1] = jnp.exp(lgf * (row + 1.0))
            tab_scr[h, 2] = jnp.exp(lgb * (CHUNK - row))
            tab_scr[h, 3] = jnp.exp(lgf * (CHUNK - 1.0 - col))
            tab_scr[h, 4] = jnp.exp(lgb * col)
            gc_scr[h] = jnp.exp(lg * CHUNK)

    def rows(n):
        return slice(n * CHUNK, (n + 1) * CHUNK)

    for h in range(N_RET_HEADS):
        cols = slice(h * hd, (h + 1) * hd)
        decay, qw_f, qw_b, kwt_f, kwt_b = (tab_scr[h, i] for i in range(5))
        gc = gc_scr[h]
        gc_f = gc[0:1, :]
        gc_b = gc[1:2, :]

        kv_f, kv_b = [], []
        for n in range(n_chunks):
            kt = k_ref[rows(n), cols].astype(F32).T
            vn = v_ref[rows(n), cols]
            kv_f.append(_dot((kt * kwt_f).astype(BF16), vn))
            kv_b.append(_dot((kt * kwt_b).astype(BF16), vn))

        s = s0f_ref[h]
        prev_f = []
        for n in range(n_chunks):
            prev_f.append(s.astype(BF16))
            s = gc_f * s + kv_f[n]
        sfo_ref[h] = s
        s = s0b_ref[h]
        prev_b = [None] * n_chunks
        for n in reversed(range(n_chunks)):
            prev_b[n] = s.astype(BF16)
            s = gc_b * s + kv_b[n]
        sbo_ref[h] = s

        gn = gn_ref[:, cols]
        for n in range(n_chunks):
            qn = q_ref[rows(n), cols]
            qf = qn.astype(F32)
            scores = lax.dot_general(qn, k_ref[rows(n), cols], (((1,), (1,)), ((), ())),
                                     preferred_element_type=F32)
            o = _dot((scores * decay).astype(BF16), v_ref[rows(n), cols])
            o = o + _dot((qf * qw_f).astype(BF16), prev_f[n])
            o = o + _dot((qf * qw_b).astype(BF16), prev_b[n])
            mu = jnp.mean(o, axis=-1, keepdims=True)
            d = o - mu
            var = jnp.mean(d * d, axis=-1, keepdims=True)
            on = d * lax.rsqrt(var + EPS) * gn
            r_ref[rows(n), cols] = (on * sg_ref[rows(n), cols].astype(F32)).astype(BF16)


def _retention(q, k, v, sg, dec, gn_g, s0f, s0b, batch, seq_len):
    hd = RET_HEAD_DIM
    tok_spec = pl.BlockSpec((seq_len, RET_WIDTH), lambda b: (b, 0))
    st_spec = pl.BlockSpec((None, N_RET_HEADS, hd, hd), lambda b: (b, 0, 0, 0))
    st_shape = jax.ShapeDtypeStruct((batch, N_RET_HEADS, hd, hd), F32)
    return pl.pallas_call(
        _retention_kernel,
        grid=(batch,),
        in_specs=[tok_spec, tok_spec, tok_spec, tok_spec,
                  pl.BlockSpec(dec.shape, lambda b: (0, 0, 0)),
                  pl.BlockSpec(gn_g.shape, lambda b: (0, 0)),
                  st_spec, st_spec],
        out_specs=[tok_spec, st_spec, st_spec],
        out_shape=[jax.ShapeDtypeStruct((batch * seq_len, RET_WIDTH), BF16), st_shape, st_shape],
        scratch_shapes=[pltpu.VMEM((N_RET_HEADS, 5, CHUNK, CHUNK), F32),
                        pltpu.VMEM((N_RET_HEADS, 2, hd), F32)],
        compiler_params=pltpu.CompilerParams(dimension_semantics=("arbitrary",),
                                             vmem_limit_bytes=VMEM_LIMIT),
        name="retention",
    )(q, k, v, sg, dec, gn_g, s0f, s0b)


def _fnet_merge_kernel(uf_ref, cs_ref, cls_ref, r_ref, gf_ref, gr_ref, x_ref, mod_ref, wf_ref, wr_ref, wo_ref,
                       o_ref, xcs_ref):
    seq_len = uf_ref.shape[0]
    gd = FOURIER_GROUP_DIM

    @pl.when(pl.program_id(1) == 0)
    def _():
        for g in range(N_FOURIER_GROUPS):
            x = _dot(uf_ref[:, g * gd:(g + 1) * gd], cs_ref[...])
            xcs_ref[0:seq_len, g * gd:(g + 1) * gd] = x[:, :gd].astype(BF16)
            xcs_ref[seq_len:2 * seq_len, g * gd:(g + 1) * gd] = x[:, gd:].astype(BF16)

    f_mix = _dot(cls_ref[...], xcs_ref[...]).astype(BF16)
    f_out = _dot(f_mix, wf_ref[...].astype(BF16))
    r_out = _dot(r_ref[...], wr_ref[...].astype(BF16))
    merged = gf_ref[...].astype(F32) * f_out + gr_ref[...].astype(F32) * r_out
    mix = _dot(merged.astype(BF16), wo_ref[...].astype(BF16))
    o_ref[...] = x_ref[...] + mod_ref[0, 2:3, :] * mix


def _fnet_merge(uf, cs, cls, r, gf, gr, x2d, mod3, w_four, w_ret, w_o, batch, seq_len, mod_row_of_batch):
    rb = min(FNET_ROWS, seq_len)
    nr = seq_len // rb

    def tok(w):
        return pl.BlockSpec((rb, w), lambda b, i: (b * nr + i, 0))

    def full(a):
        return pl.BlockSpec(a.shape, lambda b, i: (0, 0))

    def once(a):
        return pl.BlockSpec(a.shape, lambda b, i: (0, 0), pipeline_mode=pl.Buffered(1))

    return pl.pallas_call(
        _fnet_merge_kernel,
        grid=(batch, nr),
        in_specs=[pl.BlockSpec((seq_len, D_MODEL), lambda b, i: (b, 0)),
                  full(cs),
                  pl.BlockSpec((rb, 2 * seq_len), lambda b, i: (i, 0)),
                  tok(RET_WIDTH), tok(D_MODEL), tok(D_MODEL), tok(D_MODEL),
                  pl.BlockSpec((1, 6, D_MODEL), lambda b, i: (mod_row_of_batch(b), 0, 0)),
                  once(w_four), once(w_ret), once(w_o)],
        out_specs=tok(D_MODEL),
        out_shape=jax.ShapeDtypeStruct((batch * seq_len, D_MODEL), F32),
        scratch_shapes=[pltpu.VMEM((2 * seq_len, D_MODEL), BF16)],
        compiler_params=pltpu.CompilerParams(dimension_semantics=("parallel", "arbitrary"),
                                             vmem_limit_bytes=VMEM_LIMIT),
        name="fnet_merge",
    )(uf, cs, cls, r, gf, gr, x2d, mod3, w_four, w_ret, w_o)


def _pack_pair(lo_f32, hi_f32):
    lo = lax.bitcast_convert_type(lo_f32.astype(BF16).astype(F32), jnp.uint32)
    hi = lax.bitcast_convert_type(hi_f32.astype(BF16).astype(F32), jnp.uint32)
    return lax.bitcast_convert_type((lo >> 16) | hi, jnp.int32)


def _unpack_pair(words_i32):
    w = lax.bitcast_convert_type(words_i32, jnp.uint32)
    lo = lax.bitcast_convert_type(w << 16, F32)
    hi = lax.bitcast_convert_type(w & jnp.uint32(0xFFFF0000), F32)
    return lo, hi


def _load_token_words(ref, lead, n_tok):
    parts = []
    for s in range(ROW_SLABS):
        idx = (pl.ds(s, n_tok, stride=ROW_SLABS), slice(None))
        parts.append(ref[lead + idx] if lead else ref[idx])
    return jnp.concatenate(parts, axis=1)


def _store_token_words(ref, words, n_tok):
    for s in range(ROW_SLABS):
        ref[pl.ds(s, n_tok, stride=ROW_SLABS), :] = words[:, s * 128:(s + 1) * 128]


def _route(scores, biased):
    tokens = scores.shape[1]
    neg = -jnp.inf
    epg = EXPERTS_PER_GROUP
    iota_g = lax.broadcasted_iota(jnp.int32, (epg, tokens), 0).astype(F32)

    def pick_first_max(cur, iota, size):
        m = jnp.max(cur, axis=0, keepdims=True)
        idx = jnp.min(jnp.where(cur == m, iota, float(size)), axis=0, keepdims=True)
        return m, idx, iota == idx

    group_scores = []
    for g in range(N_EXPERT_GROUPS):
        vals = biased[g * epg:(g + 1) * epg, :]
        m1, _, hit = pick_first_max(vals, iota_g, epg)
        m2 = jnp.max(jnp.where(hit, neg, vals), axis=0, keepdims=True)
        group_scores.append(m1 + m2)
    cur = jnp.concatenate(group_scores, axis=0)
    group_sel = jnp.zeros_like(cur)
    for _ in range(TOPK_GROUPS):
        _, _, hit = pick_first_max(cur, iota_g, N_EXPERT_GROUPS)
        group_sel = jnp.where(hit, 1.0, group_sel)
        cur = jnp.where(hit, neg, cur)
    masked = jnp.concatenate(
        [jnp.where(group_sel[g:g + 1, :] > 0.0, biased[g * epg:(g + 1) * epg, :], neg)
         for g in range(N_EXPERT_GROUPS)], axis=0)
    iota_e = lax.broadcasted_iota(jnp.int32, masked.shape, 0).astype(F32)
    sel = jnp.zeros_like(masked)
    cur = masked
    picks = []
    for _ in range(TOP_K):
        _, idx, hit = pick_first_max(cur, iota_e, N_EXPERTS)
        picks.append(idx)
        sel = jnp.where(hit, 1.0, sel)
        cur = jnp.where(hit, neg, cur)
    w = scores * sel
    return w / jnp.sum(w, axis=0, keepdims=True) * ROUTED_SCALE, sel, picks


def _router_kernel(x_ref, mod_ref, g2_ref, wrt_ref, rb_ref, hp_ref, ek_ref, rk_ref, wt_ref, cnt_ref,
                   run_scr, earlier_scr):
    tm = x_ref.shape[0]

    @pl.when(pl.program_id(0) == 0)
    def _():
        run_scr[...] = jnp.zeros_like(run_scr)
        earlier = (lax.broadcasted_iota(jnp.int32, (tm, tm), 0) < lax.broadcasted_iota(jnp.int32, (tm, tm), 1))
        earlier_scr[...] = jnp.where(earlier, 1.0, 0.0).astype(BF16)

    h = _rms_mod(x_ref[...], g2_ref[...], mod_ref[0, 3:4, :], mod_ref[0, 4:5, :])
    half = D_MODEL // 2
    _store_token_words(hp_ref, _pack_pair(h[:, :half], h[:, half:]), tm)

    def split(a):
        hi = a.astype(BF16)
        return hi, (a - hi.astype(F32)).astype(BF16)

    def dot_nt(a, b):
        return lax.dot_general(a, b, (((1,), (1,)), ((), ())), preferred_element_type=F32)

    h_hi, h_lo = split(h)
    w_hi, w_lo = split(wrt_ref[...])
    logits_t = dot_nt(w_hi, h_hi) + (dot_nt(w_hi, h_lo) + dot_nt(w_lo, h_hi))
    scores = jax.nn.sigmoid(logits_t)
    comb_t, sel, picks = _route(scores, scores + rb_ref[...])

    rank_t = _dot(sel.astype(BF16), earlier_scr[...]) + run_scr[...]
    run_scr[...] += jnp.sum(sel, axis=1, keepdims=True)
    cnt_ref[...] = jnp.broadcast_to(run_scr[...], cnt_ref.shape)

    iota_e = lax.broadcasted_iota(jnp.int32, sel.shape, 0).astype(F32)
    ranks, weights = [], []
    for idx in picks:
        hit = iota_e == idx
        ranks.append(jnp.sum(jnp.where(hit, rank_t, 0.0), axis=0, keepdims=True))
        weights.append(jnp.sum(jnp.where(hit, comb_t, 0.0), axis=0, keepdims=True))
    ek_ref[...] = jnp.concatenate(picks, axis=0).astype(jnp.int32)
    rk_ref[...] = jnp.concatenate(ranks, axis=0).astype(jnp.int32)
    w_rep = jnp.concatenate([jnp.broadcast_to(w, (SC_LANES, tm)) for w in weights], axis=0)
    wt_ref[...] = w_rep.T


def _router(x1, mod3, norm2_g, w_router_t, router_bias, seq_len, mod_row_of_batch):
    t = x1.shape[0]
    tm = TM_ROUTER

    def mod_idx(i):
        return (mod_row_of_batch((i * tm) // seq_len), 0, 0)

    def full(a):
        return pl.BlockSpec(a.shape, lambda i: (0,) * a.ndim)

    return pl.pallas_call(
        _router_kernel,
        grid=(t // tm,),
        in_specs=[pl.BlockSpec((tm, D_MODEL), lambda i: (i, 0)),
                  pl.BlockSpec((1, 6, D_MODEL), mod_idx),
                  full(norm2_g), full(w_router_t), full(router_bias)],
        out_specs=[pl.BlockSpec((tm * ROW_SLABS, 128), lambda i: (i, 0)),
                   pl.BlockSpec((TOP_K, tm), lambda i: (0, i)),
                   pl.BlockSpec((TOP_K, tm), lambda i: (0, i)),
                   pl.BlockSpec((tm, 128), lambda i: (i, 0)),
                   pl.BlockSpec((N_EXPERTS, 128), lambda i: (0, 0))],
        out_shape=[jax.ShapeDtypeStruct((t * ROW_SLABS, 128), jnp.int32),
                   jax.ShapeDtypeStruct((TOP_K, t), jnp.int32),
                   jax.ShapeDtypeStruct((TOP_K, t), jnp.int32),
                   jax.ShapeDtypeStruct((t, 128), F32),
                   jax.ShapeDtypeStruct((N_EXPERTS, 128), F32)],
        scratch_shapes=[pltpu.VMEM((N_EXPERTS, 1), F32), pltpu.VMEM((tm, tm), BF16)],
        compiler_params=pltpu.CompilerParams(dimension_semantics=("arbitrary",),
                                             vmem_limit_bytes=VMEM_LIMIT),
        name="router",
    )(x1, mod3, norm2_g, w_router_t, router_bias)


def _plan_kernel(ek_ref, rk_ref, cnt_ref, pos_ref, texp_ref, nused_ref, tend_ref, *, expert_rows):
    rows = float(expert_rows)
    cnt = cnt_ref[:, 0:1]
    tiles = jnp.floor((cnt + (rows - 1.0)) / rows)
    before = (lax.broadcasted_iota(jnp.int32, (N_EXPERTS, N_EXPERTS), 1)
              < lax.broadcasted_iota(jnp.int32, (N_EXPERTS, N_EXPERTS), 0))
    tile_start = jnp.dot(jnp.where(before, 1.0, 0.0), jnp.broadcast_to(tiles, (N_EXPERTS, 128)),
                         precision=lax.Precision.HIGHEST, preferred_element_type=F32)[:, 0:1]
    tile_end = tile_start + tiles
    row_start = tile_start * rows

    ek = ek_ref[...]
    pos = rk_ref[...].astype(F32)
    tile_id = lax.broadcasted_iota(jnp.int32, texp_ref.shape, 1).astype(F32)
    texp = jnp.zeros(texp_ref.shape, F32)
    for e in range(N_EXPERTS):
        pos = pos + jnp.where(ek == e, row_start[e:e + 1, :], 0.0)
        texp = texp + jnp.where(tile_id >= tile_end[e:e + 1, :], 1.0, 0.0)
    pos_ref[...] = pos.astype(jnp.int32)
    texp_ref[...] = jnp.minimum(texp, N_EXPERTS - 1.0).astype(jnp.int32)
    nused_ref[...] = jnp.broadcast_to(tile_end[N_EXPERTS - 1:N_EXPERTS, :], nused_ref.shape).astype(jnp.int32)
    tend_ref[...] = jnp.broadcast_to(tile_end, tend_ref.shape).astype(jnp.int32)


def _plan(ek, rk, cnt, n_tiles_pad, expert_rows):
    t = ek.shape[1]

    def full(shape):
        return pl.BlockSpec(shape, lambda: (0,) * len(shape))

    return pl.pallas_call(
        functools.partial(_plan_kernel, expert_rows=expert_rows),
        in_specs=[full(ek.shape), full(rk.shape), full(cnt.shape)],
        out_specs=[full((TOP_K, t)), full((1, n_tiles_pad)), full((1, 128)), full((N_EXPERTS, 128))],
        out_shape=[jax.ShapeDtypeStruct((TOP_K, t), jnp.int32),
                   jax.ShapeDtypeStruct((1, n_tiles_pad), jnp.int32),
                   jax.ShapeDtypeStruct((1, 128), jnp.int32),
                   jax.ShapeDtypeStruct((N_EXPERTS, 128), jnp.int32)],
        compiler_params=pltpu.CompilerParams(vmem_limit_bytes=VMEM_LIMIT),
        name="plan",
    )(ek, rk, cnt)


def _sc_mesh():
    return plsc.VectorSubcoreMesh(core_axis_name="c", subcore_axis_name="s")


def _sc_pack_weight_halves(w):
    e, k, n = w.shape
    k_half = k // 2
    rb = SC_PACK_BLOCK_WORDS // n
    units_per_expert = k_half // rb
    per_w = (e * units_per_expert) // SC_WORKERS
    lanes = SC_LANES

    @functools.partial(
        pl.kernel, out_type=jax.ShapeDtypeStruct((e * k_half, n), jnp.int32), mesh=_sc_mesh(),
        scratch_types=[pltpu.VMEM((rb, n), F32), pltpu.VMEM((rb, n), F32), pltpu.VMEM((rb, n), jnp.int32)],
        compiler_params=pltpu.CompilerParams(needs_layout_passes=False))
    def kern(w_hbm, out_hbm, a_v, b_v, o_v):
        wid = lax.axis_index("s") * SC_CORES + lax.axis_index("c")

        @pl.loop(0, per_w)
        def _(j):
            unit = wid * per_w + j
            expert = unit // units_per_expert
            blk = unit % units_per_expert
            row_a = expert * k + blk * rb
            pltpu.sync_copy(w_hbm.at[pl.ds(row_a, rb)], a_v)
            pltpu.sync_copy(w_hbm.at[pl.ds(row_a + k_half, rb)], b_v)

            @pl.loop(0, rb)
            def _(r):
                @plsc.parallel_loop(0, n, step=lanes, unroll=4)
                def _(c):
                    both = plsc.pack(a_v[r, pl.ds(c, lanes)], b_v[r, pl.ds(c, lanes)],
                                     format=plsc.PackFormat.INTERLEAVED)
                    o_v[r, pl.ds(c, lanes)] = plsc.bitcast(both, jnp.int32)

            pltpu.sync_copy(o_v, out_hbm.at[pl.ds(expert * k_half + blk * rb, rb)])

    return kern(w.reshape(e * k, n)).reshape(e, k_half, n)


def _sc_dispatch(rows, pos3, n_out, after=()):
    t = rows.shape[0]
    ch = SC_CHUNK
    per_w = (t // ch) // SC_WORKERS

    @functools.partial(
        pl.kernel, out_type=jax.ShapeDtypeStruct((n_out,) + rows.shape[1:], jnp.int32), mesh=_sc_mesh(),
        scratch_types=[pltpu.VMEM((TOP_K, ch), jnp.int32), pltpu.VMEM((ch,) + rows.shape[1:], jnp.int32),
                       pltpu.SemaphoreType.DMA])
    def k(rows_hbm, pos_hbm, *rest):
        out_hbm, idx_v, rows_v, sem = rest[len(after):]
        wid = lax.axis_index("s") * SC_CORES + lax.axis_index("c")

        @pl.loop(0, per_w)
        def _(j):
            c = wid * per_w + j
            pltpu.sync_copy(pos_hbm.at[c], idx_v)
            pltpu.sync_copy(rows_hbm.at[pl.ds(c * ch, ch)], rows_v)
            copies = [pltpu.async_copy(rows_v, out_hbm.at[idx_v.at[kk]], sem) for kk in range(TOP_K)]
            for cp in copies:
                cp.wait()

    return k(rows, pos3, *after)


def _sc_combine(table, pos3, wtok, t):
    ch = SC_CHUNK
    sub = SC_COMBINE_TOKENS
    lanes = SC_LANES
    slabs = ROW_SLABS
    per_w = (t // ch) // SC_WORKERS
    subs_per_chunk = ch // sub
    n_steps = per_w * subs_per_chunk

    @functools.partial(
        pl.kernel, out_type=jax.ShapeDtypeStruct((t, slabs, 128), jnp.int32), mesh=_sc_mesh(),
        scratch_types=[pltpu.VMEM((per_w, TOP_K, ch), jnp.int32),
                       pltpu.VMEM((2, TOP_K, sub, slabs, 128), jnp.int32),
                       pltpu.VMEM((2, sub, 128), F32),
                       pltpu.VMEM((sub, slabs, 128), jnp.int32),
                       pltpu.SemaphoreType.DMA((2,))],
        compiler_params=pltpu.CompilerParams(needs_layout_passes=False))
    def k(tab_hbm, pos_hbm, w_hbm, out_hbm, idx_v, rows_v, w_v, out_v, sem):
        wid = lax.axis_index("s") * SC_CORES + lax.axis_index("c")
        for j in range(per_w):
            pltpu.sync_copy(pos_hbm.at[wid * per_w + j], idx_v.at[j])

        def first_token(step):
            return (wid * per_w + step // subs_per_chunk) * ch + (step % subs_per_chunk) * sub

        def copies(step, slot):
            j = step // subs_per_chunk
            s = step % subs_per_chunk
            idx = [idx_v.at[j, kk, pl.ds(s * sub, sub)] for kk in range(TOP_K)]
            return ([pltpu.make_async_copy(tab_hbm.at[idx[kk]], rows_v.at[slot, kk], sem.at[slot])
                     for kk in range(TOP_K)]
                    + [pltpu.make_async_copy(w_hbm.at[pl.ds(first_token(step), sub)], w_v.at[slot], sem.at[slot])])

        for cp in copies(0, 0):
            cp.start()

        @pl.loop(0, n_steps)
        def _(step):
            slot = step % 2

            @pl.when(step + 1 < n_steps)
            def _():
                for cp in copies(step + 1, 1 - slot):
                    cp.start()

            for cp in copies(step, slot):
                cp.wait()

            @pl.loop(0, sub)
            def _(tt):
                wk = [w_v[slot, tt, pl.ds(kk * lanes, lanes)] for kk in range(TOP_K)]
                for sl in range(slabs):
                    @plsc.parallel_loop(0, 128, step=lanes, unroll=8)
                    def _(off):
                        acc_lo = jnp.zeros((lanes,), F32)
                        acc_hi = jnp.zeros((lanes,), F32)
                        for kk in range(TOP_K):
                            word = rows_v[slot, kk, tt, sl, pl.ds(off, lanes)]
                            lo, hi = plsc.unpack(plsc.bitcast(word, BF16), format=plsc.PackFormat.INTERLEAVED)
                            acc_lo = acc_lo + wk[kk] * lo
                            acc_hi = acc_hi + wk[kk] * hi
                        both = plsc.pack(acc_lo, acc_hi, format=plsc.PackFormat.INTERLEAVED)
                        out_v[tt, sl, pl.ds(off, lanes)] = plsc.bitcast(both, jnp.int32)

            pltpu.sync_copy(out_v, out_hbm.at[pl.ds(first_token(step), sub)])

    return k(table, pos3, wtok)


def _experts_kernel(texp_ref, nused_ref, tend_ref, xs_ref, weg_hbm, weu_hbm, wed_hbm, ys_ref,
                    wg_scr, wu_scr, wd_scr, wg_buf, wu_buf, wd_buf, sem, group_scr, *, expert_rows):
    step = pl.program_id(0)
    rows = expert_rows
    tiles_per_step = EXPERT_TILES_PER_STEP
    half = D_MODEL // 2
    n_used = nused_ref[0]

    def weight_copies(e, slot):
        return [pltpu.make_async_copy(weg_hbm.at[e], wg_buf.at[slot], sem.at[slot, 0]),
                pltpu.make_async_copy(weu_hbm.at[e], wu_buf.at[slot], sem.at[slot, 1]),
                pltpu.make_async_copy(wed_hbm.at[e], wd_buf.at[slot], sem.at[slot, 2])]

    def next_group(e):
        tile = tend_ref[e]
        return texp_ref[jnp.minimum(tile, n_used - 1)], tile < n_used

    def start_weights(e, slot, exists):
        @pl.when(exists)
        def _():
            for cp in weight_copies(e, slot):
                cp.start()

    @pl.when(step == 0)
    def _():
        group_scr[0] = 0
        e, exists = texp_ref[0], True
        for slot in range(WEIGHT_SLOTS - 1):
            start_weights(e, slot, exists)
            nxt, has_next = next_group(e)
            e, exists = nxt, exists & has_next

    def row_tile(tile, x_view, y_view):
        expert = texp_ref[tile]
        used = tile < n_used
        new_expert = (tile == 0) | (expert != texp_ref[jnp.maximum(tile - 1, 0)])

        @pl.when(used & new_expert)
        def _():
            group = group_scr[0]
            slot = group % WEIGHT_SLOTS
            ahead, exists = expert, True
            for _ in range(WEIGHT_SLOTS - 1):
                nxt, has_next = next_group(ahead)
                ahead, exists = nxt, exists & has_next
            start_weights(ahead, (group + WEIGHT_SLOTS - 1) % WEIGHT_SLOTS, exists)

            for cp in weight_copies(expert, slot):
                cp.wait()
            for scr, buf in ((wg_scr, wg_buf), (wu_scr, wu_buf), (wd_scr, wd_buf)):
                top, bottom = _unpack_pair(buf[slot])
                k_half = top.shape[0]
                scr[0:k_half, :] = top.astype(BF16)
                scr[k_half:2 * k_half, :] = bottom.astype(BF16)
            group_scr[0] = group + 1

        @pl.when(used)
        def _():
            lo, hi = _unpack_pair(_load_token_words(x_view, (), rows))
            lo = lo.astype(BF16)
            hi = hi.astype(BF16)
            g = _dot(lo, wg_scr[0:half, :]) + _dot(hi, wg_scr[half:D_MODEL, :])
            u = _dot(lo, wu_scr[0:half, :]) + _dot(hi, wu_scr[half:D_MODEL, :])
            y = _dot((_silu(g) * u).astype(BF16), wd_scr[...])
            _store_token_words(y_view, _pack_pair(y[:, :half], y[:, half:]), rows)

        @pl.when(jnp.logical_not(used) & (step == (n_used - 1) // tiles_per_step))
        def _():
            y_view[...] = jnp.zeros_like(y_view)

    for s in range(tiles_per_step):
        view = pl.ds(s * rows * ROW_SLABS, rows * ROW_SLABS)
        row_tile(step * tiles_per_step + s, xs_ref.at[view], ys_ref.at[view])


def _experts(texp, nused, tend, xs2d, weg, weu, wed, n_tiles, expert_rows):
    tiles_per_step = EXPERT_TILES_PER_STEP
    block = (tiles_per_step * expert_rows * ROW_SLABS, 128)
    hbm = pl.BlockSpec(memory_space=pl.ANY)

    def block_idx(j, te, nu, tn):
        return (jnp.minimum(j, (nu[0] - 1) // tiles_per_step), 0)

    grid_spec = pltpu.PrefetchScalarGridSpec(
        num_scalar_prefetch=3,
        grid=(n_tiles // tiles_per_step,),
        in_specs=[pl.BlockSpec(block, block_idx), hbm, hbm, hbm],
        out_specs=pl.BlockSpec(block, block_idx),
        scratch_shapes=[pltpu.VMEM((D_MODEL, EXPERT_DIM), BF16),
                        pltpu.VMEM((D_MODEL, EXPERT_DIM), BF16),
                        pltpu.VMEM((EXPERT_DIM, D_MODEL), BF16),
                        pltpu.VMEM((WEIGHT_SLOTS,) + weg.shape[1:], jnp.int32),
                        pltpu.VMEM((WEIGHT_SLOTS,) + weu.shape[1:], jnp.int32),
                        pltpu.VMEM((WEIGHT_SLOTS,) + wed.shape[1:], jnp.int32),
                        pltpu.SemaphoreType.DMA((WEIGHT_SLOTS, 3)),
                        pltpu.SMEM((1,), jnp.int32)],
    )
    return pl.pallas_call(
        functools.partial(_experts_kernel, expert_rows=expert_rows),
        grid_spec=grid_spec,
        out_shape=jax.ShapeDtypeStruct(xs2d.shape, jnp.int32),
        compiler_params=pltpu.CompilerParams(dimension_semantics=("arbitrary",),
                                             vmem_limit_bytes=VMEM_LIMIT),
        name="experts",
    )(texp, nused, tend, xs2d, weg, weu, wed)


def _final_kernel(x_ref, routed_ref, mod_ref, g2_ref, wsg_ref, wsu_ref, wsd_ref, fng_ref, o_ref):
    tm = x_ref.shape[0]
    x = x_ref[...]
    hb = _rms_mod(x, g2_ref[...], mod_ref[0, 3:4, :], mod_ref[0, 4:5, :]).astype(BF16)
    shared = _dot((_silu(_dot(hb, wsg_ref[...])) * _dot(hb, wsu_ref[...])).astype(BF16), wsd_ref[...])
    routed = jnp.concatenate(_unpack_pair(_load_token_words(routed_ref, (), tm)), axis=1)
    y = x + mod_ref[0, 5:6, :] * (routed + shared)
    ms = jnp.mean(y * y, axis=-1, keepdims=True)
    o_ref[...] = y * lax.rsqrt(ms + EPS) * fng_ref[...]


def _final(x1, routed2d, mod3, norm2_g, wsg, wsu, wsd, final_g, seq_len, mod_row_of_batch):
    t = x1.shape[0]
    tm = TM_FINAL

    def mod_idx(i):
        return (mod_row_of_batch((i * tm) // seq_len), 0, 0)

    def full(a):
        return pl.BlockSpec(a.shape, lambda i: (0,) * a.ndim)

    return pl.pallas_call(
        _final_kernel,
        grid=(t // tm,),
        in_specs=[pl.BlockSpec((tm, D_MODEL), lambda i: (i, 0)),
                  pl.BlockSpec((tm * ROW_SLABS, 128), lambda i: (i, 0)),
                  pl.BlockSpec((1, 6, D_MODEL), mod_idx),
                  full(norm2_g), full(wsg), full(wsu), full(wsd), full(final_g)],
        out_specs=pl.BlockSpec((tm, D_MODEL), lambda i: (i, 0)),
        out_shape=jax.ShapeDtypeStruct((t, D_MODEL), F32),
        compiler_params=pltpu.CompilerParams(dimension_semantics=("parallel",),
                                             vmem_limit_bytes=VMEM_LIMIT),
        name="final",
    )(x1, routed2d, mod3, norm2_g, wsg, wsu, wsd, final_g)


def _moe(x1, mod3, lw, seq_len, mod_row_of_batch):
    t = x1.shape[0]
    expert_rows = min(MAX_EXPERT_ROWS, TOP_K * t // N_EXPERTS // 2)
    n_tiles = TOP_K * t // expert_rows + N_EXPERTS
    n_tiles_pad = -(-n_tiles // 128) * 128
    hp2d, ek, rk, wtok, cnt = _router(x1, mod3, lw["norm2_g"], lw["w_router_t"], lw["router_bias"],
                                      seq_len, mod_row_of_batch)
    pos, texp, nused, tend = _plan(ek, rk, cnt, n_tiles_pad, expert_rows)
    pos3 = pos.reshape(TOP_K, t // SC_CHUNK, SC_CHUNK).transpose(1, 0, 2)
    xs = _sc_dispatch(hp2d.reshape(t, ROW_SLABS, 128), pos3, n_tiles * expert_rows,
                      after=(lw["weg"], lw["weu"], lw["wed"]))
    ys2d = _experts(texp.reshape(-1), nused.reshape(-1), tend[:, 0], xs.reshape(-1, 128),
                    lw["weg"], lw["weu"], lw["wed"], n_tiles, expert_rows)
    routed = _sc_combine(ys2d.reshape(-1, ROW_SLABS, 128), pos3, wtok, t)
    return _final(x1, routed.reshape(t * ROW_SLABS, 128), mod3, lw["norm2_g"],
                  lw["wsg"], lw["wsu"], lw["wsd"], lw["final_g"], seq_len, mod_row_of_batch)


def _dft_tables(seq_len):
    gd = FOURIER_GROUP_DIM
    kc = np.arange(gd)
    ang_c = ((kc[:, None] * kc[None, :]) % gd) * (2.0 * math.pi / gd)
    cs = np.concatenate([np.cos(ang_c), np.sin(ang_c)], axis=1) * (gd ** -0.5)
    kl = np.arange(seq_len)
    ang_l = ((kl[:, None] * kl[None, :]) % seq_len) * (2.0 * math.pi / seq_len)
    cls = np.concatenate([np.cos(ang_l), -np.sin(ang_l)], axis=1) * (seq_len ** -0.5)
    return jnp.asarray(cs.astype(np.float32), dtype=BF16), jnp.asarray(cls.astype(np.float32), dtype=BF16)


def _rope_tables(length):
    rows = length // GRID_W
    r = np.repeat(np.arange(rows, dtype=np.float32), GRID_W)
    col = np.tile(np.arange(GRID_W, dtype=np.float32), rows)
    nf = RET_HEAD_DIM // 4
    inv = (np.float32(ROPE_BASE) ** (-np.arange(nf, dtype=np.float32) / np.float32(nf))).astype(np.float32)
    ar = r[:, None] * inv[None]
    ac = col[:, None] * inv[None]
    ang = np.concatenate([ar, ar, ac, ac], axis=-1).astype(np.float64)
    sign = np.where((np.arange(RET_HEAD_DIM) & nf) == 0, -1.0, 1.0)
    return (jnp.asarray(np.cos(ang).astype(np.float32)),
            jnp.asarray((np.sin(ang) * sign[None, :]).astype(np.float32)))


def _trunk_path(x, mod3, mod_row_of_batch, s0f, s0b, rope, lw):
    batch, seq_len, _ = x.shape
    x2d = x.reshape(batch * seq_len, D_MODEL)
    uf, q, k, v, sg, gf, gr = _inproj(x2d, mod3, lw["norm1_g"], lw["w_in"], seq_len, mod_row_of_batch, rope)
    r, s_f, s_b = _retention(q, k, v, sg, lw["dec"], lw["gn_g"], s0f, s0b, batch, seq_len)
    cs, cls = _dft_tables(seq_len)
    x1 = _fnet_merge(uf, cs, cls, r, gf, gr, x2d, mod3, lw["w_four"], lw["w_ret"], lw["w_o"],
                     batch, seq_len, mod_row_of_batch)
    y = _moe(x1, mod3, lw, seq_len, mod_row_of_batch)
    return y.reshape(batch, seq_len, D_MODEL), s_f, s_b


def kernel(x_prompt, x_sample, state_ret_fwd, state_ret_bwd, c, c_ctx, w_ada, b_ada, norm1_g, norm2_g, w_in,
           ret_decay_fwd, ret_decay_bwd, ret_gn_g, w_four_out, w_ret_out, w_out, w_router, router_bias,
           w_exp_gate, w_exp_up, w_exp_down, w_shared_gate, w_shared_up, w_shared_down, final_norm_g):
    depth = w_ada.shape[0]
    assert depth == 1, "final norm is fused into the last layer's MoE kernel"
    n_ctx, n_lat = x_prompt.shape[0], x_sample.shape[0]
    cond = jnp.concatenate([c_ctx[None, :], c], axis=0)
    cond = jnp.pad(cond, ((0, (-cond.shape[0]) % 8), (0, 0)))
    rope = _rope_tables(x_sample.shape[1])
    zeros = jnp.zeros((n_ctx, N_RET_HEADS, RET_HEAD_DIM, RET_HEAD_DIM), F32)

    layer = 0
    mod = _ada(cond, w_ada[layer], b_ada[layer][None, :])
    mod3 = mod.reshape(mod.shape[0], 6, D_MODEL)
    dec = jnp.stack([ret_decay_fwd[layer], ret_decay_bwd[layer]], axis=1)
    lw = {
        "norm1_g": norm1_g[layer][None, :],
        "norm2_g": norm2_g[layer][None, :],
        "w_in": w_in[layer],
        "dec": jnp.broadcast_to(dec[:, :, None], (N_RET_HEADS, 2, RET_HEAD_DIM)).astype(F32),
        "gn_g": ret_gn_g[layer][None, :],
        "w_four": w_four_out[layer],
        "w_ret": w_ret_out[layer],
        "w_o": w_out[layer],
        "w_router_t": w_router[layer].T,
        "router_bias": router_bias[layer][:, None],
        "weg": _sc_pack_weight_halves(w_exp_gate[layer]),
        "weu": _sc_pack_weight_halves(w_exp_up[layer]),
        "wed": _sc_pack_weight_halves(w_exp_down[layer]),
        "wsg": w_shared_gate[layer].astype(BF16),
        "wsu": w_shared_up[layer].astype(BF16),
        "wsd": w_shared_down[layer].astype(BF16),
        "final_g": final_norm_g[None, :],
    }
    y_prompt, s_f, s_b = _trunk_path(x_prompt, mod3, lambda b: 0, zeros, zeros, None, lw)
    y_sample, _, _ = _trunk_path(x_sample, mod3, lambda b: 1 + b, state_ret_fwd[:, layer],
                                 state_ret_bwd[:, layer], rope, lw)
    return (y_prompt, y_sample, s_f[:, None], s_b[:, None])
```

```python
import functools
import math

import jax
import jax.numpy as jnp
import numpy as np
from jax import lax
from jax.experimental import pallas as pl
from jax.experimental.pallas import tpu as pltpu
from jax.experimental.pallas import tpu_sc as plsc

F32 = jnp.float32
BF16 = jnp.bfloat16

D_MODEL = 1024
GRID_W = 64
N_FOURIER_GROUPS = 8
FOURIER_GROUP_DIM = 128
N_RET_HEADS = 4
RET_HEAD_DIM = 128
RET_WIDTH = N_RET_HEADS * RET_HEAD_DIM
CHUNK = 128
N_EXPERTS = 64
N_EXPERT_GROUPS = 8
EXPERTS_PER_GROUP = N_EXPERTS // N_EXPERT_GROUPS
TOPK_GROUPS = 4
TOP_K = 8
EXPERT_DIM = 256
ROUTED_SCALE = 2.5
ROPE_BASE = 10000.0
EPS = 1e-6
Q_SCALE = RET_HEAD_DIM ** -0.5

_C_UF = (0, 1024)
_C_Q = (1024, 1536)
_C_K = (1536, 2048)
_C_V = (2048, 2560)
_C_G = (2560, 3072)
_C_GF = (3072, 4096)
_C_GR = (4096, 5120)

VMEM_LIMIT = 56 * 1024 * 1024

TM_INPROJ = 1024
TM_ROUTER = 256
FNET_ROWS = 512
TM_FINAL = 1024
EXPERT_TILES_PER_STEP = 4
MAX_EXPERT_ROWS = 512
WEIGHT_SLOTS = 3
ROW_SLABS = 4
SC_CORES = 2
SC_WORKERS = 32
SC_CHUNK = 128
SC_LANES = 16
SC_PACK_BLOCK_WORDS = 16384
SC_COMBINE_TOKENS = 8


def _silu(x):
    return x * jax.nn.sigmoid(x)


def _dot(a, b):
    return jnp.dot(a, b, preferred_element_type=F32)


def _rms_mod(x, g, shift, scale):
    ms = jnp.mean(x * x, axis=-1, keepdims=True)
    y = x * lax.rsqrt(ms + EPS) * g
    return y * (1.0 + scale) + shift


def _ada_kernel(cond_ref, w_ref, b_ref, o_ref):
    s = _silu(cond_ref[...]).astype(BF16)
    o_ref[...] = _dot(s, w_ref[...].astype(BF16)) + b_ref[...]


def _ada(cond, w_ada, b_ada):
    rows, n = cond.shape[0], w_ada.shape[1]
    tn = 1536
    return pl.pallas_call(
        _ada_kernel,
        grid=(n // tn,),
        in_specs=[pl.BlockSpec((rows, D_MODEL), lambda j: (0, 0)),
                  pl.BlockSpec((D_MODEL, tn), lambda j: (0, j)),
                  pl.BlockSpec((1, tn), lambda j: (0, j))],
        out_specs=pl.BlockSpec((rows, tn), lambda j: (0, j)),
        out_shape=jax.ShapeDtypeStruct((rows, n), F32),
        compiler_params=pltpu.CompilerParams(vmem_limit_bytes=VMEM_LIMIT),
        name="ada",
    )(cond, w_ada, b_ada)


def _rope_head(x, cos, sin_signed, first_half):
    partner = jnp.where(first_half, pltpu.roll(x, 96, 1), pltpu.roll(x, 32, 1))
    return x * cos + partner * sin_signed


def _inproj_kernel(*refs, use_rope):
    if use_rope:
        x_ref, mod_ref, g_ref, w_ref, cos_ref, sin_ref = refs[:6]
        outs = refs[6:]
    else:
        x_ref, mod_ref, g_ref, w_ref = refs[:4]
        outs = refs[4:]
    uf_o, q_o, k_o, v_o, sg_o, gf_o, gr_o = outs

    h = _rms_mod(x_ref[...], g_ref[...], mod_ref[0, 0:1, :], mod_ref[0, 1:2, :])
    hb = h.astype(BF16)

    def proj(cols):
        return _dot(hb, w_ref[:, cols[0]:cols[1]].astype(BF16))

    uf_o[...] = proj(_C_UF).astype(BF16)
    q = proj(_C_Q)
    k = proj(_C_K)
    if use_rope:
        cos = cos_ref[...]
        sin_signed = sin_ref[...]
        lane = lax.broadcasted_iota(jnp.int32, cos.shape, 1)
        first_half = (lane & 32) == 0
        for hd in range(N_RET_HEADS):
            sl = slice(hd * RET_HEAD_DIM, (hd + 1) * RET_HEAD_DIM)
            q_o[:, sl] = (_rope_head(q[:, sl], cos, sin_signed, first_half) * Q_SCALE).astype(BF16)
            k_o[:, sl] = _rope_head(k[:, sl], cos, sin_signed, first_half).astype(BF16)
    else:
        q_o[...] = (q * Q_SCALE).astype(BF16)
        k_o[...] = k.astype(BF16)
    v_o[...] = proj(_C_V).astype(BF16)
    sg_o[...] = _silu(proj(_C_G)).astype(BF16)
    gf_o[...] = jax.nn.sigmoid(proj(_C_GF)).astype(BF16)
    gr_o[...] = jax.nn.sigmoid(proj(_C_GR)).astype(BF16)


def _inproj(x2d, mod3, norm_g, w_in_f32, seq_len, mod_row_of_batch, rope):
    t = x2d.shape[0]
    tm = TM_INPROJ
    tiles_per_seq = max(seq_len // tm, 1)

    def mod_idx(i):
        return (mod_row_of_batch((i * tm) // seq_len), 0, 0)

    in_specs = [pl.BlockSpec((tm, D_MODEL), lambda i: (i, 0)),
                pl.BlockSpec((1, 6, D_MODEL), mod_idx),
                pl.BlockSpec((1, D_MODEL), lambda i: (0, 0)),
                pl.BlockSpec(w_in_f32.shape, lambda i: (0, 0), pipeline_mode=pl.Buffered(1))]
    args = [x2d, mod3, norm_g, w_in_f32]
    if rope is not None:
        in_specs += [pl.BlockSpec((tm, RET_HEAD_DIM), lambda i: (i % tiles_per_seq, 0))] * 2
        args += list(rope)
    widths = [1024, RET_WIDTH, RET_WIDTH, RET_WIDTH, RET_WIDTH, 1024, 1024]
    return pl.pallas_call(
        functools.partial(_inproj_kernel, use_rope=rope is not None),
        grid=(t // tm,),
        in_specs=in_specs,
        out_specs=[pl.BlockSpec((tm, w), lambda i: (i, 0)) for w in widths],
        out_shape=[jax.ShapeDtypeStruct((t, w), BF16) for w in widths],
        compiler_params=pltpu.CompilerParams(dimension_semantics=("parallel",),
                                             vmem_limit_bytes=VMEM_LIMIT),
        name="inproj",
    )(*args)


def _retention_kernel(q_ref, k_ref, v_ref, sg_ref, dec_ref, gn_ref, s0f_ref, s0b_ref,
                      r_ref, sfo_ref, sbo_ref, tab_scr, gc_scr):
    n_chunks = q_ref.shape[0] // CHUNK
    hd = RET_HEAD_DIM

    @pl.when(pl.program_id(0) == 0)
    def _():
        row = lax.broadcasted_iota(jnp.int32, (CHUNK, CHUNK), 0).astype(F32)
        col = lax.broadcasted_iota(jnp.int32, (CHUNK, CHUNK), 1).astype(F32)
        diff = row - col
        for h in range(N_RET_HEADS):
            dec = dec_ref[h]
            lg = jnp.minimum(dec, 0.0) - jnp.log1p(jnp.exp(-jnp.abs(dec)))
            lgf = lg[0:1, :]
            lgb = lg[1:2, :]
            tab_scr[h, 0] = jnp.exp(jnp.where(diff >= 0, lgf * diff, lgb * (-diff)))
            tab_scr[h, 1] = jnp.exp(lgf * (row + 1.0))
            tab_scr[h, 2] = jnp.exp(lgb * (CHUNK - row))
            tab_scr[h, 3] = jnp.exp(lgf * (CHUNK - 1.0 - col))
            tab_scr[h, 4] = jnp.exp(lgb * col)
            gc_scr[h] = jnp.exp(lg * CHUNK)

    def rows(n):
        return slice(n * CHUNK, (n + 1) * CHUNK)

    for h in range(N_RET_HEADS):
        cols = slice(h * hd, (h + 1) * hd)
        decay, qw_f, qw_b, kwt_f, kwt_b = (tab_scr[h, i] for i in range(5))
        gc = gc_scr[h]
        gc_f = gc[0:1, :]
        gc_b = gc[1:2, :]

        kv_f, kv_b = [], []
        for n in range(n_chunks):
            kt = k_ref[rows(n), cols].astype(F32).T
            vn = v_ref[rows(n), cols]
            kv_f.append(_dot((kt * kwt_f).astype(BF16), vn))
            kv_b.append(_dot((kt * kwt_b).astype(BF16), vn))

        s = s0f_ref[h]
        prev_f = []
        for n in range(n_chunks):
            prev_f.append(s.astype(BF16))
            s = gc_f * s + kv_f[n]
        sfo_ref[h] = s
        s = s0b_ref[h]
        prev_b = [None] * n_chunks
        for n in reversed(range(n_chunks)):
            prev_b[n] = s.astype(BF16)
            s = gc_b * s + kv_b[n]
        sbo_ref[h] = s

        gn = gn_ref[:, cols]
        for n in range(n_chunks):
            qn = q_ref[rows(n), cols]
            qf = qn.astype(F32)
            scores = lax.dot_general(qn, k_ref[rows(n), cols], (((1,), (1,)), ((), ())),
                                     preferred_element_type=F32)
            o = _dot((scores * decay).astype(BF16), v_ref[rows(n), cols])
            o = o + _dot((qf * qw_f).astype(BF16), prev_f[n])
            o = o + _dot((qf * qw_b).astype(BF16), prev_b[n])
            mu = jnp.mean(o, axis=-1, keepdims=True)
            d = o - mu
            var = jnp.mean(d * d, axis=-1, keepdims=True)
            on = d * lax.rsqrt(var + EPS) * gn
            r_ref[rows(n), cols] = (on * sg_ref[rows(n), cols].astype(F32)).astype(BF16)


def _retention(q, k, v, sg, dec, gn_g, s0f, s0b, batch, seq_len):
    hd = RET_HEAD_DIM
    tok_spec = pl.BlockSpec((seq_len, RET_WIDTH), lambda b: (b, 0))
    st_spec = pl.BlockSpec((None, N_RET_HEADS, hd, hd), lambda b: (b, 0, 0, 0))
    st_shape = jax.ShapeDtypeStruct((batch, N_RET_HEADS, hd, hd), F32)
    return pl.pallas_call(
        _retention_kernel,
        grid=(batch,),
        in_specs=[tok_spec, tok_spec, tok_spec, tok_spec,
                  pl.BlockSpec(dec.shape, lambda b: (0, 0, 0)),
                  pl.BlockSpec(gn_g.shape, lambda b: (0, 0)),
                  st_spec, st_spec],
        out_specs=[tok_spec, st_spec, st_spec],
        out_shape=[jax.ShapeDtypeStruct((batch * seq_len, RET_WIDTH), BF16), st_shape, st_shape],
        scratch_shapes=[pltpu.VMEM((N_RET_HEADS, 5, CHUNK, CHUNK), F32),
                        pltpu.VMEM((N_RET_HEADS, 2, hd), F32)],
        compiler_params=pltpu.CompilerParams(dimension_semantics=("arbitrary",),
                                             vmem_limit_bytes=VMEM_LIMIT),
        name="retention",
    )(q, k, v, sg, dec, gn_g, s0f, s0b)


def _fnet_merge_kernel(uf_ref, cs_ref, cls_ref, r_ref, gf_ref, gr_ref, x_ref, mod_ref, wf_ref, wr_ref, wo_ref,
                       o_ref, xcs_ref):
    seq_len = uf_ref.shape[0]
    gd = FOURIER_GROUP_DIM

    @pl.when(pl.program_id(1) == 0)
    def _():
        for g in range(N_FOURIER_GROUPS):
            x = _dot(uf_ref[:, g * gd:(g + 1) * gd], cs_ref[...])
            xcs_ref[0:seq_len, g * gd:(g + 1) * gd] = x[:, :gd].astype(BF16)
            xcs_ref[seq_len:2 * seq_len, g * gd:(g + 1) * gd] = x[:, gd:].astype(BF16)

    f_mix = _dot(cls_ref[...], xcs_ref[...]).astype(BF16)
    f_out = _dot(f_mix, wf_ref[...].astype(BF16))
    r_out = _dot(r_ref[...], wr_ref[...].astype(BF16))
    merged = gf_ref[...].astype(F32) * f_out + gr_ref[...].astype(F32) * r_out
    mix = _dot(merged.astype(BF16), wo_ref[...].astype(BF16))
    o_ref[...] = x_ref[...] + mod_ref[0, 2:3, :] * mix


def _fnet_merge(uf, cs, cls, r, gf, gr, x2d, mod3, w_four, w_ret, w_o, batch, seq_len, mod_row_of_batch):
    rb = min(FNET_ROWS, seq_len)
    nr = seq_len // rb

    def tok(w):
        return pl.BlockSpec((rb, w), lambda b, i: (b * nr + i, 0))

    def full(a):
        return pl.BlockSpec(a.shape, lambda b, i: (0, 0))

    def once(a):
        return pl.BlockSpec(a.shape, lambda b, i: (0, 0), pipeline_mode=pl.Buffered(1))

    return pl.pallas_call(
        _fnet_merge_kernel,
        grid=(batch, nr),
        in_specs=[pl.BlockSpec((seq_len, D_MODEL), lambda b, i: (b, 0)),
                  full(cs),
                  pl.BlockSpec((rb, 2 * seq_len), lambda b, i: (i, 0)),
                  tok(RET_WIDTH), tok(D_MODEL), tok(D_MODEL), tok(D_MODEL),
                  pl.BlockSpec((1, 6, D_MODEL), lambda b, i: (mod_row_of_batch(b), 0, 0)),
                  once(w_four), once(w_ret), once(w_o)],
        out_specs=tok(D_MODEL),
        out_shape=jax.ShapeDtypeStruct((batch * seq_len, D_MODEL), F32),
        scratch_shapes=[pltpu.VMEM((2 * seq_len, D_MODEL), BF16)],
        compiler_params=pltpu.CompilerParams(dimension_semantics=("parallel", "arbitrary"),
                                             vmem_limit_bytes=VMEM_LIMIT),
        name="fnet_merge",
    )(uf, cs, cls, r, gf, gr, x2d, mod3, w_four, w_ret, w_o)


def _pack_pair(lo_f32, hi_f32):
    lo = lax.bitcast_convert_type(lo_f32.astype(BF16).astype(F32), jnp.uint32)
    hi = lax.bitcast_convert_type(hi_f32.astype(BF16).astype(F32), jnp.uint32)
    return lax.bitcast_convert_type((lo >> 16) | hi, jnp.int32)


def _unpack_pair(words_i32):
    w = lax.bitcast_convert_type(words_i32, jnp.uint32)
    lo = lax.bitcast_convert_type(w << 16, F32)
    hi = lax.bitcast_convert_type(w & jnp.uint32(0xFFFF0000), F32)
    return lo, hi


def _load_token_words(ref, lead, n_tok):
    parts = []
    for s in range(ROW_SLABS):
        idx = (pl.ds(s, n_tok, stride=ROW_SLABS), slice(None))
        parts.append(ref[lead + idx] if lead else ref[idx])
    return jnp.concatenate(parts, axis=1)


def _store_token_words(ref, words, n_tok):
    for s in range(ROW_SLABS):
        ref[pl.ds(s, n_tok, stride=ROW_SLABS), :] = words[:, s * 128:(s + 1) * 128]


def _route(scores, biased):
    tokens = scores.shape[1]
    neg = -jnp.inf
    epg = EXPERTS_PER_GROUP
    iota_g = lax.broadcasted_iota(jnp.int32, (epg, tokens), 0).astype(F32)

    def pick_first_max(cur, iota, size):
        m = jnp.max(cur, axis=0, keepdims=True)
        idx = jnp.min(jnp.where(cur == m, iota, float(size)), axis=0, keepdims=True)
        return m, idx, iota == idx

    group_scores = []
    for g in range(N_EXPERT_GROUPS):
        vals = biased[g * epg:(g + 1) * epg, :]
        m1, _, hit = pick_first_max(vals, iota_g, epg)
        m2 = jnp.max(jnp.where(hit, neg, vals), axis=0, keepdims=True)
        group_scores.append(m1 + m2)
    cur = jnp.concatenate(group_scores, axis=0)
    group_sel = jnp.zeros_like(cur)
    for _ in range(TOPK_GROUPS):
        _, _, hit = pick_first_max(cur, iota_g, N_EXPERT_GROUPS)
        group_sel = jnp.where(hit, 1.0, group_sel)
        cur = jnp.where(hit, neg, cur)
    masked = jnp.concatenate(
        [jnp.where(group_sel[g:g + 1, :] > 0.0, biased[g * epg:(g + 1) * epg, :], neg)
         for g in range(N_EXPERT_GROUPS)], axis=0)
    iota_e = lax.broadcasted_iota(jnp.int32, masked.shape, 0).astype(F32)
    sel = jnp.zeros_like(masked)
    cur = masked
    picks = []
    for _ in range(TOP_K):
        _, idx, hit = pick_first_max(cur, iota_e, N_EXPERTS)
        picks.append(idx)
        sel = jnp.where(hit, 1.0, sel)
        cur = jnp.where(hit, neg, cur)
    w = scores * sel
    return w / jnp.sum(w, axis=0, keepdims=True) * ROUTED_SCALE, sel, picks


def _router_kernel(x_ref, mod_ref, g2_ref, wrt_ref, rb_ref, hp_ref, ek_ref, rk_ref, wt_ref, cnt_ref,
                   run_scr, earlier_scr):
    tm = x_ref.shape[0]

    @pl.when(pl.program_id(0) == 0)
    def _():
        run_scr[...] = jnp.zeros_like(run_scr)
        earlier = (lax.broadcasted_iota(jnp.int32, (tm, tm), 0) < lax.broadcasted_iota(jnp.int32, (tm, tm), 1))
        earlier_scr[...] = jnp.where(earlier, 1.0, 0.0).astype(BF16)

    h = _rms_mod(x_ref[...], g2_ref[...], mod_ref[0, 3:4, :], mod_ref[0, 4:5, :])
    half = D_MODEL // 2
    _store_token_words(hp_ref, _pack_pair(h[:, :half], h[:, half:]), tm)

    def split(a):
        hi = a.astype(BF16)
        return hi, (a - hi.astype(F32)).astype(BF16)

    def dot_nt(a, b):
        return lax.dot_general(a, b, (((1,), (1,)), ((), ())), preferred_element_type=F32)

    h_hi, h_lo = split(h)
    w_hi, w_lo = split(wrt_ref[...])
    logits_t = dot_nt(w_hi, h_hi) + (dot_nt(w_hi, h_lo) + dot_nt(w_lo, h_hi))
    scores = jax.nn.sigmoid(logits_t)
    comb_t, sel, picks = _route(scores, scores + rb_ref[...])

    rank_t = _dot(sel.astype(BF16), earlier_scr[...]) + run_scr[...]
    run_scr[...] += jnp.sum(sel, axis=1, keepdims=True)
    cnt_ref[...] = jnp.broadcast_to(run_scr[...], cnt_ref.shape)

    iota_e = lax.broadcasted_iota(jnp.int32, sel.shape, 0).astype(F32)
    ranks, weights = [], []
    for idx in picks:
        hit = iota_e == idx
        ranks.append(jnp.sum(jnp.where(hit, rank_t, 0.0), axis=0, keepdims=True))
        weights.append(jnp.sum(jnp.where(hit, comb_t, 0.0), axis=0, keepdims=True))
    ek_ref[...] = jnp.concatenate(picks, axis=0).astype(jnp.int32)
    rk_ref[...] = jnp.concatenate(ranks, axis=0).astype(jnp.int32)
    w_rep = jnp.concatenate([jnp.broadcast_to(w, (SC_LANES, tm)) for w in weights], axis=0)
    wt_ref[...] = w_rep.T


def _router(x1, mod3, norm2_g, w_router_t, router_bias, seq_len, mod_row_of_batch):
    t = x1.shape[0]
    tm = TM_ROUTER

    def mod_idx(i):
        return (mod_row_of_batch((i * tm) // seq_len), 0, 0)

    def full(a):
        return pl.BlockSpec(a.shape, lambda i: (0,) * a.ndim)

    return pl.pallas_call(
        _router_kernel,
        grid=(t // tm,),
        in_specs=[pl.BlockSpec((tm, D_MODEL), lambda i: (i, 0)),
                  pl.BlockSpec((1, 6, D_MODEL), mod_idx),
                  full(norm2_g), full(w_router_t), full(router_bias)],
        out_specs=[pl.BlockSpec((tm * ROW_SLABS, 128), lambda i: (i, 0)),
                   pl.BlockSpec((TOP_K, tm), lambda i: (0, i)),
                   pl.BlockSpec((TOP_K, tm), lambda i: (0, i)),
                   pl.BlockSpec((tm, 128), lambda i: (i, 0)),
                   pl.BlockSpec((N_EXPERTS, 128), lambda i: (0, 0))],
        out_shape=[jax.ShapeDtypeStruct((t * ROW_SLABS, 128), jnp.int32),
                   jax.ShapeDtypeStruct((TOP_K, t), jnp.int32),
                   jax.ShapeDtypeStruct((TOP_K, t), jnp.int32),
                   jax.ShapeDtypeStruct((t, 128), F32),
                   jax.ShapeDtypeStruct((N_EXPERTS, 128), F32)],
        scratch_shapes=[pltpu.VMEM((N_EXPERTS, 1), F32), pltpu.VMEM((tm, tm), BF16)],
        compiler_params=pltpu.CompilerParams(dimension_semantics=("arbitrary",),
                                             vmem_limit_bytes=VMEM_LIMIT),
        name="router",
    )(x1, mod3, norm2_g, w_router_t, router_bias)


def _plan_kernel(ek_ref, rk_ref, cnt_ref, pos_ref, texp_ref, nused_ref, tend_ref, *, expert_rows):
    rows = float(expert_rows)
    cnt = cnt_ref[:, 0:1]
    tiles = jnp.floor((cnt + (rows - 1.0)) / rows)
    before = (lax.broadcasted_iota(jnp.int32, (N_EXPERTS, N_EXPERTS), 1)
              < lax.broadcasted_iota(jnp.int32, (N_EXPERTS, N_EXPERTS), 0))
    tile_start = jnp.dot(jnp.where(before, 1.0, 0.0), jnp.broadcast_to(tiles, (N_EXPERTS, 128)),
                         precision=lax.Precision.HIGHEST, preferred_element_type=F32)[:, 0:1]
    tile_end = tile_start + tiles
    row_start = tile_start * rows

    ek = ek_ref[...]
    pos = rk_ref[...].astype(F32)
    tile_id = lax.broadcasted_iota(jnp.int32, texp_ref.shape, 1).astype(F32)
    texp = jnp.zeros(texp_ref.shape, F32)
    for e in range(N_EXPERTS):
        pos = pos + jnp.where(ek == e, row_start[e:e + 1, :], 0.0)
        texp = texp + jnp.where(tile_id >= tile_end[e:e + 1, :], 1.0, 0.0)
    pos_ref[...] = pos.astype(jnp.int32)
    texp_ref[...] = jnp.minimum(texp, N_EXPERTS - 1.0).astype(jnp.int32)
    nused_ref[...] = jnp.broadcast_to(tile_end[N_EXPERTS - 1:N_EXPERTS, :], nused_ref.shape).astype(jnp.int32)
    tend_ref[...] = jnp.broadcast_to(tile_end, tend_ref.shape).astype(jnp.int32)


def _plan(ek, rk, cnt, n_tiles_pad, expert_rows):
    t = ek.shape[1]

    def full(shape):
        return pl.BlockSpec(shape, lambda: (0,) * len(shape))

    return pl.pallas_call(
        functools.partial(_plan_kernel, expert_rows=expert_rows),
        in_specs=[full(ek.shape), full(rk.shape), full(cnt.shape)],
        out_specs=[full((TOP_K, t)), full((1, n_tiles_pad)), full((1, 128)), full((N_EXPERTS, 128))],
        out_shape=[jax.ShapeDtypeStruct((TOP_K, t), jnp.int32),
                   jax.ShapeDtypeStruct((1, n_tiles_pad), jnp.int32),
                   jax.ShapeDtypeStruct((1, 128), jnp.int32),
                   jax.ShapeDtypeStruct((N_EXPERTS, 128), jnp.int32)],
        compiler_params=pltpu.CompilerParams(vmem_limit_bytes=VMEM_LIMIT),
        name="plan",
    )(ek, rk, cnt)


def _sc_mesh():
    return plsc.VectorSubcoreMesh(core_axis_name="c", subcore_axis_name="s")


def _sc_pack_weight_halves(w):
    e, k, n = w.shape
    k_half = k // 2
    rb = SC_PACK_BLOCK_WORDS // n
    units_per_expert = k_half // rb
    per_w = (e * units_per_expert) // SC_WORKERS
    lanes = SC_LANES

    @functools.partial(
        pl.kernel, out_type=jax.ShapeDtypeStruct((e * k_half, n), jnp.int32), mesh=_sc_mesh(),
        scratch_types=[pltpu.VMEM((rb, n), F32), pltpu.VMEM((rb, n), F32), pltpu.VMEM((rb, n), jnp.int32)],
        compiler_params=pltpu.CompilerParams(needs_layout_passes=False))
    def kern(w_hbm, out_hbm, a_v, b_v, o_v):
        wid = lax.axis_index("s") * SC_CORES + lax.axis_index("c")

        @pl.loop(0, per_w)
        def _(j):
            unit = wid * per_w + j
            expert = unit // units_per_expert
            blk = unit % units_per_expert
            row_a = expert * k + blk * rb
            pltpu.sync_copy(w_hbm.at[pl.ds(row_a, rb)], a_v)
            pltpu.sync_copy(w_hbm.at[pl.ds(row_a + k_half, rb)], b_v)

            @pl.loop(0, rb)
            def _(r):
                @plsc.parallel_loop(0, n, step=lanes, unroll=4)
                def _(c):
                    both = plsc.pack(a_v[r, pl.ds(c, lanes)], b_v[r, pl.ds(c, lanes)],
                                     format=plsc.PackFormat.INTERLEAVED)
                    o_v[r, pl.ds(c, lanes)] = plsc.bitcast(both, jnp.int32)

            pltpu.sync_copy(o_v, out_hbm.at[pl.ds(expert * k_half + blk * rb, rb)])

    return kern(w.reshape(e * k, n)).reshape(e, k_half, n)


def _sc_dispatch(rows, pos3, n_out, after=()):
    t = rows.shape[0]
    ch = SC_CHUNK
    per_w = (t // ch) // SC_WORKERS

    @functools.partial(
        pl.kernel, out_type=jax.ShapeDtypeStruct((n_out,) + rows.shape[1:], jnp.int32), mesh=_sc_mesh(),
        scratch_types=[pltpu.VMEM((TOP_K, ch), jnp.int32), pltpu.VMEM((ch,) + rows.shape[1:], jnp.int32),
                       pltpu.SemaphoreType.DMA])
    def k(rows_hbm, pos_hbm, *rest):
        out_hbm, idx_v, rows_v, sem = rest[len(after):]
        wid = lax.axis_index("s") * SC_CORES + lax.axis_index("c")

        @pl.loop(0, per_w)
        def _(j):
            c = wid * per_w + j
            pltpu.sync_copy(pos_hbm.at[c], idx_v)
            pltpu.sync_copy(rows_hbm.at[pl.ds(c * ch, ch)], rows_v)
            copies = [pltpu.async_copy(rows_v, out_hbm.at[idx_v.at[kk]], sem) for kk in range(TOP_K)]
            for cp in copies:
                cp.wait()

    return k(rows, pos3, *after)


def _sc_combine(table, pos3, wtok, t):
    ch = SC_CHUNK
    sub = SC_COMBINE_TOKENS
    lanes = SC_LANES
    slabs = ROW_SLABS
    per_w = (t // ch) // SC_WORKERS
    subs_per_chunk = ch // sub
    n_steps = per_w * subs_per_chunk

    @functools.partial(
        pl.kernel, out_type=jax.ShapeDtypeStruct((t, slabs, 128), jnp.int32), mesh=_sc_mesh(),
        scratch_types=[pltpu.VMEM((per_w, TOP_K, ch), jnp.int32),
                       pltpu.VMEM((2, TOP_K, sub, slabs, 128), jnp.int32),
                       pltpu.VMEM((2, sub, 128), F32),
                       pltpu.VMEM((sub, slabs, 128), jnp.int32),
                       pltpu.SemaphoreType.DMA((2,))],
        compiler_params=pltpu.CompilerParams(needs_layout_passes=False))
    def k(tab_hbm, pos_hbm, w_hbm, out_hbm, idx_v, rows_v, w_v, out_v, sem):
        wid = lax.axis_index("s") * SC_CORES + lax.axis_index("c")
        for j in range(per_w):
            pltpu.sync_copy(pos_hbm.at[wid * per_w + j], idx_v.at[j])

        def first_token(step):
            return (wid * per_w + step // subs_per_chunk) * ch + (step % subs_per_chunk) * sub

        def copies(step, slot):
            j = step // subs_per_chunk
            s = step % subs_per_chunk
            idx = [idx_v.at[j, kk, pl.ds(s * sub, sub)] for kk in range(TOP_K)]
            return ([pltpu.make_async_copy(tab_hbm.at[idx[kk]], rows_v.at[slot, kk], sem.at[slot])
                     for kk in range(TOP_K)]
                    + [pltpu.make_async_copy(w_hbm.at[pl.ds(first_token(step), sub)], w_v.at[slot], sem.at[slot])])

        for cp in copies(0, 0):
            cp.start()

        @pl.loop(0, n_steps)
        def _(step):
            slot = step % 2

            @pl.when(step + 1 < n_steps)
            def _():
                for cp in copies(step + 1, 1 - slot):
                    cp.start()

            for cp in copies(step, slot):
                cp.wait()

            @pl.loop(0, sub)
            def _(tt):
                wk = [w_v[slot, tt, pl.ds(kk * lanes, lanes)] for kk in range(TOP_K)]
                for sl in range(slabs):
                    @plsc.parallel_loop(0, 128, step=lanes, unroll=8)
                    def _(off):
                        acc_lo = jnp.zeros((lanes,), F32)
                        acc_hi = jnp.zeros((lanes,), F32)
                        for kk in range(TOP_K):
                            word = rows_v[slot, kk, tt, sl, pl.ds(off, lanes)]
                            lo = plsc.bitcast(word << 16, F32)
                            hi = plsc.bitcast(word & jnp.int32(-65536), F32)
                            acc_lo = acc_lo + wk[kk] * lo
                            acc_hi = acc_hi + wk[kk] * hi
                        both = plsc.pack(acc_lo, acc_hi, format=plsc.PackFormat.INTERLEAVED)
                        out_v[tt, sl, pl.ds(off, lanes)] = plsc.bitcast(both, jnp.int32)

            pltpu.sync_copy(out_v, out_hbm.at[pl.ds(first_token(step), sub)])

    return k(table, pos3, wtok)


def _experts_kernel(texp_ref, nused_ref, tend_ref, xs_ref, weg_hbm, weu_hbm, wed_hbm, ys_ref,
                    wg_scr, wu_scr, wd_scr, wg_buf, wu_buf, wd_buf, sem, group_scr, *, expert_rows):
    step = pl.program_id(0)
    rows = expert_rows
    tiles_per_step = EXPERT_TILES_PER_STEP
    half = D_MODEL // 2
    n_used = nused_ref[0]

    def weight_copies(e, slot):
        return [pltpu.make_async_copy(weg_hbm.at[e], wg_buf.at[slot], sem.at[slot, 0]),
                pltpu.make_async_copy(weu_hbm.at[e], wu_buf.at[slot], sem.at[slot, 1]),
                pltpu.make_async_copy(wed_hbm.at[e], wd_buf.at[slot], sem.at[slot, 2])]

    def next_group(e):
        tile = tend_ref[e]
        return texp_ref[jnp.minimum(tile, n_used - 1)], tile < n_used

    def start_weights(e, slot, exists):
        @pl.when(exists)
        def _():
            for cp in weight_copies(e, slot):
                cp.start()

    @pl.when(step == 0)
    def _():
        group_scr[0] = 0
        e, exists = texp_ref[0], True
        for slot in range(WEIGHT_SLOTS - 1):
            start_weights(e, slot, exists)
            nxt, has_next = next_group(e)
            e, exists = nxt, exists & has_next

    def row_tile(tile, x_view, y_view):
        expert = texp_ref[tile]
        used = tile < n_used
        new_expert = (tile == 0) | (expert != texp_ref[jnp.maximum(tile - 1, 0)])

        @pl.when(used & new_expert)
        def _():
            group = group_scr[0]
            slot = group % WEIGHT_SLOTS
            ahead, exists = expert, True
            for _ in range(WEIGHT_SLOTS - 1):
                nxt, has_next = next_group(ahead)
                ahead, exists = nxt, exists & has_next
            start_weights(ahead, (group + WEIGHT_SLOTS - 1) % WEIGHT_SLOTS, exists)

            for cp in weight_copies(expert, slot):
                cp.wait()
            for scr, buf in ((wg_scr, wg_buf), (wu_scr, wu_buf), (wd_scr, wd_buf)):
                top, bottom = _unpack_pair(buf[slot])
                k_half = top.shape[0]
                scr[0:k_half, :] = top.astype(BF16)
                scr[k_half:2 * k_half, :] = bottom.astype(BF16)
            group_scr[0] = group + 1

        @pl.when(used)
        def _():
            lo, hi = _unpack_pair(_load_token_words(x_view, (), rows))
            lo = lo.astype(BF16)
            hi = hi.astype(BF16)
            g = _dot(lo, wg_scr[0:half, :]) + _dot(hi, wg_scr[half:D_MODEL, :])
            u = _dot(lo, wu_scr[0:half, :]) + _dot(hi, wu_scr[half:D_MODEL, :])
            y = _dot((_silu(g) * u).astype(BF16), wd_scr[...])
            _store_token_words(y_view, _pack_pair(y[:, :half], y[:, half:]), rows)

        @pl.when(jnp.logical_not(used) & (step == (n_used - 1) // tiles_per_step))
        def _():
            y_view[...] = jnp.zeros_like(y_view)

    for s in range(tiles_per_step):
        view = pl.ds(s * rows * ROW_SLABS, rows * ROW_SLABS)
        row_tile(step * tiles_per_step + s, xs_ref.at[view], ys_ref.at[view])


def _experts(texp, nused, tend, xs2d, weg, weu, wed, n_tiles, expert_rows):
    tiles_per_step = EXPERT_TILES_PER_STEP
    block = (tiles_per_step * expert_rows * ROW_SLABS, 128)
    hbm = pl.BlockSpec(memory_space=pl.ANY)

    def block_idx(j, te, nu, tn):
        return (jnp.minimum(j, (nu[0] - 1) // tiles_per_step), 0)

    grid_spec = pltpu.PrefetchScalarGridSpec(
        num_scalar_prefetch=3,
        grid=(n_tiles // tiles_per_step,),
        in_specs=[pl.BlockSpec(block, block_idx), hbm, hbm, hbm],
        out_specs=pl.BlockSpec(block, block_idx),
        scratch_shapes=[pltpu.VMEM((D_MODEL, EXPERT_DIM), BF16),
                        pltpu.VMEM((D_MODEL, EXPERT_DIM), BF16),
                        pltpu.VMEM((EXPERT_DIM, D_MODEL), BF16),
                        pltpu.VMEM((WEIGHT_SLOTS,) + weg.shape[1:], jnp.int32),
                        pltpu.VMEM((WEIGHT_SLOTS,) + weu.shape[1:], jnp.int32),
                        pltpu.VMEM((WEIGHT_SLOTS,) + wed.shape[1:], jnp.int32),
                        pltpu.SemaphoreType.DMA((WEIGHT_SLOTS, 3)),
                        pltpu.SMEM((1,), jnp.int32)],
    )
    return pl.pallas_call(
        functools.partial(_experts_kernel, expert_rows=expert_rows),
        grid_spec=grid_spec,
        out_shape=jax.ShapeDtypeStruct(xs2d.shape, jnp.int32),
        compiler_params=pltpu.CompilerParams(dimension_semantics=("arbitrary",),
                                             vmem_limit_bytes=VMEM_LIMIT),
        name="experts",
    )(texp, nused, tend, xs2d, weg, weu, wed)


def _final_kernel(x_ref, routed_ref, mod_ref, g2_ref, wsg_ref, wsu_ref, wsd_ref, fng_ref, o_ref):
    tm = x_ref.shape[0]
    x = x_ref[...]
    hb = _rms_mod(x, g2_ref[...], mod_ref[0, 3:4, :], mod_ref[0, 4:5, :]).astype(BF16)
    shared = _dot((_silu(_dot(hb, wsg_ref[...])) * _dot(hb, wsu_ref[...])).astype(BF16), wsd_ref[...])
    routed = jnp.concatenate(_unpack_pair(_load_token_words(routed_ref, (), tm)), axis=1)
    y = x + mod_ref[0, 5:6, :] * (routed + shared)
    ms = jnp.mean(y * y, axis=-1, keepdims=True)
    o_ref[...] = y * lax.rsqrt(ms + EPS) * fng_ref[...]


def _final(x1, routed2d, mod3, norm2_g, wsg, wsu, wsd, final_g, seq_len, mod_row_of_batch):
    t = x1.shape[0]
    tm = TM_FINAL

    def mod_idx(i):
        return (mod_row_of_batch((i * tm) // seq_len), 0, 0)

    def full(a):
        return pl.BlockSpec(a.shape, lambda i: (0,) * a.ndim)

    return pl.pallas_call(
        _final_kernel,
        grid=(t // tm,),
        in_specs=[pl.BlockSpec((tm, D_MODEL), lambda i: (i, 0)),
                  pl.BlockSpec((tm * ROW_SLABS, 128), lambda i: (i, 0)),
                  pl.BlockSpec((1, 6, D_MODEL), mod_idx),
                  full(norm2_g), full(wsg), full(wsu), full(wsd), full(final_g)],
        out_specs=pl.BlockSpec((tm, D_MODEL), lambda i: (i, 0)),
        out_shape=jax.ShapeDtypeStruct((t, D_MODEL), F32),
        compiler_params=pltpu.CompilerParams(dimension_semantics=("parallel",),
                                             vmem_limit_bytes=VMEM_LIMIT),
        name="final",
    )(x1, routed2d, mod3, norm2_g, wsg, wsu, wsd, final_g)


def _moe(x1, mod3, lw, seq_len, mod_row_of_batch):
    t = x1.shape[0]
    expert_rows = min(MAX_EXPERT_ROWS, TOP_K * t // N_EXPERTS // 2)
    n_tiles = TOP_K * t // expert_rows + N_EXPERTS
    n_tiles_pad = -(-n_tiles // 128) * 128
    hp2d, ek, rk, wtok, cnt = _router(x1, mod3, lw["norm2_g"], lw["w_router_t"], lw["router_bias"],
                                      seq_len, mod_row_of_batch)
    pos, texp, nused, tend = _plan(ek, rk, cnt, n_tiles_pad, expert_rows)
    pos3 = pos.reshape(TOP_K, t // SC_CHUNK, SC_CHUNK).transpose(1, 0, 2)
    xs = _sc_dispatch(hp2d.reshape(t, ROW_SLABS, 128), pos3, n_tiles * expert_rows,
                      after=(lw["weg"], lw["weu"], lw["wed"]))
    ys2d = _experts(texp.reshape(-1), nused.reshape(-1), tend[:, 0], xs.reshape(-1, 128),
                    lw["weg"], lw["weu"], lw["wed"], n_tiles, expert_rows)
    routed = _sc_combine(ys2d.reshape(-1, ROW_SLABS, 128), pos3, wtok, t)
    return _final(x1, routed.reshape(t * ROW_SLABS, 128), mod3, lw["norm2_g"],
                  lw["wsg"], lw["wsu"], lw["wsd"], lw["final_g"], seq_len, mod_row_of_batch)


def _dft_tables(seq_len):
    gd = FOURIER_GROUP_DIM
    kc = np.arange(gd)
    ang_c = ((kc[:, None] * kc[None, :]) % gd) * (2.0 * math.pi / gd)
    cs = np.concatenate([np.cos(ang_c), np.sin(ang_c)], axis=1) * (gd ** -0.5)
    kl = np.arange(seq_len)
    ang_l = ((kl[:, None] * kl[None, :]) % seq_len) * (2.0 * math.pi / seq_len)
    cls = np.concatenate([np.cos(ang_l), -np.sin(ang_l)], axis=1) * (seq_len ** -0.5)
    return jnp.asarray(cs.astype(np.float32), dtype=BF16), jnp.asarray(cls.astype(np.float32), dtype=BF16)


def _rope_tables(length):
    rows = length // GRID_W
    r = np.repeat(np.arange(rows, dtype=np.float32), GRID_W)
    col = np.tile(np.arange(GRID_W, dtype=np.float32), rows)
    nf = RET_HEAD_DIM // 4
    inv = (np.float32(ROPE_BASE) ** (-np.arange(nf, dtype=np.float32) / np.float32(nf))).astype(np.float32)
    ar = r[:, None] * inv[None]
    ac = col[:, None] * inv[None]
    ang = np.concatenate([ar, ar, ac, ac], axis=-1).astype(np.float64)
    sign = np.where((np.arange(RET_HEAD_DIM) & nf) == 0, -1.0, 1.0)
    return (jnp.asarray(np.cos(ang).astype(np.float32)),
            jnp.asarray((np.sin(ang) * sign[None, :]).astype(np.float32)))


def _trunk_path(x, mod3, mod_row_of_batch, s0f, s0b, rope, lw):
    batch, seq_len, _ = x.shape
    x2d = x.reshape(batch * seq_len, D_MODEL)
    uf, q, k, v, sg, gf, gr = _inproj(x2d, mod3, lw["norm1_g"], lw["w_in"], seq_len, mod_row_of_batch, rope)
    r, s_f, s_b = _retention(q, k, v, sg, lw["dec"], lw["gn_g"], s0f, s0b, batch, seq_len)
    cs, cls = _dft_tables(seq_len)
    x1 = _fnet_merge(uf, cs, cls, r, gf, gr, x2d, mod3, lw["w_four"], lw["w_ret"], lw["w_o"],
                     batch, seq_len, mod_row_of_batch)
    y = _moe(x1, mod3, lw, seq_len, mod_row_of_batch)
    return y.reshape(batch, seq_len, D_MODEL), s_f, s_b


def kernel(x_prompt, x_sample, state_ret_fwd, state_ret_bwd, c, c_ctx, w_ada, b_ada, norm1_g, norm2_g, w_in,
           ret_decay_fwd, ret_decay_bwd, ret_gn_g, w_four_out, w_ret_out, w_out, w_router, router_bias,
           w_exp_gate, w_exp_up, w_exp_down, w_shared_gate, w_shared_up, w_shared_down, final_norm_g):
    depth = w_ada.shape[0]
    assert depth == 1, "final norm is fused into the last layer's MoE kernel"
    n_ctx, n_lat = x_prompt.shape[0], x_sample.shape[0]
    cond = jnp.concatenate([c_ctx[None, :], c], axis=0)
    cond = jnp.pad(cond, ((0, (-cond.shape[0]) % 8), (0, 0)))
    rope = _rope_tables(x_sample.shape[1])
    zeros = jnp.zeros((n_ctx, N_RET_HEADS, RET_HEAD_DIM, RET_HEAD_DIM), F32)

    layer = 0
    mod = _ada(cond, w_ada[layer], b_ada[layer][None, :])
    mod3 = mod.reshape(mod.shape[0], 6, D_MODEL)
    dec = jnp.stack([ret_decay_fwd[layer], ret_decay_bwd[layer]], axis=1)
    lw = {
        "norm1_g": norm1_g[layer][None, :],
        "norm2_g": norm2_g[layer][None, :],
        "w_in": w_in[layer],
        "dec": jnp.broadcast_to(dec[:, :, None], (N_RET_HEADS, 2, RET_HEAD_DIM)).astype(F32),
        "gn_g": ret_gn_g[layer][None, :],
        "w_four": w_four_out[layer],
        "w_ret": w_ret_out[layer],
        "w_o": w_out[layer],
        "w_router_t": w_router[layer].T,
        "router_bias": router_bias[layer][:, None],
        "weg": _sc_pack_weight_halves(w_exp_gate[layer]),
        "weu": _sc_pack_weight_halves(w_exp_up[layer]),
        "wed": _sc_pack_weight_halves(w_exp_down[layer]),
        "wsg": w_shared_gate[layer].astype(BF16),
        "wsu": w_shared_up[layer].astype(BF16),
        "wsd": w_shared_down[layer].astype(BF16),
        "final_g": final_norm_g[None, :],
    }
    y_prompt, s_f, s_b = _trunk_path(x_prompt, mod3, lambda b: 0, zeros, zeros, None, lw)
    y_sample, _, _ = _trunk_path(x_sample, mod3, lambda b: 1 + b, state_ret_fwd[:, layer],
                                 state_ret_bwd[:, layer], rope, lw)
    return (y_prompt, y_sample, s_f[:, None], s_b[:, None])
```

```python
import functools
import math

import jax
import jax.numpy as jnp
import numpy as np
from jax import lax
from jax.experimental import pallas as pl
from jax.experimental.pallas import tpu as pltpu
from jax.experimental.pallas import tpu_sc as plsc

F32 = jnp.float32
BF16 = jnp.bfloat16

D_MODEL = 1024
GRID_W = 64
N_FOURIER_GROUPS = 8
FOURIER_GROUP_DIM = 128
N_RET_HEADS = 4
RET_HEAD_DIM = 128
RET_WIDTH = N_RET_HEADS * RET_HEAD_DIM
CHUNK = 128
N_EXPERTS = 64
N_EXPERT_GROUPS = 8
EXPERTS_PER_GROUP = N_EXPERTS // N_EXPERT_GROUPS
TOPK_GROUPS = 4
TOP_K = 8
EXPERT_DIM = 256
ROUTED_SCALE = 2.5
ROPE_BASE = 10000.0
EPS = 1e-6
Q_SCALE = RET_HEAD_DIM ** -0.5

_C_UF = (0, 1024)
_C_Q = (1024, 1536)
_C_K = (1536, 2048)
_C_V = (2048, 2560)
_C_G = (2560, 3072)
_C_GF = (3072, 4096)
_C_GR = (4096, 5120)

VMEM_LIMIT = 56 * 1024 * 1024

TM_INPROJ = 1024
TM_ROUTER = 512
FNET_ROWS = 512
TM_FINAL = 512
EXPERT_TILES_PER_STEP = 4
MAX_EXPERT_ROWS = 512
WEIGHT_SLOTS = 3
ROW_SLABS = 4
SC_CORES = 2
SC_WORKERS = 32
SC_CHUNK = 128
SC_LANES = 16
SC_PACK_BLOCK_WORDS = 16384
SC_COMBINE_TOKENS = 8


def _silu(x):
    return x * jax.nn.sigmoid(x)


def _dot(a, b):
    return jnp.dot(a, b, preferred_element_type=F32)


def _rms_mod(x, g, shift, scale):
    ms = jnp.mean(x * x, axis=-1, keepdims=True)
    y = x * lax.rsqrt(ms + EPS) * g
    return y * (1.0 + scale) + shift


def _ada_kernel(cond_ref, w_ref, b_ref, o_ref):
    s = _silu(cond_ref[...]).astype(BF16)
    o_ref[...] = _dot(s, w_ref[...].astype(BF16)) + b_ref[...]


def _ada(cond, w_ada, b_ada):
    rows, n = cond.shape[0], w_ada.shape[1]
    tn = 1536
    return pl.pallas_call(
        _ada_kernel,
        grid=(n // tn,),
        in_specs=[pl.BlockSpec((rows, D_MODEL), lambda j: (0, 0)),
                  pl.BlockSpec((D_MODEL, tn), lambda j: (0, j)),
                  pl.BlockSpec((1, tn), lambda j: (0, j))],
        out_specs=pl.BlockSpec((rows, tn), lambda j: (0, j)),
        out_shape=jax.ShapeDtypeStruct((rows, n), F32),
        compiler_params=pltpu.CompilerParams(vmem_limit_bytes=VMEM_LIMIT),
        name="ada",
    )(cond, w_ada, b_ada)


def _rope_head(x, cos, sin_signed, first_half):
    partner = jnp.where(first_half, pltpu.roll(x, 96, 1), pltpu.roll(x, 32, 1))
    return x * cos + partner * sin_signed


def _inproj_kernel(*refs, use_rope):
    if use_rope:
        x_ref, mod_ref, g_ref, w_ref, cos_ref, sin_ref = refs[:6]
        outs = refs[6:]
    else:
        x_ref, mod_ref, g_ref, w_ref = refs[:4]
        outs = refs[4:]
    uf_o, q_o, k_o, v_o, sg_o, gf_o, gr_o = outs

    h = _rms_mod(x_ref[...], g_ref[...], mod_ref[0, 0:1, :], mod_ref[0, 1:2, :])
    hb = h.astype(BF16)

    def proj(cols):
        return _dot(hb, w_ref[:, cols[0]:cols[1]].astype(BF16))

    uf_o[...] = proj(_C_UF).astype(BF16)
    q = proj(_C_Q)
    k = proj(_C_K)
    if use_rope:
        cos = cos_ref[...]
        sin_signed = sin_ref[...]
        lane = lax.broadcasted_iota(jnp.int32, cos.shape, 1)
        first_half = (lane & 32) == 0
        for hd in range(N_RET_HEADS):
            sl = slice(hd * RET_HEAD_DIM, (hd + 1) * RET_HEAD_DIM)
            q_o[:, sl] = (_rope_head(q[:, sl], cos, sin_signed, first_half) * Q_SCALE).astype(BF16)
            k_o[:, sl] = _rope_head(k[:, sl], cos, sin_signed, first_half).astype(BF16)
    else:
        q_o[...] = (q * Q_SCALE).astype(BF16)
        k_o[...] = k.astype(BF16)
    v_o[...] = proj(_C_V).astype(BF16)
    sg_o[...] = _silu(proj(_C_G)).astype(BF16)
    gf_o[...] = jax.nn.sigmoid(proj(_C_GF)).astype(BF16)
    gr_o[...] = jax.nn.sigmoid(proj(_C_GR)).astype(BF16)


def _inproj(x2d, mod3, norm_g, w_in_f32, seq_len, mod_row_of_batch, rope):
    t = x2d.shape[0]
    tm = TM_INPROJ
    tiles_per_seq = max(seq_len // tm, 1)

    def mod_idx(i):
        return (mod_row_of_batch((i * tm) // seq_len), 0, 0)

    in_specs = [pl.BlockSpec((tm, D_MODEL), lambda i: (i, 0)),
                pl.BlockSpec((1, 6, D_MODEL), mod_idx),
                pl.BlockSpec((1, D_MODEL), lambda i: (0, 0)),
                pl.BlockSpec(w_in_f32.shape, lambda i: (0, 0), pipeline_mode=pl.Buffered(1))]
    args = [x2d, mod3, norm_g, w_in_f32]
    if rope is not None:
        in_specs += [pl.BlockSpec((tm, RET_HEAD_DIM), lambda i: (i % tiles_per_seq, 0))] * 2
        args += list(rope)
    widths = [1024, RET_WIDTH, RET_WIDTH, RET_WIDTH, RET_WIDTH, 1024, 1024]
    return pl.pallas_call(
        functools.partial(_inproj_kernel, use_rope=rope is not None),
        grid=(t // tm,),
        in_specs=in_specs,
        out_specs=[pl.BlockSpec((tm, w), lambda i: (i, 0)) for w in widths],
        out_shape=[jax.ShapeDtypeStruct((t, w), BF16) for w in widths],
        compiler_params=pltpu.CompilerParams(dimension_semantics=("parallel",),
                                             vmem_limit_bytes=VMEM_LIMIT),
        name="inproj",
    )(*args)


def _retention_kernel(q_ref, k_ref, v_ref, sg_ref, dec_ref, gn_ref, s0f_ref, s0b_ref,
                      r_ref, sfo_ref, sbo_ref, tab_scr, gc_scr):
    n_chunks = q_ref.shape[0] // CHUNK
    hd = RET_HEAD_DIM

    @pl.when(pl.program_id(0) == 0)
    def _():
        row = lax.broadcasted_iota(jnp.int32, (CHUNK, CHUNK), 0).astype(F32)
        col = lax.broadcasted_iota(jnp.int32, (CHUNK, CHUNK), 1).astype(F32)
        diff = row - col
        for h in range(N_RET_HEADS):
            dec = dec_ref[h]
            lg = jnp.minimum(dec, 0.0) - jnp.log1p(jnp.exp(-jnp.abs(dec)))
            lgf = lg[0:1, :]
            lgb = lg[1:2, :]
            tab_scr[h, 0] = jnp.exp(jnp.where(diff >= 0, lgf * diff, lgb * (-diff)))
            tab_scr[h, 1] = jnp.exp(lgf * (row + 1.0))
            tab_scr[h, 2] = jnp.exp(lgb * (CHUNK - row))
            tab_scr[h, 3] = jnp.exp(lgf * (CHUNK - 1.0 - col))
            tab_scr[h, 4] = jnp.exp(lgb * col)
            gc_scr[h] = jnp.exp(lg * CHUNK)

    def rows(n):
        return slice(n * CHUNK, (n + 1) * CHUNK)

    for h in range(N_RET_HEADS):
        cols = slice(h * hd, (h + 1) * hd)
        decay, qw_f, qw_b, kwt_f, kwt_b = (tab_scr[h, i] for i in range(5))
        gc = gc_scr[h]
        gc_f = gc[0:1, :]
        gc_b = gc[1:2, :]

        kv_f, kv_b = [], []
        for n in range(n_chunks):
            kt = k_ref[rows(n), cols].astype(F32).T
            vn = v_ref[rows(n), cols]
            kv_f.append(_dot((kt * kwt_f).astype(BF16), vn))
            kv_b.append(_dot((kt * kwt_b).astype(BF16), vn))

        s = s0f_ref[h]
        prev_f = []
        for n in range(n_chunks):
            prev_f.append(s.astype(BF16))
            s = gc_f * s + kv_f[n]
        sfo_ref[h] = s
        s = s0b_ref[h]
        prev_b = [None] * n_chunks
        for n in reversed(range(n_chunks)):
            prev_b[n] = s.astype(BF16)
            s = gc_b * s + kv_b[n]
        sbo_ref[h] = s

        gn = gn_ref[:, cols]
        for n in range(n_chunks):
            qn = q_ref[rows(n), cols]
            qf = qn.astype(F32)
            scores = lax.dot_general(qn, k_ref[rows(n), cols], (((1,), (1,)), ((), ())),
                                     preferred_element_type=F32)
            o = _dot((scores * decay).astype(BF16), v_ref[rows(n), cols])
            o = o + _dot((qf * qw_f).astype(BF16), prev_f[n])
            o = o + _dot((qf * qw_b).astype(BF16), prev_b[n])
            mu = jnp.mean(o, axis=-1, keepdims=True)
            d = o - mu
            var = jnp.mean(d * d, axis=-1, keepdims=True)
            on = d * lax.rsqrt(var + EPS) * gn
            r_ref[rows(n), cols] = (on * sg_ref[rows(n), cols].astype(F32)).astype(BF16)


def _retention(q, k, v, sg, dec, gn_g, s0f, s0b, batch, seq_len):
    hd = RET_HEAD_DIM
    tok_spec = pl.BlockSpec((seq_len, RET_WIDTH), lambda b: (b, 0))
    st_spec = pl.BlockSpec((None, N_RET_HEADS, hd, hd), lambda b: (b, 0, 0, 0))
    st_shape = jax.ShapeDtypeStruct((batch, N_RET_HEADS, hd, hd), F32)
    return pl.pallas_call(
        _retention_kernel,
        grid=(batch,),
        in_specs=[tok_spec, tok_spec, tok_spec, tok_spec,
                  pl.BlockSpec(dec.shape, lambda b: (0, 0, 0)),
                  pl.BlockSpec(gn_g.shape, lambda b: (0, 0)),
                  st_spec, st_spec],
        out_specs=[tok_spec, st_spec, st_spec],
        out_shape=[jax.ShapeDtypeStruct((batch * seq_len, RET_WIDTH), BF16), st_shape, st_shape],
        scratch_shapes=[pltpu.VMEM((N_RET_HEADS, 5, CHUNK, CHUNK), F32),
                        pltpu.VMEM((N_RET_HEADS, 2, hd), F32)],
        compiler_params=pltpu.CompilerParams(dimension_semantics=("arbitrary",),
                                             vmem_limit_bytes=VMEM_LIMIT),
        name="retention",
    )(q, k, v, sg, dec, gn_g, s0f, s0b)


def _fnet_merge_kernel(uf_ref, cs_ref, cls_ref, r_ref, gf_ref, gr_ref, x_ref, mod_ref, wf_ref, wr_ref, wo_ref,
                       o_ref, xcs_ref):
    seq_len = uf_ref.shape[0]
    gd = FOURIER_GROUP_DIM

    @pl.when(pl.program_id(1) == 0)
    def _():
        for g in range(N_FOURIER_GROUPS):
            x = _dot(uf_ref[:, g * gd:(g + 1) * gd], cs_ref[...])
            xcs_ref[0:seq_len, g * gd:(g + 1) * gd] = x[:, :gd].astype(BF16)
            xcs_ref[seq_len:2 * seq_len, g * gd:(g + 1) * gd] = x[:, gd:].astype(BF16)

    f_mix = _dot(cls_ref[...], xcs_ref[...]).astype(BF16)
    f_out = _dot(f_mix, wf_ref[...].astype(BF16))
    r_out = _dot(r_ref[...], wr_ref[...].astype(BF16))
    merged = gf_ref[...].astype(F32) * f_out + gr_ref[...].astype(F32) * r_out
    mix = _dot(merged.astype(BF16), wo_ref[...].astype(BF16))
    o_ref[...] = x_ref[...] + mod_ref[0, 2:3, :] * mix


def _fnet_merge(uf, cs, cls, r, gf, gr, x2d, mod3, w_four, w_ret, w_o, batch, seq_len, mod_row_of_batch):
    rb = min(FNET_ROWS, seq_len)
    nr = seq_len // rb

    def tok(w):
        return pl.BlockSpec((rb, w), lambda b, i: (b * nr + i, 0))

    def full(a):
        return pl.BlockSpec(a.shape, lambda b, i: (0, 0))

    def once(a):
        return pl.BlockSpec(a.shape, lambda b, i: (0, 0), pipeline_mode=pl.Buffered(1))

    return pl.pallas_call(
        _fnet_merge_kernel,
        grid=(batch, nr),
        in_specs=[pl.BlockSpec((seq_len, D_MODEL), lambda b, i: (b, 0)),
                  full(cs),
                  pl.BlockSpec((rb, 2 * seq_len), lambda b, i: (i, 0)),
                  tok(RET_WIDTH), tok(D_MODEL), tok(D_MODEL), tok(D_MODEL),
                  pl.BlockSpec((1, 6, D_MODEL), lambda b, i: (mod_row_of_batch(b), 0, 0)),
                  once(w_four), once(w_ret), once(w_o)],
        out_specs=tok(D_MODEL),
        out_shape=jax.ShapeDtypeStruct((batch * seq_len, D_MODEL), F32),
        scratch_shapes=[pltpu.VMEM((2 * seq_len, D_MODEL), BF16)],
        compiler_params=pltpu.CompilerParams(dimension_semantics=("parallel", "arbitrary"),
                                             vmem_limit_bytes=VMEM_LIMIT),
        name="fnet_merge",
    )(uf, cs, cls, r, gf, gr, x2d, mod3, w_four, w_ret, w_o)


def _pack_pair(lo_f32, hi_f32):
    lo = lax.bitcast_convert_type(lo_f32.astype(BF16).astype(F32), jnp.uint32)
    hi = lax.bitcast_convert_type(hi_f32.astype(BF16).astype(F32), jnp.uint32)
    return lax.bitcast_convert_type((lo >> 16) | hi, jnp.int32)


def _unpack_pair(words_i32):
    w = lax.bitcast_convert_type(words_i32, jnp.uint32)
    lo = lax.bitcast_convert_type(w << 16, F32)
    hi = lax.bitcast_convert_type(w & jnp.uint32(0xFFFF0000), F32)
    return lo, hi


def _load_token_words(ref, lead, n_tok):
    parts = []
    for s in range(ROW_SLABS):
        idx = (pl.ds(s, n_tok, stride=ROW_SLABS), slice(None))
        parts.append(ref[lead + idx] if lead else ref[idx])
    return jnp.concatenate(parts, axis=1)


def _store_token_words(ref, words, n_tok):
    for s in range(ROW_SLABS):
        ref[pl.ds(s, n_tok, stride=ROW_SLABS), :] = words[:, s * 128:(s + 1) * 128]


def _route(scores, biased):
    tokens = scores.shape[1]
    neg = -jnp.inf
    epg = EXPERTS_PER_GROUP
    iota_g = lax.broadcasted_iota(jnp.int32, (epg, tokens), 0).astype(F32)

    def pick_first_max(cur, iota, size):
        m = jnp.max(cur, axis=0, keepdims=True)
        idx = jnp.min(jnp.where(cur == m, iota, float(size)), axis=0, keepdims=True)
        return m, idx, iota == idx

    group_scores = []
    for g in range(N_EXPERT_GROUPS):
        vals = biased[g * epg:(g + 1) * epg, :]
        m1, _, hit = pick_first_max(vals, iota_g, epg)
        m2 = jnp.max(jnp.where(hit, neg, vals), axis=0, keepdims=True)
        group_scores.append(m1 + m2)
    cur = jnp.concatenate(group_scores, axis=0)
    group_sel = jnp.zeros_like(cur)
    for _ in range(TOPK_GROUPS):
        _, _, hit = pick_first_max(cur, iota_g, N_EXPERT_GROUPS)
        group_sel = jnp.where(hit, 1.0, group_sel)
        cur = jnp.where(hit, neg, cur)
    masked = jnp.concatenate(
        [jnp.where(group_sel[g:g + 1, :] > 0.0, biased[g * epg:(g + 1) * epg, :], neg)
         for g in range(N_EXPERT_GROUPS)], axis=0)
    iota_e = lax.broadcasted_iota(jnp.int32, masked.shape, 0).astype(F32)
    sel = jnp.zeros_like(masked)
    cur = masked
    picks = []
    for _ in range(TOP_K):
        _, idx, hit = pick_first_max(cur, iota_e, N_EXPERTS)
        picks.append(idx)
        sel = jnp.where(hit, 1.0, sel)
        cur = jnp.where(hit, neg, cur)
    w = scores * sel
    return w / jnp.sum(w, axis=0, keepdims=True) * ROUTED_SCALE, sel, picks


def _router_kernel(x_ref, mod_ref, g2_ref, wrt_ref, rb_ref, hp_ref, ek_ref, rk_ref, wt_ref, cnt_ref,
                   run_scr, earlier_scr):
    tm = x_ref.shape[0]

    @pl.when(pl.program_id(0) == 0)
    def _():
        run_scr[...] = jnp.zeros_like(run_scr)
        earlier = (lax.broadcasted_iota(jnp.int32, (tm, tm), 0) < lax.broadcasted_iota(jnp.int32, (tm, tm), 1))
        earlier_scr[...] = jnp.where(earlier, 1.0, 0.0).astype(BF16)

    h = _rms_mod(x_ref[...], g2_ref[...], mod_ref[0, 3:4, :], mod_ref[0, 4:5, :])
    half = D_MODEL // 2
    _store_token_words(hp_ref, _pack_pair(h[:, :half], h[:, half:]), tm)

    def split(a):
        hi = a.astype(BF16)
        return hi, (a - hi.astype(F32)).astype(BF16)

    def dot_nt(a, b):
        return lax.dot_general(a, b, (((1,), (1,)), ((), ())), preferred_element_type=F32)

    h_hi, h_lo = split(h)
    w_hi, w_lo = split(wrt_ref[...])
    logits_t = dot_nt(w_hi, h_hi) + (dot_nt(w_hi, h_lo) + dot_nt(w_lo, h_hi))
    scores = jax.nn.sigmoid(logits_t)
    comb_t, sel, picks = _route(scores, scores + rb_ref[...])

    rank_t = _dot(sel.astype(BF16), earlier_scr[...]) + run_scr[...]
    run_scr[...] += jnp.sum(sel, axis=1, keepdims=True)
    cnt_ref[...] = jnp.broadcast_to(run_scr[...], cnt_ref.shape)

    iota_e = lax.broadcasted_iota(jnp.int32, sel.shape, 0).astype(F32)
    ranks, weights = [], []
    for idx in picks:
        hit = iota_e == idx
        ranks.append(jnp.sum(jnp.where(hit, rank_t, 0.0), axis=0, keepdims=True))
        weights.append(jnp.sum(jnp.where(hit, comb_t, 0.0), axis=0, keepdims=True))
    ek_ref[...] = jnp.concatenate(picks, axis=0).astype(jnp.int32)
    rk_ref[...] = jnp.concatenate(ranks, axis=0).astype(jnp.int32)
    w_rep = jnp.concatenate([jnp.broadcast_to(w, (SC_LANES, tm)) for w in weights], axis=0)
    wt_ref[...] = w_rep.T


def _router(x1, mod3, norm2_g, w_router_t, router_bias, seq_len, mod_row_of_batch):
    t = x1.shape[0]
    tm = TM_ROUTER

    def mod_idx(i):
        return (mod_row_of_batch((i * tm) // seq_len), 0, 0)

    def full(a):
        return pl.BlockSpec(a.shape, lambda i: (0,) * a.ndim)

    return pl.pallas_call(
        _router_kernel,
        grid=(t // tm,),
        in_specs=[pl.BlockSpec((tm, D_MODEL), lambda i: (i, 0)),
                  pl.BlockSpec((1, 6, D_MODEL), mod_idx),
                  full(norm2_g), full(w_router_t), full(router_bias)],
        out_specs=[pl.BlockSpec((tm * ROW_SLABS, 128), lambda i: (i, 0)),
                   pl.BlockSpec((TOP_K, tm), lambda i: (0, i)),
                   pl.BlockSpec((TOP_K, tm), lambda i: (0, i)),
                   pl.BlockSpec((tm, 128), lambda i: (i, 0)),
                   pl.BlockSpec((N_EXPERTS, 128), lambda i: (0, 0))],
        out_shape=[jax.ShapeDtypeStruct((t * ROW_SLABS, 128), jnp.int32),
                   jax.ShapeDtypeStruct((TOP_K, t), jnp.int32),
                   jax.ShapeDtypeStruct((TOP_K, t), jnp.int32),
                   jax.ShapeDtypeStruct((t, 128), F32),
                   jax.ShapeDtypeStruct((N_EXPERTS, 128), F32)],
        scratch_shapes=[pltpu.VMEM((N_EXPERTS, 1), F32), pltpu.VMEM((tm, tm), BF16)],
        compiler_params=pltpu.CompilerParams(dimension_semantics=("arbitrary",),
                                             vmem_limit_bytes=VMEM_LIMIT),
        name="router",
    )(x1, mod3, norm2_g, w_router_t, router_bias)


def _plan_kernel(ek_ref, rk_ref, cnt_ref, pos_ref, texp_ref, nused_ref, tend_ref, *, expert_rows):
    rows = float(expert_rows)
    cnt = cnt_ref[:, 0:1]
    tiles = jnp.floor((cnt + (rows - 1.0)) / rows)
    before = (lax.broadcasted_iota(jnp.int32, (N_EXPERTS, N_EXPERTS), 1)
              < lax.broadcasted_iota(jnp.int32, (N_EXPERTS, N_EXPERTS), 0))
    tile_start = jnp.dot(jnp.where(before, 1.0, 0.0), jnp.broadcast_to(tiles, (N_EXPERTS, 128)),
                         precision=lax.Precision.HIGHEST, preferred_element_type=F32)[:, 0:1]
    tile_end = tile_start + tiles
    row_start = tile_start * rows

    ek = ek_ref[...]
    pos = rk_ref[...].astype(F32)
    tile_id = lax.broadcasted_iota(jnp.int32, texp_ref.shape, 1).astype(F32)
    texp = jnp.zeros(texp_ref.shape, F32)
    for e in range(N_EXPERTS):
        pos = pos + jnp.where(ek == e, row_start[e:e + 1, :], 0.0)
        texp = texp + jnp.where(tile_id >= tile_end[e:e + 1, :], 1.0, 0.0)
    pos_ref[...] = pos.astype(jnp.int32)
    texp_ref[...] = jnp.minimum(texp, N_EXPERTS - 1.0).astype(jnp.int32)
    nused_ref[...] = jnp.broadcast_to(tile_end[N_EXPERTS - 1:N_EXPERTS, :], nused_ref.shape).astype(jnp.int32)
    tend_ref[...] = jnp.broadcast_to(tile_end, tend_ref.shape).astype(jnp.int32)


def _plan(ek, rk, cnt, n_tiles_pad, expert_rows):
    t = ek.shape[1]

    def full(shape):
        return pl.BlockSpec(shape, lambda: (0,) * len(shape))

    return pl.pallas_call(
        functools.partial(_plan_kernel, expert_rows=expert_rows),
        in_specs=[full(ek.shape), full(rk.shape), full(cnt.shape)],
        out_specs=[full((TOP_K, t)), full((1, n_tiles_pad)), full((1, 128)), full((N_EXPERTS, 128))],
        out_shape=[jax.ShapeDtypeStruct((TOP_K, t), jnp.int32),
                   jax.ShapeDtypeStruct((1, n_tiles_pad), jnp.int32),
                   jax.ShapeDtypeStruct((1, 128), jnp.int32),
                   jax.ShapeDtypeStruct((N_EXPERTS, 128), jnp.int32)],
        compiler_params=pltpu.CompilerParams(vmem_limit_bytes=VMEM_LIMIT),
        name="plan",
    )(ek, rk, cnt)


def _sc_mesh():
    return plsc.VectorSubcoreMesh(core_axis_name="c", subcore_axis_name="s")


def _sc_pack_weight_halves(w):
    e, k, n = w.shape
    k_half = k // 2
    rb = SC_PACK_BLOCK_WORDS // n
    units_per_expert = k_half // rb
    per_w = (e * units_per_expert) // SC_WORKERS
    lanes = SC_LANES

    @functools.partial(
        pl.kernel, out_type=jax.ShapeDtypeStruct((e * k_half, n), jnp.int32), mesh=_sc_mesh(),
        scratch_types=[pltpu.VMEM((rb, n), F32), pltpu.VMEM((rb, n), F32), pltpu.VMEM((rb, n), jnp.int32)],
        compiler_params=pltpu.CompilerParams(needs_layout_passes=False))
    def kern(w_hbm, out_hbm, a_v, b_v, o_v):
        wid = lax.axis_index("s") * SC_CORES + lax.axis_index("c")

        @pl.loop(0, per_w)
        def _(j):
            unit = wid * per_w + j
            expert = unit // units_per_expert
            blk = unit % units_per_expert
            row_a = expert * k + blk * rb
            pltpu.sync_copy(w_hbm.at[pl.ds(row_a, rb)], a_v)
            pltpu.sync_copy(w_hbm.at[pl.ds(row_a + k_half, rb)], b_v)

            @pl.loop(0, rb)
            def _(r):
                @plsc.parallel_loop(0, n, step=lanes, unroll=4)
                def _(c):
                    both = plsc.pack(a_v[r, pl.ds(c, lanes)], b_v[r, pl.ds(c, lanes)],
                                     format=plsc.PackFormat.INTERLEAVED)
                    o_v[r, pl.ds(c, lanes)] = plsc.bitcast(both, jnp.int32)

            pltpu.sync_copy(o_v, out_hbm.at[pl.ds(expert * k_half + blk * rb, rb)])

    return kern(w.reshape(e * k, n)).reshape(e, k_half, n)


def _sc_dispatch(rows, pos3, n_out, after=()):
    t = rows.shape[0]
    ch = SC_CHUNK
    per_w = (t // ch) // SC_WORKERS

    @functools.partial(
        pl.kernel, out_type=jax.ShapeDtypeStruct((n_out,) + rows.shape[1:], jnp.int32), mesh=_sc_mesh(),
        scratch_types=[pltpu.VMEM((TOP_K, ch), jnp.int32), pltpu.VMEM((ch,) + rows.shape[1:], jnp.int32),
                       pltpu.SemaphoreType.DMA])
    def k(rows_hbm, pos_hbm, *rest):
        out_hbm, idx_v, rows_v, sem = rest[len(after):]
        wid = lax.axis_index("s") * SC_CORES + lax.axis_index("c")

        @pl.loop(0, per_w)
        def _(j):
            c = wid * per_w + j
            pltpu.sync_copy(pos_hbm.at[c], idx_v)
            pltpu.sync_copy(rows_hbm.at[pl.ds(c * ch, ch)], rows_v)
            copies = [pltpu.async_copy(rows_v, out_hbm.at[idx_v.at[kk]], sem) for kk in range(TOP_K)]
            for cp in copies:
                cp.wait()

    return k(rows, pos3, *after)


def _sc_combine(table, pos3, wtok, t):
    ch = SC_CHUNK
    sub = SC_COMBINE_TOKENS
    lanes = SC_LANES
    slabs = ROW_SLABS
    per_w = (t // ch) // SC_WORKERS
    subs_per_chunk = ch // sub
    n_steps = per_w * subs_per_chunk

    @functools.partial(
        pl.kernel, out_type=jax.ShapeDtypeStruct((t, slabs, 128), jnp.int32), mesh=_sc_mesh(),
        scratch_types=[pltpu.VMEM((per_w, TOP_K, ch), jnp.int32),
                       pltpu.VMEM((2, TOP_K, sub, slabs, 128), jnp.int32),
                       pltpu.VMEM((2, sub, 128), F32),
                       pltpu.VMEM((sub, slabs, 128), jnp.int32),
                       pltpu.SemaphoreType.DMA((2,))],
        compiler_params=pltpu.CompilerParams(needs_layout_passes=False))
    def k(tab_hbm, pos_hbm, w_hbm, out_hbm, idx_v, rows_v, w_v, out_v, sem):
        wid = lax.axis_index("s") * SC_CORES + lax.axis_index("c")
        for j in range(per_w):
            pltpu.sync_copy(pos_hbm.at[wid * per_w + j], idx_v.at[j])

        def first_token(step):
            return (wid * per_w + step // subs_per_chunk) * ch + (step % subs_per_chunk) * sub

        def copies(step, slot):
            j = step // subs_per_chunk
            s = step % subs_per_chunk
            idx = [idx_v.at[j, kk, pl.ds(s * sub, sub)] for kk in range(TOP_K)]
            return ([pltpu.make_async_copy(tab_hbm.at[idx[kk]], rows_v.at[slot, kk], sem.at[slot])
                     for kk in range(TOP_K)]
                    + [pltpu.make_async_copy(w_hbm.at[pl.ds(first_token(step), sub)], w_v.at[slot], sem.at[slot])])

        for cp in copies(0, 0):
            cp.start()

        @pl.loop(0, n_steps)
        def _(step):
            slot = step % 2

            @pl.when(step + 1 < n_steps)
            def _():
                for cp in copies(step + 1, 1 - slot):
                    cp.start()

            for cp in copies(step, slot):
                cp.wait()

            @pl.loop(0, sub)
            def _(tt):
                wk = [w_v[slot, tt, pl.ds(kk * lanes, lanes)] for kk in range(TOP_K)]
                for sl in range(slabs):
                    @plsc.parallel_loop(0, 128, step=lanes, unroll=8)
                    def _(off):
                        acc_lo = jnp.zeros((lanes,), F32)
                        acc_hi = jnp.zeros((lanes,), F32)
                        for kk in range(TOP_K):
                            word = rows_v[slot, kk, tt, sl, pl.ds(off, lanes)]
                            lo = plsc.bitcast(word << 16, F32)
                            hi = plsc.bitcast(word & jnp.int32(-65536), F32)
                            acc_lo = acc_lo + wk[kk] * lo
                            acc_hi = acc_hi + wk[kk] * hi
                        both = plsc.pack(acc_lo, acc_hi, format=plsc.PackFormat.INTERLEAVED)
                        out_v[tt, sl, pl.ds(off, lanes)] = plsc.bitcast(both, jnp.int32)

            pltpu.sync_copy(out_v, out_hbm.at[pl.ds(first_token(step), sub)])

    return k(table, pos3, wtok)


def _experts_kernel(texp_ref, nused_ref, tend_ref, xs_ref, weg_hbm, weu_hbm, wed_hbm, ys_ref,
                    wg_scr, wu_scr, wd_scr, wg_buf, wu_buf, wd_buf, sem, group_scr, *, expert_rows):
    step = pl.program_id(0)
    rows = expert_rows
    tiles_per_step = EXPERT_TILES_PER_STEP
    half = D_MODEL // 2
    n_used = nused_ref[0]

    def weight_copies(e, slot):
        return [pltpu.make_async_copy(weg_hbm.at[e], wg_buf.at[slot], sem.at[slot, 0]),
                pltpu.make_async_copy(weu_hbm.at[e], wu_buf.at[slot], sem.at[slot, 1]),
                pltpu.make_async_copy(wed_hbm.at[e], wd_buf.at[slot], sem.at[slot, 2])]

    def next_group(e):
        tile = tend_ref[e]
        return texp_ref[jnp.minimum(tile, n_used - 1)], tile < n_used

    def start_weights(e, slot, exists):
        @pl.when(exists)
        def _():
            for cp in weight_copies(e, slot):
                cp.start()

    @pl.when(step == 0)
    def _():
        group_scr[0] = 0
        e, exists = texp_ref[0], True
        for slot in range(WEIGHT_SLOTS - 1):
            start_weights(e, slot, exists)
            nxt, has_next = next_group(e)
            e, exists = nxt, exists & has_next

    def row_tile(tile, x_view, y_view):
        expert = texp_ref[tile]
        used = tile < n_used
        new_expert = (tile == 0) | (expert != texp_ref[jnp.maximum(tile - 1, 0)])

        @pl.when(used & new_expert)
        def _():
            group = group_scr[0]
            slot = group % WEIGHT_SLOTS
            ahead, exists = expert, True
            for _ in range(WEIGHT_SLOTS - 1):
                nxt, has_next = next_group(ahead)
                ahead, exists = nxt, exists & has_next
            start_weights(ahead, (group + WEIGHT_SLOTS - 1) % WEIGHT_SLOTS, exists)

            for cp in weight_copies(expert, slot):
                cp.wait()
            for scr, buf in ((wg_scr, wg_buf), (wu_scr, wu_buf), (wd_scr, wd_buf)):
                top, bottom = _unpack_pair(buf[slot])
                k_half = top.shape[0]
                scr[0:k_half, :] = top.astype(BF16)
                scr[k_half:2 * k_half, :] = bottom.astype(BF16)
            group_scr[0] = group + 1

        @pl.when(used)
        def _():
            lo, hi = _unpack_pair(_load_token_words(x_view, (), rows))
            lo = lo.astype(BF16)
            hi = hi.astype(BF16)
            g = _dot(lo, wg_scr[0:half, :]) + _dot(hi, wg_scr[half:D_MODEL, :])
            u = _dot(lo, wu_scr[0:half, :]) + _dot(hi, wu_scr[half:D_MODEL, :])
            y = _dot((_silu(g) * u).astype(BF16), wd_scr[...])
            _store_token_words(y_view, _pack_pair(y[:, :half], y[:, half:]), rows)

        @pl.when(jnp.logical_not(used) & (step == (n_used - 1) // tiles_per_step))
        def _():
            y_view[...] = jnp.zeros_like(y_view)

    for s in range(tiles_per_step):
        view = pl.ds(s * rows * ROW_SLABS, rows * ROW_SLABS)
        row_tile(step * tiles_per_step + s, xs_ref.at[view], ys_ref.at[view])


def _experts(texp, nused, tend, xs2d, weg, weu, wed, n_tiles, expert_rows):
    tiles_per_step = EXPERT_TILES_PER_STEP
    block = (tiles_per_step * expert_rows * ROW_SLABS, 128)
    hbm = pl.BlockSpec(memory_space=pl.ANY)

    def block_idx(j, te, nu, tn):
        return (jnp.minimum(j, (nu[0] - 1) // tiles_per_step), 0)

    grid_spec = pltpu.PrefetchScalarGridSpec(
        num_scalar_prefetch=3,
        grid=(n_tiles // tiles_per_step,),
        in_specs=[pl.BlockSpec(block, block_idx), hbm, hbm, hbm],
        out_specs=pl.BlockSpec(block, block_idx),
        scratch_shapes=[pltpu.VMEM((D_MODEL, EXPERT_DIM), BF16),
                        pltpu.VMEM((D_MODEL, EXPERT_DIM), BF16),
                        pltpu.VMEM((EXPERT_DIM, D_MODEL), BF16),
                        pltpu.VMEM((WEIGHT_SLOTS,) + weg.shape[1:], jnp.int32),
                        pltpu.VMEM((WEIGHT_SLOTS,) + weu.shape[1:], jnp.int32),
                        pltpu.VMEM((WEIGHT_SLOTS,) + wed.shape[1:], jnp.int32),
                        pltpu.SemaphoreType.DMA((WEIGHT_SLOTS, 3)),
                        pltpu.SMEM((1,), jnp.int32)],
    )
    return pl.pallas_call(
        functools.partial(_experts_kernel, expert_rows=expert_rows),
        grid_spec=grid_spec,
        out_shape=jax.ShapeDtypeStruct(xs2d.shape, jnp.int32),
        compiler_params=pltpu.CompilerParams(dimension_semantics=("arbitrary",),
                                             vmem_limit_bytes=VMEM_LIMIT),
        name="experts",
    )(texp, nused, tend, xs2d, weg, weu, wed)


def _final_kernel(x_ref, routed_ref, mod_ref, g2_ref, wsg_ref, wsu_ref, wsd_ref, fng_ref, o_ref):
    tm = x_ref.shape[0]
    x = x_ref[...]
    hb = _rms_mod(x, g2_ref[...], mod_ref[0, 3:4, :], mod_ref[0, 4:5, :]).astype(BF16)
    shared = _dot((_silu(_dot(hb, wsg_ref[...])) * _dot(hb, wsu_ref[...])).astype(BF16), wsd_ref[...])
    routed = jnp.concatenate(_unpack_pair(_load_token_words(routed_ref, (), tm)), axis=1)
    y = x + mod_ref[0, 5:6, :] * (routed + shared)
    ms = jnp.mean(y * y, axis=-1, keepdims=True)
    o_ref[...] = y * lax.rsqrt(ms + EPS) * fng_ref[...]


def _final(x1, routed2d, mod3, norm2_g, wsg, wsu, wsd, final_g, seq_len, mod_row_of_batch):
    t = x1.shape[0]
    tm = TM_FINAL

    def mod_idx(i):
        return (mod_row_of_batch((i * tm) // seq_len), 0, 0)

    def full(a):
        return pl.BlockSpec(a.shape, lambda i: (0,) * a.ndim)

    return pl.pallas_call(
        _final_kernel,
        grid=(t // tm,),
        in_specs=[pl.BlockSpec((tm, D_MODEL), lambda i: (i, 0)),
                  pl.BlockSpec((tm * ROW_SLABS, 128), lambda i: (i, 0)),
                  pl.BlockSpec((1, 6, D_MODEL), mod_idx),
                  full(norm2_g), full(wsg), full(wsu), full(wsd), full(final_g)],
        out_specs=pl.BlockSpec((tm, D_MODEL), lambda i: (i, 0)),
        out_shape=jax.ShapeDtypeStruct((t, D_MODEL), F32),
        compiler_params=pltpu.CompilerParams(dimension_semantics=("parallel",),
                                             vmem_limit_bytes=VMEM_LIMIT),
        name="final",
    )(x1, routed2d, mod3, norm2_g, wsg, wsu, wsd, final_g)


def _moe(x1, mod3, lw, seq_len, mod_row_of_batch):
    t = x1.shape[0]
    expert_rows = min(MAX_EXPERT_ROWS, TOP_K * t // N_EXPERTS // 2)
    n_tiles = TOP_K * t // expert_rows + N_EXPERTS
    n_tiles_pad = -(-n_tiles // 128) * 128
    hp2d, ek, rk, wtok, cnt = _router(x1, mod3, lw["norm2_g"], lw["w_router_t"], lw["router_bias"],
                                      seq_len, mod_row_of_batch)
    pos, texp, nused, tend = _plan(ek, rk, cnt, n_tiles_pad, expert_rows)
    pos3 = pos.reshape(TOP_K, t // SC_CHUNK, SC_CHUNK).transpose(1, 0, 2)
    xs = _sc_dispatch(hp2d.reshape(t, ROW_SLABS, 128), pos3, n_tiles * expert_rows,
                      after=(lw["weg"], lw["weu"], lw["wed"]))
    ys2d = _experts(texp.reshape(-1), nused.reshape(-1), tend[:, 0], xs.reshape(-1, 128),
                    lw["weg"], lw["weu"], lw["wed"], n_tiles, expert_rows)
    routed = _sc_combine(ys2d.reshape(-1, ROW_SLABS, 128), pos3, wtok, t)
    return _final(x1, routed.reshape(t * ROW_SLABS, 128), mod3, lw["norm2_g"],
                  lw["wsg"], lw["wsu"], lw["wsd"], lw["final_g"], seq_len, mod_row_of_batch)


def _dft_tables(seq_len):
    gd = FOURIER_GROUP_DIM
    kc = np.arange(gd)
    ang_c = ((kc[:, None] * kc[None, :]) % gd) * (2.0 * math.pi / gd)
    cs = np.concatenate([np.cos(ang_c), np.sin(ang_c)], axis=1) * (gd ** -0.5)
    kl = np.arange(seq_len)
    ang_l = ((kl[:, None] * kl[None, :]) % seq_len) * (2.0 * math.pi / seq_len)
    cls = np.concatenate([np.cos(ang_l), -np.sin(ang_l)], axis=1) * (seq_len ** -0.5)
    return jnp.asarray(cs.astype(np.float32), dtype=BF16), jnp.asarray(cls.astype(np.float32), dtype=BF16)


def _rope_tables(length):
    rows = length // GRID_W
    r = np.repeat(np.arange(rows, dtype=np.float32), GRID_W)
    col = np.tile(np.arange(GRID_W, dtype=np.float32), rows)
    nf = RET_HEAD_DIM // 4
    inv = (np.float32(ROPE_BASE) ** (-np.arange(nf, dtype=np.float32) / np.float32(nf))).astype(np.float32)
    ar = r[:, None] * inv[None]
    ac = col[:, None] * inv[None]
    ang = np.concatenate([ar, ar, ac, ac], axis=-1).astype(np.float64)
    sign = np.where((np.arange(RET_HEAD_DIM) & nf) == 0, -1.0, 1.0)
    return (jnp.asarray(np.cos(ang).astype(np.float32)),
            jnp.asarray((np.sin(ang) * sign[None, :]).astype(np.float32)))


def _trunk_path(x, mod3, mod_row_of_batch, s0f, s0b, rope, lw):
    batch, seq_len, _ = x.shape
    x2d = x.reshape(batch * seq_len, D_MODEL)
    uf, q, k, v, sg, gf, gr = _inproj(x2d, mod3, lw["norm1_g"], lw["w_in"], seq_len, mod_row_of_batch, rope)
    r, s_f, s_b = _retention(q, k, v, sg, lw["dec"], lw["gn_g"], s0f, s0b, batch, seq_len)
    cs, cls = _dft_tables(seq_len)
    x1 = _fnet_merge(uf, cs, cls, r, gf, gr, x2d, mod3, lw["w_four"], lw["w_ret"], lw["w_o"],
                     batch, seq_len, mod_row_of_batch)
    y = _moe(x1, mod3, lw, seq_len, mod_row_of_batch)
    return y.reshape(batch, seq_len, D_MODEL), s_f, s_b


def kernel(x_prompt, x_sample, state_ret_fwd, state_ret_bwd, c, c_ctx, w_ada, b_ada, norm1_g, norm2_g, w_in,
           ret_decay_fwd, ret_decay_bwd, ret_gn_g, w_four_out, w_ret_out, w_out, w_router, router_bias,
           w_exp_gate, w_exp_up, w_exp_down, w_shared_gate, w_shared_up, w_shared_down, final_norm_g):
    depth = w_ada.shape[0]
    assert depth == 1, "final norm is fused into the last layer's MoE kernel"
    n_ctx, n_lat = x_prompt.shape[0], x_sample.shape[0]
    cond = jnp.concatenate([c_ctx[None, :], c], axis=0)
    cond = jnp.pad(cond, ((0, (-cond.shape[0]) % 8), (0, 0)))
    rope = _rope_tables(x_sample.shape[1])
    zeros = jnp.zeros((n_ctx, N_RET_HEADS, RET_HEAD_DIM, RET_HEAD_DIM), F32)

    layer = 0
    mod = _ada(cond, w_ada[layer], b_ada[layer][None, :])
    mod3 = mod.reshape(mod.shape[0], 6, D_MODEL)
    dec = jnp.stack([ret_decay_fwd[layer], ret_decay_bwd[layer]], axis=1)
    lw = {
        "norm1_g": norm1_g[layer][None, :],
        "norm2_g": norm2_g[layer][None, :],
        "w_in": w_in[layer],
        "dec": jnp.broadcast_to(dec[:, :, None], (N_RET_HEADS, 2, RET_HEAD_DIM)).astype(F32),
        "gn_g": ret_gn_g[layer][None, :],
        "w_four": w_four_out[layer],
        "w_ret": w_ret_out[layer],
        "w_o": w_out[layer],
        "w_router_t": w_router[layer].T,
        "router_bias": router_bias[layer][:, None],
        "weg": _sc_pack_weight_halves(w_exp_gate[layer]),
        "weu": _sc_pack_weight_halves(w_exp_up[layer]),
        "wed": _sc_pack_weight_halves(w_exp_down[layer]),
        "wsg": w_shared_gate[layer].astype(BF16),
        "wsu": w_shared_up[layer].astype(BF16),
        "wsd": w_shared_down[layer].astype(BF16),
        "final_g": final_norm_g[None, :],
    }
    y_prompt, s_f, s_b = _trunk_path(x_prompt, mod3, lambda b: 0, zeros, zeros, None, lw)
    y_sample, _, _ = _trunk_path(x_sample, mod3, lambda b: 1 + b, state_ret_fwd[:, layer],
                                 state_ret_bwd[:, layer], rope, lw)
    return (y_prompt, y_sample, s_f[:, None], s_b[:, None])
```

```python
import functools
import math

import jax
import jax.numpy as jnp
import numpy as np
from jax import lax
from jax.experimental import pallas as pl
from jax.experimental.pallas import tpu as pltpu
from jax.experimental.pallas import tpu_sc as plsc

F32 = jnp.float32
BF16 = jnp.bfloat16

D_MODEL = 1024
GRID_W = 64
N_FOURIER_GROUPS = 8
FOURIER_GROUP_DIM = 128
N_RET_HEADS = 4
RET_HEAD_DIM = 128
RET_WIDTH = N_RET_HEADS * RET_HEAD_DIM
CHUNK = 128
N_EXPERTS = 64
N_EXPERT_GROUPS = 8
EXPERTS_PER_GROUP = N_EXPERTS // N_EXPERT_GROUPS
TOPK_GROUPS = 4
TOP_K = 8
EXPERT_DIM = 256
ROUTED_SCALE = 2.5
ROPE_BASE = 10000.0
EPS = 1e-6
Q_SCALE = RET_HEAD_DIM ** -0.5

_C_UF = (0, 1024)
_C_Q = (1024, 1536)
_C_K = (1536, 2048)
_C_V = (2048, 2560)
_C_G = (2560, 3072)
_C_GF = (3072, 4096)
_C_GR = (4096, 5120)

VMEM_LIMIT = 56 * 1024 * 1024

TM_INPROJ = 1024
TM_ROUTER = 512
FNET_ROWS = 512
TM_FINAL = 1024
EXPERT_TILES_PER_STEP = 4
MAX_EXPERT_ROWS = 512
WEIGHT_SLOTS = 3
ROW_SLABS = 4
SC_CORES = 2
SC_WORKERS = 32
SC_CHUNK = 128
SC_LANES = 16
SC_PACK_BLOCK_WORDS = 16384
SC_COMBINE_TOKENS = 8


def _silu(x):
    return x * jax.nn.sigmoid(x)


def _dot(a, b):
    return jnp.dot(a, b, preferred_element_type=F32)


def _rms_mod(x, g, shift, scale):
    ms = jnp.mean(x * x, axis=-1, keepdims=True)
    y = x * lax.rsqrt(ms + EPS) * g
    return y * (1.0 + scale) + shift


def _ada_kernel(cond_ref, w_ref, b_ref, o_ref):
    s = _silu(cond_ref[...]).astype(BF16)
    o_ref[...] = _dot(s, w_ref[...].astype(BF16)) + b_ref[...]


def _ada(cond, w_ada, b_ada):
    rows, n = cond.shape[0], w_ada.shape[1]
    tn = 1536
    return pl.pallas_call(
        _ada_kernel,
        grid=(n // tn,),
        in_specs=[pl.BlockSpec((rows, D_MODEL), lambda j: (0, 0)),
                  pl.BlockSpec((D_MODEL, tn), lambda j: (0, j)),
                  pl.BlockSpec((1, tn), lambda j: (0, j))],
        out_specs=pl.BlockSpec((rows, tn), lambda j: (0, j)),
        out_shape=jax.ShapeDtypeStruct((rows, n), F32),
        compiler_params=pltpu.CompilerParams(vmem_limit_bytes=VMEM_LIMIT),
        name="ada",
    )(cond, w_ada, b_ada)


def _rope_head(x, cos, sin_signed, first_half):
    partner = jnp.where(first_half, pltpu.roll(x, 96, 1), pltpu.roll(x, 32, 1))
    return x * cos + partner * sin_signed


def _inproj_kernel(*refs, use_rope):
    if use_rope:
        x_ref, mod_ref, g_ref, w_ref, cos_ref, sin_ref = refs[:6]
        outs = refs[6:]
    else:
        x_ref, mod_ref, g_ref, w_ref = refs[:4]
        outs = refs[4:]
    uf_o, q_o, k_o, v_o, sg_o, gf_o, gr_o = outs

    h = _rms_mod(x_ref[...], g_ref[...], mod_ref[0, 0:1, :], mod_ref[0, 1:2, :])
    hb = h.astype(BF16)

    def proj(cols):
        return _dot(hb, w_ref[:, cols[0]:cols[1]].astype(BF16))

    uf_o[...] = proj(_C_UF).astype(BF16)
    q = proj(_C_Q)
    k = proj(_C_K)
    if use_rope:
        cos = cos_ref[...]
        sin_signed = sin_ref[...]
        lane = lax.broadcasted_iota(jnp.int32, cos.shape, 1)
        first_half = (lane & 32) == 0
        for hd in range(N_RET_HEADS):
            sl = slice(hd * RET_HEAD_DIM, (hd + 1) * RET_HEAD_DIM)
            q_o[:, sl] = (_rope_head(q[:, sl], cos, sin_signed, first_half) * Q_SCALE).astype(BF16)
            k_o[:, sl] = _rope_head(k[:, sl], cos, sin_signed, first_half).astype(BF16)
    else:
        q_o[...] = (q * Q_SCALE).astype(BF16)
        k_o[...] = k.astype(BF16)
    v_o[...] = proj(_C_V).astype(BF16)
    sg_o[...] = _silu(proj(_C_G)).astype(BF16)
    gf_o[...] = jax.nn.sigmoid(proj(_C_GF)).astype(BF16)
    gr_o[...] = jax.nn.sigmoid(proj(_C_GR)).astype(BF16)


def _inproj(x2d, mod3, norm_g, w_in_f32, seq_len, mod_row_of_batch, rope):
    t = x2d.shape[0]
    tm = TM_INPROJ
    tiles_per_seq = max(seq_len // tm, 1)

    def mod_idx(i):
        return (mod_row_of_batch((i * tm) // seq_len), 0, 0)

    in_specs = [pl.BlockSpec((tm, D_MODEL), lambda i: (i, 0)),
                pl.BlockSpec((1, 6, D_MODEL), mod_idx),
                pl.BlockSpec((1, D_MODEL), lambda i: (0, 0)),
                pl.BlockSpec(w_in_f32.shape, lambda i: (0, 0), pipeline_mode=pl.Buffered(1))]
    args = [x2d, mod3, norm_g, w_in_f32]
    if rope is not None:
        in_specs += [pl.BlockSpec((tm, RET_HEAD_DIM), lambda i: (i % tiles_per_seq, 0))] * 2
        args += list(rope)
    widths = [1024, RET_WIDTH, RET_WIDTH, RET_WIDTH, RET_WIDTH, 1024, 1024]
    return pl.pallas_call(
        functools.partial(_inproj_kernel, use_rope=rope is not None),
        grid=(t // tm,),
        in_specs=in_specs,
        out_specs=[pl.BlockSpec((tm, w), lambda i: (i, 0)) for w in widths],
        out_shape=[jax.ShapeDtypeStruct((t, w), BF16) for w in widths],
        compiler_params=pltpu.CompilerParams(dimension_semantics=("parallel",),
                                             vmem_limit_bytes=VMEM_LIMIT),
        name="inproj",
    )(*args)


def _retention_kernel(q_ref, k_ref, v_ref, sg_ref, dec_ref, gn_ref, s0f_ref, s0b_ref,
                      r_ref, sfo_ref, sbo_ref, tab_scr, gc_scr):
    n_chunks = q_ref.shape[0] // CHUNK
    hd = RET_HEAD_DIM

    @pl.when(pl.program_id(0) == 0)
    def _():
        row = lax.broadcasted_iota(jnp.int32, (CHUNK, CHUNK), 0).astype(F32)
        col = lax.broadcasted_iota(jnp.int32, (CHUNK, CHUNK), 1).astype(F32)
        diff = row - col
        for h in range(N_RET_HEADS):
            dec = dec_ref[h]
            lg = jnp.minimum(dec, 0.0) - jnp.log1p(jnp.exp(-jnp.abs(dec)))
            lgf = lg[0:1, :]
            lgb = lg[1:2, :]
            tab_scr[h, 0] = jnp.exp(jnp.where(diff >= 0, lgf * diff, lgb * (-diff)))
            tab_scr[h, 1] = jnp.exp(lgf * (row + 1.0))
            tab_scr[h, 2] = jnp.exp(lgb * (CHUNK - row))
            tab_scr[h, 3] = jnp.exp(lgf * (CHUNK - 1.0 - col))
            tab_scr[h, 4] = jnp.exp(lgb * col)
            gc_scr[h] = jnp.exp(lg * CHUNK)

    def rows(n):
        return slice(n * CHUNK, (n + 1) * CHUNK)

    for h in range(N_RET_HEADS):
        cols = slice(h * hd, (h + 1) * hd)
        decay, qw_f, qw_b, kwt_f, kwt_b = (tab_scr[h, i] for i in range(5))
        gc = gc_scr[h]
        gc_f = gc[0:1, :]
        gc_b = gc[1:2, :]

        kv_f, kv_b = [], []
        for n in range(n_chunks):
            kt = k_ref[rows(n), cols].astype(F32).T
            vn = v_ref[rows(n), cols]
            kv_f.append(_dot((kt * kwt_f).astype(BF16), vn))
            kv_b.append(_dot((kt * kwt_b).astype(BF16), vn))

        s = s0f_ref[h]
        prev_f = []
        for n in range(n_chunks):
            prev_f.append(s.astype(BF16))
            s = gc_f * s + kv_f[n]
        sfo_ref[h] = s
        s = s0b_ref[h]
        prev_b = [None] * n_chunks
        for n in reversed(range(n_chunks)):
            prev_b[n] = s.astype(BF16)
            s = gc_b * s + kv_b[n]
        sbo_ref[h] = s

        gn = gn_ref[:, cols]
        for n in range(n_chunks):
            qn = q_ref[rows(n), cols]
            qf = qn.astype(F32)
            scores = lax.dot_general(qn, k_ref[rows(n), cols], (((1,), (1,)), ((), ())),
                                     preferred_element_type=F32)
            o = _dot((scores * decay).astype(BF16), v_ref[rows(n), cols])
            o = o + _dot((qf * qw_f).astype(BF16), prev_f[n])
            o = o + _dot((qf * qw_b).astype(BF16), prev_b[n])
            mu = jnp.mean(o, axis=-1, keepdims=True)
            d = o - mu
            var = jnp.mean(d * d, axis=-1, keepdims=True)
            on = d * lax.rsqrt(var + EPS) * gn
            r_ref[rows(n), cols] = (on * sg_ref[rows(n), cols].astype(F32)).astype(BF16)


def _retention(q, k, v, sg, dec, gn_g, s0f, s0b, batch, seq_len):
    hd = RET_HEAD_DIM
    tok_spec = pl.BlockSpec((seq_len, RET_WIDTH), lambda b: (b, 0))
    st_spec = pl.BlockSpec((None, N_RET_HEADS, hd, hd), lambda b: (b, 0, 0, 0))
    st_shape = jax.ShapeDtypeStruct((batch, N_RET_HEADS, hd, hd), F32)
    return pl.pallas_call(
        _retention_kernel,
        grid=(batch,),
        in_specs=[tok_spec, tok_spec, tok_spec, tok_spec,
                  pl.BlockSpec(dec.shape, lambda b: (0, 0, 0)),
                  pl.BlockSpec(gn_g.shape, lambda b: (0, 0)),
                  st_spec, st_spec],
        out_specs=[tok_spec, st_spec, st_spec],
        out_shape=[jax.ShapeDtypeStruct((batch * seq_len, RET_WIDTH), BF16), st_shape, st_shape],
        scratch_shapes=[pltpu.VMEM((N_RET_HEADS, 5, CHUNK, CHUNK), F32),
                        pltpu.VMEM((N_RET_HEADS, 2, hd), F32)],
        compiler_params=pltpu.CompilerParams(dimension_semantics=("arbitrary",),
                                             vmem_limit_bytes=VMEM_LIMIT),
        name="retention",
    )(q, k, v, sg, dec, gn_g, s0f, s0b)


def _fnet_merge_kernel(uf_ref, cs_ref, cls_ref, r_ref, gf_ref, gr_ref, x_ref, mod_ref, wf_ref, wr_ref, wo_ref,
                       o_ref, xcs_ref):
    seq_len = uf_ref.shape[0]
    gd = FOURIER_GROUP_DIM

    @pl.when(pl.program_id(1) == 0)
    def _():
        for g in range(N_FOURIER_GROUPS):
            x = _dot(uf_ref[:, g * gd:(g + 1) * gd], cs_ref[...])
            xcs_ref[0:seq_len, g * gd:(g + 1) * gd] = x[:, :gd].astype(BF16)
            xcs_ref[seq_len:2 * seq_len, g * gd:(g + 1) * gd] = x[:, gd:].astype(BF16)

    f_mix = _dot(cls_ref[...], xcs_ref[...]).astype(BF16)
    f_out = _dot(f_mix, wf_ref[...].astype(BF16))
    r_out = _dot(r_ref[...], wr_ref[...].astype(BF16))
    merged = gf_ref[...].astype(F32) * f_out + gr_ref[...].astype(F32) * r_out
    mix = _dot(merged.astype(BF16), wo_ref[...].astype(BF16))
    o_ref[...] = x_ref[...] + mod_ref[0, 2:3, :] * mix


def _fnet_merge(uf, cs, cls, r, gf, gr, x2d, mod3, w_four, w_ret, w_o, batch, seq_len, mod_row_of_batch):
    rb = min(FNET_ROWS, seq_len)
    nr = seq_len // rb

    def tok(w):
        return pl.BlockSpec((rb, w), lambda b, i: (b * nr + i, 0))

    def full(a):
        return pl.BlockSpec(a.shape, lambda b, i: (0, 0))

    def once(a):
        return pl.BlockSpec(a.shape, lambda b, i: (0, 0), pipeline_mode=pl.Buffered(1))

    return pl.pallas_call(
        _fnet_merge_kernel,
        grid=(batch, nr),
        in_specs=[pl.BlockSpec((seq_len, D_MODEL), lambda b, i: (b, 0)),
                  full(cs),
                  pl.BlockSpec((rb, 2 * seq_len), lambda b, i: (i, 0)),
                  tok(RET_WIDTH), tok(D_MODEL), tok(D_MODEL), tok(D_MODEL),
                  pl.BlockSpec((1, 6, D_MODEL), lambda b, i: (mod_row_of_batch(b), 0, 0)),
                  once(w_four), once(w_ret), once(w_o)],
        out_specs=tok(D_MODEL),
        out_shape=jax.ShapeDtypeStruct((batch * seq_len, D_MODEL), F32),
        scratch_shapes=[pltpu.VMEM((2 * seq_len, D_MODEL), BF16)],
        compiler_params=pltpu.CompilerParams(dimension_semantics=("parallel", "arbitrary"),
                                             vmem_limit_bytes=VMEM_LIMIT),
        name="fnet_merge",
    )(uf, cs, cls, r, gf, gr, x2d, mod3, w_four, w_ret, w_o)


def _pack_pair(lo_f32, hi_f32):
    lo = lax.bitcast_convert_type(lo_f32.astype(BF16).astype(F32), jnp.uint32)
    hi = lax.bitcast_convert_type(hi_f32.astype(BF16).astype(F32), jnp.uint32)
    return lax.bitcast_convert_type((lo >> 16) | hi, jnp.int32)


def _unpack_pair(words_i32):
    w = lax.bitcast_convert_type(words_i32, jnp.uint32)
    lo = lax.bitcast_convert_type(w << 16, F32)
    hi = lax.bitcast_convert_type(w & jnp.uint32(0xFFFF0000), F32)
    return lo, hi


def _load_token_words(ref, lead, n_tok):
    parts = []
    for s in range(ROW_SLABS):
        idx = (pl.ds(s, n_tok, stride=ROW_SLABS), slice(None))
        parts.append(ref[lead + idx] if lead else ref[idx])
    return jnp.concatenate(parts, axis=1)


def _store_token_words(ref, words, n_tok):
    for s in range(ROW_SLABS):
        ref[pl.ds(s, n_tok, stride=ROW_SLABS), :] = words[:, s * 128:(s + 1) * 128]


def _route(scores, biased):
    tokens = scores.shape[1]
    neg = -jnp.inf
    epg = EXPERTS_PER_GROUP
    iota_g = lax.broadcasted_iota(jnp.int32, (epg, tokens), 0).astype(F32)

    def pick_first_max(cur, iota, size):
        m = jnp.max(cur, axis=0, keepdims=True)
        idx = jnp.min(jnp.where(cur == m, iota, float(size)), axis=0, keepdims=True)
        return m, idx, iota == idx

    group_scores = []
    for g in range(N_EXPERT_GROUPS):
        vals = biased[g * epg:(g + 1) * epg, :]
        m1, _, hit = pick_first_max(vals, iota_g, epg)
        m2 = jnp.max(jnp.where(hit, neg, vals), axis=0, keepdims=True)
        group_scores.append(m1 + m2)
    cur = jnp.concatenate(group_scores, axis=0)
    group_sel = jnp.zeros_like(cur)
    for _ in range(TOPK_GROUPS):
        _, _, hit = pick_first_max(cur, iota_g, N_EXPERT_GROUPS)
        group_sel = jnp.where(hit, 1.0, group_sel)
        cur = jnp.where(hit, neg, cur)
    masked = jnp.concatenate(
        [jnp.where(group_sel[g:g + 1, :] > 0.0, biased[g * epg:(g + 1) * epg, :], neg)
         for g in range(N_EXPERT_GROUPS)], axis=0)
    iota_e = lax.broadcasted_iota(jnp.int32, masked.shape, 0).astype(F32)
    sel = jnp.zeros_like(masked)
    cur = masked
    picks = []
    for _ in range(TOP_K):
        _, idx, hit = pick_first_max(cur, iota_e, N_EXPERTS)
        picks.append(idx)
        sel = jnp.where(hit, 1.0, sel)
        cur = jnp.where(hit, neg, cur)
    w = scores * sel
    return w / jnp.sum(w, axis=0, keepdims=True) * ROUTED_SCALE, sel, picks


def _router_kernel(x_ref, mod_ref, g2_ref, wrt_ref, rb_ref, hp_ref, ek_ref, rk_ref, wt_ref, cnt_ref,
                   run_scr, earlier_scr):
    tm = x_ref.shape[0]

    @pl.when(pl.program_id(0) == 0)
    def _():
        run_scr[...] = jnp.zeros_like(run_scr)
        earlier = (lax.broadcasted_iota(jnp.int32, (tm, tm), 0) < lax.broadcasted_iota(jnp.int32, (tm, tm), 1))
        earlier_scr[...] = jnp.where(earlier, 1.0, 0.0).astype(BF16)

    h = _rms_mod(x_ref[...], g2_ref[...], mod_ref[0, 3:4, :], mod_ref[0, 4:5, :])
    half = D_MODEL // 2
    _store_token_words(hp_ref, _pack_pair(h[:, :half], h[:, half:]), tm)

    def split(a):
        hi = a.astype(BF16)
        return hi, (a - hi.astype(F32)).astype(BF16)

    def dot_nt(a, b):
        return lax.dot_general(a, b, (((1,), (1,)), ((), ())), preferred_element_type=F32)

    h_hi, h_lo = split(h)
    w_hi, w_lo = split(wrt_ref[...])
    logits_t = dot_nt(w_hi, h_hi) + (dot_nt(w_hi, h_lo) + dot_nt(w_lo, h_hi))
    scores = jax.nn.sigmoid(logits_t)
    comb_t, sel, picks = _route(scores, scores + rb_ref[...])

    rank_t = _dot(sel.astype(BF16), earlier_scr[...]) + run_scr[...]
    run_scr[...] += jnp.sum(sel, axis=1, keepdims=True)
    cnt_ref[...] = jnp.broadcast_to(run_scr[...], cnt_ref.shape)

    iota_e = lax.broadcasted_iota(jnp.int32, sel.shape, 0).astype(F32)
    ranks, weights = [], []
    for idx in picks:
        hit = iota_e == idx
        ranks.append(jnp.sum(jnp.where(hit, rank_t, 0.0), axis=0, keepdims=True))
        weights.append(jnp.sum(jnp.where(hit, comb_t, 0.0), axis=0, keepdims=True))
    ek_ref[...] = jnp.concatenate(picks, axis=0).astype(jnp.int32)
    rk_ref[...] = jnp.concatenate(ranks, axis=0).astype(jnp.int32)
    w_rep = jnp.concatenate([jnp.broadcast_to(w, (SC_LANES, tm)) for w in weights], axis=0)
    wt_ref[...] = w_rep.T


def _router(x1, mod3, norm2_g, w_router_t, router_bias, seq_len, mod_row_of_batch):
    t = x1.shape[0]
    tm = TM_ROUTER

    def mod_idx(i):
        return (mod_row_of_batch((i * tm) // seq_len), 0, 0)

    def full(a):
        return pl.BlockSpec(a.shape, lambda i: (0,) * a.ndim)

    return pl.pallas_call(
        _router_kernel,
        grid=(t // tm,),
        in_specs=[pl.BlockSpec((tm, D_MODEL), lambda i: (i, 0)),
                  pl.BlockSpec((1, 6, D_MODEL), mod_idx),
                  full(norm2_g), full(w_router_t), full(router_bias)],
        out_specs=[pl.BlockSpec((tm * ROW_SLABS, 128), lambda i: (i, 0)),
                   pl.BlockSpec((TOP_K, tm), lambda i: (0, i)),
                   pl.BlockSpec((TOP_K, tm), lambda i: (0, i)),
                   pl.BlockSpec((tm, 128), lambda i: (i, 0)),
                   pl.BlockSpec((N_EXPERTS, 128), lambda i: (0, 0))],
        out_shape=[jax.ShapeDtypeStruct((t * ROW_SLABS, 128), jnp.int32),
                   jax.ShapeDtypeStruct((TOP_K, t), jnp.int32),
                   jax.ShapeDtypeStruct((TOP_K, t), jnp.int32),
                   jax.ShapeDtypeStruct((t, 128), F32),
                   jax.ShapeDtypeStruct((N_EXPERTS, 128), F32)],
        scratch_shapes=[pltpu.VMEM((N_EXPERTS, 1), F32), pltpu.VMEM((tm, tm), BF16)],
        compiler_params=pltpu.CompilerParams(dimension_semantics=("arbitrary",),
                                             vmem_limit_bytes=VMEM_LIMIT),
        name="router",
    )(x1, mod3, norm2_g, w_router_t, router_bias)


def _plan_kernel(ek_ref, rk_ref, cnt_ref, pos_ref, texp_ref, nused_ref, tend_ref, *, expert_rows):
    rows = float(expert_rows)
    cnt = cnt_ref[:, 0:1]
    tiles = jnp.floor((cnt + (rows - 1.0)) / rows)
    before = (lax.broadcasted_iota(jnp.int32, (N_EXPERTS, N_EXPERTS), 1)
              < lax.broadcasted_iota(jnp.int32, (N_EXPERTS, N_EXPERTS), 0))
    tile_start = jnp.dot(jnp.where(before, 1.0, 0.0), jnp.broadcast_to(tiles, (N_EXPERTS, 128)),
                         precision=lax.Precision.HIGHEST, preferred_element_type=F32)[:, 0:1]
    tile_end = tile_start + tiles
    row_start = tile_start * rows

    ek = ek_ref[...]
    pos = rk_ref[...].astype(F32)
    tile_id = lax.broadcasted_iota(jnp.int32, texp_ref.shape, 1).astype(F32)
    texp = jnp.zeros(texp_ref.shape, F32)
    for e in range(N_EXPERTS):
        pos = pos + jnp.where(ek == e, row_start[e:e + 1, :], 0.0)
        texp = texp + jnp.where(tile_id >= tile_end[e:e + 1, :], 1.0, 0.0)
    pos_ref[...] = pos.astype(jnp.int32)
    texp_ref[...] = jnp.minimum(texp, N_EXPERTS - 1.0).astype(jnp.int32)
    nused_ref[...] = jnp.broadcast_to(tile_end[N_EXPERTS - 1:N_EXPERTS, :], nused_ref.shape).astype(jnp.int32)
    tend_ref[...] = jnp.broadcast_to(tile_end, tend_ref.shape).astype(jnp.int32)


def _plan(ek, rk, cnt, n_tiles_pad, expert_rows):
    t = ek.shape[1]

    def full(shape):
        return pl.BlockSpec(shape, lambda: (0,) * len(shape))

    return pl.pallas_call(
        functools.partial(_plan_kernel, expert_rows=expert_rows),
        in_specs=[full(ek.shape), full(rk.shape), full(cnt.shape)],
        out_specs=[full((TOP_K, t)), full((1, n_tiles_pad)), full((1, 128)), full((N_EXPERTS, 128))],
        out_shape=[jax.ShapeDtypeStruct((TOP_K, t), jnp.int32),
                   jax.ShapeDtypeStruct((1, n_tiles_pad), jnp.int32),
                   jax.ShapeDtypeStruct((1, 128), jnp.int32),
                   jax.ShapeDtypeStruct((N_EXPERTS, 128), jnp.int32)],
        compiler_params=pltpu.CompilerParams(vmem_limit_bytes=VMEM_LIMIT),
        name="plan",
    )(ek, rk, cnt)


def _sc_mesh():
    return plsc.VectorSubcoreMesh(core_axis_name="c", subcore_axis_name="s")


def _sc_pack_weight_halves(w):
    e, k, n = w.shape
    k_half = k // 2
    rb = SC_PACK_BLOCK_WORDS // n
    units_per_expert = k_half // rb
    per_w = (e * units_per_expert) // SC_WORKERS
    lanes = SC_LANES

    @functools.partial(
        pl.kernel, out_type=jax.ShapeDtypeStruct((e * k_half, n), jnp.int32), mesh=_sc_mesh(),
        scratch_types=[pltpu.VMEM((rb, n), F32), pltpu.VMEM((rb, n), F32), pltpu.VMEM((rb, n), jnp.int32)],
        compiler_params=pltpu.CompilerParams(needs_layout_passes=False))
    def kern(w_hbm, out_hbm, a_v, b_v, o_v):
        wid = lax.axis_index("s") * SC_CORES + lax.axis_index("c")

        @pl.loop(0, per_w)
        def _(j):
            unit = wid * per_w + j
            expert = unit // units_per_expert
            blk = unit % units_per_expert
            row_a = expert * k + blk * rb
            pltpu.sync_copy(w_hbm.at[pl.ds(row_a, rb)], a_v)
            pltpu.sync_copy(w_hbm.at[pl.ds(row_a + k_half, rb)], b_v)

            @pl.loop(0, rb)
            def _(r):
                @plsc.parallel_loop(0, n, step=lanes, unroll=4)
                def _(c):
                    both = plsc.pack(a_v[r, pl.ds(c, lanes)], b_v[r, pl.ds(c, lanes)],
                                     format=plsc.PackFormat.INTERLEAVED)
                    o_v[r, pl.ds(c, lanes)] = plsc.bitcast(both, jnp.int32)

            pltpu.sync_copy(o_v, out_hbm.at[pl.ds(expert * k_half + blk * rb, rb)])

    return kern(w.reshape(e * k, n)).reshape(e, k_half, n)


def _sc_dispatch(rows, pos3, n_out, after=()):
    t = rows.shape[0]
    ch = SC_CHUNK
    per_w = (t // ch) // SC_WORKERS

    @functools.partial(
        pl.kernel, out_type=jax.ShapeDtypeStruct((n_out,) + rows.shape[1:], jnp.int32), mesh=_sc_mesh(),
        scratch_types=[pltpu.VMEM((TOP_K, ch), jnp.int32), pltpu.VMEM((ch,) + rows.shape[1:], jnp.int32),
                       pltpu.SemaphoreType.DMA])
    def k(rows_hbm, pos_hbm, *rest):
        out_hbm, idx_v, rows_v, sem = rest[len(after):]
        wid = lax.axis_index("s") * SC_CORES + lax.axis_index("c")

        @pl.loop(0, per_w)
        def _(j):
            c = wid * per_w + j
            pltpu.sync_copy(pos_hbm.at[c], idx_v)
            pltpu.sync_copy(rows_hbm.at[pl.ds(c * ch, ch)], rows_v)
            copies = [pltpu.async_copy(rows_v, out_hbm.at[idx_v.at[kk]], sem) for kk in range(TOP_K)]
            for cp in copies:
                cp.wait()

    return k(rows, pos3, *after)


def _sc_combine(table, pos3, wtok, t):
    ch = SC_CHUNK
    sub = SC_COMBINE_TOKENS
    lanes = SC_LANES
    slabs = ROW_SLABS
    per_w = (t // ch) // SC_WORKERS
    subs_per_chunk = ch // sub
    n_steps = per_w * subs_per_chunk

    @functools.partial(
        pl.kernel, out_type=jax.ShapeDtypeStruct((t, slabs, 128), jnp.int32), mesh=_sc_mesh(),
        scratch_types=[pltpu.VMEM((per_w, TOP_K, ch), jnp.int32),
                       pltpu.VMEM((2, TOP_K, sub, slabs, 128), jnp.int32),
                       pltpu.VMEM((2, sub, 128), F32),
                       pltpu.VMEM((sub, slabs, 128), jnp.int32),
                       pltpu.SemaphoreType.DMA((2,))],
        compiler_params=pltpu.CompilerParams(needs_layout_passes=False))
    def k(tab_hbm, pos_hbm, w_hbm, out_hbm, idx_v, rows_v, w_v, out_v, sem):
        wid = lax.axis_index("s") * SC_CORES + lax.axis_index("c")
        for j in range(per_w):
            pltpu.sync_copy(pos_hbm.at[wid * per_w + j], idx_v.at[j])

        def first_token(step):
            return (wid * per_w + step // subs_per_chunk) * ch + (step % subs_per_chunk) * sub

        def copies(step, slot):
            j = step // subs_per_chunk
            s = step % subs_per_chunk
            idx = [idx_v.at[j, kk, pl.ds(s * sub, sub)] for kk in range(TOP_K)]
            return ([pltpu.make_async_copy(tab_hbm.at[idx[kk]], rows_v.at[slot, kk], sem.at[slot])
                     for kk in range(TOP_K)]
                    + [pltpu.make_async_copy(w_hbm.at[pl.ds(first_token(step), sub)], w_v.at[slot], sem.at[slot])])

        for cp in copies(0, 0):
            cp.start()

        @pl.loop(0, n_steps)
        def _(step):
            slot = step % 2

            @pl.when(step + 1 < n_steps)
            def _():
                for cp in copies(step + 1, 1 - slot):
                    cp.start()

            for cp in copies(step, slot):
                cp.wait()

            @pl.loop(0, sub)
            def _(tt):
                wk = [w_v[slot, tt, pl.ds(kk * lanes, lanes)] for kk in range(TOP_K)]
                for sl in range(slabs):
                    @plsc.parallel_loop(0, 128, step=lanes, unroll=8)
                    def _(off):
                        acc_lo = jnp.zeros((lanes,), F32)
                        acc_hi = jnp.zeros((lanes,), F32)
                        for kk in range(TOP_K):
                            word = rows_v[slot, kk, tt, sl, pl.ds(off, lanes)]
                            lo = plsc.bitcast(word << 16, F32)
                            hi = plsc.bitcast(word & jnp.int32(-65536), F32)
                            acc_lo = acc_lo + wk[kk] * lo
                            acc_hi = acc_hi + wk[kk] * hi
                        both = plsc.pack(acc_lo, acc_hi, format=plsc.PackFormat.INTERLEAVED)
                        out_v[tt, sl, pl.ds(off, lanes)] = plsc.bitcast(both, jnp.int32)

            pltpu.sync_copy(out_v, out_hbm.at[pl.ds(first_token(step), sub)])

    return k(table, pos3, wtok)


def _experts_kernel(texp_ref, nused_ref, tend_ref, xs_ref, weg_hbm, weu_hbm, wed_hbm, ys_ref,
                    wg_scr, wu_scr, wd_scr, wg_buf, wu_buf, wd_buf, sem, group_scr, *, expert_rows):
    step = pl.program_id(0)
    rows = expert_rows
    tiles_per_step = EXPERT_TILES_PER_STEP
    half = D_MODEL // 2
    n_used = nused_ref[0]

    def weight_copies(e, slot):
        return [pltpu.make_async_copy(weg_hbm.at[e], wg_buf.at[slot], sem.at[slot, 0]),
                pltpu.make_async_copy(weu_hbm.at[e], wu_buf.at[slot], sem.at[slot, 1]),
                pltpu.make_async_copy(wed_hbm.at[e], wd_buf.at[slot], sem.at[slot, 2])]

    def next_group(e):
        tile = tend_ref[e]
        return texp_ref[jnp.minimum(tile, n_used - 1)], tile < n_used

    def start_weights(e, slot, exists):
        @pl.when(exists)
        def _():
            for cp in weight_copies(e, slot):
                cp.start(priority=1)

    @pl.when(step == 0)
    def _():
        group_scr[0] = 0
        e, exists = texp_ref[0], True
        for slot in range(WEIGHT_SLOTS - 1):
            start_weights(e, slot, exists)
            nxt, has_next = next_group(e)
            e, exists = nxt, exists & has_next

    def row_tile(tile, x_view, y_view):
        expert = texp_ref[tile]
        used = tile < n_used
        new_expert = (tile == 0) | (expert != texp_ref[jnp.maximum(tile - 1, 0)])

        @pl.when(used & new_expert)
        def _():
            group = group_scr[0]
            slot = group % WEIGHT_SLOTS
            ahead, exists = expert, True
            for _ in range(WEIGHT_SLOTS - 1):
                nxt, has_next = next_group(ahead)
                ahead, exists = nxt, exists & has_next
            start_weights(ahead, (group + WEIGHT_SLOTS - 1) % WEIGHT_SLOTS, exists)

            for cp in weight_copies(expert, slot):
                cp.wait()
            for scr, buf in ((wg_scr, wg_buf), (wu_scr, wu_buf), (wd_scr, wd_buf)):
                top, bottom = _unpack_pair(buf[slot])
                k_half = top.shape[0]
                scr[0:k_half, :] = top.astype(BF16)
                scr[k_half:2 * k_half, :] = bottom.astype(BF16)
            group_scr[0] = group + 1

        @pl.when(used)
        def _():
            lo, hi = _unpack_pair(_load_token_words(x_view, (), rows))
            lo = lo.astype(BF16)
            hi = hi.astype(BF16)
            g = _dot(lo, wg_scr[0:half, :]) + _dot(hi, wg_scr[half:D_MODEL, :])
            u = _dot(lo, wu_scr[0:half, :]) + _dot(hi, wu_scr[half:D_MODEL, :])
            y = _dot((_silu(g) * u).astype(BF16), wd_scr[...])
            _store_token_words(y_view, _pack_pair(y[:, :half], y[:, half:]), rows)

        @pl.when(jnp.logical_not(used) & (step == (n_used - 1) // tiles_per_step))
        def _():
            y_view[...] = jnp.zeros_like(y_view)

    for s in range(tiles_per_step):
        view = pl.ds(s * rows * ROW_SLABS, rows * ROW_SLABS)
        row_tile(step * tiles_per_step + s, xs_ref.at[view], ys_ref.at[view])


def _experts(texp, nused, tend, xs2d, weg, weu, wed, n_tiles, expert_rows):
    tiles_per_step = EXPERT_TILES_PER_STEP
    block = (tiles_per_step * expert_rows * ROW_SLABS, 128)
    hbm = pl.BlockSpec(memory_space=pl.ANY)

    def block_idx(j, te, nu, tn):
        return (jnp.minimum(j, (nu[0] - 1) // tiles_per_step), 0)

    grid_spec = pltpu.PrefetchScalarGridSpec(
        num_scalar_prefetch=3,
        grid=(n_tiles // tiles_per_step,),
        in_specs=[pl.BlockSpec(block, block_idx), hbm, hbm, hbm],
        out_specs=pl.BlockSpec(block, block_idx),
        scratch_shapes=[pltpu.VMEM((D_MODEL, EXPERT_DIM), BF16),
                        pltpu.VMEM((D_MODEL, EXPERT_DIM), BF16),
                        pltpu.VMEM((EXPERT_DIM, D_MODEL), BF16),
                        pltpu.VMEM((WEIGHT_SLOTS,) + weg.shape[1:], jnp.int32),
                        pltpu.VMEM((WEIGHT_SLOTS,) + weu.shape[1:], jnp.int32),
                        pltpu.VMEM((WEIGHT_SLOTS,) + wed.shape[1:], jnp.int32),
                        pltpu.SemaphoreType.DMA((WEIGHT_SLOTS, 3)),
                        pltpu.SMEM((1,), jnp.int32)],
    )
    return pl.pallas_call(
        functools.partial(_experts_kernel, expert_rows=expert_rows),
        grid_spec=grid_spec,
        out_shape=jax.ShapeDtypeStruct(xs2d.shape, jnp.int32),
        compiler_params=pltpu.CompilerParams(dimension_semantics=("arbitrary",),
                                             vmem_limit_bytes=VMEM_LIMIT),
        name="experts",
    )(texp, nused, tend, xs2d, weg, weu, wed)


def _final_kernel(x_ref, routed_ref, mod_ref, g2_ref, wsg_ref, wsu_ref, wsd_ref, fng_ref, o_ref):
    tm = x_ref.shape[0]
    x = x_ref[...]
    hb = _rms_mod(x, g2_ref[...], mod_ref[0, 3:4, :], mod_ref[0, 4:5, :]).astype(BF16)
    shared = _dot((_silu(_dot(hb, wsg_ref[...])) * _dot(hb, wsu_ref[...])).astype(BF16), wsd_ref[...])
    routed = jnp.concatenate(_unpack_pair(_load_token_words(routed_ref, (), tm)), axis=1)
    y = x + mod_ref[0, 5:6, :] * (routed + shared)
    ms = jnp.mean(y * y, axis=-1, keepdims=True)
    o_ref[...] = y * lax.rsqrt(ms + EPS) * fng_ref[...]


def _final(x1, routed2d, mod3, norm2_g, wsg, wsu, wsd, final_g, seq_len, mod_row_of_batch):
    t = x1.shape[0]
    tm = TM_FINAL

    def mod_idx(i):
        return (mod_row_of_batch((i * tm) // seq_len), 0, 0)

    def full(a):
        return pl.BlockSpec(a.shape, lambda i: (0,) * a.ndim)

    return pl.pallas_call(
        _final_kernel,
        grid=(t // tm,),
        in_specs=[pl.BlockSpec((tm, D_MODEL), lambda i: (i, 0)),
                  pl.BlockSpec((tm * ROW_SLABS, 128), lambda i: (i, 0)),
                  pl.BlockSpec((1, 6, D_MODEL), mod_idx),
                  full(norm2_g), full(wsg), full(wsu), full(wsd), full(final_g)],
        out_specs=pl.BlockSpec((tm, D_MODEL), lambda i: (i, 0)),
        out_shape=jax.ShapeDtypeStruct((t, D_MODEL), F32),
        compiler_params=pltpu.CompilerParams(dimension_semantics=("parallel",),
                                             vmem_limit_bytes=VMEM_LIMIT),
        name="final",
    )(x1, routed2d, mod3, norm2_g, wsg, wsu, wsd, final_g)


def _moe(x1, mod3, lw, seq_len, mod_row_of_batch):
    t = x1.shape[0]
    expert_rows = min(MAX_EXPERT_ROWS, TOP_K * t // N_EXPERTS // 2)
    n_tiles = TOP_K * t // expert_rows + N_EXPERTS
    n_tiles_pad = -(-n_tiles // 128) * 128
    hp2d, ek, rk, wtok, cnt = _router(x1, mod3, lw["norm2_g"], lw["w_router_t"], lw["router_bias"],
                                      seq_len, mod_row_of_batch)
    pos, texp, nused, tend = _plan(ek, rk, cnt, n_tiles_pad, expert_rows)
    pos3 = pos.reshape(TOP_K, t // SC_CHUNK, SC_CHUNK).transpose(1, 0, 2)
    xs = _sc_dispatch(hp2d.reshape(t, ROW_SLABS, 128), pos3, n_tiles * expert_rows,
                      after=(lw["weg"], lw["weu"], lw["wed"]))
    ys2d = _experts(texp.reshape(-1), nused.reshape(-1), tend[:, 0], xs.reshape(-1, 128),
                    lw["weg"], lw["weu"], lw["wed"], n_tiles, expert_rows)
    routed = _sc_combine(ys2d.reshape(-1, ROW_SLABS, 128), pos3, wtok, t)
    return _final(x1, routed.reshape(t * ROW_SLABS, 128), mod3, lw["norm2_g"],
                  lw["wsg"], lw["wsu"], lw["wsd"], lw["final_g"], seq_len, mod_row_of_batch)


def _dft_tables(seq_len):
    gd = FOURIER_GROUP_DIM
    kc = np.arange(gd)
    ang_c = ((kc[:, None] * kc[None, :]) % gd) * (2.0 * math.pi / gd)
    cs = np.concatenate([np.cos(ang_c), np.sin(ang_c)], axis=1) * (gd ** -0.5)
    kl = np.arange(seq_len)
    ang_l = ((kl[:, None] * kl[None, :]) % seq_len) * (2.0 * math.pi / seq_len)
    cls = np.concatenate([np.cos(ang_l), -np.sin(ang_l)], axis=1) * (seq_len ** -0.5)
    return jnp.asarray(cs.astype(np.float32), dtype=BF16), jnp.asarray(cls.astype(np.float32), dtype=BF16)


def _rope_tables(length):
    rows = length // GRID_W
    r = np.repeat(np.arange(rows, dtype=np.float32), GRID_W)
    col = np.tile(np.arange(GRID_W, dtype=np.float32), rows)
    nf = RET_HEAD_DIM // 4
    inv = (np.float32(ROPE_BASE) ** (-np.arange(nf, dtype=np.float32) / np.float32(nf))).astype(np.float32)
    ar = r[:, None] * inv[None]
    ac = col[:, None] * inv[None]
    ang = np.concatenate([ar, ar, ac, ac], axis=-1).astype(np.float64)
    sign = np.where((np.arange(RET_HEAD_DIM) & nf) == 0, -1.0, 1.0)
    return (jnp.asarray(np.cos(ang).astype(np.float32)),
            jnp.asarray((np.sin(ang) * sign[None, :]).astype(np.float32)))


def _trunk_path(x, mod3, mod_row_of_batch, s0f, s0b, rope, lw):
    batch, seq_len, _ = x.shape
    x2d = x.reshape(batch * seq_len, D_MODEL)
    uf, q, k, v, sg, gf, gr = _inproj(x2d, mod3, lw["norm1_g"], lw["w_in"], seq_len, mod_row_of_batch, rope)
    r, s_f, s_b = _retention(q, k, v, sg, lw["dec"], lw["gn_g"], s0f, s0b, batch, seq_len)
    cs, cls = _dft_tables(seq_len)
    x1 = _fnet_merge(uf, cs, cls, r, gf, gr, x2d, mod3, lw["w_four"], lw["w_ret"], lw["w_o"],
                     batch, seq_len, mod_row_of_batch)
    y = _moe(x1, mod3, lw, seq_len, mod_row_of_batch)
    return y.reshape(batch, seq_len, D_MODEL), s_f, s_b


def kernel(x_prompt, x_sample, state_ret_fwd, state_ret_bwd, c, c_ctx, w_ada, b_ada, norm1_g, norm2_g, w_in,
           ret_decay_fwd, ret_decay_bwd, ret_gn_g, w_four_out, w_ret_out, w_out, w_router, router_bias,
           w_exp_gate, w_exp_up, w_exp_down, w_shared_gate, w_shared_up, w_shared_down, final_norm_g):
    depth = w_ada.shape[0]
    assert depth == 1, "final norm is fused into the last layer's MoE kernel"
    n_ctx, n_lat = x_prompt.shape[0], x_sample.shape[0]
    cond = jnp.concatenate([c_ctx[None, :], c], axis=0)
    cond = jnp.pad(cond, ((0, (-cond.shape[0]) % 8), (0, 0)))
    rope = _rope_tables(x_sample.shape[1])
    zeros = jnp.zeros((n_ctx, N_RET_HEADS, RET_HEAD_DIM, RET_HEAD_DIM), F32)

    layer = 0
    mod = _ada(cond, w_ada[layer], b_ada[layer][None, :])
    mod3 = mod.reshape(mod.shape[0], 6, D_MODEL)
    dec = jnp.stack([ret_decay_fwd[layer], ret_decay_bwd[layer]], axis=1)
    lw = {
        "norm1_g": norm1_g[layer][None, :],
        "norm2_g": norm2_g[layer][None, :],
        "w_in": w_in[layer],
        "dec": jnp.broadcast_to(dec[:, :, None], (N_RET_HEADS, 2, RET_HEAD_DIM)).astype(F32),
        "gn_g": ret_gn_g[layer][None, :],
        "w_four": w_four_out[layer],
        "w_ret": w_ret_out[layer],
        "w_o": w_out[layer],
        "w_router_t": w_router[layer].T,
        "router_bias": router_bias[layer][:, None],
        "weg": _sc_pack_weight_halves(w_exp_gate[layer]),
        "weu": _sc_pack_weight_halves(w_exp_up[layer]),
        "wed": _sc_pack_weight_halves(w_exp_down[layer]),
        "wsg": w_shared_gate[layer].astype(BF16),
        "wsu": w_shared_up[layer].astype(BF16),
        "wsd": w_shared_down[layer].astype(BF16),
        "final_g": final_norm_g[None, :],
    }
    y_prompt, s_f, s_b = _trunk_path(x_prompt, mod3, lambda b: 0, zeros, zeros, None, lw)
    y_sample, _, _ = _trunk_path(x_sample, mod3, lambda b: 1 + b, state_ret_fwd[:, layer],
                                 state_ret_bwd[:, layer], rope, lw)
    return (y_prompt, y_sample, s_f[:, None], s_b[:, None])
```

```python
import functools
import math

import jax
import jax.numpy as jnp
import numpy as np
from jax import lax
from jax.experimental import pallas as pl
from jax.experimental.pallas import tpu as pltpu
from jax.experimental.pallas import tpu_sc as plsc

F32 = jnp.float32
BF16 = jnp.bfloat16

D_MODEL = 1024
GRID_W = 64
N_FOURIER_GROUPS = 8
FOURIER_GROUP_DIM = 128
N_RET_HEADS = 4
RET_HEAD_DIM = 128
RET_WIDTH = N_RET_HEADS * RET_HEAD_DIM
CHUNK = 128
N_EXPERTS = 64
N_EXPERT_GROUPS = 8
EXPERTS_PER_GROUP = N_EXPERTS // N_EXPERT_GROUPS
TOPK_GROUPS = 4
TOP_K = 8
EXPERT_DIM = 256
ROUTED_SCALE = 2.5
ROPE_BASE = 10000.0
EPS = 1e-6
Q_SCALE = RET_HEAD_DIM ** -0.5

_C_UF = (0, 1024)
_C_Q = (1024, 1536)
_C_K = (1536, 2048)
_C_V = (2048, 2560)
_C_G = (2560, 3072)
_C_GF = (3072, 4096)
_C_GR = (4096, 5120)

VMEM_LIMIT = 56 * 1024 * 1024

TM_INPROJ = 1024
TM_ROUTER = 512
FNET_ROWS = 512
TM_FINAL = 1024
EXPERT_TILES_PER_STEP = 4
MAX_EXPERT_ROWS = 512
WEIGHT_SLOTS = 3
ROW_SLABS = 4
SC_CORES = 2
SC_WORKERS = 32
SC_CHUNK = 128
SC_LANES = 16
SC_PACK_BLOCK_WORDS = 16384
SC_COMBINE_TOKENS = 8


def _silu(x):
    return x * jax.nn.sigmoid(x)


def _dot(a, b):
    return jnp.dot(a, b, preferred_element_type=F32)


def _rms_mod(x, g, shift, scale):
    ms = jnp.mean(x * x, axis=-1, keepdims=True)
    y = x * lax.rsqrt(ms + EPS) * g
    return y * (1.0 + scale) + shift


def _ada_kernel(cond_ref, w_ref, b_ref, o_ref):
    s = _silu(cond_ref[...]).astype(BF16)
    o_ref[...] = _dot(s, w_ref[...].astype(BF16)) + b_ref[...]


def _ada(cond, w_ada, b_ada):
    rows, n = cond.shape[0], w_ada.shape[1]
    tn = 1536
    return pl.pallas_call(
        _ada_kernel,
        grid=(n // tn,),
        in_specs=[pl.BlockSpec((rows, D_MODEL), lambda j: (0, 0)),
                  pl.BlockSpec((D_MODEL, tn), lambda j: (0, j)),
                  pl.BlockSpec((1, tn), lambda j: (0, j))],
        out_specs=pl.BlockSpec((rows, tn), lambda j: (0, j)),
        out_shape=jax.ShapeDtypeStruct((rows, n), F32),
        compiler_params=pltpu.CompilerParams(vmem_limit_bytes=VMEM_LIMIT),
        name="ada",
    )(cond, w_ada, b_ada)


def _rope_head(x, cos, sin_signed, first_half):
    partner = jnp.where(first_half, pltpu.roll(x, 96, 1), pltpu.roll(x, 32, 1))
    return x * cos + partner * sin_signed


def _inproj_kernel(*refs, use_rope):
    if use_rope:
        x_ref, mod_ref, g_ref, w_ref, cos_ref, sin_ref = refs[:6]
        outs = refs[6:]
    else:
        x_ref, mod_ref, g_ref, w_ref = refs[:4]
        outs = refs[4:]
    uf_o, q_o, k_o, v_o, sg_o, gf_o, gr_o = outs

    h = _rms_mod(x_ref[...], g_ref[...], mod_ref[0, 0:1, :], mod_ref[0, 1:2, :])
    hb = h.astype(BF16)

    def proj(cols):
        return _dot(hb, w_ref[:, cols[0]:cols[1]].astype(BF16))

    uf_o[...] = proj(_C_UF).astype(BF16)
    q = proj(_C_Q)
    k = proj(_C_K)
    if use_rope:
        cos = cos_ref[...]
        sin_signed = sin_ref[...]
        lane = lax.broadcasted_iota(jnp.int32, cos.shape, 1)
        first_half = (lane & 32) == 0
        for hd in range(N_RET_HEADS):
            sl = slice(hd * RET_HEAD_DIM, (hd + 1) * RET_HEAD_DIM)
            q_o[:, sl] = (_rope_head(q[:, sl], cos, sin_signed, first_half) * Q_SCALE).astype(BF16)
            k_o[:, sl] = _rope_head(k[:, sl], cos, sin_signed, first_half).astype(BF16)
    else:
        q_o[...] = (q * Q_SCALE).astype(BF16)
        k_o[...] = k.astype(BF16)
    v_o[...] = proj(_C_V).astype(BF16)
    sg_o[...] = _silu(proj(_C_G)).astype(BF16)
    gf_o[...] = jax.nn.sigmoid(proj(_C_GF)).astype(BF16)
    gr_o[...] = jax.nn.sigmoid(proj(_C_GR)).astype(BF16)


def _inproj(x2d, mod3, norm_g, w_in_f32, seq_len, mod_row_of_batch, rope):
    t = x2d.shape[0]
    tm = TM_INPROJ
    tiles_per_seq = max(seq_len // tm, 1)

    def mod_idx(i):
        return (mod_row_of_batch((i * tm) // seq_len), 0, 0)

    in_specs = [pl.BlockSpec((tm, D_MODEL), lambda i: (i, 0)),
                pl.BlockSpec((1, 6, D_MODEL), mod_idx),
                pl.BlockSpec((1, D_MODEL), lambda i: (0, 0)),
                pl.BlockSpec(w_in_f32.shape, lambda i: (0, 0), pipeline_mode=pl.Buffered(1))]
    args = [x2d, mod3, norm_g, w_in_f32]
    if rope is not None:
        in_specs += [pl.BlockSpec((tm, RET_HEAD_DIM), lambda i: (i % tiles_per_seq, 0))] * 2
        args += list(rope)
    widths = [1024, RET_WIDTH, RET_WIDTH, RET_WIDTH, RET_WIDTH, 1024, 1024]
    return pl.pallas_call(
        functools.partial(_inproj_kernel, use_rope=rope is not None),
        grid=(t // tm,),
        in_specs=in_specs,
        out_specs=[pl.BlockSpec((tm, w), lambda i: (i, 0)) for w in widths],
        out_shape=[jax.ShapeDtypeStruct((t, w), BF16) for w in widths],
        compiler_params=pltpu.CompilerParams(dimension_semantics=("parallel",),
                                             vmem_limit_bytes=VMEM_LIMIT),
        name="inproj",
    )(*args)


def _retention_kernel(q_ref, k_ref, v_ref, sg_ref, dec_ref, gn_ref, s0f_ref, s0b_ref,
                      r_ref, sfo_ref, sbo_ref, tab_scr, gc_scr):
    n_chunks = q_ref.shape[0] // CHUNK
    hd = RET_HEAD_DIM

    @pl.when(pl.program_id(0) == 0)
    def _():
        row = lax.broadcasted_iota(jnp.int32, (CHUNK, CHUNK), 0).astype(F32)
        col = lax.broadcasted_iota(jnp.int32, (CHUNK, CHUNK), 1).astype(F32)
        diff = row - col
        for h in range(N_RET_HEADS):
            dec = dec_ref[h]
            lg = jnp.minimum(dec, 0.0) - jnp.log1p(jnp.exp(-jnp.abs(dec)))
            lgf = lg[0:1, :]
            lgb = lg[1:2, :]
            tab_scr[h, 0] = jnp.exp(jnp.where(diff >= 0, lgf * diff, lgb * (-diff)))
            tab_scr[h, 1] = jnp.exp(lgf * (row + 1.0))
            tab_scr[h, 2] = jnp.exp(lgb * (CHUNK - row))
            tab_scr[h, 3] = jnp.exp(lgf * (CHUNK - 1.0 - col))
            tab_scr[h, 4] = jnp.exp(lgb * col)
            gc_scr[h] = jnp.exp(lg * CHUNK)

    def rows(n):
        return slice(n * CHUNK, (n + 1) * CHUNK)

    for h in range(N_RET_HEADS):
        cols = slice(h * hd, (h + 1) * hd)
        decay, qw_f, qw_b, kwt_f, kwt_b = (tab_scr[h, i] for i in range(5))
        gc = gc_scr[h]
        gc_f = gc[0:1, :]
        gc_b = gc[1:2, :]

        kv_f, kv_b = [], []
        for n in range(n_chunks):
            kt = k_ref[rows(n), cols].astype(F32).T
            vn = v_ref[rows(n), cols]
            kv_f.append(_dot((kt * kwt_f).astype(BF16), vn))
            kv_b.append(_dot((kt * kwt_b).astype(BF16), vn))

        s = s0f_ref[h]
        prev_f = []
        for n in range(n_chunks):
            prev_f.append(s.astype(BF16))
            s = gc_f * s + kv_f[n]
        sfo_ref[h] = s
        s = s0b_ref[h]
        prev_b = [None] * n_chunks
        for n in reversed(range(n_chunks)):
            prev_b[n] = s.astype(BF16)
            s = gc_b * s + kv_b[n]
        sbo_ref[h] = s

        gn = gn_ref[:, cols]
        for n in range(n_chunks):
            qn = q_ref[rows(n), cols]
            qf = qn.astype(F32)
            scores = lax.dot_general(qn, k_ref[rows(n), cols], (((1,), (1,)), ((), ())),
                                     preferred_element_type=F32)
            o = _dot((scores * decay).astype(BF16), v_ref[rows(n), cols])
            o = o + _dot((qf * qw_f).astype(BF16), prev_f[n])
            o = o + _dot((qf * qw_b).astype(BF16), prev_b[n])
            mu = jnp.mean(o, axis=-1, keepdims=True)
            d = o - mu
            var = jnp.mean(d * d, axis=-1, keepdims=True)
            on = d * lax.rsqrt(var + EPS) * gn
            r_ref[rows(n), cols] = (on * sg_ref[rows(n), cols].astype(F32)).astype(BF16)


def _retention(q, k, v, sg, dec, gn_g, s0f, s0b, batch, seq_len):
    hd = RET_HEAD_DIM
    tok_spec = pl.BlockSpec((seq_len, RET_WIDTH), lambda b: (b, 0))
    st_spec = pl.BlockSpec((None, N_RET_HEADS, hd, hd), lambda b: (b, 0, 0, 0))
    st_shape = jax.ShapeDtypeStruct((batch, N_RET_HEADS, hd, hd), F32)
    return pl.pallas_call(
        _retention_kernel,
        grid=(batch,),
        in_specs=[tok_spec, tok_spec, tok_spec, tok_spec,
                  pl.BlockSpec(dec.shape, lambda b: (0, 0, 0)),
                  pl.BlockSpec(gn_g.shape, lambda b: (0, 0)),
                  st_spec, st_spec],
        out_specs=[tok_spec, st_spec, st_spec],
        out_shape=[jax.ShapeDtypeStruct((batch * seq_len, RET_WIDTH), BF16), st_shape, st_shape],
        scratch_shapes=[pltpu.VMEM((N_RET_HEADS, 5, CHUNK, CHUNK), F32),
                        pltpu.VMEM((N_RET_HEADS, 2, hd), F32)],
        compiler_params=pltpu.CompilerParams(dimension_semantics=("arbitrary",),
                                             vmem_limit_bytes=VMEM_LIMIT),
        name="retention",
    )(q, k, v, sg, dec, gn_g, s0f, s0b)


def _fnet_merge_kernel(uf_ref, cs_ref, cls_ref, r_ref, gf_ref, gr_ref, x_ref, mod_ref, wf_ref, wr_ref, wo_ref,
                       o_ref, xcs_ref):
    seq_len = uf_ref.shape[0]
    gd = FOURIER_GROUP_DIM

    @pl.when(pl.program_id(1) == 0)
    def _():
        for g in range(N_FOURIER_GROUPS):
            x = _dot(uf_ref[:, g * gd:(g + 1) * gd], cs_ref[...])
            xcs_ref[0:seq_len, g * gd:(g + 1) * gd] = x[:, :gd].astype(BF16)
            xcs_ref[seq_len:2 * seq_len, g * gd:(g + 1) * gd] = x[:, gd:].astype(BF16)

    f_mix = _dot(cls_ref[...], xcs_ref[...]).astype(BF16)
    f_out = _dot(f_mix, wf_ref[...].astype(BF16))
    r_out = _dot(r_ref[...], wr_ref[...].astype(BF16))
    merged = gf_ref[...].astype(F32) * f_out + gr_ref[...].astype(F32) * r_out
    mix = _dot(merged.astype(BF16), wo_ref[...].astype(BF16))
    o_ref[...] = x_ref[...] + mod_ref[0, 2:3, :] * mix


def _fnet_merge(uf, cs, cls, r, gf, gr, x2d, mod3, w_four, w_ret, w_o, batch, seq_len, mod_row_of_batch):
    rb = min(FNET_ROWS, seq_len)
    nr = seq_len // rb

    def tok(w):
        return pl.BlockSpec((rb, w), lambda b, i: (b * nr + i, 0))

    def full(a):
        return pl.BlockSpec(a.shape, lambda b, i: (0, 0))

    def once(a):
        return pl.BlockSpec(a.shape, lambda b, i: (0, 0), pipeline_mode=pl.Buffered(1))

    return pl.pallas_call(
        _fnet_merge_kernel,
        grid=(batch, nr),
        in_specs=[pl.BlockSpec((seq_len, D_MODEL), lambda b, i: (b, 0)),
                  full(cs),
                  pl.BlockSpec((rb, 2 * seq_len), lambda b, i: (i, 0)),
                  tok(RET_WIDTH), tok(D_MODEL), tok(D_MODEL), tok(D_MODEL),
                  pl.BlockSpec((1, 6, D_MODEL), lambda b, i: (mod_row_of_batch(b), 0, 0)),
                  once(w_four), once(w_ret), once(w_o)],
        out_specs=tok(D_MODEL),
        out_shape=jax.ShapeDtypeStruct((batch * seq_len, D_MODEL), F32),
        scratch_shapes=[pltpu.VMEM((2 * seq_len, D_MODEL), BF16)],
        compiler_params=pltpu.CompilerParams(dimension_semantics=("parallel", "arbitrary"),
                                             vmem_limit_bytes=VMEM_LIMIT),
        name="fnet_merge",
    )(uf, cs, cls, r, gf, gr, x2d, mod3, w_four, w_ret, w_o)


def _pack_pair(lo_f32, hi_f32):
    lo = lax.bitcast_convert_type(lo_f32.astype(BF16).astype(F32), jnp.uint32)
    hi = lax.bitcast_convert_type(hi_f32.astype(BF16).astype(F32), jnp.uint32)
    return lax.bitcast_convert_type((lo >> 16) | hi, jnp.int32)


def _unpack_pair(words_i32):
    w = lax.bitcast_convert_type(words_i32, jnp.uint32)
    lo = lax.bitcast_convert_type(w << 16, F32)
    hi = lax.bitcast_convert_type(w & jnp.uint32(0xFFFF0000), F32)
    return lo, hi


def _load_token_words(ref, lead, n_tok):
    parts = []
    for s in range(ROW_SLABS):
        idx = (pl.ds(s, n_tok, stride=ROW_SLABS), slice(None))
        parts.append(ref[lead + idx] if lead else ref[idx])
    return jnp.concatenate(parts, axis=1)


def _store_token_words(ref, words, n_tok):
    for s in range(ROW_SLABS):
        ref[pl.ds(s, n_tok, stride=ROW_SLABS), :] = words[:, s * 128:(s + 1) * 128]


def _route(scores, biased):
    tokens = scores.shape[1]
    neg = -jnp.inf
    epg = EXPERTS_PER_GROUP
    iota_g = lax.broadcasted_iota(jnp.int32, (epg, tokens), 0).astype(F32)

    def pick_first_max(cur, iota, size):
        m = jnp.max(cur, axis=0, keepdims=True)
        idx = jnp.min(jnp.where(cur == m, iota, float(size)), axis=0, keepdims=True)
        return m, idx, iota == idx

    group_scores = []
    for g in range(N_EXPERT_GROUPS):
        vals = biased[g * epg:(g + 1) * epg, :]
        m1, _, hit = pick_first_max(vals, iota_g, epg)
        m2 = jnp.max(jnp.where(hit, neg, vals), axis=0, keepdims=True)
        group_scores.append(m1 + m2)
    cur = jnp.concatenate(group_scores, axis=0)
    group_sel = jnp.zeros_like(cur)
    for _ in range(TOPK_GROUPS):
        _, _, hit = pick_first_max(cur, iota_g, N_EXPERT_GROUPS)
        group_sel = jnp.where(hit, 1.0, group_sel)
        cur = jnp.where(hit, neg, cur)
    masked = jnp.concatenate(
        [jnp.where(group_sel[g:g + 1, :] > 0.0, biased[g * epg:(g + 1) * epg, :], neg)
         for g in range(N_EXPERT_GROUPS)], axis=0)
    iota_e = lax.broadcasted_iota(jnp.int32, masked.shape, 0).astype(F32)
    sel = jnp.zeros_like(masked)
    cur = masked
    picks = []
    for _ in range(TOP_K):
        _, idx, hit = pick_first_max(cur, iota_e, N_EXPERTS)
        picks.append(idx)
        sel = jnp.where(hit, 1.0, sel)
        cur = jnp.where(hit, neg, cur)
    w = scores * sel
    return w / jnp.sum(w, axis=0, keepdims=True) * ROUTED_SCALE, sel, picks


def _router_kernel(x_ref, mod_ref, g2_ref, wrt_ref, rb_ref, hp_ref, ek_ref, rk_ref, wt_ref, cnt_ref,
                   run_scr, earlier_scr):
    tm = x_ref.shape[0]

    @pl.when(pl.program_id(0) == 0)
    def _():
        run_scr[...] = jnp.zeros_like(run_scr)
        earlier = (lax.broadcasted_iota(jnp.int32, (tm, tm), 0) < lax.broadcasted_iota(jnp.int32, (tm, tm), 1))
        earlier_scr[...] = jnp.where(earlier, 1.0, 0.0).astype(BF16)

    h = _rms_mod(x_ref[...], g2_ref[...], mod_ref[0, 3:4, :], mod_ref[0, 4:5, :])
    half = D_MODEL // 2
    _store_token_words(hp_ref, _pack_pair(h[:, :half], h[:, half:]), tm)

    def split(a):
        hi = a.astype(BF16)
        return hi, (a - hi.astype(F32)).astype(BF16)

    def dot_nt(a, b):
        return lax.dot_general(a, b, (((1,), (1,)), ((), ())), preferred_element_type=F32)

    h_hi, h_lo = split(h)
    w_hi, w_lo = split(wrt_ref[...])
    logits_t = dot_nt(w_hi, h_hi) + (dot_nt(w_hi, h_lo) + dot_nt(w_lo, h_hi))
    scores = jax.nn.sigmoid(logits_t)
    comb_t, sel, picks = _route(scores, scores + rb_ref[...])

    rank_t = _dot(sel.astype(BF16), earlier_scr[...]) + run_scr[...]
    run_scr[...] += jnp.sum(sel, axis=1, keepdims=True)
    cnt_ref[...] = jnp.broadcast_to(run_scr[...], cnt_ref.shape)

    iota_e = lax.broadcasted_iota(jnp.int32, sel.shape, 0).astype(F32)
    ranks, weights = [], []
    for idx in picks:
        hit = iota_e == idx
        ranks.append(jnp.sum(jnp.where(hit, rank_t, 0.0), axis=0, keepdims=True))
        weights.append(jnp.sum(jnp.where(hit, comb_t, 0.0), axis=0, keepdims=True))
    ek_ref[...] = jnp.concatenate(picks, axis=0).astype(jnp.int32)
    rk_ref[...] = jnp.concatenate(ranks, axis=0).astype(jnp.int32)
    w_rep = jnp.concatenate([jnp.broadcast_to(w, (SC_LANES, tm)) for w in weights], axis=0)
    wt_ref[...] = w_rep.T


def _router(x1, mod3, norm2_g, w_router_t, router_bias, seq_len, mod_row_of_batch):
    t = x1.shape[0]
    tm = TM_ROUTER

    def mod_idx(i):
        return (mod_row_of_batch((i * tm) // seq_len), 0, 0)

    def full(a):
        return pl.BlockSpec(a.shape, lambda i: (0,) * a.ndim)

    return pl.pallas_call(
        _router_kernel,
        grid=(t // tm,),
        in_specs=[pl.BlockSpec((tm, D_MODEL), lambda i: (i, 0)),
                  pl.BlockSpec((1, 6, D_MODEL), mod_idx),
                  full(norm2_g), full(w_router_t), full(router_bias)],
        out_specs=[pl.BlockSpec((tm * ROW_SLABS, 128), lambda i: (i, 0)),
                   pl.BlockSpec((TOP_K, tm), lambda i: (0, i)),
                   pl.BlockSpec((TOP_K, tm), lambda i: (0, i)),
                   pl.BlockSpec((tm, 128), lambda i: (i, 0)),
                   pl.BlockSpec((N_EXPERTS, 128), lambda i: (0, 0))],
        out_shape=[jax.ShapeDtypeStruct((t * ROW_SLABS, 128), jnp.int32),
                   jax.ShapeDtypeStruct((TOP_K, t), jnp.int32),
                   jax.ShapeDtypeStruct((TOP_K, t), jnp.int32),
                   jax.ShapeDtypeStruct((t, 128), F32),
                   jax.ShapeDtypeStruct((N_EXPERTS, 128), F32)],
        scratch_shapes=[pltpu.VMEM((N_EXPERTS, 1), F32), pltpu.VMEM((tm, tm), BF16)],
        compiler_params=pltpu.CompilerParams(dimension_semantics=("arbitrary",),
                                             vmem_limit_bytes=VMEM_LIMIT),
        name="router",
    )(x1, mod3, norm2_g, w_router_t, router_bias)


def _plan_kernel(ek_ref, rk_ref, cnt_ref, pos_ref, texp_ref, nused_ref, tend_ref, *, expert_rows):
    rows = float(expert_rows)
    cnt = cnt_ref[:, 0:1]
    tiles = jnp.floor((cnt + (rows - 1.0)) / rows)
    before = (lax.broadcasted_iota(jnp.int32, (N_EXPERTS, N_EXPERTS), 1)
              < lax.broadcasted_iota(jnp.int32, (N_EXPERTS, N_EXPERTS), 0))
    tile_start = jnp.dot(jnp.where(before, 1.0, 0.0), jnp.broadcast_to(tiles, (N_EXPERTS, 128)),
                         precision=lax.Precision.HIGHEST, preferred_element_type=F32)[:, 0:1]
    tile_end = tile_start + tiles
    row_start = tile_start * rows

    ek = ek_ref[...]
    pos = rk_ref[...].astype(F32)
    tile_id = lax.broadcasted_iota(jnp.int32, texp_ref.shape, 1).astype(F32)
    texp = jnp.zeros(texp_ref.shape, F32)
    for e in range(N_EXPERTS):
        pos = pos + jnp.where(ek == e, row_start[e:e + 1, :], 0.0)
        texp = texp + jnp.where(tile_id >= tile_end[e:e + 1, :], 1.0, 0.0)
    pos_ref[...] = pos.astype(jnp.int32)
    texp_ref[...] = jnp.minimum(texp, N_EXPERTS - 1.0).astype(jnp.int32)
    nused_ref[...] = jnp.broadcast_to(tile_end[N_EXPERTS - 1:N_EXPERTS, :], nused_ref.shape).astype(jnp.int32)
    tend_ref[...] = jnp.broadcast_to(tile_end, tend_ref.shape).astype(jnp.int32)


def _plan(ek, rk, cnt, n_tiles_pad, expert_rows):
    t = ek.shape[1]

    def full(shape):
        return pl.BlockSpec(shape, lambda: (0,) * len(shape))

    return pl.pallas_call(
        functools.partial(_plan_kernel, expert_rows=expert_rows),
        in_specs=[full(ek.shape), full(rk.shape), full(cnt.shape)],
        out_specs=[full((TOP_K, t)), full((1, n_tiles_pad)), full((1, 128)), full((N_EXPERTS, 128))],
        out_shape=[jax.ShapeDtypeStruct((TOP_K, t), jnp.int32),
                   jax.ShapeDtypeStruct((1, n_tiles_pad), jnp.int32),
                   jax.ShapeDtypeStruct((1, 128), jnp.int32),
                   jax.ShapeDtypeStruct((N_EXPERTS, 128), jnp.int32)],
        compiler_params=pltpu.CompilerParams(vmem_limit_bytes=VMEM_LIMIT),
        name="plan",
    )(ek, rk, cnt)


def _sc_mesh():
    return plsc.VectorSubcoreMesh(core_axis_name="c", subcore_axis_name="s")


def _sc_pack_weight_halves(w):
    e, k, n = w.shape
    k_half = k // 2
    rb = SC_PACK_BLOCK_WORDS // n
    units_per_expert = k_half // rb
    per_w = (e * units_per_expert) // SC_WORKERS
    lanes = SC_LANES

    @functools.partial(
        pl.kernel, out_type=jax.ShapeDtypeStruct((e * k_half, n), jnp.int32), mesh=_sc_mesh(),
        scratch_types=[pltpu.VMEM((rb, n), F32), pltpu.VMEM((rb, n), F32), pltpu.VMEM((rb, n), jnp.int32)],
        compiler_params=pltpu.CompilerParams(needs_layout_passes=False))
    def kern(w_hbm, out_hbm, a_v, b_v, o_v):
        wid = lax.axis_index("s") * SC_CORES + lax.axis_index("c")

        @pl.loop(0, per_w)
        def _(j):
            unit = wid * per_w + j
            expert = unit // units_per_expert
            blk = unit % units_per_expert
            row_a = expert * k + blk * rb
            pltpu.sync_copy(w_hbm.at[pl.ds(row_a, rb)], a_v)
            pltpu.sync_copy(w_hbm.at[pl.ds(row_a + k_half, rb)], b_v)

            @pl.loop(0, rb)
            def _(r):
                @plsc.parallel_loop(0, n, step=lanes, unroll=4)
                def _(c):
                    both = plsc.pack(a_v[r, pl.ds(c, lanes)], b_v[r, pl.ds(c, lanes)],
                                     format=plsc.PackFormat.INTERLEAVED)
                    o_v[r, pl.ds(c, lanes)] = plsc.bitcast(both, jnp.int32)

            pltpu.sync_copy(o_v, out_hbm.at[pl.ds(expert * k_half + blk * rb, rb)])

    return kern(w.reshape(e * k, n)).reshape(e, k_half, n)


def _sc_dispatch(rows, pos3, n_out, after=()):
    t = rows.shape[0]
    ch = SC_CHUNK
    per_w = (t // ch) // SC_WORKERS

    @functools.partial(
        pl.kernel, out_type=jax.ShapeDtypeStruct((n_out,) + rows.shape[1:], jnp.int32), mesh=_sc_mesh(),
        scratch_types=[pltpu.VMEM((TOP_K, ch), jnp.int32), pltpu.VMEM((ch,) + rows.shape[1:], jnp.int32),
                       pltpu.SemaphoreType.DMA])
    def k(rows_hbm, pos_hbm, *rest):
        out_hbm, idx_v, rows_v, sem = rest[len(after):]
        wid = lax.axis_index("s") * SC_CORES + lax.axis_index("c")

        @pl.loop(0, per_w)
        def _(j):
            c = wid * per_w + j
            pltpu.sync_copy(pos_hbm.at[c], idx_v)
            pltpu.sync_copy(rows_hbm.at[pl.ds(c * ch, ch)], rows_v)
            copies = [pltpu.async_copy(rows_v, out_hbm.at[idx_v.at[kk]], sem) for kk in range(TOP_K)]
            for cp in copies:
                cp.wait()

    return k(rows, pos3, *after)


def _sc_combine(table, pos3, wtok, t):
    ch = SC_CHUNK
    sub = SC_COMBINE_TOKENS
    lanes = SC_LANES
    slabs = ROW_SLABS
    per_w = (t // ch) // SC_WORKERS
    subs_per_chunk = ch // sub
    n_steps = per_w * subs_per_chunk

    @functools.partial(
        pl.kernel, out_type=jax.ShapeDtypeStruct((t, slabs, 128), jnp.int32), mesh=_sc_mesh(),
        scratch_types=[pltpu.VMEM((per_w, TOP_K, ch), jnp.int32),
                       pltpu.VMEM((2, TOP_K, sub, slabs, 128), jnp.int32),
                       pltpu.VMEM((2, sub, 128), F32),
                       pltpu.VMEM((2, sub, slabs, 128), jnp.int32),
                       pltpu.SemaphoreType.DMA((2,)),
                       pltpu.SemaphoreType.DMA((2,))],
        compiler_params=pltpu.CompilerParams(needs_layout_passes=False))
    def k(tab_hbm, pos_hbm, w_hbm, out_hbm, idx_v, rows_v, w_v, out_v, sem, out_sem):
        wid = lax.axis_index("s") * SC_CORES + lax.axis_index("c")
        for j in range(per_w):
            pltpu.sync_copy(pos_hbm.at[wid * per_w + j], idx_v.at[j])

        def first_token(step):
            return (wid * per_w + step // subs_per_chunk) * ch + (step % subs_per_chunk) * sub

        def copies(step, slot):
            j = step // subs_per_chunk
            s = step % subs_per_chunk
            idx = [idx_v.at[j, kk, pl.ds(s * sub, sub)] for kk in range(TOP_K)]
            return ([pltpu.make_async_copy(tab_hbm.at[idx[kk]], rows_v.at[slot, kk], sem.at[slot])
                     for kk in range(TOP_K)]
                    + [pltpu.make_async_copy(w_hbm.at[pl.ds(first_token(step), sub)], w_v.at[slot], sem.at[slot])])

        def store(step, slot):
            return pltpu.make_async_copy(out_v.at[slot], out_hbm.at[pl.ds(first_token(step), sub)], out_sem.at[slot])

        for cp in copies(0, 0):
            cp.start()

        @pl.loop(0, n_steps)
        def _(step):
            slot = step % 2

            @pl.when(step + 1 < n_steps)
            def _():
                for cp in copies(step + 1, 1 - slot):
                    cp.start()

            for cp in copies(step, slot):
                cp.wait()

            @pl.when(step >= 2)
            def _():
                store(step - 2, slot).wait()

            @pl.loop(0, sub)
            def _(tt):
                wk = [w_v[slot, tt, pl.ds(kk * lanes, lanes)] for kk in range(TOP_K)]
                for sl in range(slabs):
                    @plsc.parallel_loop(0, 128, step=lanes, unroll=8)
                    def _(off):
                        acc_lo = jnp.zeros((lanes,), F32)
                        acc_hi = jnp.zeros((lanes,), F32)
                        for kk in range(TOP_K):
                            word = rows_v[slot, kk, tt, sl, pl.ds(off, lanes)]
                            lo = plsc.bitcast(word << 16, F32)
                            hi = plsc.bitcast(word & jnp.int32(-65536), F32)
                            acc_lo = acc_lo + wk[kk] * lo
                            acc_hi = acc_hi + wk[kk] * hi
                        both = plsc.pack(acc_lo, acc_hi, format=plsc.PackFormat.INTERLEAVED)
                        out_v[slot, tt, sl, pl.ds(off, lanes)] = plsc.bitcast(both, jnp.int32)

            store(step, slot).start()

        for step in (n_steps - 2, n_steps - 1):
            store(step, step % 2).wait()

    return k(table, pos3, wtok)


def _experts_kernel(texp_ref, nused_ref, tend_ref, xs_ref, weg_hbm, weu_hbm, wed_hbm, ys_ref,
                    wg_scr, wu_scr, wd_scr, wg_buf, wu_buf, wd_buf, sem, group_scr, *, expert_rows):
    step = pl.program_id(0)
    rows = expert_rows
    tiles_per_step = EXPERT_TILES_PER_STEP
    half = D_MODEL // 2
    n_used = nused_ref[0]

    def weight_copies(e, slot):
        return [pltpu.make_async_copy(weg_hbm.at[e], wg_buf.at[slot], sem.at[slot, 0]),
                pltpu.make_async_copy(weu_hbm.at[e], wu_buf.at[slot], sem.at[slot, 1]),
                pltpu.make_async_copy(wed_hbm.at[e], wd_buf.at[slot], sem.at[slot, 2])]

    def next_group(e):
        tile = tend_ref[e]
        return texp_ref[jnp.minimum(tile, n_used - 1)], tile < n_used

    def start_weights(e, slot, exists):
        @pl.when(exists)
        def _():
            for cp in weight_copies(e, slot):
                cp.start()

    @pl.when(step == 0)
    def _():
        group_scr[0] = 0
        e, exists = texp_ref[0], True
        for slot in range(WEIGHT_SLOTS - 1):
            start_weights(e, slot, exists)
            nxt, has_next = next_group(e)
            e, exists = nxt, exists & has_next

    def row_tile(tile, x_view, y_view):
        expert = texp_ref[tile]
        used = tile < n_used
        new_expert = (tile == 0) | (expert != texp_ref[jnp.maximum(tile - 1, 0)])

        @pl.when(used & new_expert)
        def _():
            group = group_scr[0]
            slot = group % WEIGHT_SLOTS
            ahead, exists = expert, True
            for _ in range(WEIGHT_SLOTS - 1):
                nxt, has_next = next_group(ahead)
                ahead, exists = nxt, exists & has_next
            start_weights(ahead, (group + WEIGHT_SLOTS - 1) % WEIGHT_SLOTS, exists)

            for cp in weight_copies(expert, slot):
                cp.wait()
            for scr, buf in ((wg_scr, wg_buf), (wu_scr, wu_buf), (wd_scr, wd_buf)):
                top, bottom = _unpack_pair(buf[slot])
                k_half = top.shape[0]
                scr[0:k_half, :] = top.astype(BF16)
                scr[k_half:2 * k_half, :] = bottom.astype(BF16)
            group_scr[0] = group + 1

        @pl.when(used)
        def _():
            lo, hi = _unpack_pair(_load_token_words(x_view, (), rows))
            lo = lo.astype(BF16)
            hi = hi.astype(BF16)
            g = _dot(lo, wg_scr[0:half, :]) + _dot(hi, wg_scr[half:D_MODEL, :])
            u = _dot(lo, wu_scr[0:half, :]) + _dot(hi, wu_scr[half:D_MODEL, :])
            y = _dot((_silu(g) * u).astype(BF16), wd_scr[...])
            _store_token_words(y_view, _pack_pair(y[:, :half], y[:, half:]), rows)

        @pl.when(jnp.logical_not(used) & (step == (n_used - 1) // tiles_per_step))
        def _():
            y_view[...] = jnp.zeros_like(y_view)

    for s in range(tiles_per_step):
        view = pl.ds(s * rows * ROW_SLABS, rows * ROW_SLABS)
        row_tile(step * tiles_per_step + s, xs_ref.at[view], ys_ref.at[view])


def _experts(texp, nused, tend, xs2d, weg, weu, wed, n_tiles, expert_rows):
    tiles_per_step = EXPERT_TILES_PER_STEP
    block = (tiles_per_step * expert_rows * ROW_SLABS, 128)
    hbm = pl.BlockSpec(memory_space=pl.ANY)

    def block_idx(j, te, nu, tn):
        return (jnp.minimum(j, (nu[0] - 1) // tiles_per_step), 0)

    grid_spec = pltpu.PrefetchScalarGridSpec(
        num_scalar_prefetch=3,
        grid=(n_tiles // tiles_per_step,),
        in_specs=[pl.BlockSpec(block, block_idx), hbm, hbm, hbm],
        out_specs=pl.BlockSpec(block, block_idx),
        scratch_shapes=[pltpu.VMEM((D_MODEL, EXPERT_DIM), BF16),
                        pltpu.VMEM((D_MODEL, EXPERT_DIM), BF16),
                        pltpu.VMEM((EXPERT_DIM, D_MODEL), BF16),
                        pltpu.VMEM((WEIGHT_SLOTS,) + weg.shape[1:], jnp.int32),
                        pltpu.VMEM((WEIGHT_SLOTS,) + weu.shape[1:], jnp.int32),
                        pltpu.VMEM((WEIGHT_SLOTS,) + wed.shape[1:], jnp.int32),
                        pltpu.SemaphoreType.DMA((WEIGHT_SLOTS, 3)),
                        pltpu.SMEM((1,), jnp.int32)],
    )
    return pl.pallas_call(
        functools.partial(_experts_kernel, expert_rows=expert_rows),
        grid_spec=grid_spec,
        out_shape=jax.ShapeDtypeStruct(xs2d.shape, jnp.int32),
        compiler_params=pltpu.CompilerParams(dimension_semantics=("arbitrary",),
                                             vmem_limit_bytes=VMEM_LIMIT),
        name="experts",
    )(texp, nused, tend, xs2d, weg, weu, wed)


def _final_kernel(x_ref, routed_ref, mod_ref, g2_ref, wsg_ref, wsu_ref, wsd_ref, fng_ref, o_ref):
    tm = x_ref.shape[0]
    x = x_ref[...]
    hb = _rms_mod(x, g2_ref[...], mod_ref[0, 3:4, :], mod_ref[0, 4:5, :]).astype(BF16)
    shared = _dot((_silu(_dot(hb, wsg_ref[...])) * _dot(hb, wsu_ref[...])).astype(BF16), wsd_ref[...])
    routed = jnp.concatenate(_unpack_pair(_load_token_words(routed_ref, (), tm)), axis=1)
    y = x + mod_ref[0, 5:6, :] * (routed + shared)
    ms = jnp.mean(y * y, axis=-1, keepdims=True)
    o_ref[...] = y * lax.rsqrt(ms + EPS) * fng_ref[...]


def _final(x1, routed2d, mod3, norm2_g, wsg, wsu, wsd, final_g, seq_len, mod_row_of_batch):
    t = x1.shape[0]
    tm = TM_FINAL

    def mod_idx(i):
        return (mod_row_of_batch((i * tm) // seq_len), 0, 0)

    def full(a):
        return pl.BlockSpec(a.shape, lambda i: (0,) * a.ndim)

    return pl.pallas_call(
        _final_kernel,
        grid=(t // tm,),
        in_specs=[pl.BlockSpec((tm, D_MODEL), lambda i: (i, 0)),
                  pl.BlockSpec((tm * ROW_SLABS, 128), lambda i: (i, 0)),
                  pl.BlockSpec((1, 6, D_MODEL), mod_idx),
                  full(norm2_g), full(wsg), full(wsu), full(wsd), full(final_g)],
        out_specs=pl.BlockSpec((tm, D_MODEL), lambda i: (i, 0)),
        out_shape=jax.ShapeDtypeStruct((t, D_MODEL), F32),
        compiler_params=pltpu.CompilerParams(dimension_semantics=("parallel",),
                                             vmem_limit_bytes=VMEM_LIMIT),
        name="final",
    )(x1, routed2d, mod3, norm2_g, wsg, wsu, wsd, final_g)


def _moe(x1, mod3, lw, seq_len, mod_row_of_batch):
    t = x1.shape[0]
    expert_rows = min(MAX_EXPERT_ROWS, TOP_K * t // N_EXPERTS // 2)
    n_tiles = TOP_K * t // expert_rows + N_EXPERTS
    n_tiles_pad = -(-n_tiles // 128) * 128
    hp2d, ek, rk, wtok, cnt = _router(x1, mod3, lw["norm2_g"], lw["w_router_t"], lw["router_bias"],
                                      seq_len, mod_row_of_batch)
    pos, texp, nused, tend = _plan(ek, rk, cnt, n_tiles_pad, expert_rows)
    pos3 = pos.reshape(TOP_K, t // SC_CHUNK, SC_CHUNK).transpose(1, 0, 2)
    xs = _sc_dispatch(hp2d.reshape(t, ROW_SLABS, 128), pos3, n_tiles * expert_rows,
                      after=(lw["weg"], lw["weu"], lw["wed"]))
    ys2d = _experts(texp.reshape(-1), nused.reshape(-1), tend[:, 0], xs.reshape(-1, 128),
                    lw["weg"], lw["weu"], lw["wed"], n_tiles, expert_rows)
    routed = _sc_combine(ys2d.reshape(-1, ROW_SLABS, 128), pos3, wtok, t)
    return _final(x1, routed.reshape(t * ROW_SLABS, 128), mod3, lw["norm2_g"],
                  lw["wsg"], lw["wsu"], lw["wsd"], lw["final_g"], seq_len, mod_row_of_batch)


def _dft_tables(seq_len):
    gd = FOURIER_GROUP_DIM
    kc = np.arange(gd)
    ang_c = ((kc[:, None] * kc[None, :]) % gd) * (2.0 * math.pi / gd)
    cs = np.concatenate([np.cos(ang_c), np.sin(ang_c)], axis=1) * (gd ** -0.5)
    kl = np.arange(seq_len)
    ang_l = ((kl[:, None] * kl[None, :]) % seq_len) * (2.0 * math.pi / seq_len)
    cls = np.concatenate([np.cos(ang_l), -np.sin(ang_l)], axis=1) * (seq_len ** -0.5)
    return jnp.asarray(cs.astype(np.float32), dtype=BF16), jnp.asarray(cls.astype(np.float32), dtype=BF16)


def _rope_tables(length):
    rows = length // GRID_W
    r = np.repeat(np.arange(rows, dtype=np.float32), GRID_W)
    col = np.tile(np.arange(GRID_W, dtype=np.float32), rows)
    nf = RET_HEAD_DIM // 4
    inv = (np.float32(ROPE_BASE) ** (-np.arange(nf, dtype=np.float32) / np.float32(nf))).astype(np.float32)
    ar = r[:, None] * inv[None]
    ac = col[:, None] * inv[None]
    ang = np.concatenate([ar, ar, ac, ac], axis=-1).astype(np.float64)
    sign = np.where((np.arange(RET_HEAD_DIM) & nf) == 0, -1.0, 1.0)
    return (jnp.asarray(np.cos(ang).astype(np.float32)),
            jnp.asarray((np.sin(ang) * sign[None, :]).astype(np.float32)))


def _trunk_path(x, mod3, mod_row_of_batch, s0f, s0b, rope, lw):
    batch, seq_len, _ = x.shape
    x2d = x.reshape(batch * seq_len, D_MODEL)
    uf, q, k, v, sg, gf, gr = _inproj(x2d, mod3, lw["norm1_g"], lw["w_in"], seq_len, mod_row_of_batch, rope)
    r, s_f, s_b = _retention(q, k, v, sg, lw["dec"], lw["gn_g"], s0f, s0b, batch, seq_len)
    cs, cls = _dft_tables(seq_len)
    x1 = _fnet_merge(uf, cs, cls, r, gf, gr, x2d, mod3, lw["w_four"], lw["w_ret"], lw["w_o"],
                     batch, seq_len, mod_row_of_batch)
    y = _moe(x1, mod3, lw, seq_len, mod_row_of_batch)
    return y.reshape(batch, seq_len, D_MODEL), s_f, s_b


def kernel(x_prompt, x_sample, state_ret_fwd, state_ret_bwd, c, c_ctx, w_ada, b_ada, norm1_g, norm2_g, w_in,
           ret_decay_fwd, ret_decay_bwd, ret_gn_g, w_four_out, w_ret_out, w_out, w_router, router_bias,
           w_exp_gate, w_exp_up, w_exp_down, w_shared_gate, w_shared_up, w_shared_down, final_norm_g):
    depth = w_ada.shape[0]
    assert depth == 1, "final norm is fused into the last layer's MoE kernel"
    n_ctx, n_lat = x_prompt.shape[0], x_sample.shape[0]
    cond = jnp.concatenate([c_ctx[None, :], c], axis=0)
    cond = jnp.pad(cond, ((0, (-cond.shape[0]) % 8), (0, 0)))
    rope = _rope_tables(x_sample.shape[1])
    zeros = jnp.zeros((n_ctx, N_RET_HEADS, RET_HEAD_DIM, RET_HEAD_DIM), F32)

    layer = 0
    mod = _ada(cond, w_ada[layer], b_ada[layer][None, :])
    mod3 = mod.reshape(mod.shape[0], 6, D_MODEL)
    dec = jnp.stack([ret_decay_fwd[layer], ret_decay_bwd[layer]], axis=1)
    lw = {
        "norm1_g": norm1_g[layer][None, :],
        "norm2_g": norm2_g[layer][None, :],
        "w_in": w_in[layer],
        "dec": jnp.broadcast_to(dec[:, :, None], (N_RET_HEADS, 2, RET_HEAD_DIM)).astype(F32),
        "gn_g": ret_gn_g[layer][None, :],
        "w_four": w_four_out[layer],
        "w_ret": w_ret_out[layer],
        "w_o": w_out[layer],
        "w_router_t": w_router[layer].T,
        "router_bias": router_bias[layer][:, None],
        "weg": _sc_pack_weight_halves(w_exp_gate[layer]),
        "weu": _sc_pack_weight_halves(w_exp_up[layer]),
        "wed": _sc_pack_weight_halves(w_exp_down[layer]),
        "wsg": w_shared_gate[layer].astype(BF16),
        "wsu": w_shared_up[layer].astype(BF16),
        "wsd": w_shared_down[layer].astype(BF16),
        "final_g": final_norm_g[None, :],
    }
    y_prompt, s_f, s_b = _trunk_path(x_prompt, mod3, lambda b: 0, zeros, zeros, None, lw)
    y_sample, _, _ = _trunk_path(x_sample, mod3, lambda b: 1 + b, state_ret_fwd[:, layer],
                                 state_ret_bwd[:, layer], rope, lw)
    return (y_prompt, y_sample, s_f[:, None], s_b[:, None])
```

```python
import functools
import math

import jax
import jax.numpy as jnp
import numpy as np
from jax import lax
from jax.experimental import pallas as pl
from jax.experimental.pallas import tpu as pltpu
from jax.experimental.pallas import tpu_sc as plsc

F32 = jnp.float32
BF16 = jnp.bfloat16

D_MODEL = 1024
GRID_W = 64
N_FOURIER_GROUPS = 8
FOURIER_GROUP_DIM = 128
N_RET_HEADS = 4
RET_HEAD_DIM = 128
RET_WIDTH = N_RET_HEADS * RET_HEAD_DIM
CHUNK = 128
N_EXPERTS = 64
N_EXPERT_GROUPS = 8
EXPERTS_PER_GROUP = N_EXPERTS // N_EXPERT_GROUPS
TOPK_GROUPS = 4
TOP_K = 8
EXPERT_DIM = 256
ROUTED_SCALE = 2.5
ROPE_BASE = 10000.0
EPS = 1e-6
Q_SCALE = RET_HEAD_DIM ** -0.5

_C_UF = (0, 1024)
_C_Q = (1024, 1536)
_C_K = (1536, 2048)
_C_V = (2048, 2560)
_C_G = (2560, 3072)
_C_GF = (3072, 4096)
_C_GR = (4096, 5120)

VMEM_LIMIT = 56 * 1024 * 1024

TM_INPROJ = 1024
TM_ROUTER = 512
FNET_ROWS = 512
TM_FINAL = 1024
EXPERT_TILES_PER_STEP = 4
MAX_EXPERT_ROWS = 512
ROW_SLOTS = 3
WEIGHT_SLOTS = 3
ROW_SLABS = 4
SC_CORES = 2
SC_WORKERS = 32
SC_CHUNK = 128
SC_LANES = 16
SC_PACK_BLOCK_WORDS = 16384
SC_COMBINE_TOKENS = 8


def _silu(x):
    return x * jax.nn.sigmoid(x)


def _dot(a, b):
    return jnp.dot(a, b, preferred_element_type=F32)


def _rms_mod(x, g, shift, scale):
    ms = jnp.mean(x * x, axis=-1, keepdims=True)
    y = x * lax.rsqrt(ms + EPS) * g
    return y * (1.0 + scale) + shift


def _ada_kernel(cond_ref, w_ref, b_ref, o_ref):
    s = _silu(cond_ref[...]).astype(BF16)
    o_ref[...] = _dot(s, w_ref[...].astype(BF16)) + b_ref[...]


def _ada(cond, w_ada, b_ada):
    rows, n = cond.shape[0], w_ada.shape[1]
    tn = 1536
    return pl.pallas_call(
        _ada_kernel,
        grid=(n // tn,),
        in_specs=[pl.BlockSpec((rows, D_MODEL), lambda j: (0, 0)),
                  pl.BlockSpec((D_MODEL, tn), lambda j: (0, j)),
                  pl.BlockSpec((1, tn), lambda j: (0, j))],
        out_specs=pl.BlockSpec((rows, tn), lambda j: (0, j)),
        out_shape=jax.ShapeDtypeStruct((rows, n), F32),
        compiler_params=pltpu.CompilerParams(vmem_limit_bytes=VMEM_LIMIT),
        name="ada",
    )(cond, w_ada, b_ada)


def _rope_head(x, cos, sin_signed, first_half):
    partner = jnp.where(first_half, pltpu.roll(x, 96, 1), pltpu.roll(x, 32, 1))
    return x * cos + partner * sin_signed


def _inproj_kernel(*refs, use_rope):
    if use_rope:
        x_ref, mod_ref, g_ref, w_ref, cos_ref, sin_ref = refs[:6]
        outs = refs[6:]
    else:
        x_ref, mod_ref, g_ref, w_ref = refs[:4]
        outs = refs[4:]
    uf_o, q_o, k_o, v_o, sg_o, gf_o, gr_o = outs

    h = _rms_mod(x_ref[...], g_ref[...], mod_ref[0, 0:1, :], mod_ref[0, 1:2, :])
    hb = h.astype(BF16)

    def proj(cols):
        return _dot(hb, w_ref[:, cols[0]:cols[1]].astype(BF16))

    uf_o[...] = proj(_C_UF).astype(BF16)
    q = proj(_C_Q)
    k = proj(_C_K)
    if use_rope:
        cos = cos_ref[...]
        sin_signed = sin_ref[...]
        lane = lax.broadcasted_iota(jnp.int32, cos.shape, 1)
        first_half = (lane & 32) == 0
        for hd in range(N_RET_HEADS):
            sl = slice(hd * RET_HEAD_DIM, (hd + 1) * RET_HEAD_DIM)
            q_o[:, sl] = (_rope_head(q[:, sl], cos, sin_signed, first_half) * Q_SCALE).astype(BF16)
            k_o[:, sl] = _rope_head(k[:, sl], cos, sin_signed, first_half).astype(BF16)
    else:
        q_o[...] = (q * Q_SCALE).astype(BF16)
        k_o[...] = k.astype(BF16)
    v_o[...] = proj(_C_V).astype(BF16)
    sg_o[...] = _silu(proj(_C_G)).astype(BF16)
    gf_o[...] = jax.nn.sigmoid(proj(_C_GF)).astype(BF16)
    gr_o[...] = jax.nn.sigmoid(proj(_C_GR)).astype(BF16)


def _inproj(x2d, mod3, norm_g, w_in_f32, seq_len, mod_row_of_batch, rope):
    t = x2d.shape[0]
    tm = TM_INPROJ
    tiles_per_seq = max(seq_len // tm, 1)

    def mod_idx(i):
        return (mod_row_of_batch((i * tm) // seq_len), 0, 0)

    in_specs = [pl.BlockSpec((tm, D_MODEL), lambda i: (i, 0)),
                pl.BlockSpec((1, 6, D_MODEL), mod_idx),
                pl.BlockSpec((1, D_MODEL), lambda i: (0, 0)),
                pl.BlockSpec(w_in_f32.shape, lambda i: (0, 0), pipeline_mode=pl.Buffered(1))]
    args = [x2d, mod3, norm_g, w_in_f32]
    if rope is not None:
        in_specs += [pl.BlockSpec((tm, RET_HEAD_DIM), lambda i: (i % tiles_per_seq, 0))] * 2
        args += list(rope)
    widths = [1024, RET_WIDTH, RET_WIDTH, RET_WIDTH, RET_WIDTH, 1024, 1024]
    return pl.pallas_call(
        functools.partial(_inproj_kernel, use_rope=rope is not None),
        grid=(t // tm,),
        in_specs=in_specs,
        out_specs=[pl.BlockSpec((tm, w), lambda i: (i, 0)) for w in widths],
        out_shape=[jax.ShapeDtypeStruct((t, w), BF16) for w in widths],
        compiler_params=pltpu.CompilerParams(dimension_semantics=("parallel",),
                                             vmem_limit_bytes=VMEM_LIMIT),
        name="inproj",
    )(*args)


def _retention_kernel(q_ref, k_ref, v_ref, sg_ref, dec_ref, gn_ref, s0f_ref, s0b_ref,
                      r_ref, sfo_ref, sbo_ref, tab_scr, gc_scr):
    n_chunks = q_ref.shape[0] // CHUNK
    hd = RET_HEAD_DIM

    @pl.when(pl.program_id(0) == 0)
    def _():
        row = lax.broadcasted_iota(jnp.int32, (CHUNK, CHUNK), 0).astype(F32)
        col = lax.broadcasted_iota(jnp.int32, (CHUNK, CHUNK), 1).astype(F32)
        diff = row - col
        for h in range(N_RET_HEADS):
            dec = dec_ref[h]
            lg = jnp.minimum(dec, 0.0) - jnp.log1p(jnp.exp(-jnp.abs(dec)))
            lgf = lg[0:1, :]
            lgb = lg[1:2, :]
            tab_scr[h, 0] = jnp.exp(jnp.where(diff >= 0, lgf * diff, lgb * (-diff)))
            tab_scr[h, 1] = jnp.exp(lgf * (row + 1.0))
            tab_scr[h, 2] = jnp.exp(lgb * (CHUNK - row))
            tab_scr[h, 3] = jnp.exp(lgf * (CHUNK - 1.0 - col))
            tab_scr[h, 4] = jnp.exp(lgb * col)
            gc_scr[h] = jnp.exp(lg * CHUNK)

    def rows(n):
        return slice(n * CHUNK, (n + 1) * CHUNK)

    for h in range(N_RET_HEADS):
        cols = slice(h * hd, (h + 1) * hd)
        decay, qw_f, qw_b, kwt_f, kwt_b = (tab_scr[h, i] for i in range(5))
        gc = gc_scr[h]
        gc_f = gc[0:1, :]
        gc_b = gc[1:2, :]

        kv_f, kv_b = [], []
        for n in range(n_chunks):
            kt = k_ref[rows(n), cols].astype(F32).T
            vn = v_ref[rows(n), cols]
            kv_f.append(_dot((kt * kwt_f).astype(BF16), vn))
            kv_b.append(_dot((kt * kwt_b).astype(BF16), vn))

        s = s0f_ref[h]
        prev_f = []
        for n in range(n_chunks):
            prev_f.append(s.astype(BF16))
            s = gc_f * s + kv_f[n]
        sfo_ref[h] = s
        s = s0b_ref[h]
        prev_b = [None] * n_chunks
        for n in reversed(range(n_chunks)):
            prev_b[n] = s.astype(BF16)
            s = gc_b * s + kv_b[n]
        sbo_ref[h] = s

        gn = gn_ref[:, cols]
        for n in range(n_chunks):
            qn = q_ref[rows(n), cols]
            qf = qn.astype(F32)
            scores = lax.dot_general(qn, k_ref[rows(n), cols], (((1,), (1,)), ((), ())),
                                     preferred_element_type=F32)
            o = _dot((scores * decay).astype(BF16), v_ref[rows(n), cols])
            o = o + _dot((qf * qw_f).astype(BF16), prev_f[n])
            o = o + _dot((qf * qw_b).astype(BF16), prev_b[n])
            mu = jnp.mean(o, axis=-1, keepdims=True)
            d = o - mu
            var = jnp.mean(d * d, axis=-1, keepdims=True)
            on = d * lax.rsqrt(var + EPS) * gn
            r_ref[rows(n), cols] = (on * sg_ref[rows(n), cols].astype(F32)).astype(BF16)


def _retention(q, k, v, sg, dec, gn_g, s0f, s0b, batch, seq_len):
    hd = RET_HEAD_DIM
    tok_spec = pl.BlockSpec((seq_len, RET_WIDTH), lambda b: (b, 0))
    st_spec = pl.BlockSpec((None, N_RET_HEADS, hd, hd), lambda b: (b, 0, 0, 0))
    st_shape = jax.ShapeDtypeStruct((batch, N_RET_HEADS, hd, hd), F32)
    return pl.pallas_call(
        _retention_kernel,
        grid=(batch,),
        in_specs=[tok_spec, tok_spec, tok_spec, tok_spec,
                  pl.BlockSpec(dec.shape, lambda b: (0, 0, 0)),
                  pl.BlockSpec(gn_g.shape, lambda b: (0, 0)),
                  st_spec, st_spec],
        out_specs=[tok_spec, st_spec, st_spec],
        out_shape=[jax.ShapeDtypeStruct((batch * seq_len, RET_WIDTH), BF16), st_shape, st_shape],
        scratch_shapes=[pltpu.VMEM((N_RET_HEADS, 5, CHUNK, CHUNK), F32),
                        pltpu.VMEM((N_RET_HEADS, 2, hd), F32)],
        compiler_params=pltpu.CompilerParams(dimension_semantics=("arbitrary",),
                                             vmem_limit_bytes=VMEM_LIMIT),
        name="retention",
    )(q, k, v, sg, dec, gn_g, s0f, s0b)


def _fnet_merge_kernel(uf_ref, cs_ref, cls_ref, r_ref, gf_ref, gr_ref, x_ref, mod_ref, wf_ref, wr_ref, wo_ref,
                       o_ref, xcs_ref):
    seq_len = uf_ref.shape[0]
    gd = FOURIER_GROUP_DIM

    @pl.when(pl.program_id(1) == 0)
    def _():
        for g in range(N_FOURIER_GROUPS):
            x = _dot(uf_ref[:, g * gd:(g + 1) * gd], cs_ref[...])
            xcs_ref[0:seq_len, g * gd:(g + 1) * gd] = x[:, :gd].astype(BF16)
            xcs_ref[seq_len:2 * seq_len, g * gd:(g + 1) * gd] = x[:, gd:].astype(BF16)

    f_mix = _dot(cls_ref[...], xcs_ref[...]).astype(BF16)
    f_out = _dot(f_mix, wf_ref[...].astype(BF16))
    r_out = _dot(r_ref[...], wr_ref[...].astype(BF16))
    merged = gf_ref[...].astype(F32) * f_out + gr_ref[...].astype(F32) * r_out
    mix = _dot(merged.astype(BF16), wo_ref[...].astype(BF16))
    o_ref[...] = x_ref[...] + mod_ref[0, 2:3, :] * mix


def _fnet_merge(uf, cs, cls, r, gf, gr, x2d, mod3, w_four, w_ret, w_o, batch, seq_len, mod_row_of_batch):
    rb = min(FNET_ROWS, seq_len)
    nr = seq_len // rb

    def tok(w):
        return pl.BlockSpec((rb, w), lambda b, i: (b * nr + i, 0))

    def full(a):
        return pl.BlockSpec(a.shape, lambda b, i: (0, 0))

    def once(a):
        return pl.BlockSpec(a.shape, lambda b, i: (0, 0), pipeline_mode=pl.Buffered(1))

    return pl.pallas_call(
        _fnet_merge_kernel,
        grid=(batch, nr),
        in_specs=[pl.BlockSpec((seq_len, D_MODEL), lambda b, i: (b, 0)),
                  full(cs),
                  pl.BlockSpec((rb, 2 * seq_len), lambda b, i: (i, 0)),
                  tok(RET_WIDTH), tok(D_MODEL), tok(D_MODEL), tok(D_MODEL),
                  pl.BlockSpec((1, 6, D_MODEL), lambda b, i: (mod_row_of_batch(b), 0, 0)),
                  once(w_four), once(w_ret), once(w_o)],
        out_specs=tok(D_MODEL),
        out_shape=jax.ShapeDtypeStruct((batch * seq_len, D_MODEL), F32),
        scratch_shapes=[pltpu.VMEM((2 * seq_len, D_MODEL), BF16)],
        compiler_params=pltpu.CompilerParams(dimension_semantics=("parallel", "arbitrary"),
                                             vmem_limit_bytes=VMEM_LIMIT),
        name="fnet_merge",
    )(uf, cs, cls, r, gf, gr, x2d, mod3, w_four, w_ret, w_o)


def _pack_pair(lo_f32, hi_f32):
    lo = lax.bitcast_convert_type(lo_f32.astype(BF16).astype(F32), jnp.uint32)
    hi = lax.bitcast_convert_type(hi_f32.astype(BF16).astype(F32), jnp.uint32)
    return lax.bitcast_convert_type((lo >> 16) | hi, jnp.int32)


def _unpack_pair(words_i32):
    w = lax.bitcast_convert_type(words_i32, jnp.uint32)
    lo = lax.bitcast_convert_type(w << 16, F32)
    hi = lax.bitcast_convert_type(w & jnp.uint32(0xFFFF0000), F32)
    return lo, hi


def _load_token_words(ref, lead, n_tok):
    parts = []
    for s in range(ROW_SLABS):
        idx = (pl.ds(s, n_tok, stride=ROW_SLABS), slice(None))
        parts.append(ref[lead + idx] if lead else ref[idx])
    return jnp.concatenate(parts, axis=1)


def _store_token_words(ref, words, n_tok):
    for s in range(ROW_SLABS):
        ref[pl.ds(s, n_tok, stride=ROW_SLABS), :] = words[:, s * 128:(s + 1) * 128]


def _route(scores, biased):
    tokens = scores.shape[1]
    neg = -jnp.inf
    epg = EXPERTS_PER_GROUP
    iota_g = lax.broadcasted_iota(jnp.int32, (epg, tokens), 0).astype(F32)

    def pick_first_max(cur, iota, size):
        m = jnp.max(cur, axis=0, keepdims=True)
        idx = jnp.min(jnp.where(cur == m, iota, float(size)), axis=0, keepdims=True)
        return m, idx, iota == idx

    group_scores = []
    for g in range(N_EXPERT_GROUPS):
        vals = biased[g * epg:(g + 1) * epg, :]
        m1, _, hit = pick_first_max(vals, iota_g, epg)
        m2 = jnp.max(jnp.where(hit, neg, vals), axis=0, keepdims=True)
        group_scores.append(m1 + m2)
    cur = jnp.concatenate(group_scores, axis=0)
    group_sel = jnp.zeros_like(cur)
    for _ in range(TOPK_GROUPS):
        _, _, hit = pick_first_max(cur, iota_g, N_EXPERT_GROUPS)
        group_sel = jnp.where(hit, 1.0, group_sel)
        cur = jnp.where(hit, neg, cur)
    masked = jnp.concatenate(
        [jnp.where(group_sel[g:g + 1, :] > 0.0, biased[g * epg:(g + 1) * epg, :], neg)
         for g in range(N_EXPERT_GROUPS)], axis=0)
    iota_e = lax.broadcasted_iota(jnp.int32, masked.shape, 0).astype(F32)
    sel = jnp.zeros_like(masked)
    cur = masked
    picks = []
    for _ in range(TOP_K):
        _, idx, hit = pick_first_max(cur, iota_e, N_EXPERTS)
        picks.append(idx)
        sel = jnp.where(hit, 1.0, sel)
        cur = jnp.where(hit, neg, cur)
    w = scores * sel
    return w / jnp.sum(w, axis=0, keepdims=True) * ROUTED_SCALE, sel, picks


def _router_kernel(x_ref, mod_ref, g2_ref, wrt_ref, rb_ref, hp_ref, ek_ref, rk_ref, wt_ref, cnt_ref,
                   run_scr, earlier_scr):
    tm = x_ref.shape[0]

    @pl.when(pl.program_id(0) == 0)
    def _():
        run_scr[...] = jnp.zeros_like(run_scr)
        earlier = (lax.broadcasted_iota(jnp.int32, (tm, tm), 0) < lax.broadcasted_iota(jnp.int32, (tm, tm), 1))
        earlier_scr[...] = jnp.where(earlier, 1.0, 0.0).astype(BF16)

    h = _rms_mod(x_ref[...], g2_ref[...], mod_ref[0, 3:4, :], mod_ref[0, 4:5, :])
    half = D_MODEL // 2
    _store_token_words(hp_ref, _pack_pair(h[:, :half], h[:, half:]), tm)

    def split(a):
        hi = a.astype(BF16)
        return hi, (a - hi.astype(F32)).astype(BF16)

    def dot_nt(a, b):
        return lax.dot_general(a, b, (((1,), (1,)), ((), ())), preferred_element_type=F32)

    h_hi, h_lo = split(h)
    w_hi, w_lo = split(wrt_ref[...])
    logits_t = dot_nt(w_hi, h_hi) + (dot_nt(w_hi, h_lo) + dot_nt(w_lo, h_hi))
    scores = jax.nn.sigmoid(logits_t)
    comb_t, sel, picks = _route(scores, scores + rb_ref[...])

    rank_t = _dot(sel.astype(BF16), earlier_scr[...]) + run_scr[...]
    run_scr[...] += jnp.sum(sel, axis=1, keepdims=True)
    cnt_ref[...] = jnp.broadcast_to(run_scr[...], cnt_ref.shape)

    iota_e = lax.broadcasted_iota(jnp.int32, sel.shape, 0).astype(F32)
    ranks, weights = [], []
    for idx in picks:
        hit = iota_e == idx
        ranks.append(jnp.sum(jnp.where(hit, rank_t, 0.0), axis=0, keepdims=True))
        weights.append(jnp.sum(jnp.where(hit, comb_t, 0.0), axis=0, keepdims=True))
    ek_ref[...] = jnp.concatenate(picks, axis=0).astype(jnp.int32)
    rk_ref[...] = jnp.concatenate(ranks, axis=0).astype(jnp.int32)
    w_rep = jnp.concatenate([jnp.broadcast_to(w, (SC_LANES, tm)) for w in weights], axis=0)
    wt_ref[...] = w_rep.T


def _router(x1, mod3, norm2_g, w_router_t, router_bias, seq_len, mod_row_of_batch):
    t = x1.shape[0]
    tm = TM_ROUTER

    def mod_idx(i):
        return (mod_row_of_batch((i * tm) // seq_len), 0, 0)

    def full(a):
        return pl.BlockSpec(a.shape, lambda i: (0,) * a.ndim)

    return pl.pallas_call(
        _router_kernel,
        grid=(t // tm,),
        in_specs=[pl.BlockSpec((tm, D_MODEL), lambda i: (i, 0)),
                  pl.BlockSpec((1, 6, D_MODEL), mod_idx),
                  full(norm2_g), full(w_router_t), full(router_bias)],
        out_specs=[pl.BlockSpec((tm * ROW_SLABS, 128), lambda i: (i, 0)),
                   pl.BlockSpec((TOP_K, tm), lambda i: (0, i)),
                   pl.BlockSpec((TOP_K, tm), lambda i: (0, i)),
                   pl.BlockSpec((tm, 128), lambda i: (i, 0)),
                   pl.BlockSpec((N_EXPERTS, 128), lambda i: (0, 0))],
        out_shape=[jax.ShapeDtypeStruct((t * ROW_SLABS, 128), jnp.int32),
                   jax.ShapeDtypeStruct((TOP_K, t), jnp.int32),
                   jax.ShapeDtypeStruct((TOP_K, t), jnp.int32),
                   jax.ShapeDtypeStruct((t, 128), F32),
                   jax.ShapeDtypeStruct((N_EXPERTS, 128), F32)],
        scratch_shapes=[pltpu.VMEM((N_EXPERTS, 1), F32), pltpu.VMEM((tm, tm), BF16)],
        compiler_params=pltpu.CompilerParams(dimension_semantics=("arbitrary",),
                                             vmem_limit_bytes=VMEM_LIMIT),
        name="router",
    )(x1, mod3, norm2_g, w_router_t, router_bias)


def _plan_kernel(ek_ref, rk_ref, cnt_ref, pos_ref, texp_ref, nused_ref, tend_ref, *, expert_rows):
    rows = float(expert_rows)
    cnt = cnt_ref[:, 0:1]
    tiles = jnp.floor((cnt + (rows - 1.0)) / rows)
    before = (lax.broadcasted_iota(jnp.int32, (N_EXPERTS, N_EXPERTS), 1)
              < lax.broadcasted_iota(jnp.int32, (N_EXPERTS, N_EXPERTS), 0))
    tile_start = jnp.dot(jnp.where(before, 1.0, 0.0), jnp.broadcast_to(tiles, (N_EXPERTS, 128)),
                         precision=lax.Precision.HIGHEST, preferred_element_type=F32)[:, 0:1]
    tile_end = tile_start + tiles
    row_start = tile_start * rows

    ek = ek_ref[...]
    pos = rk_ref[...].astype(F32)
    tile_id = lax.broadcasted_iota(jnp.int32, texp_ref.shape, 1).astype(F32)
    texp = jnp.zeros(texp_ref.shape, F32)
    for e in range(N_EXPERTS):
        pos = pos + jnp.where(ek == e, row_start[e:e + 1, :], 0.0)
        texp = texp + jnp.where(tile_id >= tile_end[e:e + 1, :], 1.0, 0.0)
    pos_ref[...] = pos.astype(jnp.int32)
    texp_ref[...] = jnp.minimum(texp, N_EXPERTS - 1.0).astype(jnp.int32)
    nused_ref[...] = jnp.broadcast_to(tile_end[N_EXPERTS - 1:N_EXPERTS, :], nused_ref.shape).astype(jnp.int32)
    tend_ref[...] = jnp.broadcast_to(tile_end, tend_ref.shape).astype(jnp.int32)


def _plan(ek, rk, cnt, n_tiles_pad, expert_rows):
    t = ek.shape[1]

    def full(shape):
        return pl.BlockSpec(shape, lambda: (0,) * len(shape))

    return pl.pallas_call(
        functools.partial(_plan_kernel, expert_rows=expert_rows),
        in_specs=[full(ek.shape), full(rk.shape), full(cnt.shape)],
        out_specs=[full((TOP_K, t)), full((1, n_tiles_pad)), full((1, 128)), full((N_EXPERTS, 128))],
        out_shape=[jax.ShapeDtypeStruct((TOP_K, t), jnp.int32),
                   jax.ShapeDtypeStruct((1, n_tiles_pad), jnp.int32),
                   jax.ShapeDtypeStruct((1, 128), jnp.int32),
                   jax.ShapeDtypeStruct((N_EXPERTS, 128), jnp.int32)],
        compiler_params=pltpu.CompilerParams(vmem_limit_bytes=VMEM_LIMIT),
        name="plan",
    )(ek, rk, cnt)


def _sc_mesh():
    return plsc.VectorSubcoreMesh(core_axis_name="c", subcore_axis_name="s")


def _sc_pack_weight_halves(w):
    e, k, n = w.shape
    k_half = k // 2
    rb = SC_PACK_BLOCK_WORDS // n
    units_per_expert = k_half // rb
    per_w = (e * units_per_expert) // SC_WORKERS
    lanes = SC_LANES

    @functools.partial(
        pl.kernel, out_type=jax.ShapeDtypeStruct((e * k_half, n), jnp.int32), mesh=_sc_mesh(),
        scratch_types=[pltpu.VMEM((rb, n), F32), pltpu.VMEM((rb, n), F32), pltpu.VMEM((rb, n), jnp.int32)],
        compiler_params=pltpu.CompilerParams(needs_layout_passes=False))
    def kern(w_hbm, out_hbm, a_v, b_v, o_v):
        wid = lax.axis_index("s") * SC_CORES + lax.axis_index("c")

        @pl.loop(0, per_w)
        def _(j):
            unit = wid * per_w + j
            expert = unit // units_per_expert
            blk = unit % units_per_expert
            row_a = expert * k + blk * rb
            pltpu.sync_copy(w_hbm.at[pl.ds(row_a, rb)], a_v)
            pltpu.sync_copy(w_hbm.at[pl.ds(row_a + k_half, rb)], b_v)

            @pl.loop(0, rb)
            def _(r):
                @plsc.parallel_loop(0, n, step=lanes, unroll=4)
                def _(c):
                    both = plsc.pack(a_v[r, pl.ds(c, lanes)], b_v[r, pl.ds(c, lanes)],
                                     format=plsc.PackFormat.INTERLEAVED)
                    o_v[r, pl.ds(c, lanes)] = plsc.bitcast(both, jnp.int32)

            pltpu.sync_copy(o_v, out_hbm.at[pl.ds(expert * k_half + blk * rb, rb)])

    return kern(w.reshape(e * k, n)).reshape(e, k_half, n)


def _sc_dispatch(rows, pos3, n_out, after=()):
    t = rows.shape[0]
    ch = SC_CHUNK
    per_w = (t // ch) // SC_WORKERS

    @functools.partial(
        pl.kernel, out_type=jax.ShapeDtypeStruct((n_out,) + rows.shape[1:], jnp.int32), mesh=_sc_mesh(),
        scratch_types=[pltpu.VMEM((TOP_K, ch), jnp.int32), pltpu.VMEM((ch,) + rows.shape[1:], jnp.int32),
                       pltpu.SemaphoreType.DMA])
    def k(rows_hbm, pos_hbm, *rest):
        out_hbm, idx_v, rows_v, sem = rest[len(after):]
        wid = lax.axis_index("s") * SC_CORES + lax.axis_index("c")

        @pl.loop(0, per_w)
        def _(j):
            c = wid * per_w + j
            pltpu.sync_copy(pos_hbm.at[c], idx_v)
            pltpu.sync_copy(rows_hbm.at[pl.ds(c * ch, ch)], rows_v)
            copies = [pltpu.async_copy(rows_v, out_hbm.at[idx_v.at[kk]], sem) for kk in range(TOP_K)]
            for cp in copies:
                cp.wait()

    return k(rows, pos3, *after)


def _sc_combine(table, pos3, wtok, t):
    ch = SC_CHUNK
    sub = SC_COMBINE_TOKENS
    lanes = SC_LANES
    slabs = ROW_SLABS
    per_w = (t // ch) // SC_WORKERS
    subs_per_chunk = ch // sub
    n_steps = per_w * subs_per_chunk

    @functools.partial(
        pl.kernel, out_type=jax.ShapeDtypeStruct((t, slabs, 128), jnp.int32), mesh=_sc_mesh(),
        scratch_types=[pltpu.VMEM((per_w, TOP_K, ch), jnp.int32),
                       pltpu.VMEM((2, TOP_K, sub, slabs, 128), jnp.int32),
                       pltpu.VMEM((2, sub, 128), F32),
                       pltpu.VMEM((sub, slabs, 128), jnp.int32),
                       pltpu.SemaphoreType.DMA((2,))],
        compiler_params=pltpu.CompilerParams(needs_layout_passes=False))
    def k(tab_hbm, pos_hbm, w_hbm, out_hbm, idx_v, rows_v, w_v, out_v, sem):
        wid = lax.axis_index("s") * SC_CORES + lax.axis_index("c")
        for j in range(per_w):
            pltpu.sync_copy(pos_hbm.at[wid * per_w + j], idx_v.at[j])

        def first_token(step):
            return (wid * per_w + step // subs_per_chunk) * ch + (step % subs_per_chunk) * sub

        def copies(step, slot):
            j = step // subs_per_chunk
            s = step % subs_per_chunk
            idx = [idx_v.at[j, kk, pl.ds(s * sub, sub)] for kk in range(TOP_K)]
            return ([pltpu.make_async_copy(tab_hbm.at[idx[kk]], rows_v.at[slot, kk], sem.at[slot])
                     for kk in range(TOP_K)]
                    + [pltpu.make_async_copy(w_hbm.at[pl.ds(first_token(step), sub)], w_v.at[slot], sem.at[slot])])

        for cp in copies(0, 0):
            cp.start()

        @pl.loop(0, n_steps)
        def _(step):
            slot = step % 2

            @pl.when(step + 1 < n_steps)
            def _():
                for cp in copies(step + 1, 1 - slot):
                    cp.start()

            for cp in copies(step, slot):
                cp.wait()

            @pl.loop(0, sub)
            def _(tt):
                wk = [w_v[slot, tt, pl.ds(kk * lanes, lanes)] for kk in range(TOP_K)]
                for sl in range(slabs):
                    @plsc.parallel_loop(0, 128, step=lanes, unroll=8)
                    def _(off):
                        acc_lo = jnp.zeros((lanes,), F32)
                        acc_hi = jnp.zeros((lanes,), F32)
                        for kk in range(TOP_K):
                            word = rows_v[slot, kk, tt, sl, pl.ds(off, lanes)]
                            lo = plsc.bitcast(word << 16, F32)
                            hi = plsc.bitcast(word & jnp.int32(-65536), F32)
                            acc_lo = acc_lo + wk[kk] * lo
                            acc_hi = acc_hi + wk[kk] * hi
                        both = plsc.pack(acc_lo, acc_hi, format=plsc.PackFormat.INTERLEAVED)
                        out_v[tt, sl, pl.ds(off, lanes)] = plsc.bitcast(both, jnp.int32)

            pltpu.sync_copy(out_v, out_hbm.at[pl.ds(first_token(step), sub)])

    return k(table, pos3, wtok)


def _experts_kernel(texp_ref, nused_ref, tend_ref, xs_hbm, weg_hbm, weu_hbm, wed_hbm, ys_ref,
                    wg_scr, wu_scr, wd_scr, wg_buf, wu_buf, wd_buf, sem, group_scr, xs_buf, xs_sem, *, expert_rows):
    step = pl.program_id(0)
    rows = expert_rows
    tiles_per_step = EXPERT_TILES_PER_STEP
    half = D_MODEL // 2
    n_used = nused_ref[0]
    block_rows = tiles_per_step * rows * ROW_SLABS
    last_block = (n_used - 1) // tiles_per_step

    def row_copy(b, slot):
        return pltpu.make_async_copy(xs_hbm.at[pl.ds(b * block_rows, block_rows)], xs_buf.at[slot], xs_sem.at[slot])

    def weight_copies(e, slot):
        return [pltpu.make_async_copy(weg_hbm.at[e], wg_buf.at[slot], sem.at[slot, 0]),
                pltpu.make_async_copy(weu_hbm.at[e], wu_buf.at[slot], sem.at[slot, 1]),
                pltpu.make_async_copy(wed_hbm.at[e], wd_buf.at[slot], sem.at[slot, 2])]

    def next_group(e):
        tile = tend_ref[e]
        return texp_ref[jnp.minimum(tile, n_used - 1)], tile < n_used

    def start_weights(e, slot, exists):
        @pl.when(exists)
        def _():
            for cp in weight_copies(e, slot):
                cp.start()

    @pl.when(step == 0)
    def _():
        group_scr[0] = 0
        e, exists = texp_ref[0], True
        for slot in range(WEIGHT_SLOTS - 1):
            start_weights(e, slot, exists)
            nxt, has_next = next_group(e)
            e, exists = nxt, exists & has_next
        for b in range(ROW_SLOTS - 1):
            @pl.when(b <= last_block)
            def _():
                row_copy(b, b).start()

    @pl.when(step <= last_block)
    def _():
        ahead = step + ROW_SLOTS - 1

        @pl.when(ahead <= last_block)
        def _():
            row_copy(ahead, ahead % ROW_SLOTS).start()

        row_copy(step, step % ROW_SLOTS).wait()

    def row_tile(tile, x_view, y_view):
        expert = texp_ref[tile]
        used = tile < n_used
        new_expert = (tile == 0) | (expert != texp_ref[jnp.maximum(tile - 1, 0)])

        @pl.when(used & new_expert)
        def _():
            group = group_scr[0]
            slot = group % WEIGHT_SLOTS
            ahead, exists = expert, True
            for _ in range(WEIGHT_SLOTS - 1):
                nxt, has_next = next_group(ahead)
                ahead, exists = nxt, exists & has_next
            start_weights(ahead, (group + WEIGHT_SLOTS - 1) % WEIGHT_SLOTS, exists)

            for cp in weight_copies(expert, slot):
                cp.wait()
            for scr, buf in ((wg_scr, wg_buf), (wu_scr, wu_buf), (wd_scr, wd_buf)):
                top, bottom = _unpack_pair(buf[slot])
                k_half = top.shape[0]
                scr[0:k_half, :] = top.astype(BF16)
                scr[k_half:2 * k_half, :] = bottom.astype(BF16)
            group_scr[0] = group + 1

        @pl.when(used)
        def _():
            lo, hi = _unpack_pair(_load_token_words(x_view, (), rows))
            lo = lo.astype(BF16)
            hi = hi.astype(BF16)
            g = _dot(lo, wg_scr[0:half, :]) + _dot(hi, wg_scr[half:D_MODEL, :])
            u = _dot(lo, wu_scr[0:half, :]) + _dot(hi, wu_scr[half:D_MODEL, :])
            y = _dot((_silu(g) * u).astype(BF16), wd_scr[...])
            _store_token_words(y_view, _pack_pair(y[:, :half], y[:, half:]), rows)

        @pl.when(jnp.logical_not(used) & (step == (n_used - 1) // tiles_per_step))
        def _():
            y_view[...] = jnp.zeros_like(y_view)

    for s in range(tiles_per_step):
        view = pl.ds(s * rows * ROW_SLABS, rows * ROW_SLABS)
        row_tile(step * tiles_per_step + s, xs_buf.at[step % ROW_SLOTS, view], ys_ref.at[view])


def _experts(texp, nused, tend, xs2d, weg, weu, wed, n_tiles, expert_rows):
    tiles_per_step = EXPERT_TILES_PER_STEP
    block = (tiles_per_step * expert_rows * ROW_SLABS, 128)
    hbm = pl.BlockSpec(memory_space=pl.ANY)

    def block_idx(j, te, nu, tn):
        return (jnp.minimum(j, (nu[0] - 1) // tiles_per_step), 0)

    grid_spec = pltpu.PrefetchScalarGridSpec(
        num_scalar_prefetch=3,
        grid=(n_tiles // tiles_per_step,),
        in_specs=[hbm, hbm, hbm, hbm],
        out_specs=pl.BlockSpec(block, block_idx),
        scratch_shapes=[pltpu.VMEM((D_MODEL, EXPERT_DIM), BF16),
                        pltpu.VMEM((D_MODEL, EXPERT_DIM), BF16),
                        pltpu.VMEM((EXPERT_DIM, D_MODEL), BF16),
                        pltpu.VMEM((WEIGHT_SLOTS,) + weg.shape[1:], jnp.int32),
                        pltpu.VMEM((WEIGHT_SLOTS,) + weu.shape[1:], jnp.int32),
                        pltpu.VMEM((WEIGHT_SLOTS,) + wed.shape[1:], jnp.int32),
                        pltpu.SemaphoreType.DMA((WEIGHT_SLOTS, 3)),
                        pltpu.SMEM((1,), jnp.int32),
                        pltpu.VMEM((ROW_SLOTS,) + block, jnp.int32),
                        pltpu.SemaphoreType.DMA((ROW_SLOTS,))],
    )
    return pl.pallas_call(
        functools.partial(_experts_kernel, expert_rows=expert_rows),
        grid_spec=grid_spec,
        out_shape=jax.ShapeDtypeStruct(xs2d.shape, jnp.int32),
        compiler_params=pltpu.CompilerParams(dimension_semantics=("arbitrary",),
                                             vmem_limit_bytes=VMEM_LIMIT),
        name="experts",
    )(texp, nused, tend, xs2d, weg, weu, wed)


def _final_kernel(x_ref, routed_ref, mod_ref, g2_ref, wsg_ref, wsu_ref, wsd_ref, fng_ref, o_ref):
    tm = x_ref.shape[0]
    x = x_ref[...]
    hb = _rms_mod(x, g2_ref[...], mod_ref[0, 3:4, :], mod_ref[0, 4:5, :]).astype(BF16)
    shared = _dot((_silu(_dot(hb, wsg_ref[...])) * _dot(hb, wsu_ref[...])).astype(BF16), wsd_ref[...])
    routed = jnp.concatenate(_unpack_pair(_load_token_words(routed_ref, (), tm)), axis=1)
    y = x + mod_ref[0, 5:6, :] * (routed + shared)
    ms = jnp.mean(y * y, axis=-1, keepdims=True)
    o_ref[...] = y * lax.rsqrt(ms + EPS) * fng_ref[...]


def _final(x1, routed2d, mod3, norm2_g, wsg, wsu, wsd, final_g, seq_len, mod_row_of_batch):
    t = x1.shape[0]
    tm = TM_FINAL

    def mod_idx(i):
        return (mod_row_of_batch((i * tm) // seq_len), 0, 0)

    def full(a):
        return pl.BlockSpec(a.shape, lambda i: (0,) * a.ndim)

    return pl.pallas_call(
        _final_kernel,
        grid=(t // tm,),
        in_specs=[pl.BlockSpec((tm, D_MODEL), lambda i: (i, 0)),
                  pl.BlockSpec((tm * ROW_SLABS, 128), lambda i: (i, 0)),
                  pl.BlockSpec((1, 6, D_MODEL), mod_idx),
                  full(norm2_g), full(wsg), full(wsu), full(wsd), full(final_g)],
        out_specs=pl.BlockSpec((tm, D_MODEL), lambda i: (i, 0)),
        out_shape=jax.ShapeDtypeStruct((t, D_MODEL), F32),
        compiler_params=pltpu.CompilerParams(dimension_semantics=("parallel",),
                                             vmem_limit_bytes=VMEM_LIMIT),
        name="final",
    )(x1, routed2d, mod3, norm2_g, wsg, wsu, wsd, final_g)


def _moe(x1, mod3, lw, seq_len, mod_row_of_batch):
    t = x1.shape[0]
    expert_rows = min(MAX_EXPERT_ROWS, TOP_K * t // N_EXPERTS // 2)
    n_tiles = TOP_K * t // expert_rows + N_EXPERTS
    n_tiles_pad = -(-n_tiles // 128) * 128
    hp2d, ek, rk, wtok, cnt = _router(x1, mod3, lw["norm2_g"], lw["w_router_t"], lw["router_bias"],
                                      seq_len, mod_row_of_batch)
    pos, texp, nused, tend = _plan(ek, rk, cnt, n_tiles_pad, expert_rows)
    pos3 = pos.reshape(TOP_K, t // SC_CHUNK, SC_CHUNK).transpose(1, 0, 2)
    xs = _sc_dispatch(hp2d.reshape(t, ROW_SLABS, 128), pos3, n_tiles * expert_rows,
                      after=(lw["weg"], lw["weu"], lw["wed"]))
    ys2d = _experts(texp.reshape(-1), nused.reshape(-1), tend[:, 0], xs.reshape(-1, 128),
                    lw["weg"], lw["weu"], lw["wed"], n_tiles, expert_rows)
    routed = _sc_combine(ys2d.reshape(-1, ROW_SLABS, 128), pos3, wtok, t)
    return _final(x1, routed.reshape(t * ROW_SLABS, 128), mod3, lw["norm2_g"],
                  lw["wsg"], lw["wsu"], lw["wsd"], lw["final_g"], seq_len, mod_row_of_batch)


def _dft_tables(seq_len):
    gd = FOURIER_GROUP_DIM
    kc = np.arange(gd)
    ang_c = ((kc[:, None] * kc[None, :]) % gd) * (2.0 * math.pi / gd)
    cs = np.concatenate([np.cos(ang_c), np.sin(ang_c)], axis=1) * (gd ** -0.5)
    kl = np.arange(seq_len)
    ang_l = ((kl[:, None] * kl[None, :]) % seq_len) * (2.0 * math.pi / seq_len)
    cls = np.concatenate([np.cos(ang_l), -np.sin(ang_l)], axis=1) * (seq_len ** -0.5)
    return jnp.asarray(cs.astype(np.float32), dtype=BF16), jnp.asarray(cls.astype(np.float32), dtype=BF16)


def _rope_tables(length):
    rows = length // GRID_W
    r = np.repeat(np.arange(rows, dtype=np.float32), GRID_W)
    col = np.tile(np.arange(GRID_W, dtype=np.float32), rows)
    nf = RET_HEAD_DIM // 4
    inv = (np.float32(ROPE_BASE) ** (-np.arange(nf, dtype=np.float32) / np.float32(nf))).astype(np.float32)
    ar = r[:, None] * inv[None]
    ac = col[:, None] * inv[None]
    ang = np.concatenate([ar, ar, ac, ac], axis=-1).astype(np.float64)
    sign = np.where((np.arange(RET_HEAD_DIM) & nf) == 0, -1.0, 1.0)
    return (jnp.asarray(np.cos(ang).astype(np.float32)),
            jnp.asarray((np.sin(ang) * sign[None, :]).astype(np.float32)))


def _trunk_path(x, mod3, mod_row_of_batch, s0f, s0b, rope, lw):
    batch, seq_len, _ = x.shape
    x2d = x.reshape(batch * seq_len, D_MODEL)
    uf, q, k, v, sg, gf, gr = _inproj(x2d, mod3, lw["norm1_g"], lw["w_in"], seq_len, mod_row_of_batch, rope)
    r, s_f, s_b = _retention(q, k, v, sg, lw["dec"], lw["gn_g"], s0f, s0b, batch, seq_len)
    cs, cls = _dft_tables(seq_len)
    x1 = _fnet_merge(uf, cs, cls, r, gf, gr, x2d, mod3, lw["w_four"], lw["w_ret"], lw["w_o"],
                     batch, seq_len, mod_row_of_batch)
    y = _moe(x1, mod3, lw, seq_len, mod_row_of_batch)
    return y.reshape(batch, seq_len, D_MODEL), s_f, s_b


def kernel(x_prompt, x_sample, state_ret_fwd, state_ret_bwd, c, c_ctx, w_ada, b_ada, norm1_g, norm2_g, w_in,
           ret_decay_fwd, ret_decay_bwd, ret_gn_g, w_four_out, w_ret_out, w_out, w_router, router_bias,
           w_exp_gate, w_exp_up, w_exp_down, w_shared_gate, w_shared_up, w_shared_down, final_norm_g):
    depth = w_ada.shape[0]
    assert depth == 1, "final norm is fused into the last layer's MoE kernel"
    n_ctx, n_lat = x_prompt.shape[0], x_sample.shape[0]
    cond = jnp.concatenate([c_ctx[None, :], c], axis=0)
    cond = jnp.pad(cond, ((0, (-cond.shape[0]) % 8), (0, 0)))
    rope = _rope_tables(x_sample.shape[1])
    zeros = jnp.zeros((n_ctx, N_RET_HEADS, RET_HEAD_DIM, RET_HEAD_DIM), F32)

    layer = 0
    mod = _ada(cond, w_ada[layer], b_ada[layer][None, :])
    mod3 = mod.reshape(mod.shape[0], 6, D_MODEL)
    dec = jnp.stack([ret_decay_fwd[layer], ret_decay_bwd[layer]], axis=1)
    lw = {
        "norm1_g": norm1_g[layer][None, :],
        "norm2_g": norm2_g[layer][None, :],
        "w_in": w_in[layer],
        "dec": jnp.broadcast_to(dec[:, :, None], (N_RET_HEADS, 2, RET_HEAD_DIM)).astype(F32),
        "gn_g": ret_gn_g[layer][None, :],
        "w_four": w_four_out[layer],
        "w_ret": w_ret_out[layer],
        "w_o": w_out[layer],
        "w_router_t": w_router[layer].T,
        "router_bias": router_bias[layer][:, None],
        "weg": _sc_pack_weight_halves(w_exp_gate[layer]),
        "weu": _sc_pack_weight_halves(w_exp_up[layer]),
        "wed": _sc_pack_weight_halves(w_exp_down[layer]),
        "wsg": w_shared_gate[layer].astype(BF16),
        "wsu": w_shared_up[layer].astype(BF16),
        "wsd": w_shared_down[layer].astype(BF16),
        "final_g": final_norm_g[None, :],
    }
    y_prompt, s_f, s_b = _trunk_path(x_prompt, mod3, lambda b: 0, zeros, zeros, None, lw)
    y_sample, _, _ = _trunk_path(x_sample, mod3, lambda b: 1 + b, state_ret_fwd[:, layer],
                                 state_ret_bwd[:, layer], rope, lw)
    return (y_prompt, y_sample, s_f[:, None], s_b[:, None])
```
